```python
import jax, jax.numpy as jnp
from jax import lax
import numpy as np

D_MODEL = 1024
BATCH = 8
SEQ = 4096
DEPTH = 1

CHUNK = 64
Q_BLOCK = 128
MLA_HEADS = 8
QK_NOPE_DIM = 64
QK_ROPE_DIM = 32
V_HEAD_DIM = 64
Q_LORA_RANK = 384
KV_LORA_RANK = 256
MLA_WIDTH = MLA_HEADS * V_HEAD_DIM
ROPE_THETA = 10000.0
POOL_WINDOWS = (2, 4, 8, 16)
POOL_GROUPS = len(POOL_WINDOWS)
POOL_WIDTH = D_MODEL // 2
POOL_GROUP_DIM = POOL_WIDTH // POOL_GROUPS
N_BRANCHES = 2
EPS = 1e-6

IN_SIZES = (Q_LORA_RANK, KV_LORA_RANK, QK_ROPE_DIM, MLA_WIDTH, POOL_WIDTH, POOL_WIDTH, N_BRANCHES * D_MODEL)
IN_SPLITS = tuple(int(s) for s in np.cumsum(IN_SIZES)[:-1])
IN_TOTAL = int(sum(IN_SIZES))

kernel_name = "hybrid_mla_pool_gated_block"


def rms_norm(x, g):
    xf = x.astype(jnp.float32)
    y = xf * lax.rsqrt(jnp.mean(xf * xf, axis=-1, keepdims=True) + EPS)
    return (y * g.astype(jnp.float32)).astype(x.dtype)


def rope_tables(seq):
    half = QK_ROPE_DIM // 2
    inv_freq = ROPE_THETA ** (-jnp.arange(half, dtype=jnp.float32) / half)
    ang = jnp.arange(seq, dtype=jnp.float32)[:, None] * inv_freq[None, :]
    return jnp.cos(ang), jnp.sin(ang)


def apply_rope(x, cos, sin):
    xf = x.astype(jnp.float32)
    x1, x2 = jnp.split(xf, 2, axis=-1)
    out = jnp.concatenate([x1 * cos - x2 * sin, x1 * sin + x2 * cos], axis=-1)
    return out.astype(x.dtype)


def mla_attention(zq, zkv, zkr, q_norm, w_uq, kv_norm, w_ukv):
    b, s, _ = zq.shape
    cos, sin = rope_tables(s)
    c_q = rms_norm(zq, q_norm)
    q = jnp.einsum('bsr,rhd->bshd', c_q, w_uq)
    q_nope = q[..., :QK_NOPE_DIM]
    q_rope = apply_rope(q[..., QK_NOPE_DIM:], cos[None, :, None, :], sin[None, :, None, :])
    c_kv = rms_norm(zkv, kv_norm)
    kv = jnp.einsum('bsr,rhd->bshd', c_kv, w_ukv)
    k_nope = kv[..., :QK_NOPE_DIM]
    v = kv[..., QK_NOPE_DIM:]
    k_rope = apply_rope(zkr, cos[None], sin[None])
    scale = (QK_NOPE_DIM + QK_ROPE_DIM) ** -0.5
    key_chunk = jnp.arange(s) // CHUNK
    n_blocks = s // Q_BLOCK

    def one_block(i):
        start = i * Q_BLOCK
        qn = lax.dynamic_slice_in_dim(q_nope, start, Q_BLOCK, axis=1)
        qr = lax.dynamic_slice_in_dim(q_rope, start, Q_BLOCK, axis=1)
        sc = (jnp.einsum('bqhd,bkhd->bhqk', qn, k_nope).astype(jnp.float32)
              + jnp.einsum('bqhr,bkr->bhqk', qr, k_rope).astype(jnp.float32)) * scale
        q_chunk = (start + jnp.arange(Q_BLOCK)) // CHUNK
        mask = key_chunk[None, :] <= q_chunk[:, None]
        sc = jnp.where(mask[None, None], sc, -jnp.inf)
        p = jax.nn.softmax(sc, axis=-1).astype(v.dtype)
        return jnp.einsum('bhqk,bkhd->bqhd', p, v)

    o = lax.map(one_block, jnp.arange(n_blocks))
    o = jnp.transpose(o, (1, 0, 2, 3, 4)).reshape(b, s, MLA_WIDTH)
    return o


def multiscale_pool(u, pool_w, pool_scale):
    b, s, _ = u.shape
    uf = u.astype(jnp.float32)
    csum = jnp.cumsum(uf, axis=1)
    t = jnp.arange(s)
    outs = []
    for gi, w in enumerate(POOL_WINDOWS):
        sl = slice(gi * POOL_GROUP_DIM, (gi + 1) * POOL_GROUP_DIM)
        cg = csum[..., sl]
        shifted = jnp.pad(cg, ((0, 0), (w, 0), (0, 0)))[:, :s]
        count = jnp.minimum(t + 1, w).astype(jnp.float32)[None, :, None]
        d = (cg - shifted) / count - uf[..., sl]
        outs.append(jnp.einsum('bsc,cd->bsd', d, pool_w[gi].astype(jnp.float32)))
    y = jnp.concatenate(outs, axis=-1) * pool_scale.astype(jnp.float32)
    return y.astype(u.dtype)


def _fwd_setup_inputs(seed: int = 0) -> dict:
    key = jax.random.key(seed)
    ks = jax.random.split(key, 14)
    f = jnp.float32
    def nrm(k, shape, fan_in):
        return jax.random.normal(k, shape, f) * (fan_in ** -0.5)
    return {
        "x": jax.random.normal(ks[0], (BATCH, SEQ, D_MODEL), f),
        "norm_in": 1.0 + 0.02 * jax.random.normal(ks[1], (D_MODEL,), f),
        "w_in": nrm(ks[2], (D_MODEL, IN_TOTAL), D_MODEL),
        "q_norm": 1.0 + 0.02 * jax.random.normal(ks[3], (Q_LORA_RANK,), f),
        "w_uq": nrm(ks[4], (Q_LORA_RANK, MLA_HEADS, QK_NOPE_DIM + QK_ROPE_DIM), Q_LORA_RANK),
        "kv_norm": 1.0 + 0.02 * jax.random.normal(ks[5], (KV_LORA_RANK,), f),
        "w_ukv": nrm(ks[6], (KV_LORA_RANK, MLA_HEADS, QK_NOPE_DIM + V_HEAD_DIM), KV_LORA_RANK),
        "pool_w": nrm(ks[7], (POOL_GROUPS, POOL_GROUP_DIM, POOL_GROUP_DIM), POOL_GROUP_DIM),
        "pool_scale": 1.0 + 0.02 * jax.random.normal(ks[8], (POOL_WIDTH,), f),
        "w_branch_attn": nrm(ks[9], (MLA_WIDTH, D_MODEL), MLA_WIDTH),
        "w_branch_pool": nrm(ks[10], (POOL_WIDTH, D_MODEL), POOL_WIDTH),
        "w_out": nrm(ks[11], (D_MODEL, D_MODEL), D_MODEL),
        "norm_final": 1.0 + 0.02 * jax.random.normal(ks[12], (D_MODEL,), f),
    }


def _fwd_reference(x, norm_in, w_in, q_norm, w_uq, kv_norm, w_ukv, pool_w, pool_scale,
              w_branch_attn, w_branch_pool, w_out, norm_final):
    h = x
    for _ in range(DEPTH):
        hn = rms_norm(h, norm_in)
        z = jnp.einsum('bsd,de->bse', hn, w_in)
        zq, zkv, zkr, g_attn, u_pool, g_pool, g_merge = jnp.split(z, IN_SPLITS, axis=-1)
        y_attn = mla_attention(zq, zkv, zkr, q_norm, w_uq, kv_norm, w_ukv) * jax.nn.silu(g_attn)
        y_pool = multiscale_pool(u_pool, pool_w, pool_scale) * jax.nn.silu(g_pool)
        a = jnp.einsum('bsc,cd->bsd', y_attn, w_branch_attn)
        p = jnp.einsum('bsc,cd->bsd', y_pool, w_branch_pool)
        gate_a, gate_p = jnp.split(jax.nn.sigmoid(g_merge.astype(jnp.float32)).astype(h.dtype), 2, axis=-1)
        merged = gate_a * a + gate_p * p
        h = h + jnp.einsum('bsd,de->bse', merged, w_out)
    return rms_norm(h, norm_final)


import jax as _jax
import jax.numpy as _jnp

TWIN_FORMAT = 'train_step'
FWD_PARAMS = ['x', 'norm_in', 'w_in', 'q_norm', 'w_uq', 'kv_norm', 'w_ukv', 'pool_w', 'pool_scale', 'w_branch_attn', 'w_branch_pool', 'w_out', 'norm_final']
TWIN_WEIGHTS = ['norm_in', 'w_in', 'q_norm', 'w_uq', 'kv_norm', 'w_ukv', 'pool_w', 'pool_scale', 'w_branch_attn', 'w_branch_pool', 'w_out', 'norm_final']
TWIN_DIFF_INPUT = 'x'
TWIN_INPUTS = ['x', 'norm_in', 'w_in', 'q_norm', 'w_uq', 'kv_norm', 'w_ukv', 'pool_w', 'pool_scale', 'w_branch_attn', 'w_branch_pool', 'w_out', 'norm_final', 'loss_target', 'm_norm_in', 'm_w_in', 'm_q_norm', 'm_w_uq', 'm_kv_norm', 'm_w_ukv', 'm_pool_w', 'm_pool_scale', 'm_w_branch_attn', 'm_w_branch_pool', 'm_w_out', 'm_norm_final', 'v_norm_in', 'v_w_in', 'v_q_norm', 'v_w_uq', 'v_kv_norm', 'v_w_ukv', 'v_pool_w', 'v_pool_scale', 'v_w_branch_attn', 'v_w_branch_pool', 'v_w_out', 'v_norm_final']
TWIN_OUTPUTS = ['loss', 'grad_x', 'grad_norm_in', 'grad_w_in', 'grad_q_norm', 'grad_w_uq', 'grad_kv_norm', 'grad_w_ukv', 'grad_pool_w', 'grad_pool_scale', 'grad_w_branch_attn', 'grad_w_branch_pool', 'grad_w_out', 'grad_norm_final', 'delta_norm_in', 'delta_w_in', 'delta_q_norm', 'delta_w_uq', 'delta_kv_norm', 'delta_w_ukv', 'delta_pool_w', 'delta_pool_scale', 'delta_w_branch_attn', 'delta_w_branch_pool', 'delta_w_out', 'delta_norm_final', 'new_m_norm_in', 'new_m_w_in', 'new_m_q_norm', 'new_m_w_uq', 'new_m_kv_norm', 'new_m_w_ukv', 'new_m_pool_w', 'new_m_pool_scale', 'new_m_w_branch_attn', 'new_m_w_branch_pool', 'new_m_w_out', 'new_m_norm_final', 'new_v_norm_in', 'new_v_w_in', 'new_v_q_norm', 'new_v_w_uq', 'new_v_kv_norm', 'new_v_w_ukv', 'new_v_pool_w', 'new_v_pool_scale', 'new_v_w_branch_attn', 'new_v_w_branch_pool', 'new_v_w_out', 'new_v_norm_final']
TWIN_LEAF_KINDS = {'loss': 'loss', 'grad_x': 'grad_x', 'grad_norm_in': 'grad_w', 'grad_w_in': 'grad_w', 'grad_q_norm': 'grad_w', 'grad_w_uq': 'grad_w', 'grad_kv_norm': 'grad_w', 'grad_w_ukv': 'grad_w', 'grad_pool_w': 'grad_w', 'grad_pool_scale': 'grad_w', 'grad_w_branch_attn': 'grad_w', 'grad_w_branch_pool': 'grad_w', 'grad_w_out': 'grad_w', 'grad_norm_final': 'grad_w', 'delta_norm_in': 'delta_w', 'delta_w_in': 'delta_w', 'delta_q_norm': 'delta_w', 'delta_w_uq': 'delta_w', 'delta_kv_norm': 'delta_w', 'delta_w_ukv': 'delta_w', 'delta_pool_w': 'delta_w', 'delta_pool_scale': 'delta_w', 'delta_w_branch_attn': 'delta_w', 'delta_w_branch_pool': 'delta_w', 'delta_w_out': 'delta_w', 'delta_norm_final': 'delta_w', 'new_m_norm_in': 'new_m', 'new_m_w_in': 'new_m', 'new_m_q_norm': 'new_m', 'new_m_w_uq': 'new_m', 'new_m_kv_norm': 'new_m', 'new_m_w_ukv': 'new_m', 'new_m_pool_w': 'new_m', 'new_m_pool_scale': 'new_m', 'new_m_w_branch_attn': 'new_m', 'new_m_w_branch_pool': 'new_m', 'new_m_w_out': 'new_m', 'new_m_norm_final': 'new_m', 'new_v_norm_in': 'new_v', 'new_v_w_in': 'new_v', 'new_v_q_norm': 'new_v', 'new_v_w_uq': 'new_v', 'new_v_kv_norm': 'new_v', 'new_v_w_ukv': 'new_v', 'new_v_pool_w': 'new_v', 'new_v_pool_scale': 'new_v', 'new_v_w_branch_attn': 'new_v', 'new_v_w_branch_pool': 'new_v', 'new_v_w_out': 'new_v', 'new_v_norm_final': 'new_v'}


def _forward(args):
    return _fwd_reference(*[args[k] for k in FWD_PARAMS])


def _output_shape():
    out = _jax.eval_shape(lambda: _forward(_fwd_setup_inputs(0)))
    return out.shape, out.dtype

N_MICROBATCH = 1
ADAM_LR = 0.001
ADAM_B1 = 0.9
ADAM_B2 = 0.999
ADAM_EPS = 1e-08
ADAM_WD = 0.01
ADAM_STEP = 10
PER_EXAMPLE_BATCH_AXIS = {'x': 0, 'loss_target': 0}
SHARED_INPUTS = []
_WEIGHT_DTYPES = {'norm_in': _jnp.float32, 'w_in': _jnp.float32, 'q_norm': _jnp.float32, 'w_uq': _jnp.float32, 'kv_norm': _jnp.float32, 'w_ukv': _jnp.float32, 'pool_w': _jnp.float32, 'pool_scale': _jnp.float32, 'w_branch_attn': _jnp.float32, 'w_branch_pool': _jnp.float32, 'w_out': _jnp.float32, 'norm_final': _jnp.float32}
MOMENT_SCALE = {'norm_in': 7.620964e-02, 'w_in': 3.597904e-02, 'q_norm': 1.337516e-02, 'w_uq': 9.464471e-03, 'kv_norm': 2.303644e-02, 'w_ukv': 1.101585e-02, 'pool_w': 6.837753e-02, 'pool_scale': 7.005910e-02, 'w_branch_attn': 8.551660e-03, 'w_branch_pool': 4.825563e-02, 'w_out': 4.857604e-02, 'norm_final': 3.201303e+01}


def _to_microbatches(a, axis):
    t = _jnp.moveaxis(a, axis, 0)
    t = t.reshape((N_MICROBATCH, t.shape[0] // N_MICROBATCH) + t.shape[1:])
    return _jnp.moveaxis(t, 1, axis + 1)


def setup_inputs(seed: int = 0) -> dict:
    inp = _fwd_setup_inputs(seed)
    key = _jax.random.fold_in(_jax.random.key(seed), 7919)
    shape, _ = _output_shape()
    out = dict(inp)
    out["loss_target"] = _jax.random.normal(_jax.random.fold_in(key, 0), shape, _jnp.float32)
    for i, name in enumerate(TWIN_WEIGHTS):
        w = inp[name].astype(_jnp.float32)
        if MOMENT_SCALE is None:
            s = _jnp.sqrt(_jnp.mean(_jnp.square(w)) + 1e-30)
        else:
            s = MOMENT_SCALE[name]
        km, kv = _jax.random.split(_jax.random.fold_in(key, i + 1))
        out[name] = w
        out["m_" + name] = s * _jax.random.normal(km, w.shape, _jnp.float32)
        out["v_" + name] = (s * s) * _jax.random.uniform(kv, w.shape, _jnp.float32, 0.5, 1.5)
    if N_MICROBATCH > 1:
        for name, axis in PER_EXAMPLE_BATCH_AXIS.items():
            out[name] = _to_microbatches(out[name], axis)
    return {'x': out['x'], 'norm_in': out['norm_in'], 'w_in': out['w_in'], 'q_norm': out['q_norm'], 'w_uq': out['w_uq'], 'kv_norm': out['kv_norm'], 'w_ukv': out['w_ukv'], 'pool_w': out['pool_w'], 'pool_scale': out['pool_scale'], 'w_branch_attn': out['w_branch_attn'], 'w_branch_pool': out['w_branch_pool'], 'w_out': out['w_out'], 'norm_final': out['norm_final'], 'loss_target': out['loss_target'], 'm_norm_in': out['m_norm_in'], 'm_w_in': out['m_w_in'], 'm_q_norm': out['m_q_norm'], 'm_w_uq': out['m_w_uq'], 'm_kv_norm': out['m_kv_norm'], 'm_w_ukv': out['m_w_ukv'], 'm_pool_w': out['m_pool_w'], 'm_pool_scale': out['m_pool_scale'], 'm_w_branch_attn': out['m_w_branch_attn'], 'm_w_branch_pool': out['m_w_branch_pool'], 'm_w_out': out['m_w_out'], 'm_norm_final': out['m_norm_final'], 'v_norm_in': out['v_norm_in'], 'v_w_in': out['v_w_in'], 'v_q_norm': out['v_q_norm'], 'v_w_uq': out['v_w_uq'], 'v_kv_norm': out['v_kv_norm'], 'v_w_ukv': out['v_w_ukv'], 'v_pool_w': out['v_pool_w'], 'v_pool_scale': out['v_pool_scale'], 'v_w_branch_attn': out['v_w_branch_attn'], 'v_w_branch_pool': out['v_w_branch_pool'], 'v_w_out': out['v_w_out'], 'v_norm_final': out['v_norm_final']}


def _loss(weights, diff, rest, loss_target):
    with _jax.named_scope("forward"):
        args = {**rest, TWIN_DIFF_INPUT: diff, **{k: w.astype(_WEIGHT_DTYPES[k]) for k, w in weights.items()}}
        y = _forward(args)
    with _jax.named_scope("loss_head"):
        err = _jnp.square(y.astype(_jnp.float32) - loss_target)
        return 0.5 * _jnp.sum(_jnp.mean(err, axis=-1)) if err.ndim else 0.5 * err


def _adamw(w, g, m, v):
    m = ADAM_B1 * m + (1.0 - ADAM_B1) * g
    v = ADAM_B2 * v + (1.0 - ADAM_B2) * _jnp.square(g)
    m_hat = m / (1.0 - ADAM_B1 ** ADAM_STEP)
    v_hat = v / (1.0 - ADAM_B2 ** ADAM_STEP)
    delta = -ADAM_LR * (m_hat / (_jnp.sqrt(v_hat) + ADAM_EPS) + ADAM_WD * w)
    return delta, m, v


def reference(x, norm_in, w_in, q_norm, w_uq, kv_norm, w_ukv, pool_w, pool_scale, w_branch_attn, w_branch_pool, w_out, norm_final, loss_target, m_norm_in, m_w_in, m_q_norm, m_w_uq, m_kv_norm, m_w_ukv, m_pool_w, m_pool_scale, m_w_branch_attn, m_w_branch_pool, m_w_out, m_norm_final, v_norm_in, v_w_in, v_q_norm, v_w_uq, v_kv_norm, v_w_ukv, v_pool_w, v_pool_scale, v_w_branch_attn, v_w_branch_pool, v_w_out, v_norm_final):
    given = dict(x=x, norm_in=norm_in, w_in=w_in, q_norm=q_norm, w_uq=w_uq, kv_norm=kv_norm, w_ukv=w_ukv, pool_w=pool_w, pool_scale=pool_scale, w_branch_attn=w_branch_attn, w_branch_pool=w_branch_pool, w_out=w_out, norm_final=norm_final, loss_target=loss_target, m_norm_in=m_norm_in, m_w_in=m_w_in, m_q_norm=m_q_norm, m_w_uq=m_w_uq, m_kv_norm=m_kv_norm, m_w_ukv=m_w_ukv, m_pool_w=m_pool_w, m_pool_scale=m_pool_scale, m_w_branch_attn=m_w_branch_attn, m_w_branch_pool=m_w_branch_pool, m_w_out=m_w_out, m_norm_final=m_norm_final, v_norm_in=v_norm_in, v_w_in=v_w_in, v_q_norm=v_q_norm, v_w_uq=v_w_uq, v_kv_norm=v_kv_norm, v_w_ukv=v_w_ukv, v_pool_w=v_pool_w, v_pool_scale=v_pool_scale, v_w_branch_attn=v_w_branch_attn, v_w_branch_pool=v_w_branch_pool, v_w_out=v_w_out, v_norm_final=v_norm_final)
    weights = {n: given[n] for n in TWIN_WEIGHTS}
    shared = {n: given[n] for n in SHARED_INPUTS}
    per_example = {n: given[n] for n in ['x']}
    grad_fn = _jax.value_and_grad(_loss, argnums=(0, 1))

    def one_microbatch(ex, loss_target):
        ex = dict(ex)
        diff = ex.pop(TWIN_DIFF_INPUT)
        return grad_fn(weights, diff, {**shared, **ex}, loss_target)

    if N_MICROBATCH == 1:
        loss, (grad_w, grad_x) = one_microbatch(per_example, given["loss_target"])
    else:
        def body(carry, xs):
            loss_sum, grad_sum = carry
            l_k, (gw_k, gx_k) = one_microbatch(xs[0], xs[1])
            with _jax.named_scope("update"):
                return (loss_sum + l_k, _jax.tree.map(_jnp.add, grad_sum, gw_k)), gx_k

        init = (_jnp.zeros((), _jnp.float32), _jax.tree.map(_jnp.zeros_like, weights))
        (loss, grad_w), grad_x = _jax.lax.scan(body, init, (per_example, given["loss_target"]))
    with _jax.named_scope("update"):
        delta_w, new_m, new_v = {}, {}, {}
        for n in TWIN_WEIGHTS:
            delta_w[n], new_m[n], new_v[n] = _adamw(weights[n], grad_w[n], given["m_" + n], given["v_" + n])
    return (loss, grad_x, *[grad_w[n] for n in TWIN_WEIGHTS], *[delta_w[n] for n in TWIN_WEIGHTS],
            *[new_m[n] for n in TWIN_WEIGHTS], *[new_v[n] for n in TWIN_WEIGHTS])
```

```python
import functools

import jax
import jax.numpy as jnp
import numpy as np
from jax import lax
from jax.experimental import pallas as pl
from jax.experimental.pallas import tpu as pltpu

F32 = jnp.float32
BF16 = jnp.bfloat16
SDS = jax.ShapeDtypeStruct

D_MODEL = 1024
HEADS = 8
NOPE = 64
ROPE = 32
VDIM = 64
Q_RANK = 384
KV_RANK = 256
MLA_W = HEADS * VDIM
POOL_W = 512
POOL_GROUPS = 4
GROUP = POOL_W // POOL_GROUPS
CHUNK = 64
ROPE_THETA = 10000.0
EPS = 1e-6
SCALE = (NOPE + ROPE) ** -0.5
IN_TOTAL = 4256
ADAM_LR, ADAM_B1, ADAM_B2, ADAM_EPS, ADAM_WD, ADAM_STEP = 0.001, 0.9, 0.999, 1e-08, 0.01, 10

N_DEV = 8
LANES = 128
HEAD_PAD = LANES
HW = HEADS * HEAD_PAD

GM, GA, UP, GP, ZQ, ZKV, ZKR, ZTOT = 0, 2048, 2560, 3072, 3584, 3968, 4224, 4352
FRONT_W = ZTOT - ZQ
ORIG = dict(zq=(0, 384), zkv=(384, 256), zkr=(640, 32), ga=(672, 512), up=(1184, 512), gp=(1696, 512), gm=(2208, 2048))

VMEM_LIMIT = 56 * 1024 * 1024


def _cp(sem=None, **kw):
    if sem is not None:
        kw["dimension_semantics"] = sem
    return pltpu.CompilerParams(vmem_limit_bytes=VMEM_LIMIT, **kw)


def _mm(a, b):
    return lax.dot_general(a, b, (((1,), (0,)), ((), ())), preferred_element_type=F32)


def _mm_nt(a, b):
    return lax.dot_general(a, b, (((1,), (1,)), ((), ())), preferred_element_type=F32)


def _mm_tn(a, b):
    return lax.dot_general(a, b, (((0,), (0,)), ((), ())), preferred_element_type=F32)


def _row_spec(tm, w):
    return pl.BlockSpec((tm, w), lambda i: (i, 0))


def _full_spec(a):
    nd = len(a.shape)
    return pl.BlockSpec(a.shape, lambda *_: (0,) * nd)


def _rope(v, c, sa, sb, sign):
    n = v.shape[-1]
    reps = n // LANES
    if reps > 1:
        c, sa, sb = (jnp.tile(t, (1, reps)) for t in (c, sa, sb))
    up = pltpu.roll(v, n - ROPE // 2, 1)
    dn = pltpu.roll(v, ROPE // 2, 1)
    return v * c + sign * (up * sa + dn * sb)


def _rope_tables(T):
    half = ROPE // 2
    inv_freq = ROPE_THETA ** (-jnp.arange(half, dtype=F32) / half)
    ang = jnp.arange(T, dtype=F32)[:, None] * inv_freq[None, :]
    cos, sin = jnp.cos(ang), jnp.sin(ang)
    z16 = jnp.zeros((T, half), F32)
    z32 = jnp.zeros((T, LANES - NOPE - ROPE), F32)
    c = jnp.concatenate([jnp.ones((T, NOPE), F32), cos, cos, z32], axis=1)
    sa = jnp.concatenate([jnp.zeros((T, NOPE), F32), -sin, z16, z32], axis=1)
    sb = jnp.concatenate([jnp.zeros((T, NOPE), F32), z16, sin, z32], axis=1)
    return c, sa, sb


def _silu_parts(g):
    sg = jax.nn.sigmoid(g)
    return g * sg, sg + g * sg * (1.0 - sg)


def _in_proj(x2, norm_in, w_in_pad, q_norm, wuq_pad, kv_norm, wukv, rc, rsa, rsb, tm):
    T = x2.shape[0]

    def body(x_ref, nin_ref, win_ref, qn_ref, wuq_ref, kvn_ref, wukv_ref, c_ref, sa_ref, sb_ref,
             hn_ref, zgm_ref, zga_ref, zup_ref, zgp_ref, zfr_ref, q_ref, k_ref, kv_ref):
        xf = x_ref[...]
        r = lax.rsqrt(jnp.mean(xf * xf, axis=-1, keepdims=True) + EPS)
        hn = (xf * r * nin_ref[...]).astype(BF16)
        hn_ref[...] = hn
        z = _mm(hn, win_ref[...])
        zgm_ref[...] = z[:, GM:GA]
        zga_ref[...] = z[:, GA:UP]
        zup_ref[...] = z[:, UP:GP]
        zgp_ref[...] = z[:, GP:ZQ]
        zfr_ref[...] = z[:, ZQ:ZTOT]
        zq, zkv, zkr = z[:, ZQ:ZKV], z[:, ZKV:ZKR], z[:, ZKR:ZTOT]
        c, sa, sb = c_ref[...], sa_ref[...], sb_ref[...]
        rq = lax.rsqrt(jnp.mean(zq * zq, axis=-1, keepdims=True) + EPS)
        cq = (zq * rq * qn_ref[...]).astype(BF16)
        q = _rope(_mm(cq, wuq_ref[...]), c, sa, sb, 1.0)
        q_ref[...] = q.astype(BF16)
        rkv = lax.rsqrt(jnp.mean(zkv * zkv, axis=-1, keepdims=True) + EPS)
        ckv = (zkv * rkv * kvn_ref[...]).astype(BF16)
        kv = _mm(ckv, wukv_ref[...])
        kv_ref[...] = kv.astype(BF16)
        kr = _rope(zkr, c, sa, sb, 1.0)
        lane = lax.broadcasted_iota(jnp.int32, kv.shape, 1) % LANES
        k_ref[...] = jnp.where(lane < NOPE, kv, jnp.tile(kr, (1, HEADS))).astype(BF16)

    ins = (x2, norm_in, w_in_pad, q_norm, wuq_pad, kv_norm, wukv, rc, rsa, rsb)
    in_specs = [_row_spec(tm, D_MODEL), _full_spec(norm_in), _full_spec(w_in_pad), _full_spec(q_norm), _full_spec(wuq_pad),
                _full_spec(kv_norm), _full_spec(wukv), _row_spec(tm, LANES), _row_spec(tm, LANES), _row_spec(tm, LANES)]
    widths = [(D_MODEL, BF16), (GA - GM, F32), (UP - GA, F32), (GP - UP, F32), (ZQ - GP, F32), (FRONT_W, F32),
              (HW, BF16), (HW, BF16), (HW, BF16)]
    return pl.pallas_call(
        body, name="in_proj", grid=(T // tm,), in_specs=in_specs,
        out_specs=[_row_spec(tm, w) for w, _ in widths],
        out_shape=[SDS((T, w), dt) for w, dt in widths],
        compiler_params=_cp(("parallel",)),
    )(*ins)


def _diag_mask(tq, transposed):
    r = lax.broadcasted_iota(jnp.int32, (tq, tq), 0) // CHUNK
    c = lax.broadcasted_iota(jnp.int32, (tq, tq), 1) // CHUNK
    return (r <= c) if transposed else (c <= r)


def _attn_fwd(q_att, k_att, kv, tq):
    T = q_att.shape[0]

    def body(q_ref, k_ref, kv_ref, o_ref, lse_ref):
        i = pl.program_id(1)
        mask = _diag_mask(tq, False)
        lane = lax.broadcasted_iota(jnp.int32, (tq, LANES), 1)
        res = []
        for h in range(2):
            hs = slice(h * LANES, (h + 1) * LANES)
            qh = q_ref[:, hs]

            def step(j, carry, masked, hs=hs, qh=qh):
                m, l, acc = carry
                rows = pl.ds(pl.multiple_of(j * tq, tq), tq)
                s = _mm_nt(qh, k_ref[rows, hs]) * SCALE
                if masked:
                    s = jnp.where(mask, s, -jnp.inf)
                m_new = jnp.maximum(m, jnp.max(s, axis=-1, keepdims=True))
                alpha = jnp.exp(m - m_new)
                p = jnp.exp(s - m_new)
                l = alpha * l + jnp.sum(p, axis=-1, keepdims=True)
                acc = alpha * acc + _mm(p.astype(BF16), kv_ref[rows, hs])
                return m_new, l, acc

            init = (jnp.full((tq, 1), -jnp.inf, F32), jnp.zeros((tq, 1), F32), jnp.zeros((tq, LANES), F32))
            carry = lax.fori_loop(0, i, functools.partial(step, masked=False), init)
            m, l, acc = step(i, carry, True)
            res.append((acc / l, m + jnp.log(l)))
        (oa, la), (ob, lb) = res
        o_ref[...] = jnp.where(lane < VDIM, pltpu.roll(oa, VDIM, 1), ob)
        lse_ref[...] = jnp.where(lane < VDIM, la, lb)

    qspec = pl.BlockSpec((tq, 2 * LANES), lambda p, i: (i, p))
    kspec = pl.BlockSpec((T, 2 * LANES), lambda p, i: (0, p))
    ospec = pl.BlockSpec((tq, LANES), lambda p, i: (i, p))
    return pl.pallas_call(
        body, name="attn_fwd", grid=(HEADS // 2, T // tq), in_specs=[qspec, kspec, kspec], out_specs=[ospec, ospec],
        out_shape=[SDS((T, MLA_W), F32), SDS((T, MLA_W), F32)],
        compiler_params=_cp(("parallel", "parallel")),
    )(q_att, k_att, kv)


def _pick(g, vals):
    out = vals[-1]
    for k in range(len(vals) - 2, -1, -1):
        out = jnp.where(g == k, vals[k], out)
    return out


def _window_sum(u, g, forward):
    T = u.shape[0]
    row = lax.broadcasted_iota(jnp.int32, u.shape, 0)

    def sh(s, k):
        if forward:
            return jnp.where(row >= k, pltpu.roll(s, k, 0), 0.0)
        return jnp.where(row < T - k, pltpu.roll(s, T - k, 0), 0.0)

    sums, s = [], u
    for k in (1, 2, 4, 8):
        s = s + sh(s, k)
        sums.append(s)
    return _pick(g, sums)


def _pool_count(shape, g):
    row = lax.broadcasted_iota(jnp.int32, shape, 0)
    return jnp.minimum(row + 1, lax.shift_left(jnp.int32(2), g)).astype(F32)


def _pool_fwd(zup, zgp, pool_w, pool_scale):
    T = zup.shape[0]

    def body(u_ref, g_ref, w_ref, sc_ref, y_ref):
        g = pl.program_id(0)
        u = u_ref[...]
        d = _window_sum(u, g, True) / _pool_count(u.shape, g) - u
        lin = _mm(d.astype(BF16), w_ref[0].astype(BF16))
        silu, _ = _silu_parts(g_ref[...])
        y_ref[...] = (lin * sc_ref[...] * silu).astype(BF16)

    col = pl.BlockSpec((T, GROUP), lambda g: (0, g))
    return pl.pallas_call(
        body, name="pool_fwd", grid=(POOL_GROUPS,),
        in_specs=[col, col, pl.BlockSpec((1, GROUP, GROUP), lambda g: (g, 0, 0)), pl.BlockSpec((1, GROUP), lambda g: (0, g))],
        out_specs=col, out_shape=SDS((T, POOL_W), BF16), compiler_params=_cp(("parallel",)),
    )(zup, zgp, pool_w, pool_scale)


def _pool_bwd(zup, zgp, dyp, pool_w, pool_scale):
    T = zup.shape[0]

    def body(u_ref, g_ref, dy_ref, w_ref, sc_ref, du_ref, dg_ref, gw_ref, gsc_ref):
        g = pl.program_id(0)
        u = u_ref[...]
        cnt = _pool_count(u.shape, g)
        d = (_window_sum(u, g, True) / cnt - u).astype(BF16)
        wb = w_ref[0].astype(BF16)
        lin = _mm(d, wb)
        sc = sc_ref[...]
        silu, dsilu = _silu_parts(g_ref[...])
        dy = dy_ref[...]
        dg_ref[...] = (dy * lin * sc * dsilu).astype(BF16)
        dpre = dy * silu
        gsc_ref[...] = jnp.sum(dpre * lin, axis=0, keepdims=True)
        dlin = (dpre * sc).astype(BF16)
        gw_ref[0] = _mm_tn(d, dlin)
        dd = _mm_nt(dlin, wb)
        du_ref[...] = (_window_sum(dd / cnt, g, False) - dd).astype(BF16)

    col = pl.BlockSpec((T, GROUP), lambda g: (0, g))
    wspec = pl.BlockSpec((1, GROUP, GROUP), lambda g: (g, 0, 0))
    vspec = pl.BlockSpec((1, GROUP), lambda g: (0, g))
    return pl.pallas_call(
        body, name="pool_bwd", grid=(POOL_GROUPS,), in_specs=[col, col, col, wspec, vspec], out_specs=[col, col, wspec, vspec],
        out_shape=[SDS((T, POOL_W), BF16), SDS((T, POOL_W), BF16), SDS((POOL_GROUPS, GROUP, GROUP), F32), SDS((1, POOL_W), F32)],
        compiler_params=_cp(("parallel",)),
    )(zup, zgp, dyp, pool_w, pool_scale)


def _tail(x2, tgt, o, zga, ypool, zgm, wba, wbp, wout, norm_final, tm):
    T = x2.shape[0]

    def body(x_ref, tgt_ref, o_ref, zga_ref, yp_ref, zgm_ref, wba_ref, wbp_ref, wout_ref, nf_ref,
             loss_ref, dh_ref, dgm_ref, dya_ref, dyp_ref, gwout_ref, gwba_ref, gwbp_ref, gnf_ref):
        @pl.when(pl.program_id(0) == 0)
        def _():
            for ref in (loss_ref, gwout_ref, gwba_ref, gwbp_ref, gnf_ref):
                ref[...] = jnp.zeros_like(ref)

        silu, _ = _silu_parts(zga_ref[...])
        ya = (o_ref[...] * silu).astype(BF16)
        yp = yp_ref[...]
        wba_v, wbp_v, wout_v = wba_ref[...], wbp_ref[...], wout_ref[...]
        a = _mm(ya, wba_v)
        p = _mm(yp, wbp_v)
        gate = jax.nn.sigmoid(zgm_ref[...])
        ga, gp = gate[:, :D_MODEL], gate[:, D_MODEL:]
        mg = (ga * a + gp * p).astype(BF16)
        h = x_ref[...] + _mm(mg, wout_v)
        r = lax.rsqrt(jnp.mean(h * h, axis=-1, keepdims=True) + EPS)
        gf = nf_ref[...]
        hr = h * r
        e = hr * gf - tgt_ref[...]
        loss_ref[...] += (0.5 / D_MODEL) * jnp.sum(e * e)
        dy = e * (1.0 / D_MODEL)
        gnf_ref[...] += jnp.sum(dy * hr, axis=0, keepdims=True)
        u = dy * gf
        dh = r * (u - hr * jnp.mean(u * hr, axis=-1, keepdims=True))
        dh_ref[...] = dh
        dhb = dh.astype(BF16)
        dmg = _mm_nt(dhb, wout_v)
        gwout_ref[...] += _mm_tn(mg, dhb)
        dgm_ref[:, :D_MODEL] = (dmg * a * ga * (1.0 - ga)).astype(BF16)
        dgm_ref[:, D_MODEL:] = (dmg * p * gp * (1.0 - gp)).astype(BF16)
        dab = (dmg * ga).astype(BF16)
        dpb = (dmg * gp).astype(BF16)
        dya_ref[...] = _mm_nt(dab, wba_v)
        gwba_ref[...] += _mm_tn(ya, dab)
        dyp_ref[...] = _mm_nt(dpb, wbp_v)
        gwbp_ref[...] += _mm_tn(yp, dpb)

    ins = (x2, tgt, o, zga, ypool, zgm, wba, wbp, wout, norm_final)
    in_specs = [_row_spec(tm, D_MODEL), _row_spec(tm, D_MODEL), _row_spec(tm, MLA_W), _row_spec(tm, MLA_W), _row_spec(tm, POOL_W),
                _row_spec(tm, 2 * D_MODEL), _full_spec(wba), _full_spec(wbp), _full_spec(wout), _full_spec(norm_final)]
    outs = [SDS((8, LANES), F32), SDS((T, D_MODEL), F32), SDS((T, 2 * D_MODEL), BF16), SDS((T, MLA_W), F32), SDS((T, POOL_W), F32),
            SDS((D_MODEL, D_MODEL), F32), SDS((MLA_W, D_MODEL), F32), SDS((POOL_W, D_MODEL), F32), SDS((1, D_MODEL), F32)]
    out_specs = [_full_spec(outs[0]), _row_spec(tm, D_MODEL), _row_spec(tm, 2 * D_MODEL), _row_spec(tm, MLA_W), _row_spec(tm, POOL_W),
                 _full_spec(outs[5]), _full_spec(outs[6]), _full_spec(outs[7]), _full_spec(outs[8])]
    return pl.pallas_call(
        body, name="tail", grid=(T // tm,), in_specs=in_specs, out_specs=out_specs, out_shape=outs,
        compiler_params=_cp(("arbitrary",)),
    )(*ins)


def _attn_bwd_dq(q_att, k_att, kv, o, zga, dya, lse, tq):
    T = q_att.shape[0]

    def body(q_ref, k_ref, kv_ref, o_ref, zga_ref, dya_ref, lse_ref, dq_ref, dga_ref, dcap_ref, doop_ref):
        i = pl.program_id(1)
        mask = _diag_mask(tq, False)
        lane = lax.broadcasted_iota(jnp.int32, (tq, LANES), 1)
        lo = lane < VDIM
        silu, dsilu = _silu_parts(zga_ref[...])
        dya_v, o_v = dya_ref[...], o_ref[...]
        do = dya_v * silu
        dga_ref[...] = (dya_v * o_v * dsilu).astype(BF16)
        prod = do * o_v
        dcap_a = jnp.sum(jnp.where(lo, prod, 0.0), axis=-1, keepdims=True)
        dcap_b = jnp.sum(jnp.where(lo, 0.0, prod), axis=-1, keepdims=True)
        dcap_ref[...] = jnp.where(lo, dcap_a, dcap_b)
        doops = (pltpu.roll(jnp.where(lo, do, 0.0), VDIM, 1).astype(BF16), jnp.where(lo, 0.0, do).astype(BF16))
        lse_v = lse_ref[...]
        heads = ((lse_v[:, :1], dcap_a), (lse_v[:, VDIM:VDIM + 1], dcap_b))
        for h in range(2):
            hs = slice(h * LANES, (h + 1) * LANES)
            qh = q_ref[:, hs]
            doop = doops[h]
            doop_ref[:, hs] = doop
            lse_h, dcap_h = heads[h]

            def step(j, dq, masked, hs=hs, qh=qh, doop=doop, lse_h=lse_h, dcap_h=dcap_h):
                rows = pl.ds(pl.multiple_of(j * tq, tq), tq)
                kh = k_ref[rows, hs]
                s = _mm_nt(qh, kh) * SCALE
                p = jnp.exp(s - lse_h)
                if masked:
                    p = jnp.where(mask, p, 0.0)
                dp = _mm_nt(doop, kv_ref[rows, hs])
                ds = (p * (dp - dcap_h) * SCALE).astype(BF16)
                return dq + _mm(ds, kh)

            dq = lax.fori_loop(0, i, functools.partial(step, masked=False), jnp.zeros((tq, LANES), F32))
            dq_ref[:, hs] = step(i, dq, True).astype(BF16)

    qspec = pl.BlockSpec((tq, 2 * LANES), lambda p, i: (i, p))
    kspec = pl.BlockSpec((T, 2 * LANES), lambda p, i: (0, p))
    ospec = pl.BlockSpec((tq, LANES), lambda p, i: (i, p))
    return pl.pallas_call(
        body, name="attn_bwd_dq", grid=(HEADS // 2, T // tq),
        in_specs=[qspec, kspec, kspec, ospec, ospec, ospec, ospec], out_specs=[qspec, ospec, ospec, qspec],
        out_shape=[SDS((T, HW), BF16), SDS((T, MLA_W), BF16), SDS((T, MLA_W), F32), SDS((T, HW), BF16)],
        compiler_params=_cp(("parallel", "parallel")),
    )(q_att, k_att, kv, o, zga, dya, lse)


def _attn_bwd_dkv(q_att, k_att, kv, doop, lse_rows, dcap_rows, tq):
    T = q_att.shape[0]
    nq = T // tq

    def body(q_ref, k_ref, kv_ref, doop_ref, lse_ref, dcap_ref, dkv_ref, dkr_ref):
        j = pl.program_id(1)
        mask = _diag_mask(tq, True)
        lane = lax.broadcasted_iota(jnp.int32, (tq, LANES), 1)
        dkr = jnp.zeros((tq, LANES), F32)
        for h in range(2):
            hs = slice(h * LANES, (h + 1) * LANES)
            kh = k_ref[:, hs]
            vh = kv_ref[:, hs]

            def step(i, carry, masked, h=h, hs=hs, kh=kh, vh=vh):
                dk, dv = carry
                off = pl.multiple_of(i * tq, tq)
                rows = pl.ds(off, tq)
                qh = q_ref[rows, hs]
                doop_h = doop_ref[rows, hs]
                st = _mm_nt(kh, qh) * SCALE
                pt = jnp.exp(st - lse_ref[0, h:h + 1, rows])
                if masked:
                    pt = jnp.where(mask, pt, 0.0)
                dv = dv + _mm(pt.astype(BF16), doop_h)
                dpt = _mm_nt(vh, doop_h)
                dst = (pt * (dpt - dcap_ref[0, h:h + 1, rows]) * SCALE).astype(BF16)
                return dk + _mm(dst, qh), dv

            zero = jnp.zeros((tq, LANES), F32)
            carry = step(j, (zero, zero), True)
            dk, dv = lax.fori_loop(j + 1, nq, functools.partial(step, masked=False), carry)
            dkv_ref[:, hs] = jnp.where(lane < NOPE, dk, dv).astype(BF16)
            dkr = dkr + dk
        dkr_ref[0] = jnp.where((lane >= NOPE) & (lane < NOPE + ROPE), dkr, 0.0)

    kspec = pl.BlockSpec((tq, 2 * LANES), lambda p, j: (j, p))
    qspec = pl.BlockSpec((T, 2 * LANES), lambda p, j: (0, p))
    rspec = pl.BlockSpec((1, 2, T), lambda p, j: (p, 0, 0))
    return pl.pallas_call(
        body, name="attn_bwd_dkv", grid=(HEADS // 2, nq),
        in_specs=[qspec, kspec, kspec, qspec, rspec, rspec],
        out_specs=[kspec, pl.BlockSpec((1, tq, LANES), lambda p, j: (p, j, 0))],
        out_shape=[SDS((T, HW), BF16), SDS((HEADS // 2, T, LANES), F32)],
        compiler_params=_cp(("parallel", "parallel")),
    )(q_att, k_att, kv, doop, lse_rows, dcap_rows)


def _rms_bwd(z, gain, dout):
    r = lax.rsqrt(jnp.mean(z * z, axis=-1, keepdims=True) + EPS)
    zr = z * r
    u = dout * gain
    return r * (u - zr * jnp.mean(u * zr, axis=-1, keepdims=True)), jnp.sum(dout * zr, axis=0, keepdims=True)


def _mla_bwd(dq_att, dkv_nat, dkr4, zfr, q_norm, wuq_pad, kv_norm, wukv, rc, rsa, rsb, tm):
    T = dq_att.shape[0]

    def body(dq_ref, dkv_ref, dkr_ref, zfr_ref, qn_ref, wuq_ref, kvn_ref, wukv_ref, c_ref, sa_ref, sb_ref,
             dfr_ref, gwuq_ref, gwukv_ref, gqn_ref, gkvn_ref):
        @pl.when(pl.program_id(0) == 0)
        def _():
            for ref in (gwuq_ref, gwukv_ref, gqn_ref, gkvn_ref):
                ref[...] = jnp.zeros_like(ref)

        c, sa, sb = c_ref[...], sa_ref[...], sb_ref[...]
        zq, zkv = zfr_ref[:, :Q_RANK], zfr_ref[:, Q_RANK:Q_RANK + KV_RANK]
        qn, kvn = qn_ref[...], kvn_ref[...]
        cq = (zq * lax.rsqrt(jnp.mean(zq * zq, axis=-1, keepdims=True) + EPS) * qn).astype(BF16)
        ckv = (zkv * lax.rsqrt(jnp.mean(zkv * zkv, axis=-1, keepdims=True) + EPS) * kvn).astype(BF16)
        dq = _rope(dq_ref[...].astype(F32), c, sa, sb, -1.0).astype(BF16)
        gwuq_ref[...] += _mm_tn(cq, dq)
        dzq, gqn = _rms_bwd(zq, qn, _mm_nt(dq, wuq_ref[...]))
        gqn_ref[...] += gqn
        dkv = dkv_ref[...]
        gwukv_ref[...] += _mm_tn(ckv, dkv)
        dzkv, gkvn = _rms_bwd(zkv, kvn, _mm_nt(dkv, wukv_ref[...]))
        gkvn_ref[...] += gkvn
        dkr = dkr_ref[0] + dkr_ref[1] + dkr_ref[2] + dkr_ref[3]
        dfr_ref[:, :Q_RANK] = dzq.astype(BF16)
        dfr_ref[:, Q_RANK:Q_RANK + KV_RANK] = dzkv.astype(BF16)
        dfr_ref[:, Q_RANK + KV_RANK:] = _rope(dkr, c, sa, sb, -1.0).astype(BF16)

    ins = (dq_att, dkv_nat, dkr4, zfr, q_norm, wuq_pad, kv_norm, wukv, rc, rsa, rsb)
    in_specs = [_row_spec(tm, HW), _row_spec(tm, HW), pl.BlockSpec((HEADS // 2, tm, LANES), lambda i: (0, i, 0)), _row_spec(tm, FRONT_W),
                _full_spec(q_norm), _full_spec(wuq_pad), _full_spec(kv_norm), _full_spec(wukv),
                _row_spec(tm, LANES), _row_spec(tm, LANES), _row_spec(tm, LANES)]
    outs = [SDS((T, FRONT_W), BF16), SDS((Q_RANK, HW), F32), SDS((KV_RANK, HW), F32), SDS((1, Q_RANK), F32), SDS((1, KV_RANK), F32)]
    out_specs = [_row_spec(tm, FRONT_W)] + [_full_spec(s) for s in outs[1:]]
    return pl.pallas_call(
        body, name="mla_bwd", grid=(T // tm,), in_specs=in_specs, out_specs=out_specs, out_shape=outs,
        compiler_params=_cp(("arbitrary",)),
    )(*ins)


_DZ_COLS = ((GM, GA), (GA, UP), (UP, GP), (GP, ZQ), (ZQ, ZTOT))


def _in_proj_bwd_x(dzs, x2, dh, norm_in, w_in_pad, tm):
    T = x2.shape[0]

    def body(d0, d1, d2, d3, d4, x_ref, dh_ref, nin_ref, win_ref, gx_ref, gnin_ref):
        @pl.when(pl.program_id(0) == 0)
        def _():
            gnin_ref[...] = jnp.zeros_like(gnin_ref)

        dhn = None
        for ref, (lo, hi) in zip((d0, d1, d2, d3, d4), _DZ_COLS):
            t = _mm_nt(ref[...], win_ref[:, lo:hi])
            dhn = t if dhn is None else dhn + t
        dx, gnin = _rms_bwd(x_ref[...], nin_ref[...], dhn)
        gnin_ref[...] += gnin
        gx_ref[...] = dx + dh_ref[...]

    in_specs = [_row_spec(tm, hi - lo) for lo, hi in _DZ_COLS] + [_row_spec(tm, D_MODEL), _row_spec(tm, D_MODEL),
                                                                  _full_spec(norm_in), _full_spec(w_in_pad)]
    outs = [SDS((T, D_MODEL), F32), SDS((1, D_MODEL), F32)]
    return pl.pallas_call(
        body, name="in_proj_bwd_x", grid=(T // tm,), in_specs=in_specs, out_specs=[_row_spec(tm, D_MODEL), _full_spec(outs[1])],
        out_shape=outs, compiler_params=_cp(("arbitrary",)),
    )(*dzs, x2, dh, norm_in, w_in_pad)


def _in_proj_bwd_w(dzs, hn, tm):
    T = hn.shape[0]

    def body(d0, d1, d2, d3, d4, hn_ref, gw_ref):
        @pl.when(pl.program_id(0) == 0)
        def _():
            gw_ref[...] = jnp.zeros_like(gw_ref)

        hn_v = hn_ref[...]
        for ref, (lo, hi) in zip((d0, d1, d2, d3, d4), _DZ_COLS):
            gw_ref[:, lo:hi] += _mm_tn(hn_v, ref[...])

    in_specs = [_row_spec(tm, hi - lo) for lo, hi in _DZ_COLS] + [_row_spec(tm, D_MODEL)]
    out = SDS((D_MODEL, ZTOT), F32)
    return pl.pallas_call(
        body, name="in_proj_bwd_w", grid=(T // tm,), in_specs=in_specs, out_specs=_full_spec(out), out_shape=out,
        compiler_params=_cp(("arbitrary",)),
    )(*dzs, hn)


def _pad_w_in(w_in):
    def piece(k):
        lo, n = ORIG[k]
        return w_in[:, lo:lo + n]
    z = functools.partial(jnp.zeros, dtype=w_in.dtype)
    rows = w_in.shape[0]
    return jnp.concatenate([piece("gm"), piece("ga"), piece("up"), piece("gp"), piece("zq"), piece("zkv"),
                            z((rows, NOPE)), piece("zkr"), z((rows, LANES - NOPE - ROPE))], axis=1)


def _unpad_g_in(g):
    return jnp.concatenate([g[:, ZQ:ZKV], g[:, ZKV:ZKR], g[:, ZKR + NOPE:ZKR + NOPE + ROPE], g[:, GA:UP], g[:, UP:GP], g[:, GP:ZQ],
                            g[:, GM:GA]], axis=1)


def _local_step(x2, tgt, norm_in, w_in, q_norm, w_uq, kv_norm, w_ukv, pool_w, pool_scale, w_ba, w_bp, w_out, norm_final):
    T = x2.shape[0]
    tm = min(256, T)
    tq = min(256, T)
    row = lambda v: v.reshape(1, -1)
    w_in_pad = _pad_w_in(w_in)
    wuq_pad = jnp.pad(w_uq, ((0, 0), (0, 0), (0, HEAD_PAD - NOPE - ROPE))).reshape(Q_RANK, HW)
    wukv = w_ukv.reshape(KV_RANK, HW)
    rc, rsa, rsb = _rope_tables(T)

    hn, zgm, zga, zup, zgp, zfr, q_att, k_att, kv = _in_proj(
        x2, row(norm_in), w_in_pad, row(q_norm), wuq_pad, row(kv_norm), wukv, rc, rsa, rsb, tm)
    o, lse = _attn_fwd(q_att, k_att, kv, tq)
    ypool = _pool_fwd(zup, zgp, pool_w, row(pool_scale))
    loss8, dh, dgm, dya, dyp, g_wout, g_wba, g_wbp, g_nf = _tail(
        x2, tgt, o, zga, ypool, zgm, w_ba, w_bp, w_out, row(norm_final), tm)
    dup, dgp, g_pool_w, g_pool_scale = _pool_bwd(zup, zgp, dyp, pool_w, row(pool_scale))
    dq_att, dga, dcap, doop = _attn_bwd_dq(q_att, k_att, kv, o, zga, dya, lse, tq)
    to_rows = lambda a: a[:, ::VDIM].T.reshape(HEADS // 2, 2, T)
    dkv_nat, dkr4 = _attn_bwd_dkv(q_att, k_att, kv, doop, to_rows(lse), to_rows(dcap), tq)
    dfr, g_wuq_pad, g_wukv, g_qn, g_kvn = _mla_bwd(
        dq_att, dkv_nat, dkr4, zfr, row(q_norm), wuq_pad, row(kv_norm), wukv, rc, rsa, rsb, tm)
    dzs = (dgm, dga, dup, dgp, dfr)
    grad_x, g_nin = _in_proj_bwd_x(dzs, x2, dh, row(norm_in), w_in_pad, tm)
    g_win_pad = _in_proj_bwd_w(dzs, hn, tm)

    grads = dict(
        norm_in=g_nin.reshape(-1), w_in=_unpad_g_in(g_win_pad), q_norm=g_qn.reshape(-1),
        w_uq=g_wuq_pad.reshape(Q_RANK, HEADS, HEAD_PAD)[:, :, :NOPE + ROPE], kv_norm=g_kvn.reshape(-1),
        w_ukv=g_wukv.reshape(KV_RANK, HEADS, NOPE + VDIM), pool_w=g_pool_w, pool_scale=g_pool_scale.reshape(-1),
        w_branch_attn=g_wba, w_branch_pool=g_wbp, w_out=g_wout, norm_final=g_nf.reshape(-1))
    return loss8[0, 0], grad_x, grads


MESH_ID = pl.DeviceIdType.MESH
VMEM_SPEC = pl.BlockSpec(memory_space=pltpu.VMEM)
HBM_SPEC = pl.BlockSpec(memory_space=pl.ANY)


def _mesh_pos():
    return lax.axis_index("x"), lax.axis_index("y"), lax.axis_index("c")


def _slot(px, py, pc):
    return 4 * px + 2 * py + pc


def _all_gather_bf16(shards):
    n = len(shards)

    def body(*refs):
        ins, outs, stage = refs[:n], refs[n:2 * n], refs[2 * n:3 * n]
        send_sems, recv_sems, local_sems = refs[3 * n:]
        x, y, c = _mesh_pos()
        me, sibling = (x, y, c), (x, y, 1 - c)
        chips = [(1 - x, y), (x, 1 - y), (1 - x, 1 - y)]

        def copy(a, k, block, to, from_stage=False):
            dst = outs[a].at[_slot(*block)]
            return pltpu.make_async_remote_copy(
                src_ref=stage[a] if from_stage else dst, dst_ref=dst, send_sem=send_sems.at[7 * a + k],
                recv_sem=recv_sems.at[7 * a + k], device_id=to, device_id_type=MESH_ID)

        for a in range(n):
            stage[a][...] = ins[a][...].astype(BF16)
        mine = [pltpu.make_async_copy(stage[a], outs[a].at[_slot(*me)], local_sems.at[a]) for a in range(n)]
        for cp in mine:
            cp.start()
        first = []
        for a in range(n):
            first.append(copy(a, 0, me, sibling, True))
            first += [copy(a, 1 + j, me, (*chip, c), True) for j, chip in enumerate(chips)]
        for cp in first:
            cp.start()
        passed = []
        for a in range(n):
            for j, chip in enumerate(chips):
                copy(a, 1 + j, (*chip, c), me).wait_recv()
                passed.append(copy(a, 4 + j, (*chip, c), sibling))
                passed[-1].start()
        for a in range(n):
            copy(a, 0, sibling, me).wait_recv()
            for j, chip in enumerate(chips):
                copy(a, 4 + j, (*chip, 1 - c), me).wait_recv()
        for cp in first + passed:
            cp.wait_send()
        for cp in mine:
            cp.wait()

    return pl.pallas_call(
        body, name="all_gather_weights",
        in_specs=[VMEM_SPEC] * n, out_specs=[HBM_SPEC] * n,
        out_shape=[SDS((N_DEV,) + s.shape, BF16) for s in shards],
        scratch_shapes=[pltpu.VMEM(s.shape, BF16) for s in shards]
        + [pltpu.SemaphoreType.DMA((7 * n,)), pltpu.SemaphoreType.DMA((7 * n,)), pltpu.SemaphoreType.DMA((n,))],
        compiler_params=_cp(),
    )(*shards)


def _tree_sum(land, out_ref):
    rows = land.shape[1]
    step = 128 if rows % 128 == 0 and rows > 128 else rows

    def chunk(i, carry):
        rs = pl.ds(pl.multiple_of(i * step, step), step)
        v = [land[k, rs, :].astype(F32) for k in range(N_DEV)]
        out_ref[rs, :] = ((v[0] + v[1]) + (v[2] + v[3])) + ((v[4] + v[5]) + (v[6] + v[7]))
        return carry

    lax.fori_loop(0, rows // step, chunk, 0)


def _reduce_scatter(slabs, packed):
    n = len(slabs)

    def body(*refs):
        slab_refs, packed_ref = refs[:n], refs[n]
        out_refs, ptot_ref = refs[n + 1:2 * n + 1], refs[2 * n + 1]
        land, pland = refs[2 * n + 2:3 * n + 2], refs[3 * n + 2]
        send_sems, recv_sems, local_sems = refs[3 * n + 3:]
        x, y, c = _mesh_pos()

        def peer(r):
            return (1 - x if r & 4 else x, 1 - y if r & 2 else y, 1 - c if r & 1 else c)

        copies = []
        for a in range(n):
            copies.append(pltpu.make_async_copy(slab_refs[a].at[_slot(x, y, c)], land[a].at[0], local_sems.at[a]))
            for r in range(1, N_DEV):
                copies.append(pltpu.make_async_remote_copy(
                    src_ref=slab_refs[a].at[_slot(*peer(r))], dst_ref=land[a].at[r], send_sem=send_sems.at[7 * a + r - 1],
                    recv_sem=recv_sems.at[7 * a + r - 1], device_id=peer(r), device_id_type=MESH_ID))
        pland[0] = packed_ref[...]
        for r in range(1, N_DEV):
            copies.append(pltpu.make_async_remote_copy(
                src_ref=packed_ref, dst_ref=pland.at[r], send_sem=send_sems.at[7 * n + r - 1],
                recv_sem=recv_sems.at[7 * n + r - 1], device_id=peer(r), device_id_type=MESH_ID))
        for cp in copies:
            cp.start()
        for cp in copies:
            cp.wait()
        for a in range(n):
            _tree_sum(land[a], out_refs[a])
        _tree_sum(pland, ptot_ref)

    shapes = [s.shape[1:] for s in slabs]
    return pl.pallas_call(
        body, name="reduce_scatter_grads",
        in_specs=[HBM_SPEC] * n + [VMEM_SPEC], out_specs=[VMEM_SPEC] * (n + 1),
        out_shape=[SDS(s, F32) for s in shapes] + [SDS(packed.shape, F32)],
        scratch_shapes=[pltpu.VMEM((N_DEV,) + s, BF16) for s in shapes] + [pltpu.VMEM((N_DEV,) + packed.shape, F32)]
        + [pltpu.SemaphoreType.DMA((7 * (n + 1),)), pltpu.SemaphoreType.DMA((7 * (n + 1),)), pltpu.SemaphoreType.DMA((n,))],
        compiler_params=_cp(),
    )(*slabs, packed)


def _adamw(ws, gs, ms, vs):
    n = len(ws)

    def body(*refs):
        for k in range(n):
            w, g, m, v = (refs[j * n + k][...] for j in range(4))
            d_ref, nm_ref, nv_ref = (refs[(4 + j) * n + k] for j in range(3))
            m = ADAM_B1 * m + (1.0 - ADAM_B1) * g
            v = ADAM_B2 * v + (1.0 - ADAM_B2) * jnp.square(g)
            m_hat = m / (1.0 - ADAM_B1 ** ADAM_STEP)
            v_hat = v / (1.0 - ADAM_B2 ** ADAM_STEP)
            d_ref[...] = -ADAM_LR * (m_hat / (jnp.sqrt(v_hat) + ADAM_EPS) + ADAM_WD * w)
            nm_ref[...] = m
            nv_ref[...] = v

    outs = pl.pallas_call(
        body, name="adamw", in_specs=[VMEM_SPEC] * (4 * n), out_specs=[VMEM_SPEC] * (3 * n),
        out_shape=[SDS(w.shape, F32) for w in ws] * 3, compiler_params=_cp(),
    )(*ws, *gs, *ms, *vs)
    return outs[:n], outs[n:2 * n], outs[2 * n:]


WEIGHTS = ("norm_in", "w_in", "q_norm", "w_uq", "kv_norm", "w_ukv", "pool_w", "pool_scale", "w_branch_attn", "w_branch_pool",
           "w_out", "norm_final")
SHARDED = ("w_in", "w_branch_attn", "w_branch_pool", "w_out", "w_uq", "w_ukv")
REPLICATED = ("norm_in", "q_norm", "kv_norm", "pool_scale", "norm_final", "pool_w")
SUBLANES = 8


def _cols_to_slabs(g):
    r = g.shape[0]
    return g.reshape(r, N_DEV, -1).transpose(1, 0, 2)


def _slabs_to_cols(s):
    return s.transpose(1, 0, 2).reshape(s.shape[1], -1)


def _pack_rows(a):
    a = a.reshape(-1, LANES)
    return jnp.pad(a, ((0, -a.shape[0] % SUBLANES), (0, 0)))


def kernel(x, norm_in, w_in, q_norm, w_uq, kv_norm, w_ukv, pool_w, pool_scale, w_branch_attn, w_branch_pool, w_out, norm_final, loss_target, m_norm_in, m_w_in, m_q_norm, m_w_uq, m_kv_norm, m_w_ukv, m_pool_w, m_pool_scale, m_w_branch_attn, m_w_branch_pool, m_w_out, m_norm_final, v_norm_in, v_w_in, v_q_norm, v_w_uq, v_kv_norm, v_w_ukv, v_pool_w, v_pool_scale, v_w_branch_attn, v_w_branch_pool, v_w_out, v_norm_final):
    w = dict(norm_in=norm_in, w_in=w_in, q_norm=q_norm, w_uq=w_uq, kv_norm=kv_norm, w_ukv=w_ukv, pool_w=pool_w, pool_scale=pool_scale,
             w_branch_attn=w_branch_attn, w_branch_pool=w_branch_pool, w_out=w_out, norm_final=norm_final)
    m = dict(norm_in=m_norm_in, w_in=m_w_in, q_norm=m_q_norm, w_uq=m_w_uq, kv_norm=m_kv_norm, w_ukv=m_w_ukv, pool_w=m_pool_w,
             pool_scale=m_pool_scale, w_branch_attn=m_w_branch_attn, w_branch_pool=m_w_branch_pool, w_out=m_w_out, norm_final=m_norm_final)
    v = dict(norm_in=v_norm_in, w_in=v_w_in, q_norm=v_q_norm, w_uq=v_w_uq, kv_norm=v_kv_norm, w_ukv=v_w_ukv, pool_w=v_pool_w,
             pool_scale=v_pool_scale, w_branch_attn=v_w_branch_attn, w_branch_pool=v_w_branch_pool, w_out=v_w_out, norm_final=v_norm_final)

    def as2d(name, a):
        if name in ("w_uq", "w_ukv"):
            return a.reshape(a.shape[0], -1)
        if name == "pool_w":
            return a.reshape(-1, GROUP)
        return a.reshape(1, -1) if a.ndim == 1 else a

    full = dict(zip(SHARDED, _all_gather_bf16([as2d(k, w[k]) for k in SHARDED])))
    loss_part, grad_x, grads = _local_step(
        x[0], loss_target[0], norm_in, _slabs_to_cols(full["w_in"]), q_norm,
        full["w_uq"].reshape(Q_RANK, HEADS, NOPE + ROPE), kv_norm, full["w_ukv"].reshape(KV_RANK, HEADS, NOPE + VDIM),
        pool_w, pool_scale, _slabs_to_cols(full["w_branch_attn"]), _slabs_to_cols(full["w_branch_pool"]),
        full["w_out"].reshape(D_MODEL, D_MODEL), norm_final)

    slabs = [
        _cols_to_slabs(grads["w_in"]), _cols_to_slabs(grads["w_branch_attn"]), _cols_to_slabs(grads["w_branch_pool"]),
        grads["w_out"].reshape(N_DEV, D_MODEL // N_DEV, D_MODEL), grads["w_uq"].reshape(N_DEV, Q_RANK // N_DEV, -1),
        grads["w_ukv"].reshape(N_DEV, KV_RANK // N_DEV, -1)]
    parts = [_pack_rows(grads[k]) for k in REPLICATED]
    parts.append(_pack_rows(jnp.zeros((LANES,), F32).at[0].set(loss_part)))
    *sums, ptot = _reduce_scatter([s.astype(BF16) for s in slabs], jnp.concatenate(parts, axis=0))
    g2d = dict(zip(SHARDED, sums))
    row = 0
    for k, part in zip(REPLICATED, parts):
        g2d[k] = as2d(k, ptot[row:row + w[k].size // LANES].reshape(w[k].shape))
        row += part.shape[0]
    loss = ptot[row, 0]

    deltas, new_m, new_v = _adamw([as2d(k, w[k]) for k in WEIGHTS], [g2d[k] for k in WEIGHTS],
                                  [as2d(k, m[k]) for k in WEIGHTS], [as2d(k, v[k]) for k in WEIGHTS])
    shaped = lambda arrs: [a.reshape(w[k].shape) for k, a in zip(WEIGHTS, arrs)]
    return (loss, grad_x[None], *shaped([g2d[k] for k in WEIGHTS]), *shaped(deltas), *shaped(new_m), *shaped(new_v))
```

```python
import functools

import jax
import jax.numpy as jnp
import numpy as np
from jax import lax
from jax.experimental import pallas as pl
from jax.experimental.pallas import tpu as pltpu

F32 = jnp.float32
BF16 = jnp.bfloat16
SDS = jax.ShapeDtypeStruct

D_MODEL = 1024
HEADS = 8
NOPE = 64
ROPE = 32
VDIM = 64
Q_RANK = 384
KV_RANK = 256
MLA_W = HEADS * VDIM
POOL_W = 512
POOL_GROUPS = 4
GROUP = POOL_W // POOL_GROUPS
CHUNK = 64
ROPE_THETA = 10000.0
EPS = 1e-6
SCALE = (NOPE + ROPE) ** -0.5
LOG2E = 1.4426950408889634
LN2 = 0.6931471805599453
QK_SCALE_LOG2 = SCALE * LOG2E
IN_TOTAL = 4256
ADAM_LR, ADAM_B1, ADAM_B2, ADAM_EPS, ADAM_WD, ADAM_STEP = 0.001, 0.9, 0.999, 1e-08, 0.01, 10

N_DEV = 8
LANES = 128
HEAD_PAD = LANES
HW = HEADS * HEAD_PAD

GM, GA, UP, GP, ZQ, ZKV, ZKR, ZTOT = 0, 2048, 2560, 3072, 3584, 3968, 4224, 4352
FRONT_W = ZTOT - ZQ
ORIG = dict(zq=(0, 384), zkv=(384, 256), zkr=(640, 32), ga=(672, 512), up=(1184, 512), gp=(1696, 512), gm=(2208, 2048))

VMEM_LIMIT = 56 * 1024 * 1024


def _cp(sem=None, **kw):
    if sem is not None:
        kw["dimension_semantics"] = sem
    return pltpu.CompilerParams(vmem_limit_bytes=VMEM_LIMIT, **kw)


def _mm(a, b):
    return lax.dot_general(a, b, (((1,), (0,)), ((), ())), preferred_element_type=F32)


def _mm_nt(a, b):
    return lax.dot_general(a, b, (((1,), (1,)), ((), ())), preferred_element_type=F32)


def _mm_tn(a, b):
    return lax.dot_general(a, b, (((0,), (0,)), ((), ())), preferred_element_type=F32)


def _row_spec(tm, w):
    return pl.BlockSpec((tm, w), lambda i: (i, 0))


def _full_spec(a):
    nd = len(a.shape)
    return pl.BlockSpec(a.shape, lambda *_: (0,) * nd)


def _rope(v, c, sa, sb, sign):
    n = v.shape[-1]
    reps = n // LANES
    if reps > 1:
        c, sa, sb = (jnp.tile(t, (1, reps)) for t in (c, sa, sb))
    up = pltpu.roll(v, n - ROPE // 2, 1)
    dn = pltpu.roll(v, ROPE // 2, 1)
    return v * c + sign * (up * sa + dn * sb)


def _rope_tables(T):
    half = ROPE // 2
    inv_freq = ROPE_THETA ** (-jnp.arange(half, dtype=F32) / half)
    ang = jnp.arange(T, dtype=F32)[:, None] * inv_freq[None, :]
    cos, sin = jnp.cos(ang), jnp.sin(ang)
    z16 = jnp.zeros((T, half), F32)
    z32 = jnp.zeros((T, LANES - NOPE - ROPE), F32)
    c = jnp.concatenate([jnp.ones((T, NOPE), F32), cos, cos, z32], axis=1)
    sa = jnp.concatenate([jnp.zeros((T, NOPE), F32), -sin, z16, z32], axis=1)
    sb = jnp.concatenate([jnp.zeros((T, NOPE), F32), z16, sin, z32], axis=1)
    return c, sa, sb


def _silu_parts(g):
    sg = jax.nn.sigmoid(g)
    return g * sg, sg + g * sg * (1.0 - sg)


def _in_proj(x2, norm_in, w_in_pad, q_norm, wuq_pad, kv_norm, wukv, rc, rsa, rsb, tm):
    T = x2.shape[0]

    def body(x_ref, nin_ref, win_ref, qn_ref, wuq_ref, kvn_ref, wukv_ref, c_ref, sa_ref, sb_ref,
             hn_ref, zgm_ref, zga_ref, zup_ref, zgp_ref, zfr_ref, q_ref, k_ref, v_ref):
        xf = x_ref[...]
        r = lax.rsqrt(jnp.mean(xf * xf, axis=-1, keepdims=True) + EPS)
        hn = (xf * r * nin_ref[...]).astype(BF16)
        hn_ref[...] = hn
        z = _mm(hn, win_ref[...])
        zgm_ref[...] = z[:, GM:GA]
        zga_ref[...] = z[:, GA:UP]
        zup_ref[...] = z[:, UP:GP]
        zgp_ref[...] = z[:, GP:ZQ]
        zfr_ref[...] = z[:, ZQ:ZTOT]
        zq, zkv, zkr = z[:, ZQ:ZKV], z[:, ZKV:ZKR], z[:, ZKR:ZTOT]
        c, sa, sb = c_ref[...], sa_ref[...], sb_ref[...]
        rq = lax.rsqrt(jnp.mean(zq * zq, axis=-1, keepdims=True) + EPS)
        cq = (zq * rq * qn_ref[...]).astype(BF16)
        q = _rope(_mm(cq, wuq_ref[...]), c, sa, sb, 1.0)
        q_ref[...] = (q * QK_SCALE_LOG2).astype(BF16)
        rkv = lax.rsqrt(jnp.mean(zkv * zkv, axis=-1, keepdims=True) + EPS)
        ckv = (zkv * rkv * kvn_ref[...]).astype(BF16)
        kv = _mm(ckv, wukv_ref[...])
        kr = _rope(zkr, c, sa, sb, 1.0)
        lane = lax.broadcasted_iota(jnp.int32, kv.shape, 1) % LANES
        k_ref[...] = jnp.where(lane < NOPE, kv, jnp.tile(kr, (1, HEADS))).astype(BF16)
        v_ref[...] = jnp.where(lane < NOPE, 1.0, kv).astype(BF16)

    ins = (x2, norm_in, w_in_pad, q_norm, wuq_pad, kv_norm, wukv, rc, rsa, rsb)
    in_specs = [_row_spec(tm, D_MODEL), _full_spec(norm_in), _full_spec(w_in_pad), _full_spec(q_norm), _full_spec(wuq_pad),
                _full_spec(kv_norm), _full_spec(wukv), _row_spec(tm, LANES), _row_spec(tm, LANES), _row_spec(tm, LANES)]
    widths = [(D_MODEL, BF16), (GA - GM, F32), (UP - GA, F32), (GP - UP, F32), (ZQ - GP, F32), (FRONT_W, F32),
              (HW, BF16), (HW, BF16), (HW, BF16)]
    return pl.pallas_call(
        body, name="in_proj", grid=(T // tm,), in_specs=in_specs,
        out_specs=[_row_spec(tm, w) for w, _ in widths],
        out_shape=[SDS((T, w), dt) for w, dt in widths],
        compiler_params=_cp(("parallel",)),
    )(*ins)


def _diag_mask(tq, transposed):
    r = lax.broadcasted_iota(jnp.int32, (tq, tq), 0) // CHUNK
    c = lax.broadcasted_iota(jnp.int32, (tq, tq), 1) // CHUNK
    return (r <= c) if transposed else (c <= r)


_HEAD_LANES = (slice(0, LANES), slice(LANES, 2 * LANES))


def _attn_fwd(q_att, k_att, v_att, tq):
    T = q_att.shape[0]

    def body(q_ref, k_ref, v_ref, o_ref, lse_ref):
        i = pl.program_id(1)
        mask = _diag_mask(tq, False)
        lane = lax.broadcasted_iota(jnp.int32, (tq, LANES), 1)
        qs = [q_ref[:, hs] for hs in _HEAD_LANES]

        def step(j, carry, masked):
            rows = pl.ds(pl.multiple_of(j * tq, tq), tq)
            out = []
            for (m, acc), qh, hs in zip(carry, qs, _HEAD_LANES):
                s = _mm_nt(qh, k_ref[rows, hs])
                if masked:
                    s = jnp.where(mask, s, -jnp.inf)
                m_new = jnp.maximum(m, jnp.max(s, axis=-1, keepdims=True))
                p = jnp.exp2(s - m_new).astype(BF16)
                out.append((m_new, jnp.exp2(m - m_new) * acc + _mm(p, v_ref[rows, hs])))
            return tuple(out)

        init = ((jnp.full((tq, 1), -jnp.inf, F32), jnp.zeros((tq, LANES), F32)),) * 2
        carry = lax.fori_loop(0, i, functools.partial(step, masked=False), init)
        (ma, acca), (mb, accb) = step(i, carry, True)
        la, lb = acca[:, :1], accb[:, :1]
        o_ref[...] = jnp.where(lane < VDIM, pltpu.roll(acca / la, VDIM, 1), accb / lb)
        lse_ref[...] = jnp.where(lane < VDIM, ma + jnp.log2(la), mb + jnp.log2(lb))

    qspec = pl.BlockSpec((tq, 2 * LANES), lambda p, i: (i, p))
    kspec = pl.BlockSpec((T, 2 * LANES), lambda p, i: (0, p))
    ospec = pl.BlockSpec((tq, LANES), lambda p, i: (i, p))
    return pl.pallas_call(
        body, name="attn_fwd", grid=(HEADS // 2, T // tq), in_specs=[qspec, kspec, kspec], out_specs=[ospec, ospec],
        out_shape=[SDS((T, MLA_W), F32), SDS((T, MLA_W), F32)],
        compiler_params=_cp(("parallel", "parallel")),
    )(q_att, k_att, v_att)


def _pick(g, vals):
    out = vals[-1]
    for k in range(len(vals) - 2, -1, -1):
        out = jnp.where(g == k, vals[k], out)
    return out


def _window_sum(u, g, forward):
    T = u.shape[0]
    row = lax.broadcasted_iota(jnp.int32, u.shape, 0)

    def sh(s, k):
        if forward:
            return jnp.where(row >= k, pltpu.roll(s, k, 0), 0.0)
        return jnp.where(row < T - k, pltpu.roll(s, T - k, 0), 0.0)

    sums, s = [], u
    for k in (1, 2, 4, 8):
        s = s + sh(s, k)
        sums.append(s)
    return _pick(g, sums)


def _pool_count(shape, g):
    row = lax.broadcasted_iota(jnp.int32, shape, 0)
    return jnp.minimum(row + 1, lax.shift_left(jnp.int32(2), g)).astype(F32)


def _pool_fwd(zup, zgp, pool_w, pool_scale):
    T = zup.shape[0]

    def body(u_ref, g_ref, w_ref, sc_ref, y_ref):
        g = pl.program_id(0)
        u = u_ref[...]
        d = _window_sum(u, g, True) / _pool_count(u.shape, g) - u
        lin = _mm(d.astype(BF16), w_ref[0].astype(BF16))
        silu, _ = _silu_parts(g_ref[...])
        y_ref[...] = (lin * sc_ref[...] * silu).astype(BF16)

    col = pl.BlockSpec((T, GROUP), lambda g: (0, g))
    return pl.pallas_call(
        body, name="pool_fwd", grid=(POOL_GROUPS,),
        in_specs=[col, col, pl.BlockSpec((1, GROUP, GROUP), lambda g: (g, 0, 0)), pl.BlockSpec((1, GROUP), lambda g: (0, g))],
        out_specs=col, out_shape=SDS((T, POOL_W), BF16), compiler_params=_cp(("parallel",)),
    )(zup, zgp, pool_w, pool_scale)


def _pool_bwd(zup, zgp, dyp, pool_w, pool_scale):
    T = zup.shape[0]

    def body(u_ref, g_ref, dy_ref, w_ref, sc_ref, du_ref, dg_ref, gw_ref, gsc_ref):
        g = pl.program_id(0)
        u = u_ref[...]
        cnt = _pool_count(u.shape, g)
        d = (_window_sum(u, g, True) / cnt - u).astype(BF16)
        wb = w_ref[0].astype(BF16)
        lin = _mm(d, wb)
        sc = sc_ref[...]
        silu, dsilu = _silu_parts(g_ref[...])
        dy = dy_ref[...]
        dg_ref[...] = (dy * lin * sc * dsilu).astype(BF16)
        dpre = dy * silu
        gsc_ref[...] = jnp.sum(dpre * lin, axis=0, keepdims=True)
        dlin = (dpre * sc).astype(BF16)
        gw_ref[0] = _mm_tn(d, dlin)
        dd = _mm_nt(dlin, wb)
        du_ref[...] = (_window_sum(dd / cnt, g, False) - dd).astype(BF16)

    col = pl.BlockSpec((T, GROUP), lambda g: (0, g))
    wspec = pl.BlockSpec((1, GROUP, GROUP), lambda g: (g, 0, 0))
    vspec = pl.BlockSpec((1, GROUP), lambda g: (0, g))
    return pl.pallas_call(
        body, name="pool_bwd", grid=(POOL_GROUPS,), in_specs=[col, col, col, wspec, vspec], out_specs=[col, col, wspec, vspec],
        out_shape=[SDS((T, POOL_W), BF16), SDS((T, POOL_W), BF16), SDS((POOL_GROUPS, GROUP, GROUP), F32), SDS((1, POOL_W), F32)],
        compiler_params=_cp(("parallel",)),
    )(zup, zgp, dyp, pool_w, pool_scale)


def _tail(x2, tgt, o, zga, ypool, zgm, wba, wbp, wout, norm_final, tm):
    T = x2.shape[0]

    def body(x_ref, tgt_ref, o_ref, zga_ref, yp_ref, zgm_ref, wba_ref, wbp_ref, wout_ref, nf_ref,
             loss_ref, dh_ref, dgm_ref, dya_ref, dyp_ref, gwout_ref, gwba_ref, gwbp_ref, gnf_ref):
        @pl.when(pl.program_id(0) == 0)
        def _():
            for ref in (loss_ref, gwout_ref, gwba_ref, gwbp_ref, gnf_ref):
                ref[...] = jnp.zeros_like(ref)

        silu, _ = _silu_parts(zga_ref[...])
        ya = (o_ref[...] * silu).astype(BF16)
        yp = yp_ref[...]
        wba_v, wbp_v, wout_v = wba_ref[...], wbp_ref[...], wout_ref[...]
        a = _mm(ya, wba_v)
        p = _mm(yp, wbp_v)
        gate = jax.nn.sigmoid(zgm_ref[...])
        ga, gp = gate[:, :D_MODEL], gate[:, D_MODEL:]
        mg = (ga * a + gp * p).astype(BF16)
        h = x_ref[...] + _mm(mg, wout_v)
        r = lax.rsqrt(jnp.mean(h * h, axis=-1, keepdims=True) + EPS)
        gf = nf_ref[...]
        hr = h * r
        e = hr * gf - tgt_ref[...]
        loss_ref[...] += (0.5 / D_MODEL) * jnp.sum(e * e)
        dy = e * (1.0 / D_MODEL)
        gnf_ref[...] += jnp.sum(dy * hr, axis=0, keepdims=True)
        u = dy * gf
        dh = r * (u - hr * jnp.mean(u * hr, axis=-1, keepdims=True))
        dh_ref[...] = dh
        dhb = dh.astype(BF16)
        dmg = _mm_nt(dhb, wout_v)
        gwout_ref[...] += _mm_tn(mg, dhb)
        dgm_ref[:, :D_MODEL] = (dmg * a * ga * (1.0 - ga)).astype(BF16)
        dgm_ref[:, D_MODEL:] = (dmg * p * gp * (1.0 - gp)).astype(BF16)
        dab = (dmg * ga).astype(BF16)
        dpb = (dmg * gp).astype(BF16)
        dya_ref[...] = _mm_nt(dab, wba_v)
        gwba_ref[...] += _mm_tn(ya, dab)
        dyp_ref[...] = _mm_nt(dpb, wbp_v)
        gwbp_ref[...] += _mm_tn(yp, dpb)

    ins = (x2, tgt, o, zga, ypool, zgm, wba, wbp, wout, norm_final)
    in_specs = [_row_spec(tm, D_MODEL), _row_spec(tm, D_MODEL), _row_spec(tm, MLA_W), _row_spec(tm, MLA_W), _row_spec(tm, POOL_W),
                _row_spec(tm, 2 * D_MODEL), _full_spec(wba), _full_spec(wbp), _full_spec(wout), _full_spec(norm_final)]
    outs = [SDS((8, LANES), F32), SDS((T, D_MODEL), F32), SDS((T, 2 * D_MODEL), BF16), SDS((T, MLA_W), F32), SDS((T, POOL_W), F32),
            SDS((D_MODEL, D_MODEL), F32), SDS((MLA_W, D_MODEL), F32), SDS((POOL_W, D_MODEL), F32), SDS((1, D_MODEL), F32)]
    out_specs = [_full_spec(outs[0]), _row_spec(tm, D_MODEL), _row_spec(tm, 2 * D_MODEL), _row_spec(tm, MLA_W), _row_spec(tm, POOL_W),
                 _full_spec(outs[5]), _full_spec(outs[6]), _full_spec(outs[7]), _full_spec(outs[8])]
    return pl.pallas_call(
        body, name="tail", grid=(T // tm,), in_specs=in_specs, out_specs=out_specs, out_shape=outs,
        compiler_params=_cp(("arbitrary",)),
    )(*ins)


def _attn_bwd_dq(q_att, k_att, v_att, o, zga, dya, lse, tq):
    T = q_att.shape[0]

    def body(q_ref, k_ref, v_ref, o_ref, zga_ref, dya_ref, lse_ref, dq_ref, dga_ref, dcap_ref, doop_ref):
        i = pl.program_id(1)
        mask = _diag_mask(tq, False)
        lane = lax.broadcasted_iota(jnp.int32, (tq, LANES), 1)
        lo = lane < VDIM
        silu, dsilu = _silu_parts(zga_ref[...])
        dya_v, o_v = dya_ref[...], o_ref[...]
        do = dya_v * silu
        dga_ref[...] = (dya_v * o_v * dsilu).astype(BF16)
        prod = do * o_v
        dcap_a = jnp.sum(jnp.where(lo, prod, 0.0), axis=-1, keepdims=True)
        dcap_b = jnp.sum(jnp.where(lo, 0.0, prod), axis=-1, keepdims=True)
        dcap_ref[...] = jnp.where(lo, dcap_a, dcap_b)
        doops = (pltpu.roll(jnp.where(lo, do, 0.0), VDIM, 1).astype(BF16), jnp.where(lo, 0.0, do).astype(BF16))
        lse_v = lse_ref[...]
        lses = (lse_v[:, :1], lse_v[:, VDIM:VDIM + 1])
        dcaps = (dcap_a, dcap_b)
        qs = [q_ref[:, hs] for hs in _HEAD_LANES]
        for hs, doop in zip(_HEAD_LANES, doops):
            doop_ref[:, hs] = doop

        def step(j, dqs, masked):
            rows = pl.ds(pl.multiple_of(j * tq, tq), tq)
            out = []
            for dq, qh, doop, lse_h, dcap_h, hs in zip(dqs, qs, doops, lses, dcaps, _HEAD_LANES):
                kh = k_ref[rows, hs]
                p = jnp.exp2(_mm_nt(qh, kh) - lse_h)
                if masked:
                    p = jnp.where(mask, p, 0.0)
                dp = _mm_nt(doop, v_ref[rows, hs])
                out.append(dq + _mm((p * (dp - dcap_h)).astype(BF16), kh))
            return tuple(out)

        zero = jnp.zeros((tq, LANES), F32)
        dqs = step(i, lax.fori_loop(0, i, functools.partial(step, masked=False), (zero, zero)), True)
        for dq, hs in zip(dqs, _HEAD_LANES):
            dq_ref[:, hs] = (dq * SCALE).astype(BF16)

    qspec = pl.BlockSpec((tq, 2 * LANES), lambda p, i: (i, p))
    kspec = pl.BlockSpec((T, 2 * LANES), lambda p, i: (0, p))
    ospec = pl.BlockSpec((tq, LANES), lambda p, i: (i, p))
    return pl.pallas_call(
        body, name="attn_bwd_dq", grid=(HEADS // 2, T // tq),
        in_specs=[qspec, kspec, kspec, ospec, ospec, ospec, ospec], out_specs=[qspec, ospec, ospec, qspec],
        out_shape=[SDS((T, HW), BF16), SDS((T, MLA_W), BF16), SDS((T, MLA_W), F32), SDS((T, HW), BF16)],
        compiler_params=_cp(("parallel", "parallel")),
    )(q_att, k_att, v_att, o, zga, dya, lse)


def _attn_bwd_dkv(q_att, k_att, v_att, doop, lse_rows, dcap_rows, tq):
    T = q_att.shape[0]
    nq = T // tq

    def body(q_ref, k_ref, v_ref, doop_ref, lse_ref, dcap_ref, dkv_ref, dkr_ref):
        j = pl.program_id(1)
        mask = _diag_mask(tq, True)
        lane = lax.broadcasted_iota(jnp.int32, (tq, LANES), 1)
        ks = [k_ref[:, hs] for hs in _HEAD_LANES]
        vs = [v_ref[:, hs] for hs in _HEAD_LANES]

        def step(i, carry, masked):
            rows = pl.ds(pl.multiple_of(i * tq, tq), tq)
            out = []
            for h, ((dk, dv), kh, vh, hs) in enumerate(zip(carry, ks, vs, _HEAD_LANES)):
                qh = q_ref[rows, hs]
                doop_h = doop_ref[rows, hs]
                pt = jnp.exp2(_mm_nt(kh, qh) - lse_ref[0, h:h + 1, rows])
                if masked:
                    pt = jnp.where(mask, pt, 0.0)
                dv = dv + _mm(pt.astype(BF16), doop_h)
                dpt = _mm_nt(vh, doop_h)
                dst = (pt * (dpt - dcap_ref[0, h:h + 1, rows])).astype(BF16)
                out.append((dk + _mm(dst, qh), dv))
            return tuple(out)

        zero = jnp.zeros((tq, LANES), F32)
        carry = step(j, ((zero, zero), (zero, zero)), True)
        (dka, dva), (dkb, dvb) = lax.fori_loop(j + 1, nq, functools.partial(step, masked=False), carry)
        dka, dkb = dka * LN2, dkb * LN2
        dkv_ref[:, _HEAD_LANES[0]] = jnp.where(lane < NOPE, dka, dva).astype(BF16)
        dkv_ref[:, _HEAD_LANES[1]] = jnp.where(lane < NOPE, dkb, dvb).astype(BF16)
        dkr_ref[0] = jnp.where((lane >= NOPE) & (lane < NOPE + ROPE), dka + dkb, 0.0)

    kspec = pl.BlockSpec((tq, 2 * LANES), lambda p, j: (j, p))
    qspec = pl.BlockSpec((T, 2 * LANES), lambda p, j: (0, p))
    rspec = pl.BlockSpec((1, 2, T), lambda p, j: (p, 0, 0))
    return pl.pallas_call(
        body, name="attn_bwd_dkv", grid=(HEADS // 2, nq),
        in_specs=[qspec, kspec, kspec, qspec, rspec, rspec],
        out_specs=[kspec, pl.BlockSpec((1, tq, LANES), lambda p, j: (p, j, 0))],
        out_shape=[SDS((T, HW), BF16), SDS((HEADS // 2, T, LANES), F32)],
        compiler_params=_cp(("parallel", "parallel")),
    )(q_att, k_att, v_att, doop, lse_rows, dcap_rows)


def _rms_bwd(z, gain, dout):
    r = lax.rsqrt(jnp.mean(z * z, axis=-1, keepdims=True) + EPS)
    zr = z * r
    u = dout * gain
    return r * (u - zr * jnp.mean(u * zr, axis=-1, keepdims=True)), jnp.sum(dout * zr, axis=0, keepdims=True)


def _mla_bwd(dq_att, dkv_nat, dkr4, zfr, q_norm, wuq_pad, kv_norm, wukv, rc, rsa, rsb, tm):
    T = dq_att.shape[0]

    def body(dq_ref, dkv_ref, dkr_ref, zfr_ref, qn_ref, wuq_ref, kvn_ref, wukv_ref, c_ref, sa_ref, sb_ref,
             dfr_ref, gwuq_ref, gwukv_ref, gqn_ref, gkvn_ref):
        @pl.when(pl.program_id(0) == 0)
        def _():
            for ref in (gwuq_ref, gwukv_ref, gqn_ref, gkvn_ref):
                ref[...] = jnp.zeros_like(ref)

        c, sa, sb = c_ref[...], sa_ref[...], sb_ref[...]
        zq, zkv = zfr_ref[:, :Q_RANK], zfr_ref[:, Q_RANK:Q_RANK + KV_RANK]
        qn, kvn = qn_ref[...], kvn_ref[...]
        cq = (zq * lax.rsqrt(jnp.mean(zq * zq, axis=-1, keepdims=True) + EPS) * qn).astype(BF16)
        ckv = (zkv * lax.rsqrt(jnp.mean(zkv * zkv, axis=-1, keepdims=True) + EPS) * kvn).astype(BF16)
        dq = _rope(dq_ref[...].astype(F32), c, sa, sb, -1.0).astype(BF16)
        gwuq_ref[...] += _mm_tn(cq, dq)
        dzq, gqn = _rms_bwd(zq, qn, _mm_nt(dq, wuq_ref[...]))
        gqn_ref[...] += gqn
        dkv = dkv_ref[...]
        gwukv_ref[...] += _mm_tn(ckv, dkv)
        dzkv, gkvn = _rms_bwd(zkv, kvn, _mm_nt(dkv, wukv_ref[...]))
        gkvn_ref[...] += gkvn
        dkr = dkr_ref[0] + dkr_ref[1] + dkr_ref[2] + dkr_ref[3]
        dfr_ref[:, :Q_RANK] = dzq.astype(BF16)
        dfr_ref[:, Q_RANK:Q_RANK + KV_RANK] = dzkv.astype(BF16)
        dfr_ref[:, Q_RANK + KV_RANK:] = _rope(dkr, c, sa, sb, -1.0).astype(BF16)

    ins = (dq_att, dkv_nat, dkr4, zfr, q_norm, wuq_pad, kv_norm, wukv, rc, rsa, rsb)
    in_specs = [_row_spec(tm, HW), _row_spec(tm, HW), pl.BlockSpec((HEADS // 2, tm, LANES), lambda i: (0, i, 0)), _row_spec(tm, FRONT_W),
                _full_spec(q_norm), _full_spec(wuq_pad), _full_spec(kv_norm), _full_spec(wukv),
                _row_spec(tm, LANES), _row_spec(tm, LANES), _row_spec(tm, LANES)]
    outs = [SDS((T, FRONT_W), BF16), SDS((Q_RANK, HW), F32), SDS((KV_RANK, HW), F32), SDS((1, Q_RANK), F32), SDS((1, KV_RANK), F32)]
    out_specs = [_row_spec(tm, FRONT_W)] + [_full_spec(s) for s in outs[1:]]
    return pl.pallas_call(
        body, name="mla_bwd", grid=(T // tm,), in_specs=in_specs, out_specs=out_specs, out_shape=outs,
        compiler_params=_cp(("arbitrary",)),
    )(*ins)


_DZ_COLS = ((GM, GA), (GA, UP), (UP, GP), (GP, ZQ), (ZQ, ZTOT))


def _in_proj_bwd_x(dzs, x2, dh, norm_in, w_in_pad, tm):
    T = x2.shape[0]

    def body(d0, d1, d2, d3, d4, x_ref, dh_ref, nin_ref, win_ref, gx_ref, gnin_ref):
        @pl.when(pl.program_id(0) == 0)
        def _():
            gnin_ref[...] = jnp.zeros_like(gnin_ref)

        dhn = None
        for ref, (lo, hi) in zip((d0, d1, d2, d3, d4), _DZ_COLS):
            t = _mm_nt(ref[...], win_ref[:, lo:hi])
            dhn = t if dhn is None else dhn + t
        dx, gnin = _rms_bwd(x_ref[...], nin_ref[...], dhn)
        gnin_ref[...] += gnin
        gx_ref[...] = dx + dh_ref[...]

    in_specs = [_row_spec(tm, hi - lo) for lo, hi in _DZ_COLS] + [_row_spec(tm, D_MODEL), _row_spec(tm, D_MODEL),
                                                                  _full_spec(norm_in), _full_spec(w_in_pad)]
    outs = [SDS((T, D_MODEL), F32), SDS((1, D_MODEL), F32)]
    return pl.pallas_call(
        body, name="in_proj_bwd_x", grid=(T // tm,), in_specs=in_specs, out_specs=[_row_spec(tm, D_MODEL), _full_spec(outs[1])],
        out_shape=outs, compiler_params=_cp(("arbitrary",)),
    )(*dzs, x2, dh, norm_in, w_in_pad)


def _in_proj_bwd_w(dzs, hn, tm):
    T = hn.shape[0]

    def body(d0, d1, d2, d3, d4, hn_ref, gw_ref):
        @pl.when(pl.program_id(0) == 0)
        def _():
            gw_ref[...] = jnp.zeros_like(gw_ref)

        hn_v = hn_ref[...]
        for ref, (lo, hi) in zip((d0, d1, d2, d3, d4), _DZ_COLS):
            gw_ref[:, lo:hi] += _mm_tn(hn_v, ref[...])

    in_specs = [_row_spec(tm, hi - lo) for lo, hi in _DZ_COLS] + [_row_spec(tm, D_MODEL)]
    out = SDS((D_MODEL, ZTOT), F32)
    return pl.pallas_call(
        body, name="in_proj_bwd_w", grid=(T // tm,), in_specs=in_specs, out_specs=_full_spec(out), out_shape=out,
        compiler_params=_cp(("arbitrary",)),
    )(*dzs, hn)


def _pad_w_in(w_in):
    def piece(k):
        lo, n = ORIG[k]
        return w_in[:, lo:lo + n]
    z = functools.partial(jnp.zeros, dtype=w_in.dtype)
    rows = w_in.shape[0]
    return jnp.concatenate([piece("gm"), piece("ga"), piece("up"), piece("gp"), piece("zq"), piece("zkv"),
                            z((rows, NOPE)), piece("zkr"), z((rows, LANES - NOPE - ROPE))], axis=1)


def _unpad_g_in(g):
    return jnp.concatenate([g[:, ZQ:ZKV], g[:, ZKV:ZKR], g[:, ZKR + NOPE:ZKR + NOPE + ROPE], g[:, GA:UP], g[:, UP:GP], g[:, GP:ZQ],
                            g[:, GM:GA]], axis=1)


def _local_step(x2, tgt, norm_in, w_in, q_norm, w_uq, kv_norm, w_ukv, pool_w, pool_scale, w_ba, w_bp, w_out, norm_final):
    T = x2.shape[0]
    tm = min(256, T)
    tq = min(512, T)
    row = lambda v: v.reshape(1, -1)
    w_in_pad = _pad_w_in(w_in)
    wuq_pad = jnp.pad(w_uq, ((0, 0), (0, 0), (0, HEAD_PAD - NOPE - ROPE))).reshape(Q_RANK, HW)
    wukv = w_ukv.reshape(KV_RANK, HW)
    rc, rsa, rsb = _rope_tables(T)

    hn, zgm, zga, zup, zgp, zfr, q_att, k_att, v_att = _in_proj(
        x2, row(norm_in), w_in_pad, row(q_norm), wuq_pad, row(kv_norm), wukv, rc, rsa, rsb, tm)
    o, lse = _attn_fwd(q_att, k_att, v_att, tq)
    ypool = _pool_fwd(zup, zgp, pool_w, row(pool_scale))
    loss8, dh, dgm, dya, dyp, g_wout, g_wba, g_wbp, g_nf = _tail(
        x2, tgt, o, zga, ypool, zgm, w_ba, w_bp, w_out, row(norm_final), tm)
    dup, dgp, g_pool_w, g_pool_scale = _pool_bwd(zup, zgp, dyp, pool_w, row(pool_scale))
    dq_att, dga, dcap, doop = _attn_bwd_dq(q_att, k_att, v_att, o, zga, dya, lse, tq)
    to_rows = lambda a: a[:, ::VDIM].T.reshape(HEADS // 2, 2, T)
    dkv_nat, dkr4 = _attn_bwd_dkv(q_att, k_att, v_att, doop, to_rows(lse), to_rows(dcap), tq)
    dfr, g_wuq_pad, g_wukv, g_qn, g_kvn = _mla_bwd(
        dq_att, dkv_nat, dkr4, zfr, row(q_norm), wuq_pad, row(kv_norm), wukv, rc, rsa, rsb, tm)
    dzs = (dgm, dga, dup, dgp, dfr)
    grad_x, g_nin = _in_proj_bwd_x(dzs, x2, dh, row(norm_in), w_in_pad, tm)
    g_win_pad = _in_proj_bwd_w(dzs, hn, tm)

    grads = dict(
        norm_in=g_nin.reshape(-1), w_in=_unpad_g_in(g_win_pad), q_norm=g_qn.reshape(-1),
        w_uq=g_wuq_pad.reshape(Q_RANK, HEADS, HEAD_PAD)[:, :, :NOPE + ROPE], kv_norm=g_kvn.reshape(-1),
        w_ukv=g_wukv.reshape(KV_RANK, HEADS, NOPE + VDIM), pool_w=g_pool_w, pool_scale=g_pool_scale.reshape(-1),
        w_branch_attn=g_wba, w_branch_pool=g_wbp, w_out=g_wout, norm_final=g_nf.reshape(-1))
    return loss8[0, 0], grad_x, grads


MESH_ID = pl.DeviceIdType.MESH
VMEM_SPEC = pl.BlockSpec(memory_space=pltpu.VMEM)
HBM_SPEC = pl.BlockSpec(memory_space=pl.ANY)


def _mesh_pos():
    return lax.axis_index("x"), lax.axis_index("y"), lax.axis_index("c")


def _slot(px, py, pc):
    return 4 * px + 2 * py + pc


def _all_gather_bf16(shards):
    n = len(shards)

    def body(*refs):
        ins, outs, stage = refs[:n], refs[n:2 * n], refs[2 * n:3 * n]
        send_sems, recv_sems, local_sems = refs[3 * n:]
        x, y, c = _mesh_pos()
        me, sibling = (x, y, c), (x, y, 1 - c)
        chips = [(1 - x, y), (x, 1 - y), (1 - x, 1 - y)]

        def copy(a, k, block, to, from_stage=False):
            dst = outs[a].at[_slot(*block)]
            return pltpu.make_async_remote_copy(
                src_ref=stage[a] if from_stage else dst, dst_ref=dst, send_sem=send_sems.at[7 * a + k],
                recv_sem=recv_sems.at[7 * a + k], device_id=to, device_id_type=MESH_ID)

        for a in range(n):
            stage[a][...] = ins[a][...].astype(BF16)
        mine = [pltpu.make_async_copy(stage[a], outs[a].at[_slot(*me)], local_sems.at[a]) for a in range(n)]
        for cp in mine:
            cp.start()
        first = []
        for a in range(n):
            first.append(copy(a, 0, me, sibling, True))
            first += [copy(a, 1 + j, me, (*chip, c), True) for j, chip in enumerate(chips)]
        for cp in first:
            cp.start()
        passed = []
        for a in range(n):
            for j, chip in enumerate(chips):
                copy(a, 1 + j, (*chip, c), me).wait_recv()
                passed.append(copy(a, 4 + j, (*chip, c), sibling))
                passed[-1].start()
        for a in range(n):
            copy(a, 0, sibling, me).wait_recv()
            for j, chip in enumerate(chips):
                copy(a, 4 + j, (*chip, 1 - c), me).wait_recv()
        for cp in first + passed:
            cp.wait_send()
        for cp in mine:
            cp.wait()

    return pl.pallas_call(
        body, name="all_gather_weights",
        in_specs=[VMEM_SPEC] * n, out_specs=[HBM_SPEC] * n,
        out_shape=[SDS((N_DEV,) + s.shape, BF16) for s in shards],
        scratch_shapes=[pltpu.VMEM(s.shape, BF16) for s in shards]
        + [pltpu.SemaphoreType.DMA((7 * n,)), pltpu.SemaphoreType.DMA((7 * n,)), pltpu.SemaphoreType.DMA((n,))],
        compiler_params=_cp(),
    )(*shards)


def _tree_sum(land, out_ref):
    rows = land.shape[1]
    step = 128 if rows % 128 == 0 and rows > 128 else rows

    def chunk(i, carry):
        rs = pl.ds(pl.multiple_of(i * step, step), step)
        v = [land[k, rs, :].astype(F32) for k in range(N_DEV)]
        out_ref[rs, :] = ((v[0] + v[1]) + (v[2] + v[3])) + ((v[4] + v[5]) + (v[6] + v[7]))
        return carry

    lax.fori_loop(0, rows // step, chunk, 0)


def _reduce_scatter(slabs, packed):
    n = len(slabs)

    def body(*refs):
        slab_refs, packed_ref = refs[:n], refs[n]
        out_refs, ptot_ref = refs[n + 1:2 * n + 1], refs[2 * n + 1]
        land, pland = refs[2 * n + 2:3 * n + 2], refs[3 * n + 2]
        send_sems, recv_sems, local_sems = refs[3 * n + 3:]
        x, y, c = _mesh_pos()

        def peer(r):
            return (1 - x if r & 4 else x, 1 - y if r & 2 else y, 1 - c if r & 1 else c)

        copies = []
        for a in range(n):
            copies.append(pltpu.make_async_copy(slab_refs[a].at[_slot(x, y, c)], land[a].at[0], local_sems.at[a]))
            for r in range(1, N_DEV):
                copies.append(pltpu.make_async_remote_copy(
                    src_ref=slab_refs[a].at[_slot(*peer(r))], dst_ref=land[a].at[r], send_sem=send_sems.at[7 * a + r - 1],
                    recv_sem=recv_sems.at[7 * a + r - 1], device_id=peer(r), device_id_type=MESH_ID))
        pland[0] = packed_ref[...]
        for r in range(1, N_DEV):
            copies.append(pltpu.make_async_remote_copy(
                src_ref=packed_ref, dst_ref=pland.at[r], send_sem=send_sems.at[7 * n + r - 1],
                recv_sem=recv_sems.at[7 * n + r - 1], device_id=peer(r), device_id_type=MESH_ID))
        for cp in copies:
            cp.start()
        for cp in copies:
            cp.wait()
        for a in range(n):
            _tree_sum(land[a], out_refs[a])
        _tree_sum(pland, ptot_ref)

    shapes = [s.shape[1:] for s in slabs]
    return pl.pallas_call(
        body, name="reduce_scatter_grads",
        in_specs=[HBM_SPEC] * n + [VMEM_SPEC], out_specs=[VMEM_SPEC] * (n + 1),
        out_shape=[SDS(s, F32) for s in shapes] + [SDS(packed.shape, F32)],
        scratch_shapes=[pltpu.VMEM((N_DEV,) + s, BF16) for s in shapes] + [pltpu.VMEM((N_DEV,) + packed.shape, F32)]
        + [pltpu.SemaphoreType.DMA((7 * (n + 1),)), pltpu.SemaphoreType.DMA((7 * (n + 1),)), pltpu.SemaphoreType.DMA((n,))],
        compiler_params=_cp(),
    )(*slabs, packed)


def _adamw(ws, gs, ms, vs):
    n = len(ws)

    def body(*refs):
        for k in range(n):
            w, g, m, v = (refs[j * n + k][...] for j in range(4))
            d_ref, nm_ref, nv_ref = (refs[(4 + j) * n + k] for j in range(3))
            m = ADAM_B1 * m + (1.0 - ADAM_B1) * g
            v = ADAM_B2 * v + (1.0 - ADAM_B2) * jnp.square(g)
            m_hat = m / (1.0 - ADAM_B1 ** ADAM_STEP)
            v_hat = v / (1.0 - ADAM_B2 ** ADAM_STEP)
            d_ref[...] = -ADAM_LR * (m_hat / (jnp.sqrt(v_hat) + ADAM_EPS) + ADAM_WD * w)
            nm_ref[...] = m
            nv_ref[...] = v

    outs = pl.pallas_call(
        body, name="adamw", in_specs=[VMEM_SPEC] * (4 * n), out_specs=[VMEM_SPEC] * (3 * n),
        out_shape=[SDS(w.shape, F32) for w in ws] * 3, compiler_params=_cp(),
    )(*ws, *gs, *ms, *vs)
    return outs[:n], outs[n:2 * n], outs[2 * n:]


WEIGHTS = ("norm_in", "w_in", "q_norm", "w_uq", "kv_norm", "w_ukv", "pool_w", "pool_scale", "w_branch_attn", "w_branch_pool",
           "w_out", "norm_final")
SHARDED = ("w_in", "w_branch_attn", "w_branch_pool", "w_out", "w_uq", "w_ukv")
REPLICATED = ("norm_in", "q_norm", "kv_norm", "pool_scale", "norm_final", "pool_w")
SUBLANES = 8


def _cols_to_slabs(g):
    r = g.shape[0]
    return g.reshape(r, N_DEV, -1).transpose(1, 0, 2)


def _slabs_to_cols(s):
    return s.transpose(1, 0, 2).reshape(s.shape[1], -1)


def _pack_rows(a):
    a = a.reshape(-1, LANES)
    return jnp.pad(a, ((0, -a.shape[0] % SUBLANES), (0, 0)))


def kernel(x, norm_in, w_in, q_norm, w_uq, kv_norm, w_ukv, pool_w, pool_scale, w_branch_attn, w_branch_pool, w_out, norm_final, loss_target, m_norm_in, m_w_in, m_q_norm, m_w_uq, m_kv_norm, m_w_ukv, m_pool_w, m_pool_scale, m_w_branch_attn, m_w_branch_pool, m_w_out, m_norm_final, v_norm_in, v_w_in, v_q_norm, v_w_uq, v_kv_norm, v_w_ukv, v_pool_w, v_pool_scale, v_w_branch_attn, v_w_branch_pool, v_w_out, v_norm_final):
    w = dict(norm_in=norm_in, w_in=w_in, q_norm=q_norm, w_uq=w_uq, kv_norm=kv_norm, w_ukv=w_ukv, pool_w=pool_w, pool_scale=pool_scale,
             w_branch_attn=w_branch_attn, w_branch_pool=w_branch_pool, w_out=w_out, norm_final=norm_final)
    m = dict(norm_in=m_norm_in, w_in=m_w_in, q_norm=m_q_norm, w_uq=m_w_uq, kv_norm=m_kv_norm, w_ukv=m_w_ukv, pool_w=m_pool_w,
             pool_scale=m_pool_scale, w_branch_attn=m_w_branch_attn, w_branch_pool=m_w_branch_pool, w_out=m_w_out, norm_final=m_norm_final)
    v = dict(norm_in=v_norm_in, w_in=v_w_in, q_norm=v_q_norm, w_uq=v_w_uq, kv_norm=v_kv_norm, w_ukv=v_w_ukv, pool_w=v_pool_w,
             pool_scale=v_pool_scale, w_branch_attn=v_w_branch_attn, w_branch_pool=v_w_branch_pool, w_out=v_w_out, norm_final=v_norm_final)

    def as2d(name, a):
        if name in ("w_uq", "w_ukv"):
            return a.reshape(a.shape[0], -1)
        if name == "pool_w":
            return a.reshape(-1, GROUP)
        return a.reshape(1, -1) if a.ndim == 1 else a

    full = dict(zip(SHARDED, _all_gather_bf16([as2d(k, w[k]) for k in SHARDED])))
    loss_part, grad_x, grads = _local_step(
        x[0], loss_target[0], norm_in, _slabs_to_cols(full["w_in"]), q_norm,
        full["w_uq"].reshape(Q_RANK, HEADS, NOPE + ROPE), kv_norm, full["w_ukv"].reshape(KV_RANK, HEADS, NOPE + VDIM),
        pool_w, pool_scale, _slabs_to_cols(full["w_branch_attn"]), _slabs_to_cols(full["w_branch_pool"]),
        full["w_out"].reshape(D_MODEL, D_MODEL), norm_final)

    slabs = [
        _cols_to_slabs(grads["w_in"]), _cols_to_slabs(grads["w_branch_attn"]), _cols_to_slabs(grads["w_branch_pool"]),
        grads["w_out"].reshape(N_DEV, D_MODEL // N_DEV, D_MODEL), grads["w_uq"].reshape(N_DEV, Q_RANK // N_DEV, -1),
        grads["w_ukv"].reshape(N_DEV, KV_RANK // N_DEV, -1)]
    parts = [_pack_rows(grads[k]) for k in REPLICATED]
    parts.append(_pack_rows(jnp.zeros((LANES,), F32).at[0].set(loss_part)))
    *sums, ptot = _reduce_scatter([s.astype(BF16) for s in slabs], jnp.concatenate(parts, axis=0))
    g2d = dict(zip(SHARDED, sums))
    row = 0
    for k, part in zip(REPLICATED, parts):
        g2d[k] = as2d(k, ptot[row:row + w[k].size // LANES].reshape(w[k].shape))
        row += part.shape[0]
    loss = ptot[row, 0]

    deltas, new_m, new_v = _adamw([as2d(k, w[k]) for k in WEIGHTS], [g2d[k] for k in WEIGHTS],
                                  [as2d(k, m[k]) for k in WEIGHTS], [as2d(k, v[k]) for k in WEIGHTS])
    shaped = lambda arrs: [a.reshape(w[k].shape) for k, a in zip(WEIGHTS, arrs)]
    return (loss, grad_x[None], *shaped([g2d[k] for k in WEIGHTS]), *shaped(deltas), *shaped(new_m), *shaped(new_v))
```

```python
import functools

import jax
import jax.numpy as jnp
import numpy as np
from jax import lax
from jax.experimental import pallas as pl
from jax.experimental.pallas import tpu as pltpu

F32 = jnp.float32
BF16 = jnp.bfloat16
SDS = jax.ShapeDtypeStruct

D_MODEL = 1024
HEADS = 8
NOPE = 64
ROPE = 32
VDIM = 64
Q_RANK = 384
KV_RANK = 256
MLA_W = HEADS * VDIM
POOL_W = 512
POOL_GROUPS = 4
GROUP = POOL_W // POOL_GROUPS
CHUNK = 64
ROPE_THETA = 10000.0
EPS = 1e-6
SCALE = (NOPE + ROPE) ** -0.5
LOG2E = 1.4426950408889634
LN2 = 0.6931471805599453
QK_SCALE_LOG2 = SCALE * LOG2E
IN_TOTAL = 4256
ADAM_LR, ADAM_B1, ADAM_B2, ADAM_EPS, ADAM_WD, ADAM_STEP = 0.001, 0.9, 0.999, 1e-08, 0.01, 10

N_DEV = 8
LANES = 128
HEAD_PAD = LANES
HW = HEADS * HEAD_PAD

ZQ, ZKV, ZKR, GA, UP, GP, GM, ZTOT = 0, 384, 640, 768, 1280, 1792, 2304, 4352
FRONT_W = GA
ZKR_ORIG = 640

VMEM_LIMIT = 56 * 1024 * 1024


def _cp(sem=None, **kw):
    if sem is not None:
        kw["dimension_semantics"] = sem
    return pltpu.CompilerParams(vmem_limit_bytes=VMEM_LIMIT, **kw)


def _mm(a, b):
    return lax.dot_general(a, b, (((1,), (0,)), ((), ())), preferred_element_type=F32)


def _mm_nt(a, b):
    return lax.dot_general(a, b, (((1,), (1,)), ((), ())), preferred_element_type=F32)


def _mm_tn(a, b):
    return lax.dot_general(a, b, (((0,), (0,)), ((), ())), preferred_element_type=F32)


def _row_spec(tm, w):
    return pl.BlockSpec((tm, w), lambda i: (i, 0))


def _full_spec(a):
    nd = len(a.shape)
    return pl.BlockSpec(a.shape, lambda *_: (0,) * nd)


def _rope(v, c, sa, sb, sign):
    n = v.shape[-1]
    reps = n // LANES
    if reps > 1:
        c, sa, sb = (jnp.tile(t, (1, reps)) for t in (c, sa, sb))
    up = pltpu.roll(v, n - ROPE // 2, 1)
    dn = pltpu.roll(v, ROPE // 2, 1)
    return v * c + sign * (up * sa + dn * sb)


def _rope_tables(T):
    half = ROPE // 2
    inv_freq = ROPE_THETA ** (-jnp.arange(half, dtype=F32) / half)
    ang = jnp.arange(T, dtype=F32)[:, None] * inv_freq[None, :]
    cos, sin = jnp.cos(ang), jnp.sin(ang)
    z16 = jnp.zeros((T, half), F32)
    z32 = jnp.zeros((T, LANES - NOPE - ROPE), F32)
    c = jnp.concatenate([jnp.ones((T, NOPE), F32), cos, cos, z32], axis=1)
    sa = jnp.concatenate([jnp.zeros((T, NOPE), F32), -sin, z16, z32], axis=1)
    sb = jnp.concatenate([jnp.zeros((T, NOPE), F32), z16, sin, z32], axis=1)
    return c, sa, sb


def _silu_parts(g):
    sg = jax.nn.sigmoid(g)
    return g * sg, sg + g * sg * (1.0 - sg)


def _in_proj(x2, norm_in, w_in_pad, q_norm, wuq_pad, kv_norm, wukv, rc, rsa, rsb, tm):
    T = x2.shape[0]

    def body(x_ref, nin_ref, win_ref, qn_ref, wuq_ref, kvn_ref, wukv_ref, c_ref, sa_ref, sb_ref,
             hn_ref, zgm_ref, zga_ref, zup_ref, zgp_ref, zfr_ref, q_ref, k_ref, v_ref):
        xf = x_ref[...]
        r = lax.rsqrt(jnp.mean(xf * xf, axis=-1, keepdims=True) + EPS)
        hn = (xf * r * nin_ref[...]).astype(BF16)
        hn_ref[...] = hn
        z = _mm_nt(hn, win_ref[...])
        zgm_ref[...] = z[:, GM:ZTOT]
        zga_ref[...] = z[:, GA:UP]
        zup_ref[...] = z[:, UP:GP]
        zgp_ref[...] = z[:, GP:GM]
        zfr_ref[...] = z[:, ZQ:GA]
        zq, zkv, zkr = z[:, ZQ:ZKV], z[:, ZKV:ZKR], z[:, ZKR:GA]
        c, sa, sb = c_ref[...], sa_ref[...], sb_ref[...]
        rq = lax.rsqrt(jnp.mean(zq * zq, axis=-1, keepdims=True) + EPS)
        cq = (zq * rq * qn_ref[...]).astype(BF16)
        q = _rope(_mm(cq, wuq_ref[...]), c, sa, sb, 1.0)
        q_ref[...] = (q * QK_SCALE_LOG2).astype(BF16)
        rkv = lax.rsqrt(jnp.mean(zkv * zkv, axis=-1, keepdims=True) + EPS)
        ckv = (zkv * rkv * kvn_ref[...]).astype(BF16)
        kv = _mm(ckv, wukv_ref[...])
        kr = _rope(zkr, c, sa, sb, 1.0)
        lane = lax.broadcasted_iota(jnp.int32, kv.shape, 1) % LANES
        k_ref[...] = jnp.where(lane < NOPE, kv, jnp.tile(kr, (1, HEADS))).astype(BF16)
        v_ref[...] = jnp.where(lane < NOPE, 1.0, kv).astype(BF16)

    ins = (x2, norm_in, w_in_pad, q_norm, wuq_pad, kv_norm, wukv, rc, rsa, rsb)
    in_specs = [_row_spec(tm, D_MODEL), _full_spec(norm_in), _full_spec(w_in_pad), _full_spec(q_norm), _full_spec(wuq_pad),
                _full_spec(kv_norm), _full_spec(wukv), _row_spec(tm, LANES), _row_spec(tm, LANES), _row_spec(tm, LANES)]
    widths = [(D_MODEL, BF16), (ZTOT - GM, F32), (UP - GA, F32), (GP - UP, F32), (GM - GP, F32), (FRONT_W, F32),
              (HW, BF16), (HW, BF16), (HW, BF16)]
    return pl.pallas_call(
        body, name="in_proj", grid=(T // tm,), in_specs=in_specs,
        out_specs=[_row_spec(tm, w) for w, _ in widths],
        out_shape=[SDS((T, w), dt) for w, dt in widths],
        compiler_params=_cp(("parallel",)),
    )(*ins)


def _diag_mask(tq, transposed):
    r = lax.broadcasted_iota(jnp.int32, (tq, tq), 0) // CHUNK
    c = lax.broadcasted_iota(jnp.int32, (tq, tq), 1) // CHUNK
    return (r <= c) if transposed else (c <= r)


_HEAD_LANES = (slice(0, LANES), slice(LANES, 2 * LANES))


def _pair_rows_spec(tq):
    return pl.BlockSpec((1, 2, tq), lambda p, i: (p, 0, i))


def _store_pair_rows(ref, pair):
    t = pair.T
    ref[0, 0:1, :] = t[0:1, :]
    ref[0, 1:2, :] = t[VDIM:VDIM + 1, :]


def _attn_fwd(q_att, k_att, v_att, tq):
    T = q_att.shape[0]

    def body(q_ref, k_ref, v_ref, o_ref, lse_ref, lser_ref):
        i = pl.program_id(1)
        mask = _diag_mask(tq, False)
        lane = lax.broadcasted_iota(jnp.int32, (tq, LANES), 1)
        qs = [q_ref[:, hs] for hs in _HEAD_LANES]

        def step(j, carry, masked):
            rows = pl.ds(pl.multiple_of(j * tq, tq), tq)
            out = []
            for (m, acc), qh, hs in zip(carry, qs, _HEAD_LANES):
                s = _mm_nt(qh, k_ref[rows, hs])
                if masked:
                    s = jnp.where(mask, s, -jnp.inf)
                m_new = jnp.maximum(m, jnp.max(s, axis=-1, keepdims=True))
                p = jnp.exp2(s - m_new).astype(BF16)
                out.append((m_new, jnp.exp2(m - m_new) * acc + _mm(p, v_ref[rows, hs])))
            return tuple(out)

        init = ((jnp.full((tq, 1), -jnp.inf, F32), jnp.zeros((tq, LANES), F32)),) * 2
        carry = lax.fori_loop(0, i, functools.partial(step, masked=False), init)
        (ma, acca), (mb, accb) = step(i, carry, True)
        la, lb = acca[:, :1], accb[:, :1]
        o_ref[...] = jnp.where(lane < VDIM, pltpu.roll(acca / la, VDIM, 1), accb / lb)
        lse = jnp.where(lane < VDIM, ma + jnp.log2(la), mb + jnp.log2(lb))
        lse_ref[...] = lse
        _store_pair_rows(lser_ref, lse)

    qspec = pl.BlockSpec((tq, 2 * LANES), lambda p, i: (i, p))
    kspec = pl.BlockSpec((T, 2 * LANES), lambda p, i: (0, p))
    ospec = pl.BlockSpec((tq, LANES), lambda p, i: (i, p))
    return pl.pallas_call(
        body, name="attn_fwd", grid=(HEADS // 2, T // tq), in_specs=[qspec, kspec, kspec],
        out_specs=[ospec, ospec, _pair_rows_spec(tq)],
        out_shape=[SDS((T, MLA_W), F32), SDS((T, MLA_W), F32), SDS((HEADS // 2, 2, T), F32)],
        compiler_params=_cp(("parallel", "parallel")),
    )(q_att, k_att, v_att)


def _pick(g, vals):
    out = vals[-1]
    for k in range(len(vals) - 2, -1, -1):
        out = jnp.where(g == k, vals[k], out)
    return out


def _window_sum(u, g, forward):
    T = u.shape[0]
    row = lax.broadcasted_iota(jnp.int32, u.shape, 0)

    def sh(s, k):
        if forward:
            return jnp.where(row >= k, pltpu.roll(s, k, 0), 0.0)
        return jnp.where(row < T - k, pltpu.roll(s, T - k, 0), 0.0)

    sums, s = [], u
    for k in (1, 2, 4, 8):
        s = s + sh(s, k)
        sums.append(s)
    return _pick(g, sums)


def _pool_count(shape, g):
    row = lax.broadcasted_iota(jnp.int32, shape, 0)
    return jnp.minimum(row + 1, lax.shift_left(jnp.int32(2), g)).astype(F32)


def _pool_fwd(zup, zgp, pool_w, pool_scale):
    T = zup.shape[0]

    def body(u_ref, g_ref, w_ref, sc_ref, y_ref):
        g = pl.program_id(0)
        u = u_ref[...]
        d = _window_sum(u, g, True) / _pool_count(u.shape, g) - u
        lin = _mm(d.astype(BF16), w_ref[0].astype(BF16))
        silu, _ = _silu_parts(g_ref[...])
        y_ref[...] = (lin * sc_ref[...] * silu).astype(BF16)

    col = pl.BlockSpec((T, GROUP), lambda g: (0, g))
    return pl.pallas_call(
        body, name="pool_fwd", grid=(POOL_GROUPS,),
        in_specs=[col, col, pl.BlockSpec((1, GROUP, GROUP), lambda g: (g, 0, 0)), pl.BlockSpec((1, GROUP), lambda g: (0, g))],
        out_specs=col, out_shape=SDS((T, POOL_W), BF16), compiler_params=_cp(("parallel",)),
    )(zup, zgp, pool_w, pool_scale)


def _pool_bwd(zup, zgp, dyp, pool_w, pool_scale):
    T = zup.shape[0]

    def body(u_ref, g_ref, dy_ref, w_ref, sc_ref, du_ref, dg_ref, gw_ref, gsc_ref):
        g = pl.program_id(0)
        u = u_ref[...]
        cnt = _pool_count(u.shape, g)
        d = (_window_sum(u, g, True) / cnt - u).astype(BF16)
        wb = w_ref[0].astype(BF16)
        lin = _mm(d, wb)
        sc = sc_ref[...]
        silu, dsilu = _silu_parts(g_ref[...])
        dy = dy_ref[...]
        dg_ref[...] = (dy * lin * sc * dsilu).astype(BF16)
        dpre = dy * silu
        gsc_ref[...] = jnp.sum(dpre * lin, axis=0, keepdims=True)
        dlin = (dpre * sc).astype(BF16)
        gw_ref[0] = _mm_tn(d, dlin)
        dd = _mm_nt(dlin, wb)
        du_ref[...] = (_window_sum(dd / cnt, g, False) - dd).astype(BF16)

    col = pl.BlockSpec((T, GROUP), lambda g: (0, g))
    wspec = pl.BlockSpec((1, GROUP, GROUP), lambda g: (g, 0, 0))
    vspec = pl.BlockSpec((1, GROUP), lambda g: (0, g))
    return pl.pallas_call(
        body, name="pool_bwd", grid=(POOL_GROUPS,), in_specs=[col, col, col, wspec, vspec], out_specs=[col, col, wspec, vspec],
        out_shape=[SDS((T, POOL_W), BF16), SDS((T, POOL_W), BF16), SDS((POOL_GROUPS, GROUP, GROUP), F32), SDS((1, POOL_W), F32)],
        compiler_params=_cp(("parallel",)),
    )(zup, zgp, dyp, pool_w, pool_scale)


def _tail(x2, tgt, o, zga, ypool, zgm, wba, wbp, wout, norm_final, tm):
    T = x2.shape[0]

    def body(x_ref, tgt_ref, o_ref, zga_ref, yp_ref, zgm_ref, wba_ref, wbp_ref, wout_ref, nf_ref,
             loss_ref, dh_ref, dgm_ref, dya_ref, dyp_ref, gwout_ref, gwba_ref, gwbp_ref, gnf_ref):
        @pl.when(pl.program_id(0) == 0)
        def _():
            for ref in (loss_ref, gwout_ref, gwba_ref, gwbp_ref, gnf_ref):
                ref[...] = jnp.zeros_like(ref)

        silu, _ = _silu_parts(zga_ref[...])
        ya = (o_ref[...] * silu).astype(BF16)
        yp = yp_ref[...]
        wba_v, wbp_v, wout_v = wba_ref[...], wbp_ref[...], wout_ref[...]
        a = _mm(ya, wba_v)
        p = _mm(yp, wbp_v)
        gate = jax.nn.sigmoid(zgm_ref[...])
        ga, gp = gate[:, :D_MODEL], gate[:, D_MODEL:]
        mg = (ga * a + gp * p).astype(BF16)
        h = x_ref[...] + _mm(mg, wout_v)
        r = lax.rsqrt(jnp.mean(h * h, axis=-1, keepdims=True) + EPS)
        gf = nf_ref[...]
        hr = h * r
        e = hr * gf - tgt_ref[...]
        loss_ref[...] += (0.5 / D_MODEL) * jnp.sum(e * e)
        dy = e * (1.0 / D_MODEL)
        gnf_ref[...] += jnp.sum(dy * hr, axis=0, keepdims=True)
        u = dy * gf
        dh = r * (u - hr * jnp.mean(u * hr, axis=-1, keepdims=True))
        dh_ref[...] = dh
        dhb = dh.astype(BF16)
        dmg = _mm_nt(dhb, wout_v)
        gwout_ref[...] += _mm_tn(mg, dhb)
        dgm_ref[:, :D_MODEL] = (dmg * a * ga * (1.0 - ga)).astype(BF16)
        dgm_ref[:, D_MODEL:] = (dmg * p * gp * (1.0 - gp)).astype(BF16)
        dab = (dmg * ga).astype(BF16)
        dpb = (dmg * gp).astype(BF16)
        dya_ref[...] = _mm_nt(dab, wba_v)
        gwba_ref[...] += _mm_tn(ya, dab)
        dyp_ref[...] = _mm_nt(dpb, wbp_v)
        gwbp_ref[...] += _mm_tn(yp, dpb)

    ins = (x2, tgt, o, zga, ypool, zgm, wba, wbp, wout, norm_final)
    in_specs = [_row_spec(tm, D_MODEL), _row_spec(tm, D_MODEL), _row_spec(tm, MLA_W), _row_spec(tm, MLA_W), _row_spec(tm, POOL_W),
                _row_spec(tm, 2 * D_MODEL), _full_spec(wba), _full_spec(wbp), _full_spec(wout), _full_spec(norm_final)]
    outs = [SDS((8, LANES), F32), SDS((T, D_MODEL), F32), SDS((T, 2 * D_MODEL), BF16), SDS((T, MLA_W), F32), SDS((T, POOL_W), F32),
            SDS((D_MODEL, D_MODEL), F32), SDS((MLA_W, D_MODEL), F32), SDS((POOL_W, D_MODEL), F32), SDS((1, D_MODEL), F32)]
    out_specs = [_full_spec(outs[0]), _row_spec(tm, D_MODEL), _row_spec(tm, 2 * D_MODEL), _row_spec(tm, MLA_W), _row_spec(tm, POOL_W),
                 _full_spec(outs[5]), _full_spec(outs[6]), _full_spec(outs[7]), _full_spec(outs[8])]
    return pl.pallas_call(
        body, name="tail", grid=(T // tm,), in_specs=in_specs, out_specs=out_specs, out_shape=outs,
        compiler_params=_cp(("arbitrary",)),
    )(*ins)


def _attn_bwd_dq(q_att, k_att, v_att, o, zga, dya, lse, tq):
    T = q_att.shape[0]

    def body(q_ref, k_ref, v_ref, o_ref, zga_ref, dya_ref, lse_ref, dq_ref, dga_ref, dcapr_ref, doop_ref):
        i = pl.program_id(1)
        mask = _diag_mask(tq, False)
        lane = lax.broadcasted_iota(jnp.int32, (tq, LANES), 1)
        lo = lane < VDIM
        silu, dsilu = _silu_parts(zga_ref[...])
        dya_v, o_v = dya_ref[...], o_ref[...]
        do = dya_v * silu
        dga_ref[...] = (dya_v * o_v * dsilu).astype(BF16)
        prod = do * o_v
        dcap_a = jnp.sum(jnp.where(lo, prod, 0.0), axis=-1, keepdims=True)
        dcap_b = jnp.sum(jnp.where(lo, 0.0, prod), axis=-1, keepdims=True)
        _store_pair_rows(dcapr_ref, jnp.where(lo, dcap_a, dcap_b))
        doops = (pltpu.roll(jnp.where(lo, do, 0.0), VDIM, 1).astype(BF16), jnp.where(lo, 0.0, do).astype(BF16))
        lse_v = lse_ref[...]
        lses = (lse_v[:, :1], lse_v[:, VDIM:VDIM + 1])
        dcaps = (dcap_a, dcap_b)
        qs = [q_ref[:, hs] for hs in _HEAD_LANES]
        for hs, doop in zip(_HEAD_LANES, doops):
            doop_ref[:, hs] = doop

        def step(j, dqs, masked):
            rows = pl.ds(pl.multiple_of(j * tq, tq), tq)
            out = []
            for dq, qh, doop, lse_h, dcap_h, hs in zip(dqs, qs, doops, lses, dcaps, _HEAD_LANES):
                kh = k_ref[rows, hs]
                p = jnp.exp2(_mm_nt(qh, kh) - lse_h)
                if masked:
                    p = jnp.where(mask, p, 0.0)
                dp = _mm_nt(doop, v_ref[rows, hs])
                out.append(dq + _mm((p * (dp - dcap_h)).astype(BF16), kh))
            return tuple(out)

        zero = jnp.zeros((tq, LANES), F32)
        dqs = step(i, lax.fori_loop(0, i, functools.partial(step, masked=False), (zero, zero)), True)
        for dq, hs in zip(dqs, _HEAD_LANES):
            dq_ref[:, hs] = (dq * SCALE).astype(BF16)

    qspec = pl.BlockSpec((tq, 2 * LANES), lambda p, i: (i, p))
    kspec = pl.BlockSpec((T, 2 * LANES), lambda p, i: (0, p))
    ospec = pl.BlockSpec((tq, LANES), lambda p, i: (i, p))
    return pl.pallas_call(
        body, name="attn_bwd_dq", grid=(HEADS // 2, T // tq),
        in_specs=[qspec, kspec, kspec, ospec, ospec, ospec, ospec], out_specs=[qspec, ospec, _pair_rows_spec(tq), qspec],
        out_shape=[SDS((T, HW), BF16), SDS((T, MLA_W), BF16), SDS((HEADS // 2, 2, T), F32), SDS((T, HW), BF16)],
        compiler_params=_cp(("parallel", "parallel")),
    )(q_att, k_att, v_att, o, zga, dya, lse)


def _attn_bwd_dkv(q_att, k_att, v_att, doop, lse_rows, dcap_rows, tq):
    T = q_att.shape[0]
    nq = T // tq

    def body(q_ref, k_ref, v_ref, doop_ref, lse_ref, dcap_ref, dkv_ref, dkr_ref):
        j = pl.program_id(1)
        mask = _diag_mask(tq, True)
        lane = lax.broadcasted_iota(jnp.int32, (tq, LANES), 1)
        ks = [k_ref[:, hs] for hs in _HEAD_LANES]
        vs = [v_ref[:, hs] for hs in _HEAD_LANES]

        def step(i, carry, masked):
            rows = pl.ds(pl.multiple_of(i * tq, tq), tq)
            out = []
            for h, ((dk, dv), kh, vh, hs) in enumerate(zip(carry, ks, vs, _HEAD_LANES)):
                qh = q_ref[rows, hs]
                doop_h = doop_ref[rows, hs]
                pt = jnp.exp2(_mm_nt(kh, qh) - lse_ref[0, h:h + 1, rows])
                if masked:
                    pt = jnp.where(mask, pt, 0.0)
                dv = dv + _mm(pt.astype(BF16), doop_h)
                dpt = _mm_nt(vh, doop_h)
                dst = (pt * (dpt - dcap_ref[0, h:h + 1, rows])).astype(BF16)
                out.append((dk + _mm(dst, qh), dv))
            return tuple(out)

        zero = jnp.zeros((tq, LANES), F32)
        carry = step(j, ((zero, zero), (zero, zero)), True)
        (dka, dva), (dkb, dvb) = lax.fori_loop(j + 1, nq, functools.partial(step, masked=False), carry)
        dka, dkb = dka * LN2, dkb * LN2
        dkv_ref[:, _HEAD_LANES[0]] = jnp.where(lane < NOPE, dka, dva).astype(BF16)
        dkv_ref[:, _HEAD_LANES[1]] = jnp.where(lane < NOPE, dkb, dvb).astype(BF16)
        dkr_ref[0] = jnp.where((lane >= NOPE) & (lane < NOPE + ROPE), dka + dkb, 0.0)

    kspec = pl.BlockSpec((tq, 2 * LANES), lambda p, j: (j, p))
    qspec = pl.BlockSpec((T, 2 * LANES), lambda p, j: (0, p))
    rspec = pl.BlockSpec((1, 2, T), lambda p, j: (p, 0, 0))
    return pl.pallas_call(
        body, name="attn_bwd_dkv", grid=(HEADS // 2, nq),
        in_specs=[qspec, kspec, kspec, qspec, rspec, rspec],
        out_specs=[kspec, pl.BlockSpec((1, tq, LANES), lambda p, j: (p, j, 0))],
        out_shape=[SDS((T, HW), BF16), SDS((HEADS // 2, T, LANES), F32)],
        compiler_params=_cp(("parallel", "parallel")),
    )(q_att, k_att, v_att, doop, lse_rows, dcap_rows)


def _rms_bwd(z, gain, dout):
    r = lax.rsqrt(jnp.mean(z * z, axis=-1, keepdims=True) + EPS)
    zr = z * r
    u = dout * gain
    return r * (u - zr * jnp.mean(u * zr, axis=-1, keepdims=True)), jnp.sum(dout * zr, axis=0, keepdims=True)


def _mla_bwd(dq_att, dkv_nat, dkr4, zfr, q_norm, wuq_pad, kv_norm, wukv, rc, rsa, rsb, tm):
    T = dq_att.shape[0]

    def body(dq_ref, dkv_ref, dkr_ref, zfr_ref, qn_ref, wuq_ref, kvn_ref, wukv_ref, c_ref, sa_ref, sb_ref,
             dfr_ref, gwuq_ref, gwukv_ref, gqn_ref, gkvn_ref):
        @pl.when(pl.program_id(0) == 0)
        def _():
            for ref in (gwuq_ref, gwukv_ref, gqn_ref, gkvn_ref):
                ref[...] = jnp.zeros_like(ref)

        c, sa, sb = c_ref[...], sa_ref[...], sb_ref[...]
        zq, zkv = zfr_ref[:, :Q_RANK], zfr_ref[:, Q_RANK:Q_RANK + KV_RANK]
        qn, kvn = qn_ref[...], kvn_ref[...]
        cq = (zq * lax.rsqrt(jnp.mean(zq * zq, axis=-1, keepdims=True) + EPS) * qn).astype(BF16)
        ckv = (zkv * lax.rsqrt(jnp.mean(zkv * zkv, axis=-1, keepdims=True) + EPS) * kvn).astype(BF16)
        dq = _rope(dq_ref[...].astype(F32), c, sa, sb, -1.0).astype(BF16)
        gwuq_ref[...] += _mm_tn(cq, dq)
        dzq, gqn = _rms_bwd(zq, qn, _mm_nt(dq, wuq_ref[...]))
        gqn_ref[...] += gqn
        dkv = dkv_ref[...]
        gwukv_ref[...] += _mm_tn(ckv, dkv)
        dzkv, gkvn = _rms_bwd(zkv, kvn, _mm_nt(dkv, wukv_ref[...]))
        gkvn_ref[...] += gkvn
        dkr = dkr_ref[0] + dkr_ref[1] + dkr_ref[2] + dkr_ref[3]
        dfr_ref[:, :Q_RANK] = dzq.astype(BF16)
        dfr_ref[:, Q_RANK:Q_RANK + KV_RANK] = dzkv.astype(BF16)
        dfr_ref[:, Q_RANK + KV_RANK:] = _rope(dkr, c, sa, sb, -1.0).astype(BF16)

    ins = (dq_att, dkv_nat, dkr4, zfr, q_norm, wuq_pad, kv_norm, wukv, rc, rsa, rsb)
    in_specs = [_row_spec(tm, HW), _row_spec(tm, HW), pl.BlockSpec((HEADS // 2, tm, LANES), lambda i: (0, i, 0)), _row_spec(tm, FRONT_W),
                _full_spec(q_norm), _full_spec(wuq_pad), _full_spec(kv_norm), _full_spec(wukv),
                _row_spec(tm, LANES), _row_spec(tm, LANES), _row_spec(tm, LANES)]
    outs = [SDS((T, FRONT_W), BF16), SDS((Q_RANK, HW), F32), SDS((KV_RANK, HW), F32), SDS((1, Q_RANK), F32), SDS((1, KV_RANK), F32)]
    out_specs = [_row_spec(tm, FRONT_W)] + [_full_spec(s) for s in outs[1:]]
    return pl.pallas_call(
        body, name="mla_bwd", grid=(T // tm,), in_specs=in_specs, out_specs=out_specs, out_shape=outs,
        compiler_params=_cp(("arbitrary",)),
    )(*ins)


_DZ_COLS = ((GM, ZTOT), (GA, UP), (UP, GP), (GP, GM), (ZQ, GA))


def _in_proj_bwd_x(dzs, x2, dh, norm_in, w_in_pad, tm):
    T = x2.shape[0]

    def body(d0, d1, d2, d3, d4, x_ref, dh_ref, nin_ref, win_ref, gx_ref, gnin_ref):
        @pl.when(pl.program_id(0) == 0)
        def _():
            gnin_ref[...] = jnp.zeros_like(gnin_ref)

        dhn = None
        for ref, (lo, hi) in zip((d0, d1, d2, d3, d4), _DZ_COLS):
            t = _mm(ref[...], win_ref[lo:hi, :])
            dhn = t if dhn is None else dhn + t
        dx, gnin = _rms_bwd(x_ref[...], nin_ref[...], dhn)
        gnin_ref[...] += gnin
        gx_ref[...] = dx + dh_ref[...]

    in_specs = [_row_spec(tm, hi - lo) for lo, hi in _DZ_COLS] + [_row_spec(tm, D_MODEL), _row_spec(tm, D_MODEL),
                                                                  _full_spec(norm_in), _full_spec(w_in_pad)]
    outs = [SDS((T, D_MODEL), F32), SDS((1, D_MODEL), F32)]
    return pl.pallas_call(
        body, name="in_proj_bwd_x", grid=(T // tm,), in_specs=in_specs, out_specs=[_row_spec(tm, D_MODEL), _full_spec(outs[1])],
        out_shape=outs, compiler_params=_cp(("arbitrary",)),
    )(*dzs, x2, dh, norm_in, w_in_pad)


def _in_proj_bwd_w(dzs, hn, tm):
    T = hn.shape[0]

    def body(d0, d1, d2, d3, d4, hn_ref, gw_ref):
        @pl.when(pl.program_id(0) == 0)
        def _():
            gw_ref[...] = jnp.zeros_like(gw_ref)

        hn_v = hn_ref[...]
        for ref, (lo, hi) in zip((d0, d1, d2, d3, d4), _DZ_COLS):
            gw_ref[lo:hi, :] += _mm_tn(ref[...], hn_v)

    in_specs = [_row_spec(tm, hi - lo) for lo, hi in _DZ_COLS] + [_row_spec(tm, D_MODEL)]
    out = SDS((ZTOT, D_MODEL), F32)
    return pl.pallas_call(
        body, name="in_proj_bwd_w", grid=(T // tm,), in_specs=in_specs, out_specs=_full_spec(out), out_shape=out,
        compiler_params=_cp(("arbitrary",)),
    )(*dzs, hn)


def _pad_w_in_t(w_in_t):
    z = functools.partial(jnp.zeros, dtype=w_in_t.dtype)
    cols = w_in_t.shape[1]
    return jnp.concatenate([w_in_t[:ZKR_ORIG], z((NOPE, cols)), w_in_t[ZKR_ORIG:ZKR_ORIG + ROPE], z((LANES - NOPE - ROPE, cols)),
                            w_in_t[ZKR_ORIG + ROPE:]], axis=0)


def _unpad_g_in_t(g):
    return jnp.concatenate([g[:ZKR], g[ZKR + NOPE:ZKR + NOPE + ROPE], g[GA:]], axis=0)


def _local_step(x2, tgt, norm_in, w_in_t, q_norm, w_uq, kv_norm, w_ukv, pool_w, pool_scale, w_ba, w_bp, w_out, norm_final):
    T = x2.shape[0]
    tm = min(256, T)
    tq = min(512, T)
    row = lambda v: v.reshape(1, -1)
    w_in_pad = _pad_w_in_t(w_in_t)
    wuq_pad = jnp.pad(w_uq, ((0, 0), (0, 0), (0, HEAD_PAD - NOPE - ROPE))).reshape(Q_RANK, HW)
    wukv = w_ukv.reshape(KV_RANK, HW)
    rc, rsa, rsb = _rope_tables(T)

    hn, zgm, zga, zup, zgp, zfr, q_att, k_att, v_att = _in_proj(
        x2, row(norm_in), w_in_pad, row(q_norm), wuq_pad, row(kv_norm), wukv, rc, rsa, rsb, tm)
    o, lse, lse_rows = _attn_fwd(q_att, k_att, v_att, tq)
    ypool = _pool_fwd(zup, zgp, pool_w, row(pool_scale))
    loss8, dh, dgm, dya, dyp, g_wout, g_wba, g_wbp, g_nf = _tail(
        x2, tgt, o, zga, ypool, zgm, w_ba, w_bp, w_out, row(norm_final), tm)
    dup, dgp, g_pool_w, g_pool_scale = _pool_bwd(zup, zgp, dyp, pool_w, row(pool_scale))
    dq_att, dga, dcap_rows, doop = _attn_bwd_dq(q_att, k_att, v_att, o, zga, dya, lse, tq)
    dkv_nat, dkr4 = _attn_bwd_dkv(q_att, k_att, v_att, doop, lse_rows, dcap_rows, tq)
    dfr, g_wuq_pad, g_wukv, g_qn, g_kvn = _mla_bwd(
        dq_att, dkv_nat, dkr4, zfr, row(q_norm), wuq_pad, row(kv_norm), wukv, rc, rsa, rsb, tm)
    dzs = (dgm, dga, dup, dgp, dfr)
    grad_x, g_nin = _in_proj_bwd_x(dzs, x2, dh, row(norm_in), w_in_pad, tm)
    g_win_pad = _in_proj_bwd_w(dzs, hn, tm)

    grads = dict(
        norm_in=g_nin.reshape(-1), w_in=_unpad_g_in_t(g_win_pad), q_norm=g_qn.reshape(-1),
        w_uq=g_wuq_pad.reshape(Q_RANK, HEADS, HEAD_PAD)[:, :, :NOPE + ROPE], kv_norm=g_kvn.reshape(-1),
        w_ukv=g_wukv.reshape(KV_RANK, HEADS, NOPE + VDIM), pool_w=g_pool_w, pool_scale=g_pool_scale.reshape(-1),
        w_branch_attn=g_wba, w_branch_pool=g_wbp, w_out=g_wout, norm_final=g_nf.reshape(-1))
    return loss8[0, 0], grad_x, grads


MESH_ID = pl.DeviceIdType.MESH
VMEM_SPEC = pl.BlockSpec(memory_space=pltpu.VMEM)
HBM_SPEC = pl.BlockSpec(memory_space=pl.ANY)


def _mesh_pos():
    return lax.axis_index("x"), lax.axis_index("y"), lax.axis_index("c")


def _slot(px, py, pc):
    return 4 * px + 2 * py + pc


def _all_gather_bf16(shards):
    n = len(shards)

    def body(*refs):
        ins, outs, stage = refs[:n], refs[n:2 * n], refs[2 * n:3 * n]
        send_sems, recv_sems, local_sems = refs[3 * n:]
        x, y, c = _mesh_pos()
        me, sibling = (x, y, c), (x, y, 1 - c)
        chips = [(1 - x, y), (x, 1 - y), (1 - x, 1 - y)]

        def copy(a, k, block, to, from_stage=False):
            dst = outs[a].at[_slot(*block)]
            return pltpu.make_async_remote_copy(
                src_ref=stage[a] if from_stage else dst, dst_ref=dst, send_sem=send_sems.at[7 * a + k],
                recv_sem=recv_sems.at[7 * a + k], device_id=to, device_id_type=MESH_ID)

        for a in range(n):
            stage[a][...] = ins[a][...].astype(BF16)
        mine = [pltpu.make_async_copy(stage[a], outs[a].at[_slot(*me)], local_sems.at[a]) for a in range(n)]
        for cp in mine:
            cp.start()
        first = []
        for a in range(n):
            first.append(copy(a, 0, me, sibling, True))
            first += [copy(a, 1 + j, me, (*chip, c), True) for j, chip in enumerate(chips)]
        for cp in first:
            cp.start()
        passed = []
        for a in range(n):
            for j, chip in enumerate(chips):
                copy(a, 1 + j, (*chip, c), me).wait_recv()
                passed.append(copy(a, 4 + j, (*chip, c), sibling))
                passed[-1].start()
        for a in range(n):
            copy(a, 0, sibling, me).wait_recv()
            for j, chip in enumerate(chips):
                copy(a, 4 + j, (*chip, 1 - c), me).wait_recv()
        for cp in first + passed:
            cp.wait_send()
        for cp in mine:
            cp.wait()

    return pl.pallas_call(
        body, name="all_gather_weights",
        in_specs=[VMEM_SPEC] * n, out_specs=[HBM_SPEC] * n,
        out_shape=[SDS((N_DEV,) + s.shape, BF16) for s in shards],
        scratch_shapes=[pltpu.VMEM(s.shape, BF16) for s in shards]
        + [pltpu.SemaphoreType.DMA((7 * n,)), pltpu.SemaphoreType.DMA((7 * n,)), pltpu.SemaphoreType.DMA((n,))],
        compiler_params=_cp(),
    )(*shards)


def _tree_sum(land, out_ref):
    rows = land.shape[1]
    step = 128 if rows % 128 == 0 and rows > 128 else rows

    def chunk(i, carry):
        rs = pl.ds(pl.multiple_of(i * step, step), step)
        v = [land[k, rs, :].astype(F32) for k in range(N_DEV)]
        out_ref[rs, :] = ((v[0] + v[1]) + (v[2] + v[3])) + ((v[4] + v[5]) + (v[6] + v[7]))
        return carry

    lax.fori_loop(0, rows // step, chunk, 0)


def _reduce_scatter(slabs, packed):
    n = len(slabs)

    def body(*refs):
        slab_refs, packed_ref = refs[:n], refs[n]
        out_refs, ptot_ref = refs[n + 1:2 * n + 1], refs[2 * n + 1]
        land, pland = refs[2 * n + 2:3 * n + 2], refs[3 * n + 2]
        send_sems, recv_sems, local_sems = refs[3 * n + 3:]
        x, y, c = _mesh_pos()

        def peer(r):
            return (1 - x if r & 4 else x, 1 - y if r & 2 else y, 1 - c if r & 1 else c)

        copies = []
        for a in range(n):
            copies.append(pltpu.make_async_copy(slab_refs[a].at[_slot(x, y, c)], land[a].at[0], local_sems.at[a]))
            for r in range(1, N_DEV):
                copies.append(pltpu.make_async_remote_copy(
                    src_ref=slab_refs[a].at[_slot(*peer(r))], dst_ref=land[a].at[r], send_sem=send_sems.at[7 * a + r - 1],
                    recv_sem=recv_sems.at[7 * a + r - 1], device_id=peer(r), device_id_type=MESH_ID))
        pland[0] = packed_ref[...]
        for r in range(1, N_DEV):
            copies.append(pltpu.make_async_remote_copy(
                src_ref=packed_ref, dst_ref=pland.at[r], send_sem=send_sems.at[7 * n + r - 1],
                recv_sem=recv_sems.at[7 * n + r - 1], device_id=peer(r), device_id_type=MESH_ID))
        for cp in copies:
            cp.start()
        for cp in copies:
            cp.wait()
        for a in range(n):
            _tree_sum(land[a], out_refs[a])
        _tree_sum(pland, ptot_ref)

    shapes = [s.shape[1:] for s in slabs]
    return pl.pallas_call(
        body, name="reduce_scatter_grads",
        in_specs=[HBM_SPEC] * n + [VMEM_SPEC], out_specs=[VMEM_SPEC] * (n + 1),
        out_shape=[SDS(s, F32) for s in shapes] + [SDS(packed.shape, F32)],
        scratch_shapes=[pltpu.VMEM((N_DEV,) + s, BF16) for s in shapes] + [pltpu.VMEM((N_DEV,) + packed.shape, F32)]
        + [pltpu.SemaphoreType.DMA((7 * (n + 1),)), pltpu.SemaphoreType.DMA((7 * (n + 1),)), pltpu.SemaphoreType.DMA((n,))],
        compiler_params=_cp(),
    )(*slabs, packed)


def _adamw(ws, gs, ms, vs):
    n = len(ws)

    def body(*refs):
        for k in range(n):
            w, g, m, v = (refs[j * n + k][...] for j in range(4))
            d_ref, nm_ref, nv_ref = (refs[(4 + j) * n + k] for j in range(3))
            m = ADAM_B1 * m + (1.0 - ADAM_B1) * g
            v = ADAM_B2 * v + (1.0 - ADAM_B2) * jnp.square(g)
            m_hat = m / (1.0 - ADAM_B1 ** ADAM_STEP)
            v_hat = v / (1.0 - ADAM_B2 ** ADAM_STEP)
            d_ref[...] = -ADAM_LR * (m_hat / (jnp.sqrt(v_hat) + ADAM_EPS) + ADAM_WD * w)
            nm_ref[...] = m
            nv_ref[...] = v

    outs = pl.pallas_call(
        body, name="adamw", in_specs=[VMEM_SPEC] * (4 * n), out_specs=[VMEM_SPEC] * (3 * n),
        out_shape=[SDS(w.shape, F32) for w in ws] * 3, compiler_params=_cp(),
    )(*ws, *gs, *ms, *vs)
    return outs[:n], outs[n:2 * n], outs[2 * n:]


WEIGHTS = ("norm_in", "w_in", "q_norm", "w_uq", "kv_norm", "w_ukv", "pool_w", "pool_scale", "w_branch_attn", "w_branch_pool",
           "w_out", "norm_final")
SHARDED = ("w_in", "w_branch_attn", "w_branch_pool", "w_out", "w_uq", "w_ukv")
REPLICATED = ("norm_in", "q_norm", "kv_norm", "pool_scale", "norm_final", "pool_w")
SUBLANES = 8


def _cols_to_slabs(g):
    r = g.shape[0]
    return g.reshape(r, N_DEV, -1).transpose(1, 0, 2)


def _slabs_to_cols(s):
    return s.transpose(1, 0, 2).reshape(s.shape[1], -1)


def _pack_rows(a):
    a = a.reshape(-1, LANES)
    return jnp.pad(a, ((0, -a.shape[0] % SUBLANES), (0, 0)))


def kernel(x, norm_in, w_in, q_norm, w_uq, kv_norm, w_ukv, pool_w, pool_scale, w_branch_attn, w_branch_pool, w_out, norm_final, loss_target, m_norm_in, m_w_in, m_q_norm, m_w_uq, m_kv_norm, m_w_ukv, m_pool_w, m_pool_scale, m_w_branch_attn, m_w_branch_pool, m_w_out, m_norm_final, v_norm_in, v_w_in, v_q_norm, v_w_uq, v_kv_norm, v_w_ukv, v_pool_w, v_pool_scale, v_w_branch_attn, v_w_branch_pool, v_w_out, v_norm_final):
    w = dict(norm_in=norm_in, w_in=w_in, q_norm=q_norm, w_uq=w_uq, kv_norm=kv_norm, w_ukv=w_ukv, pool_w=pool_w, pool_scale=pool_scale,
             w_branch_attn=w_branch_attn, w_branch_pool=w_branch_pool, w_out=w_out, norm_final=norm_final)
    m = dict(norm_in=m_norm_in, w_in=m_w_in, q_norm=m_q_norm, w_uq=m_w_uq, kv_norm=m_kv_norm, w_ukv=m_w_ukv, pool_w=m_pool_w,
             pool_scale=m_pool_scale, w_branch_attn=m_w_branch_attn, w_branch_pool=m_w_branch_pool, w_out=m_w_out, norm_final=m_norm_final)
    v = dict(norm_in=v_norm_in, w_in=v_w_in, q_norm=v_q_norm, w_uq=v_w_uq, kv_norm=v_kv_norm, w_ukv=v_w_ukv, pool_w=v_pool_w,
             pool_scale=v_pool_scale, w_branch_attn=v_w_branch_attn, w_branch_pool=v_w_branch_pool, w_out=v_w_out, norm_final=v_norm_final)

    def as2d(name, a):
        if name == "w_in":
            return a.T
        if name in ("w_uq", "w_ukv"):
            return a.reshape(a.shape[0], -1)
        if name == "pool_w":
            return a.reshape(-1, GROUP)
        return a.reshape(1, -1) if a.ndim == 1 else a

    def unshape(name, a):
        return a.T if name == "w_in" else a.reshape(w[name].shape)

    full = dict(zip(SHARDED, _all_gather_bf16([as2d(k, w[k]) for k in SHARDED])))
    loss_part, grad_x, grads = _local_step(
        x.reshape(x.shape[1:]), loss_target.reshape(x.shape[1:]), norm_in, full["w_in"].reshape(IN_TOTAL, D_MODEL), q_norm,
        full["w_uq"].reshape(Q_RANK, HEADS, NOPE + ROPE), kv_norm, full["w_ukv"].reshape(KV_RANK, HEADS, NOPE + VDIM),
        pool_w, pool_scale, _slabs_to_cols(full["w_branch_attn"]), _slabs_to_cols(full["w_branch_pool"]),
        full["w_out"].reshape(D_MODEL, D_MODEL), norm_final)

    slabs = [
        grads["w_in"].reshape(N_DEV, IN_TOTAL // N_DEV, D_MODEL), _cols_to_slabs(grads["w_branch_attn"]),
        _cols_to_slabs(grads["w_branch_pool"]),
        grads["w_out"].reshape(N_DEV, D_MODEL // N_DEV, D_MODEL), grads["w_uq"].reshape(N_DEV, Q_RANK // N_DEV, -1),
        grads["w_ukv"].reshape(N_DEV, KV_RANK // N_DEV, -1)]
    parts = [_pack_rows(grads[k]) for k in REPLICATED]
    parts.append(_pack_rows(jnp.zeros((LANES,), F32).at[0].set(loss_part)))
    *sums, ptot = _reduce_scatter([s.astype(BF16) for s in slabs], jnp.concatenate(parts, axis=0))
    g2d = dict(zip(SHARDED, sums))
    row = 0
    for k, part in zip(REPLICATED, parts):
        g2d[k] = as2d(k, ptot[row:row + w[k].size // LANES].reshape(w[k].shape))
        row += part.shape[0]
    loss = ptot[row, 0]

    deltas, new_m, new_v = _adamw([as2d(k, w[k]) for k in WEIGHTS], [g2d[k] for k in WEIGHTS],
                                  [as2d(k, m[k]) for k in WEIGHTS], [as2d(k, v[k]) for k in WEIGHTS])
    shaped = lambda arrs: [unshape(k, a) for k, a in zip(WEIGHTS, arrs)]
    return (loss, grad_x.reshape(x.shape), *shaped([g2d[k] for k in WEIGHTS]), *shaped(deltas), *shaped(new_m), *shaped(new_v))
```

```python
import functools

import jax
import jax.numpy as jnp
import numpy as np
from jax import lax
from jax.experimental import pallas as pl
from jax.experimental.pallas import tpu as pltpu

F32 = jnp.float32
BF16 = jnp.bfloat16
SDS = jax.ShapeDtypeStruct

D_MODEL = 1024
HEADS = 8
NOPE = 64
ROPE = 32
VDIM = 64
Q_RANK = 384
KV_RANK = 256
MLA_W = HEADS * VDIM
POOL_W = 512
POOL_GROUPS = 4
GROUP = POOL_W // POOL_GROUPS
CHUNK = 64
ROPE_THETA = 10000.0
EPS = 1e-6
SCALE = (NOPE + ROPE) ** -0.5
LOG2E = 1.4426950408889634
LN2 = 0.6931471805599453
QK_SCALE_LOG2 = SCALE * LOG2E
IN_TOTAL = 4256
ADAM_LR, ADAM_B1, ADAM_B2, ADAM_EPS, ADAM_WD, ADAM_STEP = 0.001, 0.9, 0.999, 1e-08, 0.01, 10

N_DEV = 8
LANES = 128
HEAD_PAD = LANES
HW = HEADS * HEAD_PAD

ZQ, ZKV, ZKR, GA, UP, GP, GM, ZTOT = 0, 384, 640, 768, 1280, 1792, 2304, 4352
FRONT_W = GA
ZKR_ORIG = 640

VMEM_LIMIT = 56 * 1024 * 1024


def _cp(sem=None, **kw):
    if sem is not None:
        kw["dimension_semantics"] = sem
    return pltpu.CompilerParams(vmem_limit_bytes=VMEM_LIMIT, **kw)


def _mm(a, b):
    return lax.dot_general(a, b, (((1,), (0,)), ((), ())), preferred_element_type=F32)


def _mm_nt(a, b):
    return lax.dot_general(a, b, (((1,), (1,)), ((), ())), preferred_element_type=F32)


def _mm_tn(a, b):
    return lax.dot_general(a, b, (((0,), (0,)), ((), ())), preferred_element_type=F32)


def _row_spec(tm, w):
    return pl.BlockSpec((tm, w), lambda i: (i, 0))


def _full_spec(a):
    nd = len(a.shape)
    return pl.BlockSpec(a.shape, lambda *_: (0,) * nd)


def _rope(v, c, sa, sb, sign):
    n = v.shape[-1]
    reps = n // LANES
    if reps > 1:
        c, sa, sb = (jnp.tile(t, (1, reps)) for t in (c, sa, sb))
    up = pltpu.roll(v, n - ROPE // 2, 1)
    dn = pltpu.roll(v, ROPE // 2, 1)
    return v * c + sign * (up * sa + dn * sb)


def _rope_tables(T):
    half = ROPE // 2
    inv_freq = ROPE_THETA ** (-jnp.arange(half, dtype=F32) / half)
    ang = jnp.arange(T, dtype=F32)[:, None] * inv_freq[None, :]
    cos, sin = jnp.cos(ang), jnp.sin(ang)
    z16 = jnp.zeros((T, half), F32)
    z32 = jnp.zeros((T, LANES - NOPE - ROPE), F32)
    c = jnp.concatenate([jnp.ones((T, NOPE), F32), cos, cos, z32], axis=1)
    sa = jnp.concatenate([jnp.zeros((T, NOPE), F32), -sin, z16, z32], axis=1)
    sb = jnp.concatenate([jnp.zeros((T, NOPE), F32), z16, sin, z32], axis=1)
    return c, sa, sb


def _silu_parts(g):
    sg = jax.nn.sigmoid(g)
    return g * sg, sg + g * sg * (1.0 - sg)


def _in_proj(x2, norm_in, w_in_pad, q_norm, wuq_pad, kv_norm, wukv, rc, rsa, rsb, tm):
    T = x2.shape[0]

    def body(x_ref, nin_ref, win_ref, qn_ref, wuq_ref, kvn_ref, wukv_ref, c_ref, sa_ref, sb_ref,
             hn_ref, zgm_ref, zga_ref, zup_ref, zgp_ref, zfr_ref, q_ref, k_ref, v_ref):
        xf = x_ref[...]
        r = lax.rsqrt(jnp.mean(xf * xf, axis=-1, keepdims=True) + EPS)
        hn = (xf * r * nin_ref[...]).astype(BF16)
        hn_ref[...] = hn
        z = _mm_nt(hn, win_ref[...])
        zgm_ref[...] = z[:, GM:ZTOT]
        zga_ref[...] = z[:, GA:UP]
        zup_ref[...] = z[:, UP:GP]
        zgp_ref[...] = z[:, GP:GM]
        zfr_ref[...] = z[:, ZQ:GA]
        zq, zkv, zkr = z[:, ZQ:ZKV], z[:, ZKV:ZKR], z[:, ZKR:GA]
        c, sa, sb = c_ref[...], sa_ref[...], sb_ref[...]
        rq = lax.rsqrt(jnp.mean(zq * zq, axis=-1, keepdims=True) + EPS)
        cq = (zq * rq * qn_ref[...]).astype(BF16)
        q = _rope(_mm(cq, wuq_ref[...]), c, sa, sb, 1.0)
        q_ref[...] = (q * QK_SCALE_LOG2).astype(BF16)
        rkv = lax.rsqrt(jnp.mean(zkv * zkv, axis=-1, keepdims=True) + EPS)
        ckv = (zkv * rkv * kvn_ref[...]).astype(BF16)
        kv = _mm(ckv, wukv_ref[...])
        kr = _rope(zkr, c, sa, sb, 1.0)
        lane = lax.broadcasted_iota(jnp.int32, kv.shape, 1) % LANES
        k_ref[...] = jnp.where(lane < NOPE, kv, jnp.tile(kr, (1, HEADS))).astype(BF16)
        v_ref[...] = jnp.where(lane < NOPE, 1.0, kv).astype(BF16)

    ins = (x2, norm_in, w_in_pad, q_norm, wuq_pad, kv_norm, wukv, rc, rsa, rsb)
    in_specs = [_row_spec(tm, D_MODEL), _full_spec(norm_in), _full_spec(w_in_pad), _full_spec(q_norm), _full_spec(wuq_pad),
                _full_spec(kv_norm), _full_spec(wukv), _row_spec(tm, LANES), _row_spec(tm, LANES), _row_spec(tm, LANES)]
    widths = [(D_MODEL, BF16), (ZTOT - GM, F32), (UP - GA, F32), (GP - UP, F32), (GM - GP, F32), (FRONT_W, F32),
              (HW, BF16), (HW, BF16), (HW, BF16)]
    return pl.pallas_call(
        body, name="in_proj", grid=(T // tm,), in_specs=in_specs,
        out_specs=[_row_spec(tm, w) for w, _ in widths],
        out_shape=[SDS((T, w), dt) for w, dt in widths],
        compiler_params=_cp(("parallel",)),
    )(*ins)


def _diag_mask(tq, transposed):
    r = lax.broadcasted_iota(jnp.int32, (tq, tq), 0) // CHUNK
    c = lax.broadcasted_iota(jnp.int32, (tq, tq), 1) // CHUNK
    return (r <= c) if transposed else (c <= r)


_HEAD_LANES = (slice(0, LANES), slice(LANES, 2 * LANES))


def _pair_rows_spec(tq):
    return pl.BlockSpec((1, 2, tq), lambda p, i: (p, 0, i))


def _store_pair_rows(ref, pair):
    t = pair.T
    ref[0, 0:1, :] = t[0:1, :]
    ref[0, 1:2, :] = t[VDIM:VDIM + 1, :]


def _attn_fwd(q_att, k_att, v_att, tq):
    T = q_att.shape[0]

    def body(q_ref, k_ref, v_ref, o_ref, lse_ref, lser_ref):
        i = pl.program_id(1)
        mask = _diag_mask(tq, False)
        lane = lax.broadcasted_iota(jnp.int32, (tq, LANES), 1)
        qs = [q_ref[:, hs] for hs in _HEAD_LANES]

        def step(j, carry, masked):
            rows = pl.ds(pl.multiple_of(j * tq, tq), tq)
            out = []
            for (m, acc), qh, hs in zip(carry, qs, _HEAD_LANES):
                s = _mm_nt(qh, k_ref[rows, hs])
                if masked:
                    s = jnp.where(mask, s, -jnp.inf)
                m_new = jnp.maximum(m, jnp.max(s, axis=-1, keepdims=True))
                p = jnp.exp2(s - m_new).astype(BF16)
                out.append((m_new, jnp.exp2(m - m_new) * acc + _mm(p, v_ref[rows, hs])))
            return tuple(out)

        init = ((jnp.full((tq, 1), -jnp.inf, F32), jnp.zeros((tq, LANES), F32)),) * 2
        carry = lax.fori_loop(0, i, functools.partial(step, masked=False), init)
        (ma, acca), (mb, accb) = step(i, carry, True)
        la, lb = acca[:, :1], accb[:, :1]
        o_ref[...] = jnp.where(lane < VDIM, pltpu.roll(acca / la, VDIM, 1), accb / lb)
        lse = jnp.where(lane < VDIM, ma + jnp.log2(la), mb + jnp.log2(lb))
        lse_ref[...] = lse
        _store_pair_rows(lser_ref, lse)

    qspec = pl.BlockSpec((tq, 2 * LANES), lambda p, i: (i, p))
    kspec = pl.BlockSpec((T, 2 * LANES), lambda p, i: (0, p))
    ospec = pl.BlockSpec((tq, LANES), lambda p, i: (i, p))
    return pl.pallas_call(
        body, name="attn_fwd", grid=(HEADS // 2, T // tq), in_specs=[qspec, kspec, kspec],
        out_specs=[ospec, ospec, _pair_rows_spec(tq)],
        out_shape=[SDS((T, MLA_W), F32), SDS((T, MLA_W), F32), SDS((HEADS // 2, 2, T), F32)],
        compiler_params=_cp(("parallel", "parallel")),
    )(q_att, k_att, v_att)


def _pick(g, vals):
    out = vals[-1]
    for k in range(len(vals) - 2, -1, -1):
        out = jnp.where(g == k, vals[k], out)
    return out


def _window_sum(u, g, forward):
    T = u.shape[0]
    row = lax.broadcasted_iota(jnp.int32, u.shape, 0)

    def sh(s, k):
        if forward:
            return jnp.where(row >= k, pltpu.roll(s, k, 0), 0.0)
        return jnp.where(row < T - k, pltpu.roll(s, T - k, 0), 0.0)

    sums, s = [], u
    for k in (1, 2, 4, 8):
        s = s + sh(s, k)
        sums.append(s)
    return _pick(g, sums)


def _pool_count(shape, g):
    row = lax.broadcasted_iota(jnp.int32, shape, 0)
    return jnp.minimum(row + 1, lax.shift_left(jnp.int32(2), g)).astype(F32)


def _pool_fwd(zup, zgp, pool_w, pool_scale):
    T = zup.shape[0]

    def body(u_ref, g_ref, w_ref, sc_ref, y_ref):
        g = pl.program_id(0)
        u = u_ref[...]
        d = _window_sum(u, g, True) / _pool_count(u.shape, g) - u
        lin = _mm(d.astype(BF16), w_ref[0].astype(BF16))
        silu, _ = _silu_parts(g_ref[...])
        y_ref[...] = (lin * sc_ref[...] * silu).astype(BF16)

    col = pl.BlockSpec((T, GROUP), lambda g: (0, g))
    return pl.pallas_call(
        body, name="pool_fwd", grid=(POOL_GROUPS,),
        in_specs=[col, col, pl.BlockSpec((1, GROUP, GROUP), lambda g: (g, 0, 0)), pl.BlockSpec((1, GROUP), lambda g: (0, g))],
        out_specs=col, out_shape=SDS((T, POOL_W), BF16), compiler_params=_cp(("parallel",)),
    )(zup, zgp, pool_w, pool_scale)


def _pool_bwd(zup, zgp, dyp, pool_w, pool_scale):
    T = zup.shape[0]

    def body(u_ref, g_ref, dy_ref, w_ref, sc_ref, du_ref, dg_ref, gw_ref, gsc_ref):
        g = pl.program_id(0)
        u = u_ref[...]
        cnt = _pool_count(u.shape, g)
        d = (_window_sum(u, g, True) / cnt - u).astype(BF16)
        wb = w_ref[0].astype(BF16)
        lin = _mm(d, wb)
        sc = sc_ref[...]
        silu, dsilu = _silu_parts(g_ref[...])
        dy = dy_ref[...]
        dg_ref[...] = (dy * lin * sc * dsilu).astype(BF16)
        dpre = dy * silu
        gsc_ref[...] = jnp.sum(dpre * lin, axis=0, keepdims=True)
        dlin = (dpre * sc).astype(BF16)
        gw_ref[0] = _mm_tn(d, dlin)
        dd = _mm_nt(dlin, wb)
        du_ref[...] = (_window_sum(dd / cnt, g, False) - dd).astype(BF16)

    col = pl.BlockSpec((T, GROUP), lambda g: (0, g))
    wspec = pl.BlockSpec((1, GROUP, GROUP), lambda g: (g, 0, 0))
    vspec = pl.BlockSpec((1, GROUP), lambda g: (0, g))
    return pl.pallas_call(
        body, name="pool_bwd", grid=(POOL_GROUPS,), in_specs=[col, col, col, wspec, vspec], out_specs=[col, col, wspec, vspec],
        out_shape=[SDS((T, POOL_W), BF16), SDS((T, POOL_W), BF16), SDS((POOL_GROUPS, GROUP, GROUP), F32), SDS((1, POOL_W), F32)],
        compiler_params=_cp(("parallel",)),
    )(zup, zgp, dyp, pool_w, pool_scale)


def _tail(x2, tgt, o, zga, ypool, zgm, wba, wbp, wout, norm_final, tm):
    T = x2.shape[0]

    def body(x_ref, tgt_ref, o_ref, zga_ref, yp_ref, zgm_ref, wba_ref, wbp_ref, wout_ref, nf_ref,
             loss_ref, dh_ref, dgm_ref, dya_ref, dyp_ref, gwout_ref, gwba_ref, gwbp_ref, gnf_ref):
        @pl.when(pl.program_id(0) == 0)
        def _():
            for ref in (loss_ref, gwout_ref, gwba_ref, gwbp_ref, gnf_ref):
                ref[...] = jnp.zeros_like(ref)

        silu, _ = _silu_parts(zga_ref[...])
        ya = (o_ref[...] * silu).astype(BF16)
        yp = yp_ref[...]
        wba_v, wbp_v, wout_v = wba_ref[...], wbp_ref[...], wout_ref[...]
        a = _mm(ya, wba_v)
        p = _mm(yp, wbp_v)
        gate = jax.nn.sigmoid(zgm_ref[...])
        ga, gp = gate[:, :D_MODEL], gate[:, D_MODEL:]
        mg = (ga * a + gp * p).astype(BF16)
        h = x_ref[...] + _mm(mg, wout_v)
        r = lax.rsqrt(jnp.mean(h * h, axis=-1, keepdims=True) + EPS)
        gf = nf_ref[...]
        hr = h * r
        e = hr * gf - tgt_ref[...]
        loss_ref[...] += (0.5 / D_MODEL) * jnp.sum(e * e)
        dy = e * (1.0 / D_MODEL)
        gnf_ref[...] += jnp.sum(dy * hr, axis=0, keepdims=True)
        u = dy * gf
        dh = r * (u - hr * jnp.mean(u * hr, axis=-1, keepdims=True))
        dh_ref[...] = dh
        dhb = dh.astype(BF16)
        dmg = _mm_nt(dhb, wout_v)
        gwout_ref[...] += _mm_tn(mg, dhb)
        dgm_ref[:, :D_MODEL] = (dmg * a * ga * (1.0 - ga)).astype(BF16)
        dgm_ref[:, D_MODEL:] = (dmg * p * gp * (1.0 - gp)).astype(BF16)
        dab = (dmg * ga).astype(BF16)
        dpb = (dmg * gp).astype(BF16)
        dya_ref[...] = _mm_nt(dab, wba_v)
        gwba_ref[...] += _mm_tn(ya, dab)
        dyp_ref[...] = _mm_nt(dpb, wbp_v)
        gwbp_ref[...] += _mm_tn(yp, dpb)

    ins = (x2, tgt, o, zga, ypool, zgm, wba, wbp, wout, norm_final)
    in_specs = [_row_spec(tm, D_MODEL), _row_spec(tm, D_MODEL), _row_spec(tm, MLA_W), _row_spec(tm, MLA_W), _row_spec(tm, POOL_W),
                _row_spec(tm, 2 * D_MODEL), _full_spec(wba), _full_spec(wbp), _full_spec(wout), _full_spec(norm_final)]
    outs = [SDS((8, LANES), F32), SDS((T, D_MODEL), F32), SDS((T, 2 * D_MODEL), BF16), SDS((T, MLA_W), F32), SDS((T, POOL_W), F32),
            SDS((D_MODEL, D_MODEL), F32), SDS((MLA_W, D_MODEL), F32), SDS((POOL_W, D_MODEL), F32), SDS((1, D_MODEL), F32)]
    out_specs = [_full_spec(outs[0]), _row_spec(tm, D_MODEL), _row_spec(tm, 2 * D_MODEL), _row_spec(tm, MLA_W), _row_spec(tm, POOL_W),
                 _full_spec(outs[5]), _full_spec(outs[6]), _full_spec(outs[7]), _full_spec(outs[8])]
    return pl.pallas_call(
        body, name="tail", grid=(T // tm,), in_specs=in_specs, out_specs=out_specs, out_shape=outs,
        compiler_params=_cp(("arbitrary",)),
    )(*ins)


def _attn_bwd_dq(q_att, k_att, v_att, o, zga, dya, lse, tq):
    T = q_att.shape[0]

    def body(q_ref, k_ref, v_ref, o_ref, zga_ref, dya_ref, lse_ref, dq_ref, dga_ref, dcapr_ref, doop_ref):
        i = pl.program_id(1)
        mask = _diag_mask(tq, False)
        lane = lax.broadcasted_iota(jnp.int32, (tq, LANES), 1)
        lo = lane < VDIM
        silu, dsilu = _silu_parts(zga_ref[...])
        dya_v, o_v = dya_ref[...], o_ref[...]
        do = dya_v * silu
        dga_ref[...] = (dya_v * o_v * dsilu).astype(BF16)
        prod = do * o_v
        dcap_a = jnp.sum(jnp.where(lo, prod, 0.0), axis=-1, keepdims=True)
        dcap_b = jnp.sum(jnp.where(lo, 0.0, prod), axis=-1, keepdims=True)
        _store_pair_rows(dcapr_ref, jnp.where(lo, dcap_a, dcap_b))
        doops = (pltpu.roll(jnp.where(lo, do, 0.0), VDIM, 1).astype(BF16), jnp.where(lo, 0.0, do).astype(BF16))
        lse_v = lse_ref[...]
        lses = (lse_v[:, :1], lse_v[:, VDIM:VDIM + 1])
        dcaps = (dcap_a, dcap_b)
        qs = [q_ref[:, hs] for hs in _HEAD_LANES]
        for hs, doop in zip(_HEAD_LANES, doops):
            doop_ref[:, hs] = doop

        def step(j, dqs, masked):
            rows = pl.ds(pl.multiple_of(j * tq, tq), tq)
            out = []
            for dq, qh, doop, lse_h, dcap_h, hs in zip(dqs, qs, doops, lses, dcaps, _HEAD_LANES):
                kh = k_ref[rows, hs]
                p = jnp.exp2(_mm_nt(qh, kh) - lse_h)
                if masked:
                    p = jnp.where(mask, p, 0.0)
                dp = _mm_nt(doop, v_ref[rows, hs])
                out.append(dq + _mm((p * (dp - dcap_h)).astype(BF16), kh))
            return tuple(out)

        zero = jnp.zeros((tq, LANES), F32)
        dqs = step(i, lax.fori_loop(0, i, functools.partial(step, masked=False), (zero, zero)), True)
        for dq, hs in zip(dqs, _HEAD_LANES):
            dq_ref[:, hs] = (dq * SCALE).astype(BF16)

    qspec = pl.BlockSpec((tq, 2 * LANES), lambda p, i: (i, p))
    kspec = pl.BlockSpec((T, 2 * LANES), lambda p, i: (0, p))
    ospec = pl.BlockSpec((tq, LANES), lambda p, i: (i, p))
    return pl.pallas_call(
        body, name="attn_bwd_dq", grid=(HEADS // 2, T // tq),
        in_specs=[qspec, kspec, kspec, ospec, ospec, ospec, ospec], out_specs=[qspec, ospec, _pair_rows_spec(tq), qspec],
        out_shape=[SDS((T, HW), BF16), SDS((T, MLA_W), BF16), SDS((HEADS // 2, 2, T), F32), SDS((T, HW), BF16)],
        compiler_params=_cp(("parallel", "parallel")),
    )(q_att, k_att, v_att, o, zga, dya, lse)


def _attn_bwd_dkv(q_att, k_att, v_att, doop, lse_rows, dcap_rows, tq):
    T = q_att.shape[0]
    nq = T // tq

    def body(q_ref, k_ref, v_ref, doop_ref, lse_ref, dcap_ref, dkv_ref, dkr_ref):
        j = pl.program_id(1)
        mask = _diag_mask(tq, True)
        lane = lax.broadcasted_iota(jnp.int32, (tq, LANES), 1)
        ks = [k_ref[:, hs] for hs in _HEAD_LANES]
        vs = [v_ref[:, hs] for hs in _HEAD_LANES]

        def step(i, carry, masked):
            rows = pl.ds(pl.multiple_of(i * tq, tq), tq)
            out = []
            for h, ((dk, dv), kh, vh, hs) in enumerate(zip(carry, ks, vs, _HEAD_LANES)):
                qh = q_ref[rows, hs]
                doop_h = doop_ref[rows, hs]
                pt = jnp.exp2(_mm_nt(kh, qh) - lse_ref[0, h:h + 1, rows])
                if masked:
                    pt = jnp.where(mask, pt, 0.0)
                dv = dv + _mm(pt.astype(BF16), doop_h)
                dpt = _mm_nt(vh, doop_h)
                dst = (pt * (dpt - dcap_ref[0, h:h + 1, rows])).astype(BF16)
                out.append((dk + _mm(dst, qh), dv))
            return tuple(out)

        zero = jnp.zeros((tq, LANES), F32)
        carry = step(j, ((zero, zero), (zero, zero)), True)
        (dka, dva), (dkb, dvb) = lax.fori_loop(j + 1, nq, functools.partial(step, masked=False), carry)
        dka, dkb = dka * LN2, dkb * LN2
        dkv_ref[:, _HEAD_LANES[0]] = jnp.where(lane < NOPE, dka, dva).astype(BF16)
        dkv_ref[:, _HEAD_LANES[1]] = jnp.where(lane < NOPE, dkb, dvb).astype(BF16)
        dkr_ref[0] = jnp.where((lane >= NOPE) & (lane < NOPE + ROPE), dka + dkb, 0.0)

    kspec = pl.BlockSpec((tq, 2 * LANES), lambda p, j: (j, p))
    qspec = pl.BlockSpec((T, 2 * LANES), lambda p, j: (0, p))
    rspec = pl.BlockSpec((1, 2, T), lambda p, j: (p, 0, 0))
    return pl.pallas_call(
        body, name="attn_bwd_dkv", grid=(HEADS // 2, nq),
        in_specs=[qspec, kspec, kspec, qspec, rspec, rspec],
        out_specs=[kspec, pl.BlockSpec((1, tq, LANES), lambda p, j: (p, j, 0))],
        out_shape=[SDS((T, HW), BF16), SDS((HEADS // 2, T, LANES), F32)],
        compiler_params=_cp(("parallel", "parallel")),
    )(q_att, k_att, v_att, doop, lse_rows, dcap_rows)


def _rms_bwd(z, gain, dout):
    r = lax.rsqrt(jnp.mean(z * z, axis=-1, keepdims=True) + EPS)
    zr = z * r
    u = dout * gain
    return r * (u - zr * jnp.mean(u * zr, axis=-1, keepdims=True)), jnp.sum(dout * zr, axis=0, keepdims=True)


def _mla_bwd(dq_att, dkv_nat, dkr4, zfr, q_norm, wuq_pad, kv_norm, wukv, rc, rsa, rsb, tm):
    T = dq_att.shape[0]

    def body(dq_ref, dkv_ref, dkr_ref, zfr_ref, qn_ref, wuq_ref, kvn_ref, wukv_ref, c_ref, sa_ref, sb_ref,
             dfr_ref, gwuq_ref, gwukv_ref, gqn_ref, gkvn_ref):
        @pl.when(pl.program_id(0) == 0)
        def _():
            for ref in (gwuq_ref, gwukv_ref, gqn_ref, gkvn_ref):
                ref[...] = jnp.zeros_like(ref)

        c, sa, sb = c_ref[...], sa_ref[...], sb_ref[...]
        zq, zkv = zfr_ref[:, :Q_RANK], zfr_ref[:, Q_RANK:Q_RANK + KV_RANK]
        qn, kvn = qn_ref[...], kvn_ref[...]
        cq = (zq * lax.rsqrt(jnp.mean(zq * zq, axis=-1, keepdims=True) + EPS) * qn).astype(BF16)
        ckv = (zkv * lax.rsqrt(jnp.mean(zkv * zkv, axis=-1, keepdims=True) + EPS) * kvn).astype(BF16)
        dq = _rope(dq_ref[...].astype(F32), c, sa, sb, -1.0).astype(BF16)
        gwuq_ref[...] += _mm_tn(cq, dq)
        dzq, gqn = _rms_bwd(zq, qn, _mm_nt(dq, wuq_ref[...]))
        gqn_ref[...] += gqn
        dkv = dkv_ref[...]
        gwukv_ref[...] += _mm_tn(ckv, dkv)
        dzkv, gkvn = _rms_bwd(zkv, kvn, _mm_nt(dkv, wukv_ref[...]))
        gkvn_ref[...] += gkvn
        dkr = dkr_ref[0] + dkr_ref[1] + dkr_ref[2] + dkr_ref[3]
        dfr_ref[:, :Q_RANK] = dzq.astype(BF16)
        dfr_ref[:, Q_RANK:Q_RANK + KV_RANK] = dzkv.astype(BF16)
        dfr_ref[:, Q_RANK + KV_RANK:] = _rope(dkr, c, sa, sb, -1.0).astype(BF16)

    ins = (dq_att, dkv_nat, dkr4, zfr, q_norm, wuq_pad, kv_norm, wukv, rc, rsa, rsb)
    in_specs = [_row_spec(tm, HW), _row_spec(tm, HW), pl.BlockSpec((HEADS // 2, tm, LANES), lambda i: (0, i, 0)), _row_spec(tm, FRONT_W),
                _full_spec(q_norm), _full_spec(wuq_pad), _full_spec(kv_norm), _full_spec(wukv),
                _row_spec(tm, LANES), _row_spec(tm, LANES), _row_spec(tm, LANES)]
    outs = [SDS((T, FRONT_W), BF16), SDS((Q_RANK, HW), F32), SDS((KV_RANK, HW), F32), SDS((1, Q_RANK), F32), SDS((1, KV_RANK), F32)]
    out_specs = [_row_spec(tm, FRONT_W)] + [_full_spec(s) for s in outs[1:]]
    return pl.pallas_call(
        body, name="mla_bwd", grid=(T // tm,), in_specs=in_specs, out_specs=out_specs, out_shape=outs,
        compiler_params=_cp(("arbitrary",)),
    )(*ins)


_DZ_COLS = ((GM, ZTOT), (GA, UP), (UP, GP), (GP, GM), (ZQ, GA))


def _in_proj_bwd_x(dzs, x2, dh, norm_in, w_in_pad, tm):
    T = x2.shape[0]

    def body(d0, d1, d2, d3, d4, x_ref, dh_ref, nin_ref, win_ref, gx_ref, gnin_ref):
        @pl.when(pl.program_id(0) == 0)
        def _():
            gnin_ref[...] = jnp.zeros_like(gnin_ref)

        dhn = None
        for ref, (lo, hi) in zip((d0, d1, d2, d3, d4), _DZ_COLS):
            t = _mm(ref[...], win_ref[lo:hi, :])
            dhn = t if dhn is None else dhn + t
        dx, gnin = _rms_bwd(x_ref[...], nin_ref[...], dhn)
        gnin_ref[...] += gnin
        gx_ref[...] = dx + dh_ref[...]

    in_specs = [_row_spec(tm, hi - lo) for lo, hi in _DZ_COLS] + [_row_spec(tm, D_MODEL), _row_spec(tm, D_MODEL),
                                                                  _full_spec(norm_in), _full_spec(w_in_pad)]
    outs = [SDS((T, D_MODEL), F32), SDS((1, D_MODEL), F32)]
    return pl.pallas_call(
        body, name="in_proj_bwd_x", grid=(T // tm,), in_specs=in_specs, out_specs=[_row_spec(tm, D_MODEL), _full_spec(outs[1])],
        out_shape=outs, compiler_params=_cp(("arbitrary",)),
    )(*dzs, x2, dh, norm_in, w_in_pad)


def _in_proj_bwd_w(dzs, hn, tm):
    T = hn.shape[0]

    def body(d0, d1, d2, d3, d4, hn_ref, gw_ref):
        @pl.when(pl.program_id(0) == 0)
        def _():
            gw_ref[...] = jnp.zeros_like(gw_ref)

        hn_v = hn_ref[...]
        for ref, (lo, hi) in zip((d0, d1, d2, d3, d4), _DZ_COLS):
            gw_ref[lo:hi, :] += _mm_tn(ref[...], hn_v)

    in_specs = [_row_spec(tm, hi - lo) for lo, hi in _DZ_COLS] + [_row_spec(tm, D_MODEL)]
    out = SDS((ZTOT, D_MODEL), F32)
    return pl.pallas_call(
        body, name="in_proj_bwd_w", grid=(T // tm,), in_specs=in_specs, out_specs=_full_spec(out), out_shape=out,
        compiler_params=_cp(("arbitrary",)),
    )(*dzs, hn)


def _pad_w_in_t(w_in_t):
    z = functools.partial(jnp.zeros, dtype=w_in_t.dtype)
    cols = w_in_t.shape[1]
    return jnp.concatenate([w_in_t[:ZKR_ORIG], z((NOPE, cols)), w_in_t[ZKR_ORIG:ZKR_ORIG + ROPE], z((LANES - NOPE - ROPE, cols)),
                            w_in_t[ZKR_ORIG + ROPE:]], axis=0)


def _unpad_g_in_t(g):
    return jnp.concatenate([g[:ZKR], g[ZKR + NOPE:ZKR + NOPE + ROPE], g[GA:]], axis=0)


def _local_step(x2, tgt, norm_in, w_in_t, q_norm, w_uq, kv_norm, w_ukv, pool_w, pool_scale, w_ba, w_bp, w_out, norm_final):
    T = x2.shape[0]
    tm = min(256, T)
    tq = min(512, T)
    row = lambda v: v.reshape(1, -1)
    w_in_pad = _pad_w_in_t(w_in_t)
    wuq_pad = jnp.pad(w_uq, ((0, 0), (0, 0), (0, HEAD_PAD - NOPE - ROPE))).reshape(Q_RANK, HW)
    wukv = w_ukv.reshape(KV_RANK, HW)
    rc, rsa, rsb = _rope_tables(T)

    hn, zgm, zga, zup, zgp, zfr, q_att, k_att, v_att = _in_proj(
        x2, row(norm_in), w_in_pad, row(q_norm), wuq_pad, row(kv_norm), wukv, rc, rsa, rsb, tm)
    o, lse, lse_rows = _attn_fwd(q_att, k_att, v_att, tq)
    ypool = _pool_fwd(zup, zgp, pool_w, row(pool_scale))
    loss8, dh, dgm, dya, dyp, g_wout, g_wba, g_wbp, g_nf = _tail(
        x2, tgt, o, zga, ypool, zgm, w_ba, w_bp, w_out, row(norm_final), tm)
    dup, dgp, g_pool_w, g_pool_scale = _pool_bwd(zup, zgp, dyp, pool_w, row(pool_scale))
    dq_att, dga, dcap_rows, doop = _attn_bwd_dq(q_att, k_att, v_att, o, zga, dya, lse, tq)
    dkv_nat, dkr4 = _attn_bwd_dkv(q_att, k_att, v_att, doop, lse_rows, dcap_rows, tq)
    dfr, g_wuq_pad, g_wukv, g_qn, g_kvn = _mla_bwd(
        dq_att, dkv_nat, dkr4, zfr, row(q_norm), wuq_pad, row(kv_norm), wukv, rc, rsa, rsb, tm)
    dzs = (dgm, dga, dup, dgp, dfr)
    grad_x, g_nin = _in_proj_bwd_x(dzs, x2, dh, row(norm_in), w_in_pad, tm)
    g_win_pad = _in_proj_bwd_w(dzs, hn, tm)

    grads = dict(
        norm_in=g_nin.reshape(-1), w_in=_unpad_g_in_t(g_win_pad), q_norm=g_qn.reshape(-1),
        w_uq=g_wuq_pad.reshape(Q_RANK, HEADS, HEAD_PAD)[:, :, :NOPE + ROPE], kv_norm=g_kvn.reshape(-1),
        w_ukv=g_wukv.reshape(KV_RANK, HEADS, NOPE + VDIM), pool_w=g_pool_w, pool_scale=g_pool_scale.reshape(-1),
        w_branch_attn=g_wba, w_branch_pool=g_wbp, w_out=g_wout, norm_final=g_nf.reshape(-1))
    return loss8[0, 0], grad_x, grads


MESH_ID = pl.DeviceIdType.MESH
VMEM_SPEC = pl.BlockSpec(memory_space=pltpu.VMEM)
HBM_SPEC = pl.BlockSpec(memory_space=pl.ANY)


def _mesh_pos():
    return lax.axis_index("x"), lax.axis_index("y"), lax.axis_index("c")


def _slot(px, py, pc):
    return 4 * px + 2 * py + pc


def _all_gather_bf16(shards):
    n = len(shards)

    def body(*refs):
        ins, outs, stage = refs[:n], refs[n:2 * n], refs[2 * n:3 * n]
        send_sems, recv_sems, local_sems = refs[3 * n:]
        x, y, c = _mesh_pos()
        me, sibling = (x, y, c), (x, y, 1 - c)
        chips = [(1 - x, y), (x, 1 - y), (1 - x, 1 - y)]

        def copy(a, k, block, to, from_stage=False):
            dst = outs[a].at[_slot(*block)]
            return pltpu.make_async_remote_copy(
                src_ref=stage[a] if from_stage else dst, dst_ref=dst, send_sem=send_sems.at[7 * a + k],
                recv_sem=recv_sems.at[7 * a + k], device_id=to, device_id_type=MESH_ID)

        for a in range(n):
            stage[a][...] = ins[a][...].astype(BF16)
        mine = [pltpu.make_async_copy(stage[a], outs[a].at[_slot(*me)], local_sems.at[a]) for a in range(n)]
        for cp in mine:
            cp.start()
        first = []
        for a in range(n):
            first.append(copy(a, 0, me, sibling, True))
            first += [copy(a, 1 + j, me, (*chip, c), True) for j, chip in enumerate(chips)]
        for cp in first:
            cp.start()
        passed = []
        for a in range(n):
            for j, chip in enumerate(chips):
                copy(a, 1 + j, (*chip, c), me).wait_recv()
                passed.append(copy(a, 4 + j, (*chip, c), sibling))
                passed[-1].start()
        for a in range(n):
            copy(a, 0, sibling, me).wait_recv()
            for j, chip in enumerate(chips):
                copy(a, 4 + j, (*chip, 1 - c), me).wait_recv()
        for cp in first + passed:
            cp.wait_send()
        for cp in mine:
            cp.wait()

    return pl.pallas_call(
        body, name="all_gather_weights",
        in_specs=[VMEM_SPEC] * n, out_specs=[HBM_SPEC] * n,
        out_shape=[SDS((N_DEV,) + s.shape, BF16) for s in shards],
        scratch_shapes=[pltpu.VMEM(s.shape, BF16) for s in shards]
        + [pltpu.SemaphoreType.DMA((7 * n,)), pltpu.SemaphoreType.DMA((7 * n,)), pltpu.SemaphoreType.DMA((n,))],
        compiler_params=_cp(),
    )(*shards)


N_CHIPS = 4


def _reduce_scatter(slabs, packed):
    n = len(slabs)

    def body(*refs):
        slab_refs, packed_ref = refs[:n], refs[n]
        out_refs, ptot_ref = refs[n + 1:2 * n + 1], refs[2 * n + 1]
        own1, land1, send2, land2 = (refs[(2 + k) * n + 2:(3 + k) * n + 2] for k in range(4))
        pland1, pland2, loc_sems, send1_sems, recv1_sems, send2_sems, recv2_sems = refs[6 * n + 2:]
        x, y, c = _mesh_pos()
        sibling = (x, y, 1 - c)

        def chip(r):
            return (1 - x if r & 2 else x, 1 - y if r & 1 else y)

        def remote(src, dst, send_sem, recv_sem, to):
            return pltpu.make_async_remote_copy(src_ref=src, dst_ref=dst, send_sem=send_sem, recv_sem=recv_sem, device_id=to,
                                                device_id_type=MESH_ID)

        stage1 = []
        for a in range(n):
            for r in range(N_CHIPS):
                k = N_CHIPS * a + r
                stage1.append(pltpu.make_async_copy(slab_refs[a].at[_slot(*chip(r), c)], own1[a].at[r], loc_sems.at[k]))
                stage1.append(remote(slab_refs[a].at[_slot(*chip(r), 1 - c)], land1[a].at[r], send1_sems.at[k], recv1_sems.at[k], sibling))
        stage1.append(remote(packed_ref, pland1, send1_sems.at[N_CHIPS * n], recv1_sems.at[N_CHIPS * n], sibling))
        for cp in stage1:
            cp.start()
        for cp in stage1:
            cp.wait()
        for a in range(n):
            out_refs[a][...] = own1[a][0].astype(F32) + land1[a][0].astype(F32)
            for r in range(1, N_CHIPS):
                send2[a][r - 1] = (own1[a][r].astype(F32) + land1[a][r].astype(F32)).astype(BF16)
        pland2[0] = packed_ref[...] + pland1[...]

        stage2 = []
        for a in range(n):
            for r in range(1, N_CHIPS):
                k = (N_CHIPS - 1) * a + r - 1
                stage2.append(remote(send2[a].at[r - 1], land2[a].at[r - 1], send2_sems.at[k], recv2_sems.at[k], (*chip(r), c)))
        for r in range(1, N_CHIPS):
            k = (N_CHIPS - 1) * n + r - 1
            stage2.append(remote(pland2.at[0], pland2.at[r], send2_sems.at[k], recv2_sems.at[k], (*chip(r), c)))
        for cp in stage2:
            cp.start()
        for cp in stage2:
            cp.wait()
        for a in range(n):
            out_refs[a][...] = out_refs[a][...] + ((land2[a][0].astype(F32) + land2[a][1].astype(F32)) + land2[a][2].astype(F32))
        ptot_ref[...] = (pland2[0] + pland2[1]) + (pland2[2] + pland2[3])

    shapes = [s.shape[1:] for s in slabs]
    n1, n2 = N_CHIPS * n + 1, (N_CHIPS - 1) * (n + 1)
    return pl.pallas_call(
        body, name="reduce_scatter_grads",
        in_specs=[HBM_SPEC] * n + [VMEM_SPEC], out_specs=[VMEM_SPEC] * (n + 1),
        out_shape=[SDS(s, F32) for s in shapes] + [SDS(packed.shape, F32)],
        scratch_shapes=[pltpu.VMEM((N_CHIPS,) + s, BF16) for s in shapes] * 2 + [pltpu.VMEM((N_CHIPS - 1,) + s, BF16) for s in shapes] * 2
        + [pltpu.VMEM(packed.shape, F32), pltpu.VMEM((N_CHIPS,) + packed.shape, F32)]
        + [pltpu.SemaphoreType.DMA((N_CHIPS * n,)), pltpu.SemaphoreType.DMA((n1,)), pltpu.SemaphoreType.DMA((n1,)),
           pltpu.SemaphoreType.DMA((n2,)), pltpu.SemaphoreType.DMA((n2,))],
        compiler_params=_cp(),
    )(*slabs, packed)


def _adamw(ws, gs, ms, vs):
    n = len(ws)

    def body(*refs):
        for k in range(n):
            w, g, m, v = (refs[j * n + k][...] for j in range(4))
            d_ref, nm_ref, nv_ref = (refs[(4 + j) * n + k] for j in range(3))
            m = ADAM_B1 * m + (1.0 - ADAM_B1) * g
            v = ADAM_B2 * v + (1.0 - ADAM_B2) * jnp.square(g)
            m_hat = m / (1.0 - ADAM_B1 ** ADAM_STEP)
            v_hat = v / (1.0 - ADAM_B2 ** ADAM_STEP)
            d_ref[...] = -ADAM_LR * (m_hat / (jnp.sqrt(v_hat) + ADAM_EPS) + ADAM_WD * w)
            nm_ref[...] = m
            nv_ref[...] = v

    outs = pl.pallas_call(
        body, name="adamw", in_specs=[VMEM_SPEC] * (4 * n), out_specs=[VMEM_SPEC] * (3 * n),
        out_shape=[SDS(w.shape, F32) for w in ws] * 3, compiler_params=_cp(),
    )(*ws, *gs, *ms, *vs)
    return outs[:n], outs[n:2 * n], outs[2 * n:]


WEIGHTS = ("norm_in", "w_in", "q_norm", "w_uq", "kv_norm", "w_ukv", "pool_w", "pool_scale", "w_branch_attn", "w_branch_pool",
           "w_out", "norm_final")
SHARDED = ("w_in", "w_branch_attn", "w_branch_pool", "w_out", "w_uq", "w_ukv")
REPLICATED = ("norm_in", "q_norm", "kv_norm", "pool_scale", "norm_final", "pool_w")
SUBLANES = 8


def _cols_to_slabs(g):
    r = g.shape[0]
    return g.reshape(r, N_DEV, -1).transpose(1, 0, 2)


def _slabs_to_cols(s):
    return s.transpose(1, 0, 2).reshape(s.shape[1], -1)


def _pack_rows(a):
    a = a.reshape(-1, LANES)
    return jnp.pad(a, ((0, -a.shape[0] % SUBLANES), (0, 0)))


def kernel(x, norm_in, w_in, q_norm, w_uq, kv_norm, w_ukv, pool_w, pool_scale, w_branch_attn, w_branch_pool, w_out, norm_final, loss_target, m_norm_in, m_w_in, m_q_norm, m_w_uq, m_kv_norm, m_w_ukv, m_pool_w, m_pool_scale, m_w_branch_attn, m_w_branch_pool, m_w_out, m_norm_final, v_norm_in, v_w_in, v_q_norm, v_w_uq, v_kv_norm, v_w_ukv, v_pool_w, v_pool_scale, v_w_branch_attn, v_w_branch_pool, v_w_out, v_norm_final):
    w = dict(norm_in=norm_in, w_in=w_in, q_norm=q_norm, w_uq=w_uq, kv_norm=kv_norm, w_ukv=w_ukv, pool_w=pool_w, pool_scale=pool_scale,
             w_branch_attn=w_branch_attn, w_branch_pool=w_branch_pool, w_out=w_out, norm_final=norm_final)
    m = dict(norm_in=m_norm_in, w_in=m_w_in, q_norm=m_q_norm, w_uq=m_w_uq, kv_norm=m_kv_norm, w_ukv=m_w_ukv, pool_w=m_pool_w,
             pool_scale=m_pool_scale, w_branch_attn=m_w_branch_attn, w_branch_pool=m_w_branch_pool, w_out=m_w_out, norm_final=m_norm_final)
    v = dict(norm_in=v_norm_in, w_in=v_w_in, q_norm=v_q_norm, w_uq=v_w_uq, kv_norm=v_kv_norm, w_ukv=v_w_ukv, pool_w=v_pool_w,
             pool_scale=v_pool_scale, w_branch_attn=v_w_branch_attn, w_branch_pool=v_w_branch_pool, w_out=v_w_out, norm_final=v_norm_final)

    def as2d(name, a):
        if name == "w_in":
            return a.T
        if name in ("w_uq", "w_ukv"):
            return a.reshape(a.shape[0], -1)
        if name == "pool_w":
            return a.reshape(-1, GROUP)
        return a.reshape(1, -1) if a.ndim == 1 else a

    def unshape(name, a):
        return a.T if name == "w_in" else a.reshape(w[name].shape)

    full = dict(zip(SHARDED, _all_gather_bf16([as2d(k, w[k]) for k in SHARDED])))
    loss_part, grad_x, grads = _local_step(
        x.reshape(x.shape[1:]), loss_target.reshape(x.shape[1:]), norm_in, full["w_in"].reshape(IN_TOTAL, D_MODEL), q_norm,
        full["w_uq"].reshape(Q_RANK, HEADS, NOPE + ROPE), kv_norm, full["w_ukv"].reshape(KV_RANK, HEADS, NOPE + VDIM),
        pool_w, pool_scale, _slabs_to_cols(full["w_branch_attn"]), _slabs_to_cols(full["w_branch_pool"]),
        full["w_out"].reshape(D_MODEL, D_MODEL), norm_final)

    slabs = [
        grads["w_in"].reshape(N_DEV, IN_TOTAL // N_DEV, D_MODEL), _cols_to_slabs(grads["w_branch_attn"]),
        _cols_to_slabs(grads["w_branch_pool"]),
        grads["w_out"].reshape(N_DEV, D_MODEL // N_DEV, D_MODEL), grads["w_uq"].reshape(N_DEV, Q_RANK // N_DEV, -1),
        grads["w_ukv"].reshape(N_DEV, KV_RANK // N_DEV, -1)]
    parts = [_pack_rows(grads[k]) for k in REPLICATED]
    parts.append(_pack_rows(jnp.zeros((LANES,), F32).at[0].set(loss_part)))
    *sums, ptot = _reduce_scatter([s.astype(BF16) for s in slabs], jnp.concatenate(parts, axis=0))
    g2d = dict(zip(SHARDED, sums))
    row = 0
    for k, part in zip(REPLICATED, parts):
        g2d[k] = as2d(k, ptot[row:row + w[k].size // LANES].reshape(w[k].shape))
        row += part.shape[0]
    loss = ptot[row, 0]

    deltas, new_m, new_v = _adamw([as2d(k, w[k]) for k in WEIGHTS], [g2d[k] for k in WEIGHTS],
                                  [as2d(k, m[k]) for k in WEIGHTS], [as2d(k, v[k]) for k in WEIGHTS])
    shaped = lambda arrs: [unshape(k, a) for k, a in zip(WEIGHTS, arrs)]
    return (loss, grad_x.reshape(x.shape), *shaped([g2d[k] for k in WEIGHTS]), *shaped(deltas), *shaped(new_m), *shaped(new_v))
```

```python
import functools

import jax
import jax.numpy as jnp
import numpy as np
from jax import lax
from jax.experimental import pallas as pl
from jax.experimental.pallas import tpu as pltpu

F32 = jnp.float32
BF16 = jnp.bfloat16
SDS = jax.ShapeDtypeStruct

D_MODEL = 1024
HEADS = 8
NOPE = 64
ROPE = 32
VDIM = 64
Q_RANK = 384
KV_RANK = 256
MLA_W = HEADS * VDIM
POOL_W = 512
POOL_GROUPS = 4
GROUP = POOL_W // POOL_GROUPS
CHUNK = 64
ROPE_THETA = 10000.0
EPS = 1e-6
SCALE = (NOPE + ROPE) ** -0.5
LOG2E = 1.4426950408889634
LN2 = 0.6931471805599453
QK_SCALE_LOG2 = SCALE * LOG2E
IN_TOTAL = 4256
ADAM_LR, ADAM_B1, ADAM_B2, ADAM_EPS, ADAM_WD, ADAM_STEP = 0.001, 0.9, 0.999, 1e-08, 0.01, 10

N_DEV = 8
LANES = 128
HEAD_PAD = LANES
HW = HEADS * HEAD_PAD

ZQ, ZKV, ZKR, GA, UP, GP, GM, ZTOT = 0, 384, 640, 768, 1280, 1792, 2304, 4352
FRONT_W = GA
ZKR_ORIG = 640

VMEM_LIMIT = 56 * 1024 * 1024


def _cp(sem=None, **kw):
    if sem is not None:
        kw["dimension_semantics"] = sem
    return pltpu.CompilerParams(vmem_limit_bytes=VMEM_LIMIT, **kw)


def _mm(a, b):
    return lax.dot_general(a, b, (((1,), (0,)), ((), ())), preferred_element_type=F32)


def _mm_nt(a, b):
    return lax.dot_general(a, b, (((1,), (1,)), ((), ())), preferred_element_type=F32)


def _mm_tn(a, b):
    return lax.dot_general(a, b, (((0,), (0,)), ((), ())), preferred_element_type=F32)


def _row_spec(tm, w):
    return pl.BlockSpec((tm, w), lambda i: (i, 0))


def _full_spec(a):
    nd = len(a.shape)
    return pl.BlockSpec(a.shape, lambda *_: (0,) * nd)


def _rope(v, c, sa, sb, sign):
    n = v.shape[-1]
    reps = n // LANES
    if reps > 1:
        c, sa, sb = (jnp.tile(t, (1, reps)) for t in (c, sa, sb))
    up = pltpu.roll(v, n - ROPE // 2, 1)
    dn = pltpu.roll(v, ROPE // 2, 1)
    return v * c + sign * (up * sa + dn * sb)


def _rope_tables(T):
    half = ROPE // 2
    inv_freq = ROPE_THETA ** (-jnp.arange(half, dtype=F32) / half)
    ang = jnp.arange(T, dtype=F32)[:, None] * inv_freq[None, :]
    cos, sin = jnp.cos(ang), jnp.sin(ang)
    z16 = jnp.zeros((T, half), F32)
    z32 = jnp.zeros((T, LANES - NOPE - ROPE), F32)
    c = jnp.concatenate([jnp.ones((T, NOPE), F32), cos, cos, z32], axis=1)
    sa = jnp.concatenate([jnp.zeros((T, NOPE), F32), -sin, z16, z32], axis=1)
    sb = jnp.concatenate([jnp.zeros((T, NOPE), F32), z16, sin, z32], axis=1)
    return c, sa, sb


def _silu_parts(g):
    sg = jax.nn.sigmoid(g)
    return g * sg, sg + g * sg * (1.0 - sg)


def _in_proj(x2, norm_in, w_in_pad, q_norm, wuq_pad, kv_norm, wukv, rc, rsa, rsb, tm):
    T = x2.shape[0]

    def body(x_ref, nin_ref, win_ref, qn_ref, wuq_ref, kvn_ref, wukv_ref, c_ref, sa_ref, sb_ref,
             hn_ref, zgm_ref, zga_ref, zup_ref, zgp_ref, zfr_ref, q_ref, k_ref, v_ref):
        xf = x_ref[...]
        r = lax.rsqrt(jnp.mean(xf * xf, axis=-1, keepdims=True) + EPS)
        hn = (xf * r * nin_ref[...]).astype(BF16)
        hn_ref[...] = hn
        z = _mm_nt(hn, win_ref[...])
        zgm_ref[...] = z[:, GM:ZTOT]
        zga_ref[...] = z[:, GA:UP]
        zup_ref[...] = z[:, UP:GP]
        zgp_ref[...] = z[:, GP:GM]
        zfr_ref[...] = z[:, ZQ:GA]
        zq, zkv, zkr = z[:, ZQ:ZKV], z[:, ZKV:ZKR], z[:, ZKR:GA]
        c, sa, sb = c_ref[...], sa_ref[...], sb_ref[...]
        rq = lax.rsqrt(jnp.mean(zq * zq, axis=-1, keepdims=True) + EPS)
        cq = (zq * rq * qn_ref[...]).astype(BF16)
        q = _rope(_mm(cq, wuq_ref[...]), c, sa, sb, 1.0)
        q_ref[...] = (q * QK_SCALE_LOG2).astype(BF16)
        rkv = lax.rsqrt(jnp.mean(zkv * zkv, axis=-1, keepdims=True) + EPS)
        ckv = (zkv * rkv * kvn_ref[...]).astype(BF16)
        kv = _mm(ckv, wukv_ref[...])
        kr = _rope(zkr, c, sa, sb, 1.0)
        lane = lax.broadcasted_iota(jnp.int32, kv.shape, 1) % LANES
        k_ref[...] = jnp.where(lane < NOPE, kv, jnp.tile(kr, (1, HEADS))).astype(BF16)
        v_ref[...] = jnp.where(lane < NOPE, 1.0, kv).astype(BF16)

    ins = (x2, norm_in, w_in_pad, q_norm, wuq_pad, kv_norm, wukv, rc, rsa, rsb)
    in_specs = [_row_spec(tm, D_MODEL), _full_spec(norm_in), _full_spec(w_in_pad), _full_spec(q_norm), _full_spec(wuq_pad),
                _full_spec(kv_norm), _full_spec(wukv), _row_spec(tm, LANES), _row_spec(tm, LANES), _row_spec(tm, LANES)]
    widths = [(D_MODEL, BF16), (ZTOT - GM, F32), (UP - GA, F32), (GP - UP, F32), (GM - GP, F32), (FRONT_W, F32),
              (HW, BF16), (HW, BF16), (HW, BF16)]
    return pl.pallas_call(
        body, name="in_proj", grid=(T // tm,), in_specs=in_specs,
        out_specs=[_row_spec(tm, w) for w, _ in widths],
        out_shape=[SDS((T, w), dt) for w, dt in widths],
        compiler_params=_cp(("parallel",)),
    )(*ins)


def _diag_mask(tq, transposed):
    r = lax.broadcasted_iota(jnp.int32, (tq, tq), 0) // CHUNK
    c = lax.broadcasted_iota(jnp.int32, (tq, tq), 1) // CHUNK
    return (r <= c) if transposed else (c <= r)


_HEAD_LANES = (slice(0, LANES), slice(LANES, 2 * LANES))


def _pair_rows_spec(tq):
    return pl.BlockSpec((1, 2, tq), lambda p, i: (p, 0, i))


def _store_pair_rows(ref, k, pair):
    t = pair.T
    ref[k, 0:1, :] = t[0:1, :]
    ref[k, 1:2, :] = t[VDIM:VDIM + 1, :]


def _attn_fwd(q_att, k_att, v_att, tq):
    T = q_att.shape[0]

    def body(q_ref, k_ref, v_ref, o_ref, lser_ref):
        i = pl.program_id(1)
        mask = _diag_mask(tq, False)
        lane = lax.broadcasted_iota(jnp.int32, (tq, LANES), 1)
        qs = [q_ref[:, hs] for hs in _HEAD_LANES]

        def step(j, carry, masked):
            rows = pl.ds(pl.multiple_of(j * tq, tq), tq)
            out = []
            for (m, acc), qh, hs in zip(carry, qs, _HEAD_LANES):
                s = _mm_nt(qh, k_ref[rows, hs])
                if masked:
                    s = jnp.where(mask, s, -jnp.inf)
                m_new = jnp.maximum(m, jnp.max(s, axis=-1, keepdims=True))
                p = jnp.exp2(s - m_new).astype(BF16)
                out.append((m_new, jnp.exp2(m - m_new) * acc + _mm(p, v_ref[rows, hs])))
            return tuple(out)

        init = ((jnp.full((tq, 1), -jnp.inf, F32), jnp.zeros((tq, LANES), F32)),) * 2
        carry = lax.fori_loop(0, i, functools.partial(step, masked=False), init)
        (ma, acca), (mb, accb) = step(i, carry, True)
        la, lb = acca[:, :1], accb[:, :1]
        o_ref[...] = jnp.where(lane < VDIM, pltpu.roll(acca / la, VDIM, 1), accb / lb)
        _store_pair_rows(lser_ref, 0, jnp.where(lane < VDIM, ma + jnp.log2(la), mb + jnp.log2(lb)))

    qspec = pl.BlockSpec((tq, 2 * LANES), lambda p, i: (i, p))
    kspec = pl.BlockSpec((T, 2 * LANES), lambda p, i: (0, p))
    ospec = pl.BlockSpec((tq, LANES), lambda p, i: (i, p))
    return pl.pallas_call(
        body, name="attn_fwd", grid=(HEADS // 2, T // tq), in_specs=[qspec, kspec, kspec],
        out_specs=[ospec, _pair_rows_spec(tq)],
        out_shape=[SDS((T, MLA_W), F32), SDS((HEADS // 2, 2, T), F32)],
        compiler_params=_cp(("parallel", "parallel")),
    )(q_att, k_att, v_att)


def _pick(g, vals):
    out = vals[-1]
    for k in range(len(vals) - 2, -1, -1):
        out = jnp.where(g == k, vals[k], out)
    return out


def _window_sum(u, g, forward):
    T = u.shape[0]
    row = lax.broadcasted_iota(jnp.int32, u.shape, 0)

    def sh(s, k):
        if forward:
            return jnp.where(row >= k, pltpu.roll(s, k, 0), 0.0)
        return jnp.where(row < T - k, pltpu.roll(s, T - k, 0), 0.0)

    sums, s = [], u
    for k in (1, 2, 4, 8):
        s = s + sh(s, k)
        sums.append(s)
    return _pick(g, sums)


def _pool_count(shape, g):
    row = lax.broadcasted_iota(jnp.int32, shape, 0)
    return jnp.minimum(row + 1, lax.shift_left(jnp.int32(2), g)).astype(F32)


def _pool_fwd(zup, zgp, pool_w, pool_scale):
    T = zup.shape[0]

    def body(u_ref, g_ref, w_ref, sc_ref, y_ref):
        g = pl.program_id(0)
        u = u_ref[...]
        d = _window_sum(u, g, True) / _pool_count(u.shape, g) - u
        lin = _mm(d.astype(BF16), w_ref[0].astype(BF16))
        silu, _ = _silu_parts(g_ref[...])
        y_ref[...] = (lin * sc_ref[...] * silu).astype(BF16)

    col = pl.BlockSpec((T, GROUP), lambda g: (0, g))
    return pl.pallas_call(
        body, name="pool_fwd", grid=(POOL_GROUPS,),
        in_specs=[col, col, pl.BlockSpec((1, GROUP, GROUP), lambda g: (g, 0, 0)), pl.BlockSpec((1, GROUP), lambda g: (0, g))],
        out_specs=col, out_shape=SDS((T, POOL_W), BF16), compiler_params=_cp(("parallel",)),
    )(zup, zgp, pool_w, pool_scale)


def _pool_bwd(zup, zgp, dyp, pool_w, pool_scale):
    T = zup.shape[0]

    def body(u_ref, g_ref, dy_ref, w_ref, sc_ref, du_ref, dg_ref, gw_ref, gsc_ref):
        g = pl.program_id(0)
        u = u_ref[...]
        cnt = _pool_count(u.shape, g)
        d = (_window_sum(u, g, True) / cnt - u).astype(BF16)
        wb = w_ref[0].astype(BF16)
        lin = _mm(d, wb)
        sc = sc_ref[...]
        silu, dsilu = _silu_parts(g_ref[...])
        dy = dy_ref[...]
        dg_ref[...] = (dy * lin * sc * dsilu).astype(BF16)
        dpre = dy * silu
        gsc_ref[...] = jnp.sum(dpre * lin, axis=0, keepdims=True)
        dlin = (dpre * sc).astype(BF16)
        gw_ref[0] = _mm_tn(d, dlin)
        dd = _mm_nt(dlin, wb)
        du_ref[...] = (_window_sum(dd / cnt, g, False) - dd).astype(BF16)

    col = pl.BlockSpec((T, GROUP), lambda g: (0, g))
    wspec = pl.BlockSpec((1, GROUP, GROUP), lambda g: (g, 0, 0))
    vspec = pl.BlockSpec((1, GROUP), lambda g: (0, g))
    return pl.pallas_call(
        body, name="pool_bwd", grid=(POOL_GROUPS,), in_specs=[col, col, col, wspec, vspec], out_specs=[col, col, wspec, vspec],
        out_shape=[SDS((T, POOL_W), BF16), SDS((T, POOL_W), BF16), SDS((POOL_GROUPS, GROUP, GROUP), F32), SDS((1, POOL_W), F32)],
        compiler_params=_cp(("parallel",)),
    )(zup, zgp, dyp, pool_w, pool_scale)


def _tail(x2, tgt, o, zga, ypool, zgm, wba, wbp, wout, norm_final, tm):
    T = x2.shape[0]

    def body(x_ref, tgt_ref, o_ref, zga_ref, yp_ref, zgm_ref, wba_ref, wbp_ref, wout_ref, nf_ref,
             loss_ref, dh_ref, dgm_ref, doop_ref, dga_ref, dcapr_ref, dyp_ref, gwout_ref, gwba_ref, gwbp_ref, gnf_ref):
        @pl.when(pl.program_id(0) == 0)
        def _():
            for ref in (loss_ref, gwout_ref, gwba_ref, gwbp_ref, gnf_ref):
                ref[...] = jnp.zeros_like(ref)

        o_v = o_ref[...]
        silu, dsilu = _silu_parts(zga_ref[...])
        ya = (o_v * silu).astype(BF16)
        yp = yp_ref[...]
        wba_v, wbp_v, wout_v = wba_ref[...], wbp_ref[...], wout_ref[...]
        a = _mm(ya, wba_v)
        p = _mm(yp, wbp_v)
        gate = jax.nn.sigmoid(zgm_ref[...])
        ga, gp = gate[:, :D_MODEL], gate[:, D_MODEL:]
        mg = (ga * a + gp * p).astype(BF16)
        h = x_ref[...] + _mm(mg, wout_v)
        r = lax.rsqrt(jnp.mean(h * h, axis=-1, keepdims=True) + EPS)
        gf = nf_ref[...]
        hr = h * r
        e = hr * gf - tgt_ref[...]
        loss_ref[...] += (0.5 / D_MODEL) * jnp.sum(e * e)
        dy = e * (1.0 / D_MODEL)
        gnf_ref[...] += jnp.sum(dy * hr, axis=0, keepdims=True)
        u = dy * gf
        dh = r * (u - hr * jnp.mean(u * hr, axis=-1, keepdims=True))
        dh_ref[...] = dh
        dhb = dh.astype(BF16)
        dmg = _mm_nt(dhb, wout_v)
        gwout_ref[...] += _mm_tn(mg, dhb)
        dgm_ref[:, :D_MODEL] = (dmg * a * ga * (1.0 - ga)).astype(BF16)
        dgm_ref[:, D_MODEL:] = (dmg * p * gp * (1.0 - gp)).astype(BF16)
        dab = (dmg * ga).astype(BF16)
        dpb = (dmg * gp).astype(BF16)
        dya = _mm_nt(dab, wba_v)
        gwba_ref[...] += _mm_tn(ya, dab)
        dyp_ref[...] = _mm_nt(dpb, wbp_v)
        gwbp_ref[...] += _mm_tn(yp, dpb)
        do = dya * silu
        dga_ref[...] = (dya * o_v * dsilu).astype(BF16)
        prod = do * o_v
        lo = lax.broadcasted_iota(jnp.int32, (tm, LANES), 1) < VDIM
        for pair in range(HEADS // 2):
            ls = slice(pair * LANES, (pair + 1) * LANES)
            do_p, prod_p = do[:, ls], prod[:, ls]
            dcap_a = jnp.sum(jnp.where(lo, prod_p, 0.0), axis=-1, keepdims=True)
            dcap_b = jnp.sum(jnp.where(lo, 0.0, prod_p), axis=-1, keepdims=True)
            _store_pair_rows(dcapr_ref, pair, jnp.where(lo, dcap_a, dcap_b))
            doop_ref[:, 2 * pair * LANES:(2 * pair + 1) * LANES] = jnp.where(lo, 0.0, pltpu.roll(do_p, VDIM, 1)).astype(BF16)
            doop_ref[:, (2 * pair + 1) * LANES:(2 * pair + 2) * LANES] = jnp.where(lo, 0.0, do_p).astype(BF16)

    ins = (x2, tgt, o, zga, ypool, zgm, wba, wbp, wout, norm_final)
    in_specs = [_row_spec(tm, D_MODEL), _row_spec(tm, D_MODEL), _row_spec(tm, MLA_W), _row_spec(tm, MLA_W), _row_spec(tm, POOL_W),
                _row_spec(tm, 2 * D_MODEL), _full_spec(wba), _full_spec(wbp), _full_spec(wout), _full_spec(norm_final)]
    outs = [SDS((8, LANES), F32), SDS((T, D_MODEL), F32), SDS((T, 2 * D_MODEL), BF16), SDS((T, HW), BF16), SDS((T, MLA_W), BF16),
            SDS((HEADS // 2, 2, T), F32), SDS((T, POOL_W), F32),
            SDS((D_MODEL, D_MODEL), F32), SDS((MLA_W, D_MODEL), F32), SDS((POOL_W, D_MODEL), F32), SDS((1, D_MODEL), F32)]
    out_specs = [_full_spec(outs[0]), _row_spec(tm, D_MODEL), _row_spec(tm, 2 * D_MODEL), _row_spec(tm, HW), _row_spec(tm, MLA_W),
                 pl.BlockSpec((HEADS // 2, 2, tm), lambda i: (0, 0, i)), _row_spec(tm, POOL_W),
                 _full_spec(outs[7]), _full_spec(outs[8]), _full_spec(outs[9]), _full_spec(outs[10])]
    return pl.pallas_call(
        body, name="tail", grid=(T // tm,), in_specs=in_specs, out_specs=out_specs, out_shape=outs,
        compiler_params=_cp(("arbitrary",)),
    )(*ins)


def _attn_bwd(q_att, k_att, v_att, doop, lse_rows, dcap_rows, tq):
    T = q_att.shape[0]
    nq = T // tq

    def body(q_ref, k_ref, v_ref, doop_ref, lse_ref, dcap_ref, dq_ref, dkv_ref, dkr_ref, dq_acc):
        j = pl.program_id(1)
        mask = _diag_mask(tq, True)
        lane = lax.broadcasted_iota(jnp.int32, (tq, LANES), 1)
        ks = [k_ref[:, hs] for hs in _HEAD_LANES]
        vs = [v_ref[:, hs] for hs in _HEAD_LANES]

        @pl.when(j == 0)
        def _():
            dq_acc[...] = jnp.zeros_like(dq_acc)

        def step(i, carry, masked):
            rows = pl.ds(pl.multiple_of(i * tq, tq), tq)
            out = []
            for h, ((dk, dv), kh, vh, hs) in enumerate(zip(carry, ks, vs, _HEAD_LANES)):
                qh = q_ref[rows, hs]
                doop_h = doop_ref[rows, hs]
                pt = jnp.exp2(_mm_nt(kh, qh) - lse_ref[0, h:h + 1, rows])
                if masked:
                    pt = jnp.where(mask, pt, 0.0)
                dv = dv + _mm(pt.astype(BF16), doop_h)
                dpt = _mm_nt(vh, doop_h)
                dst = (pt * (dpt - dcap_ref[0, h:h + 1, rows])).astype(BF16)
                dq_acc[rows, hs] += _mm_tn(dst, kh)
                out.append((dk + _mm(dst, qh), dv))
            return tuple(out)

        zero = jnp.zeros((tq, LANES), F32)
        carry = step(j, ((zero, zero), (zero, zero)), True)
        (dka, dva), (dkb, dvb) = lax.fori_loop(j + 1, nq, functools.partial(step, masked=False), carry)
        dka, dkb = dka * LN2, dkb * LN2
        dkv_ref[:, _HEAD_LANES[0]] = jnp.where(lane < NOPE, dka, dva).astype(BF16)
        dkv_ref[:, _HEAD_LANES[1]] = jnp.where(lane < NOPE, dkb, dvb).astype(BF16)
        dkr_ref[0] = jnp.where((lane >= NOPE) & (lane < NOPE + ROPE), dka + dkb, 0.0)

        @pl.when(j == nq - 1)
        def _():
            dq_ref[...] = (dq_acc[...] * SCALE).astype(BF16)

    kspec = pl.BlockSpec((tq, 2 * LANES), lambda p, j: (j, p))
    qspec = pl.BlockSpec((T, 2 * LANES), lambda p, j: (0, p))
    rspec = pl.BlockSpec((1, 2, T), lambda p, j: (p, 0, 0))
    return pl.pallas_call(
        body, name="attn_bwd", grid=(HEADS // 2, nq),
        in_specs=[qspec, kspec, kspec, qspec, rspec, rspec],
        out_specs=[qspec, kspec, pl.BlockSpec((1, tq, LANES), lambda p, j: (p, j, 0))],
        out_shape=[SDS((T, HW), BF16), SDS((T, HW), BF16), SDS((HEADS // 2, T, LANES), F32)],
        scratch_shapes=[pltpu.VMEM((T, 2 * LANES), F32)],
        compiler_params=_cp(("parallel", "arbitrary")),
    )(q_att, k_att, v_att, doop, lse_rows, dcap_rows)


def _rms_bwd(z, gain, dout):
    r = lax.rsqrt(jnp.mean(z * z, axis=-1, keepdims=True) + EPS)
    zr = z * r
    u = dout * gain
    return r * (u - zr * jnp.mean(u * zr, axis=-1, keepdims=True)), jnp.sum(dout * zr, axis=0, keepdims=True)


def _mla_bwd(dq_att, dkv_nat, dkr4, zfr, q_norm, wuq_pad, kv_norm, wukv, rc, rsa, rsb, tm):
    T = dq_att.shape[0]

    def body(dq_ref, dkv_ref, dkr_ref, zfr_ref, qn_ref, wuq_ref, kvn_ref, wukv_ref, c_ref, sa_ref, sb_ref,
             dfr_ref, gwuq_ref, gwukv_ref, gqn_ref, gkvn_ref):
        @pl.when(pl.program_id(0) == 0)
        def _():
            for ref in (gwuq_ref, gwukv_ref, gqn_ref, gkvn_ref):
                ref[...] = jnp.zeros_like(ref)

        c, sa, sb = c_ref[...], sa_ref[...], sb_ref[...]
        zq, zkv = zfr_ref[:, :Q_RANK], zfr_ref[:, Q_RANK:Q_RANK + KV_RANK]
        qn, kvn = qn_ref[...], kvn_ref[...]
        cq = (zq * lax.rsqrt(jnp.mean(zq * zq, axis=-1, keepdims=True) + EPS) * qn).astype(BF16)
        ckv = (zkv * lax.rsqrt(jnp.mean(zkv * zkv, axis=-1, keepdims=True) + EPS) * kvn).astype(BF16)
        dq = _rope(dq_ref[...].astype(F32), c, sa, sb, -1.0).astype(BF16)
        gwuq_ref[...] += _mm_tn(cq, dq)
        dzq, gqn = _rms_bwd(zq, qn, _mm_nt(dq, wuq_ref[...]))
        gqn_ref[...] += gqn
        dkv = dkv_ref[...]
        gwukv_ref[...] += _mm_tn(ckv, dkv)
        dzkv, gkvn = _rms_bwd(zkv, kvn, _mm_nt(dkv, wukv_ref[...]))
        gkvn_ref[...] += gkvn
        dkr = dkr_ref[0] + dkr_ref[1] + dkr_ref[2] + dkr_ref[3]
        dfr_ref[:, :Q_RANK] = dzq.astype(BF16)
        dfr_ref[:, Q_RANK:Q_RANK + KV_RANK] = dzkv.astype(BF16)
        dfr_ref[:, Q_RANK + KV_RANK:] = _rope(dkr, c, sa, sb, -1.0).astype(BF16)

    ins = (dq_att, dkv_nat, dkr4, zfr, q_norm, wuq_pad, kv_norm, wukv, rc, rsa, rsb)
    in_specs = [_row_spec(tm, HW), _row_spec(tm, HW), pl.BlockSpec((HEADS // 2, tm, LANES), lambda i: (0, i, 0)), _row_spec(tm, FRONT_W),
                _full_spec(q_norm), _full_spec(wuq_pad), _full_spec(kv_norm), _full_spec(wukv),
                _row_spec(tm, LANES), _row_spec(tm, LANES), _row_spec(tm, LANES)]
    outs = [SDS((T, FRONT_W), BF16), SDS((Q_RANK, HW), F32), SDS((KV_RANK, HW), F32), SDS((1, Q_RANK), F32), SDS((1, KV_RANK), F32)]
    out_specs = [_row_spec(tm, FRONT_W)] + [_full_spec(s) for s in outs[1:]]
    return pl.pallas_call(
        body, name="mla_bwd", grid=(T // tm,), in_specs=in_specs, out_specs=out_specs, out_shape=outs,
        compiler_params=_cp(("arbitrary",)),
    )(*ins)


_DZ_COLS = ((GM, ZTOT), (GA, UP), (UP, GP), (GP, GM), (ZQ, GA))


def _in_proj_bwd_x(dzs, x2, dh, norm_in, w_in_pad, tm):
    T = x2.shape[0]

    def body(d0, d1, d2, d3, d4, x_ref, dh_ref, nin_ref, win_ref, gx_ref, gnin_ref):
        @pl.when(pl.program_id(0) == 0)
        def _():
            gnin_ref[...] = jnp.zeros_like(gnin_ref)

        dhn = None
        for ref, (lo, hi) in zip((d0, d1, d2, d3, d4), _DZ_COLS):
            t = _mm(ref[...], win_ref[lo:hi, :])
            dhn = t if dhn is None else dhn + t
        dx, gnin = _rms_bwd(x_ref[...], nin_ref[...], dhn)
        gnin_ref[...] += gnin
        gx_ref[...] = dx + dh_ref[...]

    in_specs = [_row_spec(tm, hi - lo) for lo, hi in _DZ_COLS] + [_row_spec(tm, D_MODEL), _row_spec(tm, D_MODEL),
                                                                  _full_spec(norm_in), _full_spec(w_in_pad)]
    outs = [SDS((T, D_MODEL), F32), SDS((1, D_MODEL), F32)]
    return pl.pallas_call(
        body, name="in_proj_bwd_x", grid=(T // tm,), in_specs=in_specs, out_specs=[_row_spec(tm, D_MODEL), _full_spec(outs[1])],
        out_shape=outs, compiler_params=_cp(("arbitrary",)),
    )(*dzs, x2, dh, norm_in, w_in_pad)


def _in_proj_bwd_w(dzs, hn, tm):
    T = hn.shape[0]

    def body(d0, d1, d2, d3, d4, hn_ref, gw_ref):
        @pl.when(pl.program_id(0) == 0)
        def _():
            gw_ref[...] = jnp.zeros_like(gw_ref)

        hn_v = hn_ref[...]
        for ref, (lo, hi) in zip((d0, d1, d2, d3, d4), _DZ_COLS):
            gw_ref[lo:hi, :] += _mm_tn(ref[...], hn_v)

    in_specs = [_row_spec(tm, hi - lo) for lo, hi in _DZ_COLS] + [_row_spec(tm, D_MODEL)]
    out = SDS((ZTOT, D_MODEL), F32)
    return pl.pallas_call(
        body, name="in_proj_bwd_w", grid=(T // tm,), in_specs=in_specs, out_specs=_full_spec(out), out_shape=out,
        compiler_params=_cp(("arbitrary",)),
    )(*dzs, hn)


def _pad_w_in_t(w_in_t):
    z = functools.partial(jnp.zeros, dtype=w_in_t.dtype)
    cols = w_in_t.shape[1]
    return jnp.concatenate([w_in_t[:ZKR_ORIG], z((NOPE, cols)), w_in_t[ZKR_ORIG:ZKR_ORIG + ROPE], z((LANES - NOPE - ROPE, cols)),
                            w_in_t[ZKR_ORIG + ROPE:]], axis=0)


def _unpad_g_in_t(g):
    return jnp.concatenate([g[:ZKR], g[ZKR + NOPE:ZKR + NOPE + ROPE], g[GA:]], axis=0)


def _local_step(x2, tgt, norm_in, w_in_t, q_norm, w_uq, kv_norm, w_ukv, pool_w, pool_scale, w_ba, w_bp, w_out, norm_final):
    T = x2.shape[0]
    tm = min(256, T)
    tq = min(512, T)
    row = lambda v: v.reshape(1, -1)
    w_in_pad = _pad_w_in_t(w_in_t)
    wuq_pad = jnp.pad(w_uq, ((0, 0), (0, 0), (0, HEAD_PAD - NOPE - ROPE))).reshape(Q_RANK, HW)
    wukv = w_ukv.reshape(KV_RANK, HW)
    rc, rsa, rsb = _rope_tables(T)

    hn, zgm, zga, zup, zgp, zfr, q_att, k_att, v_att = _in_proj(
        x2, row(norm_in), w_in_pad, row(q_norm), wuq_pad, row(kv_norm), wukv, rc, rsa, rsb, tm)
    o, lse_rows = _attn_fwd(q_att, k_att, v_att, tq)
    ypool = _pool_fwd(zup, zgp, pool_w, row(pool_scale))
    loss8, dh, dgm, doop, dga, dcap_rows, dyp, g_wout, g_wba, g_wbp, g_nf = _tail(
        x2, tgt, o, zga, ypool, zgm, w_ba, w_bp, w_out, row(norm_final), tm)
    dup, dgp, g_pool_w, g_pool_scale = _pool_bwd(zup, zgp, dyp, pool_w, row(pool_scale))
    dq_att, dkv_nat, dkr4 = _attn_bwd(q_att, k_att, v_att, doop, lse_rows, dcap_rows, tq)
    dfr, g_wuq_pad, g_wukv, g_qn, g_kvn = _mla_bwd(
        dq_att, dkv_nat, dkr4, zfr, row(q_norm), wuq_pad, row(kv_norm), wukv, rc, rsa, rsb, tm)
    dzs = (dgm, dga, dup, dgp, dfr)
    grad_x, g_nin = _in_proj_bwd_x(dzs, x2, dh, row(norm_in), w_in_pad, tm)
    g_win_pad = _in_proj_bwd_w(dzs, hn, tm)

    grads = dict(
        norm_in=g_nin.reshape(-1), w_in=_unpad_g_in_t(g_win_pad), q_norm=g_qn.reshape(-1),
        w_uq=g_wuq_pad.reshape(Q_RANK, HEADS, HEAD_PAD)[:, :, :NOPE + ROPE], kv_norm=g_kvn.reshape(-1),
        w_ukv=g_wukv.reshape(KV_RANK, HEADS, NOPE + VDIM), pool_w=g_pool_w, pool_scale=g_pool_scale.reshape(-1),
        w_branch_attn=g_wba, w_branch_pool=g_wbp, w_out=g_wout, norm_final=g_nf.reshape(-1))
    return loss8[0, 0], grad_x, grads


MESH_ID = pl.DeviceIdType.MESH
VMEM_SPEC = pl.BlockSpec(memory_space=pltpu.VMEM)
HBM_SPEC = pl.BlockSpec(memory_space=pl.ANY)


def _mesh_pos():
    return lax.axis_index("x"), lax.axis_index("y"), lax.axis_index("c")


def _slot(px, py, pc):
    return 4 * px + 2 * py + pc


def _all_gather_bf16(shards):
    n = len(shards)

    def body(*refs):
        ins, outs, stage = refs[:n], refs[n:2 * n], refs[2 * n:3 * n]
        send_sems, recv_sems, local_sems = refs[3 * n:]
        x, y, c = _mesh_pos()
        me, sibling = (x, y, c), (x, y, 1 - c)
        chips = [(1 - x, y), (x, 1 - y), (1 - x, 1 - y)]

        def copy(a, k, block, to, from_stage=False):
            dst = outs[a].at[_slot(*block)]
            return pltpu.make_async_remote_copy(
                src_ref=stage[a] if from_stage else dst, dst_ref=dst, send_sem=send_sems.at[7 * a + k],
                recv_sem=recv_sems.at[7 * a + k], device_id=to, device_id_type=MESH_ID)

        for a in range(n):
            stage[a][...] = ins[a][...].astype(BF16)
        mine = [pltpu.make_async_copy(stage[a], outs[a].at[_slot(*me)], local_sems.at[a]) for a in range(n)]
        for cp in mine:
            cp.start()
        first = []
        for a in range(n):
            first.append(copy(a, 0, me, sibling, True))
            first += [copy(a, 1 + j, me, (*chip, c), True) for j, chip in enumerate(chips)]
        for cp in first:
            cp.start()
        passed = []
        for a in range(n):
            for j, chip in enumerate(chips):
                copy(a, 1 + j, (*chip, c), me).wait_recv()
                passed.append(copy(a, 4 + j, (*chip, c), sibling))
                passed[-1].start()
        for a in range(n):
            copy(a, 0, sibling, me).wait_recv()
            for j, chip in enumerate(chips):
                copy(a, 4 + j, (*chip, 1 - c), me).wait_recv()
        for cp in first + passed:
            cp.wait_send()
        for cp in mine:
            cp.wait()

    return pl.pallas_call(
        body, name="all_gather_weights",
        in_specs=[VMEM_SPEC] * n, out_specs=[HBM_SPEC] * n,
        out_shape=[SDS((N_DEV,) + s.shape, BF16) for s in shards],
        scratch_shapes=[pltpu.VMEM(s.shape, BF16) for s in shards]
        + [pltpu.SemaphoreType.DMA((7 * n,)), pltpu.SemaphoreType.DMA((7 * n,)), pltpu.SemaphoreType.DMA((n,))],
        compiler_params=_cp(),
    )(*shards)


N_CHIPS = 4


def _reduce_scatter(slabs, packed):
    n = len(slabs)

    def body(*refs):
        slab_refs, packed_ref = refs[:n], refs[n]
        out_refs, ptot_ref = refs[n + 1:2 * n + 1], refs[2 * n + 1]
        own1, land1, send2, land2 = (refs[(2 + k) * n + 2:(3 + k) * n + 2] for k in range(4))
        pland1, pland2, loc_sems, send1_sems, recv1_sems, send2_sems, recv2_sems = refs[6 * n + 2:]
        x, y, c = _mesh_pos()
        sibling = (x, y, 1 - c)

        def chip(r):
            return (1 - x if r & 2 else x, 1 - y if r & 1 else y)

        def remote(src, dst, send_sem, recv_sem, to):
            return pltpu.make_async_remote_copy(src_ref=src, dst_ref=dst, send_sem=send_sem, recv_sem=recv_sem, device_id=to,
                                                device_id_type=MESH_ID)

        stage1 = []
        for a in range(n):
            for r in range(N_CHIPS):
                k = N_CHIPS * a + r
                stage1.append(pltpu.make_async_copy(slab_refs[a].at[_slot(*chip(r), c)], own1[a].at[r], loc_sems.at[k]))
                stage1.append(remote(slab_refs[a].at[_slot(*chip(r), 1 - c)], land1[a].at[r], send1_sems.at[k], recv1_sems.at[k], sibling))
        stage1.append(remote(packed_ref, pland1, send1_sems.at[N_CHIPS * n], recv1_sems.at[N_CHIPS * n], sibling))
        for cp in stage1:
            cp.start()
        for cp in stage1:
            cp.wait()
        for a in range(n):
            out_refs[a][...] = own1[a][0].astype(F32) + land1[a][0].astype(F32)
            for r in range(1, N_CHIPS):
                send2[a][r - 1] = (own1[a][r].astype(F32) + land1[a][r].astype(F32)).astype(BF16)
        pland2[0] = packed_ref[...] + pland1[...]

        stage2 = []
        for a in range(n):
            for r in range(1, N_CHIPS):
                k = (N_CHIPS - 1) * a + r - 1
                stage2.append(remote(send2[a].at[r - 1], land2[a].at[r - 1], send2_sems.at[k], recv2_sems.at[k], (*chip(r), c)))
        for r in range(1, N_CHIPS):
            k = (N_CHIPS - 1) * n + r - 1
            stage2.append(remote(pland2.at[0], pland2.at[r], send2_sems.at[k], recv2_sems.at[k], (*chip(r), c)))
        for cp in stage2:
            cp.start()
        for cp in stage2:
            cp.wait()
        for a in range(n):
            out_refs[a][...] = out_refs[a][...] + ((land2[a][0].astype(F32) + land2[a][1].astype(F32)) + land2[a][2].astype(F32))
        ptot_ref[...] = (pland2[0] + pland2[1]) + (pland2[2] + pland2[3])

    shapes = [s.shape[1:] for s in slabs]
    n1, n2 = N_CHIPS * n + 1, (N_CHIPS - 1) * (n + 1)
    return pl.pallas_call(
        body, name="reduce_scatter_grads",
        in_specs=[HBM_SPEC] * n + [VMEM_SPEC], out_specs=[VMEM_SPEC] * (n + 1),
        out_shape=[SDS(s, F32) for s in shapes] + [SDS(packed.shape, F32)],
        scratch_shapes=[pltpu.VMEM((N_CHIPS,) + s, BF16) for s in shapes] * 2 + [pltpu.VMEM((N_CHIPS - 1,) + s, BF16) for s in shapes] * 2
        + [pltpu.VMEM(packed.shape, F32), pltpu.VMEM((N_CHIPS,) + packed.shape, F32)]
        + [pltpu.SemaphoreType.DMA((N_CHIPS * n,)), pltpu.SemaphoreType.DMA((n1,)), pltpu.SemaphoreType.DMA((n1,)),
           pltpu.SemaphoreType.DMA((n2,)), pltpu.SemaphoreType.DMA((n2,))],
        compiler_params=_cp(),
    )(*slabs, packed)


def _adamw(ws, gs, ms, vs):
    n = len(ws)

    def body(*refs):
        for k in range(n):
            w, g, m, v = (refs[j * n + k][...] for j in range(4))
            d_ref, nm_ref, nv_ref = (refs[(4 + j) * n + k] for j in range(3))
            m = ADAM_B1 * m + (1.0 - ADAM_B1) * g
            v = ADAM_B2 * v + (1.0 - ADAM_B2) * jnp.square(g)
            m_hat = m / (1.0 - ADAM_B1 ** ADAM_STEP)
            v_hat = v / (1.0 - ADAM_B2 ** ADAM_STEP)
            d_ref[...] = -ADAM_LR * (m_hat / (jnp.sqrt(v_hat) + ADAM_EPS) + ADAM_WD * w)
            nm_ref[...] = m
            nv_ref[...] = v

    outs = pl.pallas_call(
        body, name="adamw", in_specs=[VMEM_SPEC] * (4 * n), out_specs=[VMEM_SPEC] * (3 * n),
        out_shape=[SDS(w.shape, F32) for w in ws] * 3, compiler_params=_cp(),
    )(*ws, *gs, *ms, *vs)
    return outs[:n], outs[n:2 * n], outs[2 * n:]


WEIGHTS = ("norm_in", "w_in", "q_norm", "w_uq", "kv_norm", "w_ukv", "pool_w", "pool_scale", "w_branch_attn", "w_branch_pool",
           "w_out", "norm_final")
SHARDED = ("w_in", "w_branch_attn", "w_branch_pool", "w_out", "w_uq", "w_ukv")
REPLICATED = ("norm_in", "q_norm", "kv_norm", "pool_scale", "norm_final", "pool_w")
SUBLANES = 8


def _cols_to_slabs(g):
    r = g.shape[0]
    return g.reshape(r, N_DEV, -1).transpose(1, 0, 2)


def _slabs_to_cols(s):
    return s.transpose(1, 0, 2).reshape(s.shape[1], -1)


def _pack_rows(a):
    a = a.reshape(-1, LANES)
    return jnp.pad(a, ((0, -a.shape[0] % SUBLANES), (0, 0)))


def kernel(x, norm_in, w_in, q_norm, w_uq, kv_norm, w_ukv, pool_w, pool_scale, w_branch_attn, w_branch_pool, w_out, norm_final, loss_target, m_norm_in, m_w_in, m_q_norm, m_w_uq, m_kv_norm, m_w_ukv, m_pool_w, m_pool_scale, m_w_branch_attn, m_w_branch_pool, m_w_out, m_norm_final, v_norm_in, v_w_in, v_q_norm, v_w_uq, v_kv_norm, v_w_ukv, v_pool_w, v_pool_scale, v_w_branch_attn, v_w_branch_pool, v_w_out, v_norm_final):
    w = dict(norm_in=norm_in, w_in=w_in, q_norm=q_norm, w_uq=w_uq, kv_norm=kv_norm, w_ukv=w_ukv, pool_w=pool_w, pool_scale=pool_scale,
             w_branch_attn=w_branch_attn, w_branch_pool=w_branch_pool, w_out=w_out, norm_final=norm_final)
    m = dict(norm_in=m_norm_in, w_in=m_w_in, q_norm=m_q_norm, w_uq=m_w_uq, kv_norm=m_kv_norm, w_ukv=m_w_ukv, pool_w=m_pool_w,
             pool_scale=m_pool_scale, w_branch_attn=m_w_branch_attn, w_branch_pool=m_w_branch_pool, w_out=m_w_out, norm_final=m_norm_final)
    v = dict(norm_in=v_norm_in, w_in=v_w_in, q_norm=v_q_norm, w_uq=v_w_uq, kv_norm=v_kv_norm, w_ukv=v_w_ukv, pool_w=v_pool_w,
             pool_scale=v_pool_scale, w_branch_attn=v_w_branch_attn, w_branch_pool=v_w_branch_pool, w_out=v_w_out, norm_final=v_norm_final)

    def as2d(name, a):
        if name == "w_in":
            return a.T
        if name in ("w_uq", "w_ukv"):
            return a.reshape(a.shape[0], -1)
        if name == "pool_w":
            return a.reshape(-1, GROUP)
        return a.reshape(1, -1) if a.ndim == 1 else a

    def unshape(name, a):
        return a.T if name == "w_in" else a.reshape(w[name].shape)

    full = dict(zip(SHARDED, _all_gather_bf16([as2d(k, w[k]) for k in SHARDED])))
    loss_part, grad_x, grads = _local_step(
        x.reshape(x.shape[1:]), loss_target.reshape(x.shape[1:]), norm_in, full["w_in"].reshape(IN_TOTAL, D_MODEL), q_norm,
        full["w_uq"].reshape(Q_RANK, HEADS, NOPE + ROPE), kv_norm, full["w_ukv"].reshape(KV_RANK, HEADS, NOPE + VDIM),
        pool_w, pool_scale, _slabs_to_cols(full["w_branch_attn"]), _slabs_to_cols(full["w_branch_pool"]),
        full["w_out"].reshape(D_MODEL, D_MODEL), norm_final)

    slabs = [
        grads["w_in"].reshape(N_DEV, IN_TOTAL // N_DEV, D_MODEL), _cols_to_slabs(grads["w_branch_attn"]),
        _cols_to_slabs(grads["w_branch_pool"]),
        grads["w_out"].reshape(N_DEV, D_MODEL // N_DEV, D_MODEL), grads["w_uq"].reshape(N_DEV, Q_RANK // N_DEV, -1),
        grads["w_ukv"].reshape(N_DEV, KV_RANK // N_DEV, -1)]
    parts = [_pack_rows(grads[k]) for k in REPLICATED]
    parts.append(_pack_rows(jnp.zeros((LANES,), F32).at[0].set(loss_part)))
    *sums, ptot = _reduce_scatter([s.astype(BF16) for s in slabs], jnp.concatenate(parts, axis=0))
    g2d = dict(zip(SHARDED, sums))
    row = 0
    for k, part in zip(REPLICATED, parts):
        g2d[k] = as2d(k, ptot[row:row + w[k].size // LANES].reshape(w[k].shape))
        row += part.shape[0]
    loss = ptot[row, 0]

    deltas, new_m, new_v = _adamw([as2d(k, w[k]) for k in WEIGHTS], [g2d[k] for k in WEIGHTS],
                                  [as2d(k, m[k]) for k in WEIGHTS], [as2d(k, v[k]) for k in WEIGHTS])
    shaped = lambda arrs: [unshape(k, a) for k, a in zip(WEIGHTS, arrs)]
    return (loss, grad_x.reshape(x.shape), *shaped([g2d[k] for k in WEIGHTS]), *shaped(deltas), *shaped(new_m), *shaped(new_v))
```

```python
import functools

import jax
import jax.numpy as jnp
import numpy as np
from jax import lax
from jax.experimental import pallas as pl
from jax.experimental.pallas import tpu as pltpu

F32 = jnp.float32
BF16 = jnp.bfloat16
SDS = jax.ShapeDtypeStruct

D_MODEL = 1024
HEADS = 8
NOPE = 64
ROPE = 32
VDIM = 64
Q_RANK = 384
KV_RANK = 256
MLA_W = HEADS * VDIM
POOL_W = 512
POOL_GROUPS = 4
GROUP = POOL_W // POOL_GROUPS
CHUNK = 64
ROPE_THETA = 10000.0
EPS = 1e-6
SCALE = (NOPE + ROPE) ** -0.5
LOG2E = 1.4426950408889634
LN2 = 0.6931471805599453
QK_SCALE_LOG2 = SCALE * LOG2E
IN_TOTAL = 4256
ADAM_LR, ADAM_B1, ADAM_B2, ADAM_EPS, ADAM_WD, ADAM_STEP = 0.001, 0.9, 0.999, 1e-08, 0.01, 10

N_DEV = 8
LANES = 128
HEAD_PAD = LANES
HW = HEADS * HEAD_PAD

ZQ, ZKV, ZKR, GA, UP, GP, GM, ZTOT = 0, 384, 640, 768, 1280, 1792, 2304, 4352
FRONT_W = GA
ZKR_ORIG = 640

VMEM_LIMIT = 56 * 1024 * 1024


def _cp(sem=None, **kw):
    if sem is not None:
        kw["dimension_semantics"] = sem
    return pltpu.CompilerParams(vmem_limit_bytes=VMEM_LIMIT, **kw)


def _mm(a, b):
    return lax.dot_general(a, b, (((1,), (0,)), ((), ())), preferred_element_type=F32)


def _mm_nt(a, b):
    return lax.dot_general(a, b, (((1,), (1,)), ((), ())), preferred_element_type=F32)


def _mm_tn(a, b):
    return lax.dot_general(a, b, (((0,), (0,)), ((), ())), preferred_element_type=F32)


def _row_spec(tm, w):
    return pl.BlockSpec((tm, w), lambda i: (i, 0))


def _full_spec(a):
    nd = len(a.shape)
    return pl.BlockSpec(a.shape, lambda *_: (0,) * nd)


def _rope(v, c, sa, sb, sign):
    n = v.shape[-1]
    reps = n // LANES
    if reps > 1:
        c, sa, sb = (jnp.tile(t, (1, reps)) for t in (c, sa, sb))
    up = pltpu.roll(v, n - ROPE // 2, 1)
    dn = pltpu.roll(v, ROPE // 2, 1)
    return v * c + sign * (up * sa + dn * sb)


def _rope_tables(T):
    half = ROPE // 2
    inv_freq = np.float32(ROPE_THETA) ** (-np.arange(half, dtype=np.float32) / np.float32(half))
    ang = np.arange(T, dtype=np.float32)[:, None] * inv_freq[None, :].astype(np.float32)
    cos, sin = np.cos(ang.astype(np.float64)).astype(np.float32), np.sin(ang.astype(np.float64)).astype(np.float32)
    z16 = np.zeros((T, half), np.float32)
    z32 = np.zeros((T, LANES - NOPE - ROPE), np.float32)
    c = np.concatenate([np.ones((T, NOPE), np.float32), cos, cos, z32], axis=1)
    sa = np.concatenate([np.zeros((T, NOPE), np.float32), -sin, z16, z32], axis=1)
    sb = np.concatenate([np.zeros((T, NOPE), np.float32), z16, sin, z32], axis=1)
    return jnp.asarray(c), jnp.asarray(sa), jnp.asarray(sb)


def _silu_parts(g):
    sg = jax.nn.sigmoid(g)
    return g * sg, sg + g * sg * (1.0 - sg)


def _in_proj(x2, norm_in, w_in_pad, q_norm, wuq_pad, kv_norm, wukv, rc, rsa, rsb, tm):
    T = x2.shape[0]

    def body(x_ref, nin_ref, win_ref, qn_ref, wuq_ref, kvn_ref, wukv_ref, c_ref, sa_ref, sb_ref,
             hn_ref, zgm_ref, zga_ref, zup_ref, zgp_ref, zfr_ref, q_ref, k_ref, v_ref):
        xf = x_ref[...]
        r = lax.rsqrt(jnp.mean(xf * xf, axis=-1, keepdims=True) + EPS)
        hn = (xf * r * nin_ref[...]).astype(BF16)
        hn_ref[...] = hn
        z = _mm_nt(hn, win_ref[...])
        zgm_ref[...] = z[:, GM:ZTOT]
        zga_ref[...] = z[:, GA:UP]
        zup_ref[...] = z[:, UP:GP]
        zgp_ref[...] = z[:, GP:GM]
        zfr_ref[...] = z[:, ZQ:GA]
        zq, zkv, zkr = z[:, ZQ:ZKV], z[:, ZKV:ZKR], z[:, ZKR:GA]
        c, sa, sb = c_ref[...], sa_ref[...], sb_ref[...]
        rq = lax.rsqrt(jnp.mean(zq * zq, axis=-1, keepdims=True) + EPS)
        cq = (zq * rq * qn_ref[...]).astype(BF16)
        q = _rope(_mm(cq, wuq_ref[...]), c, sa, sb, 1.0)
        q_ref[...] = (q * QK_SCALE_LOG2).astype(BF16)
        rkv = lax.rsqrt(jnp.mean(zkv * zkv, axis=-1, keepdims=True) + EPS)
        ckv = (zkv * rkv * kvn_ref[...]).astype(BF16)
        kv = _mm(ckv, wukv_ref[...])
        kr = _rope(zkr, c, sa, sb, 1.0)
        lane = lax.broadcasted_iota(jnp.int32, kv.shape, 1) % LANES
        k_ref[...] = jnp.where(lane < NOPE, kv, jnp.tile(kr, (1, HEADS))).astype(BF16)
        v_ref[...] = jnp.where(lane < NOPE, 1.0, kv).astype(BF16)

    ins = (x2, norm_in, w_in_pad, q_norm, wuq_pad, kv_norm, wukv, rc, rsa, rsb)
    in_specs = [_row_spec(tm, D_MODEL), _full_spec(norm_in), _full_spec(w_in_pad), _full_spec(q_norm), _full_spec(wuq_pad),
                _full_spec(kv_norm), _full_spec(wukv), _row_spec(tm, LANES), _row_spec(tm, LANES), _row_spec(tm, LANES)]
    widths = [(D_MODEL, BF16), (ZTOT - GM, F32), (UP - GA, F32), (GP - UP, F32), (GM - GP, F32), (FRONT_W, F32),
              (HW, BF16), (HW, BF16), (HW, BF16)]
    return pl.pallas_call(
        body, name="in_proj", grid=(T // tm,), in_specs=in_specs,
        out_specs=[_row_spec(tm, w) for w, _ in widths],
        out_shape=[SDS((T, w), dt) for w, dt in widths],
        compiler_params=_cp(("parallel",)),
    )(*ins)


def _diag_mask(tq, transposed):
    r = lax.broadcasted_iota(jnp.int32, (tq, tq), 0) // CHUNK
    c = lax.broadcasted_iota(jnp.int32, (tq, tq), 1) // CHUNK
    return (r <= c) if transposed else (c <= r)


_HEAD_LANES = (slice(0, LANES), slice(LANES, 2 * LANES))


def _pair_rows_spec(tq):
    return pl.BlockSpec((1, 2, tq), lambda p, i: (p, 0, i))


def _store_pair_rows(ref, k, pair):
    t = pair.T
    ref[k, 0:1, :] = t[0:1, :]
    ref[k, 1:2, :] = t[VDIM:VDIM + 1, :]


def _attn_fwd(q_att, k_att, v_att, tq, hps):
    T = q_att.shape[0]
    head_lanes = [slice(h * LANES, (h + 1) * LANES) for h in range(hps)]

    def body(q_ref, k_ref, v_ref, o_ref, lser_ref):
        i = pl.program_id(1)
        mask = _diag_mask(tq, False)
        lane = lax.broadcasted_iota(jnp.int32, (tq, LANES), 1)
        qs = [q_ref[:, hs] for hs in head_lanes]

        def step(j, carry, masked):
            rows = pl.ds(pl.multiple_of(j * tq, tq), tq)
            out = []
            for (m, acc), qh, hs in zip(carry, qs, head_lanes):
                s = _mm_nt(qh, k_ref[rows, hs])
                if masked:
                    s = jnp.where(mask, s, -jnp.inf)
                m_new = jnp.maximum(m, jnp.max(s, axis=-1, keepdims=True))
                p = jnp.exp2(s - m_new).astype(BF16)
                out.append((m_new, jnp.exp2(m - m_new) * acc + _mm(p, v_ref[rows, hs])))
            return tuple(out)

        init = ((jnp.full((tq, 1), -jnp.inf, F32), jnp.zeros((tq, LANES), F32)),) * hps
        res = step(i, lax.fori_loop(0, i, functools.partial(step, masked=False), init), True)
        for pair in range(hps // 2):
            (ma, acca), (mb, accb) = res[2 * pair], res[2 * pair + 1]
            la, lb = acca[:, :1], accb[:, :1]
            o_ref[:, pair * LANES:(pair + 1) * LANES] = jnp.where(lane < VDIM, pltpu.roll(acca / la, VDIM, 1), accb / lb)
            _store_pair_rows(lser_ref, pair, jnp.where(lane < VDIM, ma + jnp.log2(la), mb + jnp.log2(lb)))

    qspec = pl.BlockSpec((tq, hps * LANES), lambda p, i: (i, p))
    kspec = pl.BlockSpec((T, hps * LANES), lambda p, i: (0, p))
    ospec = pl.BlockSpec((tq, hps * VDIM), lambda p, i: (i, p))
    return pl.pallas_call(
        body, name="attn_fwd", grid=(HEADS // hps, T // tq), in_specs=[qspec, kspec, kspec],
        out_specs=[ospec, pl.BlockSpec((hps // 2, 2, tq), lambda p, i: (p, 0, i))],
        out_shape=[SDS((T, MLA_W), F32), SDS((HEADS // 2, 2, T), F32)],
        compiler_params=_cp(("parallel", "parallel")),
    )(q_att, k_att, v_att)


def _pick(g, vals):
    out = vals[-1]
    for k in range(len(vals) - 2, -1, -1):
        out = jnp.where(g == k, vals[k], out)
    return out


def _window_sum(u, g, forward):
    T = u.shape[0]
    row = lax.broadcasted_iota(jnp.int32, u.shape, 0)

    def sh(s, k):
        if forward:
            return jnp.where(row >= k, pltpu.roll(s, k, 0), 0.0)
        return jnp.where(row < T - k, pltpu.roll(s, T - k, 0), 0.0)

    sums, s = [], u
    for k in (1, 2, 4, 8):
        s = s + sh(s, k)
        sums.append(s)
    return _pick(g, sums)


def _pool_count(shape, g):
    row = lax.broadcasted_iota(jnp.int32, shape, 0)
    return jnp.minimum(row + 1, lax.shift_left(jnp.int32(2), g)).astype(F32)


def _pool_fwd(zup, zgp, pool_w, pool_scale):
    T = zup.shape[0]

    def body(u_ref, g_ref, w_ref, sc_ref, y_ref):
        g = pl.program_id(0)
        u = u_ref[...]
        d = _window_sum(u, g, True) / _pool_count(u.shape, g) - u
        lin = _mm(d.astype(BF16), w_ref[0].astype(BF16))
        silu, _ = _silu_parts(g_ref[...])
        y_ref[...] = (lin * sc_ref[...] * silu).astype(BF16)

    col = pl.BlockSpec((T, GROUP), lambda g: (0, g))
    return pl.pallas_call(
        body, name="pool_fwd", grid=(POOL_GROUPS,),
        in_specs=[col, col, pl.BlockSpec((1, GROUP, GROUP), lambda g: (g, 0, 0)), pl.BlockSpec((1, GROUP), lambda g: (0, g))],
        out_specs=col, out_shape=SDS((T, POOL_W), BF16), compiler_params=_cp(("parallel",)),
    )(zup, zgp, pool_w, pool_scale)


def _pool_bwd(zup, zgp, dyp, pool_w, pool_scale):
    T = zup.shape[0]

    def body(u_ref, g_ref, dy_ref, w_ref, sc_ref, du_ref, dg_ref, gw_ref, gsc_ref):
        g = pl.program_id(0)
        u = u_ref[...]
        cnt = _pool_count(u.shape, g)
        d = (_window_sum(u, g, True) / cnt - u).astype(BF16)
        wb = w_ref[0].astype(BF16)
        lin = _mm(d, wb)
        sc = sc_ref[...]
        silu, dsilu = _silu_parts(g_ref[...])
        dy = dy_ref[...]
        dg_ref[...] = (dy * lin * sc * dsilu).astype(BF16)
        dpre = dy * silu
        gsc_ref[...] = jnp.sum(dpre * lin, axis=0, keepdims=True)
        dlin = (dpre * sc).astype(BF16)
        gw_ref[0] = _mm_tn(d, dlin)
        dd = _mm_nt(dlin, wb)
        du_ref[...] = (_window_sum(dd / cnt, g, False) - dd).astype(BF16)

    col = pl.BlockSpec((T, GROUP), lambda g: (0, g))
    wspec = pl.BlockSpec((1, GROUP, GROUP), lambda g: (g, 0, 0))
    vspec = pl.BlockSpec((1, GROUP), lambda g: (0, g))
    return pl.pallas_call(
        body, name="pool_bwd", grid=(POOL_GROUPS,), in_specs=[col, col, col, wspec, vspec], out_specs=[col, col, wspec, vspec],
        out_shape=[SDS((T, POOL_W), BF16), SDS((T, POOL_W), BF16), SDS((POOL_GROUPS, GROUP, GROUP), F32), SDS((1, POOL_W), F32)],
        compiler_params=_cp(("parallel",)),
    )(zup, zgp, dyp, pool_w, pool_scale)


def _tail(x2, tgt, o, zga, ypool, zgm, wba, wbp, wout, norm_final, tm):
    T = x2.shape[0]

    def body(x_ref, tgt_ref, o_ref, zga_ref, yp_ref, zgm_ref, wba_ref, wbp_ref, wout_ref, nf_ref,
             loss_ref, dh_ref, dgm_ref, doop_ref, dga_ref, dcapr_ref, dyp_ref, gwout_ref, gwba_ref, gwbp_ref, gnf_ref):
        @pl.when(pl.program_id(0) == 0)
        def _():
            for ref in (loss_ref, gwout_ref, gwba_ref, gwbp_ref, gnf_ref):
                ref[...] = jnp.zeros_like(ref)

        o_v = o_ref[...]
        silu, dsilu = _silu_parts(zga_ref[...])
        ya = (o_v * silu).astype(BF16)
        yp = yp_ref[...]
        wba_v, wbp_v, wout_v = wba_ref[...], wbp_ref[...], wout_ref[...]
        a = _mm(ya, wba_v)
        p = _mm(yp, wbp_v)
        gate = jax.nn.sigmoid(zgm_ref[...])
        ga, gp = gate[:, :D_MODEL], gate[:, D_MODEL:]
        mg = (ga * a + gp * p).astype(BF16)
        h = x_ref[...] + _mm(mg, wout_v)
        r = lax.rsqrt(jnp.mean(h * h, axis=-1, keepdims=True) + EPS)
        gf = nf_ref[...]
        hr = h * r
        e = hr * gf - tgt_ref[...]
        loss_ref[...] += (0.5 / D_MODEL) * jnp.sum(e * e)
        dy = e * (1.0 / D_MODEL)
        gnf_ref[...] += jnp.sum(dy * hr, axis=0, keepdims=True)
        u = dy * gf
        dh = r * (u - hr * jnp.mean(u * hr, axis=-1, keepdims=True))
        dh_ref[...] = dh
        dhb = dh.astype(BF16)
        dmg = _mm_nt(dhb, wout_v)
        gwout_ref[...] += _mm_tn(mg, dhb)
        dgm_ref[:, :D_MODEL] = (dmg * a * ga * (1.0 - ga)).astype(BF16)
        dgm_ref[:, D_MODEL:] = (dmg * p * gp * (1.0 - gp)).astype(BF16)
        dab = (dmg * ga).astype(BF16)
        dpb = (dmg * gp).astype(BF16)
        dya = _mm_nt(dab, wba_v)
        gwba_ref[...] += _mm_tn(ya, dab)
        dyp_ref[...] = _mm_nt(dpb, wbp_v)
        gwbp_ref[...] += _mm_tn(yp, dpb)
        do = dya * silu
        dga_ref[...] = (dya * o_v * dsilu).astype(BF16)
        prod = do * o_v
        lo = lax.broadcasted_iota(jnp.int32, (tm, LANES), 1) < VDIM
        for pair in range(HEADS // 2):
            ls = slice(pair * LANES, (pair + 1) * LANES)
            do_p, prod_p = do[:, ls], prod[:, ls]
            dcap_a = jnp.sum(jnp.where(lo, prod_p, 0.0), axis=-1, keepdims=True)
            dcap_b = jnp.sum(jnp.where(lo, 0.0, prod_p), axis=-1, keepdims=True)
            _store_pair_rows(dcapr_ref, pair, jnp.where(lo, dcap_a, dcap_b))
            doop_ref[:, 2 * pair * LANES:(2 * pair + 1) * LANES] = jnp.where(lo, 0.0, pltpu.roll(do_p, VDIM, 1)).astype(BF16)
            doop_ref[:, (2 * pair + 1) * LANES:(2 * pair + 2) * LANES] = jnp.where(lo, 0.0, do_p).astype(BF16)

    ins = (x2, tgt, o, zga, ypool, zgm, wba, wbp, wout, norm_final)
    in_specs = [_row_spec(tm, D_MODEL), _row_spec(tm, D_MODEL), _row_spec(tm, MLA_W), _row_spec(tm, MLA_W), _row_spec(tm, POOL_W),
                _row_spec(tm, 2 * D_MODEL), _full_spec(wba), _full_spec(wbp), _full_spec(wout), _full_spec(norm_final)]
    outs = [SDS((8, LANES), F32), SDS((T, D_MODEL), F32), SDS((T, 2 * D_MODEL), BF16), SDS((T, HW), BF16), SDS((T, MLA_W), BF16),
            SDS((HEADS // 2, 2, T), F32), SDS((T, POOL_W), F32),
            SDS((D_MODEL, D_MODEL), F32), SDS((MLA_W, D_MODEL), F32), SDS((POOL_W, D_MODEL), F32), SDS((1, D_MODEL), F32)]
    out_specs = [_full_spec(outs[0]), _row_spec(tm, D_MODEL), _row_spec(tm, 2 * D_MODEL), _row_spec(tm, HW), _row_spec(tm, MLA_W),
                 pl.BlockSpec((HEADS // 2, 2, tm), lambda i: (0, 0, i)), _row_spec(tm, POOL_W),
                 _full_spec(outs[7]), _full_spec(outs[8]), _full_spec(outs[9]), _full_spec(outs[10])]
    return pl.pallas_call(
        body, name="tail", grid=(T // tm,), in_specs=in_specs, out_specs=out_specs, out_shape=outs,
        compiler_params=_cp(("arbitrary",)),
    )(*ins)


def _attn_bwd(q_att, k_att, v_att, doop, lse_rows, dcap_rows, tq):
    T = q_att.shape[0]
    nq = T // tq

    def body(q_ref, k_ref, v_ref, doop_ref, lse_ref, dcap_ref, dq_ref, dkv_ref, dkr_ref, dq_acc):
        j = pl.program_id(1)
        mask = _diag_mask(tq, True)
        lane = lax.broadcasted_iota(jnp.int32, (tq, LANES), 1)
        ks = [k_ref[:, hs] for hs in _HEAD_LANES]
        vs = [v_ref[:, hs] for hs in _HEAD_LANES]

        @pl.when(j == 0)
        def _():
            dq_acc[...] = jnp.zeros_like(dq_acc)

        def step(i, carry, masked):
            rows = pl.ds(pl.multiple_of(i * tq, tq), tq)
            out = []
            for h, ((dk, dv), kh, vh, hs) in enumerate(zip(carry, ks, vs, _HEAD_LANES)):
                qh = q_ref[rows, hs]
                doop_h = doop_ref[rows, hs]
                pt = jnp.exp2(_mm_nt(kh, qh) - lse_ref[0, h:h + 1, rows])
                if masked:
                    pt = jnp.where(mask, pt, 0.0)
                dv = dv + _mm(pt.astype(BF16), doop_h)
                dpt = _mm_nt(vh, doop_h)
                dst = (pt * (dpt - dcap_ref[0, h:h + 1, rows])).astype(BF16)
                dq_acc[rows, hs] += _mm_tn(dst, kh)
                out.append((dk + _mm(dst, qh), dv))
            return tuple(out)

        zero = jnp.zeros((tq, LANES), F32)
        carry = step(j, ((zero, zero), (zero, zero)), True)
        (dka, dva), (dkb, dvb) = lax.fori_loop(j + 1, nq, functools.partial(step, masked=False), carry)
        dka, dkb = dka * LN2, dkb * LN2
        dkv_ref[:, _HEAD_LANES[0]] = jnp.where(lane < NOPE, dka, dva).astype(BF16)
        dkv_ref[:, _HEAD_LANES[1]] = jnp.where(lane < NOPE, dkb, dvb).astype(BF16)
        dkr_ref[0] = jnp.where((lane >= NOPE) & (lane < NOPE + ROPE), dka + dkb, 0.0)

        @pl.when(j == nq - 1)
        def _():
            dq_ref[...] = (dq_acc[...] * SCALE).astype(BF16)

    kspec = pl.BlockSpec((tq, 2 * LANES), lambda p, j: (j, p))
    qspec = pl.BlockSpec((T, 2 * LANES), lambda p, j: (0, p))
    rspec = pl.BlockSpec((1, 2, T), lambda p, j: (p, 0, 0))
    return pl.pallas_call(
        body, name="attn_bwd", grid=(HEADS // 2, nq),
        in_specs=[qspec, kspec, kspec, qspec, rspec, rspec],
        out_specs=[qspec, kspec, pl.BlockSpec((1, tq, LANES), lambda p, j: (p, j, 0))],
        out_shape=[SDS((T, HW), BF16), SDS((T, HW), BF16), SDS((HEADS // 2, T, LANES), F32)],
        scratch_shapes=[pltpu.VMEM((T, 2 * LANES), F32)],
        compiler_params=_cp(("parallel", "arbitrary")),
    )(q_att, k_att, v_att, doop, lse_rows, dcap_rows)


def _rms_bwd(z, gain, dout):
    r = lax.rsqrt(jnp.mean(z * z, axis=-1, keepdims=True) + EPS)
    zr = z * r
    u = dout * gain
    return r * (u - zr * jnp.mean(u * zr, axis=-1, keepdims=True)), jnp.sum(dout * zr, axis=0, keepdims=True)


def _mla_bwd(dq_att, dkv_nat, dkr4, zfr, q_norm, wuq_pad, kv_norm, wukv, rc, rsa, rsb, tm):
    T = dq_att.shape[0]

    def body(dq_ref, dkv_ref, dkr_ref, zfr_ref, qn_ref, wuq_ref, kvn_ref, wukv_ref, c_ref, sa_ref, sb_ref,
             dfr_ref, gwuq_ref, gwukv_ref, gqn_ref, gkvn_ref):
        @pl.when(pl.program_id(0) == 0)
        def _():
            for ref in (gwuq_ref, gwukv_ref, gqn_ref, gkvn_ref):
                ref[...] = jnp.zeros_like(ref)

        c, sa, sb = c_ref[...], sa_ref[...], sb_ref[...]
        zq, zkv = zfr_ref[:, :Q_RANK], zfr_ref[:, Q_RANK:Q_RANK + KV_RANK]
        qn, kvn = qn_ref[...], kvn_ref[...]
        cq = (zq * lax.rsqrt(jnp.mean(zq * zq, axis=-1, keepdims=True) + EPS) * qn).astype(BF16)
        ckv = (zkv * lax.rsqrt(jnp.mean(zkv * zkv, axis=-1, keepdims=True) + EPS) * kvn).astype(BF16)
        dq = _rope(dq_ref[...].astype(F32), c, sa, sb, -1.0).astype(BF16)
        gwuq_ref[...] += _mm_tn(cq, dq)
        dzq, gqn = _rms_bwd(zq, qn, _mm_nt(dq, wuq_ref[...]))
        gqn_ref[...] += gqn
        dkv = dkv_ref[...]
        gwukv_ref[...] += _mm_tn(ckv, dkv)
        dzkv, gkvn = _rms_bwd(zkv, kvn, _mm_nt(dkv, wukv_ref[...]))
        gkvn_ref[...] += gkvn
        dkr = dkr_ref[0] + dkr_ref[1] + dkr_ref[2] + dkr_ref[3]
        dfr_ref[:, :Q_RANK] = dzq.astype(BF16)
        dfr_ref[:, Q_RANK:Q_RANK + KV_RANK] = dzkv.astype(BF16)
        dfr_ref[:, Q_RANK + KV_RANK:] = _rope(dkr, c, sa, sb, -1.0).astype(BF16)

    ins = (dq_att, dkv_nat, dkr4, zfr, q_norm, wuq_pad, kv_norm, wukv, rc, rsa, rsb)
    in_specs = [_row_spec(tm, HW), _row_spec(tm, HW), pl.BlockSpec((HEADS // 2, tm, LANES), lambda i: (0, i, 0)), _row_spec(tm, FRONT_W),
                _full_spec(q_norm), _full_spec(wuq_pad), _full_spec(kv_norm), _full_spec(wukv),
                _row_spec(tm, LANES), _row_spec(tm, LANES), _row_spec(tm, LANES)]
    outs = [SDS((T, FRONT_W), BF16), SDS((Q_RANK, HW), F32), SDS((KV_RANK, HW), F32), SDS((1, Q_RANK), F32), SDS((1, KV_RANK), F32)]
    out_specs = [_row_spec(tm, FRONT_W)] + [_full_spec(s) for s in outs[1:]]
    return pl.pallas_call(
        body, name="mla_bwd", grid=(T // tm,), in_specs=in_specs, out_specs=out_specs, out_shape=outs,
        compiler_params=_cp(("arbitrary",)),
    )(*ins)


_DZ_COLS = ((GM, ZTOT), (GA, UP), (UP, GP), (GP, GM), (ZQ, GA))


def _in_proj_bwd_x(dzs, x2, dh, norm_in, w_in_pad, tm):
    T = x2.shape[0]

    def body(d0, d1, d2, d3, d4, x_ref, dh_ref, nin_ref, win_ref, gx_ref, gnin_ref):
        @pl.when(pl.program_id(0) == 0)
        def _():
            gnin_ref[...] = jnp.zeros_like(gnin_ref)

        dhn = None
        for ref, (lo, hi) in zip((d0, d1, d2, d3, d4), _DZ_COLS):
            t = _mm(ref[...], win_ref[lo:hi, :])
            dhn = t if dhn is None else dhn + t
        dx, gnin = _rms_bwd(x_ref[...], nin_ref[...], dhn)
        gnin_ref[...] += gnin
        gx_ref[...] = dx + dh_ref[...]

    in_specs = [_row_spec(tm, hi - lo) for lo, hi in _DZ_COLS] + [_row_spec(tm, D_MODEL), _row_spec(tm, D_MODEL),
                                                                  _full_spec(norm_in), _full_spec(w_in_pad)]
    outs = [SDS((T, D_MODEL), F32), SDS((1, D_MODEL), F32)]
    return pl.pallas_call(
        body, name="in_proj_bwd_x", grid=(T // tm,), in_specs=in_specs, out_specs=[_row_spec(tm, D_MODEL), _full_spec(outs[1])],
        out_shape=outs, compiler_params=_cp(("arbitrary",)),
    )(*dzs, x2, dh, norm_in, w_in_pad)


def _in_proj_bwd_w(dzs, hn, tm):
    T = hn.shape[0]

    def body(d0, d1, d2, d3, d4, hn_ref, gw_ref):
        @pl.when(pl.program_id(0) == 0)
        def _():
            gw_ref[...] = jnp.zeros_like(gw_ref)

        hn_v = hn_ref[...]
        for ref, (lo, hi) in zip((d0, d1, d2, d3, d4), _DZ_COLS):
            gw_ref[lo:hi, :] += _mm_tn(ref[...], hn_v)

    in_specs = [_row_spec(tm, hi - lo) for lo, hi in _DZ_COLS] + [_row_spec(tm, D_MODEL)]
    out = SDS((ZTOT, D_MODEL), F32)
    return pl.pallas_call(
        body, name="in_proj_bwd_w", grid=(T // tm,), in_specs=in_specs, out_specs=_full_spec(out), out_shape=out,
        compiler_params=_cp(("arbitrary",)),
    )(*dzs, hn)


def _pad_w_in_t(w_in_t):
    z = functools.partial(jnp.zeros, dtype=w_in_t.dtype)
    cols = w_in_t.shape[1]
    return jnp.concatenate([w_in_t[:ZKR_ORIG], z((NOPE, cols)), w_in_t[ZKR_ORIG:ZKR_ORIG + ROPE], z((LANES - NOPE - ROPE, cols)),
                            w_in_t[ZKR_ORIG + ROPE:]], axis=0)


def _unpad_g_in_t(g):
    return jnp.concatenate([g[:ZKR], g[ZKR + NOPE:ZKR + NOPE + ROPE], g[GA:]], axis=0)


def _local_step(x2, tgt, norm_in, w_in_t, q_norm, w_uq, kv_norm, w_ukv, pool_w, pool_scale, w_ba, w_bp, w_out, norm_final):
    T = x2.shape[0]
    tm = min(512, T)
    tq = min(512, T)
    row = lambda v: v.reshape(1, -1)
    w_in_pad = _pad_w_in_t(w_in_t)
    wuq_pad = jnp.pad(w_uq, ((0, 0), (0, 0), (0, HEAD_PAD - NOPE - ROPE))).reshape(Q_RANK, HW)
    wukv = w_ukv.reshape(KV_RANK, HW)
    rc, rsa, rsb = _rope_tables(T)

    hn, zgm, zga, zup, zgp, zfr, q_att, k_att, v_att = _in_proj(
        x2, row(norm_in), w_in_pad, row(q_norm), wuq_pad, row(kv_norm), wukv, rc, rsa, rsb, tm)
    o, lse_rows = _attn_fwd(q_att, k_att, v_att, tq, 4)
    ypool = _pool_fwd(zup, zgp, pool_w, row(pool_scale))
    loss8, dh, dgm, doop, dga, dcap_rows, dyp, g_wout, g_wba, g_wbp, g_nf = _tail(
        x2, tgt, o, zga, ypool, zgm, w_ba, w_bp, w_out, row(norm_final), min(256, T))
    dup, dgp, g_pool_w, g_pool_scale = _pool_bwd(zup, zgp, dyp, pool_w, row(pool_scale))
    dq_att, dkv_nat, dkr4 = _attn_bwd(q_att, k_att, v_att, doop, lse_rows, dcap_rows, tq)
    dfr, g_wuq_pad, g_wukv, g_qn, g_kvn = _mla_bwd(
        dq_att, dkv_nat, dkr4, zfr, row(q_norm), wuq_pad, row(kv_norm), wukv, rc, rsa, rsb, tm)
    dzs = (dgm, dga, dup, dgp, dfr)
    grad_x, g_nin = _in_proj_bwd_x(dzs, x2, dh, row(norm_in), w_in_pad, tm)
    g_win_pad = _in_proj_bwd_w(dzs, hn, tm)

    grads = dict(
        norm_in=g_nin.reshape(-1), w_in=_unpad_g_in_t(g_win_pad), q_norm=g_qn.reshape(-1),
        w_uq=g_wuq_pad.reshape(Q_RANK, HEADS, HEAD_PAD)[:, :, :NOPE + ROPE], kv_norm=g_kvn.reshape(-1),
        w_ukv=g_wukv.reshape(KV_RANK, HEADS, NOPE + VDIM), pool_w=g_pool_w, pool_scale=g_pool_scale.reshape(-1),
        w_branch_attn=g_wba, w_branch_pool=g_wbp, w_out=g_wout, norm_final=g_nf.reshape(-1))
    return loss8[0, 0], grad_x, grads


MESH_ID = pl.DeviceIdType.MESH
VMEM_SPEC = pl.BlockSpec(memory_space=pltpu.VMEM)
HBM_SPEC = pl.BlockSpec(memory_space=pl.ANY)


def _mesh_pos():
    return lax.axis_index("x"), lax.axis_index("y"), lax.axis_index("c")


def _slot(px, py, pc):
    return 4 * px + 2 * py + pc


def _all_gather_bf16(shards):
    n = len(shards)

    def body(*refs):
        ins, outs, stage = refs[:n], refs[n:2 * n], refs[2 * n:3 * n]
        send_sems, recv_sems, local_sems = refs[3 * n:]
        x, y, c = _mesh_pos()
        me, sibling = (x, y, c), (x, y, 1 - c)
        chips = [(1 - x, y), (x, 1 - y), (1 - x, 1 - y)]

        def copy(a, k, block, to, from_stage=False):
            dst = outs[a].at[_slot(*block)]
            return pltpu.make_async_remote_copy(
                src_ref=stage[a] if from_stage else dst, dst_ref=dst, send_sem=send_sems.at[7 * a + k],
                recv_sem=recv_sems.at[7 * a + k], device_id=to, device_id_type=MESH_ID)

        for a in range(n):
            stage[a][...] = ins[a][...].astype(BF16)
        mine = [pltpu.make_async_copy(stage[a], outs[a].at[_slot(*me)], local_sems.at[a]) for a in range(n)]
        for cp in mine:
            cp.start()
        first = []
        for a in range(n):
            first.append(copy(a, 0, me, sibling, True))
            first += [copy(a, 1 + j, me, (*chip, c), True) for j, chip in enumerate(chips)]
        for cp in first:
            cp.start()
        passed = []
        for a in range(n):
            for j, chip in enumerate(chips):
                copy(a, 1 + j, (*chip, c), me).wait_recv()
                passed.append(copy(a, 4 + j, (*chip, c), sibling))
                passed[-1].start()
        for a in range(n):
            copy(a, 0, sibling, me).wait_recv()
            for j, chip in enumerate(chips):
                copy(a, 4 + j, (*chip, 1 - c), me).wait_recv()
        for cp in first + passed:
            cp.wait_send()
        for cp in mine:
            cp.wait()

    return pl.pallas_call(
        body, name="all_gather_weights",
        in_specs=[VMEM_SPEC] * n, out_specs=[HBM_SPEC] * n,
        out_shape=[SDS((N_DEV,) + s.shape, BF16) for s in shards],
        scratch_shapes=[pltpu.VMEM(s.shape, BF16) for s in shards]
        + [pltpu.SemaphoreType.DMA((7 * n,)), pltpu.SemaphoreType.DMA((7 * n,)), pltpu.SemaphoreType.DMA((n,))],
        compiler_params=_cp(),
    )(*shards)


N_CHIPS = 4


def _reduce_scatter(slabs, packed):
    n = len(slabs)

    def body(*refs):
        slab_refs, packed_ref = refs[:n], refs[n]
        out_refs, ptot_ref = refs[n + 1:2 * n + 1], refs[2 * n + 1]
        own1, land1, send2, land2 = (refs[(2 + k) * n + 2:(3 + k) * n + 2] for k in range(4))
        pland1, pland2, loc_sems, send1_sems, recv1_sems, send2_sems, recv2_sems = refs[6 * n + 2:]
        x, y, c = _mesh_pos()
        sibling = (x, y, 1 - c)

        def chip(r):
            return (1 - x if r & 2 else x, 1 - y if r & 1 else y)

        def remote(src, dst, send_sem, recv_sem, to):
            return pltpu.make_async_remote_copy(src_ref=src, dst_ref=dst, send_sem=send_sem, recv_sem=recv_sem, device_id=to,
                                                device_id_type=MESH_ID)

        stage1 = []
        for a in range(n):
            for r in range(N_CHIPS):
                k = N_CHIPS * a + r
                stage1.append(pltpu.make_async_copy(slab_refs[a].at[_slot(*chip(r), c)], own1[a].at[r], loc_sems.at[k]))
                stage1.append(remote(slab_refs[a].at[_slot(*chip(r), 1 - c)], land1[a].at[r], send1_sems.at[k], recv1_sems.at[k], sibling))
        stage1.append(remote(packed_ref, pland1, send1_sems.at[N_CHIPS * n], recv1_sems.at[N_CHIPS * n], sibling))
        for cp in stage1:
            cp.start()
        for cp in stage1:
            cp.wait()
        for a in range(n):
            out_refs[a][...] = own1[a][0].astype(F32) + land1[a][0].astype(F32)
            for r in range(1, N_CHIPS):
                send2[a][r - 1] = (own1[a][r].astype(F32) + land1[a][r].astype(F32)).astype(BF16)
        pland2[0] = packed_ref[...] + pland1[...]

        stage2 = []
        for a in range(n):
            for r in range(1, N_CHIPS):
                k = (N_CHIPS - 1) * a + r - 1
                stage2.append(remote(send2[a].at[r - 1], land2[a].at[r - 1], send2_sems.at[k], recv2_sems.at[k], (*chip(r), c)))
        for r in range(1, N_CHIPS):
            k = (N_CHIPS - 1) * n + r - 1
            stage2.append(remote(pland2.at[0], pland2.at[r], send2_sems.at[k], recv2_sems.at[k], (*chip(r), c)))
        for cp in stage2:
            cp.start()
        for cp in stage2:
            cp.wait()
        for a in range(n):
            out_refs[a][...] = out_refs[a][...] + ((land2[a][0].astype(F32) + land2[a][1].astype(F32)) + land2[a][2].astype(F32))
        ptot_ref[...] = (pland2[0] + pland2[1]) + (pland2[2] + pland2[3])

    shapes = [s.shape[1:] for s in slabs]
    n1, n2 = N_CHIPS * n + 1, (N_CHIPS - 1) * (n + 1)
    return pl.pallas_call(
        body, name="reduce_scatter_grads",
        in_specs=[HBM_SPEC] * n + [VMEM_SPEC], out_specs=[VMEM_SPEC] * (n + 1),
        out_shape=[SDS(s, F32) for s in shapes] + [SDS(packed.shape, F32)],
        scratch_shapes=[pltpu.VMEM((N_CHIPS,) + s, BF16) for s in shapes] * 2 + [pltpu.VMEM((N_CHIPS - 1,) + s, BF16) for s in shapes] * 2
        + [pltpu.VMEM(packed.shape, F32), pltpu.VMEM((N_CHIPS,) + packed.shape, F32)]
        + [pltpu.SemaphoreType.DMA((N_CHIPS * n,)), pltpu.SemaphoreType.DMA((n1,)), pltpu.SemaphoreType.DMA((n1,)),
           pltpu.SemaphoreType.DMA((n2,)), pltpu.SemaphoreType.DMA((n2,))],
        compiler_params=_cp(),
    )(*slabs, packed)


def _adamw(ws, gs, ms, vs):
    n = len(ws)

    def body(*refs):
        for k in range(n):
            w, g, m, v = (refs[j * n + k][...] for j in range(4))
            d_ref, nm_ref, nv_ref = (refs[(4 + j) * n + k] for j in range(3))
            m = ADAM_B1 * m + (1.0 - ADAM_B1) * g
            v = ADAM_B2 * v + (1.0 - ADAM_B2) * jnp.square(g)
            m_hat = m / (1.0 - ADAM_B1 ** ADAM_STEP)
            v_hat = v / (1.0 - ADAM_B2 ** ADAM_STEP)
            d_ref[...] = -ADAM_LR * (m_hat / (jnp.sqrt(v_hat) + ADAM_EPS) + ADAM_WD * w)
            nm_ref[...] = m
            nv_ref[...] = v

    outs = pl.pallas_call(
        body, name="adamw", in_specs=[VMEM_SPEC] * (4 * n), out_specs=[VMEM_SPEC] * (3 * n),
        out_shape=[SDS(w.shape, F32) for w in ws] * 3, compiler_params=_cp(),
    )(*ws, *gs, *ms, *vs)
    return outs[:n], outs[n:2 * n], outs[2 * n:]


WEIGHTS = ("norm_in", "w_in", "q_norm", "w_uq", "kv_norm", "w_ukv", "pool_w", "pool_scale", "w_branch_attn", "w_branch_pool",
           "w_out", "norm_final")
SHARDED = ("w_in", "w_branch_attn", "w_branch_pool", "w_out", "w_uq", "w_ukv")
REPLICATED = ("norm_in", "q_norm", "kv_norm", "pool_scale", "norm_final", "pool_w")
SUBLANES = 8


def _cols_to_slabs(g):
    r = g.shape[0]
    return g.reshape(r, N_DEV, -1).transpose(1, 0, 2)


def _slabs_to_cols(s):
    return s.transpose(1, 0, 2).reshape(s.shape[1], -1)


def _pack_rows(a):
    a = a.reshape(-1, LANES)
    return jnp.pad(a, ((0, -a.shape[0] % SUBLANES), (0, 0)))


def kernel(x, norm_in, w_in, q_norm, w_uq, kv_norm, w_ukv, pool_w, pool_scale, w_branch_attn, w_branch_pool, w_out, norm_final, loss_target, m_norm_in, m_w_in, m_q_norm, m_w_uq, m_kv_norm, m_w_ukv, m_pool_w, m_pool_scale, m_w_branch_attn, m_w_branch_pool, m_w_out, m_norm_final, v_norm_in, v_w_in, v_q_norm, v_w_uq, v_kv_norm, v_w_ukv, v_pool_w, v_pool_scale, v_w_branch_attn, v_w_branch_pool, v_w_out, v_norm_final):
    w = dict(norm_in=norm_in, w_in=w_in, q_norm=q_norm, w_uq=w_uq, kv_norm=kv_norm, w_ukv=w_ukv, pool_w=pool_w, pool_scale=pool_scale,
             w_branch_attn=w_branch_attn, w_branch_pool=w_branch_pool, w_out=w_out, norm_final=norm_final)
    m = dict(norm_in=m_norm_in, w_in=m_w_in, q_norm=m_q_norm, w_uq=m_w_uq, kv_norm=m_kv_norm, w_ukv=m_w_ukv, pool_w=m_pool_w,
             pool_scale=m_pool_scale, w_branch_attn=m_w_branch_attn, w_branch_pool=m_w_branch_pool, w_out=m_w_out, norm_final=m_norm_final)
    v = dict(norm_in=v_norm_in, w_in=v_w_in, q_norm=v_q_norm, w_uq=v_w_uq, kv_norm=v_kv_norm, w_ukv=v_w_ukv, pool_w=v_pool_w,
             pool_scale=v_pool_scale, w_branch_attn=v_w_branch_attn, w_branch_pool=v_w_branch_pool, w_out=v_w_out, norm_final=v_norm_final)

    def as2d(name, a):
        if name == "w_in":
            return a.T
        if name in ("w_uq", "w_ukv"):
            return a.reshape(a.shape[0], -1)
        if name == "pool_w":
            return a.reshape(-1, GROUP)
        return a.reshape(1, -1) if a.ndim == 1 else a

    def unshape(name, a):
        return a.T if name == "w_in" else a.reshape(w[name].shape)

    full = dict(zip(SHARDED, _all_gather_bf16([as2d(k, w[k]) for k in SHARDED])))
    loss_part, grad_x, grads = _local_step(
        x.reshape(x.shape[1:]), loss_target.reshape(x.shape[1:]), norm_in, full["w_in"].reshape(IN_TOTAL, D_MODEL), q_norm,
        full["w_uq"].reshape(Q_RANK, HEADS, NOPE + ROPE), kv_norm, full["w_ukv"].reshape(KV_RANK, HEADS, NOPE + VDIM),
        pool_w, pool_scale, _slabs_to_cols(full["w_branch_attn"]), _slabs_to_cols(full["w_branch_pool"]),
        full["w_out"].reshape(D_MODEL, D_MODEL), norm_final)

    slabs = [
        grads["w_in"].reshape(N_DEV, IN_TOTAL // N_DEV, D_MODEL), _cols_to_slabs(grads["w_branch_attn"]),
        _cols_to_slabs(grads["w_branch_pool"]),
        grads["w_out"].reshape(N_DEV, D_MODEL // N_DEV, D_MODEL), grads["w_uq"].reshape(N_DEV, Q_RANK // N_DEV, -1),
        grads["w_ukv"].reshape(N_DEV, KV_RANK // N_DEV, -1)]
    parts = [_pack_rows(grads[k]) for k in REPLICATED]
    parts.append(_pack_rows(jnp.zeros((LANES,), F32).at[0].set(loss_part)))
    *sums, ptot = _reduce_scatter([s.astype(BF16) for s in slabs], jnp.concatenate(parts, axis=0))
    g2d = dict(zip(SHARDED, sums))
    row = 0
    for k, part in zip(REPLICATED, parts):
        g2d[k] = as2d(k, ptot[row:row + w[k].size // LANES].reshape(w[k].shape))
        row += part.shape[0]
    loss = ptot[row, 0]

    deltas, new_m, new_v = _adamw([as2d(k, w[k]) for k in WEIGHTS], [g2d[k] for k in WEIGHTS],
                                  [as2d(k, m[k]) for k in WEIGHTS], [as2d(k, v[k]) for k in WEIGHTS])
    shaped = lambda arrs: [unshape(k, a) for k, a in zip(WEIGHTS, arrs)]
    return (loss, grad_x.reshape(x.shape), *shaped([g2d[k] for k in WEIGHTS]), *shaped(deltas), *shaped(new_m), *shaped(new_v))
```

```python
import functools

import jax
import jax.numpy as jnp
import numpy as np
from jax import lax
from jax.experimental import pallas as pl
from jax.experimental.pallas import tpu as pltpu

F32 = jnp.float32
BF16 = jnp.bfloat16
SDS = jax.ShapeDtypeStruct

D_MODEL = 1024
HEADS = 8
NOPE = 64
ROPE = 32
VDIM = 64
Q_RANK = 384
KV_RANK = 256
MLA_W = HEADS * VDIM
POOL_W = 512
POOL_GROUPS = 4
GROUP = POOL_W // POOL_GROUPS
CHUNK = 64
ROPE_THETA = 10000.0
EPS = 1e-6
SCALE = (NOPE + ROPE) ** -0.5
LOG2E = 1.4426950408889634
LN2 = 0.6931471805599453
QK_SCALE_LOG2 = SCALE * LOG2E
IN_TOTAL = 4256
ADAM_LR, ADAM_B1, ADAM_B2, ADAM_EPS, ADAM_WD, ADAM_STEP = 0.001, 0.9, 0.999, 1e-08, 0.01, 10

N_DEV = 8
LANES = 128
HEAD_PAD = LANES
HW = HEADS * HEAD_PAD

ZQ, ZKV, ZKR, GA, UP, GP, GM, ZTOT = 0, 384, 640, 768, 1280, 1792, 2304, 4352
FRONT_W = GA
ZKR_ORIG = 640

VMEM_LIMIT = 56 * 1024 * 1024


def _cp(sem=None, **kw):
    if sem is not None:
        kw["dimension_semantics"] = sem
    return pltpu.CompilerParams(vmem_limit_bytes=VMEM_LIMIT, **kw)


def _mm(a, b):
    return lax.dot_general(a, b, (((1,), (0,)), ((), ())), preferred_element_type=F32)


def _mm_nt(a, b):
    return lax.dot_general(a, b, (((1,), (1,)), ((), ())), preferred_element_type=F32)


def _mm_tn(a, b):
    return lax.dot_general(a, b, (((0,), (0,)), ((), ())), preferred_element_type=F32)


def _row_spec(tm, w):
    return pl.BlockSpec((tm, w), lambda i: (i, 0))


def _full_spec(a):
    nd = len(a.shape)
    return pl.BlockSpec(a.shape, lambda *_: (0,) * nd)


def _rope(v, c, sa, sb, sign):
    n = v.shape[-1]
    reps = n // LANES
    if reps > 1:
        c, sa, sb = (jnp.tile(t, (1, reps)) for t in (c, sa, sb))
    up = pltpu.roll(v, n - ROPE // 2, 1)
    dn = pltpu.roll(v, ROPE // 2, 1)
    return v * c + sign * (up * sa + dn * sb)


def _rope_tables(T):
    half = ROPE // 2
    inv_freq = np.float32(ROPE_THETA) ** (-np.arange(half, dtype=np.float32) / np.float32(half))
    ang = np.arange(T, dtype=np.float32)[:, None] * inv_freq[None, :].astype(np.float32)
    cos, sin = np.cos(ang.astype(np.float64)).astype(np.float32), np.sin(ang.astype(np.float64)).astype(np.float32)
    z16 = np.zeros((T, half), np.float32)
    z32 = np.zeros((T, LANES - NOPE - ROPE), np.float32)
    c = np.concatenate([np.ones((T, NOPE), np.float32), cos, cos, z32], axis=1)
    sa = np.concatenate([np.zeros((T, NOPE), np.float32), -sin, z16, z32], axis=1)
    sb = np.concatenate([np.zeros((T, NOPE), np.float32), z16, sin, z32], axis=1)
    return jnp.asarray(c), jnp.asarray(sa), jnp.asarray(sb)


def _silu_parts(g):
    sg = jax.nn.sigmoid(g)
    return g * sg, sg + g * sg * (1.0 - sg)


def _in_proj(x2, norm_in, w_in_pad, q_norm, wuq_pad, kv_norm, wukv, rc, rsa, rsb, tm):
    T = x2.shape[0]

    def body(x_ref, nin_ref, win_ref, qn_ref, wuq_ref, kvn_ref, wukv_ref, c_ref, sa_ref, sb_ref,
             hn_ref, zgm_ref, zga_ref, zup_ref, zgp_ref, zfr_ref, q_ref, k_ref, v_ref):
        xf = x_ref[...]
        r = lax.rsqrt(jnp.mean(xf * xf, axis=-1, keepdims=True) + EPS)
        hn = (xf * r * nin_ref[...]).astype(BF16)
        hn_ref[...] = hn
        z = _mm_nt(hn, win_ref[...])
        zgm_ref[...] = z[:, GM:ZTOT]
        zga_ref[...] = z[:, GA:UP]
        zup_ref[...] = z[:, UP:GP]
        zgp_ref[...] = z[:, GP:GM]
        zfr_ref[...] = z[:, ZQ:GA]
        zq, zkv, zkr = z[:, ZQ:ZKV], z[:, ZKV:ZKR], z[:, ZKR:GA]
        c, sa, sb = c_ref[...], sa_ref[...], sb_ref[...]
        rq = lax.rsqrt(jnp.mean(zq * zq, axis=-1, keepdims=True) + EPS)
        cq = (zq * rq * qn_ref[...]).astype(BF16)
        q = _rope(_mm(cq, wuq_ref[...]), c, sa, sb, 1.0)
        q_ref[...] = (q * QK_SCALE_LOG2).astype(BF16)
        rkv = lax.rsqrt(jnp.mean(zkv * zkv, axis=-1, keepdims=True) + EPS)
        ckv = (zkv * rkv * kvn_ref[...]).astype(BF16)
        kv = _mm(ckv, wukv_ref[...])
        kr = _rope(zkr, c, sa, sb, 1.0)
        lane = lax.broadcasted_iota(jnp.int32, kv.shape, 1) % LANES
        k_ref[...] = jnp.where(lane < NOPE, kv, jnp.tile(kr, (1, HEADS))).astype(BF16)
        v_ref[...] = jnp.where(lane < NOPE, 1.0, kv).astype(BF16)

    ins = (x2, norm_in, w_in_pad, q_norm, wuq_pad, kv_norm, wukv, rc, rsa, rsb)
    in_specs = [_row_spec(tm, D_MODEL), _full_spec(norm_in), _full_spec(w_in_pad), _full_spec(q_norm), _full_spec(wuq_pad),
                _full_spec(kv_norm), _full_spec(wukv), _row_spec(tm, LANES), _row_spec(tm, LANES), _row_spec(tm, LANES)]
    widths = [(D_MODEL, BF16), (ZTOT - GM, F32), (UP - GA, F32), (GP - UP, F32), (GM - GP, F32), (FRONT_W, F32),
              (HW, BF16), (HW, BF16), (HW, BF16)]
    return pl.pallas_call(
        body, name="in_proj", grid=(T // tm,), in_specs=in_specs,
        out_specs=[_row_spec(tm, w) for w, _ in widths],
        out_shape=[SDS((T, w), dt) for w, dt in widths],
        compiler_params=_cp(("parallel",)),
    )(*ins)


def _diag_mask(tq, transposed):
    r = lax.broadcasted_iota(jnp.int32, (tq, tq), 0) // CHUNK
    c = lax.broadcasted_iota(jnp.int32, (tq, tq), 1) // CHUNK
    return (r <= c) if transposed else (c <= r)


_HEAD_LANES = (slice(0, LANES), slice(LANES, 2 * LANES))


def _pair_rows_spec(tq):
    return pl.BlockSpec((1, 2, tq), lambda p, i: (p, 0, i))


def _store_pair_rows(ref, k, pair):
    t = pair.T
    ref[k, 0:1, :] = t[0:1, :]
    ref[k, 1:2, :] = t[VDIM:VDIM + 1, :]


def _attn_fwd(q_att, k_att, v_att, tq, hps):
    T = q_att.shape[0]
    head_lanes = [slice(h * LANES, (h + 1) * LANES) for h in range(hps)]

    def body(q_ref, k_ref, v_ref, o_ref, lser_ref):
        i = pl.program_id(1)
        mask = _diag_mask(tq, False)
        lane = lax.broadcasted_iota(jnp.int32, (tq, LANES), 1)
        qs = [q_ref[:, hs] for hs in head_lanes]

        def step(j, carry, masked):
            rows = pl.ds(pl.multiple_of(j * tq, tq), tq)
            out = []
            for (m, acc), qh, hs in zip(carry, qs, head_lanes):
                s = _mm_nt(qh, k_ref[rows, hs])
                if masked:
                    s = jnp.where(mask, s, -jnp.inf)
                m_new = jnp.maximum(m, jnp.max(s, axis=-1, keepdims=True))
                p = jnp.exp2(s - m_new).astype(BF16)
                out.append((m_new, jnp.exp2(m - m_new) * acc + _mm(p, v_ref[rows, hs])))
            return tuple(out)

        init = ((jnp.full((tq, 1), -jnp.inf, F32), jnp.zeros((tq, LANES), F32)),) * hps
        res = step(i, lax.fori_loop(0, i, functools.partial(step, masked=False), init), True)
        for pair in range(hps // 2):
            (ma, acca), (mb, accb) = res[2 * pair], res[2 * pair + 1]
            la, lb = acca[:, :1], accb[:, :1]
            o_ref[:, pair * LANES:(pair + 1) * LANES] = jnp.where(lane < VDIM, pltpu.roll(acca / la, VDIM, 1), accb / lb)
            _store_pair_rows(lser_ref, pair, jnp.where(lane < VDIM, ma + jnp.log2(la), mb + jnp.log2(lb)))

    qspec = pl.BlockSpec((tq, hps * LANES), lambda p, i: (i, p))
    kspec = pl.BlockSpec((T, hps * LANES), lambda p, i: (0, p))
    ospec = pl.BlockSpec((tq, hps * VDIM), lambda p, i: (i, p))
    return pl.pallas_call(
        body, name="attn_fwd", grid=(HEADS // hps, T // tq), in_specs=[qspec, kspec, kspec],
        out_specs=[ospec, pl.BlockSpec((hps // 2, 2, tq), lambda p, i: (p, 0, i))],
        out_shape=[SDS((T, MLA_W), F32), SDS((HEADS // 2, 2, T), F32)],
        compiler_params=_cp(("parallel", "parallel")),
    )(q_att, k_att, v_att)


def _pick(g, vals):
    out = vals[-1]
    for k in range(len(vals) - 2, -1, -1):
        out = jnp.where(g == k, vals[k], out)
    return out


def _window_sum(u, g, forward):
    T = u.shape[0]
    row = lax.broadcasted_iota(jnp.int32, u.shape, 0)

    def sh(s, k):
        if forward:
            return jnp.where(row >= k, pltpu.roll(s, k, 0), 0.0)
        return jnp.where(row < T - k, pltpu.roll(s, T - k, 0), 0.0)

    sums, s = [], u
    for k in (1, 2, 4, 8):
        s = s + sh(s, k)
        sums.append(s)
    return _pick(g, sums)


def _pool_count(shape, g):
    row = lax.broadcasted_iota(jnp.int32, shape, 0)
    return jnp.minimum(row + 1, lax.shift_left(jnp.int32(2), g)).astype(F32)


def _pool_fwd(zup, zgp, pool_w, pool_scale):
    T = zup.shape[0]

    def body(u_ref, g_ref, w_ref, sc_ref, y_ref):
        g = pl.program_id(0)
        u = u_ref[...]
        d = _window_sum(u, g, True) / _pool_count(u.shape, g) - u
        lin = _mm(d.astype(BF16), w_ref[0].astype(BF16))
        silu, _ = _silu_parts(g_ref[...])
        y_ref[...] = (lin * sc_ref[...] * silu).astype(BF16)

    col = pl.BlockSpec((T, GROUP), lambda g: (0, g))
    return pl.pallas_call(
        body, name="pool_fwd", grid=(POOL_GROUPS,),
        in_specs=[col, col, pl.BlockSpec((1, GROUP, GROUP), lambda g: (g, 0, 0)), pl.BlockSpec((1, GROUP), lambda g: (0, g))],
        out_specs=col, out_shape=SDS((T, POOL_W), BF16), compiler_params=_cp(("parallel",)),
    )(zup, zgp, pool_w, pool_scale)


def _pool_bwd(zup, zgp, dyp, pool_w, pool_scale):
    T = zup.shape[0]

    def body(u_ref, g_ref, dy_ref, w_ref, sc_ref, du_ref, dg_ref, gw_ref, gsc_ref):
        g = pl.program_id(0)
        u = u_ref[...]
        cnt = _pool_count(u.shape, g)
        d = (_window_sum(u, g, True) / cnt - u).astype(BF16)
        wb = w_ref[0].astype(BF16)
        lin = _mm(d, wb)
        sc = sc_ref[...]
        silu, dsilu = _silu_parts(g_ref[...])
        dy = dy_ref[...]
        dg_ref[...] = (dy * lin * sc * dsilu).astype(BF16)
        dpre = dy * silu
        gsc_ref[...] = jnp.sum(dpre * lin, axis=0, keepdims=True)
        dlin = (dpre * sc).astype(BF16)
        gw_ref[0] = _mm_tn(d, dlin)
        dd = _mm_nt(dlin, wb)
        du_ref[...] = (_window_sum(dd / cnt, g, False) - dd).astype(BF16)

    col = pl.BlockSpec((T, GROUP), lambda g: (0, g))
    wspec = pl.BlockSpec((1, GROUP, GROUP), lambda g: (g, 0, 0))
    vspec = pl.BlockSpec((1, GROUP), lambda g: (0, g))
    return pl.pallas_call(
        body, name="pool_bwd", grid=(POOL_GROUPS,), in_specs=[col, col, col, wspec, vspec], out_specs=[col, col, wspec, vspec],
        out_shape=[SDS((T, POOL_W), BF16), SDS((T, POOL_W), BF16), SDS((POOL_GROUPS, GROUP, GROUP), F32), SDS((1, POOL_W), F32)],
        compiler_params=_cp(("parallel",)),
    )(zup, zgp, dyp, pool_w, pool_scale)


def _tail(x2, tgt, o, zga, ypool, zgm, wba, wbp, wout, norm_final, tm):
    T = x2.shape[0]

    def body(x_ref, tgt_ref, o_ref, zga_ref, yp_ref, zgm_ref, wba_ref, wbp_ref, wout_ref, nf_ref,
             loss_ref, dh_ref, dgm_ref, doop_ref, dga_ref, dcapr_ref, dyp_ref, gwout_ref, gwba_ref, gwbp_ref, gnf_ref):
        @pl.when(pl.program_id(0) == 0)
        def _():
            for ref in (loss_ref, gwout_ref, gwba_ref, gwbp_ref, gnf_ref):
                ref[...] = jnp.zeros_like(ref)

        o_v = o_ref[...]
        silu, dsilu = _silu_parts(zga_ref[...])
        ya = (o_v * silu).astype(BF16)
        yp = yp_ref[...]
        wba_v, wbp_v, wout_v = wba_ref[...], wbp_ref[...], wout_ref[...]
        a = _mm(ya, wba_v)
        p = _mm(yp, wbp_v)
        gate = jax.nn.sigmoid(zgm_ref[...])
        ga, gp = gate[:, :D_MODEL], gate[:, D_MODEL:]
        mg = (ga * a + gp * p).astype(BF16)
        h = x_ref[...] + _mm(mg, wout_v)
        r = lax.rsqrt(jnp.mean(h * h, axis=-1, keepdims=True) + EPS)
        gf = nf_ref[...]
        hr = h * r
        e = hr * gf - tgt_ref[...]
        loss_ref[...] += (0.5 / D_MODEL) * jnp.sum(e * e)
        dy = e * (1.0 / D_MODEL)
        gnf_ref[...] += jnp.sum(dy * hr, axis=0, keepdims=True)
        u = dy * gf
        dh = r * (u - hr * jnp.mean(u * hr, axis=-1, keepdims=True))
        dh_ref[...] = dh
        dhb = dh.astype(BF16)
        dmg = _mm_nt(dhb, wout_v)
        gwout_ref[...] += _mm_tn(mg, dhb)
        dgm_ref[:, :D_MODEL] = (dmg * a * ga * (1.0 - ga)).astype(BF16)
        dgm_ref[:, D_MODEL:] = (dmg * p * gp * (1.0 - gp)).astype(BF16)
        dab = (dmg * ga).astype(BF16)
        dpb = (dmg * gp).astype(BF16)
        dya = _mm_nt(dab, wba_v)
        gwba_ref[...] += _mm_tn(ya, dab)
        dyp_ref[...] = _mm_nt(dpb, wbp_v)
        gwbp_ref[...] += _mm_tn(yp, dpb)
        do = dya * silu
        dga_ref[...] = (dya * o_v * dsilu).astype(BF16)
        prod = do * o_v
        lo = lax.broadcasted_iota(jnp.int32, (tm, LANES), 1) < VDIM
        for pair in range(HEADS // 2):
            ls = slice(pair * LANES, (pair + 1) * LANES)
            do_p, prod_p = do[:, ls], prod[:, ls]
            dcap_a = jnp.sum(jnp.where(lo, prod_p, 0.0), axis=-1, keepdims=True)
            dcap_b = jnp.sum(jnp.where(lo, 0.0, prod_p), axis=-1, keepdims=True)
            _store_pair_rows(dcapr_ref, pair, jnp.where(lo, dcap_a, dcap_b))
            doop_ref[:, 2 * pair * LANES:(2 * pair + 1) * LANES] = jnp.where(lo, 0.0, pltpu.roll(do_p, VDIM, 1)).astype(BF16)
            doop_ref[:, (2 * pair + 1) * LANES:(2 * pair + 2) * LANES] = jnp.where(lo, 0.0, do_p).astype(BF16)

    ins = (x2, tgt, o, zga, ypool, zgm, wba, wbp, wout, norm_final)
    in_specs = [_row_spec(tm, D_MODEL), _row_spec(tm, D_MODEL), _row_spec(tm, MLA_W), _row_spec(tm, MLA_W), _row_spec(tm, POOL_W),
                _row_spec(tm, 2 * D_MODEL), _full_spec(wba), _full_spec(wbp), _full_spec(wout), _full_spec(norm_final)]
    outs = [SDS((8, LANES), F32), SDS((T, D_MODEL), F32), SDS((T, 2 * D_MODEL), BF16), SDS((T, HW), BF16), SDS((T, MLA_W), BF16),
            SDS((HEADS // 2, 2, T), F32), SDS((T, POOL_W), F32),
            SDS((D_MODEL, D_MODEL), F32), SDS((MLA_W, D_MODEL), F32), SDS((POOL_W, D_MODEL), F32), SDS((1, D_MODEL), F32)]
    out_specs = [_full_spec(outs[0]), _row_spec(tm, D_MODEL), _row_spec(tm, 2 * D_MODEL), _row_spec(tm, HW), _row_spec(tm, MLA_W),
                 pl.BlockSpec((HEADS // 2, 2, tm), lambda i: (0, 0, i)), _row_spec(tm, POOL_W),
                 _full_spec(outs[7]), _full_spec(outs[8]), _full_spec(outs[9]), _full_spec(outs[10])]
    return pl.pallas_call(
        body, name="tail", grid=(T // tm,), in_specs=in_specs, out_specs=out_specs, out_shape=outs,
        compiler_params=_cp(("arbitrary",)),
    )(*ins)


def _attn_bwd(q_att, k_att, v_att, doop, lse_rows, dcap_rows, tq, slabs, packed):
    T = q_att.shape[0]
    nq = T // tq
    n = len(slabs)
    n_pairs = HEADS // 2

    def body(q_ref, k_ref, v_ref, doop_ref, lse_ref, dcap_ref, *rest):
        slab_refs, packed_ref = rest[:n], rest[n]
        dq_ref, dkv_ref, dkr_ref = rest[n + 1:n + 4]
        sum_refs, ptot_ref = rest[n + 4:2 * n + 4], rest[2 * n + 4]
        dq_acc = rest[2 * n + 5]
        rs = _ReduceScatter(slab_refs, packed_ref, sum_refs, ptot_ref, rest[2 * n + 6:])
        pair, j = pl.program_id(0), pl.program_id(1)
        pl.when((pair == 0) & (j == 0))(rs.start1)
        pl.when((pair == 1) & (j == 0))(rs.finish1_start2)
        mask = _diag_mask(tq, True)
        lane = lax.broadcasted_iota(jnp.int32, (tq, LANES), 1)
        ks = [k_ref[:, hs] for hs in _HEAD_LANES]
        vs = [v_ref[:, hs] for hs in _HEAD_LANES]

        @pl.when(j == 0)
        def _():
            dq_acc[...] = jnp.zeros_like(dq_acc)

        def step(i, carry, masked):
            rows = pl.ds(pl.multiple_of(i * tq, tq), tq)
            out = []
            for h, ((dk, dv), kh, vh, hs) in enumerate(zip(carry, ks, vs, _HEAD_LANES)):
                qh = q_ref[rows, hs]
                doop_h = doop_ref[rows, hs]
                pt = jnp.exp2(_mm_nt(kh, qh) - lse_ref[0, h:h + 1, rows])
                if masked:
                    pt = jnp.where(mask, pt, 0.0)
                dv = dv + _mm(pt.astype(BF16), doop_h)
                dpt = _mm_nt(vh, doop_h)
                dst = (pt * (dpt - dcap_ref[0, h:h + 1, rows])).astype(BF16)
                dq_acc[rows, hs] += _mm_tn(dst, kh)
                out.append((dk + _mm(dst, qh), dv))
            return tuple(out)

        zero = jnp.zeros((tq, LANES), F32)
        carry = step(j, ((zero, zero), (zero, zero)), True)
        (dka, dva), (dkb, dvb) = lax.fori_loop(j + 1, nq, functools.partial(step, masked=False), carry)
        dka, dkb = dka * LN2, dkb * LN2
        dkv_ref[:, _HEAD_LANES[0]] = jnp.where(lane < NOPE, dka, dva).astype(BF16)
        dkv_ref[:, _HEAD_LANES[1]] = jnp.where(lane < NOPE, dkb, dvb).astype(BF16)
        dkr_ref[0] = jnp.where((lane >= NOPE) & (lane < NOPE + ROPE), dka + dkb, 0.0)

        @pl.when(j == nq - 1)
        def _():
            dq_ref[...] = (dq_acc[...] * SCALE).astype(BF16)

        pl.when((pair == n_pairs - 1) & (j == nq - 1))(rs.finish2)

    kspec = pl.BlockSpec((tq, 2 * LANES), lambda p, j: (j, p))
    qspec = pl.BlockSpec((T, 2 * LANES), lambda p, j: (0, p))
    rspec = pl.BlockSpec((1, 2, T), lambda p, j: (p, 0, 0))
    sums = [SDS(s.shape[1:], F32) for s in slabs] + [SDS(packed.shape, F32)]
    return pl.pallas_call(
        body, name="attn_bwd", grid=(n_pairs, nq),
        in_specs=[qspec, kspec, kspec, qspec, rspec, rspec] + [HBM_SPEC] * n + [_full_spec(packed)],
        out_specs=[qspec, kspec, pl.BlockSpec((1, tq, LANES), lambda p, j: (p, j, 0))] + [_full_spec(s) for s in sums],
        out_shape=[SDS((T, HW), BF16), SDS((T, HW), BF16), SDS((n_pairs, T, LANES), F32)] + sums,
        scratch_shapes=[pltpu.VMEM((T, 2 * LANES), F32)] + _rs_scratch([s.shape for s in sums[:-1]], packed.shape),
        compiler_params=_cp(("arbitrary", "arbitrary")),
    )(q_att, k_att, v_att, doop, lse_rows, dcap_rows, *slabs, packed)


def _rms_bwd(z, gain, dout):
    r = lax.rsqrt(jnp.mean(z * z, axis=-1, keepdims=True) + EPS)
    zr = z * r
    u = dout * gain
    return r * (u - zr * jnp.mean(u * zr, axis=-1, keepdims=True)), jnp.sum(dout * zr, axis=0, keepdims=True)


def _mla_bwd(dq_att, dkv_nat, dkr4, zfr, q_norm, wuq_pad, kv_norm, wukv, rc, rsa, rsb, tm):
    T = dq_att.shape[0]

    def body(dq_ref, dkv_ref, dkr_ref, zfr_ref, qn_ref, wuq_ref, kvn_ref, wukv_ref, c_ref, sa_ref, sb_ref,
             dfr_ref, gwuq_ref, gwukv_ref, gqn_ref, gkvn_ref):
        @pl.when(pl.program_id(0) == 0)
        def _():
            for ref in (gwuq_ref, gwukv_ref, gqn_ref, gkvn_ref):
                ref[...] = jnp.zeros_like(ref)

        c, sa, sb = c_ref[...], sa_ref[...], sb_ref[...]
        zq, zkv = zfr_ref[:, :Q_RANK], zfr_ref[:, Q_RANK:Q_RANK + KV_RANK]
        qn, kvn = qn_ref[...], kvn_ref[...]
        cq = (zq * lax.rsqrt(jnp.mean(zq * zq, axis=-1, keepdims=True) + EPS) * qn).astype(BF16)
        ckv = (zkv * lax.rsqrt(jnp.mean(zkv * zkv, axis=-1, keepdims=True) + EPS) * kvn).astype(BF16)
        dq = _rope(dq_ref[...].astype(F32), c, sa, sb, -1.0).astype(BF16)
        gwuq_ref[...] += _mm_tn(cq, dq)
        dzq, gqn = _rms_bwd(zq, qn, _mm_nt(dq, wuq_ref[...]))
        gqn_ref[...] += gqn
        dkv = dkv_ref[...]
        gwukv_ref[...] += _mm_tn(ckv, dkv)
        dzkv, gkvn = _rms_bwd(zkv, kvn, _mm_nt(dkv, wukv_ref[...]))
        gkvn_ref[...] += gkvn
        dkr = dkr_ref[0] + dkr_ref[1] + dkr_ref[2] + dkr_ref[3]
        dfr_ref[:, :Q_RANK] = dzq.astype(BF16)
        dfr_ref[:, Q_RANK:Q_RANK + KV_RANK] = dzkv.astype(BF16)
        dfr_ref[:, Q_RANK + KV_RANK:] = _rope(dkr, c, sa, sb, -1.0).astype(BF16)

    ins = (dq_att, dkv_nat, dkr4, zfr, q_norm, wuq_pad, kv_norm, wukv, rc, rsa, rsb)
    in_specs = [_row_spec(tm, HW), _row_spec(tm, HW), pl.BlockSpec((HEADS // 2, tm, LANES), lambda i: (0, i, 0)), _row_spec(tm, FRONT_W),
                _full_spec(q_norm), _full_spec(wuq_pad), _full_spec(kv_norm), _full_spec(wukv),
                _row_spec(tm, LANES), _row_spec(tm, LANES), _row_spec(tm, LANES)]
    outs = [SDS((T, FRONT_W), BF16), SDS((Q_RANK, HW), F32), SDS((KV_RANK, HW), F32), SDS((1, Q_RANK), F32), SDS((1, KV_RANK), F32)]
    out_specs = [_row_spec(tm, FRONT_W)] + [_full_spec(s) for s in outs[1:]]
    return pl.pallas_call(
        body, name="mla_bwd", grid=(T // tm,), in_specs=in_specs, out_specs=out_specs, out_shape=outs,
        compiler_params=_cp(("arbitrary",)),
    )(*ins)


_DZ_COLS = ((GM, ZTOT), (GA, UP), (UP, GP), (GP, GM), (ZQ, GA))


def _in_proj_bwd_x(dzs, x2, dh, norm_in, w_in_pad, tm, slabs):
    T = x2.shape[0]
    steps = T // tm
    n = len(slabs)

    def body(d0, d1, d2, d3, d4, x_ref, dh_ref, nin_ref, win_ref, *rest):
        slab_refs, (gx_ref, gnin_ref), sum_refs = rest[:n], rest[n:n + 2], rest[n + 2:2 * n + 2]
        rs = _ReduceScatter(slab_refs, None, sum_refs, None, rest[2 * n + 2:])
        step = pl.program_id(0)

        @pl.when(step == 0)
        def _():
            gnin_ref[...] = jnp.zeros_like(gnin_ref)
            rs.start1()

        pl.when(step == min(2, steps - 1))(rs.finish1_start2)
        dhn = None
        for ref, (lo, hi) in zip((d0, d1, d2, d3, d4), _DZ_COLS):
            t = _mm(ref[...], win_ref[lo:hi, :])
            dhn = t if dhn is None else dhn + t
        dx, gnin = _rms_bwd(x_ref[...], nin_ref[...], dhn)
        gnin_ref[...] += gnin
        gx_ref[...] = dx + dh_ref[...]
        pl.when(step == steps - 1)(rs.finish2)

    in_specs = [_row_spec(tm, hi - lo) for lo, hi in _DZ_COLS] + [_row_spec(tm, D_MODEL), _row_spec(tm, D_MODEL),
                                                                  _full_spec(norm_in), _full_spec(w_in_pad)] + [HBM_SPEC] * n
    sums = [SDS(s.shape[1:], F32) for s in slabs]
    outs = [SDS((T, D_MODEL), F32), SDS((1, D_MODEL), F32)] + sums
    return pl.pallas_call(
        body, name="in_proj_bwd_x", grid=(steps,), in_specs=in_specs,
        out_specs=[_row_spec(tm, D_MODEL), _full_spec(outs[1])] + [_full_spec(s) for s in sums],
        out_shape=outs, scratch_shapes=_rs_scratch([s.shape for s in sums], None), compiler_params=_cp(("arbitrary",)),
    )(*dzs, x2, dh, norm_in, w_in_pad, *slabs)


def _in_proj_bwd_w(dzs, hn, tm):
    T = hn.shape[0]

    def body(d0, d1, d2, d3, d4, hn_ref, gw_ref):
        @pl.when(pl.program_id(0) == 0)
        def _():
            gw_ref[...] = jnp.zeros_like(gw_ref)

        hn_v = hn_ref[...]
        for ref, (lo, hi) in zip((d0, d1, d2, d3, d4), _DZ_COLS):
            gw_ref[lo:hi, :] += _mm_tn(ref[...], hn_v)

    in_specs = [_row_spec(tm, hi - lo) for lo, hi in _DZ_COLS] + [_row_spec(tm, D_MODEL)]
    out = SDS((ZTOT, D_MODEL), F32)
    return pl.pallas_call(
        body, name="in_proj_bwd_w", grid=(T // tm,), in_specs=in_specs, out_specs=_full_spec(out), out_shape=out,
        compiler_params=_cp(("arbitrary",)),
    )(*dzs, hn)


def _pad_w_in_t(w_in_t):
    z = functools.partial(jnp.zeros, dtype=w_in_t.dtype)
    cols = w_in_t.shape[1]
    return jnp.concatenate([w_in_t[:ZKR_ORIG], z((NOPE, cols)), w_in_t[ZKR_ORIG:ZKR_ORIG + ROPE], z((LANES - NOPE - ROPE, cols)),
                            w_in_t[ZKR_ORIG + ROPE:]], axis=0)


def _unpad_g_in_t(g):
    return jnp.concatenate([g[:ZKR], g[ZKR + NOPE:ZKR + NOPE + ROPE], g[GA:]], axis=0)


def _local_step(x2, tgt, norm_in, w_in_t, q_norm, w_uq, kv_norm, w_ukv, pool_w, pool_scale, w_ba, w_bp, w_out, norm_final):
    T = x2.shape[0]
    tm = min(512, T)
    tq = min(512, T)
    row = lambda v: v.reshape(1, -1)
    w_in_pad = _pad_w_in_t(w_in_t)
    wuq_pad = jnp.pad(w_uq, ((0, 0), (0, 0), (0, HEAD_PAD - NOPE - ROPE))).reshape(Q_RANK, HW)
    wukv = w_ukv.reshape(KV_RANK, HW)
    rc, rsa, rsb = _rope_tables(T)

    hn, zgm, zga, zup, zgp, zfr, q_att, k_att, v_att = _in_proj(
        x2, row(norm_in), w_in_pad, row(q_norm), wuq_pad, row(kv_norm), wukv, rc, rsa, rsb, tm)
    o, lse_rows = _attn_fwd(q_att, k_att, v_att, tq, 4)
    ypool = _pool_fwd(zup, zgp, pool_w, row(pool_scale))
    loss8, dh, dgm, doop, dga, dcap_rows, dyp, g_wout, g_wba, g_wbp, g_nf = _tail(
        x2, tgt, o, zga, ypool, zgm, w_ba, w_bp, w_out, row(norm_final), min(256, T))
    dup, dgp, g_pool_w, g_pool_scale = _pool_bwd(zup, zgp, dyp, pool_w, row(pool_scale))

    bf = lambda a: a.astype(BF16)
    slabs = [bf(g_wout).reshape(N_DEV, D_MODEL // N_DEV, D_MODEL), _cols_to_slabs(bf(g_wba)), _cols_to_slabs(bf(g_wbp))]
    early = [g_pool_w, g_pool_scale, g_nf, loss8[0]]
    packed = jnp.concatenate([_pack_rows(a) for a in early], axis=0)
    dq_att, dkv_nat, dkr4, s_wout, s_wba, s_wbp, tot_early = _attn_bwd(
        q_att, k_att, v_att, doop, lse_rows, dcap_rows, tq, slabs, packed)
    s_pool_w, s_pool_scale, s_nf, s_loss = _unpack_rows(tot_early, early)

    dfr, g_wuq_pad, g_wukv, g_qn, g_kvn = _mla_bwd(
        dq_att, dkv_nat, dkr4, zfr, row(q_norm), wuq_pad, row(kv_norm), wukv, rc, rsa, rsb, tm)
    dzs = (dgm, dga, dup, dgp, dfr)
    g_win = _unpad_g_in_t(_in_proj_bwd_w(dzs, hn, tm))
    slabs = [bf(g_win).reshape(N_DEV, IN_TOTAL // N_DEV, D_MODEL),
             bf(g_wuq_pad.reshape(Q_RANK, HEADS, HEAD_PAD)[:, :, :NOPE + ROPE]).reshape(N_DEV, Q_RANK // N_DEV, -1),
             bf(g_wukv).reshape(N_DEV, KV_RANK // N_DEV, HW)]
    grad_x, g_nin, s_win, s_wuq, s_wukv = _in_proj_bwd_x(dzs, x2, dh, row(norm_in), w_in_pad, min(256, T), slabs)
    late = [g_nin, g_qn, g_kvn]
    (tot_late,) = _reduce_scatter([], jnp.concatenate([_pack_rows(a) for a in late], axis=0))
    s_nin, s_qn, s_kvn = _unpack_rows(tot_late, late)

    grads = dict(norm_in=s_nin, w_in=s_win, q_norm=s_qn, w_uq=s_wuq, kv_norm=s_kvn, w_ukv=s_wukv, pool_w=s_pool_w.reshape(-1, GROUP),
                 pool_scale=s_pool_scale, w_branch_attn=s_wba, w_branch_pool=s_wbp, w_out=s_wout, norm_final=s_nf)
    return s_loss[0], grad_x, grads


MESH_ID = pl.DeviceIdType.MESH
VMEM_SPEC = pl.BlockSpec(memory_space=pltpu.VMEM)
HBM_SPEC = pl.BlockSpec(memory_space=pl.ANY)


def _mesh_pos():
    return lax.axis_index("x"), lax.axis_index("y"), lax.axis_index("c")


def _slot(px, py, pc):
    return 4 * px + 2 * py + pc


def _all_gather_bf16(shards):
    n = len(shards)

    def body(*refs):
        ins, outs, stage = refs[:n], refs[n:2 * n], refs[2 * n:3 * n]
        send_sems, recv_sems, local_sems = refs[3 * n:]
        x, y, c = _mesh_pos()
        me, sibling = (x, y, c), (x, y, 1 - c)
        chips = [(1 - x, y), (x, 1 - y), (1 - x, 1 - y)]

        def copy(a, k, block, to, from_stage=False):
            dst = outs[a].at[_slot(*block)]
            return pltpu.make_async_remote_copy(
                src_ref=stage[a] if from_stage else dst, dst_ref=dst, send_sem=send_sems.at[7 * a + k],
                recv_sem=recv_sems.at[7 * a + k], device_id=to, device_id_type=MESH_ID)

        for a in range(n):
            stage[a][...] = ins[a][...].astype(BF16)
        mine = [pltpu.make_async_copy(stage[a], outs[a].at[_slot(*me)], local_sems.at[a]) for a in range(n)]
        for cp in mine:
            cp.start()
        first = []
        for a in range(n):
            first.append(copy(a, 0, me, sibling, True))
            first += [copy(a, 1 + j, me, (*chip, c), True) for j, chip in enumerate(chips)]
        for cp in first:
            cp.start()
        passed = []
        for a in range(n):
            for j, chip in enumerate(chips):
                copy(a, 1 + j, (*chip, c), me).wait_recv()
                passed.append(copy(a, 4 + j, (*chip, c), sibling))
                passed[-1].start()
        for a in range(n):
            copy(a, 0, sibling, me).wait_recv()
            for j, chip in enumerate(chips):
                copy(a, 4 + j, (*chip, 1 - c), me).wait_recv()
        for cp in first + passed:
            cp.wait_send()
        for cp in mine:
            cp.wait()

    return pl.pallas_call(
        body, name="all_gather_weights",
        in_specs=[VMEM_SPEC] * n, out_specs=[HBM_SPEC] * n,
        out_shape=[SDS((N_DEV,) + s.shape, BF16) for s in shards],
        scratch_shapes=[pltpu.VMEM(s.shape, BF16) for s in shards]
        + [pltpu.SemaphoreType.DMA((7 * n,)), pltpu.SemaphoreType.DMA((7 * n,)), pltpu.SemaphoreType.DMA((n,))],
        compiler_params=_cp(),
    )(*shards)


N_CHIPS = 4


def _reduce_scatter(slabs, packed):
    def body(*refs):
        n = len(slabs)
        rs = _ReduceScatter(refs[:n], refs[n], refs[n + 1:2 * n + 1], refs[2 * n + 1], refs[2 * n + 2:])
        rs.start1()
        rs.finish1_start2()
        rs.finish2()

    shapes = [s.shape[1:] for s in slabs]
    return pl.pallas_call(
        body, name="reduce_scatter_grads",
        in_specs=[HBM_SPEC] * len(slabs) + [VMEM_SPEC], out_specs=[VMEM_SPEC] * (len(slabs) + 1),
        out_shape=[SDS(s, F32) for s in shapes] + [SDS(packed.shape, F32)],
        scratch_shapes=_rs_scratch(shapes, packed.shape), compiler_params=_cp(),
    )(*slabs, packed)


def _rs_scratch(shapes, packed_shape):
    n = len(shapes)
    n1, n2 = N_CHIPS * n + 1, (N_CHIPS - 1) * (n + 1)
    dma = pltpu.SemaphoreType.DMA
    packed = [] if packed_shape is None else [pltpu.VMEM(packed_shape, F32), pltpu.VMEM((N_CHIPS,) + tuple(packed_shape), F32)]
    return ([pltpu.VMEM((N_CHIPS,) + tuple(s), BF16) for s in shapes] * 2 + [pltpu.VMEM((N_CHIPS - 1,) + tuple(s), BF16) for s in shapes] * 2
            + packed + [dma((max(N_CHIPS * n, 1),)), dma((n1,)), dma((n1,)), dma((n2,)), dma((n2,))])


class _ReduceScatter:
    def __init__(self, slab_refs, packed_ref, out_refs, ptot_ref, scratch):
        n = self.n = len(slab_refs)
        self.slabs, self.packed, self.outs, self.ptot = slab_refs, packed_ref, out_refs, ptot_ref
        self.own1, self.land1, self.send2, self.land2 = (scratch[k * n:(k + 1) * n] for k in range(4))
        rest = scratch[4 * n:]
        if packed_ref is not None:
            self.pland1, self.pland2 = rest[:2]
            rest = rest[2:]
        self.loc_sems, self.send1_sems, self.recv1_sems, self.send2_sems, self.recv2_sems = rest
        self.x, self.y, self.c = _mesh_pos()

    def _chip(self, r):
        return (1 - self.x if r & 2 else self.x, 1 - self.y if r & 1 else self.y)

    @staticmethod
    def _remote(src, dst, send_sem, recv_sem, to):
        return pltpu.make_async_remote_copy(src_ref=src, dst_ref=dst, send_sem=send_sem, recv_sem=recv_sem, device_id=to,
                                            device_id_type=MESH_ID)

    def _copies1(self):
        c, sibling = self.c, (self.x, self.y, 1 - self.c)
        cps = []
        for a in range(self.n):
            for r in range(N_CHIPS):
                k = N_CHIPS * a + r
                cps.append(pltpu.make_async_copy(self.slabs[a].at[_slot(*self._chip(r), c)], self.own1[a].at[r], self.loc_sems.at[k]))
                cps.append(self._remote(self.slabs[a].at[_slot(*self._chip(r), 1 - c)], self.land1[a].at[r],
                                        self.send1_sems.at[k], self.recv1_sems.at[k], sibling))
        if self.packed is not None:
            k = N_CHIPS * self.n
            cps.append(self._remote(self.packed, self.pland1, self.send1_sems.at[k], self.recv1_sems.at[k], sibling))
        return cps

    def _copies2(self):
        cps = []
        for a in range(self.n):
            for r in range(1, N_CHIPS):
                k = (N_CHIPS - 1) * a + r - 1
                cps.append(self._remote(self.send2[a].at[r - 1], self.land2[a].at[r - 1], self.send2_sems.at[k], self.recv2_sems.at[k],
                                        (*self._chip(r), self.c)))
        if self.packed is not None:
            for r in range(1, N_CHIPS):
                k = (N_CHIPS - 1) * self.n + r - 1
                cps.append(self._remote(self.pland2.at[0], self.pland2.at[r], self.send2_sems.at[k], self.recv2_sems.at[k],
                                        (*self._chip(r), self.c)))
        return cps

    def start1(self):
        for cp in self._copies1():
            cp.start()

    def finish1_start2(self):
        for cp in self._copies1():
            cp.wait()
        for a in range(self.n):
            self.outs[a][...] = self.own1[a][0].astype(F32) + self.land1[a][0].astype(F32)
            for r in range(1, N_CHIPS):
                self.send2[a][r - 1] = (self.own1[a][r].astype(F32) + self.land1[a][r].astype(F32)).astype(BF16)
        if self.packed is not None:
            self.pland2[0] = self.packed[...] + self.pland1[...]
        for cp in self._copies2():
            cp.start()

    def finish2(self):
        for cp in self._copies2():
            cp.wait()
        for a in range(self.n):
            l2 = self.land2[a]
            self.outs[a][...] = self.outs[a][...] + ((l2[0].astype(F32) + l2[1].astype(F32)) + l2[2].astype(F32))
        if self.packed is not None:
            p2 = self.pland2
            self.ptot[...] = (p2[0] + p2[1]) + (p2[2] + p2[3])


def _adamw(ws, gs, ms, vs):
    n = len(ws)

    def body(*refs):
        for k in range(n):
            w, g, m, v = (refs[j * n + k][...] for j in range(4))
            d_ref, nm_ref, nv_ref = (refs[(4 + j) * n + k] for j in range(3))
            m = ADAM_B1 * m + (1.0 - ADAM_B1) * g
            v = ADAM_B2 * v + (1.0 - ADAM_B2) * jnp.square(g)
            m_hat = m / (1.0 - ADAM_B1 ** ADAM_STEP)
            v_hat = v / (1.0 - ADAM_B2 ** ADAM_STEP)
            d_ref[...] = -ADAM_LR * (m_hat / (jnp.sqrt(v_hat) + ADAM_EPS) + ADAM_WD * w)
            nm_ref[...] = m
            nv_ref[...] = v

    outs = pl.pallas_call(
        body, name="adamw", in_specs=[VMEM_SPEC] * (4 * n), out_specs=[VMEM_SPEC] * (3 * n),
        out_shape=[SDS(w.shape, F32) for w in ws] * 3, compiler_params=_cp(),
    )(*ws, *gs, *ms, *vs)
    return outs[:n], outs[n:2 * n], outs[2 * n:]


WEIGHTS = ("norm_in", "w_in", "q_norm", "w_uq", "kv_norm", "w_ukv", "pool_w", "pool_scale", "w_branch_attn", "w_branch_pool",
           "w_out", "norm_final")
SHARDED = ("w_in", "w_branch_attn", "w_branch_pool", "w_out", "w_uq", "w_ukv")
REPLICATED = ("norm_in", "q_norm", "kv_norm", "pool_scale", "norm_final", "pool_w")
SUBLANES = 8


def _cols_to_slabs(g):
    r = g.shape[0]
    return g.reshape(r, N_DEV, -1).transpose(1, 0, 2)


def _slabs_to_cols(s):
    return s.transpose(1, 0, 2).reshape(s.shape[1], -1)


def _pack_rows(a):
    a = a.reshape(-1, LANES)
    return jnp.pad(a, ((0, -a.shape[0] % SUBLANES), (0, 0)))


def _unpack_rows(packed, like):
    out, row = [], 0
    for a in like:
        rows = a.size // LANES
        out.append(packed[row:row + rows].reshape(a.shape))
        row += rows + (-rows % SUBLANES)
    return out


def kernel(x, norm_in, w_in, q_norm, w_uq, kv_norm, w_ukv, pool_w, pool_scale, w_branch_attn, w_branch_pool, w_out, norm_final, loss_target, m_norm_in, m_w_in, m_q_norm, m_w_uq, m_kv_norm, m_w_ukv, m_pool_w, m_pool_scale, m_w_branch_attn, m_w_branch_pool, m_w_out, m_norm_final, v_norm_in, v_w_in, v_q_norm, v_w_uq, v_kv_norm, v_w_ukv, v_pool_w, v_pool_scale, v_w_branch_attn, v_w_branch_pool, v_w_out, v_norm_final):
    w = dict(norm_in=norm_in, w_in=w_in, q_norm=q_norm, w_uq=w_uq, kv_norm=kv_norm, w_ukv=w_ukv, pool_w=pool_w, pool_scale=pool_scale,
             w_branch_attn=w_branch_attn, w_branch_pool=w_branch_pool, w_out=w_out, norm_final=norm_final)
    m = dict(norm_in=m_norm_in, w_in=m_w_in, q_norm=m_q_norm, w_uq=m_w_uq, kv_norm=m_kv_norm, w_ukv=m_w_ukv, pool_w=m_pool_w,
             pool_scale=m_pool_scale, w_branch_attn=m_w_branch_attn, w_branch_pool=m_w_branch_pool, w_out=m_w_out, norm_final=m_norm_final)
    v = dict(norm_in=v_norm_in, w_in=v_w_in, q_norm=v_q_norm, w_uq=v_w_uq, kv_norm=v_kv_norm, w_ukv=v_w_ukv, pool_w=v_pool_w,
             pool_scale=v_pool_scale, w_branch_attn=v_w_branch_attn, w_branch_pool=v_w_branch_pool, w_out=v_w_out, norm_final=v_norm_final)

    def as2d(name, a):
        if name == "w_in":
            return a.T
        if name in ("w_uq", "w_ukv"):
            return a.reshape(a.shape[0], -1)
        if name == "pool_w":
            return a.reshape(-1, GROUP)
        return a.reshape(1, -1) if a.ndim == 1 else a

    def unshape(name, a):
        return a.T if name == "w_in" else a.reshape(w[name].shape)

    full = dict(zip(SHARDED, _all_gather_bf16([as2d(k, w[k]) for k in SHARDED])))
    loss, grad_x, g2d = _local_step(
        x.reshape(x.shape[1:]), loss_target.reshape(x.shape[1:]), norm_in, full["w_in"].reshape(IN_TOTAL, D_MODEL), q_norm,
        full["w_uq"].reshape(Q_RANK, HEADS, NOPE + ROPE), kv_norm, full["w_ukv"].reshape(KV_RANK, HEADS, NOPE + VDIM),
        pool_w, pool_scale, _slabs_to_cols(full["w_branch_attn"]), _slabs_to_cols(full["w_branch_pool"]),
        full["w_out"].reshape(D_MODEL, D_MODEL), norm_final)

    deltas, new_m, new_v = _adamw([as2d(k, w[k]) for k in WEIGHTS], [g2d[k] for k in WEIGHTS],
                                  [as2d(k, m[k]) for k in WEIGHTS], [as2d(k, v[k]) for k in WEIGHTS])
    shaped = lambda arrs: [unshape(k, a) for k, a in zip(WEIGHTS, arrs)]
    return (loss, grad_x.reshape(x.shape), *shaped([g2d[k] for k in WEIGHTS]), *shaped(deltas), *shaped(new_m), *shaped(new_v))
```

```python
import functools

import jax
import jax.numpy as jnp
import numpy as np
from jax import lax
from jax.experimental import pallas as pl
from jax.experimental.pallas import tpu as pltpu

F32 = jnp.float32
BF16 = jnp.bfloat16
SDS = jax.ShapeDtypeStruct

D_MODEL = 1024
HEADS = 8
NOPE = 64
ROPE = 32
VDIM = 64
Q_RANK = 384
KV_RANK = 256
MLA_W = HEADS * VDIM
POOL_W = 512
POOL_GROUPS = 4
GROUP = POOL_W // POOL_GROUPS
CHUNK = 64
ROPE_THETA = 10000.0
EPS = 1e-6
SCALE = (NOPE + ROPE) ** -0.5
LOG2E = 1.4426950408889634
LN2 = 0.6931471805599453
QK_SCALE_LOG2 = SCALE * LOG2E
IN_TOTAL = 4256
ADAM_LR, ADAM_B1, ADAM_B2, ADAM_EPS, ADAM_WD, ADAM_STEP = 0.001, 0.9, 0.999, 1e-08, 0.01, 10

N_DEV = 8
LANES = 128
HEAD_PAD = LANES
HW = HEADS * HEAD_PAD

ZQ, ZKV, ZKR, GA, UP, GP, GM, ZTOT = 0, 384, 640, 768, 1280, 1792, 2304, 4352
FRONT_W = GA
ZKR_ORIG = 640

VMEM_LIMIT = 56 * 1024 * 1024


def _cp(sem=None, **kw):
    if sem is not None:
        kw["dimension_semantics"] = sem
    return pltpu.CompilerParams(vmem_limit_bytes=VMEM_LIMIT, **kw)


def _mm(a, b):
    return lax.dot_general(a, b, (((1,), (0,)), ((), ())), preferred_element_type=F32)


def _mm_nt(a, b):
    return lax.dot_general(a, b, (((1,), (1,)), ((), ())), preferred_element_type=F32)


def _mm_tn(a, b):
    return lax.dot_general(a, b, (((0,), (0,)), ((), ())), preferred_element_type=F32)


def _row_spec(tm, w):
    return pl.BlockSpec((tm, w), lambda i: (i, 0))


def _full_spec(a):
    nd = len(a.shape)
    return pl.BlockSpec(a.shape, lambda *_: (0,) * nd)


def _rope(v, c, sa, sb, sign):
    n = v.shape[-1]
    reps = n // LANES
    if reps > 1:
        c, sa, sb = (jnp.tile(t, (1, reps)) for t in (c, sa, sb))
    up = pltpu.roll(v, n - ROPE // 2, 1)
    dn = pltpu.roll(v, ROPE // 2, 1)
    return v * c + sign * (up * sa + dn * sb)


def _rope_tables(T):
    half = ROPE // 2
    inv_freq = np.float32(ROPE_THETA) ** (-np.arange(half, dtype=np.float32) / np.float32(half))
    ang = np.arange(T, dtype=np.float32)[:, None] * inv_freq[None, :].astype(np.float32)
    cos, sin = np.cos(ang.astype(np.float64)).astype(np.float32), np.sin(ang.astype(np.float64)).astype(np.float32)
    z16 = np.zeros((T, half), np.float32)
    z32 = np.zeros((T, LANES - NOPE - ROPE), np.float32)
    c = np.concatenate([np.ones((T, NOPE), np.float32), cos, cos, z32], axis=1)
    sa = np.concatenate([np.zeros((T, NOPE), np.float32), -sin, z16, z32], axis=1)
    sb = np.concatenate([np.zeros((T, NOPE), np.float32), z16, sin, z32], axis=1)
    return jnp.asarray(c), jnp.asarray(sa), jnp.asarray(sb)


def _silu_parts(g):
    sg = jax.nn.sigmoid(g)
    return g * sg, sg + g * sg * (1.0 - sg)


def _in_proj(x2, norm_in, w_in_pad, q_norm, wuq_pad, kv_norm, wukv, rc, rsa, rsb, tm):
    T = x2.shape[0]

    def body(x_ref, nin_ref, win_ref, qn_ref, wuq_ref, kvn_ref, wukv_ref, c_ref, sa_ref, sb_ref,
             hn_ref, zgm_ref, zga_ref, zup_ref, zgp_ref, zfr_ref, q_ref, k_ref, v_ref):
        xf = x_ref[...]
        r = lax.rsqrt(jnp.mean(xf * xf, axis=-1, keepdims=True) + EPS)
        hn = (xf * r * nin_ref[...]).astype(BF16)
        hn_ref[...] = hn
        z = _mm_nt(hn, win_ref[...])
        zgm_ref[...] = z[:, GM:ZTOT]
        zga_ref[...] = z[:, GA:UP]
        zup_ref[...] = z[:, UP:GP]
        zgp_ref[...] = z[:, GP:GM]
        zfr_ref[...] = z[:, ZQ:GA]
        zq, zkv, zkr = z[:, ZQ:ZKV], z[:, ZKV:ZKR], z[:, ZKR:GA]
        c, sa, sb = c_ref[...], sa_ref[...], sb_ref[...]
        rq = lax.rsqrt(jnp.mean(zq * zq, axis=-1, keepdims=True) + EPS)
        cq = (zq * rq * qn_ref[...]).astype(BF16)
        q = _rope(_mm(cq, wuq_ref[...]), c, sa, sb, 1.0)
        q_ref[...] = (q * QK_SCALE_LOG2).astype(BF16)
        rkv = lax.rsqrt(jnp.mean(zkv * zkv, axis=-1, keepdims=True) + EPS)
        ckv = (zkv * rkv * kvn_ref[...]).astype(BF16)
        kv = _mm(ckv, wukv_ref[...])
        kr = _rope(zkr, c, sa, sb, 1.0)
        lane = lax.broadcasted_iota(jnp.int32, kv.shape, 1) % LANES
        k_ref[...] = jnp.where(lane < NOPE, kv, jnp.tile(kr, (1, HEADS))).astype(BF16)
        v_ref[...] = jnp.where(lane < NOPE, 1.0, kv).astype(BF16)

    ins = (x2, norm_in, w_in_pad, q_norm, wuq_pad, kv_norm, wukv, rc, rsa, rsb)
    in_specs = [_row_spec(tm, D_MODEL), _full_spec(norm_in), _full_spec(w_in_pad), _full_spec(q_norm), _full_spec(wuq_pad),
                _full_spec(kv_norm), _full_spec(wukv), _row_spec(tm, LANES), _row_spec(tm, LANES), _row_spec(tm, LANES)]
    widths = [(D_MODEL, BF16), (ZTOT - GM, F32), (UP - GA, F32), (GP - UP, F32), (GM - GP, F32), (FRONT_W, F32),
              (HW, BF16), (HW, BF16), (HW, BF16)]
    return pl.pallas_call(
        body, name="in_proj", grid=(T // tm,), in_specs=in_specs,
        out_specs=[_row_spec(tm, w) for w, _ in widths],
        out_shape=[SDS((T, w), dt) for w, dt in widths],
        compiler_params=_cp(("parallel",)),
    )(*ins)


def _diag_mask(tq, transposed):
    r = lax.broadcasted_iota(jnp.int32, (tq, tq), 0) // CHUNK
    c = lax.broadcasted_iota(jnp.int32, (tq, tq), 1) // CHUNK
    return (r <= c) if transposed else (c <= r)


_HEAD_LANES = (slice(0, LANES), slice(LANES, 2 * LANES))


def _pair_rows_spec(tq):
    return pl.BlockSpec((1, 2, tq), lambda p, i: (p, 0, i))


def _store_pair_rows(ref, k, pair):
    t = pair.T
    ref[k, 0:1, :] = t[0:1, :]
    ref[k, 1:2, :] = t[VDIM:VDIM + 1, :]


def _attn_fwd(q_att, k_att, v_att, tq, hps):
    T = q_att.shape[0]
    head_lanes = [slice(h * LANES, (h + 1) * LANES) for h in range(hps)]

    def body(q_ref, k_ref, v_ref, o_ref, lser_ref):
        i = pl.program_id(1)
        mask = _diag_mask(tq, False)
        lane = lax.broadcasted_iota(jnp.int32, (tq, LANES), 1)
        qs = [q_ref[:, hs] for hs in head_lanes]

        def step(j, carry, masked):
            rows = pl.ds(pl.multiple_of(j * tq, tq), tq)
            out = []
            for (m, acc), qh, hs in zip(carry, qs, head_lanes):
                s = _mm_nt(qh, k_ref[rows, hs])
                if masked:
                    s = jnp.where(mask, s, -jnp.inf)
                m_new = jnp.maximum(m, jnp.max(s, axis=-1, keepdims=True))
                p = jnp.exp2(s - m_new).astype(BF16)
                out.append((m_new, jnp.exp2(m - m_new) * acc + _mm(p, v_ref[rows, hs])))
            return tuple(out)

        init = ((jnp.full((tq, 1), -jnp.inf, F32), jnp.zeros((tq, LANES), F32)),) * hps
        res = step(i, lax.fori_loop(0, i, functools.partial(step, masked=False), init), True)
        for pair in range(hps // 2):
            (ma, acca), (mb, accb) = res[2 * pair], res[2 * pair + 1]
            la, lb = acca[:, :1], accb[:, :1]
            o_ref[:, pair * LANES:(pair + 1) * LANES] = jnp.where(lane < VDIM, pltpu.roll(acca / la, VDIM, 1), accb / lb)
            _store_pair_rows(lser_ref, pair, jnp.where(lane < VDIM, ma + jnp.log2(la), mb + jnp.log2(lb)))

    qspec = pl.BlockSpec((tq, hps * LANES), lambda p, i: (i, p))
    kspec = pl.BlockSpec((T, hps * LANES), lambda p, i: (0, p))
    ospec = pl.BlockSpec((tq, hps * VDIM), lambda p, i: (i, p))
    return pl.pallas_call(
        body, name="attn_fwd", grid=(HEADS // hps, T // tq), in_specs=[qspec, kspec, kspec],
        out_specs=[ospec, pl.BlockSpec((hps // 2, 2, tq), lambda p, i: (p, 0, i))],
        out_shape=[SDS((T, MLA_W), F32), SDS((HEADS // 2, 2, T), F32)],
        compiler_params=_cp(("parallel", "parallel")),
    )(q_att, k_att, v_att)


def _pick(g, vals):
    out = vals[-1]
    for k in range(len(vals) - 2, -1, -1):
        out = jnp.where(g == k, vals[k], out)
    return out


def _window_sum(u, g, forward):
    T = u.shape[0]
    row = lax.broadcasted_iota(jnp.int32, u.shape, 0)

    def sh(s, k):
        if forward:
            return jnp.where(row >= k, pltpu.roll(s, k, 0), 0.0)
        return jnp.where(row < T - k, pltpu.roll(s, T - k, 0), 0.0)

    sums, s = [], u
    for k in (1, 2, 4, 8):
        s = s + sh(s, k)
        sums.append(s)
    return _pick(g, sums)


def _pool_count(shape, g):
    row = lax.broadcasted_iota(jnp.int32, shape, 0)
    return jnp.minimum(row + 1, lax.shift_left(jnp.int32(2), g)).astype(F32)


def _pool_fwd(zup, zgp, pool_w, pool_scale):
    T = zup.shape[0]

    def body(u_ref, g_ref, w_ref, sc_ref, y_ref):
        g = pl.program_id(0)
        u = u_ref[...]
        d = _window_sum(u, g, True) / _pool_count(u.shape, g) - u
        lin = _mm(d.astype(BF16), w_ref[0].astype(BF16))
        silu, _ = _silu_parts(g_ref[...])
        y_ref[...] = (lin * sc_ref[...] * silu).astype(BF16)

    col = pl.BlockSpec((T, GROUP), lambda g: (0, g))
    return pl.pallas_call(
        body, name="pool_fwd", grid=(POOL_GROUPS,),
        in_specs=[col, col, pl.BlockSpec((1, GROUP, GROUP), lambda g: (g, 0, 0)), pl.BlockSpec((1, GROUP), lambda g: (0, g))],
        out_specs=col, out_shape=SDS((T, POOL_W), BF16), compiler_params=_cp(("parallel",)),
    )(zup, zgp, pool_w, pool_scale)


def _pool_bwd(zup, zgp, dyp, pool_w, pool_scale):
    T = zup.shape[0]

    def body(u_ref, g_ref, dy_ref, w_ref, sc_ref, du_ref, dg_ref, gw_ref, gsc_ref):
        g = pl.program_id(0)
        u = u_ref[...]
        cnt = _pool_count(u.shape, g)
        d = (_window_sum(u, g, True) / cnt - u).astype(BF16)
        wb = w_ref[0].astype(BF16)
        lin = _mm(d, wb)
        sc = sc_ref[...]
        silu, dsilu = _silu_parts(g_ref[...])
        dy = dy_ref[...]
        dg_ref[...] = (dy * lin * sc * dsilu).astype(BF16)
        dpre = dy * silu
        gsc_ref[...] = jnp.sum(dpre * lin, axis=0, keepdims=True)
        dlin = (dpre * sc).astype(BF16)
        gw_ref[0] = _mm_tn(d, dlin)
        dd = _mm_nt(dlin, wb)
        du_ref[...] = (_window_sum(dd / cnt, g, False) - dd).astype(BF16)

    col = pl.BlockSpec((T, GROUP), lambda g: (0, g))
    wspec = pl.BlockSpec((1, GROUP, GROUP), lambda g: (g, 0, 0))
    vspec = pl.BlockSpec((1, GROUP), lambda g: (0, g))
    return pl.pallas_call(
        body, name="pool_bwd", grid=(POOL_GROUPS,), in_specs=[col, col, col, wspec, vspec], out_specs=[col, col, wspec, vspec],
        out_shape=[SDS((T, POOL_W), BF16), SDS((T, POOL_W), BF16), SDS((POOL_GROUPS, GROUP, GROUP), F32), SDS((1, POOL_W), F32)],
        compiler_params=_cp(("parallel",)),
    )(zup, zgp, dyp, pool_w, pool_scale)


def _tail(x2, tgt, o, zga, ypool, zgm, wba, wbp, wout, norm_final, tm):
    T = x2.shape[0]

    def body(x_ref, tgt_ref, o_ref, zga_ref, yp_ref, zgm_ref, wba_ref, wbp_ref, wout_ref, nf_ref,
             loss_ref, dh_ref, dgm_ref, doop_ref, dga_ref, dcapr_ref, dyp_ref, gwout_ref, gwba_ref, gwbp_ref, gnf_ref):
        @pl.when(pl.program_id(0) == 0)
        def _():
            for ref in (loss_ref, gwout_ref, gwba_ref, gwbp_ref, gnf_ref):
                ref[...] = jnp.zeros_like(ref)

        o_v = o_ref[...]
        silu, dsilu = _silu_parts(zga_ref[...])
        ya = (o_v * silu).astype(BF16)
        yp = yp_ref[...]
        wba_v, wbp_v, wout_v = wba_ref[...], wbp_ref[...], wout_ref[...]
        a = _mm(ya, wba_v)
        p = _mm(yp, wbp_v)
        gate = jax.nn.sigmoid(zgm_ref[...])
        ga, gp = gate[:, :D_MODEL], gate[:, D_MODEL:]
        mg = (ga * a + gp * p).astype(BF16)
        h = x_ref[...] + _mm(mg, wout_v)
        r = lax.rsqrt(jnp.mean(h * h, axis=-1, keepdims=True) + EPS)
        gf = nf_ref[...]
        hr = h * r
        e = hr * gf - tgt_ref[...]
        loss_ref[...] += (0.5 / D_MODEL) * jnp.sum(e * e)
        dy = e * (1.0 / D_MODEL)
        gnf_ref[...] += jnp.sum(dy * hr, axis=0, keepdims=True)
        u = dy * gf
        dh = r * (u - hr * jnp.mean(u * hr, axis=-1, keepdims=True))
        dh_ref[...] = dh
        dhb = dh.astype(BF16)
        dmg = _mm_nt(dhb, wout_v)
        gwout_ref[...] += _mm_tn(mg, dhb)
        dgm_ref[:, :D_MODEL] = (dmg * a * ga * (1.0 - ga)).astype(BF16)
        dgm_ref[:, D_MODEL:] = (dmg * p * gp * (1.0 - gp)).astype(BF16)
        dab = (dmg * ga).astype(BF16)
        dpb = (dmg * gp).astype(BF16)
        dya = _mm_nt(dab, wba_v)
        gwba_ref[...] += _mm_tn(ya, dab)
        dyp_ref[...] = _mm_nt(dpb, wbp_v)
        gwbp_ref[...] += _mm_tn(yp, dpb)
        do = dya * silu
        dga_ref[...] = (dya * o_v * dsilu).astype(BF16)
        prod = do * o_v
        lo = lax.broadcasted_iota(jnp.int32, (tm, LANES), 1) < VDIM
        for pair in range(HEADS // 2):
            ls = slice(pair * LANES, (pair + 1) * LANES)
            do_p, prod_p = do[:, ls], prod[:, ls]
            dcap_a = jnp.sum(jnp.where(lo, prod_p, 0.0), axis=-1, keepdims=True)
            dcap_b = jnp.sum(jnp.where(lo, 0.0, prod_p), axis=-1, keepdims=True)
            _store_pair_rows(dcapr_ref, pair, jnp.where(lo, dcap_a, dcap_b))
            doop_ref[:, 2 * pair * LANES:(2 * pair + 1) * LANES] = jnp.where(lo, 0.0, pltpu.roll(do_p, VDIM, 1)).astype(BF16)
            doop_ref[:, (2 * pair + 1) * LANES:(2 * pair + 2) * LANES] = jnp.where(lo, 0.0, do_p).astype(BF16)

    ins = (x2, tgt, o, zga, ypool, zgm, wba, wbp, wout, norm_final)
    in_specs = [_row_spec(tm, D_MODEL), _row_spec(tm, D_MODEL), _row_spec(tm, MLA_W), _row_spec(tm, MLA_W), _row_spec(tm, POOL_W),
                _row_spec(tm, 2 * D_MODEL), _full_spec(wba), _full_spec(wbp), _full_spec(wout), _full_spec(norm_final)]
    outs = [SDS((8, LANES), F32), SDS((T, D_MODEL), F32), SDS((T, 2 * D_MODEL), BF16), SDS((T, HW), BF16), SDS((T, MLA_W), BF16),
            SDS((HEADS // 2, 2, T), F32), SDS((T, POOL_W), F32),
            SDS((D_MODEL, D_MODEL), F32), SDS((MLA_W, D_MODEL), F32), SDS((POOL_W, D_MODEL), F32), SDS((1, D_MODEL), F32)]
    out_specs = [_full_spec(outs[0]), _row_spec(tm, D_MODEL), _row_spec(tm, 2 * D_MODEL), _row_spec(tm, HW), _row_spec(tm, MLA_W),
                 pl.BlockSpec((HEADS // 2, 2, tm), lambda i: (0, 0, i)), _row_spec(tm, POOL_W),
                 _full_spec(outs[7]), _full_spec(outs[8]), _full_spec(outs[9]), _full_spec(outs[10])]
    return pl.pallas_call(
        body, name="tail", grid=(T // tm,), in_specs=in_specs, out_specs=out_specs, out_shape=outs,
        compiler_params=_cp(("arbitrary",)),
    )(*ins)


def _attn_bwd(q_att, k_att, v_att, doop, lse_rows, dcap_rows, tq, slabs, packed):
    T = q_att.shape[0]
    nq = T // tq
    n = len(slabs)
    n_pairs = HEADS // 2

    def body(q_ref, k_ref, v_ref, doop_ref, lse_ref, dcap_ref, *rest):
        slab_refs, packed_ref = rest[:n], rest[n]
        dq_ref, dkv_ref, dkr_ref = rest[n + 1:n + 4]
        sum_refs, ptot_ref = rest[n + 4:2 * n + 4], rest[2 * n + 4]
        dq_acc = rest[2 * n + 5]
        rs = _ReduceScatter(slab_refs, packed_ref, sum_refs, ptot_ref, rest[2 * n + 6:])
        pair, j = pl.program_id(0), pl.program_id(1)
        pl.when((pair == 0) & (j == 0))(rs.start1)
        pl.when((pair == 1) & (j == 0))(rs.finish1_start2)
        mask = _diag_mask(tq, True)
        lane = lax.broadcasted_iota(jnp.int32, (tq, LANES), 1)
        ks = [k_ref[:, hs] for hs in _HEAD_LANES]
        vs = [v_ref[:, hs] for hs in _HEAD_LANES]

        @pl.when(j == 0)
        def _():
            dq_acc[...] = jnp.zeros_like(dq_acc)

        def step(i, carry, masked):
            rows = pl.ds(pl.multiple_of(i * tq, tq), tq)
            out = []
            for h, ((dk, dv), kh, vh, hs) in enumerate(zip(carry, ks, vs, _HEAD_LANES)):
                qh = q_ref[rows, hs]
                doop_h = doop_ref[rows, hs]
                pt = jnp.exp2(_mm_nt(kh, qh) - lse_ref[0, h:h + 1, rows])
                if masked:
                    pt = jnp.where(mask, pt, 0.0)
                dv = dv + _mm(pt.astype(BF16), doop_h)
                dpt = _mm_nt(vh, doop_h)
                dst = (pt * (dpt - dcap_ref[0, h:h + 1, rows])).astype(BF16)
                dq_acc[rows, hs] += _mm_tn(dst, kh)
                out.append((dk + _mm(dst, qh), dv))
            return tuple(out)

        zero = jnp.zeros((tq, LANES), F32)
        carry = step(j, ((zero, zero), (zero, zero)), True)
        (dka, dva), (dkb, dvb) = lax.fori_loop(j + 1, nq, functools.partial(step, masked=False), carry)
        dka, dkb = dka * LN2, dkb * LN2
        dkv_ref[:, _HEAD_LANES[0]] = jnp.where(lane < NOPE, dka, dva).astype(BF16)
        dkv_ref[:, _HEAD_LANES[1]] = jnp.where(lane < NOPE, dkb, dvb).astype(BF16)
        dkr_ref[0] = jnp.where((lane >= NOPE) & (lane < NOPE + ROPE), dka + dkb, 0.0)

        @pl.when(j == nq - 1)
        def _():
            dq_ref[...] = (dq_acc[...] * SCALE).astype(BF16)

        pl.when((pair == n_pairs - 1) & (j == nq - 1))(rs.finish2)

    kspec = pl.BlockSpec((tq, 2 * LANES), lambda p, j: (j, p))
    qspec = pl.BlockSpec((T, 2 * LANES), lambda p, j: (0, p))
    rspec = pl.BlockSpec((1, 2, T), lambda p, j: (p, 0, 0))
    sums = [SDS(s.shape[1:], F32) for s in slabs] + [SDS(packed.shape, F32)]
    return pl.pallas_call(
        body, name="attn_bwd", grid=(n_pairs, nq),
        in_specs=[qspec, kspec, kspec, qspec, rspec, rspec] + [HBM_SPEC] * n + [_full_spec(packed)],
        out_specs=[qspec, kspec, pl.BlockSpec((1, tq, LANES), lambda p, j: (p, j, 0))] + [_full_spec(s) for s in sums],
        out_shape=[SDS((T, HW), BF16), SDS((T, HW), BF16), SDS((n_pairs, T, LANES), F32)] + sums,
        scratch_shapes=[pltpu.VMEM((T, 2 * LANES), F32)] + _rs_scratch([s.shape for s in sums[:-1]], packed.shape),
        compiler_params=_cp(("arbitrary", "arbitrary")),
    )(q_att, k_att, v_att, doop, lse_rows, dcap_rows, *slabs, packed)


def _rms_bwd(z, gain, dout):
    r = lax.rsqrt(jnp.mean(z * z, axis=-1, keepdims=True) + EPS)
    zr = z * r
    u = dout * gain
    return r * (u - zr * jnp.mean(u * zr, axis=-1, keepdims=True)), jnp.sum(dout * zr, axis=0, keepdims=True)


def _mla_bwd(dq_att, dkv_nat, dkr4, zfr, q_norm, wuq_pad, kv_norm, wukv, rc, rsa, rsb, tm):
    T = dq_att.shape[0]

    def body(dq_ref, dkv_ref, dkr_ref, zfr_ref, qn_ref, wuq_ref, kvn_ref, wukv_ref, c_ref, sa_ref, sb_ref,
             dfr_ref, gwuq_ref, gwukv_ref, gqn_ref, gkvn_ref):
        @pl.when(pl.program_id(0) == 0)
        def _():
            for ref in (gwuq_ref, gwukv_ref, gqn_ref, gkvn_ref):
                ref[...] = jnp.zeros_like(ref)

        c, sa, sb = c_ref[...], sa_ref[...], sb_ref[...]
        zq, zkv = zfr_ref[:, :Q_RANK], zfr_ref[:, Q_RANK:Q_RANK + KV_RANK]
        qn, kvn = qn_ref[...], kvn_ref[...]
        cq = (zq * lax.rsqrt(jnp.mean(zq * zq, axis=-1, keepdims=True) + EPS) * qn).astype(BF16)
        ckv = (zkv * lax.rsqrt(jnp.mean(zkv * zkv, axis=-1, keepdims=True) + EPS) * kvn).astype(BF16)
        dq = _rope(dq_ref[...].astype(F32), c, sa, sb, -1.0).astype(BF16)
        gwuq_ref[...] += _mm_tn(cq, dq)
        dzq, gqn = _rms_bwd(zq, qn, _mm_nt(dq, wuq_ref[...]))
        gqn_ref[...] += gqn
        dkv = dkv_ref[...]
        gwukv_ref[...] += _mm_tn(ckv, dkv)
        dzkv, gkvn = _rms_bwd(zkv, kvn, _mm_nt(dkv, wukv_ref[...]))
        gkvn_ref[...] += gkvn
        dkr = dkr_ref[0] + dkr_ref[1] + dkr_ref[2] + dkr_ref[3]
        dfr_ref[:, :Q_RANK] = dzq.astype(BF16)
        dfr_ref[:, Q_RANK:Q_RANK + KV_RANK] = dzkv.astype(BF16)
        dfr_ref[:, Q_RANK + KV_RANK:] = _rope(dkr, c, sa, sb, -1.0).astype(BF16)

    ins = (dq_att, dkv_nat, dkr4, zfr, q_norm, wuq_pad, kv_norm, wukv, rc, rsa, rsb)
    in_specs = [_row_spec(tm, HW), _row_spec(tm, HW), pl.BlockSpec((HEADS // 2, tm, LANES), lambda i: (0, i, 0)), _row_spec(tm, FRONT_W),
                _full_spec(q_norm), _full_spec(wuq_pad), _full_spec(kv_norm), _full_spec(wukv),
                _row_spec(tm, LANES), _row_spec(tm, LANES), _row_spec(tm, LANES)]
    outs = [SDS((T, FRONT_W), BF16), SDS((Q_RANK, HW), F32), SDS((KV_RANK, HW), F32), SDS((1, Q_RANK), F32), SDS((1, KV_RANK), F32)]
    out_specs = [_row_spec(tm, FRONT_W)] + [_full_spec(s) for s in outs[1:]]
    return pl.pallas_call(
        body, name="mla_bwd", grid=(T // tm,), in_specs=in_specs, out_specs=out_specs, out_shape=outs,
        compiler_params=_cp(("arbitrary",)),
    )(*ins)


_DZ_COLS = ((GM, ZTOT), (GA, UP), (UP, GP), (GP, GM), (ZQ, GA))


def _in_proj_bwd_x(dzs, x2, dh, norm_in, w_in_pad, tm, slabs):
    T = x2.shape[0]
    steps = T // tm
    n = len(slabs)

    def body(d0, d1, d2, d3, d4, x_ref, dh_ref, nin_ref, win_ref, *rest):
        slab_refs, (gx_ref, gnin_ref), sum_refs = rest[:n], rest[n:n + 2], rest[n + 2:2 * n + 2]
        rs = _ReduceScatter(slab_refs, None, sum_refs, None, rest[2 * n + 2:])
        step = pl.program_id(0)

        @pl.when(step == 0)
        def _():
            gnin_ref[...] = jnp.zeros_like(gnin_ref)
            rs.start1()

        pl.when(step == min(2, steps - 1))(rs.finish1_start2)
        dhn = None
        for ref, (lo, hi) in zip((d0, d1, d2, d3, d4), _DZ_COLS):
            t = _mm(ref[...], win_ref[lo:hi, :])
            dhn = t if dhn is None else dhn + t
        dx, gnin = _rms_bwd(x_ref[...], nin_ref[...], dhn)
        gnin_ref[...] += gnin
        gx_ref[...] = dx + dh_ref[...]
        pl.when(step == steps - 1)(rs.finish2)

    in_specs = [_row_spec(tm, hi - lo) for lo, hi in _DZ_COLS] + [_row_spec(tm, D_MODEL), _row_spec(tm, D_MODEL),
                                                                  _full_spec(norm_in), _full_spec(w_in_pad)] + [HBM_SPEC] * n
    sums = [SDS(s.shape[1:], F32) for s in slabs]
    outs = [SDS((T, D_MODEL), F32), SDS((1, D_MODEL), F32)] + sums
    return pl.pallas_call(
        body, name="in_proj_bwd_x", grid=(steps,), in_specs=in_specs,
        out_specs=[_row_spec(tm, D_MODEL), _full_spec(outs[1])] + [_full_spec(s) for s in sums],
        out_shape=outs, scratch_shapes=_rs_scratch([s.shape for s in sums], None), compiler_params=_cp(("arbitrary",)),
    )(*dzs, x2, dh, norm_in, w_in_pad, *slabs)


SLAB_ROWS = IN_TOTAL // N_DEV


def _slab_segments(k):
    cuts = [(0, ZKR_ORIG, 0), (ZKR_ORIG, ZKR_ORIG + ROPE, NOPE), (ZKR_ORIG + ROPE, IN_TOTAL, LANES - ROPE)]
    lo, hi = k * SLAB_ROWS, (k + 1) * SLAB_ROWS
    return [(max(lo, a) - lo, max(lo, a) + shift, min(hi, b) - max(lo, a)) for a, b, shift in cuts if min(hi, b) > max(lo, a)]


def _in_proj_bwd_w(dzs, hn, tm):
    T = hn.shape[0]
    steps = T // tm

    def body(d0, d1, d2, d3, d4, hn_ref, slab_ref, acc_ref):
        @pl.when(pl.program_id(0) == 0)
        def _():
            acc_ref[...] = jnp.zeros_like(acc_ref)

        hn_v = hn_ref[...]
        for ref, (lo, hi) in zip((d0, d1, d2, d3, d4), _DZ_COLS):
            acc_ref[lo:hi, :] += _mm_tn(ref[...], hn_v)

        @pl.when(pl.program_id(0) == steps - 1)
        def _():
            for k in range(N_DEV):
                for at, src, rows in _slab_segments(k):
                    slab_ref[k, at:at + rows, :] = acc_ref[src:src + rows, :].astype(BF16)

    in_specs = [_row_spec(tm, hi - lo) for lo, hi in _DZ_COLS] + [_row_spec(tm, D_MODEL)]
    out = SDS((N_DEV, SLAB_ROWS, D_MODEL), BF16)
    return pl.pallas_call(
        body, name="in_proj_bwd_w", grid=(steps,), in_specs=in_specs, out_specs=_full_spec(out), out_shape=out,
        scratch_shapes=[pltpu.VMEM((ZTOT, D_MODEL), F32)], compiler_params=_cp(("arbitrary",)),
    )(*dzs, hn)


def _local_step(x2, tgt, norm_in, w_in_pad, q_norm, w_uq, kv_norm, w_ukv, pool_w, pool_scale, w_ba, w_bp, w_out, norm_final):
    T = x2.shape[0]
    tm = min(512, T)
    tq = min(512, T)
    row = lambda v: v.reshape(1, -1)
    wuq_pad = jnp.pad(w_uq, ((0, 0), (0, 0), (0, HEAD_PAD - NOPE - ROPE))).reshape(Q_RANK, HW)
    wukv = w_ukv.reshape(KV_RANK, HW)
    rc, rsa, rsb = _rope_tables(T)

    hn, zgm, zga, zup, zgp, zfr, q_att, k_att, v_att = _in_proj(
        x2, row(norm_in), w_in_pad, row(q_norm), wuq_pad, row(kv_norm), wukv, rc, rsa, rsb, tm)
    o, lse_rows = _attn_fwd(q_att, k_att, v_att, tq, 4)
    ypool = _pool_fwd(zup, zgp, pool_w, row(pool_scale))
    loss8, dh, dgm, doop, dga, dcap_rows, dyp, g_wout, g_wba, g_wbp, g_nf = _tail(
        x2, tgt, o, zga, ypool, zgm, w_ba, w_bp, w_out, row(norm_final), min(256, T))
    dup, dgp, g_pool_w, g_pool_scale = _pool_bwd(zup, zgp, dyp, pool_w, row(pool_scale))

    bf = lambda a: a.astype(BF16)
    slabs = [bf(g_wout).reshape(N_DEV, D_MODEL // N_DEV, D_MODEL), _cols_to_slabs(bf(g_wba)), _cols_to_slabs(bf(g_wbp))]
    early = [g_pool_w, g_pool_scale, g_nf, loss8[0]]
    packed = jnp.concatenate([_pack_rows(a) for a in early], axis=0)
    dq_att, dkv_nat, dkr4, s_wout, s_wba, s_wbp, tot_early = _attn_bwd(
        q_att, k_att, v_att, doop, lse_rows, dcap_rows, tq, slabs, packed)
    s_pool_w, s_pool_scale, s_nf, s_loss = _unpack_rows(tot_early, early)

    dfr, g_wuq_pad, g_wukv, g_qn, g_kvn = _mla_bwd(
        dq_att, dkv_nat, dkr4, zfr, row(q_norm), wuq_pad, row(kv_norm), wukv, rc, rsa, rsb, tm)
    dzs = (dgm, dga, dup, dgp, dfr)
    slabs = [_in_proj_bwd_w(dzs, hn, tm),
             bf(g_wuq_pad.reshape(Q_RANK, HEADS, HEAD_PAD)[:, :, :NOPE + ROPE]).reshape(N_DEV, Q_RANK // N_DEV, -1),
             bf(g_wukv).reshape(N_DEV, KV_RANK // N_DEV, HW)]
    grad_x, g_nin, s_win, s_wuq, s_wukv = _in_proj_bwd_x(dzs, x2, dh, row(norm_in), w_in_pad, min(256, T), slabs)
    late = [g_nin, g_qn, g_kvn]
    (tot_late,) = _reduce_scatter([], jnp.concatenate([_pack_rows(a) for a in late], axis=0))
    s_nin, s_qn, s_kvn = _unpack_rows(tot_late, late)

    grads = dict(norm_in=s_nin, w_in=s_win, q_norm=s_qn, w_uq=s_wuq, kv_norm=s_kvn, w_ukv=s_wukv, pool_w=s_pool_w.reshape(-1, GROUP),
                 pool_scale=s_pool_scale, w_branch_attn=s_wba, w_branch_pool=s_wbp, w_out=s_wout, norm_final=s_nf)
    return s_loss[0], grad_x, grads


MESH_ID = pl.DeviceIdType.MESH
VMEM_SPEC = pl.BlockSpec(memory_space=pltpu.VMEM)
HBM_SPEC = pl.BlockSpec(memory_space=pl.ANY)


def _mesh_pos():
    return lax.axis_index("x"), lax.axis_index("y"), lax.axis_index("c")


def _slot(px, py, pc):
    return 4 * px + 2 * py + pc


def _all_gather_bf16(shards):
    n = len(shards)

    def body(*refs):
        ins, outs, stage = refs[:n], refs[n:2 * n], refs[2 * n:3 * n]
        land0, send_sems, recv_sems, local_sems = refs[3 * n:]
        wpad_ref, dests = outs[0], (land0,) + tuple(outs[1:])
        x, y, c = _mesh_pos()
        me, sibling = (x, y, c), (x, y, 1 - c)
        chips = [(1 - x, y), (x, 1 - y), (1 - x, 1 - y)]

        def copy(a, k, block, to, from_stage=False):
            dst = dests[a].at[_slot(*block)]
            return pltpu.make_async_remote_copy(
                src_ref=stage[a] if from_stage else dst, dst_ref=dst, send_sem=send_sems.at[7 * a + k],
                recv_sem=recv_sems.at[7 * a + k], device_id=to, device_id_type=MESH_ID)

        for a in range(n):
            stage[a][...] = ins[a][...].astype(BF16)
        mine = [pltpu.make_async_copy(stage[a], dests[a].at[_slot(*me)], local_sems.at[a]) for a in range(n)]
        for cp in mine:
            cp.start()
        first = []
        for a in range(n):
            first.append(copy(a, 0, me, sibling, True))
            first += [copy(a, 1 + j, me, (*chip, c), True) for j, chip in enumerate(chips)]
        for cp in first:
            cp.start()
        passed = []
        for a in range(n):
            for j, chip in enumerate(chips):
                copy(a, 1 + j, (*chip, c), me).wait_recv()
                passed.append(copy(a, 4 + j, (*chip, c), sibling))
                passed[-1].start()
        for a in range(n):
            copy(a, 0, sibling, me).wait_recv()
            for j, chip in enumerate(chips):
                copy(a, 4 + j, (*chip, 1 - c), me).wait_recv()
        for cp in first + passed:
            cp.wait_send()
        for cp in mine:
            cp.wait()
        wpad_ref[ZKR:GA, :] = jnp.zeros((GA - ZKR, D_MODEL), BF16)
        for k in range(N_DEV):
            for at, dst, rows in _slab_segments(k):
                wpad_ref[dst:dst + rows, :] = land0[k, at:at + rows, :]

    return pl.pallas_call(
        body, name="all_gather_weights",
        in_specs=[VMEM_SPEC] * n, out_specs=[VMEM_SPEC] + [HBM_SPEC] * (n - 1),
        out_shape=[SDS((ZTOT, D_MODEL), BF16)] + [SDS((N_DEV,) + s.shape, BF16) for s in shards[1:]],
        scratch_shapes=[pltpu.VMEM(s.shape, BF16) for s in shards] + [pltpu.VMEM((N_DEV,) + shards[0].shape, BF16)]
        + [pltpu.SemaphoreType.DMA((7 * n,)), pltpu.SemaphoreType.DMA((7 * n,)), pltpu.SemaphoreType.DMA((n,))],
        compiler_params=_cp(),
    )(*shards)


N_CHIPS = 4


def _reduce_scatter(slabs, packed):
    def body(*refs):
        n = len(slabs)
        rs = _ReduceScatter(refs[:n], refs[n], refs[n + 1:2 * n + 1], refs[2 * n + 1], refs[2 * n + 2:])
        rs.start1()
        rs.finish1_start2()
        rs.finish2()

    shapes = [s.shape[1:] for s in slabs]
    return pl.pallas_call(
        body, name="reduce_scatter_grads",
        in_specs=[HBM_SPEC] * len(slabs) + [VMEM_SPEC], out_specs=[VMEM_SPEC] * (len(slabs) + 1),
        out_shape=[SDS(s, F32) for s in shapes] + [SDS(packed.shape, F32)],
        scratch_shapes=_rs_scratch(shapes, packed.shape), compiler_params=_cp(),
    )(*slabs, packed)


def _rs_scratch(shapes, packed_shape):
    n = len(shapes)
    n1, n2 = N_CHIPS * n + 1, (N_CHIPS - 1) * (n + 1)
    dma = pltpu.SemaphoreType.DMA
    packed = [] if packed_shape is None else [pltpu.VMEM(packed_shape, F32), pltpu.VMEM((N_CHIPS,) + tuple(packed_shape), F32)]
    return ([pltpu.VMEM((N_CHIPS,) + tuple(s), BF16) for s in shapes] * 2 + [pltpu.VMEM((N_CHIPS - 1,) + tuple(s), BF16) for s in shapes] * 2
            + packed + [dma((max(N_CHIPS * n, 1),)), dma((n1,)), dma((n1,)), dma((n2,)), dma((n2,))])


class _ReduceScatter:
    def __init__(self, slab_refs, packed_ref, out_refs, ptot_ref, scratch):
        n = self.n = len(slab_refs)
        self.slabs, self.packed, self.outs, self.ptot = slab_refs, packed_ref, out_refs, ptot_ref
        self.own1, self.land1, self.send2, self.land2 = (scratch[k * n:(k + 1) * n] for k in range(4))
        rest = scratch[4 * n:]
        if packed_ref is not None:
            self.pland1, self.pland2 = rest[:2]
            rest = rest[2:]
        self.loc_sems, self.send1_sems, self.recv1_sems, self.send2_sems, self.recv2_sems = rest
        self.x, self.y, self.c = _mesh_pos()

    def _chip(self, r):
        return (1 - self.x if r & 2 else self.x, 1 - self.y if r & 1 else self.y)

    @staticmethod
    def _remote(src, dst, send_sem, recv_sem, to):
        return pltpu.make_async_remote_copy(src_ref=src, dst_ref=dst, send_sem=send_sem, recv_sem=recv_sem, device_id=to,
                                            device_id_type=MESH_ID)

    def _copies1(self):
        c, sibling = self.c, (self.x, self.y, 1 - self.c)
        cps = []
        for a in range(self.n):
            for r in range(N_CHIPS):
                k = N_CHIPS * a + r
                cps.append(pltpu.make_async_copy(self.slabs[a].at[_slot(*self._chip(r), c)], self.own1[a].at[r], self.loc_sems.at[k]))
                cps.append(self._remote(self.slabs[a].at[_slot(*self._chip(r), 1 - c)], self.land1[a].at[r],
                                        self.send1_sems.at[k], self.recv1_sems.at[k], sibling))
        if self.packed is not None:
            k = N_CHIPS * self.n
            cps.append(self._remote(self.packed, self.pland1, self.send1_sems.at[k], self.recv1_sems.at[k], sibling))
        return cps

    def _copies2(self):
        cps = []
        for a in range(self.n):
            for r in range(1, N_CHIPS):
                k = (N_CHIPS - 1) * a + r - 1
                cps.append(self._remote(self.send2[a].at[r - 1], self.land2[a].at[r - 1], self.send2_sems.at[k], self.recv2_sems.at[k],
                                        (*self._chip(r), self.c)))
        if self.packed is not None:
            for r in range(1, N_CHIPS):
                k = (N_CHIPS - 1) * self.n + r - 1
                cps.append(self._remote(self.pland2.at[0], self.pland2.at[r], self.send2_sems.at[k], self.recv2_sems.at[k],
                                        (*self._chip(r), self.c)))
        return cps

    def start1(self):
        for cp in self._copies1():
            cp.start()

    def finish1_start2(self):
        for cp in self._copies1():
            cp.wait()
        for a in range(self.n):
            self.outs[a][...] = self.own1[a][0].astype(F32) + self.land1[a][0].astype(F32)
            for r in range(1, N_CHIPS):
                self.send2[a][r - 1] = (self.own1[a][r].astype(F32) + self.land1[a][r].astype(F32)).astype(BF16)
        if self.packed is not None:
            self.pland2[0] = self.packed[...] + self.pland1[...]
        for cp in self._copies2():
            cp.start()

    def finish2(self):
        for cp in self._copies2():
            cp.wait()
        for a in range(self.n):
            l2 = self.land2[a]
            self.outs[a][...] = self.outs[a][...] + ((l2[0].astype(F32) + l2[1].astype(F32)) + l2[2].astype(F32))
        if self.packed is not None:
            p2 = self.pland2
            self.ptot[...] = (p2[0] + p2[1]) + (p2[2] + p2[3])


def _adamw(ws, gs, ms, vs):
    n = len(ws)

    def body(*refs):
        for k in range(n):
            w, g, m, v = (refs[j * n + k][...] for j in range(4))
            d_ref, nm_ref, nv_ref = (refs[(4 + j) * n + k] for j in range(3))
            m = ADAM_B1 * m + (1.0 - ADAM_B1) * g
            v = ADAM_B2 * v + (1.0 - ADAM_B2) * jnp.square(g)
            m_hat = m / (1.0 - ADAM_B1 ** ADAM_STEP)
            v_hat = v / (1.0 - ADAM_B2 ** ADAM_STEP)
            d_ref[...] = -ADAM_LR * (m_hat / (jnp.sqrt(v_hat) + ADAM_EPS) + ADAM_WD * w)
            nm_ref[...] = m
            nv_ref[...] = v

    outs = pl.pallas_call(
        body, name="adamw", in_specs=[VMEM_SPEC] * (4 * n), out_specs=[VMEM_SPEC] * (3 * n),
        out_shape=[SDS(w.shape, F32) for w in ws] * 3, compiler_params=_cp(),
    )(*ws, *gs, *ms, *vs)
    return outs[:n], outs[n:2 * n], outs[2 * n:]


WEIGHTS = ("norm_in", "w_in", "q_norm", "w_uq", "kv_norm", "w_ukv", "pool_w", "pool_scale", "w_branch_attn", "w_branch_pool",
           "w_out", "norm_final")
SHARDED = ("w_in", "w_branch_attn", "w_branch_pool", "w_out", "w_uq", "w_ukv")
REPLICATED = ("norm_in", "q_norm", "kv_norm", "pool_scale", "norm_final", "pool_w")
SUBLANES = 8


def _cols_to_slabs(g):
    r = g.shape[0]
    return g.reshape(r, N_DEV, -1).transpose(1, 0, 2)


def _slabs_to_cols(s):
    return s.transpose(1, 0, 2).reshape(s.shape[1], -1)


def _pack_rows(a):
    a = a.reshape(-1, LANES)
    return jnp.pad(a, ((0, -a.shape[0] % SUBLANES), (0, 0)))


def _unpack_rows(packed, like):
    out, row = [], 0
    for a in like:
        rows = a.size // LANES
        out.append(packed[row:row + rows].reshape(a.shape))
        row += rows + (-rows % SUBLANES)
    return out


def kernel(x, norm_in, w_in, q_norm, w_uq, kv_norm, w_ukv, pool_w, pool_scale, w_branch_attn, w_branch_pool, w_out, norm_final, loss_target, m_norm_in, m_w_in, m_q_norm, m_w_uq, m_kv_norm, m_w_ukv, m_pool_w, m_pool_scale, m_w_branch_attn, m_w_branch_pool, m_w_out, m_norm_final, v_norm_in, v_w_in, v_q_norm, v_w_uq, v_kv_norm, v_w_ukv, v_pool_w, v_pool_scale, v_w_branch_attn, v_w_branch_pool, v_w_out, v_norm_final):
    w = dict(norm_in=norm_in, w_in=w_in, q_norm=q_norm, w_uq=w_uq, kv_norm=kv_norm, w_ukv=w_ukv, pool_w=pool_w, pool_scale=pool_scale,
             w_branch_attn=w_branch_attn, w_branch_pool=w_branch_pool, w_out=w_out, norm_final=norm_final)
    m = dict(norm_in=m_norm_in, w_in=m_w_in, q_norm=m_q_norm, w_uq=m_w_uq, kv_norm=m_kv_norm, w_ukv=m_w_ukv, pool_w=m_pool_w,
             pool_scale=m_pool_scale, w_branch_attn=m_w_branch_attn, w_branch_pool=m_w_branch_pool, w_out=m_w_out, norm_final=m_norm_final)
    v = dict(norm_in=v_norm_in, w_in=v_w_in, q_norm=v_q_norm, w_uq=v_w_uq, kv_norm=v_kv_norm, w_ukv=v_w_ukv, pool_w=v_pool_w,
             pool_scale=v_pool_scale, w_branch_attn=v_w_branch_attn, w_branch_pool=v_w_branch_pool, w_out=v_w_out, norm_final=v_norm_final)

    def as2d(name, a):
        if name == "w_in":
            return a.T
        if name in ("w_uq", "w_ukv"):
            return a.reshape(a.shape[0], -1)
        if name == "pool_w":
            return a.reshape(-1, GROUP)
        return a.reshape(1, -1) if a.ndim == 1 else a

    def unshape(name, a):
        return a.T if name == "w_in" else a.reshape(w[name].shape)

    full = dict(zip(SHARDED, _all_gather_bf16([as2d(k, w[k]) for k in SHARDED])))
    loss, grad_x, g2d = _local_step(
        x.reshape(x.shape[1:]), loss_target.reshape(x.shape[1:]), norm_in, full["w_in"], q_norm,
        full["w_uq"].reshape(Q_RANK, HEADS, NOPE + ROPE), kv_norm, full["w_ukv"].reshape(KV_RANK, HEADS, NOPE + VDIM),
        pool_w, pool_scale, _slabs_to_cols(full["w_branch_attn"]), _slabs_to_cols(full["w_branch_pool"]),
        full["w_out"].reshape(D_MODEL, D_MODEL), norm_final)

    deltas, new_m, new_v = _adamw([as2d(k, w[k]) for k in WEIGHTS], [g2d[k] for k in WEIGHTS],
                                  [as2d(k, m[k]) for k in WEIGHTS], [as2d(k, v[k]) for k in WEIGHTS])
    shaped = lambda arrs: [unshape(k, a) for k, a in zip(WEIGHTS, arrs)]
    return (loss, grad_x.reshape(x.shape), *shaped([g2d[k] for k in WEIGHTS]), *shaped(deltas), *shaped(new_m), *shaped(new_v))
```

```python
import functools

import jax
import jax.numpy as jnp
import numpy as np
from jax import lax
from jax.experimental import pallas as pl
from jax.experimental.pallas import tpu as pltpu

F32 = jnp.float32
BF16 = jnp.bfloat16
SDS = jax.ShapeDtypeStruct

D_MODEL = 1024
HEADS = 8
NOPE = 64
ROPE = 32
VDIM = 64
Q_RANK = 384
KV_RANK = 256
MLA_W = HEADS * VDIM
POOL_W = 512
POOL_GROUPS = 4
GROUP = POOL_W // POOL_GROUPS
CHUNK = 64
ROPE_THETA = 10000.0
EPS = 1e-6
SCALE = (NOPE + ROPE) ** -0.5
LOG2E = 1.4426950408889634
LN2 = 0.6931471805599453
QK_SCALE_LOG2 = SCALE * LOG2E
IN_TOTAL = 4256
ADAM_LR, ADAM_B1, ADAM_B2, ADAM_EPS, ADAM_WD, ADAM_STEP = 0.001, 0.9, 0.999, 1e-08, 0.01, 10

N_DEV = 8
LANES = 128
HEAD_PAD = LANES
HW = HEADS * HEAD_PAD

ZQ, ZKV, ZKR, GA, UP, GP, GM, ZTOT = 0, 384, 640, 768, 1280, 1792, 2304, 4352
FRONT_W = GA
ZKR_ORIG = 640

VMEM_LIMIT = 56 * 1024 * 1024


def _cp(sem=None, **kw):
    if sem is not None:
        kw["dimension_semantics"] = sem
    return pltpu.CompilerParams(vmem_limit_bytes=VMEM_LIMIT, **kw)


def _mm(a, b):
    return lax.dot_general(a, b, (((1,), (0,)), ((), ())), preferred_element_type=F32)


def _mm_nt(a, b):
    return lax.dot_general(a, b, (((1,), (1,)), ((), ())), preferred_element_type=F32)


def _mm_tn(a, b):
    return lax.dot_general(a, b, (((0,), (0,)), ((), ())), preferred_element_type=F32)


def _row_spec(tm, w):
    return pl.BlockSpec((tm, w), lambda i: (i, 0))


def _full_spec(a):
    nd = len(a.shape)
    return pl.BlockSpec(a.shape, lambda *_: (0,) * nd)


def _rope(v, c, sa, sb, sign):
    n = v.shape[-1]
    reps = n // LANES
    if reps > 1:
        c, sa, sb = (jnp.tile(t, (1, reps)) for t in (c, sa, sb))
    up = pltpu.roll(v, n - ROPE // 2, 1)
    dn = pltpu.roll(v, ROPE // 2, 1)
    return v * c + sign * (up * sa + dn * sb)


def _rope_tables(T):
    half = ROPE // 2
    inv_freq = np.float32(ROPE_THETA) ** (-np.arange(half, dtype=np.float32) / np.float32(half))
    ang = np.arange(T, dtype=np.float32)[:, None] * inv_freq[None, :].astype(np.float32)
    cos, sin = np.cos(ang.astype(np.float64)).astype(np.float32), np.sin(ang.astype(np.float64)).astype(np.float32)
    z16 = np.zeros((T, half), np.float32)
    z32 = np.zeros((T, LANES - NOPE - ROPE), np.float32)
    c = np.concatenate([np.ones((T, NOPE), np.float32), cos, cos, z32], axis=1)
    sa = np.concatenate([np.zeros((T, NOPE), np.float32), -sin, z16, z32], axis=1)
    sb = np.concatenate([np.zeros((T, NOPE), np.float32), z16, sin, z32], axis=1)
    return jnp.asarray(c), jnp.asarray(sa), jnp.asarray(sb)


def _silu_parts(g):
    sg = jax.nn.sigmoid(g)
    return g * sg, sg + g * sg * (1.0 - sg)


def _in_proj(x2, norm_in, w_in_pad, q_norm, wuq_pad, kv_norm, wukv, rc, rsa, rsb, tm, late_shards):
    T = x2.shape[0]
    steps = T // tm
    n = len(late_shards)

    def body(x_ref, nin_ref, win_ref, qn_ref, wuq_ref, kvn_ref, wukv_ref, c_ref, sa_ref, sb_ref, *rest):
        hn_ref, zgm_ref, zga_ref, zup_ref, zgp_ref, zfr_ref, q_ref, k_ref, v_ref = rest[n:n + 9]
        ag = _AllGather(rest[:n], rest[n + 9:2 * n + 9], rest[2 * n + 9:])
        step = pl.program_id(0)
        pl.when(step == 0)(ag.start)
        pl.when(step == min(3, steps - 1))(ag.forward)
        xf = x_ref[...]
        r = lax.rsqrt(jnp.mean(xf * xf, axis=-1, keepdims=True) + EPS)
        hn = (xf * r * nin_ref[...]).astype(BF16)
        hn_ref[...] = hn
        z = _mm_nt(hn, win_ref[...])
        zgm_ref[...] = z[:, GM:ZTOT]
        zga_ref[...] = z[:, GA:UP]
        zup_ref[...] = z[:, UP:GP]
        zgp_ref[...] = z[:, GP:GM]
        zfr_ref[...] = z[:, ZQ:GA]
        zq, zkv, zkr = z[:, ZQ:ZKV], z[:, ZKV:ZKR], z[:, ZKR:GA]
        c, sa, sb = c_ref[...], sa_ref[...], sb_ref[...]
        rq = lax.rsqrt(jnp.mean(zq * zq, axis=-1, keepdims=True) + EPS)
        cq = (zq * rq * qn_ref[...]).astype(BF16)
        q = _rope(_mm(cq, wuq_ref[...]), c, sa, sb, 1.0)
        q_ref[...] = (q * QK_SCALE_LOG2).astype(BF16)
        rkv = lax.rsqrt(jnp.mean(zkv * zkv, axis=-1, keepdims=True) + EPS)
        ckv = (zkv * rkv * kvn_ref[...]).astype(BF16)
        kv = _mm(ckv, wukv_ref[...])
        kr = _rope(zkr, c, sa, sb, 1.0)
        lane = lax.broadcasted_iota(jnp.int32, kv.shape, 1) % LANES
        k_ref[...] = jnp.where(lane < NOPE, kv, jnp.tile(kr, (1, HEADS))).astype(BF16)
        v_ref[...] = jnp.where(lane < NOPE, 1.0, kv).astype(BF16)
        pl.when(step == steps - 1)(ag.finish)

    ins = (x2, norm_in, w_in_pad, q_norm, wuq_pad, kv_norm, wukv, rc, rsa, rsb)
    in_specs = [_row_spec(tm, D_MODEL), _full_spec(norm_in), _full_spec(w_in_pad), _full_spec(q_norm), _full_spec(wuq_pad),
                _full_spec(kv_norm), _full_spec(wukv), _row_spec(tm, LANES), _row_spec(tm, LANES), _row_spec(tm, LANES)]
    widths = [(D_MODEL, BF16), (ZTOT - GM, F32), (UP - GA, F32), (GP - UP, F32), (GM - GP, F32), (FRONT_W, F32),
              (HW, BF16), (HW, BF16), (HW, BF16)]
    return pl.pallas_call(
        body, name="in_proj", grid=(steps,), in_specs=in_specs + [_full_spec(s) for s in late_shards],
        out_specs=[_row_spec(tm, w) for w, _ in widths] + [HBM_SPEC] * n,
        out_shape=[SDS((T, w), dt) for w, dt in widths] + [SDS((N_DEV,) + s.shape, BF16) for s in late_shards],
        scratch_shapes=_ag_scratch([s.shape for s in late_shards]), compiler_params=_cp(("arbitrary",)),
    )(*ins, *late_shards)


def _diag_mask(tq, transposed):
    r = lax.broadcasted_iota(jnp.int32, (tq, tq), 0) // CHUNK
    c = lax.broadcasted_iota(jnp.int32, (tq, tq), 1) // CHUNK
    return (r <= c) if transposed else (c <= r)


_HEAD_LANES = (slice(0, LANES), slice(LANES, 2 * LANES))


def _pair_rows_spec(tq):
    return pl.BlockSpec((1, 2, tq), lambda p, i: (p, 0, i))


def _store_pair_rows(ref, k, pair):
    t = pair.T
    ref[k, 0:1, :] = t[0:1, :]
    ref[k, 1:2, :] = t[VDIM:VDIM + 1, :]


def _attn_fwd(q_att, k_att, v_att, tq, hps):
    T = q_att.shape[0]
    head_lanes = [slice(h * LANES, (h + 1) * LANES) for h in range(hps)]

    def body(q_ref, k_ref, v_ref, o_ref, lser_ref):
        i = pl.program_id(1)
        mask = _diag_mask(tq, False)
        lane = lax.broadcasted_iota(jnp.int32, (tq, LANES), 1)
        qs = [q_ref[:, hs] for hs in head_lanes]

        def step(j, carry, masked):
            rows = pl.ds(pl.multiple_of(j * tq, tq), tq)
            out = []
            for (m, acc), qh, hs in zip(carry, qs, head_lanes):
                s = _mm_nt(qh, k_ref[rows, hs])
                if masked:
                    s = jnp.where(mask, s, -jnp.inf)
                m_new = jnp.maximum(m, jnp.max(s, axis=-1, keepdims=True))
                p = jnp.exp2(s - m_new).astype(BF16)
                out.append((m_new, jnp.exp2(m - m_new) * acc + _mm(p, v_ref[rows, hs])))
            return tuple(out)

        init = ((jnp.full((tq, 1), -jnp.inf, F32), jnp.zeros((tq, LANES), F32)),) * hps
        res = step(i, lax.fori_loop(0, i, functools.partial(step, masked=False), init), True)
        for pair in range(hps // 2):
            (ma, acca), (mb, accb) = res[2 * pair], res[2 * pair + 1]
            la, lb = acca[:, :1], accb[:, :1]
            o_ref[:, pair * LANES:(pair + 1) * LANES] = jnp.where(lane < VDIM, pltpu.roll(acca / la, VDIM, 1), accb / lb)
            _store_pair_rows(lser_ref, pair, jnp.where(lane < VDIM, ma + jnp.log2(la), mb + jnp.log2(lb)))

    qspec = pl.BlockSpec((tq, hps * LANES), lambda p, i: (i, p))
    kspec = pl.BlockSpec((T, hps * LANES), lambda p, i: (0, p))
    ospec = pl.BlockSpec((tq, hps * VDIM), lambda p, i: (i, p))
    return pl.pallas_call(
        body, name="attn_fwd", grid=(HEADS // hps, T // tq), in_specs=[qspec, kspec, kspec],
        out_specs=[ospec, pl.BlockSpec((hps // 2, 2, tq), lambda p, i: (p, 0, i))],
        out_shape=[SDS((T, MLA_W), F32), SDS((HEADS // 2, 2, T), F32)],
        compiler_params=_cp(("parallel", "parallel")),
    )(q_att, k_att, v_att)


def _pick(g, vals):
    out = vals[-1]
    for k in range(len(vals) - 2, -1, -1):
        out = jnp.where(g == k, vals[k], out)
    return out


def _window_sum(u, g, forward):
    T = u.shape[0]
    row = lax.broadcasted_iota(jnp.int32, u.shape, 0)

    def sh(s, k):
        if forward:
            return jnp.where(row >= k, pltpu.roll(s, k, 0), 0.0)
        return jnp.where(row < T - k, pltpu.roll(s, T - k, 0), 0.0)

    sums, s = [], u
    for k in (1, 2, 4, 8):
        s = s + sh(s, k)
        sums.append(s)
    return _pick(g, sums)


def _pool_count(shape, g):
    row = lax.broadcasted_iota(jnp.int32, shape, 0)
    return jnp.minimum(row + 1, lax.shift_left(jnp.int32(2), g)).astype(F32)


def _pool_fwd(zup, zgp, pool_w, pool_scale):
    T = zup.shape[0]

    def body(u_ref, g_ref, w_ref, sc_ref, y_ref):
        g = pl.program_id(0)
        u = u_ref[...]
        d = _window_sum(u, g, True) / _pool_count(u.shape, g) - u
        lin = _mm(d.astype(BF16), w_ref[0].astype(BF16))
        silu, _ = _silu_parts(g_ref[...])
        y_ref[...] = (lin * sc_ref[...] * silu).astype(BF16)

    col = pl.BlockSpec((T, GROUP), lambda g: (0, g))
    return pl.pallas_call(
        body, name="pool_fwd", grid=(POOL_GROUPS,),
        in_specs=[col, col, pl.BlockSpec((1, GROUP, GROUP), lambda g: (g, 0, 0)), pl.BlockSpec((1, GROUP), lambda g: (0, g))],
        out_specs=col, out_shape=SDS((T, POOL_W), BF16), compiler_params=_cp(("parallel",)),
    )(zup, zgp, pool_w, pool_scale)


def _pool_bwd(zup, zgp, dyp, pool_w, pool_scale):
    T = zup.shape[0]

    def body(u_ref, g_ref, dy_ref, w_ref, sc_ref, du_ref, dg_ref, gw_ref, gsc_ref):
        g = pl.program_id(0)
        u = u_ref[...]
        cnt = _pool_count(u.shape, g)
        d = (_window_sum(u, g, True) / cnt - u).astype(BF16)
        wb = w_ref[0].astype(BF16)
        lin = _mm(d, wb)
        sc = sc_ref[...]
        silu, dsilu = _silu_parts(g_ref[...])
        dy = dy_ref[...]
        dg_ref[...] = (dy * lin * sc * dsilu).astype(BF16)
        dpre = dy * silu
        gsc_ref[...] = jnp.sum(dpre * lin, axis=0, keepdims=True)
        dlin = (dpre * sc).astype(BF16)
        gw_ref[0] = _mm_tn(d, dlin)
        dd = _mm_nt(dlin, wb)
        du_ref[...] = (_window_sum(dd / cnt, g, False) - dd).astype(BF16)

    col = pl.BlockSpec((T, GROUP), lambda g: (0, g))
    wspec = pl.BlockSpec((1, GROUP, GROUP), lambda g: (g, 0, 0))
    vspec = pl.BlockSpec((1, GROUP), lambda g: (0, g))
    return pl.pallas_call(
        body, name="pool_bwd", grid=(POOL_GROUPS,), in_specs=[col, col, col, wspec, vspec], out_specs=[col, col, wspec, vspec],
        out_shape=[SDS((T, POOL_W), BF16), SDS((T, POOL_W), BF16), SDS((POOL_GROUPS, GROUP, GROUP), F32), SDS((1, POOL_W), F32)],
        compiler_params=_cp(("parallel",)),
    )(zup, zgp, dyp, pool_w, pool_scale)


def _tail(x2, tgt, o, zga, ypool, zgm, wba, wbp, wout, norm_final, tm):
    T = x2.shape[0]

    def body(x_ref, tgt_ref, o_ref, zga_ref, yp_ref, zgm_ref, wba_ref, wbp_ref, wout_ref, nf_ref,
             loss_ref, dh_ref, dgm_ref, doop_ref, dga_ref, dcapr_ref, dyp_ref, gwout_ref, gwba_ref, gwbp_ref, gnf_ref):
        @pl.when(pl.program_id(0) == 0)
        def _():
            for ref in (loss_ref, gwout_ref, gwba_ref, gwbp_ref, gnf_ref):
                ref[...] = jnp.zeros_like(ref)

        o_v = o_ref[...]
        silu, dsilu = _silu_parts(zga_ref[...])
        ya = (o_v * silu).astype(BF16)
        yp = yp_ref[...]
        wba_v, wbp_v, wout_v = wba_ref[...], wbp_ref[...], wout_ref[...]
        a = _mm(ya, wba_v)
        p = _mm(yp, wbp_v)
        gate = jax.nn.sigmoid(zgm_ref[...])
        ga, gp = gate[:, :D_MODEL], gate[:, D_MODEL:]
        mg = (ga * a + gp * p).astype(BF16)
        h = x_ref[...] + _mm(mg, wout_v)
        r = lax.rsqrt(jnp.mean(h * h, axis=-1, keepdims=True) + EPS)
        gf = nf_ref[...]
        hr = h * r
        e = hr * gf - tgt_ref[...]
        loss_ref[...] += (0.5 / D_MODEL) * jnp.sum(e * e)
        dy = e * (1.0 / D_MODEL)
        gnf_ref[...] += jnp.sum(dy * hr, axis=0, keepdims=True)
        u = dy * gf
        dh = r * (u - hr * jnp.mean(u * hr, axis=-1, keepdims=True))
        dh_ref[...] = dh
        dhb = dh.astype(BF16)
        dmg = _mm_nt(dhb, wout_v)
        gwout_ref[...] += _mm_tn(mg, dhb)
        dgm_ref[:, :D_MODEL] = (dmg * a * ga * (1.0 - ga)).astype(BF16)
        dgm_ref[:, D_MODEL:] = (dmg * p * gp * (1.0 - gp)).astype(BF16)
        dab = (dmg * ga).astype(BF16)
        dpb = (dmg * gp).astype(BF16)
        dya = _mm_nt(dab, wba_v)
        gwba_ref[...] += _mm_tn(ya, dab)
        dyp_ref[...] = _mm_nt(dpb, wbp_v)
        gwbp_ref[...] += _mm_tn(yp, dpb)
        do = dya * silu
        dga_ref[...] = (dya * o_v * dsilu).astype(BF16)
        prod = do * o_v
        lo = lax.broadcasted_iota(jnp.int32, (tm, LANES), 1) < VDIM
        for pair in range(HEADS // 2):
            ls = slice(pair * LANES, (pair + 1) * LANES)
            do_p, prod_p = do[:, ls], prod[:, ls]
            dcap_a = jnp.sum(jnp.where(lo, prod_p, 0.0), axis=-1, keepdims=True)
            dcap_b = jnp.sum(jnp.where(lo, 0.0, prod_p), axis=-1, keepdims=True)
            _store_pair_rows(dcapr_ref, pair, jnp.where(lo, dcap_a, dcap_b))
            doop_ref[:, 2 * pair * LANES:(2 * pair + 1) * LANES] = jnp.where(lo, 0.0, pltpu.roll(do_p, VDIM, 1)).astype(BF16)
            doop_ref[:, (2 * pair + 1) * LANES:(2 * pair + 2) * LANES] = jnp.where(lo, 0.0, do_p).astype(BF16)

    ins = (x2, tgt, o, zga, ypool, zgm, wba, wbp, wout, norm_final)
    in_specs = [_row_spec(tm, D_MODEL), _row_spec(tm, D_MODEL), _row_spec(tm, MLA_W), _row_spec(tm, MLA_W), _row_spec(tm, POOL_W),
                _row_spec(tm, 2 * D_MODEL), _full_spec(wba), _full_spec(wbp), _full_spec(wout), _full_spec(norm_final)]
    outs = [SDS((8, LANES), F32), SDS((T, D_MODEL), F32), SDS((T, 2 * D_MODEL), BF16), SDS((T, HW), BF16), SDS((T, MLA_W), BF16),
            SDS((HEADS // 2, 2, T), F32), SDS((T, POOL_W), F32),
            SDS((D_MODEL, D_MODEL), F32), SDS((MLA_W, D_MODEL), F32), SDS((POOL_W, D_MODEL), F32), SDS((1, D_MODEL), F32)]
    out_specs = [_full_spec(outs[0]), _row_spec(tm, D_MODEL), _row_spec(tm, 2 * D_MODEL), _row_spec(tm, HW), _row_spec(tm, MLA_W),
                 pl.BlockSpec((HEADS // 2, 2, tm), lambda i: (0, 0, i)), _row_spec(tm, POOL_W),
                 _full_spec(outs[7]), _full_spec(outs[8]), _full_spec(outs[9]), _full_spec(outs[10])]
    return pl.pallas_call(
        body, name="tail", grid=(T // tm,), in_specs=in_specs, out_specs=out_specs, out_shape=outs,
        compiler_params=_cp(("arbitrary",)),
    )(*ins)


def _attn_bwd(q_att, k_att, v_att, doop, lse_rows, dcap_rows, tq, slabs, packed):
    T = q_att.shape[0]
    nq = T // tq
    n = len(slabs)
    n_pairs = HEADS // 2

    def body(q_ref, k_ref, v_ref, doop_ref, lse_ref, dcap_ref, *rest):
        slab_refs, packed_ref = rest[:n], rest[n]
        dq_ref, dkv_ref, dkr_ref = rest[n + 1:n + 4]
        sum_refs, ptot_ref = rest[n + 4:2 * n + 4], rest[2 * n + 4]
        dq_acc = rest[2 * n + 5]
        rs = _ReduceScatter(slab_refs, packed_ref, sum_refs, ptot_ref, rest[2 * n + 6:])
        pair, j = pl.program_id(0), pl.program_id(1)
        pl.when((pair == 0) & (j == 0))(rs.start1)
        pl.when((pair == 1) & (j == 0))(rs.finish1_start2)
        mask = _diag_mask(tq, True)
        lane = lax.broadcasted_iota(jnp.int32, (tq, LANES), 1)
        ks = [k_ref[:, hs] for hs in _HEAD_LANES]
        vs = [v_ref[:, hs] for hs in _HEAD_LANES]

        @pl.when(j == 0)
        def _():
            dq_acc[...] = jnp.zeros_like(dq_acc)

        def step(i, carry, masked):
            rows = pl.ds(pl.multiple_of(i * tq, tq), tq)
            out = []
            for h, ((dk, dv), kh, vh, hs) in enumerate(zip(carry, ks, vs, _HEAD_LANES)):
                qh = q_ref[rows, hs]
                doop_h = doop_ref[rows, hs]
                pt = jnp.exp2(_mm_nt(kh, qh) - lse_ref[0, h:h + 1, rows])
                if masked:
                    pt = jnp.where(mask, pt, 0.0)
                dv = dv + _mm(pt.astype(BF16), doop_h)
                dpt = _mm_nt(vh, doop_h)
                dst = (pt * (dpt - dcap_ref[0, h:h + 1, rows])).astype(BF16)
                dq_acc[rows, hs] += _mm_tn(dst, kh)
                out.append((dk + _mm(dst, qh), dv))
            return tuple(out)

        zero = jnp.zeros((tq, LANES), F32)
        carry = step(j, ((zero, zero), (zero, zero)), True)
        (dka, dva), (dkb, dvb) = lax.fori_loop(j + 1, nq, functools.partial(step, masked=False), carry)
        dka, dkb = dka * LN2, dkb * LN2
        dkv_ref[:, _HEAD_LANES[0]] = jnp.where(lane < NOPE, dka, dva).astype(BF16)
        dkv_ref[:, _HEAD_LANES[1]] = jnp.where(lane < NOPE, dkb, dvb).astype(BF16)
        dkr_ref[0] = jnp.where((lane >= NOPE) & (lane < NOPE + ROPE), dka + dkb, 0.0)

        @pl.when(j == nq - 1)
        def _():
            dq_ref[...] = (dq_acc[...] * SCALE).astype(BF16)

        pl.when((pair == n_pairs - 1) & (j == nq - 1))(rs.finish2)

    kspec = pl.BlockSpec((tq, 2 * LANES), lambda p, j: (j, p))
    qspec = pl.BlockSpec((T, 2 * LANES), lambda p, j: (0, p))
    rspec = pl.BlockSpec((1, 2, T), lambda p, j: (p, 0, 0))
    sums = [SDS(s.shape[1:], F32) for s in slabs] + [SDS(packed.shape, F32)]
    return pl.pallas_call(
        body, name="attn_bwd", grid=(n_pairs, nq),
        in_specs=[qspec, kspec, kspec, qspec, rspec, rspec] + [HBM_SPEC] * n + [_full_spec(packed)],
        out_specs=[qspec, kspec, pl.BlockSpec((1, tq, LANES), lambda p, j: (p, j, 0))] + [_full_spec(s) for s in sums],
        out_shape=[SDS((T, HW), BF16), SDS((T, HW), BF16), SDS((n_pairs, T, LANES), F32)] + sums,
        scratch_shapes=[pltpu.VMEM((T, 2 * LANES), F32)] + _rs_scratch([s.shape for s in sums[:-1]], packed.shape),
        compiler_params=_cp(("arbitrary", "arbitrary")),
    )(q_att, k_att, v_att, doop, lse_rows, dcap_rows, *slabs, packed)


def _rms_bwd(z, gain, dout):
    r = lax.rsqrt(jnp.mean(z * z, axis=-1, keepdims=True) + EPS)
    zr = z * r
    u = dout * gain
    return r * (u - zr * jnp.mean(u * zr, axis=-1, keepdims=True)), jnp.sum(dout * zr, axis=0, keepdims=True)


def _mla_bwd(dq_att, dkv_nat, dkr4, zfr, q_norm, wuq_pad, kv_norm, wukv, rc, rsa, rsb, tm):
    T = dq_att.shape[0]

    def body(dq_ref, dkv_ref, dkr_ref, zfr_ref, qn_ref, wuq_ref, kvn_ref, wukv_ref, c_ref, sa_ref, sb_ref,
             dfr_ref, gwuq_ref, gwukv_ref, gqn_ref, gkvn_ref):
        @pl.when(pl.program_id(0) == 0)
        def _():
            for ref in (gwuq_ref, gwukv_ref, gqn_ref, gkvn_ref):
                ref[...] = jnp.zeros_like(ref)

        c, sa, sb = c_ref[...], sa_ref[...], sb_ref[...]
        zq, zkv = zfr_ref[:, :Q_RANK], zfr_ref[:, Q_RANK:Q_RANK + KV_RANK]
        qn, kvn = qn_ref[...], kvn_ref[...]
        cq = (zq * lax.rsqrt(jnp.mean(zq * zq, axis=-1, keepdims=True) + EPS) * qn).astype(BF16)
        ckv = (zkv * lax.rsqrt(jnp.mean(zkv * zkv, axis=-1, keepdims=True) + EPS) * kvn).astype(BF16)
        dq = _rope(dq_ref[...].astype(F32), c, sa, sb, -1.0).astype(BF16)
        gwuq_ref[...] += _mm_tn(cq, dq)
        dzq, gqn = _rms_bwd(zq, qn, _mm_nt(dq, wuq_ref[...]))
        gqn_ref[...] += gqn
        dkv = dkv_ref[...]
        gwukv_ref[...] += _mm_tn(ckv, dkv)
        dzkv, gkvn = _rms_bwd(zkv, kvn, _mm_nt(dkv, wukv_ref[...]))
        gkvn_ref[...] += gkvn
        dkr = dkr_ref[0] + dkr_ref[1] + dkr_ref[2] + dkr_ref[3]
        dfr_ref[:, :Q_RANK] = dzq.astype(BF16)
        dfr_ref[:, Q_RANK:Q_RANK + KV_RANK] = dzkv.astype(BF16)
        dfr_ref[:, Q_RANK + KV_RANK:] = _rope(dkr, c, sa, sb, -1.0).astype(BF16)

    ins = (dq_att, dkv_nat, dkr4, zfr, q_norm, wuq_pad, kv_norm, wukv, rc, rsa, rsb)
    in_specs = [_row_spec(tm, HW), _row_spec(tm, HW), pl.BlockSpec((HEADS // 2, tm, LANES), lambda i: (0, i, 0)), _row_spec(tm, FRONT_W),
                _full_spec(q_norm), _full_spec(wuq_pad), _full_spec(kv_norm), _full_spec(wukv),
                _row_spec(tm, LANES), _row_spec(tm, LANES), _row_spec(tm, LANES)]
    outs = [SDS((T, FRONT_W), BF16), SDS((Q_RANK, HW), F32), SDS((KV_RANK, HW), F32), SDS((1, Q_RANK), F32), SDS((1, KV_RANK), F32)]
    out_specs = [_row_spec(tm, FRONT_W)] + [_full_spec(s) for s in outs[1:]]
    return pl.pallas_call(
        body, name="mla_bwd", grid=(T // tm,), in_specs=in_specs, out_specs=out_specs, out_shape=outs,
        compiler_params=_cp(("arbitrary",)),
    )(*ins)


_DZ_COLS = ((GM, ZTOT), (GA, UP), (UP, GP), (GP, GM), (ZQ, GA))


def _in_proj_bwd_x(dzs, x2, dh, norm_in, w_in_pad, tm, slabs):
    T = x2.shape[0]
    steps = T // tm
    n = len(slabs)

    def body(d0, d1, d2, d3, d4, x_ref, dh_ref, nin_ref, win_ref, *rest):
        slab_refs, (gx_ref, gnin_ref), sum_refs = rest[:n], rest[n:n + 2], rest[n + 2:2 * n + 2]
        rs = _ReduceScatter(slab_refs, None, sum_refs, None, rest[2 * n + 2:])
        step = pl.program_id(0)

        @pl.when(step == 0)
        def _():
            gnin_ref[...] = jnp.zeros_like(gnin_ref)
            rs.start1()

        pl.when(step == min(2, steps - 1))(rs.finish1_start2)
        dhn = None
        for ref, (lo, hi) in zip((d0, d1, d2, d3, d4), _DZ_COLS):
            t = _mm(ref[...], win_ref[lo:hi, :])
            dhn = t if dhn is None else dhn + t
        dx, gnin = _rms_bwd(x_ref[...], nin_ref[...], dhn)
        gnin_ref[...] += gnin
        gx_ref[...] = dx + dh_ref[...]
        pl.when(step == steps - 1)(rs.finish2)

    in_specs = [_row_spec(tm, hi - lo) for lo, hi in _DZ_COLS] + [_row_spec(tm, D_MODEL), _row_spec(tm, D_MODEL),
                                                                  _full_spec(norm_in), _full_spec(w_in_pad)] + [HBM_SPEC] * n
    sums = [SDS(s.shape[1:], F32) for s in slabs]
    outs = [SDS((T, D_MODEL), F32), SDS((1, D_MODEL), F32)] + sums
    return pl.pallas_call(
        body, name="in_proj_bwd_x", grid=(steps,), in_specs=in_specs,
        out_specs=[_row_spec(tm, D_MODEL), _full_spec(outs[1])] + [_full_spec(s) for s in sums],
        out_shape=outs, scratch_shapes=_rs_scratch([s.shape for s in sums], None), compiler_params=_cp(("arbitrary",)),
    )(*dzs, x2, dh, norm_in, w_in_pad, *slabs)


SLAB_ROWS = IN_TOTAL // N_DEV


def _slab_segments(k):
    cuts = [(0, ZKR_ORIG, 0), (ZKR_ORIG, ZKR_ORIG + ROPE, NOPE), (ZKR_ORIG + ROPE, IN_TOTAL, LANES - ROPE)]
    lo, hi = k * SLAB_ROWS, (k + 1) * SLAB_ROWS
    return [(max(lo, a) - lo, max(lo, a) + shift, min(hi, b) - max(lo, a)) for a, b, shift in cuts if min(hi, b) > max(lo, a)]


def _in_proj_bwd_w(dzs, hn, tm):
    T = hn.shape[0]
    steps = T // tm

    def body(d0, d1, d2, d3, d4, hn_ref, slab_ref, acc_ref):
        @pl.when(pl.program_id(0) == 0)
        def _():
            acc_ref[...] = jnp.zeros_like(acc_ref)

        hn_v = hn_ref[...]
        for ref, (lo, hi) in zip((d0, d1, d2, d3, d4), _DZ_COLS):
            acc_ref[lo:hi, :] += _mm_tn(ref[...], hn_v)

        @pl.when(pl.program_id(0) == steps - 1)
        def _():
            for k in range(N_DEV):
                for at, src, rows in _slab_segments(k):
                    slab_ref[k, at:at + rows, :] = acc_ref[src:src + rows, :].astype(BF16)

    in_specs = [_row_spec(tm, hi - lo) for lo, hi in _DZ_COLS] + [_row_spec(tm, D_MODEL)]
    out = SDS((N_DEV, SLAB_ROWS, D_MODEL), BF16)
    return pl.pallas_call(
        body, name="in_proj_bwd_w", grid=(steps,), in_specs=in_specs, out_specs=_full_spec(out), out_shape=out,
        scratch_shapes=[pltpu.VMEM((ZTOT, D_MODEL), F32)], compiler_params=_cp(("arbitrary",)),
    )(*dzs, hn)


def _local_step(x2, tgt, norm_in, w_in_pad, q_norm, w_uq, kv_norm, w_ukv, pool_w, pool_scale, late_shards, norm_final):
    T = x2.shape[0]
    tm = min(512, T)
    tq = min(512, T)
    row = lambda v: v.reshape(1, -1)
    wuq_pad = jnp.pad(w_uq, ((0, 0), (0, 0), (0, HEAD_PAD - NOPE - ROPE))).reshape(Q_RANK, HW)
    wukv = w_ukv.reshape(KV_RANK, HW)
    rc, rsa, rsb = _rope_tables(T)

    hn, zgm, zga, zup, zgp, zfr, q_att, k_att, v_att, w_ba, w_bp, w_out = _in_proj(
        x2, row(norm_in), w_in_pad, row(q_norm), wuq_pad, row(kv_norm), wukv, rc, rsa, rsb, tm, late_shards)
    w_ba, w_bp, w_out = _slabs_to_cols(w_ba), _slabs_to_cols(w_bp), w_out.reshape(D_MODEL, D_MODEL)
    o, lse_rows = _attn_fwd(q_att, k_att, v_att, tq, 4)
    ypool = _pool_fwd(zup, zgp, pool_w, row(pool_scale))
    loss8, dh, dgm, doop, dga, dcap_rows, dyp, g_wout, g_wba, g_wbp, g_nf = _tail(
        x2, tgt, o, zga, ypool, zgm, w_ba, w_bp, w_out, row(norm_final), min(256, T))
    dup, dgp, g_pool_w, g_pool_scale = _pool_bwd(zup, zgp, dyp, pool_w, row(pool_scale))

    bf = lambda a: a.astype(BF16)
    slabs = [bf(g_wout).reshape(N_DEV, D_MODEL // N_DEV, D_MODEL), _cols_to_slabs(bf(g_wba)), _cols_to_slabs(bf(g_wbp))]
    early = [g_pool_w, g_pool_scale, g_nf, loss8[0]]
    packed = jnp.concatenate([_pack_rows(a) for a in early], axis=0)
    dq_att, dkv_nat, dkr4, s_wout, s_wba, s_wbp, tot_early = _attn_bwd(
        q_att, k_att, v_att, doop, lse_rows, dcap_rows, tq, slabs, packed)
    s_pool_w, s_pool_scale, s_nf, s_loss = _unpack_rows(tot_early, early)

    dfr, g_wuq_pad, g_wukv, g_qn, g_kvn = _mla_bwd(
        dq_att, dkv_nat, dkr4, zfr, row(q_norm), wuq_pad, row(kv_norm), wukv, rc, rsa, rsb, tm)
    dzs = (dgm, dga, dup, dgp, dfr)
    slabs = [_in_proj_bwd_w(dzs, hn, tm),
             bf(g_wuq_pad.reshape(Q_RANK, HEADS, HEAD_PAD)[:, :, :NOPE + ROPE]).reshape(N_DEV, Q_RANK // N_DEV, -1),
             bf(g_wukv).reshape(N_DEV, KV_RANK // N_DEV, HW)]
    grad_x, g_nin, s_win, s_wuq, s_wukv = _in_proj_bwd_x(dzs, x2, dh, row(norm_in), w_in_pad, min(256, T), slabs)
    late = [g_nin, g_qn, g_kvn]
    (tot_late,) = _reduce_scatter([], jnp.concatenate([_pack_rows(a) for a in late], axis=0))
    s_nin, s_qn, s_kvn = _unpack_rows(tot_late, late)

    grads = dict(norm_in=s_nin, w_in=s_win, q_norm=s_qn, w_uq=s_wuq, kv_norm=s_kvn, w_ukv=s_wukv, pool_w=s_pool_w.reshape(-1, GROUP),
                 pool_scale=s_pool_scale, w_branch_attn=s_wba, w_branch_pool=s_wbp, w_out=s_wout, norm_final=s_nf)
    return s_loss[0], grad_x, grads


MESH_ID = pl.DeviceIdType.MESH
VMEM_SPEC = pl.BlockSpec(memory_space=pltpu.VMEM)
HBM_SPEC = pl.BlockSpec(memory_space=pl.ANY)


def _mesh_pos():
    return lax.axis_index("x"), lax.axis_index("y"), lax.axis_index("c")


def _slot(px, py, pc):
    return 4 * px + 2 * py + pc


def _all_gather_bf16(shards):
    n = len(shards)

    def body(*refs):
        ins, outs = refs[:n], refs[n:2 * n]
        land0, scratch = refs[2 * n], refs[2 * n + 1:]
        wpad_ref = outs[0]
        ag = _AllGather(ins, (land0,) + tuple(outs[1:]), scratch)
        ag.start()
        ag.forward()
        ag.finish()
        wpad_ref[ZKR:GA, :] = jnp.zeros((GA - ZKR, D_MODEL), BF16)
        for k in range(N_DEV):
            for at, dst, rows in _slab_segments(k):
                wpad_ref[dst:dst + rows, :] = land0[k, at:at + rows, :]

    return pl.pallas_call(
        body, name="all_gather_weights",
        in_specs=[VMEM_SPEC] * n, out_specs=[VMEM_SPEC] + [HBM_SPEC] * (n - 1),
        out_shape=[SDS((ZTOT, D_MODEL), BF16)] + [SDS((N_DEV,) + s.shape, BF16) for s in shards[1:]],
        scratch_shapes=[pltpu.VMEM((N_DEV,) + shards[0].shape, BF16)] + _ag_scratch([s.shape for s in shards]),
        compiler_params=_cp(),
    )(*shards)


def _ag_scratch(shapes):
    n = len(shapes)
    dma = pltpu.SemaphoreType.DMA
    return [pltpu.VMEM(tuple(s), BF16) for s in shapes] + [dma((7 * n,)), dma((7 * n,)), dma((n,))]


class _AllGather:
    def __init__(self, in_refs, dest_refs, scratch):
        n = self.n = len(in_refs)
        self.ins, self.dests, self.stage = in_refs, dest_refs, scratch[:n]
        self.send_sems, self.recv_sems, self.local_sems = scratch[n:]
        x, y, c = _mesh_pos()
        self.c, self.me, self.sibling = c, (x, y, c), (x, y, 1 - c)
        self.chips = [(1 - x, y), (x, 1 - y), (1 - x, 1 - y)]

    def _copy(self, a, k, block, to, from_stage=False):
        dst = self.dests[a].at[_slot(*block)]
        return pltpu.make_async_remote_copy(
            src_ref=self.stage[a] if from_stage else dst, dst_ref=dst, send_sem=self.send_sems.at[7 * a + k],
            recv_sem=self.recv_sems.at[7 * a + k], device_id=to, device_id_type=MESH_ID)

    def _mine(self):
        return [pltpu.make_async_copy(self.stage[a], self.dests[a].at[_slot(*self.me)], self.local_sems.at[a]) for a in range(self.n)]

    def _first(self):
        cps = []
        for a in range(self.n):
            cps.append(self._copy(a, 0, self.me, self.sibling, True))
            cps += [self._copy(a, 1 + j, self.me, (*chip, self.c), True) for j, chip in enumerate(self.chips)]
        return cps

    def _passed(self):
        return [self._copy(a, 4 + j, (*chip, self.c), self.sibling) for a in range(self.n) for j, chip in enumerate(self.chips)]

    def start(self):
        for a in range(self.n):
            self.stage[a][...] = self.ins[a][...].astype(BF16)
        for cp in self._mine() + self._first():
            cp.start()

    def forward(self):
        passed = self._passed()
        for a in range(self.n):
            for j, chip in enumerate(self.chips):
                self._copy(a, 1 + j, (*chip, self.c), self.me).wait_recv()
                passed[3 * a + j].start()

    def finish(self):
        for a in range(self.n):
            self._copy(a, 0, self.sibling, self.me).wait_recv()
            for j, chip in enumerate(self.chips):
                self._copy(a, 4 + j, (*chip, 1 - self.c), self.me).wait_recv()
        for cp in self._first() + self._passed():
            cp.wait_send()
        for cp in self._mine():
            cp.wait()


N_CHIPS = 4


def _reduce_scatter(slabs, packed):
    def body(*refs):
        n = len(slabs)
        rs = _ReduceScatter(refs[:n], refs[n], refs[n + 1:2 * n + 1], refs[2 * n + 1], refs[2 * n + 2:])
        rs.start1()
        rs.finish1_start2()
        rs.finish2()

    shapes = [s.shape[1:] for s in slabs]
    return pl.pallas_call(
        body, name="reduce_scatter_grads",
        in_specs=[HBM_SPEC] * len(slabs) + [VMEM_SPEC], out_specs=[VMEM_SPEC] * (len(slabs) + 1),
        out_shape=[SDS(s, F32) for s in shapes] + [SDS(packed.shape, F32)],
        scratch_shapes=_rs_scratch(shapes, packed.shape), compiler_params=_cp(),
    )(*slabs, packed)


def _rs_scratch(shapes, packed_shape):
    n = len(shapes)
    n1, n2 = N_CHIPS * n + 1, (N_CHIPS - 1) * (n + 1)
    dma = pltpu.SemaphoreType.DMA
    packed = [] if packed_shape is None else [pltpu.VMEM(packed_shape, F32), pltpu.VMEM((N_CHIPS,) + tuple(packed_shape), F32)]
    return ([pltpu.VMEM((N_CHIPS,) + tuple(s), BF16) for s in shapes] * 2 + [pltpu.VMEM((N_CHIPS - 1,) + tuple(s), BF16) for s in shapes] * 2
            + packed + [dma((max(N_CHIPS * n, 1),)), dma((n1,)), dma((n1,)), dma((n2,)), dma((n2,))])


class _ReduceScatter:
    def __init__(self, slab_refs, packed_ref, out_refs, ptot_ref, scratch):
        n = self.n = len(slab_refs)
        self.slabs, self.packed, self.outs, self.ptot = slab_refs, packed_ref, out_refs, ptot_ref
        self.own1, self.land1, self.send2, self.land2 = (scratch[k * n:(k + 1) * n] for k in range(4))
        rest = scratch[4 * n:]
        if packed_ref is not None:
            self.pland1, self.pland2 = rest[:2]
            rest = rest[2:]
        self.loc_sems, self.send1_sems, self.recv1_sems, self.send2_sems, self.recv2_sems = rest
        self.x, self.y, self.c = _mesh_pos()

    def _chip(self, r):
        return (1 - self.x if r & 2 else self.x, 1 - self.y if r & 1 else self.y)

    @staticmethod
    def _remote(src, dst, send_sem, recv_sem, to):
        return pltpu.make_async_remote_copy(src_ref=src, dst_ref=dst, send_sem=send_sem, recv_sem=recv_sem, device_id=to,
                                            device_id_type=MESH_ID)

    def _copies1(self):
        c, sibling = self.c, (self.x, self.y, 1 - self.c)
        cps = []
        for a in range(self.n):
            for r in range(N_CHIPS):
                k = N_CHIPS * a + r
                cps.append(pltpu.make_async_copy(self.slabs[a].at[_slot(*self._chip(r), c)], self.own1[a].at[r], self.loc_sems.at[k]))
                cps.append(self._remote(self.slabs[a].at[_slot(*self._chip(r), 1 - c)], self.land1[a].at[r],
                                        self.send1_sems.at[k], self.recv1_sems.at[k], sibling))
        if self.packed is not None:
            k = N_CHIPS * self.n
            cps.append(self._remote(self.packed, self.pland1, self.send1_sems.at[k], self.recv1_sems.at[k], sibling))
        return cps

    def _copies2(self):
        cps = []
        for a in range(self.n):
            for r in range(1, N_CHIPS):
                k = (N_CHIPS - 1) * a + r - 1
                cps.append(self._remote(self.send2[a].at[r - 1], self.land2[a].at[r - 1], self.send2_sems.at[k], self.recv2_sems.at[k],
                                        (*self._chip(r), self.c)))
        if self.packed is not None:
            for r in range(1, N_CHIPS):
                k = (N_CHIPS - 1) * self.n + r - 1
                cps.append(self._remote(self.pland2.at[0], self.pland2.at[r], self.send2_sems.at[k], self.recv2_sems.at[k],
                                        (*self._chip(r), self.c)))
        return cps

    def start1(self):
        for cp in self._copies1():
            cp.start()

    def finish1_start2(self):
        for cp in self._copies1():
            cp.wait()
        for a in range(self.n):
            self.outs[a][...] = self.own1[a][0].astype(F32) + self.land1[a][0].astype(F32)
            for r in range(1, N_CHIPS):
                self.send2[a][r - 1] = (self.own1[a][r].astype(F32) + self.land1[a][r].astype(F32)).astype(BF16)
        if self.packed is not None:
            self.pland2[0] = self.packed[...] + self.pland1[...]
        for cp in self._copies2():
            cp.start()

    def finish2(self):
        for cp in self._copies2():
            cp.wait()
        for a in range(self.n):
            l2 = self.land2[a]
            self.outs[a][...] = self.outs[a][...] + ((l2[0].astype(F32) + l2[1].astype(F32)) + l2[2].astype(F32))
        if self.packed is not None:
            p2 = self.pland2
            self.ptot[...] = (p2[0] + p2[1]) + (p2[2] + p2[3])


def _adamw(ws, gs, ms, vs):
    n = len(ws)

    def body(*refs):
        for k in range(n):
            w, g, m, v = (refs[j * n + k][...] for j in range(4))
            d_ref, nm_ref, nv_ref = (refs[(4 + j) * n + k] for j in range(3))
            m = ADAM_B1 * m + (1.0 - ADAM_B1) * g
            v = ADAM_B2 * v + (1.0 - ADAM_B2) * jnp.square(g)
            m_hat = m / (1.0 - ADAM_B1 ** ADAM_STEP)
            v_hat = v / (1.0 - ADAM_B2 ** ADAM_STEP)
            d_ref[...] = -ADAM_LR * (m_hat / (jnp.sqrt(v_hat) + ADAM_EPS) + ADAM_WD * w)
            nm_ref[...] = m
            nv_ref[...] = v

    outs = pl.pallas_call(
        body, name="adamw", in_specs=[VMEM_SPEC] * (4 * n), out_specs=[VMEM_SPEC] * (3 * n),
        out_shape=[SDS(w.shape, F32) for w in ws] * 3, compiler_params=_cp(),
    )(*ws, *gs, *ms, *vs)
    return outs[:n], outs[n:2 * n], outs[2 * n:]


WEIGHTS = ("norm_in", "w_in", "q_norm", "w_uq", "kv_norm", "w_ukv", "pool_w", "pool_scale", "w_branch_attn", "w_branch_pool",
           "w_out", "norm_final")
SUBLANES = 8


def _cols_to_slabs(g):
    r = g.shape[0]
    return g.reshape(r, N_DEV, -1).transpose(1, 0, 2)


def _slabs_to_cols(s):
    return s.transpose(1, 0, 2).reshape(s.shape[1], -1)


def _pack_rows(a):
    a = a.reshape(-1, LANES)
    return jnp.pad(a, ((0, -a.shape[0] % SUBLANES), (0, 0)))


def _unpack_rows(packed, like):
    out, row = [], 0
    for a in like:
        rows = a.size // LANES
        out.append(packed[row:row + rows].reshape(a.shape))
        row += rows + (-rows % SUBLANES)
    return out


def kernel(x, norm_in, w_in, q_norm, w_uq, kv_norm, w_ukv, pool_w, pool_scale, w_branch_attn, w_branch_pool, w_out, norm_final, loss_target, m_norm_in, m_w_in, m_q_norm, m_w_uq, m_kv_norm, m_w_ukv, m_pool_w, m_pool_scale, m_w_branch_attn, m_w_branch_pool, m_w_out, m_norm_final, v_norm_in, v_w_in, v_q_norm, v_w_uq, v_kv_norm, v_w_ukv, v_pool_w, v_pool_scale, v_w_branch_attn, v_w_branch_pool, v_w_out, v_norm_final):
    w = dict(norm_in=norm_in, w_in=w_in, q_norm=q_norm, w_uq=w_uq, kv_norm=kv_norm, w_ukv=w_ukv, pool_w=pool_w, pool_scale=pool_scale,
             w_branch_attn=w_branch_attn, w_branch_pool=w_branch_pool, w_out=w_out, norm_final=norm_final)
    m = dict(norm_in=m_norm_in, w_in=m_w_in, q_norm=m_q_norm, w_uq=m_w_uq, kv_norm=m_kv_norm, w_ukv=m_w_ukv, pool_w=m_pool_w,
             pool_scale=m_pool_scale, w_branch_attn=m_w_branch_attn, w_branch_pool=m_w_branch_pool, w_out=m_w_out, norm_final=m_norm_final)
    v = dict(norm_in=v_norm_in, w_in=v_w_in, q_norm=v_q_norm, w_uq=v_w_uq, kv_norm=v_kv_norm, w_ukv=v_w_ukv, pool_w=v_pool_w,
             pool_scale=v_pool_scale, w_branch_attn=v_w_branch_attn, w_branch_pool=v_w_branch_pool, w_out=v_w_out, norm_final=v_norm_final)

    def as2d(name, a):
        if name == "w_in":
            return a.T
        if name in ("w_uq", "w_ukv"):
            return a.reshape(a.shape[0], -1)
        if name == "pool_w":
            return a.reshape(-1, GROUP)
        return a.reshape(1, -1) if a.ndim == 1 else a

    def unshape(name, a):
        return a.T if name == "w_in" else a.reshape(w[name].shape)

    w_in_pad, w_uq_full, w_ukv_full = _all_gather_bf16([as2d(k, w[k]) for k in ("w_in", "w_uq", "w_ukv")])
    loss, grad_x, g2d = _local_step(
        x.reshape(x.shape[1:]), loss_target.reshape(x.shape[1:]), norm_in, w_in_pad, q_norm,
        w_uq_full.reshape(Q_RANK, HEADS, NOPE + ROPE), kv_norm, w_ukv_full.reshape(KV_RANK, HEADS, NOPE + VDIM),
        pool_w, pool_scale, [w_branch_attn, w_branch_pool, w_out], norm_final)

    deltas, new_m, new_v = _adamw([as2d(k, w[k]) for k in WEIGHTS], [g2d[k] for k in WEIGHTS],
                                  [as2d(k, m[k]) for k in WEIGHTS], [as2d(k, v[k]) for k in WEIGHTS])
    shaped = lambda arrs: [unshape(k, a) for k, a in zip(WEIGHTS, arrs)]
    return (loss, grad_x.reshape(x.shape), *shaped([g2d[k] for k in WEIGHTS]), *shaped(deltas), *shaped(new_m), *shaped(new_v))
```

```python
import functools

import jax
import jax.numpy as jnp
import numpy as np
from jax import lax
from jax.experimental import pallas as pl
from jax.experimental.pallas import tpu as pltpu

F32 = jnp.float32
BF16 = jnp.bfloat16
SDS = jax.ShapeDtypeStruct

D_MODEL = 1024
HEADS = 8
NOPE = 64
ROPE = 32
VDIM = 64
Q_RANK = 384
KV_RANK = 256
MLA_W = HEADS * VDIM
POOL_W = 512
POOL_GROUPS = 4
GROUP = POOL_W // POOL_GROUPS
CHUNK = 64
ROPE_THETA = 10000.0
EPS = 1e-6
SCALE = (NOPE + ROPE) ** -0.5
LOG2E = 1.4426950408889634
LN2 = 0.6931471805599453
QK_SCALE_LOG2 = SCALE * LOG2E
IN_TOTAL = 4256
ADAM_LR, ADAM_B1, ADAM_B2, ADAM_EPS, ADAM_WD, ADAM_STEP = 0.001, 0.9, 0.999, 1e-08, 0.01, 10

N_DEV = 8
LANES = 128
HEAD_PAD = LANES
HW = HEADS * HEAD_PAD

ZQ, ZKV, ZKR, GA, UP, GP, GM, ZTOT = 0, 384, 640, 768, 1280, 1792, 2304, 4352
FRONT_W = GA
ZKR_ORIG = 640

VMEM_LIMIT = 56 * 1024 * 1024


def _cp(sem=None, **kw):
    if sem is not None:
        kw["dimension_semantics"] = sem
    return pltpu.CompilerParams(vmem_limit_bytes=VMEM_LIMIT, **kw)


def _mm(a, b):
    return lax.dot_general(a, b, (((1,), (0,)), ((), ())), preferred_element_type=F32)


def _mm_nt(a, b):
    return lax.dot_general(a, b, (((1,), (1,)), ((), ())), preferred_element_type=F32)


def _mm_tn(a, b):
    return lax.dot_general(a, b, (((0,), (0,)), ((), ())), preferred_element_type=F32)


def _row_spec(tm, w):
    return pl.BlockSpec((tm, w), lambda i: (i, 0))


def _full_spec(a):
    nd = len(a.shape)
    return pl.BlockSpec(a.shape, lambda *_: (0,) * nd)


def _rope(v, c, sa, sb, sign):
    n = v.shape[-1]
    reps = n // LANES
    if reps > 1:
        c, sa, sb = (jnp.tile(t, (1, reps)) for t in (c, sa, sb))
    up = pltpu.roll(v, n - ROPE // 2, 1)
    dn = pltpu.roll(v, ROPE // 2, 1)
    return v * c + sign * (up * sa + dn * sb)


def _rope_tables(T):
    half = ROPE // 2
    inv_freq = np.float32(ROPE_THETA) ** (-np.arange(half, dtype=np.float32) / np.float32(half))
    ang = np.arange(T, dtype=np.float32)[:, None] * inv_freq[None, :].astype(np.float32)
    cos, sin = np.cos(ang.astype(np.float64)).astype(np.float32), np.sin(ang.astype(np.float64)).astype(np.float32)
    z16 = np.zeros((T, half), np.float32)
    z32 = np.zeros((T, LANES - NOPE - ROPE), np.float32)
    c = np.concatenate([np.ones((T, NOPE), np.float32), cos, cos, z32], axis=1)
    sa = np.concatenate([np.zeros((T, NOPE), np.float32), -sin, z16, z32], axis=1)
    sb = np.concatenate([np.zeros((T, NOPE), np.float32), z16, sin, z32], axis=1)
    return jnp.asarray(c), jnp.asarray(sa), jnp.asarray(sb)


def _silu_parts(g):
    sg = jax.nn.sigmoid(g)
    return g * sg, sg + g * sg * (1.0 - sg)


def _in_proj(x2, norm_in, w_in_pad, q_norm, wuq_pad, kv_norm, wukv, rc, rsa, rsb, tm, late_shards):
    T = x2.shape[0]
    steps = T // tm
    n = len(late_shards)

    def body(x_ref, nin_ref, win_ref, qn_ref, wuq_ref, kvn_ref, wukv_ref, c_ref, sa_ref, sb_ref, *rest):
        hn_ref, zgm_ref, zga_ref, zup_ref, zgp_ref, zfr_ref, q_ref, k_ref, v_ref = rest[n:n + 9]
        ag = _AllGather(rest[:n], rest[n + 9:2 * n + 9], rest[2 * n + 9:])
        step = pl.program_id(0)
        pl.when(step == 0)(ag.start)
        pl.when(step == min(5, steps - 1))(ag.forward)
        xf = x_ref[...]
        r = lax.rsqrt(jnp.mean(xf * xf, axis=-1, keepdims=True) + EPS)
        hn = (xf * r * nin_ref[...]).astype(BF16)
        hn_ref[...] = hn
        z = _mm_nt(hn, win_ref[...])
        zgm_ref[...] = z[:, GM:ZTOT]
        zga_ref[...] = z[:, GA:UP]
        zup_ref[...] = z[:, UP:GP]
        zgp_ref[...] = z[:, GP:GM]
        zfr_ref[...] = z[:, ZQ:GA]
        zq, zkv, zkr = z[:, ZQ:ZKV], z[:, ZKV:ZKR], z[:, ZKR:GA]
        c, sa, sb = c_ref[...], sa_ref[...], sb_ref[...]
        rq = lax.rsqrt(jnp.mean(zq * zq, axis=-1, keepdims=True) + EPS)
        cq = (zq * rq * qn_ref[...]).astype(BF16)
        q = _rope(_mm(cq, wuq_ref[...]), c, sa, sb, 1.0)
        q_ref[...] = (q * QK_SCALE_LOG2).astype(BF16)
        rkv = lax.rsqrt(jnp.mean(zkv * zkv, axis=-1, keepdims=True) + EPS)
        ckv = (zkv * rkv * kvn_ref[...]).astype(BF16)
        kv = _mm(ckv, wukv_ref[...])
        kr = _rope(zkr, c, sa, sb, 1.0)
        lane = lax.broadcasted_iota(jnp.int32, kv.shape, 1) % LANES
        k_ref[...] = jnp.where(lane < NOPE, kv, jnp.tile(kr, (1, HEADS))).astype(BF16)
        v_ref[...] = jnp.where(lane < NOPE, 1.0, kv).astype(BF16)
        pl.when(step == steps - 1)(ag.finish)

    ins = (x2, norm_in, w_in_pad, q_norm, wuq_pad, kv_norm, wukv, rc, rsa, rsb)
    in_specs = [_row_spec(tm, D_MODEL), _full_spec(norm_in), _full_spec(w_in_pad), _full_spec(q_norm), _full_spec(wuq_pad),
                _full_spec(kv_norm), _full_spec(wukv), _row_spec(tm, LANES), _row_spec(tm, LANES), _row_spec(tm, LANES)]
    widths = [(D_MODEL, BF16), (ZTOT - GM, F32), (UP - GA, F32), (GP - UP, F32), (GM - GP, F32), (FRONT_W, F32),
              (HW, BF16), (HW, BF16), (HW, BF16)]
    return pl.pallas_call(
        body, name="in_proj", grid=(steps,), in_specs=in_specs + [_full_spec(s) for s in late_shards],
        out_specs=[_row_spec(tm, w) for w, _ in widths] + [HBM_SPEC] * n,
        out_shape=[SDS((T, w), dt) for w, dt in widths] + [SDS((N_DEV,) + s.shape, BF16) for s in late_shards],
        scratch_shapes=_ag_scratch([s.shape for s in late_shards]), compiler_params=_cp(("arbitrary",)),
    )(*ins, *late_shards)


def _chunk_mask(n_q, n_k, q_off, transposed):
    shape = (n_k, n_q) if transposed else (n_q, n_k)
    q = (lax.broadcasted_iota(jnp.int32, shape, 1 if transposed else 0) + q_off) // CHUNK
    k = lax.broadcasted_iota(jnp.int32, shape, 0 if transposed else 1) // CHUNK
    return k <= q


_HEAD_LANES = (slice(0, LANES), slice(LANES, 2 * LANES))


def _pair_rows_spec(tq):
    return pl.BlockSpec((1, 2, tq), lambda p, i: (p, 0, i))


def _store_pair_rows(ref, k, pair):
    t = pair.T
    ref[k, 0:1, :] = t[0:1, :]
    ref[k, 1:2, :] = t[VDIM:VDIM + 1, :]


def _attn_fwd(q_att, k_att, v_att, tq, hps):
    T = q_att.shape[0]
    head_lanes = [slice(h * LANES, (h + 1) * LANES) for h in range(hps)]

    half = tq // 2

    def body(q_ref, k_ref, v_ref, o_ref, lser_ref):
        i = pl.program_id(1)
        lane = lax.broadcasted_iota(jnp.int32, (tq, LANES), 1)
        qs = [q_ref[:, hs] for hs in head_lanes]

        def online(m, acc, s, v):
            m_new = jnp.maximum(m, jnp.max(s, axis=-1, keepdims=True))
            p = jnp.exp2(s - m_new).astype(BF16)
            return m_new, jnp.exp2(m - m_new) * acc + _mm(p, v)

        def step(j, carry):
            rows = pl.ds(pl.multiple_of(j * tq, tq), tq)
            return tuple(online(m, acc, _mm_nt(qh, k_ref[rows, hs]), v_ref[rows, hs])
                         for (m, acc), qh, hs in zip(carry, qs, head_lanes))

        def diagonal(carry):
            top = pl.ds(pl.multiple_of(i * tq, tq), half)
            both = pl.ds(pl.multiple_of(i * tq, tq), tq)
            mask_t, mask_b = _chunk_mask(half, half, 0, False), _chunk_mask(half, tq, half, False)
            out = []
            for (m, acc), qh, hs in zip(carry, qs, head_lanes):
                s_t = jnp.where(mask_t, _mm_nt(qh[:half], k_ref[top, hs]), -jnp.inf)
                s_b = jnp.where(mask_b, _mm_nt(qh[half:], k_ref[both, hs]), -jnp.inf)
                m_t, acc_t = online(m[:half], acc[:half], s_t, v_ref[top, hs])
                m_b, acc_b = online(m[half:], acc[half:], s_b, v_ref[both, hs])
                out.append((jnp.concatenate([m_t, m_b], axis=0), jnp.concatenate([acc_t, acc_b], axis=0)))
            return out

        init = ((jnp.full((tq, 1), -jnp.inf, F32), jnp.zeros((tq, LANES), F32)),) * hps
        res = diagonal(lax.fori_loop(0, i, step, init))
        for pair in range(hps // 2):
            (ma, acca), (mb, accb) = res[2 * pair], res[2 * pair + 1]
            la, lb = acca[:, :1], accb[:, :1]
            o_ref[:, pair * LANES:(pair + 1) * LANES] = jnp.where(lane < VDIM, pltpu.roll(acca / la, VDIM, 1), accb / lb)
            _store_pair_rows(lser_ref, pair, jnp.where(lane < VDIM, ma + jnp.log2(la), mb + jnp.log2(lb)))

    qspec = pl.BlockSpec((tq, hps * LANES), lambda p, i: (i, p))
    kspec = pl.BlockSpec((T, hps * LANES), lambda p, i: (0, p))
    ospec = pl.BlockSpec((tq, hps * VDIM), lambda p, i: (i, p))
    return pl.pallas_call(
        body, name="attn_fwd", grid=(HEADS // hps, T // tq), in_specs=[qspec, kspec, kspec],
        out_specs=[ospec, pl.BlockSpec((hps // 2, 2, tq), lambda p, i: (p, 0, i))],
        out_shape=[SDS((T, MLA_W), F32), SDS((HEADS // 2, 2, T), F32)],
        compiler_params=_cp(("parallel", "parallel")),
    )(q_att, k_att, v_att)


def _pick(g, vals):
    out = vals[-1]
    for k in range(len(vals) - 2, -1, -1):
        out = jnp.where(g == k, vals[k], out)
    return out


def _window_sum(u, g, forward):
    T = u.shape[0]
    row = lax.broadcasted_iota(jnp.int32, u.shape, 0)

    def sh(s, k):
        if forward:
            return jnp.where(row >= k, pltpu.roll(s, k, 0), 0.0)
        return jnp.where(row < T - k, pltpu.roll(s, T - k, 0), 0.0)

    sums, s = [], u
    for k in (1, 2, 4, 8):
        s = s + sh(s, k)
        sums.append(s)
    return _pick(g, sums)


def _pool_count(shape, g):
    row = lax.broadcasted_iota(jnp.int32, shape, 0)
    return jnp.minimum(row + 1, lax.shift_left(jnp.int32(2), g)).astype(F32)


def _pool_fwd(zup, zgp, pool_w, pool_scale):
    T = zup.shape[0]

    def body(u_ref, g_ref, w_ref, sc_ref, y_ref):
        g = pl.program_id(0)
        u = u_ref[...]
        d = _window_sum(u, g, True) / _pool_count(u.shape, g) - u
        lin = _mm(d.astype(BF16), w_ref[0].astype(BF16))
        silu, _ = _silu_parts(g_ref[...])
        y_ref[...] = (lin * sc_ref[...] * silu).astype(BF16)

    col = pl.BlockSpec((T, GROUP), lambda g: (0, g))
    return pl.pallas_call(
        body, name="pool_fwd", grid=(POOL_GROUPS,),
        in_specs=[col, col, pl.BlockSpec((1, GROUP, GROUP), lambda g: (g, 0, 0)), pl.BlockSpec((1, GROUP), lambda g: (0, g))],
        out_specs=col, out_shape=SDS((T, POOL_W), BF16), compiler_params=_cp(("parallel",)),
    )(zup, zgp, pool_w, pool_scale)


def _pool_bwd(zup, zgp, dyp, pool_w, pool_scale):
    T = zup.shape[0]

    def body(u_ref, g_ref, dy_ref, w_ref, sc_ref, du_ref, dg_ref, gw_ref, gsc_ref):
        g = pl.program_id(0)
        u = u_ref[...]
        cnt = _pool_count(u.shape, g)
        d = (_window_sum(u, g, True) / cnt - u).astype(BF16)
        wb = w_ref[0].astype(BF16)
        lin = _mm(d, wb)
        sc = sc_ref[...]
        silu, dsilu = _silu_parts(g_ref[...])
        dy = dy_ref[...]
        dg_ref[...] = (dy * lin * sc * dsilu).astype(BF16)
        dpre = dy * silu
        gsc_ref[...] = jnp.sum(dpre * lin, axis=0, keepdims=True)
        dlin = (dpre * sc).astype(BF16)
        gw_ref[0] = _mm_tn(d, dlin)
        dd = _mm_nt(dlin, wb)
        du_ref[...] = (_window_sum(dd / cnt, g, False) - dd).astype(BF16)

    col = pl.BlockSpec((T, GROUP), lambda g: (0, g))
    wspec = pl.BlockSpec((1, GROUP, GROUP), lambda g: (g, 0, 0))
    vspec = pl.BlockSpec((1, GROUP), lambda g: (0, g))
    return pl.pallas_call(
        body, name="pool_bwd", grid=(POOL_GROUPS,), in_specs=[col, col, col, wspec, vspec], out_specs=[col, col, wspec, vspec],
        out_shape=[SDS((T, POOL_W), BF16), SDS((T, POOL_W), BF16), SDS((POOL_GROUPS, GROUP, GROUP), F32), SDS((1, POOL_W), F32)],
        compiler_params=_cp(("parallel",)),
    )(zup, zgp, dyp, pool_w, pool_scale)


def _tail(x2, tgt, o, zga, ypool, zgm, wba, wbp, wout, norm_final, tm):
    T = x2.shape[0]

    def body(x_ref, tgt_ref, o_ref, zga_ref, yp_ref, zgm_ref, wba_ref, wbp_ref, wout_ref, nf_ref,
             loss_ref, dh_ref, dgm_ref, doop_ref, dga_ref, dcapr_ref, dyp_ref, gwout_ref, gwba_ref, gwbp_ref, gnf_ref):
        @pl.when(pl.program_id(0) == 0)
        def _():
            for ref in (loss_ref, gwout_ref, gwba_ref, gwbp_ref, gnf_ref):
                ref[...] = jnp.zeros_like(ref)

        o_v = o_ref[...]
        silu, dsilu = _silu_parts(zga_ref[...])
        ya = (o_v * silu).astype(BF16)
        yp = yp_ref[...]
        wba_v, wbp_v, wout_v = wba_ref[...], wbp_ref[...], wout_ref[...]
        a = _mm(ya, wba_v)
        p = _mm(yp, wbp_v)
        gate = jax.nn.sigmoid(zgm_ref[...])
        ga, gp = gate[:, :D_MODEL], gate[:, D_MODEL:]
        mg = (ga * a + gp * p).astype(BF16)
        h = x_ref[...] + _mm(mg, wout_v)
        r = lax.rsqrt(jnp.mean(h * h, axis=-1, keepdims=True) + EPS)
        gf = nf_ref[...]
        hr = h * r
        e = hr * gf - tgt_ref[...]
        loss_ref[...] += (0.5 / D_MODEL) * jnp.sum(e * e)
        dy = e * (1.0 / D_MODEL)
        gnf_ref[...] += jnp.sum(dy * hr, axis=0, keepdims=True)
        u = dy * gf
        dh = r * (u - hr * jnp.mean(u * hr, axis=-1, keepdims=True))
        dh_ref[...] = dh
        dhb = dh.astype(BF16)
        dmg = _mm_nt(dhb, wout_v)
        gwout_ref[...] += _mm_tn(mg, dhb)
        dgm_ref[:, :D_MODEL] = (dmg * a * ga * (1.0 - ga)).astype(BF16)
        dgm_ref[:, D_MODEL:] = (dmg * p * gp * (1.0 - gp)).astype(BF16)
        dab = (dmg * ga).astype(BF16)
        dpb = (dmg * gp).astype(BF16)
        dya = _mm_nt(dab, wba_v)
        gwba_ref[...] += _mm_tn(ya, dab)
        dyp_ref[...] = _mm_nt(dpb, wbp_v)
        gwbp_ref[...] += _mm_tn(yp, dpb)
        do = dya * silu
        dga_ref[...] = (dya * o_v * dsilu).astype(BF16)
        prod = do * o_v
        lo = lax.broadcasted_iota(jnp.int32, (tm, LANES), 1) < VDIM
        for pair in range(HEADS // 2):
            ls = slice(pair * LANES, (pair + 1) * LANES)
            do_p, prod_p = do[:, ls], prod[:, ls]
            dcap_a = jnp.sum(jnp.where(lo, prod_p, 0.0), axis=-1, keepdims=True)
            dcap_b = jnp.sum(jnp.where(lo, 0.0, prod_p), axis=-1, keepdims=True)
            _store_pair_rows(dcapr_ref, pair, jnp.where(lo, dcap_a, dcap_b))
            doop_ref[:, 2 * pair * LANES:(2 * pair + 1) * LANES] = jnp.where(lo, 0.0, pltpu.roll(do_p, VDIM, 1)).astype(BF16)
            doop_ref[:, (2 * pair + 1) * LANES:(2 * pair + 2) * LANES] = jnp.where(lo, 0.0, do_p).astype(BF16)

    ins = (x2, tgt, o, zga, ypool, zgm, wba, wbp, wout, norm_final)
    in_specs = [_row_spec(tm, D_MODEL), _row_spec(tm, D_MODEL), _row_spec(tm, MLA_W), _row_spec(tm, MLA_W), _row_spec(tm, POOL_W),
                _row_spec(tm, 2 * D_MODEL), _full_spec(wba), _full_spec(wbp), _full_spec(wout), _full_spec(norm_final)]
    outs = [SDS((8, LANES), F32), SDS((T, D_MODEL), F32), SDS((T, 2 * D_MODEL), BF16), SDS((T, HW), BF16), SDS((T, MLA_W), BF16),
            SDS((HEADS // 2, 2, T), F32), SDS((T, POOL_W), F32),
            SDS((D_MODEL, D_MODEL), F32), SDS((MLA_W, D_MODEL), F32), SDS((POOL_W, D_MODEL), F32), SDS((1, D_MODEL), F32)]
    out_specs = [_full_spec(outs[0]), _row_spec(tm, D_MODEL), _row_spec(tm, 2 * D_MODEL), _row_spec(tm, HW), _row_spec(tm, MLA_W),
                 pl.BlockSpec((HEADS // 2, 2, tm), lambda i: (0, 0, i)), _row_spec(tm, POOL_W),
                 _full_spec(outs[7]), _full_spec(outs[8]), _full_spec(outs[9]), _full_spec(outs[10])]
    return pl.pallas_call(
        body, name="tail", grid=(T // tm,), in_specs=in_specs, out_specs=out_specs, out_shape=outs,
        compiler_params=_cp(("arbitrary",)),
    )(*ins)


def _attn_bwd(q_att, k_att, v_att, doop, lse_rows, dcap_rows, tq, slabs, packed):
    T = q_att.shape[0]
    nq = T // tq
    n = len(slabs)
    n_pairs = HEADS // 2

    def body(q_ref, k_ref, v_ref, doop_ref, lse_ref, dcap_ref, *rest):
        slab_refs, packed_ref = rest[:n], rest[n]
        dq_ref, dkv_ref, dkr_ref = rest[n + 1:n + 4]
        sum_refs, ptot_ref = rest[n + 4:2 * n + 4], rest[2 * n + 4]
        dq_acc = rest[2 * n + 5]
        rs = _ReduceScatter(slab_refs, packed_ref, sum_refs, ptot_ref, rest[2 * n + 6:])
        pair, j = pl.program_id(0), pl.program_id(1)
        pl.when((pair == 0) & (j == 0))(rs.start1)
        pl.when((pair == 1) & (j == 0))(rs.finish1_start2)
        lane = lax.broadcasted_iota(jnp.int32, (tq, LANES), 1)
        ks = [k_ref[:, hs] for hs in _HEAD_LANES]
        vs = [v_ref[:, hs] for hs in _HEAD_LANES]
        half = tq // 2

        @pl.when(j == 0)
        def _():
            dq_acc[...] = jnp.zeros_like(dq_acc)

        def part(h, hs, kh, vh, rows, mask):
            qh = q_ref[rows, hs]
            doop_h = doop_ref[rows, hs]
            pt = jnp.exp2(_mm_nt(kh, qh) - lse_ref[0, h:h + 1, rows])
            if mask is not None:
                pt = jnp.where(mask, pt, 0.0)
            dv = _mm(pt.astype(BF16), doop_h)
            dst = (pt * (_mm_nt(vh, doop_h) - dcap_ref[0, h:h + 1, rows])).astype(BF16)
            dq_acc[rows, hs] += _mm_tn(dst, kh)
            return _mm(dst, qh), dv

        def step(i, carry):
            rows = pl.ds(pl.multiple_of(i * tq, tq), tq)
            out = []
            for h, ((dk, dv), kh, vh, hs) in enumerate(zip(carry, ks, vs, _HEAD_LANES)):
                dk_i, dv_i = part(h, hs, kh, vh, rows, None)
                out.append((dk + dk_i, dv + dv_i))
            return tuple(out)

        def diagonal():
            both = pl.ds(pl.multiple_of(j * tq, tq), tq)
            late = pl.ds(pl.multiple_of(j * tq + half, half), half)
            mask_t, mask_b = _chunk_mask(tq, half, 0, True), _chunk_mask(half, half, 0, True)
            out = []
            for h, (kh, vh, hs) in enumerate(zip(ks, vs, _HEAD_LANES)):
                dk_t, dv_t = part(h, hs, kh[:half], vh[:half], both, mask_t)
                dk_b, dv_b = part(h, hs, kh[half:], vh[half:], late, mask_b)
                out.append((jnp.concatenate([dk_t, dk_b], axis=0), jnp.concatenate([dv_t, dv_b], axis=0)))
            return tuple(out)

        (dka, dva), (dkb, dvb) = lax.fori_loop(j + 1, nq, step, diagonal())
        dka, dkb = dka * LN2, dkb * LN2
        dkv_ref[:, _HEAD_LANES[0]] = jnp.where(lane < NOPE, dka, dva).astype(BF16)
        dkv_ref[:, _HEAD_LANES[1]] = jnp.where(lane < NOPE, dkb, dvb).astype(BF16)
        dkr_ref[0] = jnp.where((lane >= NOPE) & (lane < NOPE + ROPE), dka + dkb, 0.0)

        @pl.when(j == nq - 1)
        def _():
            dq_ref[...] = (dq_acc[...] * SCALE).astype(BF16)

        pl.when((pair == n_pairs - 1) & (j == nq - 1))(rs.finish2)

    kspec = pl.BlockSpec((tq, 2 * LANES), lambda p, j: (j, p))
    qspec = pl.BlockSpec((T, 2 * LANES), lambda p, j: (0, p))
    rspec = pl.BlockSpec((1, 2, T), lambda p, j: (p, 0, 0))
    sums = [SDS(s.shape[1:], F32) for s in slabs] + [SDS(packed.shape, F32)]
    return pl.pallas_call(
        body, name="attn_bwd", grid=(n_pairs, nq),
        in_specs=[qspec, kspec, kspec, qspec, rspec, rspec] + [HBM_SPEC] * n + [_full_spec(packed)],
        out_specs=[qspec, kspec, pl.BlockSpec((1, tq, LANES), lambda p, j: (p, j, 0))] + [_full_spec(s) for s in sums],
        out_shape=[SDS((T, HW), BF16), SDS((T, HW), BF16), SDS((n_pairs, T, LANES), F32)] + sums,
        scratch_shapes=[pltpu.VMEM((T, 2 * LANES), F32)] + _rs_scratch([s.shape for s in sums[:-1]], packed.shape),
        compiler_params=_cp(("arbitrary", "arbitrary")),
    )(q_att, k_att, v_att, doop, lse_rows, dcap_rows, *slabs, packed)


def _rms_bwd(z, gain, dout):
    r = lax.rsqrt(jnp.mean(z * z, axis=-1, keepdims=True) + EPS)
    zr = z * r
    u = dout * gain
    return r * (u - zr * jnp.mean(u * zr, axis=-1, keepdims=True)), jnp.sum(dout * zr, axis=0, keepdims=True)


def _mla_bwd(dq_att, dkv_nat, dkr4, zfr, q_norm, wuq_pad, kv_norm, wukv, rc, rsa, rsb, tm):
    T = dq_att.shape[0]

    def body(dq_ref, dkv_ref, dkr_ref, zfr_ref, qn_ref, wuq_ref, kvn_ref, wukv_ref, c_ref, sa_ref, sb_ref,
             dfr_ref, gwuq_ref, gwukv_ref, gqn_ref, gkvn_ref):
        @pl.when(pl.program_id(0) == 0)
        def _():
            for ref in (gwuq_ref, gwukv_ref, gqn_ref, gkvn_ref):
                ref[...] = jnp.zeros_like(ref)

        c, sa, sb = c_ref[...], sa_ref[...], sb_ref[...]
        zq, zkv = zfr_ref[:, :Q_RANK], zfr_ref[:, Q_RANK:Q_RANK + KV_RANK]
        qn, kvn = qn_ref[...], kvn_ref[...]
        cq = (zq * lax.rsqrt(jnp.mean(zq * zq, axis=-1, keepdims=True) + EPS) * qn).astype(BF16)
        ckv = (zkv * lax.rsqrt(jnp.mean(zkv * zkv, axis=-1, keepdims=True) + EPS) * kvn).astype(BF16)
        dq = _rope(dq_ref[...].astype(F32), c, sa, sb, -1.0).astype(BF16)
        gwuq_ref[...] += _mm_tn(cq, dq)
        dzq, gqn = _rms_bwd(zq, qn, _mm_nt(dq, wuq_ref[...]))
        gqn_ref[...] += gqn
        dkv = dkv_ref[...]
        gwukv_ref[...] += _mm_tn(ckv, dkv)
        dzkv, gkvn = _rms_bwd(zkv, kvn, _mm_nt(dkv, wukv_ref[...]))
        gkvn_ref[...] += gkvn
        dkr = dkr_ref[0] + dkr_ref[1] + dkr_ref[2] + dkr_ref[3]
        dfr_ref[:, :Q_RANK] = dzq.astype(BF16)
        dfr_ref[:, Q_RANK:Q_RANK + KV_RANK] = dzkv.astype(BF16)
        dfr_ref[:, Q_RANK + KV_RANK:] = _rope(dkr, c, sa, sb, -1.0).astype(BF16)

    ins = (dq_att, dkv_nat, dkr4, zfr, q_norm, wuq_pad, kv_norm, wukv, rc, rsa, rsb)
    in_specs = [_row_spec(tm, HW), _row_spec(tm, HW), pl.BlockSpec((HEADS // 2, tm, LANES), lambda i: (0, i, 0)), _row_spec(tm, FRONT_W),
                _full_spec(q_norm), _full_spec(wuq_pad), _full_spec(kv_norm), _full_spec(wukv),
                _row_spec(tm, LANES), _row_spec(tm, LANES), _row_spec(tm, LANES)]
    outs = [SDS((T, FRONT_W), BF16), SDS((Q_RANK, HW), F32), SDS((KV_RANK, HW), F32), SDS((1, Q_RANK), F32), SDS((1, KV_RANK), F32)]
    out_specs = [_row_spec(tm, FRONT_W)] + [_full_spec(s) for s in outs[1:]]
    return pl.pallas_call(
        body, name="mla_bwd", grid=(T // tm,), in_specs=in_specs, out_specs=out_specs, out_shape=outs,
        compiler_params=_cp(("arbitrary",)),
    )(*ins)


_DZ_COLS = ((GM, ZTOT), (GA, UP), (UP, GP), (GP, GM), (ZQ, GA))


def _in_proj_bwd_x(dzs, x2, dh, norm_in, w_in_pad, tm, slabs):
    T = x2.shape[0]
    steps = T // tm
    n = len(slabs)

    def body(d0, d1, d2, d3, d4, x_ref, dh_ref, nin_ref, win_ref, *rest):
        slab_refs, (gx_ref, gnin_ref), sum_refs = rest[:n], rest[n:n + 2], rest[n + 2:2 * n + 2]
        rs = _ReduceScatter(slab_refs, None, sum_refs, None, rest[2 * n + 2:])
        step = pl.program_id(0)

        @pl.when(step == 0)
        def _():
            gnin_ref[...] = jnp.zeros_like(gnin_ref)
            rs.start1()

        pl.when(step == min(2, steps - 1))(rs.finish1_start2)
        dhn = None
        for ref, (lo, hi) in zip((d0, d1, d2, d3, d4), _DZ_COLS):
            t = _mm(ref[...], win_ref[lo:hi, :])
            dhn = t if dhn is None else dhn + t
        dx, gnin = _rms_bwd(x_ref[...], nin_ref[...], dhn)
        gnin_ref[...] += gnin
        gx_ref[...] = dx + dh_ref[...]
        pl.when(step == steps - 1)(rs.finish2)

    in_specs = [_row_spec(tm, hi - lo) for lo, hi in _DZ_COLS] + [_row_spec(tm, D_MODEL), _row_spec(tm, D_MODEL),
                                                                  _full_spec(norm_in), _full_spec(w_in_pad)] + [HBM_SPEC] * n
    sums = [SDS(s.shape[1:], F32) for s in slabs]
    outs = [SDS((T, D_MODEL), F32), SDS((1, D_MODEL), F32)] + sums
    return pl.pallas_call(
        body, name="in_proj_bwd_x", grid=(steps,), in_specs=in_specs,
        out_specs=[_row_spec(tm, D_MODEL), _full_spec(outs[1])] + [_full_spec(s) for s in sums],
        out_shape=outs, scratch_shapes=_rs_scratch([s.shape for s in sums], None), compiler_params=_cp(("arbitrary",)),
    )(*dzs, x2, dh, norm_in, w_in_pad, *slabs)


SLAB_ROWS = IN_TOTAL // N_DEV


def _slab_segments(k):
    cuts = [(0, ZKR_ORIG, 0), (ZKR_ORIG, ZKR_ORIG + ROPE, NOPE), (ZKR_ORIG + ROPE, IN_TOTAL, LANES - ROPE)]
    lo, hi = k * SLAB_ROWS, (k + 1) * SLAB_ROWS
    return [(max(lo, a) - lo, max(lo, a) + shift, min(hi, b) - max(lo, a)) for a, b, shift in cuts if min(hi, b) > max(lo, a)]


def _in_proj_bwd_w(dzs, hn, tm):
    T = hn.shape[0]
    steps = T // tm

    def body(d0, d1, d2, d3, d4, hn_ref, slab_ref, acc_ref):
        @pl.when(pl.program_id(0) == 0)
        def _():
            acc_ref[...] = jnp.zeros_like(acc_ref)

        hn_v = hn_ref[...]
        for ref, (lo, hi) in zip((d0, d1, d2, d3, d4), _DZ_COLS):
            acc_ref[lo:hi, :] += _mm_tn(ref[...], hn_v)

        @pl.when(pl.program_id(0) == steps - 1)
        def _():
            for k in range(N_DEV):
                for at, src, rows in _slab_segments(k):
                    slab_ref[k, at:at + rows, :] = acc_ref[src:src + rows, :].astype(BF16)

    in_specs = [_row_spec(tm, hi - lo) for lo, hi in _DZ_COLS] + [_row_spec(tm, D_MODEL)]
    out = SDS((N_DEV, SLAB_ROWS, D_MODEL), BF16)
    return pl.pallas_call(
        body, name="in_proj_bwd_w", grid=(steps,), in_specs=in_specs, out_specs=_full_spec(out), out_shape=out,
        scratch_shapes=[pltpu.VMEM((ZTOT, D_MODEL), F32)], compiler_params=_cp(("arbitrary",)),
    )(*dzs, hn)


def _local_step(x2, tgt, norm_in, w_in_pad, q_norm, w_uq, kv_norm, w_ukv, pool_w, pool_scale, late_shards, norm_final):
    T = x2.shape[0]
    tm = min(512, T)
    tq = min(512, T)
    row = lambda v: v.reshape(1, -1)
    wuq_pad = jnp.pad(w_uq, ((0, 0), (0, 0), (0, HEAD_PAD - NOPE - ROPE))).reshape(Q_RANK, HW)
    wukv = w_ukv.reshape(KV_RANK, HW)
    rc, rsa, rsb = _rope_tables(T)

    hn, zgm, zga, zup, zgp, zfr, q_att, k_att, v_att, w_ba, w_bp, w_out = _in_proj(
        x2, row(norm_in), w_in_pad, row(q_norm), wuq_pad, row(kv_norm), wukv, rc, rsa, rsb, tm, late_shards)
    w_ba, w_bp, w_out = _slabs_to_cols(w_ba), _slabs_to_cols(w_bp), w_out.reshape(D_MODEL, D_MODEL)
    o, lse_rows = _attn_fwd(q_att, k_att, v_att, tq, 4)
    ypool = _pool_fwd(zup, zgp, pool_w, row(pool_scale))
    loss8, dh, dgm, doop, dga, dcap_rows, dyp, g_wout, g_wba, g_wbp, g_nf = _tail(
        x2, tgt, o, zga, ypool, zgm, w_ba, w_bp, w_out, row(norm_final), min(256, T))
    dup, dgp, g_pool_w, g_pool_scale = _pool_bwd(zup, zgp, dyp, pool_w, row(pool_scale))

    bf = lambda a: a.astype(BF16)
    slabs = [bf(g_wout).reshape(N_DEV, D_MODEL // N_DEV, D_MODEL), _cols_to_slabs(bf(g_wba)), _cols_to_slabs(bf(g_wbp))]
    early = [g_pool_w, g_pool_scale, g_nf, loss8[0]]
    packed = jnp.concatenate([_pack_rows(a) for a in early], axis=0)
    dq_att, dkv_nat, dkr4, s_wout, s_wba, s_wbp, tot_early = _attn_bwd(
        q_att, k_att, v_att, doop, lse_rows, dcap_rows, tq, slabs, packed)
    s_pool_w, s_pool_scale, s_nf, s_loss = _unpack_rows(tot_early, early)

    dfr, g_wuq_pad, g_wukv, g_qn, g_kvn = _mla_bwd(
        dq_att, dkv_nat, dkr4, zfr, row(q_norm), wuq_pad, row(kv_norm), wukv, rc, rsa, rsb, tm)
    dzs = (dgm, dga, dup, dgp, dfr)
    slabs = [_in_proj_bwd_w(dzs, hn, tm),
             bf(g_wuq_pad.reshape(Q_RANK, HEADS, HEAD_PAD)[:, :, :NOPE + ROPE]).reshape(N_DEV, Q_RANK // N_DEV, -1),
             bf(g_wukv).reshape(N_DEV, KV_RANK // N_DEV, HW)]
    grad_x, g_nin, s_win, s_wuq, s_wukv = _in_proj_bwd_x(dzs, x2, dh, row(norm_in), w_in_pad, min(256, T), slabs)
    late = [g_nin, g_qn, g_kvn]
    (tot_late,) = _reduce_scatter([], jnp.concatenate([_pack_rows(a) for a in late], axis=0))
    s_nin, s_qn, s_kvn = _unpack_rows(tot_late, late)

    grads = dict(norm_in=s_nin, w_in=s_win, q_norm=s_qn, w_uq=s_wuq, kv_norm=s_kvn, w_ukv=s_wukv, pool_w=s_pool_w.reshape(-1, GROUP),
                 pool_scale=s_pool_scale, w_branch_attn=s_wba, w_branch_pool=s_wbp, w_out=s_wout, norm_final=s_nf)
    return s_loss[0], grad_x, grads


MESH_ID = pl.DeviceIdType.MESH
VMEM_SPEC = pl.BlockSpec(memory_space=pltpu.VMEM)
HBM_SPEC = pl.BlockSpec(memory_space=pl.ANY)


def _mesh_pos():
    return lax.axis_index("x"), lax.axis_index("y"), lax.axis_index("c")


def _slot(px, py, pc):
    return 4 * px + 2 * py + pc


def _all_gather_bf16(shards):
    n = len(shards)

    def body(*refs):
        ins, outs = refs[:n], refs[n:2 * n]
        land0, scratch = refs[2 * n], refs[2 * n + 1:]
        wpad_ref = outs[0]
        ag = _AllGather(ins, (land0,) + tuple(outs[1:]), scratch)
        ag.start()
        ag.forward()
        ag.finish()
        wpad_ref[ZKR:GA, :] = jnp.zeros((GA - ZKR, D_MODEL), BF16)
        for k in range(N_DEV):
            for at, dst, rows in _slab_segments(k):
                wpad_ref[dst:dst + rows, :] = land0[k, at:at + rows, :]

    return pl.pallas_call(
        body, name="all_gather_weights",
        in_specs=[VMEM_SPEC] * n, out_specs=[VMEM_SPEC] + [HBM_SPEC] * (n - 1),
        out_shape=[SDS((ZTOT, D_MODEL), BF16)] + [SDS((N_DEV,) + s.shape, BF16) for s in shards[1:]],
        scratch_shapes=[pltpu.VMEM((N_DEV,) + shards[0].shape, BF16)] + _ag_scratch([s.shape for s in shards]),
        compiler_params=_cp(),
    )(*shards)


def _ag_scratch(shapes):
    n = len(shapes)
    dma = pltpu.SemaphoreType.DMA
    return [pltpu.VMEM(tuple(s), BF16) for s in shapes] + [dma((7 * n,)), dma((7 * n,)), dma((n,))]


class _AllGather:
    def __init__(self, in_refs, dest_refs, scratch):
        n = self.n = len(in_refs)
        self.ins, self.dests, self.stage = in_refs, dest_refs, scratch[:n]
        self.send_sems, self.recv_sems, self.local_sems = scratch[n:]
        x, y, c = _mesh_pos()
        self.c, self.me, self.sibling = c, (x, y, c), (x, y, 1 - c)
        self.chips = [(1 - x, y), (x, 1 - y), (1 - x, 1 - y)]

    def _copy(self, a, k, block, to, from_stage=False):
        dst = self.dests[a].at[_slot(*block)]
        return pltpu.make_async_remote_copy(
            src_ref=self.stage[a] if from_stage else dst, dst_ref=dst, send_sem=self.send_sems.at[7 * a + k],
            recv_sem=self.recv_sems.at[7 * a + k], device_id=to, device_id_type=MESH_ID)

    def _mine(self):
        return [pltpu.make_async_copy(self.stage[a], self.dests[a].at[_slot(*self.me)], self.local_sems.at[a]) for a in range(self.n)]

    def _first(self):
        cps = []
        for a in range(self.n):
            cps.append(self._copy(a, 0, self.me, self.sibling, True))
            cps += [self._copy(a, 1 + j, self.me, (*chip, self.c), True) for j, chip in enumerate(self.chips)]
        return cps

    def _passed(self):
        return [self._copy(a, 4 + j, (*chip, self.c), self.sibling) for a in range(self.n) for j, chip in enumerate(self.chips)]

    def start(self):
        for a in range(self.n):
            self.stage[a][...] = self.ins[a][...].astype(BF16)
        for cp in self._mine() + self._first():
            cp.start()

    def forward(self):
        passed = self._passed()
        for a in range(self.n):
            for j, chip in enumerate(self.chips):
                self._copy(a, 1 + j, (*chip, self.c), self.me).wait_recv()
                passed[3 * a + j].start()

    def finish(self):
        for a in range(self.n):
            self._copy(a, 0, self.sibling, self.me).wait_recv()
            for j, chip in enumerate(self.chips):
                self._copy(a, 4 + j, (*chip, 1 - self.c), self.me).wait_recv()
        for cp in self._first() + self._passed():
            cp.wait_send()
        for cp in self._mine():
            cp.wait()


N_CHIPS = 4


def _reduce_scatter(slabs, packed):
    def body(*refs):
        n = len(slabs)
        rs = _ReduceScatter(refs[:n], refs[n], refs[n + 1:2 * n + 1], refs[2 * n + 1], refs[2 * n + 2:])
        rs.start1()
        rs.finish1_start2()
        rs.finish2()

    shapes = [s.shape[1:] for s in slabs]
    return pl.pallas_call(
        body, name="reduce_scatter_grads",
        in_specs=[HBM_SPEC] * len(slabs) + [VMEM_SPEC], out_specs=[VMEM_SPEC] * (len(slabs) + 1),
        out_shape=[SDS(s, F32) for s in shapes] + [SDS(packed.shape, F32)],
        scratch_shapes=_rs_scratch(shapes, packed.shape), compiler_params=_cp(),
    )(*slabs, packed)


def _rs_scratch(shapes, packed_shape):
    n = len(shapes)
    n1, n2 = N_CHIPS * n + 1, (N_CHIPS - 1) * (n + 1)
    dma = pltpu.SemaphoreType.DMA
    packed = [] if packed_shape is None else [pltpu.VMEM(packed_shape, F32), pltpu.VMEM((N_CHIPS,) + tuple(packed_shape), F32)]
    return ([pltpu.VMEM((N_CHIPS,) + tuple(s), BF16) for s in shapes] * 2 + [pltpu.VMEM((N_CHIPS - 1,) + tuple(s), BF16) for s in shapes] * 2
            + packed + [dma((max(N_CHIPS * n, 1),)), dma((n1,)), dma((n1,)), dma((n2,)), dma((n2,))])


class _ReduceScatter:
    def __init__(self, slab_refs, packed_ref, out_refs, ptot_ref, scratch):
        n = self.n = len(slab_refs)
        self.slabs, self.packed, self.outs, self.ptot = slab_refs, packed_ref, out_refs, ptot_ref
        self.own1, self.land1, self.send2, self.land2 = (scratch[k * n:(k + 1) * n] for k in range(4))
        rest = scratch[4 * n:]
        if packed_ref is not None:
            self.pland1, self.pland2 = rest[:2]
            rest = rest[2:]
        self.loc_sems, self.send1_sems, self.recv1_sems, self.send2_sems, self.recv2_sems = rest
        self.x, self.y, self.c = _mesh_pos()

    def _chip(self, r):
        return (1 - self.x if r & 2 else self.x, 1 - self.y if r & 1 else self.y)

    @staticmethod
    def _remote(src, dst, send_sem, recv_sem, to):
        return pltpu.make_async_remote_copy(src_ref=src, dst_ref=dst, send_sem=send_sem, recv_sem=recv_sem, device_id=to,
                                            device_id_type=MESH_ID)

    def _copies1(self):
        c, sibling = self.c, (self.x, self.y, 1 - self.c)
        cps = []
        for a in range(self.n):
            for r in range(N_CHIPS):
                k = N_CHIPS * a + r
                cps.append(pltpu.make_async_copy(self.slabs[a].at[_slot(*self._chip(r), c)], self.own1[a].at[r], self.loc_sems.at[k]))
                cps.append(self._remote(self.slabs[a].at[_slot(*self._chip(r), 1 - c)], self.land1[a].at[r],
                                        self.send1_sems.at[k], self.recv1_sems.at[k], sibling))
        if self.packed is not None:
            k = N_CHIPS * self.n
            cps.append(self._remote(self.packed, self.pland1, self.send1_sems.at[k], self.recv1_sems.at[k], sibling))
        return cps

    def _copies2(self):
        cps = []
        for a in range(self.n):
            for r in range(1, N_CHIPS):
                k = (N_CHIPS - 1) * a + r - 1
                cps.append(self._remote(self.send2[a].at[r - 1], self.land2[a].at[r - 1], self.send2_sems.at[k], self.recv2_sems.at[k],
                                        (*self._chip(r), self.c)))
        if self.packed is not None:
            for r in range(1, N_CHIPS):
                k = (N_CHIPS - 1) * self.n + r - 1
                cps.append(self._remote(self.pland2.at[0], self.pland2.at[r], self.send2_sems.at[k], self.recv2_sems.at[k],
                                        (*self._chip(r), self.c)))
        return cps

    def start1(self):
        for cp in self._copies1():
            cp.start()

    def finish1_start2(self):
        for cp in self._copies1():
            cp.wait()
        for a in range(self.n):
            self.outs[a][...] = self.own1[a][0].astype(F32) + self.land1[a][0].astype(F32)
            for r in range(1, N_CHIPS):
                self.send2[a][r - 1] = (self.own1[a][r].astype(F32) + self.land1[a][r].astype(F32)).astype(BF16)
        if self.packed is not None:
            self.pland2[0] = self.packed[...] + self.pland1[...]
        for cp in self._copies2():
            cp.start()

    def finish2(self):
        for cp in self._copies2():
            cp.wait()
        for a in range(self.n):
            l2 = self.land2[a]
            self.outs[a][...] = self.outs[a][...] + ((l2[0].astype(F32) + l2[1].astype(F32)) + l2[2].astype(F32))
        if self.packed is not None:
            p2 = self.pland2
            self.ptot[...] = (p2[0] + p2[1]) + (p2[2] + p2[3])


def _adamw(ws, gs, ms, vs):
    n = len(ws)

    def body(*refs):
        for k in range(n):
            w, g, m, v = (refs[j * n + k][...] for j in range(4))
            d_ref, nm_ref, nv_ref = (refs[(4 + j) * n + k] for j in range(3))
            m = ADAM_B1 * m + (1.0 - ADAM_B1) * g
            v = ADAM_B2 * v + (1.0 - ADAM_B2) * jnp.square(g)
            m_hat = m / (1.0 - ADAM_B1 ** ADAM_STEP)
            v_hat = v / (1.0 - ADAM_B2 ** ADAM_STEP)
            d_ref[...] = -ADAM_LR * (m_hat / (jnp.sqrt(v_hat) + ADAM_EPS) + ADAM_WD * w)
            nm_ref[...] = m
            nv_ref[...] = v

    outs = pl.pallas_call(
        body, name="adamw", in_specs=[VMEM_SPEC] * (4 * n), out_specs=[VMEM_SPEC] * (3 * n),
        out_shape=[SDS(w.shape, F32) for w in ws] * 3, compiler_params=_cp(),
    )(*ws, *gs, *ms, *vs)
    return outs[:n], outs[n:2 * n], outs[2 * n:]


WEIGHTS = ("norm_in", "w_in", "q_norm", "w_uq", "kv_norm", "w_ukv", "pool_w", "pool_scale", "w_branch_attn", "w_branch_pool",
           "w_out", "norm_final")
SUBLANES = 8


def _cols_to_slabs(g):
    r = g.shape[0]
    return g.reshape(r, N_DEV, -1).transpose(1, 0, 2)


def _slabs_to_cols(s):
    return s.transpose(1, 0, 2).reshape(s.shape[1], -1)


def _pack_rows(a):
    a = a.reshape(-1, LANES)
    return jnp.pad(a, ((0, -a.shape[0] % SUBLANES), (0, 0)))


def _unpack_rows(packed, like):
    out, row = [], 0
    for a in like:
        rows = a.size // LANES
        out.append(packed[row:row + rows].reshape(a.shape))
        row += rows + (-rows % SUBLANES)
    return out


def kernel(x, norm_in, w_in, q_norm, w_uq, kv_norm, w_ukv, pool_w, pool_scale, w_branch_attn, w_branch_pool, w_out, norm_final, loss_target, m_norm_in, m_w_in, m_q_norm, m_w_uq, m_kv_norm, m_w_ukv, m_pool_w, m_pool_scale, m_w_branch_attn, m_w_branch_pool, m_w_out, m_norm_final, v_norm_in, v_w_in, v_q_norm, v_w_uq, v_kv_norm, v_w_ukv, v_pool_w, v_pool_scale, v_w_branch_attn, v_w_branch_pool, v_w_out, v_norm_final):
    w = dict(norm_in=norm_in, w_in=w_in, q_norm=q_norm, w_uq=w_uq, kv_norm=kv_norm, w_ukv=w_ukv, pool_w=pool_w, pool_scale=pool_scale,
             w_branch_attn=w_branch_attn, w_branch_pool=w_branch_pool, w_out=w_out, norm_final=norm_final)
    m = dict(norm_in=m_norm_in, w_in=m_w_in, q_norm=m_q_norm, w_uq=m_w_uq, kv_norm=m_kv_norm, w_ukv=m_w_ukv, pool_w=m_pool_w,
             pool_scale=m_pool_scale, w_branch_attn=m_w_branch_attn, w_branch_pool=m_w_branch_pool, w_out=m_w_out, norm_final=m_norm_final)
    v = dict(norm_in=v_norm_in, w_in=v_w_in, q_norm=v_q_norm, w_uq=v_w_uq, kv_norm=v_kv_norm, w_ukv=v_w_ukv, pool_w=v_pool_w,
             pool_scale=v_pool_scale, w_branch_attn=v_w_branch_attn, w_branch_pool=v_w_branch_pool, w_out=v_w_out, norm_final=v_norm_final)

    def as2d(name, a):
        if name == "w_in":
            return a.T
        if name in ("w_uq", "w_ukv"):
            return a.reshape(a.shape[0], -1)
        if name == "pool_w":
            return a.reshape(-1, GROUP)
        return a.reshape(1, -1) if a.ndim == 1 else a

    def unshape(name, a):
        return a.T if name == "w_in" else a.reshape(w[name].shape)

    w_in_pad, w_uq_full, w_ukv_full = _all_gather_bf16([as2d(k, w[k]) for k in ("w_in", "w_uq", "w_ukv")])
    loss, grad_x, g2d = _local_step(
        x.reshape(x.shape[1:]), loss_target.reshape(x.shape[1:]), norm_in, w_in_pad, q_norm,
        w_uq_full.reshape(Q_RANK, HEADS, NOPE + ROPE), kv_norm, w_ukv_full.reshape(KV_RANK, HEADS, NOPE + VDIM),
        pool_w, pool_scale, [w_branch_attn, w_branch_pool, w_out], norm_final)

    deltas, new_m, new_v = _adamw([as2d(k, w[k]) for k in WEIGHTS], [g2d[k] for k in WEIGHTS],
                                  [as2d(k, m[k]) for k in WEIGHTS], [as2d(k, v[k]) for k in WEIGHTS])
    shaped = lambda arrs: [unshape(k, a) for k, a in zip(WEIGHTS, arrs)]
    return (loss, grad_x.reshape(x.shape), *shaped([g2d[k] for k in WEIGHTS]), *shaped(deltas), *shaped(new_m), *shaped(new_v))
```

```python
import functools

import jax
import jax.numpy as jnp
import numpy as np
from jax import lax
from jax.experimental import pallas as pl
from jax.experimental.pallas import tpu as pltpu

F32 = jnp.float32
BF16 = jnp.bfloat16
SDS = jax.ShapeDtypeStruct

D_MODEL = 1024
HEADS = 8
NOPE = 64
ROPE = 32
VDIM = 64
Q_RANK = 384
KV_RANK = 256
MLA_W = HEADS * VDIM
POOL_W = 512
POOL_GROUPS = 4
GROUP = POOL_W // POOL_GROUPS
CHUNK = 64
ROPE_THETA = 10000.0
EPS = 1e-6
SCALE = (NOPE + ROPE) ** -0.5
LOG2E = 1.4426950408889634
LN2 = 0.6931471805599453
QK_SCALE_LOG2 = SCALE * LOG2E
IN_TOTAL = 4256
ADAM_LR, ADAM_B1, ADAM_B2, ADAM_EPS, ADAM_WD, ADAM_STEP = 0.001, 0.9, 0.999, 1e-08, 0.01, 10

N_DEV = 8
LANES = 128
HEAD_PAD = LANES
HW = HEADS * HEAD_PAD

ZQ, ZKV, ZKR, GA, UP, GP, GM, ZTOT = 0, 384, 640, 768, 1280, 1792, 2304, 4352
FRONT_W = GA
ZKR_ORIG = 640

VMEM_LIMIT = 56 * 1024 * 1024


def _cp(sem=None, **kw):
    if sem is not None:
        kw["dimension_semantics"] = sem
    return pltpu.CompilerParams(vmem_limit_bytes=VMEM_LIMIT, **kw)


def _mm(a, b):
    return lax.dot_general(a, b, (((1,), (0,)), ((), ())), preferred_element_type=F32)


def _mm_nt(a, b):
    return lax.dot_general(a, b, (((1,), (1,)), ((), ())), preferred_element_type=F32)


def _mm_tn(a, b):
    return lax.dot_general(a, b, (((0,), (0,)), ((), ())), preferred_element_type=F32)


def _row_spec(tm, w):
    return pl.BlockSpec((tm, w), lambda i: (i, 0))


def _full_spec(a):
    nd = len(a.shape)
    return pl.BlockSpec(a.shape, lambda *_: (0,) * nd)


def _rope(v, c, sa, sb, sign):
    n = v.shape[-1]
    reps = n // LANES
    if reps > 1:
        c, sa, sb = (jnp.tile(t, (1, reps)) for t in (c, sa, sb))
    up = pltpu.roll(v, n - ROPE // 2, 1)
    dn = pltpu.roll(v, ROPE // 2, 1)
    return v * c + sign * (up * sa + dn * sb)


def _rope_tables(T):
    half = ROPE // 2
    inv_freq = np.float32(ROPE_THETA) ** (-np.arange(half, dtype=np.float32) / np.float32(half))
    ang = np.arange(T, dtype=np.float32)[:, None] * inv_freq[None, :].astype(np.float32)
    cos, sin = np.cos(ang.astype(np.float64)).astype(np.float32), np.sin(ang.astype(np.float64)).astype(np.float32)
    z16 = np.zeros((T, half), np.float32)
    z32 = np.zeros((T, LANES - NOPE - ROPE), np.float32)
    c = np.concatenate([np.ones((T, NOPE), np.float32), cos, cos, z32], axis=1)
    sa = np.concatenate([np.zeros((T, NOPE), np.float32), -sin, z16, z32], axis=1)
    sb = np.concatenate([np.zeros((T, NOPE), np.float32), z16, sin, z32], axis=1)
    return jnp.asarray(c), jnp.asarray(sa), jnp.asarray(sb)


def _silu_parts(g):
    sg = jax.nn.sigmoid(g)
    return g * sg, sg + g * sg * (1.0 - sg)


def _in_proj(x2, norm_in, w_in_pad, q_norm, wuq_pad, kv_norm, wukv, rc, rsa, rsb, tm, late_shards):
    T = x2.shape[0]
    steps = T // tm
    n = len(late_shards)

    def body(x_ref, nin_ref, win_ref, qn_ref, wuq_ref, kvn_ref, wukv_ref, c_ref, sa_ref, sb_ref, *rest):
        hn_ref, zgm_ref, zga_ref, zup_ref, zgp_ref, zfr_ref, q_ref, k_ref, v_ref = rest[n:n + 9]
        ag = _AllGather(rest[:n], rest[n + 9:2 * n + 9], rest[2 * n + 9:])
        step = pl.program_id(0)
        pl.when(step == 0)(ag.start)
        pl.when(step == min(5, steps - 1))(ag.forward)
        xf = x_ref[...]
        r = lax.rsqrt(jnp.mean(xf * xf, axis=-1, keepdims=True) + EPS)
        hn = (xf * r * nin_ref[...]).astype(BF16)
        hn_ref[...] = hn
        z = _mm_nt(hn, win_ref[...])
        zgm_ref[...] = z[:, GM:ZTOT]
        zga_ref[...] = z[:, GA:UP]
        zup_ref[...] = z[:, UP:GP]
        zgp_ref[...] = z[:, GP:GM]
        zfr_ref[...] = z[:, ZQ:GA]
        zq, zkv, zkr = z[:, ZQ:ZKV], z[:, ZKV:ZKR], z[:, ZKR:GA]
        c, sa, sb = c_ref[...], sa_ref[...], sb_ref[...]
        rq = lax.rsqrt(jnp.mean(zq * zq, axis=-1, keepdims=True) + EPS)
        cq = (zq * rq * qn_ref[...]).astype(BF16)
        q = _rope(_mm(cq, wuq_ref[...]), c, sa, sb, 1.0)
        q_ref[...] = (q * QK_SCALE_LOG2).astype(BF16)
        rkv = lax.rsqrt(jnp.mean(zkv * zkv, axis=-1, keepdims=True) + EPS)
        ckv = (zkv * rkv * kvn_ref[...]).astype(BF16)
        kv = _mm(ckv, wukv_ref[...])
        kr = _rope(zkr, c, sa, sb, 1.0)
        lane = lax.broadcasted_iota(jnp.int32, kv.shape, 1) % LANES
        k_ref[...] = jnp.where(lane < NOPE, kv, jnp.tile(kr, (1, HEADS))).astype(BF16)
        v_ref[...] = jnp.where(lane < NOPE, 1.0, kv).astype(BF16)
        pl.when(step == steps - 1)(ag.finish)

    ins = (x2, norm_in, w_in_pad, q_norm, wuq_pad, kv_norm, wukv, rc, rsa, rsb)
    in_specs = [_row_spec(tm, D_MODEL), _full_spec(norm_in), _full_spec(w_in_pad), _full_spec(q_norm), _full_spec(wuq_pad),
                _full_spec(kv_norm), _full_spec(wukv), _row_spec(tm, LANES), _row_spec(tm, LANES), _row_spec(tm, LANES)]
    widths = [(D_MODEL, BF16), (ZTOT - GM, F32), (UP - GA, F32), (GP - UP, F32), (GM - GP, F32), (FRONT_W, F32),
              (HW, BF16), (HW, BF16), (HW, BF16)]
    return pl.pallas_call(
        body, name="in_proj", grid=(steps,), in_specs=in_specs + [_full_spec(s) for s in late_shards],
        out_specs=[_row_spec(tm, w) for w, _ in widths] + [HBM_SPEC] * n,
        out_shape=[SDS((T, w), dt) for w, dt in widths] + [SDS((N_DEV,) + s.shape, BF16) for s in late_shards],
        scratch_shapes=_ag_scratch([s.shape for s in late_shards]), compiler_params=_cp(("arbitrary",)),
    )(*ins, *late_shards)


def _chunk_mask(n_q, n_k, q_off, transposed):
    shape = (n_k, n_q) if transposed else (n_q, n_k)
    q = (lax.broadcasted_iota(jnp.int32, shape, 1 if transposed else 0) + q_off) // CHUNK
    k = lax.broadcasted_iota(jnp.int32, shape, 0 if transposed else 1) // CHUNK
    return k <= q


def _store_pair_rows(ref, k, pair):
    t = pair.T
    ref[k, 0:1, :] = t[0:1, :]
    ref[k, 1:2, :] = t[VDIM:VDIM + 1, :]


def _attn_fwd(q_att, k_att, v_att, tq, hps):
    T = q_att.shape[0]
    head_lanes = [slice(h * LANES, (h + 1) * LANES) for h in range(hps)]

    def body(q_ref, k_ref, v_ref, o_ref, lser_ref):
        i = pl.program_id(1)
        mask = _chunk_mask(tq, tq, 0, False)
        lane = lax.broadcasted_iota(jnp.int32, (tq, LANES), 1)
        qs = [q_ref[:, hs] for hs in head_lanes]

        def step(j, carry, masked):
            rows = pl.ds(pl.multiple_of(j * tq, tq), tq)
            out = []
            for (m, acc), qh, hs in zip(carry, qs, head_lanes):
                s = _mm_nt(qh, k_ref[rows, hs])
                if masked:
                    s = jnp.where(mask, s, -jnp.inf)
                m_new = jnp.maximum(m, jnp.max(s, axis=-1, keepdims=True))
                p = jnp.exp2(s - m_new).astype(BF16)
                out.append((m_new, jnp.exp2(m - m_new) * acc + _mm(p, v_ref[rows, hs])))
            return tuple(out)

        init = ((jnp.full((tq, 1), -jnp.inf, F32), jnp.zeros((tq, LANES), F32)),) * hps
        res = step(i, lax.fori_loop(0, i, functools.partial(step, masked=False), init), True)
        for pair in range(hps // 2):
            (ma, acca), (mb, accb) = res[2 * pair], res[2 * pair + 1]
            la, lb = acca[:, :1], accb[:, :1]
            o_ref[:, pair * LANES:(pair + 1) * LANES] = jnp.where(lane < VDIM, pltpu.roll(acca / la, VDIM, 1), accb / lb)
            _store_pair_rows(lser_ref, pair, jnp.where(lane < VDIM, ma + jnp.log2(la), mb + jnp.log2(lb)))

    qspec = pl.BlockSpec((tq, hps * LANES), lambda p, i: (i, p))
    kspec = pl.BlockSpec((T, hps * LANES), lambda p, i: (0, p))
    ospec = pl.BlockSpec((tq, hps * VDIM), lambda p, i: (i, p))
    return pl.pallas_call(
        body, name="attn_fwd", grid=(HEADS // hps, T // tq), in_specs=[qspec, kspec, kspec],
        out_specs=[ospec, pl.BlockSpec((hps // 2, 2, tq), lambda p, i: (p, 0, i))],
        out_shape=[SDS((T, MLA_W), F32), SDS((HEADS // 2, 2, T), F32)],
        compiler_params=_cp(("parallel", "parallel")),
    )(q_att, k_att, v_att)


def _pick(g, vals):
    out = vals[-1]
    for k in range(len(vals) - 2, -1, -1):
        out = jnp.where(g == k, vals[k], out)
    return out


def _window_sum(u, g, forward):
    T = u.shape[0]
    row = lax.broadcasted_iota(jnp.int32, u.shape, 0)

    def sh(s, k):
        if forward:
            return jnp.where(row >= k, pltpu.roll(s, k, 0), 0.0)
        return jnp.where(row < T - k, pltpu.roll(s, T - k, 0), 0.0)

    sums, s = [], u
    for k in (1, 2, 4, 8):
        s = s + sh(s, k)
        sums.append(s)
    return _pick(g, sums)


def _pool_count(shape, g):
    row = lax.broadcasted_iota(jnp.int32, shape, 0)
    return jnp.minimum(row + 1, lax.shift_left(jnp.int32(2), g)).astype(F32)


def _pool_fwd(zup, zgp, pool_w, pool_scale):
    T = zup.shape[0]

    def body(u_ref, g_ref, w_ref, sc_ref, y_ref):
        g = pl.program_id(0)
        u = u_ref[...]
        d = _window_sum(u, g, True) / _pool_count(u.shape, g) - u
        lin = _mm(d.astype(BF16), w_ref[0].astype(BF16))
        silu, _ = _silu_parts(g_ref[...])
        y_ref[...] = (lin * sc_ref[...] * silu).astype(BF16)

    col = pl.BlockSpec((T, GROUP), lambda g: (0, g))
    return pl.pallas_call(
        body, name="pool_fwd", grid=(POOL_GROUPS,),
        in_specs=[col, col, pl.BlockSpec((1, GROUP, GROUP), lambda g: (g, 0, 0)), pl.BlockSpec((1, GROUP), lambda g: (0, g))],
        out_specs=col, out_shape=SDS((T, POOL_W), BF16), compiler_params=_cp(("parallel",)),
    )(zup, zgp, pool_w, pool_scale)


def _pool_bwd(zup, zgp, dyp, pool_w, pool_scale):
    T = zup.shape[0]

    def body(u_ref, g_ref, dy_ref, w_ref, sc_ref, du_ref, dg_ref, gw_ref, gsc_ref):
        g = pl.program_id(0)
        u = u_ref[...]
        cnt = _pool_count(u.shape, g)
        d = (_window_sum(u, g, True) / cnt - u).astype(BF16)
        wb = w_ref[0].astype(BF16)
        lin = _mm(d, wb)
        sc = sc_ref[...]
        silu, dsilu = _silu_parts(g_ref[...])
        dy = dy_ref[...]
        dg_ref[...] = (dy * lin * sc * dsilu).astype(BF16)
        dpre = dy * silu
        gsc_ref[...] = jnp.sum(dpre * lin, axis=0, keepdims=True)
        dlin = (dpre * sc).astype(BF16)
        gw_ref[0] = _mm_tn(d, dlin)
        dd = _mm_nt(dlin, wb)
        du_ref[...] = (_window_sum(dd / cnt, g, False) - dd).astype(BF16)

    col = pl.BlockSpec((T, GROUP), lambda g: (0, g))
    wspec = pl.BlockSpec((1, GROUP, GROUP), lambda g: (g, 0, 0))
    vspec = pl.BlockSpec((1, GROUP), lambda g: (0, g))
    return pl.pallas_call(
        body, name="pool_bwd", grid=(POOL_GROUPS,), in_specs=[col, col, col, wspec, vspec], out_specs=[col, col, wspec, vspec],
        out_shape=[SDS((T, POOL_W), BF16), SDS((T, POOL_W), BF16), SDS((POOL_GROUPS, GROUP, GROUP), F32), SDS((1, POOL_W), F32)],
        compiler_params=_cp(("parallel",)),
    )(zup, zgp, dyp, pool_w, pool_scale)


def _tail(x2, tgt, o, zga, ypool, zgm, wba, wbp, wout, norm_final, tm):
    T = x2.shape[0]

    def body(x_ref, tgt_ref, o_ref, zga_ref, yp_ref, zgm_ref, wba_ref, wbp_ref, wout_ref, nf_ref,
             loss_ref, dh_ref, dgm_ref, doop_ref, dga_ref, dcapr_ref, dyp_ref, gwout_ref, gwba_ref, gwbp_ref, gnf_ref):
        @pl.when(pl.program_id(0) == 0)
        def _():
            for ref in (loss_ref, gwout_ref, gwba_ref, gwbp_ref, gnf_ref):
                ref[...] = jnp.zeros_like(ref)

        o_v = o_ref[...]
        silu, dsilu = _silu_parts(zga_ref[...])
        ya = (o_v * silu).astype(BF16)
        yp = yp_ref[...]
        wba_v, wbp_v, wout_v = wba_ref[...], wbp_ref[...], wout_ref[...]
        a = _mm(ya, wba_v)
        p = _mm(yp, wbp_v)
        gate = jax.nn.sigmoid(zgm_ref[...])
        ga, gp = gate[:, :D_MODEL], gate[:, D_MODEL:]
        mg = (ga * a + gp * p).astype(BF16)
        h = x_ref[...] + _mm(mg, wout_v)
        r = lax.rsqrt(jnp.mean(h * h, axis=-1, keepdims=True) + EPS)
        gf = nf_ref[...]
        hr = h * r
        e = hr * gf - tgt_ref[...]
        loss_ref[...] += (0.5 / D_MODEL) * jnp.sum(e * e)
        dy = e * (1.0 / D_MODEL)
        gnf_ref[...] += jnp.sum(dy * hr, axis=0, keepdims=True)
        u = dy * gf
        dh = r * (u - hr * jnp.mean(u * hr, axis=-1, keepdims=True))
        dh_ref[...] = dh
        dhb = dh.astype(BF16)
        dmg = _mm_nt(dhb, wout_v)
        gwout_ref[...] += _mm_tn(mg, dhb)
        dgm_ref[:, :D_MODEL] = (dmg * a * ga * (1.0 - ga)).astype(BF16)
        dgm_ref[:, D_MODEL:] = (dmg * p * gp * (1.0 - gp)).astype(BF16)
        dab = (dmg * ga).astype(BF16)
        dpb = (dmg * gp).astype(BF16)
        dya = _mm_nt(dab, wba_v)
        gwba_ref[...] += _mm_tn(ya, dab)
        dyp_ref[...] = _mm_nt(dpb, wbp_v)
        gwbp_ref[...] += _mm_tn(yp, dpb)
        do = dya * silu
        dga_ref[...] = (dya * o_v * dsilu).astype(BF16)
        prod = do * o_v
        lo = lax.broadcasted_iota(jnp.int32, (tm, LANES), 1) < VDIM
        for pair in range(HEADS // 2):
            ls = slice(pair * LANES, (pair + 1) * LANES)
            do_p, prod_p = do[:, ls], prod[:, ls]
            dcap_a = jnp.sum(jnp.where(lo, prod_p, 0.0), axis=-1, keepdims=True)
            dcap_b = jnp.sum(jnp.where(lo, 0.0, prod_p), axis=-1, keepdims=True)
            _store_pair_rows(dcapr_ref, pair, jnp.where(lo, dcap_a, dcap_b))
            doop_ref[:, 2 * pair * LANES:(2 * pair + 1) * LANES] = jnp.where(lo, 0.0, pltpu.roll(do_p, VDIM, 1)).astype(BF16)
            doop_ref[:, (2 * pair + 1) * LANES:(2 * pair + 2) * LANES] = jnp.where(lo, 0.0, do_p).astype(BF16)

    ins = (x2, tgt, o, zga, ypool, zgm, wba, wbp, wout, norm_final)
    in_specs = [_row_spec(tm, D_MODEL), _row_spec(tm, D_MODEL), _row_spec(tm, MLA_W), _row_spec(tm, MLA_W), _row_spec(tm, POOL_W),
                _row_spec(tm, 2 * D_MODEL), _full_spec(wba), _full_spec(wbp), _full_spec(wout), _full_spec(norm_final)]
    outs = [SDS((8, LANES), F32), SDS((T, D_MODEL), F32), SDS((T, 2 * D_MODEL), BF16), SDS((T, HW), BF16), SDS((T, MLA_W), BF16),
            SDS((HEADS // 2, 2, T), F32), SDS((T, POOL_W), F32),
            SDS((D_MODEL, D_MODEL), F32), SDS((MLA_W, D_MODEL), F32), SDS((POOL_W, D_MODEL), F32), SDS((1, D_MODEL), F32)]
    out_specs = [_full_spec(outs[0]), _row_spec(tm, D_MODEL), _row_spec(tm, 2 * D_MODEL), _row_spec(tm, HW), _row_spec(tm, MLA_W),
                 pl.BlockSpec((HEADS // 2, 2, tm), lambda i: (0, 0, i)), _row_spec(tm, POOL_W),
                 _full_spec(outs[7]), _full_spec(outs[8]), _full_spec(outs[9]), _full_spec(outs[10])]
    return pl.pallas_call(
        body, name="tail", grid=(T // tm,), in_specs=in_specs, out_specs=out_specs, out_shape=outs,
        compiler_params=_cp(("arbitrary",)),
    )(*ins)


def _attn_bwd(q_att, k_att, v_att, doop, lse_rows, dcap_rows, tq, hps, slabs, packed):
    T = q_att.shape[0]
    nq = T // tq
    n = len(slabs)
    groups = HEADS // hps
    head_lanes = [slice(h * LANES, (h + 1) * LANES) for h in range(hps)]

    def body(q_ref, k_ref, v_ref, doop_ref, lse_ref, dcap_ref, *rest):
        slab_refs, packed_ref = rest[:n], rest[n]
        dq_ref, dkv_ref, dkr_ref = rest[n + 1:n + 4]
        sum_refs, ptot_ref = rest[n + 4:2 * n + 4], rest[2 * n + 4]
        dq_acc = rest[2 * n + 5]
        rs = _ReduceScatter(slab_refs, packed_ref, sum_refs, ptot_ref, rest[2 * n + 6:])
        group, j = pl.program_id(0), pl.program_id(1)
        pl.when((group == 0) & (j == 0))(rs.start1)
        pl.when((group == 1) & (j == 0))(rs.finish1_start2)
        mask = _chunk_mask(tq, tq, 0, True)
        lane = lax.broadcasted_iota(jnp.int32, (tq, LANES), 1)
        ks = [k_ref[:, hs] for hs in head_lanes]
        vs = [v_ref[:, hs] for hs in head_lanes]

        @pl.when(j == 0)
        def _():
            dq_acc[...] = jnp.zeros_like(dq_acc)

        def step(i, carry, masked):
            rows = pl.ds(pl.multiple_of(i * tq, tq), tq)
            out = []
            for h, ((dk, dv), kh, vh, hs) in enumerate(zip(carry, ks, vs, head_lanes)):
                qh = q_ref[rows, hs]
                doop_h = doop_ref[rows, hs]
                stat = (h // 2, slice(h % 2, h % 2 + 1), rows)
                pt = jnp.exp2(_mm_nt(kh, qh) - lse_ref[stat])
                if masked:
                    pt = jnp.where(mask, pt, 0.0)
                dv = dv + _mm(pt.astype(BF16), doop_h)
                dpt = _mm_nt(vh, doop_h)
                dst = (pt * (dpt - dcap_ref[stat])).astype(BF16)
                dq_acc[rows, hs] += _mm_tn(dst, kh)
                out.append((dk + _mm(dst, qh), dv))
            return tuple(out)

        zero = jnp.zeros((tq, LANES), F32)
        carry = step(j, ((zero, zero),) * hps, True)
        res = lax.fori_loop(j + 1, nq, functools.partial(step, masked=False), carry)
        dkr = None
        for (dk, dv), hs in zip(res, head_lanes):
            dk = dk * LN2
            dkv_ref[:, hs] = jnp.where(lane < NOPE, dk, dv).astype(BF16)
            dkr = dk if dkr is None else dkr + dk
        dkr_ref[0] = jnp.where((lane >= NOPE) & (lane < NOPE + ROPE), dkr, 0.0)

        @pl.when(j == nq - 1)
        def _():
            dq_ref[...] = (dq_acc[...] * SCALE).astype(BF16)

        pl.when((group == groups - 1) & (j == nq - 1))(rs.finish2)

    kspec = pl.BlockSpec((tq, hps * LANES), lambda p, j: (j, p))
    qspec = pl.BlockSpec((T, hps * LANES), lambda p, j: (0, p))
    rspec = pl.BlockSpec((hps // 2, 2, T), lambda p, j: (p, 0, 0))
    sums = [SDS(s.shape[1:], F32) for s in slabs] + [SDS(packed.shape, F32)]
    return pl.pallas_call(
        body, name="attn_bwd", grid=(groups, nq),
        in_specs=[qspec, kspec, kspec, qspec, rspec, rspec] + [HBM_SPEC] * n + [_full_spec(packed)],
        out_specs=[qspec, kspec, pl.BlockSpec((1, tq, LANES), lambda p, j: (p, j, 0))] + [_full_spec(s) for s in sums],
        out_shape=[SDS((T, HW), BF16), SDS((T, HW), BF16), SDS((groups, T, LANES), F32)] + sums,
        scratch_shapes=[pltpu.VMEM((T, hps * LANES), F32)] + _rs_scratch([s.shape for s in sums[:-1]], packed.shape),
        compiler_params=_cp(("arbitrary", "arbitrary")),
    )(q_att, k_att, v_att, doop, lse_rows, dcap_rows, *slabs, packed)


def _rms_bwd(z, gain, dout):
    r = lax.rsqrt(jnp.mean(z * z, axis=-1, keepdims=True) + EPS)
    zr = z * r
    u = dout * gain
    return r * (u - zr * jnp.mean(u * zr, axis=-1, keepdims=True)), jnp.sum(dout * zr, axis=0, keepdims=True)


def _mla_bwd(dq_att, dkv_nat, dkr4, zfr, q_norm, wuq_pad, kv_norm, wukv, rc, rsa, rsb, tm):
    T = dq_att.shape[0]

    def body(dq_ref, dkv_ref, dkr_ref, zfr_ref, qn_ref, wuq_ref, kvn_ref, wukv_ref, c_ref, sa_ref, sb_ref,
             dfr_ref, gwuq_ref, gwukv_ref, gqn_ref, gkvn_ref):
        @pl.when(pl.program_id(0) == 0)
        def _():
            for ref in (gwuq_ref, gwukv_ref, gqn_ref, gkvn_ref):
                ref[...] = jnp.zeros_like(ref)

        c, sa, sb = c_ref[...], sa_ref[...], sb_ref[...]
        zq, zkv = zfr_ref[:, :Q_RANK], zfr_ref[:, Q_RANK:Q_RANK + KV_RANK]
        qn, kvn = qn_ref[...], kvn_ref[...]
        cq = (zq * lax.rsqrt(jnp.mean(zq * zq, axis=-1, keepdims=True) + EPS) * qn).astype(BF16)
        ckv = (zkv * lax.rsqrt(jnp.mean(zkv * zkv, axis=-1, keepdims=True) + EPS) * kvn).astype(BF16)
        dq = _rope(dq_ref[...].astype(F32), c, sa, sb, -1.0).astype(BF16)
        gwuq_ref[...] += _mm_tn(cq, dq)
        dzq, gqn = _rms_bwd(zq, qn, _mm_nt(dq, wuq_ref[...]))
        gqn_ref[...] += gqn
        dkv = dkv_ref[...]
        gwukv_ref[...] += _mm_tn(ckv, dkv)
        dzkv, gkvn = _rms_bwd(zkv, kvn, _mm_nt(dkv, wukv_ref[...]))
        gkvn_ref[...] += gkvn
        dkr = functools.reduce(lambda a, b: a + b, [dkr_ref[g] for g in range(dkr4.shape[0])])
        dfr_ref[:, :Q_RANK] = dzq.astype(BF16)
        dfr_ref[:, Q_RANK:Q_RANK + KV_RANK] = dzkv.astype(BF16)
        dfr_ref[:, Q_RANK + KV_RANK:] = _rope(dkr, c, sa, sb, -1.0).astype(BF16)

    ins = (dq_att, dkv_nat, dkr4, zfr, q_norm, wuq_pad, kv_norm, wukv, rc, rsa, rsb)
    in_specs = [_row_spec(tm, HW), _row_spec(tm, HW), pl.BlockSpec((dkr4.shape[0], tm, LANES), lambda i: (0, i, 0)), _row_spec(tm, FRONT_W),
                _full_spec(q_norm), _full_spec(wuq_pad), _full_spec(kv_norm), _full_spec(wukv),
                _row_spec(tm, LANES), _row_spec(tm, LANES), _row_spec(tm, LANES)]
    outs = [SDS((T, FRONT_W), BF16), SDS((Q_RANK, HW), F32), SDS((KV_RANK, HW), F32), SDS((1, Q_RANK), F32), SDS((1, KV_RANK), F32)]
    out_specs = [_row_spec(tm, FRONT_W)] + [_full_spec(s) for s in outs[1:]]
    return pl.pallas_call(
        body, name="mla_bwd", grid=(T // tm,), in_specs=in_specs, out_specs=out_specs, out_shape=outs,
        compiler_params=_cp(("arbitrary",)),
    )(*ins)


_DZ_COLS = ((GM, ZTOT), (GA, UP), (UP, GP), (GP, GM), (ZQ, GA))


def _in_proj_bwd_x(dzs, x2, dh, norm_in, w_in_pad, tm, slabs):
    T = x2.shape[0]
    steps = T // tm
    n = len(slabs)

    def body(d0, d1, d2, d3, d4, x_ref, dh_ref, nin_ref, win_ref, *rest):
        slab_refs, (gx_ref, gnin_ref), sum_refs = rest[:n], rest[n:n + 2], rest[n + 2:2 * n + 2]
        rs = _ReduceScatter(slab_refs, None, sum_refs, None, rest[2 * n + 2:])
        step = pl.program_id(0)

        @pl.when(step == 0)
        def _():
            gnin_ref[...] = jnp.zeros_like(gnin_ref)
            rs.start1()

        pl.when(step == min(2, steps - 1))(rs.finish1_start2)
        dhn = None
        for ref, (lo, hi) in zip((d0, d1, d2, d3, d4), _DZ_COLS):
            t = _mm(ref[...], win_ref[lo:hi, :])
            dhn = t if dhn is None else dhn + t
        dx, gnin = _rms_bwd(x_ref[...], nin_ref[...], dhn)
        gnin_ref[...] += gnin
        gx_ref[...] = dx + dh_ref[...]
        pl.when(step == steps - 1)(rs.finish2)

    in_specs = [_row_spec(tm, hi - lo) for lo, hi in _DZ_COLS] + [_row_spec(tm, D_MODEL), _row_spec(tm, D_MODEL),
                                                                  _full_spec(norm_in), _full_spec(w_in_pad)] + [HBM_SPEC] * n
    sums = [SDS(s.shape[1:], F32) for s in slabs]
    outs = [SDS((T, D_MODEL), F32), SDS((1, D_MODEL), F32)] + sums
    return pl.pallas_call(
        body, name="in_proj_bwd_x", grid=(steps,), in_specs=in_specs,
        out_specs=[_row_spec(tm, D_MODEL), _full_spec(outs[1])] + [_full_spec(s) for s in sums],
        out_shape=outs, scratch_shapes=_rs_scratch([s.shape for s in sums], None), compiler_params=_cp(("arbitrary",)),
    )(*dzs, x2, dh, norm_in, w_in_pad, *slabs)


SLAB_ROWS = IN_TOTAL // N_DEV


def _slab_segments(k):
    cuts = [(0, ZKR_ORIG, 0), (ZKR_ORIG, ZKR_ORIG + ROPE, NOPE), (ZKR_ORIG + ROPE, IN_TOTAL, LANES - ROPE)]
    lo, hi = k * SLAB_ROWS, (k + 1) * SLAB_ROWS
    return [(max(lo, a) - lo, max(lo, a) + shift, min(hi, b) - max(lo, a)) for a, b, shift in cuts if min(hi, b) > max(lo, a)]


def _in_proj_bwd_w(dzs, hn, tm):
    T = hn.shape[0]
    steps = T // tm

    def body(d0, d1, d2, d3, d4, hn_ref, slab_ref, acc_ref):
        @pl.when(pl.program_id(0) == 0)
        def _():
            acc_ref[...] = jnp.zeros_like(acc_ref)

        hn_v = hn_ref[...]
        for ref, (lo, hi) in zip((d0, d1, d2, d3, d4), _DZ_COLS):
            acc_ref[lo:hi, :] += _mm_tn(ref[...], hn_v)

        @pl.when(pl.program_id(0) == steps - 1)
        def _():
            for k in range(N_DEV):
                for at, src, rows in _slab_segments(k):
                    slab_ref[k, at:at + rows, :] = acc_ref[src:src + rows, :].astype(BF16)

    in_specs = [_row_spec(tm, hi - lo) for lo, hi in _DZ_COLS] + [_row_spec(tm, D_MODEL)]
    out = SDS((N_DEV, SLAB_ROWS, D_MODEL), BF16)
    return pl.pallas_call(
        body, name="in_proj_bwd_w", grid=(steps,), in_specs=in_specs, out_specs=_full_spec(out), out_shape=out,
        scratch_shapes=[pltpu.VMEM((ZTOT, D_MODEL), F32)], compiler_params=_cp(("arbitrary",)),
    )(*dzs, hn)


def _local_step(x2, tgt, norm_in, w_in_pad, q_norm, w_uq, kv_norm, w_ukv, pool_w, pool_scale, late_shards, norm_final):
    T = x2.shape[0]
    tm = min(512, T)
    tq = min(512, T)
    row = lambda v: v.reshape(1, -1)
    wuq_pad = jnp.pad(w_uq, ((0, 0), (0, 0), (0, HEAD_PAD - NOPE - ROPE))).reshape(Q_RANK, HW)
    wukv = w_ukv.reshape(KV_RANK, HW)
    rc, rsa, rsb = _rope_tables(T)

    hn, zgm, zga, zup, zgp, zfr, q_att, k_att, v_att, w_ba, w_bp, w_out = _in_proj(
        x2, row(norm_in), w_in_pad, row(q_norm), wuq_pad, row(kv_norm), wukv, rc, rsa, rsb, tm, late_shards)
    w_ba, w_bp, w_out = _slabs_to_cols(w_ba), _slabs_to_cols(w_bp), w_out.reshape(D_MODEL, D_MODEL)
    o, lse_rows = _attn_fwd(q_att, k_att, v_att, tq, 4)
    ypool = _pool_fwd(zup, zgp, pool_w, row(pool_scale))
    loss8, dh, dgm, doop, dga, dcap_rows, dyp, g_wout, g_wba, g_wbp, g_nf = _tail(
        x2, tgt, o, zga, ypool, zgm, w_ba, w_bp, w_out, row(norm_final), min(256, T))
    dup, dgp, g_pool_w, g_pool_scale = _pool_bwd(zup, zgp, dyp, pool_w, row(pool_scale))

    bf = lambda a: a.astype(BF16)
    slabs = [bf(g_wout).reshape(N_DEV, D_MODEL // N_DEV, D_MODEL), _cols_to_slabs(bf(g_wba)), _cols_to_slabs(bf(g_wbp))]
    early = [g_pool_w, g_pool_scale, g_nf, loss8[0]]
    packed = jnp.concatenate([_pack_rows(a) for a in early], axis=0)
    dq_att, dkv_nat, dkr4, s_wout, s_wba, s_wbp, tot_early = _attn_bwd(
        q_att, k_att, v_att, doop, lse_rows, dcap_rows, tq, 4, slabs, packed)
    s_pool_w, s_pool_scale, s_nf, s_loss = _unpack_rows(tot_early, early)

    dfr, g_wuq_pad, g_wukv, g_qn, g_kvn = _mla_bwd(
        dq_att, dkv_nat, dkr4, zfr, row(q_norm), wuq_pad, row(kv_norm), wukv, rc, rsa, rsb, tm)
    dzs = (dgm, dga, dup, dgp, dfr)
    slabs = [_in_proj_bwd_w(dzs, hn, tm),
             bf(g_wuq_pad.reshape(Q_RANK, HEADS, HEAD_PAD)[:, :, :NOPE + ROPE]).reshape(N_DEV, Q_RANK // N_DEV, -1),
             bf(g_wukv).reshape(N_DEV, KV_RANK // N_DEV, HW)]
    grad_x, g_nin, s_win, s_wuq, s_wukv = _in_proj_bwd_x(dzs, x2, dh, row(norm_in), w_in_pad, min(256, T), slabs)
    late = [g_nin, g_qn, g_kvn]
    (tot_late,) = _reduce_scatter([], jnp.concatenate([_pack_rows(a) for a in late], axis=0))
    s_nin, s_qn, s_kvn = _unpack_rows(tot_late, late)

    grads = dict(norm_in=s_nin, w_in=s_win, q_norm=s_qn, w_uq=s_wuq, kv_norm=s_kvn, w_ukv=s_wukv, pool_w=s_pool_w.reshape(-1, GROUP),
                 pool_scale=s_pool_scale, w_branch_attn=s_wba, w_branch_pool=s_wbp, w_out=s_wout, norm_final=s_nf)
    return s_loss[0], grad_x, grads


MESH_ID = pl.DeviceIdType.MESH
VMEM_SPEC = pl.BlockSpec(memory_space=pltpu.VMEM)
HBM_SPEC = pl.BlockSpec(memory_space=pl.ANY)


def _mesh_pos():
    return lax.axis_index("x"), lax.axis_index("y"), lax.axis_index("c")


def _slot(px, py, pc):
    return 4 * px + 2 * py + pc


def _all_gather_bf16(shards):
    n = len(shards)

    def body(*refs):
        ins, outs = refs[:n], refs[n:2 * n]
        land0, scratch = refs[2 * n], refs[2 * n + 1:]
        wpad_ref = outs[0]
        ag = _AllGather(ins, (land0,) + tuple(outs[1:]), scratch)
        ag.start()
        ag.forward()
        ag.finish()
        wpad_ref[ZKR:GA, :] = jnp.zeros((GA - ZKR, D_MODEL), BF16)
        for k in range(N_DEV):
            for at, dst, rows in _slab_segments(k):
                wpad_ref[dst:dst + rows, :] = land0[k, at:at + rows, :]

    return pl.pallas_call(
        body, name="all_gather_weights",
        in_specs=[VMEM_SPEC] * n, out_specs=[VMEM_SPEC] + [HBM_SPEC] * (n - 1),
        out_shape=[SDS((ZTOT, D_MODEL), BF16)] + [SDS((N_DEV,) + s.shape, BF16) for s in shards[1:]],
        scratch_shapes=[pltpu.VMEM((N_DEV,) + shards[0].shape, BF16)] + _ag_scratch([s.shape for s in shards]),
        compiler_params=_cp(),
    )(*shards)


def _ag_scratch(shapes):
    n = len(shapes)
    dma = pltpu.SemaphoreType.DMA
    return [pltpu.VMEM(tuple(s), BF16) for s in shapes] + [dma((7 * n,)), dma((7 * n,)), dma((n,))]


class _AllGather:
    def __init__(self, in_refs, dest_refs, scratch):
        n = self.n = len(in_refs)
        self.ins, self.dests, self.stage = in_refs, dest_refs, scratch[:n]
        self.send_sems, self.recv_sems, self.local_sems = scratch[n:]
        x, y, c = _mesh_pos()
        self.c, self.me, self.sibling = c, (x, y, c), (x, y, 1 - c)
        self.chips = [(1 - x, y), (x, 1 - y), (1 - x, 1 - y)]

    def _copy(self, a, k, block, to, from_stage=False):
        dst = self.dests[a].at[_slot(*block)]
        return pltpu.make_async_remote_copy(
            src_ref=self.stage[a] if from_stage else dst, dst_ref=dst, send_sem=self.send_sems.at[7 * a + k],
            recv_sem=self.recv_sems.at[7 * a + k], device_id=to, device_id_type=MESH_ID)

    def _mine(self):
        return [pltpu.make_async_copy(self.stage[a], self.dests[a].at[_slot(*self.me)], self.local_sems.at[a]) for a in range(self.n)]

    def _first(self):
        cps = []
        for a in range(self.n):
            cps.append(self._copy(a, 0, self.me, self.sibling, True))
            cps += [self._copy(a, 1 + j, self.me, (*chip, self.c), True) for j, chip in enumerate(self.chips)]
        return cps

    def _passed(self):
        return [self._copy(a, 4 + j, (*chip, self.c), self.sibling) for a in range(self.n) for j, chip in enumerate(self.chips)]

    def start(self):
        for a in range(self.n):
            self.stage[a][...] = self.ins[a][...].astype(BF16)
        for cp in self._mine() + self._first():
            cp.start()

    def forward(self):
        passed = self._passed()
        for a in range(self.n):
            for j, chip in enumerate(self.chips):
                self._copy(a, 1 + j, (*chip, self.c), self.me).wait_recv()
                passed[3 * a + j].start()

    def finish(self):
        for a in range(self.n):
            self._copy(a, 0, self.sibling, self.me).wait_recv()
            for j, chip in enumerate(self.chips):
                self._copy(a, 4 + j, (*chip, 1 - self.c), self.me).wait_recv()
        for cp in self._first() + self._passed():
            cp.wait_send()
        for cp in self._mine():
            cp.wait()


N_CHIPS = 4


def _reduce_scatter(slabs, packed):
    def body(*refs):
        n = len(slabs)
        rs = _ReduceScatter(refs[:n], refs[n], refs[n + 1:2 * n + 1], refs[2 * n + 1], refs[2 * n + 2:])
        rs.start1()
        rs.finish1_start2()
        rs.finish2()

    shapes = [s.shape[1:] for s in slabs]
    return pl.pallas_call(
        body, name="reduce_scatter_grads",
        in_specs=[HBM_SPEC] * len(slabs) + [VMEM_SPEC], out_specs=[VMEM_SPEC] * (len(slabs) + 1),
        out_shape=[SDS(s, F32) for s in shapes] + [SDS(packed.shape, F32)],
        scratch_shapes=_rs_scratch(shapes, packed.shape), compiler_params=_cp(),
    )(*slabs, packed)


def _rs_scratch(shapes, packed_shape):
    n = len(shapes)
    n1, n2 = N_CHIPS * n + 1, (N_CHIPS - 1) * (n + 1)
    dma = pltpu.SemaphoreType.DMA
    packed = [] if packed_shape is None else [pltpu.VMEM(packed_shape, F32), pltpu.VMEM((N_CHIPS,) + tuple(packed_shape), F32)]
    return ([pltpu.VMEM((N_CHIPS,) + tuple(s), BF16) for s in shapes] * 2 + [pltpu.VMEM((N_CHIPS - 1,) + tuple(s), BF16) for s in shapes] * 2
            + packed + [dma((max(N_CHIPS * n, 1),)), dma((n1,)), dma((n1,)), dma((n2,)), dma((n2,))])


class _ReduceScatter:
    def __init__(self, slab_refs, packed_ref, out_refs, ptot_ref, scratch):
        n = self.n = len(slab_refs)
        self.slabs, self.packed, self.outs, self.ptot = slab_refs, packed_ref, out_refs, ptot_ref
        self.own1, self.land1, self.send2, self.land2 = (scratch[k * n:(k + 1) * n] for k in range(4))
        rest = scratch[4 * n:]
        if packed_ref is not None:
            self.pland1, self.pland2 = rest[:2]
            rest = rest[2:]
        self.loc_sems, self.send1_sems, self.recv1_sems, self.send2_sems, self.recv2_sems = rest
        self.x, self.y, self.c = _mesh_pos()

    def _chip(self, r):
        return (1 - self.x if r & 2 else self.x, 1 - self.y if r & 1 else self.y)

    @staticmethod
    def _remote(src, dst, send_sem, recv_sem, to):
        return pltpu.make_async_remote_copy(src_ref=src, dst_ref=dst, send_sem=send_sem, recv_sem=recv_sem, device_id=to,
                                            device_id_type=MESH_ID)

    def _copies1(self):
        c, sibling = self.c, (self.x, self.y, 1 - self.c)
        cps = []
        for a in range(self.n):
            for r in range(N_CHIPS):
                k = N_CHIPS * a + r
                cps.append(pltpu.make_async_copy(self.slabs[a].at[_slot(*self._chip(r), c)], self.own1[a].at[r], self.loc_sems.at[k]))
                cps.append(self._remote(self.slabs[a].at[_slot(*self._chip(r), 1 - c)], self.land1[a].at[r],
                                        self.send1_sems.at[k], self.recv1_sems.at[k], sibling))
        if self.packed is not None:
            k = N_CHIPS * self.n
            cps.append(self._remote(self.packed, self.pland1, self.send1_sems.at[k], self.recv1_sems.at[k], sibling))
        return cps

    def _copies2(self):
        cps = []
        for a in range(self.n):
            for r in range(1, N_CHIPS):
                k = (N_CHIPS - 1) * a + r - 1
                cps.append(self._remote(self.send2[a].at[r - 1], self.land2[a].at[r - 1], self.send2_sems.at[k], self.recv2_sems.at[k],
                                        (*self._chip(r), self.c)))
        if self.packed is not None:
            for r in range(1, N_CHIPS):
                k = (N_CHIPS - 1) * self.n + r - 1
                cps.append(self._remote(self.pland2.at[0], self.pland2.at[r], self.send2_sems.at[k], self.recv2_sems.at[k],
                                        (*self._chip(r), self.c)))
        return cps

    def start1(self):
        for cp in self._copies1():
            cp.start()

    def finish1_start2(self):
        for cp in self._copies1():
            cp.wait()
        for a in range(self.n):
            self.outs[a][...] = self.own1[a][0].astype(F32) + self.land1[a][0].astype(F32)
            for r in range(1, N_CHIPS):
                self.send2[a][r - 1] = (self.own1[a][r].astype(F32) + self.land1[a][r].astype(F32)).astype(BF16)
        if self.packed is not None:
            self.pland2[0] = self.packed[...] + self.pland1[...]
        for cp in self._copies2():
            cp.start()

    def finish2(self):
        for cp in self._copies2():
            cp.wait()
        for a in range(self.n):
            l2 = self.land2[a]
            self.outs[a][...] = self.outs[a][...] + ((l2[0].astype(F32) + l2[1].astype(F32)) + l2[2].astype(F32))
        if self.packed is not None:
            p2 = self.pland2
            self.ptot[...] = (p2[0] + p2[1]) + (p2[2] + p2[3])


def _adamw(ws, gs, ms, vs):
    n = len(ws)

    def body(*refs):
        for k in range(n):
            w, g, m, v = (refs[j * n + k][...] for j in range(4))
            d_ref, nm_ref, nv_ref = (refs[(4 + j) * n + k] for j in range(3))
            m = ADAM_B1 * m + (1.0 - ADAM_B1) * g
            v = ADAM_B2 * v + (1.0 - ADAM_B2) * jnp.square(g)
            m_hat = m / (1.0 - ADAM_B1 ** ADAM_STEP)
            v_hat = v / (1.0 - ADAM_B2 ** ADAM_STEP)
            d_ref[...] = -ADAM_LR * (m_hat / (jnp.sqrt(v_hat) + ADAM_EPS) + ADAM_WD * w)
            nm_ref[...] = m
            nv_ref[...] = v

    outs = pl.pallas_call(
        body, name="adamw", in_specs=[VMEM_SPEC] * (4 * n), out_specs=[VMEM_SPEC] * (3 * n),
        out_shape=[SDS(w.shape, F32) for w in ws] * 3, compiler_params=_cp(),
    )(*ws, *gs, *ms, *vs)
    return outs[:n], outs[n:2 * n], outs[2 * n:]


WEIGHTS = ("norm_in", "w_in", "q_norm", "w_uq", "kv_norm", "w_ukv", "pool_w", "pool_scale", "w_branch_attn", "w_branch_pool",
           "w_out", "norm_final")
SUBLANES = 8


def _cols_to_slabs(g):
    r = g.shape[0]
    return g.reshape(r, N_DEV, -1).transpose(1, 0, 2)


def _slabs_to_cols(s):
    return s.transpose(1, 0, 2).reshape(s.shape[1], -1)


def _pack_rows(a):
    a = a.reshape(-1, LANES)
    return jnp.pad(a, ((0, -a.shape[0] % SUBLANES), (0, 0)))


def _unpack_rows(packed, like):
    out, row = [], 0
    for a in like:
        rows = a.size // LANES
        out.append(packed[row:row + rows].reshape(a.shape))
        row += rows + (-rows % SUBLANES)
    return out


def kernel(x, norm_in, w_in, q_norm, w_uq, kv_norm, w_ukv, pool_w, pool_scale, w_branch_attn, w_branch_pool, w_out, norm_final, loss_target, m_norm_in, m_w_in, m_q_norm, m_w_uq, m_kv_norm, m_w_ukv, m_pool_w, m_pool_scale, m_w_branch_attn, m_w_branch_pool, m_w_out, m_norm_final, v_norm_in, v_w_in, v_q_norm, v_w_uq, v_kv_norm, v_w_ukv, v_pool_w, v_pool_scale, v_w_branch_attn, v_w_branch_pool, v_w_out, v_norm_final):
    w = dict(norm_in=norm_in, w_in=w_in, q_norm=q_norm, w_uq=w_uq, kv_norm=kv_norm, w_ukv=w_ukv, pool_w=pool_w, pool_scale=pool_scale,
             w_branch_attn=w_branch_attn, w_branch_pool=w_branch_pool, w_out=w_out, norm_final=norm_final)
    m = dict(norm_in=m_norm_in, w_in=m_w_in, q_norm=m_q_norm, w_uq=m_w_uq, kv_norm=m_kv_norm, w_ukv=m_w_ukv, pool_w=m_pool_w,
             pool_scale=m_pool_scale, w_branch_attn=m_w_branch_attn, w_branch_pool=m_w_branch_pool, w_out=m_w_out, norm_final=m_norm_final)
    v = dict(norm_in=v_norm_in, w_in=v_w_in, q_norm=v_q_norm, w_uq=v_w_uq, kv_norm=v_kv_norm, w_ukv=v_w_ukv, pool_w=v_pool_w,
             pool_scale=v_pool_scale, w_branch_attn=v_w_branch_attn, w_branch_pool=v_w_branch_pool, w_out=v_w_out, norm_final=v_norm_final)

    def as2d(name, a):
        if name == "w_in":
            return a.T
        if name in ("w_uq", "w_ukv"):
            return a.reshape(a.shape[0], -1)
        if name == "pool_w":
            return a.reshape(-1, GROUP)
        return a.reshape(1, -1) if a.ndim == 1 else a

    def unshape(name, a):
        return a.T if name == "w_in" else a.reshape(w[name].shape)

    w_in_pad, w_uq_full, w_ukv_full = _all_gather_bf16([as2d(k, w[k]) for k in ("w_in", "w_uq", "w_ukv")])
    loss, grad_x, g2d = _local_step(
        x.reshape(x.shape[1:]), loss_target.reshape(x.shape[1:]), norm_in, w_in_pad, q_norm,
        w_uq_full.reshape(Q_RANK, HEADS, NOPE + ROPE), kv_norm, w_ukv_full.reshape(KV_RANK, HEADS, NOPE + VDIM),
        pool_w, pool_scale, [w_branch_attn, w_branch_pool, w_out], norm_final)

    deltas, new_m, new_v = _adamw([as2d(k, w[k]) for k in WEIGHTS], [g2d[k] for k in WEIGHTS],
                                  [as2d(k, m[k]) for k in WEIGHTS], [as2d(k, v[k]) for k in WEIGHTS])
    shaped = lambda arrs: [unshape(k, a) for k, a in zip(WEIGHTS, arrs)]
    return (loss, grad_x.reshape(x.shape), *shaped([g2d[k] for k in WEIGHTS]), *shaped(deltas), *shaped(new_m), *shaped(new_v))
```

```python
import functools

import jax
import jax.numpy as jnp
import numpy as np
from jax import lax
from jax.experimental import pallas as pl
from jax.experimental.pallas import tpu as pltpu

F32 = jnp.float32
BF16 = jnp.bfloat16
SDS = jax.ShapeDtypeStruct

D_MODEL = 1024
HEADS = 8
NOPE = 64
ROPE = 32
VDIM = 64
Q_RANK = 384
KV_RANK = 256
MLA_W = HEADS * VDIM
POOL_W = 512
POOL_GROUPS = 4
GROUP = POOL_W // POOL_GROUPS
CHUNK = 64
ROPE_THETA = 10000.0
EPS = 1e-6
SCALE = (NOPE + ROPE) ** -0.5
LOG2E = 1.4426950408889634
LN2 = 0.6931471805599453
QK_SCALE_LOG2 = SCALE * LOG2E
IN_TOTAL = 4256
ADAM_LR, ADAM_B1, ADAM_B2, ADAM_EPS, ADAM_WD, ADAM_STEP = 0.001, 0.9, 0.999, 1e-08, 0.01, 10

N_DEV = 8
LANES = 128
HEAD_PAD = LANES
HW = HEADS * HEAD_PAD

ZQ, ZKV, ZKR, GA, UP, GP, GM, ZTOT = 0, 384, 640, 768, 1280, 1792, 2304, 4352
FRONT_W = GA
ZKR_ORIG = 640

VMEM_LIMIT = 56 * 1024 * 1024


def _cp(sem=None, **kw):
    if sem is not None:
        kw["dimension_semantics"] = sem
    return pltpu.CompilerParams(vmem_limit_bytes=VMEM_LIMIT, **kw)


def _mm(a, b):
    return lax.dot_general(a, b, (((1,), (0,)), ((), ())), preferred_element_type=F32)


def _mm_nt(a, b):
    return lax.dot_general(a, b, (((1,), (1,)), ((), ())), preferred_element_type=F32)


def _mm_tn(a, b):
    return lax.dot_general(a, b, (((0,), (0,)), ((), ())), preferred_element_type=F32)


def _row_spec(tm, w):
    return pl.BlockSpec((tm, w), lambda i: (i, 0))


def _full_spec(a):
    nd = len(a.shape)
    return pl.BlockSpec(a.shape, lambda *_: (0,) * nd)


def _rope(v, c, sa, sb, sign):
    n = v.shape[-1]
    reps = n // LANES
    if reps > 1:
        c, sa, sb = (jnp.tile(t, (1, reps)) for t in (c, sa, sb))
    up = pltpu.roll(v, n - ROPE // 2, 1)
    dn = pltpu.roll(v, ROPE // 2, 1)
    return v * c + sign * (up * sa + dn * sb)


def _rope_tables(T):
    half = ROPE // 2
    inv_freq = np.float32(ROPE_THETA) ** (-np.arange(half, dtype=np.float32) / np.float32(half))
    ang = np.arange(T, dtype=np.float32)[:, None] * inv_freq[None, :].astype(np.float32)
    cos, sin = np.cos(ang.astype(np.float64)).astype(np.float32), np.sin(ang.astype(np.float64)).astype(np.float32)
    z16 = np.zeros((T, half), np.float32)
    z32 = np.zeros((T, LANES - NOPE - ROPE), np.float32)
    c = np.concatenate([np.ones((T, NOPE), np.float32), cos, cos, z32], axis=1)
    sa = np.concatenate([np.zeros((T, NOPE), np.float32), -sin, z16, z32], axis=1)
    sb = np.concatenate([np.zeros((T, NOPE), np.float32), z16, sin, z32], axis=1)
    return jnp.asarray(c), jnp.asarray(sa), jnp.asarray(sb)


def _silu_parts(g):
    sg = jax.nn.sigmoid(g)
    return g * sg, sg + g * sg * (1.0 - sg)


def _in_proj(x2, norm_in, w_in_pad, q_norm, wuq_pad, kv_norm, wukv, rc, rsa, rsb, tm, late_shards):
    T = x2.shape[0]
    steps = T // tm
    n = len(late_shards)

    def body(x_ref, nin_ref, win_ref, qn_ref, wuq_ref, kvn_ref, wukv_ref, c_ref, sa_ref, sb_ref, *rest):
        hn_ref, zgm_ref, zga_ref, zup_ref, zgp_ref, zfr_ref, q_ref, k_ref, v_ref, vt_ref = rest[n:n + 10]
        ag = _AllGather(rest[:n], rest[n + 10:2 * n + 10], rest[2 * n + 10:])
        step = pl.program_id(0)
        pl.when(step == 0)(ag.start)
        pl.when(step == min(5, steps - 1))(ag.forward)
        xf = x_ref[...]
        r = lax.rsqrt(jnp.mean(xf * xf, axis=-1, keepdims=True) + EPS)
        hn = (xf * r * nin_ref[...]).astype(BF16)
        hn_ref[...] = hn
        z = _mm_nt(hn, win_ref[...])
        zgm_ref[...] = z[:, GM:ZTOT]
        zga_ref[...] = z[:, GA:UP]
        zup_ref[...] = z[:, UP:GP]
        zgp_ref[...] = z[:, GP:GM]
        zfr_ref[...] = z[:, ZQ:GA]
        zq, zkv, zkr = z[:, ZQ:ZKV], z[:, ZKV:ZKR], z[:, ZKR:GA]
        c, sa, sb = c_ref[...], sa_ref[...], sb_ref[...]
        rq = lax.rsqrt(jnp.mean(zq * zq, axis=-1, keepdims=True) + EPS)
        cq = (zq * rq * qn_ref[...]).astype(BF16)
        q = _rope(_mm(cq, wuq_ref[...]), c, sa, sb, 1.0)
        q_ref[...] = (q * QK_SCALE_LOG2).astype(BF16)
        rkv = lax.rsqrt(jnp.mean(zkv * zkv, axis=-1, keepdims=True) + EPS)
        ckv = (zkv * rkv * kvn_ref[...]).astype(BF16)
        kv = _mm(ckv, wukv_ref[...])
        kr = _rope(zkr, c, sa, sb, 1.0)
        lane = lax.broadcasted_iota(jnp.int32, kv.shape, 1) % LANES
        k_ref[...] = jnp.where(lane < NOPE, kv, jnp.tile(kr, (1, HEADS))).astype(BF16)
        v = jnp.where(lane < NOPE, 1.0, kv).astype(BF16)
        v_ref[...] = v
        vt_ref[...] = v.T
        pl.when(step == steps - 1)(ag.finish)

    ins = (x2, norm_in, w_in_pad, q_norm, wuq_pad, kv_norm, wukv, rc, rsa, rsb)
    in_specs = [_row_spec(tm, D_MODEL), _full_spec(norm_in), _full_spec(w_in_pad), _full_spec(q_norm), _full_spec(wuq_pad),
                _full_spec(kv_norm), _full_spec(wukv), _row_spec(tm, LANES), _row_spec(tm, LANES), _row_spec(tm, LANES)]
    widths = [(D_MODEL, BF16), (ZTOT - GM, F32), (UP - GA, F32), (GP - UP, F32), (GM - GP, F32), (FRONT_W, F32),
              (HW, BF16), (HW, BF16), (HW, BF16)]
    return pl.pallas_call(
        body, name="in_proj", grid=(steps,), in_specs=in_specs + [_full_spec(s) for s in late_shards],
        out_specs=[_row_spec(tm, w) for w, _ in widths] + [pl.BlockSpec((HW, tm), lambda i: (0, i))] + [HBM_SPEC] * n,
        out_shape=[SDS((T, w), dt) for w, dt in widths] + [SDS((HW, T), BF16)]
        + [SDS((N_DEV,) + s.shape, BF16) for s in late_shards],
        scratch_shapes=_ag_scratch([s.shape for s in late_shards]), compiler_params=_cp(("arbitrary",)),
    )(*ins, *late_shards)


def _chunk_mask(n_q, n_k, q_off, transposed):
    shape = (n_k, n_q) if transposed else (n_q, n_k)
    q = (lax.broadcasted_iota(jnp.int32, shape, 1 if transposed else 0) + q_off) // CHUNK
    k = lax.broadcasted_iota(jnp.int32, shape, 0 if transposed else 1) // CHUNK
    return k <= q


def _store_pair_rows(ref, k, pair):
    t = pair.T
    ref[k, 0:1, :] = t[0:1, :]
    ref[k, 1:2, :] = t[VDIM:VDIM + 1, :]


def _attn_fwd(q_att, k_att, vt_att, tq, hps):
    T = q_att.shape[0]
    head_lanes = [slice(h * LANES, (h + 1) * LANES) for h in range(hps)]

    def body(q_ref, k_ref, vt_ref, o_ref, lser_ref):
        i = pl.program_id(1)
        mask = _chunk_mask(tq, tq, 0, True)
        lane = lax.broadcasted_iota(jnp.int32, (tq, LANES), 1)
        qs = [q_ref[:, hs] for hs in head_lanes]

        def step(j, carry, masked):
            off = pl.multiple_of(j * tq, tq)
            sts = [_mm_nt(k_ref[pl.ds(off, tq), hs], qh) for qh, hs in zip(qs, head_lanes)]
            if masked:
                sts = [jnp.where(mask, st, -jnp.inf) for st in sts]
            ms = [jnp.maximum(m, jnp.max(st, axis=0, keepdims=True)) for (m, _), st in zip(carry, sts)]
            pts = [jnp.exp2(st - m_new).astype(BF16) for st, m_new in zip(sts, ms)]
            return tuple((m_new, jnp.exp2(m - m_new) * acc + _mm(vt_ref[hs, pl.ds(off, tq)], pt))
                         for (m, acc), m_new, pt, hs in zip(carry, ms, pts, head_lanes))

        init = ((jnp.full((1, tq), -jnp.inf, F32), jnp.zeros((LANES, tq), F32)),) * hps
        res = step(i, lax.fori_loop(0, i, functools.partial(step, masked=False), init), True)
        for pair in range(hps // 2):
            (ma, acca), (mb, accb) = res[2 * pair], res[2 * pair + 1]
            la, lb = acca[:1], accb[:1]
            oa, ob = (acca / la).T, (accb / lb).T
            o_ref[:, pair * LANES:(pair + 1) * LANES] = jnp.where(lane < VDIM, pltpu.roll(oa, VDIM, 1), ob)
            lser_ref[pair, 0:1, :] = ma + jnp.log2(la)
            lser_ref[pair, 1:2, :] = mb + jnp.log2(lb)

    qspec = pl.BlockSpec((tq, hps * LANES), lambda p, i: (i, p))
    kspec = pl.BlockSpec((T, hps * LANES), lambda p, i: (0, p))
    vspec = pl.BlockSpec((hps * LANES, T), lambda p, i: (p, 0))
    ospec = pl.BlockSpec((tq, hps * VDIM), lambda p, i: (i, p))
    return pl.pallas_call(
        body, name="attn_fwd", grid=(HEADS // hps, T // tq), in_specs=[qspec, kspec, vspec],
        out_specs=[ospec, pl.BlockSpec((hps // 2, 2, tq), lambda p, i: (p, 0, i))],
        out_shape=[SDS((T, MLA_W), F32), SDS((HEADS // 2, 2, T), F32)],
        compiler_params=_cp(("parallel", "parallel")),
    )(q_att, k_att, vt_att)


def _pick(g, vals):
    out = vals[-1]
    for k in range(len(vals) - 2, -1, -1):
        out = jnp.where(g == k, vals[k], out)
    return out


def _window_sum(u, g, forward):
    T = u.shape[0]
    row = lax.broadcasted_iota(jnp.int32, u.shape, 0)

    def sh(s, k):
        if forward:
            return jnp.where(row >= k, pltpu.roll(s, k, 0), 0.0)
        return jnp.where(row < T - k, pltpu.roll(s, T - k, 0), 0.0)

    sums, s = [], u
    for k in (1, 2, 4, 8):
        s = s + sh(s, k)
        sums.append(s)
    return _pick(g, sums)


def _pool_count(shape, g):
    row = lax.broadcasted_iota(jnp.int32, shape, 0)
    return jnp.minimum(row + 1, lax.shift_left(jnp.int32(2), g)).astype(F32)


def _pool_fwd(zup, zgp, pool_w, pool_scale):
    T = zup.shape[0]

    def body(u_ref, g_ref, w_ref, sc_ref, y_ref):
        g = pl.program_id(0)
        u = u_ref[...]
        d = _window_sum(u, g, True) / _pool_count(u.shape, g) - u
        lin = _mm(d.astype(BF16), w_ref[0].astype(BF16))
        silu, _ = _silu_parts(g_ref[...])
        y_ref[...] = (lin * sc_ref[...] * silu).astype(BF16)

    col = pl.BlockSpec((T, GROUP), lambda g: (0, g))
    return pl.pallas_call(
        body, name="pool_fwd", grid=(POOL_GROUPS,),
        in_specs=[col, col, pl.BlockSpec((1, GROUP, GROUP), lambda g: (g, 0, 0)), pl.BlockSpec((1, GROUP), lambda g: (0, g))],
        out_specs=col, out_shape=SDS((T, POOL_W), BF16), compiler_params=_cp(("parallel",)),
    )(zup, zgp, pool_w, pool_scale)


def _pool_bwd(zup, zgp, dyp, pool_w, pool_scale):
    T = zup.shape[0]

    def body(u_ref, g_ref, dy_ref, w_ref, sc_ref, du_ref, dg_ref, gw_ref, gsc_ref):
        g = pl.program_id(0)
        u = u_ref[...]
        cnt = _pool_count(u.shape, g)
        d = (_window_sum(u, g, True) / cnt - u).astype(BF16)
        wb = w_ref[0].astype(BF16)
        lin = _mm(d, wb)
        sc = sc_ref[...]
        silu, dsilu = _silu_parts(g_ref[...])
        dy = dy_ref[...]
        dg_ref[...] = (dy * lin * sc * dsilu).astype(BF16)
        dpre = dy * silu
        gsc_ref[...] = jnp.sum(dpre * lin, axis=0, keepdims=True)
        dlin = (dpre * sc).astype(BF16)
        gw_ref[0] = _mm_tn(d, dlin)
        dd = _mm_nt(dlin, wb)
        du_ref[...] = (_window_sum(dd / cnt, g, False) - dd).astype(BF16)

    col = pl.BlockSpec((T, GROUP), lambda g: (0, g))
    wspec = pl.BlockSpec((1, GROUP, GROUP), lambda g: (g, 0, 0))
    vspec = pl.BlockSpec((1, GROUP), lambda g: (0, g))
    return pl.pallas_call(
        body, name="pool_bwd", grid=(POOL_GROUPS,), in_specs=[col, col, col, wspec, vspec], out_specs=[col, col, wspec, vspec],
        out_shape=[SDS((T, POOL_W), BF16), SDS((T, POOL_W), BF16), SDS((POOL_GROUPS, GROUP, GROUP), F32), SDS((1, POOL_W), F32)],
        compiler_params=_cp(("parallel",)),
    )(zup, zgp, dyp, pool_w, pool_scale)


def _tail(x2, tgt, o, zga, ypool, zgm, wba, wbp, wout, norm_final, tm):
    T = x2.shape[0]

    def body(x_ref, tgt_ref, o_ref, zga_ref, yp_ref, zgm_ref, wba_ref, wbp_ref, wout_ref, nf_ref,
             loss_ref, dh_ref, dgm_ref, doop_ref, dga_ref, dcapr_ref, dyp_ref, gwout_ref, gwba_ref, gwbp_ref, gnf_ref):
        @pl.when(pl.program_id(0) == 0)
        def _():
            for ref in (loss_ref, gwout_ref, gwba_ref, gwbp_ref, gnf_ref):
                ref[...] = jnp.zeros_like(ref)

        o_v = o_ref[...]
        silu, dsilu = _silu_parts(zga_ref[...])
        ya = (o_v * silu).astype(BF16)
        yp = yp_ref[...]
        wba_v, wbp_v, wout_v = wba_ref[...], wbp_ref[...], wout_ref[...]
        a = _mm(ya, wba_v)
        p = _mm(yp, wbp_v)
        gate = jax.nn.sigmoid(zgm_ref[...])
        ga, gp = gate[:, :D_MODEL], gate[:, D_MODEL:]
        mg = (ga * a + gp * p).astype(BF16)
        h = x_ref[...] + _mm(mg, wout_v)
        r = lax.rsqrt(jnp.mean(h * h, axis=-1, keepdims=True) + EPS)
        gf = nf_ref[...]
        hr = h * r
        e = hr * gf - tgt_ref[...]
        loss_ref[...] += (0.5 / D_MODEL) * jnp.sum(e * e)
        dy = e * (1.0 / D_MODEL)
        gnf_ref[...] += jnp.sum(dy * hr, axis=0, keepdims=True)
        u = dy * gf
        dh = r * (u - hr * jnp.mean(u * hr, axis=-1, keepdims=True))
        dh_ref[...] = dh
        dhb = dh.astype(BF16)
        dmg = _mm_nt(dhb, wout_v)
        gwout_ref[...] += _mm_tn(mg, dhb)
        dgm_ref[:, :D_MODEL] = (dmg * a * ga * (1.0 - ga)).astype(BF16)
        dgm_ref[:, D_MODEL:] = (dmg * p * gp * (1.0 - gp)).astype(BF16)
        dab = (dmg * ga).astype(BF16)
        dpb = (dmg * gp).astype(BF16)
        dya = _mm_nt(dab, wba_v)
        gwba_ref[...] += _mm_tn(ya, dab)
        dyp_ref[...] = _mm_nt(dpb, wbp_v)
        gwbp_ref[...] += _mm_tn(yp, dpb)
        do = dya * silu
        dga_ref[...] = (dya * o_v * dsilu).astype(BF16)
        prod = do * o_v
        lo = lax.broadcasted_iota(jnp.int32, (tm, LANES), 1) < VDIM
        for pair in range(HEADS // 2):
            ls = slice(pair * LANES, (pair + 1) * LANES)
            do_p, prod_p = do[:, ls], prod[:, ls]
            dcap_a = jnp.sum(jnp.where(lo, prod_p, 0.0), axis=-1, keepdims=True)
            dcap_b = jnp.sum(jnp.where(lo, 0.0, prod_p), axis=-1, keepdims=True)
            _store_pair_rows(dcapr_ref, pair, jnp.where(lo, dcap_a, dcap_b))
            doop_ref[:, 2 * pair * LANES:(2 * pair + 1) * LANES] = jnp.where(lo, 0.0, pltpu.roll(do_p, VDIM, 1)).astype(BF16)
            doop_ref[:, (2 * pair + 1) * LANES:(2 * pair + 2) * LANES] = jnp.where(lo, 0.0, do_p).astype(BF16)

    ins = (x2, tgt, o, zga, ypool, zgm, wba, wbp, wout, norm_final)
    in_specs = [_row_spec(tm, D_MODEL), _row_spec(tm, D_MODEL), _row_spec(tm, MLA_W), _row_spec(tm, MLA_W), _row_spec(tm, POOL_W),
                _row_spec(tm, 2 * D_MODEL), _full_spec(wba), _full_spec(wbp), _full_spec(wout), _full_spec(norm_final)]
    outs = [SDS((8, LANES), F32), SDS((T, D_MODEL), F32), SDS((T, 2 * D_MODEL), BF16), SDS((T, HW), BF16), SDS((T, MLA_W), BF16),
            SDS((HEADS // 2, 2, T), F32), SDS((T, POOL_W), F32),
            SDS((D_MODEL, D_MODEL), F32), SDS((MLA_W, D_MODEL), F32), SDS((POOL_W, D_MODEL), F32), SDS((1, D_MODEL), F32)]
    out_specs = [_full_spec(outs[0]), _row_spec(tm, D_MODEL), _row_spec(tm, 2 * D_MODEL), _row_spec(tm, HW), _row_spec(tm, MLA_W),
                 pl.BlockSpec((HEADS // 2, 2, tm), lambda i: (0, 0, i)), _row_spec(tm, POOL_W),
                 _full_spec(outs[7]), _full_spec(outs[8]), _full_spec(outs[9]), _full_spec(outs[10])]
    return pl.pallas_call(
        body, name="tail", grid=(T // tm,), in_specs=in_specs, out_specs=out_specs, out_shape=outs,
        compiler_params=_cp(("arbitrary",)),
    )(*ins)


def _attn_bwd(q_att, k_att, v_att, doop, lse_rows, dcap_rows, tq, hps, slabs, packed):
    T = q_att.shape[0]
    nq = T // tq
    n = len(slabs)
    groups = HEADS // hps
    head_lanes = [slice(h * LANES, (h + 1) * LANES) for h in range(hps)]

    def body(q_ref, k_ref, v_ref, doop_ref, lse_ref, dcap_ref, *rest):
        slab_refs, packed_ref = rest[:n], rest[n]
        dq_ref, dkv_ref, dkr_ref = rest[n + 1:n + 4]
        sum_refs, ptot_ref = rest[n + 4:2 * n + 4], rest[2 * n + 4]
        dq_acc = rest[2 * n + 5]
        rs = _ReduceScatter(slab_refs, packed_ref, sum_refs, ptot_ref, rest[2 * n + 6:])
        group, j = pl.program_id(0), pl.program_id(1)
        pl.when((group == 0) & (j == 0))(rs.start1)
        pl.when((group == 1) & (j == 0))(rs.finish1_start2)
        mask = _chunk_mask(tq, tq, 0, True)
        lane = lax.broadcasted_iota(jnp.int32, (tq, LANES), 1)
        ks = [k_ref[:, hs] for hs in head_lanes]
        vs = [v_ref[:, hs] for hs in head_lanes]
        kts = [kh.T for kh in ks]

        @pl.when(j == 0)
        def _():
            dq_acc[...] = jnp.zeros_like(dq_acc)

        def step(i, carry, masked):
            rows = pl.ds(pl.multiple_of(i * tq, tq), tq)
            heads = range(hps)
            stat = lambda h: (h // 2, slice(h % 2, h % 2 + 1), rows)
            qhs = [q_ref[rows, hs] for hs in head_lanes]
            doops = [doop_ref[rows, hs] for hs in head_lanes]
            sts = [_mm_nt(ks[h], qhs[h]) for h in heads]
            dpts = [_mm_nt(vs[h], doops[h]) for h in heads]
            pts = [jnp.exp2(sts[h] - lse_ref[stat(h)]) for h in heads]
            if masked:
                pts = [jnp.where(mask, pt, 0.0) for pt in pts]
            dsts = [(pts[h] * (dpts[h] - dcap_ref[stat(h)])).astype(BF16) for h in heads]
            dvs = [_mm(pts[h].astype(BF16), doops[h]) for h in heads]
            dks = [_mm(dsts[h], qhs[h]) for h in heads]
            for h, hs in enumerate(head_lanes):
                dq_acc[hs, rows] += _mm(kts[h], dsts[h])
            return tuple((dk + dks[h], dv + dvs[h]) for h, (dk, dv) in enumerate(carry))

        zero = jnp.zeros((tq, LANES), F32)
        carry = step(j, ((zero, zero),) * hps, True)
        res = lax.fori_loop(j + 1, nq, functools.partial(step, masked=False), carry)
        dkr = None
        for (dk, dv), hs in zip(res, head_lanes):
            dk = dk * LN2
            dkv_ref[:, hs] = jnp.where(lane < NOPE, dk, dv).astype(BF16)
            dkr = dk if dkr is None else dkr + dk
        dkr_ref[0] = jnp.where((lane >= NOPE) & (lane < NOPE + ROPE), dkr, 0.0)

        @pl.when(j == nq - 1)
        def _():
            dq_ref[...] = (dq_acc[...] * SCALE).T.astype(BF16)

        pl.when((group == groups - 1) & (j == nq - 1))(rs.finish2)

    kspec = pl.BlockSpec((tq, hps * LANES), lambda p, j: (j, p))
    qspec = pl.BlockSpec((T, hps * LANES), lambda p, j: (0, p))
    rspec = pl.BlockSpec((hps // 2, 2, T), lambda p, j: (p, 0, 0))
    sums = [SDS(s.shape[1:], F32) for s in slabs] + [SDS(packed.shape, F32)]
    return pl.pallas_call(
        body, name="attn_bwd", grid=(groups, nq),
        in_specs=[qspec, kspec, kspec, qspec, rspec, rspec] + [HBM_SPEC] * n + [_full_spec(packed)],
        out_specs=[qspec, kspec, pl.BlockSpec((1, tq, LANES), lambda p, j: (p, j, 0))] + [_full_spec(s) for s in sums],
        out_shape=[SDS((T, HW), BF16), SDS((T, HW), BF16), SDS((groups, T, LANES), F32)] + sums,
        scratch_shapes=[pltpu.VMEM((hps * LANES, T), F32)] + _rs_scratch([s.shape for s in sums[:-1]], packed.shape),
        compiler_params=_cp(("arbitrary", "arbitrary")),
    )(q_att, k_att, v_att, doop, lse_rows, dcap_rows, *slabs, packed)


def _rms_bwd(z, gain, dout):
    r = lax.rsqrt(jnp.mean(z * z, axis=-1, keepdims=True) + EPS)
    zr = z * r
    u = dout * gain
    return r * (u - zr * jnp.mean(u * zr, axis=-1, keepdims=True)), jnp.sum(dout * zr, axis=0, keepdims=True)


def _mla_bwd(dq_att, dkv_nat, dkr4, zfr, q_norm, wuq_pad, kv_norm, wukv, rc, rsa, rsb, tm):
    T = dq_att.shape[0]

    def body(dq_ref, dkv_ref, dkr_ref, zfr_ref, qn_ref, wuq_ref, kvn_ref, wukv_ref, c_ref, sa_ref, sb_ref,
             dfr_ref, gwuq_ref, gwukv_ref, gqn_ref, gkvn_ref):
        @pl.when(pl.program_id(0) == 0)
        def _():
            for ref in (gwuq_ref, gwukv_ref, gqn_ref, gkvn_ref):
                ref[...] = jnp.zeros_like(ref)

        c, sa, sb = c_ref[...], sa_ref[...], sb_ref[...]
        zq, zkv = zfr_ref[:, :Q_RANK], zfr_ref[:, Q_RANK:Q_RANK + KV_RANK]
        qn, kvn = qn_ref[...], kvn_ref[...]
        cq = (zq * lax.rsqrt(jnp.mean(zq * zq, axis=-1, keepdims=True) + EPS) * qn).astype(BF16)
        ckv = (zkv * lax.rsqrt(jnp.mean(zkv * zkv, axis=-1, keepdims=True) + EPS) * kvn).astype(BF16)
        dq = _rope(dq_ref[...].astype(F32), c, sa, sb, -1.0).astype(BF16)
        gwuq_ref[...] += _mm_tn(cq, dq)
        dzq, gqn = _rms_bwd(zq, qn, _mm_nt(dq, wuq_ref[...]))
        gqn_ref[...] += gqn
        dkv = dkv_ref[...]
        gwukv_ref[...] += _mm_tn(ckv, dkv)
        dzkv, gkvn = _rms_bwd(zkv, kvn, _mm_nt(dkv, wukv_ref[...]))
        gkvn_ref[...] += gkvn
        dkr = functools.reduce(lambda a, b: a + b, [dkr_ref[g] for g in range(dkr4.shape[0])])
        dfr_ref[:, :Q_RANK] = dzq.astype(BF16)
        dfr_ref[:, Q_RANK:Q_RANK + KV_RANK] = dzkv.astype(BF16)
        dfr_ref[:, Q_RANK + KV_RANK:] = _rope(dkr, c, sa, sb, -1.0).astype(BF16)

    ins = (dq_att, dkv_nat, dkr4, zfr, q_norm, wuq_pad, kv_norm, wukv, rc, rsa, rsb)
    in_specs = [_row_spec(tm, HW), _row_spec(tm, HW), pl.BlockSpec((dkr4.shape[0], tm, LANES), lambda i: (0, i, 0)), _row_spec(tm, FRONT_W),
                _full_spec(q_norm), _full_spec(wuq_pad), _full_spec(kv_norm), _full_spec(wukv),
                _row_spec(tm, LANES), _row_spec(tm, LANES), _row_spec(tm, LANES)]
    outs = [SDS((T, FRONT_W), BF16), SDS((Q_RANK, HW), F32), SDS((KV_RANK, HW), F32), SDS((1, Q_RANK), F32), SDS((1, KV_RANK), F32)]
    out_specs = [_row_spec(tm, FRONT_W)] + [_full_spec(s) for s in outs[1:]]
    return pl.pallas_call(
        body, name="mla_bwd", grid=(T // tm,), in_specs=in_specs, out_specs=out_specs, out_shape=outs,
        compiler_params=_cp(("arbitrary",)),
    )(*ins)


_DZ_COLS = ((GM, ZTOT), (GA, UP), (UP, GP), (GP, GM), (ZQ, GA))


def _in_proj_bwd_x(dzs, x2, dh, norm_in, w_in_pad, tm, slabs):
    T = x2.shape[0]
    steps = T // tm
    n = len(slabs)

    def body(d0, d1, d2, d3, d4, x_ref, dh_ref, nin_ref, win_ref, *rest):
        slab_refs, (gx_ref, gnin_ref), sum_refs = rest[:n], rest[n:n + 2], rest[n + 2:2 * n + 2]
        rs = _ReduceScatter(slab_refs, None, sum_refs, None, rest[2 * n + 2:])
        step = pl.program_id(0)

        @pl.when(step == 0)
        def _():
            gnin_ref[...] = jnp.zeros_like(gnin_ref)
            rs.start1()

        pl.when(step == min(2, steps - 1))(rs.finish1_start2)
        dhn = None
        for ref, (lo, hi) in zip((d0, d1, d2, d3, d4), _DZ_COLS):
            t = _mm(ref[...], win_ref[lo:hi, :])
            dhn = t if dhn is None else dhn + t
        dx, gnin = _rms_bwd(x_ref[...], nin_ref[...], dhn)
        gnin_ref[...] += gnin
        gx_ref[...] = dx + dh_ref[...]
        pl.when(step == steps - 1)(rs.finish2)

    in_specs = [_row_spec(tm, hi - lo) for lo, hi in _DZ_COLS] + [_row_spec(tm, D_MODEL), _row_spec(tm, D_MODEL),
                                                                  _full_spec(norm_in), _full_spec(w_in_pad)] + [HBM_SPEC] * n
    sums = [SDS(s.shape[1:], F32) for s in slabs]
    outs = [SDS((T, D_MODEL), F32), SDS((1, D_MODEL), F32)] + sums
    return pl.pallas_call(
        body, name="in_proj_bwd_x", grid=(steps,), in_specs=in_specs,
        out_specs=[_row_spec(tm, D_MODEL), _full_spec(outs[1])] + [_full_spec(s) for s in sums],
        out_shape=outs, scratch_shapes=_rs_scratch([s.shape for s in sums], None), compiler_params=_cp(("arbitrary",)),
    )(*dzs, x2, dh, norm_in, w_in_pad, *slabs)


SLAB_ROWS = IN_TOTAL // N_DEV


def _slab_segments(k):
    cuts = [(0, ZKR_ORIG, 0), (ZKR_ORIG, ZKR_ORIG + ROPE, NOPE), (ZKR_ORIG + ROPE, IN_TOTAL, LANES - ROPE)]
    lo, hi = k * SLAB_ROWS, (k + 1) * SLAB_ROWS
    return [(max(lo, a) - lo, max(lo, a) + shift, min(hi, b) - max(lo, a)) for a, b, shift in cuts if min(hi, b) > max(lo, a)]


def _in_proj_bwd_w(dzs, hn, tm):
    T = hn.shape[0]
    steps = T // tm

    def body(d0, d1, d2, d3, d4, hn_ref, slab_ref, acc_ref):
        @pl.when(pl.program_id(0) == 0)
        def _():
            acc_ref[...] = jnp.zeros_like(acc_ref)

        hn_v = hn_ref[...]
        for ref, (lo, hi) in zip((d0, d1, d2, d3, d4), _DZ_COLS):
            acc_ref[lo:hi, :] += _mm_tn(ref[...], hn_v)

        @pl.when(pl.program_id(0) == steps - 1)
        def _():
            for k in range(N_DEV):
                for at, src, rows in _slab_segments(k):
                    slab_ref[k, at:at + rows, :] = acc_ref[src:src + rows, :].astype(BF16)

    in_specs = [_row_spec(tm, hi - lo) for lo, hi in _DZ_COLS] + [_row_spec(tm, D_MODEL)]
    out = SDS((N_DEV, SLAB_ROWS, D_MODEL), BF16)
    return pl.pallas_call(
        body, name="in_proj_bwd_w", grid=(steps,), in_specs=in_specs, out_specs=_full_spec(out), out_shape=out,
        scratch_shapes=[pltpu.VMEM((ZTOT, D_MODEL), F32)], compiler_params=_cp(("arbitrary",)),
    )(*dzs, hn)


def _local_step(x2, tgt, norm_in, w_in_pad, q_norm, w_uq, kv_norm, w_ukv, pool_w, pool_scale, late_shards, norm_final):
    T = x2.shape[0]
    tm = min(512, T)
    tq = min(512, T)
    row = lambda v: v.reshape(1, -1)
    wuq_pad = jnp.pad(w_uq, ((0, 0), (0, 0), (0, HEAD_PAD - NOPE - ROPE))).reshape(Q_RANK, HW)
    wukv = w_ukv.reshape(KV_RANK, HW)
    rc, rsa, rsb = _rope_tables(T)

    hn, zgm, zga, zup, zgp, zfr, q_att, k_att, v_att, vt_att, w_ba, w_bp, w_out = _in_proj(
        x2, row(norm_in), w_in_pad, row(q_norm), wuq_pad, row(kv_norm), wukv, rc, rsa, rsb, tm, late_shards)
    w_ba, w_bp, w_out = _slabs_to_cols(w_ba), _slabs_to_cols(w_bp), w_out.reshape(D_MODEL, D_MODEL)
    o, lse_rows = _attn_fwd(q_att, k_att, vt_att, tq, 4)
    ypool = _pool_fwd(zup, zgp, pool_w, row(pool_scale))
    loss8, dh, dgm, doop, dga, dcap_rows, dyp, g_wout, g_wba, g_wbp, g_nf = _tail(
        x2, tgt, o, zga, ypool, zgm, w_ba, w_bp, w_out, row(norm_final), min(256, T))
    dup, dgp, g_pool_w, g_pool_scale = _pool_bwd(zup, zgp, dyp, pool_w, row(pool_scale))

    bf = lambda a: a.astype(BF16)
    slabs = [bf(g_wout).reshape(N_DEV, D_MODEL // N_DEV, D_MODEL), _cols_to_slabs(bf(g_wba)), _cols_to_slabs(bf(g_wbp))]
    early = [g_pool_w, g_pool_scale, g_nf, loss8[0]]
    packed = jnp.concatenate([_pack_rows(a) for a in early], axis=0)
    dq_att, dkv_nat, dkr4, s_wout, s_wba, s_wbp, tot_early = _attn_bwd(
        q_att, k_att, v_att, doop, lse_rows, dcap_rows, tq, 2, slabs, packed)
    s_pool_w, s_pool_scale, s_nf, s_loss = _unpack_rows(tot_early, early)

    dfr, g_wuq_pad, g_wukv, g_qn, g_kvn = _mla_bwd(
        dq_att, dkv_nat, dkr4, zfr, row(q_norm), wuq_pad, row(kv_norm), wukv, rc, rsa, rsb, tm)
    dzs = (dgm, dga, dup, dgp, dfr)
    slabs = [_in_proj_bwd_w(dzs, hn, tm),
             bf(g_wuq_pad.reshape(Q_RANK, HEADS, HEAD_PAD)[:, :, :NOPE + ROPE]).reshape(N_DEV, Q_RANK // N_DEV, -1),
             bf(g_wukv).reshape(N_DEV, KV_RANK // N_DEV, HW)]
    grad_x, g_nin, s_win, s_wuq, s_wukv = _in_proj_bwd_x(dzs, x2, dh, row(norm_in), w_in_pad, min(256, T), slabs)
    late = [g_nin, g_qn, g_kvn]
    (tot_late,) = _reduce_scatter([], jnp.concatenate([_pack_rows(a) for a in late], axis=0))
    s_nin, s_qn, s_kvn = _unpack_rows(tot_late, late)

    grads = dict(norm_in=s_nin, w_in=s_win, q_norm=s_qn, w_uq=s_wuq, kv_norm=s_kvn, w_ukv=s_wukv, pool_w=s_pool_w.reshape(-1, GROUP),
                 pool_scale=s_pool_scale, w_branch_attn=s_wba, w_branch_pool=s_wbp, w_out=s_wout, norm_final=s_nf)
    return s_loss[0], grad_x, grads


MESH_ID = pl.DeviceIdType.MESH
VMEM_SPEC = pl.BlockSpec(memory_space=pltpu.VMEM)
HBM_SPEC = pl.BlockSpec(memory_space=pl.ANY)


def _mesh_pos():
    return lax.axis_index("x"), lax.axis_index("y"), lax.axis_index("c")


def _slot(px, py, pc):
    return 4 * px + 2 * py + pc


def _all_gather_bf16(shards):
    n = len(shards)

    def body(*refs):
        ins, outs = refs[:n], refs[n:2 * n]
        land0, scratch = refs[2 * n], refs[2 * n + 1:]
        wpad_ref = outs[0]
        ag = _AllGather(ins, (land0,) + tuple(outs[1:]), scratch)
        ag.start()
        ag.forward()
        ag.finish()
        wpad_ref[ZKR:GA, :] = jnp.zeros((GA - ZKR, D_MODEL), BF16)
        for k in range(N_DEV):
            for at, dst, rows in _slab_segments(k):
                wpad_ref[dst:dst + rows, :] = land0[k, at:at + rows, :]

    return pl.pallas_call(
        body, name="all_gather_weights",
        in_specs=[VMEM_SPEC] * n, out_specs=[VMEM_SPEC] + [HBM_SPEC] * (n - 1),
        out_shape=[SDS((ZTOT, D_MODEL), BF16)] + [SDS((N_DEV,) + s.shape, BF16) for s in shards[1:]],
        scratch_shapes=[pltpu.VMEM((N_DEV,) + shards[0].shape, BF16)] + _ag_scratch([s.shape for s in shards]),
        compiler_params=_cp(),
    )(*shards)


def _ag_scratch(shapes):
    n = len(shapes)
    dma = pltpu.SemaphoreType.DMA
    return [pltpu.VMEM(tuple(s), BF16) for s in shapes] + [dma((7 * n,)), dma((7 * n,)), dma((n,))]


class _AllGather:
    def __init__(self, in_refs, dest_refs, scratch):
        n = self.n = len(in_refs)
        self.ins, self.dests, self.stage = in_refs, dest_refs, scratch[:n]
        self.send_sems, self.recv_sems, self.local_sems = scratch[n:]
        x, y, c = _mesh_pos()
        self.c, self.me, self.sibling = c, (x, y, c), (x, y, 1 - c)
        self.chips = [(1 - x, y), (x, 1 - y), (1 - x, 1 - y)]

    def _copy(self, a, k, block, to, from_stage=False):
        dst = self.dests[a].at[_slot(*block)]
        return pltpu.make_async_remote_copy(
            src_ref=self.stage[a] if from_stage else dst, dst_ref=dst, send_sem=self.send_sems.at[7 * a + k],
            recv_sem=self.recv_sems.at[7 * a + k], device_id=to, device_id_type=MESH_ID)

    def _mine(self):
        return [pltpu.make_async_copy(self.stage[a], self.dests[a].at[_slot(*self.me)], self.local_sems.at[a]) for a in range(self.n)]

    def _first(self):
        cps = []
        for a in range(self.n):
            cps.append(self._copy(a, 0, self.me, self.sibling, True))
            cps += [self._copy(a, 1 + j, self.me, (*chip, self.c), True) for j, chip in enumerate(self.chips)]
        return cps

    def _passed(self):
        return [self._copy(a, 4 + j, (*chip, self.c), self.sibling) for a in range(self.n) for j, chip in enumerate(self.chips)]

    def start(self):
        for a in range(self.n):
            self.stage[a][...] = self.ins[a][...].astype(BF16)
        for cp in self._mine() + self._first():
            cp.start()

    def forward(self):
        passed = self._passed()
        for a in range(self.n):
            for j, chip in enumerate(self.chips):
                self._copy(a, 1 + j, (*chip, self.c), self.me).wait_recv()
                passed[3 * a + j].start()

    def finish(self):
        for a in range(self.n):
            self._copy(a, 0, self.sibling, self.me).wait_recv()
            for j, chip in enumerate(self.chips):
                self._copy(a, 4 + j, (*chip, 1 - self.c), self.me).wait_recv()
        for cp in self._first() + self._passed():
            cp.wait_send()
        for cp in self._mine():
            cp.wait()


N_CHIPS = 4


def _reduce_scatter(slabs, packed):
    def body(*refs):
        n = len(slabs)
        rs = _ReduceScatter(refs[:n], refs[n], refs[n + 1:2 * n + 1], refs[2 * n + 1], refs[2 * n + 2:])
        rs.start1()
        rs.finish1_start2()
        rs.finish2()

    shapes = [s.shape[1:] for s in slabs]
    return pl.pallas_call(
        body, name="reduce_scatter_grads",
        in_specs=[HBM_SPEC] * len(slabs) + [VMEM_SPEC], out_specs=[VMEM_SPEC] * (len(slabs) + 1),
        out_shape=[SDS(s, F32) for s in shapes] + [SDS(packed.shape, F32)],
        scratch_shapes=_rs_scratch(shapes, packed.shape), compiler_params=_cp(),
    )(*slabs, packed)


def _rs_scratch(shapes, packed_shape):
    n = len(shapes)
    n1, n2 = N_CHIPS * n + 1, (N_CHIPS - 1) * (n + 1)
    dma = pltpu.SemaphoreType.DMA
    packed = [] if packed_shape is None else [pltpu.VMEM(packed_shape, F32), pltpu.VMEM((N_CHIPS,) + tuple(packed_shape), F32)]
    return ([pltpu.VMEM((N_CHIPS,) + tuple(s), BF16) for s in shapes] * 2 + [pltpu.VMEM((N_CHIPS - 1,) + tuple(s), BF16) for s in shapes] * 2
            + packed + [dma((max(N_CHIPS * n, 1),)), dma((n1,)), dma((n1,)), dma((n2,)), dma((n2,))])


class _ReduceScatter:
    def __init__(self, slab_refs, packed_ref, out_refs, ptot_ref, scratch):
        n = self.n = len(slab_refs)
        self.slabs, self.packed, self.outs, self.ptot = slab_refs, packed_ref, out_refs, ptot_ref
        self.own1, self.land1, self.send2, self.land2 = (scratch[k * n:(k + 1) * n] for k in range(4))
        rest = scratch[4 * n:]
        if packed_ref is not None:
            self.pland1, self.pland2 = rest[:2]
            rest = rest[2:]
        self.loc_sems, self.send1_sems, self.recv1_sems, self.send2_sems, self.recv2_sems = rest
        self.x, self.y, self.c = _mesh_pos()

    def _chip(self, r):
        return (1 - self.x if r & 2 else self.x, 1 - self.y if r & 1 else self.y)

    @staticmethod
    def _remote(src, dst, send_sem, recv_sem, to):
        return pltpu.make_async_remote_copy(src_ref=src, dst_ref=dst, send_sem=send_sem, recv_sem=recv_sem, device_id=to,
                                            device_id_type=MESH_ID)

    def _copies1(self):
        c, sibling = self.c, (self.x, self.y, 1 - self.c)
        cps = []
        for a in range(self.n):
            for r in range(N_CHIPS):
                k = N_CHIPS * a + r
                cps.append(pltpu.make_async_copy(self.slabs[a].at[_slot(*self._chip(r), c)], self.own1[a].at[r], self.loc_sems.at[k]))
                cps.append(self._remote(self.slabs[a].at[_slot(*self._chip(r), 1 - c)], self.land1[a].at[r],
                                        self.send1_sems.at[k], self.recv1_sems.at[k], sibling))
        if self.packed is not None:
            k = N_CHIPS * self.n
            cps.append(self._remote(self.packed, self.pland1, self.send1_sems.at[k], self.recv1_sems.at[k], sibling))
        return cps

    def _copies2(self):
        cps = []
        for a in range(self.n):
            for r in range(1, N_CHIPS):
                k = (N_CHIPS - 1) * a + r - 1
                cps.append(self._remote(self.send2[a].at[r - 1], self.land2[a].at[r - 1], self.send2_sems.at[k], self.recv2_sems.at[k],
                                        (*self._chip(r), self.c)))
        if self.packed is not None:
            for r in range(1, N_CHIPS):
                k = (N_CHIPS - 1) * self.n + r - 1
                cps.append(self._remote(self.pland2.at[0], self.pland2.at[r], self.send2_sems.at[k], self.recv2_sems.at[k],
                                        (*self._chip(r), self.c)))
        return cps

    def start1(self):
        for cp in self._copies1():
            cp.start()

    def finish1_start2(self):
        for cp in self._copies1():
            cp.wait()
        for a in range(self.n):
            self.outs[a][...] = self.own1[a][0].astype(F32) + self.land1[a][0].astype(F32)
            for r in range(1, N_CHIPS):
                self.send2[a][r - 1] = (self.own1[a][r].astype(F32) + self.land1[a][r].astype(F32)).astype(BF16)
        if self.packed is not None:
            self.pland2[0] = self.packed[...] + self.pland1[...]
        for cp in self._copies2():
            cp.start()

    def finish2(self):
        for cp in self._copies2():
            cp.wait()
        for a in range(self.n):
            l2 = self.land2[a]
            self.outs[a][...] = self.outs[a][...] + ((l2[0].astype(F32) + l2[1].astype(F32)) + l2[2].astype(F32))
        if self.packed is not None:
            p2 = self.pland2
            self.ptot[...] = (p2[0] + p2[1]) + (p2[2] + p2[3])


def _adamw(ws, gs, ms, vs):
    n = len(ws)

    def body(*refs):
        for k in range(n):
            w, g, m, v = (refs[j * n + k][...] for j in range(4))
            d_ref, nm_ref, nv_ref = (refs[(4 + j) * n + k] for j in range(3))
            m = ADAM_B1 * m + (1.0 - ADAM_B1) * g
            v = ADAM_B2 * v + (1.0 - ADAM_B2) * jnp.square(g)
            m_hat = m / (1.0 - ADAM_B1 ** ADAM_STEP)
            v_hat = v / (1.0 - ADAM_B2 ** ADAM_STEP)
            d_ref[...] = -ADAM_LR * (m_hat / (jnp.sqrt(v_hat) + ADAM_EPS) + ADAM_WD * w)
            nm_ref[...] = m
            nv_ref[...] = v

    outs = pl.pallas_call(
        body, name="adamw", in_specs=[VMEM_SPEC] * (4 * n), out_specs=[VMEM_SPEC] * (3 * n),
        out_shape=[SDS(w.shape, F32) for w in ws] * 3, compiler_params=_cp(),
    )(*ws, *gs, *ms, *vs)
    return outs[:n], outs[n:2 * n], outs[2 * n:]


WEIGHTS = ("norm_in", "w_in", "q_norm", "w_uq", "kv_norm", "w_ukv", "pool_w", "pool_scale", "w_branch_attn", "w_branch_pool",
           "w_out", "norm_final")
SUBLANES = 8


def _cols_to_slabs(g):
    r = g.shape[0]
    return g.reshape(r, N_DEV, -1).transpose(1, 0, 2)


def _slabs_to_cols(s):
    return s.transpose(1, 0, 2).reshape(s.shape[1], -1)


def _pack_rows(a):
    a = a.reshape(-1, LANES)
    return jnp.pad(a, ((0, -a.shape[0] % SUBLANES), (0, 0)))


def _unpack_rows(packed, like):
    out, row = [], 0
    for a in like:
        rows = a.size // LANES
        out.append(packed[row:row + rows].reshape(a.shape))
        row += rows + (-rows % SUBLANES)
    return out


def kernel(x, norm_in, w_in, q_norm, w_uq, kv_norm, w_ukv, pool_w, pool_scale, w_branch_attn, w_branch_pool, w_out, norm_final, loss_target, m_norm_in, m_w_in, m_q_norm, m_w_uq, m_kv_norm, m_w_ukv, m_pool_w, m_pool_scale, m_w_branch_attn, m_w_branch_pool, m_w_out, m_norm_final, v_norm_in, v_w_in, v_q_norm, v_w_uq, v_kv_norm, v_w_ukv, v_pool_w, v_pool_scale, v_w_branch_attn, v_w_branch_pool, v_w_out, v_norm_final):
    w = dict(norm_in=norm_in, w_in=w_in, q_norm=q_norm, w_uq=w_uq, kv_norm=kv_norm, w_ukv=w_ukv, pool_w=pool_w, pool_scale=pool_scale,
             w_branch_attn=w_branch_attn, w_branch_pool=w_branch_pool, w_out=w_out, norm_final=norm_final)
    m = dict(norm_in=m_norm_in, w_in=m_w_in, q_norm=m_q_norm, w_uq=m_w_uq, kv_norm=m_kv_norm, w_ukv=m_w_ukv, pool_w=m_pool_w,
             pool_scale=m_pool_scale, w_branch_attn=m_w_branch_attn, w_branch_pool=m_w_branch_pool, w_out=m_w_out, norm_final=m_norm_final)
    v = dict(norm_in=v_norm_in, w_in=v_w_in, q_norm=v_q_norm, w_uq=v_w_uq, kv_norm=v_kv_norm, w_ukv=v_w_ukv, pool_w=v_pool_w,
             pool_scale=v_pool_scale, w_branch_attn=v_w_branch_attn, w_branch_pool=v_w_branch_pool, w_out=v_w_out, norm_final=v_norm_final)

    def as2d(name, a):
        if name == "w_in":
            return a.T
        if name in ("w_uq", "w_ukv"):
            return a.reshape(a.shape[0], -1)
        if name == "pool_w":
            return a.reshape(-1, GROUP)
        return a.reshape(1, -1) if a.ndim == 1 else a

    def unshape(name, a):
        return a.T if name == "w_in" else a.reshape(w[name].shape)

    w_in_pad, w_uq_full, w_ukv_full = _all_gather_bf16([as2d(k, w[k]) for k in ("w_in", "w_uq", "w_ukv")])
    loss, grad_x, g2d = _local_step(
        x.reshape(x.shape[1:]), loss_target.reshape(x.shape[1:]), norm_in, w_in_pad, q_norm,
        w_uq_full.reshape(Q_RANK, HEADS, NOPE + ROPE), kv_norm, w_ukv_full.reshape(KV_RANK, HEADS, NOPE + VDIM),
        pool_w, pool_scale, [w_branch_attn, w_branch_pool, w_out], norm_final)

    deltas, new_m, new_v = _adamw([as2d(k, w[k]) for k in WEIGHTS], [g2d[k] for k in WEIGHTS],
                                  [as2d(k, m[k]) for k in WEIGHTS], [as2d(k, v[k]) for k in WEIGHTS])
    shaped = lambda arrs: [unshape(k, a) for k, a in zip(WEIGHTS, arrs)]
    return (loss, grad_x.reshape(x.shape), *shaped([g2d[k] for k in WEIGHTS]), *shaped(deltas), *shaped(new_m), *shaped(new_v))
```

```python
import functools

import jax
import jax.numpy as jnp
import numpy as np
from jax import lax
from jax.experimental import pallas as pl
from jax.experimental.pallas import tpu as pltpu

F32 = jnp.float32
BF16 = jnp.bfloat16
SDS = jax.ShapeDtypeStruct

D_MODEL = 1024
HEADS = 8
NOPE = 64
ROPE = 32
VDIM = 64
Q_RANK = 384
KV_RANK = 256
MLA_W = HEADS * VDIM
POOL_W = 512
POOL_GROUPS = 4
GROUP = POOL_W // POOL_GROUPS
CHUNK = 64
ROPE_THETA = 10000.0
EPS = 1e-6
SCALE = (NOPE + ROPE) ** -0.5
LOG2E = 1.4426950408889634
LN2 = 0.6931471805599453
QK_SCALE_LOG2 = SCALE * LOG2E
IN_TOTAL = 4256
ADAM_LR, ADAM_B1, ADAM_B2, ADAM_EPS, ADAM_WD, ADAM_STEP = 0.001, 0.9, 0.999, 1e-08, 0.01, 10

N_DEV = 8
LANES = 128
HEAD_PAD = LANES
HW = HEADS * HEAD_PAD

ZQ, ZKV, ZKR, GA, UP, GP, GM, ZTOT = 0, 384, 640, 768, 1280, 1792, 2304, 4352
FRONT_W = GA
ZKR_ORIG = 640

VMEM_LIMIT = 56 * 1024 * 1024


def _cp(sem=None, **kw):
    if sem is not None:
        kw["dimension_semantics"] = sem
    return pltpu.CompilerParams(vmem_limit_bytes=VMEM_LIMIT, **kw)


def _mm(a, b):
    return lax.dot_general(a, b, (((1,), (0,)), ((), ())), preferred_element_type=F32)


def _mm_nt(a, b):
    return lax.dot_general(a, b, (((1,), (1,)), ((), ())), preferred_element_type=F32)


def _mm_tn(a, b):
    return lax.dot_general(a, b, (((0,), (0,)), ((), ())), preferred_element_type=F32)


def _row_spec(tm, w):
    return pl.BlockSpec((tm, w), lambda i: (i, 0))


def _full_spec(a):
    nd = len(a.shape)
    return pl.BlockSpec(a.shape, lambda *_: (0,) * nd)


def _rope(v, c, sa, sb, sign):
    n = v.shape[-1]
    reps = n // LANES
    if reps > 1:
        c, sa, sb = (jnp.tile(t, (1, reps)) for t in (c, sa, sb))
    up = pltpu.roll(v, n - ROPE // 2, 1)
    dn = pltpu.roll(v, ROPE // 2, 1)
    return v * c + sign * (up * sa + dn * sb)


def _rope_tables(T):
    half = ROPE // 2
    inv_freq = np.float32(ROPE_THETA) ** (-np.arange(half, dtype=np.float32) / np.float32(half))
    ang = np.arange(T, dtype=np.float32)[:, None] * inv_freq[None, :].astype(np.float32)
    cos, sin = np.cos(ang.astype(np.float64)).astype(np.float32), np.sin(ang.astype(np.float64)).astype(np.float32)
    z16 = np.zeros((T, half), np.float32)
    z32 = np.zeros((T, LANES - NOPE - ROPE), np.float32)
    c = np.concatenate([np.ones((T, NOPE), np.float32), cos, cos, z32], axis=1)
    sa = np.concatenate([np.zeros((T, NOPE), np.float32), -sin, z16, z32], axis=1)
    sb = np.concatenate([np.zeros((T, NOPE), np.float32), z16, sin, z32], axis=1)
    return jnp.asarray(c), jnp.asarray(sa), jnp.asarray(sb)


def _silu_parts(g):
    sg = jax.nn.sigmoid(g)
    return g * sg, sg + g * sg * (1.0 - sg)


def _in_proj(x2, norm_in, w_in_pad, q_norm, wuq_pad, kv_norm, wukv, rc, rsa, rsb, tm, late_shards):
    T = x2.shape[0]
    steps = T // tm
    n = len(late_shards)

    def body(x_ref, nin_ref, win_ref, qn_ref, wuq_ref, kvn_ref, wukv_ref, c_ref, sa_ref, sb_ref, *rest):
        hn_ref, zgm_ref, zga_ref, zup_ref, zgp_ref, zfr_ref, q_ref, k_ref, v_ref, vt_ref = rest[n:n + 10]
        ag = _AllGather(rest[:n], rest[n + 10:2 * n + 10], rest[2 * n + 10:])
        step = pl.program_id(0)
        pl.when(step == 0)(ag.start)
        pl.when(step == min(5, steps - 1))(ag.forward)
        xf = x_ref[...]
        r = lax.rsqrt(jnp.mean(xf * xf, axis=-1, keepdims=True) + EPS)
        hn = (xf * r * nin_ref[...]).astype(BF16)
        hn_ref[...] = hn
        z = _mm_nt(hn, win_ref[...])
        zgm_ref[...] = z[:, GM:ZTOT]
        zga_ref[...] = z[:, GA:UP]
        zup_ref[...] = z[:, UP:GP]
        zgp_ref[...] = z[:, GP:GM]
        zfr_ref[...] = z[:, ZQ:GA]
        zq, zkv, zkr = z[:, ZQ:ZKV], z[:, ZKV:ZKR], z[:, ZKR:GA]
        c, sa, sb = c_ref[...], sa_ref[...], sb_ref[...]
        rq = lax.rsqrt(jnp.mean(zq * zq, axis=-1, keepdims=True) + EPS)
        cq = (zq * rq * qn_ref[...]).astype(BF16)
        q = _rope(_mm(cq, wuq_ref[...]), c, sa, sb, 1.0)
        q_ref[...] = (q * QK_SCALE_LOG2).astype(BF16)
        rkv = lax.rsqrt(jnp.mean(zkv * zkv, axis=-1, keepdims=True) + EPS)
        ckv = (zkv * rkv * kvn_ref[...]).astype(BF16)
        kv = _mm(ckv, wukv_ref[...])
        kr = _rope(zkr, c, sa, sb, 1.0)
        lane = lax.broadcasted_iota(jnp.int32, kv.shape, 1) % LANES
        k_ref[...] = jnp.where(lane < NOPE, kv, jnp.tile(kr, (1, HEADS))).astype(BF16)
        v = jnp.where(lane < NOPE, 1.0, kv).astype(BF16)
        v_ref[...] = v
        vt_ref[...] = v.T
        pl.when(step == steps - 1)(ag.finish)

    ins = (x2, norm_in, w_in_pad, q_norm, wuq_pad, kv_norm, wukv, rc, rsa, rsb)
    in_specs = [_row_spec(tm, D_MODEL), _full_spec(norm_in), _full_spec(w_in_pad), _full_spec(q_norm), _full_spec(wuq_pad),
                _full_spec(kv_norm), _full_spec(wukv), _row_spec(tm, LANES), _row_spec(tm, LANES), _row_spec(tm, LANES)]
    widths = [(D_MODEL, BF16), (ZTOT - GM, F32), (UP - GA, F32), (GP - UP, F32), (GM - GP, F32), (FRONT_W, F32),
              (HW, BF16), (HW, BF16), (HW, BF16)]
    return pl.pallas_call(
        body, name="in_proj", grid=(steps,), in_specs=in_specs + [_full_spec(s) for s in late_shards],
        out_specs=[_row_spec(tm, w) for w, _ in widths] + [pl.BlockSpec((HW, tm), lambda i: (0, i))] + [HBM_SPEC] * n,
        out_shape=[SDS((T, w), dt) for w, dt in widths] + [SDS((HW, T), BF16)]
        + [SDS((N_DEV,) + s.shape, BF16) for s in late_shards],
        scratch_shapes=_ag_scratch([s.shape for s in late_shards]), compiler_params=_cp(("arbitrary",)),
    )(*ins, *late_shards)


def _chunk_mask(n_q, n_k, q_off, transposed):
    shape = (n_k, n_q) if transposed else (n_q, n_k)
    q = (lax.broadcasted_iota(jnp.int32, shape, 1 if transposed else 0) + q_off) // CHUNK
    k = lax.broadcasted_iota(jnp.int32, shape, 0 if transposed else 1) // CHUNK
    return k <= q


def _store_pair_rows(ref, k, pair):
    t = pair.T
    ref[k, 0:1, :] = t[0:1, :]
    ref[k, 1:2, :] = t[VDIM:VDIM + 1, :]


def _attn_fwd(q_att, k_att, vt_att, tq, hps):
    T = q_att.shape[0]
    head_lanes = [slice(h * LANES, (h + 1) * LANES) for h in range(hps)]

    def body(q_ref, k_ref, vt_ref, o_ref, lser_ref):
        i = pl.program_id(1)
        mask = _chunk_mask(tq, tq, 0, True)
        lane = lax.broadcasted_iota(jnp.int32, (tq, LANES), 1)
        qs = [q_ref[:, hs] for hs in head_lanes]

        def step(j, carry, masked):
            off = pl.multiple_of(j * tq, tq)
            sts = [_mm_nt(k_ref[pl.ds(off, tq), hs], qh) for qh, hs in zip(qs, head_lanes)]
            if masked:
                sts = [jnp.where(mask, st, -jnp.inf) for st in sts]
            ms = [jnp.maximum(m, jnp.max(st, axis=0, keepdims=True)) for (m, _), st in zip(carry, sts)]
            pts = [jnp.exp2(st - m_new).astype(BF16) for st, m_new in zip(sts, ms)]
            return tuple((m_new, jnp.exp2(m - m_new) * acc + _mm(vt_ref[hs, pl.ds(off, tq)], pt))
                         for (m, acc), m_new, pt, hs in zip(carry, ms, pts, head_lanes))

        init = ((jnp.full((1, tq), -jnp.inf, F32), jnp.zeros((LANES, tq), F32)),) * hps
        res = step(i, lax.fori_loop(0, i, functools.partial(step, masked=False), init), True)
        for pair in range(hps // 2):
            (ma, acca), (mb, accb) = res[2 * pair], res[2 * pair + 1]
            la, lb = acca[:1], accb[:1]
            oa, ob = (acca / la).T, (accb / lb).T
            o_ref[:, pair * LANES:(pair + 1) * LANES] = jnp.where(lane < VDIM, pltpu.roll(oa, VDIM, 1), ob)
            lser_ref[pair, 0:1, :] = ma + jnp.log2(la)
            lser_ref[pair, 1:2, :] = mb + jnp.log2(lb)

    qspec = pl.BlockSpec((tq, hps * LANES), lambda p, i: (i, p))
    kspec = pl.BlockSpec((T, hps * LANES), lambda p, i: (0, p))
    vspec = pl.BlockSpec((hps * LANES, T), lambda p, i: (p, 0))
    ospec = pl.BlockSpec((tq, hps * VDIM), lambda p, i: (i, p))
    return pl.pallas_call(
        body, name="attn_fwd", grid=(HEADS // hps, T // tq), in_specs=[qspec, kspec, vspec],
        out_specs=[ospec, pl.BlockSpec((hps // 2, 2, tq), lambda p, i: (p, 0, i))],
        out_shape=[SDS((T, MLA_W), F32), SDS((HEADS // 2, 2, T), F32)],
        compiler_params=_cp(("parallel", "parallel")),
    )(q_att, k_att, vt_att)


def _pick(g, vals):
    out = vals[-1]
    for k in range(len(vals) - 2, -1, -1):
        out = jnp.where(g == k, vals[k], out)
    return out


def _window_sum(u, g, forward):
    T = u.shape[0]
    row = lax.broadcasted_iota(jnp.int32, u.shape, 0)

    def sh(s, k):
        if forward:
            return jnp.where(row >= k, pltpu.roll(s, k, 0), 0.0)
        return jnp.where(row < T - k, pltpu.roll(s, T - k, 0), 0.0)

    sums, s = [], u
    for k in (1, 2, 4, 8):
        s = s + sh(s, k)
        sums.append(s)
    return _pick(g, sums)


def _pool_count(shape, g):
    row = lax.broadcasted_iota(jnp.int32, shape, 0)
    return jnp.minimum(row + 1, lax.shift_left(jnp.int32(2), g)).astype(F32)


def _pool_fwd(zup, zgp, pool_w, pool_scale):
    T = zup.shape[0]

    def body(u_ref, g_ref, w_ref, sc_ref, y_ref):
        g = pl.program_id(0)
        u = u_ref[...]
        d = _window_sum(u, g, True) / _pool_count(u.shape, g) - u
        lin = _mm(d.astype(BF16), w_ref[0].astype(BF16))
        silu, _ = _silu_parts(g_ref[...])
        y_ref[...] = (lin * sc_ref[...] * silu).astype(BF16)

    col = pl.BlockSpec((T, GROUP), lambda g: (0, g))
    return pl.pallas_call(
        body, name="pool_fwd", grid=(POOL_GROUPS,),
        in_specs=[col, col, pl.BlockSpec((1, GROUP, GROUP), lambda g: (g, 0, 0)), pl.BlockSpec((1, GROUP), lambda g: (0, g))],
        out_specs=col, out_shape=SDS((T, POOL_W), BF16), compiler_params=_cp(("parallel",)),
    )(zup, zgp, pool_w, pool_scale)


def _pool_bwd(zup, zgp, dyp, pool_w, pool_scale):
    T = zup.shape[0]

    def body(u_ref, g_ref, dy_ref, w_ref, sc_ref, du_ref, dg_ref, gw_ref, gsc_ref):
        g = pl.program_id(0)
        u = u_ref[...]
        cnt = _pool_count(u.shape, g)
        d = (_window_sum(u, g, True) / cnt - u).astype(BF16)
        wb = w_ref[0].astype(BF16)
        lin = _mm(d, wb)
        sc = sc_ref[...]
        silu, dsilu = _silu_parts(g_ref[...])
        dy = dy_ref[...]
        dg_ref[...] = (dy * lin * sc * dsilu).astype(BF16)
        dpre = dy * silu
        gsc_ref[...] = jnp.sum(dpre * lin, axis=0, keepdims=True)
        dlin = (dpre * sc).astype(BF16)
        gw_ref[0] = _mm_tn(d, dlin)
        dd = _mm_nt(dlin, wb)
        du_ref[...] = (_window_sum(dd / cnt, g, False) - dd).astype(BF16)

    col = pl.BlockSpec((T, GROUP), lambda g: (0, g))
    wspec = pl.BlockSpec((1, GROUP, GROUP), lambda g: (g, 0, 0))
    vspec = pl.BlockSpec((1, GROUP), lambda g: (0, g))
    return pl.pallas_call(
        body, name="pool_bwd", grid=(POOL_GROUPS,), in_specs=[col, col, col, wspec, vspec], out_specs=[col, col, wspec, vspec],
        out_shape=[SDS((T, POOL_W), BF16), SDS((T, POOL_W), BF16), SDS((POOL_GROUPS, GROUP, GROUP), F32), SDS((1, POOL_W), F32)],
        compiler_params=_cp(("parallel",)),
    )(zup, zgp, dyp, pool_w, pool_scale)


def _tail(x2, tgt, o, zga, ypool, zgm, wba, wbp, wout, norm_final, tm):
    T = x2.shape[0]
    steps = T // tm
    cols = D_MODEL // N_DEV

    def body(x_ref, tgt_ref, o_ref, zga_ref, yp_ref, zgm_ref, wba_ref, wbp_ref, wout_ref, nf_ref,
             loss_ref, dh_ref, dgm_ref, doop_ref, dga_ref, dcapr_ref, dyp_ref, swout_ref, swba_ref, swbp_ref, gnf_ref,
             gwout_ref, gwba_ref, gwbp_ref):
        @pl.when(pl.program_id(0) == 0)
        def _():
            for ref in (loss_ref, gwout_ref, gwba_ref, gwbp_ref, gnf_ref):
                ref[...] = jnp.zeros_like(ref)

        o_v = o_ref[...]
        silu, dsilu = _silu_parts(zga_ref[...])
        ya = (o_v * silu).astype(BF16)
        yp = yp_ref[...]
        wba_v = jnp.concatenate([wba_ref[k] for k in range(N_DEV)], axis=1)
        wbp_v = jnp.concatenate([wbp_ref[k] for k in range(N_DEV)], axis=1)
        wout_v = wout_ref[...]
        a = _mm(ya, wba_v)
        p = _mm(yp, wbp_v)
        gate = jax.nn.sigmoid(zgm_ref[...])
        ga, gp = gate[:, :D_MODEL], gate[:, D_MODEL:]
        mg = (ga * a + gp * p).astype(BF16)
        h = x_ref[...] + _mm(mg, wout_v)
        r = lax.rsqrt(jnp.mean(h * h, axis=-1, keepdims=True) + EPS)
        gf = nf_ref[...]
        hr = h * r
        e = hr * gf - tgt_ref[...]
        loss_ref[...] += (0.5 / D_MODEL) * jnp.sum(e * e)
        dy = e * (1.0 / D_MODEL)
        gnf_ref[...] += jnp.sum(dy * hr, axis=0, keepdims=True)
        u = dy * gf
        dh = r * (u - hr * jnp.mean(u * hr, axis=-1, keepdims=True))
        dh_ref[...] = dh
        dhb = dh.astype(BF16)
        dmg = _mm_nt(dhb, wout_v)
        gwout_ref[...] += _mm_tn(mg, dhb)
        dgm_ref[:, :D_MODEL] = (dmg * a * ga * (1.0 - ga)).astype(BF16)
        dgm_ref[:, D_MODEL:] = (dmg * p * gp * (1.0 - gp)).astype(BF16)
        dab = (dmg * ga).astype(BF16)
        dpb = (dmg * gp).astype(BF16)
        dya = _mm_nt(dab, wba_v)
        gwba_ref[...] += _mm_tn(ya, dab)
        dyp_ref[...] = _mm_nt(dpb, wbp_v)
        gwbp_ref[...] += _mm_tn(yp, dpb)
        do = dya * silu
        dga_ref[...] = (dya * o_v * dsilu).astype(BF16)
        prod = do * o_v
        lo = lax.broadcasted_iota(jnp.int32, (tm, LANES), 1) < VDIM
        for pair in range(HEADS // 2):
            ls = slice(pair * LANES, (pair + 1) * LANES)
            do_p, prod_p = do[:, ls], prod[:, ls]
            dcap_a = jnp.sum(jnp.where(lo, prod_p, 0.0), axis=-1, keepdims=True)
            dcap_b = jnp.sum(jnp.where(lo, 0.0, prod_p), axis=-1, keepdims=True)
            _store_pair_rows(dcapr_ref, pair, jnp.where(lo, dcap_a, dcap_b))
            doop_ref[:, 2 * pair * LANES:(2 * pair + 1) * LANES] = jnp.where(lo, 0.0, pltpu.roll(do_p, VDIM, 1)).astype(BF16)
            doop_ref[:, (2 * pair + 1) * LANES:(2 * pair + 2) * LANES] = jnp.where(lo, 0.0, do_p).astype(BF16)

        @pl.when(pl.program_id(0) == steps - 1)
        def _():
            for k in range(N_DEV):
                swout_ref[k] = gwout_ref[k * cols:(k + 1) * cols, :].astype(BF16)
                swba_ref[k] = gwba_ref[:, k * cols:(k + 1) * cols].astype(BF16)
                swbp_ref[k] = gwbp_ref[:, k * cols:(k + 1) * cols].astype(BF16)

    ins = (x2, tgt, o, zga, ypool, zgm, wba, wbp, wout, norm_final)
    in_specs = [_row_spec(tm, D_MODEL), _row_spec(tm, D_MODEL), _row_spec(tm, MLA_W), _row_spec(tm, MLA_W), _row_spec(tm, POOL_W),
                _row_spec(tm, 2 * D_MODEL), _full_spec(wba), _full_spec(wbp), _full_spec(wout), _full_spec(norm_final)]
    outs = [SDS((8, LANES), F32), SDS((T, D_MODEL), F32), SDS((T, 2 * D_MODEL), BF16), SDS((T, HW), BF16), SDS((T, MLA_W), BF16),
            SDS((HEADS // 2, 2, T), F32), SDS((T, POOL_W), F32),
            SDS((N_DEV, cols, D_MODEL), BF16), SDS((N_DEV, MLA_W, cols), BF16), SDS((N_DEV, POOL_W, cols), BF16), SDS((1, D_MODEL), F32)]
    out_specs = [_full_spec(outs[0]), _row_spec(tm, D_MODEL), _row_spec(tm, 2 * D_MODEL), _row_spec(tm, HW), _row_spec(tm, MLA_W),
                 pl.BlockSpec((HEADS // 2, 2, tm), lambda i: (0, 0, i)), _row_spec(tm, POOL_W),
                 _full_spec(outs[7]), _full_spec(outs[8]), _full_spec(outs[9]), _full_spec(outs[10])]
    return pl.pallas_call(
        body, name="tail", grid=(steps,), in_specs=in_specs, out_specs=out_specs, out_shape=outs,
        scratch_shapes=[pltpu.VMEM((D_MODEL, D_MODEL), F32), pltpu.VMEM((MLA_W, D_MODEL), F32), pltpu.VMEM((POOL_W, D_MODEL), F32)],
        compiler_params=_cp(("arbitrary",)),
    )(*ins)


def _attn_bwd(q_att, k_att, v_att, doop, lse_rows, dcap_rows, tq, hps, slabs, packed):
    T = q_att.shape[0]
    nq = T // tq
    n = len(slabs)
    groups = HEADS // hps
    head_lanes = [slice(h * LANES, (h + 1) * LANES) for h in range(hps)]

    def body(q_ref, k_ref, v_ref, doop_ref, lse_ref, dcap_ref, *rest):
        slab_refs, packed_ref = rest[:n], rest[n]
        dq_ref, dkv_ref, dkr_ref = rest[n + 1:n + 4]
        sum_refs, ptot_ref = rest[n + 4:2 * n + 4], rest[2 * n + 4]
        dq_acc = rest[2 * n + 5]
        rs = _ReduceScatter(slab_refs, packed_ref, sum_refs, ptot_ref, rest[2 * n + 6:])
        group, j = pl.program_id(0), pl.program_id(1)
        pl.when((group == 0) & (j == 0))(rs.start1)
        pl.when((group == 1) & (j == 0))(rs.finish1_start2)
        mask = _chunk_mask(tq, tq, 0, True)
        lane = lax.broadcasted_iota(jnp.int32, (tq, LANES), 1)
        ks = [k_ref[:, hs] for hs in head_lanes]
        vs = [v_ref[:, hs] for hs in head_lanes]
        kts = [kh.T for kh in ks]

        @pl.when(j == 0)
        def _():
            dq_acc[...] = jnp.zeros_like(dq_acc)

        def step(i, carry, masked):
            rows = pl.ds(pl.multiple_of(i * tq, tq), tq)
            heads = range(hps)
            stat = lambda h: (h // 2, slice(h % 2, h % 2 + 1), rows)
            qhs = [q_ref[rows, hs] for hs in head_lanes]
            doops = [doop_ref[rows, hs] for hs in head_lanes]
            sts = [_mm_nt(ks[h], qhs[h]) for h in heads]
            dpts = [_mm_nt(vs[h], doops[h]) for h in heads]
            pts = [jnp.exp2(sts[h] - lse_ref[stat(h)]) for h in heads]
            if masked:
                pts = [jnp.where(mask, pt, 0.0) for pt in pts]
            dsts = [(pts[h] * (dpts[h] - dcap_ref[stat(h)])).astype(BF16) for h in heads]
            dvs = [_mm(pts[h].astype(BF16), doops[h]) for h in heads]
            dks = [_mm(dsts[h], qhs[h]) for h in heads]
            for h, hs in enumerate(head_lanes):
                dq_acc[hs, rows] += _mm(kts[h], dsts[h])
            return tuple((dk + dks[h], dv + dvs[h]) for h, (dk, dv) in enumerate(carry))

        zero = jnp.zeros((tq, LANES), F32)
        carry = step(j, ((zero, zero),) * hps, True)
        res = lax.fori_loop(j + 1, nq, functools.partial(step, masked=False), carry)
        dkr = None
        for (dk, dv), hs in zip(res, head_lanes):
            dk = dk * LN2
            dkv_ref[:, hs] = jnp.where(lane < NOPE, dk, dv).astype(BF16)
            dkr = dk if dkr is None else dkr + dk
        dkr_ref[0] = jnp.where((lane >= NOPE) & (lane < NOPE + ROPE), dkr, 0.0)

        @pl.when(j == nq - 1)
        def _():
            dq_ref[...] = (dq_acc[...] * SCALE).T.astype(BF16)

        pl.when((group == groups - 1) & (j == nq - 1))(rs.finish2)

    kspec = pl.BlockSpec((tq, hps * LANES), lambda p, j: (j, p))
    qspec = pl.BlockSpec((T, hps * LANES), lambda p, j: (0, p))
    rspec = pl.BlockSpec((hps // 2, 2, T), lambda p, j: (p, 0, 0))
    sums = [SDS(s.shape[1:], F32) for s in slabs] + [SDS(packed.shape, F32)]
    return pl.pallas_call(
        body, name="attn_bwd", grid=(groups, nq),
        in_specs=[qspec, kspec, kspec, qspec, rspec, rspec] + [HBM_SPEC] * n + [_full_spec(packed)],
        out_specs=[qspec, kspec, pl.BlockSpec((1, tq, LANES), lambda p, j: (p, j, 0))] + [_full_spec(s) for s in sums],
        out_shape=[SDS((T, HW), BF16), SDS((T, HW), BF16), SDS((groups, T, LANES), F32)] + sums,
        scratch_shapes=[pltpu.VMEM((hps * LANES, T), F32)] + _rs_scratch([s.shape for s in sums[:-1]], packed.shape),
        compiler_params=_cp(("arbitrary", "arbitrary")),
    )(q_att, k_att, v_att, doop, lse_rows, dcap_rows, *slabs, packed)


def _rms_bwd(z, gain, dout):
    r = lax.rsqrt(jnp.mean(z * z, axis=-1, keepdims=True) + EPS)
    zr = z * r
    u = dout * gain
    return r * (u - zr * jnp.mean(u * zr, axis=-1, keepdims=True)), jnp.sum(dout * zr, axis=0, keepdims=True)


def _mla_bwd(dq_att, dkv_nat, dkr4, zfr, q_norm, wuq_pad, kv_norm, wukv, rc, rsa, rsb, tm):
    T = dq_att.shape[0]

    def body(dq_ref, dkv_ref, dkr_ref, zfr_ref, qn_ref, wuq_ref, kvn_ref, wukv_ref, c_ref, sa_ref, sb_ref,
             dfr_ref, gwuq_ref, gwukv_ref, gqn_ref, gkvn_ref):
        @pl.when(pl.program_id(0) == 0)
        def _():
            for ref in (gwuq_ref, gwukv_ref, gqn_ref, gkvn_ref):
                ref[...] = jnp.zeros_like(ref)

        c, sa, sb = c_ref[...], sa_ref[...], sb_ref[...]
        zq, zkv = zfr_ref[:, :Q_RANK], zfr_ref[:, Q_RANK:Q_RANK + KV_RANK]
        qn, kvn = qn_ref[...], kvn_ref[...]
        cq = (zq * lax.rsqrt(jnp.mean(zq * zq, axis=-1, keepdims=True) + EPS) * qn).astype(BF16)
        ckv = (zkv * lax.rsqrt(jnp.mean(zkv * zkv, axis=-1, keepdims=True) + EPS) * kvn).astype(BF16)
        dq = _rope(dq_ref[...].astype(F32), c, sa, sb, -1.0).astype(BF16)
        gwuq_ref[...] += _mm_tn(cq, dq)
        dzq, gqn = _rms_bwd(zq, qn, _mm_nt(dq, wuq_ref[...]))
        gqn_ref[...] += gqn
        dkv = dkv_ref[...]
        gwukv_ref[...] += _mm_tn(ckv, dkv)
        dzkv, gkvn = _rms_bwd(zkv, kvn, _mm_nt(dkv, wukv_ref[...]))
        gkvn_ref[...] += gkvn
        dkr = functools.reduce(lambda a, b: a + b, [dkr_ref[g] for g in range(dkr4.shape[0])])
        dfr_ref[:, :Q_RANK] = dzq.astype(BF16)
        dfr_ref[:, Q_RANK:Q_RANK + KV_RANK] = dzkv.astype(BF16)
        dfr_ref[:, Q_RANK + KV_RANK:] = _rope(dkr, c, sa, sb, -1.0).astype(BF16)

    ins = (dq_att, dkv_nat, dkr4, zfr, q_norm, wuq_pad, kv_norm, wukv, rc, rsa, rsb)
    in_specs = [_row_spec(tm, HW), _row_spec(tm, HW), pl.BlockSpec((dkr4.shape[0], tm, LANES), lambda i: (0, i, 0)), _row_spec(tm, FRONT_W),
                _full_spec(q_norm), _full_spec(wuq_pad), _full_spec(kv_norm), _full_spec(wukv),
                _row_spec(tm, LANES), _row_spec(tm, LANES), _row_spec(tm, LANES)]
    outs = [SDS((T, FRONT_W), BF16), SDS((Q_RANK, HW), F32), SDS((KV_RANK, HW), F32), SDS((1, Q_RANK), F32), SDS((1, KV_RANK), F32)]
    out_specs = [_row_spec(tm, FRONT_W)] + [_full_spec(s) for s in outs[1:]]
    return pl.pallas_call(
        body, name="mla_bwd", grid=(T // tm,), in_specs=in_specs, out_specs=out_specs, out_shape=outs,
        compiler_params=_cp(("arbitrary",)),
    )(*ins)


_DZ_COLS = ((GM, ZTOT), (GA, UP), (UP, GP), (GP, GM), (ZQ, GA))


def _in_proj_bwd_x(dzs, x2, dh, norm_in, w_in_pad, tm, slabs):
    T = x2.shape[0]
    steps = T // tm
    n = len(slabs)

    def body(d0, d1, d2, d3, d4, x_ref, dh_ref, nin_ref, win_ref, *rest):
        slab_refs, (gx_ref, gnin_ref), sum_refs = rest[:n], rest[n:n + 2], rest[n + 2:2 * n + 2]
        rs = _ReduceScatter(slab_refs, None, sum_refs, None, rest[2 * n + 2:])
        step = pl.program_id(0)

        @pl.when(step == 0)
        def _():
            gnin_ref[...] = jnp.zeros_like(gnin_ref)
            rs.start1()

        pl.when(step == min(2, steps - 1))(rs.finish1_start2)
        dhn = None
        for ref, (lo, hi) in zip((d0, d1, d2, d3, d4), _DZ_COLS):
            t = _mm(ref[...], win_ref[lo:hi, :])
            dhn = t if dhn is None else dhn + t
        dx, gnin = _rms_bwd(x_ref[...], nin_ref[...], dhn)
        gnin_ref[...] += gnin
        gx_ref[...] = dx + dh_ref[...]
        pl.when(step == steps - 1)(rs.finish2)

    in_specs = [_row_spec(tm, hi - lo) for lo, hi in _DZ_COLS] + [_row_spec(tm, D_MODEL), _row_spec(tm, D_MODEL),
                                                                  _full_spec(norm_in), _full_spec(w_in_pad)] + [HBM_SPEC] * n
    sums = [SDS(s.shape[1:], F32) for s in slabs]
    outs = [SDS((T, D_MODEL), F32), SDS((1, D_MODEL), F32)] + sums
    return pl.pallas_call(
        body, name="in_proj_bwd_x", grid=(steps,), in_specs=in_specs,
        out_specs=[_row_spec(tm, D_MODEL), _full_spec(outs[1])] + [_full_spec(s) for s in sums],
        out_shape=outs, scratch_shapes=_rs_scratch([s.shape for s in sums], None), compiler_params=_cp(("arbitrary",)),
    )(*dzs, x2, dh, norm_in, w_in_pad, *slabs)


SLAB_ROWS = IN_TOTAL // N_DEV


def _slab_segments(k):
    cuts = [(0, ZKR_ORIG, 0), (ZKR_ORIG, ZKR_ORIG + ROPE, NOPE), (ZKR_ORIG + ROPE, IN_TOTAL, LANES - ROPE)]
    lo, hi = k * SLAB_ROWS, (k + 1) * SLAB_ROWS
    return [(max(lo, a) - lo, max(lo, a) + shift, min(hi, b) - max(lo, a)) for a, b, shift in cuts if min(hi, b) > max(lo, a)]


def _in_proj_bwd_w(dzs, hn, tm):
    T = hn.shape[0]
    steps = T // tm

    def body(d0, d1, d2, d3, d4, hn_ref, slab_ref, acc_ref):
        @pl.when(pl.program_id(0) == 0)
        def _():
            acc_ref[...] = jnp.zeros_like(acc_ref)

        hn_v = hn_ref[...]
        for ref, (lo, hi) in zip((d0, d1, d2, d3, d4), _DZ_COLS):
            acc_ref[lo:hi, :] += _mm_tn(ref[...], hn_v)

        @pl.when(pl.program_id(0) == steps - 1)
        def _():
            for k in range(N_DEV):
                for at, src, rows in _slab_segments(k):
                    slab_ref[k, at:at + rows, :] = acc_ref[src:src + rows, :].astype(BF16)

    in_specs = [_row_spec(tm, hi - lo) for lo, hi in _DZ_COLS] + [_row_spec(tm, D_MODEL)]
    out = SDS((N_DEV, SLAB_ROWS, D_MODEL), BF16)
    return pl.pallas_call(
        body, name="in_proj_bwd_w", grid=(steps,), in_specs=in_specs, out_specs=_full_spec(out), out_shape=out,
        scratch_shapes=[pltpu.VMEM((ZTOT, D_MODEL), F32)], compiler_params=_cp(("arbitrary",)),
    )(*dzs, hn)


def _local_step(x2, tgt, norm_in, w_in_pad, q_norm, w_uq, kv_norm, w_ukv, pool_w, pool_scale, late_shards, norm_final):
    T = x2.shape[0]
    tm = min(512, T)
    tq = min(512, T)
    row = lambda v: v.reshape(1, -1)
    wuq_pad = jnp.pad(w_uq, ((0, 0), (0, 0), (0, HEAD_PAD - NOPE - ROPE))).reshape(Q_RANK, HW)
    wukv = w_ukv.reshape(KV_RANK, HW)
    rc, rsa, rsb = _rope_tables(T)

    hn, zgm, zga, zup, zgp, zfr, q_att, k_att, v_att, vt_att, w_ba, w_bp, w_out = _in_proj(
        x2, row(norm_in), w_in_pad, row(q_norm), wuq_pad, row(kv_norm), wukv, rc, rsa, rsb, tm, late_shards)
    w_out = w_out.reshape(D_MODEL, D_MODEL)
    o, lse_rows = _attn_fwd(q_att, k_att, vt_att, tq, 4)
    ypool = _pool_fwd(zup, zgp, pool_w, row(pool_scale))
    loss8, dh, dgm, doop, dga, dcap_rows, dyp, *slabs, g_nf = _tail(
        x2, tgt, o, zga, ypool, zgm, w_ba, w_bp, w_out, row(norm_final), min(256, T))
    dup, dgp, g_pool_w, g_pool_scale = _pool_bwd(zup, zgp, dyp, pool_w, row(pool_scale))

    bf = lambda a: a.astype(BF16)
    early = [g_pool_w, g_pool_scale, g_nf, loss8[0]]
    packed = jnp.concatenate([_pack_rows(a) for a in early], axis=0)
    dq_att, dkv_nat, dkr4, s_wout, s_wba, s_wbp, tot_early = _attn_bwd(
        q_att, k_att, v_att, doop, lse_rows, dcap_rows, tq, 2, slabs, packed)
    s_pool_w, s_pool_scale, s_nf, s_loss = _unpack_rows(tot_early, early)

    dfr, g_wuq_pad, g_wukv, g_qn, g_kvn = _mla_bwd(
        dq_att, dkv_nat, dkr4, zfr, row(q_norm), wuq_pad, row(kv_norm), wukv, rc, rsa, rsb, tm)
    dzs = (dgm, dga, dup, dgp, dfr)
    slabs = [_in_proj_bwd_w(dzs, hn, tm),
             bf(g_wuq_pad.reshape(N_DEV, Q_RANK // N_DEV, HEADS, HEAD_PAD)[..., :NOPE + ROPE]),
             bf(g_wukv).reshape(N_DEV, KV_RANK // N_DEV, HW)]
    grad_x, g_nin, s_win, s_wuq, s_wukv = _in_proj_bwd_x(dzs, x2, dh, row(norm_in), w_in_pad, min(256, T), slabs)
    late = [g_nin, g_qn, g_kvn]
    (tot_late,) = _reduce_scatter([], jnp.concatenate([_pack_rows(a) for a in late], axis=0))
    s_nin, s_qn, s_kvn = _unpack_rows(tot_late, late)

    grads = dict(norm_in=s_nin, w_in=s_win, q_norm=s_qn, w_uq=s_wuq, kv_norm=s_kvn, w_ukv=s_wukv, pool_w=s_pool_w.reshape(-1, GROUP),
                 pool_scale=s_pool_scale, w_branch_attn=s_wba, w_branch_pool=s_wbp, w_out=s_wout, norm_final=s_nf)
    return s_loss[0], grad_x, grads


MESH_ID = pl.DeviceIdType.MESH
VMEM_SPEC = pl.BlockSpec(memory_space=pltpu.VMEM)
HBM_SPEC = pl.BlockSpec(memory_space=pl.ANY)


def _mesh_pos():
    return lax.axis_index("x"), lax.axis_index("y"), lax.axis_index("c")


def _slot(px, py, pc):
    return 4 * px + 2 * py + pc


def _all_gather_bf16(shards):
    n = len(shards)

    def body(*refs):
        ins, outs = refs[:n], refs[n:2 * n]
        land0, scratch = refs[2 * n], refs[2 * n + 1:]
        wpad_ref = outs[0]
        ag = _AllGather(ins, (land0,) + tuple(outs[1:]), scratch)
        ag.start()
        ag.forward()
        ag.finish()
        wpad_ref[ZKR:GA, :] = jnp.zeros((GA - ZKR, D_MODEL), BF16)
        for k in range(N_DEV):
            for at, dst, rows in _slab_segments(k):
                wpad_ref[dst:dst + rows, :] = land0[k, at:at + rows, :]

    return pl.pallas_call(
        body, name="all_gather_weights",
        in_specs=[VMEM_SPEC] * n, out_specs=[VMEM_SPEC] + [HBM_SPEC] * (n - 1),
        out_shape=[SDS((ZTOT, D_MODEL), BF16)] + [SDS((N_DEV,) + s.shape, BF16) for s in shards[1:]],
        scratch_shapes=[pltpu.VMEM((N_DEV,) + shards[0].shape, BF16)] + _ag_scratch([s.shape for s in shards]),
        compiler_params=_cp(),
    )(*shards)


def _ag_scratch(shapes):
    n = len(shapes)
    dma = pltpu.SemaphoreType.DMA
    return [pltpu.VMEM(tuple(s), BF16) for s in shapes] + [dma((7 * n,)), dma((7 * n,)), dma((n,))]


class _AllGather:
    def __init__(self, in_refs, dest_refs, scratch):
        n = self.n = len(in_refs)
        self.ins, self.dests, self.stage = in_refs, dest_refs, scratch[:n]
        self.send_sems, self.recv_sems, self.local_sems = scratch[n:]
        x, y, c = _mesh_pos()
        self.c, self.me, self.sibling = c, (x, y, c), (x, y, 1 - c)
        self.chips = [(1 - x, y), (x, 1 - y), (1 - x, 1 - y)]

    def _copy(self, a, k, block, to, from_stage=False):
        dst = self.dests[a].at[_slot(*block)]
        return pltpu.make_async_remote_copy(
            src_ref=self.stage[a] if from_stage else dst, dst_ref=dst, send_sem=self.send_sems.at[7 * a + k],
            recv_sem=self.recv_sems.at[7 * a + k], device_id=to, device_id_type=MESH_ID)

    def _mine(self):
        return [pltpu.make_async_copy(self.stage[a], self.dests[a].at[_slot(*self.me)], self.local_sems.at[a]) for a in range(self.n)]

    def _first(self):
        cps = []
        for a in range(self.n):
            cps.append(self._copy(a, 0, self.me, self.sibling, True))
            cps += [self._copy(a, 1 + j, self.me, (*chip, self.c), True) for j, chip in enumerate(self.chips)]
        return cps

    def _passed(self):
        return [self._copy(a, 4 + j, (*chip, self.c), self.sibling) for a in range(self.n) for j, chip in enumerate(self.chips)]

    def start(self):
        for a in range(self.n):
            self.stage[a][...] = self.ins[a][...].astype(BF16)
        for cp in self._mine() + self._first():
            cp.start()

    def forward(self):
        passed = self._passed()
        for a in range(self.n):
            for j, chip in enumerate(self.chips):
                self._copy(a, 1 + j, (*chip, self.c), self.me).wait_recv()
                passed[3 * a + j].start()

    def finish(self):
        for a in range(self.n):
            self._copy(a, 0, self.sibling, self.me).wait_recv()
            for j, chip in enumerate(self.chips):
                self._copy(a, 4 + j, (*chip, 1 - self.c), self.me).wait_recv()
        for cp in self._first() + self._passed():
            cp.wait_send()
        for cp in self._mine():
            cp.wait()


N_CHIPS = 4


def _reduce_scatter(slabs, packed):
    def body(*refs):
        n = len(slabs)
        rs = _ReduceScatter(refs[:n], refs[n], refs[n + 1:2 * n + 1], refs[2 * n + 1], refs[2 * n + 2:])
        rs.start1()
        rs.finish1_start2()
        rs.finish2()

    shapes = [s.shape[1:] for s in slabs]
    return pl.pallas_call(
        body, name="reduce_scatter_grads",
        in_specs=[HBM_SPEC] * len(slabs) + [VMEM_SPEC], out_specs=[VMEM_SPEC] * (len(slabs) + 1),
        out_shape=[SDS(s, F32) for s in shapes] + [SDS(packed.shape, F32)],
        scratch_shapes=_rs_scratch(shapes, packed.shape), compiler_params=_cp(),
    )(*slabs, packed)


def _rs_scratch(shapes, packed_shape):
    n = len(shapes)
    n1, n2 = N_CHIPS * n + 1, (N_CHIPS - 1) * (n + 1)
    dma = pltpu.SemaphoreType.DMA
    packed = [] if packed_shape is None else [pltpu.VMEM(packed_shape, F32), pltpu.VMEM((N_CHIPS,) + tuple(packed_shape), F32)]
    return ([pltpu.VMEM((N_CHIPS,) + tuple(s), BF16) for s in shapes] * 2 + [pltpu.VMEM((N_CHIPS - 1,) + tuple(s), BF16) for s in shapes] * 2
            + packed + [dma((max(N_CHIPS * n, 1),)), dma((n1,)), dma((n1,)), dma((n2,)), dma((n2,))])


class _ReduceScatter:
    def __init__(self, slab_refs, packed_ref, out_refs, ptot_ref, scratch):
        n = self.n = len(slab_refs)
        self.slabs, self.packed, self.outs, self.ptot = slab_refs, packed_ref, out_refs, ptot_ref
        self.own1, self.land1, self.send2, self.land2 = (scratch[k * n:(k + 1) * n] for k in range(4))
        rest = scratch[4 * n:]
        if packed_ref is not None:
            self.pland1, self.pland2 = rest[:2]
            rest = rest[2:]
        self.loc_sems, self.send1_sems, self.recv1_sems, self.send2_sems, self.recv2_sems = rest
        self.x, self.y, self.c = _mesh_pos()

    def _chip(self, r):
        return (1 - self.x if r & 2 else self.x, 1 - self.y if r & 1 else self.y)

    @staticmethod
    def _remote(src, dst, send_sem, recv_sem, to):
        return pltpu.make_async_remote_copy(src_ref=src, dst_ref=dst, send_sem=send_sem, recv_sem=recv_sem, device_id=to,
                                            device_id_type=MESH_ID)

    def _copies1(self):
        c, sibling = self.c, (self.x, self.y, 1 - self.c)
        cps = []
        for a in range(self.n):
            for r in range(N_CHIPS):
                k = N_CHIPS * a + r
                cps.append(pltpu.make_async_copy(self.slabs[a].at[_slot(*self._chip(r), c)], self.own1[a].at[r], self.loc_sems.at[k]))
                cps.append(self._remote(self.slabs[a].at[_slot(*self._chip(r), 1 - c)], self.land1[a].at[r],
                                        self.send1_sems.at[k], self.recv1_sems.at[k], sibling))
        if self.packed is not None:
            k = N_CHIPS * self.n
            cps.append(self._remote(self.packed, self.pland1, self.send1_sems.at[k], self.recv1_sems.at[k], sibling))
        return cps

    def _copies2(self):
        cps = []
        for a in range(self.n):
            for r in range(1, N_CHIPS):
                k = (N_CHIPS - 1) * a + r - 1
                cps.append(self._remote(self.send2[a].at[r - 1], self.land2[a].at[r - 1], self.send2_sems.at[k], self.recv2_sems.at[k],
                                        (*self._chip(r), self.c)))
        if self.packed is not None:
            for r in range(1, N_CHIPS):
                k = (N_CHIPS - 1) * self.n + r - 1
                cps.append(self._remote(self.pland2.at[0], self.pland2.at[r], self.send2_sems.at[k], self.recv2_sems.at[k],
                                        (*self._chip(r), self.c)))
        return cps

    def start1(self):
        for cp in self._copies1():
            cp.start()

    def finish1_start2(self):
        for cp in self._copies1():
            cp.wait()
        for a in range(self.n):
            self.outs[a][...] = self.own1[a][0].astype(F32) + self.land1[a][0].astype(F32)
            for r in range(1, N_CHIPS):
                self.send2[a][r - 1] = (self.own1[a][r].astype(F32) + self.land1[a][r].astype(F32)).astype(BF16)
        if self.packed is not None:
            self.pland2[0] = self.packed[...] + self.pland1[...]
        for cp in self._copies2():
            cp.start()

    def finish2(self):
        for cp in self._copies2():
            cp.wait()
        for a in range(self.n):
            l2 = self.land2[a]
            self.outs[a][...] = self.outs[a][...] + ((l2[0].astype(F32) + l2[1].astype(F32)) + l2[2].astype(F32))
        if self.packed is not None:
            p2 = self.pland2
            self.ptot[...] = (p2[0] + p2[1]) + (p2[2] + p2[3])


def _adamw(ws, gs, ms, vs):
    n = len(ws)

    def body(*refs):
        for k in range(n):
            w, g, m, v = (refs[j * n + k][...] for j in range(4))
            d_ref, nm_ref, nv_ref = (refs[(4 + j) * n + k] for j in range(3))
            m = ADAM_B1 * m + (1.0 - ADAM_B1) * g
            v = ADAM_B2 * v + (1.0 - ADAM_B2) * jnp.square(g)
            m_hat = m / (1.0 - ADAM_B1 ** ADAM_STEP)
            v_hat = v / (1.0 - ADAM_B2 ** ADAM_STEP)
            d_ref[...] = -ADAM_LR * (m_hat / (jnp.sqrt(v_hat) + ADAM_EPS) + ADAM_WD * w)
            nm_ref[...] = m
            nv_ref[...] = v

    outs = pl.pallas_call(
        body, name="adamw", in_specs=[VMEM_SPEC] * (4 * n), out_specs=[VMEM_SPEC] * (3 * n),
        out_shape=[SDS(w.shape, F32) for w in ws] * 3, compiler_params=_cp(),
    )(*ws, *gs, *ms, *vs)
    return outs[:n], outs[n:2 * n], outs[2 * n:]


WEIGHTS = ("norm_in", "w_in", "q_norm", "w_uq", "kv_norm", "w_ukv", "pool_w", "pool_scale", "w_branch_attn", "w_branch_pool",
           "w_out", "norm_final")
SUBLANES = 8


def _cols_to_slabs(g):
    r = g.shape[0]
    return g.reshape(r, N_DEV, -1).transpose(1, 0, 2)


def _slabs_to_cols(s):
    return s.transpose(1, 0, 2).reshape(s.shape[1], -1)


def _pack_rows(a):
    a = a.reshape(-1, LANES)
    return jnp.pad(a, ((0, -a.shape[0] % SUBLANES), (0, 0)))


def _unpack_rows(packed, like):
    out, row = [], 0
    for a in like:
        rows = a.size // LANES
        out.append(packed[row:row + rows].reshape(a.shape))
        row += rows + (-rows % SUBLANES)
    return out


def kernel(x, norm_in, w_in, q_norm, w_uq, kv_norm, w_ukv, pool_w, pool_scale, w_branch_attn, w_branch_pool, w_out, norm_final, loss_target, m_norm_in, m_w_in, m_q_norm, m_w_uq, m_kv_norm, m_w_ukv, m_pool_w, m_pool_scale, m_w_branch_attn, m_w_branch_pool, m_w_out, m_norm_final, v_norm_in, v_w_in, v_q_norm, v_w_uq, v_kv_norm, v_w_ukv, v_pool_w, v_pool_scale, v_w_branch_attn, v_w_branch_pool, v_w_out, v_norm_final):
    w = dict(norm_in=norm_in, w_in=w_in, q_norm=q_norm, w_uq=w_uq, kv_norm=kv_norm, w_ukv=w_ukv, pool_w=pool_w, pool_scale=pool_scale,
             w_branch_attn=w_branch_attn, w_branch_pool=w_branch_pool, w_out=w_out, norm_final=norm_final)
    m = dict(norm_in=m_norm_in, w_in=m_w_in, q_norm=m_q_norm, w_uq=m_w_uq, kv_norm=m_kv_norm, w_ukv=m_w_ukv, pool_w=m_pool_w,
             pool_scale=m_pool_scale, w_branch_attn=m_w_branch_attn, w_branch_pool=m_w_branch_pool, w_out=m_w_out, norm_final=m_norm_final)
    v = dict(norm_in=v_norm_in, w_in=v_w_in, q_norm=v_q_norm, w_uq=v_w_uq, kv_norm=v_kv_norm, w_ukv=v_w_ukv, pool_w=v_pool_w,
             pool_scale=v_pool_scale, w_branch_attn=v_w_branch_attn, w_branch_pool=v_w_branch_pool, w_out=v_w_out, norm_final=v_norm_final)

    def as2d(name, a):
        if name == "w_in":
            return a.T
        if name == "w_uq":
            return a
        if name == "w_ukv":
            return a.reshape(a.shape[0], -1)
        if name == "pool_w":
            return a.reshape(-1, GROUP)
        return a.reshape(1, -1) if a.ndim == 1 else a

    def unshape(name, a):
        return a.T if name == "w_in" else a.reshape(w[name].shape)

    w_in_pad, w_uq_full, w_ukv_full = _all_gather_bf16([as2d(k, w[k]) for k in ("w_in", "w_uq", "w_ukv")])
    loss, grad_x, g2d = _local_step(
        x.reshape(x.shape[1:]), loss_target.reshape(x.shape[1:]), norm_in, w_in_pad, q_norm,
        w_uq_full.reshape(Q_RANK, HEADS, NOPE + ROPE), kv_norm, w_ukv_full.reshape(KV_RANK, HEADS, NOPE + VDIM),
        pool_w, pool_scale, [w_branch_attn, w_branch_pool, w_out], norm_final)

    deltas, new_m, new_v = _adamw([as2d(k, w[k]) for k in WEIGHTS], [g2d[k] for k in WEIGHTS],
                                  [as2d(k, m[k]) for k in WEIGHTS], [as2d(k, v[k]) for k in WEIGHTS])
    shaped = lambda arrs: [unshape(k, a) for k, a in zip(WEIGHTS, arrs)]
    return (loss, grad_x.reshape(x.shape), *shaped([g2d[k] for k in WEIGHTS]), *shaped(deltas), *shaped(new_m), *shaped(new_v))
```

```python
import functools

import jax
import jax.numpy as jnp
import numpy as np
from jax import lax
from jax.experimental import pallas as pl
from jax.experimental.pallas import tpu as pltpu

F32 = jnp.float32
BF16 = jnp.bfloat16
SDS = jax.ShapeDtypeStruct

D_MODEL = 1024
HEADS = 8
NOPE = 64
ROPE = 32
VDIM = 64
Q_RANK = 384
KV_RANK = 256
MLA_W = HEADS * VDIM
POOL_W = 512
POOL_GROUPS = 4
GROUP = POOL_W // POOL_GROUPS
CHUNK = 64
ROPE_THETA = 10000.0
EPS = 1e-6
SCALE = (NOPE + ROPE) ** -0.5
LOG2E = 1.4426950408889634
LN2 = 0.6931471805599453
QK_SCALE_LOG2 = SCALE * LOG2E
IN_TOTAL = 4256
ADAM_LR, ADAM_B1, ADAM_B2, ADAM_EPS, ADAM_WD, ADAM_STEP = 0.001, 0.9, 0.999, 1e-08, 0.01, 10

N_DEV = 8
LANES = 128
HEAD_PAD = LANES
HW = HEADS * HEAD_PAD

ZQ, ZKV, ZKR, GA, UP, GP, GM, ZTOT = 0, 384, 640, 768, 1280, 1792, 2304, 4352
FRONT_W = GA
ZKR_ORIG = 640

VMEM_LIMIT = 56 * 1024 * 1024


def _cp(sem=None, **kw):
    if sem is not None:
        kw["dimension_semantics"] = sem
    return pltpu.CompilerParams(vmem_limit_bytes=VMEM_LIMIT, **kw)


def _mm(a, b):
    return lax.dot_general(a, b, (((1,), (0,)), ((), ())), preferred_element_type=F32)


def _mm_nt(a, b):
    return lax.dot_general(a, b, (((1,), (1,)), ((), ())), preferred_element_type=F32)


def _mm_tn(a, b):
    return lax.dot_general(a, b, (((0,), (0,)), ((), ())), preferred_element_type=F32)


def _row_spec(tm, w):
    return pl.BlockSpec((tm, w), lambda i: (i, 0))


def _full_spec(a):
    nd = len(a.shape)
    return pl.BlockSpec(a.shape, lambda *_: (0,) * nd)


def _rope(v, c, sa, sb, sign):
    n = v.shape[-1]
    reps = n // LANES
    if reps > 1:
        c, sa, sb = (jnp.tile(t, (1, reps)) for t in (c, sa, sb))
    up = pltpu.roll(v, n - ROPE // 2, 1)
    dn = pltpu.roll(v, ROPE // 2, 1)
    return v * c + sign * (up * sa + dn * sb)


def _rope_tables(T):
    half = ROPE // 2
    inv_freq = np.float32(ROPE_THETA) ** (-np.arange(half, dtype=np.float32) / np.float32(half))
    ang = np.arange(T, dtype=np.float32)[:, None] * inv_freq[None, :].astype(np.float32)
    cos, sin = np.cos(ang.astype(np.float64)).astype(np.float32), np.sin(ang.astype(np.float64)).astype(np.float32)
    z16 = np.zeros((T, half), np.float32)
    z32 = np.zeros((T, LANES - NOPE - ROPE), np.float32)
    c = np.concatenate([np.ones((T, NOPE), np.float32), cos, cos, z32], axis=1)
    sa = np.concatenate([np.zeros((T, NOPE), np.float32), -sin, z16, z32], axis=1)
    sb = np.concatenate([np.zeros((T, NOPE), np.float32), z16, sin, z32], axis=1)
    return jnp.asarray(c), jnp.asarray(sa), jnp.asarray(sb)


def _silu_parts(g):
    sg = jax.nn.sigmoid(g)
    return g * sg, sg + g * sg * (1.0 - sg)


def _in_proj(x2, norm_in, w_in_pad, q_norm, wuq_pad, kv_norm, wukv, rc, rsa, rsb, tm, late_shards):
    T = x2.shape[0]
    steps = T // tm
    n = len(late_shards)

    def body(x_ref, nin_ref, win_ref, qn_ref, wuq_ref, kvn_ref, wukv_ref, c_ref, sa_ref, sb_ref, *rest):
        hn_ref, zgm_ref, zga_ref, zup_ref, zgp_ref, zfr_ref, q_ref, k_ref, v_ref, vt_ref = rest[n:n + 10]
        ag = _AllGather(rest[:n], rest[n + 10:2 * n + 10], rest[2 * n + 10:])
        step = pl.program_id(0)
        pl.when(step == 0)(ag.start)
        pl.when(step == min(5, steps - 1))(ag.forward)
        xf = x_ref[...]
        r = lax.rsqrt(jnp.mean(xf * xf, axis=-1, keepdims=True) + EPS)
        hn = (xf * r * nin_ref[...]).astype(BF16)
        hn_ref[...] = hn
        z = _mm_nt(hn, win_ref[...])
        zgm_ref[...] = z[:, GM:ZTOT]
        zga_ref[...] = z[:, GA:UP]
        zup_ref[...] = z[:, UP:GP]
        zgp_ref[...] = z[:, GP:GM]
        zfr_ref[...] = z[:, ZQ:GA]
        zq, zkv, zkr = z[:, ZQ:ZKV], z[:, ZKV:ZKR], z[:, ZKR:GA]
        c, sa, sb = c_ref[...], sa_ref[...], sb_ref[...]
        rq = lax.rsqrt(jnp.mean(zq * zq, axis=-1, keepdims=True) + EPS)
        cq = (zq * rq * qn_ref[...]).astype(BF16)
        q = _rope(_mm(cq, wuq_ref[...]), c, sa, sb, 1.0)
        q_ref[...] = (q * QK_SCALE_LOG2).astype(BF16)
        rkv = lax.rsqrt(jnp.mean(zkv * zkv, axis=-1, keepdims=True) + EPS)
        ckv = (zkv * rkv * kvn_ref[...]).astype(BF16)
        kv = _mm(ckv, wukv_ref[...])
        kr = _rope(zkr, c, sa, sb, 1.0)
        lane = lax.broadcasted_iota(jnp.int32, kv.shape, 1) % LANES
        k_ref[...] = jnp.where(lane < NOPE, kv, jnp.tile(kr, (1, HEADS))).astype(BF16)
        v = jnp.where(lane < NOPE, 1.0, kv).astype(BF16)
        v_ref[...] = v
        vt_ref[...] = v.T
        pl.when(step == steps - 1)(ag.finish)

    ins = (x2, norm_in, w_in_pad, q_norm, wuq_pad, kv_norm, wukv, rc, rsa, rsb)
    in_specs = [_row_spec(tm, D_MODEL), _full_spec(norm_in), _full_spec(w_in_pad), _full_spec(q_norm), _full_spec(wuq_pad),
                _full_spec(kv_norm), _full_spec(wukv), _row_spec(tm, LANES), _row_spec(tm, LANES), _row_spec(tm, LANES)]
    widths = [(D_MODEL, BF16), (ZTOT - GM, F32), (UP - GA, F32), (GP - UP, F32), (GM - GP, F32), (FRONT_W, F32),
              (HW, BF16), (HW, BF16), (HW, BF16)]
    return pl.pallas_call(
        body, name="in_proj", grid=(steps,), in_specs=in_specs + [_full_spec(s) for s in late_shards],
        out_specs=[_row_spec(tm, w) for w, _ in widths] + [pl.BlockSpec((HW, tm), lambda i: (0, i))] + [HBM_SPEC] * n,
        out_shape=[SDS((T, w), dt) for w, dt in widths] + [SDS((HW, T), BF16)]
        + [SDS((N_DEV,) + s.shape, BF16) for s in late_shards],
        scratch_shapes=_ag_scratch([s.shape for s in late_shards]), compiler_params=_cp(("arbitrary",)),
    )(*ins, *late_shards)


def _chunk_mask(n_q, n_k, q_off, transposed):
    shape = (n_k, n_q) if transposed else (n_q, n_k)
    q = (lax.broadcasted_iota(jnp.int32, shape, 1 if transposed else 0) + q_off) // CHUNK
    k = lax.broadcasted_iota(jnp.int32, shape, 0 if transposed else 1) // CHUNK
    return k <= q


def _store_pair_rows(ref, k, pair):
    t = pair.T
    ref[k, 0:1, :] = t[0:1, :]
    ref[k, 1:2, :] = t[VDIM:VDIM + 1, :]


def _attn_fwd(q_att, k_att, vt_att, tq, hps):
    T = q_att.shape[0]
    head_lanes = [slice(h * LANES, (h + 1) * LANES) for h in range(hps)]

    def body(q_ref, k_ref, vt_ref, o_ref, lser_ref):
        i = pl.program_id(1)
        mask = _chunk_mask(tq, tq, 0, True)
        lane = lax.broadcasted_iota(jnp.int32, (tq, LANES), 1)
        qs = [q_ref[:, hs] for hs in head_lanes]

        def step(j, carry, masked):
            off = pl.multiple_of(j * tq, tq)
            sts = [_mm_nt(k_ref[pl.ds(off, tq), hs], qh) for qh, hs in zip(qs, head_lanes)]
            if masked:
                sts = [jnp.where(mask, st, -jnp.inf) for st in sts]
            ms = [jnp.maximum(m, jnp.max(st, axis=0, keepdims=True)) for (m, _), st in zip(carry, sts)]
            pts = [jnp.exp2(st - m_new).astype(BF16) for st, m_new in zip(sts, ms)]
            return tuple((m_new, jnp.exp2(m - m_new) * acc + _mm(vt_ref[hs, pl.ds(off, tq)], pt))
                         for (m, acc), m_new, pt, hs in zip(carry, ms, pts, head_lanes))

        init = ((jnp.full((1, tq), -jnp.inf, F32), jnp.zeros((LANES, tq), F32)),) * hps
        res = step(i, lax.fori_loop(0, i, functools.partial(step, masked=False), init), True)
        for pair in range(hps // 2):
            (ma, acca), (mb, accb) = res[2 * pair], res[2 * pair + 1]
            la, lb = acca[:1], accb[:1]
            oa, ob = (acca / la).T, (accb / lb).T
            o_ref[:, pair * LANES:(pair + 1) * LANES] = jnp.where(lane < VDIM, pltpu.roll(oa, VDIM, 1), ob)
            lser_ref[pair, 0:1, :] = ma + jnp.log2(la)
            lser_ref[pair, 1:2, :] = mb + jnp.log2(lb)

    qspec = pl.BlockSpec((tq, hps * LANES), lambda p, i: (i, p))
    kspec = pl.BlockSpec((T, hps * LANES), lambda p, i: (0, p))
    vspec = pl.BlockSpec((hps * LANES, T), lambda p, i: (p, 0))
    ospec = pl.BlockSpec((tq, hps * VDIM), lambda p, i: (i, p))
    return pl.pallas_call(
        body, name="attn_fwd", grid=(HEADS // hps, T // tq), in_specs=[qspec, kspec, vspec],
        out_specs=[ospec, pl.BlockSpec((hps // 2, 2, tq), lambda p, i: (p, 0, i))],
        out_shape=[SDS((T, MLA_W), F32), SDS((HEADS // 2, 2, T), F32)],
        compiler_params=_cp(("parallel", "parallel")),
    )(q_att, k_att, vt_att)


def _pick(g, vals):
    out = vals[-1]
    for k in range(len(vals) - 2, -1, -1):
        out = jnp.where(g == k, vals[k], out)
    return out


def _window_sum(u, g, forward):
    T = u.shape[0]
    row = lax.broadcasted_iota(jnp.int32, u.shape, 0)

    def sh(s, k):
        if forward:
            return jnp.where(row >= k, pltpu.roll(s, k, 0), 0.0)
        return jnp.where(row < T - k, pltpu.roll(s, T - k, 0), 0.0)

    sums, s = [], u
    for k in (1, 2, 4, 8):
        s = s + sh(s, k)
        sums.append(s)
    return _pick(g, sums)


def _pool_count(shape, g):
    row = lax.broadcasted_iota(jnp.int32, shape, 0)
    return jnp.minimum(row + 1, lax.shift_left(jnp.int32(2), g)).astype(F32)


def _pool_fwd(zup, zgp, pool_w, pool_scale):
    T = zup.shape[0]

    def body(u_ref, g_ref, w_ref, sc_ref, y_ref):
        g = pl.program_id(0)
        u = u_ref[...]
        d = _window_sum(u, g, True) / _pool_count(u.shape, g) - u
        lin = _mm(d.astype(BF16), w_ref[0].astype(BF16))
        silu, _ = _silu_parts(g_ref[...])
        y_ref[...] = (lin * sc_ref[...] * silu).astype(BF16)

    col = pl.BlockSpec((T, GROUP), lambda g: (0, g))
    return pl.pallas_call(
        body, name="pool_fwd", grid=(POOL_GROUPS,),
        in_specs=[col, col, pl.BlockSpec((1, GROUP, GROUP), lambda g: (g, 0, 0)), pl.BlockSpec((1, GROUP), lambda g: (0, g))],
        out_specs=col, out_shape=SDS((T, POOL_W), BF16), compiler_params=_cp(("parallel",)),
    )(zup, zgp, pool_w, pool_scale)


def _pool_bwd(zup, zgp, dyp, pool_w, pool_scale):
    T = zup.shape[0]

    def body(u_ref, g_ref, dy_ref, w_ref, sc_ref, du_ref, dg_ref, gw_ref, gsc_ref):
        g = pl.program_id(0)
        u = u_ref[...]
        cnt = _pool_count(u.shape, g)
        d = (_window_sum(u, g, True) / cnt - u).astype(BF16)
        wb = w_ref[0].astype(BF16)
        lin = _mm(d, wb)
        sc = sc_ref[...]
        silu, dsilu = _silu_parts(g_ref[...])
        dy = dy_ref[...]
        dg_ref[...] = (dy * lin * sc * dsilu).astype(BF16)
        dpre = dy * silu
        gsc_ref[...] = jnp.sum(dpre * lin, axis=0, keepdims=True)
        dlin = (dpre * sc).astype(BF16)
        gw_ref[0] = _mm_tn(d, dlin)
        dd = _mm_nt(dlin, wb)
        du_ref[...] = (_window_sum(dd / cnt, g, False) - dd).astype(BF16)

    col = pl.BlockSpec((T, GROUP), lambda g: (0, g))
    wspec = pl.BlockSpec((1, GROUP, GROUP), lambda g: (g, 0, 0))
    vspec = pl.BlockSpec((1, GROUP), lambda g: (0, g))
    return pl.pallas_call(
        body, name="pool_bwd", grid=(POOL_GROUPS,), in_specs=[col, col, col, wspec, vspec], out_specs=[col, col, wspec, vspec],
        out_shape=[SDS((T, POOL_W), BF16), SDS((T, POOL_W), BF16), SDS((POOL_GROUPS, GROUP, GROUP), F32), SDS((1, POOL_W), F32)],
        compiler_params=_cp(("parallel",)),
    )(zup, zgp, dyp, pool_w, pool_scale)


def _tail(x2, tgt, o, zga, ypool, zgm, wba, wbp, wout, norm_final, tm):
    T = x2.shape[0]
    steps = T // tm
    cols = D_MODEL // N_DEV

    def body(x_ref, tgt_ref, o_ref, zga_ref, yp_ref, zgm_ref, wba_ref, wbp_ref, wout_ref, nf_ref,
             loss_ref, dh_ref, dgm_ref, doop_ref, dga_ref, dcapr_ref, dyp_ref, swout_ref, swba_ref, swbp_ref, gnf_ref,
             gwout_ref, gwba_ref, gwbp_ref):
        @pl.when(pl.program_id(0) == 0)
        def _():
            for ref in (loss_ref, gwout_ref, gwba_ref, gwbp_ref, gnf_ref):
                ref[...] = jnp.zeros_like(ref)

        o_v = o_ref[...]
        silu, dsilu = _silu_parts(zga_ref[...])
        ya = (o_v * silu).astype(BF16)
        yp = yp_ref[...]
        wba_v = jnp.concatenate([wba_ref[k] for k in range(N_DEV)], axis=1)
        wbp_v = jnp.concatenate([wbp_ref[k] for k in range(N_DEV)], axis=1)
        wout_v = wout_ref[...]
        a = _mm(ya, wba_v)
        p = _mm(yp, wbp_v)
        gate = jax.nn.sigmoid(zgm_ref[...])
        ga, gp = gate[:, :D_MODEL], gate[:, D_MODEL:]
        mg = (ga * a + gp * p).astype(BF16)
        h = x_ref[...] + _mm(mg, wout_v)
        r = lax.rsqrt(jnp.mean(h * h, axis=-1, keepdims=True) + EPS)
        gf = nf_ref[...]
        hr = h * r
        e = hr * gf - tgt_ref[...]
        loss_ref[...] += (0.5 / D_MODEL) * jnp.sum(e * e)
        dy = e * (1.0 / D_MODEL)
        gnf_ref[...] += jnp.sum(dy * hr, axis=0, keepdims=True)
        u = dy * gf
        dh = r * (u - hr * jnp.mean(u * hr, axis=-1, keepdims=True))
        dh_ref[...] = dh
        dhb = dh.astype(BF16)
        dmg = _mm_nt(dhb, wout_v)
        gwout_ref[...] += _mm_tn(mg, dhb)
        dgm_ref[:, :D_MODEL] = (dmg * a * ga * (1.0 - ga)).astype(BF16)
        dgm_ref[:, D_MODEL:] = (dmg * p * gp * (1.0 - gp)).astype(BF16)
        dab = (dmg * ga).astype(BF16)
        dpb = (dmg * gp).astype(BF16)
        dya = _mm_nt(dab, wba_v)
        gwba_ref[...] += _mm_tn(ya, dab)
        dyp_ref[...] = _mm_nt(dpb, wbp_v)
        gwbp_ref[...] += _mm_tn(yp, dpb)
        do = dya * silu
        dga_ref[...] = (dya * o_v * dsilu).astype(BF16)
        prod = do * o_v
        lo = lax.broadcasted_iota(jnp.int32, (tm, LANES), 1) < VDIM
        for pair in range(HEADS // 2):
            ls = slice(pair * LANES, (pair + 1) * LANES)
            do_p, prod_p = do[:, ls], prod[:, ls]
            dcap_a = jnp.sum(jnp.where(lo, prod_p, 0.0), axis=-1, keepdims=True)
            dcap_b = jnp.sum(jnp.where(lo, 0.0, prod_p), axis=-1, keepdims=True)
            _store_pair_rows(dcapr_ref, pair, jnp.where(lo, dcap_a, dcap_b))
            doop_ref[:, 2 * pair * LANES:(2 * pair + 1) * LANES] = jnp.where(lo, 0.0, pltpu.roll(do_p, VDIM, 1)).astype(BF16)
            doop_ref[:, (2 * pair + 1) * LANES:(2 * pair + 2) * LANES] = jnp.where(lo, 0.0, do_p).astype(BF16)

        @pl.when(pl.program_id(0) == steps - 1)
        def _():
            for k in range(N_DEV):
                swout_ref[k] = gwout_ref[k * cols:(k + 1) * cols, :].astype(BF16)
                swba_ref[k] = gwba_ref[:, k * cols:(k + 1) * cols].astype(BF16)
                swbp_ref[k] = gwbp_ref[:, k * cols:(k + 1) * cols].astype(BF16)

    ins = (x2, tgt, o, zga, ypool, zgm, wba, wbp, wout, norm_final)
    in_specs = [_row_spec(tm, D_MODEL), _row_spec(tm, D_MODEL), _row_spec(tm, MLA_W), _row_spec(tm, MLA_W), _row_spec(tm, POOL_W),
                _row_spec(tm, 2 * D_MODEL), _full_spec(wba), _full_spec(wbp), _full_spec(wout), _full_spec(norm_final)]
    outs = [SDS((8, LANES), F32), SDS((T, D_MODEL), F32), SDS((T, 2 * D_MODEL), BF16), SDS((T, HW), BF16), SDS((T, MLA_W), BF16),
            SDS((HEADS // 2, 2, T), F32), SDS((T, POOL_W), F32),
            SDS((N_DEV, cols, D_MODEL), BF16), SDS((N_DEV, MLA_W, cols), BF16), SDS((N_DEV, POOL_W, cols), BF16), SDS((1, D_MODEL), F32)]
    out_specs = [_full_spec(outs[0]), _row_spec(tm, D_MODEL), _row_spec(tm, 2 * D_MODEL), _row_spec(tm, HW), _row_spec(tm, MLA_W),
                 pl.BlockSpec((HEADS // 2, 2, tm), lambda i: (0, 0, i)), _row_spec(tm, POOL_W),
                 _full_spec(outs[7]), _full_spec(outs[8]), _full_spec(outs[9]), _full_spec(outs[10])]
    return pl.pallas_call(
        body, name="tail", grid=(steps,), in_specs=in_specs, out_specs=out_specs, out_shape=outs,
        scratch_shapes=[pltpu.VMEM((D_MODEL, D_MODEL), F32), pltpu.VMEM((MLA_W, D_MODEL), F32), pltpu.VMEM((POOL_W, D_MODEL), F32)],
        compiler_params=_cp(("arbitrary",)),
    )(*ins)


def _attn_bwd(q_att, k_att, v_att, doop, lse_rows, dcap_rows, tq, hps, slabs, packed):
    T = q_att.shape[0]
    nq = T // tq
    n = len(slabs)
    groups = HEADS // hps
    head_lanes = [slice(h * LANES, (h + 1) * LANES) for h in range(hps)]

    def body(q_ref, k_ref, v_ref, doop_ref, lse_ref, dcap_ref, *rest):
        slab_refs, packed_ref = rest[:n], rest[n]
        dq_ref, dkv_ref, dkr_ref = rest[n + 1:n + 4]
        sum_refs, ptot_ref = rest[n + 4:2 * n + 4], rest[2 * n + 4]
        dq_acc = rest[2 * n + 5]
        rs = _ReduceScatter(slab_refs, packed_ref, sum_refs, ptot_ref, rest[2 * n + 6:])
        group, j = pl.program_id(0), pl.program_id(1)
        pl.when((group == 0) & (j == 0))(rs.start1)
        pl.when((group == 1) & (j == 0))(rs.finish1_start2)
        mask = _chunk_mask(tq, tq, 0, True)
        lane = lax.broadcasted_iota(jnp.int32, (tq, LANES), 1)
        ks = [k_ref[:, hs] for hs in head_lanes]
        vs = [v_ref[:, hs] for hs in head_lanes]
        kts = [kh.T for kh in ks]

        @pl.when(j == 0)
        def _():
            dq_acc[...] = jnp.zeros_like(dq_acc)

        def step(i, carry, masked):
            rows = pl.ds(pl.multiple_of(i * tq, tq), tq)
            heads = range(hps)
            stat = lambda h: (h // 2, slice(h % 2, h % 2 + 1), rows)
            qhs = [q_ref[rows, hs] for hs in head_lanes]
            doops = [doop_ref[rows, hs] for hs in head_lanes]
            sts = [_mm_nt(ks[h], qhs[h]) for h in heads]
            dpts = [_mm_nt(vs[h], doops[h]) for h in heads]
            pts = [jnp.exp2(sts[h] - lse_ref[stat(h)]) for h in heads]
            if masked:
                pts = [jnp.where(mask, pt, 0.0) for pt in pts]
            dsts = [(pts[h] * (dpts[h] - dcap_ref[stat(h)])).astype(BF16) for h in heads]
            dvs = [_mm(pts[h].astype(BF16), doops[h]) for h in heads]
            dks = [_mm(dsts[h], qhs[h]) for h in heads]
            for h, hs in enumerate(head_lanes):
                dq_acc[hs, rows] += _mm(kts[h], dsts[h])
            return tuple((dk + dks[h], dv + dvs[h]) for h, (dk, dv) in enumerate(carry))

        zero = jnp.zeros((tq, LANES), F32)
        carry = step(j, ((zero, zero),) * hps, True)
        res = lax.fori_loop(j + 1, nq, functools.partial(step, masked=False), carry)
        dkr = None
        for (dk, dv), hs in zip(res, head_lanes):
            dk = dk * LN2
            dkv_ref[:, hs] = jnp.where(lane < NOPE, dk, dv).astype(BF16)
            dkr = dk if dkr is None else dkr + dk
        dkr_ref[0] = jnp.where((lane >= NOPE) & (lane < NOPE + ROPE), dkr, 0.0)

        @pl.when(j == nq - 1)
        def _():
            dq_ref[...] = (dq_acc[...] * SCALE).T.astype(BF16)

        pl.when((group == groups - 1) & (j == nq - 1))(rs.finish2)

    kspec = pl.BlockSpec((tq, hps * LANES), lambda p, j: (j, p))
    qspec = pl.BlockSpec((T, hps * LANES), lambda p, j: (0, p))
    rspec = pl.BlockSpec((hps // 2, 2, T), lambda p, j: (p, 0, 0))
    sums = [SDS(s.shape[1:], F32) for s in slabs] + [SDS(packed.shape, F32)]
    return pl.pallas_call(
        body, name="attn_bwd", grid=(groups, nq),
        in_specs=[qspec, kspec, kspec, qspec, rspec, rspec] + [HBM_SPEC] * n + [_full_spec(packed)],
        out_specs=[qspec, kspec, pl.BlockSpec((1, tq, LANES), lambda p, j: (p, j, 0))] + [_full_spec(s) for s in sums],
        out_shape=[SDS((T, HW), BF16), SDS((T, HW), BF16), SDS((groups, T, LANES), F32)] + sums,
        scratch_shapes=[pltpu.VMEM((hps * LANES, T), F32)] + _rs_scratch([s.shape for s in sums[:-1]], packed.shape),
        compiler_params=_cp(("arbitrary", "arbitrary")),
    )(q_att, k_att, v_att, doop, lse_rows, dcap_rows, *slabs, packed)


def _rms_bwd(z, gain, dout):
    r = lax.rsqrt(jnp.mean(z * z, axis=-1, keepdims=True) + EPS)
    zr = z * r
    u = dout * gain
    return r * (u - zr * jnp.mean(u * zr, axis=-1, keepdims=True)), jnp.sum(dout * zr, axis=0, keepdims=True)


def _mla_bwd(dq_att, dkv_nat, dkr4, zfr, q_norm, wuq_pad, kv_norm, wukv, rc, rsa, rsb, tm):
    T = dq_att.shape[0]

    def body(dq_ref, dkv_ref, dkr_ref, zfr_ref, qn_ref, wuq_ref, kvn_ref, wukv_ref, c_ref, sa_ref, sb_ref,
             dfr_ref, gwuq_ref, gwukv_ref, gqn_ref, gkvn_ref):
        @pl.when(pl.program_id(0) == 0)
        def _():
            for ref in (gwuq_ref, gwukv_ref, gqn_ref, gkvn_ref):
                ref[...] = jnp.zeros_like(ref)

        c, sa, sb = c_ref[...], sa_ref[...], sb_ref[...]
        zq, zkv = zfr_ref[:, :Q_RANK], zfr_ref[:, Q_RANK:Q_RANK + KV_RANK]
        qn, kvn = qn_ref[...], kvn_ref[...]
        cq = (zq * lax.rsqrt(jnp.mean(zq * zq, axis=-1, keepdims=True) + EPS) * qn).astype(BF16)
        ckv = (zkv * lax.rsqrt(jnp.mean(zkv * zkv, axis=-1, keepdims=True) + EPS) * kvn).astype(BF16)
        dq = _rope(dq_ref[...].astype(F32), c, sa, sb, -1.0).astype(BF16)
        gwuq_ref[...] += _mm_tn(cq, dq)
        dzq, gqn = _rms_bwd(zq, qn, _mm_nt(dq, wuq_ref[...]))
        gqn_ref[...] += gqn
        dkv = dkv_ref[...]
        gwukv_ref[...] += _mm_tn(ckv, dkv)
        dzkv, gkvn = _rms_bwd(zkv, kvn, _mm_nt(dkv, wukv_ref[...]))
        gkvn_ref[...] += gkvn
        dkr = functools.reduce(lambda a, b: a + b, [dkr_ref[g] for g in range(dkr4.shape[0])])
        dfr_ref[:, :Q_RANK] = dzq.astype(BF16)
        dfr_ref[:, Q_RANK:Q_RANK + KV_RANK] = dzkv.astype(BF16)
        dfr_ref[:, Q_RANK + KV_RANK:] = _rope(dkr, c, sa, sb, -1.0).astype(BF16)

    ins = (dq_att, dkv_nat, dkr4, zfr, q_norm, wuq_pad, kv_norm, wukv, rc, rsa, rsb)
    in_specs = [_row_spec(tm, HW), _row_spec(tm, HW), pl.BlockSpec((dkr4.shape[0], tm, LANES), lambda i: (0, i, 0)), _row_spec(tm, FRONT_W),
                _full_spec(q_norm), _full_spec(wuq_pad), _full_spec(kv_norm), _full_spec(wukv),
                _row_spec(tm, LANES), _row_spec(tm, LANES), _row_spec(tm, LANES)]
    outs = [SDS((T, FRONT_W), BF16), SDS((Q_RANK, HW), F32), SDS((KV_RANK, HW), F32), SDS((1, Q_RANK), F32), SDS((1, KV_RANK), F32)]
    out_specs = [_row_spec(tm, FRONT_W)] + [_full_spec(s) for s in outs[1:]]
    return pl.pallas_call(
        body, name="mla_bwd", grid=(T // tm,), in_specs=in_specs, out_specs=out_specs, out_shape=outs,
        compiler_params=_cp(("arbitrary",)),
    )(*ins)


_DZ_COLS = ((GM, ZTOT), (GA, UP), (UP, GP), (GP, GM), (ZQ, GA))


def _in_proj_bwd_x(dzs, x2, dh, norm_in, w_in_pad, tm, slabs):
    T = x2.shape[0]
    steps = T // tm
    n = len(slabs)

    def body(d0, d1, d2, d3, d4, x_ref, dh_ref, nin_ref, win_ref, *rest):
        slab_refs, (gx_ref, gnin_ref), sum_refs = rest[:n], rest[n:n + 2], rest[n + 2:2 * n + 2]
        rs = _ReduceScatter(slab_refs, None, sum_refs, None, rest[2 * n + 2:])
        step = pl.program_id(0)

        @pl.when(step == 0)
        def _():
            gnin_ref[...] = jnp.zeros_like(gnin_ref)
            rs.start1()

        pl.when(step == min(2, steps - 1))(rs.finish1_start2)
        dhn = None
        for ref, (lo, hi) in zip((d0, d1, d2, d3, d4), _DZ_COLS):
            t = _mm(ref[...], win_ref[lo:hi, :])
            dhn = t if dhn is None else dhn + t
        dx, gnin = _rms_bwd(x_ref[...], nin_ref[...], dhn)
        gnin_ref[...] += gnin
        gx_ref[...] = dx + dh_ref[...]
        pl.when(step == steps - 1)(rs.finish2)

    in_specs = [_row_spec(tm, hi - lo) for lo, hi in _DZ_COLS] + [_row_spec(tm, D_MODEL), _row_spec(tm, D_MODEL),
                                                                  _full_spec(norm_in), _full_spec(w_in_pad)] + [HBM_SPEC] * n
    sums = [SDS(s.shape[1:], F32) for s in slabs]
    outs = [SDS((T, D_MODEL), F32), SDS((1, D_MODEL), F32)] + sums
    return pl.pallas_call(
        body, name="in_proj_bwd_x", grid=(steps,), in_specs=in_specs,
        out_specs=[_row_spec(tm, D_MODEL), _full_spec(outs[1])] + [_full_spec(s) for s in sums],
        out_shape=outs, scratch_shapes=_rs_scratch([s.shape for s in sums], None), compiler_params=_cp(("arbitrary",)),
    )(*dzs, x2, dh, norm_in, w_in_pad, *slabs)


SLAB_ROWS = IN_TOTAL // N_DEV


def _slab_segments(k):
    cuts = [(0, ZKR_ORIG, 0), (ZKR_ORIG, ZKR_ORIG + ROPE, NOPE), (ZKR_ORIG + ROPE, IN_TOTAL, LANES - ROPE)]
    lo, hi = k * SLAB_ROWS, (k + 1) * SLAB_ROWS
    return [(max(lo, a) - lo, max(lo, a) + shift, min(hi, b) - max(lo, a)) for a, b, shift in cuts if min(hi, b) > max(lo, a)]


def _in_proj_bwd_w(dzs, hn, tm, slabs):
    T = hn.shape[0]
    steps = T // tm
    n = len(slabs)

    def body(d0, d1, d2, d3, d4, hn_ref, *rest):
        slab_ref, acc_ref = rest[n], rest[2 * n + 1]
        rs = _ReduceScatter(rest[:n], None, rest[n + 1:2 * n + 1], None, rest[2 * n + 2:])
        step = pl.program_id(0)

        @pl.when(step == 0)
        def _():
            acc_ref[...] = jnp.zeros_like(acc_ref)
            rs.start1()

        pl.when(step == min(2, steps - 1))(rs.finish1_start2)
        hn_v = hn_ref[...]
        for ref, (lo, hi) in zip((d0, d1, d2, d3, d4), _DZ_COLS):
            acc_ref[lo:hi, :] += _mm_tn(ref[...], hn_v)

        @pl.when(step == steps - 1)
        def _():
            for k in range(N_DEV):
                for at, src, rows in _slab_segments(k):
                    slab_ref[k, at:at + rows, :] = acc_ref[src:src + rows, :].astype(BF16)
            rs.finish2()

    in_specs = [_row_spec(tm, hi - lo) for lo, hi in _DZ_COLS] + [_row_spec(tm, D_MODEL)] + [HBM_SPEC] * n
    sums = [SDS(s.shape[1:], F32) for s in slabs]
    outs = [SDS((N_DEV, SLAB_ROWS, D_MODEL), BF16)] + sums
    return pl.pallas_call(
        body, name="in_proj_bwd_w", grid=(steps,), in_specs=in_specs, out_specs=[_full_spec(s) for s in outs], out_shape=outs,
        scratch_shapes=[pltpu.VMEM((ZTOT, D_MODEL), F32)] + _rs_scratch([s.shape for s in sums], None),
        compiler_params=_cp(("arbitrary",)),
    )(*dzs, hn, *slabs)


def _local_step(x2, tgt, norm_in, w_in_pad, q_norm, w_uq, kv_norm, w_ukv, pool_w, pool_scale, late_shards, norm_final):
    T = x2.shape[0]
    tm = min(512, T)
    tq = min(512, T)
    row = lambda v: v.reshape(1, -1)
    wuq_pad = jnp.pad(w_uq, ((0, 0), (0, 0), (0, HEAD_PAD - NOPE - ROPE))).reshape(Q_RANK, HW)
    wukv = w_ukv.reshape(KV_RANK, HW)
    rc, rsa, rsb = _rope_tables(T)

    hn, zgm, zga, zup, zgp, zfr, q_att, k_att, v_att, vt_att, w_ba, w_bp, w_out = _in_proj(
        x2, row(norm_in), w_in_pad, row(q_norm), wuq_pad, row(kv_norm), wukv, rc, rsa, rsb, tm, late_shards)
    w_out = w_out.reshape(D_MODEL, D_MODEL)
    o, lse_rows = _attn_fwd(q_att, k_att, vt_att, tq, 4)
    ypool = _pool_fwd(zup, zgp, pool_w, row(pool_scale))
    loss8, dh, dgm, doop, dga, dcap_rows, dyp, *slabs, g_nf = _tail(
        x2, tgt, o, zga, ypool, zgm, w_ba, w_bp, w_out, row(norm_final), min(256, T))
    dup, dgp, g_pool_w, g_pool_scale = _pool_bwd(zup, zgp, dyp, pool_w, row(pool_scale))

    bf = lambda a: a.astype(BF16)
    early = [g_pool_w, g_pool_scale, g_nf, loss8[0]]
    packed = jnp.concatenate([_pack_rows(a) for a in early], axis=0)
    dq_att, dkv_nat, dkr4, s_wout, s_wba, s_wbp, tot_early = _attn_bwd(
        q_att, k_att, v_att, doop, lse_rows, dcap_rows, tq, 2, slabs, packed)
    s_pool_w, s_pool_scale, s_nf, s_loss = _unpack_rows(tot_early, early)

    dfr, g_wuq_pad, g_wukv, g_qn, g_kvn = _mla_bwd(
        dq_att, dkv_nat, dkr4, zfr, row(q_norm), wuq_pad, row(kv_norm), wukv, rc, rsa, rsb, tm)
    dzs = (dgm, dga, dup, dgp, dfr)
    slabs = [bf(g_wuq_pad.reshape(N_DEV, Q_RANK // N_DEV, HEADS, HEAD_PAD)[..., :NOPE + ROPE]),
             bf(g_wukv).reshape(N_DEV, KV_RANK // N_DEV, HW)]
    win_slabs, s_wuq, s_wukv = _in_proj_bwd_w(dzs, hn, tm, slabs)
    grad_x, g_nin, s_win = _in_proj_bwd_x(dzs, x2, dh, row(norm_in), w_in_pad, min(256, T), [win_slabs])
    late = [g_nin, g_qn, g_kvn]
    (tot_late,) = _reduce_scatter([], jnp.concatenate([_pack_rows(a) for a in late], axis=0))
    s_nin, s_qn, s_kvn = _unpack_rows(tot_late, late)

    grads = dict(norm_in=s_nin, w_in=s_win, q_norm=s_qn, w_uq=s_wuq, kv_norm=s_kvn, w_ukv=s_wukv, pool_w=s_pool_w.reshape(-1, GROUP),
                 pool_scale=s_pool_scale, w_branch_attn=s_wba, w_branch_pool=s_wbp, w_out=s_wout, norm_final=s_nf)
    return s_loss[0], grad_x, grads


MESH_ID = pl.DeviceIdType.MESH
VMEM_SPEC = pl.BlockSpec(memory_space=pltpu.VMEM)
HBM_SPEC = pl.BlockSpec(memory_space=pl.ANY)


def _mesh_pos():
    return lax.axis_index("x"), lax.axis_index("y"), lax.axis_index("c")


def _slot(px, py, pc):
    return 4 * px + 2 * py + pc


def _all_gather_bf16(shards):
    n = len(shards)

    def body(*refs):
        ins, outs = refs[:n], refs[n:2 * n]
        land0, scratch = refs[2 * n], refs[2 * n + 1:]
        wpad_ref = outs[0]
        ag = _AllGather(ins, (land0,) + tuple(outs[1:]), scratch)
        ag.start()
        ag.forward()
        ag.finish()
        wpad_ref[ZKR:GA, :] = jnp.zeros((GA - ZKR, D_MODEL), BF16)
        for k in range(N_DEV):
            for at, dst, rows in _slab_segments(k):
                wpad_ref[dst:dst + rows, :] = land0[k, at:at + rows, :]

    return pl.pallas_call(
        body, name="all_gather_weights",
        in_specs=[VMEM_SPEC] * n, out_specs=[VMEM_SPEC] + [HBM_SPEC] * (n - 1),
        out_shape=[SDS((ZTOT, D_MODEL), BF16)] + [SDS((N_DEV,) + s.shape, BF16) for s in shards[1:]],
        scratch_shapes=[pltpu.VMEM((N_DEV,) + shards[0].shape, BF16)] + _ag_scratch([s.shape for s in shards]),
        compiler_params=_cp(),
    )(*shards)


def _ag_scratch(shapes):
    n = len(shapes)
    dma = pltpu.SemaphoreType.DMA
    return [pltpu.VMEM(tuple(s), BF16) for s in shapes] + [dma((7 * n,)), dma((7 * n,)), dma((n,))]


class _AllGather:
    def __init__(self, in_refs, dest_refs, scratch):
        n = self.n = len(in_refs)
        self.ins, self.dests, self.stage = in_refs, dest_refs, scratch[:n]
        self.send_sems, self.recv_sems, self.local_sems = scratch[n:]
        x, y, c = _mesh_pos()
        self.c, self.me, self.sibling = c, (x, y, c), (x, y, 1 - c)
        self.chips = [(1 - x, y), (x, 1 - y), (1 - x, 1 - y)]

    def _copy(self, a, k, block, to, from_stage=False):
        dst = self.dests[a].at[_slot(*block)]
        return pltpu.make_async_remote_copy(
            src_ref=self.stage[a] if from_stage else dst, dst_ref=dst, send_sem=self.send_sems.at[7 * a + k],
            recv_sem=self.recv_sems.at[7 * a + k], device_id=to, device_id_type=MESH_ID)

    def _mine(self):
        return [pltpu.make_async_copy(self.stage[a], self.dests[a].at[_slot(*self.me)], self.local_sems.at[a]) for a in range(self.n)]

    def _first(self):
        cps = []
        for a in range(self.n):
            cps.append(self._copy(a, 0, self.me, self.sibling, True))
            cps += [self._copy(a, 1 + j, self.me, (*chip, self.c), True) for j, chip in enumerate(self.chips)]
        return cps

    def _passed(self):
        return [self._copy(a, 4 + j, (*chip, self.c), self.sibling) for a in range(self.n) for j, chip in enumerate(self.chips)]

    def start(self):
        for a in range(self.n):
            self.stage[a][...] = self.ins[a][...].astype(BF16)
        for cp in self._mine() + self._first():
            cp.start()

    def forward(self):
        passed = self._passed()
        for a in range(self.n):
            for j, chip in enumerate(self.chips):
                self._copy(a, 1 + j, (*chip, self.c), self.me).wait_recv()
                passed[3 * a + j].start()

    def finish(self):
        for a in range(self.n):
            self._copy(a, 0, self.sibling, self.me).wait_recv()
            for j, chip in enumerate(self.chips):
                self._copy(a, 4 + j, (*chip, 1 - self.c), self.me).wait_recv()
        for cp in self._first() + self._passed():
            cp.wait_send()
        for cp in self._mine():
            cp.wait()


N_CHIPS = 4


def _reduce_scatter(slabs, packed):
    def body(*refs):
        n = len(slabs)
        rs = _ReduceScatter(refs[:n], refs[n], refs[n + 1:2 * n + 1], refs[2 * n + 1], refs[2 * n + 2:])
        rs.start1()
        rs.finish1_start2()
        rs.finish2()

    shapes = [s.shape[1:] for s in slabs]
    return pl.pallas_call(
        body, name="reduce_scatter_grads",
        in_specs=[HBM_SPEC] * len(slabs) + [VMEM_SPEC], out_specs=[VMEM_SPEC] * (len(slabs) + 1),
        out_shape=[SDS(s, F32) for s in shapes] + [SDS(packed.shape, F32)],
        scratch_shapes=_rs_scratch(shapes, packed.shape), compiler_params=_cp(),
    )(*slabs, packed)


def _rs_scratch(shapes, packed_shape):
    n = len(shapes)
    n1, n2 = N_CHIPS * n + 1, (N_CHIPS - 1) * (n + 1)
    dma = pltpu.SemaphoreType.DMA
    packed = [] if packed_shape is None else [pltpu.VMEM(packed_shape, F32), pltpu.VMEM((N_CHIPS,) + tuple(packed_shape), F32)]
    return ([pltpu.VMEM((N_CHIPS,) + tuple(s), BF16) for s in shapes] * 2 + [pltpu.VMEM((N_CHIPS - 1,) + tuple(s), BF16) for s in shapes] * 2
            + packed + [dma((max(N_CHIPS * n, 1),)), dma((n1,)), dma((n1,)), dma((n2,)), dma((n2,))])


class _ReduceScatter:
    def __init__(self, slab_refs, packed_ref, out_refs, ptot_ref, scratch):
        n = self.n = len(slab_refs)
        self.slabs, self.packed, self.outs, self.ptot = slab_refs, packed_ref, out_refs, ptot_ref
        self.own1, self.land1, self.send2, self.land2 = (scratch[k * n:(k + 1) * n] for k in range(4))
        rest = scratch[4 * n:]
        if packed_ref is not None:
            self.pland1, self.pland2 = rest[:2]
            rest = rest[2:]
        self.loc_sems, self.send1_sems, self.recv1_sems, self.send2_sems, self.recv2_sems = rest
        self.x, self.y, self.c = _mesh_pos()

    def _chip(self, r):
        return (1 - self.x if r & 2 else self.x, 1 - self.y if r & 1 else self.y)

    @staticmethod
    def _remote(src, dst, send_sem, recv_sem, to):
        return pltpu.make_async_remote_copy(src_ref=src, dst_ref=dst, send_sem=send_sem, recv_sem=recv_sem, device_id=to,
                                            device_id_type=MESH_ID)

    def _copies1(self):
        c, sibling = self.c, (self.x, self.y, 1 - self.c)
        cps = []
        for a in range(self.n):
            for r in range(N_CHIPS):
                k = N_CHIPS * a + r
                cps.append(pltpu.make_async_copy(self.slabs[a].at[_slot(*self._chip(r), c)], self.own1[a].at[r], self.loc_sems.at[k]))
                cps.append(self._remote(self.slabs[a].at[_slot(*self._chip(r), 1 - c)], self.land1[a].at[r],
                                        self.send1_sems.at[k], self.recv1_sems.at[k], sibling))
        if self.packed is not None:
            k = N_CHIPS * self.n
            cps.append(self._remote(self.packed, self.pland1, self.send1_sems.at[k], self.recv1_sems.at[k], sibling))
        return cps

    def _copies2(self):
        cps = []
        for a in range(self.n):
            for r in range(1, N_CHIPS):
                k = (N_CHIPS - 1) * a + r - 1
                cps.append(self._remote(self.send2[a].at[r - 1], self.land2[a].at[r - 1], self.send2_sems.at[k], self.recv2_sems.at[k],
                                        (*self._chip(r), self.c)))
        if self.packed is not None:
            for r in range(1, N_CHIPS):
                k = (N_CHIPS - 1) * self.n + r - 1
                cps.append(self._remote(self.pland2.at[0], self.pland2.at[r], self.send2_sems.at[k], self.recv2_sems.at[k],
                                        (*self._chip(r), self.c)))
        return cps

    def start1(self):
        for cp in self._copies1():
            cp.start()

    def finish1_start2(self):
        for cp in self._copies1():
            cp.wait()
        for a in range(self.n):
            self.outs[a][...] = self.own1[a][0].astype(F32) + self.land1[a][0].astype(F32)
            for r in range(1, N_CHIPS):
                self.send2[a][r - 1] = (self.own1[a][r].astype(F32) + self.land1[a][r].astype(F32)).astype(BF16)
        if self.packed is not None:
            self.pland2[0] = self.packed[...] + self.pland1[...]
        for cp in self._copies2():
            cp.start()

    def finish2(self):
        for cp in self._copies2():
            cp.wait()
        for a in range(self.n):
            l2 = self.land2[a]
            self.outs[a][...] = self.outs[a][...] + ((l2[0].astype(F32) + l2[1].astype(F32)) + l2[2].astype(F32))
        if self.packed is not None:
            p2 = self.pland2
            self.ptot[...] = (p2[0] + p2[1]) + (p2[2] + p2[3])


def _adamw(ws, gs, ms, vs):
    n = len(ws)

    def body(*refs):
        for k in range(n):
            w, g, m, v = (refs[j * n + k][...] for j in range(4))
            d_ref, nm_ref, nv_ref = (refs[(4 + j) * n + k] for j in range(3))
            m = ADAM_B1 * m + (1.0 - ADAM_B1) * g
            v = ADAM_B2 * v + (1.0 - ADAM_B2) * jnp.square(g)
            m_hat = m / (1.0 - ADAM_B1 ** ADAM_STEP)
            v_hat = v / (1.0 - ADAM_B2 ** ADAM_STEP)
            d_ref[...] = -ADAM_LR * (m_hat / (jnp.sqrt(v_hat) + ADAM_EPS) + ADAM_WD * w)
            nm_ref[...] = m
            nv_ref[...] = v

    outs = pl.pallas_call(
        body, name="adamw", in_specs=[VMEM_SPEC] * (4 * n), out_specs=[VMEM_SPEC] * (3 * n),
        out_shape=[SDS(w.shape, F32) for w in ws] * 3, compiler_params=_cp(),
    )(*ws, *gs, *ms, *vs)
    return outs[:n], outs[n:2 * n], outs[2 * n:]


WEIGHTS = ("norm_in", "w_in", "q_norm", "w_uq", "kv_norm", "w_ukv", "pool_w", "pool_scale", "w_branch_attn", "w_branch_pool",
           "w_out", "norm_final")
SUBLANES = 8


def _pack_rows(a):
    a = a.reshape(-1, LANES)
    return jnp.pad(a, ((0, -a.shape[0] % SUBLANES), (0, 0)))


def _unpack_rows(packed, like):
    out, row = [], 0
    for a in like:
        rows = a.size // LANES
        out.append(packed[row:row + rows].reshape(a.shape))
        row += rows + (-rows % SUBLANES)
    return out


def kernel(x, norm_in, w_in, q_norm, w_uq, kv_norm, w_ukv, pool_w, pool_scale, w_branch_attn, w_branch_pool, w_out, norm_final, loss_target, m_norm_in, m_w_in, m_q_norm, m_w_uq, m_kv_norm, m_w_ukv, m_pool_w, m_pool_scale, m_w_branch_attn, m_w_branch_pool, m_w_out, m_norm_final, v_norm_in, v_w_in, v_q_norm, v_w_uq, v_kv_norm, v_w_ukv, v_pool_w, v_pool_scale, v_w_branch_attn, v_w_branch_pool, v_w_out, v_norm_final):
    w = dict(norm_in=norm_in, w_in=w_in, q_norm=q_norm, w_uq=w_uq, kv_norm=kv_norm, w_ukv=w_ukv, pool_w=pool_w, pool_scale=pool_scale,
             w_branch_attn=w_branch_attn, w_branch_pool=w_branch_pool, w_out=w_out, norm_final=norm_final)
    m = dict(norm_in=m_norm_in, w_in=m_w_in, q_norm=m_q_norm, w_uq=m_w_uq, kv_norm=m_kv_norm, w_ukv=m_w_ukv, pool_w=m_pool_w,
             pool_scale=m_pool_scale, w_branch_attn=m_w_branch_attn, w_branch_pool=m_w_branch_pool, w_out=m_w_out, norm_final=m_norm_final)
    v = dict(norm_in=v_norm_in, w_in=v_w_in, q_norm=v_q_norm, w_uq=v_w_uq, kv_norm=v_kv_norm, w_ukv=v_w_ukv, pool_w=v_pool_w,
             pool_scale=v_pool_scale, w_branch_attn=v_w_branch_attn, w_branch_pool=v_w_branch_pool, w_out=v_w_out, norm_final=v_norm_final)

    def as2d(name, a):
        if name == "w_in":
            return a.T
        if name == "w_uq":
            return a
        if name == "w_ukv":
            return a.reshape(a.shape[0], -1)
        if name == "pool_w":
            return a.reshape(-1, GROUP)
        return a.reshape(1, -1) if a.ndim == 1 else a

    def unshape(name, a):
        return a.T if name == "w_in" else a.reshape(w[name].shape)

    w_in_pad, w_uq_full, w_ukv_full = _all_gather_bf16([as2d(k, w[k]) for k in ("w_in", "w_uq", "w_ukv")])
    loss, grad_x, g2d = _local_step(
        x.reshape(x.shape[1:]), loss_target.reshape(x.shape[1:]), norm_in, w_in_pad, q_norm,
        w_uq_full.reshape(Q_RANK, HEADS, NOPE + ROPE), kv_norm, w_ukv_full.reshape(KV_RANK, HEADS, NOPE + VDIM),
        pool_w, pool_scale, [w_branch_attn, w_branch_pool, w_out], norm_final)

    deltas, new_m, new_v = _adamw([as2d(k, w[k]) for k in WEIGHTS], [g2d[k] for k in WEIGHTS],
                                  [as2d(k, m[k]) for k in WEIGHTS], [as2d(k, v[k]) for k in WEIGHTS])
    shaped = lambda arrs: [unshape(k, a) for k, a in zip(WEIGHTS, arrs)]
    return (loss, grad_x.reshape(x.shape), *shaped([g2d[k] for k in WEIGHTS]), *shaped(deltas), *shaped(new_m), *shaped(new_v))
```

```python
import functools

import jax
import jax.numpy as jnp
import numpy as np
from jax import lax
from jax.experimental import pallas as pl
from jax.experimental.pallas import tpu as pltpu

F32 = jnp.float32
BF16 = jnp.bfloat16
SDS = jax.ShapeDtypeStruct

D_MODEL = 1024
HEADS = 8
NOPE = 64
ROPE = 32
VDIM = 64
Q_RANK = 384
KV_RANK = 256
MLA_W = HEADS * VDIM
POOL_W = 512
POOL_GROUPS = 4
GROUP = POOL_W // POOL_GROUPS
CHUNK = 64
ROPE_THETA = 10000.0
EPS = 1e-6
SCALE = (NOPE + ROPE) ** -0.5
LOG2E = 1.4426950408889634
LN2 = 0.6931471805599453
QK_SCALE_LOG2 = SCALE * LOG2E
IN_TOTAL = 4256
ADAM_LR, ADAM_B1, ADAM_B2, ADAM_EPS, ADAM_WD, ADAM_STEP = 0.001, 0.9, 0.999, 1e-08, 0.01, 10

N_DEV = 8
LANES = 128
HEAD_PAD = LANES
HW = HEADS * HEAD_PAD

ZQ, ZKV, ZKR, GA, UP, GP, GM, ZTOT = 0, 384, 640, 768, 1280, 1792, 2304, 4352
FRONT_W = GA
ZKR_ORIG = 640

VMEM_LIMIT = 56 * 1024 * 1024


def _cp(sem=None, **kw):
    if sem is not None:
        kw["dimension_semantics"] = sem
    return pltpu.CompilerParams(vmem_limit_bytes=VMEM_LIMIT, **kw)


def _mm(a, b):
    return lax.dot_general(a, b, (((1,), (0,)), ((), ())), preferred_element_type=F32)


def _mm_nt(a, b):
    return lax.dot_general(a, b, (((1,), (1,)), ((), ())), preferred_element_type=F32)


def _mm_tn(a, b):
    return lax.dot_general(a, b, (((0,), (0,)), ((), ())), preferred_element_type=F32)


def _row_spec(tm, w):
    return pl.BlockSpec((tm, w), lambda i: (i, 0))


def _full_spec(a):
    nd = len(a.shape)
    return pl.BlockSpec(a.shape, lambda *_: (0,) * nd)


def _rope(v, c, sa, sb, sign):
    n = v.shape[-1]
    reps = n // LANES
    if reps > 1:
        c, sa, sb = (jnp.tile(t, (1, reps)) for t in (c, sa, sb))
    up = pltpu.roll(v, n - ROPE // 2, 1)
    dn = pltpu.roll(v, ROPE // 2, 1)
    return v * c + sign * (up * sa + dn * sb)


def _rope_tables(T):
    half = ROPE // 2
    inv_freq = np.float32(ROPE_THETA) ** (-np.arange(half, dtype=np.float32) / np.float32(half))
    ang = np.arange(T, dtype=np.float32)[:, None] * inv_freq[None, :].astype(np.float32)
    cos, sin = np.cos(ang.astype(np.float64)).astype(np.float32), np.sin(ang.astype(np.float64)).astype(np.float32)
    z16 = np.zeros((T, half), np.float32)
    z32 = np.zeros((T, LANES - NOPE - ROPE), np.float32)
    c = np.concatenate([np.ones((T, NOPE), np.float32), cos, cos, z32], axis=1)
    sa = np.concatenate([np.zeros((T, NOPE), np.float32), -sin, z16, z32], axis=1)
    sb = np.concatenate([np.zeros((T, NOPE), np.float32), z16, sin, z32], axis=1)
    return jnp.asarray(c), jnp.asarray(sa), jnp.asarray(sb)


def _silu_parts(g):
    sg = jax.nn.sigmoid(g)
    return g * sg, sg + g * sg * (1.0 - sg)


def _in_proj(x2, norm_in, w_in_pad, q_norm, wuq_pad, kv_norm, wukv, rc, rsa, rsb, tm, late_shards):
    T = x2.shape[0]
    steps = T // tm
    n = len(late_shards)

    def body(x_ref, nin_ref, win_ref, qn_ref, wuq_ref, kvn_ref, wukv_ref, c_ref, sa_ref, sb_ref, *rest):
        hn_ref, zgm_ref, zga_ref, zup_ref, zgp_ref, zfr_ref, q_ref, k_ref, v_ref, vt_ref = rest[n:n + 10]
        ag = _AllGather(rest[:n], rest[n + 10:2 * n + 10], rest[2 * n + 10:])
        step = pl.program_id(0)
        pl.when(step == 0)(ag.start)
        pl.when(step == min(5, steps - 1))(ag.forward)
        xf = x_ref[...]
        r = lax.rsqrt(jnp.mean(xf * xf, axis=-1, keepdims=True) + EPS)
        hn = (xf * r * nin_ref[...]).astype(BF16)
        hn_ref[...] = hn
        z = _mm_nt(hn, win_ref[...])
        zgm_ref[...] = z[:, GM:ZTOT]
        zga_ref[...] = z[:, GA:UP]
        zup_ref[...] = z[:, UP:GP]
        zgp_ref[...] = z[:, GP:GM]
        zfr_ref[...] = z[:, ZQ:GA]
        zq, zkv, zkr = z[:, ZQ:ZKV], z[:, ZKV:ZKR], z[:, ZKR:GA]
        c, sa, sb = c_ref[...], sa_ref[...], sb_ref[...]
        rq = lax.rsqrt(jnp.mean(zq * zq, axis=-1, keepdims=True) + EPS)
        cq = (zq * rq * qn_ref[...]).astype(BF16)
        q = _rope(_mm(cq, wuq_ref[...]), c, sa, sb, 1.0)
        q_ref[...] = (q * QK_SCALE_LOG2).astype(BF16)
        rkv = lax.rsqrt(jnp.mean(zkv * zkv, axis=-1, keepdims=True) + EPS)
        ckv = (zkv * rkv * kvn_ref[...]).astype(BF16)
        kv = _mm(ckv, wukv_ref[...])
        kr = _rope(zkr, c, sa, sb, 1.0)
        lane = lax.broadcasted_iota(jnp.int32, kv.shape, 1) % LANES
        k_ref[...] = jnp.where(lane < NOPE, kv, jnp.tile(kr, (1, HEADS))).astype(BF16)
        v = jnp.where(lane < NOPE, 1.0, kv).astype(BF16)
        v_ref[...] = v
        vt_ref[...] = v.T
        pl.when(step == steps - 1)(ag.finish)

    ins = (x2, norm_in, w_in_pad, q_norm, wuq_pad, kv_norm, wukv, rc, rsa, rsb)
    in_specs = [_row_spec(tm, D_MODEL), _full_spec(norm_in), _full_spec(w_in_pad), _full_spec(q_norm), _full_spec(wuq_pad),
                _full_spec(kv_norm), _full_spec(wukv), _row_spec(tm, LANES), _row_spec(tm, LANES), _row_spec(tm, LANES)]
    widths = [(D_MODEL, BF16), (ZTOT - GM, F32), (UP - GA, F32), (GP - UP, F32), (GM - GP, F32), (FRONT_W, F32),
              (HW, BF16), (HW, BF16), (HW, BF16)]
    return pl.pallas_call(
        body, name="in_proj", grid=(steps,), in_specs=in_specs + [_full_spec(s) for s in late_shards],
        out_specs=[_row_spec(tm, w) for w, _ in widths] + [pl.BlockSpec((HW, tm), lambda i: (0, i))] + [HBM_SPEC] * n,
        out_shape=[SDS((T, w), dt) for w, dt in widths] + [SDS((HW, T), BF16)]
        + [SDS((N_DEV,) + s.shape, BF16) for s in late_shards],
        scratch_shapes=_ag_scratch([s.shape for s in late_shards]), compiler_params=_cp(("arbitrary",)),
    )(*ins, *late_shards)


def _chunk_mask(n_q, n_k, q_off, transposed):
    shape = (n_k, n_q) if transposed else (n_q, n_k)
    q = (lax.broadcasted_iota(jnp.int32, shape, 1 if transposed else 0) + q_off) // CHUNK
    k = lax.broadcasted_iota(jnp.int32, shape, 0 if transposed else 1) // CHUNK
    return k <= q


def _store_pair_rows(ref, k, pair):
    t = pair.T
    ref[k, 0:1, :] = t[0:1, :]
    ref[k, 1:2, :] = t[VDIM:VDIM + 1, :]


def _attn_fwd(q_att, k_att, vt_att, tq, hps):
    T = q_att.shape[0]
    head_lanes = [slice(h * LANES, (h + 1) * LANES) for h in range(hps)]

    def body(q_ref, k_ref, vt_ref, o_ref, lser_ref):
        i = pl.program_id(1)
        mask = _chunk_mask(tq, tq, 0, True)
        lane = lax.broadcasted_iota(jnp.int32, (tq, LANES), 1)
        qs = [q_ref[:, hs] for hs in head_lanes]

        def step(j, carry, masked):
            off = pl.multiple_of(j * tq, tq)
            sts = [_mm_nt(k_ref[pl.ds(off, tq), hs], qh) for qh, hs in zip(qs, head_lanes)]
            if masked:
                sts = [jnp.where(mask, st, -jnp.inf) for st in sts]
            ms = [jnp.maximum(m, jnp.max(st, axis=0, keepdims=True)) for (m, _), st in zip(carry, sts)]
            pts = [jnp.exp2(st - m_new).astype(BF16) for st, m_new in zip(sts, ms)]
            return tuple((m_new, jnp.exp2(m - m_new) * acc + _mm(vt_ref[hs, pl.ds(off, tq)], pt))
                         for (m, acc), m_new, pt, hs in zip(carry, ms, pts, head_lanes))

        init = ((jnp.full((1, tq), -jnp.inf, F32), jnp.zeros((LANES, tq), F32)),) * hps
        res = step(i, lax.fori_loop(0, i, functools.partial(step, masked=False), init), True)
        for pair in range(hps // 2):
            (ma, acca), (mb, accb) = res[2 * pair], res[2 * pair + 1]
            la, lb = acca[:1], accb[:1]
            oa, ob = (acca / la).T, (accb / lb).T
            o_ref[:, pair * LANES:(pair + 1) * LANES] = jnp.where(lane < VDIM, pltpu.roll(oa, VDIM, 1), ob)
            lser_ref[pair, 0:1, :] = ma + jnp.log2(la)
            lser_ref[pair, 1:2, :] = mb + jnp.log2(lb)

    qspec = pl.BlockSpec((tq, hps * LANES), lambda p, i: (i, p))
    kspec = pl.BlockSpec((T, hps * LANES), lambda p, i: (0, p))
    vspec = pl.BlockSpec((hps * LANES, T), lambda p, i: (p, 0))
    ospec = pl.BlockSpec((tq, hps * VDIM), lambda p, i: (i, p))
    return pl.pallas_call(
        body, name="attn_fwd", grid=(HEADS // hps, T // tq), in_specs=[qspec, kspec, vspec],
        out_specs=[ospec, pl.BlockSpec((hps // 2, 2, tq), lambda p, i: (p, 0, i))],
        out_shape=[SDS((T, MLA_W), F32), SDS((HEADS // 2, 2, T), F32)],
        compiler_params=_cp(("parallel", "parallel")),
    )(q_att, k_att, vt_att)


def _pick(g, vals):
    out = vals[-1]
    for k in range(len(vals) - 2, -1, -1):
        out = jnp.where(g == k, vals[k], out)
    return out


def _window_sum(u, g, forward):
    T = u.shape[0]
    row = lax.broadcasted_iota(jnp.int32, u.shape, 0)

    def sh(s, k):
        if forward:
            return jnp.where(row >= k, pltpu.roll(s, k, 0), 0.0)
        return jnp.where(row < T - k, pltpu.roll(s, T - k, 0), 0.0)

    sums, s = [], u
    for k in (1, 2, 4, 8):
        s = s + sh(s, k)
        sums.append(s)
    return _pick(g, sums)


def _pool_count(shape, g):
    row = lax.broadcasted_iota(jnp.int32, shape, 0)
    return jnp.minimum(row + 1, lax.shift_left(jnp.int32(2), g)).astype(F32)


def _pool_fwd(zup, zgp, pool_w, pool_scale):
    T = zup.shape[0]

    def body(u_ref, g_ref, w_ref, sc_ref, y_ref):
        g = pl.program_id(0)
        u = u_ref[...]
        d = _window_sum(u, g, True) / _pool_count(u.shape, g) - u
        lin = _mm(d.astype(BF16), w_ref[0].astype(BF16))
        silu, _ = _silu_parts(g_ref[...])
        y_ref[...] = (lin * sc_ref[...] * silu).astype(BF16)

    col = pl.BlockSpec((T, GROUP), lambda g: (0, g))
    return pl.pallas_call(
        body, name="pool_fwd", grid=(POOL_GROUPS,),
        in_specs=[col, col, pl.BlockSpec((1, GROUP, GROUP), lambda g: (g, 0, 0)), pl.BlockSpec((1, GROUP), lambda g: (0, g))],
        out_specs=col, out_shape=SDS((T, POOL_W), BF16), compiler_params=_cp(("parallel",)),
    )(zup, zgp, pool_w, pool_scale)


def _pool_bwd(zup, zgp, dyp, pool_w, pool_scale):
    T = zup.shape[0]

    def body(u_ref, g_ref, dy_ref, w_ref, sc_ref, du_ref, dg_ref, gw_ref, gsc_ref):
        g = pl.program_id(0)
        u = u_ref[...]
        cnt = _pool_count(u.shape, g)
        d = (_window_sum(u, g, True) / cnt - u).astype(BF16)
        wb = w_ref[0].astype(BF16)
        lin = _mm(d, wb)
        sc = sc_ref[...]
        silu, dsilu = _silu_parts(g_ref[...])
        dy = dy_ref[...]
        dg_ref[...] = (dy * lin * sc * dsilu).astype(BF16)
        dpre = dy * silu
        gsc_ref[...] = jnp.sum(dpre * lin, axis=0, keepdims=True)
        dlin = (dpre * sc).astype(BF16)
        gw_ref[0] = _mm_tn(d, dlin)
        dd = _mm_nt(dlin, wb)
        du_ref[...] = (_window_sum(dd / cnt, g, False) - dd).astype(BF16)

    col = pl.BlockSpec((T, GROUP), lambda g: (0, g))
    wspec = pl.BlockSpec((1, GROUP, GROUP), lambda g: (g, 0, 0))
    vspec = pl.BlockSpec((1, GROUP), lambda g: (0, g))
    return pl.pallas_call(
        body, name="pool_bwd", grid=(POOL_GROUPS,), in_specs=[col, col, col, wspec, vspec], out_specs=[col, col, wspec, vspec],
        out_shape=[SDS((T, POOL_W), BF16), SDS((T, POOL_W), BF16), SDS((POOL_GROUPS, GROUP, GROUP), F32), SDS((1, POOL_W), F32)],
        compiler_params=_cp(("parallel",)),
    )(zup, zgp, dyp, pool_w, pool_scale)


def _tail(x2, tgt, o, zga, ypool, zgm, wba, wbp, wout, norm_final, tm):
    T = x2.shape[0]
    steps = T // tm
    cols = D_MODEL // N_DEV

    def body(x_ref, tgt_ref, o_ref, zga_ref, yp_ref, zgm_ref, wba_ref, wbp_ref, wout_ref, nf_ref,
             loss_ref, dh_ref, dgm_ref, doop_ref, dga_ref, dcapr_ref, dyp_ref, swout_ref, swba_ref, swbp_ref, gnf_ref,
             gwout_ref, gwba_ref, gwbp_ref):
        @pl.when(pl.program_id(0) == 0)
        def _():
            for ref in (loss_ref, gwout_ref, gwba_ref, gwbp_ref, gnf_ref):
                ref[...] = jnp.zeros_like(ref)

        o_v = o_ref[...]
        silu, dsilu = _silu_parts(zga_ref[...])
        ya = (o_v * silu).astype(BF16)
        yp = yp_ref[...]
        wba_v = jnp.concatenate([wba_ref[k] for k in range(N_DEV)], axis=1)
        wbp_v = jnp.concatenate([wbp_ref[k] for k in range(N_DEV)], axis=1)
        wout_v = wout_ref[...]
        a = _mm(ya, wba_v)
        p = _mm(yp, wbp_v)
        gate = jax.nn.sigmoid(zgm_ref[...])
        ga, gp = gate[:, :D_MODEL], gate[:, D_MODEL:]
        mg = (ga * a + gp * p).astype(BF16)
        h = x_ref[...] + _mm(mg, wout_v)
        r = lax.rsqrt(jnp.mean(h * h, axis=-1, keepdims=True) + EPS)
        gf = nf_ref[...]
        hr = h * r
        e = hr * gf - tgt_ref[...]
        loss_ref[...] += (0.5 / D_MODEL) * jnp.sum(e * e)
        dy = e * (1.0 / D_MODEL)
        gnf_ref[...] += jnp.sum(dy * hr, axis=0, keepdims=True)
        u = dy * gf
        dh = r * (u - hr * jnp.mean(u * hr, axis=-1, keepdims=True))
        dh_ref[...] = dh
        dhb = dh.astype(BF16)
        dmg = _mm_nt(dhb, wout_v)
        gwout_ref[...] += _mm_tn(mg, dhb)
        dgm_ref[:, :D_MODEL] = (dmg * a * ga * (1.0 - ga)).astype(BF16)
        dgm_ref[:, D_MODEL:] = (dmg * p * gp * (1.0 - gp)).astype(BF16)
        dab = (dmg * ga).astype(BF16)
        dpb = (dmg * gp).astype(BF16)
        dya = _mm_nt(dab, wba_v)
        gwba_ref[...] += _mm_tn(ya, dab)
        dyp_ref[...] = _mm_nt(dpb, wbp_v)
        gwbp_ref[...] += _mm_tn(yp, dpb)
        do = dya * silu
        dga_ref[...] = (dya * o_v * dsilu).astype(BF16)
        prod = do * o_v
        lo = lax.broadcasted_iota(jnp.int32, (tm, LANES), 1) < VDIM
        for pair in range(HEADS // 2):
            ls = slice(pair * LANES, (pair + 1) * LANES)
            do_p, prod_p = do[:, ls], prod[:, ls]
            dcap_a = jnp.sum(jnp.where(lo, prod_p, 0.0), axis=-1, keepdims=True)
            dcap_b = jnp.sum(jnp.where(lo, 0.0, prod_p), axis=-1, keepdims=True)
            _store_pair_rows(dcapr_ref, pair, jnp.where(lo, dcap_a, dcap_b))
            doop_ref[:, 2 * pair * LANES:(2 * pair + 1) * LANES] = jnp.where(lo, 0.0, pltpu.roll(do_p, VDIM, 1)).astype(BF16)
            doop_ref[:, (2 * pair + 1) * LANES:(2 * pair + 2) * LANES] = jnp.where(lo, 0.0, do_p).astype(BF16)

        @pl.when(pl.program_id(0) == steps - 1)
        def _():
            for k in range(N_DEV):
                swout_ref[k] = gwout_ref[k * cols:(k + 1) * cols, :].astype(BF16)
                swba_ref[k] = gwba_ref[:, k * cols:(k + 1) * cols].astype(BF16)
                swbp_ref[k] = gwbp_ref[:, k * cols:(k + 1) * cols].astype(BF16)

    ins = (x2, tgt, o, zga, ypool, zgm, wba, wbp, wout, norm_final)
    in_specs = [_row_spec(tm, D_MODEL), _row_spec(tm, D_MODEL), _row_spec(tm, MLA_W), _row_spec(tm, MLA_W), _row_spec(tm, POOL_W),
                _row_spec(tm, 2 * D_MODEL), _full_spec(wba), _full_spec(wbp), _full_spec(wout), _full_spec(norm_final)]
    outs = [SDS((8, LANES), F32), SDS((T, D_MODEL), F32), SDS((T, 2 * D_MODEL), BF16), SDS((T, HW), BF16), SDS((T, MLA_W), BF16),
            SDS((HEADS // 2, 2, T), F32), SDS((T, POOL_W), F32),
            SDS((N_DEV, cols, D_MODEL), BF16), SDS((N_DEV, MLA_W, cols), BF16), SDS((N_DEV, POOL_W, cols), BF16), SDS((1, D_MODEL), F32)]
    out_specs = [_full_spec(outs[0]), _row_spec(tm, D_MODEL), _row_spec(tm, 2 * D_MODEL), _row_spec(tm, HW), _row_spec(tm, MLA_W),
                 pl.BlockSpec((HEADS // 2, 2, tm), lambda i: (0, 0, i)), _row_spec(tm, POOL_W),
                 _full_spec(outs[7]), _full_spec(outs[8]), _full_spec(outs[9]), _full_spec(outs[10])]
    return pl.pallas_call(
        body, name="tail", grid=(steps,), in_specs=in_specs, out_specs=out_specs, out_shape=outs,
        scratch_shapes=[pltpu.VMEM((D_MODEL, D_MODEL), F32), pltpu.VMEM((MLA_W, D_MODEL), F32), pltpu.VMEM((POOL_W, D_MODEL), F32)],
        compiler_params=_cp(("arbitrary",)),
    )(*ins)


def _attn_bwd(q_att, k_att, v_att, doop, lse_rows, dcap_rows, tq, hps, slabs, packed):
    T = q_att.shape[0]
    nq = T // tq
    n = len(slabs)
    groups = HEADS // hps
    head_lanes = [slice(h * LANES, (h + 1) * LANES) for h in range(hps)]

    def body(q_ref, k_ref, v_ref, doop_ref, lse_ref, dcap_ref, *rest):
        slab_refs, packed_ref = rest[:n], rest[n]
        dq_ref, dkv_ref, dkr_ref = rest[n + 1:n + 4]
        sum_refs, ptot_ref = rest[n + 4:2 * n + 4], rest[2 * n + 4]
        dq_acc = rest[2 * n + 5]
        rs = _ReduceScatter(slab_refs, packed_ref, sum_refs, ptot_ref, rest[2 * n + 6:])
        group, j = pl.program_id(0), pl.program_id(1)
        pl.when((group == 0) & (j == 0))(rs.start1)
        pl.when((group == 1) & (j == 0))(rs.finish1_start2)
        mask = _chunk_mask(tq, tq, 0, True)
        lane = lax.broadcasted_iota(jnp.int32, (tq, LANES), 1)
        ks = [k_ref[:, hs] for hs in head_lanes]
        vs = [v_ref[:, hs] for hs in head_lanes]
        kts = [kh.T for kh in ks]

        @pl.when(j == 0)
        def _():
            dq_acc[...] = jnp.zeros_like(dq_acc)

        def step(i, carry, masked):
            rows = pl.ds(pl.multiple_of(i * tq, tq), tq)
            heads = range(hps)
            stat = lambda h: (h // 2, slice(h % 2, h % 2 + 1), rows)
            qhs = [q_ref[rows, hs] for hs in head_lanes]
            doops = [doop_ref[rows, hs] for hs in head_lanes]
            sts = [_mm_nt(ks[h], qhs[h]) for h in heads]
            dpts = [_mm_nt(vs[h], doops[h]) for h in heads]
            pts = [jnp.exp2(sts[h] - lse_ref[stat(h)]) for h in heads]
            if masked:
                pts = [jnp.where(mask, pt, 0.0) for pt in pts]
            dsts = [(pts[h] * (dpts[h] - dcap_ref[stat(h)])).astype(BF16) for h in heads]
            dvs = [_mm(pts[h].astype(BF16), doops[h]) for h in heads]
            dks = [_mm(dsts[h], qhs[h]) for h in heads]
            for h, hs in enumerate(head_lanes):
                dq_acc[hs, rows] += _mm(kts[h], dsts[h])
            return tuple((dk + dks[h], dv + dvs[h]) for h, (dk, dv) in enumerate(carry))

        zero = jnp.zeros((tq, LANES), F32)
        carry = step(j, ((zero, zero),) * hps, True)
        res = lax.fori_loop(j + 1, nq, functools.partial(step, masked=False), carry)
        dkr = None
        for (dk, dv), hs in zip(res, head_lanes):
            dk = dk * LN2
            dkv_ref[:, hs] = jnp.where(lane < NOPE, dk, dv).astype(BF16)
            dkr = dk if dkr is None else dkr + dk
        dkr_ref[0] = jnp.where((lane >= NOPE) & (lane < NOPE + ROPE), dkr, 0.0)

        @pl.when(j == nq - 1)
        def _():
            dq_ref[...] = (dq_acc[...] * SCALE).T.astype(BF16)

        pl.when((group == groups - 1) & (j == nq - 1))(rs.finish2)

    kspec = pl.BlockSpec((tq, hps * LANES), lambda p, j: (j, p))
    qspec = pl.BlockSpec((T, hps * LANES), lambda p, j: (0, p))
    rspec = pl.BlockSpec((hps // 2, 2, T), lambda p, j: (p, 0, 0))
    sums = [SDS(s.shape[1:], F32) for s in slabs] + [SDS(packed.shape, F32)]
    return pl.pallas_call(
        body, name="attn_bwd", grid=(groups, nq),
        in_specs=[qspec, kspec, kspec, qspec, rspec, rspec] + [HBM_SPEC] * n + [_full_spec(packed)],
        out_specs=[qspec, kspec, pl.BlockSpec((1, tq, LANES), lambda p, j: (p, j, 0))] + [_full_spec(s) for s in sums],
        out_shape=[SDS((T, HW), BF16), SDS((T, HW), BF16), SDS((groups, T, LANES), F32)] + sums,
        scratch_shapes=[pltpu.VMEM((hps * LANES, T), F32)] + _rs_scratch([s.shape for s in sums[:-1]], packed.shape),
        compiler_params=_cp(("arbitrary", "arbitrary")),
    )(q_att, k_att, v_att, doop, lse_rows, dcap_rows, *slabs, packed)


def _rms_bwd(z, gain, dout):
    r = lax.rsqrt(jnp.mean(z * z, axis=-1, keepdims=True) + EPS)
    zr = z * r
    u = dout * gain
    return r * (u - zr * jnp.mean(u * zr, axis=-1, keepdims=True)), jnp.sum(dout * zr, axis=0, keepdims=True)


def _mla_bwd(dq_att, dkv_nat, dkr4, zfr, q_norm, wuq_pad, kv_norm, wukv, rc, rsa, rsb, tm):
    T = dq_att.shape[0]

    def body(dq_ref, dkv_ref, dkr_ref, zfr_ref, qn_ref, wuq_ref, kvn_ref, wukv_ref, c_ref, sa_ref, sb_ref,
             dfr_ref, gwuq_ref, gwukv_ref, gqn_ref, gkvn_ref):
        @pl.when(pl.program_id(0) == 0)
        def _():
            for ref in (gwuq_ref, gwukv_ref, gqn_ref, gkvn_ref):
                ref[...] = jnp.zeros_like(ref)

        c, sa, sb = c_ref[...], sa_ref[...], sb_ref[...]
        zq, zkv = zfr_ref[:, :Q_RANK], zfr_ref[:, Q_RANK:Q_RANK + KV_RANK]
        qn, kvn = qn_ref[...], kvn_ref[...]
        cq = (zq * lax.rsqrt(jnp.mean(zq * zq, axis=-1, keepdims=True) + EPS) * qn).astype(BF16)
        ckv = (zkv * lax.rsqrt(jnp.mean(zkv * zkv, axis=-1, keepdims=True) + EPS) * kvn).astype(BF16)
        dq = _rope(dq_ref[...].astype(F32), c, sa, sb, -1.0).astype(BF16)
        gwuq_ref[...] += _mm_tn(cq, dq)
        dzq, gqn = _rms_bwd(zq, qn, _mm_nt(dq, wuq_ref[...]))
        gqn_ref[...] += gqn
        dkv = dkv_ref[...]
        gwukv_ref[...] += _mm_tn(ckv, dkv)
        dzkv, gkvn = _rms_bwd(zkv, kvn, _mm_nt(dkv, wukv_ref[...]))
        gkvn_ref[...] += gkvn
        dkr = functools.reduce(lambda a, b: a + b, [dkr_ref[g] for g in range(dkr4.shape[0])])
        dfr_ref[:, :Q_RANK] = dzq.astype(BF16)
        dfr_ref[:, Q_RANK:Q_RANK + KV_RANK] = dzkv.astype(BF16)
        dfr_ref[:, Q_RANK + KV_RANK:] = _rope(dkr, c, sa, sb, -1.0).astype(BF16)

    ins = (dq_att, dkv_nat, dkr4, zfr, q_norm, wuq_pad, kv_norm, wukv, rc, rsa, rsb)
    in_specs = [_row_spec(tm, HW), _row_spec(tm, HW), pl.BlockSpec((dkr4.shape[0], tm, LANES), lambda i: (0, i, 0)), _row_spec(tm, FRONT_W),
                _full_spec(q_norm), _full_spec(wuq_pad), _full_spec(kv_norm), _full_spec(wukv),
                _row_spec(tm, LANES), _row_spec(tm, LANES), _row_spec(tm, LANES)]
    outs = [SDS((T, FRONT_W), BF16), SDS((Q_RANK, HW), F32), SDS((KV_RANK, HW), F32), SDS((1, Q_RANK), F32), SDS((1, KV_RANK), F32)]
    out_specs = [_row_spec(tm, FRONT_W)] + [_full_spec(s) for s in outs[1:]]
    return pl.pallas_call(
        body, name="mla_bwd", grid=(T // tm,), in_specs=in_specs, out_specs=out_specs, out_shape=outs,
        compiler_params=_cp(("arbitrary",)),
    )(*ins)


_DZ_COLS = ((GM, ZTOT), (GA, UP), (UP, GP), (GP, GM), (ZQ, GA))


def _in_proj_bwd_x(dzs, x2, dh, norm_in, w_in_pad, tm, slabs):
    T = x2.shape[0]
    steps = T // tm
    n = len(slabs)

    def body(d0, d1, d2, d3, d4, x_ref, dh_ref, nin_ref, win_ref, *rest):
        slab_refs, (gx_ref, gnin_ref), sum_refs = rest[:n], rest[n:n + 2], rest[n + 2:2 * n + 2]
        rs = _ReduceScatter(slab_refs, None, sum_refs, None, rest[2 * n + 2:])
        step = pl.program_id(0)

        @pl.when(step == 0)
        def _():
            gnin_ref[...] = jnp.zeros_like(gnin_ref)
            rs.start1()

        pl.when(step == min(2, steps - 1))(rs.finish1_start2)
        dhn = None
        for ref, (lo, hi) in zip((d0, d1, d2, d3, d4), _DZ_COLS):
            t = _mm(ref[...], win_ref[lo:hi, :])
            dhn = t if dhn is None else dhn + t
        dx, gnin = _rms_bwd(x_ref[...], nin_ref[...], dhn)
        gnin_ref[...] += gnin
        gx_ref[...] = dx + dh_ref[...]
        pl.when(step == steps - 1)(rs.finish2)

    in_specs = [_row_spec(tm, hi - lo) for lo, hi in _DZ_COLS] + [_row_spec(tm, D_MODEL), _row_spec(tm, D_MODEL),
                                                                  _full_spec(norm_in), _full_spec(w_in_pad)] + [HBM_SPEC] * n
    sums = [SDS(s.shape[1:], F32) for s in slabs]
    outs = [SDS((T, D_MODEL), F32), SDS((1, D_MODEL), F32)] + sums
    return pl.pallas_call(
        body, name="in_proj_bwd_x", grid=(steps,), in_specs=in_specs,
        out_specs=[_row_spec(tm, D_MODEL), _full_spec(outs[1])] + [_full_spec(s) for s in sums],
        out_shape=outs, scratch_shapes=_rs_scratch([s.shape for s in sums], None), compiler_params=_cp(("arbitrary",)),
    )(*dzs, x2, dh, norm_in, w_in_pad, *slabs)


SLAB_ROWS = IN_TOTAL // N_DEV


def _slab_segments(k):
    cuts = [(0, ZKR_ORIG, 0), (ZKR_ORIG, ZKR_ORIG + ROPE, NOPE), (ZKR_ORIG + ROPE, IN_TOTAL, LANES - ROPE)]
    lo, hi = k * SLAB_ROWS, (k + 1) * SLAB_ROWS
    return [(max(lo, a) - lo, max(lo, a) + shift, min(hi, b) - max(lo, a)) for a, b, shift in cuts if min(hi, b) > max(lo, a)]


def _in_proj_bwd_w(dzs, hn, tm):
    T = hn.shape[0]
    steps = T // tm

    def body(d0, d1, d2, d3, d4, hn_ref, slab_ref, acc_ref):
        @pl.when(pl.program_id(0) == 0)
        def _():
            acc_ref[...] = jnp.zeros_like(acc_ref)

        hn_v = hn_ref[...]
        for ref, (lo, hi) in zip((d0, d1, d2, d3, d4), _DZ_COLS):
            acc_ref[lo:hi, :] += _mm_tn(ref[...], hn_v)

        @pl.when(pl.program_id(0) == steps - 1)
        def _():
            for k in range(N_DEV):
                for at, src, rows in _slab_segments(k):
                    slab_ref[k, at:at + rows, :] = acc_ref[src:src + rows, :].astype(BF16)

    in_specs = [_row_spec(tm, hi - lo) for lo, hi in _DZ_COLS] + [_row_spec(tm, D_MODEL)]
    out = SDS((N_DEV, SLAB_ROWS, D_MODEL), BF16)
    return pl.pallas_call(
        body, name="in_proj_bwd_w", grid=(steps,), in_specs=in_specs, out_specs=_full_spec(out), out_shape=out,
        scratch_shapes=[pltpu.VMEM((ZTOT, D_MODEL), F32)], compiler_params=_cp(("arbitrary",)),
    )(*dzs, hn)


def _local_step(x2, tgt, norm_in, w_in_pad, q_norm, w_uq, kv_norm, w_ukv, pool_w, pool_scale, late_shards, norm_final):
    T = x2.shape[0]
    tm = min(512, T)
    tq = min(512, T)
    row = lambda v: v.reshape(1, -1)
    wuq_pad = jnp.pad(w_uq, ((0, 0), (0, 0), (0, HEAD_PAD - NOPE - ROPE))).reshape(Q_RANK, HW)
    wukv = w_ukv.reshape(KV_RANK, HW)
    rc, rsa, rsb = _rope_tables(T)

    hn, zgm, zga, zup, zgp, zfr, q_att, k_att, v_att, vt_att, w_ba, w_bp, w_out = _in_proj(
        x2, row(norm_in), w_in_pad, row(q_norm), wuq_pad, row(kv_norm), wukv, rc, rsa, rsb, tm, late_shards)
    w_out = w_out.reshape(D_MODEL, D_MODEL)
    o, lse_rows = _attn_fwd(q_att, k_att, vt_att, tq, 4)
    ypool = _pool_fwd(zup, zgp, pool_w, row(pool_scale))
    loss8, dh, dgm, doop, dga, dcap_rows, dyp, *slabs, g_nf = _tail(
        x2, tgt, o, zga, ypool, zgm, w_ba, w_bp, w_out, row(norm_final), min(256, T))
    dup, dgp, g_pool_w, g_pool_scale = _pool_bwd(zup, zgp, dyp, pool_w, row(pool_scale))

    bf = lambda a: a.astype(BF16)
    early = [g_pool_w, g_pool_scale, g_nf, loss8[0]]
    packed = jnp.concatenate([_pack_rows(a) for a in early], axis=0)
    dq_att, dkv_nat, dkr4, s_wout, s_wba, s_wbp, tot_early = _attn_bwd(
        q_att, k_att, v_att, doop, lse_rows, dcap_rows, tq, 2, slabs, packed)
    s_pool_w, s_pool_scale, s_nf, s_loss = _unpack_rows(tot_early, early)

    dfr, g_wuq_pad, g_wukv, g_qn, g_kvn = _mla_bwd(
        dq_att, dkv_nat, dkr4, zfr, row(q_norm), wuq_pad, row(kv_norm), wukv, rc, rsa, rsb, tm)
    dzs = (dgm, dga, dup, dgp, dfr)
    slabs = [_in_proj_bwd_w(dzs, hn, tm),
             bf(g_wuq_pad.reshape(N_DEV, Q_RANK // N_DEV, HEADS, HEAD_PAD)[..., :NOPE + ROPE]),
             bf(g_wukv).reshape(N_DEV, KV_RANK // N_DEV, HW)]
    grad_x, g_nin, s_win, s_wuq, s_wukv = _in_proj_bwd_x(dzs, x2, dh, row(norm_in), w_in_pad, min(256, T), slabs)
    late = [g_nin, g_qn, g_kvn]
    (tot_late,) = _reduce_scatter([], jnp.concatenate([_pack_rows(a) for a in late], axis=0))
    s_nin, s_qn, s_kvn = _unpack_rows(tot_late, late)

    grads = dict(norm_in=s_nin, w_in=s_win, q_norm=s_qn, w_uq=s_wuq, kv_norm=s_kvn, w_ukv=s_wukv, pool_w=s_pool_w.reshape(-1, GROUP),
                 pool_scale=s_pool_scale, w_branch_attn=s_wba, w_branch_pool=s_wbp, w_out=s_wout, norm_final=s_nf)
    return s_loss[0], grad_x, grads


MESH_ID = pl.DeviceIdType.MESH
VMEM_SPEC = pl.BlockSpec(memory_space=pltpu.VMEM)
HBM_SPEC = pl.BlockSpec(memory_space=pl.ANY)


def _mesh_pos():
    return lax.axis_index("x"), lax.axis_index("y"), lax.axis_index("c")


def _slot(px, py, pc):
    return 4 * px + 2 * py + pc


def _all_gather_bf16(shards):
    n = len(shards)

    def body(*refs):
        ins, outs = refs[:n], refs[n:2 * n]
        land0, scratch = refs[2 * n], refs[2 * n + 1:]
        wpad_ref = outs[0]
        ag = _AllGather(ins, (land0,) + tuple(outs[1:]), scratch)
        ag.start()
        ag.forward()
        ag.finish()
        wpad_ref[ZKR:GA, :] = jnp.zeros((GA - ZKR, D_MODEL), BF16)
        for k in range(N_DEV):
            for at, dst, rows in _slab_segments(k):
                wpad_ref[dst:dst + rows, :] = land0[k, at:at + rows, :]

    return pl.pallas_call(
        body, name="all_gather_weights",
        in_specs=[VMEM_SPEC] * n, out_specs=[VMEM_SPEC] + [HBM_SPEC] * (n - 1),
        out_shape=[SDS((ZTOT, D_MODEL), BF16)] + [SDS((N_DEV,) + s.shape, BF16) for s in shards[1:]],
        scratch_shapes=[pltpu.VMEM((N_DEV,) + shards[0].shape, BF16)] + _ag_scratch([s.shape for s in shards]),
        compiler_params=_cp(),
    )(*shards)


def _ag_scratch(shapes):
    n = len(shapes)
    dma = pltpu.SemaphoreType.DMA
    return [pltpu.VMEM(tuple(s), BF16) for s in shapes] + [dma((_AllGather.COPIES * n,)), dma((_AllGather.COPIES * n,)), dma((n,))]


class _AllGather:
    COPIES = 8

    def __init__(self, in_refs, dest_refs, scratch):
        n = self.n = len(in_refs)
        self.ins, self.dests, self.stage = in_refs, dest_refs, scratch[:n]
        self.send_sems, self.recv_sems, self.local_sems = scratch[n:]
        x, y, c = _mesh_pos()
        self.c, self.me, self.sibling = c, (x, y, c), (x, y, 1 - c)
        self.xn, self.yn, self.diag = (1 - x, y), (x, 1 - y), (1 - x, 1 - y)

    def _halves(self, a):
        rows = self.stage[a].shape[0]
        cut = (rows // 2 + 15) // 16 * 16
        return pl.ds(0, cut), pl.ds(cut, rows - cut)

    def _copy(self, a, k, block, to, from_stage=False, rows=None):
        dst = self.dests[a].at[_slot(*block)]
        src = self.stage[a] if from_stage else dst
        if rows is not None:
            src, dst = src.at[rows], dst.at[rows]
        return pltpu.make_async_remote_copy(
            src_ref=src, dst_ref=dst, send_sem=self.send_sems.at[self.COPIES * a + k],
            recv_sem=self.recv_sems.at[self.COPIES * a + k], device_id=to, device_id_type=MESH_ID)

    def _mine(self):
        return [pltpu.make_async_copy(self.stage[a], self.dests[a].at[_slot(*self.me)], self.local_sems.at[a]) for a in range(self.n)]

    def _first(self, a):
        return [self._copy(a, 0, self.me, self.sibling, True), self._copy(a, 1, self.me, (*self.xn, self.c), True),
                self._copy(a, 2, self.me, (*self.yn, self.c), True)]

    def _relays(self, a):
        lo, hi = self._halves(a)
        return [self._copy(a, 3, (*self.xn, self.c), (*self.yn, self.c), rows=lo),
                self._copy(a, 4, (*self.yn, self.c), (*self.xn, self.c), rows=hi)]

    def _passes(self, a):
        return [self._copy(a, 5 + j, (*chip, self.c), self.sibling) for j, chip in enumerate((self.xn, self.yn, self.diag))]

    def start(self):
        for a in range(self.n):
            self.stage[a][...] = self.ins[a][...].astype(BF16)
        for cp in self._mine():
            cp.start()
        for a in range(self.n):
            for cp in self._first(a):
                cp.start()

    def forward(self):
        for a in range(self.n):
            relays, passes = self._relays(a), self._passes(a)
            for j, chip in enumerate((self.xn, self.yn)):
                self._copy(a, 1 + j, (*chip, self.c), self.me).wait_recv()
                relays[j].start()
                passes[j].start()

    def finish(self):
        for a in range(self.n):
            lo, hi = self._halves(a)
            self._copy(a, 3, (*self.diag, self.c), self.me, rows=lo).wait_recv()
            self._copy(a, 4, (*self.diag, self.c), self.me, rows=hi).wait_recv()
            self._passes(a)[2].start()
        for a in range(self.n):
            self._copy(a, 0, self.sibling, self.me).wait_recv()
            for j, chip in enumerate((self.xn, self.yn, self.diag)):
                self._copy(a, 5 + j, (*chip, 1 - self.c), self.me).wait_recv()
            for cp in self._first(a) + self._relays(a) + self._passes(a):
                cp.wait_send()
        for cp in self._mine():
            cp.wait()


N_CHIPS = 4


def _reduce_scatter(slabs, packed):
    def body(*refs):
        n = len(slabs)
        rs = _ReduceScatter(refs[:n], refs[n], refs[n + 1:2 * n + 1], refs[2 * n + 1], refs[2 * n + 2:])
        rs.start1()
        rs.finish1_start2()
        rs.finish2()

    shapes = [s.shape[1:] for s in slabs]
    return pl.pallas_call(
        body, name="reduce_scatter_grads",
        in_specs=[HBM_SPEC] * len(slabs) + [VMEM_SPEC], out_specs=[VMEM_SPEC] * (len(slabs) + 1),
        out_shape=[SDS(s, F32) for s in shapes] + [SDS(packed.shape, F32)],
        scratch_shapes=_rs_scratch(shapes, packed.shape), compiler_params=_cp(),
    )(*slabs, packed)


def _rs_scratch(shapes, packed_shape):
    n = len(shapes)
    n1, n2 = N_CHIPS * n + 1, (N_CHIPS - 1) * (n + 1)
    dma = pltpu.SemaphoreType.DMA
    packed = [] if packed_shape is None else [pltpu.VMEM(packed_shape, F32), pltpu.VMEM((N_CHIPS,) + tuple(packed_shape), F32)]
    return ([pltpu.VMEM((N_CHIPS,) + tuple(s), BF16) for s in shapes] * 2 + [pltpu.VMEM((N_CHIPS - 1,) + tuple(s), BF16) for s in shapes] * 2
            + packed + [dma((max(N_CHIPS * n, 1),)), dma((n1,)), dma((n1,)), dma((n2,)), dma((n2,))])


class _ReduceScatter:
    def __init__(self, slab_refs, packed_ref, out_refs, ptot_ref, scratch):
        n = self.n = len(slab_refs)
        self.slabs, self.packed, self.outs, self.ptot = slab_refs, packed_ref, out_refs, ptot_ref
        self.own1, self.land1, self.send2, self.land2 = (scratch[k * n:(k + 1) * n] for k in range(4))
        rest = scratch[4 * n:]
        if packed_ref is not None:
            self.pland1, self.pland2 = rest[:2]
            rest = rest[2:]
        self.loc_sems, self.send1_sems, self.recv1_sems, self.send2_sems, self.recv2_sems = rest
        self.x, self.y, self.c = _mesh_pos()

    def _chip(self, r):
        return (1 - self.x if r & 2 else self.x, 1 - self.y if r & 1 else self.y)

    @staticmethod
    def _remote(src, dst, send_sem, recv_sem, to):
        return pltpu.make_async_remote_copy(src_ref=src, dst_ref=dst, send_sem=send_sem, recv_sem=recv_sem, device_id=to,
                                            device_id_type=MESH_ID)

    def _copies1(self):
        c, sibling = self.c, (self.x, self.y, 1 - self.c)
        cps = []
        for a in range(self.n):
            for r in range(N_CHIPS):
                k = N_CHIPS * a + r
                cps.append(pltpu.make_async_copy(self.slabs[a].at[_slot(*self._chip(r), c)], self.own1[a].at[r], self.loc_sems.at[k]))
                cps.append(self._remote(self.slabs[a].at[_slot(*self._chip(r), 1 - c)], self.land1[a].at[r],
                                        self.send1_sems.at[k], self.recv1_sems.at[k], sibling))
        if self.packed is not None:
            k = N_CHIPS * self.n
            cps.append(self._remote(self.packed, self.pland1, self.send1_sems.at[k], self.recv1_sems.at[k], sibling))
        return cps

    def _copies2(self):
        cps = []
        for a in range(self.n):
            for r in range(1, N_CHIPS):
                k = (N_CHIPS - 1) * a + r - 1
                cps.append(self._remote(self.send2[a].at[r - 1], self.land2[a].at[r - 1], self.send2_sems.at[k], self.recv2_sems.at[k],
                                        (*self._chip(r), self.c)))
        if self.packed is not None:
            for r in range(1, N_CHIPS):
                k = (N_CHIPS - 1) * self.n + r - 1
                cps.append(self._remote(self.pland2.at[0], self.pland2.at[r], self.send2_sems.at[k], self.recv2_sems.at[k],
                                        (*self._chip(r), self.c)))
        return cps

    def start1(self):
        for cp in self._copies1():
            cp.start()

    def finish1_start2(self):
        for cp in self._copies1():
            cp.wait()
        for a in range(self.n):
            self.outs[a][...] = self.own1[a][0].astype(F32) + self.land1[a][0].astype(F32)
            for r in range(1, N_CHIPS):
                self.send2[a][r - 1] = (self.own1[a][r].astype(F32) + self.land1[a][r].astype(F32)).astype(BF16)
        if self.packed is not None:
            self.pland2[0] = self.packed[...] + self.pland1[...]
        for cp in self._copies2():
            cp.start()

    def finish2(self):
        for cp in self._copies2():
            cp.wait()
        for a in range(self.n):
            l2 = self.land2[a]
            self.outs[a][...] = self.outs[a][...] + ((l2[0].astype(F32) + l2[1].astype(F32)) + l2[2].astype(F32))
        if self.packed is not None:
            p2 = self.pland2
            self.ptot[...] = (p2[0] + p2[1]) + (p2[2] + p2[3])


def _adamw(ws, gs, ms, vs):
    n = len(ws)

    def body(*refs):
        for k in range(n):
            w, g, m, v = (refs[j * n + k][...] for j in range(4))
            d_ref, nm_ref, nv_ref = (refs[(4 + j) * n + k] for j in range(3))
            m = ADAM_B1 * m + (1.0 - ADAM_B1) * g
            v = ADAM_B2 * v + (1.0 - ADAM_B2) * jnp.square(g)
            m_hat = m / (1.0 - ADAM_B1 ** ADAM_STEP)
            v_hat = v / (1.0 - ADAM_B2 ** ADAM_STEP)
            d_ref[...] = -ADAM_LR * (m_hat / (jnp.sqrt(v_hat) + ADAM_EPS) + ADAM_WD * w)
            nm_ref[...] = m
            nv_ref[...] = v

    outs = pl.pallas_call(
        body, name="adamw", in_specs=[VMEM_SPEC] * (4 * n), out_specs=[VMEM_SPEC] * (3 * n),
        out_shape=[SDS(w.shape, F32) for w in ws] * 3, compiler_params=_cp(),
    )(*ws, *gs, *ms, *vs)
    return outs[:n], outs[n:2 * n], outs[2 * n:]


WEIGHTS = ("norm_in", "w_in", "q_norm", "w_uq", "kv_norm", "w_ukv", "pool_w", "pool_scale", "w_branch_attn", "w_branch_pool",
           "w_out", "norm_final")
SUBLANES = 8


def _pack_rows(a):
    a = a.reshape(-1, LANES)
    return jnp.pad(a, ((0, -a.shape[0] % SUBLANES), (0, 0)))


def _unpack_rows(packed, like):
    out, row = [], 0
    for a in like:
        rows = a.size // LANES
        out.append(packed[row:row + rows].reshape(a.shape))
        row += rows + (-rows % SUBLANES)
    return out


def kernel(x, norm_in, w_in, q_norm, w_uq, kv_norm, w_ukv, pool_w, pool_scale, w_branch_attn, w_branch_pool, w_out, norm_final, loss_target, m_norm_in, m_w_in, m_q_norm, m_w_uq, m_kv_norm, m_w_ukv, m_pool_w, m_pool_scale, m_w_branch_attn, m_w_branch_pool, m_w_out, m_norm_final, v_norm_in, v_w_in, v_q_norm, v_w_uq, v_kv_norm, v_w_ukv, v_pool_w, v_pool_scale, v_w_branch_attn, v_w_branch_pool, v_w_out, v_norm_final):
    w = dict(norm_in=norm_in, w_in=w_in, q_norm=q_norm, w_uq=w_uq, kv_norm=kv_norm, w_ukv=w_ukv, pool_w=pool_w, pool_scale=pool_scale,
             w_branch_attn=w_branch_attn, w_branch_pool=w_branch_pool, w_out=w_out, norm_final=norm_final)
    m = dict(norm_in=m_norm_in, w_in=m_w_in, q_norm=m_q_norm, w_uq=m_w_uq, kv_norm=m_kv_norm, w_ukv=m_w_ukv, pool_w=m_pool_w,
             pool_scale=m_pool_scale, w_branch_attn=m_w_branch_attn, w_branch_pool=m_w_branch_pool, w_out=m_w_out, norm_final=m_norm_final)
    v = dict(norm_in=v_norm_in, w_in=v_w_in, q_norm=v_q_norm, w_uq=v_w_uq, kv_norm=v_kv_norm, w_ukv=v_w_ukv, pool_w=v_pool_w,
             pool_scale=v_pool_scale, w_branch_attn=v_w_branch_attn, w_branch_pool=v_w_branch_pool, w_out=v_w_out, norm_final=v_norm_final)

    def as2d(name, a):
        if name == "w_in":
            return a.T
        if name == "w_uq":
            return a
        if name == "w_ukv":
            return a.reshape(a.shape[0], -1)
        if name == "pool_w":
            return a.reshape(-1, GROUP)
        return a.reshape(1, -1) if a.ndim == 1 else a

    def unshape(name, a):
        return a.T if name == "w_in" else a.reshape(w[name].shape)

    w_in_pad, w_uq_full, w_ukv_full = _all_gather_bf16([as2d(k, w[k]) for k in ("w_in", "w_uq", "w_ukv")])
    loss, grad_x, g2d = _local_step(
        x.reshape(x.shape[1:]), loss_target.reshape(x.shape[1:]), norm_in, w_in_pad, q_norm,
        w_uq_full.reshape(Q_RANK, HEADS, NOPE + ROPE), kv_norm, w_ukv_full.reshape(KV_RANK, HEADS, NOPE + VDIM),
        pool_w, pool_scale, [w_branch_attn, w_branch_pool, w_out], norm_final)

    deltas, new_m, new_v = _adamw([as2d(k, w[k]) for k in WEIGHTS], [g2d[k] for k in WEIGHTS],
                                  [as2d(k, m[k]) for k in WEIGHTS], [as2d(k, v[k]) for k in WEIGHTS])
    shaped = lambda arrs: [unshape(k, a) for k, a in zip(WEIGHTS, arrs)]
    return (loss, grad_x.reshape(x.shape), *shaped([g2d[k] for k in WEIGHTS]), *shaped(deltas), *shaped(new_m), *shaped(new_v))
```

```python
import functools

import jax
import jax.numpy as jnp
import numpy as np
from jax import lax
from jax.experimental import pallas as pl
from jax.experimental.pallas import tpu as pltpu

F32 = jnp.float32
BF16 = jnp.bfloat16
SDS = jax.ShapeDtypeStruct

D_MODEL = 1024
HEADS = 8
NOPE = 64
ROPE = 32
VDIM = 64
Q_RANK = 384
KV_RANK = 256
MLA_W = HEADS * VDIM
POOL_W = 512
POOL_GROUPS = 4
GROUP = POOL_W // POOL_GROUPS
CHUNK = 64
ROPE_THETA = 10000.0
EPS = 1e-6
SCALE = (NOPE + ROPE) ** -0.5
LOG2E = 1.4426950408889634
LN2 = 0.6931471805599453
QK_SCALE_LOG2 = SCALE * LOG2E
IN_TOTAL = 4256
ADAM_LR, ADAM_B1, ADAM_B2, ADAM_EPS, ADAM_WD, ADAM_STEP = 0.001, 0.9, 0.999, 1e-08, 0.01, 10

N_DEV = 8
LANES = 128
HEAD_PAD = LANES
HW = HEADS * HEAD_PAD

ZQ, ZKV, ZKR, GA, UP, GP, GM, ZTOT = 0, 384, 640, 768, 1280, 1792, 2304, 4352
FRONT_W = GA
ZKR_ORIG = 640

VMEM_LIMIT = 56 * 1024 * 1024


def _cp(sem=None, **kw):
    if sem is not None:
        kw["dimension_semantics"] = sem
    return pltpu.CompilerParams(vmem_limit_bytes=VMEM_LIMIT, **kw)


def _mm(a, b):
    return lax.dot_general(a, b, (((1,), (0,)), ((), ())), preferred_element_type=F32)


def _mm_nt(a, b):
    return lax.dot_general(a, b, (((1,), (1,)), ((), ())), preferred_element_type=F32)


def _mm_tn(a, b):
    return lax.dot_general(a, b, (((0,), (0,)), ((), ())), preferred_element_type=F32)


def _row_spec(tm, w):
    return pl.BlockSpec((tm, w), lambda i: (i, 0))


def _full_spec(a):
    nd = len(a.shape)
    return pl.BlockSpec(a.shape, lambda *_: (0,) * nd)


def _rope(v, c, sa, sb, sign):
    n = v.shape[-1]
    reps = n // LANES
    if reps > 1:
        c, sa, sb = (jnp.tile(t, (1, reps)) for t in (c, sa, sb))
    up = pltpu.roll(v, n - ROPE // 2, 1)
    dn = pltpu.roll(v, ROPE // 2, 1)
    return v * c + sign * (up * sa + dn * sb)


def _rope_tables(T):
    half = ROPE // 2
    inv_freq = np.float32(ROPE_THETA) ** (-np.arange(half, dtype=np.float32) / np.float32(half))
    ang = np.arange(T, dtype=np.float32)[:, None] * inv_freq[None, :].astype(np.float32)
    cos, sin = np.cos(ang.astype(np.float64)).astype(np.float32), np.sin(ang.astype(np.float64)).astype(np.float32)
    z16 = np.zeros((T, half), np.float32)
    z32 = np.zeros((T, LANES - NOPE - ROPE), np.float32)
    c = np.concatenate([np.ones((T, NOPE), np.float32), cos, cos, z32], axis=1)
    sa = np.concatenate([np.zeros((T, NOPE), np.float32), -sin, z16, z32], axis=1)
    sb = np.concatenate([np.zeros((T, NOPE), np.float32), z16, sin, z32], axis=1)
    return jnp.asarray(c), jnp.asarray(sa), jnp.asarray(sb)


def _silu_parts(g):
    sg = jax.nn.sigmoid(g)
    return g * sg, sg + g * sg * (1.0 - sg)


def _in_proj(x2, norm_in, w_in_pad, q_norm, wuq_pad, kv_norm, wukv, rc, rsa, rsb, tm, late_shards):
    T = x2.shape[0]
    steps = T // tm
    n = len(late_shards)

    def body(x_ref, nin_ref, win_ref, qn_ref, wuq_ref, kvn_ref, wukv_ref, c_ref, sa_ref, sb_ref, *rest):
        hn_ref, zgm_ref, zga_ref, zup_ref, zgp_ref, zfr_ref, q_ref, k_ref, v_ref, vt_ref = rest[n:n + 10]
        ag = _AllGather(rest[:n], rest[n + 10:2 * n + 10], rest[2 * n + 10:])
        step = pl.program_id(0)
        pl.when(step == 0)(ag.start)
        pl.when(step == min(5, steps - 1))(ag.forward)
        xf = x_ref[...]
        r = lax.rsqrt(jnp.mean(xf * xf, axis=-1, keepdims=True) + EPS)
        hn = (xf * r * nin_ref[...]).astype(BF16)
        hn_ref[...] = hn
        z = _mm_nt(hn, win_ref[...])
        zgm_ref[...] = z[:, GM:ZTOT]
        zga_ref[...] = z[:, GA:UP]
        zup_ref[...] = z[:, UP:GP]
        zgp_ref[...] = z[:, GP:GM]
        zfr_ref[...] = z[:, ZQ:GA]
        zq, zkv, zkr = z[:, ZQ:ZKV], z[:, ZKV:ZKR], z[:, ZKR:GA]
        c, sa, sb = c_ref[...], sa_ref[...], sb_ref[...]
        rq = lax.rsqrt(jnp.mean(zq * zq, axis=-1, keepdims=True) + EPS)
        cq = (zq * rq * qn_ref[...]).astype(BF16)
        q = _rope(_mm(cq, wuq_ref[...]), c, sa, sb, 1.0)
        q_ref[...] = (q * QK_SCALE_LOG2).astype(BF16)
        rkv = lax.rsqrt(jnp.mean(zkv * zkv, axis=-1, keepdims=True) + EPS)
        ckv = (zkv * rkv * kvn_ref[...]).astype(BF16)
        kv = _mm(ckv, wukv_ref[...])
        kr = _rope(zkr, c, sa, sb, 1.0)
        lane = lax.broadcasted_iota(jnp.int32, kv.shape, 1) % LANES
        k_ref[...] = jnp.where(lane < NOPE, kv, jnp.tile(kr, (1, HEADS))).astype(BF16)
        v = jnp.where(lane < NOPE, 1.0, kv).astype(BF16)
        v_ref[...] = v
        vt_ref[...] = v.T
        pl.when(step == steps - 1)(ag.finish)

    ins = (x2, norm_in, w_in_pad, q_norm, wuq_pad, kv_norm, wukv, rc, rsa, rsb)
    in_specs = [_row_spec(tm, D_MODEL), _full_spec(norm_in), _full_spec(w_in_pad), _full_spec(q_norm), _full_spec(wuq_pad),
                _full_spec(kv_norm), _full_spec(wukv), _row_spec(tm, LANES), _row_spec(tm, LANES), _row_spec(tm, LANES)]
    widths = [(D_MODEL, BF16), (ZTOT - GM, F32), (UP - GA, F32), (GP - UP, F32), (GM - GP, F32), (FRONT_W, F32),
              (HW, BF16), (HW, BF16), (HW, BF16)]
    return pl.pallas_call(
        body, name="in_proj", grid=(steps,), in_specs=in_specs + [_full_spec(s) for s in late_shards],
        out_specs=[_row_spec(tm, w) for w, _ in widths] + [pl.BlockSpec((HW, tm), lambda i: (0, i))] + [HBM_SPEC] * n,
        out_shape=[SDS((T, w), dt) for w, dt in widths] + [SDS((HW, T), BF16)]
        + [SDS((N_DEV,) + s.shape, BF16) for s in late_shards],
        scratch_shapes=_ag_scratch([s.shape for s in late_shards]), compiler_params=_cp(("arbitrary",)),
    )(*ins, *late_shards)


def _chunk_mask(n_q, n_k, q_off, transposed):
    shape = (n_k, n_q) if transposed else (n_q, n_k)
    q = (lax.broadcasted_iota(jnp.int32, shape, 1 if transposed else 0) + q_off) // CHUNK
    k = lax.broadcasted_iota(jnp.int32, shape, 0 if transposed else 1) // CHUNK
    return k <= q


def _store_pair_rows(ref, k, pair):
    t = pair.T
    ref[k, 0:1, :] = t[0:1, :]
    ref[k, 1:2, :] = t[VDIM:VDIM + 1, :]


def _attn_fwd(q_att, k_att, vt_att, tq, hps):
    T = q_att.shape[0]
    head_lanes = [slice(h * LANES, (h + 1) * LANES) for h in range(hps)]

    def body(q_ref, k_ref, vt_ref, o_ref, lser_ref):
        i = pl.program_id(1)
        mask = _chunk_mask(tq, tq, 0, True)
        lane = lax.broadcasted_iota(jnp.int32, (tq, LANES), 1)
        qs = [q_ref[:, hs] for hs in head_lanes]

        def step(j, carry, masked):
            off = pl.multiple_of(j * tq, tq)
            sts = [_mm_nt(k_ref[pl.ds(off, tq), hs], qh) for qh, hs in zip(qs, head_lanes)]
            if masked:
                sts = [jnp.where(mask, st, -jnp.inf) for st in sts]
            ms = [jnp.maximum(m, jnp.max(st, axis=0, keepdims=True)) for (m, _), st in zip(carry, sts)]
            pts = [jnp.exp2(st - m_new).astype(BF16) for st, m_new in zip(sts, ms)]
            return tuple((m_new, jnp.exp2(m - m_new) * acc + _mm(vt_ref[hs, pl.ds(off, tq)], pt))
                         for (m, acc), m_new, pt, hs in zip(carry, ms, pts, head_lanes))

        init = ((jnp.full((1, tq), -jnp.inf, F32), jnp.zeros((LANES, tq), F32)),) * hps
        res = step(i, lax.fori_loop(0, i, functools.partial(step, masked=False), init), True)
        for pair in range(hps // 2):
            (ma, acca), (mb, accb) = res[2 * pair], res[2 * pair + 1]
            la, lb = acca[:1], accb[:1]
            oa, ob = (acca / la).T, (accb / lb).T
            o_ref[:, pair * LANES:(pair + 1) * LANES] = jnp.where(lane < VDIM, pltpu.roll(oa, VDIM, 1), ob)
            lser_ref[pair, 0:1, :] = ma + jnp.log2(la)
            lser_ref[pair, 1:2, :] = mb + jnp.log2(lb)

    qspec = pl.BlockSpec((tq, hps * LANES), lambda p, i: (i, p))
    kspec = pl.BlockSpec((T, hps * LANES), lambda p, i: (0, p))
    vspec = pl.BlockSpec((hps * LANES, T), lambda p, i: (p, 0))
    ospec = pl.BlockSpec((tq, hps * VDIM), lambda p, i: (i, p))
    return pl.pallas_call(
        body, name="attn_fwd", grid=(HEADS // hps, T // tq), in_specs=[qspec, kspec, vspec],
        out_specs=[ospec, pl.BlockSpec((hps // 2, 2, tq), lambda p, i: (p, 0, i))],
        out_shape=[SDS((T, MLA_W), F32), SDS((HEADS // 2, 2, T), F32)],
        compiler_params=_cp(("parallel", "parallel")),
    )(q_att, k_att, vt_att)


def _pick(g, vals):
    out = vals[-1]
    for k in range(len(vals) - 2, -1, -1):
        out = jnp.where(g == k, vals[k], out)
    return out


def _window_sum(u, g, forward):
    T = u.shape[0]
    row = lax.broadcasted_iota(jnp.int32, u.shape, 0)

    def sh(s, k):
        if forward:
            return jnp.where(row >= k, pltpu.roll(s, k, 0), 0.0)
        return jnp.where(row < T - k, pltpu.roll(s, T - k, 0), 0.0)

    sums, s = [], u
    for k in (1, 2, 4, 8):
        s = s + sh(s, k)
        sums.append(s)
    return _pick(g, sums)


def _pool_count(shape, g):
    row = lax.broadcasted_iota(jnp.int32, shape, 0)
    return jnp.minimum(row + 1, lax.shift_left(jnp.int32(2), g)).astype(F32)


def _pool_fwd(zup, zgp, pool_w, pool_scale):
    T = zup.shape[0]

    def body(u_ref, g_ref, w_ref, sc_ref, y_ref):
        g = pl.program_id(0)
        u = u_ref[...]
        d = _window_sum(u, g, True) / _pool_count(u.shape, g) - u
        lin = _mm(d.astype(BF16), w_ref[0].astype(BF16))
        silu, _ = _silu_parts(g_ref[...])
        y_ref[...] = (lin * sc_ref[...] * silu).astype(BF16)

    col = pl.BlockSpec((T, GROUP), lambda g: (0, g))
    return pl.pallas_call(
        body, name="pool_fwd", grid=(POOL_GROUPS,),
        in_specs=[col, col, pl.BlockSpec((1, GROUP, GROUP), lambda g: (g, 0, 0)), pl.BlockSpec((1, GROUP), lambda g: (0, g))],
        out_specs=col, out_shape=SDS((T, POOL_W), BF16), compiler_params=_cp(("parallel",)),
    )(zup, zgp, pool_w, pool_scale)


def _pool_bwd(zup, zgp, dyp, pool_w, pool_scale):
    T = zup.shape[0]

    def body(u_ref, g_ref, dy_ref, w_ref, sc_ref, du_ref, dg_ref, gw_ref, gsc_ref):
        g = pl.program_id(0)
        u = u_ref[...]
        cnt = _pool_count(u.shape, g)
        d = (_window_sum(u, g, True) / cnt - u).astype(BF16)
        wb = w_ref[0].astype(BF16)
        lin = _mm(d, wb)
        sc = sc_ref[...]
        silu, dsilu = _silu_parts(g_ref[...])
        dy = dy_ref[...]
        dg_ref[...] = (dy * lin * sc * dsilu).astype(BF16)
        dpre = dy * silu
        gsc_ref[...] = jnp.sum(dpre * lin, axis=0, keepdims=True)
        dlin = (dpre * sc).astype(BF16)
        gw_ref[0] = _mm_tn(d, dlin)
        dd = _mm_nt(dlin, wb)
        du_ref[...] = (_window_sum(dd / cnt, g, False) - dd).astype(BF16)

    col = pl.BlockSpec((T, GROUP), lambda g: (0, g))
    wspec = pl.BlockSpec((1, GROUP, GROUP), lambda g: (g, 0, 0))
    vspec = pl.BlockSpec((1, GROUP), lambda g: (0, g))
    return pl.pallas_call(
        body, name="pool_bwd", grid=(POOL_GROUPS,), in_specs=[col, col, col, wspec, vspec], out_specs=[col, col, wspec, vspec],
        out_shape=[SDS((T, POOL_W), BF16), SDS((T, POOL_W), BF16), SDS((POOL_GROUPS, GROUP, GROUP), F32), SDS((1, POOL_W), F32)],
        compiler_params=_cp(("parallel",)),
    )(zup, zgp, dyp, pool_w, pool_scale)


def _tail(x2, tgt, o, zga, ypool, zgm, wba, wbp, wout, norm_final, tm):
    T = x2.shape[0]
    steps = T // tm
    cols = D_MODEL // N_DEV

    def body(x_ref, tgt_ref, o_ref, zga_ref, yp_ref, zgm_ref, wba_ref, wbp_ref, wout_ref, nf_ref,
             loss_ref, dh_ref, dgm_ref, doop_ref, dga_ref, dcapr_ref, dyp_ref, swout_ref, swba_ref, swbp_ref, gnf_ref,
             gwout_ref, gwba_ref, gwbp_ref):
        @pl.when(pl.program_id(0) == 0)
        def _():
            for ref in (loss_ref, gwout_ref, gwba_ref, gwbp_ref, gnf_ref):
                ref[...] = jnp.zeros_like(ref)

        o_v = o_ref[...]
        silu, dsilu = _silu_parts(zga_ref[...])
        ya = (o_v * silu).astype(BF16)
        yp = yp_ref[...]
        wba_v = jnp.concatenate([wba_ref[k] for k in range(N_DEV)], axis=1)
        wbp_v = jnp.concatenate([wbp_ref[k] for k in range(N_DEV)], axis=1)
        wout_v = wout_ref[...]
        a = _mm(ya, wba_v)
        p = _mm(yp, wbp_v)
        gate = jax.nn.sigmoid(zgm_ref[...])
        ga, gp = gate[:, :D_MODEL], gate[:, D_MODEL:]
        mg = (ga * a + gp * p).astype(BF16)
        h = x_ref[...] + _mm(mg, wout_v)
        r = lax.rsqrt(jnp.mean(h * h, axis=-1, keepdims=True) + EPS)
        gf = nf_ref[...]
        hr = h * r
        e = hr * gf - tgt_ref[...]
        loss_ref[...] += (0.5 / D_MODEL) * jnp.sum(e * e)
        dy = e * (1.0 / D_MODEL)
        gnf_ref[...] += jnp.sum(dy * hr, axis=0, keepdims=True)
        u = dy * gf
        dh = r * (u - hr * jnp.mean(u * hr, axis=-1, keepdims=True))
        dh_ref[...] = dh
        dhb = dh.astype(BF16)
        dmg = _mm_nt(dhb, wout_v)
        gwout_ref[...] += _mm_tn(mg, dhb)
        dgm_ref[:, :D_MODEL] = (dmg * a * ga * (1.0 - ga)).astype(BF16)
        dgm_ref[:, D_MODEL:] = (dmg * p * gp * (1.0 - gp)).astype(BF16)
        dab = (dmg * ga).astype(BF16)
        dpb = (dmg * gp).astype(BF16)
        dya = _mm_nt(dab, wba_v)
        gwba_ref[...] += _mm_tn(ya, dab)
        dyp_ref[...] = _mm_nt(dpb, wbp_v)
        gwbp_ref[...] += _mm_tn(yp, dpb)
        do = dya * silu
        dga_ref[...] = (dya * o_v * dsilu).astype(BF16)
        prod = do * o_v
        lo = lax.broadcasted_iota(jnp.int32, (tm, LANES), 1) < VDIM
        for pair in range(HEADS // 2):
            ls = slice(pair * LANES, (pair + 1) * LANES)
            do_p, prod_p = do[:, ls], prod[:, ls]
            dcap_a = jnp.sum(jnp.where(lo, prod_p, 0.0), axis=-1, keepdims=True)
            dcap_b = jnp.sum(jnp.where(lo, 0.0, prod_p), axis=-1, keepdims=True)
            _store_pair_rows(dcapr_ref, pair, jnp.where(lo, dcap_a, dcap_b))
            doop_ref[:, 2 * pair * LANES:(2 * pair + 1) * LANES] = jnp.where(lo, 0.0, pltpu.roll(do_p, VDIM, 1)).astype(BF16)
            doop_ref[:, (2 * pair + 1) * LANES:(2 * pair + 2) * LANES] = jnp.where(lo, 0.0, do_p).astype(BF16)

        @pl.when(pl.program_id(0) == steps - 1)
        def _():
            for k in range(N_DEV):
                swout_ref[k] = gwout_ref[k * cols:(k + 1) * cols, :].astype(BF16)
                swba_ref[k] = gwba_ref[:, k * cols:(k + 1) * cols].astype(BF16)
                swbp_ref[k] = gwbp_ref[:, k * cols:(k + 1) * cols].astype(BF16)

    ins = (x2, tgt, o, zga, ypool, zgm, wba, wbp, wout, norm_final)
    in_specs = [_row_spec(tm, D_MODEL), _row_spec(tm, D_MODEL), _row_spec(tm, MLA_W), _row_spec(tm, MLA_W), _row_spec(tm, POOL_W),
                _row_spec(tm, 2 * D_MODEL), _full_spec(wba), _full_spec(wbp), _full_spec(wout), _full_spec(norm_final)]
    outs = [SDS((8, LANES), F32), SDS((T, D_MODEL), F32), SDS((T, 2 * D_MODEL), BF16), SDS((T, HW), BF16), SDS((T, MLA_W), BF16),
            SDS((HEADS // 2, 2, T), F32), SDS((T, POOL_W), F32),
            SDS((N_DEV, cols, D_MODEL), BF16), SDS((N_DEV, MLA_W, cols), BF16), SDS((N_DEV, POOL_W, cols), BF16), SDS((1, D_MODEL), F32)]
    out_specs = [_full_spec(outs[0]), _row_spec(tm, D_MODEL), _row_spec(tm, 2 * D_MODEL), _row_spec(tm, HW), _row_spec(tm, MLA_W),
                 pl.BlockSpec((HEADS // 2, 2, tm), lambda i: (0, 0, i)), _row_spec(tm, POOL_W),
                 _full_spec(outs[7]), _full_spec(outs[8]), _full_spec(outs[9]), _full_spec(outs[10])]
    return pl.pallas_call(
        body, name="tail", grid=(steps,), in_specs=in_specs, out_specs=out_specs, out_shape=outs,
        scratch_shapes=[pltpu.VMEM((D_MODEL, D_MODEL), F32), pltpu.VMEM((MLA_W, D_MODEL), F32), pltpu.VMEM((POOL_W, D_MODEL), F32)],
        compiler_params=_cp(("arbitrary",)),
    )(*ins)


def _attn_bwd(q_att, k_att, v_att, doop, lse_rows, dcap_rows, tq, hps, slabs, packed):
    T = q_att.shape[0]
    nq = T // tq
    n = len(slabs)
    groups = HEADS // hps
    head_lanes = [slice(h * LANES, (h + 1) * LANES) for h in range(hps)]

    def body(q_ref, k_ref, v_ref, doop_ref, lse_ref, dcap_ref, *rest):
        slab_refs, packed_ref = rest[:n], rest[n]
        dq_ref, dkv_ref, dkr_ref = rest[n + 1:n + 4]
        sum_refs, ptot_ref = rest[n + 4:2 * n + 4], rest[2 * n + 4]
        dq_acc = rest[2 * n + 5]
        rs = _ReduceScatter(slab_refs, packed_ref, sum_refs, ptot_ref, rest[2 * n + 6:])
        group, j = pl.program_id(0), pl.program_id(1)
        pl.when((group == 0) & (j == 0))(rs.start1)
        pl.when((group == 1) & (j == 0))(rs.finish1_start2)
        pl.when((group == groups - 1) & (j == 0))(rs.relay2)
        mask = _chunk_mask(tq, tq, 0, True)
        lane = lax.broadcasted_iota(jnp.int32, (tq, LANES), 1)
        ks = [k_ref[:, hs] for hs in head_lanes]
        vs = [v_ref[:, hs] for hs in head_lanes]
        kts = [kh.T for kh in ks]

        @pl.when(j == 0)
        def _():
            dq_acc[...] = jnp.zeros_like(dq_acc)

        def step(i, carry, masked):
            rows = pl.ds(pl.multiple_of(i * tq, tq), tq)
            heads = range(hps)
            stat = lambda h: (h // 2, slice(h % 2, h % 2 + 1), rows)
            qhs = [q_ref[rows, hs] for hs in head_lanes]
            doops = [doop_ref[rows, hs] for hs in head_lanes]
            sts = [_mm_nt(ks[h], qhs[h]) for h in heads]
            dpts = [_mm_nt(vs[h], doops[h]) for h in heads]
            pts = [jnp.exp2(sts[h] - lse_ref[stat(h)]) for h in heads]
            if masked:
                pts = [jnp.where(mask, pt, 0.0) for pt in pts]
            dsts = [(pts[h] * (dpts[h] - dcap_ref[stat(h)])).astype(BF16) for h in heads]
            dvs = [_mm(pts[h].astype(BF16), doops[h]) for h in heads]
            dks = [_mm(dsts[h], qhs[h]) for h in heads]
            for h, hs in enumerate(head_lanes):
                dq_acc[hs, rows] += _mm(kts[h], dsts[h])
            return tuple((dk + dks[h], dv + dvs[h]) for h, (dk, dv) in enumerate(carry))

        zero = jnp.zeros((tq, LANES), F32)
        carry = step(j, ((zero, zero),) * hps, True)
        res = lax.fori_loop(j + 1, nq, functools.partial(step, masked=False), carry)
        dkr = None
        for (dk, dv), hs in zip(res, head_lanes):
            dk = dk * LN2
            dkv_ref[:, hs] = jnp.where(lane < NOPE, dk, dv).astype(BF16)
            dkr = dk if dkr is None else dkr + dk
        dkr_ref[0] = jnp.where((lane >= NOPE) & (lane < NOPE + ROPE), dkr, 0.0)

        @pl.when(j == nq - 1)
        def _():
            dq_ref[...] = (dq_acc[...] * SCALE).T.astype(BF16)

        pl.when((group == groups - 1) & (j == nq - 1))(rs.finish2)

    kspec = pl.BlockSpec((tq, hps * LANES), lambda p, j: (j, p))
    qspec = pl.BlockSpec((T, hps * LANES), lambda p, j: (0, p))
    rspec = pl.BlockSpec((hps // 2, 2, T), lambda p, j: (p, 0, 0))
    sums = [SDS(s.shape[1:], F32) for s in slabs] + [SDS(packed.shape, F32)]
    return pl.pallas_call(
        body, name="attn_bwd", grid=(groups, nq),
        in_specs=[qspec, kspec, kspec, qspec, rspec, rspec] + [HBM_SPEC] * n + [_full_spec(packed)],
        out_specs=[qspec, kspec, pl.BlockSpec((1, tq, LANES), lambda p, j: (p, j, 0))] + [_full_spec(s) for s in sums],
        out_shape=[SDS((T, HW), BF16), SDS((T, HW), BF16), SDS((groups, T, LANES), F32)] + sums,
        scratch_shapes=[pltpu.VMEM((hps * LANES, T), F32)] + _rs_scratch([s.shape for s in sums[:-1]], packed.shape),
        compiler_params=_cp(("arbitrary", "arbitrary")),
    )(q_att, k_att, v_att, doop, lse_rows, dcap_rows, *slabs, packed)


def _rms_bwd(z, gain, dout):
    r = lax.rsqrt(jnp.mean(z * z, axis=-1, keepdims=True) + EPS)
    zr = z * r
    u = dout * gain
    return r * (u - zr * jnp.mean(u * zr, axis=-1, keepdims=True)), jnp.sum(dout * zr, axis=0, keepdims=True)


def _mla_bwd(dq_att, dkv_nat, dkr4, zfr, q_norm, wuq_pad, kv_norm, wukv, rc, rsa, rsb, tm):
    T = dq_att.shape[0]

    def body(dq_ref, dkv_ref, dkr_ref, zfr_ref, qn_ref, wuq_ref, kvn_ref, wukv_ref, c_ref, sa_ref, sb_ref,
             dfr_ref, gwuq_ref, gwukv_ref, gqn_ref, gkvn_ref):
        @pl.when(pl.program_id(0) == 0)
        def _():
            for ref in (gwuq_ref, gwukv_ref, gqn_ref, gkvn_ref):
                ref[...] = jnp.zeros_like(ref)

        c, sa, sb = c_ref[...], sa_ref[...], sb_ref[...]
        zq, zkv = zfr_ref[:, :Q_RANK], zfr_ref[:, Q_RANK:Q_RANK + KV_RANK]
        qn, kvn = qn_ref[...], kvn_ref[...]
        cq = (zq * lax.rsqrt(jnp.mean(zq * zq, axis=-1, keepdims=True) + EPS) * qn).astype(BF16)
        ckv = (zkv * lax.rsqrt(jnp.mean(zkv * zkv, axis=-1, keepdims=True) + EPS) * kvn).astype(BF16)
        dq = _rope(dq_ref[...].astype(F32), c, sa, sb, -1.0).astype(BF16)
        gwuq_ref[...] += _mm_tn(cq, dq)
        dzq, gqn = _rms_bwd(zq, qn, _mm_nt(dq, wuq_ref[...]))
        gqn_ref[...] += gqn
        dkv = dkv_ref[...]
        gwukv_ref[...] += _mm_tn(ckv, dkv)
        dzkv, gkvn = _rms_bwd(zkv, kvn, _mm_nt(dkv, wukv_ref[...]))
        gkvn_ref[...] += gkvn
        dkr = functools.reduce(lambda a, b: a + b, [dkr_ref[g] for g in range(dkr4.shape[0])])
        dfr_ref[:, :Q_RANK] = dzq.astype(BF16)
        dfr_ref[:, Q_RANK:Q_RANK + KV_RANK] = dzkv.astype(BF16)
        dfr_ref[:, Q_RANK + KV_RANK:] = _rope(dkr, c, sa, sb, -1.0).astype(BF16)

    ins = (dq_att, dkv_nat, dkr4, zfr, q_norm, wuq_pad, kv_norm, wukv, rc, rsa, rsb)
    in_specs = [_row_spec(tm, HW), _row_spec(tm, HW), pl.BlockSpec((dkr4.shape[0], tm, LANES), lambda i: (0, i, 0)), _row_spec(tm, FRONT_W),
                _full_spec(q_norm), _full_spec(wuq_pad), _full_spec(kv_norm), _full_spec(wukv),
                _row_spec(tm, LANES), _row_spec(tm, LANES), _row_spec(tm, LANES)]
    outs = [SDS((T, FRONT_W), BF16), SDS((Q_RANK, HW), F32), SDS((KV_RANK, HW), F32), SDS((1, Q_RANK), F32), SDS((1, KV_RANK), F32)]
    out_specs = [_row_spec(tm, FRONT_W)] + [_full_spec(s) for s in outs[1:]]
    return pl.pallas_call(
        body, name="mla_bwd", grid=(T // tm,), in_specs=in_specs, out_specs=out_specs, out_shape=outs,
        compiler_params=_cp(("arbitrary",)),
    )(*ins)


_DZ_COLS = ((GM, ZTOT), (GA, UP), (UP, GP), (GP, GM), (ZQ, GA))


def _in_proj_bwd_x(dzs, x2, dh, norm_in, w_in_pad, tm, slabs):
    T = x2.shape[0]
    steps = T // tm
    n = len(slabs)

    def body(d0, d1, d2, d3, d4, x_ref, dh_ref, nin_ref, win_ref, *rest):
        slab_refs, (gx_ref, gnin_ref), sum_refs = rest[:n], rest[n:n + 2], rest[n + 2:2 * n + 2]
        rs = _ReduceScatter(slab_refs, None, sum_refs, None, rest[2 * n + 2:])
        step = pl.program_id(0)

        @pl.when(step == 0)
        def _():
            gnin_ref[...] = jnp.zeros_like(gnin_ref)
            rs.start1()

        pl.when(step == min(2, steps - 1))(rs.finish1_start2)
        pl.when(step == min(steps * 11 // 16, steps - 1))(rs.relay2)
        dhn = None
        for ref, (lo, hi) in zip((d0, d1, d2, d3, d4), _DZ_COLS):
            t = _mm(ref[...], win_ref[lo:hi, :])
            dhn = t if dhn is None else dhn + t
        dx, gnin = _rms_bwd(x_ref[...], nin_ref[...], dhn)
        gnin_ref[...] += gnin
        gx_ref[...] = dx + dh_ref[...]
        pl.when(step == steps - 1)(rs.finish2)

    in_specs = [_row_spec(tm, hi - lo) for lo, hi in _DZ_COLS] + [_row_spec(tm, D_MODEL), _row_spec(tm, D_MODEL),
                                                                  _full_spec(norm_in), _full_spec(w_in_pad)] + [HBM_SPEC] * n
    sums = [SDS(s.shape[1:], F32) for s in slabs]
    outs = [SDS((T, D_MODEL), F32), SDS((1, D_MODEL), F32)] + sums
    return pl.pallas_call(
        body, name="in_proj_bwd_x", grid=(steps,), in_specs=in_specs,
        out_specs=[_row_spec(tm, D_MODEL), _full_spec(outs[1])] + [_full_spec(s) for s in sums],
        out_shape=outs, scratch_shapes=_rs_scratch([s.shape for s in sums], None), compiler_params=_cp(("arbitrary",)),
    )(*dzs, x2, dh, norm_in, w_in_pad, *slabs)


SLAB_ROWS = IN_TOTAL // N_DEV


def _slab_segments(k):
    cuts = [(0, ZKR_ORIG, 0), (ZKR_ORIG, ZKR_ORIG + ROPE, NOPE), (ZKR_ORIG + ROPE, IN_TOTAL, LANES - ROPE)]
    lo, hi = k * SLAB_ROWS, (k + 1) * SLAB_ROWS
    return [(max(lo, a) - lo, max(lo, a) + shift, min(hi, b) - max(lo, a)) for a, b, shift in cuts if min(hi, b) > max(lo, a)]


def _in_proj_bwd_w(dzs, hn, tm):
    T = hn.shape[0]
    steps = T // tm

    def body(d0, d1, d2, d3, d4, hn_ref, slab_ref, acc_ref):
        @pl.when(pl.program_id(0) == 0)
        def _():
            acc_ref[...] = jnp.zeros_like(acc_ref)

        hn_v = hn_ref[...]
        for ref, (lo, hi) in zip((d0, d1, d2, d3, d4), _DZ_COLS):
            acc_ref[lo:hi, :] += _mm_tn(ref[...], hn_v)

        @pl.when(pl.program_id(0) == steps - 1)
        def _():
            for k in range(N_DEV):
                for at, src, rows in _slab_segments(k):
                    slab_ref[k, at:at + rows, :] = acc_ref[src:src + rows, :].astype(BF16)

    in_specs = [_row_spec(tm, hi - lo) for lo, hi in _DZ_COLS] + [_row_spec(tm, D_MODEL)]
    out = SDS((N_DEV, SLAB_ROWS, D_MODEL), BF16)
    return pl.pallas_call(
        body, name="in_proj_bwd_w", grid=(steps,), in_specs=in_specs, out_specs=_full_spec(out), out_shape=out,
        scratch_shapes=[pltpu.VMEM((ZTOT, D_MODEL), F32)], compiler_params=_cp(("arbitrary",)),
    )(*dzs, hn)


def _local_step(x2, tgt, norm_in, w_in_pad, q_norm, w_uq, kv_norm, w_ukv, pool_w, pool_scale, late_shards, norm_final):
    T = x2.shape[0]
    tm = min(512, T)
    tq = min(512, T)
    row = lambda v: v.reshape(1, -1)
    wuq_pad = jnp.pad(w_uq, ((0, 0), (0, 0), (0, HEAD_PAD - NOPE - ROPE))).reshape(Q_RANK, HW)
    wukv = w_ukv.reshape(KV_RANK, HW)
    rc, rsa, rsb = _rope_tables(T)

    hn, zgm, zga, zup, zgp, zfr, q_att, k_att, v_att, vt_att, w_ba, w_bp, w_out = _in_proj(
        x2, row(norm_in), w_in_pad, row(q_norm), wuq_pad, row(kv_norm), wukv, rc, rsa, rsb, tm, late_shards)
    w_out = w_out.reshape(D_MODEL, D_MODEL)
    o, lse_rows = _attn_fwd(q_att, k_att, vt_att, tq, 4)
    ypool = _pool_fwd(zup, zgp, pool_w, row(pool_scale))
    loss8, dh, dgm, doop, dga, dcap_rows, dyp, *slabs, g_nf = _tail(
        x2, tgt, o, zga, ypool, zgm, w_ba, w_bp, w_out, row(norm_final), min(256, T))
    dup, dgp, g_pool_w, g_pool_scale = _pool_bwd(zup, zgp, dyp, pool_w, row(pool_scale))

    bf = lambda a: a.astype(BF16)
    early = [g_pool_w, g_pool_scale, g_nf, loss8[0]]
    packed = jnp.concatenate([_pack_rows(a) for a in early], axis=0)
    dq_att, dkv_nat, dkr4, s_wout, s_wba, s_wbp, tot_early = _attn_bwd(
        q_att, k_att, v_att, doop, lse_rows, dcap_rows, tq, 2, slabs, packed)
    s_pool_w, s_pool_scale, s_nf, s_loss = _unpack_rows(tot_early, early)

    dfr, g_wuq_pad, g_wukv, g_qn, g_kvn = _mla_bwd(
        dq_att, dkv_nat, dkr4, zfr, row(q_norm), wuq_pad, row(kv_norm), wukv, rc, rsa, rsb, tm)
    dzs = (dgm, dga, dup, dgp, dfr)
    slabs = [_in_proj_bwd_w(dzs, hn, tm),
             bf(g_wuq_pad.reshape(N_DEV, Q_RANK // N_DEV, HEADS, HEAD_PAD)[..., :NOPE + ROPE]),
             bf(g_wukv).reshape(N_DEV, KV_RANK // N_DEV, HW)]
    grad_x, g_nin, s_win, s_wuq, s_wukv = _in_proj_bwd_x(dzs, x2, dh, row(norm_in), w_in_pad, min(256, T), slabs)
    late = [g_nin, g_qn, g_kvn]
    (tot_late,) = _reduce_scatter([], jnp.concatenate([_pack_rows(a) for a in late], axis=0))
    s_nin, s_qn, s_kvn = _unpack_rows(tot_late, late)

    grads = dict(norm_in=s_nin, w_in=s_win, q_norm=s_qn, w_uq=s_wuq, kv_norm=s_kvn, w_ukv=s_wukv, pool_w=s_pool_w.reshape(-1, GROUP),
                 pool_scale=s_pool_scale, w_branch_attn=s_wba, w_branch_pool=s_wbp, w_out=s_wout, norm_final=s_nf)
    return s_loss[0], grad_x, grads


MESH_ID = pl.DeviceIdType.MESH
VMEM_SPEC = pl.BlockSpec(memory_space=pltpu.VMEM)
HBM_SPEC = pl.BlockSpec(memory_space=pl.ANY)


def _mesh_pos():
    return lax.axis_index("x"), lax.axis_index("y"), lax.axis_index("c")


def _slot(px, py, pc):
    return 4 * px + 2 * py + pc


def _all_gather_bf16(shards):
    n = len(shards)

    def body(*refs):
        ins, outs = refs[:n], refs[n:2 * n]
        land0, scratch = refs[2 * n], refs[2 * n + 1:]
        wpad_ref = outs[0]
        ag = _AllGather(ins, (land0,) + tuple(outs[1:]), scratch)
        ag.start()
        ag.forward()
        ag.finish()
        wpad_ref[ZKR:GA, :] = jnp.zeros((GA - ZKR, D_MODEL), BF16)
        for k in range(N_DEV):
            for at, dst, rows in _slab_segments(k):
                wpad_ref[dst:dst + rows, :] = land0[k, at:at + rows, :]

    return pl.pallas_call(
        body, name="all_gather_weights",
        in_specs=[VMEM_SPEC] * n, out_specs=[VMEM_SPEC] + [HBM_SPEC] * (n - 1),
        out_shape=[SDS((ZTOT, D_MODEL), BF16)] + [SDS((N_DEV,) + s.shape, BF16) for s in shards[1:]],
        scratch_shapes=[pltpu.VMEM((N_DEV,) + shards[0].shape, BF16)] + _ag_scratch([s.shape for s in shards]),
        compiler_params=_cp(),
    )(*shards)


def _ag_scratch(shapes):
    n = len(shapes)
    dma = pltpu.SemaphoreType.DMA
    return [pltpu.VMEM(tuple(s), BF16) for s in shapes] + [dma((_AllGather.COPIES * n,)), dma((_AllGather.COPIES * n,)), dma((n,))]


class _AllGather:
    COPIES = 8

    def __init__(self, in_refs, dest_refs, scratch):
        n = self.n = len(in_refs)
        self.ins, self.dests, self.stage = in_refs, dest_refs, scratch[:n]
        self.send_sems, self.recv_sems, self.local_sems = scratch[n:]
        x, y, c = _mesh_pos()
        self.c, self.me, self.sibling = c, (x, y, c), (x, y, 1 - c)
        self.xn, self.yn, self.diag = (1 - x, y), (x, 1 - y), (1 - x, 1 - y)

    def _halves(self, a):
        rows = self.stage[a].shape[0]
        cut = (rows // 2 + 15) // 16 * 16
        return pl.ds(0, cut), pl.ds(cut, rows - cut)

    def _copy(self, a, k, block, to, from_stage=False, rows=None):
        dst = self.dests[a].at[_slot(*block)]
        src = self.stage[a] if from_stage else dst
        if rows is not None:
            src, dst = src.at[rows], dst.at[rows]
        return pltpu.make_async_remote_copy(
            src_ref=src, dst_ref=dst, send_sem=self.send_sems.at[self.COPIES * a + k],
            recv_sem=self.recv_sems.at[self.COPIES * a + k], device_id=to, device_id_type=MESH_ID)

    def _mine(self):
        return [pltpu.make_async_copy(self.stage[a], self.dests[a].at[_slot(*self.me)], self.local_sems.at[a]) for a in range(self.n)]

    def _first(self, a):
        return [self._copy(a, 0, self.me, self.sibling, True), self._copy(a, 1, self.me, (*self.xn, self.c), True),
                self._copy(a, 2, self.me, (*self.yn, self.c), True)]

    def _relays(self, a):
        lo, hi = self._halves(a)
        return [self._copy(a, 3, (*self.xn, self.c), (*self.yn, self.c), rows=lo),
                self._copy(a, 4, (*self.yn, self.c), (*self.xn, self.c), rows=hi)]

    def _passes(self, a):
        return [self._copy(a, 5 + j, (*chip, self.c), self.sibling) for j, chip in enumerate((self.xn, self.yn, self.diag))]

    def start(self):
        for a in range(self.n):
            self.stage[a][...] = self.ins[a][...].astype(BF16)
        for cp in self._mine():
            cp.start()
        for a in range(self.n):
            for cp in self._first(a):
                cp.start()

    def forward(self):
        for a in range(self.n):
            relays, passes = self._relays(a), self._passes(a)
            for j, chip in enumerate((self.xn, self.yn)):
                self._copy(a, 1 + j, (*chip, self.c), self.me).wait_recv()
                relays[j].start()
                passes[j].start()

    def finish(self):
        for a in range(self.n):
            lo, hi = self._halves(a)
            self._copy(a, 3, (*self.diag, self.c), self.me, rows=lo).wait_recv()
            self._copy(a, 4, (*self.diag, self.c), self.me, rows=hi).wait_recv()
            self._passes(a)[2].start()
        for a in range(self.n):
            self._copy(a, 0, self.sibling, self.me).wait_recv()
            for j, chip in enumerate((self.xn, self.yn, self.diag)):
                self._copy(a, 5 + j, (*chip, 1 - self.c), self.me).wait_recv()
            for cp in self._first(a) + self._relays(a) + self._passes(a):
                cp.wait_send()
        for cp in self._mine():
            cp.wait()


N_CHIPS = 4


def _reduce_scatter(slabs, packed):
    def body(*refs):
        n = len(slabs)
        rs = _ReduceScatter(refs[:n], refs[n], refs[n + 1:2 * n + 1], refs[2 * n + 1], refs[2 * n + 2:])
        rs.start1()
        rs.finish1_start2()
        rs.relay2()
        rs.finish2()

    shapes = [s.shape[1:] for s in slabs]
    return pl.pallas_call(
        body, name="reduce_scatter_grads",
        in_specs=[HBM_SPEC] * len(slabs) + [VMEM_SPEC], out_specs=[VMEM_SPEC] * (len(slabs) + 1),
        out_shape=[SDS(s, F32) for s in shapes] + [SDS(packed.shape, F32)],
        scratch_shapes=_rs_scratch(shapes, packed.shape), compiler_params=_cp(),
    )(*slabs, packed)


def _rs_scratch(shapes, packed_shape):
    n = len(shapes)
    n1, n2 = N_CHIPS * n + 1, _ReduceScatter.L2_COPIES * n + N_CHIPS - 1
    dma = pltpu.SemaphoreType.DMA
    packed = [] if packed_shape is None else [pltpu.VMEM(packed_shape, F32), pltpu.VMEM((N_CHIPS,) + tuple(packed_shape), F32)]
    return ([pltpu.VMEM((N_CHIPS,) + tuple(s), BF16) for s in shapes] * 2 + [pltpu.VMEM((N_CHIPS - 1,) + tuple(s), BF16) for s in shapes] * 2
            + packed + [dma((max(N_CHIPS * n, 1),)), dma((n1,)), dma((n1,)), dma((n2,)), dma((n2,))])


class _ReduceScatter:
    L2_COPIES = 6

    def __init__(self, slab_refs, packed_ref, out_refs, ptot_ref, scratch):
        n = self.n = len(slab_refs)
        self.slabs, self.packed, self.outs, self.ptot = slab_refs, packed_ref, out_refs, ptot_ref
        self.own1, self.land1, self.send2, self.land2 = (scratch[k * n:(k + 1) * n] for k in range(4))
        rest = scratch[4 * n:]
        if packed_ref is not None:
            self.pland1, self.pland2 = rest[:2]
            rest = rest[2:]
        self.loc_sems, self.send1_sems, self.recv1_sems, self.send2_sems, self.recv2_sems = rest
        self.x, self.y, self.c = _mesh_pos()

    def _chip(self, r):
        return (1 - self.x if r & 2 else self.x, 1 - self.y if r & 1 else self.y)

    @staticmethod
    def _remote(src, dst, send_sem, recv_sem, to):
        return pltpu.make_async_remote_copy(src_ref=src, dst_ref=dst, send_sem=send_sem, recv_sem=recv_sem, device_id=to,
                                            device_id_type=MESH_ID)

    def _copies1(self):
        c, sibling = self.c, (self.x, self.y, 1 - self.c)
        cps = []
        for a in range(self.n):
            for r in range(N_CHIPS):
                k = N_CHIPS * a + r
                cps.append(pltpu.make_async_copy(self.slabs[a].at[_slot(*self._chip(r), c)], self.own1[a].at[r], self.loc_sems.at[k]))
                cps.append(self._remote(self.slabs[a].at[_slot(*self._chip(r), 1 - c)], self.land1[a].at[r],
                                        self.send1_sems.at[k], self.recv1_sems.at[k], sibling))
        if self.packed is not None:
            k = N_CHIPS * self.n
            cps.append(self._remote(self.packed, self.pland1, self.send1_sems.at[k], self.recv1_sems.at[k], sibling))
        return cps

    def _halves(self, a):
        rows = self.send2[a].shape[1]
        cut = (rows // 2 + 15) // 16 * 16
        return pl.ds(0, cut), pl.ds(cut, rows - cut)

    def _copies2(self, a):
        lo, hi = self._halves(a)
        s2, l2 = self.send2[a], self.land2[a]
        xn, yn = (*self._chip(2), self.c), (*self._chip(1), self.c)
        plan = [(1, lo, xn), (2, lo, xn), (0, hi, yn), (2, hi, yn), (0, lo, yn), (1, hi, xn)]
        return [self._remote(s2.at[slot].at[rows], l2.at[slot].at[rows], self.send2_sems.at[self.L2_COPIES * a + k],
                             self.recv2_sems.at[self.L2_COPIES * a + k], to) for k, (slot, rows, to) in enumerate(plan)]

    def _copies2_packed(self):
        base = self.L2_COPIES * self.n - 1
        return [self._remote(self.pland2.at[0], self.pland2.at[r], self.send2_sems.at[base + r], self.recv2_sems.at[base + r],
                             (*self._chip(r), self.c)) for r in range(1, N_CHIPS)]

    def start1(self):
        for cp in self._copies1():
            cp.start()

    def finish1_start2(self):
        for cp in self._copies1():
            cp.wait()
        for a in range(self.n):
            self.outs[a][...] = self.own1[a][0].astype(F32) + self.land1[a][0].astype(F32)
            for r in range(1, N_CHIPS):
                self.send2[a][r - 1] = (self.own1[a][r].astype(F32) + self.land1[a][r].astype(F32)).astype(BF16)
            for cp in self._copies2(a)[:4]:
                cp.start()
        if self.packed is not None:
            self.pland2[0] = self.packed[...] + self.pland1[...]
            for cp in self._copies2_packed():
                cp.start()

    def relay2(self):
        for a in range(self.n):
            lo, hi = self._halves(a)
            cps, s2, l2 = self._copies2(a), self.send2[a], self.land2[a]
            cps[1].wait_recv()
            s2[0, lo] = (s2[0, lo].astype(F32) + l2[2, lo].astype(F32)).astype(BF16)
            cps[4].start()
            cps[3].wait_recv()
            s2[1, hi] = (s2[1, hi].astype(F32) + l2[2, hi].astype(F32)).astype(BF16)
            cps[5].start()

    def finish2(self):
        for a in range(self.n):
            cps = self._copies2(a)
            for k in (0, 2, 4, 5):
                cps[k].wait_recv()
            for cp in cps:
                cp.wait_send()
            l2 = self.land2[a]
            self.outs[a][...] = self.outs[a][...] + (l2[0].astype(F32) + l2[1].astype(F32))
        if self.packed is not None:
            for cp in self._copies2_packed():
                cp.wait()
            p2 = self.pland2
            self.ptot[...] = (p2[0] + p2[1]) + (p2[2] + p2[3])


def _adamw(ws, gs, ms, vs):
    n = len(ws)

    def body(*refs):
        for k in range(n):
            w, g, m, v = (refs[j * n + k][...] for j in range(4))
            d_ref, nm_ref, nv_ref = (refs[(4 + j) * n + k] for j in range(3))
            m = ADAM_B1 * m + (1.0 - ADAM_B1) * g
            v = ADAM_B2 * v + (1.0 - ADAM_B2) * jnp.square(g)
            m_hat = m / (1.0 - ADAM_B1 ** ADAM_STEP)
            v_hat = v / (1.0 - ADAM_B2 ** ADAM_STEP)
            d_ref[...] = -ADAM_LR * (m_hat / (jnp.sqrt(v_hat) + ADAM_EPS) + ADAM_WD * w)
            nm_ref[...] = m
            nv_ref[...] = v

    outs = pl.pallas_call(
        body, name="adamw", in_specs=[VMEM_SPEC] * (4 * n), out_specs=[VMEM_SPEC] * (3 * n),
        out_shape=[SDS(w.shape, F32) for w in ws] * 3, compiler_params=_cp(),
    )(*ws, *gs, *ms, *vs)
    return outs[:n], outs[n:2 * n], outs[2 * n:]


WEIGHTS = ("norm_in", "w_in", "q_norm", "w_uq", "kv_norm", "w_ukv", "pool_w", "pool_scale", "w_branch_attn", "w_branch_pool",
           "w_out", "norm_final")
SUBLANES = 8


def _pack_rows(a):
    a = a.reshape(-1, LANES)
    return jnp.pad(a, ((0, -a.shape[0] % SUBLANES), (0, 0)))


def _unpack_rows(packed, like):
    out, row = [], 0
    for a in like:
        rows = a.size // LANES
        out.append(packed[row:row + rows].reshape(a.shape))
        row += rows + (-rows % SUBLANES)
    return out


def kernel(x, norm_in, w_in, q_norm, w_uq, kv_norm, w_ukv, pool_w, pool_scale, w_branch_attn, w_branch_pool, w_out, norm_final, loss_target, m_norm_in, m_w_in, m_q_norm, m_w_uq, m_kv_norm, m_w_ukv, m_pool_w, m_pool_scale, m_w_branch_attn, m_w_branch_pool, m_w_out, m_norm_final, v_norm_in, v_w_in, v_q_norm, v_w_uq, v_kv_norm, v_w_ukv, v_pool_w, v_pool_scale, v_w_branch_attn, v_w_branch_pool, v_w_out, v_norm_final):
    w = dict(norm_in=norm_in, w_in=w_in, q_norm=q_norm, w_uq=w_uq, kv_norm=kv_norm, w_ukv=w_ukv, pool_w=pool_w, pool_scale=pool_scale,
             w_branch_attn=w_branch_attn, w_branch_pool=w_branch_pool, w_out=w_out, norm_final=norm_final)
    m = dict(norm_in=m_norm_in, w_in=m_w_in, q_norm=m_q_norm, w_uq=m_w_uq, kv_norm=m_kv_norm, w_ukv=m_w_ukv, pool_w=m_pool_w,
             pool_scale=m_pool_scale, w_branch_attn=m_w_branch_attn, w_branch_pool=m_w_branch_pool, w_out=m_w_out, norm_final=m_norm_final)
    v = dict(norm_in=v_norm_in, w_in=v_w_in, q_norm=v_q_norm, w_uq=v_w_uq, kv_norm=v_kv_norm, w_ukv=v_w_ukv, pool_w=v_pool_w,
             pool_scale=v_pool_scale, w_branch_attn=v_w_branch_attn, w_branch_pool=v_w_branch_pool, w_out=v_w_out, norm_final=v_norm_final)

    def as2d(name, a):
        if name == "w_in":
            return a.T
        if name == "w_uq":
            return a
        if name == "w_ukv":
            return a.reshape(a.shape[0], -1)
        if name == "pool_w":
            return a.reshape(-1, GROUP)
        return a.reshape(1, -1) if a.ndim == 1 else a

    def unshape(name, a):
        return a.T if name == "w_in" else a.reshape(w[name].shape)

    w_in_pad, w_uq_full, w_ukv_full = _all_gather_bf16([as2d(k, w[k]) for k in ("w_in", "w_uq", "w_ukv")])
    loss, grad_x, g2d = _local_step(
        x.reshape(x.shape[1:]), loss_target.reshape(x.shape[1:]), norm_in, w_in_pad, q_norm,
        w_uq_full.reshape(Q_RANK, HEADS, NOPE + ROPE), kv_norm, w_ukv_full.reshape(KV_RANK, HEADS, NOPE + VDIM),
        pool_w, pool_scale, [w_branch_attn, w_branch_pool, w_out], norm_final)

    deltas, new_m, new_v = _adamw([as2d(k, w[k]) for k in WEIGHTS], [g2d[k] for k in WEIGHTS],
                                  [as2d(k, m[k]) for k in WEIGHTS], [as2d(k, v[k]) for k in WEIGHTS])
    shaped = lambda arrs: [unshape(k, a) for k, a in zip(WEIGHTS, arrs)]
    return (loss, grad_x.reshape(x.shape), *shaped([g2d[k] for k in WEIGHTS]), *shaped(deltas), *shaped(new_m), *shaped(new_v))
```

```python
import functools

import jax
import jax.numpy as jnp
import numpy as np
from jax import lax
from jax.experimental import pallas as pl
from jax.experimental.pallas import tpu as pltpu

F32 = jnp.float32
BF16 = jnp.bfloat16
SDS = jax.ShapeDtypeStruct

D_MODEL = 1024
HEADS = 8
NOPE = 64
ROPE = 32
VDIM = 64
Q_RANK = 384
KV_RANK = 256
MLA_W = HEADS * VDIM
POOL_W = 512
POOL_GROUPS = 4
GROUP = POOL_W // POOL_GROUPS
CHUNK = 64
ROPE_THETA = 10000.0
EPS = 1e-6
SCALE = (NOPE + ROPE) ** -0.5
LOG2E = 1.4426950408889634
LN2 = 0.6931471805599453
QK_SCALE_LOG2 = SCALE * LOG2E
IN_TOTAL = 4256
ADAM_LR, ADAM_B1, ADAM_B2, ADAM_EPS, ADAM_WD, ADAM_STEP = 0.001, 0.9, 0.999, 1e-08, 0.01, 10

N_DEV = 8
LANES = 128
HEAD_PAD = LANES
HW = HEADS * HEAD_PAD

ZQ, ZKV, ZKR, GA, UP, GP, GM, ZTOT = 0, 384, 640, 768, 1280, 1792, 2304, 4352
FRONT_W = GA
ZKR_ORIG = 640

VMEM_LIMIT = 56 * 1024 * 1024


def _cp(sem=None, **kw):
    if sem is not None:
        kw["dimension_semantics"] = sem
    return pltpu.CompilerParams(vmem_limit_bytes=VMEM_LIMIT, **kw)


def _mm(a, b):
    return lax.dot_general(a, b, (((1,), (0,)), ((), ())), preferred_element_type=F32)


def _mm_nt(a, b):
    return lax.dot_general(a, b, (((1,), (1,)), ((), ())), preferred_element_type=F32)


def _mm_tn(a, b):
    return lax.dot_general(a, b, (((0,), (0,)), ((), ())), preferred_element_type=F32)


def _row_spec(tm, w):
    return pl.BlockSpec((tm, w), lambda i: (i, 0))


def _full_spec(a):
    nd = len(a.shape)
    return pl.BlockSpec(a.shape, lambda *_: (0,) * nd)


def _rope(v, c, sa, sb, sign):
    n = v.shape[-1]
    reps = n // LANES
    if reps > 1:
        c, sa, sb = (jnp.tile(t, (1, reps)) for t in (c, sa, sb))
    up = pltpu.roll(v, n - ROPE // 2, 1)
    dn = pltpu.roll(v, ROPE // 2, 1)
    return v * c + sign * (up * sa + dn * sb)


def _rope_tables(T):
    half = ROPE // 2
    inv_freq = np.float32(ROPE_THETA) ** (-np.arange(half, dtype=np.float32) / np.float32(half))
    ang = np.arange(T, dtype=np.float32)[:, None] * inv_freq[None, :].astype(np.float32)
    cos, sin = np.cos(ang.astype(np.float64)).astype(np.float32), np.sin(ang.astype(np.float64)).astype(np.float32)
    z16 = np.zeros((T, half), np.float32)
    z32 = np.zeros((T, LANES - NOPE - ROPE), np.float32)
    c = np.concatenate([np.ones((T, NOPE), np.float32), cos, cos, z32], axis=1)
    sa = np.concatenate([np.zeros((T, NOPE), np.float32), -sin, z16, z32], axis=1)
    sb = np.concatenate([np.zeros((T, NOPE), np.float32), z16, sin, z32], axis=1)
    return jnp.asarray(c), jnp.asarray(sa), jnp.asarray(sb)


def _silu_parts(g):
    sg = jax.nn.sigmoid(g)
    return g * sg, sg + g * sg * (1.0 - sg)


def _in_proj(x2, norm_in, w_in_pad, q_norm, wuq_pad, kv_norm, wukv, rc, rsa, rsb, tm, late_shards):
    T = x2.shape[0]
    steps = T // tm
    n = len(late_shards)

    def body(x_ref, nin_ref, win_ref, qn_ref, wuq_ref, kvn_ref, wukv_ref, c_ref, sa_ref, sb_ref, *rest):
        hn_ref, zgm_ref, zga_ref, zup_ref, zgp_ref, zfr_ref, q_ref, k_ref, v_ref, vt_ref = rest[n:n + 10]
        ag = _AllGather(rest[:n], rest[n + 10:2 * n + 10], rest[2 * n + 10:])
        step = pl.program_id(0)
        pl.when(step == 0)(ag.start)
        pl.when(step == min(3, steps - 1))(ag.forward)
        xf = x_ref[...]
        r = lax.rsqrt(jnp.mean(xf * xf, axis=-1, keepdims=True) + EPS)
        hn = (xf * r * nin_ref[...]).astype(BF16)
        hn_ref[...] = hn
        z = _mm_nt(hn, win_ref[...])
        zgm_ref[...] = z[:, GM:ZTOT]
        zga_ref[...] = z[:, GA:UP]
        zup_ref[...] = z[:, UP:GP]
        zgp_ref[...] = z[:, GP:GM]
        zfr_ref[...] = z[:, ZQ:GA]
        zq, zkv, zkr = z[:, ZQ:ZKV], z[:, ZKV:ZKR], z[:, ZKR:GA]
        c, sa, sb = c_ref[...], sa_ref[...], sb_ref[...]
        rq = lax.rsqrt(jnp.mean(zq * zq, axis=-1, keepdims=True) + EPS)
        cq = (zq * rq * qn_ref[...]).astype(BF16)
        q = _rope(_mm(cq, wuq_ref[...]), c, sa, sb, 1.0)
        q_ref[...] = (q * QK_SCALE_LOG2).astype(BF16)
        rkv = lax.rsqrt(jnp.mean(zkv * zkv, axis=-1, keepdims=True) + EPS)
        ckv = (zkv * rkv * kvn_ref[...]).astype(BF16)
        kv = _mm(ckv, wukv_ref[...])
        kr = _rope(zkr, c, sa, sb, 1.0)
        lane = lax.broadcasted_iota(jnp.int32, kv.shape, 1) % LANES
        k_ref[...] = jnp.where(lane < NOPE, kv, jnp.tile(kr, (1, HEADS))).astype(BF16)
        v = jnp.where(lane < NOPE, 1.0, kv).astype(BF16)
        v_ref[...] = v
        vt_ref[...] = v.T
        pl.when(step == steps - 1)(ag.finish)

    ins = (x2, norm_in, w_in_pad, q_norm, wuq_pad, kv_norm, wukv, rc, rsa, rsb)
    in_specs = [_row_spec(tm, D_MODEL), _full_spec(norm_in), _full_spec(w_in_pad), _full_spec(q_norm), _full_spec(wuq_pad),
                _full_spec(kv_norm), _full_spec(wukv), _row_spec(tm, LANES), _row_spec(tm, LANES), _row_spec(tm, LANES)]
    widths = [(D_MODEL, BF16), (ZTOT - GM, F32), (UP - GA, F32), (GP - UP, F32), (GM - GP, F32), (FRONT_W, F32),
              (HW, BF16), (HW, BF16), (HW, BF16)]
    return pl.pallas_call(
        body, name="in_proj", grid=(steps,), in_specs=in_specs + [_full_spec(s) for s in late_shards],
        out_specs=[_row_spec(tm, w) for w, _ in widths] + [pl.BlockSpec((HW, tm), lambda i: (0, i))] + [HBM_SPEC] * n,
        out_shape=[SDS((T, w), dt) for w, dt in widths] + [SDS((HW, T), BF16)]
        + [SDS((N_DEV,) + s.shape, BF16) for s in late_shards],
        scratch_shapes=_ag_scratch([s.shape for s in late_shards]), compiler_params=_cp(("arbitrary",)),
    )(*ins, *late_shards)


def _chunk_mask(n_q, n_k, q_off, transposed):
    shape = (n_k, n_q) if transposed else (n_q, n_k)
    q = (lax.broadcasted_iota(jnp.int32, shape, 1 if transposed else 0) + q_off) // CHUNK
    k = lax.broadcasted_iota(jnp.int32, shape, 0 if transposed else 1) // CHUNK
    return k <= q


def _store_pair_rows(ref, k, pair):
    t = pair.T
    ref[k, 0:1, :] = t[0:1, :]
    ref[k, 1:2, :] = t[VDIM:VDIM + 1, :]


def _attn_fwd(q_att, k_att, vt_att, tq, hps):
    T = q_att.shape[0]
    head_lanes = [slice(h * LANES, (h + 1) * LANES) for h in range(hps)]

    def body(q_ref, k_ref, vt_ref, o_ref, lser_ref):
        i = pl.program_id(1)
        mask = _chunk_mask(tq, tq, 0, True)
        lane = lax.broadcasted_iota(jnp.int32, (tq, LANES), 1)
        qs = [q_ref[:, hs] for hs in head_lanes]

        def step(j, carry, masked):
            off = pl.multiple_of(j * tq, tq)
            sts = [_mm_nt(k_ref[pl.ds(off, tq), hs], qh) for qh, hs in zip(qs, head_lanes)]
            if masked:
                sts = [jnp.where(mask, st, -jnp.inf) for st in sts]
            ms = [jnp.maximum(m, jnp.max(st, axis=0, keepdims=True)) for (m, _), st in zip(carry, sts)]
            pts = [jnp.exp2(st - m_new).astype(BF16) for st, m_new in zip(sts, ms)]
            return tuple((m_new, jnp.exp2(m - m_new) * acc + _mm(vt_ref[hs, pl.ds(off, tq)], pt))
                         for (m, acc), m_new, pt, hs in zip(carry, ms, pts, head_lanes))

        init = ((jnp.full((1, tq), -jnp.inf, F32), jnp.zeros((LANES, tq), F32)),) * hps
        res = step(i, lax.fori_loop(0, i, functools.partial(step, masked=False), init), True)
        for pair in range(hps // 2):
            (ma, acca), (mb, accb) = res[2 * pair], res[2 * pair + 1]
            la, lb = acca[:1], accb[:1]
            oa, ob = (acca / la).T, (accb / lb).T
            o_ref[:, pair * LANES:(pair + 1) * LANES] = jnp.where(lane < VDIM, pltpu.roll(oa, VDIM, 1), ob)
            lser_ref[pair, 0:1, :] = ma + jnp.log2(la)
            lser_ref[pair, 1:2, :] = mb + jnp.log2(lb)

    qspec = pl.BlockSpec((tq, hps * LANES), lambda p, i: (i, p))
    kspec = pl.BlockSpec((T, hps * LANES), lambda p, i: (0, p))
    vspec = pl.BlockSpec((hps * LANES, T), lambda p, i: (p, 0))
    ospec = pl.BlockSpec((tq, hps * VDIM), lambda p, i: (i, p))
    return pl.pallas_call(
        body, name="attn_fwd", grid=(HEADS // hps, T // tq), in_specs=[qspec, kspec, vspec],
        out_specs=[ospec, pl.BlockSpec((hps // 2, 2, tq), lambda p, i: (p, 0, i))],
        out_shape=[SDS((T, MLA_W), F32), SDS((HEADS // 2, 2, T), F32)],
        compiler_params=_cp(("parallel", "parallel")),
    )(q_att, k_att, vt_att)


def _pick(g, vals):
    out = vals[-1]
    for k in range(len(vals) - 2, -1, -1):
        out = jnp.where(g == k, vals[k], out)
    return out


def _window_sum(u, g, forward):
    T = u.shape[0]
    row = lax.broadcasted_iota(jnp.int32, u.shape, 0)

    def sh(s, k):
        if forward:
            return jnp.where(row >= k, pltpu.roll(s, k, 0), 0.0)
        return jnp.where(row < T - k, pltpu.roll(s, T - k, 0), 0.0)

    sums, s = [], u
    for k in (1, 2, 4, 8):
        s = s + sh(s, k)
        sums.append(s)
    return _pick(g, sums)


def _pool_count(shape, g):
    row = lax.broadcasted_iota(jnp.int32, shape, 0)
    return jnp.minimum(row + 1, lax.shift_left(jnp.int32(2), g)).astype(F32)


def _pool_fwd(zup, zgp, pool_w, pool_scale):
    T = zup.shape[0]

    def body(u_ref, g_ref, w_ref, sc_ref, y_ref):
        g = pl.program_id(0)
        u = u_ref[...]
        d = _window_sum(u, g, True) / _pool_count(u.shape, g) - u
        lin = _mm(d.astype(BF16), w_ref[0].astype(BF16))
        silu, _ = _silu_parts(g_ref[...])
        y_ref[...] = (lin * sc_ref[...] * silu).astype(BF16)

    col = pl.BlockSpec((T, GROUP), lambda g: (0, g))
    return pl.pallas_call(
        body, name="pool_fwd", grid=(POOL_GROUPS,),
        in_specs=[col, col, pl.BlockSpec((1, GROUP, GROUP), lambda g: (g, 0, 0)), pl.BlockSpec((1, GROUP), lambda g: (0, g))],
        out_specs=col, out_shape=SDS((T, POOL_W), BF16), compiler_params=_cp(("parallel",)),
    )(zup, zgp, pool_w, pool_scale)


def _pool_bwd(zup, zgp, dyp, pool_w, pool_scale):
    T = zup.shape[0]

    def body(u_ref, g_ref, dy_ref, w_ref, sc_ref, du_ref, dg_ref, gw_ref, gsc_ref):
        g = pl.program_id(0)
        u = u_ref[...]
        cnt = _pool_count(u.shape, g)
        d = (_window_sum(u, g, True) / cnt - u).astype(BF16)
        wb = w_ref[0].astype(BF16)
        lin = _mm(d, wb)
        sc = sc_ref[...]
        silu, dsilu = _silu_parts(g_ref[...])
        dy = dy_ref[...]
        dg_ref[...] = (dy * lin * sc * dsilu).astype(BF16)
        dpre = dy * silu
        gsc_ref[...] = jnp.sum(dpre * lin, axis=0, keepdims=True)
        dlin = (dpre * sc).astype(BF16)
        gw_ref[0] = _mm_tn(d, dlin)
        dd = _mm_nt(dlin, wb)
        du_ref[...] = (_window_sum(dd / cnt, g, False) - dd).astype(BF16)

    col = pl.BlockSpec((T, GROUP), lambda g: (0, g))
    wspec = pl.BlockSpec((1, GROUP, GROUP), lambda g: (g, 0, 0))
    vspec = pl.BlockSpec((1, GROUP), lambda g: (0, g))
    return pl.pallas_call(
        body, name="pool_bwd", grid=(POOL_GROUPS,), in_specs=[col, col, col, wspec, vspec], out_specs=[col, col, wspec, vspec],
        out_shape=[SDS((T, POOL_W), BF16), SDS((T, POOL_W), BF16), SDS((POOL_GROUPS, GROUP, GROUP), F32), SDS((1, POOL_W), F32)],
        compiler_params=_cp(("parallel",)),
    )(zup, zgp, dyp, pool_w, pool_scale)


def _tail(x2, tgt, o, zga, ypool, zgm, wba, wbp, wout, norm_final, tm):
    T = x2.shape[0]
    steps = T // tm
    cols = D_MODEL // N_DEV

    def body(x_ref, tgt_ref, o_ref, zga_ref, yp_ref, zgm_ref, wba_ref, wbp_ref, wout_ref, nf_ref,
             loss_ref, dh_ref, dgm_ref, doop_ref, dga_ref, dcapr_ref, dyp_ref, swout_ref, swba_ref, swbp_ref, gnf_ref,
             gwout_ref, gwba_ref, gwbp_ref):
        @pl.when(pl.program_id(0) == 0)
        def _():
            for ref in (loss_ref, gwout_ref, gwba_ref, gwbp_ref, gnf_ref):
                ref[...] = jnp.zeros_like(ref)

        o_v = o_ref[...]
        silu, dsilu = _silu_parts(zga_ref[...])
        ya = (o_v * silu).astype(BF16)
        yp = yp_ref[...]
        wba_v = jnp.concatenate([wba_ref[k] for k in range(N_DEV)], axis=1)
        wbp_v = jnp.concatenate([wbp_ref[k] for k in range(N_DEV)], axis=1)
        wout_v = wout_ref[...]
        a = _mm(ya, wba_v)
        p = _mm(yp, wbp_v)
        gate = jax.nn.sigmoid(zgm_ref[...])
        ga, gp = gate[:, :D_MODEL], gate[:, D_MODEL:]
        mg = (ga * a + gp * p).astype(BF16)
        h = x_ref[...] + _mm(mg, wout_v)
        r = lax.rsqrt(jnp.mean(h * h, axis=-1, keepdims=True) + EPS)
        gf = nf_ref[...]
        hr = h * r
        e = hr * gf - tgt_ref[...]
        loss_ref[...] += (0.5 / D_MODEL) * jnp.sum(e * e)
        dy = e * (1.0 / D_MODEL)
        gnf_ref[...] += jnp.sum(dy * hr, axis=0, keepdims=True)
        u = dy * gf
        dh = r * (u - hr * jnp.mean(u * hr, axis=-1, keepdims=True))
        dh_ref[...] = dh
        dhb = dh.astype(BF16)
        dmg = _mm_nt(dhb, wout_v)
        gwout_ref[...] += _mm_tn(mg, dhb)
        dgm_ref[:, :D_MODEL] = (dmg * a * ga * (1.0 - ga)).astype(BF16)
        dgm_ref[:, D_MODEL:] = (dmg * p * gp * (1.0 - gp)).astype(BF16)
        dab = (dmg * ga).astype(BF16)
        dpb = (dmg * gp).astype(BF16)
        dya = _mm_nt(dab, wba_v)
        gwba_ref[...] += _mm_tn(ya, dab)
        dyp_ref[...] = _mm_nt(dpb, wbp_v)
        gwbp_ref[...] += _mm_tn(yp, dpb)
        do = dya * silu
        dga_ref[...] = (dya * o_v * dsilu).astype(BF16)
        prod = do * o_v
        lo = lax.broadcasted_iota(jnp.int32, (tm, LANES), 1) < VDIM
        for pair in range(HEADS // 2):
            ls = slice(pair * LANES, (pair + 1) * LANES)
            do_p, prod_p = do[:, ls], prod[:, ls]
            dcap_a = jnp.sum(jnp.where(lo, prod_p, 0.0), axis=-1, keepdims=True)
            dcap_b = jnp.sum(jnp.where(lo, 0.0, prod_p), axis=-1, keepdims=True)
            _store_pair_rows(dcapr_ref, pair, jnp.where(lo, dcap_a, dcap_b))
            doop_ref[:, 2 * pair * LANES:(2 * pair + 1) * LANES] = jnp.where(lo, 0.0, pltpu.roll(do_p, VDIM, 1)).astype(BF16)
            doop_ref[:, (2 * pair + 1) * LANES:(2 * pair + 2) * LANES] = jnp.where(lo, 0.0, do_p).astype(BF16)

        @pl.when(pl.program_id(0) == steps - 1)
        def _():
            for k in range(N_DEV):
                swout_ref[k] = gwout_ref[k * cols:(k + 1) * cols, :].astype(BF16)
                swba_ref[k] = gwba_ref[:, k * cols:(k + 1) * cols].astype(BF16)
                swbp_ref[k] = gwbp_ref[:, k * cols:(k + 1) * cols].astype(BF16)

    ins = (x2, tgt, o, zga, ypool, zgm, wba, wbp, wout, norm_final)
    in_specs = [_row_spec(tm, D_MODEL), _row_spec(tm, D_MODEL), _row_spec(tm, MLA_W), _row_spec(tm, MLA_W), _row_spec(tm, POOL_W),
                _row_spec(tm, 2 * D_MODEL), _full_spec(wba), _full_spec(wbp), _full_spec(wout), _full_spec(norm_final)]
    outs = [SDS((8, LANES), F32), SDS((T, D_MODEL), F32), SDS((T, 2 * D_MODEL), BF16), SDS((T, HW), BF16), SDS((T, MLA_W), BF16),
            SDS((HEADS // 2, 2, T), F32), SDS((T, POOL_W), F32),
            SDS((N_DEV, cols, D_MODEL), BF16), SDS((N_DEV, MLA_W, cols), BF16), SDS((N_DEV, POOL_W, cols), BF16), SDS((1, D_MODEL), F32)]
    out_specs = [_full_spec(outs[0]), _row_spec(tm, D_MODEL), _row_spec(tm, 2 * D_MODEL), _row_spec(tm, HW), _row_spec(tm, MLA_W),
                 pl.BlockSpec((HEADS // 2, 2, tm), lambda i: (0, 0, i)), _row_spec(tm, POOL_W),
                 _full_spec(outs[7]), _full_spec(outs[8]), _full_spec(outs[9]), _full_spec(outs[10])]
    return pl.pallas_call(
        body, name="tail", grid=(steps,), in_specs=in_specs, out_specs=out_specs, out_shape=outs,
        scratch_shapes=[pltpu.VMEM((D_MODEL, D_MODEL), F32), pltpu.VMEM((MLA_W, D_MODEL), F32), pltpu.VMEM((POOL_W, D_MODEL), F32)],
        compiler_params=_cp(("arbitrary",)),
    )(*ins)


def _attn_bwd(q_att, k_att, v_att, doop, lse_rows, dcap_rows, tq, hps, slabs, packed):
    T = q_att.shape[0]
    nq = T // tq
    n = len(slabs)
    groups = HEADS // hps
    head_lanes = [slice(h * LANES, (h + 1) * LANES) for h in range(hps)]

    def body(q_ref, k_ref, v_ref, doop_ref, lse_ref, dcap_ref, *rest):
        slab_refs, packed_ref = rest[:n], rest[n]
        dq_ref, dkv_ref, dkr_ref = rest[n + 1:n + 4]
        sum_refs, ptot_ref = rest[n + 4:2 * n + 4], rest[2 * n + 4]
        dq_acc = rest[2 * n + 5]
        rs = _ReduceScatter(slab_refs, packed_ref, sum_refs, ptot_ref, rest[2 * n + 6:])
        group, j = pl.program_id(0), pl.program_id(1)
        pl.when((group == 0) & (j == 0))(rs.start1)
        pl.when((group == 1) & (j == 0))(rs.finish1_start2)
        pl.when((group == groups - 1) & (j == 0))(rs.relay2)
        mask = _chunk_mask(tq, tq, 0, True)
        lane = lax.broadcasted_iota(jnp.int32, (tq, LANES), 1)
        ks = [k_ref[:, hs] for hs in head_lanes]
        vs = [v_ref[:, hs] for hs in head_lanes]
        kts = [kh.T for kh in ks]

        @pl.when(j == 0)
        def _():
            dq_acc[...] = jnp.zeros_like(dq_acc)

        def step(i, carry, masked):
            rows = pl.ds(pl.multiple_of(i * tq, tq), tq)
            heads = range(hps)
            stat = lambda h: (h // 2, slice(h % 2, h % 2 + 1), rows)
            qhs = [q_ref[rows, hs] for hs in head_lanes]
            doops = [doop_ref[rows, hs] for hs in head_lanes]
            sts = [_mm_nt(ks[h], qhs[h]) for h in heads]
            dpts = [_mm_nt(vs[h], doops[h]) for h in heads]
            pts = [jnp.exp2(sts[h] - lse_ref[stat(h)]) for h in heads]
            if masked:
                pts = [jnp.where(mask, pt, 0.0) for pt in pts]
            dsts = [(pts[h] * (dpts[h] - dcap_ref[stat(h)])).astype(BF16) for h in heads]
            dvs = [_mm(pts[h].astype(BF16), doops[h]) for h in heads]
            dks = [_mm(dsts[h], qhs[h]) for h in heads]
            for h, hs in enumerate(head_lanes):
                dq_acc[hs, rows] += _mm(kts[h], dsts[h])
            return tuple((dk + dks[h], dv + dvs[h]) for h, (dk, dv) in enumerate(carry))

        zero = jnp.zeros((tq, LANES), F32)
        carry = step(j, ((zero, zero),) * hps, True)
        res = lax.fori_loop(j + 1, nq, functools.partial(step, masked=False), carry)
        dkr = None
        for (dk, dv), hs in zip(res, head_lanes):
            dk = dk * LN2
            dkv_ref[:, hs] = jnp.where(lane < NOPE, dk, dv).astype(BF16)
            dkr = dk if dkr is None else dkr + dk
        dkr_ref[0] = jnp.where((lane >= NOPE) & (lane < NOPE + ROPE), dkr, 0.0)

        @pl.when(j == nq - 1)
        def _():
            dq_ref[...] = (dq_acc[...] * SCALE).T.astype(BF16)

        pl.when((group == groups - 1) & (j == nq - 1))(rs.finish2)

    kspec = pl.BlockSpec((tq, hps * LANES), lambda p, j: (j, p))
    qspec = pl.BlockSpec((T, hps * LANES), lambda p, j: (0, p))
    rspec = pl.BlockSpec((hps // 2, 2, T), lambda p, j: (p, 0, 0))
    sums = [SDS(s.shape[1:], F32) for s in slabs] + [SDS(packed.shape, F32)]
    return pl.pallas_call(
        body, name="attn_bwd", grid=(groups, nq),
        in_specs=[qspec, kspec, kspec, qspec, rspec, rspec] + [HBM_SPEC] * n + [_full_spec(packed)],
        out_specs=[qspec, kspec, pl.BlockSpec((1, tq, LANES), lambda p, j: (p, j, 0))] + [_full_spec(s) for s in sums],
        out_shape=[SDS((T, HW), BF16), SDS((T, HW), BF16), SDS((groups, T, LANES), F32)] + sums,
        scratch_shapes=[pltpu.VMEM((hps * LANES, T), F32)] + _rs_scratch([s.shape for s in sums[:-1]], packed.shape),
        compiler_params=_cp(("arbitrary", "arbitrary")),
    )(q_att, k_att, v_att, doop, lse_rows, dcap_rows, *slabs, packed)


def _rms_bwd(z, gain, dout):
    r = lax.rsqrt(jnp.mean(z * z, axis=-1, keepdims=True) + EPS)
    zr = z * r
    u = dout * gain
    return r * (u - zr * jnp.mean(u * zr, axis=-1, keepdims=True)), jnp.sum(dout * zr, axis=0, keepdims=True)


def _mla_bwd(dq_att, dkv_nat, dkr4, zfr, q_norm, wuq_pad, kv_norm, wukv, rc, rsa, rsb, tm):
    T = dq_att.shape[0]

    def body(dq_ref, dkv_ref, dkr_ref, zfr_ref, qn_ref, wuq_ref, kvn_ref, wukv_ref, c_ref, sa_ref, sb_ref,
             dfr_ref, gwuq_ref, gwukv_ref, gqn_ref, gkvn_ref):
        @pl.when(pl.program_id(0) == 0)
        def _():
            for ref in (gwuq_ref, gwukv_ref, gqn_ref, gkvn_ref):
                ref[...] = jnp.zeros_like(ref)

        c, sa, sb = c_ref[...], sa_ref[...], sb_ref[...]
        zq, zkv = zfr_ref[:, :Q_RANK], zfr_ref[:, Q_RANK:Q_RANK + KV_RANK]
        qn, kvn = qn_ref[...], kvn_ref[...]
        cq = (zq * lax.rsqrt(jnp.mean(zq * zq, axis=-1, keepdims=True) + EPS) * qn).astype(BF16)
        ckv = (zkv * lax.rsqrt(jnp.mean(zkv * zkv, axis=-1, keepdims=True) + EPS) * kvn).astype(BF16)
        dq = _rope(dq_ref[...].astype(F32), c, sa, sb, -1.0).astype(BF16)
        gwuq_ref[...] += _mm_tn(cq, dq)
        dzq, gqn = _rms_bwd(zq, qn, _mm_nt(dq, wuq_ref[...]))
        gqn_ref[...] += gqn
        dkv = dkv_ref[...]
        gwukv_ref[...] += _mm_tn(ckv, dkv)
        dzkv, gkvn = _rms_bwd(zkv, kvn, _mm_nt(dkv, wukv_ref[...]))
        gkvn_ref[...] += gkvn
        dkr = functools.reduce(lambda a, b: a + b, [dkr_ref[g] for g in range(dkr4.shape[0])])
        dfr_ref[:, :Q_RANK] = dzq.astype(BF16)
        dfr_ref[:, Q_RANK:Q_RANK + KV_RANK] = dzkv.astype(BF16)
        dfr_ref[:, Q_RANK + KV_RANK:] = _rope(dkr, c, sa, sb, -1.0).astype(BF16)

    ins = (dq_att, dkv_nat, dkr4, zfr, q_norm, wuq_pad, kv_norm, wukv, rc, rsa, rsb)
    in_specs = [_row_spec(tm, HW), _row_spec(tm, HW), pl.BlockSpec((dkr4.shape[0], tm, LANES), lambda i: (0, i, 0)), _row_spec(tm, FRONT_W),
                _full_spec(q_norm), _full_spec(wuq_pad), _full_spec(kv_norm), _full_spec(wukv),
                _row_spec(tm, LANES), _row_spec(tm, LANES), _row_spec(tm, LANES)]
    outs = [SDS((T, FRONT_W), BF16), SDS((Q_RANK, HW), F32), SDS((KV_RANK, HW), F32), SDS((1, Q_RANK), F32), SDS((1, KV_RANK), F32)]
    out_specs = [_row_spec(tm, FRONT_W)] + [_full_spec(s) for s in outs[1:]]
    return pl.pallas_call(
        body, name="mla_bwd", grid=(T // tm,), in_specs=in_specs, out_specs=out_specs, out_shape=outs,
        compiler_params=_cp(("arbitrary",)),
    )(*ins)


_DZ_COLS = ((GM, ZTOT), (GA, UP), (UP, GP), (GP, GM), (ZQ, GA))


def _in_proj_bwd_x(dzs, x2, dh, norm_in, w_in_pad, tm, slabs):
    T = x2.shape[0]
    steps = T // tm
    n = len(slabs)

    def body(d0, d1, d2, d3, d4, x_ref, dh_ref, nin_ref, win_ref, *rest):
        slab_refs, (gx_ref, gnin_ref), sum_refs = rest[:n], rest[n:n + 2], rest[n + 2:2 * n + 2]
        rs = _ReduceScatter(slab_refs, None, sum_refs, None, rest[2 * n + 2:])
        step = pl.program_id(0)

        @pl.when(step == 0)
        def _():
            gnin_ref[...] = jnp.zeros_like(gnin_ref)
            rs.start1()

        pl.when(step == min(2, steps - 1))(rs.finish1_start2)
        pl.when(step == min(steps * 11 // 16, steps - 1))(rs.relay2)
        dhn = None
        for ref, (lo, hi) in zip((d0, d1, d2, d3, d4), _DZ_COLS):
            t = _mm(ref[...], win_ref[lo:hi, :])
            dhn = t if dhn is None else dhn + t
        dx, gnin = _rms_bwd(x_ref[...], nin_ref[...], dhn)
        gnin_ref[...] += gnin
        gx_ref[...] = dx + dh_ref[...]
        pl.when(step == steps - 1)(rs.finish2)

    in_specs = [_row_spec(tm, hi - lo) for lo, hi in _DZ_COLS] + [_row_spec(tm, D_MODEL), _row_spec(tm, D_MODEL),
                                                                  _full_spec(norm_in), _full_spec(w_in_pad)] + [HBM_SPEC] * n
    sums = [SDS(s.shape[1:], F32) for s in slabs]
    outs = [SDS((T, D_MODEL), F32), SDS((1, D_MODEL), F32)] + sums
    return pl.pallas_call(
        body, name="in_proj_bwd_x", grid=(steps,), in_specs=in_specs,
        out_specs=[_row_spec(tm, D_MODEL), _full_spec(outs[1])] + [_full_spec(s) for s in sums],
        out_shape=outs, scratch_shapes=_rs_scratch([s.shape for s in sums], None), compiler_params=_cp(("arbitrary",)),
    )(*dzs, x2, dh, norm_in, w_in_pad, *slabs)


SLAB_ROWS = IN_TOTAL // N_DEV


def _slab_segments(k):
    cuts = [(0, ZKR_ORIG, 0), (ZKR_ORIG, ZKR_ORIG + ROPE, NOPE), (ZKR_ORIG + ROPE, IN_TOTAL, LANES - ROPE)]
    lo, hi = k * SLAB_ROWS, (k + 1) * SLAB_ROWS
    return [(max(lo, a) - lo, max(lo, a) + shift, min(hi, b) - max(lo, a)) for a, b, shift in cuts if min(hi, b) > max(lo, a)]


def _in_proj_bwd_w(dzs, hn, tm):
    T = hn.shape[0]
    steps = T // tm

    def body(d0, d1, d2, d3, d4, hn_ref, slab_ref, acc_ref):
        @pl.when(pl.program_id(0) == 0)
        def _():
            acc_ref[...] = jnp.zeros_like(acc_ref)

        hn_v = hn_ref[...]
        for ref, (lo, hi) in zip((d0, d1, d2, d3, d4), _DZ_COLS):
            acc_ref[lo:hi, :] += _mm_tn(ref[...], hn_v)

        @pl.when(pl.program_id(0) == steps - 1)
        def _():
            for k in range(N_DEV):
                for at, src, rows in _slab_segments(k):
                    slab_ref[k, at:at + rows, :] = acc_ref[src:src + rows, :].astype(BF16)

    in_specs = [_row_spec(tm, hi - lo) for lo, hi in _DZ_COLS] + [_row_spec(tm, D_MODEL)]
    out = SDS((N_DEV, SLAB_ROWS, D_MODEL), BF16)
    return pl.pallas_call(
        body, name="in_proj_bwd_w", grid=(steps,), in_specs=in_specs, out_specs=_full_spec(out), out_shape=out,
        scratch_shapes=[pltpu.VMEM((ZTOT, D_MODEL), F32)], compiler_params=_cp(("arbitrary",)),
    )(*dzs, hn)


def _local_step(x2, tgt, norm_in, w_in_pad, q_norm, w_uq, kv_norm, w_ukv, pool_w, pool_scale, late_shards, norm_final):
    T = x2.shape[0]
    tm = min(512, T)
    tq = min(512, T)
    row = lambda v: v.reshape(1, -1)
    wuq_pad = jnp.pad(w_uq, ((0, 0), (0, 0), (0, HEAD_PAD - NOPE - ROPE))).reshape(Q_RANK, HW)
    wukv = w_ukv.reshape(KV_RANK, HW)
    rc, rsa, rsb = _rope_tables(T)

    hn, zgm, zga, zup, zgp, zfr, q_att, k_att, v_att, vt_att, w_ba, w_bp, w_out = _in_proj(
        x2, row(norm_in), w_in_pad, row(q_norm), wuq_pad, row(kv_norm), wukv, rc, rsa, rsb, tm, late_shards)
    w_out = w_out.reshape(D_MODEL, D_MODEL)
    o, lse_rows = _attn_fwd(q_att, k_att, vt_att, tq, 4)
    ypool = _pool_fwd(zup, zgp, pool_w, row(pool_scale))
    loss8, dh, dgm, doop, dga, dcap_rows, dyp, *slabs, g_nf = _tail(
        x2, tgt, o, zga, ypool, zgm, w_ba, w_bp, w_out, row(norm_final), min(256, T))
    dup, dgp, g_pool_w, g_pool_scale = _pool_bwd(zup, zgp, dyp, pool_w, row(pool_scale))

    bf = lambda a: a.astype(BF16)
    early = [g_pool_w, g_pool_scale, g_nf, loss8[0]]
    packed = jnp.concatenate([_pack_rows(a) for a in early], axis=0)
    dq_att, dkv_nat, dkr4, s_wout, s_wba, s_wbp, tot_early = _attn_bwd(
        q_att, k_att, v_att, doop, lse_rows, dcap_rows, tq, 2, slabs, packed)
    s_pool_w, s_pool_scale, s_nf, s_loss = _unpack_rows(tot_early, early)

    dfr, g_wuq_pad, g_wukv, g_qn, g_kvn = _mla_bwd(
        dq_att, dkv_nat, dkr4, zfr, row(q_norm), wuq_pad, row(kv_norm), wukv, rc, rsa, rsb, tm)
    dzs = (dgm, dga, dup, dgp, dfr)
    slabs = [_in_proj_bwd_w(dzs, hn, tm),
             bf(g_wuq_pad.reshape(N_DEV, Q_RANK // N_DEV, HEADS, HEAD_PAD)[..., :NOPE + ROPE]),
             bf(g_wukv).reshape(N_DEV, KV_RANK // N_DEV, HW)]
    grad_x, g_nin, s_win, s_wuq, s_wukv = _in_proj_bwd_x(dzs, x2, dh, row(norm_in), w_in_pad, min(256, T), slabs)
    late = [g_nin, g_qn, g_kvn]
    (tot_late,) = _reduce_scatter([], jnp.concatenate([_pack_rows(a) for a in late], axis=0))
    s_nin, s_qn, s_kvn = _unpack_rows(tot_late, late)

    grads = dict(norm_in=s_nin, w_in=s_win, q_norm=s_qn, w_uq=s_wuq, kv_norm=s_kvn, w_ukv=s_wukv, pool_w=s_pool_w.reshape(-1, GROUP),
                 pool_scale=s_pool_scale, w_branch_attn=s_wba, w_branch_pool=s_wbp, w_out=s_wout, norm_final=s_nf)
    return s_loss[0], grad_x, grads


MESH_ID = pl.DeviceIdType.MESH
VMEM_SPEC = pl.BlockSpec(memory_space=pltpu.VMEM)
HBM_SPEC = pl.BlockSpec(memory_space=pl.ANY)


def _mesh_pos():
    return lax.axis_index("x"), lax.axis_index("y"), lax.axis_index("c")


def _slot(px, py, pc):
    return 4 * px + 2 * py + pc


def _all_gather_bf16(shards):
    n = len(shards)

    def body(*refs):
        ins, outs = refs[:n], refs[n:2 * n]
        land0, scratch = refs[2 * n], refs[2 * n + 1:]
        wpad_ref = outs[0]
        ag = _AllGather(ins, (land0,) + tuple(outs[1:]), scratch)
        ag.start()
        ag.forward()
        ag.finish()
        wpad_ref[ZKR:GA, :] = jnp.zeros((GA - ZKR, D_MODEL), BF16)
        for k in range(N_DEV):
            for at, dst, rows in _slab_segments(k):
                wpad_ref[dst:dst + rows, :] = land0[k, at:at + rows, :]

    return pl.pallas_call(
        body, name="all_gather_weights",
        in_specs=[VMEM_SPEC] * n, out_specs=[VMEM_SPEC] + [HBM_SPEC] * (n - 1),
        out_shape=[SDS((ZTOT, D_MODEL), BF16)] + [SDS((N_DEV,) + s.shape, BF16) for s in shards[1:]],
        scratch_shapes=[pltpu.VMEM((N_DEV,) + shards[0].shape, BF16)] + _ag_scratch([s.shape for s in shards]),
        compiler_params=_cp(),
    )(*shards)


def _ag_scratch(shapes):
    n = len(shapes)
    dma = pltpu.SemaphoreType.DMA
    return [pltpu.VMEM(tuple(s), BF16) for s in shapes] + [dma((_AllGather.COPIES * n,)), dma((_AllGather.COPIES * n,)), dma((n,))]


class _AllGather:
    COPIES = 8

    def __init__(self, in_refs, dest_refs, scratch):
        n = self.n = len(in_refs)
        self.ins, self.dests, self.stage = in_refs, dest_refs, scratch[:n]
        self.send_sems, self.recv_sems, self.local_sems = scratch[n:]
        x, y, c = _mesh_pos()
        self.c, self.me, self.sibling = c, (x, y, c), (x, y, 1 - c)
        self.xn, self.yn, self.diag = (1 - x, y), (x, 1 - y), (1 - x, 1 - y)

    def _halves(self, a):
        rows = self.stage[a].shape[0]
        cut = (rows // 2 + 15) // 16 * 16
        return pl.ds(0, cut), pl.ds(cut, rows - cut)

    def _copy(self, a, k, block, to, from_stage=False, rows=None):
        dst = self.dests[a].at[_slot(*block)]
        src = self.stage[a] if from_stage else dst
        if rows is not None:
            src, dst = src.at[rows], dst.at[rows]
        return pltpu.make_async_remote_copy(
            src_ref=src, dst_ref=dst, send_sem=self.send_sems.at[self.COPIES * a + k],
            recv_sem=self.recv_sems.at[self.COPIES * a + k], device_id=to, device_id_type=MESH_ID)

    def _mine(self):
        return [pltpu.make_async_copy(self.stage[a], self.dests[a].at[_slot(*self.me)], self.local_sems.at[a]) for a in range(self.n)]

    def _first(self, a):
        return [self._copy(a, 0, self.me, self.sibling, True), self._copy(a, 1, self.me, (*self.xn, self.c), True),
                self._copy(a, 2, self.me, (*self.yn, self.c), True)]

    def _relays(self, a):
        lo, hi = self._halves(a)
        return [self._copy(a, 3, (*self.xn, self.c), (*self.yn, self.c), rows=lo),
                self._copy(a, 4, (*self.yn, self.c), (*self.xn, self.c), rows=hi)]

    def _passes(self, a):
        return [self._copy(a, 5 + j, (*chip, self.c), self.sibling) for j, chip in enumerate((self.xn, self.yn, self.diag))]

    def start(self):
        for a in range(self.n):
            self.stage[a][...] = self.ins[a][...].astype(BF16)
        for cp in self._mine():
            cp.start()
        for a in range(self.n):
            for cp in self._first(a):
                cp.start()

    def forward(self):
        for a in range(self.n):
            relays, passes = self._relays(a), self._passes(a)
            for j, chip in enumerate((self.xn, self.yn)):
                self._copy(a, 1 + j, (*chip, self.c), self.me).wait_recv()
                relays[j].start()
                passes[j].start()

    def finish(self):
        for a in range(self.n):
            lo, hi = self._halves(a)
            self._copy(a, 3, (*self.diag, self.c), self.me, rows=lo).wait_recv()
            self._copy(a, 4, (*self.diag, self.c), self.me, rows=hi).wait_recv()
            self._passes(a)[2].start()
        for a in range(self.n):
            self._copy(a, 0, self.sibling, self.me).wait_recv()
            for j, chip in enumerate((self.xn, self.yn, self.diag)):
                self._copy(a, 5 + j, (*chip, 1 - self.c), self.me).wait_recv()
            for cp in self._first(a) + self._relays(a) + self._passes(a):
                cp.wait_send()
        for cp in self._mine():
            cp.wait()


N_CHIPS = 4


def _reduce_scatter(slabs, packed):
    def body(*refs):
        n = len(slabs)
        rs = _ReduceScatter(refs[:n], refs[n], refs[n + 1:2 * n + 1], refs[2 * n + 1], refs[2 * n + 2:])
        rs.start1()
        rs.finish1_start2()
        rs.relay2()
        rs.finish2()

    shapes = [s.shape[1:] for s in slabs]
    return pl.pallas_call(
        body, name="reduce_scatter_grads",
        in_specs=[HBM_SPEC] * len(slabs) + [VMEM_SPEC], out_specs=[VMEM_SPEC] * (len(slabs) + 1),
        out_shape=[SDS(s, F32) for s in shapes] + [SDS(packed.shape, F32)],
        scratch_shapes=_rs_scratch(shapes, packed.shape), compiler_params=_cp(),
    )(*slabs, packed)


def _rs_scratch(shapes, packed_shape):
    n = len(shapes)
    n1, n2 = N_CHIPS * n + 1, _ReduceScatter.L2_COPIES * n + N_CHIPS - 1
    dma = pltpu.SemaphoreType.DMA
    packed = [] if packed_shape is None else [pltpu.VMEM(packed_shape, F32), pltpu.VMEM((N_CHIPS,) + tuple(packed_shape), F32)]
    return ([pltpu.VMEM((N_CHIPS,) + tuple(s), BF16) for s in shapes] * 2 + [pltpu.VMEM((N_CHIPS - 1,) + tuple(s), BF16) for s in shapes] * 2
            + packed + [dma((max(N_CHIPS * n, 1),)), dma((n1,)), dma((n1,)), dma((n2,)), dma((n2,))])


class _ReduceScatter:
    L2_COPIES = 6

    def __init__(self, slab_refs, packed_ref, out_refs, ptot_ref, scratch):
        n = self.n = len(slab_refs)
        self.slabs, self.packed, self.outs, self.ptot = slab_refs, packed_ref, out_refs, ptot_ref
        self.own1, self.land1, self.send2, self.land2 = (scratch[k * n:(k + 1) * n] for k in range(4))
        rest = scratch[4 * n:]
        if packed_ref is not None:
            self.pland1, self.pland2 = rest[:2]
            rest = rest[2:]
        self.loc_sems, self.send1_sems, self.recv1_sems, self.send2_sems, self.recv2_sems = rest
        self.x, self.y, self.c = _mesh_pos()

    def _chip(self, r):
        return (1 - self.x if r & 2 else self.x, 1 - self.y if r & 1 else self.y)

    @staticmethod
    def _remote(src, dst, send_sem, recv_sem, to):
        return pltpu.make_async_remote_copy(src_ref=src, dst_ref=dst, send_sem=send_sem, recv_sem=recv_sem, device_id=to,
                                            device_id_type=MESH_ID)

    def _copies1(self):
        c, sibling = self.c, (self.x, self.y, 1 - self.c)
        cps = []
        for a in range(self.n):
            for r in range(N_CHIPS):
                k = N_CHIPS * a + r
                cps.append(pltpu.make_async_copy(self.slabs[a].at[_slot(*self._chip(r), c)], self.own1[a].at[r], self.loc_sems.at[k]))
                cps.append(self._remote(self.slabs[a].at[_slot(*self._chip(r), 1 - c)], self.land1[a].at[r],
                                        self.send1_sems.at[k], self.recv1_sems.at[k], sibling))
        if self.packed is not None:
            k = N_CHIPS * self.n
            cps.append(self._remote(self.packed, self.pland1, self.send1_sems.at[k], self.recv1_sems.at[k], sibling))
        return cps

    def _halves(self, a):
        rows = self.send2[a].shape[1]
        cut = (rows // 2 + 15) // 16 * 16
        return pl.ds(0, cut), pl.ds(cut, rows - cut)

    def _copies2(self, a):
        lo, hi = self._halves(a)
        s2, l2 = self.send2[a], self.land2[a]
        xn, yn = (*self._chip(2), self.c), (*self._chip(1), self.c)
        plan = [(1, lo, xn), (2, lo, xn), (0, hi, yn), (2, hi, yn), (0, lo, yn), (1, hi, xn)]
        return [self._remote(s2.at[slot].at[rows], l2.at[slot].at[rows], self.send2_sems.at[self.L2_COPIES * a + k],
                             self.recv2_sems.at[self.L2_COPIES * a + k], to) for k, (slot, rows, to) in enumerate(plan)]

    def _copies2_packed(self):
        base = self.L2_COPIES * self.n - 1
        return [self._remote(self.pland2.at[0], self.pland2.at[r], self.send2_sems.at[base + r], self.recv2_sems.at[base + r],
                             (*self._chip(r), self.c)) for r in range(1, N_CHIPS)]

    def start1(self):
        for cp in self._copies1():
            cp.start()

    def finish1_start2(self):
        for cp in self._copies1():
            cp.wait()
        for a in range(self.n):
            self.outs[a][...] = self.own1[a][0].astype(F32) + self.land1[a][0].astype(F32)
            for r in range(1, N_CHIPS):
                self.send2[a][r - 1] = (self.own1[a][r].astype(F32) + self.land1[a][r].astype(F32)).astype(BF16)
            for cp in self._copies2(a)[:4]:
                cp.start()
        if self.packed is not None:
            self.pland2[0] = self.packed[...] + self.pland1[...]
            for cp in self._copies2_packed():
                cp.start()

    def relay2(self):
        for a in range(self.n):
            lo, hi = self._halves(a)
            cps, s2, l2 = self._copies2(a), self.send2[a], self.land2[a]
            cps[1].wait_recv()
            s2[0, lo] = (s2[0, lo].astype(F32) + l2[2, lo].astype(F32)).astype(BF16)
            cps[4].start()
            cps[3].wait_recv()
            s2[1, hi] = (s2[1, hi].astype(F32) + l2[2, hi].astype(F32)).astype(BF16)
            cps[5].start()

    def finish2(self):
        for a in range(self.n):
            cps = self._copies2(a)
            for k in (0, 2, 4, 5):
                cps[k].wait_recv()
            for cp in cps:
                cp.wait_send()
            l2 = self.land2[a]
            self.outs[a][...] = self.outs[a][...] + (l2[0].astype(F32) + l2[1].astype(F32))
        if self.packed is not None:
            for cp in self._copies2_packed():
                cp.wait()
            p2 = self.pland2
            self.ptot[...] = (p2[0] + p2[1]) + (p2[2] + p2[3])


def _adamw(ws, gs, ms, vs):
    n = len(ws)

    def body(*refs):
        for k in range(n):
            w, g, m, v = (refs[j * n + k][...] for j in range(4))
            d_ref, nm_ref, nv_ref = (refs[(4 + j) * n + k] for j in range(3))
            m = ADAM_B1 * m + (1.0 - ADAM_B1) * g
            v = ADAM_B2 * v + (1.0 - ADAM_B2) * jnp.square(g)
            m_hat = m / (1.0 - ADAM_B1 ** ADAM_STEP)
            v_hat = v / (1.0 - ADAM_B2 ** ADAM_STEP)
            d_ref[...] = -ADAM_LR * (m_hat / (jnp.sqrt(v_hat) + ADAM_EPS) + ADAM_WD * w)
            nm_ref[...] = m
            nv_ref[...] = v

    outs = pl.pallas_call(
        body, name="adamw", in_specs=[VMEM_SPEC] * (4 * n), out_specs=[VMEM_SPEC] * (3 * n),
        out_shape=[SDS(w.shape, F32) for w in ws] * 3, compiler_params=_cp(),
    )(*ws, *gs, *ms, *vs)
    return outs[:n], outs[n:2 * n], outs[2 * n:]


WEIGHTS = ("norm_in", "w_in", "q_norm", "w_uq", "kv_norm", "w_ukv", "pool_w", "pool_scale", "w_branch_attn", "w_branch_pool",
           "w_out", "norm_final")
SUBLANES = 8


def _pack_rows(a):
    a = a.reshape(-1, LANES)
    return jnp.pad(a, ((0, -a.shape[0] % SUBLANES), (0, 0)))


def _unpack_rows(packed, like):
    out, row = [], 0
    for a in like:
        rows = a.size // LANES
        out.append(packed[row:row + rows].reshape(a.shape))
        row += rows + (-rows % SUBLANES)
    return out


def kernel(x, norm_in, w_in, q_norm, w_uq, kv_norm, w_ukv, pool_w, pool_scale, w_branch_attn, w_branch_pool, w_out, norm_final, loss_target, m_norm_in, m_w_in, m_q_norm, m_w_uq, m_kv_norm, m_w_ukv, m_pool_w, m_pool_scale, m_w_branch_attn, m_w_branch_pool, m_w_out, m_norm_final, v_norm_in, v_w_in, v_q_norm, v_w_uq, v_kv_norm, v_w_ukv, v_pool_w, v_pool_scale, v_w_branch_attn, v_w_branch_pool, v_w_out, v_norm_final):
    w = dict(norm_in=norm_in, w_in=w_in, q_norm=q_norm, w_uq=w_uq, kv_norm=kv_norm, w_ukv=w_ukv, pool_w=pool_w, pool_scale=pool_scale,
             w_branch_attn=w_branch_attn, w_branch_pool=w_branch_pool, w_out=w_out, norm_final=norm_final)
    m = dict(norm_in=m_norm_in, w_in=m_w_in, q_norm=m_q_norm, w_uq=m_w_uq, kv_norm=m_kv_norm, w_ukv=m_w_ukv, pool_w=m_pool_w,
             pool_scale=m_pool_scale, w_branch_attn=m_w_branch_attn, w_branch_pool=m_w_branch_pool, w_out=m_w_out, norm_final=m_norm_final)
    v = dict(norm_in=v_norm_in, w_in=v_w_in, q_norm=v_q_norm, w_uq=v_w_uq, kv_norm=v_kv_norm, w_ukv=v_w_ukv, pool_w=v_pool_w,
             pool_scale=v_pool_scale, w_branch_attn=v_w_branch_attn, w_branch_pool=v_w_branch_pool, w_out=v_w_out, norm_final=v_norm_final)

    def as2d(name, a):
        if name == "w_in":
            return a.T
        if name == "w_uq":
            return a
        if name == "w_ukv":
            return a.reshape(a.shape[0], -1)
        if name == "pool_w":
            return a.reshape(-1, GROUP)
        return a.reshape(1, -1) if a.ndim == 1 else a

    def unshape(name, a):
        return a.T if name == "w_in" else a.reshape(w[name].shape)

    w_in_pad, w_uq_full, w_ukv_full = _all_gather_bf16([as2d(k, w[k]) for k in ("w_in", "w_uq", "w_ukv")])
    loss, grad_x, g2d = _local_step(
        x.reshape(x.shape[1:]), loss_target.reshape(x.shape[1:]), norm_in, w_in_pad, q_norm,
        w_uq_full.reshape(Q_RANK, HEADS, NOPE + ROPE), kv_norm, w_ukv_full.reshape(KV_RANK, HEADS, NOPE + VDIM),
        pool_w, pool_scale, [w_branch_attn, w_branch_pool, w_out], norm_final)

    deltas, new_m, new_v = _adamw([as2d(k, w[k]) for k in WEIGHTS], [g2d[k] for k in WEIGHTS],
                                  [as2d(k, m[k]) for k in WEIGHTS], [as2d(k, v[k]) for k in WEIGHTS])
    shaped = lambda arrs: [unshape(k, a) for k, a in zip(WEIGHTS, arrs)]
    return (loss, grad_x.reshape(x.shape), *shaped([g2d[k] for k in WEIGHTS]), *shaped(deltas), *shaped(new_m), *shaped(new_v))
```

```python
import functools

import jax
import jax.numpy as jnp
import numpy as np
from jax import lax
from jax.experimental import pallas as pl
from jax.experimental.pallas import tpu as pltpu

F32 = jnp.float32
BF16 = jnp.bfloat16
SDS = jax.ShapeDtypeStruct

D_MODEL = 1024
HEADS = 8
NOPE = 64
ROPE = 32
VDIM = 64
Q_RANK = 384
KV_RANK = 256
MLA_W = HEADS * VDIM
POOL_W = 512
POOL_GROUPS = 4
GROUP = POOL_W // POOL_GROUPS
CHUNK = 64
ROPE_THETA = 10000.0
EPS = 1e-6
SCALE = (NOPE + ROPE) ** -0.5
LOG2E = 1.4426950408889634
LN2 = 0.6931471805599453
QK_SCALE_LOG2 = SCALE * LOG2E
IN_TOTAL = 4256
ADAM_LR, ADAM_B1, ADAM_B2, ADAM_EPS, ADAM_WD, ADAM_STEP = 0.001, 0.9, 0.999, 1e-08, 0.01, 10

N_DEV = 8
LANES = 128
HEAD_PAD = LANES
HW = HEADS * HEAD_PAD

ZQ, ZKV, ZKR, GA, UP, GP, GM, ZTOT = 0, 384, 640, 768, 1280, 1792, 2304, 4352
FRONT_W = GA
ZKR_ORIG = 640

VMEM_LIMIT = 62 * 1024 * 1024


def _cp(sem=None, **kw):
    if sem is not None:
        kw["dimension_semantics"] = sem
    return pltpu.CompilerParams(vmem_limit_bytes=VMEM_LIMIT, **kw)


def _mm(a, b):
    return lax.dot_general(a, b, (((1,), (0,)), ((), ())), preferred_element_type=F32)


def _mm_nt(a, b):
    return lax.dot_general(a, b, (((1,), (1,)), ((), ())), preferred_element_type=F32)


def _mm_tn(a, b):
    return lax.dot_general(a, b, (((0,), (0,)), ((), ())), preferred_element_type=F32)


def _row_spec(tm, w):
    return pl.BlockSpec((tm, w), lambda i: (i, 0))


def _full_spec(a):
    nd = len(a.shape)
    return pl.BlockSpec(a.shape, lambda *_: (0,) * nd)


def _rope(v, c, sa, sb, sign):
    n = v.shape[-1]
    reps = n // LANES
    if reps > 1:
        c, sa, sb = (jnp.tile(t, (1, reps)) for t in (c, sa, sb))
    up = pltpu.roll(v, n - ROPE // 2, 1)
    dn = pltpu.roll(v, ROPE // 2, 1)
    return v * c + sign * (up * sa + dn * sb)


def _rope_tables(T):
    half = ROPE // 2
    inv_freq = np.float32(ROPE_THETA) ** (-np.arange(half, dtype=np.float32) / np.float32(half))
    ang = np.arange(T, dtype=np.float32)[:, None] * inv_freq[None, :].astype(np.float32)
    cos, sin = np.cos(ang.astype(np.float64)).astype(np.float32), np.sin(ang.astype(np.float64)).astype(np.float32)
    z16 = np.zeros((T, half), np.float32)
    z32 = np.zeros((T, LANES - NOPE - ROPE), np.float32)
    c = np.concatenate([np.ones((T, NOPE), np.float32), cos, cos, z32], axis=1)
    sa = np.concatenate([np.zeros((T, NOPE), np.float32), -sin, z16, z32], axis=1)
    sb = np.concatenate([np.zeros((T, NOPE), np.float32), z16, sin, z32], axis=1)
    return jnp.asarray(c), jnp.asarray(sa), jnp.asarray(sb)


def _silu_parts(g):
    sg = jax.nn.sigmoid(g)
    return g * sg, sg + g * sg * (1.0 - sg)


def _in_proj(x2, norm_in, w_in_pad, q_norm, wuq_pad, kv_norm, wukv, rc, rsa, rsb, tm, late_shards):
    T = x2.shape[0]
    steps = T // tm
    n = len(late_shards)

    def body(x_ref, nin_ref, win_ref, qn_ref, wuq_ref, kvn_ref, wukv_ref, c_ref, sa_ref, sb_ref, *rest):
        hn_ref, zgm_ref, zga_ref, zup_ref, zgp_ref, zfr_ref, q_ref, k_ref, v_ref, vt_ref = rest[n:n + 10]
        ag = _AllGather(rest[:n], rest[n + 10:2 * n + 10], rest[2 * n + 10:])
        step = pl.program_id(0)
        pl.when(step == 0)(ag.start)
        pl.when(step == min(3, steps - 1))(ag.forward)
        xf = x_ref[...]
        r = lax.rsqrt(jnp.mean(xf * xf, axis=-1, keepdims=True) + EPS)
        hn = (xf * r * nin_ref[...]).astype(BF16)
        hn_ref[...] = hn
        z = _mm_nt(hn, win_ref[...])
        zgm_ref[...] = z[:, GM:ZTOT]
        zga_ref[...] = z[:, GA:UP]
        zup_ref[...] = z[:, UP:GP]
        zgp_ref[...] = z[:, GP:GM]
        zfr_ref[...] = z[:, ZQ:GA]
        zq, zkv, zkr = z[:, ZQ:ZKV], z[:, ZKV:ZKR], z[:, ZKR:GA]
        c, sa, sb = c_ref[...], sa_ref[...], sb_ref[...]
        rq = lax.rsqrt(jnp.mean(zq * zq, axis=-1, keepdims=True) + EPS)
        cq = (zq * rq * qn_ref[...]).astype(BF16)
        q = _rope(_mm(cq, wuq_ref[...]), c, sa, sb, 1.0)
        q_ref[...] = (q * QK_SCALE_LOG2).astype(BF16)
        rkv = lax.rsqrt(jnp.mean(zkv * zkv, axis=-1, keepdims=True) + EPS)
        ckv = (zkv * rkv * kvn_ref[...]).astype(BF16)
        kv = _mm(ckv, wukv_ref[...])
        kr = _rope(zkr, c, sa, sb, 1.0)
        lane = lax.broadcasted_iota(jnp.int32, kv.shape, 1) % LANES
        k_ref[...] = jnp.where(lane < NOPE, kv, jnp.tile(kr, (1, HEADS))).astype(BF16)
        v = jnp.where(lane < NOPE, 1.0, kv).astype(BF16)
        v_ref[...] = v
        vt_ref[...] = v.T
        pl.when(step == steps - 1)(ag.finish)

    ins = (x2, norm_in, w_in_pad, q_norm, wuq_pad, kv_norm, wukv, rc, rsa, rsb)
    in_specs = [_row_spec(tm, D_MODEL), _full_spec(norm_in), _full_spec(w_in_pad), _full_spec(q_norm), _full_spec(wuq_pad),
                _full_spec(kv_norm), _full_spec(wukv), _row_spec(tm, LANES), _row_spec(tm, LANES), _row_spec(tm, LANES)]
    widths = [(D_MODEL, BF16), (ZTOT - GM, F32), (UP - GA, F32), (GP - UP, F32), (GM - GP, F32), (FRONT_W, F32),
              (HW, BF16), (HW, BF16), (HW, BF16)]
    return pl.pallas_call(
        body, name="in_proj", grid=(steps,), in_specs=in_specs + [_full_spec(s) for s in late_shards],
        out_specs=[_row_spec(tm, w) for w, _ in widths] + [pl.BlockSpec((HW, tm), lambda i: (0, i))] + [HBM_SPEC] * n,
        out_shape=[SDS((T, w), dt) for w, dt in widths] + [SDS((HW, T), BF16)]
        + [SDS((N_DEV,) + s.shape, BF16) for s in late_shards],
        scratch_shapes=_ag_scratch([s.shape for s in late_shards]), compiler_params=_cp(("arbitrary",)),
    )(*ins, *late_shards)


def _chunk_mask(n_q, n_k, q_off, transposed):
    shape = (n_k, n_q) if transposed else (n_q, n_k)
    q = (lax.broadcasted_iota(jnp.int32, shape, 1 if transposed else 0) + q_off) // CHUNK
    k = lax.broadcasted_iota(jnp.int32, shape, 0 if transposed else 1) // CHUNK
    return k <= q


def _store_pair_rows(ref, k, pair):
    t = pair.T
    ref[k, 0:1, :] = t[0:1, :]
    ref[k, 1:2, :] = t[VDIM:VDIM + 1, :]


def _attn_fwd(q_att, k_att, vt_att, tq, hps):
    T = q_att.shape[0]
    head_lanes = [slice(h * LANES, (h + 1) * LANES) for h in range(hps)]

    def body(q_ref, k_ref, vt_ref, o_ref, lser_ref):
        i = pl.program_id(1)
        mask = _chunk_mask(tq, tq, 0, True)
        lane = lax.broadcasted_iota(jnp.int32, (tq, LANES), 1)
        qs = [q_ref[:, hs] for hs in head_lanes]

        def step(j, carry, masked):
            off = pl.multiple_of(j * tq, tq)
            sts = [_mm_nt(k_ref[pl.ds(off, tq), hs], qh) for qh, hs in zip(qs, head_lanes)]
            if masked:
                sts = [jnp.where(mask, st, -jnp.inf) for st in sts]
            ms = [jnp.maximum(m, jnp.max(st, axis=0, keepdims=True)) for (m, _), st in zip(carry, sts)]
            pts = [jnp.exp2(st - m_new).astype(BF16) for st, m_new in zip(sts, ms)]
            return tuple((m_new, jnp.exp2(m - m_new) * acc + _mm(vt_ref[hs, pl.ds(off, tq)], pt))
                         for (m, acc), m_new, pt, hs in zip(carry, ms, pts, head_lanes))

        init = ((jnp.full((1, tq), -jnp.inf, F32), jnp.zeros((LANES, tq), F32)),) * hps
        res = step(i, lax.fori_loop(0, i, functools.partial(step, masked=False), init), True)
        for pair in range(hps // 2):
            (ma, acca), (mb, accb) = res[2 * pair], res[2 * pair + 1]
            la, lb = acca[:1], accb[:1]
            oa, ob = (acca / la).T, (accb / lb).T
            o_ref[:, pair * LANES:(pair + 1) * LANES] = jnp.where(lane < VDIM, pltpu.roll(oa, VDIM, 1), ob)
            lser_ref[pair, 0:1, :] = ma + jnp.log2(la)
            lser_ref[pair, 1:2, :] = mb + jnp.log2(lb)

    qspec = pl.BlockSpec((tq, hps * LANES), lambda p, i: (i, p))
    kspec = pl.BlockSpec((T, hps * LANES), lambda p, i: (0, p))
    vspec = pl.BlockSpec((hps * LANES, T), lambda p, i: (p, 0))
    ospec = pl.BlockSpec((tq, hps * VDIM), lambda p, i: (i, p))
    return pl.pallas_call(
        body, name="attn_fwd", grid=(HEADS // hps, T // tq), in_specs=[qspec, kspec, vspec],
        out_specs=[ospec, pl.BlockSpec((hps // 2, 2, tq), lambda p, i: (p, 0, i))],
        out_shape=[SDS((T, MLA_W), F32), SDS((HEADS // 2, 2, T), F32)],
        compiler_params=_cp(("parallel", "parallel")),
    )(q_att, k_att, vt_att)


def _pick(g, vals):
    out = vals[-1]
    for k in range(len(vals) - 2, -1, -1):
        out = jnp.where(g == k, vals[k], out)
    return out


def _window_sum(u, g, forward):
    T = u.shape[0]
    row = lax.broadcasted_iota(jnp.int32, u.shape, 0)

    def sh(s, k):
        if forward:
            return jnp.where(row >= k, pltpu.roll(s, k, 0), 0.0)
        return jnp.where(row < T - k, pltpu.roll(s, T - k, 0), 0.0)

    sums, s = [], u
    for k in (1, 2, 4, 8):
        s = s + sh(s, k)
        sums.append(s)
    return _pick(g, sums)


def _pool_count(shape, g):
    row = lax.broadcasted_iota(jnp.int32, shape, 0)
    return jnp.minimum(row + 1, lax.shift_left(jnp.int32(2), g)).astype(F32)


def _pool_fwd(zup, zgp, pool_w, pool_scale):
    T = zup.shape[0]

    def body(u_ref, g_ref, w_ref, sc_ref, y_ref):
        g = pl.program_id(0)
        u = u_ref[...]
        d = _window_sum(u, g, True) / _pool_count(u.shape, g) - u
        lin = _mm(d.astype(BF16), w_ref[0].astype(BF16))
        silu, _ = _silu_parts(g_ref[...])
        y_ref[...] = (lin * sc_ref[...] * silu).astype(BF16)

    col = pl.BlockSpec((T, GROUP), lambda g: (0, g))
    return pl.pallas_call(
        body, name="pool_fwd", grid=(POOL_GROUPS,),
        in_specs=[col, col, pl.BlockSpec((1, GROUP, GROUP), lambda g: (g, 0, 0)), pl.BlockSpec((1, GROUP), lambda g: (0, g))],
        out_specs=col, out_shape=SDS((T, POOL_W), BF16), compiler_params=_cp(("parallel",)),
    )(zup, zgp, pool_w, pool_scale)


def _pool_bwd(zup, zgp, dyp, pool_w, pool_scale):
    T = zup.shape[0]

    def body(u_ref, g_ref, dy_ref, w_ref, sc_ref, du_ref, dg_ref, gw_ref, gsc_ref):
        g = pl.program_id(0)
        u = u_ref[...]
        cnt = _pool_count(u.shape, g)
        d = (_window_sum(u, g, True) / cnt - u).astype(BF16)
        wb = w_ref[0].astype(BF16)
        lin = _mm(d, wb)
        sc = sc_ref[...]
        silu, dsilu = _silu_parts(g_ref[...])
        dy = dy_ref[...]
        dg_ref[...] = (dy * lin * sc * dsilu).astype(BF16)
        dpre = dy * silu
        gsc_ref[...] = jnp.sum(dpre * lin, axis=0, keepdims=True)
        dlin = (dpre * sc).astype(BF16)
        gw_ref[0] = _mm_tn(d, dlin)
        dd = _mm_nt(dlin, wb)
        du_ref[...] = (_window_sum(dd / cnt, g, False) - dd).astype(BF16)

    col = pl.BlockSpec((T, GROUP), lambda g: (0, g))
    wspec = pl.BlockSpec((1, GROUP, GROUP), lambda g: (g, 0, 0))
    vspec = pl.BlockSpec((1, GROUP), lambda g: (0, g))
    return pl.pallas_call(
        body, name="pool_bwd", grid=(POOL_GROUPS,), in_specs=[col, col, col, wspec, vspec], out_specs=[col, col, wspec, vspec],
        out_shape=[SDS((T, POOL_W), BF16), SDS((T, POOL_W), BF16), SDS((POOL_GROUPS, GROUP, GROUP), F32), SDS((1, POOL_W), F32)],
        compiler_params=_cp(("parallel",)),
    )(zup, zgp, dyp, pool_w, pool_scale)


def _tail(x2, tgt, o, zga, ypool, zgm, wba, wbp, wout, norm_final, tm):
    T = x2.shape[0]
    steps = T // tm
    cols = D_MODEL // N_DEV

    def body(x_ref, tgt_ref, o_ref, zga_ref, yp_ref, zgm_ref, wba_ref, wbp_ref, wout_ref, nf_ref,
             loss_ref, dh_ref, dgm_ref, doop_ref, dga_ref, dcapr_ref, dyp_ref, swout_ref, swba_ref, swbp_ref, gnf_ref,
             gwout_ref, gwba_ref, gwbp_ref):
        @pl.when(pl.program_id(0) == 0)
        def _():
            for ref in (loss_ref, gwout_ref, gwba_ref, gwbp_ref, gnf_ref):
                ref[...] = jnp.zeros_like(ref)

        o_v = o_ref[...]
        silu, dsilu = _silu_parts(zga_ref[...])
        ya = (o_v * silu).astype(BF16)
        yp = yp_ref[...]
        wba_v = jnp.concatenate([wba_ref[k] for k in range(N_DEV)], axis=1)
        wbp_v = jnp.concatenate([wbp_ref[k] for k in range(N_DEV)], axis=1)
        wout_v = wout_ref[...]
        a = _mm(ya, wba_v)
        p = _mm(yp, wbp_v)
        gate = jax.nn.sigmoid(zgm_ref[...])
        ga, gp = gate[:, :D_MODEL], gate[:, D_MODEL:]
        mg = (ga * a + gp * p).astype(BF16)
        h = x_ref[...] + _mm(mg, wout_v)
        r = lax.rsqrt(jnp.mean(h * h, axis=-1, keepdims=True) + EPS)
        gf = nf_ref[...]
        hr = h * r
        e = hr * gf - tgt_ref[...]
        loss_ref[...] += (0.5 / D_MODEL) * jnp.sum(e * e)
        dy = e * (1.0 / D_MODEL)
        gnf_ref[...] += jnp.sum(dy * hr, axis=0, keepdims=True)
        u = dy * gf
        dh = r * (u - hr * jnp.mean(u * hr, axis=-1, keepdims=True))
        dh_ref[...] = dh
        dhb = dh.astype(BF16)
        dmg = _mm_nt(dhb, wout_v)
        gwout_ref[...] += _mm_tn(mg, dhb)
        dgm_ref[:, :D_MODEL] = (dmg * a * ga * (1.0 - ga)).astype(BF16)
        dgm_ref[:, D_MODEL:] = (dmg * p * gp * (1.0 - gp)).astype(BF16)
        dab = (dmg * ga).astype(BF16)
        dpb = (dmg * gp).astype(BF16)
        dya = _mm_nt(dab, wba_v)
        gwba_ref[...] += _mm_tn(ya, dab)
        dyp_ref[...] = _mm_nt(dpb, wbp_v)
        gwbp_ref[...] += _mm_tn(yp, dpb)
        do = dya * silu
        dga_ref[...] = (dya * o_v * dsilu).astype(BF16)
        prod = do * o_v
        lo = lax.broadcasted_iota(jnp.int32, (tm, LANES), 1) < VDIM
        for pair in range(HEADS // 2):
            ls = slice(pair * LANES, (pair + 1) * LANES)
            do_p, prod_p = do[:, ls], prod[:, ls]
            dcap_a = jnp.sum(jnp.where(lo, prod_p, 0.0), axis=-1, keepdims=True)
            dcap_b = jnp.sum(jnp.where(lo, 0.0, prod_p), axis=-1, keepdims=True)
            _store_pair_rows(dcapr_ref, pair, jnp.where(lo, dcap_a, dcap_b))
            doop_ref[:, 2 * pair * LANES:(2 * pair + 1) * LANES] = jnp.where(lo, 0.0, pltpu.roll(do_p, VDIM, 1)).astype(BF16)
            doop_ref[:, (2 * pair + 1) * LANES:(2 * pair + 2) * LANES] = jnp.where(lo, 0.0, do_p).astype(BF16)

        @pl.when(pl.program_id(0) == steps - 1)
        def _():
            for k in range(N_DEV):
                swout_ref[k] = gwout_ref[k * cols:(k + 1) * cols, :].astype(BF16)
                swba_ref[k] = gwba_ref[:, k * cols:(k + 1) * cols].astype(BF16)
                swbp_ref[k] = gwbp_ref[:, k * cols:(k + 1) * cols].astype(BF16)

    ins = (x2, tgt, o, zga, ypool, zgm, wba, wbp, wout, norm_final)
    in_specs = [_row_spec(tm, D_MODEL), _row_spec(tm, D_MODEL), _row_spec(tm, MLA_W), _row_spec(tm, MLA_W), _row_spec(tm, POOL_W),
                _row_spec(tm, 2 * D_MODEL), _full_spec(wba), _full_spec(wbp), _full_spec(wout), _full_spec(norm_final)]
    outs = [SDS((8, LANES), F32), SDS((T, D_MODEL), F32), SDS((T, 2 * D_MODEL), BF16), SDS((T, HW), BF16), SDS((T, MLA_W), BF16),
            SDS((HEADS // 2, 2, T), F32), SDS((T, POOL_W), F32),
            SDS((N_DEV, cols, D_MODEL), BF16), SDS((N_DEV, MLA_W, cols), BF16), SDS((N_DEV, POOL_W, cols), BF16), SDS((1, D_MODEL), F32)]
    out_specs = [_full_spec(outs[0]), _row_spec(tm, D_MODEL), _row_spec(tm, 2 * D_MODEL), _row_spec(tm, HW), _row_spec(tm, MLA_W),
                 pl.BlockSpec((HEADS // 2, 2, tm), lambda i: (0, 0, i)), _row_spec(tm, POOL_W),
                 _full_spec(outs[7]), _full_spec(outs[8]), _full_spec(outs[9]), _full_spec(outs[10])]
    return pl.pallas_call(
        body, name="tail", grid=(steps,), in_specs=in_specs, out_specs=out_specs, out_shape=outs,
        scratch_shapes=[pltpu.VMEM((D_MODEL, D_MODEL), F32), pltpu.VMEM((MLA_W, D_MODEL), F32), pltpu.VMEM((POOL_W, D_MODEL), F32)],
        compiler_params=_cp(("arbitrary",)),
    )(*ins)


def _attn_bwd(q_att, k_att, v_att, doop, lse_rows, dcap_rows, tq, hps, slabs, packed):
    T = q_att.shape[0]
    nq = T // tq
    n = len(slabs)
    groups = HEADS // hps
    head_lanes = [slice(h * LANES, (h + 1) * LANES) for h in range(hps)]

    def body(q_ref, k_ref, v_ref, doop_ref, lse_ref, dcap_ref, *rest):
        slab_refs, packed_ref = rest[:n], rest[n]
        dq_ref, dkv_ref, dkr_ref = rest[n + 1:n + 4]
        sum_refs, ptot_ref = rest[n + 4:2 * n + 4], rest[2 * n + 4]
        dq_acc = rest[2 * n + 5]
        rs = _ReduceScatter(slab_refs, packed_ref, sum_refs, ptot_ref, rest[2 * n + 6:])
        group, j = pl.program_id(0), pl.program_id(1)
        tick, last = group * nq + j, groups * nq - 1
        pl.when(tick == 0)(rs.start1)
        pl.when(tick == min(1, last))(rs.finish1_start2)
        pl.when(tick == min(3, last))(rs.relay2)
        mask = _chunk_mask(tq, tq, 0, True)
        lane = lax.broadcasted_iota(jnp.int32, (tq, LANES), 1)
        ks = [k_ref[:, hs] for hs in head_lanes]
        vs = [v_ref[:, hs] for hs in head_lanes]
        kts = [kh.T for kh in ks]

        @pl.when(j == 0)
        def _():
            dq_acc[...] = jnp.zeros_like(dq_acc)

        def step(i, carry, masked):
            rows = pl.ds(pl.multiple_of(i * tq, tq), tq)
            heads = range(hps)
            stat = lambda h: (h // 2, slice(h % 2, h % 2 + 1), rows)
            qhs = [q_ref[rows, hs] for hs in head_lanes]
            doops = [doop_ref[rows, hs] for hs in head_lanes]
            sts = [_mm_nt(ks[h], qhs[h]) for h in heads]
            dpts = [_mm_nt(vs[h], doops[h]) for h in heads]
            pts = [jnp.exp2(sts[h] - lse_ref[stat(h)]) for h in heads]
            if masked:
                pts = [jnp.where(mask, pt, 0.0) for pt in pts]
            dsts = [(pts[h] * (dpts[h] - dcap_ref[stat(h)])).astype(BF16) for h in heads]
            dvs = [_mm(pts[h].astype(BF16), doops[h]) for h in heads]
            dks = [_mm(dsts[h], qhs[h]) for h in heads]
            for h, hs in enumerate(head_lanes):
                dq_acc[hs, rows] += _mm(kts[h], dsts[h])
            return tuple((dk + dks[h], dv + dvs[h]) for h, (dk, dv) in enumerate(carry))

        zero = jnp.zeros((tq, LANES), F32)
        carry = step(j, ((zero, zero),) * hps, True)
        res = lax.fori_loop(j + 1, nq, functools.partial(step, masked=False), carry)
        dkr = None
        for (dk, dv), hs in zip(res, head_lanes):
            dk = dk * LN2
            dkv_ref[:, hs] = jnp.where(lane < NOPE, dk, dv).astype(BF16)
            dkr = dk if dkr is None else dkr + dk
        dkr_ref[0] = jnp.where((lane >= NOPE) & (lane < NOPE + ROPE), dkr, 0.0)

        @pl.when(j == nq - 1)
        def _():
            dq_ref[...] = (dq_acc[...] * SCALE).T.astype(BF16)

        pl.when(tick == last)(rs.finish2)

    kspec = pl.BlockSpec((tq, hps * LANES), lambda p, j: (j, p))
    qspec = pl.BlockSpec((T, hps * LANES), lambda p, j: (0, p))
    rspec = pl.BlockSpec((hps // 2, 2, T), lambda p, j: (p, 0, 0))
    sums = [SDS(s.shape[1:], F32) for s in slabs] + [SDS(packed.shape, F32)]
    return pl.pallas_call(
        body, name="attn_bwd", grid=(groups, nq),
        in_specs=[qspec, kspec, kspec, qspec, rspec, rspec] + [HBM_SPEC] * n + [_full_spec(packed)],
        out_specs=[qspec, kspec, pl.BlockSpec((1, tq, LANES), lambda p, j: (p, j, 0))] + [_full_spec(s) for s in sums],
        out_shape=[SDS((T, HW), BF16), SDS((T, HW), BF16), SDS((groups, T, LANES), F32)] + sums,
        scratch_shapes=[pltpu.VMEM((hps * LANES, T), F32)] + _rs_scratch([s.shape for s in sums[:-1]], packed.shape),
        compiler_params=_cp(("arbitrary", "arbitrary")),
    )(q_att, k_att, v_att, doop, lse_rows, dcap_rows, *slabs, packed)


def _rms_bwd(z, gain, dout):
    r = lax.rsqrt(jnp.mean(z * z, axis=-1, keepdims=True) + EPS)
    zr = z * r
    u = dout * gain
    return r * (u - zr * jnp.mean(u * zr, axis=-1, keepdims=True)), jnp.sum(dout * zr, axis=0, keepdims=True)


def _mla_bwd(dq_att, dkv_nat, dkr4, zfr, q_norm, wuq_pad, kv_norm, wukv, rc, rsa, rsb, tm):
    T = dq_att.shape[0]

    def body(dq_ref, dkv_ref, dkr_ref, zfr_ref, qn_ref, wuq_ref, kvn_ref, wukv_ref, c_ref, sa_ref, sb_ref,
             dfr_ref, gwuq_ref, gwukv_ref, gqn_ref, gkvn_ref):
        @pl.when(pl.program_id(0) == 0)
        def _():
            for ref in (gwuq_ref, gwukv_ref, gqn_ref, gkvn_ref):
                ref[...] = jnp.zeros_like(ref)

        c, sa, sb = c_ref[...], sa_ref[...], sb_ref[...]
        zq, zkv = zfr_ref[:, :Q_RANK], zfr_ref[:, Q_RANK:Q_RANK + KV_RANK]
        qn, kvn = qn_ref[...], kvn_ref[...]
        cq = (zq * lax.rsqrt(jnp.mean(zq * zq, axis=-1, keepdims=True) + EPS) * qn).astype(BF16)
        ckv = (zkv * lax.rsqrt(jnp.mean(zkv * zkv, axis=-1, keepdims=True) + EPS) * kvn).astype(BF16)
        dq = _rope(dq_ref[...].astype(F32), c, sa, sb, -1.0).astype(BF16)
        gwuq_ref[...] += _mm_tn(cq, dq)
        dzq, gqn = _rms_bwd(zq, qn, _mm_nt(dq, wuq_ref[...]))
        gqn_ref[...] += gqn
        dkv = dkv_ref[...]
        gwukv_ref[...] += _mm_tn(ckv, dkv)
        dzkv, gkvn = _rms_bwd(zkv, kvn, _mm_nt(dkv, wukv_ref[...]))
        gkvn_ref[...] += gkvn
        dkr = functools.reduce(lambda a, b: a + b, [dkr_ref[g] for g in range(dkr4.shape[0])])
        dfr_ref[:, :Q_RANK] = dzq.astype(BF16)
        dfr_ref[:, Q_RANK:Q_RANK + KV_RANK] = dzkv.astype(BF16)
        dfr_ref[:, Q_RANK + KV_RANK:] = _rope(dkr, c, sa, sb, -1.0).astype(BF16)

    ins = (dq_att, dkv_nat, dkr4, zfr, q_norm, wuq_pad, kv_norm, wukv, rc, rsa, rsb)
    in_specs = [_row_spec(tm, HW), _row_spec(tm, HW), pl.BlockSpec((dkr4.shape[0], tm, LANES), lambda i: (0, i, 0)), _row_spec(tm, FRONT_W),
                _full_spec(q_norm), _full_spec(wuq_pad), _full_spec(kv_norm), _full_spec(wukv),
                _row_spec(tm, LANES), _row_spec(tm, LANES), _row_spec(tm, LANES)]
    outs = [SDS((T, FRONT_W), BF16), SDS((Q_RANK, HW), F32), SDS((KV_RANK, HW), F32), SDS((1, Q_RANK), F32), SDS((1, KV_RANK), F32)]
    out_specs = [_row_spec(tm, FRONT_W)] + [_full_spec(s) for s in outs[1:]]
    return pl.pallas_call(
        body, name="mla_bwd", grid=(T // tm,), in_specs=in_specs, out_specs=out_specs, out_shape=outs,
        compiler_params=_cp(("arbitrary",)),
    )(*ins)


_DZ_COLS = ((GM, ZTOT), (GA, UP), (UP, GP), (GP, GM), (ZQ, GA))


def _in_proj_bwd_x(dzs, x2, dh, norm_in, w_in_pad, tm, slabs):
    T = x2.shape[0]
    steps = T // tm
    n = len(slabs)

    def body(d0, d1, d2, d3, d4, x_ref, dh_ref, nin_ref, win_ref, *rest):
        slab_refs, (gx_ref, gnin_ref), sum_refs = rest[:n], rest[n:n + 2], rest[n + 2:2 * n + 2]
        rs = _ReduceScatter(slab_refs, None, sum_refs, None, rest[2 * n + 2:])
        step = pl.program_id(0)

        @pl.when(step == 0)
        def _():
            gnin_ref[...] = jnp.zeros_like(gnin_ref)
            rs.start1()

        pl.when(step == min(2, steps - 1))(rs.finish1_start2)
        pl.when(step == min(steps * 11 // 16, steps - 1))(rs.relay2)
        dhn = None
        for ref, (lo, hi) in zip((d0, d1, d2, d3, d4), _DZ_COLS):
            t = _mm(ref[...], win_ref[lo:hi, :])
            dhn = t if dhn is None else dhn + t
        dx, gnin = _rms_bwd(x_ref[...], nin_ref[...], dhn)
        gnin_ref[...] += gnin
        gx_ref[...] = dx + dh_ref[...]
        pl.when(step == steps - 1)(rs.finish2)

    in_specs = [_row_spec(tm, hi - lo) for lo, hi in _DZ_COLS] + [_row_spec(tm, D_MODEL), _row_spec(tm, D_MODEL),
                                                                  _full_spec(norm_in), _full_spec(w_in_pad)] + [HBM_SPEC] * n
    sums = [SDS(s.shape[1:], F32) for s in slabs]
    outs = [SDS((T, D_MODEL), F32), SDS((1, D_MODEL), F32)] + sums
    return pl.pallas_call(
        body, name="in_proj_bwd_x", grid=(steps,), in_specs=in_specs,
        out_specs=[_row_spec(tm, D_MODEL), _full_spec(outs[1])] + [_full_spec(s) for s in sums],
        out_shape=outs, scratch_shapes=_rs_scratch([s.shape for s in sums], None), compiler_params=_cp(("arbitrary",)),
    )(*dzs, x2, dh, norm_in, w_in_pad, *slabs)


SLAB_ROWS = IN_TOTAL // N_DEV


def _slab_segments(k):
    cuts = [(0, ZKR_ORIG, 0), (ZKR_ORIG, ZKR_ORIG + ROPE, NOPE), (ZKR_ORIG + ROPE, IN_TOTAL, LANES - ROPE)]
    lo, hi = k * SLAB_ROWS, (k + 1) * SLAB_ROWS
    return [(max(lo, a) - lo, max(lo, a) + shift, min(hi, b) - max(lo, a)) for a, b, shift in cuts if min(hi, b) > max(lo, a)]


def _in_proj_bwd_w(dzs, hn, tm):
    T = hn.shape[0]
    steps = T // tm

    def body(d0, d1, d2, d3, d4, hn_ref, slab_ref, acc_ref):
        @pl.when(pl.program_id(0) == 0)
        def _():
            acc_ref[...] = jnp.zeros_like(acc_ref)

        hn_v = hn_ref[...]
        for ref, (lo, hi) in zip((d0, d1, d2, d3, d4), _DZ_COLS):
            acc_ref[lo:hi, :] += _mm_tn(ref[...], hn_v)

        @pl.when(pl.program_id(0) == steps - 1)
        def _():
            for k in range(N_DEV):
                for at, src, rows in _slab_segments(k):
                    slab_ref[k, at:at + rows, :] = acc_ref[src:src + rows, :].astype(BF16)

    in_specs = [_row_spec(tm, hi - lo) for lo, hi in _DZ_COLS] + [_row_spec(tm, D_MODEL)]
    out = SDS((N_DEV, SLAB_ROWS, D_MODEL), BF16)
    return pl.pallas_call(
        body, name="in_proj_bwd_w", grid=(steps,), in_specs=in_specs, out_specs=_full_spec(out), out_shape=out,
        scratch_shapes=[pltpu.VMEM((ZTOT, D_MODEL), F32)], compiler_params=_cp(("arbitrary",)),
    )(*dzs, hn)


def _local_step(x2, tgt, norm_in, w_in_pad, q_norm, w_uq, kv_norm, w_ukv, pool_w, pool_scale, late_shards, norm_final):
    T = x2.shape[0]
    tm, tm_small, tq = min(512, T), min(256, T), min(512, T)
    heads_fwd, heads_bwd = 4, 4
    row = lambda v: v.reshape(1, -1)
    wuq_pad = jnp.pad(w_uq, ((0, 0), (0, 0), (0, HEAD_PAD - NOPE - ROPE))).reshape(Q_RANK, HW)
    wukv = w_ukv.reshape(KV_RANK, HW)
    rc, rsa, rsb = _rope_tables(T)

    hn, zgm, zga, zup, zgp, zfr, q_att, k_att, v_att, vt_att, w_ba, w_bp, w_out = _in_proj(
        x2, row(norm_in), w_in_pad, row(q_norm), wuq_pad, row(kv_norm), wukv, rc, rsa, rsb, tm, late_shards)
    w_out = w_out.reshape(D_MODEL, D_MODEL)
    o, lse_rows = _attn_fwd(q_att, k_att, vt_att, tq, heads_fwd)
    ypool = _pool_fwd(zup, zgp, pool_w, row(pool_scale))
    loss8, dh, dgm, doop, dga, dcap_rows, dyp, *slabs, g_nf = _tail(
        x2, tgt, o, zga, ypool, zgm, w_ba, w_bp, w_out, row(norm_final), tm_small)
    dup, dgp, g_pool_w, g_pool_scale = _pool_bwd(zup, zgp, dyp, pool_w, row(pool_scale))

    bf = lambda a: a.astype(BF16)
    early = [g_pool_w, g_pool_scale, g_nf, loss8[0]]
    packed = jnp.concatenate([_pack_rows(a) for a in early], axis=0)
    dq_att, dkv_nat, dkr4, s_wout, s_wba, s_wbp, tot_early = _attn_bwd(
        q_att, k_att, v_att, doop, lse_rows, dcap_rows, tq, heads_bwd, slabs, packed)
    s_pool_w, s_pool_scale, s_nf, s_loss = _unpack_rows(tot_early, early)

    dfr, g_wuq_pad, g_wukv, g_qn, g_kvn = _mla_bwd(
        dq_att, dkv_nat, dkr4, zfr, row(q_norm), wuq_pad, row(kv_norm), wukv, rc, rsa, rsb, tm)
    dzs = (dgm, dga, dup, dgp, dfr)
    slabs = [_in_proj_bwd_w(dzs, hn, tm),
             bf(g_wuq_pad.reshape(N_DEV, Q_RANK // N_DEV, HEADS, HEAD_PAD)[..., :NOPE + ROPE]),
             bf(g_wukv).reshape(N_DEV, KV_RANK // N_DEV, HW)]
    grad_x, g_nin, s_win, s_wuq, s_wukv = _in_proj_bwd_x(dzs, x2, dh, row(norm_in), w_in_pad, tm_small, slabs)
    late = [g_nin, g_qn, g_kvn]
    (tot_late,) = _reduce_scatter([], jnp.concatenate([_pack_rows(a) for a in late], axis=0))
    s_nin, s_qn, s_kvn = _unpack_rows(tot_late, late)

    grads = dict(norm_in=s_nin, w_in=s_win, q_norm=s_qn, w_uq=s_wuq, kv_norm=s_kvn, w_ukv=s_wukv, pool_w=s_pool_w.reshape(-1, GROUP),
                 pool_scale=s_pool_scale, w_branch_attn=s_wba, w_branch_pool=s_wbp, w_out=s_wout, norm_final=s_nf)
    return s_loss[0], grad_x, grads


MESH_ID = pl.DeviceIdType.MESH
VMEM_SPEC = pl.BlockSpec(memory_space=pltpu.VMEM)
HBM_SPEC = pl.BlockSpec(memory_space=pl.ANY)


def _mesh_pos():
    return lax.axis_index("x"), lax.axis_index("y"), lax.axis_index("c")


def _slot(px, py, pc):
    return 4 * px + 2 * py + pc


def _all_gather_bf16(shards):
    n = len(shards)

    def body(*refs):
        ins, outs = refs[:n], refs[n:2 * n]
        land0, scratch = refs[2 * n], refs[2 * n + 1:]
        wpad_ref = outs[0]
        ag = _AllGather(ins, (land0,) + tuple(outs[1:]), scratch)
        ag.start()
        ag.forward()
        ag.finish()
        wpad_ref[ZKR:GA, :] = jnp.zeros((GA - ZKR, D_MODEL), BF16)
        for k in range(N_DEV):
            for at, dst, rows in _slab_segments(k):
                wpad_ref[dst:dst + rows, :] = land0[k, at:at + rows, :]

    return pl.pallas_call(
        body, name="all_gather_weights",
        in_specs=[VMEM_SPEC] * n, out_specs=[VMEM_SPEC] + [HBM_SPEC] * (n - 1),
        out_shape=[SDS((ZTOT, D_MODEL), BF16)] + [SDS((N_DEV,) + s.shape, BF16) for s in shards[1:]],
        scratch_shapes=[pltpu.VMEM((N_DEV,) + shards[0].shape, BF16)] + _ag_scratch([s.shape for s in shards]),
        compiler_params=_cp(),
    )(*shards)


def _ag_scratch(shapes):
    n = len(shapes)
    dma = pltpu.SemaphoreType.DMA
    return [pltpu.VMEM(tuple(s), BF16) for s in shapes] + [dma((_AllGather.COPIES * n,)), dma((_AllGather.COPIES * n,)), dma((n,))]


class _AllGather:
    COPIES = 8

    def __init__(self, in_refs, dest_refs, scratch):
        n = self.n = len(in_refs)
        self.ins, self.dests, self.stage = in_refs, dest_refs, scratch[:n]
        self.send_sems, self.recv_sems, self.local_sems = scratch[n:]
        x, y, c = _mesh_pos()
        self.c, self.me, self.sibling = c, (x, y, c), (x, y, 1 - c)
        self.xn, self.yn, self.diag = (1 - x, y), (x, 1 - y), (1 - x, 1 - y)

    def _halves(self, a):
        rows = self.stage[a].shape[0]
        cut = (rows // 2 + 15) // 16 * 16
        return pl.ds(0, cut), pl.ds(cut, rows - cut)

    def _copy(self, a, k, block, to, from_stage=False, rows=None):
        dst = self.dests[a].at[_slot(*block)]
        src = self.stage[a] if from_stage else dst
        if rows is not None:
            src, dst = src.at[rows], dst.at[rows]
        return pltpu.make_async_remote_copy(
            src_ref=src, dst_ref=dst, send_sem=self.send_sems.at[self.COPIES * a + k],
            recv_sem=self.recv_sems.at[self.COPIES * a + k], device_id=to, device_id_type=MESH_ID)

    def _mine(self):
        return [pltpu.make_async_copy(self.stage[a], self.dests[a].at[_slot(*self.me)], self.local_sems.at[a]) for a in range(self.n)]

    def _first(self, a):
        return [self._copy(a, 0, self.me, self.sibling, True), self._copy(a, 1, self.me, (*self.xn, self.c), True),
                self._copy(a, 2, self.me, (*self.yn, self.c), True)]

    def _relays(self, a):
        lo, hi = self._halves(a)
        return [self._copy(a, 3, (*self.xn, self.c), (*self.yn, self.c), rows=lo),
                self._copy(a, 4, (*self.yn, self.c), (*self.xn, self.c), rows=hi)]

    def _passes(self, a):
        return [self._copy(a, 5 + j, (*chip, self.c), self.sibling) for j, chip in enumerate((self.xn, self.yn, self.diag))]

    def start(self):
        for a in range(self.n):
            self.stage[a][...] = self.ins[a][...].astype(BF16)
        for cp in self._mine():
            cp.start()
        for a in range(self.n):
            for cp in self._first(a):
                cp.start()

    def forward(self):
        for a in range(self.n):
            relays, passes = self._relays(a), self._passes(a)
            for j, chip in enumerate((self.xn, self.yn)):
                self._copy(a, 1 + j, (*chip, self.c), self.me).wait_recv()
                relays[j].start()
                passes[j].start()

    def finish(self):
        for a in range(self.n):
            lo, hi = self._halves(a)
            self._copy(a, 3, (*self.diag, self.c), self.me, rows=lo).wait_recv()
            self._copy(a, 4, (*self.diag, self.c), self.me, rows=hi).wait_recv()
            self._passes(a)[2].start()
        for a in range(self.n):
            self._copy(a, 0, self.sibling, self.me).wait_recv()
            for j, chip in enumerate((self.xn, self.yn, self.diag)):
                self._copy(a, 5 + j, (*chip, 1 - self.c), self.me).wait_recv()
            for cp in self._first(a) + self._relays(a) + self._passes(a):
                cp.wait_send()
        for cp in self._mine():
            cp.wait()


N_CHIPS = 4


def _reduce_scatter(slabs, packed):
    def body(*refs):
        n = len(slabs)
        rs = _ReduceScatter(refs[:n], refs[n], refs[n + 1:2 * n + 1], refs[2 * n + 1], refs[2 * n + 2:])
        rs.start1()
        rs.finish1_start2()
        rs.relay2()
        rs.finish2()

    shapes = [s.shape[1:] for s in slabs]
    return pl.pallas_call(
        body, name="reduce_scatter_grads",
        in_specs=[HBM_SPEC] * len(slabs) + [VMEM_SPEC], out_specs=[VMEM_SPEC] * (len(slabs) + 1),
        out_shape=[SDS(s, F32) for s in shapes] + [SDS(packed.shape, F32)],
        scratch_shapes=_rs_scratch(shapes, packed.shape), compiler_params=_cp(),
    )(*slabs, packed)


def _rs_scratch(shapes, packed_shape):
    n = len(shapes)
    n1, n2 = N_CHIPS * n + 1, _ReduceScatter.L2_COPIES * n + N_CHIPS - 1
    dma = pltpu.SemaphoreType.DMA
    packed = [] if packed_shape is None else [pltpu.VMEM(packed_shape, F32), pltpu.VMEM((N_CHIPS,) + tuple(packed_shape), F32)]
    return ([pltpu.VMEM((N_CHIPS,) + tuple(s), BF16) for s in shapes] * 2 + [pltpu.VMEM((N_CHIPS - 1,) + tuple(s), BF16) for s in shapes] * 2
            + packed + [dma((max(N_CHIPS * n, 1),)), dma((n1,)), dma((n1,)), dma((n2,)), dma((n2,))])


class _ReduceScatter:
    L2_COPIES = 6

    def __init__(self, slab_refs, packed_ref, out_refs, ptot_ref, scratch):
        n = self.n = len(slab_refs)
        self.slabs, self.packed, self.outs, self.ptot = slab_refs, packed_ref, out_refs, ptot_ref
        self.own1, self.land1, self.send2, self.land2 = (scratch[k * n:(k + 1) * n] for k in range(4))
        rest = scratch[4 * n:]
        if packed_ref is not None:
            self.pland1, self.pland2 = rest[:2]
            rest = rest[2:]
        self.loc_sems, self.send1_sems, self.recv1_sems, self.send2_sems, self.recv2_sems = rest
        self.x, self.y, self.c = _mesh_pos()

    def _chip(self, r):
        return (1 - self.x if r & 2 else self.x, 1 - self.y if r & 1 else self.y)

    @staticmethod
    def _remote(src, dst, send_sem, recv_sem, to):
        return pltpu.make_async_remote_copy(src_ref=src, dst_ref=dst, send_sem=send_sem, recv_sem=recv_sem, device_id=to,
                                            device_id_type=MESH_ID)

    def _copies1(self):
        c, sibling = self.c, (self.x, self.y, 1 - self.c)
        cps = []
        for a in range(self.n):
            for r in range(N_CHIPS):
                k = N_CHIPS * a + r
                cps.append(pltpu.make_async_copy(self.slabs[a].at[_slot(*self._chip(r), c)], self.own1[a].at[r], self.loc_sems.at[k]))
                cps.append(self._remote(self.slabs[a].at[_slot(*self._chip(r), 1 - c)], self.land1[a].at[r],
                                        self.send1_sems.at[k], self.recv1_sems.at[k], sibling))
        if self.packed is not None:
            k = N_CHIPS * self.n
            cps.append(self._remote(self.packed, self.pland1, self.send1_sems.at[k], self.recv1_sems.at[k], sibling))
        return cps

    def _halves(self, a):
        rows = self.send2[a].shape[1]
        cut = (rows // 2 + 15) // 16 * 16
        return pl.ds(0, cut), pl.ds(cut, rows - cut)

    def _copies2(self, a):
        lo, hi = self._halves(a)
        s2, l2 = self.send2[a], self.land2[a]
        xn, yn = (*self._chip(2), self.c), (*self._chip(1), self.c)
        plan = [(1, lo, xn), (2, lo, xn), (0, hi, yn), (2, hi, yn), (0, lo, yn), (1, hi, xn)]
        return [self._remote(s2.at[slot].at[rows], l2.at[slot].at[rows], self.send2_sems.at[self.L2_COPIES * a + k],
                             self.recv2_sems.at[self.L2_COPIES * a + k], to) for k, (slot, rows, to) in enumerate(plan)]

    def _copies2_packed(self):
        base = self.L2_COPIES * self.n - 1
        return [self._remote(self.pland2.at[0], self.pland2.at[r], self.send2_sems.at[base + r], self.recv2_sems.at[base + r],
                             (*self._chip(r), self.c)) for r in range(1, N_CHIPS)]

    def start1(self):
        for cp in self._copies1():
            cp.start()

    def finish1_start2(self):
        for cp in self._copies1():
            cp.wait()
        for a in range(self.n):
            self.outs[a][...] = self.own1[a][0].astype(F32) + self.land1[a][0].astype(F32)
            for r in range(1, N_CHIPS):
                self.send2[a][r - 1] = (self.own1[a][r].astype(F32) + self.land1[a][r].astype(F32)).astype(BF16)
            for cp in self._copies2(a)[:4]:
                cp.start()
        if self.packed is not None:
            self.pland2[0] = self.packed[...] + self.pland1[...]
            for cp in self._copies2_packed():
                cp.start()

    def relay2(self):
        for a in range(self.n):
            lo, hi = self._halves(a)
            cps, s2, l2 = self._copies2(a), self.send2[a], self.land2[a]
            cps[1].wait_recv()
            s2[0, lo] = (s2[0, lo].astype(F32) + l2[2, lo].astype(F32)).astype(BF16)
            cps[4].start()
            cps[3].wait_recv()
            s2[1, hi] = (s2[1, hi].astype(F32) + l2[2, hi].astype(F32)).astype(BF16)
            cps[5].start()

    def finish2(self):
        for a in range(self.n):
            cps = self._copies2(a)
            for k in (0, 2, 4, 5):
                cps[k].wait_recv()
            for cp in cps:
                cp.wait_send()
            l2 = self.land2[a]
            self.outs[a][...] = self.outs[a][...] + (l2[0].astype(F32) + l2[1].astype(F32))
        if self.packed is not None:
            for cp in self._copies2_packed():
                cp.wait()
            p2 = self.pland2
            self.ptot[...] = (p2[0] + p2[1]) + (p2[2] + p2[3])


def _adamw(ws, gs, ms, vs):
    n = len(ws)

    def body(*refs):
        for k in range(n):
            w, g, m, v = (refs[j * n + k][...] for j in range(4))
            d_ref, nm_ref, nv_ref = (refs[(4 + j) * n + k] for j in range(3))
            m = ADAM_B1 * m + (1.0 - ADAM_B1) * g
            v = ADAM_B2 * v + (1.0 - ADAM_B2) * jnp.square(g)
            m_hat = m / (1.0 - ADAM_B1 ** ADAM_STEP)
            v_hat = v / (1.0 - ADAM_B2 ** ADAM_STEP)
            d_ref[...] = -ADAM_LR * (m_hat / (jnp.sqrt(v_hat) + ADAM_EPS) + ADAM_WD * w)
            nm_ref[...] = m
            nv_ref[...] = v

    outs = pl.pallas_call(
        body, name="adamw", in_specs=[VMEM_SPEC] * (4 * n), out_specs=[VMEM_SPEC] * (3 * n),
        out_shape=[SDS(w.shape, F32) for w in ws] * 3, compiler_params=_cp(),
    )(*ws, *gs, *ms, *vs)
    return outs[:n], outs[n:2 * n], outs[2 * n:]


WEIGHTS = ("norm_in", "w_in", "q_norm", "w_uq", "kv_norm", "w_ukv", "pool_w", "pool_scale", "w_branch_attn", "w_branch_pool",
           "w_out", "norm_final")
SUBLANES = 8


def _pack_rows(a):
    a = a.reshape(-1, LANES)
    return jnp.pad(a, ((0, -a.shape[0] % SUBLANES), (0, 0)))


def _unpack_rows(packed, like):
    out, row = [], 0
    for a in like:
        rows = a.size // LANES
        out.append(packed[row:row + rows].reshape(a.shape))
        row += rows + (-rows % SUBLANES)
    return out


def kernel(x, norm_in, w_in, q_norm, w_uq, kv_norm, w_ukv, pool_w, pool_scale, w_branch_attn, w_branch_pool, w_out, norm_final, loss_target, m_norm_in, m_w_in, m_q_norm, m_w_uq, m_kv_norm, m_w_ukv, m_pool_w, m_pool_scale, m_w_branch_attn, m_w_branch_pool, m_w_out, m_norm_final, v_norm_in, v_w_in, v_q_norm, v_w_uq, v_kv_norm, v_w_ukv, v_pool_w, v_pool_scale, v_w_branch_attn, v_w_branch_pool, v_w_out, v_norm_final):
    w = dict(norm_in=norm_in, w_in=w_in, q_norm=q_norm, w_uq=w_uq, kv_norm=kv_norm, w_ukv=w_ukv, pool_w=pool_w, pool_scale=pool_scale,
             w_branch_attn=w_branch_attn, w_branch_pool=w_branch_pool, w_out=w_out, norm_final=norm_final)
    m = dict(norm_in=m_norm_in, w_in=m_w_in, q_norm=m_q_norm, w_uq=m_w_uq, kv_norm=m_kv_norm, w_ukv=m_w_ukv, pool_w=m_pool_w,
             pool_scale=m_pool_scale, w_branch_attn=m_w_branch_attn, w_branch_pool=m_w_branch_pool, w_out=m_w_out, norm_final=m_norm_final)
    v = dict(norm_in=v_norm_in, w_in=v_w_in, q_norm=v_q_norm, w_uq=v_w_uq, kv_norm=v_kv_norm, w_ukv=v_w_ukv, pool_w=v_pool_w,
             pool_scale=v_pool_scale, w_branch_attn=v_w_branch_attn, w_branch_pool=v_w_branch_pool, w_out=v_w_out, norm_final=v_norm_final)

    def as2d(name, a):
        if name == "w_in":
            return a.T
        if name == "w_uq":
            return a
        if name == "w_ukv":
            return a.reshape(a.shape[0], -1)
        if name == "pool_w":
            return a.reshape(-1, GROUP)
        return a.reshape(1, -1) if a.ndim == 1 else a

    def unshape(name, a):
        return a.T if name == "w_in" else a.reshape(w[name].shape)

    w_in_pad, w_uq_full, w_ukv_full = _all_gather_bf16([as2d(k, w[k]) for k in ("w_in", "w_uq", "w_ukv")])
    loss, grad_x, g2d = _local_step(
        x.reshape(x.shape[1:]), loss_target.reshape(x.shape[1:]), norm_in, w_in_pad, q_norm,
        w_uq_full.reshape(Q_RANK, HEADS, NOPE + ROPE), kv_norm, w_ukv_full.reshape(KV_RANK, HEADS, NOPE + VDIM),
        pool_w, pool_scale, [w_branch_attn, w_branch_pool, w_out], norm_final)

    deltas, new_m, new_v = _adamw([as2d(k, w[k]) for k in WEIGHTS], [g2d[k] for k in WEIGHTS],
                                  [as2d(k, m[k]) for k in WEIGHTS], [as2d(k, v[k]) for k in WEIGHTS])
    shaped = lambda arrs: [unshape(k, a) for k, a in zip(WEIGHTS, arrs)]
    return (loss, grad_x.reshape(x.shape), *shaped([g2d[k] for k in WEIGHTS]), *shaped(deltas), *shaped(new_m), *shaped(new_v))
```

```python
import functools

import jax
import jax.numpy as jnp
import numpy as np
from jax import lax
from jax.experimental import pallas as pl
from jax.experimental.pallas import tpu as pltpu

F32 = jnp.float32
BF16 = jnp.bfloat16
SDS = jax.ShapeDtypeStruct

D_MODEL = 1024
HEADS = 8
NOPE = 64
ROPE = 32
VDIM = 64
Q_RANK = 384
KV_RANK = 256
MLA_W = HEADS * VDIM
POOL_W = 512
POOL_GROUPS = 4
GROUP = POOL_W // POOL_GROUPS
CHUNK = 64
ROPE_THETA = 10000.0
EPS = 1e-6
SCALE = (NOPE + ROPE) ** -0.5
LOG2E = 1.4426950408889634
LN2 = 0.6931471805599453
QK_SCALE_LOG2 = SCALE * LOG2E
IN_TOTAL = 4256
ADAM_LR, ADAM_B1, ADAM_B2, ADAM_EPS, ADAM_WD, ADAM_STEP = 0.001, 0.9, 0.999, 1e-08, 0.01, 10

N_DEV = 8
LANES = 128
HEAD_PAD = LANES
HW = HEADS * HEAD_PAD

ZQ, ZKV, ZKR, GA, UP, GP, GM, ZTOT = 0, 384, 640, 768, 1280, 1792, 2304, 4352
FRONT_W = GA
ZKR_ORIG = 640

VMEM_LIMIT = 62 * 1024 * 1024


def _cp(sem=None, **kw):
    if sem is not None:
        kw["dimension_semantics"] = sem
    return pltpu.CompilerParams(vmem_limit_bytes=VMEM_LIMIT, **kw)


def _mm(a, b):
    return lax.dot_general(a, b, (((1,), (0,)), ((), ())), preferred_element_type=F32)


def _mm_nt(a, b):
    return lax.dot_general(a, b, (((1,), (1,)), ((), ())), preferred_element_type=F32)


def _mm_tn(a, b):
    return lax.dot_general(a, b, (((0,), (0,)), ((), ())), preferred_element_type=F32)


def _row_spec(tm, w):
    return pl.BlockSpec((tm, w), lambda i: (i, 0))


def _full_spec(a):
    nd = len(a.shape)
    return pl.BlockSpec(a.shape, lambda *_: (0,) * nd)


def _rope(v, c, sa, sb, sign):
    n = v.shape[-1]
    reps = n // LANES
    if reps > 1:
        c, sa, sb = (jnp.tile(t, (1, reps)) for t in (c, sa, sb))
    up = pltpu.roll(v, n - ROPE // 2, 1)
    dn = pltpu.roll(v, ROPE // 2, 1)
    return v * c + sign * (up * sa + dn * sb)


def _rope_tables(T):
    half = ROPE // 2
    inv_freq = np.float32(ROPE_THETA) ** (-np.arange(half, dtype=np.float32) / np.float32(half))
    ang = np.arange(T, dtype=np.float32)[:, None] * inv_freq[None, :].astype(np.float32)
    cos, sin = np.cos(ang.astype(np.float64)).astype(np.float32), np.sin(ang.astype(np.float64)).astype(np.float32)
    z16 = np.zeros((T, half), np.float32)
    z32 = np.zeros((T, LANES - NOPE - ROPE), np.float32)
    c = np.concatenate([np.ones((T, NOPE), np.float32), cos, cos, z32], axis=1)
    sa = np.concatenate([np.zeros((T, NOPE), np.float32), -sin, z16, z32], axis=1)
    sb = np.concatenate([np.zeros((T, NOPE), np.float32), z16, sin, z32], axis=1)
    return jnp.asarray(c), jnp.asarray(sa), jnp.asarray(sb)


def _silu_parts(g):
    sg = jax.nn.sigmoid(g)
    return g * sg, sg + g * sg * (1.0 - sg)


def _in_proj(x2, norm_in, w_in_pad, q_norm, wuq_pad, kv_norm, wukv, rc, rsa, rsb, tm, late_shards):
    T = x2.shape[0]
    steps = T // tm
    n = len(late_shards)

    def body(x_ref, nin_ref, win_ref, qn_ref, wuq_ref, kvn_ref, wukv_ref, c_ref, sa_ref, sb_ref, *rest):
        hn_ref, zgm_ref, zga_ref, zup_ref, zgp_ref, zfr_ref, q_ref, k_ref, v_ref, vt_ref = rest[n:n + 10]
        ag = _AllGather(rest[:n], rest[n + 10:2 * n + 10], rest[2 * n + 10:])
        step = pl.program_id(0)
        pl.when(step == 0)(ag.start)
        pl.when(step == min(3, steps - 1))(ag.forward)
        xf = x_ref[...]
        r = lax.rsqrt(jnp.mean(xf * xf, axis=-1, keepdims=True) + EPS)
        hn = (xf * r * nin_ref[...]).astype(BF16)
        hn_ref[...] = hn
        zfr = _mm_nt(hn, win_ref[ZQ:GA, :])
        zfr_ref[...] = zfr
        zq, zkv, zkr = zfr[:, ZQ:ZKV], zfr[:, ZKV:ZKR], zfr[:, ZKR:GA]
        c, sa, sb = c_ref[...], sa_ref[...], sb_ref[...]
        rq = lax.rsqrt(jnp.mean(zq * zq, axis=-1, keepdims=True) + EPS)
        cq = (zq * rq * qn_ref[...]).astype(BF16)
        rkv = lax.rsqrt(jnp.mean(zkv * zkv, axis=-1, keepdims=True) + EPS)
        ckv = (zkv * rkv * kvn_ref[...]).astype(BF16)
        zga_ref[...] = _mm_nt(hn, win_ref[GA:UP, :])
        zup_ref[...] = _mm_nt(hn, win_ref[UP:GP, :])
        zgp_ref[...] = _mm_nt(hn, win_ref[GP:GM, :])
        q_raw = _mm(cq, wuq_ref[...])
        kv = _mm(ckv, wukv_ref[...])
        zgm_ref[...] = _mm_nt(hn, win_ref[GM:ZTOT, :])
        q = _rope(q_raw, c, sa, sb, 1.0)
        q_ref[...] = (q * QK_SCALE_LOG2).astype(BF16)
        kr = _rope(zkr, c, sa, sb, 1.0)
        lane = lax.broadcasted_iota(jnp.int32, kv.shape, 1) % LANES
        k_ref[...] = jnp.where(lane < NOPE, kv, jnp.tile(kr, (1, HEADS))).astype(BF16)
        v = jnp.where(lane < NOPE, 1.0, kv).astype(BF16)
        v_ref[...] = v
        vt_ref[...] = v.T
        pl.when(step == steps - 1)(ag.finish)

    ins = (x2, norm_in, w_in_pad, q_norm, wuq_pad, kv_norm, wukv, rc, rsa, rsb)
    in_specs = [_row_spec(tm, D_MODEL), _full_spec(norm_in), _full_spec(w_in_pad), _full_spec(q_norm), _full_spec(wuq_pad),
                _full_spec(kv_norm), _full_spec(wukv), _row_spec(tm, LANES), _row_spec(tm, LANES), _row_spec(tm, LANES)]
    widths = [(D_MODEL, BF16), (ZTOT - GM, F32), (UP - GA, F32), (GP - UP, F32), (GM - GP, F32), (FRONT_W, F32),
              (HW, BF16), (HW, BF16), (HW, BF16)]
    return pl.pallas_call(
        body, name="in_proj", grid=(steps,), in_specs=in_specs + [_full_spec(s) for s in late_shards],
        out_specs=[_row_spec(tm, w) for w, _ in widths] + [pl.BlockSpec((HW, tm), lambda i: (0, i))] + [HBM_SPEC] * n,
        out_shape=[SDS((T, w), dt) for w, dt in widths] + [SDS((HW, T), BF16)]
        + [SDS((N_DEV,) + s.shape, BF16) for s in late_shards],
        scratch_shapes=_ag_scratch([s.shape for s in late_shards]), compiler_params=_cp(("arbitrary",)),
    )(*ins, *late_shards)


def _chunk_mask(n_q, n_k, q_off, transposed):
    shape = (n_k, n_q) if transposed else (n_q, n_k)
    q = (lax.broadcasted_iota(jnp.int32, shape, 1 if transposed else 0) + q_off) // CHUNK
    k = lax.broadcasted_iota(jnp.int32, shape, 0 if transposed else 1) // CHUNK
    return k <= q


def _store_pair_rows(ref, k, pair):
    t = pair.T
    ref[k, 0:1, :] = t[0:1, :]
    ref[k, 1:2, :] = t[VDIM:VDIM + 1, :]


def _attn_fwd(q_att, k_att, vt_att, tq, hps):
    T = q_att.shape[0]
    head_lanes = [slice(h * LANES, (h + 1) * LANES) for h in range(hps)]

    def body(q_ref, k_ref, vt_ref, o_ref, lser_ref):
        i = pl.program_id(1)
        mask = _chunk_mask(tq, tq, 0, True)
        lane = lax.broadcasted_iota(jnp.int32, (tq, LANES), 1)
        qs = [q_ref[:, hs] for hs in head_lanes]

        def step(j, carry, masked):
            off = pl.multiple_of(j * tq, tq)
            sts = [_mm_nt(k_ref[pl.ds(off, tq), hs], qh) for qh, hs in zip(qs, head_lanes)]
            if masked:
                sts = [jnp.where(mask, st, -jnp.inf) for st in sts]
            ms = [jnp.maximum(m, jnp.max(st, axis=0, keepdims=True)) for (m, _), st in zip(carry, sts)]
            pts = [jnp.exp2(st - m_new).astype(BF16) for st, m_new in zip(sts, ms)]
            return tuple((m_new, jnp.exp2(m - m_new) * acc + _mm(vt_ref[hs, pl.ds(off, tq)], pt))
                         for (m, acc), m_new, pt, hs in zip(carry, ms, pts, head_lanes))

        init = ((jnp.full((1, tq), -jnp.inf, F32), jnp.zeros((LANES, tq), F32)),) * hps
        res = step(i, lax.fori_loop(0, i, functools.partial(step, masked=False), init), True)
        for pair in range(hps // 2):
            (ma, acca), (mb, accb) = res[2 * pair], res[2 * pair + 1]
            la, lb = acca[:1], accb[:1]
            oa, ob = (acca / la).T, (accb / lb).T
            o_ref[:, pair * LANES:(pair + 1) * LANES] = jnp.where(lane < VDIM, pltpu.roll(oa, VDIM, 1), ob)
            lser_ref[pair, 0:1, :] = ma + jnp.log2(la)
            lser_ref[pair, 1:2, :] = mb + jnp.log2(lb)

    qspec = pl.BlockSpec((tq, hps * LANES), lambda p, i: (i, p))
    kspec = pl.BlockSpec((T, hps * LANES), lambda p, i: (0, p))
    vspec = pl.BlockSpec((hps * LANES, T), lambda p, i: (p, 0))
    ospec = pl.BlockSpec((tq, hps * VDIM), lambda p, i: (i, p))
    return pl.pallas_call(
        body, name="attn_fwd", grid=(HEADS // hps, T // tq), in_specs=[qspec, kspec, vspec],
        out_specs=[ospec, pl.BlockSpec((hps // 2, 2, tq), lambda p, i: (p, 0, i))],
        out_shape=[SDS((T, MLA_W), F32), SDS((HEADS // 2, 2, T), F32)],
        compiler_params=_cp(("parallel", "parallel")),
    )(q_att, k_att, vt_att)


def _pick(g, vals):
    out = vals[-1]
    for k in range(len(vals) - 2, -1, -1):
        out = jnp.where(g == k, vals[k], out)
    return out


def _window_sum(u, g, forward):
    T = u.shape[0]
    row = lax.broadcasted_iota(jnp.int32, u.shape, 0)

    def sh(s, k):
        if forward:
            return jnp.where(row >= k, pltpu.roll(s, k, 0), 0.0)
        return jnp.where(row < T - k, pltpu.roll(s, T - k, 0), 0.0)

    sums, s = [], u
    for k in (1, 2, 4, 8):
        s = s + sh(s, k)
        sums.append(s)
    return _pick(g, sums)


def _pool_count(shape, g):
    row = lax.broadcasted_iota(jnp.int32, shape, 0)
    return jnp.minimum(row + 1, lax.shift_left(jnp.int32(2), g)).astype(F32)


def _pool_fwd(zup, zgp, pool_w, pool_scale):
    T = zup.shape[0]

    def body(u_ref, g_ref, w_ref, sc_ref, y_ref):
        g = pl.program_id(0)
        u = u_ref[...]
        d = _window_sum(u, g, True) / _pool_count(u.shape, g) - u
        lin = _mm(d.astype(BF16), w_ref[0].astype(BF16))
        silu, _ = _silu_parts(g_ref[...])
        y_ref[...] = (lin * sc_ref[...] * silu).astype(BF16)

    col = pl.BlockSpec((T, GROUP), lambda g: (0, g))
    return pl.pallas_call(
        body, name="pool_fwd", grid=(POOL_GROUPS,),
        in_specs=[col, col, pl.BlockSpec((1, GROUP, GROUP), lambda g: (g, 0, 0)), pl.BlockSpec((1, GROUP), lambda g: (0, g))],
        out_specs=col, out_shape=SDS((T, POOL_W), BF16), compiler_params=_cp(("parallel",)),
    )(zup, zgp, pool_w, pool_scale)


def _pool_bwd(zup, zgp, dyp, pool_w, pool_scale):
    T = zup.shape[0]

    def body(u_ref, g_ref, dy_ref, w_ref, sc_ref, du_ref, dg_ref, gw_ref, gsc_ref):
        g = pl.program_id(0)
        u = u_ref[...]
        cnt = _pool_count(u.shape, g)
        d = (_window_sum(u, g, True) / cnt - u).astype(BF16)
        wb = w_ref[0].astype(BF16)
        lin = _mm(d, wb)
        sc = sc_ref[...]
        silu, dsilu = _silu_parts(g_ref[...])
        dy = dy_ref[...]
        dg_ref[...] = (dy * lin * sc * dsilu).astype(BF16)
        dpre = dy * silu
        gsc_ref[...] = jnp.sum(dpre * lin, axis=0, keepdims=True)
        dlin = (dpre * sc).astype(BF16)
        gw_ref[0] = _mm_tn(d, dlin)
        dd = _mm_nt(dlin, wb)
        du_ref[...] = (_window_sum(dd / cnt, g, False) - dd).astype(BF16)

    col = pl.BlockSpec((T, GROUP), lambda g: (0, g))
    wspec = pl.BlockSpec((1, GROUP, GROUP), lambda g: (g, 0, 0))
    vspec = pl.BlockSpec((1, GROUP), lambda g: (0, g))
    return pl.pallas_call(
        body, name="pool_bwd", grid=(POOL_GROUPS,), in_specs=[col, col, col, wspec, vspec], out_specs=[col, col, wspec, vspec],
        out_shape=[SDS((T, POOL_W), BF16), SDS((T, POOL_W), BF16), SDS((POOL_GROUPS, GROUP, GROUP), F32), SDS((1, POOL_W), F32)],
        compiler_params=_cp(("parallel",)),
    )(zup, zgp, dyp, pool_w, pool_scale)


def _tail(x2, tgt, o, zga, ypool, zgm, wba, wbp, wout, norm_final, tm):
    T = x2.shape[0]
    steps = T // tm
    cols = D_MODEL // N_DEV

    def body(x_ref, tgt_ref, o_ref, zga_ref, yp_ref, zgm_ref, wba_ref, wbp_ref, wout_ref, nf_ref,
             loss_ref, dh_ref, dgm_ref, doop_ref, dga_ref, dcapr_ref, dyp_ref, swout_ref, swba_ref, swbp_ref, gnf_ref,
             gwout_ref, gwba_ref, gwbp_ref):
        @pl.when(pl.program_id(0) == 0)
        def _():
            for ref in (loss_ref, gwout_ref, gwba_ref, gwbp_ref, gnf_ref):
                ref[...] = jnp.zeros_like(ref)

        o_v = o_ref[...]
        silu, dsilu = _silu_parts(zga_ref[...])
        ya = (o_v * silu).astype(BF16)
        yp = yp_ref[...]
        wba_v = jnp.concatenate([wba_ref[k] for k in range(N_DEV)], axis=1)
        wbp_v = jnp.concatenate([wbp_ref[k] for k in range(N_DEV)], axis=1)
        wout_v = wout_ref[...]
        a = _mm(ya, wba_v)
        p = _mm(yp, wbp_v)
        gate = jax.nn.sigmoid(zgm_ref[...])
        ga, gp = gate[:, :D_MODEL], gate[:, D_MODEL:]
        mg = (ga * a + gp * p).astype(BF16)
        h = x_ref[...] + _mm(mg, wout_v)
        r = lax.rsqrt(jnp.mean(h * h, axis=-1, keepdims=True) + EPS)
        gf = nf_ref[...]
        hr = h * r
        e = hr * gf - tgt_ref[...]
        loss_ref[...] += (0.5 / D_MODEL) * jnp.sum(e * e)
        dy = e * (1.0 / D_MODEL)
        gnf_ref[...] += jnp.sum(dy * hr, axis=0, keepdims=True)
        u = dy * gf
        dh = r * (u - hr * jnp.mean(u * hr, axis=-1, keepdims=True))
        dh_ref[...] = dh
        dhb = dh.astype(BF16)
        dmg = _mm_nt(dhb, wout_v)
        dab = (dmg * ga).astype(BF16)
        dpb = (dmg * gp).astype(BF16)
        dya = _mm_nt(dab, wba_v)
        dyp_ref[...] = _mm_nt(dpb, wbp_v)
        gwout_ref[...] += _mm_tn(mg, dhb)
        gwba_ref[...] += _mm_tn(ya, dab)
        gwbp_ref[...] += _mm_tn(yp, dpb)
        dgm_ref[:, :D_MODEL] = (dmg * a * ga * (1.0 - ga)).astype(BF16)
        dgm_ref[:, D_MODEL:] = (dmg * p * gp * (1.0 - gp)).astype(BF16)
        do = dya * silu
        dga_ref[...] = (dya * o_v * dsilu).astype(BF16)
        prod = do * o_v
        lo = lax.broadcasted_iota(jnp.int32, (tm, LANES), 1) < VDIM
        for pair in range(HEADS // 2):
            ls = slice(pair * LANES, (pair + 1) * LANES)
            do_p, prod_p = do[:, ls], prod[:, ls]
            dcap_a = jnp.sum(jnp.where(lo, prod_p, 0.0), axis=-1, keepdims=True)
            dcap_b = jnp.sum(jnp.where(lo, 0.0, prod_p), axis=-1, keepdims=True)
            _store_pair_rows(dcapr_ref, pair, jnp.where(lo, dcap_a, dcap_b))
            doop_ref[:, 2 * pair * LANES:(2 * pair + 1) * LANES] = jnp.where(lo, 0.0, pltpu.roll(do_p, VDIM, 1)).astype(BF16)
            doop_ref[:, (2 * pair + 1) * LANES:(2 * pair + 2) * LANES] = jnp.where(lo, 0.0, do_p).astype(BF16)

        @pl.when(pl.program_id(0) == steps - 1)
        def _():
            for k in range(N_DEV):
                swout_ref[k] = gwout_ref[k * cols:(k + 1) * cols, :].astype(BF16)
                swba_ref[k] = gwba_ref[:, k * cols:(k + 1) * cols].astype(BF16)
                swbp_ref[k] = gwbp_ref[:, k * cols:(k + 1) * cols].astype(BF16)

    ins = (x2, tgt, o, zga, ypool, zgm, wba, wbp, wout, norm_final)
    in_specs = [_row_spec(tm, D_MODEL), _row_spec(tm, D_MODEL), _row_spec(tm, MLA_W), _row_spec(tm, MLA_W), _row_spec(tm, POOL_W),
                _row_spec(tm, 2 * D_MODEL), _full_spec(wba), _full_spec(wbp), _full_spec(wout), _full_spec(norm_final)]
    outs = [SDS((8, LANES), F32), SDS((T, D_MODEL), F32), SDS((T, 2 * D_MODEL), BF16), SDS((T, HW), BF16), SDS((T, MLA_W), BF16),
            SDS((HEADS // 2, 2, T), F32), SDS((T, POOL_W), F32),
            SDS((N_DEV, cols, D_MODEL), BF16), SDS((N_DEV, MLA_W, cols), BF16), SDS((N_DEV, POOL_W, cols), BF16), SDS((1, D_MODEL), F32)]
    out_specs = [_full_spec(outs[0]), _row_spec(tm, D_MODEL), _row_spec(tm, 2 * D_MODEL), _row_spec(tm, HW), _row_spec(tm, MLA_W),
                 pl.BlockSpec((HEADS // 2, 2, tm), lambda i: (0, 0, i)), _row_spec(tm, POOL_W),
                 _full_spec(outs[7]), _full_spec(outs[8]), _full_spec(outs[9]), _full_spec(outs[10])]
    return pl.pallas_call(
        body, name="tail", grid=(steps,), in_specs=in_specs, out_specs=out_specs, out_shape=outs,
        scratch_shapes=[pltpu.VMEM((D_MODEL, D_MODEL), F32), pltpu.VMEM((MLA_W, D_MODEL), F32), pltpu.VMEM((POOL_W, D_MODEL), F32)],
        compiler_params=_cp(("arbitrary",)),
    )(*ins)


def _attn_bwd(q_att, k_att, v_att, doop, lse_rows, dcap_rows, tq, hps, slabs, packed):
    T = q_att.shape[0]
    nq = T // tq
    n = len(slabs)
    groups = HEADS // hps
    head_lanes = [slice(h * LANES, (h + 1) * LANES) for h in range(hps)]

    def body(q_ref, k_ref, v_ref, doop_ref, lse_ref, dcap_ref, *rest):
        slab_refs, packed_ref = rest[:n], rest[n]
        dq_ref, dkv_ref, dkr_ref = rest[n + 1:n + 4]
        sum_refs, ptot_ref = rest[n + 4:2 * n + 4], rest[2 * n + 4]
        dq_acc = rest[2 * n + 5]
        rs = _ReduceScatter(slab_refs, packed_ref, sum_refs, ptot_ref, rest[2 * n + 6:])
        group, j = pl.program_id(0), pl.program_id(1)
        tick, last = group * nq + j, groups * nq - 1
        pl.when(tick == 0)(rs.start1)
        pl.when(tick == min(1, last))(rs.finish1_start2)
        pl.when(tick == min(3, last))(rs.relay2)
        mask = _chunk_mask(tq, tq, 0, True)
        lane = lax.broadcasted_iota(jnp.int32, (tq, LANES), 1)
        ks = [k_ref[:, hs] for hs in head_lanes]
        vs = [v_ref[:, hs] for hs in head_lanes]
        kts = [kh.T for kh in ks]

        @pl.when(j == 0)
        def _():
            dq_acc[...] = jnp.zeros_like(dq_acc)

        def step(i, carry, masked):
            rows = pl.ds(pl.multiple_of(i * tq, tq), tq)
            heads = range(hps)
            stat = lambda h: (h // 2, slice(h % 2, h % 2 + 1), rows)
            qhs = [q_ref[rows, hs] for hs in head_lanes]
            doops = [doop_ref[rows, hs] for hs in head_lanes]
            sts = [_mm_nt(ks[h], qhs[h]) for h in heads]
            dpts = [_mm_nt(vs[h], doops[h]) for h in heads]
            pts = [jnp.exp2(sts[h] - lse_ref[stat(h)]) for h in heads]
            if masked:
                pts = [jnp.where(mask, pt, 0.0) for pt in pts]
            dsts = [(pts[h] * (dpts[h] - dcap_ref[stat(h)])).astype(BF16) for h in heads]
            dvs = [_mm(pts[h].astype(BF16), doops[h]) for h in heads]
            dks = [_mm(dsts[h], qhs[h]) for h in heads]
            for h, hs in enumerate(head_lanes):
                dq_acc[hs, rows] += _mm(kts[h], dsts[h])
            return tuple((dk + dks[h], dv + dvs[h]) for h, (dk, dv) in enumerate(carry))

        zero = jnp.zeros((tq, LANES), F32)
        carry = step(j, ((zero, zero),) * hps, True)
        res = lax.fori_loop(j + 1, nq, functools.partial(step, masked=False), carry)
        dkr = None
        for (dk, dv), hs in zip(res, head_lanes):
            dk = dk * LN2
            dkv_ref[:, hs] = jnp.where(lane < NOPE, dk, dv).astype(BF16)
            dkr = dk if dkr is None else dkr + dk
        dkr_ref[0] = jnp.where((lane >= NOPE) & (lane < NOPE + ROPE), dkr, 0.0)

        @pl.when(j == nq - 1)
        def _():
            dq_ref[...] = (dq_acc[...] * SCALE).T.astype(BF16)

        pl.when(tick == last)(rs.finish2)

    kspec = pl.BlockSpec((tq, hps * LANES), lambda p, j: (j, p))
    qspec = pl.BlockSpec((T, hps * LANES), lambda p, j: (0, p))
    rspec = pl.BlockSpec((hps // 2, 2, T), lambda p, j: (p, 0, 0))
    sums = [SDS(s.shape[1:], F32) for s in slabs] + [SDS(packed.shape, F32)]
    return pl.pallas_call(
        body, name="attn_bwd", grid=(groups, nq),
        in_specs=[qspec, kspec, kspec, qspec, rspec, rspec] + [HBM_SPEC] * n + [_full_spec(packed)],
        out_specs=[qspec, kspec, pl.BlockSpec((1, tq, LANES), lambda p, j: (p, j, 0))] + [_full_spec(s) for s in sums],
        out_shape=[SDS((T, HW), BF16), SDS((T, HW), BF16), SDS((groups, T, LANES), F32)] + sums,
        scratch_shapes=[pltpu.VMEM((hps * LANES, T), F32)] + _rs_scratch([s.shape for s in sums[:-1]], packed.shape),
        compiler_params=_cp(("arbitrary", "arbitrary")),
    )(q_att, k_att, v_att, doop, lse_rows, dcap_rows, *slabs, packed)


def _rms_bwd(z, gain, dout):
    r = lax.rsqrt(jnp.mean(z * z, axis=-1, keepdims=True) + EPS)
    zr = z * r
    u = dout * gain
    return r * (u - zr * jnp.mean(u * zr, axis=-1, keepdims=True)), jnp.sum(dout * zr, axis=0, keepdims=True)


def _mla_bwd(dq_att, dkv_nat, dkr4, zfr, q_norm, wuq_pad, kv_norm, wukv, rc, rsa, rsb, tm):
    T = dq_att.shape[0]

    def body(dq_ref, dkv_ref, dkr_ref, zfr_ref, qn_ref, wuq_ref, kvn_ref, wukv_ref, c_ref, sa_ref, sb_ref,
             dfr_ref, gwuq_ref, gwukv_ref, gqn_ref, gkvn_ref):
        @pl.when(pl.program_id(0) == 0)
        def _():
            for ref in (gwuq_ref, gwukv_ref, gqn_ref, gkvn_ref):
                ref[...] = jnp.zeros_like(ref)

        c, sa, sb = c_ref[...], sa_ref[...], sb_ref[...]
        zq, zkv = zfr_ref[:, :Q_RANK], zfr_ref[:, Q_RANK:Q_RANK + KV_RANK]
        qn, kvn = qn_ref[...], kvn_ref[...]
        dkv = dkv_ref[...]
        dckv = _mm_nt(dkv, wukv_ref[...])
        ckv = (zkv * lax.rsqrt(jnp.mean(zkv * zkv, axis=-1, keepdims=True) + EPS) * kvn).astype(BF16)
        gwukv_ref[...] += _mm_tn(ckv, dkv)
        cq = (zq * lax.rsqrt(jnp.mean(zq * zq, axis=-1, keepdims=True) + EPS) * qn).astype(BF16)
        dq = _rope(dq_ref[...].astype(F32), c, sa, sb, -1.0).astype(BF16)
        dzkv, gkvn = _rms_bwd(zkv, kvn, dckv)
        gkvn_ref[...] += gkvn
        gwuq_ref[...] += _mm_tn(cq, dq)
        dzq, gqn = _rms_bwd(zq, qn, _mm_nt(dq, wuq_ref[...]))
        gqn_ref[...] += gqn
        dkr = functools.reduce(lambda a, b: a + b, [dkr_ref[g] for g in range(dkr4.shape[0])])
        dfr_ref[:, :Q_RANK] = dzq.astype(BF16)
        dfr_ref[:, Q_RANK:Q_RANK + KV_RANK] = dzkv.astype(BF16)
        dfr_ref[:, Q_RANK + KV_RANK:] = _rope(dkr, c, sa, sb, -1.0).astype(BF16)

    ins = (dq_att, dkv_nat, dkr4, zfr, q_norm, wuq_pad, kv_norm, wukv, rc, rsa, rsb)
    in_specs = [_row_spec(tm, HW), _row_spec(tm, HW), pl.BlockSpec((dkr4.shape[0], tm, LANES), lambda i: (0, i, 0)), _row_spec(tm, FRONT_W),
                _full_spec(q_norm), _full_spec(wuq_pad), _full_spec(kv_norm), _full_spec(wukv),
                _row_spec(tm, LANES), _row_spec(tm, LANES), _row_spec(tm, LANES)]
    outs = [SDS((T, FRONT_W), BF16), SDS((Q_RANK, HW), F32), SDS((KV_RANK, HW), F32), SDS((1, Q_RANK), F32), SDS((1, KV_RANK), F32)]
    out_specs = [_row_spec(tm, FRONT_W)] + [_full_spec(s) for s in outs[1:]]
    return pl.pallas_call(
        body, name="mla_bwd", grid=(T // tm,), in_specs=in_specs, out_specs=out_specs, out_shape=outs,
        compiler_params=_cp(("arbitrary",)),
    )(*ins)


_DZ_COLS = ((GM, ZTOT), (GA, UP), (UP, GP), (GP, GM), (ZQ, GA))


def _in_proj_bwd_x(dzs, x2, dh, norm_in, w_in_pad, tm, slabs):
    T = x2.shape[0]
    steps = T // tm
    n = len(slabs)

    def body(d0, d1, d2, d3, d4, x_ref, dh_ref, nin_ref, win_ref, *rest):
        slab_refs, (gx_ref, gnin_ref), sum_refs = rest[:n], rest[n:n + 2], rest[n + 2:2 * n + 2]
        rs = _ReduceScatter(slab_refs, None, sum_refs, None, rest[2 * n + 2:])
        step = pl.program_id(0)

        @pl.when(step == 0)
        def _():
            gnin_ref[...] = jnp.zeros_like(gnin_ref)
            rs.start1()

        pl.when(step == min(2, steps - 1))(rs.finish1_start2)
        pl.when(step == min(steps * 11 // 16, steps - 1))(rs.relay2)
        dhn = None
        for ref, (lo, hi) in zip((d0, d1, d2, d3, d4), _DZ_COLS):
            t = _mm(ref[...], win_ref[lo:hi, :])
            dhn = t if dhn is None else dhn + t
        dx, gnin = _rms_bwd(x_ref[...], nin_ref[...], dhn)
        gnin_ref[...] += gnin
        gx_ref[...] = dx + dh_ref[...]
        pl.when(step == steps - 1)(rs.finish2)

    in_specs = [_row_spec(tm, hi - lo) for lo, hi in _DZ_COLS] + [_row_spec(tm, D_MODEL), _row_spec(tm, D_MODEL),
                                                                  _full_spec(norm_in), _full_spec(w_in_pad)] + [HBM_SPEC] * n
    sums = [SDS(s.shape[1:], F32) for s in slabs]
    outs = [SDS((T, D_MODEL), F32), SDS((1, D_MODEL), F32)] + sums
    return pl.pallas_call(
        body, name="in_proj_bwd_x", grid=(steps,), in_specs=in_specs,
        out_specs=[_row_spec(tm, D_MODEL), _full_spec(outs[1])] + [_full_spec(s) for s in sums],
        out_shape=outs, scratch_shapes=_rs_scratch([s.shape for s in sums], None), compiler_params=_cp(("arbitrary",)),
    )(*dzs, x2, dh, norm_in, w_in_pad, *slabs)


SLAB_ROWS = IN_TOTAL // N_DEV


def _slab_segments(k):
    cuts = [(0, ZKR_ORIG, 0), (ZKR_ORIG, ZKR_ORIG + ROPE, NOPE), (ZKR_ORIG + ROPE, IN_TOTAL, LANES - ROPE)]
    lo, hi = k * SLAB_ROWS, (k + 1) * SLAB_ROWS
    return [(max(lo, a) - lo, max(lo, a) + shift, min(hi, b) - max(lo, a)) for a, b, shift in cuts if min(hi, b) > max(lo, a)]


def _in_proj_bwd_w(dzs, hn, tm):
    T = hn.shape[0]
    steps = T // tm

    def body(d0, d1, d2, d3, d4, hn_ref, slab_ref, acc_ref):
        @pl.when(pl.program_id(0) == 0)
        def _():
            acc_ref[...] = jnp.zeros_like(acc_ref)

        hn_v = hn_ref[...]
        for ref, (lo, hi) in zip((d0, d1, d2, d3, d4), _DZ_COLS):
            acc_ref[lo:hi, :] += _mm_tn(ref[...], hn_v)

        @pl.when(pl.program_id(0) == steps - 1)
        def _():
            for k in range(N_DEV):
                for at, src, rows in _slab_segments(k):
                    slab_ref[k, at:at + rows, :] = acc_ref[src:src + rows, :].astype(BF16)

    in_specs = [_row_spec(tm, hi - lo) for lo, hi in _DZ_COLS] + [_row_spec(tm, D_MODEL)]
    out = SDS((N_DEV, SLAB_ROWS, D_MODEL), BF16)
    return pl.pallas_call(
        body, name="in_proj_bwd_w", grid=(steps,), in_specs=in_specs, out_specs=_full_spec(out), out_shape=out,
        scratch_shapes=[pltpu.VMEM((ZTOT, D_MODEL), F32)], compiler_params=_cp(("arbitrary",)),
    )(*dzs, hn)


def _local_step(x2, tgt, norm_in, w_in_pad, q_norm, w_uq, kv_norm, w_ukv, pool_w, pool_scale, late_shards, norm_final):
    T = x2.shape[0]
    tm, tm_small, tq = min(512, T), min(256, T), min(512, T)
    heads_fwd, heads_bwd = 4, 4
    row = lambda v: v.reshape(1, -1)
    wuq_pad = jnp.pad(w_uq, ((0, 0), (0, 0), (0, HEAD_PAD - NOPE - ROPE))).reshape(Q_RANK, HW)
    wukv = w_ukv.reshape(KV_RANK, HW)
    rc, rsa, rsb = _rope_tables(T)

    hn, zgm, zga, zup, zgp, zfr, q_att, k_att, v_att, vt_att, w_ba, w_bp, w_out = _in_proj(
        x2, row(norm_in), w_in_pad, row(q_norm), wuq_pad, row(kv_norm), wukv, rc, rsa, rsb, tm, late_shards)
    w_out = w_out.reshape(D_MODEL, D_MODEL)
    o, lse_rows = _attn_fwd(q_att, k_att, vt_att, tq, heads_fwd)
    ypool = _pool_fwd(zup, zgp, pool_w, row(pool_scale))
    loss8, dh, dgm, doop, dga, dcap_rows, dyp, *slabs, g_nf = _tail(
        x2, tgt, o, zga, ypool, zgm, w_ba, w_bp, w_out, row(norm_final), tm_small)
    dup, dgp, g_pool_w, g_pool_scale = _pool_bwd(zup, zgp, dyp, pool_w, row(pool_scale))

    bf = lambda a: a.astype(BF16)
    early = [g_pool_w, g_pool_scale, g_nf, loss8[0]]
    packed = jnp.concatenate([_pack_rows(a) for a in early], axis=0)
    dq_att, dkv_nat, dkr4, s_wout, s_wba, s_wbp, tot_early = _attn_bwd(
        q_att, k_att, v_att, doop, lse_rows, dcap_rows, tq, heads_bwd, slabs, packed)
    s_pool_w, s_pool_scale, s_nf, s_loss = _unpack_rows(tot_early, early)

    dfr, g_wuq_pad, g_wukv, g_qn, g_kvn = _mla_bwd(
        dq_att, dkv_nat, dkr4, zfr, row(q_norm), wuq_pad, row(kv_norm), wukv, rc, rsa, rsb, tm)
    dzs = (dgm, dga, dup, dgp, dfr)
    slabs = [_in_proj_bwd_w(dzs, hn, tm),
             bf(g_wuq_pad.reshape(N_DEV, Q_RANK // N_DEV, HEADS, HEAD_PAD)[..., :NOPE + ROPE]),
             bf(g_wukv).reshape(N_DEV, KV_RANK // N_DEV, HW)]
    grad_x, g_nin, s_win, s_wuq, s_wukv = _in_proj_bwd_x(dzs, x2, dh, row(norm_in), w_in_pad, tm_small, slabs)
    late = [g_nin, g_qn, g_kvn]
    (tot_late,) = _reduce_scatter([], jnp.concatenate([_pack_rows(a) for a in late], axis=0))
    s_nin, s_qn, s_kvn = _unpack_rows(tot_late, late)

    grads = dict(norm_in=s_nin, w_in=s_win, q_norm=s_qn, w_uq=s_wuq, kv_norm=s_kvn, w_ukv=s_wukv, pool_w=s_pool_w.reshape(-1, GROUP),
                 pool_scale=s_pool_scale, w_branch_attn=s_wba, w_branch_pool=s_wbp, w_out=s_wout, norm_final=s_nf)
    return s_loss[0], grad_x, grads


MESH_ID = pl.DeviceIdType.MESH
VMEM_SPEC = pl.BlockSpec(memory_space=pltpu.VMEM)
HBM_SPEC = pl.BlockSpec(memory_space=pl.ANY)


def _mesh_pos():
    return lax.axis_index("x"), lax.axis_index("y"), lax.axis_index("c")


def _slot(px, py, pc):
    return 4 * px + 2 * py + pc


def _all_gather_bf16(shards):
    n = len(shards)

    def body(*refs):
        ins, outs = refs[:n], refs[n:2 * n]
        land0, scratch = refs[2 * n], refs[2 * n + 1:]
        wpad_ref = outs[0]
        ag = _AllGather(ins, (land0,) + tuple(outs[1:]), scratch)
        ag.start()
        ag.forward()
        ag.finish()
        wpad_ref[ZKR:GA, :] = jnp.zeros((GA - ZKR, D_MODEL), BF16)
        for k in range(N_DEV):
            for at, dst, rows in _slab_segments(k):
                wpad_ref[dst:dst + rows, :] = land0[k, at:at + rows, :]

    return pl.pallas_call(
        body, name="all_gather_weights",
        in_specs=[VMEM_SPEC] * n, out_specs=[VMEM_SPEC] + [HBM_SPEC] * (n - 1),
        out_shape=[SDS((ZTOT, D_MODEL), BF16)] + [SDS((N_DEV,) + s.shape, BF16) for s in shards[1:]],
        scratch_shapes=[pltpu.VMEM((N_DEV,) + shards[0].shape, BF16)] + _ag_scratch([s.shape for s in shards]),
        compiler_params=_cp(),
    )(*shards)


def _ag_scratch(shapes):
    n = len(shapes)
    dma = pltpu.SemaphoreType.DMA
    return [pltpu.VMEM(tuple(s), BF16) for s in shapes] + [dma((_AllGather.COPIES * n,)), dma((_AllGather.COPIES * n,)), dma((n,))]


class _AllGather:
    COPIES = 8

    def __init__(self, in_refs, dest_refs, scratch):
        n = self.n = len(in_refs)
        self.ins, self.dests, self.stage = in_refs, dest_refs, scratch[:n]
        self.send_sems, self.recv_sems, self.local_sems = scratch[n:]
        x, y, c = _mesh_pos()
        self.c, self.me, self.sibling = c, (x, y, c), (x, y, 1 - c)
        self.xn, self.yn, self.diag = (1 - x, y), (x, 1 - y), (1 - x, 1 - y)

    def _halves(self, a):
        rows = self.stage[a].shape[0]
        cut = (rows // 2 + 15) // 16 * 16
        return pl.ds(0, cut), pl.ds(cut, rows - cut)

    def _copy(self, a, k, block, to, from_stage=False, rows=None):
        dst = self.dests[a].at[_slot(*block)]
        src = self.stage[a] if from_stage else dst
        if rows is not None:
            src, dst = src.at[rows], dst.at[rows]
        return pltpu.make_async_remote_copy(
            src_ref=src, dst_ref=dst, send_sem=self.send_sems.at[self.COPIES * a + k],
            recv_sem=self.recv_sems.at[self.COPIES * a + k], device_id=to, device_id_type=MESH_ID)

    def _mine(self):
        return [pltpu.make_async_copy(self.stage[a], self.dests[a].at[_slot(*self.me)], self.local_sems.at[a]) for a in range(self.n)]

    def _first(self, a):
        return [self._copy(a, 0, self.me, self.sibling, True), self._copy(a, 1, self.me, (*self.xn, self.c), True),
                self._copy(a, 2, self.me, (*self.yn, self.c), True)]

    def _relays(self, a):
        lo, hi = self._halves(a)
        return [self._copy(a, 3, (*self.xn, self.c), (*self.yn, self.c), rows=lo),
                self._copy(a, 4, (*self.yn, self.c), (*self.xn, self.c), rows=hi)]

    def _passes(self, a):
        return [self._copy(a, 5 + j, (*chip, self.c), self.sibling) for j, chip in enumerate((self.xn, self.yn, self.diag))]

    def start(self):
        for a in range(self.n):
            self.stage[a][...] = self.ins[a][...].astype(BF16)
        for cp in self._mine():
            cp.start()
        for a in range(self.n):
            for cp in self._first(a):
                cp.start()

    def forward(self):
        for a in range(self.n):
            relays, passes = self._relays(a), self._passes(a)
            for j, chip in enumerate((self.xn, self.yn)):
                self._copy(a, 1 + j, (*chip, self.c), self.me).wait_recv()
                relays[j].start()
                passes[j].start()

    def finish(self):
        for a in range(self.n):
            lo, hi = self._halves(a)
            self._copy(a, 3, (*self.diag, self.c), self.me, rows=lo).wait_recv()
            self._copy(a, 4, (*self.diag, self.c), self.me, rows=hi).wait_recv()
            self._passes(a)[2].start()
        for a in range(self.n):
            self._copy(a, 0, self.sibling, self.me).wait_recv()
            for j, chip in enumerate((self.xn, self.yn, self.diag)):
                self._copy(a, 5 + j, (*chip, 1 - self.c), self.me).wait_recv()
            for cp in self._first(a) + self._relays(a) + self._passes(a):
                cp.wait_send()
        for cp in self._mine():
            cp.wait()


N_CHIPS = 4


def _reduce_scatter(slabs, packed):
    def body(*refs):
        n = len(slabs)
        rs = _ReduceScatter(refs[:n], refs[n], refs[n + 1:2 * n + 1], refs[2 * n + 1], refs[2 * n + 2:])
        rs.start1()
        rs.finish1_start2()
        rs.relay2()
        rs.finish2()

    shapes = [s.shape[1:] for s in slabs]
    return pl.pallas_call(
        body, name="reduce_scatter_grads",
        in_specs=[HBM_SPEC] * len(slabs) + [VMEM_SPEC], out_specs=[VMEM_SPEC] * (len(slabs) + 1),
        out_shape=[SDS(s, F32) for s in shapes] + [SDS(packed.shape, F32)],
        scratch_shapes=_rs_scratch(shapes, packed.shape), compiler_params=_cp(),
    )(*slabs, packed)


def _rs_scratch(shapes, packed_shape):
    n = len(shapes)
    n1, n2 = N_CHIPS * n + 1, _ReduceScatter.L2_COPIES * n + N_CHIPS - 1
    dma = pltpu.SemaphoreType.DMA
    packed = [] if packed_shape is None else [pltpu.VMEM(packed_shape, F32), pltpu.VMEM((N_CHIPS,) + tuple(packed_shape), F32)]
    return ([pltpu.VMEM((N_CHIPS,) + tuple(s), BF16) for s in shapes] * 2 + [pltpu.VMEM((N_CHIPS - 1,) + tuple(s), BF16) for s in shapes] * 2
            + packed + [dma((max(N_CHIPS * n, 1),)), dma((n1,)), dma((n1,)), dma((n2,)), dma((n2,))])


class _ReduceScatter:
    L2_COPIES = 6

    def __init__(self, slab_refs, packed_ref, out_refs, ptot_ref, scratch):
        n = self.n = len(slab_refs)
        self.slabs, self.packed, self.outs, self.ptot = slab_refs, packed_ref, out_refs, ptot_ref
        self.own1, self.land1, self.send2, self.land2 = (scratch[k * n:(k + 1) * n] for k in range(4))
        rest = scratch[4 * n:]
        if packed_ref is not None:
            self.pland1, self.pland2 = rest[:2]
            rest = rest[2:]
        self.loc_sems, self.send1_sems, self.recv1_sems, self.send2_sems, self.recv2_sems = rest
        self.x, self.y, self.c = _mesh_pos()

    def _chip(self, r):
        return (1 - self.x if r & 2 else self.x, 1 - self.y if r & 1 else self.y)

    @staticmethod
    def _remote(src, dst, send_sem, recv_sem, to):
        return pltpu.make_async_remote_copy(src_ref=src, dst_ref=dst, send_sem=send_sem, recv_sem=recv_sem, device_id=to,
                                            device_id_type=MESH_ID)

    def _copies1(self):
        c, sibling = self.c, (self.x, self.y, 1 - self.c)
        cps = []
        for a in range(self.n):
            for r in range(N_CHIPS):
                k = N_CHIPS * a + r
                cps.append(pltpu.make_async_copy(self.slabs[a].at[_slot(*self._chip(r), c)], self.own1[a].at[r], self.loc_sems.at[k]))
                cps.append(self._remote(self.slabs[a].at[_slot(*self._chip(r), 1 - c)], self.land1[a].at[r],
                                        self.send1_sems.at[k], self.recv1_sems.at[k], sibling))
        if self.packed is not None:
            k = N_CHIPS * self.n
            cps.append(self._remote(self.packed, self.pland1, self.send1_sems.at[k], self.recv1_sems.at[k], sibling))
        return cps

    def _halves(self, a):
        rows = self.send2[a].shape[1]
        cut = (rows // 2 + 15) // 16 * 16
        return pl.ds(0, cut), pl.ds(cut, rows - cut)

    def _copies2(self, a):
        lo, hi = self._halves(a)
        s2, l2 = self.send2[a], self.land2[a]
        xn, yn = (*self._chip(2), self.c), (*self._chip(1), self.c)
        plan = [(1, lo, xn), (2, lo, xn), (0, hi, yn), (2, hi, yn), (0, lo, yn), (1, hi, xn)]
        return [self._remote(s2.at[slot].at[rows], l2.at[slot].at[rows], self.send2_sems.at[self.L2_COPIES * a + k],
                             self.recv2_sems.at[self.L2_COPIES * a + k], to) for k, (slot, rows, to) in enumerate(plan)]

    def _copies2_packed(self):
        base = self.L2_COPIES * self.n - 1
        return [self._remote(self.pland2.at[0], self.pland2.at[r], self.send2_sems.at[base + r], self.recv2_sems.at[base + r],
                             (*self._chip(r), self.c)) for r in range(1, N_CHIPS)]

    def start1(self):
        for cp in self._copies1():
            cp.start()

    def finish1_start2(self):
        for cp in self._copies1():
            cp.wait()
        for a in range(self.n):
            self.outs[a][...] = self.own1[a][0].astype(F32) + self.land1[a][0].astype(F32)
            for r in range(1, N_CHIPS):
                self.send2[a][r - 1] = (self.own1[a][r].astype(F32) + self.land1[a][r].astype(F32)).astype(BF16)
            for cp in self._copies2(a)[:4]:
                cp.start()
        if self.packed is not None:
            self.pland2[0] = self.packed[...] + self.pland1[...]
            for cp in self._copies2_packed():
                cp.start()

    def relay2(self):
        for a in range(self.n):
            lo, hi = self._halves(a)
            cps, s2, l2 = self._copies2(a), self.send2[a], self.land2[a]
            cps[1].wait_recv()
            s2[0, lo] = (s2[0, lo].astype(F32) + l2[2, lo].astype(F32)).astype(BF16)
            cps[4].start()
            cps[3].wait_recv()
            s2[1, hi] = (s2[1, hi].astype(F32) + l2[2, hi].astype(F32)).astype(BF16)
            cps[5].start()

    def finish2(self):
        for a in range(self.n):
            cps = self._copies2(a)
            for k in (0, 2, 4, 5):
                cps[k].wait_recv()
            for cp in cps:
                cp.wait_send()
            l2 = self.land2[a]
            self.outs[a][...] = self.outs[a][...] + (l2[0].astype(F32) + l2[1].astype(F32))
        if self.packed is not None:
            for cp in self._copies2_packed():
                cp.wait()
            p2 = self.pland2
            self.ptot[...] = (p2[0] + p2[1]) + (p2[2] + p2[3])


def _adamw(ws, gs, ms, vs):
    n = len(ws)

    def body(*refs):
        for k in range(n):
            w, g, m, v = (refs[j * n + k][...] for j in range(4))
            d_ref, nm_ref, nv_ref = (refs[(4 + j) * n + k] for j in range(3))
            m = ADAM_B1 * m + (1.0 - ADAM_B1) * g
            v = ADAM_B2 * v + (1.0 - ADAM_B2) * jnp.square(g)
            m_hat = m / (1.0 - ADAM_B1 ** ADAM_STEP)
            v_hat = v / (1.0 - ADAM_B2 ** ADAM_STEP)
            d_ref[...] = -ADAM_LR * (m_hat / (jnp.sqrt(v_hat) + ADAM_EPS) + ADAM_WD * w)
            nm_ref[...] = m
            nv_ref[...] = v

    outs = pl.pallas_call(
        body, name="adamw", in_specs=[VMEM_SPEC] * (4 * n), out_specs=[VMEM_SPEC] * (3 * n),
        out_shape=[SDS(w.shape, F32) for w in ws] * 3, compiler_params=_cp(),
    )(*ws, *gs, *ms, *vs)
    return outs[:n], outs[n:2 * n], outs[2 * n:]


WEIGHTS = ("norm_in", "w_in", "q_norm", "w_uq", "kv_norm", "w_ukv", "pool_w", "pool_scale", "w_branch_attn", "w_branch_pool",
           "w_out", "norm_final")
SUBLANES = 8


def _pack_rows(a):
    a = a.reshape(-1, LANES)
    return jnp.pad(a, ((0, -a.shape[0] % SUBLANES), (0, 0)))


def _unpack_rows(packed, like):
    out, row = [], 0
    for a in like:
        rows = a.size // LANES
        out.append(packed[row:row + rows].reshape(a.shape))
        row += rows + (-rows % SUBLANES)
    return out


def kernel(x, norm_in, w_in, q_norm, w_uq, kv_norm, w_ukv, pool_w, pool_scale, w_branch_attn, w_branch_pool, w_out, norm_final, loss_target, m_norm_in, m_w_in, m_q_norm, m_w_uq, m_kv_norm, m_w_ukv, m_pool_w, m_pool_scale, m_w_branch_attn, m_w_branch_pool, m_w_out, m_norm_final, v_norm_in, v_w_in, v_q_norm, v_w_uq, v_kv_norm, v_w_ukv, v_pool_w, v_pool_scale, v_w_branch_attn, v_w_branch_pool, v_w_out, v_norm_final):
    w = dict(norm_in=norm_in, w_in=w_in, q_norm=q_norm, w_uq=w_uq, kv_norm=kv_norm, w_ukv=w_ukv, pool_w=pool_w, pool_scale=pool_scale,
             w_branch_attn=w_branch_attn, w_branch_pool=w_branch_pool, w_out=w_out, norm_final=norm_final)
    m = dict(norm_in=m_norm_in, w_in=m_w_in, q_norm=m_q_norm, w_uq=m_w_uq, kv_norm=m_kv_norm, w_ukv=m_w_ukv, pool_w=m_pool_w,
             pool_scale=m_pool_scale, w_branch_attn=m_w_branch_attn, w_branch_pool=m_w_branch_pool, w_out=m_w_out, norm_final=m_norm_final)
    v = dict(norm_in=v_norm_in, w_in=v_w_in, q_norm=v_q_norm, w_uq=v_w_uq, kv_norm=v_kv_norm, w_ukv=v_w_ukv, pool_w=v_pool_w,
             pool_scale=v_pool_scale, w_branch_attn=v_w_branch_attn, w_branch_pool=v_w_branch_pool, w_out=v_w_out, norm_final=v_norm_final)

    def as2d(name, a):
        if name == "w_in":
            return a.T
        if name == "w_uq":
            return a
        if name == "w_ukv":
            return a.reshape(a.shape[0], -1)
        if name == "pool_w":
            return a.reshape(-1, GROUP)
        return a.reshape(1, -1) if a.ndim == 1 else a

    def unshape(name, a):
        return a.T if name == "w_in" else a.reshape(w[name].shape)

    w_in_pad, w_uq_full, w_ukv_full = _all_gather_bf16([as2d(k, w[k]) for k in ("w_in", "w_uq", "w_ukv")])
    loss, grad_x, g2d = _local_step(
        x.reshape(x.shape[1:]), loss_target.reshape(x.shape[1:]), norm_in, w_in_pad, q_norm,
        w_uq_full.reshape(Q_RANK, HEADS, NOPE + ROPE), kv_norm, w_ukv_full.reshape(KV_RANK, HEADS, NOPE + VDIM),
        pool_w, pool_scale, [w_branch_attn, w_branch_pool, w_out], norm_final)

    deltas, new_m, new_v = _adamw([as2d(k, w[k]) for k in WEIGHTS], [g2d[k] for k in WEIGHTS],
                                  [as2d(k, m[k]) for k in WEIGHTS], [as2d(k, v[k]) for k in WEIGHTS])
    shaped = lambda arrs: [unshape(k, a) for k, a in zip(WEIGHTS, arrs)]
    return (loss, grad_x.reshape(x.shape), *shaped([g2d[k] for k in WEIGHTS]), *shaped(deltas), *shaped(new_m), *shaped(new_v))
```

```python
import functools

import jax
import jax.numpy as jnp
import numpy as np
from jax import lax
from jax.experimental import pallas as pl
from jax.experimental.pallas import tpu as pltpu

F32 = jnp.float32
BF16 = jnp.bfloat16
SDS = jax.ShapeDtypeStruct

D_MODEL = 1024
HEADS = 8
NOPE = 64
ROPE = 32
VDIM = 64
Q_RANK = 384
KV_RANK = 256
MLA_W = HEADS * VDIM
POOL_W = 512
POOL_GROUPS = 4
GROUP = POOL_W // POOL_GROUPS
CHUNK = 64
ROPE_THETA = 10000.0
EPS = 1e-6
SCALE = (NOPE + ROPE) ** -0.5
LOG2E = 1.4426950408889634
LN2 = 0.6931471805599453
QK_SCALE_LOG2 = SCALE * LOG2E
IN_TOTAL = 4256
ADAM_LR, ADAM_B1, ADAM_B2, ADAM_EPS, ADAM_WD, ADAM_STEP = 0.001, 0.9, 0.999, 1e-08, 0.01, 10

N_DEV = 8
LANES = 128
HEAD_PAD = LANES
HW = HEADS * HEAD_PAD

ZQ, ZKV, ZKR, GA, UP, GP, GM, ZTOT = 0, 384, 640, 768, 1280, 1792, 2304, 4352
FRONT_W = GA
ZKR_ORIG = 640

VMEM_LIMIT = 62 * 1024 * 1024


def _cp(sem=None, **kw):
    if sem is not None:
        kw["dimension_semantics"] = sem
    return pltpu.CompilerParams(vmem_limit_bytes=VMEM_LIMIT, **kw)


def _mm(a, b):
    return lax.dot_general(a, b, (((1,), (0,)), ((), ())), preferred_element_type=F32)


def _mm_nt(a, b):
    return lax.dot_general(a, b, (((1,), (1,)), ((), ())), preferred_element_type=F32)


def _mm_tn(a, b):
    return lax.dot_general(a, b, (((0,), (0,)), ((), ())), preferred_element_type=F32)


def _row_spec(tm, w):
    return pl.BlockSpec((tm, w), lambda i: (i, 0))


def _full_spec(a):
    nd = len(a.shape)
    return pl.BlockSpec(a.shape, lambda *_: (0,) * nd)


def _rope(v, c, sa, sb, sign):
    n = v.shape[-1]
    reps = n // LANES
    if reps > 1:
        c, sa, sb = (jnp.tile(t, (1, reps)) for t in (c, sa, sb))
    up = pltpu.roll(v, n - ROPE // 2, 1)
    dn = pltpu.roll(v, ROPE // 2, 1)
    return v * c + sign * (up * sa + dn * sb)


def _rope_tables(T):
    half = ROPE // 2
    inv_freq = np.float32(ROPE_THETA) ** (-np.arange(half, dtype=np.float32) / np.float32(half))
    ang = np.arange(T, dtype=np.float32)[:, None] * inv_freq[None, :].astype(np.float32)
    cos, sin = np.cos(ang.astype(np.float64)).astype(np.float32), np.sin(ang.astype(np.float64)).astype(np.float32)
    z16 = np.zeros((T, half), np.float32)
    z32 = np.zeros((T, LANES - NOPE - ROPE), np.float32)
    c = np.concatenate([np.ones((T, NOPE), np.float32), cos, cos, z32], axis=1)
    sa = np.concatenate([np.zeros((T, NOPE), np.float32), -sin, z16, z32], axis=1)
    sb = np.concatenate([np.zeros((T, NOPE), np.float32), z16, sin, z32], axis=1)
    return jnp.asarray(c), jnp.asarray(sa), jnp.asarray(sb)


def _silu_parts(g):
    sg = jax.nn.sigmoid(g)
    return g * sg, sg + g * sg * (1.0 - sg)


def _in_proj(x2, norm_in, w_in_pad, q_norm, wuq_pad, kv_norm, wukv, rc, rsa, rsb, tm, late_shards):
    T = x2.shape[0]
    steps = T // tm
    n = len(late_shards)

    def body(x_ref, nin_ref, win_ref, qn_ref, wuq_ref, kvn_ref, wukv_ref, c_ref, sa_ref, sb_ref, *rest):
        hn_ref, zgm_ref, zga_ref, zup_ref, zgp_ref, zfr_ref, q_ref, k_ref, v_ref, vt_ref = rest[n:n + 10]
        ag = _AllGather(rest[:n], rest[n + 10:2 * n + 10], rest[2 * n + 10:])
        step = pl.program_id(0)
        pl.when(step == 0)(ag.start)
        pl.when(step == min(3, steps - 1))(ag.forward)
        xf = x_ref[...]
        r = lax.rsqrt(jnp.mean(xf * xf, axis=-1, keepdims=True) + EPS)
        hn = (xf * r * nin_ref[...]).astype(BF16)
        hn_ref[...] = hn
        zfr = _mm_nt(hn, win_ref[ZQ:GA, :])
        zfr_ref[...] = zfr
        zq, zkv, zkr = zfr[:, ZQ:ZKV], zfr[:, ZKV:ZKR], zfr[:, ZKR:GA]
        c, sa, sb = c_ref[...], sa_ref[...], sb_ref[...]
        rq = lax.rsqrt(jnp.mean(zq * zq, axis=-1, keepdims=True) + EPS)
        cq = (zq * rq * qn_ref[...]).astype(BF16)
        rkv = lax.rsqrt(jnp.mean(zkv * zkv, axis=-1, keepdims=True) + EPS)
        ckv = (zkv * rkv * kvn_ref[...]).astype(BF16)
        zga_ref[...] = _mm_nt(hn, win_ref[GA:UP, :])
        zup_ref[...] = _mm_nt(hn, win_ref[UP:GP, :])
        zgp_ref[...] = _mm_nt(hn, win_ref[GP:GM, :])
        q_raw = _mm(cq, wuq_ref[...])
        kv = _mm(ckv, wukv_ref[...])
        zgm_ref[...] = _mm_nt(hn, win_ref[GM:ZTOT, :])
        q = _rope(q_raw, c, sa, sb, 1.0)
        q_ref[...] = (q * QK_SCALE_LOG2).astype(BF16)
        kr = _rope(zkr, c, sa, sb, 1.0)
        lane = lax.broadcasted_iota(jnp.int32, kv.shape, 1) % LANES
        k_ref[...] = jnp.where(lane < NOPE, kv, jnp.tile(kr, (1, HEADS))).astype(BF16)
        v = jnp.where(lane < NOPE, 1.0, kv).astype(BF16)
        v_ref[...] = v
        vt_ref[...] = v.T
        pl.when(step == steps - 1)(ag.finish)

    ins = (x2, norm_in, w_in_pad, q_norm, wuq_pad, kv_norm, wukv, rc, rsa, rsb)
    in_specs = [_row_spec(tm, D_MODEL), _full_spec(norm_in), _full_spec(w_in_pad), _full_spec(q_norm), _full_spec(wuq_pad),
                _full_spec(kv_norm), _full_spec(wukv), _row_spec(tm, LANES), _row_spec(tm, LANES), _row_spec(tm, LANES)]
    widths = [(D_MODEL, BF16), (ZTOT - GM, F32), (UP - GA, F32), (GP - UP, F32), (GM - GP, F32), (FRONT_W, F32),
              (HW, BF16), (HW, BF16), (HW, BF16)]
    return pl.pallas_call(
        body, name="in_proj", grid=(steps,), in_specs=in_specs + [_full_spec(s) for s in late_shards],
        out_specs=[_row_spec(tm, w) for w, _ in widths] + [pl.BlockSpec((HW, tm), lambda i: (0, i))] + [HBM_SPEC] * n,
        out_shape=[SDS((T, w), dt) for w, dt in widths] + [SDS((HW, T), BF16)]
        + [SDS((N_DEV,) + s.shape, BF16) for s in late_shards],
        scratch_shapes=_ag_scratch([s.shape for s in late_shards]), compiler_params=_cp(("arbitrary",)),
    )(*ins, *late_shards)


def _chunk_mask(n_q, n_k, q_off, transposed):
    shape = (n_k, n_q) if transposed else (n_q, n_k)
    q = (lax.broadcasted_iota(jnp.int32, shape, 1 if transposed else 0) + q_off) // CHUNK
    k = lax.broadcasted_iota(jnp.int32, shape, 0 if transposed else 1) // CHUNK
    return k <= q


def _store_pair_rows(ref, k, pair):
    t = pair.T
    ref[k, 0:1, :] = t[0:1, :]
    ref[k, 1:2, :] = t[VDIM:VDIM + 1, :]


def _attn_fwd(q_att, k_att, vt_att, tq, hps):
    T = q_att.shape[0]
    head_lanes = [slice(h * LANES, (h + 1) * LANES) for h in range(hps)]

    def body(q_ref, k_ref, vt_ref, o_ref, lser_ref):
        i = pl.program_id(1)
        mask = _chunk_mask(tq, tq, 0, True)
        lane = lax.broadcasted_iota(jnp.int32, (tq, LANES), 1)
        qs = [q_ref[:, hs] for hs in head_lanes]

        def step(j, carry, masked):
            off = pl.multiple_of(j * tq, tq)
            sts = [_mm_nt(k_ref[pl.ds(off, tq), hs], qh) for qh, hs in zip(qs, head_lanes)]
            if masked:
                sts = [jnp.where(mask, st, -jnp.inf) for st in sts]
            ms = [jnp.maximum(m, jnp.max(st, axis=0, keepdims=True)) for (m, _), st in zip(carry, sts)]
            pts = [jnp.exp2(st - m_new).astype(BF16) for st, m_new in zip(sts, ms)]
            return tuple((m_new, jnp.exp2(m - m_new) * acc + _mm(vt_ref[hs, pl.ds(off, tq)], pt))
                         for (m, acc), m_new, pt, hs in zip(carry, ms, pts, head_lanes))

        init = ((jnp.full((1, tq), -jnp.inf, F32), jnp.zeros((LANES, tq), F32)),) * hps
        res = step(i, lax.fori_loop(0, i, functools.partial(step, masked=False), init), True)
        for pair in range(hps // 2):
            (ma, acca), (mb, accb) = res[2 * pair], res[2 * pair + 1]
            la, lb = acca[:1], accb[:1]
            oa, ob = (acca / la).T, (accb / lb).T
            o_ref[:, pair * LANES:(pair + 1) * LANES] = jnp.where(lane < VDIM, pltpu.roll(oa, VDIM, 1), ob)
            lser_ref[pair, 0:1, :] = ma + jnp.log2(la)
            lser_ref[pair, 1:2, :] = mb + jnp.log2(lb)

    qspec = pl.BlockSpec((tq, hps * LANES), lambda p, i: (i, p))
    kspec = pl.BlockSpec((T, hps * LANES), lambda p, i: (0, p))
    vspec = pl.BlockSpec((hps * LANES, T), lambda p, i: (p, 0))
    ospec = pl.BlockSpec((tq, hps * VDIM), lambda p, i: (i, p))
    return pl.pallas_call(
        body, name="attn_fwd", grid=(HEADS // hps, T // tq), in_specs=[qspec, kspec, vspec],
        out_specs=[ospec, pl.BlockSpec((hps // 2, 2, tq), lambda p, i: (p, 0, i))],
        out_shape=[SDS((T, MLA_W), F32), SDS((HEADS // 2, 2, T), F32)],
        compiler_params=_cp(("parallel", "parallel")),
    )(q_att, k_att, vt_att)


def _pick(g, vals):
    out = vals[-1]
    for k in range(len(vals) - 2, -1, -1):
        out = jnp.where(g == k, vals[k], out)
    return out


def _window_sum(u, g, forward):
    T = u.shape[0]
    row = lax.broadcasted_iota(jnp.int32, u.shape, 0)

    def sh(s, k):
        if forward:
            return jnp.where(row >= k, pltpu.roll(s, k, 0), 0.0)
        return jnp.where(row < T - k, pltpu.roll(s, T - k, 0), 0.0)

    sums, s = [], u
    for k in (1, 2, 4, 8):
        s = s + sh(s, k)
        sums.append(s)
    return _pick(g, sums)


def _pool_count(shape, g):
    row = lax.broadcasted_iota(jnp.int32, shape, 0)
    return jnp.minimum(row + 1, lax.shift_left(jnp.int32(2), g)).astype(F32)


def _pool_fwd(zup, zgp, pool_w, pool_scale):
    T = zup.shape[0]

    def body(u_ref, g_ref, w_ref, sc_ref, y_ref):
        g = pl.program_id(0)
        u = u_ref[...]
        d = _window_sum(u, g, True) / _pool_count(u.shape, g) - u
        lin = _mm(d.astype(BF16), w_ref[0].astype(BF16))
        silu, _ = _silu_parts(g_ref[...])
        y_ref[...] = (lin * sc_ref[...] * silu).astype(BF16)

    col = pl.BlockSpec((T, GROUP), lambda g: (0, g))
    return pl.pallas_call(
        body, name="pool_fwd", grid=(POOL_GROUPS,),
        in_specs=[col, col, pl.BlockSpec((1, GROUP, GROUP), lambda g: (g, 0, 0)), pl.BlockSpec((1, GROUP), lambda g: (0, g))],
        out_specs=col, out_shape=SDS((T, POOL_W), BF16), compiler_params=_cp(("parallel",)),
    )(zup, zgp, pool_w, pool_scale)


def _pool_bwd(zup, zgp, dyp, pool_w, pool_scale):
    T = zup.shape[0]

    def body(u_ref, g_ref, dy_ref, w_ref, sc_ref, du_ref, dg_ref, gw_ref, gsc_ref):
        g = pl.program_id(0)
        u = u_ref[...]
        cnt = _pool_count(u.shape, g)
        d = (_window_sum(u, g, True) / cnt - u).astype(BF16)
        wb = w_ref[0].astype(BF16)
        lin = _mm(d, wb)
        sc = sc_ref[...]
        silu, dsilu = _silu_parts(g_ref[...])
        dy = dy_ref[...]
        dg_ref[...] = (dy * lin * sc * dsilu).astype(BF16)
        dpre = dy * silu
        gsc_ref[...] = jnp.sum(dpre * lin, axis=0, keepdims=True)
        dlin = (dpre * sc).astype(BF16)
        gw_ref[0] = _mm_tn(d, dlin)
        dd = _mm_nt(dlin, wb)
        du_ref[...] = (_window_sum(dd / cnt, g, False) - dd).astype(BF16)

    col = pl.BlockSpec((T, GROUP), lambda g: (0, g))
    wspec = pl.BlockSpec((1, GROUP, GROUP), lambda g: (g, 0, 0))
    vspec = pl.BlockSpec((1, GROUP), lambda g: (0, g))
    return pl.pallas_call(
        body, name="pool_bwd", grid=(POOL_GROUPS,), in_specs=[col, col, col, wspec, vspec], out_specs=[col, col, wspec, vspec],
        out_shape=[SDS((T, POOL_W), BF16), SDS((T, POOL_W), BF16), SDS((POOL_GROUPS, GROUP, GROUP), F32), SDS((1, POOL_W), F32)],
        compiler_params=_cp(("parallel",)),
    )(zup, zgp, dyp, pool_w, pool_scale)


def _tail(x2, tgt, o, zga, ypool, zgm, wba, wbp, wout, norm_final, tm):
    T = x2.shape[0]
    steps = T // tm
    cols = D_MODEL // N_DEV

    def body(x_ref, tgt_ref, o_ref, zga_ref, yp_ref, zgm_ref, wba_ref, wbp_ref, wout_ref, nf_ref,
             loss_ref, dh_ref, dgm_ref, doop_ref, dga_ref, dcapr_ref, dyp_ref, swout_ref, swba_ref, swbp_ref, gnf_ref,
             gwout_ref, gwba_ref, gwbp_ref):
        @pl.when(pl.program_id(0) == 0)
        def _():
            for ref in (loss_ref, gwout_ref, gwba_ref, gwbp_ref, gnf_ref):
                ref[...] = jnp.zeros_like(ref)

        o_v = o_ref[...]
        silu, dsilu = _silu_parts(zga_ref[...])
        ya = (o_v * silu).astype(BF16)
        yp = yp_ref[...]
        wba_v = jnp.concatenate([wba_ref[k] for k in range(N_DEV)], axis=1)
        wbp_v = jnp.concatenate([wbp_ref[k] for k in range(N_DEV)], axis=1)
        wout_v = wout_ref[...]
        a = _mm(ya, wba_v)
        p = _mm(yp, wbp_v)
        gate = jax.nn.sigmoid(zgm_ref[...])
        ga, gp = gate[:, :D_MODEL], gate[:, D_MODEL:]
        mg = (ga * a + gp * p).astype(BF16)
        h = x_ref[...] + _mm(mg, wout_v)
        r = lax.rsqrt(jnp.mean(h * h, axis=-1, keepdims=True) + EPS)
        gf = nf_ref[...]
        hr = h * r
        e = hr * gf - tgt_ref[...]
        loss_ref[...] += (0.5 / D_MODEL) * jnp.sum(e * e)
        dy = e * (1.0 / D_MODEL)
        gnf_ref[...] += jnp.sum(dy * hr, axis=0, keepdims=True)
        u = dy * gf
        dh = r * (u - hr * jnp.mean(u * hr, axis=-1, keepdims=True))
        dh_ref[...] = dh
        dhb = dh.astype(BF16)
        dmg = _mm_nt(dhb, wout_v)
        dab = (dmg * ga).astype(BF16)
        dpb = (dmg * gp).astype(BF16)
        dya = _mm_nt(dab, wba_v)
        dyp_ref[...] = _mm_nt(dpb, wbp_v)
        gwout_ref[...] += _mm_tn(mg, dhb)
        gwba_ref[...] += _mm_tn(ya, dab)
        gwbp_ref[...] += _mm_tn(yp, dpb)
        dgm_ref[:, :D_MODEL] = (dmg * a * ga * (1.0 - ga)).astype(BF16)
        dgm_ref[:, D_MODEL:] = (dmg * p * gp * (1.0 - gp)).astype(BF16)
        do = dya * silu
        dga_ref[...] = (dya * o_v * dsilu).astype(BF16)
        prod = do * o_v
        lo = lax.broadcasted_iota(jnp.int32, (tm, LANES), 1) < VDIM
        for pair in range(HEADS // 2):
            ls = slice(pair * LANES, (pair + 1) * LANES)
            do_p, prod_p = do[:, ls], prod[:, ls]
            dcap_a = jnp.sum(jnp.where(lo, prod_p, 0.0), axis=-1, keepdims=True)
            dcap_b = jnp.sum(jnp.where(lo, 0.0, prod_p), axis=-1, keepdims=True)
            _store_pair_rows(dcapr_ref, pair, jnp.where(lo, dcap_a, dcap_b))
            doop_ref[:, 2 * pair * LANES:(2 * pair + 1) * LANES] = jnp.where(lo, 0.0, pltpu.roll(do_p, VDIM, 1)).astype(BF16)
            doop_ref[:, (2 * pair + 1) * LANES:(2 * pair + 2) * LANES] = jnp.where(lo, 0.0, do_p).astype(BF16)

        @pl.when(pl.program_id(0) == steps - 1)
        def _():
            for k in range(N_DEV):
                swout_ref[k] = gwout_ref[k * cols:(k + 1) * cols, :].astype(BF16)
                swba_ref[k] = gwba_ref[:, k * cols:(k + 1) * cols].astype(BF16)
                swbp_ref[k] = gwbp_ref[:, k * cols:(k + 1) * cols].astype(BF16)

    ins = (x2, tgt, o, zga, ypool, zgm, wba, wbp, wout, norm_final)
    in_specs = [_row_spec(tm, D_MODEL), _row_spec(tm, D_MODEL), _row_spec(tm, MLA_W), _row_spec(tm, MLA_W), _row_spec(tm, POOL_W),
                _row_spec(tm, 2 * D_MODEL), _full_spec(wba), _full_spec(wbp), _full_spec(wout), _full_spec(norm_final)]
    outs = [SDS((8, LANES), F32), SDS((T, D_MODEL), F32), SDS((T, 2 * D_MODEL), BF16), SDS((T, HW), BF16), SDS((T, MLA_W), BF16),
            SDS((HEADS // 2, 2, T), F32), SDS((T, POOL_W), F32),
            SDS((N_DEV, cols, D_MODEL), BF16), SDS((N_DEV, MLA_W, cols), BF16), SDS((N_DEV, POOL_W, cols), BF16), SDS((1, D_MODEL), F32)]
    out_specs = [_full_spec(outs[0]), _row_spec(tm, D_MODEL), _row_spec(tm, 2 * D_MODEL), _row_spec(tm, HW), _row_spec(tm, MLA_W),
                 pl.BlockSpec((HEADS // 2, 2, tm), lambda i: (0, 0, i)), _row_spec(tm, POOL_W),
                 _full_spec(outs[7]), _full_spec(outs[8]), _full_spec(outs[9]), _full_spec(outs[10])]
    return pl.pallas_call(
        body, name="tail", grid=(steps,), in_specs=in_specs, out_specs=out_specs, out_shape=outs,
        scratch_shapes=[pltpu.VMEM((D_MODEL, D_MODEL), F32), pltpu.VMEM((MLA_W, D_MODEL), F32), pltpu.VMEM((POOL_W, D_MODEL), F32)],
        compiler_params=_cp(("arbitrary",)),
    )(*ins)


def _attn_bwd(q_att, k_att, v_att, doop, lse_rows, dcap_rows, tq, hps, slabs, packed):
    T = q_att.shape[0]
    nq = T // tq
    n = len(slabs)
    groups = HEADS // hps
    head_lanes = [slice(h * LANES, (h + 1) * LANES) for h in range(hps)]

    def body(q_ref, k_ref, v_ref, doop_ref, lse_ref, dcap_ref, *rest):
        slab_refs, packed_ref = rest[:n], rest[n]
        dq_ref, dkv_ref, dkr_ref = rest[n + 1:n + 4]
        sum_refs, ptot_ref = rest[n + 4:2 * n + 4], rest[2 * n + 4]
        dq_acc = rest[2 * n + 5]
        rs = _ReduceScatter(slab_refs, packed_ref, sum_refs, ptot_ref, rest[2 * n + 6:])
        group, j = pl.program_id(0), pl.program_id(1)
        tick, last = group * nq + j, groups * nq - 1
        pl.when(tick == 0)(rs.start1)
        pl.when(tick == min(1, last))(rs.finish1_start2)
        pl.when(tick == min(3, last))(rs.relay2)
        mask = _chunk_mask(tq, tq, 0, True)
        lane = lax.broadcasted_iota(jnp.int32, (tq, LANES), 1)
        ks = [k_ref[:, hs] for hs in head_lanes]
        vs = [v_ref[:, hs] for hs in head_lanes]
        kts = [kh.T for kh in ks]

        @pl.when(j == 0)
        def _():
            dq_acc[...] = jnp.zeros_like(dq_acc)

        def step(i, carry, masked):
            rows = pl.ds(pl.multiple_of(i * tq, tq), tq)
            heads = range(hps)
            stat = lambda h: (h // 2, slice(h % 2, h % 2 + 1), rows)
            qhs = [q_ref[rows, hs] for hs in head_lanes]
            doops = [doop_ref[rows, hs] for hs in head_lanes]
            sts = [_mm_nt(ks[h], qhs[h]) for h in heads]
            dpts = [_mm_nt(vs[h], doops[h]) for h in heads]
            pts = [jnp.exp2(sts[h] - lse_ref[stat(h)]) for h in heads]
            if masked:
                pts = [jnp.where(mask, pt, 0.0) for pt in pts]
            dsts = [(pts[h] * (dpts[h] - dcap_ref[stat(h)])).astype(BF16) for h in heads]
            dvs = [_mm(pts[h].astype(BF16), doops[h]) for h in heads]
            dks = [_mm(dsts[h], qhs[h]) for h in heads]
            for h, hs in enumerate(head_lanes):
                dq_acc[hs, rows] += _mm(kts[h], dsts[h])
            return tuple((dk + dks[h], dv + dvs[h]) for h, (dk, dv) in enumerate(carry))

        zero = jnp.zeros((tq, LANES), F32)
        carry = step(j, ((zero, zero),) * hps, True)
        res = lax.fori_loop(j + 1, nq, functools.partial(step, masked=False), carry)
        dkr = None
        for (dk, dv), hs in zip(res, head_lanes):
            dk = dk * LN2
            dkv_ref[:, hs] = jnp.where(lane < NOPE, dk, dv).astype(BF16)
            dkr = dk if dkr is None else dkr + dk
        dkr_ref[0] = jnp.where((lane >= NOPE) & (lane < NOPE + ROPE), dkr, 0.0)

        @pl.when(j == nq - 1)
        def _():
            dq_ref[...] = (dq_acc[...] * SCALE).T.astype(BF16)

        pl.when(tick == last)(rs.finish2)

    kspec = pl.BlockSpec((tq, hps * LANES), lambda p, j: (j, p))
    qspec = pl.BlockSpec((T, hps * LANES), lambda p, j: (0, p))
    rspec = pl.BlockSpec((hps // 2, 2, T), lambda p, j: (p, 0, 0))
    sums = [SDS(s.shape[1:], F32) for s in slabs] + [SDS(packed.shape, F32)]
    return pl.pallas_call(
        body, name="attn_bwd", grid=(groups, nq),
        in_specs=[qspec, kspec, kspec, qspec, rspec, rspec] + [HBM_SPEC] * n + [_full_spec(packed)],
        out_specs=[qspec, kspec, pl.BlockSpec((1, tq, LANES), lambda p, j: (p, j, 0))] + [_full_spec(s) for s in sums],
        out_shape=[SDS((T, HW), BF16), SDS((T, HW), BF16), SDS((groups, T, LANES), F32)] + sums,
        scratch_shapes=[pltpu.VMEM((hps * LANES, T), F32)] + _rs_scratch([s.shape for s in sums[:-1]], packed.shape),
        compiler_params=_cp(("arbitrary", "arbitrary")),
    )(q_att, k_att, v_att, doop, lse_rows, dcap_rows, *slabs, packed)


def _rms_bwd(z, gain, dout):
    r = lax.rsqrt(jnp.mean(z * z, axis=-1, keepdims=True) + EPS)
    zr = z * r
    u = dout * gain
    return r * (u - zr * jnp.mean(u * zr, axis=-1, keepdims=True)), jnp.sum(dout * zr, axis=0, keepdims=True)


def _mla_bwd(dq_att, dkv_nat, dkr4, zfr, q_norm, wuq_pad, kv_norm, wukv, rc, rsa, rsb, tm):
    T = dq_att.shape[0]

    def body(dq_ref, dkv_ref, dkr_ref, zfr_ref, qn_ref, wuq_ref, kvn_ref, wukv_ref, c_ref, sa_ref, sb_ref,
             dfr_ref, gwuq_ref, gwukv_ref, gqn_ref, gkvn_ref):
        @pl.when(pl.program_id(0) == 0)
        def _():
            for ref in (gwuq_ref, gwukv_ref, gqn_ref, gkvn_ref):
                ref[...] = jnp.zeros_like(ref)

        c, sa, sb = c_ref[...], sa_ref[...], sb_ref[...]
        zq, zkv = zfr_ref[:, :Q_RANK], zfr_ref[:, Q_RANK:Q_RANK + KV_RANK]
        qn, kvn = qn_ref[...], kvn_ref[...]
        dkv = dkv_ref[...]
        dckv = _mm_nt(dkv, wukv_ref[...])
        ckv = (zkv * lax.rsqrt(jnp.mean(zkv * zkv, axis=-1, keepdims=True) + EPS) * kvn).astype(BF16)
        gwukv_ref[...] += _mm_tn(ckv, dkv)
        cq = (zq * lax.rsqrt(jnp.mean(zq * zq, axis=-1, keepdims=True) + EPS) * qn).astype(BF16)
        dq = _rope(dq_ref[...].astype(F32), c, sa, sb, -1.0).astype(BF16)
        dzkv, gkvn = _rms_bwd(zkv, kvn, dckv)
        gkvn_ref[...] += gkvn
        gwuq_ref[...] += _mm_tn(cq, dq)
        dzq, gqn = _rms_bwd(zq, qn, _mm_nt(dq, wuq_ref[...]))
        gqn_ref[...] += gqn
        dkr = functools.reduce(lambda a, b: a + b, [dkr_ref[g] for g in range(dkr4.shape[0])])
        dfr_ref[:, :Q_RANK] = dzq.astype(BF16)
        dfr_ref[:, Q_RANK:Q_RANK + KV_RANK] = dzkv.astype(BF16)
        dfr_ref[:, Q_RANK + KV_RANK:] = _rope(dkr, c, sa, sb, -1.0).astype(BF16)

    ins = (dq_att, dkv_nat, dkr4, zfr, q_norm, wuq_pad, kv_norm, wukv, rc, rsa, rsb)
    in_specs = [_row_spec(tm, HW), _row_spec(tm, HW), pl.BlockSpec((dkr4.shape[0], tm, LANES), lambda i: (0, i, 0)), _row_spec(tm, FRONT_W),
                _full_spec(q_norm), _full_spec(wuq_pad), _full_spec(kv_norm), _full_spec(wukv),
                _row_spec(tm, LANES), _row_spec(tm, LANES), _row_spec(tm, LANES)]
    outs = [SDS((T, FRONT_W), BF16), SDS((Q_RANK, HW), F32), SDS((KV_RANK, HW), F32), SDS((1, Q_RANK), F32), SDS((1, KV_RANK), F32)]
    out_specs = [_row_spec(tm, FRONT_W)] + [_full_spec(s) for s in outs[1:]]
    return pl.pallas_call(
        body, name="mla_bwd", grid=(T // tm,), in_specs=in_specs, out_specs=out_specs, out_shape=outs,
        compiler_params=_cp(("arbitrary",)),
    )(*ins)


_DZ_COLS = ((GM, ZTOT), (GA, UP), (UP, GP), (GP, GM), (ZQ, GA))


def _in_proj_bwd_x(dzs, x2, dh, norm_in, w_in_pad, tm, slabs):
    T = x2.shape[0]
    steps = T // tm
    n = len(slabs)

    def body(d0, d1, d2, d3, d4, x_ref, dh_ref, nin_ref, win_ref, *rest):
        slab_refs, (gx_ref, gnin_ref), sum_refs = rest[:n], rest[n:n + 2], rest[n + 2:2 * n + 2]
        rs = _ReduceScatter(slab_refs, None, sum_refs, None, rest[2 * n + 2:])
        step = pl.program_id(0)

        @pl.when(step == 0)
        def _():
            gnin_ref[...] = jnp.zeros_like(gnin_ref)
            rs.start1()

        pl.when(step == min(2, steps - 1))(rs.finish1_start2)
        pl.when(step == min(steps * 11 // 16, steps - 1))(rs.relay2)
        dhn = None
        for ref, (lo, hi) in zip((d0, d1, d2, d3, d4), _DZ_COLS):
            t = _mm(ref[...], win_ref[lo:hi, :])
            dhn = t if dhn is None else dhn + t
        dx, gnin = _rms_bwd(x_ref[...], nin_ref[...], dhn)
        gnin_ref[...] += gnin
        gx_ref[...] = dx + dh_ref[...]
        pl.when(step == steps - 1)(rs.finish2)

    in_specs = [_row_spec(tm, hi - lo) for lo, hi in _DZ_COLS] + [_row_spec(tm, D_MODEL), _row_spec(tm, D_MODEL),
                                                                  _full_spec(norm_in), _full_spec(w_in_pad)] + [HBM_SPEC] * n
    sums = [SDS(s.shape[1:], F32) for s in slabs]
    outs = [SDS((T, D_MODEL), F32), SDS((1, D_MODEL), F32)] + sums
    return pl.pallas_call(
        body, name="in_proj_bwd_x", grid=(steps,), in_specs=in_specs,
        out_specs=[_row_spec(tm, D_MODEL), _full_spec(outs[1])] + [_full_spec(s) for s in sums],
        out_shape=outs, scratch_shapes=_rs_scratch([s.shape for s in sums], None), compiler_params=_cp(("arbitrary",)),
    )(*dzs, x2, dh, norm_in, w_in_pad, *slabs)


SLAB_ROWS = IN_TOTAL // N_DEV


def _slab_segments(k):
    cuts = [(0, ZKR_ORIG, 0), (ZKR_ORIG, ZKR_ORIG + ROPE, NOPE), (ZKR_ORIG + ROPE, IN_TOTAL, LANES - ROPE)]
    lo, hi = k * SLAB_ROWS, (k + 1) * SLAB_ROWS
    return [(max(lo, a) - lo, max(lo, a) + shift, min(hi, b) - max(lo, a)) for a, b, shift in cuts if min(hi, b) > max(lo, a)]


def _in_proj_bwd_w(dzs, hn, tm):
    T = hn.shape[0]
    steps = T // tm

    def body(d0, d1, d2, d3, d4, hn_ref, slab_ref, acc_ref):
        @pl.when(pl.program_id(0) == 0)
        def _():
            acc_ref[...] = jnp.zeros_like(acc_ref)

        hn_v = hn_ref[...]
        for ref, (lo, hi) in zip((d0, d1, d2, d3, d4), _DZ_COLS):
            acc_ref[lo:hi, :] += _mm_tn(ref[...], hn_v)

        @pl.when(pl.program_id(0) == steps - 1)
        def _():
            for k in range(N_DEV):
                for at, src, rows in _slab_segments(k):
                    slab_ref[k, at:at + rows, :] = acc_ref[src:src + rows, :].astype(BF16)

    in_specs = [_row_spec(tm, hi - lo) for lo, hi in _DZ_COLS] + [_row_spec(tm, D_MODEL)]
    out = SDS((N_DEV, SLAB_ROWS, D_MODEL), BF16)
    return pl.pallas_call(
        body, name="in_proj_bwd_w", grid=(steps,), in_specs=in_specs, out_specs=_full_spec(out), out_shape=out,
        scratch_shapes=[pltpu.VMEM((ZTOT, D_MODEL), F32)], compiler_params=_cp(("arbitrary",)),
    )(*dzs, hn)


def _local_step(x2, tgt, norm_in, w_in_pad, q_norm, w_uq, kv_norm, w_ukv, pool_w, pool_scale, late_shards, norm_final):
    T = x2.shape[0]
    tm, tm_small, tq = min(512, T), min(256, T), min(512, T)
    heads_fwd, heads_bwd = 4, 4
    row = lambda v: v.reshape(1, -1)
    wuq_pad = jnp.pad(w_uq, ((0, 0), (0, 0), (0, HEAD_PAD - NOPE - ROPE))).reshape(Q_RANK, HW)
    wukv = w_ukv.reshape(KV_RANK, HW)
    rc, rsa, rsb = _rope_tables(T)

    hn, zgm, zga, zup, zgp, zfr, q_att, k_att, v_att, vt_att, w_ba, w_bp, w_out = _in_proj(
        x2, row(norm_in), w_in_pad, row(q_norm), wuq_pad, row(kv_norm), wukv, rc, rsa, rsb, tm, late_shards)
    w_out = w_out.reshape(D_MODEL, D_MODEL)
    o, lse_rows = _attn_fwd(q_att, k_att, vt_att, tq, heads_fwd)
    ypool = _pool_fwd(zup, zgp, pool_w, row(pool_scale))
    loss8, dh, dgm, doop, dga, dcap_rows, dyp, *slabs, g_nf = _tail(
        x2, tgt, o, zga, ypool, zgm, w_ba, w_bp, w_out, row(norm_final), tm_small)
    dup, dgp, g_pool_w, g_pool_scale = _pool_bwd(zup, zgp, dyp, pool_w, row(pool_scale))

    bf = lambda a: a.astype(BF16)
    early = [g_pool_w, g_pool_scale, g_nf, loss8[0]]
    packed = jnp.concatenate([_pack_rows(a) for a in early], axis=0)
    dq_att, dkv_nat, dkr4, s_wout, s_wba, s_wbp, tot_early = _attn_bwd(
        q_att, k_att, v_att, doop, lse_rows, dcap_rows, tq, heads_bwd, slabs, packed)
    s_pool_w, s_pool_scale, s_nf, s_loss = _unpack_rows(tot_early, early)

    dfr, g_wuq_pad, g_wukv, g_qn, g_kvn = _mla_bwd(
        dq_att, dkv_nat, dkr4, zfr, row(q_norm), wuq_pad, row(kv_norm), wukv, rc, rsa, rsb, tm)
    dzs = (dgm, dga, dup, dgp, dfr)
    slabs = [_in_proj_bwd_w(dzs, hn, min(1024, T)),
             bf(g_wuq_pad.reshape(N_DEV, Q_RANK // N_DEV, HEADS, HEAD_PAD)[..., :NOPE + ROPE]),
             bf(g_wukv).reshape(N_DEV, KV_RANK // N_DEV, HW)]
    grad_x, g_nin, s_win, s_wuq, s_wukv = _in_proj_bwd_x(dzs, x2, dh, row(norm_in), w_in_pad, tm_small, slabs)
    late = [g_nin, g_qn, g_kvn]
    (tot_late,) = _reduce_scatter([], jnp.concatenate([_pack_rows(a) for a in late], axis=0))
    s_nin, s_qn, s_kvn = _unpack_rows(tot_late, late)

    grads = dict(norm_in=s_nin, w_in=s_win, q_norm=s_qn, w_uq=s_wuq, kv_norm=s_kvn, w_ukv=s_wukv, pool_w=s_pool_w.reshape(-1, GROUP),
                 pool_scale=s_pool_scale, w_branch_attn=s_wba, w_branch_pool=s_wbp, w_out=s_wout, norm_final=s_nf)
    return s_loss[0], grad_x, grads


MESH_ID = pl.DeviceIdType.MESH
VMEM_SPEC = pl.BlockSpec(memory_space=pltpu.VMEM)
HBM_SPEC = pl.BlockSpec(memory_space=pl.ANY)


def _mesh_pos():
    return lax.axis_index("x"), lax.axis_index("y"), lax.axis_index("c")


def _slot(px, py, pc):
    return 4 * px + 2 * py + pc


def _all_gather_bf16(shards):
    n = len(shards)

    def body(*refs):
        ins, outs = refs[:n], refs[n:2 * n]
        land0, scratch = refs[2 * n], refs[2 * n + 1:]
        wpad_ref = outs[0]
        ag = _AllGather(ins, (land0,) + tuple(outs[1:]), scratch)
        ag.start()
        ag.forward()
        ag.finish()
        wpad_ref[ZKR:GA, :] = jnp.zeros((GA - ZKR, D_MODEL), BF16)
        for k in range(N_DEV):
            for at, dst, rows in _slab_segments(k):
                wpad_ref[dst:dst + rows, :] = land0[k, at:at + rows, :]

    return pl.pallas_call(
        body, name="all_gather_weights",
        in_specs=[VMEM_SPEC] * n, out_specs=[VMEM_SPEC] + [HBM_SPEC] * (n - 1),
        out_shape=[SDS((ZTOT, D_MODEL), BF16)] + [SDS((N_DEV,) + s.shape, BF16) for s in shards[1:]],
        scratch_shapes=[pltpu.VMEM((N_DEV,) + shards[0].shape, BF16)] + _ag_scratch([s.shape for s in shards]),
        compiler_params=_cp(),
    )(*shards)


def _ag_scratch(shapes):
    n = len(shapes)
    dma = pltpu.SemaphoreType.DMA
    return [pltpu.VMEM(tuple(s), BF16) for s in shapes] + [dma((_AllGather.COPIES * n,)), dma((_AllGather.COPIES * n,)), dma((n,))]


class _AllGather:
    COPIES = 8

    def __init__(self, in_refs, dest_refs, scratch):
        n = self.n = len(in_refs)
        self.ins, self.dests, self.stage = in_refs, dest_refs, scratch[:n]
        self.send_sems, self.recv_sems, self.local_sems = scratch[n:]
        x, y, c = _mesh_pos()
        self.c, self.me, self.sibling = c, (x, y, c), (x, y, 1 - c)
        self.xn, self.yn, self.diag = (1 - x, y), (x, 1 - y), (1 - x, 1 - y)

    def _halves(self, a):
        rows = self.stage[a].shape[0]
        cut = (rows // 2 + 15) // 16 * 16
        return pl.ds(0, cut), pl.ds(cut, rows - cut)

    def _copy(self, a, k, block, to, from_stage=False, rows=None):
        dst = self.dests[a].at[_slot(*block)]
        src = self.stage[a] if from_stage else dst
        if rows is not None:
            src, dst = src.at[rows], dst.at[rows]
        return pltpu.make_async_remote_copy(
            src_ref=src, dst_ref=dst, send_sem=self.send_sems.at[self.COPIES * a + k],
            recv_sem=self.recv_sems.at[self.COPIES * a + k], device_id=to, device_id_type=MESH_ID)

    def _mine(self):
        return [pltpu.make_async_copy(self.stage[a], self.dests[a].at[_slot(*self.me)], self.local_sems.at[a]) for a in range(self.n)]

    def _first(self, a):
        return [self._copy(a, 0, self.me, self.sibling, True), self._copy(a, 1, self.me, (*self.xn, self.c), True),
                self._copy(a, 2, self.me, (*self.yn, self.c), True)]

    def _relays(self, a):
        lo, hi = self._halves(a)
        return [self._copy(a, 3, (*self.xn, self.c), (*self.yn, self.c), rows=lo),
                self._copy(a, 4, (*self.yn, self.c), (*self.xn, self.c), rows=hi)]

    def _passes(self, a):
        return [self._copy(a, 5 + j, (*chip, self.c), self.sibling) for j, chip in enumerate((self.xn, self.yn, self.diag))]

    def start(self):
        for a in range(self.n):
            self.stage[a][...] = self.ins[a][...].astype(BF16)
        for cp in self._mine():
            cp.start()
        for a in range(self.n):
            for cp in self._first(a):
                cp.start()

    def forward(self):
        for a in range(self.n):
            relays, passes = self._relays(a), self._passes(a)
            for j, chip in enumerate((self.xn, self.yn)):
                self._copy(a, 1 + j, (*chip, self.c), self.me).wait_recv()
                relays[j].start()
                passes[j].start()

    def finish(self):
        for a in range(self.n):
            lo, hi = self._halves(a)
            self._copy(a, 3, (*self.diag, self.c), self.me, rows=lo).wait_recv()
            self._copy(a, 4, (*self.diag, self.c), self.me, rows=hi).wait_recv()
            self._passes(a)[2].start()
        for a in range(self.n):
            self._copy(a, 0, self.sibling, self.me).wait_recv()
            for j, chip in enumerate((self.xn, self.yn, self.diag)):
                self._copy(a, 5 + j, (*chip, 1 - self.c), self.me).wait_recv()
            for cp in self._first(a) + self._relays(a) + self._passes(a):
                cp.wait_send()
        for cp in self._mine():
            cp.wait()


N_CHIPS = 4


def _reduce_scatter(slabs, packed):
    def body(*refs):
        n = len(slabs)
        rs = _ReduceScatter(refs[:n], refs[n], refs[n + 1:2 * n + 1], refs[2 * n + 1], refs[2 * n + 2:])
        rs.start1()
        rs.finish1_start2()
        rs.relay2()
        rs.finish2()

    shapes = [s.shape[1:] for s in slabs]
    return pl.pallas_call(
        body, name="reduce_scatter_grads",
        in_specs=[HBM_SPEC] * len(slabs) + [VMEM_SPEC], out_specs=[VMEM_SPEC] * (len(slabs) + 1),
        out_shape=[SDS(s, F32) for s in shapes] + [SDS(packed.shape, F32)],
        scratch_shapes=_rs_scratch(shapes, packed.shape), compiler_params=_cp(),
    )(*slabs, packed)


def _rs_scratch(shapes, packed_shape):
    n = len(shapes)
    n1, n2 = N_CHIPS * n + 1, _ReduceScatter.L2_COPIES * n + N_CHIPS - 1
    dma = pltpu.SemaphoreType.DMA
    packed = [] if packed_shape is None else [pltpu.VMEM(packed_shape, F32), pltpu.VMEM((N_CHIPS,) + tuple(packed_shape), F32)]
    return ([pltpu.VMEM((N_CHIPS,) + tuple(s), BF16) for s in shapes] * 2 + [pltpu.VMEM((N_CHIPS - 1,) + tuple(s), BF16) for s in shapes] * 2
            + packed + [dma((max(N_CHIPS * n, 1),)), dma((n1,)), dma((n1,)), dma((n2,)), dma((n2,))])


class _ReduceScatter:
    L2_COPIES = 6

    def __init__(self, slab_refs, packed_ref, out_refs, ptot_ref, scratch):
        n = self.n = len(slab_refs)
        self.slabs, self.packed, self.outs, self.ptot = slab_refs, packed_ref, out_refs, ptot_ref
        self.own1, self.land1, self.send2, self.land2 = (scratch[k * n:(k + 1) * n] for k in range(4))
        rest = scratch[4 * n:]
        if packed_ref is not None:
            self.pland1, self.pland2 = rest[:2]
            rest = rest[2:]
        self.loc_sems, self.send1_sems, self.recv1_sems, self.send2_sems, self.recv2_sems = rest
        self.x, self.y, self.c = _mesh_pos()

    def _chip(self, r):
        return (1 - self.x if r & 2 else self.x, 1 - self.y if r & 1 else self.y)

    @staticmethod
    def _remote(src, dst, send_sem, recv_sem, to):
        return pltpu.make_async_remote_copy(src_ref=src, dst_ref=dst, send_sem=send_sem, recv_sem=recv_sem, device_id=to,
                                            device_id_type=MESH_ID)

    def _copies1(self):
        c, sibling = self.c, (self.x, self.y, 1 - self.c)
        cps = []
        for a in range(self.n):
            for r in range(N_CHIPS):
                k = N_CHIPS * a + r
                cps.append(pltpu.make_async_copy(self.slabs[a].at[_slot(*self._chip(r), c)], self.own1[a].at[r], self.loc_sems.at[k]))
                cps.append(self._remote(self.slabs[a].at[_slot(*self._chip(r), 1 - c)], self.land1[a].at[r],
                                        self.send1_sems.at[k], self.recv1_sems.at[k], sibling))
        if self.packed is not None:
            k = N_CHIPS * self.n
            cps.append(self._remote(self.packed, self.pland1, self.send1_sems.at[k], self.recv1_sems.at[k], sibling))
        return cps

    def _halves(self, a):
        rows = self.send2[a].shape[1]
        cut = (rows // 2 + 15) // 16 * 16
        return pl.ds(0, cut), pl.ds(cut, rows - cut)

    def _copies2(self, a):
        lo, hi = self._halves(a)
        s2, l2 = self.send2[a], self.land2[a]
        xn, yn = (*self._chip(2), self.c), (*self._chip(1), self.c)
        plan = [(1, lo, xn), (2, lo, xn), (0, hi, yn), (2, hi, yn), (0, lo, yn), (1, hi, xn)]
        return [self._remote(s2.at[slot].at[rows], l2.at[slot].at[rows], self.send2_sems.at[self.L2_COPIES * a + k],
                             self.recv2_sems.at[self.L2_COPIES * a + k], to) for k, (slot, rows, to) in enumerate(plan)]

    def _copies2_packed(self):
        base = self.L2_COPIES * self.n - 1
        return [self._remote(self.pland2.at[0], self.pland2.at[r], self.send2_sems.at[base + r], self.recv2_sems.at[base + r],
                             (*self._chip(r), self.c)) for r in range(1, N_CHIPS)]

    def start1(self):
        for cp in self._copies1():
            cp.start()

    def finish1_start2(self):
        for cp in self._copies1():
            cp.wait()
        for a in range(self.n):
            self.outs[a][...] = self.own1[a][0].astype(F32) + self.land1[a][0].astype(F32)
            for r in range(1, N_CHIPS):
                self.send2[a][r - 1] = (self.own1[a][r].astype(F32) + self.land1[a][r].astype(F32)).astype(BF16)
            for cp in self._copies2(a)[:4]:
                cp.start()
        if self.packed is not None:
            self.pland2[0] = self.packed[...] + self.pland1[...]
            for cp in self._copies2_packed():
                cp.start()

    def relay2(self):
        for a in range(self.n):
            lo, hi = self._halves(a)
            cps, s2, l2 = self._copies2(a), self.send2[a], self.land2[a]
            cps[1].wait_recv()
            s2[0, lo] = (s2[0, lo].astype(F32) + l2[2, lo].astype(F32)).astype(BF16)
            cps[4].start()
            cps[3].wait_recv()
            s2[1, hi] = (s2[1, hi].astype(F32) + l2[2, hi].astype(F32)).astype(BF16)
            cps[5].start()

    def finish2(self):
        for a in range(self.n):
            cps = self._copies2(a)
            for k in (0, 2, 4, 5):
                cps[k].wait_recv()
            for cp in cps:
                cp.wait_send()
            l2 = self.land2[a]
            self.outs[a][...] = self.outs[a][...] + (l2[0].astype(F32) + l2[1].astype(F32))
        if self.packed is not None:
            for cp in self._copies2_packed():
                cp.wait()
            p2 = self.pland2
            self.ptot[...] = (p2[0] + p2[1]) + (p2[2] + p2[3])


def _adamw(ws, gs, ms, vs):
    n = len(ws)

    def body(*refs):
        for k in range(n):
            w, g, m, v = (refs[j * n + k][...] for j in range(4))
            d_ref, nm_ref, nv_ref = (refs[(4 + j) * n + k] for j in range(3))
            m = ADAM_B1 * m + (1.0 - ADAM_B1) * g
            v = ADAM_B2 * v + (1.0 - ADAM_B2) * jnp.square(g)
            m_hat = m / (1.0 - ADAM_B1 ** ADAM_STEP)
            v_hat = v / (1.0 - ADAM_B2 ** ADAM_STEP)
            d_ref[...] = -ADAM_LR * (m_hat / (jnp.sqrt(v_hat) + ADAM_EPS) + ADAM_WD * w)
            nm_ref[...] = m
            nv_ref[...] = v

    outs = pl.pallas_call(
        body, name="adamw", in_specs=[VMEM_SPEC] * (4 * n), out_specs=[VMEM_SPEC] * (3 * n),
        out_shape=[SDS(w.shape, F32) for w in ws] * 3, compiler_params=_cp(),
    )(*ws, *gs, *ms, *vs)
    return outs[:n], outs[n:2 * n], outs[2 * n:]


WEIGHTS = ("norm_in", "w_in", "q_norm", "w_uq", "kv_norm", "w_ukv", "pool_w", "pool_scale", "w_branch_attn", "w_branch_pool",
           "w_out", "norm_final")
SUBLANES = 8


def _pack_rows(a):
    a = a.reshape(-1, LANES)
    return jnp.pad(a, ((0, -a.shape[0] % SUBLANES), (0, 0)))


def _unpack_rows(packed, like):
    out, row = [], 0
    for a in like:
        rows = a.size // LANES
        out.append(packed[row:row + rows].reshape(a.shape))
        row += rows + (-rows % SUBLANES)
    return out


def kernel(x, norm_in, w_in, q_norm, w_uq, kv_norm, w_ukv, pool_w, pool_scale, w_branch_attn, w_branch_pool, w_out, norm_final, loss_target, m_norm_in, m_w_in, m_q_norm, m_w_uq, m_kv_norm, m_w_ukv, m_pool_w, m_pool_scale, m_w_branch_attn, m_w_branch_pool, m_w_out, m_norm_final, v_norm_in, v_w_in, v_q_norm, v_w_uq, v_kv_norm, v_w_ukv, v_pool_w, v_pool_scale, v_w_branch_attn, v_w_branch_pool, v_w_out, v_norm_final):
    w = dict(norm_in=norm_in, w_in=w_in, q_norm=q_norm, w_uq=w_uq, kv_norm=kv_norm, w_ukv=w_ukv, pool_w=pool_w, pool_scale=pool_scale,
             w_branch_attn=w_branch_attn, w_branch_pool=w_branch_pool, w_out=w_out, norm_final=norm_final)
    m = dict(norm_in=m_norm_in, w_in=m_w_in, q_norm=m_q_norm, w_uq=m_w_uq, kv_norm=m_kv_norm, w_ukv=m_w_ukv, pool_w=m_pool_w,
             pool_scale=m_pool_scale, w_branch_attn=m_w_branch_attn, w_branch_pool=m_w_branch_pool, w_out=m_w_out, norm_final=m_norm_final)
    v = dict(norm_in=v_norm_in, w_in=v_w_in, q_norm=v_q_norm, w_uq=v_w_uq, kv_norm=v_kv_norm, w_ukv=v_w_ukv, pool_w=v_pool_w,
             pool_scale=v_pool_scale, w_branch_attn=v_w_branch_attn, w_branch_pool=v_w_branch_pool, w_out=v_w_out, norm_final=v_norm_final)

    def as2d(name, a):
        if name == "w_in":
            return a.T
        if name == "w_uq":
            return a
        if name == "w_ukv":
            return a.reshape(a.shape[0], -1)
        if name == "pool_w":
            return a.reshape(-1, GROUP)
        return a.reshape(1, -1) if a.ndim == 1 else a

    def unshape(name, a):
        return a.T if name == "w_in" else a.reshape(w[name].shape)

    w_in_pad, w_uq_full, w_ukv_full = _all_gather_bf16([as2d(k, w[k]) for k in ("w_in", "w_uq", "w_ukv")])
    loss, grad_x, g2d = _local_step(
        x.reshape(x.shape[1:]), loss_target.reshape(x.shape[1:]), norm_in, w_in_pad, q_norm,
        w_uq_full.reshape(Q_RANK, HEADS, NOPE + ROPE), kv_norm, w_ukv_full.reshape(KV_RANK, HEADS, NOPE + VDIM),
        pool_w, pool_scale, [w_branch_attn, w_branch_pool, w_out], norm_final)

    deltas, new_m, new_v = _adamw([as2d(k, w[k]) for k in WEIGHTS], [g2d[k] for k in WEIGHTS],
                                  [as2d(k, m[k]) for k in WEIGHTS], [as2d(k, v[k]) for k in WEIGHTS])
    shaped = lambda arrs: [unshape(k, a) for k, a in zip(WEIGHTS, arrs)]
    return (loss, grad_x.reshape(x.shape), *shaped([g2d[k] for k in WEIGHTS]), *shaped(deltas), *shaped(new_m), *shaped(new_v))
```

```python
import functools

import jax
import jax.numpy as jnp
import numpy as np
from jax import lax
from jax.experimental import pallas as pl
from jax.experimental.pallas import tpu as pltpu

F32 = jnp.float32
BF16 = jnp.bfloat16
SDS = jax.ShapeDtypeStruct

D_MODEL = 1024
HEADS = 8
NOPE = 64
ROPE = 32
VDIM = 64
Q_RANK = 384
KV_RANK = 256
MLA_W = HEADS * VDIM
POOL_W = 512
POOL_GROUPS = 4
GROUP = POOL_W // POOL_GROUPS
CHUNK = 64
ROPE_THETA = 10000.0
EPS = 1e-6
SCALE = (NOPE + ROPE) ** -0.5
LOG2E = 1.4426950408889634
LN2 = 0.6931471805599453
QK_SCALE_LOG2 = SCALE * LOG2E
IN_TOTAL = 4256
ADAM_LR, ADAM_B1, ADAM_B2, ADAM_EPS, ADAM_WD, ADAM_STEP = 0.001, 0.9, 0.999, 1e-08, 0.01, 10

N_DEV = 8
LANES = 128
HEAD_PAD = LANES
HW = HEADS * HEAD_PAD

ZQ, ZKV, ZKR, GA, UP, GP, GM, ZTOT = 0, 384, 640, 768, 1280, 1792, 2304, 4352
FRONT_W = GA
ZKR_ORIG = 640

VMEM_LIMIT = 62 * 1024 * 1024


def _cp(sem=None, **kw):
    if sem is not None:
        kw["dimension_semantics"] = sem
    return pltpu.CompilerParams(vmem_limit_bytes=VMEM_LIMIT, **kw)


def _mm(a, b):
    return lax.dot_general(a, b, (((1,), (0,)), ((), ())), preferred_element_type=F32)


def _mm_nt(a, b):
    return lax.dot_general(a, b, (((1,), (1,)), ((), ())), preferred_element_type=F32)


def _mm_tn(a, b):
    return lax.dot_general(a, b, (((0,), (0,)), ((), ())), preferred_element_type=F32)


def _row_spec(tm, w):
    return pl.BlockSpec((tm, w), lambda i: (i, 0))


def _full_spec(a):
    nd = len(a.shape)
    return pl.BlockSpec(a.shape, lambda *_: (0,) * nd)


def _rope(v, c, sa, sb, sign):
    n = v.shape[-1]
    reps = n // LANES
    if reps > 1:
        c, sa, sb = (jnp.tile(t, (1, reps)) for t in (c, sa, sb))
    up = pltpu.roll(v, n - ROPE // 2, 1)
    dn = pltpu.roll(v, ROPE // 2, 1)
    return v * c + sign * (up * sa + dn * sb)


def _rope_tables(T):
    half = ROPE // 2
    inv_freq = np.float32(ROPE_THETA) ** (-np.arange(half, dtype=np.float32) / np.float32(half))
    ang = np.arange(T, dtype=np.float32)[:, None] * inv_freq[None, :].astype(np.float32)
    cos, sin = np.cos(ang.astype(np.float64)).astype(np.float32), np.sin(ang.astype(np.float64)).astype(np.float32)
    z16 = np.zeros((T, half), np.float32)
    z32 = np.zeros((T, LANES - NOPE - ROPE), np.float32)
    c = np.concatenate([np.ones((T, NOPE), np.float32), cos, cos, z32], axis=1)
    sa = np.concatenate([np.zeros((T, NOPE), np.float32), -sin, z16, z32], axis=1)
    sb = np.concatenate([np.zeros((T, NOPE), np.float32), z16, sin, z32], axis=1)
    return jnp.asarray(c), jnp.asarray(sa), jnp.asarray(sb)


def _silu_parts(g):
    sg = jax.nn.sigmoid(g)
    return g * sg, sg + g * sg * (1.0 - sg)


def _in_proj(x2, norm_in, w_in_pad, q_norm, wuq_pad, kv_norm, wukv, rc, rsa, rsb, tm, late_shards):
    T = x2.shape[0]
    steps = T // tm
    n = len(late_shards)

    def body(x_ref, nin_ref, win_ref, qn_ref, wuq_ref, kvn_ref, wukv_ref, c_ref, sa_ref, sb_ref, *rest):
        hn_ref, zgm_ref, zga_ref, zup_ref, zgp_ref, zfr_ref, q_ref, k_ref, v_ref, vt_ref = rest[n:n + 10]
        ag = _AllGather(rest[:n], rest[n + 10:2 * n + 10], rest[2 * n + 10:])
        step = pl.program_id(0)
        pl.when(step == 0)(ag.start)
        pl.when(step == min(3, steps - 1))(ag.forward)
        xf = x_ref[...]
        r = lax.rsqrt(jnp.mean(xf * xf, axis=-1, keepdims=True) + EPS)
        hn = (xf * r * nin_ref[...]).astype(BF16)
        hn_ref[...] = hn
        zfr = _mm_nt(hn, win_ref[ZQ:GA, :])
        zfr_ref[...] = zfr
        zq, zkv, zkr = zfr[:, ZQ:ZKV], zfr[:, ZKV:ZKR], zfr[:, ZKR:GA]
        c, sa, sb = c_ref[...], sa_ref[...], sb_ref[...]
        rq = lax.rsqrt(jnp.mean(zq * zq, axis=-1, keepdims=True) + EPS)
        cq = (zq * rq * qn_ref[...]).astype(BF16)
        rkv = lax.rsqrt(jnp.mean(zkv * zkv, axis=-1, keepdims=True) + EPS)
        ckv = (zkv * rkv * kvn_ref[...]).astype(BF16)
        zga_ref[...] = _mm_nt(hn, win_ref[GA:UP, :])
        zup_ref[...] = _mm_nt(hn, win_ref[UP:GP, :])
        zgp_ref[...] = _mm_nt(hn, win_ref[GP:GM, :])
        q_raw = _mm(cq, wuq_ref[...])
        kv = _mm(ckv, wukv_ref[...])
        zgm_ref[...] = _mm_nt(hn, win_ref[GM:ZTOT, :])
        q = _rope(q_raw, c, sa, sb, 1.0)
        q_ref[...] = (q * QK_SCALE_LOG2).astype(BF16)
        kr = _rope(zkr, c, sa, sb, 1.0)
        lane = lax.broadcasted_iota(jnp.int32, kv.shape, 1) % LANES
        k_ref[...] = jnp.where(lane < NOPE, kv, jnp.tile(kr, (1, HEADS))).astype(BF16)
        v = jnp.where(lane < NOPE, 1.0, kv).astype(BF16)
        v_ref[...] = v
        vt_ref[...] = v.T
        pl.when(step == steps - 1)(ag.finish)

    ins = (x2, norm_in, w_in_pad, q_norm, wuq_pad, kv_norm, wukv, rc, rsa, rsb)
    in_specs = [_row_spec(tm, D_MODEL), _full_spec(norm_in), _full_spec(w_in_pad), _full_spec(q_norm), _full_spec(wuq_pad),
                _full_spec(kv_norm), _full_spec(wukv), _row_spec(tm, LANES), _row_spec(tm, LANES), _row_spec(tm, LANES)]
    widths = [(D_MODEL, BF16), (ZTOT - GM, F32), (UP - GA, F32), (GP - UP, F32), (GM - GP, F32), (FRONT_W, F32),
              (HW, BF16), (HW, BF16), (HW, BF16)]
    return pl.pallas_call(
        body, name="in_proj", grid=(steps,), in_specs=in_specs + [_full_spec(s) for s in late_shards],
        out_specs=[_row_spec(tm, w) for w, _ in widths] + [pl.BlockSpec((HW, tm), lambda i: (0, i))] + [HBM_SPEC] * n,
        out_shape=[SDS((T, w), dt) for w, dt in widths] + [SDS((HW, T), BF16)]
        + [SDS((N_DEV,) + s.shape, BF16) for s in late_shards],
        scratch_shapes=_ag_scratch([s.shape for s in late_shards]), compiler_params=_cp(("arbitrary",)),
    )(*ins, *late_shards)


def _chunk_mask(n_q, n_k, q_off, transposed):
    shape = (n_k, n_q) if transposed else (n_q, n_k)
    q = (lax.broadcasted_iota(jnp.int32, shape, 1 if transposed else 0) + q_off) // CHUNK
    k = lax.broadcasted_iota(jnp.int32, shape, 0 if transposed else 1) // CHUNK
    return k <= q


def _store_pair_rows(ref, k, pair):
    t = pair.T
    ref[k, 0:1, :] = t[0:1, :]
    ref[k, 1:2, :] = t[VDIM:VDIM + 1, :]


def _attn_fwd(q_att, k_att, vt_att, tq, hps):
    T = q_att.shape[0]
    head_lanes = [slice(h * LANES, (h + 1) * LANES) for h in range(hps)]

    def body(q_ref, k_ref, vt_ref, o_ref, lser_ref):
        i = pl.program_id(1)
        mask = _chunk_mask(tq, tq, 0, True)
        lane = lax.broadcasted_iota(jnp.int32, (tq, LANES), 1)
        qs = [q_ref[:, hs] for hs in head_lanes]

        def step(j, carry, masked):
            off = pl.multiple_of(j * tq, tq)
            sts = [_mm_nt(k_ref[pl.ds(off, tq), hs], qh) for qh, hs in zip(qs, head_lanes)]
            if masked:
                sts = [jnp.where(mask, st, -jnp.inf) for st in sts]
            ms = [jnp.maximum(m, jnp.max(st, axis=0, keepdims=True)) for (m, _), st in zip(carry, sts)]
            pts = [jnp.exp2(st - m_new).astype(BF16) for st, m_new in zip(sts, ms)]
            return tuple((m_new, jnp.exp2(m - m_new) * acc + _mm(vt_ref[hs, pl.ds(off, tq)], pt))
                         for (m, acc), m_new, pt, hs in zip(carry, ms, pts, head_lanes))

        init = ((jnp.full((1, tq), -jnp.inf, F32), jnp.zeros((LANES, tq), F32)),) * hps
        res = step(i, lax.fori_loop(0, i, functools.partial(step, masked=False), init), True)
        for pair in range(hps // 2):
            (ma, acca), (mb, accb) = res[2 * pair], res[2 * pair + 1]
            la, lb = acca[:1], accb[:1]
            oa, ob = (acca / la).T, (accb / lb).T
            o_ref[:, pair * LANES:(pair + 1) * LANES] = jnp.where(lane < VDIM, pltpu.roll(oa, VDIM, 1), ob)
            lser_ref[pair, 0:1, :] = ma + jnp.log2(la)
            lser_ref[pair, 1:2, :] = mb + jnp.log2(lb)

    qspec = pl.BlockSpec((tq, hps * LANES), lambda p, i: (i, p))
    kspec = pl.BlockSpec((T, hps * LANES), lambda p, i: (0, p))
    vspec = pl.BlockSpec((hps * LANES, T), lambda p, i: (p, 0))
    ospec = pl.BlockSpec((tq, hps * VDIM), lambda p, i: (i, p))
    return pl.pallas_call(
        body, name="attn_fwd", grid=(HEADS // hps, T // tq), in_specs=[qspec, kspec, vspec],
        out_specs=[ospec, pl.BlockSpec((hps // 2, 2, tq), lambda p, i: (p, 0, i))],
        out_shape=[SDS((T, MLA_W), F32), SDS((HEADS // 2, 2, T), F32)],
        compiler_params=_cp(("parallel", "parallel")),
    )(q_att, k_att, vt_att)


def _pick(g, vals):
    out = vals[-1]
    for k in range(len(vals) - 2, -1, -1):
        out = jnp.where(g == k, vals[k], out)
    return out


def _window_sum(u, g, forward):
    T = u.shape[0]
    row = lax.broadcasted_iota(jnp.int32, u.shape, 0)

    def sh(s, k):
        if forward:
            return jnp.where(row >= k, pltpu.roll(s, k, 0), 0.0)
        return jnp.where(row < T - k, pltpu.roll(s, T - k, 0), 0.0)

    sums, s = [], u
    for k in (1, 2, 4, 8):
        s = s + sh(s, k)
        sums.append(s)
    return _pick(g, sums)


MAX_WINDOW = 16


def _pool_inv_count(shape, g):
    T, n = shape
    row = lax.broadcasted_iota(jnp.int32, (MAX_WINDOW, n), 0)
    head = 1.0 / jnp.minimum(row + 1, lax.shift_left(jnp.int32(2), g)).astype(F32)
    inv_w = _pick(g, [0.5, 0.25, 0.125, 0.0625])
    return jnp.concatenate([head, jnp.broadcast_to(inv_w, (T - MAX_WINDOW, n)).astype(F32)], axis=0)


def _pool_fwd(zup, zgp, pool_w, pool_scale):
    T = zup.shape[0]

    def body(u_ref, g_ref, w_ref, sc_ref, y_ref):
        g = pl.program_id(0)
        u = u_ref[...]
        d = _window_sum(u, g, True) * _pool_inv_count(u.shape, g) - u
        lin = _mm(d.astype(BF16), w_ref[0].astype(BF16))
        silu, _ = _silu_parts(g_ref[...])
        y_ref[...] = (lin * sc_ref[...] * silu).astype(BF16)

    col = pl.BlockSpec((T, GROUP), lambda g: (0, g))
    return pl.pallas_call(
        body, name="pool_fwd", grid=(POOL_GROUPS,),
        in_specs=[col, col, pl.BlockSpec((1, GROUP, GROUP), lambda g: (g, 0, 0)), pl.BlockSpec((1, GROUP), lambda g: (0, g))],
        out_specs=col, out_shape=SDS((T, POOL_W), BF16), compiler_params=_cp(("parallel",)),
    )(zup, zgp, pool_w, pool_scale)


def _pool_bwd(zup, zgp, dyp, pool_w, pool_scale):
    T = zup.shape[0]

    def body(u_ref, g_ref, dy_ref, w_ref, sc_ref, du_ref, dg_ref, gw_ref, gsc_ref):
        g = pl.program_id(0)
        u = u_ref[...]
        inv = _pool_inv_count(u.shape, g)
        d = (_window_sum(u, g, True) * inv - u).astype(BF16)
        wb = w_ref[0].astype(BF16)
        lin = _mm(d, wb)
        sc = sc_ref[...]
        silu, dsilu = _silu_parts(g_ref[...])
        dy = dy_ref[...]
        dg_ref[...] = (dy * lin * sc * dsilu).astype(BF16)
        dpre = dy * silu
        gsc_ref[...] = jnp.sum(dpre * lin, axis=0, keepdims=True)
        dlin = (dpre * sc).astype(BF16)
        gw_ref[0] = _mm_tn(d, dlin)
        dd = _mm_nt(dlin, wb)
        du_ref[...] = (_window_sum(dd * inv, g, False) - dd).astype(BF16)

    col = pl.BlockSpec((T, GROUP), lambda g: (0, g))
    wspec = pl.BlockSpec((1, GROUP, GROUP), lambda g: (g, 0, 0))
    vspec = pl.BlockSpec((1, GROUP), lambda g: (0, g))
    return pl.pallas_call(
        body, name="pool_bwd", grid=(POOL_GROUPS,), in_specs=[col, col, col, wspec, vspec], out_specs=[col, col, wspec, vspec],
        out_shape=[SDS((T, POOL_W), BF16), SDS((T, POOL_W), BF16), SDS((POOL_GROUPS, GROUP, GROUP), F32), SDS((1, POOL_W), F32)],
        compiler_params=_cp(("parallel",)),
    )(zup, zgp, dyp, pool_w, pool_scale)


def _tail(x2, tgt, o, zga, ypool, zgm, wba, wbp, wout, norm_final, tm):
    T = x2.shape[0]
    steps = T // tm
    cols = D_MODEL // N_DEV

    def body(x_ref, tgt_ref, o_ref, zga_ref, yp_ref, zgm_ref, wba_ref, wbp_ref, wout_ref, nf_ref,
             loss_ref, dh_ref, dgm_ref, doop_ref, dga_ref, dcapr_ref, dyp_ref, swout_ref, swba_ref, swbp_ref, gnf_ref,
             gwout_ref, gwba_ref, gwbp_ref):
        @pl.when(pl.program_id(0) == 0)
        def _():
            for ref in (loss_ref, gwout_ref, gwba_ref, gwbp_ref, gnf_ref):
                ref[...] = jnp.zeros_like(ref)

        o_v = o_ref[...]
        silu, dsilu = _silu_parts(zga_ref[...])
        ya = (o_v * silu).astype(BF16)
        yp = yp_ref[...]
        wba_v = jnp.concatenate([wba_ref[k] for k in range(N_DEV)], axis=1)
        wbp_v = jnp.concatenate([wbp_ref[k] for k in range(N_DEV)], axis=1)
        wout_v = wout_ref[...]
        a = _mm(ya, wba_v)
        p = _mm(yp, wbp_v)
        gate = jax.nn.sigmoid(zgm_ref[...])
        ga, gp = gate[:, :D_MODEL], gate[:, D_MODEL:]
        mg = (ga * a + gp * p).astype(BF16)
        h = x_ref[...] + _mm(mg, wout_v)
        r = lax.rsqrt(jnp.mean(h * h, axis=-1, keepdims=True) + EPS)
        gf = nf_ref[...]
        hr = h * r
        e = hr * gf - tgt_ref[...]
        loss_ref[...] += (0.5 / D_MODEL) * jnp.sum(e * e)
        dy = e * (1.0 / D_MODEL)
        gnf_ref[...] += jnp.sum(dy * hr, axis=0, keepdims=True)
        u = dy * gf
        dh = r * (u - hr * jnp.mean(u * hr, axis=-1, keepdims=True))
        dh_ref[...] = dh
        dhb = dh.astype(BF16)
        dmg = _mm_nt(dhb, wout_v)
        dab = (dmg * ga).astype(BF16)
        dpb = (dmg * gp).astype(BF16)
        dya = _mm_nt(dab, wba_v)
        dyp_ref[...] = _mm_nt(dpb, wbp_v)
        gwout_ref[...] += _mm_tn(mg, dhb)
        gwba_ref[...] += _mm_tn(ya, dab)
        gwbp_ref[...] += _mm_tn(yp, dpb)
        dgm_ref[:, :D_MODEL] = (dmg * a * ga * (1.0 - ga)).astype(BF16)
        dgm_ref[:, D_MODEL:] = (dmg * p * gp * (1.0 - gp)).astype(BF16)
        do = dya * silu
        dga_ref[...] = (dya * o_v * dsilu).astype(BF16)
        prod = do * o_v
        lo = lax.broadcasted_iota(jnp.int32, (tm, LANES), 1) < VDIM
        for pair in range(HEADS // 2):
            ls = slice(pair * LANES, (pair + 1) * LANES)
            do_p, prod_p = do[:, ls], prod[:, ls]
            dcap_a = jnp.sum(jnp.where(lo, prod_p, 0.0), axis=-1, keepdims=True)
            dcap_b = jnp.sum(jnp.where(lo, 0.0, prod_p), axis=-1, keepdims=True)
            _store_pair_rows(dcapr_ref, pair, jnp.where(lo, dcap_a, dcap_b))
            doop_ref[:, 2 * pair * LANES:(2 * pair + 1) * LANES] = jnp.where(lo, 0.0, pltpu.roll(do_p, VDIM, 1)).astype(BF16)
            doop_ref[:, (2 * pair + 1) * LANES:(2 * pair + 2) * LANES] = jnp.where(lo, 0.0, do_p).astype(BF16)

        @pl.when(pl.program_id(0) == steps - 1)
        def _():
            for k in range(N_DEV):
                swout_ref[k] = gwout_ref[k * cols:(k + 1) * cols, :].astype(BF16)
                swba_ref[k] = gwba_ref[:, k * cols:(k + 1) * cols].astype(BF16)
                swbp_ref[k] = gwbp_ref[:, k * cols:(k + 1) * cols].astype(BF16)

    ins = (x2, tgt, o, zga, ypool, zgm, wba, wbp, wout, norm_final)
    in_specs = [_row_spec(tm, D_MODEL), _row_spec(tm, D_MODEL), _row_spec(tm, MLA_W), _row_spec(tm, MLA_W), _row_spec(tm, POOL_W),
                _row_spec(tm, 2 * D_MODEL), _full_spec(wba), _full_spec(wbp), _full_spec(wout), _full_spec(norm_final)]
    outs = [SDS((8, LANES), F32), SDS((T, D_MODEL), F32), SDS((T, 2 * D_MODEL), BF16), SDS((T, HW), BF16), SDS((T, MLA_W), BF16),
            SDS((HEADS // 2, 2, T), F32), SDS((T, POOL_W), F32),
            SDS((N_DEV, cols, D_MODEL), BF16), SDS((N_DEV, MLA_W, cols), BF16), SDS((N_DEV, POOL_W, cols), BF16), SDS((1, D_MODEL), F32)]
    out_specs = [_full_spec(outs[0]), _row_spec(tm, D_MODEL), _row_spec(tm, 2 * D_MODEL), _row_spec(tm, HW), _row_spec(tm, MLA_W),
                 pl.BlockSpec((HEADS // 2, 2, tm), lambda i: (0, 0, i)), _row_spec(tm, POOL_W),
                 _full_spec(outs[7]), _full_spec(outs[8]), _full_spec(outs[9]), _full_spec(outs[10])]
    return pl.pallas_call(
        body, name="tail", grid=(steps,), in_specs=in_specs, out_specs=out_specs, out_shape=outs,
        scratch_shapes=[pltpu.VMEM((D_MODEL, D_MODEL), F32), pltpu.VMEM((MLA_W, D_MODEL), F32), pltpu.VMEM((POOL_W, D_MODEL), F32)],
        compiler_params=_cp(("arbitrary",)),
    )(*ins)


def _attn_bwd(q_att, k_att, v_att, doop, lse_rows, dcap_rows, tq, hps, slabs, packed):
    T = q_att.shape[0]
    nq = T // tq
    n = len(slabs)
    groups = HEADS // hps
    head_lanes = [slice(h * LANES, (h + 1) * LANES) for h in range(hps)]

    def body(q_ref, k_ref, v_ref, doop_ref, lse_ref, dcap_ref, *rest):
        slab_refs, packed_ref = rest[:n], rest[n]
        dq_ref, dkv_ref, dkr_ref = rest[n + 1:n + 4]
        sum_refs, ptot_ref = rest[n + 4:2 * n + 4], rest[2 * n + 4]
        dq_acc = rest[2 * n + 5]
        rs = _ReduceScatter(slab_refs, packed_ref, sum_refs, ptot_ref, rest[2 * n + 6:])
        group, j = pl.program_id(0), pl.program_id(1)
        tick, last = group * nq + j, groups * nq - 1
        pl.when(tick == 0)(rs.start1)
        pl.when(tick == min(1, last))(rs.finish1_start2)
        pl.when(tick == min(3, last))(rs.relay2)
        mask = _chunk_mask(tq, tq, 0, True)
        lane = lax.broadcasted_iota(jnp.int32, (tq, LANES), 1)
        ks = [k_ref[:, hs] for hs in head_lanes]
        vs = [v_ref[:, hs] for hs in head_lanes]
        kts = [kh.T for kh in ks]

        @pl.when(j == 0)
        def _():
            dq_acc[...] = jnp.zeros_like(dq_acc)

        def step(i, carry, masked):
            rows = pl.ds(pl.multiple_of(i * tq, tq), tq)
            heads = range(hps)
            stat = lambda h: (h // 2, slice(h % 2, h % 2 + 1), rows)
            qhs = [q_ref[rows, hs] for hs in head_lanes]
            doops = [doop_ref[rows, hs] for hs in head_lanes]
            sts = [_mm_nt(ks[h], qhs[h]) for h in heads]
            dpts = [_mm_nt(vs[h], doops[h]) for h in heads]
            pts = [jnp.exp2(sts[h] - lse_ref[stat(h)]) for h in heads]
            if masked:
                pts = [jnp.where(mask, pt, 0.0) for pt in pts]
            dsts = [(pts[h] * (dpts[h] - dcap_ref[stat(h)])).astype(BF16) for h in heads]
            dvs = [_mm(pts[h].astype(BF16), doops[h]) for h in heads]
            dks = [_mm(dsts[h], qhs[h]) for h in heads]
            for h, hs in enumerate(head_lanes):
                dq_acc[hs, rows] += _mm(kts[h], dsts[h])
            return tuple((dk + dks[h], dv + dvs[h]) for h, (dk, dv) in enumerate(carry))

        zero = jnp.zeros((tq, LANES), F32)
        carry = step(j, ((zero, zero),) * hps, True)
        res = lax.fori_loop(j + 1, nq, functools.partial(step, masked=False), carry)
        dkr = None
        for (dk, dv), hs in zip(res, head_lanes):
            dk = dk * LN2
            dkv_ref[:, hs] = jnp.where(lane < NOPE, dk, dv).astype(BF16)
            dkr = dk if dkr is None else dkr + dk
        dkr_ref[0] = jnp.where((lane >= NOPE) & (lane < NOPE + ROPE), dkr, 0.0)

        @pl.when(j == nq - 1)
        def _():
            dq_ref[...] = (dq_acc[...] * SCALE).T.astype(BF16)

        pl.when(tick == last)(rs.finish2)

    kspec = pl.BlockSpec((tq, hps * LANES), lambda p, j: (j, p))
    qspec = pl.BlockSpec((T, hps * LANES), lambda p, j: (0, p))
    rspec = pl.BlockSpec((hps // 2, 2, T), lambda p, j: (p, 0, 0))
    sums = [SDS(s.shape[1:], F32) for s in slabs] + [SDS(packed.shape, F32)]
    return pl.pallas_call(
        body, name="attn_bwd", grid=(groups, nq),
        in_specs=[qspec, kspec, kspec, qspec, rspec, rspec] + [HBM_SPEC] * n + [_full_spec(packed)],
        out_specs=[qspec, kspec, pl.BlockSpec((1, tq, LANES), lambda p, j: (p, j, 0))] + [_full_spec(s) for s in sums],
        out_shape=[SDS((T, HW), BF16), SDS((T, HW), BF16), SDS((groups, T, LANES), F32)] + sums,
        scratch_shapes=[pltpu.VMEM((hps * LANES, T), F32)] + _rs_scratch([s.shape for s in sums[:-1]], packed.shape),
        compiler_params=_cp(("arbitrary", "arbitrary")),
    )(q_att, k_att, v_att, doop, lse_rows, dcap_rows, *slabs, packed)


def _rms_bwd(z, gain, dout):
    r = lax.rsqrt(jnp.mean(z * z, axis=-1, keepdims=True) + EPS)
    zr = z * r
    u = dout * gain
    return r * (u - zr * jnp.mean(u * zr, axis=-1, keepdims=True)), jnp.sum(dout * zr, axis=0, keepdims=True)


def _mla_bwd(dq_att, dkv_nat, dkr4, zfr, q_norm, wuq_pad, kv_norm, wukv, rc, rsa, rsb, tm):
    T = dq_att.shape[0]

    def body(dq_ref, dkv_ref, dkr_ref, zfr_ref, qn_ref, wuq_ref, kvn_ref, wukv_ref, c_ref, sa_ref, sb_ref,
             dfr_ref, gwuq_ref, gwukv_ref, gqn_ref, gkvn_ref):
        @pl.when(pl.program_id(0) == 0)
        def _():
            for ref in (gwuq_ref, gwukv_ref, gqn_ref, gkvn_ref):
                ref[...] = jnp.zeros_like(ref)

        c, sa, sb = c_ref[...], sa_ref[...], sb_ref[...]
        zq, zkv = zfr_ref[:, :Q_RANK], zfr_ref[:, Q_RANK:Q_RANK + KV_RANK]
        qn, kvn = qn_ref[...], kvn_ref[...]
        dkv = dkv_ref[...]
        dckv = _mm_nt(dkv, wukv_ref[...])
        ckv = (zkv * lax.rsqrt(jnp.mean(zkv * zkv, axis=-1, keepdims=True) + EPS) * kvn).astype(BF16)
        gwukv_ref[...] += _mm_tn(ckv, dkv)
        cq = (zq * lax.rsqrt(jnp.mean(zq * zq, axis=-1, keepdims=True) + EPS) * qn).astype(BF16)
        dq = _rope(dq_ref[...].astype(F32), c, sa, sb, -1.0).astype(BF16)
        dzkv, gkvn = _rms_bwd(zkv, kvn, dckv)
        gkvn_ref[...] += gkvn
        gwuq_ref[...] += _mm_tn(cq, dq)
        dzq, gqn = _rms_bwd(zq, qn, _mm_nt(dq, wuq_ref[...]))
        gqn_ref[...] += gqn
        dkr = functools.reduce(lambda a, b: a + b, [dkr_ref[g] for g in range(dkr4.shape[0])])
        dfr_ref[:, :Q_RANK] = dzq.astype(BF16)
        dfr_ref[:, Q_RANK:Q_RANK + KV_RANK] = dzkv.astype(BF16)
        dfr_ref[:, Q_RANK + KV_RANK:] = _rope(dkr, c, sa, sb, -1.0).astype(BF16)

    ins = (dq_att, dkv_nat, dkr4, zfr, q_norm, wuq_pad, kv_norm, wukv, rc, rsa, rsb)
    in_specs = [_row_spec(tm, HW), _row_spec(tm, HW), pl.BlockSpec((dkr4.shape[0], tm, LANES), lambda i: (0, i, 0)), _row_spec(tm, FRONT_W),
                _full_spec(q_norm), _full_spec(wuq_pad), _full_spec(kv_norm), _full_spec(wukv),
                _row_spec(tm, LANES), _row_spec(tm, LANES), _row_spec(tm, LANES)]
    outs = [SDS((T, FRONT_W), BF16), SDS((Q_RANK, HW), F32), SDS((KV_RANK, HW), F32), SDS((1, Q_RANK), F32), SDS((1, KV_RANK), F32)]
    out_specs = [_row_spec(tm, FRONT_W)] + [_full_spec(s) for s in outs[1:]]
    return pl.pallas_call(
        body, name="mla_bwd", grid=(T // tm,), in_specs=in_specs, out_specs=out_specs, out_shape=outs,
        compiler_params=_cp(("arbitrary",)),
    )(*ins)


_DZ_COLS = ((GM, ZTOT), (GA, UP), (UP, GP), (GP, GM), (ZQ, GA))


def _in_proj_bwd_x(dzs, x2, dh, norm_in, w_in_pad, tm, slabs):
    T = x2.shape[0]
    steps = T // tm
    n = len(slabs)

    def body(d0, d1, d2, d3, d4, x_ref, dh_ref, nin_ref, win_ref, *rest):
        slab_refs, (gx_ref, gnin_ref), sum_refs = rest[:n], rest[n:n + 2], rest[n + 2:2 * n + 2]
        rs = _ReduceScatter(slab_refs, None, sum_refs, None, rest[2 * n + 2:])
        step = pl.program_id(0)

        @pl.when(step == 0)
        def _():
            gnin_ref[...] = jnp.zeros_like(gnin_ref)
            rs.start1()

        pl.when(step == min(2, steps - 1))(rs.finish1_start2)
        pl.when(step == min(steps * 11 // 16, steps - 1))(rs.relay2)
        dhn = None
        for ref, (lo, hi) in zip((d0, d1, d2, d3, d4), _DZ_COLS):
            t = _mm(ref[...], win_ref[lo:hi, :])
            dhn = t if dhn is None else dhn + t
        dx, gnin = _rms_bwd(x_ref[...], nin_ref[...], dhn)
        gnin_ref[...] += gnin
        gx_ref[...] = dx + dh_ref[...]
        pl.when(step == steps - 1)(rs.finish2)

    in_specs = [_row_spec(tm, hi - lo) for lo, hi in _DZ_COLS] + [_row_spec(tm, D_MODEL), _row_spec(tm, D_MODEL),
                                                                  _full_spec(norm_in), _full_spec(w_in_pad)] + [HBM_SPEC] * n
    sums = [SDS(s.shape[1:], F32) for s in slabs]
    outs = [SDS((T, D_MODEL), F32), SDS((1, D_MODEL), F32)] + sums
    return pl.pallas_call(
        body, name="in_proj_bwd_x", grid=(steps,), in_specs=in_specs,
        out_specs=[_row_spec(tm, D_MODEL), _full_spec(outs[1])] + [_full_spec(s) for s in sums],
        out_shape=outs, scratch_shapes=_rs_scratch([s.shape for s in sums], None), compiler_params=_cp(("arbitrary",)),
    )(*dzs, x2, dh, norm_in, w_in_pad, *slabs)


SLAB_ROWS = IN_TOTAL // N_DEV


def _slab_segments(k):
    cuts = [(0, ZKR_ORIG, 0), (ZKR_ORIG, ZKR_ORIG + ROPE, NOPE), (ZKR_ORIG + ROPE, IN_TOTAL, LANES - ROPE)]
    lo, hi = k * SLAB_ROWS, (k + 1) * SLAB_ROWS
    return [(max(lo, a) - lo, max(lo, a) + shift, min(hi, b) - max(lo, a)) for a, b, shift in cuts if min(hi, b) > max(lo, a)]


def _in_proj_bwd_w(dzs, hn, tm):
    T = hn.shape[0]
    steps = T // tm

    def body(d0, d1, d2, d3, d4, hn_ref, slab_ref, acc_ref):
        @pl.when(pl.program_id(0) == 0)
        def _():
            acc_ref[...] = jnp.zeros_like(acc_ref)

        hn_v = hn_ref[...]
        for ref, (lo, hi) in zip((d0, d1, d2, d3, d4), _DZ_COLS):
            acc_ref[lo:hi, :] += _mm_tn(ref[...], hn_v)

        @pl.when(pl.program_id(0) == steps - 1)
        def _():
            for k in range(N_DEV):
                for at, src, rows in _slab_segments(k):
                    slab_ref[k, at:at + rows, :] = acc_ref[src:src + rows, :].astype(BF16)

    in_specs = [_row_spec(tm, hi - lo) for lo, hi in _DZ_COLS] + [_row_spec(tm, D_MODEL)]
    out = SDS((N_DEV, SLAB_ROWS, D_MODEL), BF16)
    return pl.pallas_call(
        body, name="in_proj_bwd_w", grid=(steps,), in_specs=in_specs, out_specs=_full_spec(out), out_shape=out,
        scratch_shapes=[pltpu.VMEM((ZTOT, D_MODEL), F32)], compiler_params=_cp(("arbitrary",)),
    )(*dzs, hn)


def _local_step(x2, tgt, norm_in, w_in_pad, q_norm, w_uq, kv_norm, w_ukv, pool_w, pool_scale, late_shards, norm_final):
    T = x2.shape[0]
    tm, tm_small, tq = min(512, T), min(256, T), min(512, T)
    heads_fwd, heads_bwd = 4, 4
    row = lambda v: v.reshape(1, -1)
    wuq_pad = jnp.pad(w_uq, ((0, 0), (0, 0), (0, HEAD_PAD - NOPE - ROPE))).reshape(Q_RANK, HW)
    wukv = w_ukv.reshape(KV_RANK, HW)
    rc, rsa, rsb = _rope_tables(T)

    hn, zgm, zga, zup, zgp, zfr, q_att, k_att, v_att, vt_att, w_ba, w_bp, w_out = _in_proj(
        x2, row(norm_in), w_in_pad, row(q_norm), wuq_pad, row(kv_norm), wukv, rc, rsa, rsb, tm, late_shards)
    w_out = w_out.reshape(D_MODEL, D_MODEL)
    o, lse_rows = _attn_fwd(q_att, k_att, vt_att, tq, heads_fwd)
    ypool = _pool_fwd(zup, zgp, pool_w, row(pool_scale))
    loss8, dh, dgm, doop, dga, dcap_rows, dyp, *slabs, g_nf = _tail(
        x2, tgt, o, zga, ypool, zgm, w_ba, w_bp, w_out, row(norm_final), tm_small)
    dup, dgp, g_pool_w, g_pool_scale = _pool_bwd(zup, zgp, dyp, pool_w, row(pool_scale))

    bf = lambda a: a.astype(BF16)
    early = [g_pool_w, g_pool_scale, g_nf, loss8[0]]
    packed = jnp.concatenate([_pack_rows(a) for a in early], axis=0)
    dq_att, dkv_nat, dkr4, s_wout, s_wba, s_wbp, tot_early = _attn_bwd(
        q_att, k_att, v_att, doop, lse_rows, dcap_rows, tq, heads_bwd, slabs, packed)
    s_pool_w, s_pool_scale, s_nf, s_loss = _unpack_rows(tot_early, early)

    dfr, g_wuq_pad, g_wukv, g_qn, g_kvn = _mla_bwd(
        dq_att, dkv_nat, dkr4, zfr, row(q_norm), wuq_pad, row(kv_norm), wukv, rc, rsa, rsb, tm)
    dzs = (dgm, dga, dup, dgp, dfr)
    slabs = [_in_proj_bwd_w(dzs, hn, tm),
             bf(g_wuq_pad.reshape(N_DEV, Q_RANK // N_DEV, HEADS, HEAD_PAD)[..., :NOPE + ROPE]),
             bf(g_wukv).reshape(N_DEV, KV_RANK // N_DEV, HW)]
    grad_x, g_nin, s_win, s_wuq, s_wukv = _in_proj_bwd_x(dzs, x2, dh, row(norm_in), w_in_pad, tm_small, slabs)
    late = [g_nin, g_qn, g_kvn]
    (tot_late,) = _reduce_scatter([], jnp.concatenate([_pack_rows(a) for a in late], axis=0))
    s_nin, s_qn, s_kvn = _unpack_rows(tot_late, late)

    grads = dict(norm_in=s_nin, w_in=s_win, q_norm=s_qn, w_uq=s_wuq, kv_norm=s_kvn, w_ukv=s_wukv, pool_w=s_pool_w.reshape(-1, GROUP),
                 pool_scale=s_pool_scale, w_branch_attn=s_wba, w_branch_pool=s_wbp, w_out=s_wout, norm_final=s_nf)
    return s_loss[0], grad_x, grads


MESH_ID = pl.DeviceIdType.MESH
VMEM_SPEC = pl.BlockSpec(memory_space=pltpu.VMEM)
HBM_SPEC = pl.BlockSpec(memory_space=pl.ANY)


def _mesh_pos():
    return lax.axis_index("x"), lax.axis_index("y"), lax.axis_index("c")


BF16_TILE_ROWS = 16


def _half_rows(rows):
    cut = -(-(rows // 2) // BF16_TILE_ROWS) * BF16_TILE_ROWS
    return pl.ds(0, cut), pl.ds(cut, rows - cut)


def _slot(px, py, pc):
    return 4 * px + 2 * py + pc


def _all_gather_bf16(shards):
    n = len(shards)

    def body(*refs):
        ins, outs = refs[:n], refs[n:2 * n]
        land0, scratch = refs[2 * n], refs[2 * n + 1:]
        wpad_ref = outs[0]
        ag = _AllGather(ins, (land0,) + tuple(outs[1:]), scratch)
        ag.start()
        ag.forward()
        ag.finish()
        wpad_ref[ZKR:GA, :] = jnp.zeros((GA - ZKR, D_MODEL), BF16)
        for k in range(N_DEV):
            for at, dst, rows in _slab_segments(k):
                wpad_ref[dst:dst + rows, :] = land0[k, at:at + rows, :]

    return pl.pallas_call(
        body, name="all_gather_weights",
        in_specs=[VMEM_SPEC] * n, out_specs=[VMEM_SPEC] + [HBM_SPEC] * (n - 1),
        out_shape=[SDS((ZTOT, D_MODEL), BF16)] + [SDS((N_DEV,) + s.shape, BF16) for s in shards[1:]],
        scratch_shapes=[pltpu.VMEM((N_DEV,) + shards[0].shape, BF16)] + _ag_scratch([s.shape for s in shards]),
        compiler_params=_cp(),
    )(*shards)


def _ag_scratch(shapes):
    n = len(shapes)
    dma = pltpu.SemaphoreType.DMA
    return [pltpu.VMEM(tuple(s), BF16) for s in shapes] + [dma((_AllGather.COPIES * n,)), dma((_AllGather.COPIES * n,)), dma((n,))]


class _AllGather:
    COPIES = 8

    def __init__(self, in_refs, dest_refs, scratch):
        n = self.n = len(in_refs)
        self.ins, self.dests, self.stage = in_refs, dest_refs, scratch[:n]
        self.send_sems, self.recv_sems, self.local_sems = scratch[n:]
        x, y, c = _mesh_pos()
        self.c, self.me, self.sibling = c, (x, y, c), (x, y, 1 - c)
        self.xn, self.yn, self.diag = (1 - x, y), (x, 1 - y), (1 - x, 1 - y)

    def _halves(self, a):
        return _half_rows(self.stage[a].shape[0])

    def _copy(self, a, k, block, to, from_stage=False, rows=None):
        dst = self.dests[a].at[_slot(*block)]
        src = self.stage[a] if from_stage else dst
        if rows is not None:
            src, dst = src.at[rows], dst.at[rows]
        return pltpu.make_async_remote_copy(
            src_ref=src, dst_ref=dst, send_sem=self.send_sems.at[self.COPIES * a + k],
            recv_sem=self.recv_sems.at[self.COPIES * a + k], device_id=to, device_id_type=MESH_ID)

    def _mine(self):
        return [pltpu.make_async_copy(self.stage[a], self.dests[a].at[_slot(*self.me)], self.local_sems.at[a]) for a in range(self.n)]

    def _first(self, a):
        return [self._copy(a, 0, self.me, self.sibling, True), self._copy(a, 1, self.me, (*self.xn, self.c), True),
                self._copy(a, 2, self.me, (*self.yn, self.c), True)]

    def _relays(self, a):
        lo, hi = self._halves(a)
        return [self._copy(a, 3, (*self.xn, self.c), (*self.yn, self.c), rows=lo),
                self._copy(a, 4, (*self.yn, self.c), (*self.xn, self.c), rows=hi)]

    def _passes(self, a):
        return [self._copy(a, 5 + j, (*chip, self.c), self.sibling) for j, chip in enumerate((self.xn, self.yn, self.diag))]

    def start(self):
        for a in range(self.n):
            self.stage[a][...] = self.ins[a][...].astype(BF16)
        for cp in self._mine():
            cp.start()
        for a in range(self.n):
            for cp in self._first(a):
                cp.start()

    def forward(self):
        for a in range(self.n):
            relays, passes = self._relays(a), self._passes(a)
            for j, chip in enumerate((self.xn, self.yn)):
                self._copy(a, 1 + j, (*chip, self.c), self.me).wait_recv()
                relays[j].start()
                passes[j].start()

    def finish(self):
        for a in range(self.n):
            lo, hi = self._halves(a)
            self._copy(a, 3, (*self.diag, self.c), self.me, rows=lo).wait_recv()
            self._copy(a, 4, (*self.diag, self.c), self.me, rows=hi).wait_recv()
            self._passes(a)[2].start()
        for a in range(self.n):
            self._copy(a, 0, self.sibling, self.me).wait_recv()
            for j, chip in enumerate((self.xn, self.yn, self.diag)):
                self._copy(a, 5 + j, (*chip, 1 - self.c), self.me).wait_recv()
            for cp in self._first(a) + self._relays(a) + self._passes(a):
                cp.wait_send()
        for cp in self._mine():
            cp.wait()


N_CHIPS = 4


def _reduce_scatter(slabs, packed):
    def body(*refs):
        n = len(slabs)
        rs = _ReduceScatter(refs[:n], refs[n], refs[n + 1:2 * n + 1], refs[2 * n + 1], refs[2 * n + 2:])
        rs.start1()
        rs.finish1_start2()
        rs.relay2()
        rs.finish2()

    shapes = [s.shape[1:] for s in slabs]
    return pl.pallas_call(
        body, name="reduce_scatter_grads",
        in_specs=[HBM_SPEC] * len(slabs) + [VMEM_SPEC], out_specs=[VMEM_SPEC] * (len(slabs) + 1),
        out_shape=[SDS(s, F32) for s in shapes] + [SDS(packed.shape, F32)],
        scratch_shapes=_rs_scratch(shapes, packed.shape), compiler_params=_cp(),
    )(*slabs, packed)


def _rs_scratch(shapes, packed_shape):
    n = len(shapes)
    n1, n2 = N_CHIPS * n + 1, _ReduceScatter.L2_COPIES * n + N_CHIPS - 1
    dma = pltpu.SemaphoreType.DMA
    packed = [] if packed_shape is None else [pltpu.VMEM(packed_shape, F32), pltpu.VMEM((N_CHIPS,) + tuple(packed_shape), F32)]
    return ([pltpu.VMEM((N_CHIPS,) + tuple(s), BF16) for s in shapes] * 2 + [pltpu.VMEM((N_CHIPS - 1,) + tuple(s), BF16) for s in shapes] * 2
            + packed + [dma((max(N_CHIPS * n, 1),)), dma((n1,)), dma((n1,)), dma((n2,)), dma((n2,))])


class _ReduceScatter:
    L2_COPIES = 6

    def __init__(self, slab_refs, packed_ref, out_refs, ptot_ref, scratch):
        n = self.n = len(slab_refs)
        self.slabs, self.packed, self.outs, self.ptot = slab_refs, packed_ref, out_refs, ptot_ref
        self.own1, self.land1, self.send2, self.land2 = (scratch[k * n:(k + 1) * n] for k in range(4))
        rest = scratch[4 * n:]
        if packed_ref is not None:
            self.pland1, self.pland2 = rest[:2]
            rest = rest[2:]
        self.loc_sems, self.send1_sems, self.recv1_sems, self.send2_sems, self.recv2_sems = rest
        self.x, self.y, self.c = _mesh_pos()

    def _chip(self, r):
        return (1 - self.x if r & 2 else self.x, 1 - self.y if r & 1 else self.y)

    @staticmethod
    def _remote(src, dst, send_sem, recv_sem, to):
        return pltpu.make_async_remote_copy(src_ref=src, dst_ref=dst, send_sem=send_sem, recv_sem=recv_sem, device_id=to,
                                            device_id_type=MESH_ID)

    def _copies1(self):
        c, sibling = self.c, (self.x, self.y, 1 - self.c)
        cps = []
        for a in range(self.n):
            for r in range(N_CHIPS):
                k = N_CHIPS * a + r
                cps.append(pltpu.make_async_copy(self.slabs[a].at[_slot(*self._chip(r), c)], self.own1[a].at[r], self.loc_sems.at[k]))
                cps.append(self._remote(self.slabs[a].at[_slot(*self._chip(r), 1 - c)], self.land1[a].at[r],
                                        self.send1_sems.at[k], self.recv1_sems.at[k], sibling))
        if self.packed is not None:
            k = N_CHIPS * self.n
            cps.append(self._remote(self.packed, self.pland1, self.send1_sems.at[k], self.recv1_sems.at[k], sibling))
        return cps

    def _halves(self, a):
        return _half_rows(self.send2[a].shape[1])

    def _copy2(self, a, k):
        lo, hi = self._halves(a)
        xn, yn = (*self._chip(2), self.c), (*self._chip(1), self.c)
        slot, rows, to = [(1, lo, xn), (2, lo, xn), (0, hi, yn), (2, hi, yn), (0, lo, yn), (1, hi, xn)][k]
        return self._remote(self.send2[a].at[slot].at[rows], self.land2[a].at[slot].at[rows],
                            self.send2_sems.at[self.L2_COPIES * a + k], self.recv2_sems.at[self.L2_COPIES * a + k], to)

    def _copies2_packed(self):
        base = self.L2_COPIES * self.n - 1
        return [self._remote(self.pland2.at[0], self.pland2.at[r], self.send2_sems.at[base + r], self.recv2_sems.at[base + r],
                             (*self._chip(r), self.c)) for r in range(1, N_CHIPS)]

    def start1(self):
        for cp in self._copies1():
            cp.start()

    def finish1_start2(self):
        for cp in self._copies1():
            cp.wait()
        for a in range(self.n):
            self.outs[a][...] = self.own1[a][0].astype(F32) + self.land1[a][0].astype(F32)
            for r in range(1, N_CHIPS):
                self.send2[a][r - 1] = (self.own1[a][r].astype(F32) + self.land1[a][r].astype(F32)).astype(BF16)
            for k in range(4):
                self._copy2(a, k).start()
        if self.packed is not None:
            self.pland2[0] = self.packed[...] + self.pland1[...]
            for cp in self._copies2_packed():
                cp.start()

    def relay2(self):
        for a in range(self.n):
            lo, hi = self._halves(a)
            s2, l2 = self.send2[a], self.land2[a]
            self._copy2(a, 1).wait_recv()
            s2[0, lo] = (s2[0, lo].astype(F32) + l2[2, lo].astype(F32)).astype(BF16)
            self._copy2(a, 4).start()
            self._copy2(a, 3).wait_recv()
            s2[1, hi] = (s2[1, hi].astype(F32) + l2[2, hi].astype(F32)).astype(BF16)
            self._copy2(a, 5).start()

    def finish2(self):
        for a in range(self.n):
            for k in (0, 2, 4, 5):
                self._copy2(a, k).wait_recv()
            for k in range(self.L2_COPIES):
                self._copy2(a, k).wait_send()
            l2 = self.land2[a]
            self.outs[a][...] = self.outs[a][...] + (l2[0].astype(F32) + l2[1].astype(F32))
        if self.packed is not None:
            for cp in self._copies2_packed():
                cp.wait()
            p2 = self.pland2
            self.ptot[...] = (p2[0] + p2[1]) + (p2[2] + p2[3])


def _adamw(ws, gs, ms, vs):
    n = len(ws)

    def body(*refs):
        for k in range(n):
            w, g, m, v = (refs[j * n + k][...] for j in range(4))
            d_ref, nm_ref, nv_ref = (refs[(4 + j) * n + k] for j in range(3))
            m = ADAM_B1 * m + (1.0 - ADAM_B1) * g
            v = ADAM_B2 * v + (1.0 - ADAM_B2) * jnp.square(g)
            m_hat = m / (1.0 - ADAM_B1 ** ADAM_STEP)
            v_hat = v / (1.0 - ADAM_B2 ** ADAM_STEP)
            d_ref[...] = -ADAM_LR * (m_hat / (jnp.sqrt(v_hat) + ADAM_EPS) + ADAM_WD * w)
            nm_ref[...] = m
            nv_ref[...] = v

    outs = pl.pallas_call(
        body, name="adamw", in_specs=[VMEM_SPEC] * (4 * n), out_specs=[VMEM_SPEC] * (3 * n),
        out_shape=[SDS(w.shape, F32) for w in ws] * 3, compiler_params=_cp(),
    )(*ws, *gs, *ms, *vs)
    return outs[:n], outs[n:2 * n], outs[2 * n:]


WEIGHTS = ("norm_in", "w_in", "q_norm", "w_uq", "kv_norm", "w_ukv", "pool_w", "pool_scale", "w_branch_attn", "w_branch_pool",
           "w_out", "norm_final")
SUBLANES = 8


def _pack_rows(a):
    a = a.reshape(-1, LANES)
    return jnp.pad(a, ((0, -a.shape[0] % SUBLANES), (0, 0)))


def _unpack_rows(packed, like):
    out, row = [], 0
    for a in like:
        rows = a.size // LANES
        out.append(packed[row:row + rows].reshape(a.shape))
        row += rows + (-rows % SUBLANES)
    return out


def kernel(x, norm_in, w_in, q_norm, w_uq, kv_norm, w_ukv, pool_w, pool_scale, w_branch_attn, w_branch_pool, w_out, norm_final, loss_target, m_norm_in, m_w_in, m_q_norm, m_w_uq, m_kv_norm, m_w_ukv, m_pool_w, m_pool_scale, m_w_branch_attn, m_w_branch_pool, m_w_out, m_norm_final, v_norm_in, v_w_in, v_q_norm, v_w_uq, v_kv_norm, v_w_ukv, v_pool_w, v_pool_scale, v_w_branch_attn, v_w_branch_pool, v_w_out, v_norm_final):
    w = dict(norm_in=norm_in, w_in=w_in, q_norm=q_norm, w_uq=w_uq, kv_norm=kv_norm, w_ukv=w_ukv, pool_w=pool_w, pool_scale=pool_scale,
             w_branch_attn=w_branch_attn, w_branch_pool=w_branch_pool, w_out=w_out, norm_final=norm_final)
    m = dict(norm_in=m_norm_in, w_in=m_w_in, q_norm=m_q_norm, w_uq=m_w_uq, kv_norm=m_kv_norm, w_ukv=m_w_ukv, pool_w=m_pool_w,
             pool_scale=m_pool_scale, w_branch_attn=m_w_branch_attn, w_branch_pool=m_w_branch_pool, w_out=m_w_out, norm_final=m_norm_final)
    v = dict(norm_in=v_norm_in, w_in=v_w_in, q_norm=v_q_norm, w_uq=v_w_uq, kv_norm=v_kv_norm, w_ukv=v_w_ukv, pool_w=v_pool_w,
             pool_scale=v_pool_scale, w_branch_attn=v_w_branch_attn, w_branch_pool=v_w_branch_pool, w_out=v_w_out, norm_final=v_norm_final)

    def as2d(name, a):
        if name == "w_in":
            return a.T
        if name == "w_uq":
            return a
        if name == "w_ukv":
            return a.reshape(a.shape[0], -1)
        if name == "pool_w":
            return a.reshape(-1, GROUP)
        return a.reshape(1, -1) if a.ndim == 1 else a

    def unshape(name, a):
        return a.T if name == "w_in" else a.reshape(w[name].shape)

    w_in_pad, w_uq_full, w_ukv_full = _all_gather_bf16([as2d(k, w[k]) for k in ("w_in", "w_uq", "w_ukv")])
    loss, grad_x, g2d = _local_step(
        x.reshape(x.shape[1:]), loss_target.reshape(x.shape[1:]), norm_in, w_in_pad, q_norm,
        w_uq_full.reshape(Q_RANK, HEADS, NOPE + ROPE), kv_norm, w_ukv_full.reshape(KV_RANK, HEADS, NOPE + VDIM),
        pool_w, pool_scale, [w_branch_attn, w_branch_pool, w_out], norm_final)

    deltas, new_m, new_v = _adamw([as2d(k, w[k]) for k in WEIGHTS], [g2d[k] for k in WEIGHTS],
                                  [as2d(k, m[k]) for k in WEIGHTS], [as2d(k, v[k]) for k in WEIGHTS])
    shaped = lambda arrs: [unshape(k, a) for k, a in zip(WEIGHTS, arrs)]
    return (loss, grad_x.reshape(x.shape), *shaped([g2d[k] for k in WEIGHTS]), *shaped(deltas), *shaped(new_m), *shaped(new_v))
```

```python
import functools

import jax
import jax.numpy as jnp
import numpy as np
from jax import lax
from jax.experimental import pallas as pl
from jax.experimental.pallas import tpu as pltpu

F32 = jnp.float32
BF16 = jnp.bfloat16
SDS = jax.ShapeDtypeStruct

D_MODEL = 1024
HEADS = 8
NOPE = 64
ROPE = 32
VDIM = 64
Q_RANK = 384
KV_RANK = 256
MLA_W = HEADS * VDIM
POOL_W = 512
POOL_GROUPS = 4
GROUP = POOL_W // POOL_GROUPS
CHUNK = 64
ROPE_THETA = 10000.0
EPS = 1e-6
SCALE = (NOPE + ROPE) ** -0.5
LOG2E = 1.4426950408889634
LN2 = 0.6931471805599453
QK_SCALE_LOG2 = SCALE * LOG2E
IN_TOTAL = 4256
ADAM_LR, ADAM_B1, ADAM_B2, ADAM_EPS, ADAM_WD, ADAM_STEP = 0.001, 0.9, 0.999, 1e-08, 0.01, 10

N_DEV = 8
LANES = 128
HEAD_PAD = LANES
HW = HEADS * HEAD_PAD

ZQ, ZKV, ZKR, GA, UP, GP, GM, ZTOT = 0, 384, 640, 768, 1280, 1792, 2304, 4352
FRONT_W = GA
ZKR_ORIG = 640

VMEM_LIMIT = 62 * 1024 * 1024


def _cp(sem=None, **kw):
    if sem is not None:
        kw["dimension_semantics"] = sem
    return pltpu.CompilerParams(vmem_limit_bytes=VMEM_LIMIT, **kw)


def _mm(a, b):
    return lax.dot_general(a, b, (((1,), (0,)), ((), ())), preferred_element_type=F32)


def _mm_nt(a, b):
    return lax.dot_general(a, b, (((1,), (1,)), ((), ())), preferred_element_type=F32)


def _mm_tn(a, b):
    return lax.dot_general(a, b, (((0,), (0,)), ((), ())), preferred_element_type=F32)


def _row_spec(tm, w):
    return pl.BlockSpec((tm, w), lambda i: (i, 0))


def _full_spec(a):
    nd = len(a.shape)
    return pl.BlockSpec(a.shape, lambda *_: (0,) * nd)


def _rope(v, c, sa, sb, sign):
    n = v.shape[-1]
    reps = n // LANES
    if reps > 1:
        c, sa, sb = (jnp.tile(t, (1, reps)) for t in (c, sa, sb))
    up = pltpu.roll(v, n - ROPE // 2, 1)
    dn = pltpu.roll(v, ROPE // 2, 1)
    return v * c + sign * (up * sa + dn * sb)


def _rope_tables(T):
    half = ROPE // 2
    inv_freq = np.float32(ROPE_THETA) ** (-np.arange(half, dtype=np.float32) / np.float32(half))
    ang = np.arange(T, dtype=np.float32)[:, None] * inv_freq[None, :].astype(np.float32)
    cos, sin = np.cos(ang.astype(np.float64)).astype(np.float32), np.sin(ang.astype(np.float64)).astype(np.float32)
    z16 = np.zeros((T, half), np.float32)
    z32 = np.zeros((T, LANES - NOPE - ROPE), np.float32)
    c = np.concatenate([np.ones((T, NOPE), np.float32), cos, cos, z32], axis=1)
    sa = np.concatenate([np.zeros((T, NOPE), np.float32), -sin, z16, z32], axis=1)
    sb = np.concatenate([np.zeros((T, NOPE), np.float32), z16, sin, z32], axis=1)
    return jnp.asarray(c), jnp.asarray(sa), jnp.asarray(sb)


def _silu_parts(g):
    sg = jax.nn.sigmoid(g)
    return g * sg, sg + g * sg * (1.0 - sg)


def _in_proj(x2, norm_in, w_in_pad, q_norm, wuq_pad, kv_norm, wukv, rc, rsa, rsb, tm, late_shards):
    T = x2.shape[0]
    steps = T // tm
    n = len(late_shards)

    def body(x_ref, nin_ref, win_ref, qn_ref, wuq_ref, kvn_ref, wukv_ref, c_ref, sa_ref, sb_ref, *rest):
        hn_ref, zgm_ref, zga_ref, zup_ref, zgp_ref, zfr_ref, q_ref, k_ref, v_ref, vt_ref = rest[n:n + 10]
        ag = _AllGather(rest[:n], rest[n + 10:2 * n + 10], rest[2 * n + 10:])
        step = pl.program_id(0)
        pl.when(step == 0)(ag.start)
        pl.when(step == min(3, steps - 1))(ag.forward)
        xf = x_ref[...]
        r = lax.rsqrt(jnp.mean(xf * xf, axis=-1, keepdims=True) + EPS)
        hn = (xf * r * nin_ref[...]).astype(BF16)
        hn_ref[...] = hn
        zfr = _mm_nt(hn, win_ref[ZQ:GA, :])
        zfr_ref[...] = zfr
        zq, zkv, zkr = zfr[:, ZQ:ZKV], zfr[:, ZKV:ZKR], zfr[:, ZKR:GA]
        c, sa, sb = c_ref[...], sa_ref[...], sb_ref[...]
        rq = lax.rsqrt(jnp.mean(zq * zq, axis=-1, keepdims=True) + EPS)
        cq = (zq * rq * qn_ref[...]).astype(BF16)
        rkv = lax.rsqrt(jnp.mean(zkv * zkv, axis=-1, keepdims=True) + EPS)
        ckv = (zkv * rkv * kvn_ref[...]).astype(BF16)
        zga_ref[...] = _mm_nt(hn, win_ref[GA:UP, :])
        zup_ref[...] = _mm_nt(hn, win_ref[UP:GP, :])
        zgp_ref[...] = _mm_nt(hn, win_ref[GP:GM, :])
        q_raw = _mm(cq, wuq_ref[...])
        kv = _mm(ckv, wukv_ref[...])
        zgm_ref[...] = _mm_nt(hn, win_ref[GM:ZTOT, :])
        q = _rope(q_raw, c, sa, sb, 1.0)
        q_ref[...] = (q * QK_SCALE_LOG2).astype(BF16)
        kr = _rope(zkr, c, sa, sb, 1.0)
        lane = lax.broadcasted_iota(jnp.int32, kv.shape, 1) % LANES
        k_ref[...] = jnp.where(lane < NOPE, kv, jnp.tile(kr, (1, HEADS))).astype(BF16)
        v = jnp.where(lane < NOPE, 1.0, kv).astype(BF16)
        v_ref[...] = v
        vt_ref[...] = v.T
        pl.when(step == steps - 1)(ag.finish)

    ins = (x2, norm_in, w_in_pad, q_norm, wuq_pad, kv_norm, wukv, rc, rsa, rsb)
    in_specs = [_row_spec(tm, D_MODEL), _full_spec(norm_in), _full_spec(w_in_pad), _full_spec(q_norm), _full_spec(wuq_pad),
                _full_spec(kv_norm), _full_spec(wukv), _row_spec(tm, LANES), _row_spec(tm, LANES), _row_spec(tm, LANES)]
    widths = [(D_MODEL, BF16), (ZTOT - GM, F32), (UP - GA, F32), (GP - UP, F32), (GM - GP, F32), (FRONT_W, F32),
              (HW, BF16), (HW, BF16), (HW, BF16)]
    return pl.pallas_call(
        body, name="in_proj", grid=(steps,), in_specs=in_specs + [_full_spec(s) for s in late_shards],
        out_specs=[_row_spec(tm, w) for w, _ in widths] + [pl.BlockSpec((HW, tm), lambda i: (0, i))] + [HBM_SPEC] * n,
        out_shape=[SDS((T, w), dt) for w, dt in widths] + [SDS((HW, T), BF16)]
        + [SDS((N_DEV,) + s.shape, BF16) for s in late_shards],
        scratch_shapes=_ag_scratch([s.shape for s in late_shards]), compiler_params=_cp(("arbitrary",)),
    )(*ins, *late_shards)


def _chunk_mask(n_q, n_k, q_off, transposed):
    shape = (n_k, n_q) if transposed else (n_q, n_k)
    q = (lax.broadcasted_iota(jnp.int32, shape, 1 if transposed else 0) + q_off) // CHUNK
    k = lax.broadcasted_iota(jnp.int32, shape, 0 if transposed else 1) // CHUNK
    return k <= q


def _store_pair_rows(ref, k, pair):
    t = pair.T
    ref[k, 0:1, :] = t[0:1, :]
    ref[k, 1:2, :] = t[VDIM:VDIM + 1, :]


def _attn_fwd(q_att, k_att, vt_att, tq, hps):
    T = q_att.shape[0]
    head_lanes = [slice(h * LANES, (h + 1) * LANES) for h in range(hps)]

    def body(q_ref, k_ref, vt_ref, o_ref, lser_ref):
        i = pl.program_id(1)
        mask = _chunk_mask(tq, tq, 0, True)
        lane = lax.broadcasted_iota(jnp.int32, (tq, LANES), 1)
        qs = [q_ref[:, hs] for hs in head_lanes]

        def step(j, carry, masked):
            off = pl.multiple_of(j * tq, tq)
            sts = [_mm_nt(k_ref[pl.ds(off, tq), hs], qh) for qh, hs in zip(qs, head_lanes)]
            if masked:
                sts = [jnp.where(mask, st, -jnp.inf) for st in sts]
            ms = [jnp.maximum(m, jnp.max(st, axis=0, keepdims=True)) for (m, _), st in zip(carry, sts)]
            pts = [jnp.exp2(st - m_new).astype(BF16) for st, m_new in zip(sts, ms)]
            return tuple((m_new, jnp.exp2(m - m_new) * acc + _mm(vt_ref[hs, pl.ds(off, tq)], pt))
                         for (m, acc), m_new, pt, hs in zip(carry, ms, pts, head_lanes))

        init = ((jnp.full((1, tq), -jnp.inf, F32), jnp.zeros((LANES, tq), F32)),) * hps
        res = step(i, lax.fori_loop(0, i, functools.partial(step, masked=False), init), True)
        for pair in range(hps // 2):
            (ma, acca), (mb, accb) = res[2 * pair], res[2 * pair + 1]
            la, lb = acca[:1], accb[:1]
            oa, ob = (acca / la).T, (accb / lb).T
            o_ref[:, pair * LANES:(pair + 1) * LANES] = jnp.where(lane < VDIM, pltpu.roll(oa, VDIM, 1), ob)
            lser_ref[pair, 0:1, :] = ma + jnp.log2(la)
            lser_ref[pair, 1:2, :] = mb + jnp.log2(lb)

    qspec = pl.BlockSpec((tq, hps * LANES), lambda p, i: (i, p))
    kspec = pl.BlockSpec((T, hps * LANES), lambda p, i: (0, p))
    vspec = pl.BlockSpec((hps * LANES, T), lambda p, i: (p, 0))
    ospec = pl.BlockSpec((tq, hps * VDIM), lambda p, i: (i, p))
    return pl.pallas_call(
        body, name="attn_fwd", grid=(HEADS // hps, T // tq), in_specs=[qspec, kspec, vspec],
        out_specs=[ospec, pl.BlockSpec((hps // 2, 2, tq), lambda p, i: (p, 0, i))],
        out_shape=[SDS((T, MLA_W), F32), SDS((HEADS // 2, 2, T), F32)],
        compiler_params=_cp(("parallel", "parallel")),
    )(q_att, k_att, vt_att)


def _pick(g, vals):
    out = vals[-1]
    for k in range(len(vals) - 2, -1, -1):
        out = jnp.where(g == k, vals[k], out)
    return out


def _window_sum(u, g, forward):
    T = u.shape[0]
    row = lax.broadcasted_iota(jnp.int32, u.shape, 0)

    def sh(s, k):
        if forward:
            return jnp.where(row >= k, pltpu.roll(s, k, 0), 0.0)
        return jnp.where(row < T - k, pltpu.roll(s, T - k, 0), 0.0)

    sums, s = [], u
    for k in (1, 2, 4, 8):
        s = s + sh(s, k)
        sums.append(s)
    return _pick(g, sums)


MAX_WINDOW = 16


def _pool_inv_count(shape, g):
    T, n = shape
    row = lax.broadcasted_iota(jnp.int32, (MAX_WINDOW, n), 0)
    head = 1.0 / jnp.minimum(row + 1, lax.shift_left(jnp.int32(2), g)).astype(F32)
    inv_w = _pick(g, [0.5, 0.25, 0.125, 0.0625])
    return jnp.concatenate([head, jnp.broadcast_to(inv_w, (T - MAX_WINDOW, n)).astype(F32)], axis=0)


def _pool_fwd(zup, zgp, pool_w, pool_scale):
    T = zup.shape[0]

    def body(u_ref, g_ref, w_ref, sc_ref, y_ref):
        g = pl.program_id(0)
        u = u_ref[...]
        d = _window_sum(u, g, True) * _pool_inv_count(u.shape, g) - u
        lin = _mm(d.astype(BF16), w_ref[0].astype(BF16))
        silu, _ = _silu_parts(g_ref[...])
        y_ref[...] = (lin * sc_ref[...] * silu).astype(BF16)

    col = pl.BlockSpec((T, GROUP), lambda g: (0, g))
    return pl.pallas_call(
        body, name="pool_fwd", grid=(POOL_GROUPS,),
        in_specs=[col, col, pl.BlockSpec((1, GROUP, GROUP), lambda g: (g, 0, 0)), pl.BlockSpec((1, GROUP), lambda g: (0, g))],
        out_specs=col, out_shape=SDS((T, POOL_W), BF16), compiler_params=_cp(("parallel",)),
    )(zup, zgp, pool_w, pool_scale)


def _pool_bwd(zup, zgp, dyp, pool_w, pool_scale):
    T = zup.shape[0]

    def body(u_ref, g_ref, dy_ref, w_ref, sc_ref, du_ref, dg_ref, gw_ref, gsc_ref):
        g = pl.program_id(0)
        u = u_ref[...]
        inv = _pool_inv_count(u.shape, g)
        d = (_window_sum(u, g, True) * inv - u).astype(BF16)
        wb = w_ref[0].astype(BF16)
        lin = _mm(d, wb)
        sc = sc_ref[...]
        silu, dsilu = _silu_parts(g_ref[...])
        dy = dy_ref[...]
        dg_ref[...] = (dy * lin * sc * dsilu).astype(BF16)
        dpre = dy * silu
        gsc_ref[...] = jnp.sum(dpre * lin, axis=0, keepdims=True)
        dlin = (dpre * sc).astype(BF16)
        gw_ref[0] = _mm_tn(d, dlin)
        dd = _mm_nt(dlin, wb)
        du_ref[...] = (_window_sum(dd * inv, g, False) - dd).astype(BF16)

    col = pl.BlockSpec((T, GROUP), lambda g: (0, g))
    wspec = pl.BlockSpec((1, GROUP, GROUP), lambda g: (g, 0, 0))
    vspec = pl.BlockSpec((1, GROUP), lambda g: (0, g))
    return pl.pallas_call(
        body, name="pool_bwd", grid=(POOL_GROUPS,), in_specs=[col, col, col, wspec, vspec], out_specs=[col, col, wspec, vspec],
        out_shape=[SDS((T, POOL_W), BF16), SDS((T, POOL_W), BF16), SDS((POOL_GROUPS, GROUP, GROUP), F32), SDS((1, POOL_W), F32)],
        compiler_params=_cp(("parallel",)),
    )(zup, zgp, dyp, pool_w, pool_scale)


def _tail(x2, tgt, o, zga, ypool, zgm, wba, wbp, wout, norm_final, tm):
    T = x2.shape[0]
    steps = T // tm
    cols = D_MODEL // N_DEV

    def body(x_ref, tgt_ref, o_ref, zga_ref, yp_ref, zgm_ref, wba_ref, wbp_ref, wout_ref, nf_ref,
             loss_ref, dh_ref, dgm_ref, doop_ref, dga_ref, dcapr_ref, dyp_ref, swout_ref, swba_ref, swbp_ref, gnf_ref,
             gwout_ref, gwba_ref, gwbp_ref):
        @pl.when(pl.program_id(0) == 0)
        def _():
            for ref in (loss_ref, gwout_ref, gwba_ref, gwbp_ref, gnf_ref):
                ref[...] = jnp.zeros_like(ref)

        o_v = o_ref[...]
        silu, dsilu = _silu_parts(zga_ref[...])
        ya = (o_v * silu).astype(BF16)
        yp = yp_ref[...]
        wba_v = jnp.concatenate([wba_ref[k] for k in range(N_DEV)], axis=1)
        wbp_v = jnp.concatenate([wbp_ref[k] for k in range(N_DEV)], axis=1)
        wout_v = wout_ref[...]
        a = _mm(ya, wba_v)
        p = _mm(yp, wbp_v)
        gate = jax.nn.sigmoid(zgm_ref[...])
        ga, gp = gate[:, :D_MODEL], gate[:, D_MODEL:]
        mg = (ga * a + gp * p).astype(BF16)
        h = x_ref[...] + _mm(mg, wout_v)
        r = lax.rsqrt(jnp.mean(h * h, axis=-1, keepdims=True) + EPS)
        gf = nf_ref[...]
        hr = h * r
        e = hr * gf - tgt_ref[...]
        loss_ref[...] += (0.5 / D_MODEL) * jnp.sum(e * e)
        dy = e * (1.0 / D_MODEL)
        gnf_ref[...] += jnp.sum(dy * hr, axis=0, keepdims=True)
        u = dy * gf
        dh = r * (u - hr * jnp.mean(u * hr, axis=-1, keepdims=True))
        dh_ref[...] = dh
        dhb = dh.astype(BF16)
        dmg = _mm_nt(dhb, wout_v)
        dab = (dmg * ga).astype(BF16)
        dpb = (dmg * gp).astype(BF16)
        dya = _mm_nt(dab, wba_v)
        dyp_ref[...] = _mm_nt(dpb, wbp_v)
        gwout_ref[...] += _mm_tn(mg, dhb)
        gwba_ref[...] += _mm_tn(ya, dab)
        gwbp_ref[...] += _mm_tn(yp, dpb)
        dgm_ref[:, :D_MODEL] = (dmg * a * ga * (1.0 - ga)).astype(BF16)
        dgm_ref[:, D_MODEL:] = (dmg * p * gp * (1.0 - gp)).astype(BF16)
        do = dya * silu
        dga_ref[...] = (dya * o_v * dsilu).astype(BF16)
        prod = do * o_v
        lo = lax.broadcasted_iota(jnp.int32, (tm, LANES), 1) < VDIM
        for pair in range(HEADS // 2):
            ls = slice(pair * LANES, (pair + 1) * LANES)
            do_p, prod_p = do[:, ls], prod[:, ls]
            dcap_a = jnp.sum(jnp.where(lo, prod_p, 0.0), axis=-1, keepdims=True)
            dcap_b = jnp.sum(jnp.where(lo, 0.0, prod_p), axis=-1, keepdims=True)
            _store_pair_rows(dcapr_ref, pair, jnp.where(lo, dcap_a, dcap_b))
            doop_ref[:, 2 * pair * LANES:(2 * pair + 1) * LANES] = jnp.where(lo, 0.0, pltpu.roll(do_p, VDIM, 1)).astype(BF16)
            doop_ref[:, (2 * pair + 1) * LANES:(2 * pair + 2) * LANES] = jnp.where(lo, 0.0, do_p).astype(BF16)

        @pl.when(pl.program_id(0) == steps - 1)
        def _():
            for k in range(N_DEV):
                swout_ref[k] = gwout_ref[k * cols:(k + 1) * cols, :].astype(BF16)
                swba_ref[k] = gwba_ref[:, k * cols:(k + 1) * cols].astype(BF16)
                swbp_ref[k] = gwbp_ref[:, k * cols:(k + 1) * cols].astype(BF16)

    ins = (x2, tgt, o, zga, ypool, zgm, wba, wbp, wout, norm_final)
    in_specs = [_row_spec(tm, D_MODEL), _row_spec(tm, D_MODEL), _row_spec(tm, MLA_W), _row_spec(tm, MLA_W), _row_spec(tm, POOL_W),
                _row_spec(tm, 2 * D_MODEL), _full_spec(wba), _full_spec(wbp), _full_spec(wout), _full_spec(norm_final)]
    outs = [SDS((8, LANES), F32), SDS((T, D_MODEL), F32), SDS((T, 2 * D_MODEL), BF16), SDS((T, HW), BF16), SDS((T, MLA_W), BF16),
            SDS((HEADS // 2, 2, T), F32), SDS((T, POOL_W), F32),
            SDS((N_DEV, cols, D_MODEL), BF16), SDS((N_DEV, MLA_W, cols), BF16), SDS((N_DEV, POOL_W, cols), BF16), SDS((1, D_MODEL), F32)]
    out_specs = [_full_spec(outs[0]), _row_spec(tm, D_MODEL), _row_spec(tm, 2 * D_MODEL), _row_spec(tm, HW), _row_spec(tm, MLA_W),
                 pl.BlockSpec((HEADS // 2, 2, tm), lambda i: (0, 0, i)), _row_spec(tm, POOL_W),
                 _full_spec(outs[7]), _full_spec(outs[8]), _full_spec(outs[9]), _full_spec(outs[10])]
    return pl.pallas_call(
        body, name="tail", grid=(steps,), in_specs=in_specs, out_specs=out_specs, out_shape=outs,
        scratch_shapes=[pltpu.VMEM((D_MODEL, D_MODEL), F32), pltpu.VMEM((MLA_W, D_MODEL), F32), pltpu.VMEM((POOL_W, D_MODEL), F32)],
        compiler_params=_cp(("arbitrary",)),
    )(*ins)


def _attn_bwd(q_att, k_att, v_att, doop, lse_rows, dcap_rows, tq, hps, slabs, packed):
    T = q_att.shape[0]
    nq = T // tq
    n = len(slabs)
    groups = HEADS // hps
    head_lanes = [slice(h * LANES, (h + 1) * LANES) for h in range(hps)]

    def body(q_ref, k_ref, v_ref, doop_ref, lse_ref, dcap_ref, *rest):
        slab_refs, packed_ref = rest[:n], rest[n]
        dq_ref, dkv_ref, dkr_ref = rest[n + 1:n + 4]
        sum_refs, ptot_ref = rest[n + 4:2 * n + 4], rest[2 * n + 4]
        dq_acc = rest[2 * n + 5]
        rs = _ReduceScatter(slab_refs, packed_ref, sum_refs, ptot_ref, rest[2 * n + 6:])
        group, j = pl.program_id(0), pl.program_id(1)
        tick, last = group * nq + j, groups * nq - 1
        pl.when(tick == 0)(rs.start1)
        pl.when(tick == min(1, last))(rs.finish1_start2)
        pl.when(tick == min(3, last))(rs.relay2)
        mask = _chunk_mask(tq, tq, 0, True)
        lane = lax.broadcasted_iota(jnp.int32, (tq, LANES), 1)
        ks = [k_ref[:, hs] for hs in head_lanes]
        vs = [v_ref[:, hs] for hs in head_lanes]
        kts = [kh.T for kh in ks]

        @pl.when(j == 0)
        def _():
            dq_acc[...] = jnp.zeros_like(dq_acc)

        def step(i, carry, masked):
            rows = pl.ds(pl.multiple_of(i * tq, tq), tq)
            heads = range(hps)
            stat = lambda h: (h // 2, slice(h % 2, h % 2 + 1), rows)
            qhs = [q_ref[rows, hs] for hs in head_lanes]
            doops = [doop_ref[rows, hs] for hs in head_lanes]
            sts = [_mm_nt(ks[h], qhs[h]) for h in heads]
            dpts = [_mm_nt(vs[h], doops[h]) for h in heads]
            pts = [jnp.exp2(sts[h] - lse_ref[stat(h)]) for h in heads]
            if masked:
                pts = [jnp.where(mask, pt, 0.0) for pt in pts]
            dsts = [(pts[h] * (dpts[h] - dcap_ref[stat(h)])).astype(BF16) for h in heads]
            dvs = [_mm(pts[h].astype(BF16), doops[h]) for h in heads]
            dks = [_mm(dsts[h], qhs[h]) for h in heads]
            for h, hs in enumerate(head_lanes):
                dq_acc[hs, rows] += _mm(kts[h], dsts[h])
            return tuple((dk + dks[h], dv + dvs[h]) for h, (dk, dv) in enumerate(carry))

        zero = jnp.zeros((tq, LANES), F32)
        carry = step(j, ((zero, zero),) * hps, True)
        res = lax.fori_loop(j + 1, nq, functools.partial(step, masked=False), carry)
        dkr = None
        for (dk, dv), hs in zip(res, head_lanes):
            dk = dk * LN2
            dkv_ref[:, hs] = jnp.where(lane < NOPE, dk, dv).astype(BF16)
            dkr = dk if dkr is None else dkr + dk
        dkr_ref[0] = jnp.where((lane >= NOPE) & (lane < NOPE + ROPE), dkr, 0.0)

        @pl.when(j == nq - 1)
        def _():
            dq_ref[...] = (dq_acc[...] * SCALE).T.astype(BF16)

        pl.when(tick == last)(rs.finish2)

    kspec = pl.BlockSpec((tq, hps * LANES), lambda p, j: (j, p))
    qspec = pl.BlockSpec((T, hps * LANES), lambda p, j: (0, p))
    rspec = pl.BlockSpec((hps // 2, 2, T), lambda p, j: (p, 0, 0))
    sums = [SDS(s.shape[1:], F32) for s in slabs] + [SDS(packed.shape, F32)]
    return pl.pallas_call(
        body, name="attn_bwd", grid=(groups, nq),
        in_specs=[qspec, kspec, kspec, qspec, rspec, rspec] + [HBM_SPEC] * n + [_full_spec(packed)],
        out_specs=[qspec, kspec, pl.BlockSpec((1, tq, LANES), lambda p, j: (p, j, 0))] + [_full_spec(s) for s in sums],
        out_shape=[SDS((T, HW), BF16), SDS((T, HW), BF16), SDS((groups, T, LANES), F32)] + sums,
        scratch_shapes=[pltpu.VMEM((hps * LANES, T), F32)] + _rs_scratch([s.shape for s in sums[:-1]], packed.shape),
        compiler_params=_cp(("arbitrary", "arbitrary")),
    )(q_att, k_att, v_att, doop, lse_rows, dcap_rows, *slabs, packed)


def _rms_bwd(z, gain, dout):
    r = lax.rsqrt(jnp.mean(z * z, axis=-1, keepdims=True) + EPS)
    zr = z * r
    u = dout * gain
    return r * (u - zr * jnp.mean(u * zr, axis=-1, keepdims=True)), jnp.sum(dout * zr, axis=0, keepdims=True)


def _mla_bwd(dq_att, dkv_nat, dkr4, zfr, q_norm, wuq_pad, kv_norm, wukv, rc, rsa, rsb, tm):
    T = dq_att.shape[0]

    def body(dq_ref, dkv_ref, dkr_ref, zfr_ref, qn_ref, wuq_ref, kvn_ref, wukv_ref, c_ref, sa_ref, sb_ref,
             dfr_ref, gwuq_ref, gwukv_ref, gqn_ref, gkvn_ref):
        @pl.when(pl.program_id(0) == 0)
        def _():
            for ref in (gwuq_ref, gwukv_ref, gqn_ref, gkvn_ref):
                ref[...] = jnp.zeros_like(ref)

        c, sa, sb = c_ref[...], sa_ref[...], sb_ref[...]
        zq, zkv = zfr_ref[:, :Q_RANK], zfr_ref[:, Q_RANK:Q_RANK + KV_RANK]
        qn, kvn = qn_ref[...], kvn_ref[...]
        dkv = dkv_ref[...]
        dckv = _mm_nt(dkv, wukv_ref[...])
        ckv = (zkv * lax.rsqrt(jnp.mean(zkv * zkv, axis=-1, keepdims=True) + EPS) * kvn).astype(BF16)
        gwukv_ref[...] += _mm_tn(ckv, dkv)
        cq = (zq * lax.rsqrt(jnp.mean(zq * zq, axis=-1, keepdims=True) + EPS) * qn).astype(BF16)
        dq = _rope(dq_ref[...].astype(F32), c, sa, sb, -1.0).astype(BF16)
        dzkv, gkvn = _rms_bwd(zkv, kvn, dckv)
        gkvn_ref[...] += gkvn
        gwuq_ref[...] += _mm_tn(cq, dq)
        dzq, gqn = _rms_bwd(zq, qn, _mm_nt(dq, wuq_ref[...]))
        gqn_ref[...] += gqn
        dkr = functools.reduce(lambda a, b: a + b, [dkr_ref[g] for g in range(dkr4.shape[0])])
        dfr_ref[:, :Q_RANK] = dzq.astype(BF16)
        dfr_ref[:, Q_RANK:Q_RANK + KV_RANK] = dzkv.astype(BF16)
        dfr_ref[:, Q_RANK + KV_RANK:] = _rope(dkr, c, sa, sb, -1.0).astype(BF16)

    ins = (dq_att, dkv_nat, dkr4, zfr, q_norm, wuq_pad, kv_norm, wukv, rc, rsa, rsb)
    in_specs = [_row_spec(tm, HW), _row_spec(tm, HW), pl.BlockSpec((dkr4.shape[0], tm, LANES), lambda i: (0, i, 0)), _row_spec(tm, FRONT_W),
                _full_spec(q_norm), _full_spec(wuq_pad), _full_spec(kv_norm), _full_spec(wukv),
                _row_spec(tm, LANES), _row_spec(tm, LANES), _row_spec(tm, LANES)]
    outs = [SDS((T, FRONT_W), BF16), SDS((Q_RANK, HW), F32), SDS((KV_RANK, HW), F32), SDS((1, Q_RANK), F32), SDS((1, KV_RANK), F32)]
    out_specs = [_row_spec(tm, FRONT_W)] + [_full_spec(s) for s in outs[1:]]
    return pl.pallas_call(
        body, name="mla_bwd", grid=(T // tm,), in_specs=in_specs, out_specs=out_specs, out_shape=outs,
        compiler_params=_cp(("arbitrary",)),
    )(*ins)


_DZ_COLS = ((GM, ZTOT), (GA, UP), (UP, GP), (GP, GM), (ZQ, GA))


def _in_proj_bwd_x(dzs, x2, dh, norm_in, w_in_pad, tm, slabs):
    T = x2.shape[0]
    steps = T // tm
    n = len(slabs)

    def body(d0, d1, d2, d3, d4, x_ref, dh_ref, nin_ref, win_ref, *rest):
        slab_refs, (gx_ref, gnin_ref), sum_refs = rest[:n], rest[n:n + 2], rest[n + 2:2 * n + 2]
        rs = _ReduceScatter(slab_refs, None, sum_refs, None, rest[2 * n + 2:])
        step = pl.program_id(0)

        @pl.when(step == 0)
        def _():
            gnin_ref[...] = jnp.zeros_like(gnin_ref)
            rs.start1()

        pl.when(step == min(2, steps - 1))(rs.finish1_start2)
        pl.when(step == min(steps * 11 // 16, steps - 1))(rs.relay2)
        dhn = None
        for ref, (lo, hi) in zip((d0, d1, d2, d3, d4), _DZ_COLS):
            t = _mm(ref[...], win_ref[lo:hi, :])
            dhn = t if dhn is None else dhn + t
        dx, gnin = _rms_bwd(x_ref[...], nin_ref[...], dhn)
        gnin_ref[...] += gnin
        gx_ref[...] = dx + dh_ref[...]
        pl.when(step == steps - 1)(rs.finish2)

    in_specs = [_row_spec(tm, hi - lo) for lo, hi in _DZ_COLS] + [_row_spec(tm, D_MODEL), _row_spec(tm, D_MODEL),
                                                                  _full_spec(norm_in), _full_spec(w_in_pad)] + [HBM_SPEC] * n
    sums = [SDS(s.shape[1:], F32) for s in slabs]
    outs = [SDS((T, D_MODEL), F32), SDS((1, D_MODEL), F32)] + sums
    return pl.pallas_call(
        body, name="in_proj_bwd_x", grid=(steps,), in_specs=in_specs,
        out_specs=[_row_spec(tm, D_MODEL), _full_spec(outs[1])] + [_full_spec(s) for s in sums],
        out_shape=outs, scratch_shapes=_rs_scratch([s.shape for s in sums], None), compiler_params=_cp(("arbitrary",)),
    )(*dzs, x2, dh, norm_in, w_in_pad, *slabs)


SLAB_ROWS = IN_TOTAL // N_DEV


def _slab_segments(k):
    cuts = [(0, ZKR_ORIG, 0), (ZKR_ORIG, ZKR_ORIG + ROPE, NOPE), (ZKR_ORIG + ROPE, IN_TOTAL, LANES - ROPE)]
    lo, hi = k * SLAB_ROWS, (k + 1) * SLAB_ROWS
    return [(max(lo, a) - lo, max(lo, a) + shift, min(hi, b) - max(lo, a)) for a, b, shift in cuts if min(hi, b) > max(lo, a)]


def _in_proj_bwd_w(dzs, hn, tm):
    T = hn.shape[0]
    steps = T // tm

    def body(d0, d1, d2, d3, d4, hn_ref, slab_ref, acc_ref):
        @pl.when(pl.program_id(0) == 0)
        def _():
            acc_ref[...] = jnp.zeros_like(acc_ref)

        hn_v = hn_ref[...]
        for ref, (lo, hi) in zip((d0, d1, d2, d3, d4), _DZ_COLS):
            acc_ref[lo:hi, :] += _mm_tn(ref[...], hn_v)

        @pl.when(pl.program_id(0) == steps - 1)
        def _():
            for k in range(N_DEV):
                for at, src, rows in _slab_segments(k):
                    slab_ref[k, at:at + rows, :] = acc_ref[src:src + rows, :].astype(BF16)

    in_specs = [_row_spec(tm, hi - lo) for lo, hi in _DZ_COLS] + [_row_spec(tm, D_MODEL)]
    out = SDS((N_DEV, SLAB_ROWS, D_MODEL), BF16)
    return pl.pallas_call(
        body, name="in_proj_bwd_w", grid=(steps,), in_specs=in_specs, out_specs=_full_spec(out), out_shape=out,
        scratch_shapes=[pltpu.VMEM((ZTOT, D_MODEL), F32)], compiler_params=_cp(("arbitrary",)),
    )(*dzs, hn)


def _local_step(x2, tgt, norm_in, w_in_pad, q_norm, w_uq, kv_norm, w_ukv, pool_w, pool_scale, late_shards, norm_final):
    T = x2.shape[0]
    tm, tm_small, tq = min(512, T), min(256, T), min(512, T)
    heads_fwd, heads_bwd = 4, 4
    row = lambda v: v.reshape(1, -1)
    wuq_pad = jnp.pad(w_uq, ((0, 0), (0, 0), (0, HEAD_PAD - NOPE - ROPE))).reshape(Q_RANK, HW)
    wukv = w_ukv.reshape(KV_RANK, HW)
    rc, rsa, rsb = _rope_tables(T)

    hn, zgm, zga, zup, zgp, zfr, q_att, k_att, v_att, vt_att, w_ba, w_bp, w_out = _in_proj(
        x2, row(norm_in), w_in_pad, row(q_norm), wuq_pad, row(kv_norm), wukv, rc, rsa, rsb, tm, late_shards)
    w_out = w_out.reshape(D_MODEL, D_MODEL)
    o, lse_rows = _attn_fwd(q_att, k_att, vt_att, tq, heads_fwd)
    ypool = _pool_fwd(zup, zgp, pool_w, row(pool_scale))
    loss8, dh, dgm, doop, dga, dcap_rows, dyp, *slabs, g_nf = _tail(
        x2, tgt, o, zga, ypool, zgm, w_ba, w_bp, w_out, row(norm_final), tm_small)
    dup, dgp, g_pool_w, g_pool_scale = _pool_bwd(zup, zgp, dyp, pool_w, row(pool_scale))

    bf = lambda a: a.astype(BF16)
    early = [g_pool_w, g_pool_scale, g_nf, loss8[0]]
    packed = jnp.concatenate([_pack_rows(a) for a in early], axis=0)
    dq_att, dkv_nat, dkr4, s_wout, s_wba, s_wbp, tot_early = _attn_bwd(
        q_att, k_att, v_att, doop, lse_rows, dcap_rows, tq, heads_bwd, slabs, packed)
    s_pool_w, s_pool_scale, s_nf, s_loss = _unpack_rows(tot_early, early)

    dfr, g_wuq_pad, g_wukv, g_qn, g_kvn = _mla_bwd(
        dq_att, dkv_nat, dkr4, zfr, row(q_norm), wuq_pad, row(kv_norm), wukv, rc, rsa, rsb, tm)
    dzs = (dgm, dga, dup, dgp, dfr)
    slabs = [_in_proj_bwd_w(dzs, hn, tm),
             bf(g_wuq_pad.reshape(N_DEV, Q_RANK // N_DEV, HEADS, HEAD_PAD)[..., :NOPE + ROPE]),
             bf(g_wukv).reshape(N_DEV, KV_RANK // N_DEV, HW)]
    grad_x, g_nin, s_win, s_wuq, s_wukv = _in_proj_bwd_x(dzs, x2, dh, row(norm_in), w_in_pad, tm_small, slabs)
    late = [g_nin, g_qn, g_kvn]
    (tot_late,) = _reduce_scatter([], jnp.concatenate([_pack_rows(a) for a in late], axis=0))
    s_nin, s_qn, s_kvn = _unpack_rows(tot_late, late)

    grads = dict(norm_in=s_nin, w_in=s_win, q_norm=s_qn, w_uq=s_wuq, kv_norm=s_kvn, w_ukv=s_wukv, pool_w=s_pool_w.reshape(-1, GROUP),
                 pool_scale=s_pool_scale, w_branch_attn=s_wba, w_branch_pool=s_wbp, w_out=s_wout, norm_final=s_nf)
    return s_loss[0], grad_x, grads


MESH_ID = pl.DeviceIdType.MESH
VMEM_SPEC = pl.BlockSpec(memory_space=pltpu.VMEM)
HBM_SPEC = pl.BlockSpec(memory_space=pl.ANY)


def _mesh_pos():
    return lax.axis_index("x"), lax.axis_index("y"), lax.axis_index("c")


BF16_TILE_ROWS = 16


def _half_rows(rows):
    cut = -(-(rows // 2) // BF16_TILE_ROWS) * BF16_TILE_ROWS
    return pl.ds(0, cut), pl.ds(cut, rows - cut)


def _slot(px, py, pc):
    return 4 * px + 2 * py + pc


def _staged_shape(shape):
    return (shape[0], shape[1] * shape[2]) if len(shape) == 3 and shape[2] == LANES else tuple(shape)


def _all_gather_bf16(shards):
    n = len(shards)
    staged = [_staged_shape(s.shape) for s in shards]

    def body(*refs):
        ins, outs = refs[:n], refs[n:2 * n]
        land0, scratch = refs[2 * n], refs[2 * n + 1:]
        wpad_ref = outs[0]
        ag = _AllGather(ins, (land0,) + tuple(outs[1:]), scratch)
        ag.start()
        ag.forward()
        ag.finish()
        wpad_ref[ZKR:GA, :] = jnp.zeros((GA - ZKR, D_MODEL), BF16)
        for k in range(N_DEV):
            for at, dst, rows in _slab_segments(k):
                wpad_ref[dst:dst + rows, :] = land0[k, at:at + rows, :]

    return pl.pallas_call(
        body, name="all_gather_weights",
        in_specs=[VMEM_SPEC] * n, out_specs=[VMEM_SPEC] + [HBM_SPEC] * (n - 1),
        out_shape=[SDS((ZTOT, D_MODEL), BF16)] + [SDS((N_DEV,) + s, BF16) for s in staged[1:]],
        scratch_shapes=[pltpu.VMEM((N_DEV,) + staged[0], BF16)] + _ag_scratch(staged),
        compiler_params=_cp(),
    )(*shards)


def _ag_scratch(shapes):
    n = len(shapes)
    dma = pltpu.SemaphoreType.DMA
    return [pltpu.VMEM(tuple(s), BF16) for s in shapes] + [dma((_AllGather.COPIES * n,)), dma((_AllGather.COPIES * n,)), dma((n,))]


class _AllGather:
    COPIES = 8

    def __init__(self, in_refs, dest_refs, scratch):
        n = self.n = len(in_refs)
        self.ins, self.dests, self.stage = in_refs, dest_refs, scratch[:n]
        self.send_sems, self.recv_sems, self.local_sems = scratch[n:]
        x, y, c = _mesh_pos()
        self.c, self.me, self.sibling = c, (x, y, c), (x, y, 1 - c)
        self.xn, self.yn, self.diag = (1 - x, y), (x, 1 - y), (1 - x, 1 - y)

    def _halves(self, a):
        return _half_rows(self.stage[a].shape[0])

    def _copy(self, a, k, block, to, from_stage=False, rows=None):
        dst = self.dests[a].at[_slot(*block)]
        src = self.stage[a] if from_stage else dst
        if rows is not None:
            src, dst = src.at[rows], dst.at[rows]
        return pltpu.make_async_remote_copy(
            src_ref=src, dst_ref=dst, send_sem=self.send_sems.at[self.COPIES * a + k],
            recv_sem=self.recv_sems.at[self.COPIES * a + k], device_id=to, device_id_type=MESH_ID)

    def _mine(self):
        return [pltpu.make_async_copy(self.stage[a], self.dests[a].at[_slot(*self.me)], self.local_sems.at[a]) for a in range(self.n)]

    def _first(self, a):
        return [self._copy(a, 0, self.me, self.sibling, True), self._copy(a, 1, self.me, (*self.xn, self.c), True),
                self._copy(a, 2, self.me, (*self.yn, self.c), True)]

    def _relays(self, a):
        lo, hi = self._halves(a)
        return [self._copy(a, 3, (*self.xn, self.c), (*self.yn, self.c), rows=lo),
                self._copy(a, 4, (*self.yn, self.c), (*self.xn, self.c), rows=hi)]

    def _passes(self, a):
        return [self._copy(a, 5 + j, (*chip, self.c), self.sibling) for j, chip in enumerate((self.xn, self.yn, self.diag))]

    def start(self):
        for a in range(self.n):
            src, dst = self.ins[a], self.stage[a]
            if src.shape == dst.shape:
                dst[...] = src[...].astype(BF16)
            else:
                for h in range(src.shape[1]):
                    dst[:, h * LANES:(h + 1) * LANES] = src[:, h, :].astype(BF16)
        for cp in self._mine():
            cp.start()
        for a in range(self.n):
            for cp in self._first(a):
                cp.start()

    def forward(self):
        for a in range(self.n):
            relays, passes = self._relays(a), self._passes(a)
            for j, chip in enumerate((self.xn, self.yn)):
                self._copy(a, 1 + j, (*chip, self.c), self.me).wait_recv()
                relays[j].start()
                passes[j].start()

    def finish(self):
        for a in range(self.n):
            lo, hi = self._halves(a)
            self._copy(a, 3, (*self.diag, self.c), self.me, rows=lo).wait_recv()
            self._copy(a, 4, (*self.diag, self.c), self.me, rows=hi).wait_recv()
            self._passes(a)[2].start()
        for a in range(self.n):
            self._copy(a, 0, self.sibling, self.me).wait_recv()
            for j, chip in enumerate((self.xn, self.yn, self.diag)):
                self._copy(a, 5 + j, (*chip, 1 - self.c), self.me).wait_recv()
            for cp in self._first(a) + self._relays(a) + self._passes(a):
                cp.wait_send()
        for cp in self._mine():
            cp.wait()


N_CHIPS = 4


def _reduce_scatter(slabs, packed):
    def body(*refs):
        n = len(slabs)
        rs = _ReduceScatter(refs[:n], refs[n], refs[n + 1:2 * n + 1], refs[2 * n + 1], refs[2 * n + 2:])
        rs.start1()
        rs.finish1_start2()
        rs.relay2()
        rs.finish2()

    shapes = [s.shape[1:] for s in slabs]
    return pl.pallas_call(
        body, name="reduce_scatter_grads",
        in_specs=[HBM_SPEC] * len(slabs) + [VMEM_SPEC], out_specs=[VMEM_SPEC] * (len(slabs) + 1),
        out_shape=[SDS(s, F32) for s in shapes] + [SDS(packed.shape, F32)],
        scratch_shapes=_rs_scratch(shapes, packed.shape), compiler_params=_cp(),
    )(*slabs, packed)


def _rs_scratch(shapes, packed_shape):
    n = len(shapes)
    n1, n2 = N_CHIPS * n + 1, _ReduceScatter.L2_COPIES * n + N_CHIPS - 1
    dma = pltpu.SemaphoreType.DMA
    packed = [] if packed_shape is None else [pltpu.VMEM(packed_shape, F32), pltpu.VMEM((N_CHIPS,) + tuple(packed_shape), F32)]
    return ([pltpu.VMEM((N_CHIPS,) + tuple(s), BF16) for s in shapes] * 2 + [pltpu.VMEM((N_CHIPS - 1,) + tuple(s), BF16) for s in shapes] * 2
            + packed + [dma((max(N_CHIPS * n, 1),)), dma((n1,)), dma((n1,)), dma((n2,)), dma((n2,))])


class _ReduceScatter:
    L2_COPIES = 6

    def __init__(self, slab_refs, packed_ref, out_refs, ptot_ref, scratch):
        n = self.n = len(slab_refs)
        self.slabs, self.packed, self.outs, self.ptot = slab_refs, packed_ref, out_refs, ptot_ref
        self.own1, self.land1, self.send2, self.land2 = (scratch[k * n:(k + 1) * n] for k in range(4))
        rest = scratch[4 * n:]
        if packed_ref is not None:
            self.pland1, self.pland2 = rest[:2]
            rest = rest[2:]
        self.loc_sems, self.send1_sems, self.recv1_sems, self.send2_sems, self.recv2_sems = rest
        self.x, self.y, self.c = _mesh_pos()

    def _chip(self, r):
        return (1 - self.x if r & 2 else self.x, 1 - self.y if r & 1 else self.y)

    @staticmethod
    def _remote(src, dst, send_sem, recv_sem, to):
        return pltpu.make_async_remote_copy(src_ref=src, dst_ref=dst, send_sem=send_sem, recv_sem=recv_sem, device_id=to,
                                            device_id_type=MESH_ID)

    def _copies1(self):
        c, sibling = self.c, (self.x, self.y, 1 - self.c)
        cps = []
        for a in range(self.n):
            for r in range(N_CHIPS):
                k = N_CHIPS * a + r
                cps.append(pltpu.make_async_copy(self.slabs[a].at[_slot(*self._chip(r), c)], self.own1[a].at[r], self.loc_sems.at[k]))
                cps.append(self._remote(self.slabs[a].at[_slot(*self._chip(r), 1 - c)], self.land1[a].at[r],
                                        self.send1_sems.at[k], self.recv1_sems.at[k], sibling))
        if self.packed is not None:
            k = N_CHIPS * self.n
            cps.append(self._remote(self.packed, self.pland1, self.send1_sems.at[k], self.recv1_sems.at[k], sibling))
        return cps

    def _halves(self, a):
        return _half_rows(self.send2[a].shape[1])

    def _copy2(self, a, k):
        lo, hi = self._halves(a)
        xn, yn = (*self._chip(2), self.c), (*self._chip(1), self.c)
        slot, rows, to = [(1, lo, xn), (2, lo, xn), (0, hi, yn), (2, hi, yn), (0, lo, yn), (1, hi, xn)][k]
        return self._remote(self.send2[a].at[slot].at[rows], self.land2[a].at[slot].at[rows],
                            self.send2_sems.at[self.L2_COPIES * a + k], self.recv2_sems.at[self.L2_COPIES * a + k], to)

    def _copies2_packed(self):
        base = self.L2_COPIES * self.n - 1
        return [self._remote(self.pland2.at[0], self.pland2.at[r], self.send2_sems.at[base + r], self.recv2_sems.at[base + r],
                             (*self._chip(r), self.c)) for r in range(1, N_CHIPS)]

    def start1(self):
        for cp in self._copies1():
            cp.start()

    def finish1_start2(self):
        for cp in self._copies1():
            cp.wait()
        for a in range(self.n):
            self.outs[a][...] = self.own1[a][0].astype(F32) + self.land1[a][0].astype(F32)
            for r in range(1, N_CHIPS):
                self.send2[a][r - 1] = (self.own1[a][r].astype(F32) + self.land1[a][r].astype(F32)).astype(BF16)
            for k in range(4):
                self._copy2(a, k).start()
        if self.packed is not None:
            self.pland2[0] = self.packed[...] + self.pland1[...]
            for cp in self._copies2_packed():
                cp.start()

    def relay2(self):
        for a in range(self.n):
            lo, hi = self._halves(a)
            s2, l2 = self.send2[a], self.land2[a]
            self._copy2(a, 1).wait_recv()
            s2[0, lo] = (s2[0, lo].astype(F32) + l2[2, lo].astype(F32)).astype(BF16)
            self._copy2(a, 4).start()
            self._copy2(a, 3).wait_recv()
            s2[1, hi] = (s2[1, hi].astype(F32) + l2[2, hi].astype(F32)).astype(BF16)
            self._copy2(a, 5).start()

    def finish2(self):
        for a in range(self.n):
            for k in (0, 2, 4, 5):
                self._copy2(a, k).wait_recv()
            for k in range(self.L2_COPIES):
                self._copy2(a, k).wait_send()
            l2 = self.land2[a]
            self.outs[a][...] = self.outs[a][...] + (l2[0].astype(F32) + l2[1].astype(F32))
        if self.packed is not None:
            for cp in self._copies2_packed():
                cp.wait()
            p2 = self.pland2
            self.ptot[...] = (p2[0] + p2[1]) + (p2[2] + p2[3])


def _adamw(ws, gs, ms, vs):
    n = len(ws)
    regrouped = [k for k in range(n) if gs[k].shape != ws[k].shape]

    def body(*refs):
        for k in range(n):
            w_ref, g_ref, m_ref, v_ref, d_ref, nm_ref, nv_ref = (refs[j * n + k] for j in range(7))
            if k in regrouped:
                windows = [((slice(None), h), (slice(None), slice(h * LANES, (h + 1) * LANES))) for h in range(w_ref.shape[1])]
                g_out_ref = refs[7 * n + regrouped.index(k)]
            else:
                windows = [(..., ...)]
            for at, g_at in windows:
                w, g, m, v = w_ref[at], g_ref[g_at], m_ref[at], v_ref[at]
                m = ADAM_B1 * m + (1.0 - ADAM_B1) * g
                v = ADAM_B2 * v + (1.0 - ADAM_B2) * jnp.square(g)
                m_hat = m / (1.0 - ADAM_B1 ** ADAM_STEP)
                v_hat = v / (1.0 - ADAM_B2 ** ADAM_STEP)
                d_ref[at] = -ADAM_LR * (m_hat / (jnp.sqrt(v_hat) + ADAM_EPS) + ADAM_WD * w)
                nm_ref[at] = m
                nv_ref[at] = v
                if k in regrouped:
                    g_out_ref[at] = g

    shapes = [SDS(w.shape, F32) for w in ws]
    outs = pl.pallas_call(
        body, name="adamw", in_specs=[VMEM_SPEC] * (4 * n), out_specs=[VMEM_SPEC] * (3 * n + len(regrouped)),
        out_shape=shapes * 3 + [shapes[k] for k in regrouped], compiler_params=_cp(),
    )(*ws, *gs, *ms, *vs)
    grads = list(gs)
    for k, g in zip(regrouped, outs[3 * n:]):
        grads[k] = g
    return outs[:n], outs[n:2 * n], outs[2 * n:3 * n], grads


WEIGHTS = ("norm_in", "w_in", "q_norm", "w_uq", "kv_norm", "w_ukv", "pool_w", "pool_scale", "w_branch_attn", "w_branch_pool",
           "w_out", "norm_final")
SUBLANES = 8


def _pack_rows(a):
    a = a.reshape(-1, LANES)
    return jnp.pad(a, ((0, -a.shape[0] % SUBLANES), (0, 0)))


def _unpack_rows(packed, like):
    out, row = [], 0
    for a in like:
        rows = a.size // LANES
        out.append(packed[row:row + rows].reshape(a.shape))
        row += rows + (-rows % SUBLANES)
    return out


def kernel(x, norm_in, w_in, q_norm, w_uq, kv_norm, w_ukv, pool_w, pool_scale, w_branch_attn, w_branch_pool, w_out, norm_final, loss_target, m_norm_in, m_w_in, m_q_norm, m_w_uq, m_kv_norm, m_w_ukv, m_pool_w, m_pool_scale, m_w_branch_attn, m_w_branch_pool, m_w_out, m_norm_final, v_norm_in, v_w_in, v_q_norm, v_w_uq, v_kv_norm, v_w_ukv, v_pool_w, v_pool_scale, v_w_branch_attn, v_w_branch_pool, v_w_out, v_norm_final):
    w = dict(norm_in=norm_in, w_in=w_in, q_norm=q_norm, w_uq=w_uq, kv_norm=kv_norm, w_ukv=w_ukv, pool_w=pool_w, pool_scale=pool_scale,
             w_branch_attn=w_branch_attn, w_branch_pool=w_branch_pool, w_out=w_out, norm_final=norm_final)
    m = dict(norm_in=m_norm_in, w_in=m_w_in, q_norm=m_q_norm, w_uq=m_w_uq, kv_norm=m_kv_norm, w_ukv=m_w_ukv, pool_w=m_pool_w,
             pool_scale=m_pool_scale, w_branch_attn=m_w_branch_attn, w_branch_pool=m_w_branch_pool, w_out=m_w_out, norm_final=m_norm_final)
    v = dict(norm_in=v_norm_in, w_in=v_w_in, q_norm=v_q_norm, w_uq=v_w_uq, kv_norm=v_kv_norm, w_ukv=v_w_ukv, pool_w=v_pool_w,
             pool_scale=v_pool_scale, w_branch_attn=v_w_branch_attn, w_branch_pool=v_w_branch_pool, w_out=v_w_out, norm_final=v_norm_final)

    def as2d(name, a):
        if name == "w_in":
            return a.T
        if name in ("w_uq", "w_ukv"):
            return a
        return a.reshape(-1, GROUP) if name == "pool_w" else a

    def unshape(name, a):
        return a.T if name == "w_in" else a.reshape(w[name].shape)

    def like_weight(name, g):
        return g.reshape(-1) if w[name].ndim == 1 else g

    w_in_pad, w_uq_full, w_ukv_full = _all_gather_bf16([as2d(k, w[k]) for k in ("w_in", "w_uq", "w_ukv")])
    loss, grad_x, g2d = _local_step(
        x.reshape(x.shape[1:]), loss_target.reshape(x.shape[1:]), norm_in, w_in_pad, q_norm,
        w_uq_full.reshape(Q_RANK, HEADS, NOPE + ROPE), kv_norm, w_ukv_full.reshape(KV_RANK, HEADS, NOPE + VDIM),
        pool_w, pool_scale, [w_branch_attn, w_branch_pool, w_out], norm_final)

    deltas, new_m, new_v, grads = _adamw([as2d(k, w[k]) for k in WEIGHTS], [like_weight(k, g2d[k]) for k in WEIGHTS],
                                         [as2d(k, m[k]) for k in WEIGHTS], [as2d(k, v[k]) for k in WEIGHTS])
    shaped = lambda arrs: [unshape(k, a) for k, a in zip(WEIGHTS, arrs)]
    return (loss, grad_x.reshape(x.shape), *shaped(grads), *shaped(deltas), *shaped(new_m), *shaped(new_v))
```

```python
import functools

import jax
import jax.numpy as jnp
import numpy as np
from jax import lax
from jax.experimental import pallas as pl
from jax.experimental.pallas import tpu as pltpu

F32 = jnp.float32
BF16 = jnp.bfloat16
SDS = jax.ShapeDtypeStruct

D_MODEL = 1024
HEADS = 8
NOPE = 64
ROPE = 32
VDIM = 64
Q_RANK = 384
KV_RANK = 256
MLA_W = HEADS * VDIM
POOL_W = 512
POOL_GROUPS = 4
GROUP = POOL_W // POOL_GROUPS
CHUNK = 64
ROPE_THETA = 10000.0
EPS = 1e-6
SCALE = (NOPE + ROPE) ** -0.5
LOG2E = 1.4426950408889634
LN2 = 0.6931471805599453
QK_SCALE_LOG2 = SCALE * LOG2E
IN_TOTAL = 4256
ADAM_LR, ADAM_B1, ADAM_B2, ADAM_EPS, ADAM_WD, ADAM_STEP = 0.001, 0.9, 0.999, 1e-08, 0.01, 10

N_DEV = 8
LANES = 128
HEAD_PAD = LANES
HW = HEADS * HEAD_PAD

ZQ, ZKV, ZKR, GA, UP, GP, GM, ZTOT = 0, 384, 640, 768, 1280, 1792, 2304, 4352
FRONT_W = GA
ZKR_ORIG = 640

VMEM_LIMIT = 62 * 1024 * 1024


def _cp(sem=None, **kw):
    if sem is not None:
        kw["dimension_semantics"] = sem
    return pltpu.CompilerParams(vmem_limit_bytes=VMEM_LIMIT, **kw)


def _mm(a, b):
    return lax.dot_general(a, b, (((1,), (0,)), ((), ())), preferred_element_type=F32)


def _mm_nt(a, b):
    return lax.dot_general(a, b, (((1,), (1,)), ((), ())), preferred_element_type=F32)


def _mm_tn(a, b):
    return lax.dot_general(a, b, (((0,), (0,)), ((), ())), preferred_element_type=F32)


def _row_spec(tm, w):
    return pl.BlockSpec((tm, w), lambda i: (i, 0))


def _full_spec(a):
    nd = len(a.shape)
    return pl.BlockSpec(a.shape, lambda *_: (0,) * nd)


def _rope(v, c, sa, sb, sign):
    n = v.shape[-1]
    reps = n // LANES
    if reps > 1:
        c, sa, sb = (jnp.tile(t, (1, reps)) for t in (c, sa, sb))
    up = pltpu.roll(v, n - ROPE // 2, 1)
    dn = pltpu.roll(v, ROPE // 2, 1)
    return v * c + sign * (up * sa + dn * sb)


def _rope_tables(T):
    half = ROPE // 2
    inv_freq = np.float32(ROPE_THETA) ** (-np.arange(half, dtype=np.float32) / np.float32(half))
    ang = np.arange(T, dtype=np.float32)[:, None] * inv_freq[None, :].astype(np.float32)
    cos, sin = np.cos(ang.astype(np.float64)).astype(np.float32), np.sin(ang.astype(np.float64)).astype(np.float32)
    z16 = np.zeros((T, half), np.float32)
    z32 = np.zeros((T, LANES - NOPE - ROPE), np.float32)
    c = np.concatenate([np.ones((T, NOPE), np.float32), cos, cos, z32], axis=1)
    sa = np.concatenate([np.zeros((T, NOPE), np.float32), -sin, z16, z32], axis=1)
    sb = np.concatenate([np.zeros((T, NOPE), np.float32), z16, sin, z32], axis=1)
    return jnp.asarray(c), jnp.asarray(sa), jnp.asarray(sb)


def _silu_parts(g):
    sg = jax.nn.sigmoid(g)
    return g * sg, sg + g * sg * (1.0 - sg)


def _in_proj(x2, norm_in, w_in_pad, q_norm, wuq_pad, kv_norm, wukv, rc, rsa, rsb, tm, late_shards):
    T = x2.shape[0]
    steps = T // tm
    n = len(late_shards)

    def body(x_ref, nin_ref, win_ref, qn_ref, wuq_ref, kvn_ref, wukv_ref, c_ref, sa_ref, sb_ref, *rest):
        hn_ref, zgm_ref, zga_ref, zup_ref, zgp_ref, zfr_ref, q_ref, k_ref, v_ref, vt_ref = rest[n:n + 10]
        ag = _AllGather(rest[:n], rest[n + 10:2 * n + 10], rest[2 * n + 10:])
        step = pl.program_id(0)
        pl.when(step == 0)(ag.start)
        pl.when(step == min(3, steps - 1))(ag.forward)
        xf = x_ref[...]
        r = lax.rsqrt(jnp.mean(xf * xf, axis=-1, keepdims=True) + EPS)
        hn = (xf * r * nin_ref[...]).astype(BF16)
        hn_ref[...] = hn
        zfr = _mm_nt(hn, win_ref[ZQ:GA, :])
        zfr_ref[...] = zfr
        zq, zkv, zkr = zfr[:, ZQ:ZKV], zfr[:, ZKV:ZKR], zfr[:, ZKR:GA]
        c, sa, sb = c_ref[...], sa_ref[...], sb_ref[...]
        rq = lax.rsqrt(jnp.mean(zq * zq, axis=-1, keepdims=True) + EPS)
        cq = (zq * rq * qn_ref[...]).astype(BF16)
        rkv = lax.rsqrt(jnp.mean(zkv * zkv, axis=-1, keepdims=True) + EPS)
        ckv = (zkv * rkv * kvn_ref[...]).astype(BF16)
        zga_ref[...] = _mm_nt(hn, win_ref[GA:UP, :])
        zup_ref[...] = _mm_nt(hn, win_ref[UP:GP, :])
        zgp_ref[...] = _mm_nt(hn, win_ref[GP:GM, :])
        q_raw = _mm(cq, wuq_ref[...])
        kv = _mm(ckv, wukv_ref[...])
        zgm_ref[...] = _mm_nt(hn, win_ref[GM:ZTOT, :])
        q = _rope(q_raw, c, sa, sb, 1.0)
        q_ref[...] = (q * QK_SCALE_LOG2).astype(BF16)
        kr = _rope(zkr, c, sa, sb, 1.0)
        lane = lax.broadcasted_iota(jnp.int32, kv.shape, 1) % LANES
        k_ref[...] = jnp.where(lane < NOPE, kv, jnp.tile(kr, (1, HEADS))).astype(BF16)
        v = jnp.where(lane < NOPE, 1.0, kv).astype(BF16)
        v_ref[...] = v
        vt_ref[...] = v.T
        pl.when(step == steps - 1)(ag.finish)

    ins = (x2, norm_in, w_in_pad, q_norm, wuq_pad, kv_norm, wukv, rc, rsa, rsb)
    in_specs = [_row_spec(tm, D_MODEL), _full_spec(norm_in), _full_spec(w_in_pad), _full_spec(q_norm), _full_spec(wuq_pad),
                _full_spec(kv_norm), _full_spec(wukv), _row_spec(tm, LANES), _row_spec(tm, LANES), _row_spec(tm, LANES)]
    widths = [(D_MODEL, BF16), (ZTOT - GM, F32), (UP - GA, F32), (GP - UP, F32), (GM - GP, F32), (FRONT_W, F32),
              (HW, BF16), (HW, BF16), (HW, BF16)]
    return pl.pallas_call(
        body, name="in_proj", grid=(steps,), in_specs=in_specs + [_full_spec(s) for s in late_shards],
        out_specs=[_row_spec(tm, w) for w, _ in widths] + [pl.BlockSpec((HW, tm), lambda i: (0, i))] + [HBM_SPEC] * n,
        out_shape=[SDS((T, w), dt) for w, dt in widths] + [SDS((HW, T), BF16)]
        + [SDS((N_DEV,) + s.shape, BF16) for s in late_shards],
        scratch_shapes=_ag_scratch([s.shape for s in late_shards]), compiler_params=_cp(("arbitrary",)),
    )(*ins, *late_shards)


def _chunk_mask(n_q, n_k, q_off, transposed):
    shape = (n_k, n_q) if transposed else (n_q, n_k)
    q = (lax.broadcasted_iota(jnp.int32, shape, 1 if transposed else 0) + q_off) // CHUNK
    k = lax.broadcasted_iota(jnp.int32, shape, 0 if transposed else 1) // CHUNK
    return k <= q


def _store_pair_rows(ref, k, pair):
    t = pair.T
    ref[k, 0:1, :] = t[0:1, :]
    ref[k, 1:2, :] = t[VDIM:VDIM + 1, :]


def _attn_fwd(q_att, k_att, vt_att, tq, hps):
    T = q_att.shape[0]
    head_lanes = [slice(h * LANES, (h + 1) * LANES) for h in range(hps)]

    def body(q_ref, k_ref, vt_ref, o_ref, lser_ref):
        i = pl.program_id(1)
        mask = _chunk_mask(tq, tq, 0, True)
        lane = lax.broadcasted_iota(jnp.int32, (tq, LANES), 1)
        qs = [q_ref[:, hs] for hs in head_lanes]

        def step(j, carry, masked):
            off = pl.multiple_of(j * tq, tq)
            sts = [_mm_nt(k_ref[pl.ds(off, tq), hs], qh) for qh, hs in zip(qs, head_lanes)]
            if masked:
                sts = [jnp.where(mask, st, -jnp.inf) for st in sts]
            ms = [jnp.maximum(m, jnp.max(st, axis=0, keepdims=True)) for (m, _), st in zip(carry, sts)]
            pts = [jnp.exp2(st - m_new).astype(BF16) for st, m_new in zip(sts, ms)]
            return tuple((m_new, jnp.exp2(m - m_new) * acc + _mm(vt_ref[hs, pl.ds(off, tq)], pt))
                         for (m, acc), m_new, pt, hs in zip(carry, ms, pts, head_lanes))

        init = ((jnp.full((1, tq), -jnp.inf, F32), jnp.zeros((LANES, tq), F32)),) * hps
        res = step(i, lax.fori_loop(0, i, functools.partial(step, masked=False), init), True)
        for pair in range(hps // 2):
            (ma, acca), (mb, accb) = res[2 * pair], res[2 * pair + 1]
            la, lb = acca[:1], accb[:1]
            oa, ob = (acca / la).T, (accb / lb).T
            o_ref[:, pair * LANES:(pair + 1) * LANES] = jnp.where(lane < VDIM, pltpu.roll(oa, VDIM, 1), ob)
            lser_ref[pair, 0:1, :] = ma + jnp.log2(la)
            lser_ref[pair, 1:2, :] = mb + jnp.log2(lb)

    qspec = pl.BlockSpec((tq, hps * LANES), lambda p, i: (i, p))
    kspec = pl.BlockSpec((T, hps * LANES), lambda p, i: (0, p))
    vspec = pl.BlockSpec((hps * LANES, T), lambda p, i: (p, 0))
    ospec = pl.BlockSpec((tq, hps * VDIM), lambda p, i: (i, p))
    return pl.pallas_call(
        body, name="attn_fwd", grid=(HEADS // hps, T // tq), in_specs=[qspec, kspec, vspec],
        out_specs=[ospec, pl.BlockSpec((hps // 2, 2, tq), lambda p, i: (p, 0, i))],
        out_shape=[SDS((T, MLA_W), F32), SDS((HEADS // 2, 2, T), F32)],
        compiler_params=_cp(("parallel", "parallel")),
    )(q_att, k_att, vt_att)


def _pick(g, vals):
    out = vals[-1]
    for k in range(len(vals) - 2, -1, -1):
        out = jnp.where(g == k, vals[k], out)
    return out


def _window_sum(u, g, forward):
    T = u.shape[0]
    row = lax.broadcasted_iota(jnp.int32, u.shape, 0)

    def sh(s, k):
        if forward:
            return jnp.where(row >= k, pltpu.roll(s, k, 0), 0.0)
        return jnp.where(row < T - k, pltpu.roll(s, T - k, 0), 0.0)

    sums, s = [], u
    for k in (1, 2, 4, 8):
        s = s + sh(s, k)
        sums.append(s)
    return _pick(g, sums)


MAX_WINDOW = 16


def _pool_inv_count(shape, g):
    T, n = shape
    row = lax.broadcasted_iota(jnp.int32, (MAX_WINDOW, n), 0)
    head = 1.0 / jnp.minimum(row + 1, lax.shift_left(jnp.int32(2), g)).astype(F32)
    inv_w = _pick(g, [0.5, 0.25, 0.125, 0.0625])
    return jnp.concatenate([head, jnp.broadcast_to(inv_w, (T - MAX_WINDOW, n)).astype(F32)], axis=0)


def _pool_fwd(zup, zgp, pool_w, pool_scale):
    T = zup.shape[0]

    def body(u_ref, g_ref, w_ref, sc_ref, y_ref):
        g = pl.program_id(0)
        u = u_ref[...]
        d = _window_sum(u, g, True) * _pool_inv_count(u.shape, g) - u
        lin = _mm(d.astype(BF16), w_ref[0].astype(BF16))
        silu, _ = _silu_parts(g_ref[...])
        y_ref[...] = (lin * sc_ref[...] * silu).astype(BF16)

    col = pl.BlockSpec((T, GROUP), lambda g: (0, g))
    return pl.pallas_call(
        body, name="pool_fwd", grid=(POOL_GROUPS,),
        in_specs=[col, col, pl.BlockSpec((1, GROUP, GROUP), lambda g: (g, 0, 0)), pl.BlockSpec((1, GROUP), lambda g: (0, g))],
        out_specs=col, out_shape=SDS((T, POOL_W), BF16), compiler_params=_cp(("parallel",)),
    )(zup, zgp, pool_w, pool_scale)


def _pool_bwd(zup, zgp, dyp, pool_w, pool_scale):
    T = zup.shape[0]

    def body(u_ref, g_ref, dy_ref, w_ref, sc_ref, du_ref, dg_ref, gw_ref, gsc_ref):
        g = pl.program_id(0)
        u = u_ref[...]
        inv = _pool_inv_count(u.shape, g)
        d = (_window_sum(u, g, True) * inv - u).astype(BF16)
        wb = w_ref[0].astype(BF16)
        lin = _mm(d, wb)
        sc = sc_ref[...]
        silu, dsilu = _silu_parts(g_ref[...])
        dy = dy_ref[...]
        dg_ref[...] = (dy * lin * sc * dsilu).astype(BF16)
        dpre = dy * silu
        gsc_ref[...] = jnp.sum(dpre * lin, axis=0, keepdims=True)
        dlin = (dpre * sc).astype(BF16)
        gw_ref[0] = _mm_tn(d, dlin)
        dd = _mm_nt(dlin, wb)
        du_ref[...] = (_window_sum(dd * inv, g, False) - dd).astype(BF16)

    col = pl.BlockSpec((T, GROUP), lambda g: (0, g))
    wspec = pl.BlockSpec((1, GROUP, GROUP), lambda g: (g, 0, 0))
    vspec = pl.BlockSpec((1, GROUP), lambda g: (0, g))
    return pl.pallas_call(
        body, name="pool_bwd", grid=(POOL_GROUPS,), in_specs=[col, col, col, wspec, vspec], out_specs=[col, col, wspec, vspec],
        out_shape=[SDS((T, POOL_W), BF16), SDS((T, POOL_W), BF16), SDS((POOL_GROUPS, GROUP, GROUP), F32), SDS((1, POOL_W), F32)],
        compiler_params=_cp(("parallel",)),
    )(zup, zgp, dyp, pool_w, pool_scale)


def _tail(x2, tgt, o, zga, ypool, zgm, wba, wbp, wout, norm_final, tm):
    T = x2.shape[0]
    steps = T // tm
    cols = D_MODEL // N_DEV

    def body(x_ref, tgt_ref, o_ref, zga_ref, yp_ref, zgm_ref, wba_ref, wbp_ref, wout_ref, nf_ref,
             loss_ref, dh_ref, dgm_ref, doop_ref, dga_ref, dcapr_ref, dyp_ref, swout_ref, swba_ref, swbp_ref, gnf_ref,
             gwout_ref, gwba_ref, gwbp_ref):
        @pl.when(pl.program_id(0) == 0)
        def _():
            for ref in (loss_ref, gwout_ref, gwba_ref, gwbp_ref, gnf_ref):
                ref[...] = jnp.zeros_like(ref)

        o_v = o_ref[...]
        silu, dsilu = _silu_parts(zga_ref[...])
        ya = (o_v * silu).astype(BF16)
        yp = yp_ref[...]
        wba_v = jnp.concatenate([wba_ref[k] for k in range(N_DEV)], axis=1)
        wbp_v = jnp.concatenate([wbp_ref[k] for k in range(N_DEV)], axis=1)
        wout_v = wout_ref[...]
        a = _mm(ya, wba_v)
        p = _mm(yp, wbp_v)
        gate = jax.nn.sigmoid(zgm_ref[...])
        ga, gp = gate[:, :D_MODEL], gate[:, D_MODEL:]
        mg = (ga * a + gp * p).astype(BF16)
        h = x_ref[...] + _mm(mg, wout_v)
        r = lax.rsqrt(jnp.mean(h * h, axis=-1, keepdims=True) + EPS)
        gf = nf_ref[...]
        hr = h * r
        e = hr * gf - tgt_ref[...]
        loss_ref[...] += (0.5 / D_MODEL) * jnp.sum(e * e)
        dy = e * (1.0 / D_MODEL)
        gnf_ref[...] += jnp.sum(dy * hr, axis=0, keepdims=True)
        u = dy * gf
        dh = r * (u - hr * jnp.mean(u * hr, axis=-1, keepdims=True))
        dh_ref[...] = dh
        dhb = dh.astype(BF16)
        dmg = _mm_nt(dhb, wout_v)
        dab = (dmg * ga).astype(BF16)
        dpb = (dmg * gp).astype(BF16)
        dya = _mm_nt(dab, wba_v)
        dyp_ref[...] = _mm_nt(dpb, wbp_v)
        gwout_ref[...] += _mm_tn(mg, dhb)
        gwba_ref[...] += _mm_tn(ya, dab)
        gwbp_ref[...] += _mm_tn(yp, dpb)
        dgm_ref[:, :D_MODEL] = (dmg * a * ga * (1.0 - ga)).astype(BF16)
        dgm_ref[:, D_MODEL:] = (dmg * p * gp * (1.0 - gp)).astype(BF16)
        do = dya * silu
        dga_ref[...] = (dya * o_v * dsilu).astype(BF16)
        prod = do * o_v
        lo = lax.broadcasted_iota(jnp.int32, (tm, LANES), 1) < VDIM
        for pair in range(HEADS // 2):
            ls = slice(pair * LANES, (pair + 1) * LANES)
            do_p, prod_p = do[:, ls], prod[:, ls]
            dcap_a = jnp.sum(jnp.where(lo, prod_p, 0.0), axis=-1, keepdims=True)
            dcap_b = jnp.sum(jnp.where(lo, 0.0, prod_p), axis=-1, keepdims=True)
            _store_pair_rows(dcapr_ref, pair, jnp.where(lo, dcap_a, dcap_b))
            doop_ref[:, 2 * pair * LANES:(2 * pair + 1) * LANES] = jnp.where(lo, 0.0, pltpu.roll(do_p, VDIM, 1)).astype(BF16)
            doop_ref[:, (2 * pair + 1) * LANES:(2 * pair + 2) * LANES] = jnp.where(lo, 0.0, do_p).astype(BF16)

        @pl.when(pl.program_id(0) == steps - 1)
        def _():
            for k in range(N_DEV):
                swout_ref[k] = gwout_ref[k * cols:(k + 1) * cols, :].astype(BF16)
                swba_ref[k] = gwba_ref[:, k * cols:(k + 1) * cols].astype(BF16)
                swbp_ref[k] = gwbp_ref[:, k * cols:(k + 1) * cols].astype(BF16)

    ins = (x2, tgt, o, zga, ypool, zgm, wba, wbp, wout, norm_final)
    in_specs = [_row_spec(tm, D_MODEL), _row_spec(tm, D_MODEL), _row_spec(tm, MLA_W), _row_spec(tm, MLA_W), _row_spec(tm, POOL_W),
                _row_spec(tm, 2 * D_MODEL), _full_spec(wba), _full_spec(wbp), _full_spec(wout), _full_spec(norm_final)]
    outs = [SDS((8, LANES), F32), SDS((T, D_MODEL), F32), SDS((T, 2 * D_MODEL), BF16), SDS((T, HW), BF16), SDS((T, MLA_W), BF16),
            SDS((HEADS // 2, 2, T), F32), SDS((T, POOL_W), F32),
            SDS((N_DEV, cols, D_MODEL), BF16), SDS((N_DEV, MLA_W, cols), BF16), SDS((N_DEV, POOL_W, cols), BF16), SDS((1, D_MODEL), F32)]
    out_specs = [_full_spec(outs[0]), _row_spec(tm, D_MODEL), _row_spec(tm, 2 * D_MODEL), _row_spec(tm, HW), _row_spec(tm, MLA_W),
                 pl.BlockSpec((HEADS // 2, 2, tm), lambda i: (0, 0, i)), _row_spec(tm, POOL_W),
                 _full_spec(outs[7]), _full_spec(outs[8]), _full_spec(outs[9]), _full_spec(outs[10])]
    return pl.pallas_call(
        body, name="tail", grid=(steps,), in_specs=in_specs, out_specs=out_specs, out_shape=outs,
        scratch_shapes=[pltpu.VMEM((D_MODEL, D_MODEL), F32), pltpu.VMEM((MLA_W, D_MODEL), F32), pltpu.VMEM((POOL_W, D_MODEL), F32)],
        compiler_params=_cp(("arbitrary",)),
    )(*ins)


def _attn_bwd(q_att, k_att, v_att, doop, lse_rows, dcap_rows, tq, hps, slabs, packed):
    T = q_att.shape[0]
    nq = T // tq
    n = len(slabs)
    groups = HEADS // hps
    head_lanes = [slice(h * LANES, (h + 1) * LANES) for h in range(hps)]

    def body(q_ref, k_ref, v_ref, doop_ref, lse_ref, dcap_ref, *rest):
        slab_refs, packed_ref = rest[:n], rest[n]
        dq_ref, dkv_ref, dkr_ref = rest[n + 1:n + 4]
        sum_refs, ptot_ref = rest[n + 4:2 * n + 4], rest[2 * n + 4]
        dq_acc = rest[2 * n + 5]
        rs = _ReduceScatter(slab_refs, packed_ref, sum_refs, ptot_ref, rest[2 * n + 6:])
        group, j = pl.program_id(0), pl.program_id(1)
        tick, last = group * nq + j, groups * nq - 1
        pl.when(tick == 0)(rs.start1)
        pl.when(tick == min(1, last))(rs.finish1_start2)
        pl.when(tick == min(3, last))(rs.relay2)
        mask = _chunk_mask(tq, tq, 0, True)
        lane = lax.broadcasted_iota(jnp.int32, (tq, LANES), 1)
        ks = [k_ref[:, hs] for hs in head_lanes]
        vs = [v_ref[:, hs] for hs in head_lanes]
        kts = [kh.T for kh in ks]

        @pl.when(j == 0)
        def _():
            dq_acc[...] = jnp.zeros_like(dq_acc)

        def step(i, carry, masked):
            rows = pl.ds(pl.multiple_of(i * tq, tq), tq)
            heads = range(hps)
            stat = lambda h: (h // 2, slice(h % 2, h % 2 + 1), rows)
            qhs = [q_ref[rows, hs] for hs in head_lanes]
            doops = [doop_ref[rows, hs] for hs in head_lanes]
            sts = [_mm_nt(ks[h], qhs[h]) for h in heads]
            dpts = [_mm_nt(vs[h], doops[h]) for h in heads]
            pts = [jnp.exp2(sts[h] - lse_ref[stat(h)]) for h in heads]
            if masked:
                pts = [jnp.where(mask, pt, 0.0) for pt in pts]
            dsts = [(pts[h] * (dpts[h] - dcap_ref[stat(h)])).astype(BF16) for h in heads]
            dvs = [_mm(pts[h].astype(BF16), doops[h]) for h in heads]
            dks = [_mm(dsts[h], qhs[h]) for h in heads]
            for h, hs in enumerate(head_lanes):
                dq_acc[hs, rows] += _mm(kts[h], dsts[h])
            return tuple((dk + dks[h], dv + dvs[h]) for h, (dk, dv) in enumerate(carry))

        zero = jnp.zeros((tq, LANES), F32)
        carry = step(j, ((zero, zero),) * hps, True)
        res = lax.fori_loop(j + 1, nq, functools.partial(step, masked=False), carry)
        dkr = None
        for (dk, dv), hs in zip(res, head_lanes):
            dk = dk * LN2
            dkv_ref[:, hs] = jnp.where(lane < NOPE, dk, dv).astype(BF16)
            dkr = dk if dkr is None else dkr + dk
        dkr_ref[0] = jnp.where((lane >= NOPE) & (lane < NOPE + ROPE), dkr, 0.0)

        @pl.when(j == nq - 1)
        def _():
            dq_ref[...] = (dq_acc[...] * SCALE).T.astype(BF16)

        pl.when(tick == last)(rs.finish2)

    kspec = pl.BlockSpec((tq, hps * LANES), lambda p, j: (j, p))
    qspec = pl.BlockSpec((T, hps * LANES), lambda p, j: (0, p))
    rspec = pl.BlockSpec((hps // 2, 2, T), lambda p, j: (p, 0, 0))
    sums = [SDS(s.shape[1:], F32) for s in slabs] + [SDS(packed.shape, F32)]
    return pl.pallas_call(
        body, name="attn_bwd", grid=(groups, nq),
        in_specs=[qspec, kspec, kspec, qspec, rspec, rspec] + [HBM_SPEC] * n + [_full_spec(packed)],
        out_specs=[qspec, kspec, pl.BlockSpec((1, tq, LANES), lambda p, j: (p, j, 0))] + [_full_spec(s) for s in sums],
        out_shape=[SDS((T, HW), BF16), SDS((T, HW), BF16), SDS((groups, T, LANES), F32)] + sums,
        scratch_shapes=[pltpu.VMEM((hps * LANES, T), F32)] + _rs_scratch([s.shape for s in sums[:-1]], packed.shape),
        compiler_params=_cp(("arbitrary", "arbitrary")),
    )(q_att, k_att, v_att, doop, lse_rows, dcap_rows, *slabs, packed)


def _rms_bwd(z, gain, dout):
    r = lax.rsqrt(jnp.mean(z * z, axis=-1, keepdims=True) + EPS)
    zr = z * r
    u = dout * gain
    return r * (u - zr * jnp.mean(u * zr, axis=-1, keepdims=True)), jnp.sum(dout * zr, axis=0, keepdims=True)


def _mla_bwd(dq_att, dkv_nat, dkr4, zfr, q_norm, wuq_pad, kv_norm, wukv, rc, rsa, rsb, tm):
    T = dq_att.shape[0]

    def body(dq_ref, dkv_ref, dkr_ref, zfr_ref, qn_ref, wuq_ref, kvn_ref, wukv_ref, c_ref, sa_ref, sb_ref,
             dfr_ref, gwuq_out, gwukv_out, gqn_ref, gkvn_ref, gwuq_ref, gwukv_ref):
        @pl.when(pl.program_id(0) == 0)
        def _():
            for ref in (gwuq_ref, gwukv_ref, gqn_ref, gkvn_ref):
                ref[...] = jnp.zeros_like(ref)

        c, sa, sb = c_ref[...], sa_ref[...], sb_ref[...]
        zq, zkv = zfr_ref[:, :Q_RANK], zfr_ref[:, Q_RANK:Q_RANK + KV_RANK]
        qn, kvn = qn_ref[...], kvn_ref[...]
        dkv = dkv_ref[...]
        dckv = _mm_nt(dkv, wukv_ref[...])
        ckv = (zkv * lax.rsqrt(jnp.mean(zkv * zkv, axis=-1, keepdims=True) + EPS) * kvn).astype(BF16)
        gwukv_ref[...] += _mm_tn(ckv, dkv)
        cq = (zq * lax.rsqrt(jnp.mean(zq * zq, axis=-1, keepdims=True) + EPS) * qn).astype(BF16)
        dq = _rope(dq_ref[...].astype(F32), c, sa, sb, -1.0).astype(BF16)
        dzkv, gkvn = _rms_bwd(zkv, kvn, dckv)
        gkvn_ref[...] += gkvn
        gwuq_ref[...] += _mm_tn(cq, dq)
        dzq, gqn = _rms_bwd(zq, qn, _mm_nt(dq, wuq_ref[...]))
        gqn_ref[...] += gqn
        dkr = functools.reduce(lambda a, b: a + b, [dkr_ref[g] for g in range(dkr4.shape[0])])
        dfr_ref[:, :Q_RANK] = dzq.astype(BF16)
        dfr_ref[:, Q_RANK:Q_RANK + KV_RANK] = dzkv.astype(BF16)
        dfr_ref[:, Q_RANK + KV_RANK:] = _rope(dkr, c, sa, sb, -1.0).astype(BF16)

        @pl.when(pl.program_id(0) == pl.num_programs(0) - 1)
        def _():
            gwuq_out[...] = gwuq_ref[...].astype(BF16)
            gwukv_out[...] = gwukv_ref[...].astype(BF16)

    ins = (dq_att, dkv_nat, dkr4, zfr, q_norm, wuq_pad, kv_norm, wukv, rc, rsa, rsb)
    in_specs = [_row_spec(tm, HW), _row_spec(tm, HW), pl.BlockSpec((dkr4.shape[0], tm, LANES), lambda i: (0, i, 0)), _row_spec(tm, FRONT_W),
                _full_spec(q_norm), _full_spec(wuq_pad), _full_spec(kv_norm), _full_spec(wukv),
                _row_spec(tm, LANES), _row_spec(tm, LANES), _row_spec(tm, LANES)]
    outs = [SDS((T, FRONT_W), BF16), SDS((Q_RANK, HW), BF16), SDS((KV_RANK, HW), BF16), SDS((1, Q_RANK), F32), SDS((1, KV_RANK), F32)]
    out_specs = [_row_spec(tm, FRONT_W)] + [_full_spec(s) for s in outs[1:]]
    return pl.pallas_call(
        body, name="mla_bwd", grid=(T // tm,), in_specs=in_specs, out_specs=out_specs, out_shape=outs,
        scratch_shapes=[pltpu.VMEM((Q_RANK, HW), F32), pltpu.VMEM((KV_RANK, HW), F32)],
        compiler_params=_cp(("arbitrary",)),
    )(*ins)


_DZ_COLS = ((GM, ZTOT), (GA, UP), (UP, GP), (GP, GM), (ZQ, GA))


def _in_proj_bwd_x(dzs, x2, dh, norm_in, w_in_pad, tm, slabs):
    T = x2.shape[0]
    steps = T // tm
    n = len(slabs)

    def body(d0, d1, d2, d3, d4, x_ref, dh_ref, nin_ref, win_ref, *rest):
        slab_refs, (gx_ref, gnin_ref), sum_refs = rest[:n], rest[n:n + 2], rest[n + 2:2 * n + 2]
        rs = _ReduceScatter(slab_refs, None, sum_refs, None, rest[2 * n + 2:])
        step = pl.program_id(0)

        @pl.when(step == 0)
        def _():
            gnin_ref[...] = jnp.zeros_like(gnin_ref)
            rs.start1()

        pl.when(step == min(2, steps - 1))(rs.finish1_start2)
        pl.when(step == min(steps * 11 // 16, steps - 1))(rs.relay2)
        dhn = None
        for ref, (lo, hi) in zip((d0, d1, d2, d3, d4), _DZ_COLS):
            t = _mm(ref[...], win_ref[lo:hi, :])
            dhn = t if dhn is None else dhn + t
        dx, gnin = _rms_bwd(x_ref[...], nin_ref[...], dhn)
        gnin_ref[...] += gnin
        gx_ref[...] = dx + dh_ref[...]
        pl.when(step == steps - 1)(rs.finish2)

    in_specs = [_row_spec(tm, hi - lo) for lo, hi in _DZ_COLS] + [_row_spec(tm, D_MODEL), _row_spec(tm, D_MODEL),
                                                                  _full_spec(norm_in), _full_spec(w_in_pad)] + [HBM_SPEC] * n
    sums = [SDS(s.shape[1:], F32) for s in slabs]
    outs = [SDS((T, D_MODEL), F32), SDS((1, D_MODEL), F32)] + sums
    return pl.pallas_call(
        body, name="in_proj_bwd_x", grid=(steps,), in_specs=in_specs,
        out_specs=[_row_spec(tm, D_MODEL), _full_spec(outs[1])] + [_full_spec(s) for s in sums],
        out_shape=outs, scratch_shapes=_rs_scratch([s.shape for s in sums], None), compiler_params=_cp(("arbitrary",)),
    )(*dzs, x2, dh, norm_in, w_in_pad, *slabs)


SLAB_ROWS = IN_TOTAL // N_DEV


def _slab_segments(k):
    cuts = [(0, ZKR_ORIG, 0), (ZKR_ORIG, ZKR_ORIG + ROPE, NOPE), (ZKR_ORIG + ROPE, IN_TOTAL, LANES - ROPE)]
    lo, hi = k * SLAB_ROWS, (k + 1) * SLAB_ROWS
    return [(max(lo, a) - lo, max(lo, a) + shift, min(hi, b) - max(lo, a)) for a, b, shift in cuts if min(hi, b) > max(lo, a)]


def _in_proj_bwd_w(dzs, hn, tm):
    T = hn.shape[0]
    steps = T // tm

    def body(d0, d1, d2, d3, d4, hn_ref, slab_ref, acc_ref):
        @pl.when(pl.program_id(0) == 0)
        def _():
            acc_ref[...] = jnp.zeros_like(acc_ref)

        hn_v = hn_ref[...]
        for ref, (lo, hi) in zip((d0, d1, d2, d3, d4), _DZ_COLS):
            acc_ref[lo:hi, :] += _mm_tn(ref[...], hn_v)

        @pl.when(pl.program_id(0) == steps - 1)
        def _():
            for k in range(N_DEV):
                for at, src, rows in _slab_segments(k):
                    slab_ref[k, at:at + rows, :] = acc_ref[src:src + rows, :].astype(BF16)

    in_specs = [_row_spec(tm, hi - lo) for lo, hi in _DZ_COLS] + [_row_spec(tm, D_MODEL)]
    out = SDS((N_DEV, SLAB_ROWS, D_MODEL), BF16)
    return pl.pallas_call(
        body, name="in_proj_bwd_w", grid=(steps,), in_specs=in_specs, out_specs=_full_spec(out), out_shape=out,
        scratch_shapes=[pltpu.VMEM((ZTOT, D_MODEL), F32)], compiler_params=_cp(("arbitrary",)),
    )(*dzs, hn)


def _local_step(x2, tgt, norm_in, w_in_pad, q_norm, wuq_pad, kv_norm, wukv, pool_w, pool_scale, late_shards, norm_final):
    T = x2.shape[0]
    tm, tm_small, tq = min(512, T), min(256, T), min(512, T)
    heads_fwd, heads_bwd = 4, 4
    row = lambda v: v.reshape(1, -1)
    rc, rsa, rsb = _rope_tables(T)

    hn, zgm, zga, zup, zgp, zfr, q_att, k_att, v_att, vt_att, w_ba, w_bp, w_out = _in_proj(
        x2, row(norm_in), w_in_pad, row(q_norm), wuq_pad, row(kv_norm), wukv, rc, rsa, rsb, tm, late_shards)
    w_out = w_out.reshape(D_MODEL, D_MODEL)
    o, lse_rows = _attn_fwd(q_att, k_att, vt_att, tq, heads_fwd)
    ypool = _pool_fwd(zup, zgp, pool_w, row(pool_scale))
    loss8, dh, dgm, doop, dga, dcap_rows, dyp, *slabs, g_nf = _tail(
        x2, tgt, o, zga, ypool, zgm, w_ba, w_bp, w_out, row(norm_final), tm_small)
    dup, dgp, g_pool_w, g_pool_scale = _pool_bwd(zup, zgp, dyp, pool_w, row(pool_scale))

    early = [g_pool_w, g_pool_scale, g_nf, loss8[0]]
    packed = jnp.concatenate([_pack_rows(a) for a in early], axis=0)
    dq_att, dkv_nat, dkr4, s_wout, s_wba, s_wbp, tot_early = _attn_bwd(
        q_att, k_att, v_att, doop, lse_rows, dcap_rows, tq, heads_bwd, slabs, packed)
    s_pool_w, s_pool_scale, s_nf, s_loss = _unpack_rows(tot_early, early)

    dfr, g_wuq_pad, g_wukv, g_qn, g_kvn = _mla_bwd(
        dq_att, dkv_nat, dkr4, zfr, row(q_norm), wuq_pad, row(kv_norm), wukv, rc, rsa, rsb, tm)
    dzs = (dgm, dga, dup, dgp, dfr)
    slabs = [_in_proj_bwd_w(dzs, hn, tm), g_wuq_pad.reshape(N_DEV, Q_RANK // N_DEV, HW), g_wukv.reshape(N_DEV, KV_RANK // N_DEV, HW)]
    grad_x, g_nin, s_win, s_wuq, s_wukv = _in_proj_bwd_x(dzs, x2, dh, row(norm_in), w_in_pad, tm_small, slabs)
    late = [g_nin, g_qn, g_kvn]
    (tot_late,) = _reduce_scatter([], jnp.concatenate([_pack_rows(a) for a in late], axis=0))
    s_nin, s_qn, s_kvn = _unpack_rows(tot_late, late)

    grads = dict(norm_in=s_nin, w_in=s_win, q_norm=s_qn, w_uq=s_wuq, kv_norm=s_kvn, w_ukv=s_wukv, pool_w=s_pool_w.reshape(-1, GROUP),
                 pool_scale=s_pool_scale, w_branch_attn=s_wba, w_branch_pool=s_wbp, w_out=s_wout, norm_final=s_nf)
    return s_loss[0], grad_x, grads


MESH_ID = pl.DeviceIdType.MESH
VMEM_SPEC = pl.BlockSpec(memory_space=pltpu.VMEM)
HBM_SPEC = pl.BlockSpec(memory_space=pl.ANY)


def _mesh_pos():
    return lax.axis_index("x"), lax.axis_index("y"), lax.axis_index("c")


BF16_TILE_ROWS = 16


def _half_rows(rows):
    cut = -(-(rows // 2) // BF16_TILE_ROWS) * BF16_TILE_ROWS
    return pl.ds(0, cut), pl.ds(cut, rows - cut)


def _slot(px, py, pc):
    return 4 * px + 2 * py + pc


def _staged_shape(shape):
    return (shape[0], shape[1] * LANES) if len(shape) == 3 else tuple(shape)


def _all_gather_bf16(shards):
    n = len(shards)
    staged = [_staged_shape(s.shape) for s in shards]

    def body(*refs):
        ins, outs = refs[:n], refs[n:2 * n]
        land0, scratch = refs[2 * n], refs[2 * n + 1:]
        wpad_ref = outs[0]
        ag = _AllGather(ins, (land0,) + tuple(outs[1:]), scratch)
        ag.start()
        ag.forward()
        ag.finish()
        wpad_ref[ZKR:GA, :] = jnp.zeros((GA - ZKR, D_MODEL), BF16)
        for k in range(N_DEV):
            for at, dst, rows in _slab_segments(k):
                wpad_ref[dst:dst + rows, :] = land0[k, at:at + rows, :]

    return pl.pallas_call(
        body, name="all_gather_weights",
        in_specs=[VMEM_SPEC] * n, out_specs=[VMEM_SPEC] + [HBM_SPEC] * (n - 1),
        out_shape=[SDS((ZTOT, D_MODEL), BF16)] + [SDS((N_DEV,) + s, BF16) for s in staged[1:]],
        scratch_shapes=[pltpu.VMEM((N_DEV,) + staged[0], BF16)] + _ag_scratch(staged),
        compiler_params=_cp(),
    )(*shards)


def _ag_scratch(shapes):
    n = len(shapes)
    dma = pltpu.SemaphoreType.DMA
    return [pltpu.VMEM(tuple(s), BF16) for s in shapes] + [dma((_AllGather.COPIES * n,)), dma((_AllGather.COPIES * n,)), dma((n,))]


class _AllGather:
    COPIES = 8

    def __init__(self, in_refs, dest_refs, scratch):
        n = self.n = len(in_refs)
        self.ins, self.dests, self.stage = in_refs, dest_refs, scratch[:n]
        self.send_sems, self.recv_sems, self.local_sems = scratch[n:]
        x, y, c = _mesh_pos()
        self.c, self.me, self.sibling = c, (x, y, c), (x, y, 1 - c)
        self.xn, self.yn, self.diag = (1 - x, y), (x, 1 - y), (1 - x, 1 - y)

    def _halves(self, a):
        return _half_rows(self.stage[a].shape[0])

    def _copy(self, a, k, block, to, from_stage=False, rows=None):
        dst = self.dests[a].at[_slot(*block)]
        src = self.stage[a] if from_stage else dst
        if rows is not None:
            src, dst = src.at[rows], dst.at[rows]
        return pltpu.make_async_remote_copy(
            src_ref=src, dst_ref=dst, send_sem=self.send_sems.at[self.COPIES * a + k],
            recv_sem=self.recv_sems.at[self.COPIES * a + k], device_id=to, device_id_type=MESH_ID)

    def _mine(self):
        return [pltpu.make_async_copy(self.stage[a], self.dests[a].at[_slot(*self.me)], self.local_sems.at[a]) for a in range(self.n)]

    def _first(self, a):
        return [self._copy(a, 0, self.me, self.sibling, True), self._copy(a, 1, self.me, (*self.xn, self.c), True),
                self._copy(a, 2, self.me, (*self.yn, self.c), True)]

    def _relays(self, a):
        lo, hi = self._halves(a)
        return [self._copy(a, 3, (*self.xn, self.c), (*self.yn, self.c), rows=lo),
                self._copy(a, 4, (*self.yn, self.c), (*self.xn, self.c), rows=hi)]

    def _passes(self, a):
        return [self._copy(a, 5 + j, (*chip, self.c), self.sibling) for j, chip in enumerate((self.xn, self.yn, self.diag))]

    def start(self):
        for a in range(self.n):
            src, dst = self.ins[a], self.stage[a]
            if src.shape == dst.shape:
                dst[...] = src[...].astype(BF16)
            else:
                if src.shape[2] < LANES:
                    dst[...] = jnp.zeros(dst.shape, BF16)
                for h in range(src.shape[1]):
                    dst[:, h * LANES:h * LANES + src.shape[2]] = src[:, h, :].astype(BF16)
        for cp in self._mine():
            cp.start()
        for a in range(self.n):
            for cp in self._first(a):
                cp.start()

    def forward(self):
        for a in range(self.n):
            relays, passes = self._relays(a), self._passes(a)
            for j, chip in enumerate((self.xn, self.yn)):
                self._copy(a, 1 + j, (*chip, self.c), self.me).wait_recv()
                relays[j].start()
                passes[j].start()

    def finish(self):
        for a in range(self.n):
            lo, hi = self._halves(a)
            self._copy(a, 3, (*self.diag, self.c), self.me, rows=lo).wait_recv()
            self._copy(a, 4, (*self.diag, self.c), self.me, rows=hi).wait_recv()
            self._passes(a)[2].start()
        for a in range(self.n):
            self._copy(a, 0, self.sibling, self.me).wait_recv()
            for j, chip in enumerate((self.xn, self.yn, self.diag)):
                self._copy(a, 5 + j, (*chip, 1 - self.c), self.me).wait_recv()
            for cp in self._first(a) + self._relays(a) + self._passes(a):
                cp.wait_send()
        for cp in self._mine():
            cp.wait()


N_CHIPS = 4


def _reduce_scatter(slabs, packed):
    def body(*refs):
        n = len(slabs)
        rs = _ReduceScatter(refs[:n], refs[n], refs[n + 1:2 * n + 1], refs[2 * n + 1], refs[2 * n + 2:])
        rs.start1()
        rs.finish1_start2()
        rs.relay2()
        rs.finish2()

    shapes = [s.shape[1:] for s in slabs]
    return pl.pallas_call(
        body, name="reduce_scatter_grads",
        in_specs=[HBM_SPEC] * len(slabs) + [VMEM_SPEC], out_specs=[VMEM_SPEC] * (len(slabs) + 1),
        out_shape=[SDS(s, F32) for s in shapes] + [SDS(packed.shape, F32)],
        scratch_shapes=_rs_scratch(shapes, packed.shape), compiler_params=_cp(),
    )(*slabs, packed)


def _rs_scratch(shapes, packed_shape):
    n = len(shapes)
    n1, n2 = N_CHIPS * n + 1, _ReduceScatter.L2_COPIES * n + N_CHIPS - 1
    dma = pltpu.SemaphoreType.DMA
    packed = [] if packed_shape is None else [pltpu.VMEM(packed_shape, F32), pltpu.VMEM((N_CHIPS,) + tuple(packed_shape), F32)]
    return ([pltpu.VMEM((N_CHIPS,) + tuple(s), BF16) for s in shapes] * 2 + [pltpu.VMEM((N_CHIPS - 1,) + tuple(s), BF16) for s in shapes] * 2
            + packed + [dma((max(N_CHIPS * n, 1),)), dma((n1,)), dma((n1,)), dma((n2,)), dma((n2,))])


class _ReduceScatter:
    L2_COPIES = 6

    def __init__(self, slab_refs, packed_ref, out_refs, ptot_ref, scratch):
        n = self.n = len(slab_refs)
        self.slabs, self.packed, self.outs, self.ptot = slab_refs, packed_ref, out_refs, ptot_ref
        self.own1, self.land1, self.send2, self.land2 = (scratch[k * n:(k + 1) * n] for k in range(4))
        rest = scratch[4 * n:]
        if packed_ref is not None:
            self.pland1, self.pland2 = rest[:2]
            rest = rest[2:]
        self.loc_sems, self.send1_sems, self.recv1_sems, self.send2_sems, self.recv2_sems = rest
        self.x, self.y, self.c = _mesh_pos()

    def _chip(self, r):
        return (1 - self.x if r & 2 else self.x, 1 - self.y if r & 1 else self.y)

    @staticmethod
    def _remote(src, dst, send_sem, recv_sem, to):
        return pltpu.make_async_remote_copy(src_ref=src, dst_ref=dst, send_sem=send_sem, recv_sem=recv_sem, device_id=to,
                                            device_id_type=MESH_ID)

    def _copies1(self):
        c, sibling = self.c, (self.x, self.y, 1 - self.c)
        cps = []
        for a in range(self.n):
            for r in range(N_CHIPS):
                k = N_CHIPS * a + r
                cps.append(pltpu.make_async_copy(self.slabs[a].at[_slot(*self._chip(r), c)], self.own1[a].at[r], self.loc_sems.at[k]))
                cps.append(self._remote(self.slabs[a].at[_slot(*self._chip(r), 1 - c)], self.land1[a].at[r],
                                        self.send1_sems.at[k], self.recv1_sems.at[k], sibling))
        if self.packed is not None:
            k = N_CHIPS * self.n
            cps.append(self._remote(self.packed, self.pland1, self.send1_sems.at[k], self.recv1_sems.at[k], sibling))
        return cps

    def _halves(self, a):
        return _half_rows(self.send2[a].shape[1])

    def _copy2(self, a, k):
        lo, hi = self._halves(a)
        xn, yn = (*self._chip(2), self.c), (*self._chip(1), self.c)
        slot, rows, to = [(1, lo, xn), (2, lo, xn), (0, hi, yn), (2, hi, yn), (0, lo, yn), (1, hi, xn)][k]
        return self._remote(self.send2[a].at[slot].at[rows], self.land2[a].at[slot].at[rows],
                            self.send2_sems.at[self.L2_COPIES * a + k], self.recv2_sems.at[self.L2_COPIES * a + k], to)

    def _copies2_packed(self):
        base = self.L2_COPIES * self.n - 1
        return [self._remote(self.pland2.at[0], self.pland2.at[r], self.send2_sems.at[base + r], self.recv2_sems.at[base + r],
                             (*self._chip(r), self.c)) for r in range(1, N_CHIPS)]

    def start1(self):
        for cp in self._copies1():
            cp.start()

    def finish1_start2(self):
        for cp in self._copies1():
            cp.wait()
        for a in range(self.n):
            self.outs[a][...] = self.own1[a][0].astype(F32) + self.land1[a][0].astype(F32)
            for r in range(1, N_CHIPS):
                self.send2[a][r - 1] = (self.own1[a][r].astype(F32) + self.land1[a][r].astype(F32)).astype(BF16)
            for k in range(4):
                self._copy2(a, k).start()
        if self.packed is not None:
            self.pland2[0] = self.packed[...] + self.pland1[...]
            for cp in self._copies2_packed():
                cp.start()

    def relay2(self):
        for a in range(self.n):
            lo, hi = self._halves(a)
            s2, l2 = self.send2[a], self.land2[a]
            self._copy2(a, 1).wait_recv()
            s2[0, lo] = (s2[0, lo].astype(F32) + l2[2, lo].astype(F32)).astype(BF16)
            self._copy2(a, 4).start()
            self._copy2(a, 3).wait_recv()
            s2[1, hi] = (s2[1, hi].astype(F32) + l2[2, hi].astype(F32)).astype(BF16)
            self._copy2(a, 5).start()

    def finish2(self):
        for a in range(self.n):
            for k in (0, 2, 4, 5):
                self._copy2(a, k).wait_recv()
            for k in range(self.L2_COPIES):
                self._copy2(a, k).wait_send()
            l2 = self.land2[a]
            self.outs[a][...] = self.outs[a][...] + (l2[0].astype(F32) + l2[1].astype(F32))
        if self.packed is not None:
            for cp in self._copies2_packed():
                cp.wait()
            p2 = self.pland2
            self.ptot[...] = (p2[0] + p2[1]) + (p2[2] + p2[3])


def _adamw(ws, gs, ms, vs):
    n = len(ws)
    regrouped = [k for k in range(n) if gs[k].shape != ws[k].shape]

    def body(*refs):
        for k in range(n):
            w_ref, g_ref, m_ref, v_ref, d_ref, nm_ref, nv_ref = (refs[j * n + k] for j in range(7))
            if k in regrouped:
                windows = [((slice(None), h), (slice(None), slice(h * LANES, h * LANES + w_ref.shape[2]))) for h in range(w_ref.shape[1])]
                g_out_ref = refs[7 * n + regrouped.index(k)]
            else:
                windows = [(..., ...)]
            for at, g_at in windows:
                w, g, m, v = w_ref[at], g_ref[g_at], m_ref[at], v_ref[at]
                m = ADAM_B1 * m + (1.0 - ADAM_B1) * g
                v = ADAM_B2 * v + (1.0 - ADAM_B2) * jnp.square(g)
                m_hat = m / (1.0 - ADAM_B1 ** ADAM_STEP)
                v_hat = v / (1.0 - ADAM_B2 ** ADAM_STEP)
                d_ref[at] = -ADAM_LR * (m_hat / (jnp.sqrt(v_hat) + ADAM_EPS) + ADAM_WD * w)
                nm_ref[at] = m
                nv_ref[at] = v
                if k in regrouped:
                    g_out_ref[at] = g

    shapes = [SDS(w.shape, F32) for w in ws]
    outs = pl.pallas_call(
        body, name="adamw", in_specs=[VMEM_SPEC] * (4 * n), out_specs=[VMEM_SPEC] * (3 * n + len(regrouped)),
        out_shape=shapes * 3 + [shapes[k] for k in regrouped], compiler_params=_cp(),
    )(*ws, *gs, *ms, *vs)
    grads = list(gs)
    for k, g in zip(regrouped, outs[3 * n:]):
        grads[k] = g
    return outs[:n], outs[n:2 * n], outs[2 * n:3 * n], grads


WEIGHTS = ("norm_in", "w_in", "q_norm", "w_uq", "kv_norm", "w_ukv", "pool_w", "pool_scale", "w_branch_attn", "w_branch_pool",
           "w_out", "norm_final")
SUBLANES = 8


def _pack_rows(a):
    a = a.reshape(-1, LANES)
    return jnp.pad(a, ((0, -a.shape[0] % SUBLANES), (0, 0)))


def _unpack_rows(packed, like):
    out, row = [], 0
    for a in like:
        rows = a.size // LANES
        out.append(packed[row:row + rows].reshape(a.shape))
        row += rows + (-rows % SUBLANES)
    return out


def kernel(x, norm_in, w_in, q_norm, w_uq, kv_norm, w_ukv, pool_w, pool_scale, w_branch_attn, w_branch_pool, w_out, norm_final, loss_target, m_norm_in, m_w_in, m_q_norm, m_w_uq, m_kv_norm, m_w_ukv, m_pool_w, m_pool_scale, m_w_branch_attn, m_w_branch_pool, m_w_out, m_norm_final, v_norm_in, v_w_in, v_q_norm, v_w_uq, v_kv_norm, v_w_ukv, v_pool_w, v_pool_scale, v_w_branch_attn, v_w_branch_pool, v_w_out, v_norm_final):
    w = dict(norm_in=norm_in, w_in=w_in, q_norm=q_norm, w_uq=w_uq, kv_norm=kv_norm, w_ukv=w_ukv, pool_w=pool_w, pool_scale=pool_scale,
             w_branch_attn=w_branch_attn, w_branch_pool=w_branch_pool, w_out=w_out, norm_final=norm_final)
    m = dict(norm_in=m_norm_in, w_in=m_w_in, q_norm=m_q_norm, w_uq=m_w_uq, kv_norm=m_kv_norm, w_ukv=m_w_ukv, pool_w=m_pool_w,
             pool_scale=m_pool_scale, w_branch_attn=m_w_branch_attn, w_branch_pool=m_w_branch_pool, w_out=m_w_out, norm_final=m_norm_final)
    v = dict(norm_in=v_norm_in, w_in=v_w_in, q_norm=v_q_norm, w_uq=v_w_uq, kv_norm=v_kv_norm, w_ukv=v_w_ukv, pool_w=v_pool_w,
             pool_scale=v_pool_scale, w_branch_attn=v_w_branch_attn, w_branch_pool=v_w_branch_pool, w_out=v_w_out, norm_final=v_norm_final)

    def as2d(name, a):
        if name == "w_in":
            return a.T
        if name in ("w_uq", "w_ukv"):
            return a
        return a.reshape(-1, GROUP) if name == "pool_w" else a

    def unshape(name, a):
        return a.T if name == "w_in" else a.reshape(w[name].shape)

    def like_weight(name, g):
        return g.reshape(-1) if w[name].ndim == 1 else g

    w_in_pad, w_uq_full, w_ukv_full = _all_gather_bf16([as2d(k, w[k]) for k in ("w_in", "w_uq", "w_ukv")])
    loss, grad_x, g2d = _local_step(
        x.reshape(x.shape[1:]), loss_target.reshape(x.shape[1:]), norm_in, w_in_pad, q_norm,
        w_uq_full.reshape(Q_RANK, HW), kv_norm, w_ukv_full.reshape(KV_RANK, HW),
        pool_w, pool_scale, [w_branch_attn, w_branch_pool, w_out], norm_final)

    deltas, new_m, new_v, grads = _adamw([as2d(k, w[k]) for k in WEIGHTS], [like_weight(k, g2d[k]) for k in WEIGHTS],
                                         [as2d(k, m[k]) for k in WEIGHTS], [as2d(k, v[k]) for k in WEIGHTS])
    shaped = lambda arrs: [unshape(k, a) for k, a in zip(WEIGHTS, arrs)]
    return (loss, grad_x.reshape(x.shape), *shaped(grads), *shaped(deltas), *shaped(new_m), *shaped(new_v))
```

```python
import functools

import jax
import jax.numpy as jnp
import numpy as np
from jax import lax
from jax.experimental import pallas as pl
from jax.experimental.pallas import tpu as pltpu

F32 = jnp.float32
BF16 = jnp.bfloat16
SDS = jax.ShapeDtypeStruct

D_MODEL = 1024
HEADS = 8
NOPE = 64
ROPE = 32
VDIM = 64
Q_RANK = 384
KV_RANK = 256
MLA_W = HEADS * VDIM
POOL_W = 512
POOL_GROUPS = 4
GROUP = POOL_W // POOL_GROUPS
CHUNK = 64
ROPE_THETA = 10000.0
EPS = 1e-6
SCALE = (NOPE + ROPE) ** -0.5
LOG2E = 1.4426950408889634
LN2 = 0.6931471805599453
QK_SCALE_LOG2 = SCALE * LOG2E
IN_TOTAL = 4256
ADAM_LR, ADAM_B1, ADAM_B2, ADAM_EPS, ADAM_WD, ADAM_STEP = 0.001, 0.9, 0.999, 1e-08, 0.01, 10

N_DEV = 8
LANES = 128
HEAD_PAD = LANES
HW = HEADS * HEAD_PAD

ZQ, ZKV, ZKR, GA, UP, GP, GM, ZTOT = 0, 384, 640, 768, 1280, 1792, 2304, 4352
FRONT_W = GA
ZKR_ORIG = 640

VMEM_LIMIT = 62 * 1024 * 1024


def _cp(sem=None, **kw):
    if sem is not None:
        kw["dimension_semantics"] = sem
    return pltpu.CompilerParams(vmem_limit_bytes=VMEM_LIMIT, **kw)


def _mm(a, b):
    return lax.dot_general(a, b, (((1,), (0,)), ((), ())), preferred_element_type=F32)


def _mm_nt(a, b):
    return lax.dot_general(a, b, (((1,), (1,)), ((), ())), preferred_element_type=F32)


def _mm_tn(a, b):
    return lax.dot_general(a, b, (((0,), (0,)), ((), ())), preferred_element_type=F32)


def _row_spec(tm, w):
    return pl.BlockSpec((tm, w), lambda i: (i, 0))


def _full_spec(a):
    nd = len(a.shape)
    return pl.BlockSpec(a.shape, lambda *_: (0,) * nd)


def _rope(v, c, sa, sb, sign):
    n = v.shape[-1]
    reps = n // LANES
    if reps > 1:
        c, sa, sb = (jnp.tile(t, (1, reps)) for t in (c, sa, sb))
    up = pltpu.roll(v, n - ROPE // 2, 1)
    dn = pltpu.roll(v, ROPE // 2, 1)
    return v * c + sign * (up * sa + dn * sb)


def _rope_tables(T):
    half = ROPE // 2
    inv_freq = np.float32(ROPE_THETA) ** (-np.arange(half, dtype=np.float32) / np.float32(half))
    ang = np.arange(T, dtype=np.float32)[:, None] * inv_freq[None, :].astype(np.float32)
    cos, sin = np.cos(ang.astype(np.float64)).astype(np.float32), np.sin(ang.astype(np.float64)).astype(np.float32)
    z16 = np.zeros((T, half), np.float32)
    z32 = np.zeros((T, LANES - NOPE - ROPE), np.float32)
    c = np.concatenate([np.ones((T, NOPE), np.float32), cos, cos, z32], axis=1)
    sa = np.concatenate([np.zeros((T, NOPE), np.float32), -sin, z16, z32], axis=1)
    sb = np.concatenate([np.zeros((T, NOPE), np.float32), z16, sin, z32], axis=1)
    return jnp.asarray(c), jnp.asarray(sa), jnp.asarray(sb)


def _silu_parts(g):
    sg = jax.nn.sigmoid(g)
    return g * sg, sg + g * sg * (1.0 - sg)


def _in_proj(x2, norm_in, w_in_pad, q_norm, wuq_pad, kv_norm, wukv, rc, rsa, rsb, tm, late_shards):
    T = x2.shape[0]
    steps = T // tm
    n = len(late_shards)

    def body(x_ref, nin_ref, win_ref, qn_ref, wuq_ref, kvn_ref, wukv_ref, c_ref, sa_ref, sb_ref, *rest):
        hn_ref, zgm_ref, zga_ref, zup_ref, zgp_ref, zfr_ref, q_ref, k_ref, v_ref, vt_ref = rest[n:n + 10]
        ag = _AllGather(rest[:n], rest[n + 10:2 * n + 10], rest[2 * n + 10:])
        step = pl.program_id(0)
        pl.when(step == 0)(ag.start)
        pl.when(step == min(3, steps - 1))(ag.forward)
        xf = x_ref[...]
        r = lax.rsqrt(jnp.mean(xf * xf, axis=-1, keepdims=True) + EPS)
        hn = (xf * r * nin_ref[...]).astype(BF16)
        hn_ref[...] = hn
        zfr = _mm_nt(hn, win_ref[ZQ:GA, :])
        zfr_ref[...] = zfr
        zq, zkv, zkr = zfr[:, ZQ:ZKV], zfr[:, ZKV:ZKR], zfr[:, ZKR:GA]
        c, sa, sb = c_ref[...], sa_ref[...], sb_ref[...]
        rq = lax.rsqrt(jnp.mean(zq * zq, axis=-1, keepdims=True) + EPS)
        cq = (zq * rq * qn_ref[...].reshape(1, Q_RANK)).astype(BF16)
        rkv = lax.rsqrt(jnp.mean(zkv * zkv, axis=-1, keepdims=True) + EPS)
        ckv = (zkv * rkv * kvn_ref[...]).astype(BF16)
        zga_ref[...] = _mm_nt(hn, win_ref[GA:UP, :])
        zup_ref[...] = _mm_nt(hn, win_ref[UP:GP, :])
        zgp_ref[...] = _mm_nt(hn, win_ref[GP:GM, :])
        q_raw = _mm(cq, wuq_ref[...])
        kv = _mm(ckv, wukv_ref[...])
        zgm_ref[...] = _mm_nt(hn, win_ref[GM:ZTOT, :])
        q = _rope(q_raw, c, sa, sb, 1.0)
        q_ref[...] = (q * QK_SCALE_LOG2).astype(BF16)
        kr = _rope(zkr, c, sa, sb, 1.0)
        lane = lax.broadcasted_iota(jnp.int32, kv.shape, 1) % LANES
        k_ref[...] = jnp.where(lane < NOPE, kv, jnp.tile(kr, (1, HEADS))).astype(BF16)
        v = jnp.where(lane < NOPE, 1.0, kv).astype(BF16)
        v_ref[...] = v
        vt_ref[...] = v.T
        pl.when(step == steps - 1)(ag.finish)

    ins = (x2, norm_in, w_in_pad, q_norm, wuq_pad, kv_norm, wukv, rc, rsa, rsb)
    in_specs = [_row_spec(tm, D_MODEL), _full_spec(norm_in), _full_spec(w_in_pad), _full_spec(q_norm), _full_spec(wuq_pad),
                _full_spec(kv_norm), _full_spec(wukv), _row_spec(tm, LANES), _row_spec(tm, LANES), _row_spec(tm, LANES)]
    widths = [(D_MODEL, BF16), (ZTOT - GM, F32), (UP - GA, F32), (GP - UP, F32), (GM - GP, F32), (FRONT_W, F32),
              (HW, BF16), (HW, BF16), (HW, BF16)]
    return pl.pallas_call(
        body, name="in_proj", grid=(steps,), in_specs=in_specs + [_full_spec(s) for s in late_shards],
        out_specs=[_row_spec(tm, w) for w, _ in widths] + [pl.BlockSpec((HW, tm), lambda i: (0, i))] + [HBM_SPEC] * n,
        out_shape=[SDS((T, w), dt) for w, dt in widths] + [SDS((HW, T), BF16)]
        + [SDS((N_DEV,) + s.shape, BF16) for s in late_shards],
        scratch_shapes=_ag_scratch([s.shape for s in late_shards]), compiler_params=_cp(("arbitrary",)),
    )(*ins, *late_shards)


def _chunk_mask(n_q, n_k, q_off, transposed):
    shape = (n_k, n_q) if transposed else (n_q, n_k)
    q = (lax.broadcasted_iota(jnp.int32, shape, 1 if transposed else 0) + q_off) // CHUNK
    k = lax.broadcasted_iota(jnp.int32, shape, 0 if transposed else 1) // CHUNK
    return k <= q


def _store_pair_rows(ref, k, pair):
    t = pair.T
    ref[k, 0:1, :] = t[0:1, :]
    ref[k, 1:2, :] = t[VDIM:VDIM + 1, :]


def _attn_fwd(q_att, k_att, vt_att, tq, hps):
    T = q_att.shape[0]
    head_lanes = [slice(h * LANES, (h + 1) * LANES) for h in range(hps)]

    def body(q_ref, k_ref, vt_ref, o_ref, lser_ref):
        i = pl.program_id(1)
        mask = _chunk_mask(tq, tq, 0, True)
        lane = lax.broadcasted_iota(jnp.int32, (tq, LANES), 1)
        qs = [q_ref[:, hs] for hs in head_lanes]

        def step(j, carry, masked):
            off = pl.multiple_of(j * tq, tq)
            sts = [_mm_nt(k_ref[pl.ds(off, tq), hs], qh) for qh, hs in zip(qs, head_lanes)]
            if masked:
                sts = [jnp.where(mask, st, -jnp.inf) for st in sts]
            ms = [jnp.maximum(m, jnp.max(st, axis=0, keepdims=True)) for (m, _), st in zip(carry, sts)]
            pts = [jnp.exp2(st - m_new).astype(BF16) for st, m_new in zip(sts, ms)]
            return tuple((m_new, jnp.exp2(m - m_new) * acc + _mm(vt_ref[hs, pl.ds(off, tq)], pt))
                         for (m, acc), m_new, pt, hs in zip(carry, ms, pts, head_lanes))

        init = ((jnp.full((1, tq), -jnp.inf, F32), jnp.zeros((LANES, tq), F32)),) * hps
        res = step(i, lax.fori_loop(0, i, functools.partial(step, masked=False), init), True)
        for pair in range(hps // 2):
            (ma, acca), (mb, accb) = res[2 * pair], res[2 * pair + 1]
            la, lb = acca[:1], accb[:1]
            oa, ob = (acca / la).T, (accb / lb).T
            o_ref[:, pair * LANES:(pair + 1) * LANES] = jnp.where(lane < VDIM, pltpu.roll(oa, VDIM, 1), ob)
            lser_ref[pair, 0:1, :] = ma + jnp.log2(la)
            lser_ref[pair, 1:2, :] = mb + jnp.log2(lb)

    qspec = pl.BlockSpec((tq, hps * LANES), lambda p, i: (i, p))
    kspec = pl.BlockSpec((T, hps * LANES), lambda p, i: (0, p))
    vspec = pl.BlockSpec((hps * LANES, T), lambda p, i: (p, 0))
    ospec = pl.BlockSpec((tq, hps * VDIM), lambda p, i: (i, p))
    return pl.pallas_call(
        body, name="attn_fwd", grid=(HEADS // hps, T // tq), in_specs=[qspec, kspec, vspec],
        out_specs=[ospec, pl.BlockSpec((hps // 2, 2, tq), lambda p, i: (p, 0, i))],
        out_shape=[SDS((T, MLA_W), F32), SDS((HEADS // 2, 2, T), F32)],
        compiler_params=_cp(("parallel", "parallel")),
    )(q_att, k_att, vt_att)


def _pick(g, vals):
    out = vals[-1]
    for k in range(len(vals) - 2, -1, -1):
        out = jnp.where(g == k, vals[k], out)
    return out


def _window_sum(u, g, forward):
    T = u.shape[0]
    row = lax.broadcasted_iota(jnp.int32, u.shape, 0)

    def sh(s, k):
        if forward:
            return jnp.where(row >= k, pltpu.roll(s, k, 0), 0.0)
        return jnp.where(row < T - k, pltpu.roll(s, T - k, 0), 0.0)

    sums, s = [], u
    for k in (1, 2, 4, 8):
        s = s + sh(s, k)
        sums.append(s)
    return _pick(g, sums)


MAX_WINDOW = 16


def _pool_inv_count(shape, g):
    T, n = shape
    row = lax.broadcasted_iota(jnp.int32, (MAX_WINDOW, n), 0)
    head = 1.0 / jnp.minimum(row + 1, lax.shift_left(jnp.int32(2), g)).astype(F32)
    inv_w = _pick(g, [0.5, 0.25, 0.125, 0.0625])
    return jnp.concatenate([head, jnp.broadcast_to(inv_w, (T - MAX_WINDOW, n)).astype(F32)], axis=0)


def _pool_fwd(zup, zgp, pool_w, pool_scale):
    T = zup.shape[0]

    def body(u_ref, g_ref, w_ref, sc_ref, y_ref):
        g = pl.program_id(0)
        u = u_ref[...]
        d = _window_sum(u, g, True) * _pool_inv_count(u.shape, g) - u
        lin = _mm(d.astype(BF16), w_ref[0].astype(BF16))
        silu, _ = _silu_parts(g_ref[...])
        y_ref[...] = (lin * sc_ref[...] * silu).astype(BF16)

    col = pl.BlockSpec((T, GROUP), lambda g: (0, g))
    return pl.pallas_call(
        body, name="pool_fwd", grid=(POOL_GROUPS,),
        in_specs=[col, col, pl.BlockSpec((1, GROUP, GROUP), lambda g: (g, 0, 0)), pl.BlockSpec((1, GROUP), lambda g: (0, g))],
        out_specs=col, out_shape=SDS((T, POOL_W), BF16), compiler_params=_cp(("parallel",)),
    )(zup, zgp, pool_w, pool_scale)


def _pool_bwd(zup, zgp, dyp, pool_w, pool_scale):
    T = zup.shape[0]

    def body(u_ref, g_ref, dy_ref, w_ref, sc_ref, du_ref, dg_ref, gw_ref, gsc_ref):
        g = pl.program_id(0)
        u = u_ref[...]
        inv = _pool_inv_count(u.shape, g)
        d = (_window_sum(u, g, True) * inv - u).astype(BF16)
        wb = w_ref[0].astype(BF16)
        lin = _mm(d, wb)
        sc = sc_ref[...]
        silu, dsilu = _silu_parts(g_ref[...])
        dy = dy_ref[...]
        dg_ref[...] = (dy * lin * sc * dsilu).astype(BF16)
        dpre = dy * silu
        gsc_ref[...] = jnp.sum(dpre * lin, axis=0, keepdims=True)
        dlin = (dpre * sc).astype(BF16)
        gw_ref[0] = _mm_tn(d, dlin)
        dd = _mm_nt(dlin, wb)
        du_ref[...] = (_window_sum(dd * inv, g, False) - dd).astype(BF16)

    col = pl.BlockSpec((T, GROUP), lambda g: (0, g))
    wspec = pl.BlockSpec((1, GROUP, GROUP), lambda g: (g, 0, 0))
    vspec = pl.BlockSpec((1, GROUP), lambda g: (0, g))
    return pl.pallas_call(
        body, name="pool_bwd", grid=(POOL_GROUPS,), in_specs=[col, col, col, wspec, vspec], out_specs=[col, col, wspec, vspec],
        out_shape=[SDS((T, POOL_W), BF16), SDS((T, POOL_W), BF16), SDS((POOL_GROUPS, GROUP, GROUP), F32), SDS((1, POOL_W), F32)],
        compiler_params=_cp(("parallel",)),
    )(zup, zgp, dyp, pool_w, pool_scale)


def _tail(x2, tgt, o, zga, ypool, zgm, wba, wbp, wout, norm_final, tm):
    T = x2.shape[0]
    steps = T // tm
    cols = D_MODEL // N_DEV

    def body(x_ref, tgt_ref, o_ref, zga_ref, yp_ref, zgm_ref, wba_ref, wbp_ref, wout_ref, nf_ref,
             loss_ref, dh_ref, dgm_ref, doop_ref, dga_ref, dcapr_ref, dyp_ref, swout_ref, swba_ref, swbp_ref, gnf_ref,
             gwout_ref, gwba_ref, gwbp_ref):
        @pl.when(pl.program_id(0) == 0)
        def _():
            for ref in (loss_ref, gwout_ref, gwba_ref, gwbp_ref, gnf_ref):
                ref[...] = jnp.zeros_like(ref)

        o_v = o_ref[...]
        silu, dsilu = _silu_parts(zga_ref[...])
        ya = (o_v * silu).astype(BF16)
        yp = yp_ref[...]
        wba_v = jnp.concatenate([wba_ref[k] for k in range(N_DEV)], axis=1)
        wbp_v = jnp.concatenate([wbp_ref[k] for k in range(N_DEV)], axis=1)
        wout_v = wout_ref[...]
        a = _mm(ya, wba_v)
        p = _mm(yp, wbp_v)
        gate = jax.nn.sigmoid(zgm_ref[...])
        ga, gp = gate[:, :D_MODEL], gate[:, D_MODEL:]
        mg = (ga * a + gp * p).astype(BF16)
        h = x_ref[...] + _mm(mg, wout_v)
        r = lax.rsqrt(jnp.mean(h * h, axis=-1, keepdims=True) + EPS)
        gf = nf_ref[...]
        hr = h * r
        e = hr * gf - tgt_ref[...]
        loss_ref[...] += (0.5 / D_MODEL) * jnp.sum(e * e)
        dy = e * (1.0 / D_MODEL)
        gnf_ref[...] += jnp.sum(dy * hr, axis=0, keepdims=True)
        u = dy * gf
        dh = r * (u - hr * jnp.mean(u * hr, axis=-1, keepdims=True))
        dh_ref[...] = dh
        dhb = dh.astype(BF16)
        dmg = _mm_nt(dhb, wout_v)
        dab = (dmg * ga).astype(BF16)
        dpb = (dmg * gp).astype(BF16)
        dya = _mm_nt(dab, wba_v)
        dyp_ref[...] = _mm_nt(dpb, wbp_v)
        gwout_ref[...] += _mm_tn(mg, dhb)
        gwba_ref[...] += _mm_tn(ya, dab)
        gwbp_ref[...] += _mm_tn(yp, dpb)
        dgm_ref[:, :D_MODEL] = (dmg * a * ga * (1.0 - ga)).astype(BF16)
        dgm_ref[:, D_MODEL:] = (dmg * p * gp * (1.0 - gp)).astype(BF16)
        do = dya * silu
        dga_ref[...] = (dya * o_v * dsilu).astype(BF16)
        prod = do * o_v
        lo = lax.broadcasted_iota(jnp.int32, (tm, LANES), 1) < VDIM
        for pair in range(HEADS // 2):
            ls = slice(pair * LANES, (pair + 1) * LANES)
            do_p, prod_p = do[:, ls], prod[:, ls]
            dcap_a = jnp.sum(jnp.where(lo, prod_p, 0.0), axis=-1, keepdims=True)
            dcap_b = jnp.sum(jnp.where(lo, 0.0, prod_p), axis=-1, keepdims=True)
            _store_pair_rows(dcapr_ref, pair, jnp.where(lo, dcap_a, dcap_b))
            doop_ref[:, 2 * pair * LANES:(2 * pair + 1) * LANES] = jnp.where(lo, 0.0, pltpu.roll(do_p, VDIM, 1)).astype(BF16)
            doop_ref[:, (2 * pair + 1) * LANES:(2 * pair + 2) * LANES] = jnp.where(lo, 0.0, do_p).astype(BF16)

        @pl.when(pl.program_id(0) == steps - 1)
        def _():
            for k in range(N_DEV):
                swout_ref[k] = gwout_ref[k * cols:(k + 1) * cols, :].astype(BF16)
                swba_ref[k] = gwba_ref[:, k * cols:(k + 1) * cols].astype(BF16)
                swbp_ref[k] = gwbp_ref[:, k * cols:(k + 1) * cols].astype(BF16)

    ins = (x2, tgt, o, zga, ypool, zgm, wba, wbp, wout, norm_final)
    in_specs = [_row_spec(tm, D_MODEL), _row_spec(tm, D_MODEL), _row_spec(tm, MLA_W), _row_spec(tm, MLA_W), _row_spec(tm, POOL_W),
                _row_spec(tm, 2 * D_MODEL), _full_spec(wba), _full_spec(wbp), _full_spec(wout), _full_spec(norm_final)]
    outs = [SDS((8, LANES), F32), SDS((T, D_MODEL), F32), SDS((T, 2 * D_MODEL), BF16), SDS((T, HW), BF16), SDS((T, MLA_W), BF16),
            SDS((HEADS // 2, 2, T), F32), SDS((T, POOL_W), F32),
            SDS((N_DEV, cols, D_MODEL), BF16), SDS((N_DEV, MLA_W, cols), BF16), SDS((N_DEV, POOL_W, cols), BF16), SDS((1, D_MODEL), F32)]
    out_specs = [_full_spec(outs[0]), _row_spec(tm, D_MODEL), _row_spec(tm, 2 * D_MODEL), _row_spec(tm, HW), _row_spec(tm, MLA_W),
                 pl.BlockSpec((HEADS // 2, 2, tm), lambda i: (0, 0, i)), _row_spec(tm, POOL_W),
                 _full_spec(outs[7]), _full_spec(outs[8]), _full_spec(outs[9]), _full_spec(outs[10])]
    return pl.pallas_call(
        body, name="tail", grid=(steps,), in_specs=in_specs, out_specs=out_specs, out_shape=outs,
        scratch_shapes=[pltpu.VMEM((D_MODEL, D_MODEL), F32), pltpu.VMEM((MLA_W, D_MODEL), F32), pltpu.VMEM((POOL_W, D_MODEL), F32)],
        compiler_params=_cp(("arbitrary",)),
    )(*ins)


def _attn_bwd(q_att, k_att, v_att, doop, lse_rows, dcap_rows, tq, hps, slabs, packed):
    T = q_att.shape[0]
    nq = T // tq
    n = len(slabs)
    groups = HEADS // hps
    head_lanes = [slice(h * LANES, (h + 1) * LANES) for h in range(hps)]

    def body(q_ref, k_ref, v_ref, doop_ref, lse_ref, dcap_ref, *rest):
        slab_refs, packed_ref = rest[:n], rest[n]
        dq_ref, dkv_ref, dkr_ref = rest[n + 1:n + 4]
        sum_refs, ptot_ref = rest[n + 4:2 * n + 4], rest[2 * n + 4]
        dq_acc = rest[2 * n + 5]
        rs = _ReduceScatter(slab_refs, packed_ref, sum_refs, ptot_ref, rest[2 * n + 6:])
        group, j = pl.program_id(0), pl.program_id(1)
        tick, last = group * nq + j, groups * nq - 1
        pl.when(tick == 0)(rs.start1)
        pl.when(tick == min(1, last))(rs.finish1_start2)
        pl.when(tick == min(3, last))(rs.relay2)
        mask = _chunk_mask(tq, tq, 0, True)
        lane = lax.broadcasted_iota(jnp.int32, (tq, LANES), 1)
        ks = [k_ref[:, hs] for hs in head_lanes]
        vs = [v_ref[:, hs] for hs in head_lanes]
        kts = [kh.T for kh in ks]

        @pl.when(j == 0)
        def _():
            dq_acc[...] = jnp.zeros_like(dq_acc)

        def step(i, carry, masked):
            rows = pl.ds(pl.multiple_of(i * tq, tq), tq)
            heads = range(hps)
            stat = lambda h: (h // 2, slice(h % 2, h % 2 + 1), rows)
            qhs = [q_ref[rows, hs] for hs in head_lanes]
            doops = [doop_ref[rows, hs] for hs in head_lanes]
            sts = [_mm_nt(ks[h], qhs[h]) for h in heads]
            dpts = [_mm_nt(vs[h], doops[h]) for h in heads]
            pts = [jnp.exp2(sts[h] - lse_ref[stat(h)]) for h in heads]
            if masked:
                pts = [jnp.where(mask, pt, 0.0) for pt in pts]
            dsts = [(pts[h] * (dpts[h] - dcap_ref[stat(h)])).astype(BF16) for h in heads]
            dvs = [_mm(pts[h].astype(BF16), doops[h]) for h in heads]
            dks = [_mm(dsts[h], qhs[h]) for h in heads]
            for h, hs in enumerate(head_lanes):
                dq_acc[hs, rows] += _mm(kts[h], dsts[h])
            return tuple((dk + dks[h], dv + dvs[h]) for h, (dk, dv) in enumerate(carry))

        zero = jnp.zeros((tq, LANES), F32)
        carry = step(j, ((zero, zero),) * hps, True)
        res = lax.fori_loop(j + 1, nq, functools.partial(step, masked=False), carry)
        dkr = None
        for (dk, dv), hs in zip(res, head_lanes):
            dk = dk * LN2
            dkv_ref[:, hs] = jnp.where(lane < NOPE, dk, dv).astype(BF16)
            dkr = dk if dkr is None else dkr + dk
        dkr_ref[0] = jnp.where((lane >= NOPE) & (lane < NOPE + ROPE), dkr, 0.0)

        @pl.when(j == nq - 1)
        def _():
            dq_ref[...] = (dq_acc[...] * SCALE).T.astype(BF16)

        pl.when(tick == last)(rs.finish2)

    kspec = pl.BlockSpec((tq, hps * LANES), lambda p, j: (j, p))
    qspec = pl.BlockSpec((T, hps * LANES), lambda p, j: (0, p))
    rspec = pl.BlockSpec((hps // 2, 2, T), lambda p, j: (p, 0, 0))
    sums = [SDS(s.shape[1:], F32) for s in slabs] + [SDS(packed.shape, F32)]
    return pl.pallas_call(
        body, name="attn_bwd", grid=(groups, nq),
        in_specs=[qspec, kspec, kspec, qspec, rspec, rspec] + [HBM_SPEC] * n + [_full_spec(packed)],
        out_specs=[qspec, kspec, pl.BlockSpec((1, tq, LANES), lambda p, j: (p, j, 0))] + [_full_spec(s) for s in sums],
        out_shape=[SDS((T, HW), BF16), SDS((T, HW), BF16), SDS((groups, T, LANES), F32)] + sums,
        scratch_shapes=[pltpu.VMEM((hps * LANES, T), F32)] + _rs_scratch([s.shape for s in sums[:-1]], packed.shape),
        compiler_params=_cp(("arbitrary", "arbitrary")),
    )(q_att, k_att, v_att, doop, lse_rows, dcap_rows, *slabs, packed)


def _rms_bwd(z, gain, dout):
    r = lax.rsqrt(jnp.mean(z * z, axis=-1, keepdims=True) + EPS)
    zr = z * r
    u = dout * gain
    return r * (u - zr * jnp.mean(u * zr, axis=-1, keepdims=True)), jnp.sum(dout * zr, axis=0, keepdims=True)


def _mla_bwd(dq_att, dkv_nat, dkr4, zfr, q_norm, wuq_pad, kv_norm, wukv, rc, rsa, rsb, tm):
    T = dq_att.shape[0]

    def body(dq_ref, dkv_ref, dkr_ref, zfr_ref, qn_ref, wuq_ref, kvn_ref, wukv_ref, c_ref, sa_ref, sb_ref,
             dfr_ref, gwuq_out, gwukv_out, gqn_ref, gkvn_ref, gwuq_ref, gwukv_ref):
        @pl.when(pl.program_id(0) == 0)
        def _():
            for ref in (gwuq_ref, gwukv_ref, gqn_ref, gkvn_ref):
                ref[...] = jnp.zeros_like(ref)

        c, sa, sb = c_ref[...], sa_ref[...], sb_ref[...]
        zq, zkv = zfr_ref[:, :Q_RANK], zfr_ref[:, Q_RANK:Q_RANK + KV_RANK]
        qn, kvn = qn_ref[...].reshape(1, Q_RANK), kvn_ref[...]
        dkv = dkv_ref[...]
        dckv = _mm_nt(dkv, wukv_ref[...])
        ckv = (zkv * lax.rsqrt(jnp.mean(zkv * zkv, axis=-1, keepdims=True) + EPS) * kvn).astype(BF16)
        gwukv_ref[...] += _mm_tn(ckv, dkv)
        cq = (zq * lax.rsqrt(jnp.mean(zq * zq, axis=-1, keepdims=True) + EPS) * qn).astype(BF16)
        dq = _rope(dq_ref[...].astype(F32), c, sa, sb, -1.0).astype(BF16)
        dzkv, gkvn = _rms_bwd(zkv, kvn, dckv)
        gkvn_ref[...] += gkvn
        gwuq_ref[...] += _mm_tn(cq, dq)
        dzq, gqn = _rms_bwd(zq, qn, _mm_nt(dq, wuq_ref[...]))
        gqn_ref[...] += gqn
        dkr = functools.reduce(lambda a, b: a + b, [dkr_ref[g] for g in range(dkr4.shape[0])])
        dfr_ref[:, :Q_RANK] = dzq.astype(BF16)
        dfr_ref[:, Q_RANK:Q_RANK + KV_RANK] = dzkv.astype(BF16)
        dfr_ref[:, Q_RANK + KV_RANK:] = _rope(dkr, c, sa, sb, -1.0).astype(BF16)

        @pl.when(pl.program_id(0) == pl.num_programs(0) - 1)
        def _():
            gwuq_out[...] = gwuq_ref[...].astype(BF16)
            gwukv_out[...] = gwukv_ref[...].astype(BF16)

    ins = (dq_att, dkv_nat, dkr4, zfr, q_norm, wuq_pad, kv_norm, wukv, rc, rsa, rsb)
    in_specs = [_row_spec(tm, HW), _row_spec(tm, HW), pl.BlockSpec((dkr4.shape[0], tm, LANES), lambda i: (0, i, 0)), _row_spec(tm, FRONT_W),
                _full_spec(q_norm), _full_spec(wuq_pad), _full_spec(kv_norm), _full_spec(wukv),
                _row_spec(tm, LANES), _row_spec(tm, LANES), _row_spec(tm, LANES)]
    outs = [SDS((T, FRONT_W), BF16), SDS((Q_RANK, HW), BF16), SDS((KV_RANK, HW), BF16), SDS((1, Q_RANK), F32), SDS((1, KV_RANK), F32)]
    out_specs = [_row_spec(tm, FRONT_W)] + [_full_spec(s) for s in outs[1:]]
    return pl.pallas_call(
        body, name="mla_bwd", grid=(T // tm,), in_specs=in_specs, out_specs=out_specs, out_shape=outs,
        scratch_shapes=[pltpu.VMEM((Q_RANK, HW), F32), pltpu.VMEM((KV_RANK, HW), F32)],
        compiler_params=_cp(("arbitrary",)),
    )(*ins)


_DZ_COLS = ((GM, ZTOT), (GA, UP), (UP, GP), (GP, GM), (ZQ, GA))


def _in_proj_bwd_x(dzs, x2, dh, norm_in, w_in_pad, tm, slabs):
    T = x2.shape[0]
    steps = T // tm
    n = len(slabs)

    def body(d0, d1, d2, d3, d4, x_ref, dh_ref, nin_ref, win_ref, *rest):
        slab_refs, (gx_ref, gnin_ref), sum_refs = rest[:n], rest[n:n + 2], rest[n + 2:2 * n + 2]
        rs = _ReduceScatter(slab_refs, None, sum_refs, None, rest[2 * n + 2:])
        step = pl.program_id(0)

        @pl.when(step == 0)
        def _():
            gnin_ref[...] = jnp.zeros_like(gnin_ref)
            rs.start1()

        pl.when(step == min(2, steps - 1))(rs.finish1_start2)
        pl.when(step == min(steps * 11 // 16, steps - 1))(rs.relay2)
        dhn = None
        for ref, (lo, hi) in zip((d0, d1, d2, d3, d4), _DZ_COLS):
            t = _mm(ref[...], win_ref[lo:hi, :])
            dhn = t if dhn is None else dhn + t
        dx, gnin = _rms_bwd(x_ref[...], nin_ref[...], dhn)
        gnin_ref[...] += gnin
        gx_ref[...] = dx + dh_ref[...]
        pl.when(step == steps - 1)(rs.finish2)

    in_specs = [_row_spec(tm, hi - lo) for lo, hi in _DZ_COLS] + [_row_spec(tm, D_MODEL), _row_spec(tm, D_MODEL),
                                                                  _full_spec(norm_in), _full_spec(w_in_pad)] + [HBM_SPEC] * n
    sums = [SDS(s.shape[1:], F32) for s in slabs]
    outs = [SDS((T, D_MODEL), F32), SDS((1, D_MODEL), F32)] + sums
    return pl.pallas_call(
        body, name="in_proj_bwd_x", grid=(steps,), in_specs=in_specs,
        out_specs=[_row_spec(tm, D_MODEL), _full_spec(outs[1])] + [_full_spec(s) for s in sums],
        out_shape=outs, scratch_shapes=_rs_scratch([s.shape for s in sums], None), compiler_params=_cp(("arbitrary",)),
    )(*dzs, x2, dh, norm_in, w_in_pad, *slabs)


SLAB_ROWS = IN_TOTAL // N_DEV


def _slab_segments(k):
    cuts = [(0, ZKR_ORIG, 0), (ZKR_ORIG, ZKR_ORIG + ROPE, NOPE), (ZKR_ORIG + ROPE, IN_TOTAL, LANES - ROPE)]
    lo, hi = k * SLAB_ROWS, (k + 1) * SLAB_ROWS
    return [(max(lo, a) - lo, max(lo, a) + shift, min(hi, b) - max(lo, a)) for a, b, shift in cuts if min(hi, b) > max(lo, a)]


def _in_proj_bwd_w(dzs, hn, tm):
    T = hn.shape[0]
    steps = T // tm

    def body(d0, d1, d2, d3, d4, hn_ref, slab_ref, acc_ref):
        @pl.when(pl.program_id(0) == 0)
        def _():
            acc_ref[...] = jnp.zeros_like(acc_ref)

        hn_v = hn_ref[...]
        for ref, (lo, hi) in zip((d0, d1, d2, d3, d4), _DZ_COLS):
            acc_ref[lo:hi, :] += _mm_tn(ref[...], hn_v)

        @pl.when(pl.program_id(0) == steps - 1)
        def _():
            for k in range(N_DEV):
                for at, src, rows in _slab_segments(k):
                    slab_ref[k, at:at + rows, :] = acc_ref[src:src + rows, :].astype(BF16)

    in_specs = [_row_spec(tm, hi - lo) for lo, hi in _DZ_COLS] + [_row_spec(tm, D_MODEL)]
    out = SDS((N_DEV, SLAB_ROWS, D_MODEL), BF16)
    return pl.pallas_call(
        body, name="in_proj_bwd_w", grid=(steps,), in_specs=in_specs, out_specs=_full_spec(out), out_shape=out,
        scratch_shapes=[pltpu.VMEM((ZTOT, D_MODEL), F32)], compiler_params=_cp(("arbitrary",)),
    )(*dzs, hn)


def _local_step(x2, tgt, norm_in, w_in_pad, q_norm, wuq_pad, kv_norm, wukv, pool_w, pool_scale, late_shards, norm_final):
    T = x2.shape[0]
    tm, tm_small, tq = min(512, T), min(256, T), min(512, T)
    heads_fwd, heads_bwd = 4, 4
    row = lambda v: v.reshape(1, -1)
    rc, rsa, rsb = _rope_tables(T)

    hn, zgm, zga, zup, zgp, zfr, q_att, k_att, v_att, vt_att, w_ba, w_bp, w_out = _in_proj(
        x2, row(norm_in), w_in_pad, q_norm, wuq_pad, row(kv_norm), wukv, rc, rsa, rsb, tm, late_shards)
    w_out = w_out.reshape(D_MODEL, D_MODEL)
    o, lse_rows = _attn_fwd(q_att, k_att, vt_att, tq, heads_fwd)
    ypool = _pool_fwd(zup, zgp, pool_w, row(pool_scale))
    loss8, dh, dgm, doop, dga, dcap_rows, dyp, *slabs, g_nf = _tail(
        x2, tgt, o, zga, ypool, zgm, w_ba, w_bp, w_out, row(norm_final), tm_small)
    dup, dgp, g_pool_w, g_pool_scale = _pool_bwd(zup, zgp, dyp, pool_w, row(pool_scale))

    early = [g_pool_w, g_pool_scale, g_nf, loss8[0]]
    packed = jnp.concatenate([_pack_rows(a) for a in early], axis=0)
    dq_att, dkv_nat, dkr4, s_wout, s_wba, s_wbp, tot_early = _attn_bwd(
        q_att, k_att, v_att, doop, lse_rows, dcap_rows, tq, heads_bwd, slabs, packed)
    s_pool_w, s_pool_scale, s_nf, s_loss = _unpack_rows(tot_early, early)

    dfr, g_wuq_pad, g_wukv, g_qn, g_kvn = _mla_bwd(
        dq_att, dkv_nat, dkr4, zfr, q_norm, wuq_pad, row(kv_norm), wukv, rc, rsa, rsb, tm)
    dzs = (dgm, dga, dup, dgp, dfr)
    slabs = [_in_proj_bwd_w(dzs, hn, tm), g_wuq_pad.reshape(N_DEV, Q_RANK // N_DEV, HW), g_wukv.reshape(N_DEV, KV_RANK // N_DEV, HW)]
    grad_x, g_nin, s_win, s_wuq, s_wukv = _in_proj_bwd_x(dzs, x2, dh, row(norm_in), w_in_pad, tm_small, slabs)
    late = [g_nin, g_qn, g_kvn]
    (tot_late,) = _reduce_scatter([], jnp.concatenate([_pack_rows(a) for a in late], axis=0))
    s_nin, s_qn, s_kvn = _unpack_rows(tot_late, late)

    grads = dict(norm_in=s_nin, w_in=s_win, q_norm=s_qn, w_uq=s_wuq, kv_norm=s_kvn, w_ukv=s_wukv, pool_w=s_pool_w.reshape(-1, GROUP),
                 pool_scale=s_pool_scale, w_branch_attn=s_wba, w_branch_pool=s_wbp, w_out=s_wout, norm_final=s_nf)
    return s_loss[0], grad_x, grads


MESH_ID = pl.DeviceIdType.MESH
VMEM_SPEC = pl.BlockSpec(memory_space=pltpu.VMEM)
HBM_SPEC = pl.BlockSpec(memory_space=pl.ANY)


def _mesh_pos():
    return lax.axis_index("x"), lax.axis_index("y"), lax.axis_index("c")


BF16_TILE_ROWS = 16


def _half_rows(rows):
    cut = -(-(rows // 2) // BF16_TILE_ROWS) * BF16_TILE_ROWS
    return pl.ds(0, cut), pl.ds(cut, rows - cut)


def _slot(px, py, pc):
    return 4 * px + 2 * py + pc


def _staged_shape(shape):
    return (shape[0], shape[1] * LANES) if len(shape) == 3 else tuple(shape)


def _all_gather_bf16(shards):
    n = len(shards)
    staged = [_staged_shape(s.shape) for s in shards]

    def body(*refs):
        ins, outs = refs[:n], refs[n:2 * n]
        land0, scratch = refs[2 * n], refs[2 * n + 1:]
        wpad_ref = outs[0]
        ag = _AllGather(ins, (land0,) + tuple(outs[1:]), scratch)
        ag.start()
        ag.forward()
        ag.finish()
        wpad_ref[ZKR:GA, :] = jnp.zeros((GA - ZKR, D_MODEL), BF16)
        for k in range(N_DEV):
            for at, dst, rows in _slab_segments(k):
                wpad_ref[dst:dst + rows, :] = land0[k, at:at + rows, :]

    return pl.pallas_call(
        body, name="all_gather_weights",
        in_specs=[VMEM_SPEC] * n, out_specs=[VMEM_SPEC] + [HBM_SPEC] * (n - 1),
        out_shape=[SDS((ZTOT, D_MODEL), BF16)] + [SDS((N_DEV,) + s, BF16) for s in staged[1:]],
        scratch_shapes=[pltpu.VMEM((N_DEV,) + staged[0], BF16)] + _ag_scratch(staged),
        compiler_params=_cp(),
    )(*shards)


def _ag_scratch(shapes):
    n = len(shapes)
    dma = pltpu.SemaphoreType.DMA
    return [pltpu.VMEM(tuple(s), BF16) for s in shapes] + [dma((_AllGather.COPIES * n,)), dma((_AllGather.COPIES * n,)), dma((n,))]


class _AllGather:
    COPIES = 8

    def __init__(self, in_refs, dest_refs, scratch):
        n = self.n = len(in_refs)
        self.ins, self.dests, self.stage = in_refs, dest_refs, scratch[:n]
        self.send_sems, self.recv_sems, self.local_sems = scratch[n:]
        x, y, c = _mesh_pos()
        self.c, self.me, self.sibling = c, (x, y, c), (x, y, 1 - c)
        self.xn, self.yn, self.diag = (1 - x, y), (x, 1 - y), (1 - x, 1 - y)

    def _halves(self, a):
        return _half_rows(self.stage[a].shape[0])

    def _copy(self, a, k, block, to, from_stage=False, rows=None):
        dst = self.dests[a].at[_slot(*block)]
        src = self.stage[a] if from_stage else dst
        if rows is not None:
            src, dst = src.at[rows], dst.at[rows]
        return pltpu.make_async_remote_copy(
            src_ref=src, dst_ref=dst, send_sem=self.send_sems.at[self.COPIES * a + k],
            recv_sem=self.recv_sems.at[self.COPIES * a + k], device_id=to, device_id_type=MESH_ID)

    def _mine(self):
        return [pltpu.make_async_copy(self.stage[a], self.dests[a].at[_slot(*self.me)], self.local_sems.at[a]) for a in range(self.n)]

    def _first(self, a):
        return [self._copy(a, 0, self.me, self.sibling, True), self._copy(a, 1, self.me, (*self.xn, self.c), True),
                self._copy(a, 2, self.me, (*self.yn, self.c), True)]

    def _relays(self, a):
        lo, hi = self._halves(a)
        return [self._copy(a, 3, (*self.xn, self.c), (*self.yn, self.c), rows=lo),
                self._copy(a, 4, (*self.yn, self.c), (*self.xn, self.c), rows=hi)]

    def _passes(self, a):
        return [self._copy(a, 5 + j, (*chip, self.c), self.sibling) for j, chip in enumerate((self.xn, self.yn, self.diag))]

    def start(self):
        for a in range(self.n):
            src, dst = self.ins[a], self.stage[a]
            if src.shape == dst.shape:
                dst[...] = src[...].astype(BF16)
            else:
                if src.shape[2] < LANES:
                    dst[...] = jnp.zeros(dst.shape, BF16)
                for h in range(src.shape[1]):
                    dst[:, h * LANES:h * LANES + src.shape[2]] = src[:, h, :].astype(BF16)
        for cp in self._mine():
            cp.start()
        for a in range(self.n):
            for cp in self._first(a):
                cp.start()

    def forward(self):
        for a in range(self.n):
            relays, passes = self._relays(a), self._passes(a)
            for j, chip in enumerate((self.xn, self.yn)):
                self._copy(a, 1 + j, (*chip, self.c), self.me).wait_recv()
                relays[j].start()
                passes[j].start()

    def finish(self):
        for a in range(self.n):
            lo, hi = self._halves(a)
            self._copy(a, 3, (*self.diag, self.c), self.me, rows=lo).wait_recv()
            self._copy(a, 4, (*self.diag, self.c), self.me, rows=hi).wait_recv()
            self._passes(a)[2].start()
        for a in range(self.n):
            self._copy(a, 0, self.sibling, self.me).wait_recv()
            for j, chip in enumerate((self.xn, self.yn, self.diag)):
                self._copy(a, 5 + j, (*chip, 1 - self.c), self.me).wait_recv()
            for cp in self._first(a) + self._relays(a) + self._passes(a):
                cp.wait_send()
        for cp in self._mine():
            cp.wait()


N_CHIPS = 4


def _reduce_scatter(slabs, packed):
    def body(*refs):
        n = len(slabs)
        rs = _ReduceScatter(refs[:n], refs[n], refs[n + 1:2 * n + 1], refs[2 * n + 1], refs[2 * n + 2:])
        rs.start1()
        rs.finish1_start2()
        rs.relay2()
        rs.finish2()

    shapes = [s.shape[1:] for s in slabs]
    return pl.pallas_call(
        body, name="reduce_scatter_grads",
        in_specs=[HBM_SPEC] * len(slabs) + [VMEM_SPEC], out_specs=[VMEM_SPEC] * (len(slabs) + 1),
        out_shape=[SDS(s, F32) for s in shapes] + [SDS(packed.shape, F32)],
        scratch_shapes=_rs_scratch(shapes, packed.shape), compiler_params=_cp(),
    )(*slabs, packed)


def _rs_scratch(shapes, packed_shape):
    n = len(shapes)
    n1, n2 = N_CHIPS * n + 1, _ReduceScatter.L2_COPIES * n + N_CHIPS - 1
    dma = pltpu.SemaphoreType.DMA
    packed = [] if packed_shape is None else [pltpu.VMEM(packed_shape, F32), pltpu.VMEM((N_CHIPS,) + tuple(packed_shape), F32)]
    return ([pltpu.VMEM((N_CHIPS,) + tuple(s), BF16) for s in shapes] * 2 + [pltpu.VMEM((N_CHIPS - 1,) + tuple(s), BF16) for s in shapes] * 2
            + packed + [dma((max(N_CHIPS * n, 1),)), dma((n1,)), dma((n1,)), dma((n2,)), dma((n2,))])


class _ReduceScatter:
    L2_COPIES = 6

    def __init__(self, slab_refs, packed_ref, out_refs, ptot_ref, scratch):
        n = self.n = len(slab_refs)
        self.slabs, self.packed, self.outs, self.ptot = slab_refs, packed_ref, out_refs, ptot_ref
        self.own1, self.land1, self.send2, self.land2 = (scratch[k * n:(k + 1) * n] for k in range(4))
        rest = scratch[4 * n:]
        if packed_ref is not None:
            self.pland1, self.pland2 = rest[:2]
            rest = rest[2:]
        self.loc_sems, self.send1_sems, self.recv1_sems, self.send2_sems, self.recv2_sems = rest
        self.x, self.y, self.c = _mesh_pos()

    def _chip(self, r):
        return (1 - self.x if r & 2 else self.x, 1 - self.y if r & 1 else self.y)

    @staticmethod
    def _remote(src, dst, send_sem, recv_sem, to):
        return pltpu.make_async_remote_copy(src_ref=src, dst_ref=dst, send_sem=send_sem, recv_sem=recv_sem, device_id=to,
                                            device_id_type=MESH_ID)

    def _copies1(self):
        c, sibling = self.c, (self.x, self.y, 1 - self.c)
        cps = []
        for a in range(self.n):
            for r in range(N_CHIPS):
                k = N_CHIPS * a + r
                cps.append(pltpu.make_async_copy(self.slabs[a].at[_slot(*self._chip(r), c)], self.own1[a].at[r], self.loc_sems.at[k]))
                cps.append(self._remote(self.slabs[a].at[_slot(*self._chip(r), 1 - c)], self.land1[a].at[r],
                                        self.send1_sems.at[k], self.recv1_sems.at[k], sibling))
        if self.packed is not None:
            k = N_CHIPS * self.n
            cps.append(self._remote(self.packed, self.pland1, self.send1_sems.at[k], self.recv1_sems.at[k], sibling))
        return cps

    def _halves(self, a):
        return _half_rows(self.send2[a].shape[1])

    def _copy2(self, a, k):
        lo, hi = self._halves(a)
        xn, yn = (*self._chip(2), self.c), (*self._chip(1), self.c)
        slot, rows, to = [(1, lo, xn), (2, lo, xn), (0, hi, yn), (2, hi, yn), (0, lo, yn), (1, hi, xn)][k]
        return self._remote(self.send2[a].at[slot].at[rows], self.land2[a].at[slot].at[rows],
                            self.send2_sems.at[self.L2_COPIES * a + k], self.recv2_sems.at[self.L2_COPIES * a + k], to)

    def _copies2_packed(self):
        base = self.L2_COPIES * self.n - 1
        return [self._remote(self.pland2.at[0], self.pland2.at[r], self.send2_sems.at[base + r], self.recv2_sems.at[base + r],
                             (*self._chip(r), self.c)) for r in range(1, N_CHIPS)]

    def start1(self):
        for cp in self._copies1():
            cp.start()

    def finish1_start2(self):
        for cp in self._copies1():
            cp.wait()
        for a in range(self.n):
            self.outs[a][...] = self.own1[a][0].astype(F32) + self.land1[a][0].astype(F32)
            for r in range(1, N_CHIPS):
                self.send2[a][r - 1] = (self.own1[a][r].astype(F32) + self.land1[a][r].astype(F32)).astype(BF16)
            for k in range(4):
                self._copy2(a, k).start()
        if self.packed is not None:
            self.pland2[0] = self.packed[...] + self.pland1[...]
            for cp in self._copies2_packed():
                cp.start()

    def relay2(self):
        for a in range(self.n):
            lo, hi = self._halves(a)
            s2, l2 = self.send2[a], self.land2[a]
            self._copy2(a, 1).wait_recv()
            s2[0, lo] = (s2[0, lo].astype(F32) + l2[2, lo].astype(F32)).astype(BF16)
            self._copy2(a, 4).start()
            self._copy2(a, 3).wait_recv()
            s2[1, hi] = (s2[1, hi].astype(F32) + l2[2, hi].astype(F32)).astype(BF16)
            self._copy2(a, 5).start()

    def finish2(self):
        for a in range(self.n):
            for k in (0, 2, 4, 5):
                self._copy2(a, k).wait_recv()
            for k in range(self.L2_COPIES):
                self._copy2(a, k).wait_send()
            l2 = self.land2[a]
            self.outs[a][...] = self.outs[a][...] + (l2[0].astype(F32) + l2[1].astype(F32))
        if self.packed is not None:
            for cp in self._copies2_packed():
                cp.wait()
            p2 = self.pland2
            self.ptot[...] = (p2[0] + p2[1]) + (p2[2] + p2[3])


def _adamw(ws, gs, ms, vs, rewrite):
    n = len(ws)
    regrouped = [k for k in range(n) if gs[k].shape != ws[k].shape or rewrite[k]]

    def body(*refs):
        for k in range(n):
            w_ref, g_ref, m_ref, v_ref, d_ref, nm_ref, nv_ref = (refs[j * n + k] for j in range(7))
            windows = [(..., ...)]
            if g_ref.shape != w_ref.shape:
                windows = [((slice(None), h), (slice(None), slice(h * LANES, h * LANES + w_ref.shape[2]))) for h in range(w_ref.shape[1])]
            if k in regrouped:
                g_out_ref = refs[7 * n + regrouped.index(k)]
            for at, g_at in windows:
                w, g, m, v = w_ref[at], g_ref[g_at], m_ref[at], v_ref[at]
                m = ADAM_B1 * m + (1.0 - ADAM_B1) * g
                v = ADAM_B2 * v + (1.0 - ADAM_B2) * jnp.square(g)
                m_hat = m / (1.0 - ADAM_B1 ** ADAM_STEP)
                v_hat = v / (1.0 - ADAM_B2 ** ADAM_STEP)
                d_ref[at] = -ADAM_LR * (m_hat / (jnp.sqrt(v_hat) + ADAM_EPS) + ADAM_WD * w)
                nm_ref[at] = m
                nv_ref[at] = v
                if k in regrouped:
                    g_out_ref[at] = g

    shapes = [SDS(w.shape, F32) for w in ws]
    outs = pl.pallas_call(
        body, name="adamw", in_specs=[VMEM_SPEC] * (4 * n), out_specs=[VMEM_SPEC] * (3 * n + len(regrouped)),
        out_shape=shapes * 3 + [shapes[k] for k in regrouped], compiler_params=_cp(),
    )(*ws, *gs, *ms, *vs)
    grads = list(gs)
    for k, g in zip(regrouped, outs[3 * n:]):
        grads[k] = g
    return outs[:n], outs[n:2 * n], outs[2 * n:3 * n], grads


WEIGHTS = ("norm_in", "w_in", "q_norm", "w_uq", "kv_norm", "w_ukv", "pool_w", "pool_scale", "w_branch_attn", "w_branch_pool",
           "w_out", "norm_final")
SUBLANES = 8


def _pack_rows(a):
    a = a.reshape(-1, LANES)
    return jnp.pad(a, ((0, -a.shape[0] % SUBLANES), (0, 0)))


def _unpack_rows(packed, like):
    out, row = [], 0
    for a in like:
        rows = a.size // LANES
        out.append(packed[row:row + rows].reshape(a.shape))
        row += rows + (-rows % SUBLANES)
    return out


def kernel(x, norm_in, w_in, q_norm, w_uq, kv_norm, w_ukv, pool_w, pool_scale, w_branch_attn, w_branch_pool, w_out, norm_final, loss_target, m_norm_in, m_w_in, m_q_norm, m_w_uq, m_kv_norm, m_w_ukv, m_pool_w, m_pool_scale, m_w_branch_attn, m_w_branch_pool, m_w_out, m_norm_final, v_norm_in, v_w_in, v_q_norm, v_w_uq, v_kv_norm, v_w_ukv, v_pool_w, v_pool_scale, v_w_branch_attn, v_w_branch_pool, v_w_out, v_norm_final):
    w = dict(norm_in=norm_in, w_in=w_in, q_norm=q_norm, w_uq=w_uq, kv_norm=kv_norm, w_ukv=w_ukv, pool_w=pool_w, pool_scale=pool_scale,
             w_branch_attn=w_branch_attn, w_branch_pool=w_branch_pool, w_out=w_out, norm_final=norm_final)
    m = dict(norm_in=m_norm_in, w_in=m_w_in, q_norm=m_q_norm, w_uq=m_w_uq, kv_norm=m_kv_norm, w_ukv=m_w_ukv, pool_w=m_pool_w,
             pool_scale=m_pool_scale, w_branch_attn=m_w_branch_attn, w_branch_pool=m_w_branch_pool, w_out=m_w_out, norm_final=m_norm_final)
    v = dict(norm_in=v_norm_in, w_in=v_w_in, q_norm=v_q_norm, w_uq=v_w_uq, kv_norm=v_kv_norm, w_ukv=v_w_ukv, pool_w=v_pool_w,
             pool_scale=v_pool_scale, w_branch_attn=v_w_branch_attn, w_branch_pool=v_w_branch_pool, w_out=v_w_out, norm_final=v_norm_final)

    def as2d(name, a):
        if name == "w_in":
            return a.T
        if name in ("w_uq", "w_ukv"):
            return a
        return a.reshape(-1, GROUP) if name == "pool_w" else a

    def unshape(name, a):
        return a.T if name == "w_in" else a.reshape(w[name].shape)

    def like_weight(name, g):
        return g.reshape(-1) if w[name].ndim == 1 else g

    w_in_pad, w_uq_full, w_ukv_full = _all_gather_bf16([as2d(k, w[k]) for k in ("w_in", "w_uq", "w_ukv")])
    loss, grad_x, g2d = _local_step(
        x.reshape(x.shape[1:]), loss_target.reshape(x.shape[1:]), norm_in, w_in_pad, q_norm,
        w_uq_full.reshape(Q_RANK, HW), kv_norm, w_ukv_full.reshape(KV_RANK, HW),
        pool_w, pool_scale, [w_branch_attn, w_branch_pool, w_out], norm_final)

    deltas, new_m, new_v, grads = _adamw([as2d(k, w[k]) for k in WEIGHTS], [like_weight(k, g2d[k]) for k in WEIGHTS],
                                         [as2d(k, m[k]) for k in WEIGHTS], [as2d(k, v[k]) for k in WEIGHTS],
                                         [k in ("w_in", "w_branch_attn", "w_branch_pool", "w_out") for k in WEIGHTS])
    shaped = lambda arrs: [unshape(k, a) for k, a in zip(WEIGHTS, arrs)]
    return (loss, grad_x.reshape(x.shape), *shaped(grads), *shaped(deltas), *shaped(new_m), *shaped(new_v))
```

```python
import functools

import jax
import jax.numpy as jnp
import numpy as np
from jax import lax
from jax.experimental import pallas as pl
from jax.experimental.pallas import tpu as pltpu

F32 = jnp.float32
BF16 = jnp.bfloat16
SDS = jax.ShapeDtypeStruct

D_MODEL = 1024
HEADS = 8
NOPE = 64
ROPE = 32
VDIM = 64
Q_RANK = 384
KV_RANK = 256
MLA_W = HEADS * VDIM
POOL_W = 512
POOL_GROUPS = 4
GROUP = POOL_W // POOL_GROUPS
CHUNK = 64
ROPE_THETA = 10000.0
EPS = 1e-6
SCALE = (NOPE + ROPE) ** -0.5
LOG2E = 1.4426950408889634
LN2 = 0.6931471805599453
QK_SCALE_LOG2 = SCALE * LOG2E
IN_TOTAL = 4256
ADAM_LR, ADAM_B1, ADAM_B2, ADAM_EPS, ADAM_WD, ADAM_STEP = 0.001, 0.9, 0.999, 1e-08, 0.01, 10

N_DEV = 8
LANES = 128
HEAD_PAD = LANES
HW = HEADS * HEAD_PAD

ZQ, ZKV, ZKR, GA, UP, GP, GM, ZTOT = 0, 384, 640, 768, 1280, 1792, 2304, 4352
FRONT_W = GA
ZKR_ORIG = 640

VMEM_LIMIT = 62 * 1024 * 1024


def _cp(sem=None, **kw):
    if sem is not None:
        kw["dimension_semantics"] = sem
    return pltpu.CompilerParams(vmem_limit_bytes=VMEM_LIMIT, **kw)


def _mm(a, b):
    return lax.dot_general(a, b, (((1,), (0,)), ((), ())), preferred_element_type=F32)


def _mm_nt(a, b):
    return lax.dot_general(a, b, (((1,), (1,)), ((), ())), preferred_element_type=F32)


def _mm_tn(a, b):
    return lax.dot_general(a, b, (((0,), (0,)), ((), ())), preferred_element_type=F32)


def _row_spec(tm, w):
    return pl.BlockSpec((tm, w), lambda i: (i, 0))


def _full_spec(a):
    nd = len(a.shape)
    return pl.BlockSpec(a.shape, lambda *_: (0,) * nd)


def _rope(v, c, sa, sb, sign):
    n = v.shape[-1]
    reps = n // LANES
    if reps > 1:
        c, sa, sb = (jnp.tile(t, (1, reps)) for t in (c, sa, sb))
    up = pltpu.roll(v, n - ROPE // 2, 1)
    dn = pltpu.roll(v, ROPE // 2, 1)
    return v * c + sign * (up * sa + dn * sb)


def _rope_tables(T):
    half = ROPE // 2
    inv_freq = np.float32(ROPE_THETA) ** (-np.arange(half, dtype=np.float32) / np.float32(half))
    ang = np.arange(T, dtype=np.float32)[:, None] * inv_freq[None, :].astype(np.float32)
    cos, sin = np.cos(ang.astype(np.float64)).astype(np.float32), np.sin(ang.astype(np.float64)).astype(np.float32)
    z16 = np.zeros((T, half), np.float32)
    z32 = np.zeros((T, LANES - NOPE - ROPE), np.float32)
    c = np.concatenate([np.ones((T, NOPE), np.float32), cos, cos, z32], axis=1)
    sa = np.concatenate([np.zeros((T, NOPE), np.float32), -sin, z16, z32], axis=1)
    sb = np.concatenate([np.zeros((T, NOPE), np.float32), z16, sin, z32], axis=1)
    return jnp.asarray(c), jnp.asarray(sa), jnp.asarray(sb)


def _silu_parts(g):
    sg = jax.nn.sigmoid(g)
    return g * sg, sg + g * sg * (1.0 - sg)


def _in_proj(x2, norm_in, w_in_pad, q_norm, wuq_pad, kv_norm, wukv, rc, rsa, rsb, tm, late_shards):
    T = x2.shape[0]
    steps = T // tm
    n = len(late_shards)

    def body(x_ref, nin_ref, win_ref, qn_ref, wuq_ref, kvn_ref, wukv_ref, c_ref, sa_ref, sb_ref, *rest):
        hn_ref, zgm_ref, zga_ref, zup_ref, zgp_ref, zfr_ref, q_ref, k_ref, v_ref, vt_ref = rest[n:n + 10]
        ag = _AllGather(rest[:n], rest[n + 10:2 * n + 10], rest[2 * n + 10:])
        step = pl.program_id(0)
        pl.when(step == 0)(ag.start)
        pl.when(step == min(3, steps - 1))(ag.forward)
        xf = x_ref[...]
        r = lax.rsqrt(jnp.mean(xf * xf, axis=-1, keepdims=True) + EPS)
        hn = (xf * r * nin_ref[...]).astype(BF16)
        hn_ref[...] = hn
        zfr = _mm_nt(hn, win_ref[ZQ:GA, :])
        zfr_ref[...] = zfr
        zq, zkv, zkr = zfr[:, ZQ:ZKV], zfr[:, ZKV:ZKR], zfr[:, ZKR:GA]
        c, sa, sb = c_ref[...], sa_ref[...], sb_ref[...]
        rq = lax.rsqrt(jnp.mean(zq * zq, axis=-1, keepdims=True) + EPS)
        cq = (zq * rq * qn_ref[...].reshape(1, Q_RANK)).astype(BF16)
        rkv = lax.rsqrt(jnp.mean(zkv * zkv, axis=-1, keepdims=True) + EPS)
        ckv = (zkv * rkv * kvn_ref[...]).astype(BF16)
        zga_ref[...] = _mm_nt(hn, win_ref[GA:UP, :])
        zup_ref[...] = _mm_nt(hn, win_ref[UP:GP, :])
        zgp_ref[...] = _mm_nt(hn, win_ref[GP:GM, :])
        q_raw = _mm(cq, wuq_ref[...])
        kv = _mm(ckv, wukv_ref[...])
        zgm_ref[...] = _mm_nt(hn, win_ref[GM:ZTOT, :])
        q = _rope(q_raw, c, sa, sb, 1.0)
        q_ref[...] = (q * QK_SCALE_LOG2).astype(BF16)
        kr = _rope(zkr, c, sa, sb, 1.0)
        lane = lax.broadcasted_iota(jnp.int32, kv.shape, 1) % LANES
        k_ref[...] = jnp.where(lane < NOPE, kv, jnp.tile(kr, (1, HEADS))).astype(BF16)
        v = jnp.where(lane < NOPE, 1.0, kv).astype(BF16)
        v_ref[...] = v
        vt_ref[...] = v.T
        pl.when(step == steps - 1)(ag.finish)

    ins = (x2, norm_in, w_in_pad, q_norm, wuq_pad, kv_norm, wukv, rc, rsa, rsb)
    in_specs = [_row_spec(tm, D_MODEL), _full_spec(norm_in), _full_spec(w_in_pad), _full_spec(q_norm), _full_spec(wuq_pad),
                _full_spec(kv_norm), _full_spec(wukv), _row_spec(tm, LANES), _row_spec(tm, LANES), _row_spec(tm, LANES)]
    widths = [(D_MODEL, BF16), (ZTOT - GM, F32), (UP - GA, F32), (GP - UP, F32), (GM - GP, F32), (FRONT_W, F32),
              (HW, BF16), (HW, BF16), (HW, BF16)]
    return pl.pallas_call(
        body, name="in_proj", grid=(steps,), in_specs=in_specs + [_full_spec(s) for s in late_shards],
        out_specs=[_row_spec(tm, w) for w, _ in widths] + [pl.BlockSpec((HW, tm), lambda i: (0, i))] + [HBM_SPEC] * n,
        out_shape=[SDS((T, w), dt) for w, dt in widths] + [SDS((HW, T), BF16)]
        + [SDS((N_DEV,) + s.shape, BF16) for s in late_shards],
        scratch_shapes=_ag_scratch([s.shape for s in late_shards]), compiler_params=_cp(("arbitrary",)),
    )(*ins, *late_shards)


def _chunk_mask(n_q, n_k, q_off, transposed):
    shape = (n_k, n_q) if transposed else (n_q, n_k)
    q = (lax.broadcasted_iota(jnp.int32, shape, 1 if transposed else 0) + q_off) // CHUNK
    k = lax.broadcasted_iota(jnp.int32, shape, 0 if transposed else 1) // CHUNK
    return k <= q


def _store_pair_rows(ref, k, pair):
    t = pair.T
    ref[k, 0:1, :] = t[0:1, :]
    ref[k, 1:2, :] = t[VDIM:VDIM + 1, :]


def _attn_fwd(q_att, k_att, vt_att, tq, hps):
    T = q_att.shape[0]
    head_lanes = [slice(h * LANES, (h + 1) * LANES) for h in range(hps)]

    def body(q_ref, k_ref, vt_ref, o_ref, lser_ref):
        i = pl.program_id(1)
        mask = _chunk_mask(tq, tq, 0, True)
        lane = lax.broadcasted_iota(jnp.int32, (tq, LANES), 1)
        qs = [q_ref[:, hs] for hs in head_lanes]

        def step(j, carry, masked):
            off = pl.multiple_of(j * tq, tq)
            sts = [_mm_nt(k_ref[pl.ds(off, tq), hs], qh) for qh, hs in zip(qs, head_lanes)]
            if masked:
                sts = [jnp.where(mask, st, -jnp.inf) for st in sts]
            ms = [jnp.maximum(m, jnp.max(st, axis=0, keepdims=True)) for (m, _), st in zip(carry, sts)]
            pts = [jnp.exp2(st - m_new).astype(BF16) for st, m_new in zip(sts, ms)]
            return tuple((m_new, jnp.exp2(m - m_new) * acc + _mm(vt_ref[hs, pl.ds(off, tq)], pt))
                         for (m, acc), m_new, pt, hs in zip(carry, ms, pts, head_lanes))

        init = ((jnp.full((1, tq), -jnp.inf, F32), jnp.zeros((LANES, tq), F32)),) * hps
        res = step(i, lax.fori_loop(0, i, functools.partial(step, masked=False), init), True)
        for pair in range(hps // 2):
            (ma, acca), (mb, accb) = res[2 * pair], res[2 * pair + 1]
            la, lb = acca[:1], accb[:1]
            oa, ob = (acca / la).T, (accb / lb).T
            o_ref[:, pair * LANES:(pair + 1) * LANES] = jnp.where(lane < VDIM, pltpu.roll(oa, VDIM, 1), ob)
            lser_ref[pair, 0:1, :] = ma + jnp.log2(la)
            lser_ref[pair, 1:2, :] = mb + jnp.log2(lb)

    qspec = pl.BlockSpec((tq, hps * LANES), lambda p, i: (i, p))
    kspec = pl.BlockSpec((T, hps * LANES), lambda p, i: (0, p))
    vspec = pl.BlockSpec((hps * LANES, T), lambda p, i: (p, 0))
    ospec = pl.BlockSpec((tq, hps * VDIM), lambda p, i: (i, p))
    return pl.pallas_call(
        body, name="attn_fwd", grid=(HEADS // hps, T // tq), in_specs=[qspec, kspec, vspec],
        out_specs=[ospec, pl.BlockSpec((hps // 2, 2, tq), lambda p, i: (p, 0, i))],
        out_shape=[SDS((T, MLA_W), F32), SDS((HEADS // 2, 2, T), F32)],
        compiler_params=_cp(("parallel", "parallel")),
    )(q_att, k_att, vt_att)


def _pick(g, vals):
    out = vals[-1]
    for k in range(len(vals) - 2, -1, -1):
        out = jnp.where(g == k, vals[k], out)
    return out


def _window_sum(u, g, forward):
    T = u.shape[0]
    row = lax.broadcasted_iota(jnp.int32, u.shape, 0)

    def sh(s, k):
        if forward:
            return jnp.where(row >= k, pltpu.roll(s, k, 0), 0.0)
        return jnp.where(row < T - k, pltpu.roll(s, T - k, 0), 0.0)

    sums, s = [], u
    for k in (1, 2, 4, 8):
        s = s + sh(s, k)
        sums.append(s)
    return _pick(g, sums)


MAX_WINDOW = 16


def _pool_inv_count(shape, g):
    T, n = shape
    row = lax.broadcasted_iota(jnp.int32, (MAX_WINDOW, n), 0)
    head = 1.0 / jnp.minimum(row + 1, lax.shift_left(jnp.int32(2), g)).astype(F32)
    inv_w = _pick(g, [0.5, 0.25, 0.125, 0.0625])
    return jnp.concatenate([head, jnp.broadcast_to(inv_w, (T - MAX_WINDOW, n)).astype(F32)], axis=0)


def _pool_fwd(zup, zgp, pool_w, pool_scale):
    T = zup.shape[0]

    def body(u_ref, g_ref, w_ref, sc_ref, y_ref):
        g = pl.program_id(0)
        u = u_ref[...]
        d = _window_sum(u, g, True) * _pool_inv_count(u.shape, g) - u
        lin = _mm(d.astype(BF16), w_ref[0].astype(BF16))
        silu, _ = _silu_parts(g_ref[...])
        y_ref[...] = (lin * sc_ref[...] * silu).astype(BF16)

    col = pl.BlockSpec((T, GROUP), lambda g: (0, g))
    return pl.pallas_call(
        body, name="pool_fwd", grid=(POOL_GROUPS,),
        in_specs=[col, col, pl.BlockSpec((1, GROUP, GROUP), lambda g: (g, 0, 0)), pl.BlockSpec((1, GROUP), lambda g: (0, g))],
        out_specs=col, out_shape=SDS((T, POOL_W), BF16), compiler_params=_cp(("parallel",)),
    )(zup, zgp, pool_w, pool_scale)


def _pool_bwd(zup, zgp, dyp, pool_w, pool_scale):
    T = zup.shape[0]

    def body(u_ref, g_ref, dy_ref, w_ref, sc_ref, du_ref, dg_ref, gw_ref, gsc_ref):
        g = pl.program_id(0)
        u = u_ref[...]
        inv = _pool_inv_count(u.shape, g)
        d = (_window_sum(u, g, True) * inv - u).astype(BF16)
        wb = w_ref[0].astype(BF16)
        lin = _mm(d, wb)
        sc = sc_ref[...]
        silu, dsilu = _silu_parts(g_ref[...])
        dy = dy_ref[...]
        dg_ref[...] = (dy * lin * sc * dsilu).astype(BF16)
        dpre = dy * silu
        gsc_ref[...] = jnp.sum(dpre * lin, axis=0, keepdims=True)
        dlin = (dpre * sc).astype(BF16)
        gw_ref[0] = _mm_tn(d, dlin)
        dd = _mm_nt(dlin, wb)
        du_ref[...] = (_window_sum(dd * inv, g, False) - dd).astype(BF16)

    col = pl.BlockSpec((T, GROUP), lambda g: (0, g))
    wspec = pl.BlockSpec((1, GROUP, GROUP), lambda g: (g, 0, 0))
    vspec = pl.BlockSpec((1, GROUP), lambda g: (0, g))
    return pl.pallas_call(
        body, name="pool_bwd", grid=(POOL_GROUPS,), in_specs=[col, col, col, wspec, vspec], out_specs=[col, col, wspec, vspec],
        out_shape=[SDS((T, POOL_W), BF16), SDS((T, POOL_W), BF16), SDS((POOL_GROUPS, GROUP, GROUP), F32), SDS((1, POOL_W), F32)],
        compiler_params=_cp(("parallel",)),
    )(zup, zgp, dyp, pool_w, pool_scale)


def _tail(x2, tgt, o, zga, ypool, zgm, wba, wbp, wout, norm_final, tm):
    T = x2.shape[0]
    steps = T // tm
    cols = D_MODEL // N_DEV

    def body(x_ref, tgt_ref, o_ref, zga_ref, yp_ref, zgm_ref, wba_ref, wbp_ref, wout_ref, nf_ref,
             loss_ref, dh_ref, dgm_ref, doop_ref, dga_ref, dcapr_ref, dyp_ref, swout_ref, swba_ref, swbp_ref, gnf_ref,
             gwout_ref, gwba_ref, gwbp_ref):
        @pl.when(pl.program_id(0) == 0)
        def _():
            for ref in (loss_ref, gwout_ref, gwba_ref, gwbp_ref, gnf_ref):
                ref[...] = jnp.zeros_like(ref)

        o_v = o_ref[...]
        silu, dsilu = _silu_parts(zga_ref[...])
        ya = (o_v * silu).astype(BF16)
        yp = yp_ref[...]
        wba_v = jnp.concatenate([wba_ref[k] for k in range(N_DEV)], axis=1)
        wbp_v = jnp.concatenate([wbp_ref[k] for k in range(N_DEV)], axis=1)
        wout_v = wout_ref[...]
        a = _mm(ya, wba_v)
        p = _mm(yp, wbp_v)
        gate = jax.nn.sigmoid(zgm_ref[...])
        ga, gp = gate[:, :D_MODEL], gate[:, D_MODEL:]
        mg = (ga * a + gp * p).astype(BF16)
        h = x_ref[...] + _mm(mg, wout_v)
        r = lax.rsqrt(jnp.mean(h * h, axis=-1, keepdims=True) + EPS)
        gf = nf_ref[...]
        hr = h * r
        e = hr * gf - tgt_ref[...]
        loss_ref[...] += (0.5 / D_MODEL) * jnp.sum(e * e)
        dy = e * (1.0 / D_MODEL)
        gnf_ref[...] += jnp.sum(dy * hr, axis=0, keepdims=True)
        u = dy * gf
        dh = r * (u - hr * jnp.mean(u * hr, axis=-1, keepdims=True))
        dh_ref[...] = dh
        dhb = dh.astype(BF16)
        dmg = _mm_nt(dhb, wout_v)
        dab = (dmg * ga).astype(BF16)
        dpb = (dmg * gp).astype(BF16)
        dya = _mm_nt(dab, wba_v)
        dyp_ref[...] = _mm_nt(dpb, wbp_v)
        gwout_ref[...] += _mm_tn(mg, dhb)
        gwba_ref[...] += _mm_tn(ya, dab)
        gwbp_ref[...] += _mm_tn(yp, dpb)
        dgm_ref[:, :D_MODEL] = (dmg * a * ga * (1.0 - ga)).astype(BF16)
        dgm_ref[:, D_MODEL:] = (dmg * p * gp * (1.0 - gp)).astype(BF16)
        do = dya * silu
        dga_ref[...] = (dya * o_v * dsilu).astype(BF16)
        prod = do * o_v
        lo = lax.broadcasted_iota(jnp.int32, (tm, LANES), 1) < VDIM
        for pair in range(HEADS // 2):
            ls = slice(pair * LANES, (pair + 1) * LANES)
            do_p, prod_p = do[:, ls], prod[:, ls]
            dcap_a = jnp.sum(jnp.where(lo, prod_p, 0.0), axis=-1, keepdims=True)
            dcap_b = jnp.sum(jnp.where(lo, 0.0, prod_p), axis=-1, keepdims=True)
            _store_pair_rows(dcapr_ref, pair, jnp.where(lo, dcap_a, dcap_b))
            doop_ref[:, 2 * pair * LANES:(2 * pair + 1) * LANES] = jnp.where(lo, 0.0, pltpu.roll(do_p, VDIM, 1)).astype(BF16)
            doop_ref[:, (2 * pair + 1) * LANES:(2 * pair + 2) * LANES] = jnp.where(lo, 0.0, do_p).astype(BF16)

        @pl.when(pl.program_id(0) == steps - 1)
        def _():
            for k in range(N_DEV):
                swout_ref[k] = gwout_ref[k * cols:(k + 1) * cols, :].astype(BF16)
                swba_ref[k] = gwba_ref[:, k * cols:(k + 1) * cols].astype(BF16)
                swbp_ref[k] = gwbp_ref[:, k * cols:(k + 1) * cols].astype(BF16)

    ins = (x2, tgt, o, zga, ypool, zgm, wba, wbp, wout, norm_final)
    in_specs = [_row_spec(tm, D_MODEL), _row_spec(tm, D_MODEL), _row_spec(tm, MLA_W), _row_spec(tm, MLA_W), _row_spec(tm, POOL_W),
                _row_spec(tm, 2 * D_MODEL), _full_spec(wba), _full_spec(wbp), _full_spec(wout), _full_spec(norm_final)]
    outs = [SDS((8, LANES), F32), SDS((T, D_MODEL), F32), SDS((T, 2 * D_MODEL), BF16), SDS((T, HW), BF16), SDS((T, MLA_W), BF16),
            SDS((HEADS // 2, 2, T), F32), SDS((T, POOL_W), F32),
            SDS((N_DEV, cols, D_MODEL), BF16), SDS((N_DEV, MLA_W, cols), BF16), SDS((N_DEV, POOL_W, cols), BF16), SDS((1, D_MODEL), F32)]
    out_specs = [_full_spec(outs[0]), _row_spec(tm, D_MODEL), _row_spec(tm, 2 * D_MODEL), _row_spec(tm, HW), _row_spec(tm, MLA_W),
                 pl.BlockSpec((HEADS // 2, 2, tm), lambda i: (0, 0, i)), _row_spec(tm, POOL_W),
                 _full_spec(outs[7]), _full_spec(outs[8]), _full_spec(outs[9]), _full_spec(outs[10])]
    return pl.pallas_call(
        body, name="tail", grid=(steps,), in_specs=in_specs, out_specs=out_specs, out_shape=outs,
        scratch_shapes=[pltpu.VMEM((D_MODEL, D_MODEL), F32), pltpu.VMEM((MLA_W, D_MODEL), F32), pltpu.VMEM((POOL_W, D_MODEL), F32)],
        compiler_params=_cp(("arbitrary",)),
    )(*ins)


def _attn_bwd(q_att, k_att, v_att, doop, lse_rows, dcap_rows, tq, hps, slabs, packed):
    T = q_att.shape[0]
    nq = T // tq
    n = len(slabs)
    groups = HEADS // hps
    head_lanes = [slice(h * LANES, (h + 1) * LANES) for h in range(hps)]

    def body(q_ref, k_ref, v_ref, doop_ref, lse_ref, dcap_ref, *rest):
        slab_refs, packed_ref = rest[:n], rest[n]
        dq_ref, dkv_ref, dkr_ref = rest[n + 1:n + 4]
        sum_refs, ptot_ref = rest[n + 4:2 * n + 4], rest[2 * n + 4]
        dq_acc = rest[2 * n + 5]
        rs = _ReduceScatter(slab_refs, packed_ref, sum_refs, ptot_ref, rest[2 * n + 6:])
        group, j = pl.program_id(0), pl.program_id(1)
        tick, last = group * nq + j, groups * nq - 1
        pl.when(tick == 0)(rs.start1)
        pl.when(tick == min(1, last))(rs.finish1_start2)
        pl.when(tick == min(3, last))(rs.relay2)
        mask = _chunk_mask(tq, tq, 0, True)
        lane = lax.broadcasted_iota(jnp.int32, (tq, LANES), 1)
        ks = [k_ref[:, hs] for hs in head_lanes]
        vs = [v_ref[:, hs] for hs in head_lanes]
        kts = [kh.T for kh in ks]

        @pl.when(j == 0)
        def _():
            dq_acc[...] = jnp.zeros_like(dq_acc)

        def step(i, carry, masked):
            rows = pl.ds(pl.multiple_of(i * tq, tq), tq)
            heads = range(hps)
            stat = lambda h: (h // 2, slice(h % 2, h % 2 + 1), rows)
            qhs = [q_ref[rows, hs] for hs in head_lanes]
            doops = [doop_ref[rows, hs] for hs in head_lanes]
            sts = [_mm_nt(ks[h], qhs[h]) for h in heads]
            dpts = [_mm_nt(vs[h], doops[h]) for h in heads]
            pts = [jnp.exp2(sts[h] - lse_ref[stat(h)]) for h in heads]
            if masked:
                pts = [jnp.where(mask, pt, 0.0) for pt in pts]
            dsts = [(pts[h] * (dpts[h] - dcap_ref[stat(h)])).astype(BF16) for h in heads]
            dvs = [_mm(pts[h].astype(BF16), doops[h]) for h in heads]
            dks = [_mm(dsts[h], qhs[h]) for h in heads]
            for h, hs in enumerate(head_lanes):
                dq_acc[hs, rows] += _mm(kts[h], dsts[h])
            return tuple((dk + dks[h], dv + dvs[h]) for h, (dk, dv) in enumerate(carry))

        zero = jnp.zeros((tq, LANES), F32)
        carry = step(j, ((zero, zero),) * hps, True)
        res = lax.fori_loop(j + 1, nq, functools.partial(step, masked=False), carry)
        dkr = None
        for (dk, dv), hs in zip(res, head_lanes):
            dk = dk * LN2
            dkv_ref[:, hs] = jnp.where(lane < NOPE, dk, dv).astype(BF16)
            dkr = dk if dkr is None else dkr + dk
        dkr_ref[0] = jnp.where((lane >= NOPE) & (lane < NOPE + ROPE), dkr, 0.0)

        @pl.when(j == nq - 1)
        def _():
            dq_ref[...] = (dq_acc[...] * SCALE).T.astype(BF16)

        pl.when(tick == last)(rs.finish2)

    kspec = pl.BlockSpec((tq, hps * LANES), lambda p, j: (j, p))
    qspec = pl.BlockSpec((T, hps * LANES), lambda p, j: (0, p))
    rspec = pl.BlockSpec((hps // 2, 2, T), lambda p, j: (p, 0, 0))
    sums = [SDS(s.shape[1:], F32) for s in slabs] + [SDS(packed.shape, F32)]
    return pl.pallas_call(
        body, name="attn_bwd", grid=(groups, nq),
        in_specs=[qspec, kspec, kspec, qspec, rspec, rspec] + [HBM_SPEC] * n + [_full_spec(packed)],
        out_specs=[qspec, kspec, pl.BlockSpec((1, tq, LANES), lambda p, j: (p, j, 0))] + [_full_spec(s) for s in sums],
        out_shape=[SDS((T, HW), BF16), SDS((T, HW), BF16), SDS((groups, T, LANES), F32)] + sums,
        scratch_shapes=[pltpu.VMEM((hps * LANES, T), F32)] + _rs_scratch([s.shape for s in sums[:-1]], packed.shape),
        compiler_params=_cp(("arbitrary", "arbitrary")),
    )(q_att, k_att, v_att, doop, lse_rows, dcap_rows, *slabs, packed)


def _rms_bwd(z, gain, dout):
    r = lax.rsqrt(jnp.mean(z * z, axis=-1, keepdims=True) + EPS)
    zr = z * r
    u = dout * gain
    return r * (u - zr * jnp.mean(u * zr, axis=-1, keepdims=True)), jnp.sum(dout * zr, axis=0, keepdims=True)


def _mla_bwd(dq_att, dkv_nat, dkr4, zfr, q_norm, wuq_pad, kv_norm, wukv, rc, rsa, rsb, tm):
    T = dq_att.shape[0]

    def body(dq_ref, dkv_ref, dkr_ref, zfr_ref, qn_ref, wuq_ref, kvn_ref, wukv_ref, c_ref, sa_ref, sb_ref,
             dfr_ref, gwuq_out, gwukv_out, gqn_ref, gkvn_ref, gwuq_ref, gwukv_ref):
        @pl.when(pl.program_id(0) == 0)
        def _():
            for ref in (gwuq_ref, gwukv_ref, gqn_ref, gkvn_ref):
                ref[...] = jnp.zeros_like(ref)

        c, sa, sb = c_ref[...], sa_ref[...], sb_ref[...]
        zq, zkv = zfr_ref[:, :Q_RANK], zfr_ref[:, Q_RANK:Q_RANK + KV_RANK]
        qn, kvn = qn_ref[...].reshape(1, Q_RANK), kvn_ref[...]
        dkv = dkv_ref[...]
        dckv = _mm_nt(dkv, wukv_ref[...])
        ckv = (zkv * lax.rsqrt(jnp.mean(zkv * zkv, axis=-1, keepdims=True) + EPS) * kvn).astype(BF16)
        gwukv_ref[...] += _mm_tn(ckv, dkv)
        cq = (zq * lax.rsqrt(jnp.mean(zq * zq, axis=-1, keepdims=True) + EPS) * qn).astype(BF16)
        dq = _rope(dq_ref[...].astype(F32), c, sa, sb, -1.0).astype(BF16)
        dzkv, gkvn = _rms_bwd(zkv, kvn, dckv)
        gkvn_ref[...] += gkvn
        gwuq_ref[...] += _mm_tn(cq, dq)
        dzq, gqn = _rms_bwd(zq, qn, _mm_nt(dq, wuq_ref[...]))
        gqn_ref[...] += gqn
        dkr = functools.reduce(lambda a, b: a + b, [dkr_ref[g] for g in range(dkr4.shape[0])])
        dfr_ref[:, :Q_RANK] = dzq.astype(BF16)
        dfr_ref[:, Q_RANK:Q_RANK + KV_RANK] = dzkv.astype(BF16)
        dfr_ref[:, Q_RANK + KV_RANK:] = _rope(dkr, c, sa, sb, -1.0).astype(BF16)

        @pl.when(pl.program_id(0) == pl.num_programs(0) - 1)
        def _():
            gwuq_out[...] = gwuq_ref[...].astype(BF16)
            gwukv_out[...] = gwukv_ref[...].astype(BF16)

    ins = (dq_att, dkv_nat, dkr4, zfr, q_norm, wuq_pad, kv_norm, wukv, rc, rsa, rsb)
    in_specs = [_row_spec(tm, HW), _row_spec(tm, HW), pl.BlockSpec((dkr4.shape[0], tm, LANES), lambda i: (0, i, 0)), _row_spec(tm, FRONT_W),
                _full_spec(q_norm), _full_spec(wuq_pad), _full_spec(kv_norm), _full_spec(wukv),
                _row_spec(tm, LANES), _row_spec(tm, LANES), _row_spec(tm, LANES)]
    outs = [SDS((T, FRONT_W), BF16), SDS((Q_RANK, HW), BF16), SDS((KV_RANK, HW), BF16), SDS((1, Q_RANK), F32), SDS((1, KV_RANK), F32)]
    out_specs = [_row_spec(tm, FRONT_W)] + [_full_spec(s) for s in outs[1:]]
    return pl.pallas_call(
        body, name="mla_bwd", grid=(T // tm,), in_specs=in_specs, out_specs=out_specs, out_shape=outs,
        scratch_shapes=[pltpu.VMEM((Q_RANK, HW), F32), pltpu.VMEM((KV_RANK, HW), F32)],
        compiler_params=_cp(("arbitrary",)),
    )(*ins)


_DZ_COLS = ((GM, ZTOT), (GA, UP), (UP, GP), (GP, GM), (ZQ, GA))


def _in_proj_bwd_x(dzs, x2, dh, norm_in, w_in_pad, tm, slabs):
    T = x2.shape[0]
    steps = T // tm
    n = len(slabs)

    def body(d0, d1, d2, d3, d4, x_ref, dh_ref, nin_ref, win_ref, *rest):
        slab_refs, (gx_ref, gnin_ref), sum_refs = rest[:n], rest[n:n + 2], rest[n + 2:2 * n + 2]
        rs = _ReduceScatter(slab_refs, None, sum_refs, None, rest[2 * n + 2:])
        step = pl.program_id(0)

        @pl.when(step == 0)
        def _():
            gnin_ref[...] = jnp.zeros_like(gnin_ref)
            rs.start1()

        pl.when(step == min(2, steps - 1))(rs.finish1_start2)
        pl.when(step == min(steps * 11 // 16, steps - 1))(rs.relay2)
        dhn = None
        for ref, (lo, hi) in zip((d0, d1, d2, d3, d4), _DZ_COLS):
            t = _mm(ref[...], win_ref[lo:hi, :])
            dhn = t if dhn is None else dhn + t
        dx, gnin = _rms_bwd(x_ref[...], nin_ref[...], dhn)
        gnin_ref[...] += gnin
        gx_ref[...] = dx + dh_ref[...]
        pl.when(step == steps - 1)(rs.finish2)

    in_specs = [_row_spec(tm, hi - lo) for lo, hi in _DZ_COLS] + [_row_spec(tm, D_MODEL), _row_spec(tm, D_MODEL),
                                                                  _full_spec(norm_in), _full_spec(w_in_pad)] + [HBM_SPEC] * n
    sums = [SDS(s.shape[1:], F32) for s in slabs]
    outs = [SDS((T, D_MODEL), F32), SDS((1, D_MODEL), F32)] + sums
    return pl.pallas_call(
        body, name="in_proj_bwd_x", grid=(steps,), in_specs=in_specs,
        out_specs=[_row_spec(tm, D_MODEL), _full_spec(outs[1])] + [_full_spec(s) for s in sums],
        out_shape=outs, scratch_shapes=_rs_scratch([s.shape for s in sums], None), compiler_params=_cp(("arbitrary",)),
    )(*dzs, x2, dh, norm_in, w_in_pad, *slabs)


SLAB_ROWS = IN_TOTAL // N_DEV


def _slab_segments(k):
    cuts = [(0, ZKR_ORIG, 0), (ZKR_ORIG, ZKR_ORIG + ROPE, NOPE), (ZKR_ORIG + ROPE, IN_TOTAL, LANES - ROPE)]
    lo, hi = k * SLAB_ROWS, (k + 1) * SLAB_ROWS
    return [(max(lo, a) - lo, max(lo, a) + shift, min(hi, b) - max(lo, a)) for a, b, shift in cuts if min(hi, b) > max(lo, a)]


def _in_proj_bwd_w(dzs, hn, tm):
    T = hn.shape[0]
    steps = T // tm

    def body(d0, d1, d2, d3, d4, hn_ref, slab_ref, acc_ref):
        @pl.when(pl.program_id(0) == 0)
        def _():
            acc_ref[...] = jnp.zeros_like(acc_ref)

        hn_v = hn_ref[...]
        for ref, (lo, hi) in zip((d0, d1, d2, d3, d4), _DZ_COLS):
            acc_ref[lo:hi, :] += _mm_tn(ref[...], hn_v)

        @pl.when(pl.program_id(0) == steps - 1)
        def _():
            for k in range(N_DEV):
                for at, src, rows in _slab_segments(k):
                    slab_ref[k, at:at + rows, :] = acc_ref[src:src + rows, :].astype(BF16)

    in_specs = [_row_spec(tm, hi - lo) for lo, hi in _DZ_COLS] + [_row_spec(tm, D_MODEL)]
    out = SDS((N_DEV, SLAB_ROWS, D_MODEL), BF16)
    return pl.pallas_call(
        body, name="in_proj_bwd_w", grid=(steps,), in_specs=in_specs, out_specs=_full_spec(out), out_shape=out,
        scratch_shapes=[pltpu.VMEM((ZTOT, D_MODEL), F32)], compiler_params=_cp(("arbitrary",)),
    )(*dzs, hn)


def _local_step(x2, tgt, norm_in, w_in_pad, q_norm, wuq_pad, kv_norm, wukv, pool_w, pool_scale, late_shards, norm_final):
    T = x2.shape[0]
    tm, tm_small, tq = min(512, T), min(256, T), min(512, T)
    heads_fwd, heads_bwd = 4, 4
    row = lambda v: v.reshape(1, -1)
    rc, rsa, rsb = _rope_tables(T)

    hn, zgm, zga, zup, zgp, zfr, q_att, k_att, v_att, vt_att, w_ba, w_bp, w_out = _in_proj(
        x2, row(norm_in), w_in_pad, q_norm, wuq_pad, row(kv_norm), wukv, rc, rsa, rsb, tm, late_shards)
    w_out = w_out.reshape(D_MODEL, D_MODEL)
    o, lse_rows = _attn_fwd(q_att, k_att, vt_att, tq, heads_fwd)
    ypool = _pool_fwd(zup, zgp, pool_w, row(pool_scale))
    loss8, dh, dgm, doop, dga, dcap_rows, dyp, *slabs, g_nf = _tail(
        x2, tgt, o, zga, ypool, zgm, w_ba, w_bp, w_out, row(norm_final), tm_small)
    dup, dgp, g_pool_w, g_pool_scale = _pool_bwd(zup, zgp, dyp, pool_w, row(pool_scale))

    early = [g_pool_w, g_pool_scale, g_nf, loss8[0]]
    packed = jnp.concatenate([_pack_rows(a) for a in early], axis=0)
    dq_att, dkv_nat, dkr4, s_wout, s_wba, s_wbp, tot_early = _attn_bwd(
        q_att, k_att, v_att, doop, lse_rows, dcap_rows, tq, heads_bwd, slabs, packed)
    at_pool_w, at_pool_scale, at_nf, at_loss = _packed_at(early)

    dfr, g_wuq_pad, g_wukv, g_qn, g_kvn = _mla_bwd(
        dq_att, dkv_nat, dkr4, zfr, q_norm, wuq_pad, row(kv_norm), wukv, rc, rsa, rsb, tm)
    dzs = (dgm, dga, dup, dgp, dfr)
    slabs = [_in_proj_bwd_w(dzs, hn, tm), g_wuq_pad.reshape(N_DEV, Q_RANK // N_DEV, HW), g_wukv.reshape(N_DEV, KV_RANK // N_DEV, HW)]
    grad_x, g_nin, s_win, s_wuq, s_wukv = _in_proj_bwd_x(dzs, x2, dh, row(norm_in), w_in_pad, tm_small, slabs)
    late = [g_nin, g_qn, g_kvn]
    (tot_late,) = _reduce_scatter([], jnp.concatenate([_pack_rows(a) for a in late], axis=0))
    at_nin, at_qn, at_kvn = _packed_at(late)

    grads = dict(norm_in=(tot_late, at_nin), w_in=s_win, q_norm=(tot_late, at_qn), w_uq=s_wuq, kv_norm=(tot_late, at_kvn), w_ukv=s_wukv,
                 pool_w=(tot_early, at_pool_w), pool_scale=(tot_early, at_pool_scale), w_branch_attn=s_wba, w_branch_pool=s_wbp,
                 w_out=s_wout, norm_final=(tot_early, at_nf))
    return tot_early[at_loss, 0], grad_x, grads


MESH_ID = pl.DeviceIdType.MESH
VMEM_SPEC = pl.BlockSpec(memory_space=pltpu.VMEM)
HBM_SPEC = pl.BlockSpec(memory_space=pl.ANY)


def _mesh_pos():
    return lax.axis_index("x"), lax.axis_index("y"), lax.axis_index("c")


BF16_TILE_ROWS = 16


def _half_rows(rows):
    cut = -(-(rows // 2) // BF16_TILE_ROWS) * BF16_TILE_ROWS
    return pl.ds(0, cut), pl.ds(cut, rows - cut)


def _slot(px, py, pc):
    return 4 * px + 2 * py + pc


def _staged_shape(shape):
    return (shape[0], shape[1] * LANES) if len(shape) == 3 else tuple(shape)


def _all_gather_bf16(shards):
    n = len(shards)
    staged = [_staged_shape(s.shape) for s in shards]

    def body(*refs):
        ins, outs = refs[:n], refs[n:2 * n]
        land0, scratch = refs[2 * n], refs[2 * n + 1:]
        wpad_ref = outs[0]
        ag = _AllGather(ins, (land0,) + tuple(outs[1:]), scratch)
        ag.start()
        ag.forward()
        ag.finish()
        wpad_ref[ZKR:GA, :] = jnp.zeros((GA - ZKR, D_MODEL), BF16)
        for k in range(N_DEV):
            for at, dst, rows in _slab_segments(k):
                wpad_ref[dst:dst + rows, :] = land0[k, at:at + rows, :]

    return pl.pallas_call(
        body, name="all_gather_weights",
        in_specs=[VMEM_SPEC] * n, out_specs=[VMEM_SPEC] + [HBM_SPEC] * (n - 1),
        out_shape=[SDS((ZTOT, D_MODEL), BF16)] + [SDS((N_DEV,) + s, BF16) for s in staged[1:]],
        scratch_shapes=[pltpu.VMEM((N_DEV,) + staged[0], BF16)] + _ag_scratch(staged),
        compiler_params=_cp(),
    )(*shards)


def _ag_scratch(shapes):
    n = len(shapes)
    dma = pltpu.SemaphoreType.DMA
    return [pltpu.VMEM(tuple(s), BF16) for s in shapes] + [dma((_AllGather.COPIES * n,)), dma((_AllGather.COPIES * n,)), dma((n,))]


class _AllGather:
    COPIES = 8

    def __init__(self, in_refs, dest_refs, scratch):
        n = self.n = len(in_refs)
        self.ins, self.dests, self.stage = in_refs, dest_refs, scratch[:n]
        self.send_sems, self.recv_sems, self.local_sems = scratch[n:]
        x, y, c = _mesh_pos()
        self.c, self.me, self.sibling = c, (x, y, c), (x, y, 1 - c)
        self.xn, self.yn, self.diag = (1 - x, y), (x, 1 - y), (1 - x, 1 - y)

    def _halves(self, a):
        return _half_rows(self.stage[a].shape[0])

    def _copy(self, a, k, block, to, from_stage=False, rows=None):
        dst = self.dests[a].at[_slot(*block)]
        src = self.stage[a] if from_stage else dst
        if rows is not None:
            src, dst = src.at[rows], dst.at[rows]
        return pltpu.make_async_remote_copy(
            src_ref=src, dst_ref=dst, send_sem=self.send_sems.at[self.COPIES * a + k],
            recv_sem=self.recv_sems.at[self.COPIES * a + k], device_id=to, device_id_type=MESH_ID)

    def _mine(self):
        return [pltpu.make_async_copy(self.stage[a], self.dests[a].at[_slot(*self.me)], self.local_sems.at[a]) for a in range(self.n)]

    def _first(self, a):
        return [self._copy(a, 0, self.me, self.sibling, True), self._copy(a, 1, self.me, (*self.xn, self.c), True),
                self._copy(a, 2, self.me, (*self.yn, self.c), True)]

    def _relays(self, a):
        lo, hi = self._halves(a)
        return [self._copy(a, 3, (*self.xn, self.c), (*self.yn, self.c), rows=lo),
                self._copy(a, 4, (*self.yn, self.c), (*self.xn, self.c), rows=hi)]

    def _passes(self, a):
        return [self._copy(a, 5 + j, (*chip, self.c), self.sibling) for j, chip in enumerate((self.xn, self.yn, self.diag))]

    def start(self):
        for a in range(self.n):
            src, dst = self.ins[a], self.stage[a]
            if src.shape == dst.shape:
                dst[...] = src[...].astype(BF16)
            else:
                if src.shape[2] < LANES:
                    dst[...] = jnp.zeros(dst.shape, BF16)
                for h in range(src.shape[1]):
                    dst[:, h * LANES:h * LANES + src.shape[2]] = src[:, h, :].astype(BF16)
        for cp in self._mine():
            cp.start()
        for a in range(self.n):
            for cp in self._first(a):
                cp.start()

    def forward(self):
        for a in range(self.n):
            relays, passes = self._relays(a), self._passes(a)
            for j, chip in enumerate((self.xn, self.yn)):
                self._copy(a, 1 + j, (*chip, self.c), self.me).wait_recv()
                relays[j].start()
                passes[j].start()

    def finish(self):
        for a in range(self.n):
            lo, hi = self._halves(a)
            self._copy(a, 3, (*self.diag, self.c), self.me, rows=lo).wait_recv()
            self._copy(a, 4, (*self.diag, self.c), self.me, rows=hi).wait_recv()
            self._passes(a)[2].start()
        for a in range(self.n):
            self._copy(a, 0, self.sibling, self.me).wait_recv()
            for j, chip in enumerate((self.xn, self.yn, self.diag)):
                self._copy(a, 5 + j, (*chip, 1 - self.c), self.me).wait_recv()
            for cp in self._first(a) + self._relays(a) + self._passes(a):
                cp.wait_send()
        for cp in self._mine():
            cp.wait()


N_CHIPS = 4


def _reduce_scatter(slabs, packed):
    def body(*refs):
        n = len(slabs)
        rs = _ReduceScatter(refs[:n], refs[n], refs[n + 1:2 * n + 1], refs[2 * n + 1], refs[2 * n + 2:])
        rs.start1()
        rs.finish1_start2()
        rs.relay2()
        rs.finish2()

    shapes = [s.shape[1:] for s in slabs]
    return pl.pallas_call(
        body, name="reduce_scatter_grads",
        in_specs=[HBM_SPEC] * len(slabs) + [VMEM_SPEC], out_specs=[VMEM_SPEC] * (len(slabs) + 1),
        out_shape=[SDS(s, F32) for s in shapes] + [SDS(packed.shape, F32)],
        scratch_shapes=_rs_scratch(shapes, packed.shape), compiler_params=_cp(),
    )(*slabs, packed)


def _rs_scratch(shapes, packed_shape):
    n = len(shapes)
    n1, n2 = N_CHIPS * n + 1, _ReduceScatter.L2_COPIES * n + N_CHIPS - 1
    dma = pltpu.SemaphoreType.DMA
    packed = [] if packed_shape is None else [pltpu.VMEM(packed_shape, F32), pltpu.VMEM((N_CHIPS,) + tuple(packed_shape), F32)]
    return ([pltpu.VMEM((N_CHIPS,) + tuple(s), BF16) for s in shapes] * 2 + [pltpu.VMEM((N_CHIPS - 1,) + tuple(s), BF16) for s in shapes] * 2
            + packed + [dma((max(N_CHIPS * n, 1),)), dma((n1,)), dma((n1,)), dma((n2,)), dma((n2,))])


class _ReduceScatter:
    L2_COPIES = 6

    def __init__(self, slab_refs, packed_ref, out_refs, ptot_ref, scratch):
        n = self.n = len(slab_refs)
        self.slabs, self.packed, self.outs, self.ptot = slab_refs, packed_ref, out_refs, ptot_ref
        self.own1, self.land1, self.send2, self.land2 = (scratch[k * n:(k + 1) * n] for k in range(4))
        rest = scratch[4 * n:]
        if packed_ref is not None:
            self.pland1, self.pland2 = rest[:2]
            rest = rest[2:]
        self.loc_sems, self.send1_sems, self.recv1_sems, self.send2_sems, self.recv2_sems = rest
        self.x, self.y, self.c = _mesh_pos()

    def _chip(self, r):
        return (1 - self.x if r & 2 else self.x, 1 - self.y if r & 1 else self.y)

    @staticmethod
    def _remote(src, dst, send_sem, recv_sem, to):
        return pltpu.make_async_remote_copy(src_ref=src, dst_ref=dst, send_sem=send_sem, recv_sem=recv_sem, device_id=to,
                                            device_id_type=MESH_ID)

    def _copies1(self):
        c, sibling = self.c, (self.x, self.y, 1 - self.c)
        cps = []
        for a in range(self.n):
            for r in range(N_CHIPS):
                k = N_CHIPS * a + r
                cps.append(pltpu.make_async_copy(self.slabs[a].at[_slot(*self._chip(r), c)], self.own1[a].at[r], self.loc_sems.at[k]))
                cps.append(self._remote(self.slabs[a].at[_slot(*self._chip(r), 1 - c)], self.land1[a].at[r],
                                        self.send1_sems.at[k], self.recv1_sems.at[k], sibling))
        if self.packed is not None:
            k = N_CHIPS * self.n
            cps.append(self._remote(self.packed, self.pland1, self.send1_sems.at[k], self.recv1_sems.at[k], sibling))
        return cps

    def _halves(self, a):
        return _half_rows(self.send2[a].shape[1])

    def _copy2(self, a, k):
        lo, hi = self._halves(a)
        xn, yn = (*self._chip(2), self.c), (*self._chip(1), self.c)
        slot, rows, to = [(1, lo, xn), (2, lo, xn), (0, hi, yn), (2, hi, yn), (0, lo, yn), (1, hi, xn)][k]
        return self._remote(self.send2[a].at[slot].at[rows], self.land2[a].at[slot].at[rows],
                            self.send2_sems.at[self.L2_COPIES * a + k], self.recv2_sems.at[self.L2_COPIES * a + k], to)

    def _copies2_packed(self):
        base = self.L2_COPIES * self.n - 1
        return [self._remote(self.pland2.at[0], self.pland2.at[r], self.send2_sems.at[base + r], self.recv2_sems.at[base + r],
                             (*self._chip(r), self.c)) for r in range(1, N_CHIPS)]

    def start1(self):
        for cp in self._copies1():
            cp.start()

    def finish1_start2(self):
        for cp in self._copies1():
            cp.wait()
        for a in range(self.n):
            self.outs[a][...] = self.own1[a][0].astype(F32) + self.land1[a][0].astype(F32)
            for r in range(1, N_CHIPS):
                self.send2[a][r - 1] = (self.own1[a][r].astype(F32) + self.land1[a][r].astype(F32)).astype(BF16)
            for k in range(4):
                self._copy2(a, k).start()
        if self.packed is not None:
            self.pland2[0] = self.packed[...] + self.pland1[...]
            for cp in self._copies2_packed():
                cp.start()

    def relay2(self):
        for a in range(self.n):
            lo, hi = self._halves(a)
            s2, l2 = self.send2[a], self.land2[a]
            self._copy2(a, 1).wait_recv()
            s2[0, lo] = (s2[0, lo].astype(F32) + l2[2, lo].astype(F32)).astype(BF16)
            self._copy2(a, 4).start()
            self._copy2(a, 3).wait_recv()
            s2[1, hi] = (s2[1, hi].astype(F32) + l2[2, hi].astype(F32)).astype(BF16)
            self._copy2(a, 5).start()

    def finish2(self):
        for a in range(self.n):
            for k in (0, 2, 4, 5):
                self._copy2(a, k).wait_recv()
            for k in range(self.L2_COPIES):
                self._copy2(a, k).wait_send()
            l2 = self.land2[a]
            self.outs[a][...] = self.outs[a][...] + (l2[0].astype(F32) + l2[1].astype(F32))
        if self.packed is not None:
            for cp in self._copies2_packed():
                cp.wait()
            p2 = self.pland2
            self.ptot[...] = (p2[0] + p2[1]) + (p2[2] + p2[3])


def _adamw(ws, gs, ms, vs, rewrite):
    n = len(ws)
    first_row = [g[1] if isinstance(g, tuple) else None for g in gs]
    gs = [g[0] if isinstance(g, tuple) else g for g in gs]
    regrouped = [k for k in range(n) if first_row[k] is not None or gs[k].shape != ws[k].shape or rewrite[k]]
    assert all(w.shape[-1] % LANES == 0 for w, row in zip(ws, first_row) if row is not None)

    def body(*refs):
        for k in range(n):
            w_ref, g_ref, m_ref, v_ref, d_ref, nm_ref, nv_ref = (refs[j * n + k] for j in range(7))
            row = first_row[k]
            windows = [(..., ...)]
            if row is not None and len(w_ref.shape) == 1:
                windows = [(pl.ds(j * LANES, LANES), row + j) for j in range(w_ref.shape[0] // LANES)]
            elif row is not None:
                windows = [(..., pl.ds(row, w_ref.shape[0]))]
            elif g_ref.shape != w_ref.shape:
                windows = [((slice(None), h), (slice(None), slice(h * LANES, h * LANES + w_ref.shape[2]))) for h in range(w_ref.shape[1])]
            if k in regrouped:
                g_out_ref = refs[7 * n + regrouped.index(k)]
            for at, g_at in windows:
                w, g, m, v = w_ref[at], g_ref[g_at], m_ref[at], v_ref[at]
                m = ADAM_B1 * m + (1.0 - ADAM_B1) * g
                v = ADAM_B2 * v + (1.0 - ADAM_B2) * jnp.square(g)
                m_hat = m / (1.0 - ADAM_B1 ** ADAM_STEP)
                v_hat = v / (1.0 - ADAM_B2 ** ADAM_STEP)
                d_ref[at] = -ADAM_LR * (m_hat / (jnp.sqrt(v_hat) + ADAM_EPS) + ADAM_WD * w)
                nm_ref[at] = m
                nv_ref[at] = v
                if k in regrouped:
                    g_out_ref[at] = g

    shapes = [SDS(w.shape, F32) for w in ws]
    outs = pl.pallas_call(
        body, name="adamw", in_specs=[VMEM_SPEC] * (4 * n), out_specs=[VMEM_SPEC] * (3 * n + len(regrouped)),
        out_shape=shapes * 3 + [shapes[k] for k in regrouped], compiler_params=_cp(),
    )(*ws, *gs, *ms, *vs)
    grads = list(gs)
    for k, g in zip(regrouped, outs[3 * n:]):
        grads[k] = g
    return outs[:n], outs[n:2 * n], outs[2 * n:3 * n], grads


WEIGHTS = ("norm_in", "w_in", "q_norm", "w_uq", "kv_norm", "w_ukv", "pool_w", "pool_scale", "w_branch_attn", "w_branch_pool",
           "w_out", "norm_final")
SUBLANES = 8


def _pack_rows(a):
    a = a.reshape(-1, LANES)
    return jnp.pad(a, ((0, -a.shape[0] % SUBLANES), (0, 0)))


def _packed_at(like):
    out, row = [], 0
    for a in like:
        out.append(row)
        rows = a.size // LANES
        row += rows + (-rows % SUBLANES)
    return out


def kernel(x, norm_in, w_in, q_norm, w_uq, kv_norm, w_ukv, pool_w, pool_scale, w_branch_attn, w_branch_pool, w_out, norm_final, loss_target, m_norm_in, m_w_in, m_q_norm, m_w_uq, m_kv_norm, m_w_ukv, m_pool_w, m_pool_scale, m_w_branch_attn, m_w_branch_pool, m_w_out, m_norm_final, v_norm_in, v_w_in, v_q_norm, v_w_uq, v_kv_norm, v_w_ukv, v_pool_w, v_pool_scale, v_w_branch_attn, v_w_branch_pool, v_w_out, v_norm_final):
    w = dict(norm_in=norm_in, w_in=w_in, q_norm=q_norm, w_uq=w_uq, kv_norm=kv_norm, w_ukv=w_ukv, pool_w=pool_w, pool_scale=pool_scale,
             w_branch_attn=w_branch_attn, w_branch_pool=w_branch_pool, w_out=w_out, norm_final=norm_final)
    m = dict(norm_in=m_norm_in, w_in=m_w_in, q_norm=m_q_norm, w_uq=m_w_uq, kv_norm=m_kv_norm, w_ukv=m_w_ukv, pool_w=m_pool_w,
             pool_scale=m_pool_scale, w_branch_attn=m_w_branch_attn, w_branch_pool=m_w_branch_pool, w_out=m_w_out, norm_final=m_norm_final)
    v = dict(norm_in=v_norm_in, w_in=v_w_in, q_norm=v_q_norm, w_uq=v_w_uq, kv_norm=v_kv_norm, w_ukv=v_w_ukv, pool_w=v_pool_w,
             pool_scale=v_pool_scale, w_branch_attn=v_w_branch_attn, w_branch_pool=v_w_branch_pool, w_out=v_w_out, norm_final=v_norm_final)

    def as2d(name, a):
        if name == "w_in":
            return a.T
        if name in ("w_uq", "w_ukv"):
            return a
        return a.reshape(-1, GROUP) if name == "pool_w" else a

    def unshape(name, a):
        return a.T if name == "w_in" else a.reshape(w[name].shape)

    w_in_pad, w_uq_full, w_ukv_full = _all_gather_bf16([as2d(k, w[k]) for k in ("w_in", "w_uq", "w_ukv")])
    loss, grad_x, g2d = _local_step(
        x.reshape(x.shape[1:]), loss_target.reshape(x.shape[1:]), norm_in, w_in_pad, q_norm,
        w_uq_full.reshape(Q_RANK, HW), kv_norm, w_ukv_full.reshape(KV_RANK, HW),
        pool_w, pool_scale, [w_branch_attn, w_branch_pool, w_out], norm_final)

    deltas, new_m, new_v, grads = _adamw([as2d(k, w[k]) for k in WEIGHTS], [g2d[k] for k in WEIGHTS],
                                         [as2d(k, m[k]) for k in WEIGHTS], [as2d(k, v[k]) for k in WEIGHTS],
                                         [k in ("w_in", "w_branch_attn", "w_branch_pool", "w_out") for k in WEIGHTS])
    shaped = lambda arrs: [unshape(k, a) for k, a in zip(WEIGHTS, arrs)]
    return (loss, grad_x.reshape(x.shape), *shaped(grads), *shaped(deltas), *shaped(new_m), *shaped(new_v))
```

```python
import functools

import jax
import jax.numpy as jnp
import numpy as np
from jax import lax
from jax.experimental import pallas as pl
from jax.experimental.pallas import tpu as pltpu

F32 = jnp.float32
BF16 = jnp.bfloat16
SDS = jax.ShapeDtypeStruct

D_MODEL = 1024
HEADS = 8
NOPE = 64
ROPE = 32
VDIM = 64
Q_RANK = 384
KV_RANK = 256
MLA_W = HEADS * VDIM
POOL_W = 512
POOL_GROUPS = 4
GROUP = POOL_W // POOL_GROUPS
CHUNK = 64
ROPE_THETA = 10000.0
EPS = 1e-6
SCALE = (NOPE + ROPE) ** -0.5
LOG2E = 1.4426950408889634
LN2 = 0.6931471805599453
QK_SCALE_LOG2 = SCALE * LOG2E
IN_TOTAL = 4256
ADAM_LR, ADAM_B1, ADAM_B2, ADAM_EPS, ADAM_WD, ADAM_STEP = 0.001, 0.9, 0.999, 1e-08, 0.01, 10

N_DEV = 8
LANES = 128
HEAD_PAD = LANES
HW = HEADS * HEAD_PAD

ZQ, ZKV, ZKR, GA, UP, GP, GM, ZTOT = 0, 384, 640, 768, 1280, 1792, 2304, 4352
FRONT_W = GA
ZKR_ORIG = 640

VMEM_LIMIT = 62 * 1024 * 1024


def _cp(sem=None, **kw):
    if sem is not None:
        kw["dimension_semantics"] = sem
    return pltpu.CompilerParams(vmem_limit_bytes=VMEM_LIMIT, **kw)


def _mm(a, b):
    return lax.dot_general(a, b, (((1,), (0,)), ((), ())), preferred_element_type=F32)


def _mm_nt(a, b):
    return lax.dot_general(a, b, (((1,), (1,)), ((), ())), preferred_element_type=F32)


def _mm_tn(a, b):
    return lax.dot_general(a, b, (((0,), (0,)), ((), ())), preferred_element_type=F32)


def _row_spec(tm, w):
    return pl.BlockSpec((tm, w), lambda i: (i, 0))


def _full_spec(a):
    nd = len(a.shape)
    return pl.BlockSpec(a.shape, lambda *_: (0,) * nd)


def _rope(v, c, sa, sb, sign):
    n = v.shape[-1]
    reps = n // LANES
    if reps > 1:
        c, sa, sb = (jnp.tile(t, (1, reps)) for t in (c, sa, sb))
    up = pltpu.roll(v, n - ROPE // 2, 1)
    dn = pltpu.roll(v, ROPE // 2, 1)
    return v * c + sign * (up * sa + dn * sb)


def _rope_tables(T):
    half = ROPE // 2
    inv_freq = np.float32(ROPE_THETA) ** (-np.arange(half, dtype=np.float32) / np.float32(half))
    ang = np.arange(T, dtype=np.float32)[:, None] * inv_freq[None, :].astype(np.float32)
    cos, sin = np.cos(ang.astype(np.float64)).astype(np.float32), np.sin(ang.astype(np.float64)).astype(np.float32)
    z16 = np.zeros((T, half), np.float32)
    z32 = np.zeros((T, LANES - NOPE - ROPE), np.float32)
    c = np.concatenate([np.ones((T, NOPE), np.float32), cos, cos, z32], axis=1)
    sa = np.concatenate([np.zeros((T, NOPE), np.float32), -sin, z16, z32], axis=1)
    sb = np.concatenate([np.zeros((T, NOPE), np.float32), z16, sin, z32], axis=1)
    return jnp.asarray(c), jnp.asarray(sa), jnp.asarray(sb)


def _silu_parts(g):
    sg = jax.nn.sigmoid(g)
    return g * sg, sg + g * sg * (1.0 - sg)


def _in_proj(x2, norm_in, w_in_pad, q_norm, wuq_pad, kv_norm, wukv, rc, rsa, rsb, tm, late_shards):
    T = x2.shape[0]
    steps = T // tm
    n = len(late_shards)

    def body(x_ref, nin_ref, win_ref, qn_ref, wuq_ref, kvn_ref, wukv_ref, c_ref, sa_ref, sb_ref, *rest):
        hn_ref, zgm_ref, zga_ref, zup_ref, zgp_ref, zfr_ref, q_ref, k_ref, v_ref, vt_ref = rest[n:n + 10]
        ag = _AllGather(rest[:n], rest[n + 10:2 * n + 10], rest[2 * n + 10:])
        step = pl.program_id(0)
        pl.when(step == 0)(ag.start)
        pl.when(step == min(3, steps - 1))(ag.forward)
        xf = x_ref[...]
        r = lax.rsqrt(jnp.mean(xf * xf, axis=-1, keepdims=True) + EPS)
        hn = (xf * r * nin_ref[...]).astype(BF16)
        hn_ref[...] = hn
        zfr = _mm_nt(hn, win_ref[ZQ:GA, :])
        zfr_ref[...] = zfr
        zq, zkv, zkr = zfr[:, ZQ:ZKV], zfr[:, ZKV:ZKR], zfr[:, ZKR:GA]
        c, sa, sb = c_ref[...], sa_ref[...], sb_ref[...]
        rq = lax.rsqrt(jnp.mean(zq * zq, axis=-1, keepdims=True) + EPS)
        cq = (zq * rq * qn_ref[...].reshape(1, Q_RANK)).astype(BF16)
        rkv = lax.rsqrt(jnp.mean(zkv * zkv, axis=-1, keepdims=True) + EPS)
        ckv = (zkv * rkv * kvn_ref[...]).astype(BF16)
        zga_ref[...] = _mm_nt(hn, win_ref[GA:UP, :])
        zup_ref[...] = _mm_nt(hn, win_ref[UP:GP, :])
        zgp_ref[...] = _mm_nt(hn, win_ref[GP:GM, :])
        q_raw = _mm(cq, wuq_ref[...])
        kv = _mm(ckv, wukv_ref[...])
        zgm_ref[...] = _mm_nt(hn, win_ref[GM:ZTOT, :])
        q = _rope(q_raw, c, sa, sb, 1.0)
        q_ref[...] = (q * QK_SCALE_LOG2).astype(BF16)
        kr = _rope(zkr, c, sa, sb, 1.0)
        lane = lax.broadcasted_iota(jnp.int32, kv.shape, 1) % LANES
        k_ref[...] = jnp.where(lane < NOPE, kv, jnp.tile(kr, (1, HEADS))).astype(BF16)
        v = jnp.where(lane < NOPE, 1.0, kv).astype(BF16)
        v_ref[...] = v
        vt_ref[...] = v.T
        pl.when(step == steps - 1)(ag.finish)

    ins = (x2, norm_in, w_in_pad, q_norm, wuq_pad, kv_norm, wukv, rc, rsa, rsb)
    in_specs = [_row_spec(tm, D_MODEL), _full_spec(norm_in), _full_spec(w_in_pad), _full_spec(q_norm), _full_spec(wuq_pad),
                _full_spec(kv_norm), _full_spec(wukv), _row_spec(tm, LANES), _row_spec(tm, LANES), _row_spec(tm, LANES)]
    widths = [(D_MODEL, BF16), (ZTOT - GM, F32), (UP - GA, F32), (GP - UP, F32), (GM - GP, F32), (FRONT_W, F32),
              (HW, BF16), (HW, BF16), (HW, BF16)]
    return pl.pallas_call(
        body, name="in_proj", grid=(steps,), in_specs=in_specs + [_full_spec(s) for s in late_shards],
        out_specs=[_row_spec(tm, w) for w, _ in widths] + [pl.BlockSpec((HW, tm), lambda i: (0, i))] + [HBM_SPEC] * n,
        out_shape=[SDS((T, w), dt) for w, dt in widths] + [SDS((HW, T), BF16)]
        + [SDS((N_DEV,) + s.shape, BF16) for s in late_shards],
        scratch_shapes=_ag_scratch([s.shape for s in late_shards]), compiler_params=_cp(("arbitrary",)),
    )(*ins, *late_shards)


def _chunk_mask(n_q, n_k, q_off, transposed):
    shape = (n_k, n_q) if transposed else (n_q, n_k)
    q = (lax.broadcasted_iota(jnp.int32, shape, 1 if transposed else 0) + q_off) // CHUNK
    k = lax.broadcasted_iota(jnp.int32, shape, 0 if transposed else 1) // CHUNK
    return k <= q


def _store_pair_rows(ref, k, pair):
    t = pair.T
    ref[k, 0:1, :] = t[0:1, :]
    ref[k, 1:2, :] = t[VDIM:VDIM + 1, :]


def _attn_fwd(q_att, k_att, vt_att, tq, hps):
    T = q_att.shape[0]
    head_lanes = [slice(h * LANES, (h + 1) * LANES) for h in range(hps)]

    def body(q_ref, k_ref, vt_ref, o_ref, lser_ref):
        i = pl.program_id(1)
        mask = _chunk_mask(tq, tq, 0, True)
        lane = lax.broadcasted_iota(jnp.int32, (tq, LANES), 1)
        qs = [q_ref[:, hs] for hs in head_lanes]

        def step(j, carry, masked):
            off = pl.multiple_of(j * tq, tq)
            sts = [_mm_nt(k_ref[pl.ds(off, tq), hs], qh) for qh, hs in zip(qs, head_lanes)]
            if masked:
                sts = [jnp.where(mask, st, -jnp.inf) for st in sts]
            ms = [jnp.maximum(m, jnp.max(st, axis=0, keepdims=True)) for (m, _), st in zip(carry, sts)]
            pts = [jnp.exp2(st - m_new).astype(BF16) for st, m_new in zip(sts, ms)]
            return tuple((m_new, jnp.exp2(m - m_new) * acc + _mm(vt_ref[hs, pl.ds(off, tq)], pt))
                         for (m, acc), m_new, pt, hs in zip(carry, ms, pts, head_lanes))

        init = ((jnp.full((1, tq), -jnp.inf, F32), jnp.zeros((LANES, tq), F32)),) * hps
        res = step(i, lax.fori_loop(0, i, functools.partial(step, masked=False), init), True)
        for pair in range(hps // 2):
            (ma, acca), (mb, accb) = res[2 * pair], res[2 * pair + 1]
            la, lb = acca[:1], accb[:1]
            oa, ob = (acca / la).T, (accb / lb).T
            o_ref[:, pair * LANES:(pair + 1) * LANES] = jnp.where(lane < VDIM, pltpu.roll(oa, VDIM, 1), ob)
            lser_ref[pair, 0:1, :] = ma + jnp.log2(la)
            lser_ref[pair, 1:2, :] = mb + jnp.log2(lb)

    qspec = pl.BlockSpec((tq, hps * LANES), lambda p, i: (i, p))
    kspec = pl.BlockSpec((T, hps * LANES), lambda p, i: (0, p))
    vspec = pl.BlockSpec((hps * LANES, T), lambda p, i: (p, 0))
    ospec = pl.BlockSpec((tq, hps * VDIM), lambda p, i: (i, p))
    return pl.pallas_call(
        body, name="attn_fwd", grid=(HEADS // hps, T // tq), in_specs=[qspec, kspec, vspec],
        out_specs=[ospec, pl.BlockSpec((hps // 2, 2, tq), lambda p, i: (p, 0, i))],
        out_shape=[SDS((T, MLA_W), F32), SDS((HEADS // 2, 2, T), F32)],
        compiler_params=_cp(("parallel", "parallel")),
    )(q_att, k_att, vt_att)


def _pick(g, vals):
    out = vals[-1]
    for k in range(len(vals) - 2, -1, -1):
        out = jnp.where(g == k, vals[k], out)
    return out


def _window_sum(u, g, forward):
    T = u.shape[0]
    row = lax.broadcasted_iota(jnp.int32, u.shape, 0)

    def sh(s, k):
        if forward:
            return jnp.where(row >= k, pltpu.roll(s, k, 0), 0.0)
        return jnp.where(row < T - k, pltpu.roll(s, T - k, 0), 0.0)

    sums, s = [], u
    for k in (1, 2, 4, 8):
        s = s + sh(s, k)
        sums.append(s)
    return _pick(g, sums)


MAX_WINDOW = 16


def _pool_inv_count(shape, g):
    T, n = shape
    row = lax.broadcasted_iota(jnp.int32, (MAX_WINDOW, n), 0)
    head = 1.0 / jnp.minimum(row + 1, lax.shift_left(jnp.int32(2), g)).astype(F32)
    inv_w = _pick(g, [0.5, 0.25, 0.125, 0.0625])
    return jnp.concatenate([head, jnp.broadcast_to(inv_w, (T - MAX_WINDOW, n)).astype(F32)], axis=0)


def _pool_fwd(zup, zgp, pool_w, pool_scale):
    T = zup.shape[0]

    def body(u_ref, g_ref, w_ref, sc_ref, y_ref):
        g = pl.program_id(0)
        u = u_ref[...]
        d = _window_sum(u, g, True) * _pool_inv_count(u.shape, g) - u
        lin = _mm(d.astype(BF16), w_ref[0].astype(BF16))
        silu, _ = _silu_parts(g_ref[...])
        y_ref[...] = (lin * sc_ref[...] * silu).astype(BF16)

    col = pl.BlockSpec((T, GROUP), lambda g: (0, g))
    return pl.pallas_call(
        body, name="pool_fwd", grid=(POOL_GROUPS,),
        in_specs=[col, col, pl.BlockSpec((1, GROUP, GROUP), lambda g: (g, 0, 0)), pl.BlockSpec((1, GROUP), lambda g: (0, g))],
        out_specs=col, out_shape=SDS((T, POOL_W), BF16), compiler_params=_cp(("parallel",)),
    )(zup, zgp, pool_w, pool_scale)


def _pool_bwd(zup, zgp, dyp, pool_w, pool_scale):
    T = zup.shape[0]

    def body(u_ref, g_ref, dy_ref, w_ref, sc_ref, du_ref, dg_ref, gw_ref, gsc_ref):
        g = pl.program_id(0)
        u = u_ref[...]
        inv = _pool_inv_count(u.shape, g)
        d = (_window_sum(u, g, True) * inv - u).astype(BF16)
        wb = w_ref[0].astype(BF16)
        lin = _mm(d, wb)
        sc = sc_ref[...]
        silu, dsilu = _silu_parts(g_ref[...])
        dy = dy_ref[...]
        dg_ref[...] = (dy * lin * sc * dsilu).astype(BF16)
        dpre = dy * silu
        gsc_ref[...] = jnp.sum(dpre * lin, axis=0, keepdims=True)
        dlin = (dpre * sc).astype(BF16)
        gw_ref[0] = _mm_tn(d, dlin)
        dd = _mm_nt(dlin, wb)
        du_ref[...] = (_window_sum(dd * inv, g, False) - dd).astype(BF16)

    col = pl.BlockSpec((T, GROUP), lambda g: (0, g))
    wspec = pl.BlockSpec((1, GROUP, GROUP), lambda g: (g, 0, 0))
    vspec = pl.BlockSpec((1, GROUP), lambda g: (0, g))
    return pl.pallas_call(
        body, name="pool_bwd", grid=(POOL_GROUPS,), in_specs=[col, col, col, wspec, vspec], out_specs=[col, col, wspec, vspec],
        out_shape=[SDS((T, POOL_W), BF16), SDS((T, POOL_W), BF16), SDS((POOL_GROUPS, GROUP, GROUP), F32), SDS((1, POOL_W), F32)],
        compiler_params=_cp(("parallel",)),
    )(zup, zgp, dyp, pool_w, pool_scale)


def _tail(x2, tgt, o, zga, ypool, zgm, wba, wbp, wout, norm_final, tm):
    T = x2.shape[0]
    steps = T // tm
    cols = D_MODEL // N_DEV

    def body(x_ref, tgt_ref, o_ref, zga_ref, yp_ref, zgm_ref, wba_ref, wbp_ref, wout_ref, nf_ref,
             loss_ref, dh_ref, dgm_ref, doop_ref, dga_ref, dcapr_ref, dyp_ref, swout_ref, swba_ref, swbp_ref, gnf_ref,
             gwout_ref, gwba_ref, gwbp_ref):
        @pl.when(pl.program_id(0) == 0)
        def _():
            for ref in (loss_ref, gwout_ref, gwba_ref, gwbp_ref, gnf_ref):
                ref[...] = jnp.zeros_like(ref)

        o_v = o_ref[...]
        silu, dsilu = _silu_parts(zga_ref[...])
        ya = (o_v * silu).astype(BF16)
        yp = yp_ref[...]
        wba_v = jnp.concatenate([wba_ref[k] for k in range(N_DEV)], axis=1)
        wbp_v = jnp.concatenate([wbp_ref[k] for k in range(N_DEV)], axis=1)
        wout_v = wout_ref[...]
        a = _mm(ya, wba_v)
        p = _mm(yp, wbp_v)
        gate = jax.nn.sigmoid(zgm_ref[...])
        ga, gp = gate[:, :D_MODEL], gate[:, D_MODEL:]
        mg = (ga * a + gp * p).astype(BF16)
        h = x_ref[...] + _mm(mg, wout_v)
        r = lax.rsqrt(jnp.mean(h * h, axis=-1, keepdims=True) + EPS)
        gf = nf_ref[...]
        hr = h * r
        e = hr * gf - tgt_ref[...]
        loss_ref[...] += (0.5 / D_MODEL) * jnp.sum(e * e)
        dy = e * (1.0 / D_MODEL)
        gnf_ref[...] += jnp.sum(dy * hr, axis=0, keepdims=True)
        u = dy * gf
        dh = r * (u - hr * jnp.mean(u * hr, axis=-1, keepdims=True))
        dh_ref[...] = dh
        dhb = dh.astype(BF16)
        dmg = _mm_nt(dhb, wout_v)
        dab = (dmg * ga).astype(BF16)
        dpb = (dmg * gp).astype(BF16)
        dya = _mm_nt(dab, wba_v)
        dyp_ref[...] = _mm_nt(dpb, wbp_v)
        gwout_ref[...] += _mm_tn(mg, dhb)
        gwba_ref[...] += _mm_tn(ya, dab)
        gwbp_ref[...] += _mm_tn(yp, dpb)
        dgm_ref[:, :D_MODEL] = (dmg * a * ga * (1.0 - ga)).astype(BF16)
        dgm_ref[:, D_MODEL:] = (dmg * p * gp * (1.0 - gp)).astype(BF16)
        do = dya * silu
        dga_ref[...] = (dya * o_v * dsilu).astype(BF16)
        prod = do * o_v
        lo = lax.broadcasted_iota(jnp.int32, (tm, LANES), 1) < VDIM
        for pair in range(HEADS // 2):
            ls = slice(pair * LANES, (pair + 1) * LANES)
            do_p, prod_p = do[:, ls], prod[:, ls]
            dcap_a = jnp.sum(jnp.where(lo, prod_p, 0.0), axis=-1, keepdims=True)
            dcap_b = jnp.sum(jnp.where(lo, 0.0, prod_p), axis=-1, keepdims=True)
            _store_pair_rows(dcapr_ref, pair, jnp.where(lo, dcap_a, dcap_b))
            doop_ref[:, 2 * pair * LANES:(2 * pair + 1) * LANES] = jnp.where(lo, 0.0, pltpu.roll(do_p, VDIM, 1)).astype(BF16)
            doop_ref[:, (2 * pair + 1) * LANES:(2 * pair + 2) * LANES] = jnp.where(lo, 0.0, do_p).astype(BF16)

        @pl.when(pl.program_id(0) == steps - 1)
        def _():
            for k in range(N_DEV):
                swout_ref[k] = gwout_ref[k * cols:(k + 1) * cols, :].astype(BF16)
                swba_ref[k] = gwba_ref[:, k * cols:(k + 1) * cols].astype(BF16)
                swbp_ref[k] = gwbp_ref[:, k * cols:(k + 1) * cols].astype(BF16)

    ins = (x2, tgt, o, zga, ypool, zgm, wba, wbp, wout, norm_final)
    in_specs = [_row_spec(tm, D_MODEL), _row_spec(tm, D_MODEL), _row_spec(tm, MLA_W), _row_spec(tm, MLA_W), _row_spec(tm, POOL_W),
                _row_spec(tm, 2 * D_MODEL), _full_spec(wba), _full_spec(wbp), _full_spec(wout), _full_spec(norm_final)]
    outs = [SDS((8, LANES), F32), SDS((T, D_MODEL), F32), SDS((T, 2 * D_MODEL), BF16), SDS((T, HW), BF16), SDS((T, MLA_W), BF16),
            SDS((HEADS // 2, 2, T), F32), SDS((T, POOL_W), F32),
            SDS((N_DEV, cols, D_MODEL), BF16), SDS((N_DEV, MLA_W, cols), BF16), SDS((N_DEV, POOL_W, cols), BF16), SDS((1, D_MODEL), F32)]
    out_specs = [_full_spec(outs[0]), _row_spec(tm, D_MODEL), _row_spec(tm, 2 * D_MODEL), _row_spec(tm, HW), _row_spec(tm, MLA_W),
                 pl.BlockSpec((HEADS // 2, 2, tm), lambda i: (0, 0, i)), _row_spec(tm, POOL_W),
                 _full_spec(outs[7]), _full_spec(outs[8]), _full_spec(outs[9]), _full_spec(outs[10])]
    return pl.pallas_call(
        body, name="tail", grid=(steps,), in_specs=in_specs, out_specs=out_specs, out_shape=outs,
        scratch_shapes=[pltpu.VMEM((D_MODEL, D_MODEL), F32), pltpu.VMEM((MLA_W, D_MODEL), F32), pltpu.VMEM((POOL_W, D_MODEL), F32)],
        compiler_params=_cp(("arbitrary",)),
    )(*ins)


def _attn_bwd(q_att, k_att, v_att, doop, lse_rows, dcap_rows, tq, hps, slabs, packed):
    T = q_att.shape[0]
    nq = T // tq
    n = len(slabs)
    groups = HEADS // hps
    head_lanes = [slice(h * LANES, (h + 1) * LANES) for h in range(hps)]

    def body(q_ref, k_ref, v_ref, doop_ref, lse_ref, dcap_ref, *rest):
        slab_refs, packed_ref = rest[:n], rest[n]
        dq_ref, dkv_ref, dkr_ref = rest[n + 1:n + 4]
        sum_refs, ptot_ref = rest[n + 4:2 * n + 4], rest[2 * n + 4]
        dq_acc = rest[2 * n + 5]
        rs = _ReduceScatter(slab_refs, packed_ref, sum_refs, ptot_ref, rest[2 * n + 6:])
        group, j = pl.program_id(0), pl.program_id(1)
        tick, last = group * nq + j, groups * nq - 1
        pl.when(tick == 0)(rs.start1)
        pl.when(tick == min(1, last))(rs.finish1_start2)
        pl.when(tick == min(3, last))(rs.relay2)
        mask = _chunk_mask(tq, tq, 0, True)
        lane = lax.broadcasted_iota(jnp.int32, (tq, LANES), 1)
        ks = [k_ref[:, hs] for hs in head_lanes]
        vs = [v_ref[:, hs] for hs in head_lanes]
        kts = [kh.T for kh in ks]

        @pl.when(j == 0)
        def _():
            dq_acc[...] = jnp.zeros_like(dq_acc)

        def step(i, carry, masked):
            rows = pl.ds(pl.multiple_of(i * tq, tq), tq)
            heads = range(hps)
            stat = lambda h: (h // 2, slice(h % 2, h % 2 + 1), rows)
            qhs = [q_ref[rows, hs] for hs in head_lanes]
            doops = [doop_ref[rows, hs] for hs in head_lanes]
            sts = [_mm_nt(ks[h], qhs[h]) for h in heads]
            dpts = [_mm_nt(vs[h], doops[h]) for h in heads]
            pts = [jnp.exp2(sts[h] - lse_ref[stat(h)]) for h in heads]
            if masked:
                pts = [jnp.where(mask, pt, 0.0) for pt in pts]
            dsts = [(pts[h] * (dpts[h] - dcap_ref[stat(h)])).astype(BF16) for h in heads]
            dvs = [_mm(pts[h].astype(BF16), doops[h]) for h in heads]
            dks = [_mm(dsts[h], qhs[h]) for h in heads]
            for h, hs in enumerate(head_lanes):
                dq_acc[hs, rows] += _mm(kts[h], dsts[h])
            return tuple((dk + dks[h], dv + dvs[h]) for h, (dk, dv) in enumerate(carry))

        zero = jnp.zeros((tq, LANES), F32)
        carry = step(j, ((zero, zero),) * hps, True)
        res = lax.fori_loop(j + 1, nq, functools.partial(step, masked=False), carry)
        dkr = None
        for (dk, dv), hs in zip(res, head_lanes):
            dk = dk * LN2
            dkv_ref[:, hs] = jnp.where(lane < NOPE, dk, dv).astype(BF16)
            dkr = dk if dkr is None else dkr + dk
        dkr_ref[0] = jnp.where((lane >= NOPE) & (lane < NOPE + ROPE), dkr, 0.0)

        @pl.when(j == nq - 1)
        def _():
            dq_ref[...] = (dq_acc[...] * SCALE).T.astype(BF16)

        pl.when(tick == last)(rs.finish2)

    kspec = pl.BlockSpec((tq, hps * LANES), lambda p, j: (j, p))
    qspec = pl.BlockSpec((T, hps * LANES), lambda p, j: (0, p))
    rspec = pl.BlockSpec((hps // 2, 2, T), lambda p, j: (p, 0, 0))
    sums = [SDS(s.shape[1:], F32) for s in slabs] + [SDS(packed.shape, F32)]
    return pl.pallas_call(
        body, name="attn_bwd", grid=(groups, nq),
        in_specs=[qspec, kspec, kspec, qspec, rspec, rspec] + [HBM_SPEC] * n + [_full_spec(packed)],
        out_specs=[qspec, kspec, pl.BlockSpec((1, tq, LANES), lambda p, j: (p, j, 0))] + [_full_spec(s) for s in sums],
        out_shape=[SDS((T, HW), BF16), SDS((T, HW), BF16), SDS((groups, T, LANES), F32)] + sums,
        scratch_shapes=[pltpu.VMEM((hps * LANES, T), F32)] + _rs_scratch([s.shape for s in sums[:-1]], packed.shape),
        compiler_params=_cp(("arbitrary", "arbitrary")),
    )(q_att, k_att, v_att, doop, lse_rows, dcap_rows, *slabs, packed)


def _rms_bwd(z, gain, dout):
    r = lax.rsqrt(jnp.mean(z * z, axis=-1, keepdims=True) + EPS)
    zr = z * r
    u = dout * gain
    return r * (u - zr * jnp.mean(u * zr, axis=-1, keepdims=True)), jnp.sum(dout * zr, axis=0, keepdims=True)


def _mla_bwd(dq_att, dkv_nat, dkr4, zfr, q_norm, wuq_pad, kv_norm, wukv, rc, rsa, rsb, tm):
    T = dq_att.shape[0]

    def body(dq_ref, dkv_ref, dkr_ref, zfr_ref, qn_ref, wuq_ref, kvn_ref, wukv_ref, c_ref, sa_ref, sb_ref,
             dfr_ref, gwuq_out, gwukv_out, gqn_ref, gkvn_ref, gwuq_ref, gwukv_ref):
        @pl.when(pl.program_id(0) == 0)
        def _():
            for ref in (gwuq_ref, gwukv_ref, gqn_ref, gkvn_ref):
                ref[...] = jnp.zeros_like(ref)

        c, sa, sb = c_ref[...], sa_ref[...], sb_ref[...]
        zq, zkv = zfr_ref[:, :Q_RANK], zfr_ref[:, Q_RANK:Q_RANK + KV_RANK]
        qn, kvn = qn_ref[...].reshape(1, Q_RANK), kvn_ref[...]
        dkv = dkv_ref[...]
        dckv = _mm_nt(dkv, wukv_ref[...])
        ckv = (zkv * lax.rsqrt(jnp.mean(zkv * zkv, axis=-1, keepdims=True) + EPS) * kvn).astype(BF16)
        gwukv_ref[...] += _mm_tn(ckv, dkv)
        cq = (zq * lax.rsqrt(jnp.mean(zq * zq, axis=-1, keepdims=True) + EPS) * qn).astype(BF16)
        dq = _rope(dq_ref[...].astype(F32), c, sa, sb, -1.0).astype(BF16)
        dzkv, gkvn = _rms_bwd(zkv, kvn, dckv)
        gkvn_ref[...] += gkvn
        gwuq_ref[...] += _mm_tn(cq, dq)
        dzq, gqn = _rms_bwd(zq, qn, _mm_nt(dq, wuq_ref[...]))
        gqn_ref[...] += gqn
        dkr = functools.reduce(lambda a, b: a + b, [dkr_ref[g] for g in range(dkr4.shape[0])])
        dfr_ref[:, :Q_RANK] = dzq.astype(BF16)
        dfr_ref[:, Q_RANK:Q_RANK + KV_RANK] = dzkv.astype(BF16)
        dfr_ref[:, Q_RANK + KV_RANK:] = _rope(dkr, c, sa, sb, -1.0).astype(BF16)

        @pl.when(pl.program_id(0) == pl.num_programs(0) - 1)
        def _():
            gwuq_out[...] = gwuq_ref[...].astype(BF16)
            gwukv_out[...] = gwukv_ref[...].astype(BF16)

    ins = (dq_att, dkv_nat, dkr4, zfr, q_norm, wuq_pad, kv_norm, wukv, rc, rsa, rsb)
    in_specs = [_row_spec(tm, HW), _row_spec(tm, HW), pl.BlockSpec((dkr4.shape[0], tm, LANES), lambda i: (0, i, 0)), _row_spec(tm, FRONT_W),
                _full_spec(q_norm), _full_spec(wuq_pad), _full_spec(kv_norm), _full_spec(wukv),
                _row_spec(tm, LANES), _row_spec(tm, LANES), _row_spec(tm, LANES)]
    outs = [SDS((T, FRONT_W), BF16), SDS((Q_RANK, HW), BF16), SDS((KV_RANK, HW), BF16), SDS((1, Q_RANK), F32), SDS((1, KV_RANK), F32)]
    out_specs = [_row_spec(tm, FRONT_W)] + [_full_spec(s) for s in outs[1:]]
    return pl.pallas_call(
        body, name="mla_bwd", grid=(T // tm,), in_specs=in_specs, out_specs=out_specs, out_shape=outs,
        scratch_shapes=[pltpu.VMEM((Q_RANK, HW), F32), pltpu.VMEM((KV_RANK, HW), F32)],
        compiler_params=_cp(("arbitrary",)),
    )(*ins)


_DZ_COLS = ((GM, ZTOT), (GA, UP), (UP, GP), (GP, GM), (ZQ, GA))


def _in_proj_bwd_x(dzs, x2, dh, norm_in, w_in_pad, tm, slabs):
    T = x2.shape[0]
    steps = T // tm
    n = len(slabs)

    def body(d0, d1, d2, d3, d4, x_ref, dh_ref, nin_ref, win_ref, *rest):
        slab_refs, (gx_ref, gnin_ref), sum_refs = rest[:n], rest[n:n + 2], rest[n + 2:2 * n + 2]
        rs = _ReduceScatter(slab_refs, None, sum_refs, None, rest[2 * n + 2:])
        step = pl.program_id(0)

        @pl.when(step == 0)
        def _():
            gnin_ref[...] = jnp.zeros_like(gnin_ref)
            rs.start1()

        pl.when(step == min(2, steps - 1))(rs.finish1_start2)
        pl.when(step == min(steps // 2, steps - 1))(rs.relay2)
        dhn = None
        for ref, (lo, hi) in zip((d0, d1, d2, d3, d4), _DZ_COLS):
            t = _mm(ref[...], win_ref[lo:hi, :])
            dhn = t if dhn is None else dhn + t
        dx, gnin = _rms_bwd(x_ref[...], nin_ref[...], dhn)
        gnin_ref[...] += gnin
        gx_ref[...] = dx + dh_ref[...]
        pl.when(step == steps - 1)(rs.finish2)

    in_specs = [_row_spec(tm, hi - lo) for lo, hi in _DZ_COLS] + [_row_spec(tm, D_MODEL), _row_spec(tm, D_MODEL),
                                                                  _full_spec(norm_in), _full_spec(w_in_pad)] + [HBM_SPEC] * n
    sums = [SDS(s.shape[1:], F32) for s in slabs]
    outs = [SDS((T, D_MODEL), F32), SDS((1, D_MODEL), F32)] + sums
    return pl.pallas_call(
        body, name="in_proj_bwd_x", grid=(steps,), in_specs=in_specs,
        out_specs=[_row_spec(tm, D_MODEL), _full_spec(outs[1])] + [_full_spec(s) for s in sums],
        out_shape=outs, scratch_shapes=_rs_scratch([s.shape for s in sums], None), compiler_params=_cp(("arbitrary",)),
    )(*dzs, x2, dh, norm_in, w_in_pad, *slabs)


SLAB_ROWS = IN_TOTAL // N_DEV


def _slab_segments(k):
    cuts = [(0, ZKR_ORIG, 0), (ZKR_ORIG, ZKR_ORIG + ROPE, NOPE), (ZKR_ORIG + ROPE, IN_TOTAL, LANES - ROPE)]
    lo, hi = k * SLAB_ROWS, (k + 1) * SLAB_ROWS
    return [(max(lo, a) - lo, max(lo, a) + shift, min(hi, b) - max(lo, a)) for a, b, shift in cuts if min(hi, b) > max(lo, a)]


def _in_proj_bwd_w(dzs, hn, tm):
    T = hn.shape[0]
    steps = T // tm

    def body(d0, d1, d2, d3, d4, hn_ref, slab_ref, acc_ref):
        @pl.when(pl.program_id(0) == 0)
        def _():
            acc_ref[...] = jnp.zeros_like(acc_ref)

        hn_v = hn_ref[...]
        for ref, (lo, hi) in zip((d0, d1, d2, d3, d4), _DZ_COLS):
            acc_ref[lo:hi, :] += _mm_tn(ref[...], hn_v)

        @pl.when(pl.program_id(0) == steps - 1)
        def _():
            for k in range(N_DEV):
                for at, src, rows in _slab_segments(k):
                    slab_ref[k, at:at + rows, :] = acc_ref[src:src + rows, :].astype(BF16)

    in_specs = [_row_spec(tm, hi - lo) for lo, hi in _DZ_COLS] + [_row_spec(tm, D_MODEL)]
    out = SDS((N_DEV, SLAB_ROWS, D_MODEL), BF16)
    return pl.pallas_call(
        body, name="in_proj_bwd_w", grid=(steps,), in_specs=in_specs, out_specs=_full_spec(out), out_shape=out,
        scratch_shapes=[pltpu.VMEM((ZTOT, D_MODEL), F32)], compiler_params=_cp(("arbitrary",)),
    )(*dzs, hn)


def _local_step(x2, tgt, norm_in, w_in_pad, q_norm, wuq_pad, kv_norm, wukv, pool_w, pool_scale, late_shards, norm_final):
    T = x2.shape[0]
    tm, tm_small, tq = min(512, T), min(256, T), min(512, T)
    heads_fwd, heads_bwd = 4, 4
    row = lambda v: v.reshape(1, -1)
    rc, rsa, rsb = _rope_tables(T)

    hn, zgm, zga, zup, zgp, zfr, q_att, k_att, v_att, vt_att, w_ba, w_bp, w_out = _in_proj(
        x2, row(norm_in), w_in_pad, q_norm, wuq_pad, row(kv_norm), wukv, rc, rsa, rsb, tm, late_shards)
    w_out = w_out.reshape(D_MODEL, D_MODEL)
    o, lse_rows = _attn_fwd(q_att, k_att, vt_att, tq, heads_fwd)
    ypool = _pool_fwd(zup, zgp, pool_w, row(pool_scale))
    loss8, dh, dgm, doop, dga, dcap_rows, dyp, *slabs, g_nf = _tail(
        x2, tgt, o, zga, ypool, zgm, w_ba, w_bp, w_out, row(norm_final), tm_small)
    dup, dgp, g_pool_w, g_pool_scale = _pool_bwd(zup, zgp, dyp, pool_w, row(pool_scale))

    early = [g_pool_w, g_pool_scale, g_nf, loss8[0]]
    packed = jnp.concatenate([_pack_rows(a) for a in early], axis=0)
    dq_att, dkv_nat, dkr4, s_wout, s_wba, s_wbp, tot_early = _attn_bwd(
        q_att, k_att, v_att, doop, lse_rows, dcap_rows, tq, heads_bwd, slabs, packed)
    at_pool_w, at_pool_scale, at_nf, at_loss = _packed_at(early)

    dfr, g_wuq_pad, g_wukv, g_qn, g_kvn = _mla_bwd(
        dq_att, dkv_nat, dkr4, zfr, q_norm, wuq_pad, row(kv_norm), wukv, rc, rsa, rsb, tm)
    dzs = (dgm, dga, dup, dgp, dfr)
    slabs = [_in_proj_bwd_w(dzs, hn, tm), g_wuq_pad.reshape(N_DEV, Q_RANK // N_DEV, HW), g_wukv.reshape(N_DEV, KV_RANK // N_DEV, HW)]
    grad_x, g_nin, s_win, s_wuq, s_wukv = _in_proj_bwd_x(dzs, x2, dh, row(norm_in), w_in_pad, tm_small, slabs)
    late = [g_nin, g_qn, g_kvn]
    (tot_late,) = _reduce_scatter([], jnp.concatenate([_pack_rows(a) for a in late], axis=0))
    at_nin, at_qn, at_kvn = _packed_at(late)

    grads = dict(norm_in=(tot_late, at_nin), w_in=s_win, q_norm=(tot_late, at_qn), w_uq=s_wuq, kv_norm=(tot_late, at_kvn), w_ukv=s_wukv,
                 pool_w=(tot_early, at_pool_w), pool_scale=(tot_early, at_pool_scale), w_branch_attn=s_wba, w_branch_pool=s_wbp,
                 w_out=s_wout, norm_final=(tot_early, at_nf))
    return tot_early[at_loss, 0], grad_x, grads


MESH_ID = pl.DeviceIdType.MESH
VMEM_SPEC = pl.BlockSpec(memory_space=pltpu.VMEM)
HBM_SPEC = pl.BlockSpec(memory_space=pl.ANY)


def _mesh_pos():
    return lax.axis_index("x"), lax.axis_index("y"), lax.axis_index("c")


BF16_TILE_ROWS = 16


def _half_rows(rows):
    cut = -(-(rows // 2) // BF16_TILE_ROWS) * BF16_TILE_ROWS
    return pl.ds(0, cut), pl.ds(cut, rows - cut)


def _slot(px, py, pc):
    return 4 * px + 2 * py + pc


def _staged_shape(shape):
    return (shape[0], shape[1] * LANES) if len(shape) == 3 else tuple(shape)


def _all_gather_bf16(shards):
    n = len(shards)
    staged = [_staged_shape(s.shape) for s in shards]

    def body(*refs):
        ins, outs = refs[:n], refs[n:2 * n]
        land0, scratch = refs[2 * n], refs[2 * n + 1:]
        wpad_ref = outs[0]
        ag = _AllGather(ins, (land0,) + tuple(outs[1:]), scratch)
        ag.start()
        ag.forward()
        ag.finish()
        wpad_ref[ZKR:GA, :] = jnp.zeros((GA - ZKR, D_MODEL), BF16)
        for k in range(N_DEV):
            for at, dst, rows in _slab_segments(k):
                wpad_ref[dst:dst + rows, :] = land0[k, at:at + rows, :]

    return pl.pallas_call(
        body, name="all_gather_weights",
        in_specs=[VMEM_SPEC] * n, out_specs=[VMEM_SPEC] + [HBM_SPEC] * (n - 1),
        out_shape=[SDS((ZTOT, D_MODEL), BF16)] + [SDS((N_DEV,) + s, BF16) for s in staged[1:]],
        scratch_shapes=[pltpu.VMEM((N_DEV,) + staged[0], BF16)] + _ag_scratch(staged),
        compiler_params=_cp(),
    )(*shards)


def _ag_scratch(shapes):
    n = len(shapes)
    dma = pltpu.SemaphoreType.DMA
    return [pltpu.VMEM(tuple(s), BF16) for s in shapes] + [dma((_AllGather.COPIES * n,)), dma((_AllGather.COPIES * n,)), dma((n,))]


class _AllGather:
    COPIES = 8

    def __init__(self, in_refs, dest_refs, scratch):
        n = self.n = len(in_refs)
        self.ins, self.dests, self.stage = in_refs, dest_refs, scratch[:n]
        self.send_sems, self.recv_sems, self.local_sems = scratch[n:]
        x, y, c = _mesh_pos()
        self.c, self.me, self.sibling = c, (x, y, c), (x, y, 1 - c)
        self.xn, self.yn, self.diag = (1 - x, y), (x, 1 - y), (1 - x, 1 - y)

    def _halves(self, a):
        return _half_rows(self.stage[a].shape[0])

    def _copy(self, a, k, block, to, from_stage=False, rows=None):
        dst = self.dests[a].at[_slot(*block)]
        src = self.stage[a] if from_stage else dst
        if rows is not None:
            src, dst = src.at[rows], dst.at[rows]
        return pltpu.make_async_remote_copy(
            src_ref=src, dst_ref=dst, send_sem=self.send_sems.at[self.COPIES * a + k],
            recv_sem=self.recv_sems.at[self.COPIES * a + k], device_id=to, device_id_type=MESH_ID)

    def _mine(self):
        return [pltpu.make_async_copy(self.stage[a], self.dests[a].at[_slot(*self.me)], self.local_sems.at[a]) for a in range(self.n)]

    def _first(self, a):
        return [self._copy(a, 0, self.me, self.sibling, True), self._copy(a, 1, self.me, (*self.xn, self.c), True),
                self._copy(a, 2, self.me, (*self.yn, self.c), True)]

    def _relays(self, a):
        lo, hi = self._halves(a)
        return [self._copy(a, 3, (*self.xn, self.c), (*self.yn, self.c), rows=lo),
                self._copy(a, 4, (*self.yn, self.c), (*self.xn, self.c), rows=hi)]

    def _passes(self, a):
        return [self._copy(a, 5 + j, (*chip, self.c), self.sibling) for j, chip in enumerate((self.xn, self.yn, self.diag))]

    def start(self):
        for a in range(self.n):
            src, dst = self.ins[a], self.stage[a]
            if src.shape == dst.shape:
                dst[...] = src[...].astype(BF16)
            else:
                if src.shape[2] < LANES:
                    dst[...] = jnp.zeros(dst.shape, BF16)
                for h in range(src.shape[1]):
                    dst[:, h * LANES:h * LANES + src.shape[2]] = src[:, h, :].astype(BF16)
        for cp in self._mine():
            cp.start()
        for a in range(self.n):
            for cp in self._first(a):
                cp.start()

    def forward(self):
        for a in range(self.n):
            relays, passes = self._relays(a), self._passes(a)
            for j, chip in enumerate((self.xn, self.yn)):
                self._copy(a, 1 + j, (*chip, self.c), self.me).wait_recv()
                relays[j].start()
                passes[j].start()

    def finish(self):
        for a in range(self.n):
            lo, hi = self._halves(a)
            self._copy(a, 3, (*self.diag, self.c), self.me, rows=lo).wait_recv()
            self._copy(a, 4, (*self.diag, self.c), self.me, rows=hi).wait_recv()
            self._passes(a)[2].start()
        for a in range(self.n):
            self._copy(a, 0, self.sibling, self.me).wait_recv()
            for j, chip in enumerate((self.xn, self.yn, self.diag)):
                self._copy(a, 5 + j, (*chip, 1 - self.c), self.me).wait_recv()
            for cp in self._first(a) + self._relays(a) + self._passes(a):
                cp.wait_send()
        for cp in self._mine():
            cp.wait()


N_CHIPS = 4


def _reduce_scatter(slabs, packed):
    def body(*refs):
        n = len(slabs)
        rs = _ReduceScatter(refs[:n], refs[n], refs[n + 1:2 * n + 1], refs[2 * n + 1], refs[2 * n + 2:])
        rs.start1()
        rs.finish1_start2()
        rs.relay2()
        rs.finish2()

    shapes = [s.shape[1:] for s in slabs]
    return pl.pallas_call(
        body, name="reduce_scatter_grads",
        in_specs=[HBM_SPEC] * len(slabs) + [VMEM_SPEC], out_specs=[VMEM_SPEC] * (len(slabs) + 1),
        out_shape=[SDS(s, F32) for s in shapes] + [SDS(packed.shape, F32)],
        scratch_shapes=_rs_scratch(shapes, packed.shape), compiler_params=_cp(),
    )(*slabs, packed)


def _rs_scratch(shapes, packed_shape):
    n = len(shapes)
    n1, n2 = N_CHIPS * n + 1, _ReduceScatter.L2_COPIES * n + N_CHIPS - 1
    dma = pltpu.SemaphoreType.DMA
    packed = [] if packed_shape is None else [pltpu.VMEM(packed_shape, F32), pltpu.VMEM((N_CHIPS,) + tuple(packed_shape), F32)]
    return ([pltpu.VMEM((N_CHIPS,) + tuple(s), BF16) for s in shapes] * 2 + [pltpu.VMEM((N_CHIPS - 1,) + tuple(s), BF16) for s in shapes] * 2
            + packed + [dma((max(N_CHIPS * n, 1),)), dma((n1,)), dma((n1,)), dma((n2,)), dma((n2,))])


class _ReduceScatter:
    L2_COPIES = 6

    def __init__(self, slab_refs, packed_ref, out_refs, ptot_ref, scratch):
        n = self.n = len(slab_refs)
        self.slabs, self.packed, self.outs, self.ptot = slab_refs, packed_ref, out_refs, ptot_ref
        self.own1, self.land1, self.send2, self.land2 = (scratch[k * n:(k + 1) * n] for k in range(4))
        rest = scratch[4 * n:]
        if packed_ref is not None:
            self.pland1, self.pland2 = rest[:2]
            rest = rest[2:]
        self.loc_sems, self.send1_sems, self.recv1_sems, self.send2_sems, self.recv2_sems = rest
        self.x, self.y, self.c = _mesh_pos()

    def _chip(self, r):
        return (1 - self.x if r & 2 else self.x, 1 - self.y if r & 1 else self.y)

    @staticmethod
    def _remote(src, dst, send_sem, recv_sem, to):
        return pltpu.make_async_remote_copy(src_ref=src, dst_ref=dst, send_sem=send_sem, recv_sem=recv_sem, device_id=to,
                                            device_id_type=MESH_ID)

    def _copies1(self):
        c, sibling = self.c, (self.x, self.y, 1 - self.c)
        cps = []
        for a in range(self.n):
            for r in range(N_CHIPS):
                k = N_CHIPS * a + r
                cps.append(pltpu.make_async_copy(self.slabs[a].at[_slot(*self._chip(r), c)], self.own1[a].at[r], self.loc_sems.at[k]))
                cps.append(self._remote(self.slabs[a].at[_slot(*self._chip(r), 1 - c)], self.land1[a].at[r],
                                        self.send1_sems.at[k], self.recv1_sems.at[k], sibling))
        if self.packed is not None:
            k = N_CHIPS * self.n
            cps.append(self._remote(self.packed, self.pland1, self.send1_sems.at[k], self.recv1_sems.at[k], sibling))
        return cps

    def _halves(self, a):
        return _half_rows(self.send2[a].shape[1])

    def _copy2(self, a, k):
        lo, hi = self._halves(a)
        xn, yn = (*self._chip(2), self.c), (*self._chip(1), self.c)
        slot, rows, to = [(1, lo, xn), (2, lo, xn), (0, hi, yn), (2, hi, yn), (0, lo, yn), (1, hi, xn)][k]
        return self._remote(self.send2[a].at[slot].at[rows], self.land2[a].at[slot].at[rows],
                            self.send2_sems.at[self.L2_COPIES * a + k], self.recv2_sems.at[self.L2_COPIES * a + k], to)

    def _copies2_packed(self):
        base = self.L2_COPIES * self.n - 1
        return [self._remote(self.pland2.at[0], self.pland2.at[r], self.send2_sems.at[base + r], self.recv2_sems.at[base + r],
                             (*self._chip(r), self.c)) for r in range(1, N_CHIPS)]

    def start1(self):
        for cp in self._copies1():
            cp.start()

    def finish1_start2(self):
        for cp in self._copies1():
            cp.wait()
        for a in range(self.n):
            self.outs[a][...] = self.own1[a][0].astype(F32) + self.land1[a][0].astype(F32)
            for r in range(1, N_CHIPS):
                self.send2[a][r - 1] = (self.own1[a][r].astype(F32) + self.land1[a][r].astype(F32)).astype(BF16)
            for k in (1, 3, 0, 2):
                self._copy2(a, k).start()
        if self.packed is not None:
            self.pland2[0] = self.packed[...] + self.pland1[...]
            for cp in self._copies2_packed():
                cp.start()

    def relay2(self):
        for a in range(self.n):
            lo, hi = self._halves(a)
            s2, l2 = self.send2[a], self.land2[a]
            self._copy2(a, 1).wait_recv()
            s2[0, lo] = (s2[0, lo].astype(F32) + l2[2, lo].astype(F32)).astype(BF16)
            self._copy2(a, 4).start()
            self._copy2(a, 3).wait_recv()
            s2[1, hi] = (s2[1, hi].astype(F32) + l2[2, hi].astype(F32)).astype(BF16)
            self._copy2(a, 5).start()

    def finish2(self):
        for a in range(self.n):
            for k in (0, 2, 4, 5):
                self._copy2(a, k).wait_recv()
            for k in range(self.L2_COPIES):
                self._copy2(a, k).wait_send()
            l2 = self.land2[a]
            self.outs[a][...] = self.outs[a][...] + (l2[0].astype(F32) + l2[1].astype(F32))
        if self.packed is not None:
            for cp in self._copies2_packed():
                cp.wait()
            p2 = self.pland2
            self.ptot[...] = (p2[0] + p2[1]) + (p2[2] + p2[3])


def _adamw(ws, gs, ms, vs, rewrite):
    n = len(ws)
    first_row = [g[1] if isinstance(g, tuple) else None for g in gs]
    gs = [g[0] if isinstance(g, tuple) else g for g in gs]
    regrouped = [k for k in range(n) if first_row[k] is not None or gs[k].shape != ws[k].shape or rewrite[k]]
    assert all(w.shape[-1] % LANES == 0 for w, row in zip(ws, first_row) if row is not None)

    def body(*refs):
        for k in range(n):
            w_ref, g_ref, m_ref, v_ref, d_ref, nm_ref, nv_ref = (refs[j * n + k] for j in range(7))
            row = first_row[k]
            windows = [(..., ...)]
            if row is not None and len(w_ref.shape) == 1:
                windows = [(pl.ds(j * LANES, LANES), row + j) for j in range(w_ref.shape[0] // LANES)]
            elif row is not None:
                windows = [(..., pl.ds(row, w_ref.shape[0]))]
            elif g_ref.shape != w_ref.shape:
                windows = [((slice(None), h), (slice(None), slice(h * LANES, h * LANES + w_ref.shape[2]))) for h in range(w_ref.shape[1])]
            if k in regrouped:
                g_out_ref = refs[7 * n + regrouped.index(k)]
            for at, g_at in windows:
                w, g, m, v = w_ref[at], g_ref[g_at], m_ref[at], v_ref[at]
                m = ADAM_B1 * m + (1.0 - ADAM_B1) * g
                v = ADAM_B2 * v + (1.0 - ADAM_B2) * jnp.square(g)
                m_hat = m / (1.0 - ADAM_B1 ** ADAM_STEP)
                v_hat = v / (1.0 - ADAM_B2 ** ADAM_STEP)
                d_ref[at] = -ADAM_LR * (m_hat / (jnp.sqrt(v_hat) + ADAM_EPS) + ADAM_WD * w)
                nm_ref[at] = m
                nv_ref[at] = v
                if k in regrouped:
                    g_out_ref[at] = g

    shapes = [SDS(w.shape, F32) for w in ws]
    outs = pl.pallas_call(
        body, name="adamw", in_specs=[VMEM_SPEC] * (4 * n), out_specs=[VMEM_SPEC] * (3 * n + len(regrouped)),
        out_shape=shapes * 3 + [shapes[k] for k in regrouped], compiler_params=_cp(),
    )(*ws, *gs, *ms, *vs)
    grads = list(gs)
    for k, g in zip(regrouped, outs[3 * n:]):
        grads[k] = g
    return outs[:n], outs[n:2 * n], outs[2 * n:3 * n], grads


WEIGHTS = ("norm_in", "w_in", "q_norm", "w_uq", "kv_norm", "w_ukv", "pool_w", "pool_scale", "w_branch_attn", "w_branch_pool",
           "w_out", "norm_final")
SUBLANES = 8


def _pack_rows(a):
    a = a.reshape(-1, LANES)
    return jnp.pad(a, ((0, -a.shape[0] % SUBLANES), (0, 0)))


def _packed_at(like):
    out, row = [], 0
    for a in like:
        out.append(row)
        rows = a.size // LANES
        row += rows + (-rows % SUBLANES)
    return out


def kernel(x, norm_in, w_in, q_norm, w_uq, kv_norm, w_ukv, pool_w, pool_scale, w_branch_attn, w_branch_pool, w_out, norm_final, loss_target, m_norm_in, m_w_in, m_q_norm, m_w_uq, m_kv_norm, m_w_ukv, m_pool_w, m_pool_scale, m_w_branch_attn, m_w_branch_pool, m_w_out, m_norm_final, v_norm_in, v_w_in, v_q_norm, v_w_uq, v_kv_norm, v_w_ukv, v_pool_w, v_pool_scale, v_w_branch_attn, v_w_branch_pool, v_w_out, v_norm_final):
    w = dict(norm_in=norm_in, w_in=w_in, q_norm=q_norm, w_uq=w_uq, kv_norm=kv_norm, w_ukv=w_ukv, pool_w=pool_w, pool_scale=pool_scale,
             w_branch_attn=w_branch_attn, w_branch_pool=w_branch_pool, w_out=w_out, norm_final=norm_final)
    m = dict(norm_in=m_norm_in, w_in=m_w_in, q_norm=m_q_norm, w_uq=m_w_uq, kv_norm=m_kv_norm, w_ukv=m_w_ukv, pool_w=m_pool_w,
             pool_scale=m_pool_scale, w_branch_attn=m_w_branch_attn, w_branch_pool=m_w_branch_pool, w_out=m_w_out, norm_final=m_norm_final)
    v = dict(norm_in=v_norm_in, w_in=v_w_in, q_norm=v_q_norm, w_uq=v_w_uq, kv_norm=v_kv_norm, w_ukv=v_w_ukv, pool_w=v_pool_w,
             pool_scale=v_pool_scale, w_branch_attn=v_w_branch_attn, w_branch_pool=v_w_branch_pool, w_out=v_w_out, norm_final=v_norm_final)

    def as2d(name, a):
        if name == "w_in":
            return a.T
        if name in ("w_uq", "w_ukv"):
            return a
        return a.reshape(-1, GROUP) if name == "pool_w" else a

    def unshape(name, a):
        return a.T if name == "w_in" else a.reshape(w[name].shape)

    w_in_pad, w_uq_full, w_ukv_full = _all_gather_bf16([as2d(k, w[k]) for k in ("w_in", "w_uq", "w_ukv")])
    loss, grad_x, g2d = _local_step(
        x.reshape(x.shape[1:]), loss_target.reshape(x.shape[1:]), norm_in, w_in_pad, q_norm,
        w_uq_full.reshape(Q_RANK, HW), kv_norm, w_ukv_full.reshape(KV_RANK, HW),
        pool_w, pool_scale, [w_branch_attn, w_branch_pool, w_out], norm_final)

    deltas, new_m, new_v, grads = _adamw([as2d(k, w[k]) for k in WEIGHTS], [g2d[k] for k in WEIGHTS],
                                         [as2d(k, m[k]) for k in WEIGHTS], [as2d(k, v[k]) for k in WEIGHTS],
                                         [k in ("w_in", "w_branch_attn", "w_branch_pool", "w_out") for k in WEIGHTS])
    shaped = lambda arrs: [unshape(k, a) for k, a in zip(WEIGHTS, arrs)]
    return (loss, grad_x.reshape(x.shape), *shaped(grads), *shaped(deltas), *shaped(new_m), *shaped(new_v))
```

```python
import functools

import jax
import jax.numpy as jnp
import numpy as np
from jax import lax
from jax.experimental import pallas as pl
from jax.experimental.pallas import tpu as pltpu

F32 = jnp.float32
BF16 = jnp.bfloat16
SDS = jax.ShapeDtypeStruct

D_MODEL = 1024
HEADS = 8
NOPE = 64
ROPE = 32
VDIM = 64
Q_RANK = 384
KV_RANK = 256
MLA_W = HEADS * VDIM
POOL_W = 512
POOL_GROUPS = 4
GROUP = POOL_W // POOL_GROUPS
CHUNK = 64
ROPE_THETA = 10000.0
EPS = 1e-6
SCALE = (NOPE + ROPE) ** -0.5
LOG2E = 1.4426950408889634
LN2 = 0.6931471805599453
QK_SCALE_LOG2 = SCALE * LOG2E
IN_TOTAL = 4256
ADAM_LR, ADAM_B1, ADAM_B2, ADAM_EPS, ADAM_WD, ADAM_STEP = 0.001, 0.9, 0.999, 1e-08, 0.01, 10

N_DEV = 8
LANES = 128
HEAD_PAD = LANES
HW = HEADS * HEAD_PAD

ZQ, ZKV, ZKR, GA, UP, GP, GM, ZTOT = 0, 384, 640, 768, 1280, 1792, 2304, 4352
FRONT_W = GA
ZKR_ORIG = 640

VMEM_LIMIT = 62 * 1024 * 1024


def _cp(sem=None, **kw):
    if sem is not None:
        kw["dimension_semantics"] = sem
    return pltpu.CompilerParams(vmem_limit_bytes=VMEM_LIMIT, **kw)


def _mm(a, b):
    return lax.dot_general(a, b, (((1,), (0,)), ((), ())), preferred_element_type=F32)


def _mm_nt(a, b):
    return lax.dot_general(a, b, (((1,), (1,)), ((), ())), preferred_element_type=F32)


def _mm_tn(a, b):
    return lax.dot_general(a, b, (((0,), (0,)), ((), ())), preferred_element_type=F32)


def _row_spec(tm, w):
    return pl.BlockSpec((tm, w), lambda i: (i, 0))


def _full_spec(a):
    nd = len(a.shape)
    return pl.BlockSpec(a.shape, lambda *_: (0,) * nd)


def _rope(v, c, sa, sb, sign):
    n = v.shape[-1]
    reps = n // LANES
    if reps > 1:
        c, sa, sb = (jnp.tile(t, (1, reps)) for t in (c, sa, sb))
    up = pltpu.roll(v, n - ROPE // 2, 1)
    dn = pltpu.roll(v, ROPE // 2, 1)
    return v * c + sign * (up * sa + dn * sb)


def _rope_tables(T):
    half = ROPE // 2
    inv_freq = np.float32(ROPE_THETA) ** (-np.arange(half, dtype=np.float32) / np.float32(half))
    ang = np.arange(T, dtype=np.float32)[:, None] * inv_freq[None, :].astype(np.float32)
    cos, sin = np.cos(ang.astype(np.float64)).astype(np.float32), np.sin(ang.astype(np.float64)).astype(np.float32)
    z16 = np.zeros((T, half), np.float32)
    z32 = np.zeros((T, LANES - NOPE - ROPE), np.float32)
    c = np.concatenate([np.ones((T, NOPE), np.float32), cos, cos, z32], axis=1)
    sa = np.concatenate([np.zeros((T, NOPE), np.float32), -sin, z16, z32], axis=1)
    sb = np.concatenate([np.zeros((T, NOPE), np.float32), z16, sin, z32], axis=1)
    return jnp.asarray(c), jnp.asarray(sa), jnp.asarray(sb)


def _silu_parts(g):
    sg = jax.nn.sigmoid(g)
    return g * sg, sg + g * sg * (1.0 - sg)


def _in_proj(x2, norm_in, w_in_pad, q_norm, wuq_pad, kv_norm, wukv, rc, rsa, rsb, tm, late_shards):
    T = x2.shape[0]
    steps = T // tm
    n = len(late_shards)

    def body(x_ref, nin_ref, win_ref, qn_ref, wuq_ref, kvn_ref, wukv_ref, c_ref, sa_ref, sb_ref, *rest):
        hn_ref, zgm_ref, zga_ref, zup_ref, zgp_ref, zfr_ref, q_ref, k_ref, v_ref, vt_ref = rest[n:n + 10]
        ag = _AllGather(rest[:n], rest[n + 10:2 * n + 10], rest[2 * n + 10:])
        step = pl.program_id(0)
        pl.when(step == 0)(ag.start)
        pl.when(step == min(3, steps - 1))(ag.forward)
        xf = x_ref[...]
        r = lax.rsqrt(jnp.mean(xf * xf, axis=-1, keepdims=True) + EPS)
        hn = (xf * r * nin_ref[...]).astype(BF16)
        hn_ref[...] = hn
        zfr = _mm_nt(hn, win_ref[ZQ:GA, :])
        zfr_ref[...] = zfr
        zq, zkv, zkr = zfr[:, ZQ:ZKV], zfr[:, ZKV:ZKR], zfr[:, ZKR:GA]
        c, sa, sb = c_ref[...], sa_ref[...], sb_ref[...]
        rq = lax.rsqrt(jnp.mean(zq * zq, axis=-1, keepdims=True) + EPS)
        cq = (zq * rq * qn_ref[...].reshape(1, Q_RANK)).astype(BF16)
        rkv = lax.rsqrt(jnp.mean(zkv * zkv, axis=-1, keepdims=True) + EPS)
        ckv = (zkv * rkv * kvn_ref[...]).astype(BF16)
        zga_ref[...] = _mm_nt(hn, win_ref[GA:UP, :])
        zup_ref[...] = _mm_nt(hn, win_ref[UP:GP, :])
        zgp_ref[...] = _mm_nt(hn, win_ref[GP:GM, :])
        q_raw = _mm(cq, wuq_ref[...])
        kv = _mm(ckv, wukv_ref[...])
        zgm_ref[...] = _mm_nt(hn, win_ref[GM:ZTOT, :])
        q = _rope(q_raw, c, sa, sb, 1.0)
        q_ref[...] = (q * QK_SCALE_LOG2).astype(BF16)
        kr = _rope(zkr, c, sa, sb, 1.0)
        lane = lax.broadcasted_iota(jnp.int32, kv.shape, 1) % LANES
        k_ref[...] = jnp.where(lane < NOPE, kv, jnp.tile(kr, (1, HEADS))).astype(BF16)
        v = jnp.where(lane < NOPE, 1.0, kv).astype(BF16)
        v_ref[...] = v
        vt_ref[...] = v.T
        pl.when(step == steps - 1)(ag.finish)

    ins = (x2, norm_in, w_in_pad, q_norm, wuq_pad, kv_norm, wukv, rc, rsa, rsb)
    in_specs = [_row_spec(tm, D_MODEL), _full_spec(norm_in), _full_spec(w_in_pad), _full_spec(q_norm), _full_spec(wuq_pad),
                _full_spec(kv_norm), _full_spec(wukv), _row_spec(tm, LANES), _row_spec(tm, LANES), _row_spec(tm, LANES)]
    widths = [(D_MODEL, BF16), (ZTOT - GM, F32), (UP - GA, F32), (GP - UP, F32), (GM - GP, F32), (FRONT_W, F32),
              (HW, BF16), (HW, BF16), (HW, BF16)]
    return pl.pallas_call(
        body, name="in_proj", grid=(steps,), in_specs=in_specs + [_full_spec(s) for s in late_shards],
        out_specs=[_row_spec(tm, w) for w, _ in widths] + [pl.BlockSpec((HW, tm), lambda i: (0, i))] + [HBM_SPEC] * n,
        out_shape=[SDS((T, w), dt) for w, dt in widths] + [SDS((HW, T), BF16)]
        + [SDS((N_DEV,) + s.shape, BF16) for s in late_shards],
        scratch_shapes=_ag_scratch([s.shape for s in late_shards]), compiler_params=_cp(("arbitrary",)),
    )(*ins, *late_shards)


def _chunk_mask(n_q, n_k, q_off, transposed):
    shape = (n_k, n_q) if transposed else (n_q, n_k)
    q = (lax.broadcasted_iota(jnp.int32, shape, 1 if transposed else 0) + q_off) // CHUNK
    k = lax.broadcasted_iota(jnp.int32, shape, 0 if transposed else 1) // CHUNK
    return k <= q


def _store_pair_rows(ref, k, pair):
    t = pair.T
    ref[k, 0:1, :] = t[0:1, :]
    ref[k, 1:2, :] = t[VDIM:VDIM + 1, :]


def _attn_fwd(q_att, k_att, vt_att, tq, hps):
    T = q_att.shape[0]
    head_lanes = [slice(h * LANES, (h + 1) * LANES) for h in range(hps)]

    def body(q_ref, k_ref, vt_ref, o_ref, lser_ref):
        i = pl.program_id(1)
        mask = _chunk_mask(tq, tq, 0, True)
        lane = lax.broadcasted_iota(jnp.int32, (tq, LANES), 1)
        qs = [q_ref[:, hs] for hs in head_lanes]

        def step(j, carry, masked):
            off = pl.multiple_of(j * tq, tq)
            sts = [_mm_nt(k_ref[pl.ds(off, tq), hs], qh) for qh, hs in zip(qs, head_lanes)]
            if masked:
                sts = [jnp.where(mask, st, -jnp.inf) for st in sts]
            ms = [jnp.maximum(m, jnp.max(st, axis=0, keepdims=True)) for (m, _), st in zip(carry, sts)]
            pts = [jnp.exp2(st - m_new).astype(BF16) for st, m_new in zip(sts, ms)]
            return tuple((m_new, jnp.exp2(m - m_new) * acc + _mm(vt_ref[hs, pl.ds(off, tq)], pt))
                         for (m, acc), m_new, pt, hs in zip(carry, ms, pts, head_lanes))

        init = ((jnp.full((1, tq), -jnp.inf, F32), jnp.zeros((LANES, tq), F32)),) * hps
        res = step(i, lax.fori_loop(0, i, functools.partial(step, masked=False), init), True)
        for pair in range(hps // 2):
            (ma, acca), (mb, accb) = res[2 * pair], res[2 * pair + 1]
            la, lb = acca[:1], accb[:1]
            oa, ob = (acca / la).T, (accb / lb).T
            o_ref[:, pair * LANES:(pair + 1) * LANES] = jnp.where(lane < VDIM, pltpu.roll(oa, VDIM, 1), ob)
            lser_ref[pair, 0:1, :] = ma + jnp.log2(la)
            lser_ref[pair, 1:2, :] = mb + jnp.log2(lb)

    qspec = pl.BlockSpec((tq, hps * LANES), lambda p, i: (i, p))
    kspec = pl.BlockSpec((T, hps * LANES), lambda p, i: (0, p))
    vspec = pl.BlockSpec((hps * LANES, T), lambda p, i: (p, 0))
    ospec = pl.BlockSpec((tq, hps * VDIM), lambda p, i: (i, p))
    return pl.pallas_call(
        body, name="attn_fwd", grid=(HEADS // hps, T // tq), in_specs=[qspec, kspec, vspec],
        out_specs=[ospec, pl.BlockSpec((hps // 2, 2, tq), lambda p, i: (p, 0, i))],
        out_shape=[SDS((T, MLA_W), F32), SDS((HEADS // 2, 2, T), F32)],
        compiler_params=_cp(("parallel", "parallel")),
    )(q_att, k_att, vt_att)


def _pick(g, vals):
    out = vals[-1]
    for k in range(len(vals) - 2, -1, -1):
        out = jnp.where(g == k, vals[k], out)
    return out


def _window_sum(u, g, forward):
    T = u.shape[0]
    row = lax.broadcasted_iota(jnp.int32, u.shape, 0)

    def sh(s, k):
        if forward:
            return jnp.where(row >= k, pltpu.roll(s, k, 0), 0.0)
        return jnp.where(row < T - k, pltpu.roll(s, T - k, 0), 0.0)

    sums, s = [], u
    for k in (1, 2, 4, 8):
        s = s + sh(s, k)
        sums.append(s)
    return _pick(g, sums)


MAX_WINDOW = 16


def _pool_inv_count(shape, g):
    T, n = shape
    row = lax.broadcasted_iota(jnp.int32, (MAX_WINDOW, n), 0)
    head = 1.0 / jnp.minimum(row + 1, lax.shift_left(jnp.int32(2), g)).astype(F32)
    inv_w = _pick(g, [0.5, 0.25, 0.125, 0.0625])
    return jnp.concatenate([head, jnp.broadcast_to(inv_w, (T - MAX_WINDOW, n)).astype(F32)], axis=0)


def _pool_fwd(zup, zgp, pool_w, pool_scale):
    T = zup.shape[0]

    def body(u_ref, g_ref, w_ref, sc_ref, y_ref):
        g = pl.program_id(0)
        u = u_ref[...]
        d = _window_sum(u, g, True) * _pool_inv_count(u.shape, g) - u
        lin = _mm(d.astype(BF16), w_ref[0].astype(BF16))
        silu, _ = _silu_parts(g_ref[...])
        y_ref[...] = (lin * sc_ref[...] * silu).astype(BF16)

    col = pl.BlockSpec((T, GROUP), lambda g: (0, g))
    return pl.pallas_call(
        body, name="pool_fwd", grid=(POOL_GROUPS,),
        in_specs=[col, col, pl.BlockSpec((1, GROUP, GROUP), lambda g: (g, 0, 0)), pl.BlockSpec((1, GROUP), lambda g: (0, g))],
        out_specs=col, out_shape=SDS((T, POOL_W), BF16), compiler_params=_cp(("parallel",)),
    )(zup, zgp, pool_w, pool_scale)


def _pool_bwd(zup, zgp, dyp, pool_w, pool_scale):
    T = zup.shape[0]

    def body(u_ref, g_ref, dy_ref, w_ref, sc_ref, du_ref, dg_ref, gw_ref, gsc_ref):
        g = pl.program_id(0)
        u = u_ref[...]
        inv = _pool_inv_count(u.shape, g)
        d = (_window_sum(u, g, True) * inv - u).astype(BF16)
        wb = w_ref[0].astype(BF16)
        lin = _mm(d, wb)
        sc = sc_ref[...]
        silu, dsilu = _silu_parts(g_ref[...])
        dy = dy_ref[...]
        dg_ref[...] = (dy * lin * sc * dsilu).astype(BF16)
        dpre = dy * silu
        gsc_ref[...] = jnp.sum(dpre * lin, axis=0, keepdims=True)
        dlin = (dpre * sc).astype(BF16)
        gw_ref[0] = _mm_tn(d, dlin)
        dd = _mm_nt(dlin, wb)
        du_ref[...] = (_window_sum(dd * inv, g, False) - dd).astype(BF16)

    col = pl.BlockSpec((T, GROUP), lambda g: (0, g))
    wspec = pl.BlockSpec((1, GROUP, GROUP), lambda g: (g, 0, 0))
    vspec = pl.BlockSpec((1, GROUP), lambda g: (0, g))
    return pl.pallas_call(
        body, name="pool_bwd", grid=(POOL_GROUPS,), in_specs=[col, col, col, wspec, vspec], out_specs=[col, col, wspec, vspec],
        out_shape=[SDS((T, POOL_W), BF16), SDS((T, POOL_W), BF16), SDS((POOL_GROUPS, GROUP, GROUP), F32), SDS((1, POOL_W), F32)],
        compiler_params=_cp(("parallel",)),
    )(zup, zgp, dyp, pool_w, pool_scale)


def _tail(x2, tgt, o, zga, ypool, zgm, wba, wbp, wout, norm_final, tm):
    T = x2.shape[0]
    steps = T // tm
    cols = D_MODEL // N_DEV

    def body(x_ref, tgt_ref, o_ref, zga_ref, yp_ref, zgm_ref, wba_ref, wbp_ref, wout_ref, nf_ref,
             loss_ref, dh_ref, dgm_ref, doop_ref, dga_ref, dcapr_ref, dyp_ref, swout_ref, swba_ref, swbp_ref, gnf_ref,
             gwout_ref, gwba_ref, gwbp_ref):
        @pl.when(pl.program_id(0) == 0)
        def _():
            for ref in (loss_ref, gwout_ref, gwba_ref, gwbp_ref, gnf_ref):
                ref[...] = jnp.zeros_like(ref)

        o_v = o_ref[...]
        silu, dsilu = _silu_parts(zga_ref[...])
        ya = (o_v * silu).astype(BF16)
        yp = yp_ref[...]
        wba_v = jnp.concatenate([wba_ref[k] for k in range(N_DEV)], axis=1)
        wbp_v = jnp.concatenate([wbp_ref[k] for k in range(N_DEV)], axis=1)
        wout_v = wout_ref[...]
        a = _mm(ya, wba_v)
        p = _mm(yp, wbp_v)
        gate = jax.nn.sigmoid(zgm_ref[...])
        ga, gp = gate[:, :D_MODEL], gate[:, D_MODEL:]
        mg = (ga * a + gp * p).astype(BF16)
        h = x_ref[...] + _mm(mg, wout_v)
        r = lax.rsqrt(jnp.mean(h * h, axis=-1, keepdims=True) + EPS)
        gf = nf_ref[...]
        hr = h * r
        e = hr * gf - tgt_ref[...]
        loss_ref[...] += (0.5 / D_MODEL) * jnp.sum(e * e)
        dy = e * (1.0 / D_MODEL)
        gnf_ref[...] += jnp.sum(dy * hr, axis=0, keepdims=True)
        u = dy * gf
        dh = r * (u - hr * jnp.mean(u * hr, axis=-1, keepdims=True))
        dh_ref[...] = dh
        dhb = dh.astype(BF16)
        dmg = _mm_nt(dhb, wout_v)
        dab = (dmg * ga).astype(BF16)
        dpb = (dmg * gp).astype(BF16)
        dya = _mm_nt(dab, wba_v)
        dyp_ref[...] = _mm_nt(dpb, wbp_v)
        gwout_ref[...] += _mm_tn(mg, dhb)
        gwba_ref[...] += _mm_tn(ya, dab)
        gwbp_ref[...] += _mm_tn(yp, dpb)
        dgm_ref[:, :D_MODEL] = (dmg * a * ga * (1.0 - ga)).astype(BF16)
        dgm_ref[:, D_MODEL:] = (dmg * p * gp * (1.0 - gp)).astype(BF16)
        do = dya * silu
        dga_ref[...] = (dya * o_v * dsilu).astype(BF16)
        prod = do * o_v
        lo = lax.broadcasted_iota(jnp.int32, (tm, LANES), 1) < VDIM
        for pair in range(HEADS // 2):
            ls = slice(pair * LANES, (pair + 1) * LANES)
            do_p, prod_p = do[:, ls], prod[:, ls]
            dcap_a = jnp.sum(jnp.where(lo, prod_p, 0.0), axis=-1, keepdims=True)
            dcap_b = jnp.sum(jnp.where(lo, 0.0, prod_p), axis=-1, keepdims=True)
            _store_pair_rows(dcapr_ref, pair, jnp.where(lo, dcap_a, dcap_b))
            doop_ref[:, 2 * pair * LANES:(2 * pair + 1) * LANES] = jnp.where(lo, 0.0, pltpu.roll(do_p, VDIM, 1)).astype(BF16)
            doop_ref[:, (2 * pair + 1) * LANES:(2 * pair + 2) * LANES] = jnp.where(lo, 0.0, do_p).astype(BF16)

        @pl.when(pl.program_id(0) == steps - 1)
        def _():
            for k in range(N_DEV):
                swout_ref[k] = gwout_ref[k * cols:(k + 1) * cols, :].astype(BF16)
                swba_ref[k] = gwba_ref[:, k * cols:(k + 1) * cols].astype(BF16)
                swbp_ref[k] = gwbp_ref[:, k * cols:(k + 1) * cols].astype(BF16)

    ins = (x2, tgt, o, zga, ypool, zgm, wba, wbp, wout, norm_final)
    in_specs = [_row_spec(tm, D_MODEL), _row_spec(tm, D_MODEL), _row_spec(tm, MLA_W), _row_spec(tm, MLA_W), _row_spec(tm, POOL_W),
                _row_spec(tm, 2 * D_MODEL), _full_spec(wba), _full_spec(wbp), _full_spec(wout), _full_spec(norm_final)]
    outs = [SDS((8, LANES), F32), SDS((T, D_MODEL), F32), SDS((T, 2 * D_MODEL), BF16), SDS((T, HW), BF16), SDS((T, MLA_W), BF16),
            SDS((HEADS // 2, 2, T), F32), SDS((T, POOL_W), F32),
            SDS((N_DEV, cols, D_MODEL), BF16), SDS((N_DEV, MLA_W, cols), BF16), SDS((N_DEV, POOL_W, cols), BF16), SDS((1, D_MODEL), F32)]
    out_specs = [_full_spec(outs[0]), _row_spec(tm, D_MODEL), _row_spec(tm, 2 * D_MODEL), _row_spec(tm, HW), _row_spec(tm, MLA_W),
                 pl.BlockSpec((HEADS // 2, 2, tm), lambda i: (0, 0, i)), _row_spec(tm, POOL_W),
                 _full_spec(outs[7]), _full_spec(outs[8]), _full_spec(outs[9]), _full_spec(outs[10])]
    return pl.pallas_call(
        body, name="tail", grid=(steps,), in_specs=in_specs, out_specs=out_specs, out_shape=outs,
        scratch_shapes=[pltpu.VMEM((D_MODEL, D_MODEL), F32), pltpu.VMEM((MLA_W, D_MODEL), F32), pltpu.VMEM((POOL_W, D_MODEL), F32)],
        compiler_params=_cp(("arbitrary",)),
    )(*ins)


def _attn_bwd(q_att, k_att, v_att, doop, lse_rows, dcap_rows, tq, hps, slabs, packed):
    T = q_att.shape[0]
    nq = T // tq
    n = len(slabs)
    groups = HEADS // hps
    head_lanes = [slice(h * LANES, (h + 1) * LANES) for h in range(hps)]

    def body(q_ref, k_ref, v_ref, doop_ref, lse_ref, dcap_ref, *rest):
        slab_refs, packed_ref = rest[:n], rest[n]
        dq_ref, dkv_ref, dkr_ref = rest[n + 1:n + 4]
        sum_refs, ptot_ref = rest[n + 4:2 * n + 4], rest[2 * n + 4]
        dq_acc = rest[2 * n + 5]
        rs = _ReduceScatter(slab_refs, packed_ref, sum_refs, ptot_ref, rest[2 * n + 6:])
        group, j = pl.program_id(0), pl.program_id(1)
        tick, last = group * nq + j, groups * nq - 1
        pl.when(tick == 0)(rs.start1)
        pl.when(tick == min(1, last))(rs.finish1_start2)
        pl.when(tick == min(3, last))(rs.relay2)
        mask = _chunk_mask(tq, tq, 0, True)
        lane = lax.broadcasted_iota(jnp.int32, (tq, LANES), 1)
        ks = [k_ref[:, hs] for hs in head_lanes]
        vs = [v_ref[:, hs] for hs in head_lanes]
        kts = [kh.T for kh in ks]

        @pl.when(j == 0)
        def _():
            dq_acc[...] = jnp.zeros_like(dq_acc)

        def step(i, carry, masked):
            rows = pl.ds(pl.multiple_of(i * tq, tq), tq)
            heads = range(hps)
            stat = lambda h: (h // 2, slice(h % 2, h % 2 + 1), rows)
            qhs = [q_ref[rows, hs] for hs in head_lanes]
            doops = [doop_ref[rows, hs] for hs in head_lanes]
            sts = [_mm_nt(ks[h], qhs[h]) for h in heads]
            dpts = [_mm_nt(vs[h], doops[h]) for h in heads]
            pts = [jnp.exp2(sts[h] - lse_ref[stat(h)]) for h in heads]
            if masked:
                pts = [jnp.where(mask, pt, 0.0) for pt in pts]
            dsts = [(pts[h] * (dpts[h] - dcap_ref[stat(h)])).astype(BF16) for h in heads]
            dvs = [_mm(pts[h].astype(BF16), doops[h]) for h in heads]
            dks = [_mm(dsts[h], qhs[h]) for h in heads]
            for h, hs in enumerate(head_lanes):
                dq_acc[hs, rows] += _mm(kts[h], dsts[h])
            return tuple((dk + dks[h], dv + dvs[h]) for h, (dk, dv) in enumerate(carry))

        zero = jnp.zeros((tq, LANES), F32)
        carry = step(j, ((zero, zero),) * hps, True)
        res = lax.fori_loop(j + 1, nq, functools.partial(step, masked=False), carry)
        dkr = None
        for (dk, dv), hs in zip(res, head_lanes):
            dk = dk * LN2
            dkv_ref[:, hs] = jnp.where(lane < NOPE, dk, dv).astype(BF16)
            dkr = dk if dkr is None else dkr + dk
        dkr_ref[0] = jnp.where((lane >= NOPE) & (lane < NOPE + ROPE), dkr, 0.0)

        @pl.when(j == nq - 1)
        def _():
            dq_ref[...] = (dq_acc[...] * SCALE).T.astype(BF16)

        pl.when(tick == last)(rs.finish2)

    kspec = pl.BlockSpec((tq, hps * LANES), lambda p, j: (j, p))
    qspec = pl.BlockSpec((T, hps * LANES), lambda p, j: (0, p))
    rspec = pl.BlockSpec((hps // 2, 2, T), lambda p, j: (p, 0, 0))
    sums = [SDS(s.shape[1:], F32) for s in slabs] + [SDS(packed.shape, F32)]
    return pl.pallas_call(
        body, name="attn_bwd", grid=(groups, nq),
        in_specs=[qspec, kspec, kspec, qspec, rspec, rspec] + [HBM_SPEC] * n + [_full_spec(packed)],
        out_specs=[qspec, kspec, pl.BlockSpec((1, tq, LANES), lambda p, j: (p, j, 0))] + [_full_spec(s) for s in sums],
        out_shape=[SDS((T, HW), BF16), SDS((T, HW), BF16), SDS((groups, T, LANES), F32)] + sums,
        scratch_shapes=[pltpu.VMEM((hps * LANES, T), F32)] + _rs_scratch([s.shape for s in sums[:-1]], packed.shape),
        compiler_params=_cp(("arbitrary", "arbitrary")),
    )(q_att, k_att, v_att, doop, lse_rows, dcap_rows, *slabs, packed)


def _rms_bwd(z, gain, dout):
    r = lax.rsqrt(jnp.mean(z * z, axis=-1, keepdims=True) + EPS)
    zr = z * r
    u = dout * gain
    return r * (u - zr * jnp.mean(u * zr, axis=-1, keepdims=True)), jnp.sum(dout * zr, axis=0, keepdims=True)


def _mla_bwd(dq_att, dkv_nat, dkr4, zfr, q_norm, wuq_pad, kv_norm, wukv, rc, rsa, rsb, tm):
    T = dq_att.shape[0]

    def body(dq_ref, dkv_ref, dkr_ref, zfr_ref, qn_ref, wuq_ref, kvn_ref, wukv_ref, c_ref, sa_ref, sb_ref,
             dfr_ref, gwuq_out, gwukv_out, gqn_ref, gkvn_ref, gwuq_ref, gwukv_ref):
        @pl.when(pl.program_id(0) == 0)
        def _():
            for ref in (gwuq_ref, gwukv_ref, gqn_ref, gkvn_ref):
                ref[...] = jnp.zeros_like(ref)

        c, sa, sb = c_ref[...], sa_ref[...], sb_ref[...]
        zq, zkv = zfr_ref[:, :Q_RANK], zfr_ref[:, Q_RANK:Q_RANK + KV_RANK]
        qn, kvn = qn_ref[...].reshape(1, Q_RANK), kvn_ref[...]
        dkv = dkv_ref[...]
        dckv = _mm_nt(dkv, wukv_ref[...])
        ckv = (zkv * lax.rsqrt(jnp.mean(zkv * zkv, axis=-1, keepdims=True) + EPS) * kvn).astype(BF16)
        gwukv_ref[...] += _mm_tn(ckv, dkv)
        cq = (zq * lax.rsqrt(jnp.mean(zq * zq, axis=-1, keepdims=True) + EPS) * qn).astype(BF16)
        dq = _rope(dq_ref[...].astype(F32), c, sa, sb, -1.0).astype(BF16)
        dzkv, gkvn = _rms_bwd(zkv, kvn, dckv)
        gkvn_ref[...] += gkvn
        gwuq_ref[...] += _mm_tn(cq, dq)
        dzq, gqn = _rms_bwd(zq, qn, _mm_nt(dq, wuq_ref[...]))
        gqn_ref[...] += gqn
        dkr = functools.reduce(lambda a, b: a + b, [dkr_ref[g] for g in range(dkr4.shape[0])])
        dfr_ref[:, :Q_RANK] = dzq.astype(BF16)
        dfr_ref[:, Q_RANK:Q_RANK + KV_RANK] = dzkv.astype(BF16)
        dfr_ref[:, Q_RANK + KV_RANK:] = _rope(dkr, c, sa, sb, -1.0).astype(BF16)

        @pl.when(pl.program_id(0) == pl.num_programs(0) - 1)
        def _():
            gwuq_out[...] = gwuq_ref[...].astype(BF16)
            gwukv_out[...] = gwukv_ref[...].astype(BF16)

    ins = (dq_att, dkv_nat, dkr4, zfr, q_norm, wuq_pad, kv_norm, wukv, rc, rsa, rsb)
    in_specs = [_row_spec(tm, HW), _row_spec(tm, HW), pl.BlockSpec((dkr4.shape[0], tm, LANES), lambda i: (0, i, 0)), _row_spec(tm, FRONT_W),
                _full_spec(q_norm), _full_spec(wuq_pad), _full_spec(kv_norm), _full_spec(wukv),
                _row_spec(tm, LANES), _row_spec(tm, LANES), _row_spec(tm, LANES)]
    outs = [SDS((T, FRONT_W), BF16), SDS((Q_RANK, HW), BF16), SDS((KV_RANK, HW), BF16), SDS((1, Q_RANK), F32), SDS((1, KV_RANK), F32)]
    out_specs = [_row_spec(tm, FRONT_W)] + [_full_spec(s) for s in outs[1:]]
    return pl.pallas_call(
        body, name="mla_bwd", grid=(T // tm,), in_specs=in_specs, out_specs=out_specs, out_shape=outs,
        scratch_shapes=[pltpu.VMEM((Q_RANK, HW), F32), pltpu.VMEM((KV_RANK, HW), F32)],
        compiler_params=_cp(("arbitrary",)),
    )(*ins)


_DZ_COLS = ((GM, ZTOT), (GA, UP), (UP, GP), (GP, GM), (ZQ, GA))


def _in_proj_bwd_x(dzs, x2, dh, norm_in, w_in_pad, tm, slabs):
    T = x2.shape[0]
    steps = T // tm
    n = len(slabs)

    def body(d0, d1, d2, d3, d4, x_ref, dh_ref, nin_ref, win_ref, *rest):
        slab_refs, (gx_ref, gnin_ref), sum_refs = rest[:n], rest[n:n + 2], rest[n + 2:2 * n + 2]
        rs = _ReduceScatter(slab_refs, None, sum_refs, None, rest[2 * n + 2:])
        step = pl.program_id(0)

        @pl.when(step == 0)
        def _():
            gnin_ref[...] = jnp.zeros_like(gnin_ref)
            rs.start1()

        pl.when(step == min(2, steps - 1))(rs.finish1_start2)
        pl.when(step == min(steps * 11 // 16, steps - 1))(rs.relay2)
        dhn = None
        for ref, (lo, hi) in zip((d0, d1, d2, d3, d4), _DZ_COLS):
            t = _mm(ref[...], win_ref[lo:hi, :])
            dhn = t if dhn is None else dhn + t
        dx, gnin = _rms_bwd(x_ref[...], nin_ref[...], dhn)
        gnin_ref[...] += gnin
        gx_ref[...] = dx + dh_ref[...]
        pl.when(step == steps - 1)(rs.finish2)

    in_specs = [_row_spec(tm, hi - lo) for lo, hi in _DZ_COLS] + [_row_spec(tm, D_MODEL), _row_spec(tm, D_MODEL),
                                                                  _full_spec(norm_in), _full_spec(w_in_pad)] + [HBM_SPEC] * n
    sums = [SDS(s.shape[1:], F32) for s in slabs]
    outs = [SDS((T, D_MODEL), F32), SDS((1, D_MODEL), F32)] + sums
    return pl.pallas_call(
        body, name="in_proj_bwd_x", grid=(steps,), in_specs=in_specs,
        out_specs=[_row_spec(tm, D_MODEL), _full_spec(outs[1])] + [_full_spec(s) for s in sums],
        out_shape=outs, scratch_shapes=_rs_scratch([s.shape for s in sums], None), compiler_params=_cp(("arbitrary",)),
    )(*dzs, x2, dh, norm_in, w_in_pad, *slabs)


SLAB_ROWS = IN_TOTAL // N_DEV


def _slab_segments(k):
    cuts = [(0, ZKR_ORIG, 0), (ZKR_ORIG, ZKR_ORIG + ROPE, NOPE), (ZKR_ORIG + ROPE, IN_TOTAL, LANES - ROPE)]
    lo, hi = k * SLAB_ROWS, (k + 1) * SLAB_ROWS
    return [(max(lo, a) - lo, max(lo, a) + shift, min(hi, b) - max(lo, a)) for a, b, shift in cuts if min(hi, b) > max(lo, a)]


def _in_proj_bwd_w(dzs, hn, tm):
    T = hn.shape[0]
    steps = T // tm

    def body(d0, d1, d2, d3, d4, hn_ref, slab_ref, acc_ref):
        @pl.when(pl.program_id(0) == 0)
        def _():
            acc_ref[...] = jnp.zeros_like(acc_ref)

        hn_v = hn_ref[...]
        for ref, (lo, hi) in zip((d0, d1, d2, d3, d4), _DZ_COLS):
            acc_ref[lo:hi, :] += _mm_tn(ref[...], hn_v)

        @pl.when(pl.program_id(0) == steps - 1)
        def _():
            for k in range(N_DEV):
                for at, src, rows in _slab_segments(k):
                    slab_ref[k, at:at + rows, :] = acc_ref[src:src + rows, :].astype(BF16)

    in_specs = [_row_spec(tm, hi - lo) for lo, hi in _DZ_COLS] + [_row_spec(tm, D_MODEL)]
    out = SDS((N_DEV, SLAB_ROWS, D_MODEL), BF16)
    return pl.pallas_call(
        body, name="in_proj_bwd_w", grid=(steps,), in_specs=in_specs, out_specs=_full_spec(out), out_shape=out,
        scratch_shapes=[pltpu.VMEM((ZTOT, D_MODEL), F32)], compiler_params=_cp(("arbitrary",)),
    )(*dzs, hn)


def _local_step(x2, tgt, norm_in, w_in_pad, q_norm, wuq_pad, kv_norm, wukv, pool_w, pool_scale, late_shards, norm_final):
    T = x2.shape[0]
    tm, tm_small, tq = min(512, T), min(256, T), min(512, T)
    heads_fwd, heads_bwd = 4, 4
    row = lambda v: v.reshape(1, -1)
    rc, rsa, rsb = _rope_tables(T)

    hn, zgm, zga, zup, zgp, zfr, q_att, k_att, v_att, vt_att, w_ba, w_bp, w_out = _in_proj(
        x2, row(norm_in), w_in_pad, q_norm, wuq_pad, row(kv_norm), wukv, rc, rsa, rsb, tm, late_shards)
    w_out = w_out.reshape(D_MODEL, D_MODEL)
    o, lse_rows = _attn_fwd(q_att, k_att, vt_att, tq, heads_fwd)
    ypool = _pool_fwd(zup, zgp, pool_w, row(pool_scale))
    loss8, dh, dgm, doop, dga, dcap_rows, dyp, *slabs, g_nf = _tail(
        x2, tgt, o, zga, ypool, zgm, w_ba, w_bp, w_out, row(norm_final), tm_small)
    dup, dgp, g_pool_w, g_pool_scale = _pool_bwd(zup, zgp, dyp, pool_w, row(pool_scale))

    early = [g_pool_w, g_pool_scale, g_nf, loss8[0]]
    packed = jnp.concatenate([_pack_rows(a) for a in early], axis=0)
    dq_att, dkv_nat, dkr4, s_wout, s_wba, s_wbp, tot_early = _attn_bwd(
        q_att, k_att, v_att, doop, lse_rows, dcap_rows, tq, heads_bwd, slabs, packed)
    at_pool_w, at_pool_scale, at_nf, at_loss = _packed_at(early)

    dfr, g_wuq_pad, g_wukv, g_qn, g_kvn = _mla_bwd(
        dq_att, dkv_nat, dkr4, zfr, q_norm, wuq_pad, row(kv_norm), wukv, rc, rsa, rsb, tm)
    dzs = (dgm, dga, dup, dgp, dfr)
    slabs = [_in_proj_bwd_w(dzs, hn, tm), g_wuq_pad.reshape(N_DEV, Q_RANK // N_DEV, HW), g_wukv.reshape(N_DEV, KV_RANK // N_DEV, HW)]
    grad_x, g_nin, s_win, s_wuq, s_wukv = _in_proj_bwd_x(dzs, x2, dh, row(norm_in), w_in_pad, tm_small, slabs)
    late = [g_nin, g_qn, g_kvn]
    tot_late = _all_reduce_rows(late)
    at_nin, at_qn, at_kvn = _packed_at(late)

    grads = dict(norm_in=(tot_late, at_nin), w_in=s_win, q_norm=(tot_late, at_qn), w_uq=s_wuq, kv_norm=(tot_late, at_kvn), w_ukv=s_wukv,
                 pool_w=(tot_early, at_pool_w), pool_scale=(tot_early, at_pool_scale), w_branch_attn=s_wba, w_branch_pool=s_wbp,
                 w_out=s_wout, norm_final=(tot_early, at_nf))
    return tot_early[at_loss, 0], grad_x, grads


MESH_ID = pl.DeviceIdType.MESH
VMEM_SPEC = pl.BlockSpec(memory_space=pltpu.VMEM)
HBM_SPEC = pl.BlockSpec(memory_space=pl.ANY)


def _mesh_pos():
    return lax.axis_index("x"), lax.axis_index("y"), lax.axis_index("c")


BF16_TILE_ROWS = 16


def _half_rows(rows):
    cut = -(-(rows // 2) // BF16_TILE_ROWS) * BF16_TILE_ROWS
    return pl.ds(0, cut), pl.ds(cut, rows - cut)


def _slot(px, py, pc):
    return 4 * px + 2 * py + pc


def _staged_shape(shape):
    return (shape[0], shape[1] * LANES) if len(shape) == 3 else tuple(shape)


def _all_gather_bf16(shards):
    n = len(shards)
    staged = [_staged_shape(s.shape) for s in shards]

    def body(*refs):
        ins, outs = refs[:n], refs[n:2 * n]
        land0, scratch = refs[2 * n], refs[2 * n + 1:]
        wpad_ref = outs[0]
        ag = _AllGather(ins, (land0,) + tuple(outs[1:]), scratch)
        ag.start()
        ag.forward()
        ag.finish()
        wpad_ref[ZKR:GA, :] = jnp.zeros((GA - ZKR, D_MODEL), BF16)
        for k in range(N_DEV):
            for at, dst, rows in _slab_segments(k):
                wpad_ref[dst:dst + rows, :] = land0[k, at:at + rows, :]

    return pl.pallas_call(
        body, name="all_gather_weights",
        in_specs=[VMEM_SPEC] * n, out_specs=[VMEM_SPEC] + [HBM_SPEC] * (n - 1),
        out_shape=[SDS((ZTOT, D_MODEL), BF16)] + [SDS((N_DEV,) + s, BF16) for s in staged[1:]],
        scratch_shapes=[pltpu.VMEM((N_DEV,) + staged[0], BF16)] + _ag_scratch(staged),
        compiler_params=_cp(),
    )(*shards)


def _ag_scratch(shapes):
    n = len(shapes)
    dma = pltpu.SemaphoreType.DMA
    return [pltpu.VMEM(tuple(s), BF16) for s in shapes] + [dma((_AllGather.COPIES * n,)), dma((_AllGather.COPIES * n,)), dma((n,))]


class _AllGather:
    COPIES = 8

    def __init__(self, in_refs, dest_refs, scratch):
        n = self.n = len(in_refs)
        self.ins, self.dests, self.stage = in_refs, dest_refs, scratch[:n]
        self.send_sems, self.recv_sems, self.local_sems = scratch[n:]
        x, y, c = _mesh_pos()
        self.c, self.me, self.sibling = c, (x, y, c), (x, y, 1 - c)
        self.xn, self.yn, self.diag = (1 - x, y), (x, 1 - y), (1 - x, 1 - y)

    def _halves(self, a):
        return _half_rows(self.stage[a].shape[0])

    def _copy(self, a, k, block, to, from_stage=False, rows=None):
        dst = self.dests[a].at[_slot(*block)]
        src = self.stage[a] if from_stage else dst
        if rows is not None:
            src, dst = src.at[rows], dst.at[rows]
        return pltpu.make_async_remote_copy(
            src_ref=src, dst_ref=dst, send_sem=self.send_sems.at[self.COPIES * a + k],
            recv_sem=self.recv_sems.at[self.COPIES * a + k], device_id=to, device_id_type=MESH_ID)

    def _mine(self):
        return [pltpu.make_async_copy(self.stage[a], self.dests[a].at[_slot(*self.me)], self.local_sems.at[a]) for a in range(self.n)]

    def _first(self, a):
        return [self._copy(a, 0, self.me, self.sibling, True), self._copy(a, 1, self.me, (*self.xn, self.c), True),
                self._copy(a, 2, self.me, (*self.yn, self.c), True)]

    def _relays(self, a):
        lo, hi = self._halves(a)
        return [self._copy(a, 3, (*self.xn, self.c), (*self.yn, self.c), rows=lo),
                self._copy(a, 4, (*self.yn, self.c), (*self.xn, self.c), rows=hi)]

    def _passes(self, a):
        return [self._copy(a, 5 + j, (*chip, self.c), self.sibling) for j, chip in enumerate((self.xn, self.yn, self.diag))]

    def start(self):
        for a in range(self.n):
            src, dst = self.ins[a], self.stage[a]
            if src.shape == dst.shape:
                dst[...] = src[...].astype(BF16)
            else:
                if src.shape[2] < LANES:
                    dst[...] = jnp.zeros(dst.shape, BF16)
                for h in range(src.shape[1]):
                    dst[:, h * LANES:h * LANES + src.shape[2]] = src[:, h, :].astype(BF16)
        for cp in self._mine():
            cp.start()
        for a in range(self.n):
            for cp in self._first(a):
                cp.start()

    def forward(self):
        for a in range(self.n):
            relays, passes = self._relays(a), self._passes(a)
            for j, chip in enumerate((self.xn, self.yn)):
                self._copy(a, 1 + j, (*chip, self.c), self.me).wait_recv()
                relays[j].start()
                passes[j].start()

    def finish(self):
        for a in range(self.n):
            lo, hi = self._halves(a)
            self._copy(a, 3, (*self.diag, self.c), self.me, rows=lo).wait_recv()
            self._copy(a, 4, (*self.diag, self.c), self.me, rows=hi).wait_recv()
            self._passes(a)[2].start()
        for a in range(self.n):
            self._copy(a, 0, self.sibling, self.me).wait_recv()
            for j, chip in enumerate((self.xn, self.yn, self.diag)):
                self._copy(a, 5 + j, (*chip, 1 - self.c), self.me).wait_recv()
            for cp in self._first(a) + self._relays(a) + self._passes(a):
                cp.wait_send()
        for cp in self._mine():
            cp.wait()


N_CHIPS = 4


def _all_reduce_rows(parts):
    m = len(parts)
    firsts = _packed_at(parts)
    rows = firsts[-1] + -(-(parts[-1].size // LANES) // SUBLANES) * SUBLANES

    def body(*refs):
        part_refs, tot_ref, packed_ref = refs[:m], refs[m], refs[m + 1]
        packed_ref[...] = jnp.zeros_like(packed_ref)
        for ref, first in zip(part_refs, firsts):
            for j in range(ref.shape[1] // LANES):
                packed_ref[first + j:first + j + 1, :] = ref[:, j * LANES:(j + 1) * LANES]
        rs = _ReduceScatter((), packed_ref, (), tot_ref, refs[m + 2:])
        rs.start1()
        rs.finish1_start2()
        rs.relay2()
        rs.finish2()

    return pl.pallas_call(
        body, name="all_reduce_rows", in_specs=[VMEM_SPEC] * m, out_specs=VMEM_SPEC, out_shape=SDS((rows, LANES), F32),
        scratch_shapes=[pltpu.VMEM((rows, LANES), F32)] + _rs_scratch([], (rows, LANES)), compiler_params=_cp(),
    )(*parts)


def _rs_scratch(shapes, packed_shape):
    n = len(shapes)
    n1, n2 = N_CHIPS * n + 1, _ReduceScatter.L2_COPIES * n + N_CHIPS - 1
    dma = pltpu.SemaphoreType.DMA
    packed = [] if packed_shape is None else [pltpu.VMEM(packed_shape, F32), pltpu.VMEM((N_CHIPS,) + tuple(packed_shape), F32)]
    return ([pltpu.VMEM((N_CHIPS,) + tuple(s), BF16) for s in shapes] * 2 + [pltpu.VMEM((N_CHIPS - 1,) + tuple(s), BF16) for s in shapes] * 2
            + packed + [dma((max(N_CHIPS * n, 1),)), dma((n1,)), dma((n1,)), dma((n2,)), dma((n2,))])


class _ReduceScatter:
    L2_COPIES = 6

    def __init__(self, slab_refs, packed_ref, out_refs, ptot_ref, scratch):
        n = self.n = len(slab_refs)
        self.slabs, self.packed, self.outs, self.ptot = slab_refs, packed_ref, out_refs, ptot_ref
        self.own1, self.land1, self.send2, self.land2 = (scratch[k * n:(k + 1) * n] for k in range(4))
        rest = scratch[4 * n:]
        if packed_ref is not None:
            self.pland1, self.pland2 = rest[:2]
            rest = rest[2:]
        self.loc_sems, self.send1_sems, self.recv1_sems, self.send2_sems, self.recv2_sems = rest
        self.x, self.y, self.c = _mesh_pos()

    def _chip(self, r):
        return (1 - self.x if r & 2 else self.x, 1 - self.y if r & 1 else self.y)

    @staticmethod
    def _remote(src, dst, send_sem, recv_sem, to):
        return pltpu.make_async_remote_copy(src_ref=src, dst_ref=dst, send_sem=send_sem, recv_sem=recv_sem, device_id=to,
                                            device_id_type=MESH_ID)

    def _copies1(self):
        c, sibling = self.c, (self.x, self.y, 1 - self.c)
        cps = []
        for a in range(self.n):
            for r in range(N_CHIPS):
                k = N_CHIPS * a + r
                cps.append(pltpu.make_async_copy(self.slabs[a].at[_slot(*self._chip(r), c)], self.own1[a].at[r], self.loc_sems.at[k]))
                cps.append(self._remote(self.slabs[a].at[_slot(*self._chip(r), 1 - c)], self.land1[a].at[r],
                                        self.send1_sems.at[k], self.recv1_sems.at[k], sibling))
        if self.packed is not None:
            k = N_CHIPS * self.n
            cps.append(self._remote(self.packed, self.pland1, self.send1_sems.at[k], self.recv1_sems.at[k], sibling))
        return cps

    def _halves(self, a):
        return _half_rows(self.send2[a].shape[1])

    def _copy2(self, a, k):
        lo, hi = self._halves(a)
        xn, yn = (*self._chip(2), self.c), (*self._chip(1), self.c)
        slot, rows, to = [(1, lo, xn), (2, lo, xn), (0, hi, yn), (2, hi, yn), (0, lo, yn), (1, hi, xn)][k]
        return self._remote(self.send2[a].at[slot].at[rows], self.land2[a].at[slot].at[rows],
                            self.send2_sems.at[self.L2_COPIES * a + k], self.recv2_sems.at[self.L2_COPIES * a + k], to)

    def _copies2_packed(self):
        base = self.L2_COPIES * self.n - 1
        return [self._remote(self.pland2.at[0], self.pland2.at[r], self.send2_sems.at[base + r], self.recv2_sems.at[base + r],
                             (*self._chip(r), self.c)) for r in range(1, N_CHIPS)]

    def start1(self):
        for cp in self._copies1():
            cp.start()

    def finish1_start2(self):
        for cp in self._copies1():
            cp.wait()
        for a in range(self.n):
            self.outs[a][...] = self.own1[a][0].astype(F32) + self.land1[a][0].astype(F32)
            for r in range(1, N_CHIPS):
                self.send2[a][r - 1] = (self.own1[a][r].astype(F32) + self.land1[a][r].astype(F32)).astype(BF16)
            for k in range(4):
                self._copy2(a, k).start()
        if self.packed is not None:
            self.pland2[0] = self.packed[...] + self.pland1[...]
            for cp in self._copies2_packed():
                cp.start()

    def relay2(self):
        for a in range(self.n):
            lo, hi = self._halves(a)
            s2, l2 = self.send2[a], self.land2[a]
            self._copy2(a, 1).wait_recv()
            s2[0, lo] = (s2[0, lo].astype(F32) + l2[2, lo].astype(F32)).astype(BF16)
            self._copy2(a, 4).start()
            self._copy2(a, 3).wait_recv()
            s2[1, hi] = (s2[1, hi].astype(F32) + l2[2, hi].astype(F32)).astype(BF16)
            self._copy2(a, 5).start()

    def finish2(self):
        for a in range(self.n):
            for k in (0, 2, 4, 5):
                self._copy2(a, k).wait_recv()
            for k in range(self.L2_COPIES):
                self._copy2(a, k).wait_send()
            l2 = self.land2[a]
            self.outs[a][...] = self.outs[a][...] + (l2[0].astype(F32) + l2[1].astype(F32))
        if self.packed is not None:
            for cp in self._copies2_packed():
                cp.wait()
            p2 = self.pland2
            self.ptot[...] = (p2[0] + p2[1]) + (p2[2] + p2[3])


def _adamw(ws, gs, ms, vs, rewrite):
    n = len(ws)
    first_row = [g[1] if isinstance(g, tuple) else None for g in gs]
    gs = [g[0] if isinstance(g, tuple) else g for g in gs]
    regrouped = [k for k in range(n) if first_row[k] is not None or gs[k].shape != ws[k].shape or rewrite[k]]
    assert all(w.shape[-1] % LANES == 0 for w, row in zip(ws, first_row) if row is not None)

    def body(*refs):
        for k in range(n):
            w_ref, g_ref, m_ref, v_ref, d_ref, nm_ref, nv_ref = (refs[j * n + k] for j in range(7))
            row = first_row[k]
            windows = [(..., ...)]
            if row is not None and len(w_ref.shape) == 1:
                windows = [(pl.ds(j * LANES, LANES), row + j) for j in range(w_ref.shape[0] // LANES)]
            elif row is not None:
                windows = [(..., pl.ds(row, w_ref.shape[0]))]
            elif g_ref.shape != w_ref.shape:
                windows = [((slice(None), h), (slice(None), slice(h * LANES, h * LANES + w_ref.shape[2]))) for h in range(w_ref.shape[1])]
            if k in regrouped:
                g_out_ref = refs[7 * n + regrouped.index(k)]
            for at, g_at in windows:
                w, g, m, v = w_ref[at], g_ref[g_at], m_ref[at], v_ref[at]
                m = ADAM_B1 * m + (1.0 - ADAM_B1) * g
                v = ADAM_B2 * v + (1.0 - ADAM_B2) * jnp.square(g)
                m_hat = m / (1.0 - ADAM_B1 ** ADAM_STEP)
                v_hat = v / (1.0 - ADAM_B2 ** ADAM_STEP)
                d_ref[at] = -ADAM_LR * (m_hat / (jnp.sqrt(v_hat) + ADAM_EPS) + ADAM_WD * w)
                nm_ref[at] = m
                nv_ref[at] = v
                if k in regrouped:
                    g_out_ref[at] = g

    shapes = [SDS(w.shape, F32) for w in ws]
    outs = pl.pallas_call(
        body, name="adamw", in_specs=[VMEM_SPEC] * (4 * n), out_specs=[VMEM_SPEC] * (3 * n + len(regrouped)),
        out_shape=shapes * 3 + [shapes[k] for k in regrouped], compiler_params=_cp(),
    )(*ws, *gs, *ms, *vs)
    grads = list(gs)
    for k, g in zip(regrouped, outs[3 * n:]):
        grads[k] = g
    return outs[:n], outs[n:2 * n], outs[2 * n:3 * n], grads


WEIGHTS = ("norm_in", "w_in", "q_norm", "w_uq", "kv_norm", "w_ukv", "pool_w", "pool_scale", "w_branch_attn", "w_branch_pool",
           "w_out", "norm_final")
SUBLANES = 8


def _pack_rows(a):
    a = a.reshape(-1, LANES)
    return jnp.pad(a, ((0, -a.shape[0] % SUBLANES), (0, 0)))


def _packed_at(like):
    out, row = [], 0
    for a in like:
        out.append(row)
        rows = a.size // LANES
        row += rows + (-rows % SUBLANES)
    return out


def kernel(x, norm_in, w_in, q_norm, w_uq, kv_norm, w_ukv, pool_w, pool_scale, w_branch_attn, w_branch_pool, w_out, norm_final, loss_target, m_norm_in, m_w_in, m_q_norm, m_w_uq, m_kv_norm, m_w_ukv, m_pool_w, m_pool_scale, m_w_branch_attn, m_w_branch_pool, m_w_out, m_norm_final, v_norm_in, v_w_in, v_q_norm, v_w_uq, v_kv_norm, v_w_ukv, v_pool_w, v_pool_scale, v_w_branch_attn, v_w_branch_pool, v_w_out, v_norm_final):
    w = dict(norm_in=norm_in, w_in=w_in, q_norm=q_norm, w_uq=w_uq, kv_norm=kv_norm, w_ukv=w_ukv, pool_w=pool_w, pool_scale=pool_scale,
             w_branch_attn=w_branch_attn, w_branch_pool=w_branch_pool, w_out=w_out, norm_final=norm_final)
    m = dict(norm_in=m_norm_in, w_in=m_w_in, q_norm=m_q_norm, w_uq=m_w_uq, kv_norm=m_kv_norm, w_ukv=m_w_ukv, pool_w=m_pool_w,
             pool_scale=m_pool_scale, w_branch_attn=m_w_branch_attn, w_branch_pool=m_w_branch_pool, w_out=m_w_out, norm_final=m_norm_final)
    v = dict(norm_in=v_norm_in, w_in=v_w_in, q_norm=v_q_norm, w_uq=v_w_uq, kv_norm=v_kv_norm, w_ukv=v_w_ukv, pool_w=v_pool_w,
             pool_scale=v_pool_scale, w_branch_attn=v_w_branch_attn, w_branch_pool=v_w_branch_pool, w_out=v_w_out, norm_final=v_norm_final)

    def as2d(name, a):
        if name == "w_in":
            return a.T
        if name in ("w_uq", "w_ukv"):
            return a
        return a.reshape(-1, GROUP) if name == "pool_w" else a

    def unshape(name, a):
        return a.T if name == "w_in" else a.reshape(w[name].shape)

    w_in_pad, w_uq_full, w_ukv_full = _all_gather_bf16([as2d(k, w[k]) for k in ("w_in", "w_uq", "w_ukv")])
    loss, grad_x, g2d = _local_step(
        x.reshape(x.shape[1:]), loss_target.reshape(x.shape[1:]), norm_in, w_in_pad, q_norm,
        w_uq_full.reshape(Q_RANK, HW), kv_norm, w_ukv_full.reshape(KV_RANK, HW),
        pool_w, pool_scale, [w_branch_attn, w_branch_pool, w_out], norm_final)

    deltas, new_m, new_v, grads = _adamw([as2d(k, w[k]) for k in WEIGHTS], [g2d[k] for k in WEIGHTS],
                                         [as2d(k, m[k]) for k in WEIGHTS], [as2d(k, v[k]) for k in WEIGHTS],
                                         [k in ("w_in", "w_branch_attn", "w_branch_pool", "w_out") for k in WEIGHTS])
    shaped = lambda arrs: [unshape(k, a) for k, a in zip(WEIGHTS, arrs)]
    return (loss, grad_x.reshape(x.shape), *shaped(grads), *shaped(deltas), *shaped(new_m), *shaped(new_v))
```

```python
import functools

import jax
import jax.numpy as jnp
import numpy as np
from jax import lax
from jax.experimental import pallas as pl
from jax.experimental.pallas import tpu as pltpu

F32 = jnp.float32
BF16 = jnp.bfloat16
SDS = jax.ShapeDtypeStruct

D_MODEL = 1024
HEADS = 8
NOPE = 64
ROPE = 32
VDIM = 64
Q_RANK = 384
KV_RANK = 256
MLA_W = HEADS * VDIM
POOL_W = 512
POOL_GROUPS = 4
GROUP = POOL_W // POOL_GROUPS
CHUNK = 64
ROPE_THETA = 10000.0
EPS = 1e-6
SCALE = (NOPE + ROPE) ** -0.5
LOG2E = 1.4426950408889634
LN2 = 0.6931471805599453
QK_SCALE_LOG2 = SCALE * LOG2E
IN_TOTAL = 4256
ADAM_LR, ADAM_B1, ADAM_B2, ADAM_EPS, ADAM_WD, ADAM_STEP = 0.001, 0.9, 0.999, 1e-08, 0.01, 10

N_DEV = 8
LANES = 128
HEAD_PAD = LANES
HW = HEADS * HEAD_PAD

ZQ, ZKV, ZKR, GA, UP, GP, GM, ZTOT = 0, 384, 640, 768, 1280, 1792, 2304, 4352
FRONT_W = GA
ZKR_ORIG = 640

VMEM_LIMIT = 62 * 1024 * 1024


def _cp(sem=None, **kw):
    if sem is not None:
        kw["dimension_semantics"] = sem
    return pltpu.CompilerParams(vmem_limit_bytes=VMEM_LIMIT, **kw)


def _mm(a, b):
    return lax.dot_general(a, b, (((1,), (0,)), ((), ())), preferred_element_type=F32)


def _mm_nt(a, b):
    return lax.dot_general(a, b, (((1,), (1,)), ((), ())), preferred_element_type=F32)


def _mm_tn(a, b):
    return lax.dot_general(a, b, (((0,), (0,)), ((), ())), preferred_element_type=F32)


def _row_spec(tm, w):
    return pl.BlockSpec((tm, w), lambda i: (i, 0))


def _full_spec(a):
    nd = len(a.shape)
    return pl.BlockSpec(a.shape, lambda *_: (0,) * nd)


def _rope(v, c, sa, sb, sign):
    n = v.shape[-1]
    reps = n // LANES
    if reps > 1:
        c, sa, sb = (jnp.tile(t, (1, reps)) for t in (c, sa, sb))
    up = pltpu.roll(v, n - ROPE // 2, 1)
    dn = pltpu.roll(v, ROPE // 2, 1)
    return v * c + sign * (up * sa + dn * sb)


def _rope_tables(T):
    half = ROPE // 2
    inv_freq = np.float32(ROPE_THETA) ** (-np.arange(half, dtype=np.float32) / np.float32(half))
    ang = np.arange(T, dtype=np.float32)[:, None] * inv_freq[None, :].astype(np.float32)
    cos, sin = np.cos(ang.astype(np.float64)).astype(np.float32), np.sin(ang.astype(np.float64)).astype(np.float32)
    z16 = np.zeros((T, half), np.float32)
    z32 = np.zeros((T, LANES - NOPE - ROPE), np.float32)
    c = np.concatenate([np.ones((T, NOPE), np.float32), cos, cos, z32], axis=1)
    sa = np.concatenate([np.zeros((T, NOPE), np.float32), -sin, z16, z32], axis=1)
    sb = np.concatenate([np.zeros((T, NOPE), np.float32), z16, sin, z32], axis=1)
    return jnp.asarray(c), jnp.asarray(sa), jnp.asarray(sb)


def _silu_parts(g):
    sg = jax.nn.sigmoid(g)
    return g * sg, sg + g * sg * (1.0 - sg)


def _in_proj(x2, norm_in, w_in_pad, q_norm, wuq_pad, kv_norm, wukv, rc, rsa, rsb, tm, late_shards):
    T = x2.shape[0]
    steps = T // tm
    n = len(late_shards)

    def body(x_ref, nin_ref, win_ref, qn_ref, wuq_ref, kvn_ref, wukv_ref, c_ref, sa_ref, sb_ref, *rest):
        hn_ref, zgm_ref, zga_ref, zup_ref, zgp_ref, zfr_ref, q_ref, k_ref, v_ref, vt_ref = rest[n:n + 10]
        ag = _AllGather(rest[:n], rest[n + 10:2 * n + 10], rest[2 * n + 10:])
        step = pl.program_id(0)
        pl.when(step == 0)(ag.start)
        pl.when(step == min(3, steps - 1))(ag.forward)
        xf = x_ref[...]
        r = lax.rsqrt(jnp.mean(xf * xf, axis=-1, keepdims=True) + EPS)
        hn = (xf * r * nin_ref[...]).astype(BF16)
        hn_ref[...] = hn
        zfr = _mm_nt(hn, win_ref[ZQ:GA, :])
        zfr_ref[...] = zfr
        zq, zkv, zkr = zfr[:, ZQ:ZKV], zfr[:, ZKV:ZKR], zfr[:, ZKR:GA]
        c, sa, sb = c_ref[...], sa_ref[...], sb_ref[...]
        rq = lax.rsqrt(jnp.mean(zq * zq, axis=-1, keepdims=True) + EPS)
        cq = (zq * rq * qn_ref[...].reshape(1, Q_RANK)).astype(BF16)
        rkv = lax.rsqrt(jnp.mean(zkv * zkv, axis=-1, keepdims=True) + EPS)
        ckv = (zkv * rkv * kvn_ref[...]).astype(BF16)
        zga_ref[...] = _mm_nt(hn, win_ref[GA:UP, :])
        zup_ref[...] = _mm_nt(hn, win_ref[UP:GP, :])
        zgp_ref[...] = _mm_nt(hn, win_ref[GP:GM, :])
        q_raw = _mm(cq, wuq_ref[...])
        kv = _mm(ckv, wukv_ref[...])
        zgm_ref[...] = _mm_nt(hn, win_ref[GM:ZTOT, :])
        q = _rope(q_raw, c, sa, sb, 1.0)
        q_ref[...] = (q * QK_SCALE_LOG2).astype(BF16)
        kr = _rope(zkr, c, sa, sb, 1.0)
        lane = lax.broadcasted_iota(jnp.int32, kv.shape, 1) % LANES
        k_ref[...] = jnp.where(lane < NOPE, kv, jnp.tile(kr, (1, HEADS))).astype(BF16)
        v = jnp.where(lane < NOPE, 1.0, kv).astype(BF16)
        v_ref[...] = v
        vt_ref[...] = v.T
        pl.when(step == steps - 1)(ag.finish)

    ins = (x2, norm_in, w_in_pad, q_norm, wuq_pad, kv_norm, wukv, rc, rsa, rsb)
    in_specs = [_row_spec(tm, D_MODEL), _full_spec(norm_in), _full_spec(w_in_pad), _full_spec(q_norm), _full_spec(wuq_pad),
                _full_spec(kv_norm), _full_spec(wukv), _row_spec(tm, LANES), _row_spec(tm, LANES), _row_spec(tm, LANES)]
    widths = [(D_MODEL, BF16), (ZTOT - GM, F32), (UP - GA, F32), (GP - UP, F32), (GM - GP, F32), (FRONT_W, F32),
              (HW, BF16), (HW, BF16), (HW, BF16)]
    return pl.pallas_call(
        body, name="in_proj", grid=(steps,), in_specs=in_specs + [_full_spec(s) for s in late_shards],
        out_specs=[_row_spec(tm, w) for w, _ in widths] + [pl.BlockSpec((HW, tm), lambda i: (0, i))] + [HBM_SPEC] * n,
        out_shape=[SDS((T, w), dt) for w, dt in widths] + [SDS((HW, T), BF16)]
        + [SDS((N_DEV,) + s.shape, BF16) for s in late_shards],
        scratch_shapes=_ag_scratch([s.shape for s in late_shards]), compiler_params=_cp(("arbitrary",)),
    )(*ins, *late_shards)


def _chunk_mask(n_q, n_k, q_off, transposed):
    shape = (n_k, n_q) if transposed else (n_q, n_k)
    q = (lax.broadcasted_iota(jnp.int32, shape, 1 if transposed else 0) + q_off) // CHUNK
    k = lax.broadcasted_iota(jnp.int32, shape, 0 if transposed else 1) // CHUNK
    return k <= q


def _store_pair_rows(ref, k, pair):
    t = pair.T
    ref[k, 0:1, :] = t[0:1, :]
    ref[k, 1:2, :] = t[VDIM:VDIM + 1, :]


def _attn_fwd(q_att, k_att, vt_att, tq, hps):
    T = q_att.shape[0]
    head_lanes = [slice(h * LANES, (h + 1) * LANES) for h in range(hps)]

    def body(q_ref, k_ref, vt_ref, o_ref, lser_ref):
        i = pl.program_id(1)
        mask = _chunk_mask(tq, tq, 0, True)
        lane = lax.broadcasted_iota(jnp.int32, (tq, LANES), 1)
        qs = [q_ref[:, hs] for hs in head_lanes]

        def step(j, carry, masked):
            off = pl.multiple_of(j * tq, tq)
            sts = [_mm_nt(k_ref[pl.ds(off, tq), hs], qh) for qh, hs in zip(qs, head_lanes)]
            if masked:
                sts = [jnp.where(mask, st, -jnp.inf) for st in sts]
            ms = [jnp.maximum(m, jnp.max(st, axis=0, keepdims=True)) for (m, _), st in zip(carry, sts)]
            pts = [jnp.exp2(st - m_new).astype(BF16) for st, m_new in zip(sts, ms)]
            return tuple((m_new, jnp.exp2(m - m_new) * acc + _mm(vt_ref[hs, pl.ds(off, tq)], pt))
                         for (m, acc), m_new, pt, hs in zip(carry, ms, pts, head_lanes))

        init = ((jnp.full((1, tq), -jnp.inf, F32), jnp.zeros((LANES, tq), F32)),) * hps
        res = step(i, lax.fori_loop(0, i, functools.partial(step, masked=False), init), True)
        for pair in range(hps // 2):
            (ma, acca), (mb, accb) = res[2 * pair], res[2 * pair + 1]
            la, lb = acca[:1], accb[:1]
            oa, ob = (acca / la).T, (accb / lb).T
            o_ref[:, pair * LANES:(pair + 1) * LANES] = jnp.where(lane < VDIM, pltpu.roll(oa, VDIM, 1), ob)
            lser_ref[pair, 0:1, :] = ma + jnp.log2(la)
            lser_ref[pair, 1:2, :] = mb + jnp.log2(lb)

    qspec = pl.BlockSpec((tq, hps * LANES), lambda p, i: (i, p))
    kspec = pl.BlockSpec((T, hps * LANES), lambda p, i: (0, p))
    vspec = pl.BlockSpec((hps * LANES, T), lambda p, i: (p, 0))
    ospec = pl.BlockSpec((tq, hps * VDIM), lambda p, i: (i, p))
    return pl.pallas_call(
        body, name="attn_fwd", grid=(HEADS // hps, T // tq), in_specs=[qspec, kspec, vspec],
        out_specs=[ospec, pl.BlockSpec((hps // 2, 2, tq), lambda p, i: (p, 0, i))],
        out_shape=[SDS((T, MLA_W), F32), SDS((HEADS // 2, 2, T), F32)],
        compiler_params=_cp(("parallel", "parallel")),
    )(q_att, k_att, vt_att)


def _pick(g, vals):
    out = vals[-1]
    for k in range(len(vals) - 2, -1, -1):
        out = jnp.where(g == k, vals[k], out)
    return out


def _window_sum(u, g, forward):
    T = u.shape[0]
    row = lax.broadcasted_iota(jnp.int32, u.shape, 0)

    def sh(s, k):
        if forward:
            return jnp.where(row >= k, pltpu.roll(s, k, 0), 0.0)
        return jnp.where(row < T - k, pltpu.roll(s, T - k, 0), 0.0)

    sums, s = [], u
    for k in (1, 2, 4, 8):
        s = s + sh(s, k)
        sums.append(s)
    return _pick(g, sums)


MAX_WINDOW = 16


def _pool_inv_count(shape, g):
    T, n = shape
    row = lax.broadcasted_iota(jnp.int32, (MAX_WINDOW, n), 0)
    head = 1.0 / jnp.minimum(row + 1, lax.shift_left(jnp.int32(2), g)).astype(F32)
    inv_w = _pick(g, [0.5, 0.25, 0.125, 0.0625])
    return jnp.concatenate([head, jnp.broadcast_to(inv_w, (T - MAX_WINDOW, n)).astype(F32)], axis=0)


def _pool_fwd(zup, zgp, pool_w, pool_scale):
    T = zup.shape[0]

    def body(u_ref, g_ref, w_ref, sc_ref, y_ref):
        g = pl.program_id(0)
        u = u_ref[...]
        d = _window_sum(u, g, True) * _pool_inv_count(u.shape, g) - u
        lin = _mm(d.astype(BF16), w_ref[0].astype(BF16))
        silu, _ = _silu_parts(g_ref[...])
        y_ref[...] = (lin * sc_ref[...] * silu).astype(BF16)

    col = pl.BlockSpec((T, GROUP), lambda g: (0, g))
    return pl.pallas_call(
        body, name="pool_fwd", grid=(POOL_GROUPS,),
        in_specs=[col, col, pl.BlockSpec((1, GROUP, GROUP), lambda g: (g, 0, 0)), pl.BlockSpec((1, GROUP), lambda g: (0, g))],
        out_specs=col, out_shape=SDS((T, POOL_W), BF16), compiler_params=_cp(("parallel",)),
    )(zup, zgp, pool_w, pool_scale)


def _pool_bwd(zup, zgp, dyp, pool_w, pool_scale, g_nf, loss8):
    T = zup.shape[0]
    firsts = at_w, at_sc, at_nf, at_loss = _packed_at([pool_w, pool_scale, g_nf, SDS((LANES,), F32)])
    rows = at_loss + SUBLANES

    def body(u_ref, g_ref, dy_ref, w_ref, sc_ref, gnf_ref, loss_ref, du_ref, dg_ref, packed_ref):
        g = pl.program_id(0)

        @pl.when(g == 0)
        def _():
            packed_ref[at_sc:, :] = jnp.zeros((rows - at_sc, LANES), F32)
            for j in range(g_nf.shape[1] // LANES):
                packed_ref[at_nf + j:at_nf + j + 1, :] = gnf_ref[:, j * LANES:(j + 1) * LANES]
            packed_ref[at_loss:at_loss + 1, :] = loss_ref[0:1, :]

        u = u_ref[...]
        inv = _pool_inv_count(u.shape, g)
        d = (_window_sum(u, g, True) * inv - u).astype(BF16)
        wb = w_ref[0].astype(BF16)
        lin = _mm(d, wb)
        sc = sc_ref[...]
        silu, dsilu = _silu_parts(g_ref[...])
        dy = dy_ref[...]
        dg_ref[...] = (dy * lin * sc * dsilu).astype(BF16)
        dpre = dy * silu
        gsc = jnp.sum(dpre * lin, axis=0, keepdims=True)
        dlin = (dpre * sc).astype(BF16)
        gw = _mm_tn(d, dlin)
        for k in range(POOL_GROUPS):
            @pl.when(g == k)
            def _(k=k):
                packed_ref[at_w + k * GROUP:at_w + (k + 1) * GROUP, :] = gw
                packed_ref[at_sc + k:at_sc + k + 1, :] = gsc
        dd = _mm_nt(dlin, wb)
        du_ref[...] = (_window_sum(dd * inv, g, False) - dd).astype(BF16)

    col = pl.BlockSpec((T, GROUP), lambda g: (0, g))
    wspec = pl.BlockSpec((1, GROUP, GROUP), lambda g: (g, 0, 0))
    vspec = pl.BlockSpec((1, GROUP), lambda g: (0, g))
    dup, dgp, packed = pl.pallas_call(
        body, name="pool_bwd", grid=(POOL_GROUPS,), in_specs=[col, col, col, wspec, vspec, _full_spec(g_nf), _full_spec(loss8)],
        out_specs=[col, col, pl.BlockSpec((rows, LANES), lambda g: (0, 0))],
        out_shape=[SDS((T, POOL_W), BF16), SDS((T, POOL_W), BF16), SDS((rows, LANES), F32)],
        compiler_params=_cp(("arbitrary",)),
    )(zup, zgp, dyp, pool_w, pool_scale, g_nf, loss8)
    return dup, dgp, packed, firsts


def _tail(x2, tgt, o, zga, ypool, zgm, wba, wbp, wout, norm_final, tm):
    T = x2.shape[0]
    steps = T // tm
    cols = D_MODEL // N_DEV

    def body(x_ref, tgt_ref, o_ref, zga_ref, yp_ref, zgm_ref, wba_ref, wbp_ref, wout_ref, nf_ref,
             loss_ref, dh_ref, dgm_ref, doop_ref, dga_ref, dcapr_ref, dyp_ref, swout_ref, swba_ref, swbp_ref, gnf_ref,
             gwout_ref, gwba_ref, gwbp_ref):
        @pl.when(pl.program_id(0) == 0)
        def _():
            for ref in (loss_ref, gwout_ref, gwba_ref, gwbp_ref, gnf_ref):
                ref[...] = jnp.zeros_like(ref)

        o_v = o_ref[...]
        silu, dsilu = _silu_parts(zga_ref[...])
        ya = (o_v * silu).astype(BF16)
        yp = yp_ref[...]
        wba_v = jnp.concatenate([wba_ref[k] for k in range(N_DEV)], axis=1)
        wbp_v = jnp.concatenate([wbp_ref[k] for k in range(N_DEV)], axis=1)
        wout_v = wout_ref[...]
        a = _mm(ya, wba_v)
        p = _mm(yp, wbp_v)
        gate = jax.nn.sigmoid(zgm_ref[...])
        ga, gp = gate[:, :D_MODEL], gate[:, D_MODEL:]
        mg = (ga * a + gp * p).astype(BF16)
        h = x_ref[...] + _mm(mg, wout_v)
        r = lax.rsqrt(jnp.mean(h * h, axis=-1, keepdims=True) + EPS)
        gf = nf_ref[...]
        hr = h * r
        e = hr * gf - tgt_ref[...]
        loss_ref[...] += (0.5 / D_MODEL) * jnp.sum(e * e)
        dy = e * (1.0 / D_MODEL)
        gnf_ref[...] += jnp.sum(dy * hr, axis=0, keepdims=True)
        u = dy * gf
        dh = r * (u - hr * jnp.mean(u * hr, axis=-1, keepdims=True))
        dh_ref[...] = dh
        dhb = dh.astype(BF16)
        dmg = _mm_nt(dhb, wout_v)
        dab = (dmg * ga).astype(BF16)
        dpb = (dmg * gp).astype(BF16)
        dya = _mm_nt(dab, wba_v)
        dyp_ref[...] = _mm_nt(dpb, wbp_v)
        gwout_ref[...] += _mm_tn(mg, dhb)
        gwba_ref[...] += _mm_tn(ya, dab)
        gwbp_ref[...] += _mm_tn(yp, dpb)
        dgm_ref[:, :D_MODEL] = (dmg * a * ga * (1.0 - ga)).astype(BF16)
        dgm_ref[:, D_MODEL:] = (dmg * p * gp * (1.0 - gp)).astype(BF16)
        do = dya * silu
        dga_ref[...] = (dya * o_v * dsilu).astype(BF16)
        prod = do * o_v
        lo = lax.broadcasted_iota(jnp.int32, (tm, LANES), 1) < VDIM
        for pair in range(HEADS // 2):
            ls = slice(pair * LANES, (pair + 1) * LANES)
            do_p, prod_p = do[:, ls], prod[:, ls]
            dcap_a = jnp.sum(jnp.where(lo, prod_p, 0.0), axis=-1, keepdims=True)
            dcap_b = jnp.sum(jnp.where(lo, 0.0, prod_p), axis=-1, keepdims=True)
            _store_pair_rows(dcapr_ref, pair, jnp.where(lo, dcap_a, dcap_b))
            doop_ref[:, 2 * pair * LANES:(2 * pair + 1) * LANES] = jnp.where(lo, 0.0, pltpu.roll(do_p, VDIM, 1)).astype(BF16)
            doop_ref[:, (2 * pair + 1) * LANES:(2 * pair + 2) * LANES] = jnp.where(lo, 0.0, do_p).astype(BF16)

        @pl.when(pl.program_id(0) == steps - 1)
        def _():
            for k in range(N_DEV):
                swout_ref[k] = gwout_ref[k * cols:(k + 1) * cols, :].astype(BF16)
                swba_ref[k] = gwba_ref[:, k * cols:(k + 1) * cols].astype(BF16)
                swbp_ref[k] = gwbp_ref[:, k * cols:(k + 1) * cols].astype(BF16)

    ins = (x2, tgt, o, zga, ypool, zgm, wba, wbp, wout, norm_final)
    in_specs = [_row_spec(tm, D_MODEL), _row_spec(tm, D_MODEL), _row_spec(tm, MLA_W), _row_spec(tm, MLA_W), _row_spec(tm, POOL_W),
                _row_spec(tm, 2 * D_MODEL), _full_spec(wba), _full_spec(wbp), _full_spec(wout), _full_spec(norm_final)]
    outs = [SDS((8, LANES), F32), SDS((T, D_MODEL), F32), SDS((T, 2 * D_MODEL), BF16), SDS((T, HW), BF16), SDS((T, MLA_W), BF16),
            SDS((HEADS // 2, 2, T), F32), SDS((T, POOL_W), F32),
            SDS((N_DEV, cols, D_MODEL), BF16), SDS((N_DEV, MLA_W, cols), BF16), SDS((N_DEV, POOL_W, cols), BF16), SDS((1, D_MODEL), F32)]
    out_specs = [_full_spec(outs[0]), _row_spec(tm, D_MODEL), _row_spec(tm, 2 * D_MODEL), _row_spec(tm, HW), _row_spec(tm, MLA_W),
                 pl.BlockSpec((HEADS // 2, 2, tm), lambda i: (0, 0, i)), _row_spec(tm, POOL_W),
                 _full_spec(outs[7]), _full_spec(outs[8]), _full_spec(outs[9]), _full_spec(outs[10])]
    return pl.pallas_call(
        body, name="tail", grid=(steps,), in_specs=in_specs, out_specs=out_specs, out_shape=outs,
        scratch_shapes=[pltpu.VMEM((D_MODEL, D_MODEL), F32), pltpu.VMEM((MLA_W, D_MODEL), F32), pltpu.VMEM((POOL_W, D_MODEL), F32)],
        compiler_params=_cp(("arbitrary",)),
    )(*ins)


def _attn_bwd(q_att, k_att, v_att, doop, lse_rows, dcap_rows, tq, hps, slabs, packed):
    T = q_att.shape[0]
    nq = T // tq
    n = len(slabs)
    groups = HEADS // hps
    head_lanes = [slice(h * LANES, (h + 1) * LANES) for h in range(hps)]

    def body(q_ref, k_ref, v_ref, doop_ref, lse_ref, dcap_ref, *rest):
        slab_refs, packed_ref = rest[:n], rest[n]
        dq_ref, dkv_ref, dkr_ref = rest[n + 1:n + 4]
        sum_refs, ptot_ref = rest[n + 4:2 * n + 4], rest[2 * n + 4]
        dq_acc = rest[2 * n + 5]
        rs = _ReduceScatter(slab_refs, packed_ref, sum_refs, ptot_ref, rest[2 * n + 6:])
        group, j = pl.program_id(0), pl.program_id(1)
        tick, last = group * nq + j, groups * nq - 1
        pl.when(tick == 0)(rs.start1)
        pl.when(tick == min(1, last))(rs.finish1_start2)
        pl.when(tick == min(3, last))(rs.relay2)
        mask = _chunk_mask(tq, tq, 0, True)
        lane = lax.broadcasted_iota(jnp.int32, (tq, LANES), 1)
        ks = [k_ref[:, hs] for hs in head_lanes]
        vs = [v_ref[:, hs] for hs in head_lanes]
        kts = [kh.T for kh in ks]

        @pl.when(j == 0)
        def _():
            dq_acc[...] = jnp.zeros_like(dq_acc)

        def step(i, carry, masked):
            rows = pl.ds(pl.multiple_of(i * tq, tq), tq)
            heads = range(hps)
            stat = lambda h: (h // 2, slice(h % 2, h % 2 + 1), rows)
            qhs = [q_ref[rows, hs] for hs in head_lanes]
            doops = [doop_ref[rows, hs] for hs in head_lanes]
            sts = [_mm_nt(ks[h], qhs[h]) for h in heads]
            dpts = [_mm_nt(vs[h], doops[h]) for h in heads]
            pts = [jnp.exp2(sts[h] - lse_ref[stat(h)]) for h in heads]
            if masked:
                pts = [jnp.where(mask, pt, 0.0) for pt in pts]
            dsts = [(pts[h] * (dpts[h] - dcap_ref[stat(h)])).astype(BF16) for h in heads]
            dvs = [_mm(pts[h].astype(BF16), doops[h]) for h in heads]
            dks = [_mm(dsts[h], qhs[h]) for h in heads]
            for h, hs in enumerate(head_lanes):
                dq_acc[hs, rows] += _mm(kts[h], dsts[h])
            return tuple((dk + dks[h], dv + dvs[h]) for h, (dk, dv) in enumerate(carry))

        zero = jnp.zeros((tq, LANES), F32)
        carry = step(j, ((zero, zero),) * hps, True)
        res = lax.fori_loop(j + 1, nq, functools.partial(step, masked=False), carry)
        dkr = None
        for (dk, dv), hs in zip(res, head_lanes):
            dk = dk * LN2
            dkv_ref[:, hs] = jnp.where(lane < NOPE, dk, dv).astype(BF16)
            dkr = dk if dkr is None else dkr + dk
        dkr_ref[0] = jnp.where((lane >= NOPE) & (lane < NOPE + ROPE), dkr, 0.0)

        @pl.when(j == nq - 1)
        def _():
            dq_ref[...] = (dq_acc[...] * SCALE).T.astype(BF16)

        pl.when(tick == last)(rs.finish2)

    kspec = pl.BlockSpec((tq, hps * LANES), lambda p, j: (j, p))
    qspec = pl.BlockSpec((T, hps * LANES), lambda p, j: (0, p))
    rspec = pl.BlockSpec((hps // 2, 2, T), lambda p, j: (p, 0, 0))
    sums = [SDS(s.shape[1:], F32) for s in slabs] + [SDS(packed.shape, F32)]
    return pl.pallas_call(
        body, name="attn_bwd", grid=(groups, nq),
        in_specs=[qspec, kspec, kspec, qspec, rspec, rspec] + [HBM_SPEC] * n + [_full_spec(packed)],
        out_specs=[qspec, kspec, pl.BlockSpec((1, tq, LANES), lambda p, j: (p, j, 0))] + [_full_spec(s) for s in sums],
        out_shape=[SDS((T, HW), BF16), SDS((T, HW), BF16), SDS((groups, T, LANES), F32)] + sums,
        scratch_shapes=[pltpu.VMEM((hps * LANES, T), F32)] + _rs_scratch([s.shape for s in sums[:-1]], packed.shape),
        compiler_params=_cp(("arbitrary", "arbitrary")),
    )(q_att, k_att, v_att, doop, lse_rows, dcap_rows, *slabs, packed)


def _rms_bwd(z, gain, dout):
    r = lax.rsqrt(jnp.mean(z * z, axis=-1, keepdims=True) + EPS)
    zr = z * r
    u = dout * gain
    return r * (u - zr * jnp.mean(u * zr, axis=-1, keepdims=True)), jnp.sum(dout * zr, axis=0, keepdims=True)


def _mla_bwd(dq_att, dkv_nat, dkr4, zfr, q_norm, wuq_pad, kv_norm, wukv, rc, rsa, rsb, tm):
    T = dq_att.shape[0]

    def body(dq_ref, dkv_ref, dkr_ref, zfr_ref, qn_ref, wuq_ref, kvn_ref, wukv_ref, c_ref, sa_ref, sb_ref,
             dfr_ref, gwuq_out, gwukv_out, gqn_ref, gkvn_ref, gwuq_ref, gwukv_ref):
        @pl.when(pl.program_id(0) == 0)
        def _():
            for ref in (gwuq_ref, gwukv_ref, gqn_ref, gkvn_ref):
                ref[...] = jnp.zeros_like(ref)

        c, sa, sb = c_ref[...], sa_ref[...], sb_ref[...]
        zq, zkv = zfr_ref[:, :Q_RANK], zfr_ref[:, Q_RANK:Q_RANK + KV_RANK]
        qn, kvn = qn_ref[...].reshape(1, Q_RANK), kvn_ref[...]
        dkv = dkv_ref[...]
        dckv = _mm_nt(dkv, wukv_ref[...])
        ckv = (zkv * lax.rsqrt(jnp.mean(zkv * zkv, axis=-1, keepdims=True) + EPS) * kvn).astype(BF16)
        gwukv_ref[...] += _mm_tn(ckv, dkv)
        cq = (zq * lax.rsqrt(jnp.mean(zq * zq, axis=-1, keepdims=True) + EPS) * qn).astype(BF16)
        dq = _rope(dq_ref[...].astype(F32), c, sa, sb, -1.0).astype(BF16)
        dzkv, gkvn = _rms_bwd(zkv, kvn, dckv)
        gkvn_ref[...] += gkvn
        gwuq_ref[...] += _mm_tn(cq, dq)
        dzq, gqn = _rms_bwd(zq, qn, _mm_nt(dq, wuq_ref[...]))
        gqn_ref[...] += gqn
        dkr = functools.reduce(lambda a, b: a + b, [dkr_ref[g] for g in range(dkr4.shape[0])])
        dfr_ref[:, :Q_RANK] = dzq.astype(BF16)
        dfr_ref[:, Q_RANK:Q_RANK + KV_RANK] = dzkv.astype(BF16)
        dfr_ref[:, Q_RANK + KV_RANK:] = _rope(dkr, c, sa, sb, -1.0).astype(BF16)

        @pl.when(pl.program_id(0) == pl.num_programs(0) - 1)
        def _():
            gwuq_out[...] = gwuq_ref[...].astype(BF16)
            gwukv_out[...] = gwukv_ref[...].astype(BF16)

    ins = (dq_att, dkv_nat, dkr4, zfr, q_norm, wuq_pad, kv_norm, wukv, rc, rsa, rsb)
    in_specs = [_row_spec(tm, HW), _row_spec(tm, HW), pl.BlockSpec((dkr4.shape[0], tm, LANES), lambda i: (0, i, 0)), _row_spec(tm, FRONT_W),
                _full_spec(q_norm), _full_spec(wuq_pad), _full_spec(kv_norm), _full_spec(wukv),
                _row_spec(tm, LANES), _row_spec(tm, LANES), _row_spec(tm, LANES)]
    outs = [SDS((T, FRONT_W), BF16), SDS((Q_RANK, HW), BF16), SDS((KV_RANK, HW), BF16), SDS((1, Q_RANK), F32), SDS((1, KV_RANK), F32)]
    out_specs = [_row_spec(tm, FRONT_W)] + [_full_spec(s) for s in outs[1:]]
    return pl.pallas_call(
        body, name="mla_bwd", grid=(T // tm,), in_specs=in_specs, out_specs=out_specs, out_shape=outs,
        scratch_shapes=[pltpu.VMEM((Q_RANK, HW), F32), pltpu.VMEM((KV_RANK, HW), F32)],
        compiler_params=_cp(("arbitrary",)),
    )(*ins)


_DZ_COLS = ((GM, ZTOT), (GA, UP), (UP, GP), (GP, GM), (ZQ, GA))


def _in_proj_bwd_x(dzs, x2, dh, norm_in, w_in_pad, tm, slabs):
    T = x2.shape[0]
    steps = T // tm
    n = len(slabs)

    def body(d0, d1, d2, d3, d4, x_ref, dh_ref, nin_ref, win_ref, *rest):
        slab_refs, (gx_ref, gnin_ref), sum_refs = rest[:n], rest[n:n + 2], rest[n + 2:2 * n + 2]
        rs = _ReduceScatter(slab_refs, None, sum_refs, None, rest[2 * n + 2:])
        step = pl.program_id(0)

        @pl.when(step == 0)
        def _():
            gnin_ref[...] = jnp.zeros_like(gnin_ref)
            rs.start1()

        pl.when(step == min(2, steps - 1))(rs.finish1_start2)
        pl.when(step == min(steps * 11 // 16, steps - 1))(rs.relay2)
        dhn = None
        for ref, (lo, hi) in zip((d0, d1, d2, d3, d4), _DZ_COLS):
            t = _mm(ref[...], win_ref[lo:hi, :])
            dhn = t if dhn is None else dhn + t
        dx, gnin = _rms_bwd(x_ref[...], nin_ref[...], dhn)
        gnin_ref[...] += gnin
        gx_ref[...] = dx + dh_ref[...]
        pl.when(step == steps - 1)(rs.finish2)

    in_specs = [_row_spec(tm, hi - lo) for lo, hi in _DZ_COLS] + [_row_spec(tm, D_MODEL), _row_spec(tm, D_MODEL),
                                                                  _full_spec(norm_in), _full_spec(w_in_pad)] + [HBM_SPEC] * n
    sums = [SDS(s.shape[1:], F32) for s in slabs]
    outs = [SDS((T, D_MODEL), F32), SDS((1, D_MODEL), F32)] + sums
    return pl.pallas_call(
        body, name="in_proj_bwd_x", grid=(steps,), in_specs=in_specs,
        out_specs=[_row_spec(tm, D_MODEL), _full_spec(outs[1])] + [_full_spec(s) for s in sums],
        out_shape=outs, scratch_shapes=_rs_scratch([s.shape for s in sums], None), compiler_params=_cp(("arbitrary",)),
    )(*dzs, x2, dh, norm_in, w_in_pad, *slabs)


SLAB_ROWS = IN_TOTAL // N_DEV


def _slab_segments(k):
    cuts = [(0, ZKR_ORIG, 0), (ZKR_ORIG, ZKR_ORIG + ROPE, NOPE), (ZKR_ORIG + ROPE, IN_TOTAL, LANES - ROPE)]
    lo, hi = k * SLAB_ROWS, (k + 1) * SLAB_ROWS
    return [(max(lo, a) - lo, max(lo, a) + shift, min(hi, b) - max(lo, a)) for a, b, shift in cuts if min(hi, b) > max(lo, a)]


def _in_proj_bwd_w(dzs, hn, tm):
    T = hn.shape[0]
    steps = T // tm

    def body(d0, d1, d2, d3, d4, hn_ref, slab_ref, acc_ref):
        @pl.when(pl.program_id(0) == 0)
        def _():
            acc_ref[...] = jnp.zeros_like(acc_ref)

        hn_v = hn_ref[...]
        for ref, (lo, hi) in zip((d0, d1, d2, d3, d4), _DZ_COLS):
            acc_ref[lo:hi, :] += _mm_tn(ref[...], hn_v)

        @pl.when(pl.program_id(0) == steps - 1)
        def _():
            for k in range(N_DEV):
                for at, src, rows in _slab_segments(k):
                    slab_ref[k, at:at + rows, :] = acc_ref[src:src + rows, :].astype(BF16)

    in_specs = [_row_spec(tm, hi - lo) for lo, hi in _DZ_COLS] + [_row_spec(tm, D_MODEL)]
    out = SDS((N_DEV, SLAB_ROWS, D_MODEL), BF16)
    return pl.pallas_call(
        body, name="in_proj_bwd_w", grid=(steps,), in_specs=in_specs, out_specs=_full_spec(out), out_shape=out,
        scratch_shapes=[pltpu.VMEM((ZTOT, D_MODEL), F32)], compiler_params=_cp(("arbitrary",)),
    )(*dzs, hn)


def _local_step(x2, tgt, norm_in, w_in_pad, q_norm, wuq_pad, kv_norm, wukv, pool_w, pool_scale, late_shards, norm_final):
    T = x2.shape[0]
    tm, tm_small, tq = min(512, T), min(256, T), min(512, T)
    heads_fwd, heads_bwd = 4, 4
    row = lambda v: v.reshape(1, -1)
    rc, rsa, rsb = _rope_tables(T)

    hn, zgm, zga, zup, zgp, zfr, q_att, k_att, v_att, vt_att, w_ba, w_bp, w_out = _in_proj(
        x2, row(norm_in), w_in_pad, q_norm, wuq_pad, row(kv_norm), wukv, rc, rsa, rsb, tm, late_shards)
    w_out = w_out.reshape(D_MODEL, D_MODEL)
    o, lse_rows = _attn_fwd(q_att, k_att, vt_att, tq, heads_fwd)
    ypool = _pool_fwd(zup, zgp, pool_w, row(pool_scale))
    loss8, dh, dgm, doop, dga, dcap_rows, dyp, *slabs, g_nf = _tail(
        x2, tgt, o, zga, ypool, zgm, w_ba, w_bp, w_out, row(norm_final), tm_small)
    dup, dgp, packed, (at_pool_w, at_pool_scale, at_nf, at_loss) = _pool_bwd(zup, zgp, dyp, pool_w, row(pool_scale), g_nf, loss8)

    dq_att, dkv_nat, dkr4, s_wout, s_wba, s_wbp, tot_early = _attn_bwd(
        q_att, k_att, v_att, doop, lse_rows, dcap_rows, tq, heads_bwd, slabs, packed)
    dfr, g_wuq_pad, g_wukv, g_qn, g_kvn = _mla_bwd(
        dq_att, dkv_nat, dkr4, zfr, q_norm, wuq_pad, row(kv_norm), wukv, rc, rsa, rsb, tm)
    dzs = (dgm, dga, dup, dgp, dfr)
    slabs = [_in_proj_bwd_w(dzs, hn, tm), g_wuq_pad.reshape(N_DEV, Q_RANK // N_DEV, HW), g_wukv.reshape(N_DEV, KV_RANK // N_DEV, HW)]
    grad_x, g_nin, s_win, s_wuq, s_wukv = _in_proj_bwd_x(dzs, x2, dh, row(norm_in), w_in_pad, tm_small, slabs)
    late = [g_nin, g_qn, g_kvn]
    tot_late = _all_reduce_rows(late)
    at_nin, at_qn, at_kvn = _packed_at(late)

    grads = dict(norm_in=(tot_late, at_nin), w_in=s_win, q_norm=(tot_late, at_qn), w_uq=s_wuq, kv_norm=(tot_late, at_kvn), w_ukv=s_wukv,
                 pool_w=(tot_early, at_pool_w), pool_scale=(tot_early, at_pool_scale), w_branch_attn=s_wba, w_branch_pool=s_wbp,
                 w_out=s_wout, norm_final=(tot_early, at_nf))
    return tot_early[at_loss, 0], grad_x, grads


MESH_ID = pl.DeviceIdType.MESH
VMEM_SPEC = pl.BlockSpec(memory_space=pltpu.VMEM)
HBM_SPEC = pl.BlockSpec(memory_space=pl.ANY)


def _mesh_pos():
    return lax.axis_index("x"), lax.axis_index("y"), lax.axis_index("c")


BF16_TILE_ROWS = 16


def _half_rows(rows):
    cut = -(-(rows // 2) // BF16_TILE_ROWS) * BF16_TILE_ROWS
    return pl.ds(0, cut), pl.ds(cut, rows - cut)


def _slot(px, py, pc):
    return 4 * px + 2 * py + pc


def _staged_shape(shape):
    return (shape[0], shape[1] * LANES) if len(shape) == 3 else tuple(shape)


def _all_gather_bf16(shards):
    n = len(shards)
    staged = [_staged_shape(s.shape) for s in shards]

    def body(*refs):
        ins, outs = refs[:n], refs[n:2 * n]
        land0, scratch = refs[2 * n], refs[2 * n + 1:]
        wpad_ref = outs[0]
        ag = _AllGather(ins, (land0,) + tuple(outs[1:]), scratch)
        ag.start()
        ag.forward()
        ag.finish()
        wpad_ref[ZKR:GA, :] = jnp.zeros((GA - ZKR, D_MODEL), BF16)
        for k in range(N_DEV):
            for at, dst, rows in _slab_segments(k):
                wpad_ref[dst:dst + rows, :] = land0[k, at:at + rows, :]

    return pl.pallas_call(
        body, name="all_gather_weights",
        in_specs=[VMEM_SPEC] * n, out_specs=[VMEM_SPEC] + [HBM_SPEC] * (n - 1),
        out_shape=[SDS((ZTOT, D_MODEL), BF16)] + [SDS((N_DEV,) + s, BF16) for s in staged[1:]],
        scratch_shapes=[pltpu.VMEM((N_DEV,) + staged[0], BF16)] + _ag_scratch(staged),
        compiler_params=_cp(),
    )(*shards)


def _ag_scratch(shapes):
    n = len(shapes)
    dma = pltpu.SemaphoreType.DMA
    return [pltpu.VMEM(tuple(s), BF16) for s in shapes] + [dma((_AllGather.COPIES * n,)), dma((_AllGather.COPIES * n,)), dma((n,))]


class _AllGather:
    COPIES = 8

    def __init__(self, in_refs, dest_refs, scratch):
        n = self.n = len(in_refs)
        self.ins, self.dests, self.stage = in_refs, dest_refs, scratch[:n]
        self.send_sems, self.recv_sems, self.local_sems = scratch[n:]
        x, y, c = _mesh_pos()
        self.c, self.me, self.sibling = c, (x, y, c), (x, y, 1 - c)
        self.xn, self.yn, self.diag = (1 - x, y), (x, 1 - y), (1 - x, 1 - y)

    def _halves(self, a):
        return _half_rows(self.stage[a].shape[0])

    def _copy(self, a, k, block, to, from_stage=False, rows=None):
        dst = self.dests[a].at[_slot(*block)]
        src = self.stage[a] if from_stage else dst
        if rows is not None:
            src, dst = src.at[rows], dst.at[rows]
        return pltpu.make_async_remote_copy(
            src_ref=src, dst_ref=dst, send_sem=self.send_sems.at[self.COPIES * a + k],
            recv_sem=self.recv_sems.at[self.COPIES * a + k], device_id=to, device_id_type=MESH_ID)

    def _mine(self):
        return [pltpu.make_async_copy(self.stage[a], self.dests[a].at[_slot(*self.me)], self.local_sems.at[a]) for a in range(self.n)]

    def _first(self, a):
        return [self._copy(a, 0, self.me, self.sibling, True), self._copy(a, 1, self.me, (*self.xn, self.c), True),
                self._copy(a, 2, self.me, (*self.yn, self.c), True)]

    def _relays(self, a):
        lo, hi = self._halves(a)
        return [self._copy(a, 3, (*self.xn, self.c), (*self.yn, self.c), rows=lo),
                self._copy(a, 4, (*self.yn, self.c), (*self.xn, self.c), rows=hi)]

    def _passes(self, a):
        return [self._copy(a, 5 + j, (*chip, self.c), self.sibling) for j, chip in enumerate((self.xn, self.yn, self.diag))]

    def start(self):
        for a in range(self.n):
            src, dst = self.ins[a], self.stage[a]
            if src.shape == dst.shape:
                dst[...] = src[...].astype(BF16)
            else:
                if src.shape[2] < LANES:
                    dst[...] = jnp.zeros(dst.shape, BF16)
                for h in range(src.shape[1]):
                    dst[:, h * LANES:h * LANES + src.shape[2]] = src[:, h, :].astype(BF16)
        for cp in self._mine():
            cp.start()
        for a in range(self.n):
            for cp in self._first(a):
                cp.start()

    def forward(self):
        for a in range(self.n):
            relays, passes = self._relays(a), self._passes(a)
            for j, chip in enumerate((self.xn, self.yn)):
                self._copy(a, 1 + j, (*chip, self.c), self.me).wait_recv()
                relays[j].start()
                passes[j].start()

    def finish(self):
        for a in range(self.n):
            lo, hi = self._halves(a)
            self._copy(a, 3, (*self.diag, self.c), self.me, rows=lo).wait_recv()
            self._copy(a, 4, (*self.diag, self.c), self.me, rows=hi).wait_recv()
            self._passes(a)[2].start()
        for a in range(self.n):
            self._copy(a, 0, self.sibling, self.me).wait_recv()
            for j, chip in enumerate((self.xn, self.yn, self.diag)):
                self._copy(a, 5 + j, (*chip, 1 - self.c), self.me).wait_recv()
            for cp in self._first(a) + self._relays(a) + self._passes(a):
                cp.wait_send()
        for cp in self._mine():
            cp.wait()


N_CHIPS = 4


def _all_reduce_rows(parts):
    m = len(parts)
    firsts = _packed_at(parts)
    rows = firsts[-1] + -(-(parts[-1].size // LANES) // SUBLANES) * SUBLANES

    def body(*refs):
        part_refs, tot_ref, packed_ref = refs[:m], refs[m], refs[m + 1]
        packed_ref[...] = jnp.zeros_like(packed_ref)
        for ref, first in zip(part_refs, firsts):
            for j in range(ref.shape[1] // LANES):
                packed_ref[first + j:first + j + 1, :] = ref[:, j * LANES:(j + 1) * LANES]
        rs = _ReduceScatter((), packed_ref, (), tot_ref, refs[m + 2:])
        rs.start1()
        rs.finish1_start2()
        rs.relay2()
        rs.finish2()

    return pl.pallas_call(
        body, name="all_reduce_rows", in_specs=[VMEM_SPEC] * m, out_specs=VMEM_SPEC, out_shape=SDS((rows, LANES), F32),
        scratch_shapes=[pltpu.VMEM((rows, LANES), F32)] + _rs_scratch([], (rows, LANES)), compiler_params=_cp(),
    )(*parts)


def _rs_scratch(shapes, packed_shape):
    n = len(shapes)
    n1, n2 = N_CHIPS * n + 1, _ReduceScatter.L2_COPIES * n + N_CHIPS - 1
    dma = pltpu.SemaphoreType.DMA
    packed = [] if packed_shape is None else [pltpu.VMEM(packed_shape, F32), pltpu.VMEM((N_CHIPS,) + tuple(packed_shape), F32)]
    return ([pltpu.VMEM((N_CHIPS,) + tuple(s), BF16) for s in shapes] * 2 + [pltpu.VMEM((N_CHIPS - 1,) + tuple(s), BF16) for s in shapes] * 2
            + packed + [dma((max(N_CHIPS * n, 1),)), dma((n1,)), dma((n1,)), dma((n2,)), dma((n2,))])


class _ReduceScatter:
    L2_COPIES = 6

    def __init__(self, slab_refs, packed_ref, out_refs, ptot_ref, scratch):
        n = self.n = len(slab_refs)
        self.slabs, self.packed, self.outs, self.ptot = slab_refs, packed_ref, out_refs, ptot_ref
        self.own1, self.land1, self.send2, self.land2 = (scratch[k * n:(k + 1) * n] for k in range(4))
        rest = scratch[4 * n:]
        if packed_ref is not None:
            self.pland1, self.pland2 = rest[:2]
            rest = rest[2:]
        self.loc_sems, self.send1_sems, self.recv1_sems, self.send2_sems, self.recv2_sems = rest
        self.x, self.y, self.c = _mesh_pos()

    def _chip(self, r):
        return (1 - self.x if r & 2 else self.x, 1 - self.y if r & 1 else self.y)

    @staticmethod
    def _remote(src, dst, send_sem, recv_sem, to):
        return pltpu.make_async_remote_copy(src_ref=src, dst_ref=dst, send_sem=send_sem, recv_sem=recv_sem, device_id=to,
                                            device_id_type=MESH_ID)

    def _copies1(self):
        c, sibling = self.c, (self.x, self.y, 1 - self.c)
        cps = []
        for a in range(self.n):
            for r in range(N_CHIPS):
                k = N_CHIPS * a + r
                cps.append(pltpu.make_async_copy(self.slabs[a].at[_slot(*self._chip(r), c)], self.own1[a].at[r], self.loc_sems.at[k]))
                cps.append(self._remote(self.slabs[a].at[_slot(*self._chip(r), 1 - c)], self.land1[a].at[r],
                                        self.send1_sems.at[k], self.recv1_sems.at[k], sibling))
        if self.packed is not None:
            k = N_CHIPS * self.n
            cps.append(self._remote(self.packed, self.pland1, self.send1_sems.at[k], self.recv1_sems.at[k], sibling))
        return cps

    def _halves(self, a):
        return _half_rows(self.send2[a].shape[1])

    def _copy2(self, a, k):
        lo, hi = self._halves(a)
        xn, yn = (*self._chip(2), self.c), (*self._chip(1), self.c)
        slot, rows, to = [(1, lo, xn), (2, lo, xn), (0, hi, yn), (2, hi, yn), (0, lo, yn), (1, hi, xn)][k]
        return self._remote(self.send2[a].at[slot].at[rows], self.land2[a].at[slot].at[rows],
                            self.send2_sems.at[self.L2_COPIES * a + k], self.recv2_sems.at[self.L2_COPIES * a + k], to)

    def _copies2_packed(self):
        base = self.L2_COPIES * self.n - 1
        return [self._remote(self.pland2.at[0], self.pland2.at[r], self.send2_sems.at[base + r], self.recv2_sems.at[base + r],
                             (*self._chip(r), self.c)) for r in range(1, N_CHIPS)]

    def start1(self):
        for cp in self._copies1():
            cp.start()

    def finish1_start2(self):
        for cp in self._copies1():
            cp.wait()
        for a in range(self.n):
            self.outs[a][...] = self.own1[a][0].astype(F32) + self.land1[a][0].astype(F32)
            for r in range(1, N_CHIPS):
                self.send2[a][r - 1] = (self.own1[a][r].astype(F32) + self.land1[a][r].astype(F32)).astype(BF16)
            for k in range(4):
                self._copy2(a, k).start()
        if self.packed is not None:
            self.pland2[0] = self.packed[...] + self.pland1[...]
            for cp in self._copies2_packed():
                cp.start()

    def relay2(self):
        for a in range(self.n):
            lo, hi = self._halves(a)
            s2, l2 = self.send2[a], self.land2[a]
            self._copy2(a, 1).wait_recv()
            s2[0, lo] = (s2[0, lo].astype(F32) + l2[2, lo].astype(F32)).astype(BF16)
            self._copy2(a, 4).start()
            self._copy2(a, 3).wait_recv()
            s2[1, hi] = (s2[1, hi].astype(F32) + l2[2, hi].astype(F32)).astype(BF16)
            self._copy2(a, 5).start()

    def finish2(self):
        for a in range(self.n):
            for k in (0, 2, 4, 5):
                self._copy2(a, k).wait_recv()
            for k in range(self.L2_COPIES):
                self._copy2(a, k).wait_send()
            l2 = self.land2[a]
            self.outs[a][...] = self.outs[a][...] + (l2[0].astype(F32) + l2[1].astype(F32))
        if self.packed is not None:
            for cp in self._copies2_packed():
                cp.wait()
            p2 = self.pland2
            self.ptot[...] = (p2[0] + p2[1]) + (p2[2] + p2[3])


def _adamw(ws, gs, ms, vs, rewrite):
    n = len(ws)
    first_row = [g[1] if isinstance(g, tuple) else None for g in gs]
    gs = [g[0] if isinstance(g, tuple) else g for g in gs]
    regrouped = [k for k in range(n) if first_row[k] is not None or gs[k].shape != ws[k].shape or rewrite[k]]
    assert all(w.shape[-1] % LANES == 0 for w, row in zip(ws, first_row) if row is not None)

    def body(*refs):
        for k in range(n):
            w_ref, g_ref, m_ref, v_ref, d_ref, nm_ref, nv_ref = (refs[j * n + k] for j in range(7))
            row = first_row[k]
            windows = [(..., ...)]
            if row is not None and len(w_ref.shape) == 1:
                windows = [(pl.ds(j * LANES, LANES), row + j) for j in range(w_ref.shape[0] // LANES)]
            elif row is not None:
                windows = [(..., pl.ds(row, w_ref.shape[0]))]
            elif g_ref.shape != w_ref.shape:
                windows = [((slice(None), h), (slice(None), slice(h * LANES, h * LANES + w_ref.shape[2]))) for h in range(w_ref.shape[1])]
            if k in regrouped:
                g_out_ref = refs[7 * n + regrouped.index(k)]
            for at, g_at in windows:
                w, g, m, v = w_ref[at], g_ref[g_at], m_ref[at], v_ref[at]
                m = ADAM_B1 * m + (1.0 - ADAM_B1) * g
                v = ADAM_B2 * v + (1.0 - ADAM_B2) * jnp.square(g)
                m_hat = m / (1.0 - ADAM_B1 ** ADAM_STEP)
                v_hat = v / (1.0 - ADAM_B2 ** ADAM_STEP)
                d_ref[at] = -ADAM_LR * (m_hat / (jnp.sqrt(v_hat) + ADAM_EPS) + ADAM_WD * w)
                nm_ref[at] = m
                nv_ref[at] = v
                if k in regrouped:
                    g_out_ref[at] = g

    shapes = [SDS(w.shape, F32) for w in ws]
    outs = pl.pallas_call(
        body, name="adamw", in_specs=[VMEM_SPEC] * (4 * n), out_specs=[VMEM_SPEC] * (3 * n + len(regrouped)),
        out_shape=shapes * 3 + [shapes[k] for k in regrouped], compiler_params=_cp(),
    )(*ws, *gs, *ms, *vs)
    grads = list(gs)
    for k, g in zip(regrouped, outs[3 * n:]):
        grads[k] = g
    return outs[:n], outs[n:2 * n], outs[2 * n:3 * n], grads


WEIGHTS = ("norm_in", "w_in", "q_norm", "w_uq", "kv_norm", "w_ukv", "pool_w", "pool_scale", "w_branch_attn", "w_branch_pool",
           "w_out", "norm_final")
SUBLANES = 8


def _packed_at(like):
    out, row = [], 0
    for a in like:
        out.append(row)
        rows = a.size // LANES
        row += rows + (-rows % SUBLANES)
    return out


def kernel(x, norm_in, w_in, q_norm, w_uq, kv_norm, w_ukv, pool_w, pool_scale, w_branch_attn, w_branch_pool, w_out, norm_final, loss_target, m_norm_in, m_w_in, m_q_norm, m_w_uq, m_kv_norm, m_w_ukv, m_pool_w, m_pool_scale, m_w_branch_attn, m_w_branch_pool, m_w_out, m_norm_final, v_norm_in, v_w_in, v_q_norm, v_w_uq, v_kv_norm, v_w_ukv, v_pool_w, v_pool_scale, v_w_branch_attn, v_w_branch_pool, v_w_out, v_norm_final):
    w = dict(norm_in=norm_in, w_in=w_in, q_norm=q_norm, w_uq=w_uq, kv_norm=kv_norm, w_ukv=w_ukv, pool_w=pool_w, pool_scale=pool_scale,
             w_branch_attn=w_branch_attn, w_branch_pool=w_branch_pool, w_out=w_out, norm_final=norm_final)
    m = dict(norm_in=m_norm_in, w_in=m_w_in, q_norm=m_q_norm, w_uq=m_w_uq, kv_norm=m_kv_norm, w_ukv=m_w_ukv, pool_w=m_pool_w,
             pool_scale=m_pool_scale, w_branch_attn=m_w_branch_attn, w_branch_pool=m_w_branch_pool, w_out=m_w_out, norm_final=m_norm_final)
    v = dict(norm_in=v_norm_in, w_in=v_w_in, q_norm=v_q_norm, w_uq=v_w_uq, kv_norm=v_kv_norm, w_ukv=v_w_ukv, pool_w=v_pool_w,
             pool_scale=v_pool_scale, w_branch_attn=v_w_branch_attn, w_branch_pool=v_w_branch_pool, w_out=v_w_out, norm_final=v_norm_final)

    def as2d(name, a):
        if name == "w_in":
            return a.T
        if name in ("w_uq", "w_ukv"):
            return a
        return a.reshape(-1, GROUP) if name == "pool_w" else a

    def unshape(name, a):
        return a.T if name == "w_in" else a.reshape(w[name].shape)

    w_in_pad, w_uq_full, w_ukv_full = _all_gather_bf16([as2d(k, w[k]) for k in ("w_in", "w_uq", "w_ukv")])
    loss, grad_x, g2d = _local_step(
        x.reshape(x.shape[1:]), loss_target.reshape(x.shape[1:]), norm_in, w_in_pad, q_norm,
        w_uq_full.reshape(Q_RANK, HW), kv_norm, w_ukv_full.reshape(KV_RANK, HW),
        pool_w, pool_scale, [w_branch_attn, w_branch_pool, w_out], norm_final)

    deltas, new_m, new_v, grads = _adamw([as2d(k, w[k]) for k in WEIGHTS], [g2d[k] for k in WEIGHTS],
                                         [as2d(k, m[k]) for k in WEIGHTS], [as2d(k, v[k]) for k in WEIGHTS],
                                         [k in ("w_in", "w_branch_attn", "w_branch_pool", "w_out") for k in WEIGHTS])
    shaped = lambda arrs: [unshape(k, a) for k, a in zip(WEIGHTS, arrs)]
    return (loss, grad_x.reshape(x.shape), *shaped(grads), *shaped(deltas), *shaped(new_m), *shaped(new_v))
```

```python
import functools

import jax
import jax.numpy as jnp
import numpy as np
from jax import lax
from jax.experimental import pallas as pl
from jax.experimental.pallas import tpu as pltpu

F32 = jnp.float32
BF16 = jnp.bfloat16
SDS = jax.ShapeDtypeStruct

D_MODEL = 1024
HEADS = 8
NOPE = 64
ROPE = 32
VDIM = 64
Q_RANK = 384
KV_RANK = 256
MLA_W = HEADS * VDIM
POOL_W = 512
POOL_GROUPS = 4
GROUP = POOL_W // POOL_GROUPS
CHUNK = 64
ROPE_THETA = 10000.0
EPS = 1e-6
SCALE = (NOPE + ROPE) ** -0.5
LOG2E = 1.4426950408889634
LN2 = 0.6931471805599453
QK_SCALE_LOG2 = SCALE * LOG2E
IN_TOTAL = 4256
ADAM_LR, ADAM_B1, ADAM_B2, ADAM_EPS, ADAM_WD, ADAM_STEP = 0.001, 0.9, 0.999, 1e-08, 0.01, 10

N_DEV = 8
LANES = 128
HEAD_PAD = LANES
HW = HEADS * HEAD_PAD

ZQ, ZKV, ZKR, GA, UP, GP, GM, ZTOT = 0, 384, 640, 768, 1280, 1792, 2304, 4352
FRONT_W = GA
ZKR_ORIG = 640

VMEM_LIMIT = 62 * 1024 * 1024


def _cp(sem=None, **kw):
    if sem is not None:
        kw["dimension_semantics"] = sem
    return pltpu.CompilerParams(vmem_limit_bytes=VMEM_LIMIT, **kw)


def _mm(a, b):
    return lax.dot_general(a, b, (((1,), (0,)), ((), ())), preferred_element_type=F32)


def _mm_nt(a, b):
    return lax.dot_general(a, b, (((1,), (1,)), ((), ())), preferred_element_type=F32)


def _mm_tn(a, b):
    return lax.dot_general(a, b, (((0,), (0,)), ((), ())), preferred_element_type=F32)


def _row_spec(tm, w):
    return pl.BlockSpec((tm, w), lambda i: (i, 0))


def _full_spec(a):
    nd = len(a.shape)
    return pl.BlockSpec(a.shape, lambda *_: (0,) * nd)


def _rope(v, c, sa, sb, sign):
    n = v.shape[-1]
    reps = n // LANES
    if reps > 1:
        c, sa, sb = (jnp.tile(t, (1, reps)) for t in (c, sa, sb))
    up = pltpu.roll(v, n - ROPE // 2, 1)
    dn = pltpu.roll(v, ROPE // 2, 1)
    return v * c + sign * (up * sa + dn * sb)


def _rope_tables(T):
    half = ROPE // 2
    inv_freq = np.float32(ROPE_THETA) ** (-np.arange(half, dtype=np.float32) / np.float32(half))
    ang = np.arange(T, dtype=np.float32)[:, None] * inv_freq[None, :].astype(np.float32)
    cos, sin = np.cos(ang.astype(np.float64)).astype(np.float32), np.sin(ang.astype(np.float64)).astype(np.float32)
    z16 = np.zeros((T, half), np.float32)
    z32 = np.zeros((T, LANES - NOPE - ROPE), np.float32)
    c = np.concatenate([np.ones((T, NOPE), np.float32), cos, cos, z32], axis=1)
    sa = np.concatenate([np.zeros((T, NOPE), np.float32), -sin, z16, z32], axis=1)
    sb = np.concatenate([np.zeros((T, NOPE), np.float32), z16, sin, z32], axis=1)
    return jnp.asarray(c), jnp.asarray(sa), jnp.asarray(sb)


def _silu_parts(g):
    sg = jax.nn.sigmoid(g)
    return g * sg, sg + g * sg * (1.0 - sg)


def _in_proj(x2, norm_in, w_in_pad, q_norm, wuq_pad, kv_norm, wukv, rc, rsa, rsb, tm, late_shards):
    T = x2.shape[0]
    steps = T // tm
    n = len(late_shards)

    def body(x_ref, nin_ref, win_ref, qn_ref, wuq_ref, kvn_ref, wukv_ref, c_ref, sa_ref, sb_ref, *rest):
        hn_ref, zgm_ref, zga_ref, zup_ref, zgp_ref, zfr_ref, q_ref, k_ref, v_ref, vt_ref = rest[n:n + 10]
        ag = _AllGather(rest[:n], rest[n + 10:2 * n + 10], rest[2 * n + 10:])
        step = pl.program_id(0)
        pl.when(step == 0)(ag.start)
        pl.when(step == min(3, steps - 1))(ag.forward)
        xf = x_ref[...]
        r = lax.rsqrt(jnp.mean(xf * xf, axis=-1, keepdims=True) + EPS)
        hn = (xf * r * nin_ref[...]).astype(BF16)
        hn_ref[...] = hn
        zfr = _mm_nt(hn, win_ref[ZQ:GA, :])
        zfr_ref[...] = zfr
        zq, zkv, zkr = zfr[:, ZQ:ZKV], zfr[:, ZKV:ZKR], zfr[:, ZKR:GA]
        c, sa, sb = c_ref[...], sa_ref[...], sb_ref[...]
        rq = lax.rsqrt(jnp.mean(zq * zq, axis=-1, keepdims=True) + EPS)
        cq = (zq * rq * qn_ref[...].reshape(1, Q_RANK)).astype(BF16)
        rkv = lax.rsqrt(jnp.mean(zkv * zkv, axis=-1, keepdims=True) + EPS)
        ckv = (zkv * rkv * kvn_ref[...]).astype(BF16)
        zga_ref[...] = _mm_nt(hn, win_ref[GA:UP, :])
        zup_ref[...] = _mm_nt(hn, win_ref[UP:GP, :])
        zgp_ref[...] = _mm_nt(hn, win_ref[GP:GM, :])
        q_raw = _mm(cq, wuq_ref[...])
        kv = _mm(ckv, wukv_ref[...])
        zgm_ref[...] = _mm_nt(hn, win_ref[GM:ZTOT, :])
        q = _rope(q_raw, c, sa, sb, 1.0)
        q_ref[...] = (q * QK_SCALE_LOG2).astype(BF16)
        kr = _rope(zkr, c, sa, sb, 1.0)
        lane = lax.broadcasted_iota(jnp.int32, kv.shape, 1) % LANES
        k_ref[...] = jnp.where(lane < NOPE, kv, jnp.tile(kr, (1, HEADS))).astype(BF16)
        v = jnp.where(lane < NOPE, 1.0, kv).astype(BF16)
        v_ref[...] = v
        vt_ref[...] = v.T
        pl.when(step == steps - 1)(ag.finish)

    ins = (x2, norm_in, w_in_pad, q_norm, wuq_pad, kv_norm, wukv, rc, rsa, rsb)
    in_specs = [_row_spec(tm, D_MODEL), _full_spec(norm_in), _full_spec(w_in_pad), _full_spec(q_norm), _full_spec(wuq_pad),
                _full_spec(kv_norm), _full_spec(wukv), _row_spec(tm, LANES), _row_spec(tm, LANES), _row_spec(tm, LANES)]
    widths = [(D_MODEL, BF16), (ZTOT - GM, F32), (UP - GA, F32), (GP - UP, F32), (GM - GP, F32), (FRONT_W, F32),
              (HW, BF16), (HW, BF16), (HW, BF16)]
    return pl.pallas_call(
        body, name="in_proj", grid=(steps,), in_specs=in_specs + [_full_spec(s) for s in late_shards],
        out_specs=[_row_spec(tm, w) for w, _ in widths] + [pl.BlockSpec((HW, tm), lambda i: (0, i))] + [HBM_SPEC] * n,
        out_shape=[SDS((T, w), dt) for w, dt in widths] + [SDS((HW, T), BF16)]
        + [SDS((N_DEV,) + s.shape, BF16) for s in late_shards],
        scratch_shapes=_ag_scratch([s.shape for s in late_shards]), compiler_params=_cp(("arbitrary",)),
    )(*ins, *late_shards)


def _chunk_mask(n_q, n_k, q_off, transposed):
    shape = (n_k, n_q) if transposed else (n_q, n_k)
    q = (lax.broadcasted_iota(jnp.int32, shape, 1 if transposed else 0) + q_off) // CHUNK
    k = lax.broadcasted_iota(jnp.int32, shape, 0 if transposed else 1) // CHUNK
    return k <= q


def _store_pair_rows(ref, k, pair):
    t = pair.T
    ref[k, 0:1, :] = t[0:1, :]
    ref[k, 1:2, :] = t[VDIM:VDIM + 1, :]


def _attn_fwd(q_att, k_att, vt_att, tq, hps):
    T = q_att.shape[0]
    head_lanes = [slice(h * LANES, (h + 1) * LANES) for h in range(hps)]

    def body(q_ref, k_ref, vt_ref, o_ref, lser_ref):
        i = pl.program_id(1)
        mask = _chunk_mask(tq, tq, 0, True)
        lane = lax.broadcasted_iota(jnp.int32, (tq, LANES), 1)
        qs = [q_ref[:, hs] for hs in head_lanes]

        def step(j, carry, masked):
            off = pl.multiple_of(j * tq, tq)
            sts = [_mm_nt(k_ref[pl.ds(off, tq), hs], qh) for qh, hs in zip(qs, head_lanes)]
            if masked:
                sts = [jnp.where(mask, st, -jnp.inf) for st in sts]
            ms = [jnp.maximum(m, jnp.max(st, axis=0, keepdims=True)) for (m, _), st in zip(carry, sts)]
            pts = [jnp.exp2(st - m_new).astype(BF16) for st, m_new in zip(sts, ms)]
            return tuple((m_new, jnp.exp2(m - m_new) * acc + _mm(vt_ref[hs, pl.ds(off, tq)], pt))
                         for (m, acc), m_new, pt, hs in zip(carry, ms, pts, head_lanes))

        init = ((jnp.full((1, tq), -jnp.inf, F32), jnp.zeros((LANES, tq), F32)),) * hps
        res = step(i, lax.fori_loop(0, i, functools.partial(step, masked=False), init), True)
        for pair in range(hps // 2):
            (ma, acca), (mb, accb) = res[2 * pair], res[2 * pair + 1]
            la, lb = acca[:1], accb[:1]
            oa, ob = (acca / la).T, (accb / lb).T
            o_ref[:, pair * LANES:(pair + 1) * LANES] = jnp.where(lane < VDIM, pltpu.roll(oa, VDIM, 1), ob)
            lser_ref[pair, 0:1, :] = ma + jnp.log2(la)
            lser_ref[pair, 1:2, :] = mb + jnp.log2(lb)

    qspec = pl.BlockSpec((tq, hps * LANES), lambda p, i: (i, p))
    kspec = pl.BlockSpec((T, hps * LANES), lambda p, i: (0, p))
    vspec = pl.BlockSpec((hps * LANES, T), lambda p, i: (p, 0))
    ospec = pl.BlockSpec((tq, hps * VDIM), lambda p, i: (i, p))
    return pl.pallas_call(
        body, name="attn_fwd", grid=(HEADS // hps, T // tq), in_specs=[qspec, kspec, vspec],
        out_specs=[ospec, pl.BlockSpec((hps // 2, 2, tq), lambda p, i: (p, 0, i))],
        out_shape=[SDS((T, MLA_W), F32), SDS((HEADS // 2, 2, T), F32)],
        compiler_params=_cp(("parallel", "parallel")),
    )(q_att, k_att, vt_att)


def _pick(g, vals):
    out = vals[-1]
    for k in range(len(vals) - 2, -1, -1):
        out = jnp.where(g == k, vals[k], out)
    return out


def _window_sum(u, g, forward):
    T = u.shape[0]
    row = lax.broadcasted_iota(jnp.int32, u.shape, 0)

    def sh(s, k):
        if forward:
            return jnp.where(row >= k, pltpu.roll(s, k, 0), 0.0)
        return jnp.where(row < T - k, pltpu.roll(s, T - k, 0), 0.0)

    sums, s = [], u
    for k in (1, 2, 4, 8):
        s = s + sh(s, k)
        sums.append(s)
    return _pick(g, sums)


MAX_WINDOW = 16


def _pool_inv_count(shape, g):
    T, n = shape
    row = lax.broadcasted_iota(jnp.int32, (MAX_WINDOW, n), 0)
    head = 1.0 / jnp.minimum(row + 1, lax.shift_left(jnp.int32(2), g)).astype(F32)
    inv_w = _pick(g, [0.5, 0.25, 0.125, 0.0625])
    return jnp.concatenate([head, jnp.broadcast_to(inv_w, (T - MAX_WINDOW, n)).astype(F32)], axis=0)


def _pool_fwd(zup, zgp, pool_w, pool_scale):
    T = zup.shape[0]

    def body(u_ref, g_ref, w_ref, sc_ref, y_ref):
        g = pl.program_id(0)
        u = u_ref[...]
        d = _window_sum(u, g, True) * _pool_inv_count(u.shape, g) - u
        lin = _mm(d.astype(BF16), w_ref[0].astype(BF16))
        silu, _ = _silu_parts(g_ref[...])
        y_ref[...] = (lin * sc_ref[...] * silu).astype(BF16)

    col = pl.BlockSpec((T, GROUP), lambda g: (0, g))
    return pl.pallas_call(
        body, name="pool_fwd", grid=(POOL_GROUPS,),
        in_specs=[col, col, pl.BlockSpec((1, GROUP, GROUP), lambda g: (g, 0, 0)), pl.BlockSpec((1, GROUP), lambda g: (0, g))],
        out_specs=col, out_shape=SDS((T, POOL_W), BF16), compiler_params=_cp(("parallel",)),
    )(zup, zgp, pool_w, pool_scale)


def _pool_bwd(zup, zgp, dyp, pool_w, pool_scale):
    T = zup.shape[0]

    def body(u_ref, g_ref, dy_ref, w_ref, sc_ref, du_ref, dg_ref, gw_ref, gsc_ref):
        g = pl.program_id(0)
        u = u_ref[...]
        inv = _pool_inv_count(u.shape, g)
        d = (_window_sum(u, g, True) * inv - u).astype(BF16)
        wb = w_ref[0].astype(BF16)
        lin = _mm(d, wb)
        sc = sc_ref[...]
        silu, dsilu = _silu_parts(g_ref[...])
        dy = dy_ref[...]
        dg_ref[...] = (dy * lin * sc * dsilu).astype(BF16)
        dpre = dy * silu
        gsc_ref[...] = jnp.sum(dpre * lin, axis=0, keepdims=True)
        dlin = (dpre * sc).astype(BF16)
        gw_ref[0] = _mm_tn(d, dlin)
        dd = _mm_nt(dlin, wb)
        du_ref[...] = (_window_sum(dd * inv, g, False) - dd).astype(BF16)

    col = pl.BlockSpec((T, GROUP), lambda g: (0, g))
    wspec = pl.BlockSpec((1, GROUP, GROUP), lambda g: (g, 0, 0))
    vspec = pl.BlockSpec((1, GROUP), lambda g: (0, g))
    return pl.pallas_call(
        body, name="pool_bwd", grid=(POOL_GROUPS,), in_specs=[col, col, col, wspec, vspec], out_specs=[col, col, wspec, vspec],
        out_shape=[SDS((T, POOL_W), BF16), SDS((T, POOL_W), BF16), SDS((POOL_GROUPS, GROUP, GROUP), F32), SDS((1, POOL_W), F32)],
        compiler_params=_cp(("parallel",)),
    )(zup, zgp, dyp, pool_w, pool_scale)


def _tail(x2, tgt, o, zga, ypool, zgm, wba, wbp, wout, norm_final, tm):
    T = x2.shape[0]
    steps = T // tm
    cols = D_MODEL // N_DEV

    def body(x_ref, tgt_ref, o_ref, zga_ref, yp_ref, zgm_ref, wba_ref, wbp_ref, wout_ref, nf_ref,
             loss_ref, dh_ref, dgm_ref, doop_ref, dga_ref, dcapr_ref, dyp_ref, swout_ref, swba_ref, swbp_ref, gnf_ref,
             gwout_ref, gwba_ref, gwbp_ref):
        @pl.when(pl.program_id(0) == 0)
        def _():
            for ref in (loss_ref, gwout_ref, gwba_ref, gwbp_ref, gnf_ref):
                ref[...] = jnp.zeros_like(ref)

        o_v = o_ref[...]
        silu, dsilu = _silu_parts(zga_ref[...])
        ya = (o_v * silu).astype(BF16)
        yp = yp_ref[...]
        wba_v = jnp.concatenate([wba_ref[k] for k in range(N_DEV)], axis=1)
        wbp_v = jnp.concatenate([wbp_ref[k] for k in range(N_DEV)], axis=1)
        wout_v = wout_ref[...]
        a = _mm(ya, wba_v)
        p = _mm(yp, wbp_v)
        gate = jax.nn.sigmoid(zgm_ref[...])
        ga, gp = gate[:, :D_MODEL], gate[:, D_MODEL:]
        mg = (ga * a + gp * p).astype(BF16)
        h = x_ref[...] + _mm(mg, wout_v)
        r = lax.rsqrt(jnp.mean(h * h, axis=-1, keepdims=True) + EPS)
        gf = nf_ref[...]
        hr = h * r
        e = hr * gf - tgt_ref[...]
        loss_ref[...] += (0.5 / D_MODEL) * jnp.sum(e * e)
        dy = e * (1.0 / D_MODEL)
        gnf_ref[...] += jnp.sum(dy * hr, axis=0, keepdims=True)
        u = dy * gf
        dh = r * (u - hr * jnp.mean(u * hr, axis=-1, keepdims=True))
        dh_ref[...] = dh
        dhb = dh.astype(BF16)
        dmg = _mm_nt(dhb, wout_v)
        dab = (dmg * ga).astype(BF16)
        dpb = (dmg * gp).astype(BF16)
        dya = _mm_nt(dab, wba_v)
        dyp_ref[...] = _mm_nt(dpb, wbp_v)
        gwout_ref[...] += _mm_tn(mg, dhb)
        gwba_ref[...] += _mm_tn(ya, dab)
        gwbp_ref[...] += _mm_tn(yp, dpb)
        dgm_ref[:, :D_MODEL] = (dmg * a * ga * (1.0 - ga)).astype(BF16)
        dgm_ref[:, D_MODEL:] = (dmg * p * gp * (1.0 - gp)).astype(BF16)
        do = dya * silu
        dga_ref[...] = (dya * o_v * dsilu).astype(BF16)
        prod = do * o_v
        lo = lax.broadcasted_iota(jnp.int32, (tm, LANES), 1) < VDIM
        for pair in range(HEADS // 2):
            ls = slice(pair * LANES, (pair + 1) * LANES)
            do_p, prod_p = do[:, ls], prod[:, ls]
            dcap_a = jnp.sum(jnp.where(lo, prod_p, 0.0), axis=-1, keepdims=True)
            dcap_b = jnp.sum(jnp.where(lo, 0.0, prod_p), axis=-1, keepdims=True)
            _store_pair_rows(dcapr_ref, pair, jnp.where(lo, dcap_a, dcap_b))
            doop_ref[:, 2 * pair * LANES:(2 * pair + 1) * LANES] = jnp.where(lo, 0.0, pltpu.roll(do_p, VDIM, 1)).astype(BF16)
            doop_ref[:, (2 * pair + 1) * LANES:(2 * pair + 2) * LANES] = jnp.where(lo, 0.0, do_p).astype(BF16)

        @pl.when(pl.program_id(0) == steps - 1)
        def _():
            for k in range(N_DEV):
                swout_ref[k] = gwout_ref[k * cols:(k + 1) * cols, :].astype(BF16)
                swba_ref[k] = gwba_ref[:, k * cols:(k + 1) * cols].astype(BF16)
                swbp_ref[k] = gwbp_ref[:, k * cols:(k + 1) * cols].astype(BF16)

    ins = (x2, tgt, o, zga, ypool, zgm, wba, wbp, wout, norm_final)
    in_specs = [_row_spec(tm, D_MODEL), _row_spec(tm, D_MODEL), _row_spec(tm, MLA_W), _row_spec(tm, MLA_W), _row_spec(tm, POOL_W),
                _row_spec(tm, 2 * D_MODEL), _full_spec(wba), _full_spec(wbp), _full_spec(wout), _full_spec(norm_final)]
    outs = [SDS((8, LANES), F32), SDS((T, D_MODEL), F32), SDS((T, 2 * D_MODEL), BF16), SDS((T, HW), BF16), SDS((T, MLA_W), BF16),
            SDS((HEADS // 2, 2, T), F32), SDS((T, POOL_W), F32),
            SDS((N_DEV, cols, D_MODEL), BF16), SDS((N_DEV, MLA_W, cols), BF16), SDS((N_DEV, POOL_W, cols), BF16), SDS((1, D_MODEL), F32)]
    out_specs = [_full_spec(outs[0]), _row_spec(tm, D_MODEL), _row_spec(tm, 2 * D_MODEL), _row_spec(tm, HW), _row_spec(tm, MLA_W),
                 pl.BlockSpec((HEADS // 2, 2, tm), lambda i: (0, 0, i)), _row_spec(tm, POOL_W),
                 _full_spec(outs[7]), _full_spec(outs[8]), _full_spec(outs[9]), _full_spec(outs[10])]
    return pl.pallas_call(
        body, name="tail", grid=(steps,), in_specs=in_specs, out_specs=out_specs, out_shape=outs,
        scratch_shapes=[pltpu.VMEM((D_MODEL, D_MODEL), F32), pltpu.VMEM((MLA_W, D_MODEL), F32), pltpu.VMEM((POOL_W, D_MODEL), F32)],
        compiler_params=_cp(("arbitrary",)),
    )(*ins)


def _attn_bwd(q_att, k_att, v_att, doop, lse_rows, dcap_rows, tq, hps, slabs, packed):
    T = q_att.shape[0]
    nq = T // tq
    n = len(slabs)
    groups = HEADS // hps
    head_lanes = [slice(h * LANES, (h + 1) * LANES) for h in range(hps)]

    def body(q_ref, k_ref, v_ref, doop_ref, lse_ref, dcap_ref, *rest):
        slab_refs, packed_ref = rest[:n], rest[n]
        dq_ref, dkv_ref, dkr_ref = rest[n + 1:n + 4]
        sum_refs, ptot_ref = rest[n + 4:2 * n + 4], rest[2 * n + 4]
        dq_acc = rest[2 * n + 5]
        rs = _ReduceScatter(slab_refs, packed_ref, sum_refs, ptot_ref, rest[2 * n + 6:])
        group, j = pl.program_id(0), pl.program_id(1)
        tick, last = group * nq + j, groups * nq - 1
        pl.when(tick == 0)(rs.start1)
        pl.when(tick == min(1, last))(rs.finish1_start2)
        pl.when(tick == min(3, last))(rs.relay2)
        mask = _chunk_mask(tq, tq, 0, True)
        lane = lax.broadcasted_iota(jnp.int32, (tq, LANES), 1)
        ks = [k_ref[:, hs] for hs in head_lanes]
        vs = [v_ref[:, hs] for hs in head_lanes]
        kts = [kh.T for kh in ks]

        @pl.when(j == 0)
        def _():
            dq_acc[...] = jnp.zeros_like(dq_acc)

        def step(i, carry, masked):
            rows = pl.ds(pl.multiple_of(i * tq, tq), tq)
            heads = range(hps)
            stat = lambda h: (h // 2, slice(h % 2, h % 2 + 1), rows)
            qhs = [q_ref[rows, hs] for hs in head_lanes]
            doops = [doop_ref[rows, hs] for hs in head_lanes]
            sts = [_mm_nt(ks[h], qhs[h]) for h in heads]
            dpts = [_mm_nt(vs[h], doops[h]) for h in heads]
            pts = [jnp.exp2(sts[h] - lse_ref[stat(h)]) for h in heads]
            if masked:
                pts = [jnp.where(mask, pt, 0.0) for pt in pts]
            dsts = [(pts[h] * (dpts[h] - dcap_ref[stat(h)])).astype(BF16) for h in heads]
            dvs = [_mm(pts[h].astype(BF16), doops[h]) for h in heads]
            dks = [_mm(dsts[h], qhs[h]) for h in heads]
            for h, hs in enumerate(head_lanes):
                dq_acc[hs, rows] += _mm(kts[h], dsts[h])
            return tuple((dk + dks[h], dv + dvs[h]) for h, (dk, dv) in enumerate(carry))

        zero = jnp.zeros((tq, LANES), F32)
        carry = step(j, ((zero, zero),) * hps, True)
        res = lax.fori_loop(j + 1, nq, functools.partial(step, masked=False), carry)
        dkr = None
        for (dk, dv), hs in zip(res, head_lanes):
            dk = dk * LN2
            dkv_ref[:, hs] = jnp.where(lane < NOPE, dk, dv).astype(BF16)
            dkr = dk if dkr is None else dkr + dk
        dkr_ref[0] = jnp.where((lane >= NOPE) & (lane < NOPE + ROPE), dkr, 0.0)

        @pl.when(j == nq - 1)
        def _():
            dq_ref[...] = (dq_acc[...] * SCALE).T.astype(BF16)

        pl.when(tick == last)(rs.finish2)

    kspec = pl.BlockSpec((tq, hps * LANES), lambda p, j: (j, p))
    qspec = pl.BlockSpec((T, hps * LANES), lambda p, j: (0, p))
    rspec = pl.BlockSpec((hps // 2, 2, T), lambda p, j: (p, 0, 0))
    sums = [SDS(s.shape[1:], F32) for s in slabs] + [SDS(packed.shape, F32)]
    return pl.pallas_call(
        body, name="attn_bwd", grid=(groups, nq),
        in_specs=[qspec, kspec, kspec, qspec, rspec, rspec] + [HBM_SPEC] * n + [_full_spec(packed)],
        out_specs=[qspec, kspec, pl.BlockSpec((1, tq, LANES), lambda p, j: (p, j, 0))] + [_full_spec(s) for s in sums],
        out_shape=[SDS((T, HW), BF16), SDS((T, HW), BF16), SDS((groups, T, LANES), F32)] + sums,
        scratch_shapes=[pltpu.VMEM((hps * LANES, T), F32)] + _rs_scratch([s.shape for s in sums[:-1]], packed.shape),
        compiler_params=_cp(("arbitrary", "arbitrary")),
    )(q_att, k_att, v_att, doop, lse_rows, dcap_rows, *slabs, packed)


def _rms_bwd(z, gain, dout):
    r = lax.rsqrt(jnp.mean(z * z, axis=-1, keepdims=True) + EPS)
    zr = z * r
    u = dout * gain
    return r * (u - zr * jnp.mean(u * zr, axis=-1, keepdims=True)), jnp.sum(dout * zr, axis=0, keepdims=True)


def _mla_bwd(dq_att, dkv_nat, dkr4, zfr, q_norm, wuq_pad, kv_norm, wukv, rc, rsa, rsb, tm):
    T = dq_att.shape[0]

    def body(dq_ref, dkv_ref, dkr_ref, zfr_ref, qn_ref, wuq_ref, kvn_ref, wukv_ref, c_ref, sa_ref, sb_ref,
             dfr_ref, gwuq_out, gwukv_out, gqn_ref, gkvn_ref, gwuq_ref, gwukv_ref):
        @pl.when(pl.program_id(0) == 0)
        def _():
            for ref in (gwuq_ref, gwukv_ref, gqn_ref, gkvn_ref):
                ref[...] = jnp.zeros_like(ref)

        c, sa, sb = c_ref[...], sa_ref[...], sb_ref[...]
        zq, zkv = zfr_ref[:, :Q_RANK], zfr_ref[:, Q_RANK:Q_RANK + KV_RANK]
        qn, kvn = qn_ref[...].reshape(1, Q_RANK), kvn_ref[...]
        dkv = dkv_ref[...]
        dckv = _mm_nt(dkv, wukv_ref[...])
        ckv = (zkv * lax.rsqrt(jnp.mean(zkv * zkv, axis=-1, keepdims=True) + EPS) * kvn).astype(BF16)
        gwukv_ref[...] += _mm_tn(ckv, dkv)
        cq = (zq * lax.rsqrt(jnp.mean(zq * zq, axis=-1, keepdims=True) + EPS) * qn).astype(BF16)
        dq = _rope(dq_ref[...].astype(F32), c, sa, sb, -1.0).astype(BF16)
        dzkv, gkvn = _rms_bwd(zkv, kvn, dckv)
        gkvn_ref[...] += gkvn
        gwuq_ref[...] += _mm_tn(cq, dq)
        dzq, gqn = _rms_bwd(zq, qn, _mm_nt(dq, wuq_ref[...]))
        gqn_ref[...] += gqn
        dkr = functools.reduce(lambda a, b: a + b, [dkr_ref[g] for g in range(dkr4.shape[0])])
        dfr_ref[:, :Q_RANK] = dzq.astype(BF16)
        dfr_ref[:, Q_RANK:Q_RANK + KV_RANK] = dzkv.astype(BF16)
        dfr_ref[:, Q_RANK + KV_RANK:] = _rope(dkr, c, sa, sb, -1.0).astype(BF16)

        @pl.when(pl.program_id(0) == pl.num_programs(0) - 1)
        def _():
            gwuq_out[...] = gwuq_ref[...].astype(BF16)
            gwukv_out[...] = gwukv_ref[...].astype(BF16)

    ins = (dq_att, dkv_nat, dkr4, zfr, q_norm, wuq_pad, kv_norm, wukv, rc, rsa, rsb)
    in_specs = [_row_spec(tm, HW), _row_spec(tm, HW), pl.BlockSpec((dkr4.shape[0], tm, LANES), lambda i: (0, i, 0)), _row_spec(tm, FRONT_W),
                _full_spec(q_norm), _full_spec(wuq_pad), _full_spec(kv_norm), _full_spec(wukv),
                _row_spec(tm, LANES), _row_spec(tm, LANES), _row_spec(tm, LANES)]
    outs = [SDS((T, FRONT_W), BF16), SDS((Q_RANK, HW), BF16), SDS((KV_RANK, HW), BF16), SDS((1, Q_RANK), F32), SDS((1, KV_RANK), F32)]
    out_specs = [_row_spec(tm, FRONT_W)] + [_full_spec(s) for s in outs[1:]]
    return pl.pallas_call(
        body, name="mla_bwd", grid=(T // tm,), in_specs=in_specs, out_specs=out_specs, out_shape=outs,
        scratch_shapes=[pltpu.VMEM((Q_RANK, HW), F32), pltpu.VMEM((KV_RANK, HW), F32)],
        compiler_params=_cp(("arbitrary",)),
    )(*ins)


_DZ_COLS = ((GM, ZTOT), (GA, UP), (UP, GP), (GP, GM), (ZQ, GA))


def _in_proj_bwd_x(dzs, x2, dh, norm_in, w_in_pad, tm, slabs):
    T = x2.shape[0]
    steps = T // tm
    n = len(slabs)

    def body(d0, d1, d2, d3, d4, x_ref, dh_ref, nin_ref, win_ref, *rest):
        slab_refs, (gx_ref, gnin_ref), sum_refs = rest[:n], rest[n:n + 2], rest[n + 2:2 * n + 2]
        rs = _ReduceScatter(slab_refs, None, sum_refs, None, rest[2 * n + 2:])
        step = pl.program_id(0)

        @pl.when(step == 0)
        def _():
            gnin_ref[...] = jnp.zeros_like(gnin_ref)
            rs.start1()

        pl.when(step == min(2, steps - 1))(rs.finish1_start2)
        pl.when(step == min(steps * 5 // 8, steps - 1))(rs.relay2)
        dhn = None
        for ref, (lo, hi) in zip((d0, d1, d2, d3, d4), _DZ_COLS):
            t = _mm(ref[...], win_ref[lo:hi, :])
            dhn = t if dhn is None else dhn + t
        dx, gnin = _rms_bwd(x_ref[...], nin_ref[...], dhn)
        gnin_ref[...] += gnin
        gx_ref[...] = dx + dh_ref[...]
        pl.when(step == steps - 1)(rs.finish2)

    in_specs = [_row_spec(tm, hi - lo) for lo, hi in _DZ_COLS] + [_row_spec(tm, D_MODEL), _row_spec(tm, D_MODEL),
                                                                  _full_spec(norm_in), _full_spec(w_in_pad)] + [HBM_SPEC] * n
    sums = [SDS(s.shape[1:], F32) for s in slabs]
    outs = [SDS((T, D_MODEL), F32), SDS((1, D_MODEL), F32)] + sums
    return pl.pallas_call(
        body, name="in_proj_bwd_x", grid=(steps,), in_specs=in_specs,
        out_specs=[_row_spec(tm, D_MODEL), _full_spec(outs[1])] + [_full_spec(s) for s in sums],
        out_shape=outs, scratch_shapes=_rs_scratch([s.shape for s in sums], None), compiler_params=_cp(("arbitrary",)),
    )(*dzs, x2, dh, norm_in, w_in_pad, *slabs)


SLAB_ROWS = IN_TOTAL // N_DEV


def _slab_segments(k):
    cuts = [(0, ZKR_ORIG, 0), (ZKR_ORIG, ZKR_ORIG + ROPE, NOPE), (ZKR_ORIG + ROPE, IN_TOTAL, LANES - ROPE)]
    lo, hi = k * SLAB_ROWS, (k + 1) * SLAB_ROWS
    return [(max(lo, a) - lo, max(lo, a) + shift, min(hi, b) - max(lo, a)) for a, b, shift in cuts if min(hi, b) > max(lo, a)]


def _in_proj_bwd_w(dzs, hn, tm):
    T = hn.shape[0]
    steps = T // tm

    def body(d0, d1, d2, d3, d4, hn_ref, slab_ref, acc_ref):
        @pl.when(pl.program_id(0) == 0)
        def _():
            acc_ref[...] = jnp.zeros_like(acc_ref)

        hn_v = hn_ref[...]
        for ref, (lo, hi) in zip((d0, d1, d2, d3, d4), _DZ_COLS):
            acc_ref[lo:hi, :] += _mm_tn(ref[...], hn_v)

        @pl.when(pl.program_id(0) == steps - 1)
        def _():
            for k in range(N_DEV):
                for at, src, rows in _slab_segments(k):
                    slab_ref[k, at:at + rows, :] = acc_ref[src:src + rows, :].astype(BF16)

    in_specs = [_row_spec(tm, hi - lo) for lo, hi in _DZ_COLS] + [_row_spec(tm, D_MODEL)]
    out = SDS((N_DEV, SLAB_ROWS, D_MODEL), BF16)
    return pl.pallas_call(
        body, name="in_proj_bwd_w", grid=(steps,), in_specs=in_specs, out_specs=_full_spec(out), out_shape=out,
        scratch_shapes=[pltpu.VMEM((ZTOT, D_MODEL), F32)], compiler_params=_cp(("arbitrary",)),
    )(*dzs, hn)


def _local_step(x2, tgt, norm_in, w_in_pad, q_norm, wuq_pad, kv_norm, wukv, pool_w, pool_scale, late_shards, norm_final):
    T = x2.shape[0]
    tm, tm_small, tq = min(512, T), min(256, T), min(512, T)
    heads_fwd, heads_bwd = 4, 4
    row = lambda v: v.reshape(1, -1)
    rc, rsa, rsb = _rope_tables(T)

    hn, zgm, zga, zup, zgp, zfr, q_att, k_att, v_att, vt_att, w_ba, w_bp, w_out = _in_proj(
        x2, row(norm_in), w_in_pad, q_norm, wuq_pad, row(kv_norm), wukv, rc, rsa, rsb, tm, late_shards)
    w_out = w_out.reshape(D_MODEL, D_MODEL)
    o, lse_rows = _attn_fwd(q_att, k_att, vt_att, tq, heads_fwd)
    ypool = _pool_fwd(zup, zgp, pool_w, row(pool_scale))
    loss8, dh, dgm, doop, dga, dcap_rows, dyp, *slabs, g_nf = _tail(
        x2, tgt, o, zga, ypool, zgm, w_ba, w_bp, w_out, row(norm_final), tm_small)
    dup, dgp, g_pool_w, g_pool_scale = _pool_bwd(zup, zgp, dyp, pool_w, row(pool_scale))

    early = [g_pool_w, g_pool_scale, g_nf, loss8[0]]
    packed = jnp.concatenate([_pack_rows(a) for a in early], axis=0)
    dq_att, dkv_nat, dkr4, s_wout, s_wba, s_wbp, tot_early = _attn_bwd(
        q_att, k_att, v_att, doop, lse_rows, dcap_rows, tq, heads_bwd, slabs, packed)
    at_pool_w, at_pool_scale, at_nf, at_loss = _packed_at(early)

    dfr, g_wuq_pad, g_wukv, g_qn, g_kvn = _mla_bwd(
        dq_att, dkv_nat, dkr4, zfr, q_norm, wuq_pad, row(kv_norm), wukv, rc, rsa, rsb, tm)
    dzs = (dgm, dga, dup, dgp, dfr)
    slabs = [_in_proj_bwd_w(dzs, hn, tm), g_wuq_pad.reshape(N_DEV, Q_RANK // N_DEV, HW), g_wukv.reshape(N_DEV, KV_RANK // N_DEV, HW)]
    grad_x, g_nin, s_win, s_wuq, s_wukv = _in_proj_bwd_x(dzs, x2, dh, row(norm_in), w_in_pad, tm_small, slabs)
    late = [g_nin, g_qn, g_kvn]
    tot_late = _all_reduce_rows(late)
    at_nin, at_qn, at_kvn = _packed_at(late)

    grads = dict(norm_in=(tot_late, at_nin), w_in=s_win, q_norm=(tot_late, at_qn), w_uq=s_wuq, kv_norm=(tot_late, at_kvn), w_ukv=s_wukv,
                 pool_w=(tot_early, at_pool_w), pool_scale=(tot_early, at_pool_scale), w_branch_attn=s_wba, w_branch_pool=s_wbp,
                 w_out=s_wout, norm_final=(tot_early, at_nf))
    return tot_early[at_loss, 0], grad_x, grads


MESH_ID = pl.DeviceIdType.MESH
VMEM_SPEC = pl.BlockSpec(memory_space=pltpu.VMEM)
HBM_SPEC = pl.BlockSpec(memory_space=pl.ANY)


def _mesh_pos():
    return lax.axis_index("x"), lax.axis_index("y"), lax.axis_index("c")


BF16_TILE_ROWS = 16


def _half_rows(rows):
    cut = -(-(rows // 2) // BF16_TILE_ROWS) * BF16_TILE_ROWS
    return pl.ds(0, cut), pl.ds(cut, rows - cut)


def _slot(px, py, pc):
    return 4 * px + 2 * py + pc


def _staged_shape(shape):
    return (shape[0], shape[1] * LANES) if len(shape) == 3 else tuple(shape)


def _all_gather_bf16(shards):
    n = len(shards)
    staged = [_staged_shape(s.shape) for s in shards]

    def body(*refs):
        ins, outs = refs[:n], refs[n:2 * n]
        land0, scratch = refs[2 * n], refs[2 * n + 1:]
        wpad_ref = outs[0]
        ag = _AllGather(ins, (land0,) + tuple(outs[1:]), scratch)
        ag.start()
        ag.forward()
        ag.finish()
        wpad_ref[ZKR:GA, :] = jnp.zeros((GA - ZKR, D_MODEL), BF16)
        for k in range(N_DEV):
            for at, dst, rows in _slab_segments(k):
                wpad_ref[dst:dst + rows, :] = land0[k, at:at + rows, :]

    return pl.pallas_call(
        body, name="all_gather_weights",
        in_specs=[VMEM_SPEC] * n, out_specs=[VMEM_SPEC] + [HBM_SPEC] * (n - 1),
        out_shape=[SDS((ZTOT, D_MODEL), BF16)] + [SDS((N_DEV,) + s, BF16) for s in staged[1:]],
        scratch_shapes=[pltpu.VMEM((N_DEV,) + staged[0], BF16)] + _ag_scratch(staged),
        compiler_params=_cp(),
    )(*shards)


def _ag_scratch(shapes):
    n = len(shapes)
    dma = pltpu.SemaphoreType.DMA
    return [pltpu.VMEM(tuple(s), BF16) for s in shapes] + [dma((_AllGather.COPIES * n,)), dma((_AllGather.COPIES * n,)), dma((n,))]


class _AllGather:
    COPIES = 8

    def __init__(self, in_refs, dest_refs, scratch):
        n = self.n = len(in_refs)
        self.ins, self.dests, self.stage = in_refs, dest_refs, scratch[:n]
        self.send_sems, self.recv_sems, self.local_sems = scratch[n:]
        x, y, c = _mesh_pos()
        self.c, self.me, self.sibling = c, (x, y, c), (x, y, 1 - c)
        self.xn, self.yn, self.diag = (1 - x, y), (x, 1 - y), (1 - x, 1 - y)

    def _halves(self, a):
        return _half_rows(self.stage[a].shape[0])

    def _copy(self, a, k, block, to, from_stage=False, rows=None):
        dst = self.dests[a].at[_slot(*block)]
        src = self.stage[a] if from_stage else dst
        if rows is not None:
            src, dst = src.at[rows], dst.at[rows]
        return pltpu.make_async_remote_copy(
            src_ref=src, dst_ref=dst, send_sem=self.send_sems.at[self.COPIES * a + k],
            recv_sem=self.recv_sems.at[self.COPIES * a + k], device_id=to, device_id_type=MESH_ID)

    def _mine(self):
        return [pltpu.make_async_copy(self.stage[a], self.dests[a].at[_slot(*self.me)], self.local_sems.at[a]) for a in range(self.n)]

    def _first(self, a):
        return [self._copy(a, 0, self.me, self.sibling, True), self._copy(a, 1, self.me, (*self.xn, self.c), True),
                self._copy(a, 2, self.me, (*self.yn, self.c), True)]

    def _relays(self, a):
        lo, hi = self._halves(a)
        return [self._copy(a, 3, (*self.xn, self.c), (*self.yn, self.c), rows=lo),
                self._copy(a, 4, (*self.yn, self.c), (*self.xn, self.c), rows=hi)]

    def _passes(self, a):
        return [self._copy(a, 5 + j, (*chip, self.c), self.sibling) for j, chip in enumerate((self.xn, self.yn, self.diag))]

    def start(self):
        for a in range(self.n):
            src, dst = self.ins[a], self.stage[a]
            if src.shape == dst.shape:
                dst[...] = src[...].astype(BF16)
            else:
                if src.shape[2] < LANES:
                    dst[...] = jnp.zeros(dst.shape, BF16)
                for h in range(src.shape[1]):
                    dst[:, h * LANES:h * LANES + src.shape[2]] = src[:, h, :].astype(BF16)
        for cp in self._mine():
            cp.start()
        for a in range(self.n):
            for cp in self._first(a):
                cp.start()

    def forward(self):
        for a in range(self.n):
            relays, passes = self._relays(a), self._passes(a)
            for j, chip in enumerate((self.xn, self.yn)):
                self._copy(a, 1 + j, (*chip, self.c), self.me).wait_recv()
                relays[j].start()
                passes[j].start()

    def finish(self):
        for a in range(self.n):
            lo, hi = self._halves(a)
            self._copy(a, 3, (*self.diag, self.c), self.me, rows=lo).wait_recv()
            self._copy(a, 4, (*self.diag, self.c), self.me, rows=hi).wait_recv()
            self._passes(a)[2].start()
        for a in range(self.n):
            self._copy(a, 0, self.sibling, self.me).wait_recv()
            for j, chip in enumerate((self.xn, self.yn, self.diag)):
                self._copy(a, 5 + j, (*chip, 1 - self.c), self.me).wait_recv()
            for cp in self._first(a) + self._relays(a) + self._passes(a):
                cp.wait_send()
        for cp in self._mine():
            cp.wait()


N_CHIPS = 4


def _all_reduce_rows(parts):
    m = len(parts)
    firsts = _packed_at(parts)
    rows = firsts[-1] + -(-(parts[-1].size // LANES) // SUBLANES) * SUBLANES

    def body(*refs):
        part_refs, tot_ref, packed_ref = refs[:m], refs[m], refs[m + 1]
        packed_ref[...] = jnp.zeros_like(packed_ref)
        for ref, first in zip(part_refs, firsts):
            for j in range(ref.shape[1] // LANES):
                packed_ref[first + j:first + j + 1, :] = ref[:, j * LANES:(j + 1) * LANES]
        rs = _ReduceScatter((), packed_ref, (), tot_ref, refs[m + 2:])
        rs.start1()
        rs.finish1_start2()
        rs.relay2()
        rs.finish2()

    return pl.pallas_call(
        body, name="all_reduce_rows", in_specs=[VMEM_SPEC] * m, out_specs=VMEM_SPEC, out_shape=SDS((rows, LANES), F32),
        scratch_shapes=[pltpu.VMEM((rows, LANES), F32)] + _rs_scratch([], (rows, LANES)), compiler_params=_cp(),
    )(*parts)


def _rs_scratch(shapes, packed_shape):
    n = len(shapes)
    n1, n2 = N_CHIPS * n + 1, _ReduceScatter.L2_COPIES * n + N_CHIPS - 1
    dma = pltpu.SemaphoreType.DMA
    packed = [] if packed_shape is None else [pltpu.VMEM(packed_shape, F32), pltpu.VMEM((N_CHIPS,) + tuple(packed_shape), F32)]
    return ([pltpu.VMEM((N_CHIPS,) + tuple(s), BF16) for s in shapes] * 2 + [pltpu.VMEM((N_CHIPS - 1,) + tuple(s), BF16) for s in shapes] * 2
            + packed + [dma((max(N_CHIPS * n, 1),)), dma((n1,)), dma((n1,)), dma((n2,)), dma((n2,))])


class _ReduceScatter:
    L2_COPIES = 6

    def __init__(self, slab_refs, packed_ref, out_refs, ptot_ref, scratch):
        n = self.n = len(slab_refs)
        self.slabs, self.packed, self.outs, self.ptot = slab_refs, packed_ref, out_refs, ptot_ref
        self.own1, self.land1, self.send2, self.land2 = (scratch[k * n:(k + 1) * n] for k in range(4))
        rest = scratch[4 * n:]
        if packed_ref is not None:
            self.pland1, self.pland2 = rest[:2]
            rest = rest[2:]
        self.loc_sems, self.send1_sems, self.recv1_sems, self.send2_sems, self.recv2_sems = rest
        self.x, self.y, self.c = _mesh_pos()

    def _chip(self, r):
        return (1 - self.x if r & 2 else self.x, 1 - self.y if r & 1 else self.y)

    @staticmethod
    def _remote(src, dst, send_sem, recv_sem, to):
        return pltpu.make_async_remote_copy(src_ref=src, dst_ref=dst, send_sem=send_sem, recv_sem=recv_sem, device_id=to,
                                            device_id_type=MESH_ID)

    def _copies1(self):
        c, sibling = self.c, (self.x, self.y, 1 - self.c)
        cps = []
        for a in range(self.n):
            for r in range(N_CHIPS):
                k = N_CHIPS * a + r
                cps.append(pltpu.make_async_copy(self.slabs[a].at[_slot(*self._chip(r), c)], self.own1[a].at[r], self.loc_sems.at[k]))
                cps.append(self._remote(self.slabs[a].at[_slot(*self._chip(r), 1 - c)], self.land1[a].at[r],
                                        self.send1_sems.at[k], self.recv1_sems.at[k], sibling))
        if self.packed is not None:
            k = N_CHIPS * self.n
            cps.append(self._remote(self.packed, self.pland1, self.send1_sems.at[k], self.recv1_sems.at[k], sibling))
        return cps

    def _halves(self, a):
        return _half_rows(self.send2[a].shape[1])

    def _copy2(self, a, k):
        lo, hi = self._halves(a)
        xn, yn = (*self._chip(2), self.c), (*self._chip(1), self.c)
        slot, rows, to = [(1, lo, xn), (2, lo, xn), (0, hi, yn), (2, hi, yn), (0, lo, yn), (1, hi, xn)][k]
        return self._remote(self.send2[a].at[slot].at[rows], self.land2[a].at[slot].at[rows],
                            self.send2_sems.at[self.L2_COPIES * a + k], self.recv2_sems.at[self.L2_COPIES * a + k], to)

    def _copies2_packed(self):
        base = self.L2_COPIES * self.n - 1
        return [self._remote(self.pland2.at[0], self.pland2.at[r], self.send2_sems.at[base + r], self.recv2_sems.at[base + r],
                             (*self._chip(r), self.c)) for r in range(1, N_CHIPS)]

    def start1(self):
        for cp in self._copies1():
            cp.start()

    def finish1_start2(self):
        for cp in self._copies1():
            cp.wait()
        for a in range(self.n):
            self.outs[a][...] = self.own1[a][0].astype(F32) + self.land1[a][0].astype(F32)
            for r in range(1, N_CHIPS):
                self.send2[a][r - 1] = (self.own1[a][r].astype(F32) + self.land1[a][r].astype(F32)).astype(BF16)
            for k in range(4):
                self._copy2(a, k).start()
        if self.packed is not None:
            self.pland2[0] = self.packed[...] + self.pland1[...]
            for cp in self._copies2_packed():
                cp.start()

    def relay2(self):
        for a in range(self.n):
            lo, hi = self._halves(a)
            s2, l2 = self.send2[a], self.land2[a]
            self._copy2(a, 1).wait_recv()
            s2[0, lo] = (s2[0, lo].astype(F32) + l2[2, lo].astype(F32)).astype(BF16)
            self._copy2(a, 4).start()
            self._copy2(a, 3).wait_recv()
            s2[1, hi] = (s2[1, hi].astype(F32) + l2[2, hi].astype(F32)).astype(BF16)
            self._copy2(a, 5).start()

    def finish2(self):
        for a in range(self.n):
            for k in (0, 2, 4, 5):
                self._copy2(a, k).wait_recv()
            for k in range(self.L2_COPIES):
                self._copy2(a, k).wait_send()
            l2 = self.land2[a]
            self.outs[a][...] = self.outs[a][...] + (l2[0].astype(F32) + l2[1].astype(F32))
        if self.packed is not None:
            for cp in self._copies2_packed():
                cp.wait()
            p2 = self.pland2
            self.ptot[...] = (p2[0] + p2[1]) + (p2[2] + p2[3])


def _adamw(ws, gs, ms, vs, rewrite):
    n = len(ws)
    first_row = [g[1] if isinstance(g, tuple) else None for g in gs]
    gs = [g[0] if isinstance(g, tuple) else g for g in gs]
    regrouped = [k for k in range(n) if first_row[k] is not None or gs[k].shape != ws[k].shape or rewrite[k]]
    assert all(w.shape[-1] % LANES == 0 for w, row in zip(ws, first_row) if row is not None)

    def body(*refs):
        for k in range(n):
            w_ref, g_ref, m_ref, v_ref, d_ref, nm_ref, nv_ref = (refs[j * n + k] for j in range(7))
            row = first_row[k]
            windows = [(..., ...)]
            if row is not None and len(w_ref.shape) == 1:
                windows = [(pl.ds(j * LANES, LANES), row + j) for j in range(w_ref.shape[0] // LANES)]
            elif row is not None:
                windows = [(..., pl.ds(row, w_ref.shape[0]))]
            elif g_ref.shape != w_ref.shape:
                windows = [((slice(None), h), (slice(None), slice(h * LANES, h * LANES + w_ref.shape[2]))) for h in range(w_ref.shape[1])]
            if k in regrouped:
                g_out_ref = refs[7 * n + regrouped.index(k)]
            for at, g_at in windows:
                w, g, m, v = w_ref[at], g_ref[g_at], m_ref[at], v_ref[at]
                m = ADAM_B1 * m + (1.0 - ADAM_B1) * g
                v = ADAM_B2 * v + (1.0 - ADAM_B2) * jnp.square(g)
                m_hat = m / (1.0 - ADAM_B1 ** ADAM_STEP)
                v_hat = v / (1.0 - ADAM_B2 ** ADAM_STEP)
                d_ref[at] = -ADAM_LR * (m_hat / (jnp.sqrt(v_hat) + ADAM_EPS) + ADAM_WD * w)
                nm_ref[at] = m
                nv_ref[at] = v
                if k in regrouped:
                    g_out_ref[at] = g

    shapes = [SDS(w.shape, F32) for w in ws]
    outs = pl.pallas_call(
        body, name="adamw", in_specs=[VMEM_SPEC] * (4 * n), out_specs=[VMEM_SPEC] * (3 * n + len(regrouped)),
        out_shape=shapes * 3 + [shapes[k] for k in regrouped], compiler_params=_cp(),
    )(*ws, *gs, *ms, *vs)
    grads = list(gs)
    for k, g in zip(regrouped, outs[3 * n:]):
        grads[k] = g
    return outs[:n], outs[n:2 * n], outs[2 * n:3 * n], grads


WEIGHTS = ("norm_in", "w_in", "q_norm", "w_uq", "kv_norm", "w_ukv", "pool_w", "pool_scale", "w_branch_attn", "w_branch_pool",
           "w_out", "norm_final")
SUBLANES = 8


def _pack_rows(a):
    a = a.reshape(-1, LANES)
    return jnp.pad(a, ((0, -a.shape[0] % SUBLANES), (0, 0)))


def _packed_at(like):
    out, row = [], 0
    for a in like:
        out.append(row)
        rows = a.size // LANES
        row += rows + (-rows % SUBLANES)
    return out


def kernel(x, norm_in, w_in, q_norm, w_uq, kv_norm, w_ukv, pool_w, pool_scale, w_branch_attn, w_branch_pool, w_out, norm_final, loss_target, m_norm_in, m_w_in, m_q_norm, m_w_uq, m_kv_norm, m_w_ukv, m_pool_w, m_pool_scale, m_w_branch_attn, m_w_branch_pool, m_w_out, m_norm_final, v_norm_in, v_w_in, v_q_norm, v_w_uq, v_kv_norm, v_w_ukv, v_pool_w, v_pool_scale, v_w_branch_attn, v_w_branch_pool, v_w_out, v_norm_final):
    w = dict(norm_in=norm_in, w_in=w_in, q_norm=q_norm, w_uq=w_uq, kv_norm=kv_norm, w_ukv=w_ukv, pool_w=pool_w, pool_scale=pool_scale,
             w_branch_attn=w_branch_attn, w_branch_pool=w_branch_pool, w_out=w_out, norm_final=norm_final)
    m = dict(norm_in=m_norm_in, w_in=m_w_in, q_norm=m_q_norm, w_uq=m_w_uq, kv_norm=m_kv_norm, w_ukv=m_w_ukv, pool_w=m_pool_w,
             pool_scale=m_pool_scale, w_branch_attn=m_w_branch_attn, w_branch_pool=m_w_branch_pool, w_out=m_w_out, norm_final=m_norm_final)
    v = dict(norm_in=v_norm_in, w_in=v_w_in, q_norm=v_q_norm, w_uq=v_w_uq, kv_norm=v_kv_norm, w_ukv=v_w_ukv, pool_w=v_pool_w,
             pool_scale=v_pool_scale, w_branch_attn=v_w_branch_attn, w_branch_pool=v_w_branch_pool, w_out=v_w_out, norm_final=v_norm_final)

    def as2d(name, a):
        if name == "w_in":
            return a.T
        if name in ("w_uq", "w_ukv"):
            return a
        return a.reshape(-1, GROUP) if name == "pool_w" else a

    def unshape(name, a):
        return a.T if name == "w_in" else a.reshape(w[name].shape)

    w_in_pad, w_uq_full, w_ukv_full = _all_gather_bf16([as2d(k, w[k]) for k in ("w_in", "w_uq", "w_ukv")])
    loss, grad_x, g2d = _local_step(
        x.reshape(x.shape[1:]), loss_target.reshape(x.shape[1:]), norm_in, w_in_pad, q_norm,
        w_uq_full.reshape(Q_RANK, HW), kv_norm, w_ukv_full.reshape(KV_RANK, HW),
        pool_w, pool_scale, [w_branch_attn, w_branch_pool, w_out], norm_final)

    deltas, new_m, new_v, grads = _adamw([as2d(k, w[k]) for k in WEIGHTS], [g2d[k] for k in WEIGHTS],
                                         [as2d(k, m[k]) for k in WEIGHTS], [as2d(k, v[k]) for k in WEIGHTS],
                                         [k in ("w_in", "w_branch_attn", "w_branch_pool", "w_out") for k in WEIGHTS])
    shaped = lambda arrs: [unshape(k, a) for k, a in zip(WEIGHTS, arrs)]
    return (loss, grad_x.reshape(x.shape), *shaped(grads), *shaped(deltas), *shaped(new_m), *shaped(new_v))
```

```python
import functools

import jax
import jax.numpy as jnp
import numpy as np
from jax import lax
from jax.experimental import pallas as pl
from jax.experimental.pallas import tpu as pltpu

F32 = jnp.float32
BF16 = jnp.bfloat16
SDS = jax.ShapeDtypeStruct

D_MODEL = 1024
HEADS = 8
NOPE = 64
ROPE = 32
VDIM = 64
Q_RANK = 384
KV_RANK = 256
MLA_W = HEADS * VDIM
POOL_W = 512
POOL_GROUPS = 4
GROUP = POOL_W // POOL_GROUPS
CHUNK = 64
ROPE_THETA = 10000.0
EPS = 1e-6
SCALE = (NOPE + ROPE) ** -0.5
LOG2E = 1.4426950408889634
LN2 = 0.6931471805599453
QK_SCALE_LOG2 = SCALE * LOG2E
IN_TOTAL = 4256
ADAM_LR, ADAM_B1, ADAM_B2, ADAM_EPS, ADAM_WD, ADAM_STEP = 0.001, 0.9, 0.999, 1e-08, 0.01, 10

N_DEV = 8
LANES = 128
HEAD_PAD = LANES
HW = HEADS * HEAD_PAD

ZQ, ZKV, ZKR, GA, UP, GP, GM, ZTOT = 0, 384, 640, 768, 1280, 1792, 2304, 4352
FRONT_W = GA
ZKR_ORIG = 640

VMEM_LIMIT = 62 * 1024 * 1024


def _cp(sem=None, **kw):
    if sem is not None:
        kw["dimension_semantics"] = sem
    return pltpu.CompilerParams(vmem_limit_bytes=VMEM_LIMIT, **kw)


def _mm(a, b):
    return lax.dot_general(a, b, (((1,), (0,)), ((), ())), preferred_element_type=F32)


def _mm_nt(a, b):
    return lax.dot_general(a, b, (((1,), (1,)), ((), ())), preferred_element_type=F32)


def _mm_tn(a, b):
    return lax.dot_general(a, b, (((0,), (0,)), ((), ())), preferred_element_type=F32)


def _row_spec(tm, w):
    return pl.BlockSpec((tm, w), lambda i: (i, 0))


def _full_spec(a):
    nd = len(a.shape)
    return pl.BlockSpec(a.shape, lambda *_: (0,) * nd)


def _rope(v, c, sa, sb, sign):
    n = v.shape[-1]
    reps = n // LANES
    if reps > 1:
        c, sa, sb = (jnp.tile(t, (1, reps)) for t in (c, sa, sb))
    up = pltpu.roll(v, n - ROPE // 2, 1)
    dn = pltpu.roll(v, ROPE // 2, 1)
    return v * c + sign * (up * sa + dn * sb)


def _rope_tables(T):
    half = ROPE // 2
    inv_freq = np.float32(ROPE_THETA) ** (-np.arange(half, dtype=np.float32) / np.float32(half))
    ang = np.arange(T, dtype=np.float32)[:, None] * inv_freq[None, :].astype(np.float32)
    cos, sin = np.cos(ang.astype(np.float64)).astype(np.float32), np.sin(ang.astype(np.float64)).astype(np.float32)
    z16 = np.zeros((T, half), np.float32)
    z32 = np.zeros((T, LANES - NOPE - ROPE), np.float32)
    c = np.concatenate([np.ones((T, NOPE), np.float32), cos, cos, z32], axis=1)
    sa = np.concatenate([np.zeros((T, NOPE), np.float32), -sin, z16, z32], axis=1)
    sb = np.concatenate([np.zeros((T, NOPE), np.float32), z16, sin, z32], axis=1)
    return jnp.asarray(c), jnp.asarray(sa), jnp.asarray(sb)


def _silu_parts(g):
    sg = jax.nn.sigmoid(g)
    return g * sg, sg + g * sg * (1.0 - sg)


def _in_proj(x2, norm_in, w_in_pad, q_norm, wuq_pad, kv_norm, wukv, rc, rsa, rsb, tm, late_shards):
    T = x2.shape[0]
    steps = T // tm
    n = len(late_shards)

    def body(x_ref, nin_ref, win_ref, qn_ref, wuq_ref, kvn_ref, wukv_ref, c_ref, sa_ref, sb_ref, *rest):
        hn_ref, zgm_ref, zga_ref, zup_ref, zgp_ref, zfr_ref, q_ref, k_ref, v_ref, vt_ref = rest[n:n + 10]
        ag = _AllGather(rest[:n], rest[n + 10:2 * n + 10], rest[2 * n + 10:])
        step = pl.program_id(0)
        pl.when(step == 0)(ag.start)
        pl.when(step == min(3, steps - 1))(ag.forward)
        xf = x_ref[...]
        r = lax.rsqrt(jnp.mean(xf * xf, axis=-1, keepdims=True) + EPS)
        hn = (xf * r * nin_ref[...]).astype(BF16)
        hn_ref[...] = hn
        zfr = _mm_nt(hn, win_ref[ZQ:GA, :])
        zfr_ref[...] = zfr
        zq, zkv, zkr = zfr[:, ZQ:ZKV], zfr[:, ZKV:ZKR], zfr[:, ZKR:GA]
        c, sa, sb = c_ref[...], sa_ref[...], sb_ref[...]
        rq = lax.rsqrt(jnp.mean(zq * zq, axis=-1, keepdims=True) + EPS)
        cq = (zq * rq * qn_ref[...].reshape(1, Q_RANK)).astype(BF16)
        rkv = lax.rsqrt(jnp.mean(zkv * zkv, axis=-1, keepdims=True) + EPS)
        ckv = (zkv * rkv * kvn_ref[...]).astype(BF16)
        zga_ref[...] = _mm_nt(hn, win_ref[GA:UP, :])
        zup_ref[...] = _mm_nt(hn, win_ref[UP:GP, :])
        zgp_ref[...] = _mm_nt(hn, win_ref[GP:GM, :])
        q_raw = _mm(cq, wuq_ref[...])
        kv = _mm(ckv, wukv_ref[...])
        zgm_ref[...] = _mm_nt(hn, win_ref[GM:ZTOT, :])
        q = _rope(q_raw, c, sa, sb, 1.0)
        q_ref[...] = (q * QK_SCALE_LOG2).astype(BF16)
        kr = _rope(zkr, c, sa, sb, 1.0)
        lane = lax.broadcasted_iota(jnp.int32, kv.shape, 1) % LANES
        k_ref[...] = jnp.where(lane < NOPE, kv, jnp.tile(kr, (1, HEADS))).astype(BF16)
        v = jnp.where(lane < NOPE, 1.0, kv).astype(BF16)
        v_ref[...] = v
        vt_ref[...] = v.T
        pl.when(step == steps - 1)(ag.finish)

    ins = (x2, norm_in, w_in_pad, q_norm, wuq_pad, kv_norm, wukv, rc, rsa, rsb)
    in_specs = [_row_spec(tm, D_MODEL), _full_spec(norm_in), _full_spec(w_in_pad), _full_spec(q_norm), _full_spec(wuq_pad),
                _full_spec(kv_norm), _full_spec(wukv), _row_spec(tm, LANES), _row_spec(tm, LANES), _row_spec(tm, LANES)]
    widths = [(D_MODEL, BF16), (ZTOT - GM, F32), (UP - GA, F32), (GP - UP, F32), (GM - GP, F32), (FRONT_W, F32),
              (HW, BF16), (HW, BF16), (HW, BF16)]
    return pl.pallas_call(
        body, name="in_proj", grid=(steps,), in_specs=in_specs + [_full_spec(s) for s in late_shards],
        out_specs=[_row_spec(tm, w) for w, _ in widths] + [pl.BlockSpec((HW, tm), lambda i: (0, i))] + [HBM_SPEC] * n,
        out_shape=[SDS((T, w), dt) for w, dt in widths] + [SDS((HW, T), BF16)]
        + [SDS((N_DEV,) + s.shape, BF16) for s in late_shards],
        scratch_shapes=_ag_scratch([s.shape for s in late_shards]), compiler_params=_cp(("arbitrary",)),
    )(*ins, *late_shards)


def _chunk_mask(n_q, n_k, q_off, transposed):
    shape = (n_k, n_q) if transposed else (n_q, n_k)
    q = (lax.broadcasted_iota(jnp.int32, shape, 1 if transposed else 0) + q_off) // CHUNK
    k = lax.broadcasted_iota(jnp.int32, shape, 0 if transposed else 1) // CHUNK
    return k <= q


def _store_pair_rows(ref, k, pair):
    t = pair.T
    ref[k, 0:1, :] = t[0:1, :]
    ref[k, 1:2, :] = t[VDIM:VDIM + 1, :]


def _attn_fwd(q_att, k_att, vt_att, tq, hps):
    T = q_att.shape[0]
    head_lanes = [slice(h * LANES, (h + 1) * LANES) for h in range(hps)]

    def body(q_ref, k_ref, vt_ref, o_ref, lser_ref):
        i = pl.program_id(1)
        mask = _chunk_mask(tq, tq, 0, True)
        lane = lax.broadcasted_iota(jnp.int32, (tq, LANES), 1)
        qs = [q_ref[:, hs] for hs in head_lanes]

        def step(j, carry, masked):
            off = pl.multiple_of(j * tq, tq)
            sts = [_mm_nt(k_ref[pl.ds(off, tq), hs], qh) for qh, hs in zip(qs, head_lanes)]
            if masked:
                sts = [jnp.where(mask, st, -jnp.inf) for st in sts]
            ms = [jnp.maximum(m, jnp.max(st, axis=0, keepdims=True)) for (m, _), st in zip(carry, sts)]
            pts = [jnp.exp2(st - m_new).astype(BF16) for st, m_new in zip(sts, ms)]
            return tuple((m_new, jnp.exp2(m - m_new) * acc + _mm(vt_ref[hs, pl.ds(off, tq)], pt))
                         for (m, acc), m_new, pt, hs in zip(carry, ms, pts, head_lanes))

        init = ((jnp.full((1, tq), -jnp.inf, F32), jnp.zeros((LANES, tq), F32)),) * hps
        res = step(i, lax.fori_loop(0, i, functools.partial(step, masked=False), init), True)
        for pair in range(hps // 2):
            (ma, acca), (mb, accb) = res[2 * pair], res[2 * pair + 1]
            la, lb = acca[:1], accb[:1]
            oa, ob = (acca / la).T, (accb / lb).T
            o_ref[:, pair * LANES:(pair + 1) * LANES] = jnp.where(lane < VDIM, pltpu.roll(oa, VDIM, 1), ob)
            lser_ref[pair, 0:1, :] = ma + jnp.log2(la)
            lser_ref[pair, 1:2, :] = mb + jnp.log2(lb)

    qspec = pl.BlockSpec((tq, hps * LANES), lambda p, i: (i, p))
    kspec = pl.BlockSpec((T, hps * LANES), lambda p, i: (0, p))
    vspec = pl.BlockSpec((hps * LANES, T), lambda p, i: (p, 0))
    ospec = pl.BlockSpec((tq, hps * VDIM), lambda p, i: (i, p))
    return pl.pallas_call(
        body, name="attn_fwd", grid=(HEADS // hps, T // tq), in_specs=[qspec, kspec, vspec],
        out_specs=[ospec, pl.BlockSpec((hps // 2, 2, tq), lambda p, i: (p, 0, i))],
        out_shape=[SDS((T, MLA_W), F32), SDS((HEADS // 2, 2, T), F32)],
        compiler_params=_cp(("parallel", "parallel")),
    )(q_att, k_att, vt_att)


def _pick(g, vals):
    out = vals[-1]
    for k in range(len(vals) - 2, -1, -1):
        out = jnp.where(g == k, vals[k], out)
    return out


def _window_sum(u, g, forward):
    T = u.shape[0]
    row = lax.broadcasted_iota(jnp.int32, u.shape, 0)

    def sh(s, k):
        if forward:
            return jnp.where(row >= k, pltpu.roll(s, k, 0), 0.0)
        return jnp.where(row < T - k, pltpu.roll(s, T - k, 0), 0.0)

    sums, s = [], u
    for k in (1, 2, 4, 8):
        s = s + sh(s, k)
        sums.append(s)
    return _pick(g, sums)


MAX_WINDOW = 16


def _pool_inv_count(shape, g):
    T, n = shape
    row = lax.broadcasted_iota(jnp.int32, (MAX_WINDOW, n), 0)
    head = 1.0 / jnp.minimum(row + 1, lax.shift_left(jnp.int32(2), g)).astype(F32)
    inv_w = _pick(g, [0.5, 0.25, 0.125, 0.0625])
    return jnp.concatenate([head, jnp.broadcast_to(inv_w, (T - MAX_WINDOW, n)).astype(F32)], axis=0)


def _pool_fwd(zup, zgp, pool_w, pool_scale):
    T = zup.shape[0]

    def body(u_ref, g_ref, w_ref, sc_ref, y_ref):
        g = pl.program_id(0)
        u = u_ref[...]
        d = _window_sum(u, g, True) * _pool_inv_count(u.shape, g) - u
        lin = _mm(d.astype(BF16), w_ref[0].astype(BF16))
        silu, _ = _silu_parts(g_ref[...])
        y_ref[...] = (lin * sc_ref[...] * silu).astype(BF16)

    col = pl.BlockSpec((T, GROUP), lambda g: (0, g))
    return pl.pallas_call(
        body, name="pool_fwd", grid=(POOL_GROUPS,),
        in_specs=[col, col, pl.BlockSpec((1, GROUP, GROUP), lambda g: (g, 0, 0)), pl.BlockSpec((1, GROUP), lambda g: (0, g))],
        out_specs=col, out_shape=SDS((T, POOL_W), BF16), compiler_params=_cp(("parallel",)),
    )(zup, zgp, pool_w, pool_scale)


def _pool_bwd(zup, zgp, dyp, pool_w, pool_scale):
    T = zup.shape[0]

    def body(u_ref, g_ref, dy_ref, w_ref, sc_ref, du_ref, dg_ref, gw_ref, gsc_ref):
        g = pl.program_id(0)
        u = u_ref[...]
        inv = _pool_inv_count(u.shape, g)
        d = (_window_sum(u, g, True) * inv - u).astype(BF16)
        wb = w_ref[0].astype(BF16)
        lin = _mm(d, wb)
        sc = sc_ref[...]
        silu, dsilu = _silu_parts(g_ref[...])
        dy = dy_ref[...]
        dg_ref[...] = (dy * lin * sc * dsilu).astype(BF16)
        dpre = dy * silu
        gsc_ref[...] = jnp.sum(dpre * lin, axis=0, keepdims=True)
        dlin = (dpre * sc).astype(BF16)
        gw_ref[0] = _mm_tn(d, dlin)
        dd = _mm_nt(dlin, wb)
        du_ref[...] = (_window_sum(dd * inv, g, False) - dd).astype(BF16)

    col = pl.BlockSpec((T, GROUP), lambda g: (0, g))
    wspec = pl.BlockSpec((1, GROUP, GROUP), lambda g: (g, 0, 0))
    vspec = pl.BlockSpec((1, GROUP), lambda g: (0, g))
    return pl.pallas_call(
        body, name="pool_bwd", grid=(POOL_GROUPS,), in_specs=[col, col, col, wspec, vspec], out_specs=[col, col, wspec, vspec],
        out_shape=[SDS((T, POOL_W), BF16), SDS((T, POOL_W), BF16), SDS((POOL_GROUPS, GROUP, GROUP), F32), SDS((1, POOL_W), F32)],
        compiler_params=_cp(("parallel",)),
    )(zup, zgp, dyp, pool_w, pool_scale)


def _tail(x2, tgt, o, zga, ypool, zgm, wba, wbp, wout, norm_final, tm):
    T = x2.shape[0]
    steps = T // tm
    cols = D_MODEL // N_DEV

    def body(x_ref, tgt_ref, o_ref, zga_ref, yp_ref, zgm_ref, wba_ref, wbp_ref, wout_ref, nf_ref,
             loss_ref, dh_ref, dgm_ref, doop_ref, dga_ref, dcapr_ref, dyp_ref, swout_ref, swba_ref, swbp_ref, gnf_ref,
             gwout_ref, gwba_ref, gwbp_ref):
        @pl.when(pl.program_id(0) == 0)
        def _():
            for ref in (loss_ref, gwout_ref, gwba_ref, gwbp_ref, gnf_ref):
                ref[...] = jnp.zeros_like(ref)

        o_v = o_ref[...]
        silu, dsilu = _silu_parts(zga_ref[...])
        ya = (o_v * silu).astype(BF16)
        yp = yp_ref[...]
        wba_v = jnp.concatenate([wba_ref[k] for k in range(N_DEV)], axis=1)
        wbp_v = jnp.concatenate([wbp_ref[k] for k in range(N_DEV)], axis=1)
        wout_v = wout_ref[...]
        a = _mm(ya, wba_v)
        p = _mm(yp, wbp_v)
        gate = jax.nn.sigmoid(zgm_ref[...])
        ga, gp = gate[:, :D_MODEL], gate[:, D_MODEL:]
        mg = (ga * a + gp * p).astype(BF16)
        h = x_ref[...] + _mm(mg, wout_v)
        r = lax.rsqrt(jnp.mean(h * h, axis=-1, keepdims=True) + EPS)
        gf = nf_ref[...]
        hr = h * r
        e = hr * gf - tgt_ref[...]
        loss_ref[...] += (0.5 / D_MODEL) * jnp.sum(e * e)
        dy = e * (1.0 / D_MODEL)
        gnf_ref[...] += jnp.sum(dy * hr, axis=0, keepdims=True)
        u = dy * gf
        dh = r * (u - hr * jnp.mean(u * hr, axis=-1, keepdims=True))
        dh_ref[...] = dh
        dhb = dh.astype(BF16)
        dmg = _mm_nt(dhb, wout_v)
        dab = (dmg * ga).astype(BF16)
        dpb = (dmg * gp).astype(BF16)
        dya = _mm_nt(dab, wba_v)
        dyp_ref[...] = _mm_nt(dpb, wbp_v)
        gwout_ref[...] += _mm_tn(mg, dhb)
        gwba_ref[...] += _mm_tn(ya, dab)
        gwbp_ref[...] += _mm_tn(yp, dpb)
        dgm_ref[:, :D_MODEL] = (dmg * a * ga * (1.0 - ga)).astype(BF16)
        dgm_ref[:, D_MODEL:] = (dmg * p * gp * (1.0 - gp)).astype(BF16)
        do = dya * silu
        dga_ref[...] = (dya * o_v * dsilu).astype(BF16)
        prod = do * o_v
        lo = lax.broadcasted_iota(jnp.int32, (tm, LANES), 1) < VDIM
        for pair in range(HEADS // 2):
            ls = slice(pair * LANES, (pair + 1) * LANES)
            do_p, prod_p = do[:, ls], prod[:, ls]
            dcap_a = jnp.sum(jnp.where(lo, prod_p, 0.0), axis=-1, keepdims=True)
            dcap_b = jnp.sum(jnp.where(lo, 0.0, prod_p), axis=-1, keepdims=True)
            _store_pair_rows(dcapr_ref, pair, jnp.where(lo, dcap_a, dcap_b))
            doop_ref[:, 2 * pair * LANES:(2 * pair + 1) * LANES] = jnp.where(lo, 0.0, pltpu.roll(do_p, VDIM, 1)).astype(BF16)
            doop_ref[:, (2 * pair + 1) * LANES:(2 * pair + 2) * LANES] = jnp.where(lo, 0.0, do_p).astype(BF16)

        @pl.when(pl.program_id(0) == steps - 1)
        def _():
            for k in range(N_DEV):
                swout_ref[k] = gwout_ref[k * cols:(k + 1) * cols, :].astype(BF16)
                swba_ref[k] = gwba_ref[:, k * cols:(k + 1) * cols].astype(BF16)
                swbp_ref[k] = gwbp_ref[:, k * cols:(k + 1) * cols].astype(BF16)

    ins = (x2, tgt, o, zga, ypool, zgm, wba, wbp, wout, norm_final)
    in_specs = [_row_spec(tm, D_MODEL), _row_spec(tm, D_MODEL), _row_spec(tm, MLA_W), _row_spec(tm, MLA_W), _row_spec(tm, POOL_W),
                _row_spec(tm, 2 * D_MODEL), _full_spec(wba), _full_spec(wbp), _full_spec(wout), _full_spec(norm_final)]
    outs = [SDS((8, LANES), F32), SDS((T, D_MODEL), F32), SDS((T, 2 * D_MODEL), BF16), SDS((T, HW), BF16), SDS((T, MLA_W), BF16),
            SDS((HEADS // 2, 2, T), F32), SDS((T, POOL_W), F32),
            SDS((N_DEV, cols, D_MODEL), BF16), SDS((N_DEV, MLA_W, cols), BF16), SDS((N_DEV, POOL_W, cols), BF16), SDS((1, D_MODEL), F32)]
    out_specs = [_full_spec(outs[0]), _row_spec(tm, D_MODEL), _row_spec(tm, 2 * D_MODEL), _row_spec(tm, HW), _row_spec(tm, MLA_W),
                 pl.BlockSpec((HEADS // 2, 2, tm), lambda i: (0, 0, i)), _row_spec(tm, POOL_W),
                 _full_spec(outs[7]), _full_spec(outs[8]), _full_spec(outs[9]), _full_spec(outs[10])]
    return pl.pallas_call(
        body, name="tail", grid=(steps,), in_specs=in_specs, out_specs=out_specs, out_shape=outs,
        scratch_shapes=[pltpu.VMEM((D_MODEL, D_MODEL), F32), pltpu.VMEM((MLA_W, D_MODEL), F32), pltpu.VMEM((POOL_W, D_MODEL), F32)],
        compiler_params=_cp(("arbitrary",)),
    )(*ins)


def _attn_bwd(q_att, k_att, v_att, doop, lse_rows, dcap_rows, tq, hps, slabs, packed):
    T = q_att.shape[0]
    nq = T // tq
    n = len(slabs)
    groups = HEADS // hps
    head_lanes = [slice(h * LANES, (h + 1) * LANES) for h in range(hps)]

    def body(q_ref, k_ref, v_ref, doop_ref, lse_ref, dcap_ref, *rest):
        slab_refs, packed_ref = rest[:n], rest[n]
        dq_ref, dkv_ref, dkr_ref = rest[n + 1:n + 4]
        sum_refs, ptot_ref = rest[n + 4:2 * n + 4], rest[2 * n + 4]
        dq_acc = rest[2 * n + 5]
        rs = _ReduceScatter(slab_refs, packed_ref, sum_refs, ptot_ref, rest[2 * n + 6:])
        group, j = pl.program_id(0), pl.program_id(1)
        tick, last = group * nq + j, groups * nq - 1
        pl.when(tick == 0)(rs.start1)
        pl.when(tick == min(1, last))(rs.finish1_start2)
        pl.when(tick == min(3, last))(rs.relay2)
        mask = _chunk_mask(tq, tq, 0, True)
        lane = lax.broadcasted_iota(jnp.int32, (tq, LANES), 1)
        ks = [k_ref[:, hs] for hs in head_lanes]
        vs = [v_ref[:, hs] for hs in head_lanes]
        kts = [kh.T for kh in ks]

        @pl.when(j == 0)
        def _():
            dq_acc[...] = jnp.zeros_like(dq_acc)

        def step(i, carry, masked):
            rows = pl.ds(pl.multiple_of(i * tq, tq), tq)
            heads = range(hps)
            stat = lambda h: (h // 2, slice(h % 2, h % 2 + 1), rows)
            qhs = [q_ref[rows, hs] for hs in head_lanes]
            doops = [doop_ref[rows, hs] for hs in head_lanes]
            sts = [_mm_nt(ks[h], qhs[h]) for h in heads]
            dpts = [_mm_nt(vs[h], doops[h]) for h in heads]
            pts = [jnp.exp2(sts[h] - lse_ref[stat(h)]) for h in heads]
            if masked:
                pts = [jnp.where(mask, pt, 0.0) for pt in pts]
            dsts = [(pts[h] * (dpts[h] - dcap_ref[stat(h)])).astype(BF16) for h in heads]
            dvs = [_mm(pts[h].astype(BF16), doops[h]) for h in heads]
            dks = [_mm(dsts[h], qhs[h]) for h in heads]
            for h, hs in enumerate(head_lanes):
                dq_acc[hs, rows] += _mm(kts[h], dsts[h])
            return tuple((dk + dks[h], dv + dvs[h]) for h, (dk, dv) in enumerate(carry))

        zero = jnp.zeros((tq, LANES), F32)
        carry = step(j, ((zero, zero),) * hps, True)
        res = lax.fori_loop(j + 1, nq, functools.partial(step, masked=False), carry)
        dkr = None
        for (dk, dv), hs in zip(res, head_lanes):
            dk = dk * LN2
            dkv_ref[:, hs] = jnp.where(lane < NOPE, dk, dv).astype(BF16)
            dkr = dk if dkr is None else dkr + dk
        dkr_ref[0] = jnp.where((lane >= NOPE) & (lane < NOPE + ROPE), dkr, 0.0)

        @pl.when(j == nq - 1)
        def _():
            dq_ref[...] = (dq_acc[...] * SCALE).T.astype(BF16)

        pl.when(tick == last)(rs.finish2)

    kspec = pl.BlockSpec((tq, hps * LANES), lambda p, j: (j, p))
    qspec = pl.BlockSpec((T, hps * LANES), lambda p, j: (0, p))
    rspec = pl.BlockSpec((hps // 2, 2, T), lambda p, j: (p, 0, 0))
    sums = [SDS(s.shape[1:], F32) for s in slabs] + [SDS(packed.shape, F32)]
    return pl.pallas_call(
        body, name="attn_bwd", grid=(groups, nq),
        in_specs=[qspec, kspec, kspec, qspec, rspec, rspec] + [HBM_SPEC] * n + [_full_spec(packed)],
        out_specs=[qspec, kspec, pl.BlockSpec((1, tq, LANES), lambda p, j: (p, j, 0))] + [_full_spec(s) for s in sums],
        out_shape=[SDS((T, HW), BF16), SDS((T, HW), BF16), SDS((groups, T, LANES), F32)] + sums,
        scratch_shapes=[pltpu.VMEM((hps * LANES, T), F32)] + _rs_scratch([s.shape for s in sums[:-1]], packed.shape),
        compiler_params=_cp(("arbitrary", "arbitrary")),
    )(q_att, k_att, v_att, doop, lse_rows, dcap_rows, *slabs, packed)


def _rms_bwd(z, gain, dout):
    r = lax.rsqrt(jnp.mean(z * z, axis=-1, keepdims=True) + EPS)
    zr = z * r
    u = dout * gain
    return r * (u - zr * jnp.mean(u * zr, axis=-1, keepdims=True)), jnp.sum(dout * zr, axis=0, keepdims=True)


def _mla_bwd(dq_att, dkv_nat, dkr4, zfr, q_norm, wuq_pad, kv_norm, wukv, rc, rsa, rsb, tm):
    T = dq_att.shape[0]

    def body(dq_ref, dkv_ref, dkr_ref, zfr_ref, qn_ref, wuq_ref, kvn_ref, wukv_ref, c_ref, sa_ref, sb_ref,
             dfr_ref, gwuq_out, gwukv_out, gqn_ref, gkvn_ref, gwuq_ref, gwukv_ref):
        @pl.when(pl.program_id(0) == 0)
        def _():
            for ref in (gwuq_ref, gwukv_ref, gqn_ref, gkvn_ref):
                ref[...] = jnp.zeros_like(ref)

        c, sa, sb = c_ref[...], sa_ref[...], sb_ref[...]
        zq, zkv = zfr_ref[:, :Q_RANK], zfr_ref[:, Q_RANK:Q_RANK + KV_RANK]
        qn, kvn = qn_ref[...].reshape(1, Q_RANK), kvn_ref[...]
        dkv = dkv_ref[...]
        dckv = _mm_nt(dkv, wukv_ref[...])
        ckv = (zkv * lax.rsqrt(jnp.mean(zkv * zkv, axis=-1, keepdims=True) + EPS) * kvn).astype(BF16)
        gwukv_ref[...] += _mm_tn(ckv, dkv)
        cq = (zq * lax.rsqrt(jnp.mean(zq * zq, axis=-1, keepdims=True) + EPS) * qn).astype(BF16)
        dq = _rope(dq_ref[...].astype(F32), c, sa, sb, -1.0).astype(BF16)
        dzkv, gkvn = _rms_bwd(zkv, kvn, dckv)
        gkvn_ref[...] += gkvn
        gwuq_ref[...] += _mm_tn(cq, dq)
        dzq, gqn = _rms_bwd(zq, qn, _mm_nt(dq, wuq_ref[...]))
        gqn_ref[...] += gqn
        dkr = functools.reduce(lambda a, b: a + b, [dkr_ref[g] for g in range(dkr4.shape[0])])
        dfr_ref[:, :Q_RANK] = dzq.astype(BF16)
        dfr_ref[:, Q_RANK:Q_RANK + KV_RANK] = dzkv.astype(BF16)
        dfr_ref[:, Q_RANK + KV_RANK:] = _rope(dkr, c, sa, sb, -1.0).astype(BF16)

        @pl.when(pl.program_id(0) == pl.num_programs(0) - 1)
        def _():
            gwuq_out[...] = gwuq_ref[...].astype(BF16)
            gwukv_out[...] = gwukv_ref[...].astype(BF16)

    ins = (dq_att, dkv_nat, dkr4, zfr, q_norm, wuq_pad, kv_norm, wukv, rc, rsa, rsb)
    in_specs = [_row_spec(tm, HW), _row_spec(tm, HW), pl.BlockSpec((dkr4.shape[0], tm, LANES), lambda i: (0, i, 0)), _row_spec(tm, FRONT_W),
                _full_spec(q_norm), _full_spec(wuq_pad), _full_spec(kv_norm), _full_spec(wukv),
                _row_spec(tm, LANES), _row_spec(tm, LANES), _row_spec(tm, LANES)]
    outs = [SDS((T, FRONT_W), BF16), SDS((Q_RANK, HW), BF16), SDS((KV_RANK, HW), BF16), SDS((1, Q_RANK), F32), SDS((1, KV_RANK), F32)]
    out_specs = [_row_spec(tm, FRONT_W)] + [_full_spec(s) for s in outs[1:]]
    return pl.pallas_call(
        body, name="mla_bwd", grid=(T // tm,), in_specs=in_specs, out_specs=out_specs, out_shape=outs,
        scratch_shapes=[pltpu.VMEM((Q_RANK, HW), F32), pltpu.VMEM((KV_RANK, HW), F32)],
        compiler_params=_cp(("arbitrary",)),
    )(*ins)


_DZ_COLS = ((GM, ZTOT), (GA, UP), (UP, GP), (GP, GM), (ZQ, GA))


def _in_proj_bwd_x(dzs, x2, dh, norm_in, w_in_pad, tm, slabs, late_rows):
    T = x2.shape[0]
    steps = T // tm
    n, m = len(slabs), len(late_rows)
    firsts = _packed_at([norm_in] + list(late_rows))
    packed_rows = firsts[-1] + -(-(late_rows[-1].size // LANES) // SUBLANES) * SUBLANES
    n_rs_scratch = len(_rs_scratch([s.shape[1:] for s in slabs], None))

    def body(d0, d1, d2, d3, d4, x_ref, dh_ref, nin_ref, win_ref, *rest):
        slab_refs, row_refs, (gx_ref, tot_ref), sum_refs = rest[:n], rest[n:n + m], rest[n + m:n + m + 2], rest[n + m + 2:2 * n + m + 2]
        scratch = rest[2 * n + m + 2:]
        rs = _ReduceScatter(slab_refs, None, sum_refs, None, scratch[:n_rs_scratch])
        gnin_ref, packed_ref = scratch[n_rs_scratch:n_rs_scratch + 2]
        step = pl.program_id(0)

        @pl.when(step == 0)
        def _():
            gnin_ref[...] = jnp.zeros_like(gnin_ref)
            rs.start1()

        pl.when(step == min(2, steps - 1))(rs.finish1_start2)
        pl.when(step == min(steps * 11 // 16, steps - 1))(rs.relay2)
        dhn = None
        for ref, (lo, hi) in zip((d0, d1, d2, d3, d4), _DZ_COLS):
            t = _mm(ref[...], win_ref[lo:hi, :])
            dhn = t if dhn is None else dhn + t
        dx, gnin = _rms_bwd(x_ref[...], nin_ref[...], dhn)
        gnin_ref[...] += gnin
        gx_ref[...] = dx + dh_ref[...]

        @pl.when(step == steps - 1)
        def _():
            rs.finish2()
            packed_ref[...] = jnp.zeros_like(packed_ref)
            for ref, first in zip((gnin_ref,) + tuple(row_refs), firsts):
                for j in range(ref.shape[1] // LANES):
                    packed_ref[first + j:first + j + 1, :] = ref[:, j * LANES:(j + 1) * LANES]
            closing = _ReduceScatter((), packed_ref, (), tot_ref, scratch[n_rs_scratch + 2:])
            closing.start1()
            closing.finish1_start2()
            closing.finish2()

    in_specs = ([_row_spec(tm, hi - lo) for lo, hi in _DZ_COLS] + [_row_spec(tm, D_MODEL), _row_spec(tm, D_MODEL),
                                                                   _full_spec(norm_in), _full_spec(w_in_pad)] + [HBM_SPEC] * n
                + [_full_spec(r) for r in late_rows])
    sums = [SDS(s.shape[1:], F32) for s in slabs]
    outs = [SDS((T, D_MODEL), F32), SDS((packed_rows, LANES), F32)] + sums
    return pl.pallas_call(
        body, name="in_proj_bwd_x", grid=(steps,), in_specs=in_specs,
        out_specs=[_row_spec(tm, D_MODEL), _full_spec(outs[1])] + [_full_spec(s) for s in sums],
        out_shape=outs,
        scratch_shapes=(_rs_scratch([s.shape for s in sums], None) + [pltpu.VMEM((1, D_MODEL), F32), pltpu.VMEM((packed_rows, LANES), F32)]
                        + _rs_scratch([], (packed_rows, LANES))),
        compiler_params=_cp(("arbitrary",)),
    )(*dzs, x2, dh, norm_in, w_in_pad, *slabs, *late_rows)


SLAB_ROWS = IN_TOTAL // N_DEV


def _slab_segments(k):
    cuts = [(0, ZKR_ORIG, 0), (ZKR_ORIG, ZKR_ORIG + ROPE, NOPE), (ZKR_ORIG + ROPE, IN_TOTAL, LANES - ROPE)]
    lo, hi = k * SLAB_ROWS, (k + 1) * SLAB_ROWS
    return [(max(lo, a) - lo, max(lo, a) + shift, min(hi, b) - max(lo, a)) for a, b, shift in cuts if min(hi, b) > max(lo, a)]


def _in_proj_bwd_w(dzs, hn, tm):
    T = hn.shape[0]
    steps = T // tm

    def body(d0, d1, d2, d3, d4, hn_ref, slab_ref, acc_ref):
        @pl.when(pl.program_id(0) == 0)
        def _():
            acc_ref[...] = jnp.zeros_like(acc_ref)

        hn_v = hn_ref[...]
        for ref, (lo, hi) in zip((d0, d1, d2, d3, d4), _DZ_COLS):
            acc_ref[lo:hi, :] += _mm_tn(ref[...], hn_v)

        @pl.when(pl.program_id(0) == steps - 1)
        def _():
            for k in range(N_DEV):
                for at, src, rows in _slab_segments(k):
                    slab_ref[k, at:at + rows, :] = acc_ref[src:src + rows, :].astype(BF16)

    in_specs = [_row_spec(tm, hi - lo) for lo, hi in _DZ_COLS] + [_row_spec(tm, D_MODEL)]
    out = SDS((N_DEV, SLAB_ROWS, D_MODEL), BF16)
    return pl.pallas_call(
        body, name="in_proj_bwd_w", grid=(steps,), in_specs=in_specs, out_specs=_full_spec(out), out_shape=out,
        scratch_shapes=[pltpu.VMEM((ZTOT, D_MODEL), F32)], compiler_params=_cp(("arbitrary",)),
    )(*dzs, hn)


def _local_step(x2, tgt, norm_in, w_in_pad, q_norm, wuq_pad, kv_norm, wukv, pool_w, pool_scale, late_shards, norm_final):
    T = x2.shape[0]
    tm, tm_small, tq = min(512, T), min(256, T), min(512, T)
    heads_fwd, heads_bwd = 4, 4
    row = lambda v: v.reshape(1, -1)
    rc, rsa, rsb = _rope_tables(T)

    hn, zgm, zga, zup, zgp, zfr, q_att, k_att, v_att, vt_att, w_ba, w_bp, w_out = _in_proj(
        x2, row(norm_in), w_in_pad, q_norm, wuq_pad, row(kv_norm), wukv, rc, rsa, rsb, tm, late_shards)
    w_out = w_out.reshape(D_MODEL, D_MODEL)
    o, lse_rows = _attn_fwd(q_att, k_att, vt_att, tq, heads_fwd)
    ypool = _pool_fwd(zup, zgp, pool_w, row(pool_scale))
    loss8, dh, dgm, doop, dga, dcap_rows, dyp, *slabs, g_nf = _tail(
        x2, tgt, o, zga, ypool, zgm, w_ba, w_bp, w_out, row(norm_final), tm_small)
    dup, dgp, g_pool_w, g_pool_scale = _pool_bwd(zup, zgp, dyp, pool_w, row(pool_scale))

    early = [g_pool_w, g_pool_scale, g_nf, loss8[0]]
    packed = jnp.concatenate([_pack_rows(a) for a in early], axis=0)
    dq_att, dkv_nat, dkr4, s_wout, s_wba, s_wbp, tot_early = _attn_bwd(
        q_att, k_att, v_att, doop, lse_rows, dcap_rows, tq, heads_bwd, slabs, packed)
    at_pool_w, at_pool_scale, at_nf, at_loss = _packed_at(early)

    dfr, g_wuq_pad, g_wukv, g_qn, g_kvn = _mla_bwd(
        dq_att, dkv_nat, dkr4, zfr, q_norm, wuq_pad, row(kv_norm), wukv, rc, rsa, rsb, tm)
    dzs = (dgm, dga, dup, dgp, dfr)
    slabs = [_in_proj_bwd_w(dzs, hn, tm), g_wuq_pad.reshape(N_DEV, Q_RANK // N_DEV, HW), g_wukv.reshape(N_DEV, KV_RANK // N_DEV, HW)]
    grad_x, tot_late, s_win, s_wuq, s_wukv = _in_proj_bwd_x(dzs, x2, dh, row(norm_in), w_in_pad, tm_small, slabs, [g_qn, g_kvn])
    at_nin, at_qn, at_kvn = _packed_at([norm_in, g_qn, g_kvn])

    grads = dict(norm_in=(tot_late, at_nin), w_in=s_win, q_norm=(tot_late, at_qn), w_uq=s_wuq, kv_norm=(tot_late, at_kvn), w_ukv=s_wukv,
                 pool_w=(tot_early, at_pool_w), pool_scale=(tot_early, at_pool_scale), w_branch_attn=s_wba, w_branch_pool=s_wbp,
                 w_out=s_wout, norm_final=(tot_early, at_nf))
    return tot_early[at_loss, 0], grad_x, grads


MESH_ID = pl.DeviceIdType.MESH
VMEM_SPEC = pl.BlockSpec(memory_space=pltpu.VMEM)
HBM_SPEC = pl.BlockSpec(memory_space=pl.ANY)


def _mesh_pos():
    return lax.axis_index("x"), lax.axis_index("y"), lax.axis_index("c")


BF16_TILE_ROWS = 16


def _half_rows(rows):
    cut = -(-(rows // 2) // BF16_TILE_ROWS) * BF16_TILE_ROWS
    return pl.ds(0, cut), pl.ds(cut, rows - cut)


def _slot(px, py, pc):
    return 4 * px + 2 * py + pc


def _staged_shape(shape):
    return (shape[0], shape[1] * LANES) if len(shape) == 3 else tuple(shape)


def _all_gather_bf16(shards):
    n = len(shards)
    staged = [_staged_shape(s.shape) for s in shards]

    def body(*refs):
        ins, outs = refs[:n], refs[n:2 * n]
        land0, scratch = refs[2 * n], refs[2 * n + 1:]
        wpad_ref = outs[0]
        ag = _AllGather(ins, (land0,) + tuple(outs[1:]), scratch)
        ag.start()
        ag.forward()
        ag.finish()
        wpad_ref[ZKR:GA, :] = jnp.zeros((GA - ZKR, D_MODEL), BF16)
        for k in range(N_DEV):
            for at, dst, rows in _slab_segments(k):
                wpad_ref[dst:dst + rows, :] = land0[k, at:at + rows, :]

    return pl.pallas_call(
        body, name="all_gather_weights",
        in_specs=[VMEM_SPEC] * n, out_specs=[VMEM_SPEC] + [HBM_SPEC] * (n - 1),
        out_shape=[SDS((ZTOT, D_MODEL), BF16)] + [SDS((N_DEV,) + s, BF16) for s in staged[1:]],
        scratch_shapes=[pltpu.VMEM((N_DEV,) + staged[0], BF16)] + _ag_scratch(staged),
        compiler_params=_cp(),
    )(*shards)


def _ag_scratch(shapes):
    n = len(shapes)
    dma = pltpu.SemaphoreType.DMA
    return [pltpu.VMEM(tuple(s), BF16) for s in shapes] + [dma((_AllGather.COPIES * n,)), dma((_AllGather.COPIES * n,)), dma((n,))]


class _AllGather:
    COPIES = 8

    def __init__(self, in_refs, dest_refs, scratch):
        n = self.n = len(in_refs)
        self.ins, self.dests, self.stage = in_refs, dest_refs, scratch[:n]
        self.send_sems, self.recv_sems, self.local_sems = scratch[n:]
        x, y, c = _mesh_pos()
        self.c, self.me, self.sibling = c, (x, y, c), (x, y, 1 - c)
        self.xn, self.yn, self.diag = (1 - x, y), (x, 1 - y), (1 - x, 1 - y)

    def _halves(self, a):
        return _half_rows(self.stage[a].shape[0])

    def _copy(self, a, k, block, to, from_stage=False, rows=None):
        dst = self.dests[a].at[_slot(*block)]
        src = self.stage[a] if from_stage else dst
        if rows is not None:
            src, dst = src.at[rows], dst.at[rows]
        return pltpu.make_async_remote_copy(
            src_ref=src, dst_ref=dst, send_sem=self.send_sems.at[self.COPIES * a + k],
            recv_sem=self.recv_sems.at[self.COPIES * a + k], device_id=to, device_id_type=MESH_ID)

    def _mine(self):
        return [pltpu.make_async_copy(self.stage[a], self.dests[a].at[_slot(*self.me)], self.local_sems.at[a]) for a in range(self.n)]

    def _first(self, a):
        return [self._copy(a, 0, self.me, self.sibling, True), self._copy(a, 1, self.me, (*self.xn, self.c), True),
                self._copy(a, 2, self.me, (*self.yn, self.c), True)]

    def _relays(self, a):
        lo, hi = self._halves(a)
        return [self._copy(a, 3, (*self.xn, self.c), (*self.yn, self.c), rows=lo),
                self._copy(a, 4, (*self.yn, self.c), (*self.xn, self.c), rows=hi)]

    def _passes(self, a):
        return [self._copy(a, 5 + j, (*chip, self.c), self.sibling) for j, chip in enumerate((self.xn, self.yn, self.diag))]

    def start(self):
        for a in range(self.n):
            src, dst = self.ins[a], self.stage[a]
            if src.shape == dst.shape:
                dst[...] = src[...].astype(BF16)
            else:
                if src.shape[2] < LANES:
                    dst[...] = jnp.zeros(dst.shape, BF16)
                for h in range(src.shape[1]):
                    dst[:, h * LANES:h * LANES + src.shape[2]] = src[:, h, :].astype(BF16)
        for cp in self._mine():
            cp.start()
        for a in range(self.n):
            for cp in self._first(a):
                cp.start()

    def forward(self):
        for a in range(self.n):
            relays, passes = self._relays(a), self._passes(a)
            for j, chip in enumerate((self.xn, self.yn)):
                self._copy(a, 1 + j, (*chip, self.c), self.me).wait_recv()
                relays[j].start()
                passes[j].start()

    def finish(self):
        for a in range(self.n):
            lo, hi = self._halves(a)
            self._copy(a, 3, (*self.diag, self.c), self.me, rows=lo).wait_recv()
            self._copy(a, 4, (*self.diag, self.c), self.me, rows=hi).wait_recv()
            self._passes(a)[2].start()
        for a in range(self.n):
            self._copy(a, 0, self.sibling, self.me).wait_recv()
            for j, chip in enumerate((self.xn, self.yn, self.diag)):
                self._copy(a, 5 + j, (*chip, 1 - self.c), self.me).wait_recv()
            for cp in self._first(a) + self._relays(a) + self._passes(a):
                cp.wait_send()
        for cp in self._mine():
            cp.wait()


N_CHIPS = 4


def _all_reduce_rows(parts):
    m = len(parts)
    firsts = _packed_at(parts)
    rows = firsts[-1] + -(-(parts[-1].size // LANES) // SUBLANES) * SUBLANES

    def body(*refs):
        part_refs, tot_ref, packed_ref = refs[:m], refs[m], refs[m + 1]
        packed_ref[...] = jnp.zeros_like(packed_ref)
        for ref, first in zip(part_refs, firsts):
            for j in range(ref.shape[1] // LANES):
                packed_ref[first + j:first + j + 1, :] = ref[:, j * LANES:(j + 1) * LANES]
        rs = _ReduceScatter((), packed_ref, (), tot_ref, refs[m + 2:])
        rs.start1()
        rs.finish1_start2()
        rs.relay2()
        rs.finish2()

    return pl.pallas_call(
        body, name="all_reduce_rows", in_specs=[VMEM_SPEC] * m, out_specs=VMEM_SPEC, out_shape=SDS((rows, LANES), F32),
        scratch_shapes=[pltpu.VMEM((rows, LANES), F32)] + _rs_scratch([], (rows, LANES)), compiler_params=_cp(),
    )(*parts)


def _rs_scratch(shapes, packed_shape):
    n = len(shapes)
    n1, n2 = N_CHIPS * n + 1, _ReduceScatter.L2_COPIES * n + N_CHIPS - 1
    dma = pltpu.SemaphoreType.DMA
    packed = [] if packed_shape is None else [pltpu.VMEM(packed_shape, F32), pltpu.VMEM((N_CHIPS,) + tuple(packed_shape), F32)]
    return ([pltpu.VMEM((N_CHIPS,) + tuple(s), BF16) for s in shapes] * 2 + [pltpu.VMEM((N_CHIPS - 1,) + tuple(s), BF16) for s in shapes] * 2
            + packed + [dma((max(N_CHIPS * n, 1),)), dma((n1,)), dma((n1,)), dma((n2,)), dma((n2,))])


class _ReduceScatter:
    L2_COPIES = 6

    def __init__(self, slab_refs, packed_ref, out_refs, ptot_ref, scratch):
        n = self.n = len(slab_refs)
        self.slabs, self.packed, self.outs, self.ptot = slab_refs, packed_ref, out_refs, ptot_ref
        self.own1, self.land1, self.send2, self.land2 = (scratch[k * n:(k + 1) * n] for k in range(4))
        rest = scratch[4 * n:]
        if packed_ref is not None:
            self.pland1, self.pland2 = rest[:2]
            rest = rest[2:]
        self.loc_sems, self.send1_sems, self.recv1_sems, self.send2_sems, self.recv2_sems = rest
        self.x, self.y, self.c = _mesh_pos()

    def _chip(self, r):
        return (1 - self.x if r & 2 else self.x, 1 - self.y if r & 1 else self.y)

    @staticmethod
    def _remote(src, dst, send_sem, recv_sem, to):
        return pltpu.make_async_remote_copy(src_ref=src, dst_ref=dst, send_sem=send_sem, recv_sem=recv_sem, device_id=to,
                                            device_id_type=MESH_ID)

    def _copies1(self):
        c, sibling = self.c, (self.x, self.y, 1 - self.c)
        cps = []
        for a in range(self.n):
            for r in range(N_CHIPS):
                k = N_CHIPS * a + r
                cps.append(pltpu.make_async_copy(self.slabs[a].at[_slot(*self._chip(r), c)], self.own1[a].at[r], self.loc_sems.at[k]))
                cps.append(self._remote(self.slabs[a].at[_slot(*self._chip(r), 1 - c)], self.land1[a].at[r],
                                        self.send1_sems.at[k], self.recv1_sems.at[k], sibling))
        if self.packed is not None:
            k = N_CHIPS * self.n
            cps.append(self._remote(self.packed, self.pland1, self.send1_sems.at[k], self.recv1_sems.at[k], sibling))
        return cps

    def _halves(self, a):
        return _half_rows(self.send2[a].shape[1])

    def _copy2(self, a, k):
        lo, hi = self._halves(a)
        xn, yn = (*self._chip(2), self.c), (*self._chip(1), self.c)
        slot, rows, to = [(1, lo, xn), (2, lo, xn), (0, hi, yn), (2, hi, yn), (0, lo, yn), (1, hi, xn)][k]
        return self._remote(self.send2[a].at[slot].at[rows], self.land2[a].at[slot].at[rows],
                            self.send2_sems.at[self.L2_COPIES * a + k], self.recv2_sems.at[self.L2_COPIES * a + k], to)

    def _copies2_packed(self):
        base = self.L2_COPIES * self.n - 1
        return [self._remote(self.pland2.at[0], self.pland2.at[r], self.send2_sems.at[base + r], self.recv2_sems.at[base + r],
                             (*self._chip(r), self.c)) for r in range(1, N_CHIPS)]

    def start1(self):
        for cp in self._copies1():
            cp.start()

    def finish1_start2(self):
        for cp in self._copies1():
            cp.wait()
        for a in range(self.n):
            self.outs[a][...] = self.own1[a][0].astype(F32) + self.land1[a][0].astype(F32)
            for r in range(1, N_CHIPS):
                self.send2[a][r - 1] = (self.own1[a][r].astype(F32) + self.land1[a][r].astype(F32)).astype(BF16)
            for k in range(4):
                self._copy2(a, k).start()
        if self.packed is not None:
            self.pland2[0] = self.packed[...] + self.pland1[...]
            for cp in self._copies2_packed():
                cp.start()

    def relay2(self):
        for a in range(self.n):
            lo, hi = self._halves(a)
            s2, l2 = self.send2[a], self.land2[a]
            self._copy2(a, 1).wait_recv()
            s2[0, lo] = (s2[0, lo].astype(F32) + l2[2, lo].astype(F32)).astype(BF16)
            self._copy2(a, 4).start()
            self._copy2(a, 3).wait_recv()
            s2[1, hi] = (s2[1, hi].astype(F32) + l2[2, hi].astype(F32)).astype(BF16)
            self._copy2(a, 5).start()

    def finish2(self):
        for a in range(self.n):
            for k in (0, 2, 4, 5):
                self._copy2(a, k).wait_recv()
            for k in range(self.L2_COPIES):
                self._copy2(a, k).wait_send()
            l2 = self.land2[a]
            self.outs[a][...] = self.outs[a][...] + (l2[0].astype(F32) + l2[1].astype(F32))
        if self.packed is not None:
            for cp in self._copies2_packed():
                cp.wait()
            p2 = self.pland2
            self.ptot[...] = (p2[0] + p2[1]) + (p2[2] + p2[3])


def _adamw(ws, gs, ms, vs, rewrite):
    n = len(ws)
    first_row = [g[1] if isinstance(g, tuple) else None for g in gs]
    gs = [g[0] if isinstance(g, tuple) else g for g in gs]
    regrouped = [k for k in range(n) if first_row[k] is not None or gs[k].shape != ws[k].shape or rewrite[k]]
    assert all(w.shape[-1] % LANES == 0 for w, row in zip(ws, first_row) if row is not None)

    def body(*refs):
        for k in range(n):
            w_ref, g_ref, m_ref, v_ref, d_ref, nm_ref, nv_ref = (refs[j * n + k] for j in range(7))
            row = first_row[k]
            windows = [(..., ...)]
            if row is not None and len(w_ref.shape) == 1:
                windows = [(pl.ds(j * LANES, LANES), row + j) for j in range(w_ref.shape[0] // LANES)]
            elif row is not None:
                windows = [(..., pl.ds(row, w_ref.shape[0]))]
            elif g_ref.shape != w_ref.shape:
                windows = [((slice(None), h), (slice(None), slice(h * LANES, h * LANES + w_ref.shape[2]))) for h in range(w_ref.shape[1])]
            if k in regrouped:
                g_out_ref = refs[7 * n + regrouped.index(k)]
            for at, g_at in windows:
                w, g, m, v = w_ref[at], g_ref[g_at], m_ref[at], v_ref[at]
                m = ADAM_B1 * m + (1.0 - ADAM_B1) * g
                v = ADAM_B2 * v + (1.0 - ADAM_B2) * jnp.square(g)
                m_hat = m / (1.0 - ADAM_B1 ** ADAM_STEP)
                v_hat = v / (1.0 - ADAM_B2 ** ADAM_STEP)
                d_ref[at] = -ADAM_LR * (m_hat / (jnp.sqrt(v_hat) + ADAM_EPS) + ADAM_WD * w)
                nm_ref[at] = m
                nv_ref[at] = v
                if k in regrouped:
                    g_out_ref[at] = g

    shapes = [SDS(w.shape, F32) for w in ws]
    outs = pl.pallas_call(
        body, name="adamw", in_specs=[VMEM_SPEC] * (4 * n), out_specs=[VMEM_SPEC] * (3 * n + len(regrouped)),
        out_shape=shapes * 3 + [shapes[k] for k in regrouped], compiler_params=_cp(),
    )(*ws, *gs, *ms, *vs)
    grads = list(gs)
    for k, g in zip(regrouped, outs[3 * n:]):
        grads[k] = g
    return outs[:n], outs[n:2 * n], outs[2 * n:3 * n], grads


WEIGHTS = ("norm_in", "w_in", "q_norm", "w_uq", "kv_norm", "w_ukv", "pool_w", "pool_scale", "w_branch_attn", "w_branch_pool",
           "w_out", "norm_final")
SUBLANES = 8


def _pack_rows(a):
    a = a.reshape(-1, LANES)
    return jnp.pad(a, ((0, -a.shape[0] % SUBLANES), (0, 0)))


def _packed_at(like):
    out, row = [], 0
    for a in like:
        out.append(row)
        rows = a.size // LANES
        row += rows + (-rows % SUBLANES)
    return out


def kernel(x, norm_in, w_in, q_norm, w_uq, kv_norm, w_ukv, pool_w, pool_scale, w_branch_attn, w_branch_pool, w_out, norm_final, loss_target, m_norm_in, m_w_in, m_q_norm, m_w_uq, m_kv_norm, m_w_ukv, m_pool_w, m_pool_scale, m_w_branch_attn, m_w_branch_pool, m_w_out, m_norm_final, v_norm_in, v_w_in, v_q_norm, v_w_uq, v_kv_norm, v_w_ukv, v_pool_w, v_pool_scale, v_w_branch_attn, v_w_branch_pool, v_w_out, v_norm_final):
    w = dict(norm_in=norm_in, w_in=w_in, q_norm=q_norm, w_uq=w_uq, kv_norm=kv_norm, w_ukv=w_ukv, pool_w=pool_w, pool_scale=pool_scale,
             w_branch_attn=w_branch_attn, w_branch_pool=w_branch_pool, w_out=w_out, norm_final=norm_final)
    m = dict(norm_in=m_norm_in, w_in=m_w_in, q_norm=m_q_norm, w_uq=m_w_uq, kv_norm=m_kv_norm, w_ukv=m_w_ukv, pool_w=m_pool_w,
             pool_scale=m_pool_scale, w_branch_attn=m_w_branch_attn, w_branch_pool=m_w_branch_pool, w_out=m_w_out, norm_final=m_norm_final)
    v = dict(norm_in=v_norm_in, w_in=v_w_in, q_norm=v_q_norm, w_uq=v_w_uq, kv_norm=v_kv_norm, w_ukv=v_w_ukv, pool_w=v_pool_w,
             pool_scale=v_pool_scale, w_branch_attn=v_w_branch_attn, w_branch_pool=v_w_branch_pool, w_out=v_w_out, norm_final=v_norm_final)

    def as2d(name, a):
        if name == "w_in":
            return a.T
        if name in ("w_uq", "w_ukv"):
            return a
        return a.reshape(-1, GROUP) if name == "pool_w" else a

    def unshape(name, a):
        return a.T if name == "w_in" else a.reshape(w[name].shape)

    w_in_pad, w_uq_full, w_ukv_full = _all_gather_bf16([as2d(k, w[k]) for k in ("w_in", "w_uq", "w_ukv")])
    loss, grad_x, g2d = _local_step(
        x.reshape(x.shape[1:]), loss_target.reshape(x.shape[1:]), norm_in, w_in_pad, q_norm,
        w_uq_full.reshape(Q_RANK, HW), kv_norm, w_ukv_full.reshape(KV_RANK, HW),
        pool_w, pool_scale, [w_branch_attn, w_branch_pool, w_out], norm_final)

    deltas, new_m, new_v, grads = _adamw([as2d(k, w[k]) for k in WEIGHTS], [g2d[k] for k in WEIGHTS],
                                         [as2d(k, m[k]) for k in WEIGHTS], [as2d(k, v[k]) for k in WEIGHTS],
                                         [k in ("w_in", "w_branch_attn", "w_branch_pool", "w_out") for k in WEIGHTS])
    shaped = lambda arrs: [unshape(k, a) for k, a in zip(WEIGHTS, arrs)]
    return (loss, grad_x.reshape(x.shape), *shaped(grads), *shaped(deltas), *shaped(new_m), *shaped(new_v))
```

```python
import functools

import jax
import jax.numpy as jnp
import numpy as np
from jax import lax
from jax.experimental import pallas as pl
from jax.experimental.pallas import tpu as pltpu

F32 = jnp.float32
BF16 = jnp.bfloat16
SDS = jax.ShapeDtypeStruct

D_MODEL = 1024
HEADS = 8
NOPE = 64
ROPE = 32
VDIM = 64
Q_RANK = 384
KV_RANK = 256
MLA_W = HEADS * VDIM
POOL_W = 512
POOL_GROUPS = 4
GROUP = POOL_W // POOL_GROUPS
CHUNK = 64
ROPE_THETA = 10000.0
EPS = 1e-6
SCALE = (NOPE + ROPE) ** -0.5
LOG2E = 1.4426950408889634
LN2 = 0.6931471805599453
QK_SCALE_LOG2 = SCALE * LOG2E
IN_TOTAL = 4256
ADAM_LR, ADAM_B1, ADAM_B2, ADAM_EPS, ADAM_WD, ADAM_STEP = 0.001, 0.9, 0.999, 1e-08, 0.01, 10

N_DEV = 8
LANES = 128
HEAD_PAD = LANES
HW = HEADS * HEAD_PAD

ZQ, ZKV, ZKR, GA, UP, GP, GM, ZTOT = 0, 384, 640, 768, 1280, 1792, 2304, 4352
FRONT_W = GA
ZKR_ORIG = 640

VMEM_LIMIT = 62 * 1024 * 1024


def _cp(sem=None, **kw):
    if sem is not None:
        kw["dimension_semantics"] = sem
    return pltpu.CompilerParams(vmem_limit_bytes=VMEM_LIMIT, **kw)


def _mm(a, b):
    return lax.dot_general(a, b, (((1,), (0,)), ((), ())), preferred_element_type=F32)


def _mm_nt(a, b):
    return lax.dot_general(a, b, (((1,), (1,)), ((), ())), preferred_element_type=F32)


def _mm_tn(a, b):
    return lax.dot_general(a, b, (((0,), (0,)), ((), ())), preferred_element_type=F32)


def _row_spec(tm, w):
    return pl.BlockSpec((tm, w), lambda i: (i, 0))


def _full_spec(a):
    nd = len(a.shape)
    return pl.BlockSpec(a.shape, lambda *_: (0,) * nd)


def _rope(v, c, sa, sb, sign):
    n = v.shape[-1]
    reps = n // LANES
    if reps > 1:
        c, sa, sb = (jnp.tile(t, (1, reps)) for t in (c, sa, sb))
    up = pltpu.roll(v, n - ROPE // 2, 1)
    dn = pltpu.roll(v, ROPE // 2, 1)
    return v * c + sign * (up * sa + dn * sb)


def _rope_tables(T):
    half = ROPE // 2
    inv_freq = np.float32(ROPE_THETA) ** (-np.arange(half, dtype=np.float32) / np.float32(half))
    ang = np.arange(T, dtype=np.float32)[:, None] * inv_freq[None, :].astype(np.float32)
    cos, sin = np.cos(ang.astype(np.float64)).astype(np.float32), np.sin(ang.astype(np.float64)).astype(np.float32)
    z16 = np.zeros((T, half), np.float32)
    z32 = np.zeros((T, LANES - NOPE - ROPE), np.float32)
    c = np.concatenate([np.ones((T, NOPE), np.float32), cos, cos, z32], axis=1)
    sa = np.concatenate([np.zeros((T, NOPE), np.float32), -sin, z16, z32], axis=1)
    sb = np.concatenate([np.zeros((T, NOPE), np.float32), z16, sin, z32], axis=1)
    return jnp.asarray(c), jnp.asarray(sa), jnp.asarray(sb)


def _silu_parts(g):
    sg = jax.nn.sigmoid(g)
    return g * sg, sg + g * sg * (1.0 - sg)


def _in_proj(x2, norm_in, w_in_pad, q_norm, wuq_pad, kv_norm, wukv, rc, rsa, rsb, tm, late_shards):
    T = x2.shape[0]
    steps = T // tm
    n = len(late_shards)

    def body(x_ref, nin_ref, win_ref, qn_ref, wuq_ref, kvn_ref, wukv_ref, c_ref, sa_ref, sb_ref, *rest):
        hn_ref, zgm_ref, zga_ref, zup_ref, zgp_ref, zfr_ref, q_ref, k_ref, v_ref, vt_ref = rest[n:n + 10]
        ag = _AllGather(rest[:n], rest[n + 10:2 * n + 10], rest[2 * n + 10:])
        step = pl.program_id(0)
        pl.when(step == 0)(ag.start)
        pl.when(step == min(3, steps - 1))(ag.forward)
        xf = x_ref[...]
        r = lax.rsqrt(jnp.mean(xf * xf, axis=-1, keepdims=True) + EPS)
        hn = (xf * r * nin_ref[...]).astype(BF16)
        hn_ref[...] = hn
        zfr = _mm_nt(hn, win_ref[ZQ:GA, :])
        zfr_ref[...] = zfr
        zq, zkv, zkr = zfr[:, ZQ:ZKV], zfr[:, ZKV:ZKR], zfr[:, ZKR:GA]
        c, sa, sb = c_ref[...], sa_ref[...], sb_ref[...]
        rq = lax.rsqrt(jnp.mean(zq * zq, axis=-1, keepdims=True) + EPS)
        cq = (zq * rq * qn_ref[...].reshape(1, Q_RANK)).astype(BF16)
        rkv = lax.rsqrt(jnp.mean(zkv * zkv, axis=-1, keepdims=True) + EPS)
        ckv = (zkv * rkv * kvn_ref[...]).astype(BF16)
        zga_ref[...] = _mm_nt(hn, win_ref[GA:UP, :])
        zup_ref[...] = _mm_nt(hn, win_ref[UP:GP, :])
        zgp_ref[...] = _mm_nt(hn, win_ref[GP:GM, :])
        q_raw = _mm(cq, wuq_ref[...])
        kv = _mm(ckv, wukv_ref[...])
        zgm_ref[...] = _mm_nt(hn, win_ref[GM:ZTOT, :])
        q = _rope(q_raw, c, sa, sb, 1.0)
        q_ref[...] = (q * QK_SCALE_LOG2).astype(BF16)
        kr = _rope(zkr, c, sa, sb, 1.0)
        lane = lax.broadcasted_iota(jnp.int32, kv.shape, 1) % LANES
        k_ref[...] = jnp.where(lane < NOPE, kv, jnp.tile(kr, (1, HEADS))).astype(BF16)
        v = jnp.where(lane < NOPE, 1.0, kv).astype(BF16)
        v_ref[...] = v
        vt_ref[...] = v.T
        pl.when(step == steps - 1)(ag.finish)

    ins = (x2, norm_in, w_in_pad, q_norm, wuq_pad, kv_norm, wukv, rc, rsa, rsb)
    in_specs = [_row_spec(tm, D_MODEL), _full_spec(norm_in), _full_spec(w_in_pad), _full_spec(q_norm), _full_spec(wuq_pad),
                _full_spec(kv_norm), _full_spec(wukv), _row_spec(tm, LANES), _row_spec(tm, LANES), _row_spec(tm, LANES)]
    widths = [(D_MODEL, BF16), (ZTOT - GM, F32), (UP - GA, F32), (GP - UP, F32), (GM - GP, F32), (FRONT_W, F32),
              (HW, BF16), (HW, BF16), (HW, BF16)]
    return pl.pallas_call(
        body, name="in_proj", grid=(steps,), in_specs=in_specs + [_full_spec(s) for s in late_shards],
        out_specs=[_row_spec(tm, w) for w, _ in widths] + [pl.BlockSpec((HW, tm), lambda i: (0, i))] + [HBM_SPEC] * n,
        out_shape=[SDS((T, w), dt) for w, dt in widths] + [SDS((HW, T), BF16)]
        + [SDS((N_DEV,) + s.shape, BF16) for s in late_shards],
        scratch_shapes=_ag_scratch([s.shape for s in late_shards]), compiler_params=_cp(("arbitrary",)),
    )(*ins, *late_shards)


def _chunk_mask(n_q, n_k, q_off, transposed):
    shape = (n_k, n_q) if transposed else (n_q, n_k)
    q = (lax.broadcasted_iota(jnp.int32, shape, 1 if transposed else 0) + q_off) // CHUNK
    k = lax.broadcasted_iota(jnp.int32, shape, 0 if transposed else 1) // CHUNK
    return k <= q


def _store_pair_rows(ref, k, pair):
    t = pair.T
    ref[k, 0:1, :] = t[0:1, :]
    ref[k, 1:2, :] = t[VDIM:VDIM + 1, :]


def _attn_fwd(q_att, k_att, vt_att, tq, hps):
    T = q_att.shape[0]
    head_lanes = [slice(h * LANES, (h + 1) * LANES) for h in range(hps)]

    def body(q_ref, k_ref, vt_ref, o_ref, lser_ref):
        i = pl.program_id(1)
        mask = _chunk_mask(tq, tq, 0, True)
        lane = lax.broadcasted_iota(jnp.int32, (tq, LANES), 1)
        qs = [q_ref[:, hs] for hs in head_lanes]

        def step(j, carry, masked):
            off = pl.multiple_of(j * tq, tq)
            sts = [_mm_nt(k_ref[pl.ds(off, tq), hs], qh) for qh, hs in zip(qs, head_lanes)]
            if masked:
                sts = [jnp.where(mask, st, -jnp.inf) for st in sts]
            ms = [jnp.maximum(m, jnp.max(st, axis=0, keepdims=True)) for (m, _), st in zip(carry, sts)]
            pts = [jnp.exp2(st - m_new).astype(BF16) for st, m_new in zip(sts, ms)]
            return tuple((m_new, jnp.exp2(m - m_new) * acc + _mm(vt_ref[hs, pl.ds(off, tq)], pt))
                         for (m, acc), m_new, pt, hs in zip(carry, ms, pts, head_lanes))

        init = ((jnp.full((1, tq), -jnp.inf, F32), jnp.zeros((LANES, tq), F32)),) * hps
        res = step(i, lax.fori_loop(0, i, functools.partial(step, masked=False), init), True)
        for pair in range(hps // 2):
            (ma, acca), (mb, accb) = res[2 * pair], res[2 * pair + 1]
            la, lb = acca[:1], accb[:1]
            oa, ob = (acca / la).T, (accb / lb).T
            o_ref[:, pair * LANES:(pair + 1) * LANES] = jnp.where(lane < VDIM, pltpu.roll(oa, VDIM, 1), ob)
            lser_ref[pair, 0:1, :] = ma + jnp.log2(la)
            lser_ref[pair, 1:2, :] = mb + jnp.log2(lb)

    qspec = pl.BlockSpec((tq, hps * LANES), lambda p, i: (i, p))
    kspec = pl.BlockSpec((T, hps * LANES), lambda p, i: (0, p))
    vspec = pl.BlockSpec((hps * LANES, T), lambda p, i: (p, 0))
    ospec = pl.BlockSpec((tq, hps * VDIM), lambda p, i: (i, p))
    return pl.pallas_call(
        body, name="attn_fwd", grid=(HEADS // hps, T // tq), in_specs=[qspec, kspec, vspec],
        out_specs=[ospec, pl.BlockSpec((hps // 2, 2, tq), lambda p, i: (p, 0, i))],
        out_shape=[SDS((T, MLA_W), F32), SDS((HEADS // 2, 2, T), F32)],
        compiler_params=_cp(("parallel", "parallel")),
    )(q_att, k_att, vt_att)


def _pick(g, vals):
    out = vals[-1]
    for k in range(len(vals) - 2, -1, -1):
        out = jnp.where(g == k, vals[k], out)
    return out


def _window_sum(u, g, forward):
    T = u.shape[0]
    row = lax.broadcasted_iota(jnp.int32, u.shape, 0)

    def sh(s, k):
        if forward:
            return jnp.where(row >= k, pltpu.roll(s, k, 0), 0.0)
        return jnp.where(row < T - k, pltpu.roll(s, T - k, 0), 0.0)

    sums, s = [], u
    for k in (1, 2, 4, 8):
        s = s + sh(s, k)
        sums.append(s)
    return _pick(g, sums)


MAX_WINDOW = 16


def _pool_inv_count(shape, g):
    T, n = shape
    row = lax.broadcasted_iota(jnp.int32, (MAX_WINDOW, n), 0)
    head = 1.0 / jnp.minimum(row + 1, lax.shift_left(jnp.int32(2), g)).astype(F32)
    inv_w = _pick(g, [0.5, 0.25, 0.125, 0.0625])
    return jnp.concatenate([head, jnp.broadcast_to(inv_w, (T - MAX_WINDOW, n)).astype(F32)], axis=0)


def _pool_fwd(zup, zgp, pool_w, pool_scale):
    T = zup.shape[0]

    def body(u_ref, g_ref, w_ref, sc_ref, y_ref):
        g = pl.program_id(0)
        u = u_ref[...]
        d = _window_sum(u, g, True) * _pool_inv_count(u.shape, g) - u
        lin = _mm(d.astype(BF16), w_ref[0].astype(BF16))
        silu, _ = _silu_parts(g_ref[...])
        y_ref[...] = (lin * sc_ref[...] * silu).astype(BF16)

    col = pl.BlockSpec((T, GROUP), lambda g: (0, g))
    return pl.pallas_call(
        body, name="pool_fwd", grid=(POOL_GROUPS,),
        in_specs=[col, col, pl.BlockSpec((1, GROUP, GROUP), lambda g: (g, 0, 0)), pl.BlockSpec((1, GROUP), lambda g: (0, g))],
        out_specs=col, out_shape=SDS((T, POOL_W), BF16), compiler_params=_cp(("parallel",)),
    )(zup, zgp, pool_w, pool_scale)


def _pool_bwd(zup, zgp, dyp, pool_w, pool_scale):
    T = zup.shape[0]

    def body(u_ref, g_ref, dy_ref, w_ref, sc_ref, du_ref, dg_ref, gw_ref, gsc_ref):
        g = pl.program_id(0)
        u = u_ref[...]
        inv = _pool_inv_count(u.shape, g)
        d = (_window_sum(u, g, True) * inv - u).astype(BF16)
        wb = w_ref[0].astype(BF16)
        lin = _mm(d, wb)
        sc = sc_ref[...]
        silu, dsilu = _silu_parts(g_ref[...])
        dy = dy_ref[...]
        dg_ref[...] = (dy * lin * sc * dsilu).astype(BF16)
        dpre = dy * silu
        gsc_ref[...] = jnp.sum(dpre * lin, axis=0, keepdims=True)
        dlin = (dpre * sc).astype(BF16)
        gw_ref[0] = _mm_tn(d, dlin)
        dd = _mm_nt(dlin, wb)
        du_ref[...] = (_window_sum(dd * inv, g, False) - dd).astype(BF16)

    col = pl.BlockSpec((T, GROUP), lambda g: (0, g))
    wspec = pl.BlockSpec((1, GROUP, GROUP), lambda g: (g, 0, 0))
    vspec = pl.BlockSpec((1, GROUP), lambda g: (0, g))
    return pl.pallas_call(
        body, name="pool_bwd", grid=(POOL_GROUPS,), in_specs=[col, col, col, wspec, vspec], out_specs=[col, col, wspec, vspec],
        out_shape=[SDS((T, POOL_W), BF16), SDS((T, POOL_W), BF16), SDS((POOL_GROUPS, GROUP, GROUP), F32), SDS((1, POOL_W), F32)],
        compiler_params=_cp(("parallel",)),
    )(zup, zgp, dyp, pool_w, pool_scale)


def _tail(x2, tgt, o, zga, ypool, zgm, wba, wbp, wout, norm_final, tm):
    T = x2.shape[0]
    steps = T // tm
    cols = D_MODEL // N_DEV

    def body(x_ref, tgt_ref, o_ref, zga_ref, yp_ref, zgm_ref, wba_ref, wbp_ref, wout_ref, nf_ref,
             loss_ref, dh_ref, dgm_ref, doop_ref, dga_ref, dcapr_ref, dyp_ref, swout_ref, swba_ref, swbp_ref, gnf_ref,
             gwout_ref, gwba_ref, gwbp_ref):
        @pl.when(pl.program_id(0) == 0)
        def _():
            for ref in (loss_ref, gwout_ref, gwba_ref, gwbp_ref, gnf_ref):
                ref[...] = jnp.zeros_like(ref)

        o_v = o_ref[...]
        silu, dsilu = _silu_parts(zga_ref[...])
        ya = (o_v * silu).astype(BF16)
        yp = yp_ref[...]
        wba_v = jnp.concatenate([wba_ref[k] for k in range(N_DEV)], axis=1)
        wbp_v = jnp.concatenate([wbp_ref[k] for k in range(N_DEV)], axis=1)
        wout_v = wout_ref[...]
        a = _mm(ya, wba_v)
        p = _mm(yp, wbp_v)
        gate = jax.nn.sigmoid(zgm_ref[...])
        ga, gp = gate[:, :D_MODEL], gate[:, D_MODEL:]
        mg = (ga * a + gp * p).astype(BF16)
        h = x_ref[...] + _mm(mg, wout_v)
        r = lax.rsqrt(jnp.mean(h * h, axis=-1, keepdims=True) + EPS)
        gf = nf_ref[...]
        hr = h * r
        e = hr * gf - tgt_ref[...]
        loss_ref[...] += (0.5 / D_MODEL) * jnp.sum(e * e)
        dy = e * (1.0 / D_MODEL)
        gnf_ref[...] += jnp.sum(dy * hr, axis=0, keepdims=True)
        u = dy * gf
        dh = r * (u - hr * jnp.mean(u * hr, axis=-1, keepdims=True))
        dh_ref[...] = dh
        dhb = dh.astype(BF16)
        dmg = _mm_nt(dhb, wout_v)
        dab = (dmg * ga).astype(BF16)
        dpb = (dmg * gp).astype(BF16)
        dya = _mm_nt(dab, wba_v)
        dyp_ref[...] = _mm_nt(dpb, wbp_v)
        gwout_ref[...] += _mm_tn(mg, dhb)
        gwba_ref[...] += _mm_tn(ya, dab)
        gwbp_ref[...] += _mm_tn(yp, dpb)
        dgm_ref[:, :D_MODEL] = (dmg * a * ga * (1.0 - ga)).astype(BF16)
        dgm_ref[:, D_MODEL:] = (dmg * p * gp * (1.0 - gp)).astype(BF16)
        do = dya * silu
        dga_ref[...] = (dya * o_v * dsilu).astype(BF16)
        prod = do * o_v
        lo = lax.broadcasted_iota(jnp.int32, (tm, LANES), 1) < VDIM
        for pair in range(HEADS // 2):
            ls = slice(pair * LANES, (pair + 1) * LANES)
            do_p, prod_p = do[:, ls], prod[:, ls]
            dcap_a = jnp.sum(jnp.where(lo, prod_p, 0.0), axis=-1, keepdims=True)
            dcap_b = jnp.sum(jnp.where(lo, 0.0, prod_p), axis=-1, keepdims=True)
            _store_pair_rows(dcapr_ref, pair, jnp.where(lo, dcap_a, dcap_b))
            doop_ref[:, 2 * pair * LANES:(2 * pair + 1) * LANES] = jnp.where(lo, 0.0, pltpu.roll(do_p, VDIM, 1)).astype(BF16)
            doop_ref[:, (2 * pair + 1) * LANES:(2 * pair + 2) * LANES] = jnp.where(lo, 0.0, do_p).astype(BF16)

        @pl.when(pl.program_id(0) == steps - 1)
        def _():
            for k in range(N_DEV):
                swout_ref[k] = gwout_ref[k * cols:(k + 1) * cols, :].astype(BF16)
                swba_ref[k] = gwba_ref[:, k * cols:(k + 1) * cols].astype(BF16)
                swbp_ref[k] = gwbp_ref[:, k * cols:(k + 1) * cols].astype(BF16)

    ins = (x2, tgt, o, zga, ypool, zgm, wba, wbp, wout, norm_final)
    in_specs = [_row_spec(tm, D_MODEL), _row_spec(tm, D_MODEL), _row_spec(tm, MLA_W), _row_spec(tm, MLA_W), _row_spec(tm, POOL_W),
                _row_spec(tm, 2 * D_MODEL), _full_spec(wba), _full_spec(wbp), _full_spec(wout), _full_spec(norm_final)]
    outs = [SDS((8, LANES), F32), SDS((T, D_MODEL), F32), SDS((T, 2 * D_MODEL), BF16), SDS((T, HW), BF16), SDS((T, MLA_W), BF16),
            SDS((HEADS // 2, 2, T), F32), SDS((T, POOL_W), F32),
            SDS((N_DEV, cols, D_MODEL), BF16), SDS((N_DEV, MLA_W, cols), BF16), SDS((N_DEV, POOL_W, cols), BF16), SDS((1, D_MODEL), F32)]
    out_specs = [_full_spec(outs[0]), _row_spec(tm, D_MODEL), _row_spec(tm, 2 * D_MODEL), _row_spec(tm, HW), _row_spec(tm, MLA_W),
                 pl.BlockSpec((HEADS // 2, 2, tm), lambda i: (0, 0, i)), _row_spec(tm, POOL_W),
                 _full_spec(outs[7]), _full_spec(outs[8]), _full_spec(outs[9]), _full_spec(outs[10])]
    return pl.pallas_call(
        body, name="tail", grid=(steps,), in_specs=in_specs, out_specs=out_specs, out_shape=outs,
        scratch_shapes=[pltpu.VMEM((D_MODEL, D_MODEL), F32), pltpu.VMEM((MLA_W, D_MODEL), F32), pltpu.VMEM((POOL_W, D_MODEL), F32)],
        compiler_params=_cp(("arbitrary",)),
    )(*ins)


def _attn_bwd(q_att, k_att, v_att, doop, lse_rows, dcap_rows, tq, hps, slabs, packed):
    T = q_att.shape[0]
    nq = T // tq
    n = len(slabs)
    groups = HEADS // hps
    head_lanes = [slice(h * LANES, (h + 1) * LANES) for h in range(hps)]

    def body(q_ref, k_ref, v_ref, doop_ref, lse_ref, dcap_ref, *rest):
        slab_refs, packed_ref = rest[:n], rest[n]
        dq_ref, dkv_ref, dkr_ref = rest[n + 1:n + 4]
        sum_refs, ptot_ref = rest[n + 4:2 * n + 4], rest[2 * n + 4]
        dq_acc = rest[2 * n + 5]
        rs = _ReduceScatter(slab_refs, packed_ref, sum_refs, ptot_ref, rest[2 * n + 6:])
        group, j = pl.program_id(0), pl.program_id(1)
        tick, last = group * nq + j, groups * nq - 1
        pl.when(tick == 0)(rs.start1)
        pl.when(tick == min(1, last))(rs.finish1_start2)
        pl.when(tick == min(3, last))(rs.relay2)
        mask = _chunk_mask(tq, tq, 0, True)
        lane = lax.broadcasted_iota(jnp.int32, (tq, LANES), 1)
        ks = [k_ref[:, hs] for hs in head_lanes]
        vs = [v_ref[:, hs] for hs in head_lanes]
        kts = [kh.T for kh in ks]

        @pl.when(j == 0)
        def _():
            dq_acc[...] = jnp.zeros_like(dq_acc)

        def step(i, carry, masked):
            rows = pl.ds(pl.multiple_of(i * tq, tq), tq)
            heads = range(hps)
            stat = lambda h: (h // 2, slice(h % 2, h % 2 + 1), rows)
            qhs = [q_ref[rows, hs] for hs in head_lanes]
            doops = [doop_ref[rows, hs] for hs in head_lanes]
            sts = [_mm_nt(ks[h], qhs[h]) for h in heads]
            dpts = [_mm_nt(vs[h], doops[h]) for h in heads]
            pts = [jnp.exp2(sts[h] - lse_ref[stat(h)]) for h in heads]
            if masked:
                pts = [jnp.where(mask, pt, 0.0) for pt in pts]
            dsts = [(pts[h] * (dpts[h] - dcap_ref[stat(h)])).astype(BF16) for h in heads]
            dvs = [_mm(pts[h].astype(BF16), doops[h]) for h in heads]
            dks = [_mm(dsts[h], qhs[h]) for h in heads]
            for h, hs in enumerate(head_lanes):
                dq_acc[hs, rows] += _mm(kts[h], dsts[h])
            return tuple((dk + dks[h], dv + dvs[h]) for h, (dk, dv) in enumerate(carry))

        zero = jnp.zeros((tq, LANES), F32)
        carry = step(j, ((zero, zero),) * hps, True)
        res = lax.fori_loop(j + 1, nq, functools.partial(step, masked=False), carry)
        dkr = None
        for (dk, dv), hs in zip(res, head_lanes):
            dk = dk * LN2
            dkv_ref[:, hs] = jnp.where(lane < NOPE, dk, dv).astype(BF16)
            dkr = dk if dkr is None else dkr + dk
        dkr_ref[0] = jnp.where((lane >= NOPE) & (lane < NOPE + ROPE), dkr, 0.0)

        @pl.when(j == nq - 1)
        def _():
            dq_ref[...] = (dq_acc[...] * SCALE).T.astype(BF16)

        pl.when(tick == last)(rs.finish2)

    kspec = pl.BlockSpec((tq, hps * LANES), lambda p, j: (j, p))
    qspec = pl.BlockSpec((T, hps * LANES), lambda p, j: (0, p))
    rspec = pl.BlockSpec((hps // 2, 2, T), lambda p, j: (p, 0, 0))
    sums = [SDS(s.shape[1:], F32) for s in slabs] + [SDS(packed.shape, F32)]
    return pl.pallas_call(
        body, name="attn_bwd", grid=(groups, nq),
        in_specs=[qspec, kspec, kspec, qspec, rspec, rspec] + [HBM_SPEC] * n + [_full_spec(packed)],
        out_specs=[qspec, kspec, pl.BlockSpec((1, tq, LANES), lambda p, j: (p, j, 0))] + [_full_spec(s) for s in sums],
        out_shape=[SDS((T, HW), BF16), SDS((T, HW), BF16), SDS((groups, T, LANES), F32)] + sums,
        scratch_shapes=[pltpu.VMEM((hps * LANES, T), F32)] + _rs_scratch([s.shape for s in sums[:-1]], packed.shape),
        compiler_params=_cp(("arbitrary", "arbitrary")),
    )(q_att, k_att, v_att, doop, lse_rows, dcap_rows, *slabs, packed)


def _rms_bwd(z, gain, dout):
    r = lax.rsqrt(jnp.mean(z * z, axis=-1, keepdims=True) + EPS)
    zr = z * r
    u = dout * gain
    return r * (u - zr * jnp.mean(u * zr, axis=-1, keepdims=True)), jnp.sum(dout * zr, axis=0, keepdims=True)


def _mla_bwd(dq_att, dkv_nat, dkr4, zfr, q_norm, wuq_pad, kv_norm, wukv, rc, rsa, rsb, tm):
    T = dq_att.shape[0]

    def body(dq_ref, dkv_ref, dkr_ref, zfr_ref, qn_ref, wuq_ref, kvn_ref, wukv_ref, c_ref, sa_ref, sb_ref,
             dfr_ref, gwuq_out, gwukv_out, gqn_ref, gkvn_ref, gwuq_ref, gwukv_ref):
        @pl.when(pl.program_id(0) == 0)
        def _():
            for ref in (gwuq_ref, gwukv_ref, gqn_ref, gkvn_ref):
                ref[...] = jnp.zeros_like(ref)

        c, sa, sb = c_ref[...], sa_ref[...], sb_ref[...]
        zq, zkv = zfr_ref[:, :Q_RANK], zfr_ref[:, Q_RANK:Q_RANK + KV_RANK]
        qn, kvn = qn_ref[...].reshape(1, Q_RANK), kvn_ref[...]
        dkv = dkv_ref[...]
        dckv = _mm_nt(dkv, wukv_ref[...])
        ckv = (zkv * lax.rsqrt(jnp.mean(zkv * zkv, axis=-1, keepdims=True) + EPS) * kvn).astype(BF16)
        gwukv_ref[...] += _mm_tn(ckv, dkv)
        cq = (zq * lax.rsqrt(jnp.mean(zq * zq, axis=-1, keepdims=True) + EPS) * qn).astype(BF16)
        dq = _rope(dq_ref[...].astype(F32), c, sa, sb, -1.0).astype(BF16)
        dzkv, gkvn = _rms_bwd(zkv, kvn, dckv)
        gkvn_ref[...] += gkvn
        gwuq_ref[...] += _mm_tn(cq, dq)
        dzq, gqn = _rms_bwd(zq, qn, _mm_nt(dq, wuq_ref[...]))
        gqn_ref[...] += gqn
        dkr = functools.reduce(lambda a, b: a + b, [dkr_ref[g] for g in range(dkr4.shape[0])])
        dfr_ref[:, :Q_RANK] = dzq.astype(BF16)
        dfr_ref[:, Q_RANK:Q_RANK + KV_RANK] = dzkv.astype(BF16)
        dfr_ref[:, Q_RANK + KV_RANK:] = _rope(dkr, c, sa, sb, -1.0).astype(BF16)

        @pl.when(pl.program_id(0) == pl.num_programs(0) - 1)
        def _():
            gwuq_out[...] = gwuq_ref[...].astype(BF16)
            gwukv_out[...] = gwukv_ref[...].astype(BF16)

    ins = (dq_att, dkv_nat, dkr4, zfr, q_norm, wuq_pad, kv_norm, wukv, rc, rsa, rsb)
    in_specs = [_row_spec(tm, HW), _row_spec(tm, HW), pl.BlockSpec((dkr4.shape[0], tm, LANES), lambda i: (0, i, 0)), _row_spec(tm, FRONT_W),
                _full_spec(q_norm), _full_spec(wuq_pad), _full_spec(kv_norm), _full_spec(wukv),
                _row_spec(tm, LANES), _row_spec(tm, LANES), _row_spec(tm, LANES)]
    outs = [SDS((T, FRONT_W), BF16), SDS((Q_RANK, HW), BF16), SDS((KV_RANK, HW), BF16), SDS((1, Q_RANK), F32), SDS((1, KV_RANK), F32)]
    out_specs = [_row_spec(tm, FRONT_W)] + [_full_spec(s) for s in outs[1:]]
    return pl.pallas_call(
        body, name="mla_bwd", grid=(T // tm,), in_specs=in_specs, out_specs=out_specs, out_shape=outs,
        scratch_shapes=[pltpu.VMEM((Q_RANK, HW), F32), pltpu.VMEM((KV_RANK, HW), F32)],
        compiler_params=_cp(("arbitrary",)),
    )(*ins)


_DZ_COLS = ((GM, ZTOT), (GA, UP), (UP, GP), (GP, GM), (ZQ, GA))


def _in_proj_bwd_x(dzs, x2, dh, norm_in, w_in_pad, tm, slabs, late_rows):
    T = x2.shape[0]
    steps = T // tm
    n, m = len(slabs), len(late_rows)
    firsts = _packed_at([norm_in] + list(late_rows))
    packed_rows = firsts[-1] + -(-(late_rows[-1].size // LANES) // SUBLANES) * SUBLANES
    n_rs_scratch = len(_rs_scratch([s.shape[1:] for s in slabs], None))

    def body(d0, d1, d2, d3, d4, x_ref, dh_ref, nin_ref, win_ref, *rest):
        slab_refs, row_refs, (gx_ref, tot_ref), sum_refs = rest[:n], rest[n:n + m], rest[n + m:n + m + 2], rest[n + m + 2:2 * n + m + 2]
        scratch = rest[2 * n + m + 2:]
        rs = _ReduceScatter(slab_refs, None, sum_refs, None, scratch[:n_rs_scratch])
        gnin_ref, packed_ref = scratch[n_rs_scratch:n_rs_scratch + 2]
        step = pl.program_id(0)

        @pl.when(step == 0)
        def _():
            gnin_ref[...] = jnp.zeros_like(gnin_ref)
            rs.start1()

        pl.when(step == min(2, steps - 1))(rs.finish1_start2)
        pl.when(step == min(steps * 11 // 16, steps - 1))(rs.relay2)
        dhn = None
        for ref, (lo, hi) in zip((d0, d1, d2, d3, d4), _DZ_COLS):
            t = _mm(ref[...], win_ref[lo:hi, :])
            dhn = t if dhn is None else dhn + t
        dx, gnin = _rms_bwd(x_ref[...], nin_ref[...], dhn)
        gnin_ref[...] += gnin
        gx_ref[...] = dx + dh_ref[...]

        @pl.when(step == steps - 1)
        def _():
            packed_ref[...] = jnp.zeros_like(packed_ref)
            for ref, first in zip((gnin_ref,) + tuple(row_refs), firsts):
                for j in range(ref.shape[1] // LANES):
                    packed_ref[first + j:first + j + 1, :] = ref[:, j * LANES:(j + 1) * LANES]
            closing = _ReduceScatter((), packed_ref, (), tot_ref, scratch[n_rs_scratch + 2:])
            closing.start1()
            rs.finish2()
            closing.finish1_start2()
            closing.finish2()

    in_specs = ([_row_spec(tm, hi - lo) for lo, hi in _DZ_COLS] + [_row_spec(tm, D_MODEL), _row_spec(tm, D_MODEL),
                                                                   _full_spec(norm_in), _full_spec(w_in_pad)] + [HBM_SPEC] * n
                + [_full_spec(r) for r in late_rows])
    sums = [SDS(s.shape[1:], F32) for s in slabs]
    outs = [SDS((T, D_MODEL), F32), SDS((packed_rows, LANES), F32)] + sums
    return pl.pallas_call(
        body, name="in_proj_bwd_x", grid=(steps,), in_specs=in_specs,
        out_specs=[_row_spec(tm, D_MODEL), _full_spec(outs[1])] + [_full_spec(s) for s in sums],
        out_shape=outs,
        scratch_shapes=(_rs_scratch([s.shape for s in sums], None) + [pltpu.VMEM((1, D_MODEL), F32), pltpu.VMEM((packed_rows, LANES), F32)]
                        + _rs_scratch([], (packed_rows, LANES))),
        compiler_params=_cp(("arbitrary",)),
    )(*dzs, x2, dh, norm_in, w_in_pad, *slabs, *late_rows)


SLAB_ROWS = IN_TOTAL // N_DEV


def _slab_segments(k):
    cuts = [(0, ZKR_ORIG, 0), (ZKR_ORIG, ZKR_ORIG + ROPE, NOPE), (ZKR_ORIG + ROPE, IN_TOTAL, LANES - ROPE)]
    lo, hi = k * SLAB_ROWS, (k + 1) * SLAB_ROWS
    return [(max(lo, a) - lo, max(lo, a) + shift, min(hi, b) - max(lo, a)) for a, b, shift in cuts if min(hi, b) > max(lo, a)]


def _in_proj_bwd_w(dzs, hn, tm):
    T = hn.shape[0]
    steps = T // tm

    def body(d0, d1, d2, d3, d4, hn_ref, slab_ref, acc_ref):
        @pl.when(pl.program_id(0) == 0)
        def _():
            acc_ref[...] = jnp.zeros_like(acc_ref)

        hn_v = hn_ref[...]
        for ref, (lo, hi) in zip((d0, d1, d2, d3, d4), _DZ_COLS):
            acc_ref[lo:hi, :] += _mm_tn(ref[...], hn_v)

        @pl.when(pl.program_id(0) == steps - 1)
        def _():
            for k in range(N_DEV):
                for at, src, rows in _slab_segments(k):
                    slab_ref[k, at:at + rows, :] = acc_ref[src:src + rows, :].astype(BF16)

    in_specs = [_row_spec(tm, hi - lo) for lo, hi in _DZ_COLS] + [_row_spec(tm, D_MODEL)]
    out = SDS((N_DEV, SLAB_ROWS, D_MODEL), BF16)
    return pl.pallas_call(
        body, name="in_proj_bwd_w", grid=(steps,), in_specs=in_specs, out_specs=_full_spec(out), out_shape=out,
        scratch_shapes=[pltpu.VMEM((ZTOT, D_MODEL), F32)], compiler_params=_cp(("arbitrary",)),
    )(*dzs, hn)


def _local_step(x2, tgt, norm_in, w_in_pad, q_norm, wuq_pad, kv_norm, wukv, pool_w, pool_scale, late_shards, norm_final):
    T = x2.shape[0]
    tm, tm_small, tq = min(512, T), min(256, T), min(512, T)
    heads_fwd, heads_bwd = 4, 4
    row = lambda v: v.reshape(1, -1)
    rc, rsa, rsb = _rope_tables(T)

    hn, zgm, zga, zup, zgp, zfr, q_att, k_att, v_att, vt_att, w_ba, w_bp, w_out = _in_proj(
        x2, row(norm_in), w_in_pad, q_norm, wuq_pad, row(kv_norm), wukv, rc, rsa, rsb, tm, late_shards)
    w_out = w_out.reshape(D_MODEL, D_MODEL)
    o, lse_rows = _attn_fwd(q_att, k_att, vt_att, tq, heads_fwd)
    ypool = _pool_fwd(zup, zgp, pool_w, row(pool_scale))
    loss8, dh, dgm, doop, dga, dcap_rows, dyp, *slabs, g_nf = _tail(
        x2, tgt, o, zga, ypool, zgm, w_ba, w_bp, w_out, row(norm_final), tm_small)
    dup, dgp, g_pool_w, g_pool_scale = _pool_bwd(zup, zgp, dyp, pool_w, row(pool_scale))

    early = [g_pool_w, g_pool_scale, g_nf, loss8[0]]
    packed = jnp.concatenate([_pack_rows(a) for a in early], axis=0)
    dq_att, dkv_nat, dkr4, s_wout, s_wba, s_wbp, tot_early = _attn_bwd(
        q_att, k_att, v_att, doop, lse_rows, dcap_rows, tq, heads_bwd, slabs, packed)
    at_pool_w, at_pool_scale, at_nf, at_loss = _packed_at(early)

    dfr, g_wuq_pad, g_wukv, g_qn, g_kvn = _mla_bwd(
        dq_att, dkv_nat, dkr4, zfr, q_norm, wuq_pad, row(kv_norm), wukv, rc, rsa, rsb, tm)
    dzs = (dgm, dga, dup, dgp, dfr)
    slabs = [_in_proj_bwd_w(dzs, hn, tm), g_wuq_pad.reshape(N_DEV, Q_RANK // N_DEV, HW), g_wukv.reshape(N_DEV, KV_RANK // N_DEV, HW)]
    grad_x, tot_late, s_win, s_wuq, s_wukv = _in_proj_bwd_x(dzs, x2, dh, row(norm_in), w_in_pad, tm_small, slabs, [g_qn, g_kvn])
    at_nin, at_qn, at_kvn = _packed_at([norm_in, g_qn, g_kvn])

    grads = dict(norm_in=(tot_late, at_nin), w_in=s_win, q_norm=(tot_late, at_qn), w_uq=s_wuq, kv_norm=(tot_late, at_kvn), w_ukv=s_wukv,
                 pool_w=(tot_early, at_pool_w), pool_scale=(tot_early, at_pool_scale), w_branch_attn=s_wba, w_branch_pool=s_wbp,
                 w_out=s_wout, norm_final=(tot_early, at_nf))
    return tot_early[at_loss, 0], grad_x, grads


MESH_ID = pl.DeviceIdType.MESH
VMEM_SPEC = pl.BlockSpec(memory_space=pltpu.VMEM)
HBM_SPEC = pl.BlockSpec(memory_space=pl.ANY)


def _mesh_pos():
    return lax.axis_index("x"), lax.axis_index("y"), lax.axis_index("c")


BF16_TILE_ROWS = 16


def _half_rows(rows):
    cut = -(-(rows // 2) // BF16_TILE_ROWS) * BF16_TILE_ROWS
    return pl.ds(0, cut), pl.ds(cut, rows - cut)


def _slot(px, py, pc):
    return 4 * px + 2 * py + pc


def _staged_shape(shape):
    return (shape[0], shape[1] * LANES) if len(shape) == 3 else tuple(shape)


def _all_gather_bf16(shards):
    n = len(shards)
    staged = [_staged_shape(s.shape) for s in shards]

    def body(*refs):
        ins, outs = refs[:n], refs[n:2 * n]
        land0, scratch = refs[2 * n], refs[2 * n + 1:]
        wpad_ref = outs[0]
        ag = _AllGather(ins, (land0,) + tuple(outs[1:]), scratch)
        ag.start()
        ag.forward()
        ag.finish()
        wpad_ref[ZKR:GA, :] = jnp.zeros((GA - ZKR, D_MODEL), BF16)
        for k in range(N_DEV):
            for at, dst, rows in _slab_segments(k):
                wpad_ref[dst:dst + rows, :] = land0[k, at:at + rows, :]

    return pl.pallas_call(
        body, name="all_gather_weights",
        in_specs=[VMEM_SPEC] * n, out_specs=[VMEM_SPEC] + [HBM_SPEC] * (n - 1),
        out_shape=[SDS((ZTOT, D_MODEL), BF16)] + [SDS((N_DEV,) + s, BF16) for s in staged[1:]],
        scratch_shapes=[pltpu.VMEM((N_DEV,) + staged[0], BF16)] + _ag_scratch(staged),
        compiler_params=_cp(),
    )(*shards)


def _ag_scratch(shapes):
    n = len(shapes)
    dma = pltpu.SemaphoreType.DMA
    return [pltpu.VMEM(tuple(s), BF16) for s in shapes] + [dma((_AllGather.COPIES * n,)), dma((_AllGather.COPIES * n,)), dma((n,))]


class _AllGather:
    COPIES = 8

    def __init__(self, in_refs, dest_refs, scratch):
        n = self.n = len(in_refs)
        self.ins, self.dests, self.stage = in_refs, dest_refs, scratch[:n]
        self.send_sems, self.recv_sems, self.local_sems = scratch[n:]
        x, y, c = _mesh_pos()
        self.c, self.me, self.sibling = c, (x, y, c), (x, y, 1 - c)
        self.xn, self.yn, self.diag = (1 - x, y), (x, 1 - y), (1 - x, 1 - y)

    def _halves(self, a):
        return _half_rows(self.stage[a].shape[0])

    def _copy(self, a, k, block, to, from_stage=False, rows=None):
        dst = self.dests[a].at[_slot(*block)]
        src = self.stage[a] if from_stage else dst
        if rows is not None:
            src, dst = src.at[rows], dst.at[rows]
        return pltpu.make_async_remote_copy(
            src_ref=src, dst_ref=dst, send_sem=self.send_sems.at[self.COPIES * a + k],
            recv_sem=self.recv_sems.at[self.COPIES * a + k], device_id=to, device_id_type=MESH_ID)

    def _mine(self):
        return [pltpu.make_async_copy(self.stage[a], self.dests[a].at[_slot(*self.me)], self.local_sems.at[a]) for a in range(self.n)]

    def _first(self, a):
        return [self._copy(a, 0, self.me, self.sibling, True), self._copy(a, 1, self.me, (*self.xn, self.c), True),
                self._copy(a, 2, self.me, (*self.yn, self.c), True)]

    def _relays(self, a):
        lo, hi = self._halves(a)
        return [self._copy(a, 3, (*self.xn, self.c), (*self.yn, self.c), rows=lo),
                self._copy(a, 4, (*self.yn, self.c), (*self.xn, self.c), rows=hi)]

    def _passes(self, a):
        return [self._copy(a, 5 + j, (*chip, self.c), self.sibling) for j, chip in enumerate((self.xn, self.yn, self.diag))]

    def start(self):
        for a in range(self.n):
            src, dst = self.ins[a], self.stage[a]
            if src.shape == dst.shape:
                dst[...] = src[...].astype(BF16)
            else:
                if src.shape[2] < LANES:
                    dst[...] = jnp.zeros(dst.shape, BF16)
                for h in range(src.shape[1]):
                    dst[:, h * LANES:h * LANES + src.shape[2]] = src[:, h, :].astype(BF16)
        for cp in self._mine():
            cp.start()
        for a in range(self.n):
            for cp in self._first(a):
                cp.start()

    def forward(self):
        for a in range(self.n):
            relays, passes = self._relays(a), self._passes(a)
            for j, chip in enumerate((self.xn, self.yn)):
                self._copy(a, 1 + j, (*chip, self.c), self.me).wait_recv()
                relays[j].start()
                passes[j].start()

    def finish(self):
        for a in range(self.n):
            lo, hi = self._halves(a)
            self._copy(a, 3, (*self.diag, self.c), self.me, rows=lo).wait_recv()
            self._copy(a, 4, (*self.diag, self.c), self.me, rows=hi).wait_recv()
            self._passes(a)[2].start()
        for a in range(self.n):
            self._copy(a, 0, self.sibling, self.me).wait_recv()
            for j, chip in enumerate((self.xn, self.yn, self.diag)):
                self._copy(a, 5 + j, (*chip, 1 - self.c), self.me).wait_recv()
            for cp in self._first(a) + self._relays(a) + self._passes(a):
                cp.wait_send()
        for cp in self._mine():
            cp.wait()


N_CHIPS = 4


def _rs_scratch(shapes, packed_shape):
    n = len(shapes)
    n1, n2 = N_CHIPS * n + 1, _ReduceScatter.L2_COPIES * n + N_CHIPS - 1
    dma = pltpu.SemaphoreType.DMA
    packed = [] if packed_shape is None else [pltpu.VMEM(packed_shape, F32), pltpu.VMEM((N_CHIPS,) + tuple(packed_shape), F32)]
    return ([pltpu.VMEM((N_CHIPS,) + tuple(s), BF16) for s in shapes] * 2 + [pltpu.VMEM((N_CHIPS - 1,) + tuple(s), BF16) for s in shapes] * 2
            + packed + [dma((max(N_CHIPS * n, 1),)), dma((n1,)), dma((n1,)), dma((n2,)), dma((n2,))])


class _ReduceScatter:
    L2_COPIES = 6

    def __init__(self, slab_refs, packed_ref, out_refs, ptot_ref, scratch):
        n = self.n = len(slab_refs)
        self.slabs, self.packed, self.outs, self.ptot = slab_refs, packed_ref, out_refs, ptot_ref
        self.own1, self.land1, self.send2, self.land2 = (scratch[k * n:(k + 1) * n] for k in range(4))
        rest = scratch[4 * n:]
        if packed_ref is not None:
            self.pland1, self.pland2 = rest[:2]
            rest = rest[2:]
        self.loc_sems, self.send1_sems, self.recv1_sems, self.send2_sems, self.recv2_sems = rest
        self.x, self.y, self.c = _mesh_pos()

    def _chip(self, r):
        return (1 - self.x if r & 2 else self.x, 1 - self.y if r & 1 else self.y)

    @staticmethod
    def _remote(src, dst, send_sem, recv_sem, to):
        return pltpu.make_async_remote_copy(src_ref=src, dst_ref=dst, send_sem=send_sem, recv_sem=recv_sem, device_id=to,
                                            device_id_type=MESH_ID)

    def _copies1(self):
        c, sibling = self.c, (self.x, self.y, 1 - self.c)
        cps = []
        for a in range(self.n):
            for r in range(N_CHIPS):
                k = N_CHIPS * a + r
                cps.append(pltpu.make_async_copy(self.slabs[a].at[_slot(*self._chip(r), c)], self.own1[a].at[r], self.loc_sems.at[k]))
                cps.append(self._remote(self.slabs[a].at[_slot(*self._chip(r), 1 - c)], self.land1[a].at[r],
                                        self.send1_sems.at[k], self.recv1_sems.at[k], sibling))
        if self.packed is not None:
            k = N_CHIPS * self.n
            cps.append(self._remote(self.packed, self.pland1, self.send1_sems.at[k], self.recv1_sems.at[k], sibling))
        return cps

    def _halves(self, a):
        return _half_rows(self.send2[a].shape[1])

    def _copy2(self, a, k):
        lo, hi = self._halves(a)
        xn, yn = (*self._chip(2), self.c), (*self._chip(1), self.c)
        slot, rows, to = [(1, lo, xn), (2, lo, xn), (0, hi, yn), (2, hi, yn), (0, lo, yn), (1, hi, xn)][k]
        return self._remote(self.send2[a].at[slot].at[rows], self.land2[a].at[slot].at[rows],
                            self.send2_sems.at[self.L2_COPIES * a + k], self.recv2_sems.at[self.L2_COPIES * a + k], to)

    def _copies2_packed(self):
        base = self.L2_COPIES * self.n - 1
        return [self._remote(self.pland2.at[0], self.pland2.at[r], self.send2_sems.at[base + r], self.recv2_sems.at[base + r],
                             (*self._chip(r), self.c)) for r in range(1, N_CHIPS)]

    def start1(self):
        for cp in self._copies1():
            cp.start()

    def finish1_start2(self):
        for cp in self._copies1():
            cp.wait()
        for a in range(self.n):
            self.outs[a][...] = self.own1[a][0].astype(F32) + self.land1[a][0].astype(F32)
            for r in range(1, N_CHIPS):
                self.send2[a][r - 1] = (self.own1[a][r].astype(F32) + self.land1[a][r].astype(F32)).astype(BF16)
            for k in range(4):
                self._copy2(a, k).start()
        if self.packed is not None:
            self.pland2[0] = self.packed[...] + self.pland1[...]
            for cp in self._copies2_packed():
                cp.start()

    def relay2(self):
        for a in range(self.n):
            lo, hi = self._halves(a)
            s2, l2 = self.send2[a], self.land2[a]
            self._copy2(a, 1).wait_recv()
            s2[0, lo] = (s2[0, lo].astype(F32) + l2[2, lo].astype(F32)).astype(BF16)
            self._copy2(a, 4).start()
            self._copy2(a, 3).wait_recv()
            s2[1, hi] = (s2[1, hi].astype(F32) + l2[2, hi].astype(F32)).astype(BF16)
            self._copy2(a, 5).start()

    def finish2(self):
        for a in range(self.n):
            for k in (0, 2, 4, 5):
                self._copy2(a, k).wait_recv()
            for k in range(self.L2_COPIES):
                self._copy2(a, k).wait_send()
            l2 = self.land2[a]
            self.outs[a][...] = self.outs[a][...] + (l2[0].astype(F32) + l2[1].astype(F32))
        if self.packed is not None:
            for cp in self._copies2_packed():
                cp.wait()
            p2 = self.pland2
            self.ptot[...] = (p2[0] + p2[1]) + (p2[2] + p2[3])


def _adamw(ws, gs, ms, vs, rewrite):
    n = len(ws)
    first_row = [g[1] if isinstance(g, tuple) else None for g in gs]
    gs = [g[0] if isinstance(g, tuple) else g for g in gs]
    regrouped = [k for k in range(n) if first_row[k] is not None or gs[k].shape != ws[k].shape or rewrite[k]]
    assert all(w.shape[-1] % LANES == 0 for w, row in zip(ws, first_row) if row is not None)

    def body(*refs):
        for k in range(n):
            w_ref, g_ref, m_ref, v_ref, d_ref, nm_ref, nv_ref = (refs[j * n + k] for j in range(7))
            row = first_row[k]
            windows = [(..., ...)]
            if row is not None and len(w_ref.shape) == 1:
                windows = [(pl.ds(j * LANES, LANES), row + j) for j in range(w_ref.shape[0] // LANES)]
            elif row is not None:
                windows = [(..., pl.ds(row, w_ref.shape[0]))]
            elif g_ref.shape != w_ref.shape:
                windows = [((slice(None), h), (slice(None), slice(h * LANES, h * LANES + w_ref.shape[2]))) for h in range(w_ref.shape[1])]
            if k in regrouped:
                g_out_ref = refs[7 * n + regrouped.index(k)]
            for at, g_at in windows:
                w, g, m, v = w_ref[at], g_ref[g_at], m_ref[at], v_ref[at]
                m = ADAM_B1 * m + (1.0 - ADAM_B1) * g
                v = ADAM_B2 * v + (1.0 - ADAM_B2) * jnp.square(g)
                m_hat = m / (1.0 - ADAM_B1 ** ADAM_STEP)
                v_hat = v / (1.0 - ADAM_B2 ** ADAM_STEP)
                d_ref[at] = -ADAM_LR * (m_hat / (jnp.sqrt(v_hat) + ADAM_EPS) + ADAM_WD * w)
                nm_ref[at] = m
                nv_ref[at] = v
                if k in regrouped:
                    g_out_ref[at] = g

    shapes = [SDS(w.shape, F32) for w in ws]
    outs = pl.pallas_call(
        body, name="adamw", in_specs=[VMEM_SPEC] * (4 * n), out_specs=[VMEM_SPEC] * (3 * n + len(regrouped)),
        out_shape=shapes * 3 + [shapes[k] for k in regrouped], compiler_params=_cp(),
    )(*ws, *gs, *ms, *vs)
    grads = list(gs)
    for k, g in zip(regrouped, outs[3 * n:]):
        grads[k] = g
    return outs[:n], outs[n:2 * n], outs[2 * n:3 * n], grads


WEIGHTS = ("norm_in", "w_in", "q_norm", "w_uq", "kv_norm", "w_ukv", "pool_w", "pool_scale", "w_branch_attn", "w_branch_pool",
           "w_out", "norm_final")
SUBLANES = 8


def _pack_rows(a):
    a = a.reshape(-1, LANES)
    return jnp.pad(a, ((0, -a.shape[0] % SUBLANES), (0, 0)))


def _packed_at(like):
    out, row = [], 0
    for a in like:
        out.append(row)
        rows = a.size // LANES
        row += rows + (-rows % SUBLANES)
    return out


def kernel(x, norm_in, w_in, q_norm, w_uq, kv_norm, w_ukv, pool_w, pool_scale, w_branch_attn, w_branch_pool, w_out, norm_final, loss_target, m_norm_in, m_w_in, m_q_norm, m_w_uq, m_kv_norm, m_w_ukv, m_pool_w, m_pool_scale, m_w_branch_attn, m_w_branch_pool, m_w_out, m_norm_final, v_norm_in, v_w_in, v_q_norm, v_w_uq, v_kv_norm, v_w_ukv, v_pool_w, v_pool_scale, v_w_branch_attn, v_w_branch_pool, v_w_out, v_norm_final):
    w = dict(norm_in=norm_in, w_in=w_in, q_norm=q_norm, w_uq=w_uq, kv_norm=kv_norm, w_ukv=w_ukv, pool_w=pool_w, pool_scale=pool_scale,
             w_branch_attn=w_branch_attn, w_branch_pool=w_branch_pool, w_out=w_out, norm_final=norm_final)
    m = dict(norm_in=m_norm_in, w_in=m_w_in, q_norm=m_q_norm, w_uq=m_w_uq, kv_norm=m_kv_norm, w_ukv=m_w_ukv, pool_w=m_pool_w,
             pool_scale=m_pool_scale, w_branch_attn=m_w_branch_attn, w_branch_pool=m_w_branch_pool, w_out=m_w_out, norm_final=m_norm_final)
    v = dict(norm_in=v_norm_in, w_in=v_w_in, q_norm=v_q_norm, w_uq=v_w_uq, kv_norm=v_kv_norm, w_ukv=v_w_ukv, pool_w=v_pool_w,
             pool_scale=v_pool_scale, w_branch_attn=v_w_branch_attn, w_branch_pool=v_w_branch_pool, w_out=v_w_out, norm_final=v_norm_final)

    def as2d(name, a):
        if name == "w_in":
            return a.T
        if name in ("w_uq", "w_ukv"):
            return a
        return a.reshape(-1, GROUP) if name == "pool_w" else a

    def unshape(name, a):
        return a.T if name == "w_in" else a.reshape(w[name].shape)

    w_in_pad, w_uq_full, w_ukv_full = _all_gather_bf16([as2d(k, w[k]) for k in ("w_in", "w_uq", "w_ukv")])
    loss, grad_x, g2d = _local_step(
        x.reshape(x.shape[1:]), loss_target.reshape(x.shape[1:]), norm_in, w_in_pad, q_norm,
        w_uq_full.reshape(Q_RANK, HW), kv_norm, w_ukv_full.reshape(KV_RANK, HW),
        pool_w, pool_scale, [w_branch_attn, w_branch_pool, w_out], norm_final)

    deltas, new_m, new_v, grads = _adamw([as2d(k, w[k]) for k in WEIGHTS], [g2d[k] for k in WEIGHTS],
                                         [as2d(k, m[k]) for k in WEIGHTS], [as2d(k, v[k]) for k in WEIGHTS],
                                         [k in ("w_in", "w_branch_attn", "w_branch_pool", "w_out") for k in WEIGHTS])
    shaped = lambda arrs: [unshape(k, a) for k, a in zip(WEIGHTS, arrs)]
    return (loss, grad_x.reshape(x.shape), *shaped(grads), *shaped(deltas), *shaped(new_m), *shaped(new_v))
```
